```python
import jax, jax.numpy as jnp
from jax import lax
import numpy as np

D_MODEL = 1024
BATCH = 16
SEQ = 2048
DEPTH = 1

D_MIX = D_MODEL
D_CONV = D_MIX // 2
CONV_HEADS = 8
CONV_WIDTH = 31
D_POOL = D_MIX - D_CONV
POOL_WINDOWS = (2, 4, 8, 16)
POOL_GROUPS = len(POOL_WINDOWS)
POOL_GROUP_DIM = D_POOL // POOL_GROUPS
D_IN = 2 * D_CONV + D_POOL
N_MEM = 256
XATTN_HEADS = 4
XATTN_HEAD_DIM = D_MODEL // XATTN_HEADS
D_FF = 2816
FFN_CONV_WIDTH = 3
EPS = 1e-6

kernel_name = "hybrid_conformer_pool_xattn_convffn"


def rmsnorm(x, g):
    xf = x.astype(jnp.float32)
    y = xf * lax.rsqrt(jnp.mean(xf * xf, axis=-1, keepdims=True) + EPS)
    return (y * g.astype(jnp.float32)).astype(x.dtype)


def layernorm(x, g, b):
    xf = x.astype(jnp.float32)
    mu = jnp.mean(xf, axis=-1, keepdims=True)
    var = jnp.mean(jnp.square(xf - mu), axis=-1, keepdims=True)
    y = (xf - mu) * lax.rsqrt(var + EPS)
    return (y * g.astype(jnp.float32) + b.astype(jnp.float32)).astype(x.dtype)


def causal_depthwise_conv(x, w, b):
    k, c = w.shape
    y = lax.conv_general_dilated(
        x, w[:, None, :].astype(x.dtype), window_strides=(1,),
        padding=[(k - 1, 0)], dimension_numbers=("NWC", "WIO", "NWC"),
        feature_group_count=c)
    return y + b.astype(x.dtype)


def conformer_conv_mixer(u, dw_w, dw_b, ln_g, ln_b):
    val, gate = jnp.split(u, 2, axis=-1)
    h = val * jax.nn.sigmoid(gate)
    h = causal_depthwise_conv(h, dw_w, dw_b)
    h = layernorm(h, ln_g, ln_b)
    return jax.nn.silu(h)


def causal_window_mean_minus_self(v, w):
    s = v.shape[1]
    vf = v.astype(jnp.float32)
    c = jnp.cumsum(vf, axis=1)
    c_shift = jnp.pad(c, ((0, 0), (w, 0), (0, 0)))[:, :s]
    count = jnp.minimum(jnp.arange(1, s + 1, dtype=jnp.float32), float(w))
    mean = (c - c_shift) / count[None, :, None]
    return (mean - vf).astype(v.dtype)


def pooling_mixer(u, pool_w, pool_scale):
    groups = jnp.split(u, POOL_GROUPS, axis=-1)
    pooled = jnp.stack([causal_window_mean_minus_self(gv, w)
                        for gv, w in zip(groups, POOL_WINDOWS)], axis=2)
    mixed = jnp.einsum("bsgc,gcd->bsgd", pooled, pool_w.astype(u.dtype))
    b, s = u.shape[:2]
    return mixed.reshape(b, s, D_POOL) * pool_scale.astype(u.dtype)


def memory_cross_attention(h, mem_n, w_q, w_kv, w_o):
    b, s, _ = h.shape
    q = (h @ w_q).reshape(b, s, XATTN_HEADS, XATTN_HEAD_DIM)
    k, v = jnp.split(mem_n @ w_kv, 2, axis=-1)
    k = k.reshape(b, N_MEM, XATTN_HEADS, XATTN_HEAD_DIM)
    v = v.reshape(b, N_MEM, XATTN_HEADS, XATTN_HEAD_DIM)
    scores = jnp.einsum("bqhd,bkhd->bhqk", q.astype(jnp.float32), k.astype(jnp.float32))
    probs = jax.nn.softmax(scores * (XATTN_HEAD_DIM ** -0.5), axis=-1).astype(h.dtype)
    o = jnp.einsum("bhqk,bkhd->bqhd", probs, v).reshape(b, s, D_MODEL)
    return o @ w_o


def conv_ffn(h, w_up, dw_w, dw_b, w_down):
    u = causal_depthwise_conv(h @ w_up, dw_w, dw_b)
    gate, val = jnp.split(u, 2, axis=-1)
    return (jax.nn.silu(gate) * val) @ w_down


def _fwd_setup_inputs(seed: int = 0) -> dict:
    key = jax.random.key(seed)
    ks = jax.random.split(key, 24)
    f32 = jnp.float32

    def nrm(k, shape, scale):
        return jax.random.normal(k, shape, f32) * scale

    def gain(k, shape):
        return 1.0 + 0.05 * jax.random.normal(k, shape, f32)

    L = DEPTH
    return {
        "x": jax.random.normal(ks[0], (BATCH, SEQ, D_MODEL), f32),
        "mem": jax.random.normal(ks[1], (BATCH, N_MEM, D_MODEL), f32),
        "norm_mix_g": gain(ks[2], (L, D_MODEL)),
        "w_in": nrm(ks[3], (L, D_MODEL, D_IN), D_MODEL ** -0.5),
        "conv_dw_w": nrm(ks[4], (L, CONV_WIDTH, D_CONV), CONV_WIDTH ** -0.5),
        "conv_dw_b": nrm(ks[5], (L, D_CONV), 0.02),
        "conv_ln_g": gain(ks[6], (L, D_CONV)),
        "conv_ln_b": nrm(ks[7], (L, D_CONV), 0.02),
        "pool_w": nrm(ks[8], (L, POOL_GROUPS, POOL_GROUP_DIM, POOL_GROUP_DIM), POOL_GROUP_DIM ** -0.5),
        "pool_scale": gain(ks[9], (L, D_POOL)),
        "w_out": nrm(ks[10], (L, D_MIX, D_MODEL), D_MIX ** -0.5),
        "norm_xattn_g": gain(ks[11], (L, D_MODEL)),
        "norm_mem_g": gain(ks[12], (L, D_MODEL)),
        "w_q": nrm(ks[13], (L, D_MODEL, D_MODEL), D_MODEL ** -0.5),
        "w_kv": nrm(ks[14], (L, D_MODEL, 2 * D_MODEL), D_MODEL ** -0.5),
        "w_o": nrm(ks[15], (L, D_MODEL, D_MODEL), D_MODEL ** -0.5),
        "norm_ffn_g": gain(ks[16], (L, D_MODEL)),
        "w_up": nrm(ks[17], (L, D_MODEL, 2 * D_FF), D_MODEL ** -0.5),
        "ffn_dw_w": nrm(ks[18], (L, FFN_CONV_WIDTH, 2 * D_FF), FFN_CONV_WIDTH ** -0.5),
        "ffn_dw_b": nrm(ks[19], (L, 2 * D_FF), 0.02),
        "w_down": nrm(ks[20], (L, D_FF, D_MODEL), D_FF ** -0.5),
        "norm_final_g": gain(ks[21], (D_MODEL,)),
    }


def _fwd_reference(x, mem, norm_mix_g, w_in, conv_dw_w, conv_dw_b, conv_ln_g, conv_ln_b,
              pool_w, pool_scale, w_out, norm_xattn_g, norm_mem_g, w_q, w_kv, w_o,
              norm_ffn_g, w_up, ffn_dw_w, ffn_dw_b, w_down, norm_final_g):
    for l in range(DEPTH):
        h = rmsnorm(x, norm_mix_g[l])
        u = h @ w_in[l]
        u_conv = u[..., :2 * D_CONV]
        u_pool = u[..., 2 * D_CONV:]
        y_conv = conformer_conv_mixer(u_conv, conv_dw_w[l], conv_dw_b[l],
                                      conv_ln_g[l], conv_ln_b[l])
        y_pool = pooling_mixer(u_pool, pool_w[l], pool_scale[l])
        y = jnp.concatenate([y_conv, y_pool], axis=-1)
        x = x + y @ w_out[l]
        h = rmsnorm(x, norm_xattn_g[l])
        mem_n = rmsnorm(mem, norm_mem_g[l])
        x = x + memory_cross_attention(h, mem_n, w_q[l], w_kv[l], w_o[l])
        h = rmsnorm(x, norm_ffn_g[l])
        x = x + conv_ffn(h, w_up[l], ffn_dw_w[l], ffn_dw_b[l], w_down[l])
    return rmsnorm(x, norm_final_g)


import jax as _jax
import jax.numpy as _jnp

TWIN_FORMAT = 'train_step'
FWD_PARAMS = ['x', 'mem', 'norm_mix_g', 'w_in', 'conv_dw_w', 'conv_dw_b', 'conv_ln_g', 'conv_ln_b', 'pool_w', 'pool_scale', 'w_out', 'norm_xattn_g', 'norm_mem_g', 'w_q', 'w_kv', 'w_o', 'norm_ffn_g', 'w_up', 'ffn_dw_w', 'ffn_dw_b', 'w_down', 'norm_final_g']
TWIN_WEIGHTS = ['norm_mix_g', 'w_in', 'conv_dw_w', 'conv_dw_b', 'conv_ln_g', 'conv_ln_b', 'pool_w', 'pool_scale', 'w_out', 'norm_xattn_g', 'norm_mem_g', 'w_q', 'w_kv', 'w_o', 'norm_ffn_g', 'w_up', 'ffn_dw_w', 'ffn_dw_b', 'w_down', 'norm_final_g']
TWIN_DIFF_INPUT = 'x'
TWIN_INPUTS = ['x', 'mem', 'norm_mix_g', 'w_in', 'conv_dw_w', 'conv_dw_b', 'conv_ln_g', 'conv_ln_b', 'pool_w', 'pool_scale', 'w_out', 'norm_xattn_g', 'norm_mem_g', 'w_q', 'w_kv', 'w_o', 'norm_ffn_g', 'w_up', 'ffn_dw_w', 'ffn_dw_b', 'w_down', 'norm_final_g', 'loss_target', 'm_norm_mix_g', 'm_w_in', 'm_conv_dw_w', 'm_conv_dw_b', 'm_conv_ln_g', 'm_conv_ln_b', 'm_pool_w', 'm_pool_scale', 'm_w_out', 'm_norm_xattn_g', 'm_norm_mem_g', 'm_w_q', 'm_w_kv', 'm_w_o', 'm_norm_ffn_g', 'm_w_up', 'm_ffn_dw_w', 'm_ffn_dw_b', 'm_w_down', 'm_norm_final_g', 'v_norm_mix_g', 'v_w_in', 'v_conv_dw_w', 'v_conv_dw_b', 'v_conv_ln_g', 'v_conv_ln_b', 'v_pool_w', 'v_pool_scale', 'v_w_out', 'v_norm_xattn_g', 'v_norm_mem_g', 'v_w_q', 'v_w_kv', 'v_w_o', 'v_norm_ffn_g', 'v_w_up', 'v_ffn_dw_w', 'v_ffn_dw_b', 'v_w_down', 'v_norm_final_g']
TWIN_OUTPUTS = ['loss', 'grad_x', 'grad_norm_mix_g', 'grad_w_in', 'grad_conv_dw_w', 'grad_conv_dw_b', 'grad_conv_ln_g', 'grad_conv_ln_b', 'grad_pool_w', 'grad_pool_scale', 'grad_w_out', 'grad_norm_xattn_g', 'grad_norm_mem_g', 'grad_w_q', 'grad_w_kv', 'grad_w_o', 'grad_norm_ffn_g', 'grad_w_up', 'grad_ffn_dw_w', 'grad_ffn_dw_b', 'grad_w_down', 'grad_norm_final_g', 'delta_norm_mix_g', 'delta_w_in', 'delta_conv_dw_w', 'delta_conv_dw_b', 'delta_conv_ln_g', 'delta_conv_ln_b', 'delta_pool_w', 'delta_pool_scale', 'delta_w_out', 'delta_norm_xattn_g', 'delta_norm_mem_g', 'delta_w_q', 'delta_w_kv', 'delta_w_o', 'delta_norm_ffn_g', 'delta_w_up', 'delta_ffn_dw_w', 'delta_ffn_dw_b', 'delta_w_down', 'delta_norm_final_g', 'new_m_norm_mix_g', 'new_m_w_in', 'new_m_conv_dw_w', 'new_m_conv_dw_b', 'new_m_conv_ln_g', 'new_m_conv_ln_b', 'new_m_pool_w', 'new_m_pool_scale', 'new_m_w_out', 'new_m_norm_xattn_g', 'new_m_norm_mem_g', 'new_m_w_q', 'new_m_w_kv', 'new_m_w_o', 'new_m_norm_ffn_g', 'new_m_w_up', 'new_m_ffn_dw_w', 'new_m_ffn_dw_b', 'new_m_w_down', 'new_m_norm_final_g', 'new_v_norm_mix_g', 'new_v_w_in', 'new_v_conv_dw_w', 'new_v_conv_dw_b', 'new_v_conv_ln_g', 'new_v_conv_ln_b', 'new_v_pool_w', 'new_v_pool_scale', 'new_v_w_out', 'new_v_norm_xattn_g', 'new_v_norm_mem_g', 'new_v_w_q', 'new_v_w_kv', 'new_v_w_o', 'new_v_norm_ffn_g', 'new_v_w_up', 'new_v_ffn_dw_w', 'new_v_ffn_dw_b', 'new_v_w_down', 'new_v_norm_final_g']
TWIN_LEAF_KINDS = {'loss': 'loss', 'grad_x': 'grad_x', 'grad_norm_mix_g': 'grad_w', 'grad_w_in': 'grad_w', 'grad_conv_dw_w': 'grad_w', 'grad_conv_dw_b': 'grad_w', 'grad_conv_ln_g': 'grad_w', 'grad_conv_ln_b': 'grad_w', 'grad_pool_w': 'grad_w', 'grad_pool_scale': 'grad_w', 'grad_w_out': 'grad_w', 'grad_norm_xattn_g': 'grad_w', 'grad_norm_mem_g': 'grad_w', 'grad_w_q': 'grad_w', 'grad_w_kv': 'grad_w', 'grad_w_o': 'grad_w', 'grad_norm_ffn_g': 'grad_w', 'grad_w_up': 'grad_w', 'grad_ffn_dw_w': 'grad_w', 'grad_ffn_dw_b': 'grad_w', 'grad_w_down': 'grad_w', 'grad_norm_final_g': 'grad_w', 'delta_norm_mix_g': 'delta_w', 'delta_w_in': 'delta_w', 'delta_conv_dw_w': 'delta_w', 'delta_conv_dw_b': 'delta_w', 'delta_conv_ln_g': 'delta_w', 'delta_conv_ln_b': 'delta_w', 'delta_pool_w': 'delta_w', 'delta_pool_scale': 'delta_w', 'delta_w_out': 'delta_w', 'delta_norm_xattn_g': 'delta_w', 'delta_norm_mem_g': 'delta_w', 'delta_w_q': 'delta_w', 'delta_w_kv': 'delta_w', 'delta_w_o': 'delta_w', 'delta_norm_ffn_g': 'delta_w', 'delta_w_up': 'delta_w', 'delta_ffn_dw_w': 'delta_w', 'delta_ffn_dw_b': 'delta_w', 'delta_w_down': 'delta_w', 'delta_norm_final_g': 'delta_w', 'new_m_norm_mix_g': 'new_m', 'new_m_w_in': 'new_m', 'new_m_conv_dw_w': 'new_m', 'new_m_conv_dw_b': 'new_m', 'new_m_conv_ln_g': 'new_m', 'new_m_conv_ln_b': 'new_m', 'new_m_pool_w': 'new_m', 'new_m_pool_scale': 'new_m', 'new_m_w_out': 'new_m', 'new_m_norm_xattn_g': 'new_m', 'new_m_norm_mem_g': 'new_m', 'new_m_w_q': 'new_m', 'new_m_w_kv': 'new_m', 'new_m_w_o': 'new_m', 'new_m_norm_ffn_g': 'new_m', 'new_m_w_up': 'new_m', 'new_m_ffn_dw_w': 'new_m', 'new_m_ffn_dw_b': 'new_m', 'new_m_w_down': 'new_m', 'new_m_norm_final_g': 'new_m', 'new_v_norm_mix_g': 'new_v', 'new_v_w_in': 'new_v', 'new_v_conv_dw_w': 'new_v', 'new_v_conv_dw_b': 'new_v', 'new_v_conv_ln_g': 'new_v', 'new_v_conv_ln_b': 'new_v', 'new_v_pool_w': 'new_v', 'new_v_pool_scale': 'new_v', 'new_v_w_out': 'new_v', 'new_v_norm_xattn_g': 'new_v', 'new_v_norm_mem_g': 'new_v', 'new_v_w_q': 'new_v', 'new_v_w_kv': 'new_v', 'new_v_w_o': 'new_v', 'new_v_norm_ffn_g': 'new_v', 'new_v_w_up': 'new_v', 'new_v_ffn_dw_w': 'new_v', 'new_v_ffn_dw_b': 'new_v', 'new_v_w_down': 'new_v', 'new_v_norm_final_g': 'new_v'}


def _forward(args):
    return _fwd_reference(*[args[k] for k in FWD_PARAMS])


def _output_shape():
    out = _jax.eval_shape(lambda: _forward(_fwd_setup_inputs(0)))
    return out.shape, out.dtype

N_MICROBATCH = 1
ADAM_LR = 0.001
ADAM_B1 = 0.9
ADAM_B2 = 0.999
ADAM_EPS = 1e-08
ADAM_WD = 0.01
ADAM_STEP = 10
PER_EXAMPLE_BATCH_AXIS = {'x': 0, 'mem': 0, 'loss_target': 0}
SHARED_INPUTS = []
_WEIGHT_DTYPES = {'norm_mix_g': _jnp.float32, 'w_in': _jnp.float32, 'conv_dw_w': _jnp.float32, 'conv_dw_b': _jnp.float32, 'conv_ln_g': _jnp.float32, 'conv_ln_b': _jnp.float32, 'pool_w': _jnp.float32, 'pool_scale': _jnp.float32, 'w_out': _jnp.float32, 'norm_xattn_g': _jnp.float32, 'norm_mem_g': _jnp.float32, 'w_q': _jnp.float32, 'w_kv': _jnp.float32, 'w_o': _jnp.float32, 'norm_ffn_g': _jnp.float32, 'w_up': _jnp.float32, 'ffn_dw_w': _jnp.float32, 'ffn_dw_b': _jnp.float32, 'w_down': _jnp.float32, 'norm_final_g': _jnp.float32}
MOMENT_SCALE = {'norm_mix_g': 1.241121e-01, 'w_in': 1.009859e-01, 'conv_dw_w': 9.682314e-02, 'conv_dw_b': 2.153178e-01, 'conv_ln_g': 1.194347e-01, 'conv_ln_b': 1.240928e-01, 'pool_w': 1.411597e-01, 'pool_scale': 1.371563e-01, 'w_out': 1.218131e-01, 'norm_xattn_g': 1.690777e-02, 'norm_mem_g': 2.456126e-02, 'w_q': 1.673266e-02, 'w_kv': 1.715872e-02, 'w_o': 1.722088e-02, 'norm_ffn_g': 1.104724e-01, 'w_up': 4.750284e-02, 'ffn_dw_w': 4.784878e-02, 'ffn_dw_b': 4.821892e-02, 'w_down': 7.840662e-02, 'norm_final_g': 3.207737e+01}


def _to_microbatches(a, axis):
    t = _jnp.moveaxis(a, axis, 0)
    t = t.reshape((N_MICROBATCH, t.shape[0] // N_MICROBATCH) + t.shape[1:])
    return _jnp.moveaxis(t, 1, axis + 1)


def setup_inputs(seed: int = 0) -> dict:
    inp = _fwd_setup_inputs(seed)
    key = _jax.random.fold_in(_jax.random.key(seed), 7919)
    shape, _ = _output_shape()
    out = dict(inp)
    out["loss_target"] = _jax.random.normal(_jax.random.fold_in(key, 0), shape, _jnp.float32)
    for i, name in enumerate(TWIN_WEIGHTS):
        w = inp[name].astype(_jnp.float32)
        if MOMENT_SCALE is None:
            s = _jnp.sqrt(_jnp.mean(_jnp.square(w)) + 1e-30)
        else:
            s = MOMENT_SCALE[name]
        km, kv = _jax.random.split(_jax.random.fold_in(key, i + 1))
        out[name] = w
        out["m_" + name] = s * _jax.random.normal(km, w.shape, _jnp.float32)
        out["v_" + name] = (s * s) * _jax.random.uniform(kv, w.shape, _jnp.float32, 0.5, 1.5)
    if N_MICROBATCH > 1:
        for name, axis in PER_EXAMPLE_BATCH_AXIS.items():
            out[name] = _to_microbatches(out[name], axis)
    return {'x': out['x'], 'mem': out['mem'], 'norm_mix_g': out['norm_mix_g'], 'w_in': out['w_in'], 'conv_dw_w': out['conv_dw_w'], 'conv_dw_b': out['conv_dw_b'], 'conv_ln_g': out['conv_ln_g'], 'conv_ln_b': out['conv_ln_b'], 'pool_w': out['pool_w'], 'pool_scale': out['pool_scale'], 'w_out': out['w_out'], 'norm_xattn_g': out['norm_xattn_g'], 'norm_mem_g': out['norm_mem_g'], 'w_q': out['w_q'], 'w_kv': out['w_kv'], 'w_o': out['w_o'], 'norm_ffn_g': out['norm_ffn_g'], 'w_up': out['w_up'], 'ffn_dw_w': out['ffn_dw_w'], 'ffn_dw_b': out['ffn_dw_b'], 'w_down': out['w_down'], 'norm_final_g': out['norm_final_g'], 'loss_target': out['loss_target'], 'm_norm_mix_g': out['m_norm_mix_g'], 'm_w_in': out['m_w_in'], 'm_conv_dw_w': out['m_conv_dw_w'], 'm_conv_dw_b': out['m_conv_dw_b'], 'm_conv_ln_g': out['m_conv_ln_g'], 'm_conv_ln_b': out['m_conv_ln_b'], 'm_pool_w': out['m_pool_w'], 'm_pool_scale': out['m_pool_scale'], 'm_w_out': out['m_w_out'], 'm_norm_xattn_g': out['m_norm_xattn_g'], 'm_norm_mem_g': out['m_norm_mem_g'], 'm_w_q': out['m_w_q'], 'm_w_kv': out['m_w_kv'], 'm_w_o': out['m_w_o'], 'm_norm_ffn_g': out['m_norm_ffn_g'], 'm_w_up': out['m_w_up'], 'm_ffn_dw_w': out['m_ffn_dw_w'], 'm_ffn_dw_b': out['m_ffn_dw_b'], 'm_w_down': out['m_w_down'], 'm_norm_final_g': out['m_norm_final_g'], 'v_norm_mix_g': out['v_norm_mix_g'], 'v_w_in': out['v_w_in'], 'v_conv_dw_w': out['v_conv_dw_w'], 'v_conv_dw_b': out['v_conv_dw_b'], 'v_conv_ln_g': out['v_conv_ln_g'], 'v_conv_ln_b': out['v_conv_ln_b'], 'v_pool_w': out['v_pool_w'], 'v_pool_scale': out['v_pool_scale'], 'v_w_out': out['v_w_out'], 'v_norm_xattn_g': out['v_norm_xattn_g'], 'v_norm_mem_g': out['v_norm_mem_g'], 'v_w_q': out['v_w_q'], 'v_w_kv': out['v_w_kv'], 'v_w_o': out['v_w_o'], 'v_norm_ffn_g': out['v_norm_ffn_g'], 'v_w_up': out['v_w_up'], 'v_ffn_dw_w': out['v_ffn_dw_w'], 'v_ffn_dw_b': out['v_ffn_dw_b'], 'v_w_down': out['v_w_down'], 'v_norm_final_g': out['v_norm_final_g']}


def _loss(weights, diff, rest, loss_target):
    with _jax.named_scope("forward"):
        args = {**rest, TWIN_DIFF_INPUT: diff, **{k: w.astype(_WEIGHT_DTYPES[k]) for k, w in weights.items()}}
        y = _forward(args)
    with _jax.named_scope("loss_head"):
        err = _jnp.square(y.astype(_jnp.float32) - loss_target)
        return 0.5 * _jnp.sum(_jnp.mean(err, axis=-1)) if err.ndim else 0.5 * err


def _adamw(w, g, m, v):
    m = ADAM_B1 * m + (1.0 - ADAM_B1) * g
    v = ADAM_B2 * v + (1.0 - ADAM_B2) * _jnp.square(g)
    m_hat = m / (1.0 - ADAM_B1 ** ADAM_STEP)
    v_hat = v / (1.0 - ADAM_B2 ** ADAM_STEP)
    delta = -ADAM_LR * (m_hat / (_jnp.sqrt(v_hat) + ADAM_EPS) + ADAM_WD * w)
    return delta, m, v


def reference(x, mem, norm_mix_g, w_in, conv_dw_w, conv_dw_b, conv_ln_g, conv_ln_b, pool_w, pool_scale, w_out, norm_xattn_g, norm_mem_g, w_q, w_kv, w_o, norm_ffn_g, w_up, ffn_dw_w, ffn_dw_b, w_down, norm_final_g, loss_target, m_norm_mix_g, m_w_in, m_conv_dw_w, m_conv_dw_b, m_conv_ln_g, m_conv_ln_b, m_pool_w, m_pool_scale, m_w_out, m_norm_xattn_g, m_norm_mem_g, m_w_q, m_w_kv, m_w_o, m_norm_ffn_g, m_w_up, m_ffn_dw_w, m_ffn_dw_b, m_w_down, m_norm_final_g, v_norm_mix_g, v_w_in, v_conv_dw_w, v_conv_dw_b, v_conv_ln_g, v_conv_ln_b, v_pool_w, v_pool_scale, v_w_out, v_norm_xattn_g, v_norm_mem_g, v_w_q, v_w_kv, v_w_o, v_norm_ffn_g, v_w_up, v_ffn_dw_w, v_ffn_dw_b, v_w_down, v_norm_final_g):
    given = dict(x=x, mem=mem, norm_mix_g=norm_mix_g, w_in=w_in, conv_dw_w=conv_dw_w, conv_dw_b=conv_dw_b, conv_ln_g=conv_ln_g, conv_ln_b=conv_ln_b, pool_w=pool_w, pool_scale=pool_scale, w_out=w_out, norm_xattn_g=norm_xattn_g, norm_mem_g=norm_mem_g, w_q=w_q, w_kv=w_kv, w_o=w_o, norm_ffn_g=norm_ffn_g, w_up=w_up, ffn_dw_w=ffn_dw_w, ffn_dw_b=ffn_dw_b, w_down=w_down, norm_final_g=norm_final_g, loss_target=loss_target, m_norm_mix_g=m_norm_mix_g, m_w_in=m_w_in, m_conv_dw_w=m_conv_dw_w, m_conv_dw_b=m_conv_dw_b, m_conv_ln_g=m_conv_ln_g, m_conv_ln_b=m_conv_ln_b, m_pool_w=m_pool_w, m_pool_scale=m_pool_scale, m_w_out=m_w_out, m_norm_xattn_g=m_norm_xattn_g, m_norm_mem_g=m_norm_mem_g, m_w_q=m_w_q, m_w_kv=m_w_kv, m_w_o=m_w_o, m_norm_ffn_g=m_norm_ffn_g, m_w_up=m_w_up, m_ffn_dw_w=m_ffn_dw_w, m_ffn_dw_b=m_ffn_dw_b, m_w_down=m_w_down, m_norm_final_g=m_norm_final_g, v_norm_mix_g=v_norm_mix_g, v_w_in=v_w_in, v_conv_dw_w=v_conv_dw_w, v_conv_dw_b=v_conv_dw_b, v_conv_ln_g=v_conv_ln_g, v_conv_ln_b=v_conv_ln_b, v_pool_w=v_pool_w, v_pool_scale=v_pool_scale, v_w_out=v_w_out, v_norm_xattn_g=v_norm_xattn_g, v_norm_mem_g=v_norm_mem_g, v_w_q=v_w_q, v_w_kv=v_w_kv, v_w_o=v_w_o, v_norm_ffn_g=v_norm_ffn_g, v_w_up=v_w_up, v_ffn_dw_w=v_ffn_dw_w, v_ffn_dw_b=v_ffn_dw_b, v_w_down=v_w_down, v_norm_final_g=v_norm_final_g)
    weights = {n: given[n] for n in TWIN_WEIGHTS}
    shared = {n: given[n] for n in SHARED_INPUTS}
    per_example = {n: given[n] for n in ['x', 'mem']}
    grad_fn = _jax.value_and_grad(_loss, argnums=(0, 1))

    def one_microbatch(ex, loss_target):
        ex = dict(ex)
        diff = ex.pop(TWIN_DIFF_INPUT)
        return grad_fn(weights, diff, {**shared, **ex}, loss_target)

    if N_MICROBATCH == 1:
        loss, (grad_w, grad_x) = one_microbatch(per_example, given["loss_target"])
    else:
        def body(carry, xs):
            loss_sum, grad_sum = carry
            l_k, (gw_k, gx_k) = one_microbatch(xs[0], xs[1])
            with _jax.named_scope("update"):
                return (loss_sum + l_k, _jax.tree.map(_jnp.add, grad_sum, gw_k)), gx_k

        init = (_jnp.zeros((), _jnp.float32), _jax.tree.map(_jnp.zeros_like, weights))
        (loss, grad_w), grad_x = _jax.lax.scan(body, init, (per_example, given["loss_target"]))
    with _jax.named_scope("update"):
        delta_w, new_m, new_v = {}, {}, {}
        for n in TWIN_WEIGHTS:
            delta_w[n], new_m[n], new_v[n] = _adamw(weights[n], grad_w[n], given["m_" + n], given["v_" + n])
    return (loss, grad_x, *[grad_w[n] for n in TWIN_WEIGHTS], *[delta_w[n] for n in TWIN_WEIGHTS],
            *[new_m[n] for n in TWIN_WEIGHTS], *[new_v[n] for n in TWIN_WEIGHTS])
```

```python
import functools

import jax
import jax.numpy as jnp
from jax import lax
from jax.experimental import pallas as pl
from jax.experimental.pallas import tpu as pltpu

f32 = jnp.float32
_ACT = jnp.bfloat16

EPS = 1e-6
POOL_WINDOWS = (2, 4, 8, 16)
XATTN_HEADS = 4
ADAM_LR = 0.001
ADAM_B1 = 0.9
ADAM_B2 = 0.999
ADAM_EPS = 1e-08
ADAM_WD = 0.01
ADAM_STEP = 10

_VMEM_LIMIT_BYTES = 56 * 1024 * 1024
_MESH = pl.DeviceIdType.MESH
_ANY = pl.BlockSpec(memory_space=pl.ANY)
_VMEM = pl.BlockSpec(memory_space=pltpu.VMEM)

_NN = (((1,), (0,)), ((), ()))
_NT = (((1,), (1,)), ((), ()))
_TN = (((0,), (0,)), ((), ()))


def _params(n_grid):
    return pltpu.CompilerParams(dimension_semantics=("arbitrary",) * n_grid, vmem_limit_bytes=_VMEM_LIMIT_BYTES)


def _sigmoid(v):
    return 1.0 / (1.0 + jnp.exp(-v))


def _dot(a, b, dims):
    return lax.dot_general(a, b, dims, preferred_element_type=f32)


def _mm(name, a, b, *, dims, grid, a_spec, b_spec, o_spec, out_shape, nk, acc_shape=None, res=None, res_spec=None):
    def body(*refs):
        if res is None:
            a_ref, b_ref, o_ref, *scratch = refs
            r_ref = None
        else:
            a_ref, b_ref, r_ref, o_ref, *scratch = refs
        p = _dot(a_ref[...], b_ref[...], dims)

        def finish(v):
            if r_ref is not None:
                v = v + r_ref[...]
            o_ref[...] = v.astype(o_ref.dtype)

        if nk == 1:
            finish(p)
        else:
            acc = scratch[0]
            k = pl.program_id(2)

            @pl.when(k == 0)
            def _():
                acc[...] = p

            @pl.when(k > 0)
            def _():
                acc[...] += p

            @pl.when(k == nk - 1)
            def _():
                finish(acc[...])

    ins = [a, b] + ([] if res is None else [res])
    specs = [a_spec, b_spec] + ([] if res is None else [res_spec])
    return pl.pallas_call(
        body, name=name, grid=grid, in_specs=specs, out_specs=o_spec, out_shape=out_shape,
        scratch_shapes=[pltpu.VMEM(acc_shape, f32)] if nk > 1 else [], compiler_params=_params(3),
    )(*ins)


def _row_tile(m):
    return min(512, m)


def _mm_nn(name, a, b, out_dtype, tn, res=None, split_out=False):
    m, k = a.shape
    n = b.shape[1]
    tm = _row_tile(m)
    if split_out:
        out_shape = jax.ShapeDtypeStruct((n // tn, m, tn), out_dtype)
        o_spec = pl.BlockSpec((None, tm, tn), lambda j, i, kk: (j, i, 0))
    else:
        out_shape = jax.ShapeDtypeStruct((m, n), out_dtype)
        o_spec = pl.BlockSpec((tm, tn), lambda j, i, kk: (i, j))
    return _mm(
        name, a, b, dims=_NN, grid=(n // tn, m // tm, 1), nk=1,
        a_spec=pl.BlockSpec((tm, k), lambda j, i, kk: (i, 0)),
        b_spec=pl.BlockSpec((k, tn), lambda j, i, kk: (0, j)),
        o_spec=o_spec, out_shape=out_shape, res=res,
        res_spec=pl.BlockSpec((tm, tn), lambda j, i, kk: (i, j)),
    )


def _mm_nt(name, a, b, out_dtype, nk=1):
    n, kc = b.shape
    tk = kc // nk
    if a.ndim == 3:
        m = a.shape[1]
        tm = _row_tile(m)
        a_spec = pl.BlockSpec((None, tm, tk), lambda i, j, k: (k, i, 0))
    else:
        m = a.shape[0]
        tm = _row_tile(m)
        a_spec = pl.BlockSpec((tm, tk), lambda i, j, k: (i, k))
    return _mm(
        name, a, b, dims=_NT, grid=(m // tm, 1, nk), nk=nk, acc_shape=(tm, n),
        a_spec=a_spec, b_spec=pl.BlockSpec((n, tk), lambda i, j, k: (0, k)),
        o_spec=pl.BlockSpec((tm, n), lambda i, j, k: (i, 0)),
        out_shape=jax.ShapeDtypeStruct((m, n), out_dtype),
    )


def _mm_tn_rows(name, a, b, tt, tn):
    m, ka = a.shape
    nb = b.shape[1]
    nk = m // tt
    return _mm(
        name, a, b, dims=_TN, grid=(1, nb // tn, nk), nk=nk, acc_shape=(ka, tn),
        a_spec=pl.BlockSpec((tt, ka), lambda i, j, k: (k, 0)),
        b_spec=pl.BlockSpec((tt, tn), lambda i, j, k: (k, j)),
        o_spec=pl.BlockSpec((ka, tn), lambda i, j, k: (0, j)),
        out_shape=jax.ShapeDtypeStruct((ka, nb), _ACT),
    )


def _mm_tn_pieces(name, a, b, cs, tt):
    m, ka = a.shape
    nk = m // tt
    if b.ndim == 3:
        b_spec = pl.BlockSpec((None, tt, cs), lambda i, j, k: (j // 2, k, j % 2))
    else:
        b_spec = pl.BlockSpec((tt, cs), lambda i, j, k: (k, j))
    return _mm(
        name, a, b, dims=_TN, grid=(2, 4, nk), nk=nk, acc_shape=(ka // 2, cs),
        a_spec=pl.BlockSpec((tt, ka // 2), lambda i, j, k: (k, i)), b_spec=b_spec,
        o_spec=pl.BlockSpec((None, ka // 2, cs), lambda i, j, k: (2 * j + i, 0, 0)),
        out_shape=jax.ShapeDtypeStruct((8, ka // 2, cs), _ACT),
    )


def _rms_fwd(name, x, g):
    t, d = x.shape
    tm = _row_tile(t)

    def body(x_ref, g_ref, h_ref):
        xv = x_ref[...]
        r = lax.rsqrt(jnp.mean(xv * xv, axis=-1, keepdims=True) + EPS)
        h_ref[...] = (xv * r * g_ref[...]).astype(h_ref.dtype)

    return pl.pallas_call(
        body, name=name, grid=(t // tm,),
        in_specs=[pl.BlockSpec((tm, d), lambda i: (i, 0)), pl.BlockSpec((1, d), lambda i: (0, 0))],
        out_specs=pl.BlockSpec((tm, d), lambda i: (i, 0)), out_shape=jax.ShapeDtypeStruct((t, d), _ACT),
        compiler_params=_params(1),
    )(x, g)


def _rms_bwd(name, x, g, dh, dres):
    t, d = x.shape
    tm = _row_tile(t)

    def body(x_ref, g_ref, dh_ref, dres_ref, dx_ref, dxb_ref, dg_ref):
        @pl.when(pl.program_id(0) == 0)
        def _():
            dg_ref[...] = jnp.zeros_like(dg_ref)

        xv = x_ref[...]
        r = lax.rsqrt(jnp.mean(xv * xv, axis=-1, keepdims=True) + EPS)
        xn = xv * r
        dhv = dh_ref[...].astype(f32)
        dxn = dhv * g_ref[...]
        dx = r * (dxn - xn * jnp.mean(dxn * xn, axis=-1, keepdims=True)) + dres_ref[...]
        dx_ref[...] = dx
        dxb_ref[...] = dx.astype(dxb_ref.dtype)
        dg_ref[...] += jnp.sum(dhv * xn, axis=0, keepdims=True)

    row = pl.BlockSpec((tm, d), lambda i: (i, 0))
    vec = pl.BlockSpec((1, d), lambda i: (0, 0))
    return pl.pallas_call(
        body, name=name, grid=(t // tm,), in_specs=[row, vec, row, row], out_specs=[row, row, vec],
        out_shape=[jax.ShapeDtypeStruct((t, d), f32), jax.ShapeDtypeStruct((t, d), _ACT), jax.ShapeDtypeStruct((1, d), f32)],
        compiler_params=_params(1),
    )(x, g, dh, dres)


def _rms_gain_grad(name, x, dh):
    t, d = x.shape
    tm = _row_tile(t)

    def body(x_ref, dh_ref, dg_ref):
        @pl.when(pl.program_id(0) == 0)
        def _():
            dg_ref[...] = jnp.zeros_like(dg_ref)

        xv = x_ref[...]
        r = lax.rsqrt(jnp.mean(xv * xv, axis=-1, keepdims=True) + EPS)
        dg_ref[...] += jnp.sum(dh_ref[...] * (xv * r), axis=0, keepdims=True)

    row = pl.BlockSpec((tm, d), lambda i: (i, 0))
    return pl.pallas_call(
        body, name=name, grid=(t // tm,), in_specs=[row, row], out_specs=pl.BlockSpec((1, d), lambda i: (0, 0)),
        out_shape=jax.ShapeDtypeStruct((1, d), f32), compiler_params=_params(1),
    )(x, dh)


def _final_loss_bwd(x, g, tgt):
    t, d = x.shape
    tm = _row_tile(t)

    def body(x_ref, g_ref, t_ref, dx_ref, dxb_ref, dg_ref, loss_ref):
        @pl.when(pl.program_id(0) == 0)
        def _():
            dg_ref[...] = jnp.zeros_like(dg_ref)
            loss_ref[...] = jnp.zeros_like(loss_ref)

        xv = x_ref[...]
        gv = g_ref[...]
        r = lax.rsqrt(jnp.mean(xv * xv, axis=-1, keepdims=True) + EPS)
        xn = xv * r
        err = xn * gv - t_ref[...]
        loss_ref[...] += 0.5 * jnp.sum(jnp.mean(err * err, axis=-1, keepdims=True), axis=0, keepdims=True)
        dout = err * (1.0 / d)
        dxn = dout * gv
        dx = r * (dxn - xn * jnp.mean(dxn * xn, axis=-1, keepdims=True))
        dx_ref[...] = dx
        dxb_ref[...] = dx.astype(dxb_ref.dtype)
        dg_ref[...] += jnp.sum(dout * xn, axis=0, keepdims=True)

    row = pl.BlockSpec((tm, d), lambda i: (i, 0))
    vec = pl.BlockSpec((1, d), lambda i: (0, 0))
    one = pl.BlockSpec((1, 1), lambda i: (0, 0))
    return pl.pallas_call(
        body, name="final_loss_bwd", grid=(t // tm,), in_specs=[row, vec, row], out_specs=[row, row, vec, one],
        out_shape=[jax.ShapeDtypeStruct((t, d), f32), jax.ShapeDtypeStruct((t, d), _ACT),
                   jax.ShapeDtypeStruct((1, d), f32), jax.ShapeDtypeStruct((1, 1), f32)],
        compiler_params=_params(1),
    )(x, g, tgt)


_CONV_ROWS = 256
_CHUNK = 64
_HALO = 32


def _pool_counts(pos, w):
    return jnp.minimum(pos + 1.0, float(w))


def _mix_fwd(u, cw, cb, lg, lb, pw, ps, seq):
    t, c3 = u.shape
    c = c3 // 3
    kw = 31
    tm = min(_CONV_ROWS, seq)
    tps = seq // tm
    gd = c // len(POOL_WINDOWS)

    def body(u_ref, uh_ref, cw_ref, cb_ref, lg_ref, lb_ref, pw_ref, ps_ref, y_ref, hc_ref, hgbuf, pbuf):
        i = pl.program_id(0)
        keep = jnp.where(i % tps == 0, 0.0, 1.0)
        um = u_ref[...].astype(f32)
        uh = uh_ref[...].astype(f32) * keep
        hgbuf[0:_HALO, :] = uh[:, 0:c] * _sigmoid(uh[:, c:2 * c])
        hgbuf[_HALO:_HALO + tm, :] = um[:, 0:c] * _sigmoid(um[:, c:2 * c])
        pbuf[0:_HALO, :] = uh[:, 2 * c:]
        pbuf[_HALO:_HALO + tm, :] = um[:, 2 * c:]
        for r0 in range(0, tm, _CHUNK):
            acc = jnp.broadcast_to(cb_ref[...], (_CHUNK, c))
            for k in range(kw):
                off = r0 + _HALO - (kw - 1) + k
                acc = acc + cw_ref[k:k + 1, :] * hgbuf[off:off + _CHUNK, :]
            hc_ref[r0:r0 + _CHUNK, :] = acc
            mu = jnp.mean(acc, axis=-1, keepdims=True)
            xc = acc - mu
            var = jnp.mean(xc * xc, axis=-1, keepdims=True)
            hl = xc * lax.rsqrt(var + EPS) * lg_ref[...] + lb_ref[...]
            y_ref[r0:r0 + _CHUNK, 0:c] = (hl * _sigmoid(hl)).astype(y_ref.dtype)
        pos = ((i % tps) * tm).astype(f32) + lax.broadcasted_iota(jnp.int32, (tm, 1), 0).astype(f32)
        for gi, w in enumerate(POOL_WINDOWS):
            sl = slice(gi * gd, (gi + 1) * gd)
            v = pbuf[_HALO:_HALO + tm, sl]
            s = v
            for j in range(1, w):
                s = s + pbuf[_HALO - j:_HALO - j + tm, sl]
            pooled = s / _pool_counts(pos, w) - v
            mixed = _dot(pooled.astype(_ACT), pw_ref[gi].astype(_ACT), _NN)
            y_ref[:, c + gi * gd:c + (gi + 1) * gd] = (mixed * ps_ref[:, sl]).astype(y_ref.dtype)

    hb = tm // _HALO
    full = lambda shape: pl.BlockSpec(shape, lambda i: (0,) * len(shape))
    return pl.pallas_call(
        body, name="mix_fwd", grid=(t // tm,),
        in_specs=[pl.BlockSpec((tm, c3), lambda i: (i, 0)),
                  pl.BlockSpec((_HALO, c3), lambda i: (jnp.maximum(i * hb - 1, 0), 0)),
                  full((_HALO, c)), full((1, c)), full((1, c)), full((1, c)), full((len(POOL_WINDOWS), gd, gd)), full((1, c))],
        out_specs=[pl.BlockSpec((tm, 2 * c), lambda i: (i, 0)), pl.BlockSpec((tm, c), lambda i: (i, 0))],
        out_shape=[jax.ShapeDtypeStruct((t, 2 * c), _ACT), jax.ShapeDtypeStruct((t, c), f32)],
        scratch_shapes=[pltpu.VMEM((_HALO + tm, c), f32), pltpu.VMEM((_HALO + tm, c), f32)],
        compiler_params=_params(1),
    )(u, u, cw, cb, lg, lb, pw, ps)


def _mix_bwd_norm(hc, dy, lg, lb, seq):
    t, c = hc.shape
    tm = min(_CONV_ROWS, seq)

    def body(hc_ref, dy_ref, lg_ref, lb_ref, dhc_ref, sums_ref):
        @pl.when(pl.program_id(0) == 0)
        def _():
            sums_ref[...] = jnp.zeros_like(sums_ref)

        hcv = hc_ref[...]
        mu = jnp.mean(hcv, axis=-1, keepdims=True)
        xc = hcv - mu
        rstd = lax.rsqrt(jnp.mean(xc * xc, axis=-1, keepdims=True) + EPS)
        n = xc * rstd
        hl = n * lg_ref[...] + lb_ref[...]
        sg = _sigmoid(hl)
        dhl = dy_ref[...].astype(f32) * (sg * (1.0 + hl * (1.0 - sg)))
        dn = dhl * lg_ref[...]
        dhc = rstd * (dn - jnp.mean(dn, axis=-1, keepdims=True) - n * jnp.mean(dn * n, axis=-1, keepdims=True))
        dhc_ref[...] = dhc
        sums_ref[0:1, :] += jnp.sum(dhl * n, axis=0, keepdims=True)
        sums_ref[1:2, :] += jnp.sum(dhl, axis=0, keepdims=True)
        sums_ref[2:3, :] += jnp.sum(dhc, axis=0, keepdims=True)

    row = pl.BlockSpec((tm, c), lambda i: (i, 0))
    vec = pl.BlockSpec((1, c), lambda i: (0, 0))
    return pl.pallas_call(
        body, name="mix_bwd_norm", grid=(t // tm,), in_specs=[row, row, vec, vec],
        out_specs=[row, pl.BlockSpec((8, c), lambda i: (0, 0))],
        out_shape=[jax.ShapeDtypeStruct((t, c), f32), jax.ShapeDtypeStruct((8, c), f32)],
        compiler_params=_params(1),
    )(hc, dy, lg, lb)


def _mix_bwd_taps(u, dhc, dy, cw, pw, ps, seq):
    t, c3 = u.shape
    c = c3 // 3
    kw = 31
    tm = min(_CONV_ROWS, seq)
    tps = seq // tm
    ng = len(POOL_WINDOWS)
    gd = c // ng
    nh = 16

    def body(u_ref, uh_ref, dhc_ref, dhcn_ref, dy_ref, dyn_ref, cw_ref, pw_ref, ps_ref,
             du_ref, dcw_ref, dps_ref, dpw_ref, hgbuf, dcbuf, pbuf, dpbuf):
        i = pl.program_id(0)
        keep_prev = jnp.where(i % tps == 0, 0.0, 1.0)
        keep_next = jnp.where(i % tps == tps - 1, 0.0, 1.0)

        @pl.when(i == 0)
        def _():
            dcw_ref[...] = jnp.zeros_like(dcw_ref)
            dps_ref[...] = jnp.zeros_like(dps_ref)
            dpw_ref[...] = jnp.zeros_like(dpw_ref)

        uh = uh_ref[...].astype(f32) * keep_prev
        hgbuf[0:_HALO, :] = uh[:, 0:c] * _sigmoid(uh[:, c:2 * c])
        pbuf[0:_HALO, :] = uh[:, 2 * c:]
        um = u_ref[...].astype(f32)
        hgbuf[_HALO:_HALO + tm, :] = um[:, 0:c] * _sigmoid(um[:, c:2 * c])
        pbuf[_HALO:_HALO + tm, :] = um[:, 2 * c:]
        dcbuf[0:tm, :] = dhc_ref[...]
        dcbuf[tm:tm + _HALO, :] = dhcn_ref[...] * keep_next
        tap_sums = [None] * kw
        for r0 in range(0, tm, _CHUNK):
            dh = dcbuf[r0:r0 + _CHUNK, :]
            acc = jnp.zeros((_CHUNK, c), f32)
            for k in range(kw):
                off = r0 + _HALO - (kw - 1) + k
                part = jnp.sum(dh * hgbuf[off:off + _CHUNK, :], axis=0, keepdims=True)
                tap_sums[k] = part if tap_sums[k] is None else tap_sums[k] + part
                fwd = r0 + (kw - 1) - k
                acc = acc + cw_ref[k:k + 1, :] * dcbuf[fwd:fwd + _CHUNK, :]
            val = u_ref[r0:r0 + _CHUNK, 0:c].astype(f32)
            sg = _sigmoid(u_ref[r0:r0 + _CHUNK, c:2 * c].astype(f32))
            du_ref[r0:r0 + _CHUNK, 0:c] = (acc * sg).astype(du_ref.dtype)
            du_ref[r0:r0 + _CHUNK, c:2 * c] = (acc * val * sg * (1.0 - sg)).astype(du_ref.dtype)
        for k in range(kw):
            dcw_ref[k:k + 1, :] += tap_sums[k]
        base = ((i % tps) * tm).astype(f32)
        pos = base + lax.broadcasted_iota(jnp.int32, (tm, 1), 0).astype(f32)
        pos_next = base + float(tm) + lax.broadcasted_iota(jnp.int32, (nh, 1), 0).astype(f32)
        for gi, w in enumerate(POOL_WINDOWS):
            sl = slice(gi * gd, (gi + 1) * gd)
            v = pbuf[_HALO:_HALO + tm, sl]
            s = v
            for j in range(1, w):
                s = s + pbuf[_HALO - j:_HALO - j + tm, sl]
            cnt = _pool_counts(pos, w)
            pooled = (s / cnt - v).astype(_ACT)
            pwg = pw_ref[gi].astype(_ACT)
            mixed = _dot(pooled, pwg, _NN)
            dyp = dy_ref[:, sl].astype(f32)
            dps_ref[0:1, sl] += jnp.sum(dyp * mixed, axis=0, keepdims=True)
            dmix = (dyp * ps_ref[:, sl]).astype(_ACT)
            dpw_ref[gi] += _dot(pooled, dmix, _TN)
            dmix_next = (dyn_ref[:, sl].astype(f32) * ps_ref[:, sl] * keep_next).astype(_ACT)
            dpool = _dot(dmix, pwg, _NT)
            dpbuf[0:tm, sl] = dpool / cnt
            dpbuf[tm:tm + nh, sl] = _dot(dmix_next, pwg, _NT) / _pool_counts(pos_next, w)
            acc = -dpool
            for j in range(w):
                acc = acc + dpbuf[j:j + tm, sl]
            du_ref[:, 2 * c + gi * gd:2 * c + (gi + 1) * gd] = acc.astype(du_ref.dtype)

    hb = tm // _HALO
    n_halo = t // _HALO
    n_nh = t // nh
    full = lambda shape: pl.BlockSpec(shape, lambda i: (0,) * len(shape))
    return pl.pallas_call(
        body, name="mix_bwd_taps", grid=(t // tm,),
        in_specs=[pl.BlockSpec((tm, c3), lambda i: (i, 0)),
                  pl.BlockSpec((_HALO, c3), lambda i: (jnp.maximum(i * hb - 1, 0), 0)),
                  pl.BlockSpec((tm, c), lambda i: (i, 0)),
                  pl.BlockSpec((_HALO, c), lambda i: (jnp.minimum((i + 1) * hb, n_halo - 1), 0)),
                  pl.BlockSpec((tm, c), lambda i: (i, 1)),
                  pl.BlockSpec((nh, c), lambda i: (jnp.minimum((i + 1) * (tm // nh), n_nh - 1), 1)),
                  full((_HALO, c)), full((ng, gd, gd)), full((1, c))],
        out_specs=[pl.BlockSpec((tm, c3), lambda i: (i, 0)), full((_HALO, c)), full((8, c)), full((ng, gd, gd))],
        out_shape=[jax.ShapeDtypeStruct((t, c3), _ACT), jax.ShapeDtypeStruct((_HALO, c), f32),
                   jax.ShapeDtypeStruct((8, c), f32), jax.ShapeDtypeStruct((ng, gd, gd), f32)],
        scratch_shapes=[pltpu.VMEM((_HALO + tm, c), f32), pltpu.VMEM((tm + _HALO, c), f32),
                        pltpu.VMEM((_HALO + tm, c), f32), pltpu.VMEM((tm + nh, c), f32)],
        compiler_params=_params(1),
    )(u, u, dhc, dhc, dy, dy, cw, pw, ps)


def _attn_fwd(q, kv, n_seq, seq, n_mem):
    t, d = q.shape
    dh = d // XATTN_HEADS
    tq = min(512, seq)
    nq = seq // tq
    scale = dh ** -0.5

    def body(q_ref, k_ref, v_ref, o_ref):
        s = _dot(q_ref[...], k_ref[...], _NT) * scale
        e = jnp.exp(s - jnp.max(s, axis=-1, keepdims=True))
        p = e / jnp.sum(e, axis=-1, keepdims=True)
        o_ref[...] = _dot(p.astype(_ACT), v_ref[...], _NN).astype(o_ref.dtype)

    qs = pl.BlockSpec((tq, dh), lambda b, h, i: (b * nq + i, h))
    return pl.pallas_call(
        body, name="attn_fwd", grid=(n_seq, XATTN_HEADS, nq),
        in_specs=[qs, pl.BlockSpec((n_mem, dh), lambda b, h, i: (b, h)),
                  pl.BlockSpec((n_mem, dh), lambda b, h, i: (b, XATTN_HEADS + h))],
        out_specs=qs, out_shape=jax.ShapeDtypeStruct((t, d), _ACT), compiler_params=_params(3),
    )(q, kv, kv)


def _attn_bwd(q, kv, do, n_seq, seq, n_mem):
    t, d = q.shape
    dh = d // XATTN_HEADS
    tq = min(512, seq)
    nq = seq // tq
    scale = dh ** -0.5

    def body(q_ref, k_ref, v_ref, do_ref, dq_ref, dk_ref, dv_ref, dk_acc, dv_acc):
        i = pl.program_id(2)
        qv = q_ref[...]
        kvv = k_ref[...]
        dov = do_ref[...]
        s = _dot(qv, kvv, _NT) * scale
        e = jnp.exp(s - jnp.max(s, axis=-1, keepdims=True))
        p = e / jnp.sum(e, axis=-1, keepdims=True)
        dp = _dot(dov, v_ref[...], _NT)
        ds = (p * (dp - jnp.sum(dp * p, axis=-1, keepdims=True)) * scale).astype(_ACT)
        dq_ref[...] = _dot(ds, kvv, _NN).astype(dq_ref.dtype)
        dk_part = _dot(ds, qv, _TN)
        dv_part = _dot(p.astype(_ACT), dov, _TN)

        @pl.when(i == 0)
        def _():
            dk_acc[...] = dk_part
            dv_acc[...] = dv_part

        @pl.when(i > 0)
        def _():
            dk_acc[...] += dk_part
            dv_acc[...] += dv_part

        @pl.when(i == nq - 1)
        def _():
            dk_ref[...] = dk_acc[...].astype(dk_ref.dtype)
            dv_ref[...] = dv_acc[...].astype(dv_ref.dtype)

    qs = pl.BlockSpec((tq, dh), lambda b, h, i: (b * nq + i, h))
    ms = pl.BlockSpec((n_mem, dh), lambda b, h, i: (b, h))
    return pl.pallas_call(
        body, name="attn_bwd", grid=(n_seq, XATTN_HEADS, nq),
        in_specs=[qs, ms, pl.BlockSpec((n_mem, dh), lambda b, h, i: (b, XATTN_HEADS + h)), qs],
        out_specs=[qs, ms, ms],
        out_shape=[jax.ShapeDtypeStruct((t, d), _ACT), jax.ShapeDtypeStruct((n_seq * n_mem, d), _ACT),
                   jax.ShapeDtypeStruct((n_seq * n_mem, d), _ACT)],
        scratch_shapes=[pltpu.VMEM((n_mem, dh), f32), pltpu.VMEM((n_mem, dh), f32)],
        compiler_params=_params(3),
    )(q, kv, kv, do)


_FFN_COLS = 256
_FFN_HALO = 16


def _ffn_gate_fwd(up, fw, fb, seq):
    _, t, f = up.shape
    tm = min(_CONV_ROWS, seq)
    tps = seq // tm
    tc = _FFN_COLS
    nc = f // tc
    hl = _FFN_HALO

    def body(up_ref, uph_ref, wg_ref, wv_ref, bg_ref, bv_ref, a_ref, buf):
        i = pl.program_id(1)
        keep = jnp.where(i % tps == 0, 0.0, 1.0)
        buf[:, 0:hl, :] = uph_ref[...].astype(f32) * keep
        buf[:, hl:hl + tm, :] = up_ref[...].astype(f32)
        for r0 in range(0, tm, _CHUNK):
            conv = []
            for g, (w_ref, b_ref) in enumerate(((wg_ref, bg_ref), (wv_ref, bv_ref))):
                acc = jnp.broadcast_to(b_ref[...], (_CHUNK, tc))
                for k in range(3):
                    off = r0 + hl - 2 + k
                    acc = acc + w_ref[k:k + 1, :] * buf[g, off:off + _CHUNK, :]
                conv.append(acc)
            gate, val = conv
            a_ref[r0:r0 + _CHUNK, :] = (gate * _sigmoid(gate) * val).astype(a_ref.dtype)

    hb = tm // hl
    return pl.pallas_call(
        body, name="ffn_gate_fwd", grid=(nc, t // tm),
        in_specs=[pl.BlockSpec((2, tm, tc), lambda j, i: (0, i, j)),
                  pl.BlockSpec((2, hl, tc), lambda j, i: (0, jnp.maximum(i * hb - 1, 0), j)),
                  pl.BlockSpec((8, tc), lambda j, i: (0, j)), pl.BlockSpec((8, tc), lambda j, i: (0, nc + j)),
                  pl.BlockSpec((1, tc), lambda j, i: (0, j)), pl.BlockSpec((1, tc), lambda j, i: (0, nc + j))],
        out_specs=pl.BlockSpec((tm, tc), lambda j, i: (i, j)),
        out_shape=jax.ShapeDtypeStruct((t, f), _ACT),
        scratch_shapes=[pltpu.VMEM((2, hl + tm, tc), f32)], compiler_params=_params(2),
    )(up, up, fw, fw, fb, fb)


def _ffn_gate_bwd(up, da, fw, fb, seq):
    _, t, f = up.shape
    tm = min(_CONV_ROWS, seq)
    tps = seq // tm
    tc = _FFN_COLS
    nc = f // tc
    hl = _FFN_HALO

    def body(up_ref, uph_ref, upn_ref, da_ref, dan_ref, wg_ref, wv_ref, bg_ref, bv_ref,
             dup_ref, sg_ref, sv_ref, ubuf, dbuf):
        i = pl.program_id(1)
        keep_prev = jnp.where(i % tps == 0, 0.0, 1.0)
        keep_next = jnp.where(i % tps == tps - 1, 0.0, 1.0)

        @pl.when(i == 0)
        def _():
            sg_ref[...] = jnp.zeros_like(sg_ref)
            sv_ref[...] = jnp.zeros_like(sv_ref)

        ubuf[:, 0:hl, :] = uph_ref[...].astype(f32) * keep_prev
        ubuf[:, hl:hl + tm, :] = up_ref[...].astype(f32)
        ubuf[:, hl + tm:hl + tm + hl, :] = upn_ref[...].astype(f32) * keep_next
        w_refs = (wg_ref, wv_ref)
        b_refs = (bg_ref, bv_ref)

        def conv_rows(r0, rows):
            out = []
            for g in range(2):
                acc = jnp.broadcast_to(b_refs[g][...], (rows, tc))
                for k in range(3):
                    off = r0 + hl - 2 + k
                    acc = acc + w_refs[g][k:k + 1, :] * ubuf[g, off:off + rows, :]
                out.append(acc)
            return out

        def grads(r0, rows, dav):
            gate, val = conv_rows(r0, rows)
            sg = _sigmoid(gate)
            return dav * val * (sg * (1.0 + gate * (1.0 - sg))), dav * (gate * sg)

        for r0 in range(0, tm, _CHUNK):
            dg, dv = grads(r0, _CHUNK, da_ref[r0:r0 + _CHUNK, :].astype(f32))
            dbuf[0, r0:r0 + _CHUNK, :] = dg
            dbuf[1, r0:r0 + _CHUNK, :] = dv
        dg, dv = grads(tm, hl, dan_ref[...].astype(f32) * keep_next)
        dbuf[0, tm:tm + hl, :] = dg
        dbuf[1, tm:tm + hl, :] = dv
        for g, s_ref in enumerate((sg_ref, sv_ref)):
            sums = [None] * 4
            for r0 in range(0, tm, _CHUNK):
                d = dbuf[g, r0:r0 + _CHUNK, :]
                parts = [jnp.sum(d, axis=0, keepdims=True)]
                acc = jnp.zeros((_CHUNK, tc), f32)
                for k in range(3):
                    off = r0 + hl - 2 + k
                    parts.append(jnp.sum(d * ubuf[g, off:off + _CHUNK, :], axis=0, keepdims=True))
                    fwd = r0 + 2 - k
                    acc = acc + w_refs[g][k:k + 1, :] * dbuf[g, fwd:fwd + _CHUNK, :]
                dup_ref[g, r0:r0 + _CHUNK, :] = acc.astype(dup_ref.dtype)
                sums = [p if s is None else s + p for s, p in zip(sums, parts)]
            for r in range(4):
                s_ref[r:r + 1, :] += sums[r]

    hb = tm // hl
    n_halo = t // hl
    return pl.pallas_call(
        body, name="ffn_gate_bwd", grid=(nc, t // tm),
        in_specs=[pl.BlockSpec((2, tm, tc), lambda j, i: (0, i, j)),
                  pl.BlockSpec((2, hl, tc), lambda j, i: (0, jnp.maximum(i * hb - 1, 0), j)),
                  pl.BlockSpec((2, hl, tc), lambda j, i: (0, jnp.minimum((i + 1) * hb, n_halo - 1), j)),
                  pl.BlockSpec((tm, tc), lambda j, i: (i, j)),
                  pl.BlockSpec((hl, tc), lambda j, i: (jnp.minimum((i + 1) * hb, n_halo - 1), j)),
                  pl.BlockSpec((8, tc), lambda j, i: (0, j)), pl.BlockSpec((8, tc), lambda j, i: (0, nc + j)),
                  pl.BlockSpec((1, tc), lambda j, i: (0, j)), pl.BlockSpec((1, tc), lambda j, i: (0, nc + j))],
        out_specs=[pl.BlockSpec((2, tm, tc), lambda j, i: (0, i, j)),
                   pl.BlockSpec((8, tc), lambda j, i: (0, j)), pl.BlockSpec((8, tc), lambda j, i: (0, j))],
        out_shape=[jax.ShapeDtypeStruct((2, t, f), _ACT), jax.ShapeDtypeStruct((8, f), f32), jax.ShapeDtypeStruct((8, f), f32)],
        scratch_shapes=[pltpu.VMEM((2, hl + tm + hl, tc), f32), pltpu.VMEM((2, tm + hl, tc), f32)],
        compiler_params=_params(2),
    )(up, up, up, da, da, fw, fw, fb, fb)


def _adamw_math(w, g, m, v):
    m = ADAM_B1 * m + (1.0 - ADAM_B1) * g
    v = ADAM_B2 * v + (1.0 - ADAM_B2) * (g * g)
    m_hat = m / (1.0 - ADAM_B1 ** ADAM_STEP)
    v_hat = v / (1.0 - ADAM_B2 ** ADAM_STEP)
    delta = -ADAM_LR * (m_hat / (jnp.sqrt(v_hat) + ADAM_EPS) + ADAM_WD * w)
    return delta, m, v


def _adamw_shard(name, w, g, m, v):
    _, r, c = w.shape
    tr = next((cand for cand in (256, 176, 128, 64, 32, 16, 8) if r % cand == 0), r)

    def body(w_ref, g_ref, m_ref, v_ref, d_ref, mo_ref, vo_ref):
        d, mn, vn = _adamw_math(w_ref[...], g_ref[...], m_ref[...], v_ref[...])
        d_ref[...] = d
        mo_ref[...] = mn
        vo_ref[...] = vn

    s3 = pl.BlockSpec((None, tr, c), lambda i: (0, i, 0))
    s2 = pl.BlockSpec((tr, c), lambda i: (i, 0))
    shp = jax.ShapeDtypeStruct(w.shape, f32)
    return pl.pallas_call(
        body, name=name, grid=(r // tr,), in_specs=[s3, s2, s3, s3], out_specs=[s3, s3, s3], out_shape=[shp, shp, shp],
        compiler_params=_params(1),
    )(w, g, m, v)


def _adamw_small(quads):
    n = len(quads)

    def body(*refs):
        ins, outs = refs[:4 * n], refs[4 * n:]
        for p in range(n):
            w_ref, g_ref, m_ref, v_ref = ins[4 * p:4 * p + 4]
            d, mn, vn = _adamw_math(w_ref[...], g_ref[...], m_ref[...], v_ref[...])
            outs[3 * p][...] = d
            outs[3 * p + 1][...] = mn
            outs[3 * p + 2][...] = vn

    flat = [a for q in quads for a in q]
    shapes = [jax.ShapeDtypeStruct(q[0].shape, f32) for q in quads for _ in range(3)]
    outs = pl.pallas_call(
        body, name="adamw_small", in_specs=[_VMEM] * (4 * n), out_specs=[_VMEM] * (3 * n), out_shape=shapes,
        compiler_params=pltpu.CompilerParams(vmem_limit_bytes=_VMEM_LIMIT_BYTES),
    )(*flat)
    return [tuple(outs[3 * p:3 * p + 3]) for p in range(n)]


def _sum_pairs(name, a, b):
    n, r, c = a.shape

    def body(a_ref, b_ref, o_ref):
        o_ref[...] = (a_ref[...].astype(f32) + b_ref[...].astype(f32)).astype(o_ref.dtype)

    s = pl.BlockSpec((None, r, c), lambda i: (i, 0, 0))
    return pl.pallas_call(body, name=name, grid=(n,), in_specs=[s, s], out_specs=s,
                          out_shape=jax.ShapeDtypeStruct(a.shape, _ACT), compiler_params=_params(1))(a, b)


def _sum_four(name, own, got):
    r, c = own.shape

    def body(o_ref, g_ref, f_ref):
        s = o_ref[...].astype(f32) + g_ref[0].astype(f32)
        s = s + g_ref[1].astype(f32)
        f_ref[...] = s + g_ref[2].astype(f32)

    return pl.pallas_call(body, name=name, in_specs=[_VMEM, _VMEM], out_specs=_VMEM,
                          out_shape=jax.ShapeDtypeStruct((r, c), f32),
                          compiler_params=pltpu.CompilerParams(vmem_limit_bytes=_VMEM_LIMIT_BYTES))(own, got)


def _place():
    return lax.axis_index("x"), lax.axis_index("y"), lax.axis_index("c")


def _other_chips(x, y):
    return [(1 - x, y), (x, 1 - y), (1 - x, 1 - y)]


def _remote(src, dst, send_sem, recv_sem, to):
    return pltpu.make_async_remote_copy(src_ref=src, dst_ref=dst, send_sem=send_sem, recv_sem=recv_sem,
                                        device_id=to, device_id_type=_MESH)


def _allgather_weights(halves, col_sharded, smalls):
    nw = len(halves)
    ns = len(smalls)
    full_shapes = []
    for h, col in zip(halves, col_sharded):
        _, pr, pc = h.shape
        full_shapes.append((2 * pr, 4 * pc) if col else (8 * pr, pc))
    small_shapes = [(s.shape[0], 4 * s.shape[1]) for s in smalls]

    def body(*refs):
        src = refs[:nw]
        ssrc = refs[nw:nw + ns]
        dst = refs[nw + ns:2 * nw + ns]
        sdst = refs[2 * nw + ns:2 * (nw + ns)]
        ici_send, ici_recv, d2d_send, d2d_recv, loc_sem, s_send, s_recv = refs[2 * (nw + ns):]
        x, y, c = _place()
        me_chip = 2 * x + y
        chips = _other_chips(x, y)

        def piece(w, s, h):
            _, pr, pc = halves[w].shape
            if col_sharded[w]:
                return dst[w].at[pl.ds(h * pr, pr), pl.ds(s * pc, pc)]
            return dst[w].at[pl.ds((2 * s + h) * pr, pr), :]

        def small_piece(w, s):
            pc = smalls[w].shape[1]
            return sdst[w].at[:, pl.ds(s * pc, pc)]

        local = []
        for w in range(nw):
            for h in range(2):
                local.append(pltpu.make_async_copy(src[w].at[h], piece(w, me_chip, h), loc_sem.at[2 * w + h]))
        for w in range(ns):
            local.append(pltpu.make_async_copy(ssrc[w], small_piece(w, me_chip), loc_sem.at[2 * nw + w]))
        for cp in local:
            cp.start()
        first = []
        for w in range(nw):
            for j, chip in enumerate(chips):
                first.append(_remote(src[w].at[c], piece(w, me_chip, c), ici_send.at[3 * w + j], ici_recv.at[3 * w + j], (*chip, c)))
        for w in range(ns):
            for j, chip in enumerate(chips):
                first.append(_remote(ssrc[w], small_piece(w, me_chip), s_send.at[3 * w + j], s_recv.at[3 * w + j], (*chip, c)))
        for cp in first:
            cp.start()
        passed = []
        for w in range(nw):
            for j, chip in enumerate(chips):
                s = 2 * chip[0] + chip[1]
                _remote(src[w].at[c], piece(w, s, c), ici_send.at[3 * w + j], ici_recv.at[3 * w + j], (x, y, c)).wait_recv()
                cp = _remote(piece(w, s, c), piece(w, s, c), d2d_send.at[3 * w + j], d2d_recv.at[3 * w + j], (x, y, 1 - c))
                cp.start()
                passed.append(cp)
        for w in range(nw):
            for j, chip in enumerate(chips):
                s = 2 * chip[0] + chip[1]
                _remote(src[w].at[c], piece(w, s, 1 - c), d2d_send.at[3 * w + j], d2d_recv.at[3 * w + j], (x, y, c)).wait_recv()
        for w in range(ns):
            for j, chip in enumerate(chips):
                s = 2 * chip[0] + chip[1]
                _remote(ssrc[w], small_piece(w, s), s_send.at[3 * w + j], s_recv.at[3 * w + j], (x, y, c)).wait_recv()
        for cp in first + passed:
            cp.wait_send()
        for cp in local:
            cp.wait()

    out_shape = [jax.ShapeDtypeStruct(s, _ACT) for s in full_shapes] + [jax.ShapeDtypeStruct(s, f32) for s in small_shapes]
    return pl.pallas_call(
        body, name="allgather_weights", in_specs=[_ANY] * (nw + ns), out_specs=[_ANY] * (nw + ns), out_shape=out_shape,
        scratch_shapes=[pltpu.SemaphoreType.DMA((3 * nw,)), pltpu.SemaphoreType.DMA((3 * nw,)),
                        pltpu.SemaphoreType.DMA((3 * nw,)), pltpu.SemaphoreType.DMA((3 * nw,)),
                        pltpu.SemaphoreType.DMA((2 * nw + ns,)),
                        pltpu.SemaphoreType.DMA((3 * ns,)), pltpu.SemaphoreType.DMA((3 * ns,))],
    )(*halves, *smalls)


def _exchange_pair_halves(grads):
    nw = len(grads)

    def body(*refs):
        src = refs[:nw]
        mine = refs[nw:2 * nw]
        got = refs[2 * nw:3 * nw]
        send_sem, recv_sem, loc_sem = refs[3 * nw:]
        x, y, c = _place()
        copies, sends = [], []
        for w in range(nw):
            for s in range(4):
                cp = pltpu.make_async_copy(src[w].at[2 * s + c], mine[w].at[s], loc_sem.at[4 * w + s])
                cp.start()
                copies.append(cp)
                rc = _remote(src[w].at[2 * s + 1 - c], got[w].at[s], send_sem.at[4 * w + s], recv_sem.at[4 * w + s], (x, y, 1 - c))
                rc.start()
                sends.append(rc)
        for rc in sends:
            rc.wait_recv()
        for rc in sends:
            rc.wait_send()
        for cp in copies:
            cp.wait()

    shapes = [jax.ShapeDtypeStruct((4,) + g.shape[1:], g.dtype) for g in grads]
    outs = pl.pallas_call(
        body, name="rs_pair_exchange", in_specs=[_ANY] * nw, out_specs=[_ANY] * (2 * nw), out_shape=shapes + shapes,
        scratch_shapes=[pltpu.SemaphoreType.DMA((4 * nw,)), pltpu.SemaphoreType.DMA((4 * nw,)), pltpu.SemaphoreType.DMA((4 * nw,))],
    )(*grads)
    return outs[:nw], outs[nw:]


def _exchange_chip_pieces(sums):
    nw = len(sums)

    def body(*refs):
        src = refs[:nw]
        own = refs[nw:2 * nw]
        got = refs[2 * nw:3 * nw]
        send_sem, recv_sem, loc_sem = refs[3 * nw:]
        x, y, c = _place()
        me_chip = 2 * x + y
        copies, sends = [], []
        for w in range(nw):
            cp = pltpu.make_async_copy(src[w].at[me_chip], own[w], loc_sem.at[w])
            cp.start()
            copies.append(cp)
            for j, chip in enumerate(_other_chips(x, y)):
                s = 2 * chip[0] + chip[1]
                rc = _remote(src[w].at[s], got[w].at[j], send_sem.at[3 * w + j], recv_sem.at[3 * w + j], (*chip, c))
                rc.start()
                sends.append(rc)
        for rc in sends:
            rc.wait_recv()
        for rc in sends:
            rc.wait_send()
        for cp in copies:
            cp.wait()

    shapes = [jax.ShapeDtypeStruct(g.shape[1:], g.dtype) for g in sums] + [jax.ShapeDtypeStruct((3,) + g.shape[1:], g.dtype) for g in sums]
    outs = pl.pallas_call(
        body, name="rs_chip_exchange", in_specs=[_ANY] * nw, out_specs=[_ANY] * (2 * nw), out_shape=shapes,
        scratch_shapes=[pltpu.SemaphoreType.DMA((3 * nw,)), pltpu.SemaphoreType.DMA((3 * nw,)), pltpu.SemaphoreType.DMA((nw,))],
    )(*sums)
    return outs[:nw], outs[nw:]


def _swap_halves(finals):
    nw = len(finals)

    def body(*refs):
        src = refs[:nw]
        dst = refs[nw:2 * nw]
        send_sem, recv_sem, loc_sem = refs[2 * nw:]
        x, y, c = _place()
        copies, sends = [], []
        for w in range(nw):
            cp = pltpu.make_async_copy(src[w], dst[w].at[c], loc_sem.at[w])
            cp.start()
            copies.append(cp)
            rc = _remote(src[w], dst[w].at[c], send_sem.at[w], recv_sem.at[w], (x, y, 1 - c))
            rc.start()
            sends.append(rc)
        for w in range(nw):
            _remote(src[w], dst[w].at[1 - c], send_sem.at[w], recv_sem.at[w], (x, y, c)).wait_recv()
        for rc in sends:
            rc.wait_send()
        for cp in copies:
            cp.wait()

    shapes = [jax.ShapeDtypeStruct((2,) + g.shape, g.dtype) for g in finals]
    return pl.pallas_call(
        body, name="rs_swap_halves", in_specs=[_ANY] * nw, out_specs=[_ANY] * nw, out_shape=shapes,
        scratch_shapes=[pltpu.SemaphoreType.DMA((nw,)), pltpu.SemaphoreType.DMA((nw,)), pltpu.SemaphoreType.DMA((nw,))],
    )(*finals)


def _allreduce_small(parts):
    n = len(parts)

    def body(*refs):
        src = refs[:n]
        out = refs[n:2 * n]
        slots = refs[2 * n:3 * n]
        send_sem, recv_sem = refs[3 * n:]
        x, y, c = _place()
        me = 4 * x + 2 * y + c
        flips = [(bx, by, bc) for bx in (0, 1) for by in (0, 1) for bc in (0, 1)][1:]
        sends = []
        for a in range(n):
            slots[a][me] = src[a][...]
        for k, (bx, by, bc) in enumerate(flips):
            to = (1 - x if bx else x, 1 - y if by else y, 1 - c if bc else c)
            for a in range(n):
                rc = _remote(src[a], slots[a].at[me], send_sem.at[k, a], recv_sem.at[k, a], to)
                rc.start()
                sends.append(rc)
        for rc in sends:
            rc.wait_recv()
        for a in range(n):
            s = slots[a][0]
            for d in range(1, 8):
                s = s + slots[a][d]
            out[a][...] = s
        for rc in sends:
            rc.wait_send()

    return pl.pallas_call(
        body, name="allreduce_small", in_specs=[_VMEM] * n, out_specs=[_VMEM] * n,
        out_shape=[jax.ShapeDtypeStruct(p.shape, f32) for p in parts],
        scratch_shapes=[pltpu.VMEM((8,) + p.shape, f32) for p in parts] + [pltpu.SemaphoreType.DMA((7, n)), pltpu.SemaphoreType.DMA((7, n))],
        compiler_params=pltpu.CompilerParams(vmem_limit_bytes=_VMEM_LIMIT_BYTES),
    )(*parts)


def _local_step(x, mem, tgt, g_mix, g_xattn, g_mem, g_ffn, g_final, cb, lg, lb, pw, ps, fb,
                w_in, w_out, w_q, w_kv, w_o, w_up, w_down, cw, fw, n_seq, seq, n_mem):
    t, d = x.shape
    f = fb.shape[1] // 2
    c = cb.shape[1]
    h1 = _rms_fwd("norm_mix", x, g_mix)
    u = _mm_nn("proj_in", h1, w_in, _ACT, w_in.shape[1])
    y, hc = _mix_fwd(u, cw, cb, lg, lb, pw, ps, seq)
    x1 = _mm_nn("proj_out", y, w_out, f32, d, res=x)
    h2 = _rms_fwd("norm_xattn", x1, g_xattn)
    q = _mm_nn("proj_q", h2, w_q, _ACT, d)
    mem_n = _rms_fwd("norm_mem", mem, g_mem)
    kv = _mm_nn("proj_kv", mem_n, w_kv, _ACT, 2 * d)
    o = _attn_fwd(q, kv, n_seq, seq, n_mem)
    x2 = _mm_nn("proj_o", o, w_o, f32, d, res=x1)
    h3 = _rms_fwd("norm_ffn", x2, g_ffn)
    up = _mm_nn("proj_up", h3, w_up, _ACT, f, split_out=True)
    a = _ffn_gate_fwd(up, fw, fb, seq)
    x3 = _mm_nn("proj_down", a, w_down, f32, d, res=x2)
    dx3, dx3b, dg_final, loss = _final_loss_bwd(x3, g_final, tgt)
    da = _mm_nt("d_act", dx3b, w_down, _ACT)
    gw_down = _mm_tn_rows("dw_down", a, dx3b, min(512, t), 512)
    dup, sums_g, sums_v = _ffn_gate_bwd(up, da, fw, fb, seq)
    dh3 = _mm_nt("d_h3", dup, w_up, f32, nk=2)
    gw_up = _mm_tn_pieces("dw_up", h3, dup, f // 2, min(512, t))
    dx2, dx2b, dg_ffn = _rms_bwd("norm_ffn_bwd", x2, g_ffn, dh3, dx3)
    do = _mm_nt("d_o", dx2b, w_o, _ACT)
    gw_o = _mm_tn_rows("dw_o", o, dx2b, min(1024, t), d)
    dq, dk, dv = _attn_bwd(q, kv, do, n_seq, seq, n_mem)
    dkv = jnp.concatenate([dk, dv], axis=1)
    dh2 = _mm_nt("d_h2", dq, w_q, f32)
    gw_q = _mm_tn_rows("dw_q", h2, dq, min(1024, t), d)
    gw_kv = _mm_tn_pieces("dw_kv", mem_n, dkv, d // 2, mem.shape[0])
    dmem_n = _mm_nt("d_mem_n", dkv, w_kv, f32)
    dg_mem = _rms_gain_grad("norm_mem_bwd", mem, dmem_n)
    dx1, dx1b, dg_xattn = _rms_bwd("norm_xattn_bwd", x1, g_xattn, dh2, dx2)
    dy = _mm_nt("d_y", dx1b, w_out, _ACT)
    gw_out = _mm_tn_rows("dw_out", y, dx1b, min(1024, t), d)
    dhc, sums_norm = _mix_bwd_norm(hc, dy, lg, lb, seq)
    du, d_cw, d_ps, d_pw = _mix_bwd_taps(u, dhc, dy, cw, pw, ps, seq)
    dh1 = _mm_nt("d_h1", du, w_in, f32)
    gw_in = _mm_tn_pieces("dw_in", h1, du, c * 3 // 4, min(2048, t))
    grad_x, _, dg_mix = _rms_bwd("norm_mix_bwd", x, g_mix, dh1, dx1)
    zero_row = jnp.zeros((1, d), f32)
    gains = jnp.concatenate([dg_mix, dg_xattn, dg_mem, dg_ffn, dg_final, jnp.pad(loss, ((0, 0), (0, d - 1))), zero_row, zero_row], axis=0)
    conv_rows = jnp.concatenate([sums_norm[2:3], sums_norm[0:1], sums_norm[1:2], d_ps[0:1], jnp.zeros((4, c), f32)], axis=0)
    ffn_rows = jnp.concatenate([sums_g, sums_v], axis=1)
    big = [gw_in, gw_kv, gw_up,
           gw_out.reshape(8, -1, d), gw_q.reshape(8, -1, d), gw_o.reshape(8, -1, d), gw_down.reshape(8, -1, d)]
    small = [gains, conv_rows, d_pw.reshape(-1, d_pw.shape[-1]), ffn_rows, d_cw]
    return grad_x, big, small


def kernel(x, mem, norm_mix_g, w_in, conv_dw_w, conv_dw_b, conv_ln_g, conv_ln_b, pool_w, pool_scale, w_out, norm_xattn_g, norm_mem_g, w_q, w_kv, w_o, norm_ffn_g, w_up, ffn_dw_w, ffn_dw_b, w_down, norm_final_g, loss_target, m_norm_mix_g, m_w_in, m_conv_dw_w, m_conv_dw_b, m_conv_ln_g, m_conv_ln_b, m_pool_w, m_pool_scale, m_w_out, m_norm_xattn_g, m_norm_mem_g, m_w_q, m_w_kv, m_w_o, m_norm_ffn_g, m_w_up, m_ffn_dw_w, m_ffn_dw_b, m_w_down, m_norm_final_g, v_norm_mix_g, v_w_in, v_conv_dw_w, v_conv_dw_b, v_conv_ln_g, v_conv_ln_b, v_pool_w, v_pool_scale, v_w_out, v_norm_xattn_g, v_norm_mem_g, v_w_q, v_w_kv, v_w_o, v_norm_ffn_g, v_w_up, v_ffn_dw_w, v_ffn_dw_b, v_w_down, v_norm_final_g):
    n_seq, seq, d = x.shape
    n_mem = mem.shape[1]
    chip = 2 * lax.axis_index("x") + lax.axis_index("y")

    col_w = [w_in, w_kv, w_up]
    row_w = [w_out, w_q, w_o, w_down]
    halves = [w[0].astype(_ACT).reshape(2, w.shape[1] // 2, w.shape[2]) for w in col_w + row_w]
    kw = conv_dw_w.shape[1]
    cw_shard = jnp.pad(conv_dw_w[0], ((0, _HALO - kw), (0, 0)))
    fw_shard = jnp.pad(ffn_dw_w[0], ((0, 8 - ffn_dw_w.shape[1]), (0, 0)))
    fw_in, fw_kv, fw_up, fw_out, fw_q, fw_o, fw_down, cw, fw = _allgather_weights(
        halves, [True] * 3 + [False] * 4, [cw_shard, fw_shard])

    grad_x, big, small = _local_step(
        x.reshape(n_seq * seq, d), mem.reshape(n_seq * n_mem, d), loss_target.reshape(n_seq * seq, d),
        norm_mix_g, norm_xattn_g, norm_mem_g, norm_ffn_g, norm_final_g.reshape(1, d),
        conv_dw_b, conv_ln_g, conv_ln_b, pool_w[0], pool_scale, ffn_dw_b,
        fw_in, fw_out, fw_q, fw_kv, fw_o, fw_up, fw_down, cw, fw, n_seq, seq, n_mem)

    names = ["w_in", "w_kv", "w_up", "w_out", "w_q", "w_o", "w_down"]
    mine, got = _exchange_pair_halves(big)
    chip_sums = [_sum_pairs("rs_pair_sum_" + n, a, b) for n, a, b in zip(names, mine, got)]
    own, others = _exchange_chip_pieces(chip_sums)
    finals = [_sum_four("rs_chip_sum_" + n, a, b) for n, a, b in zip(names, own, others)]
    shard_grads = _swap_halves(finals)

    gains, conv_rows, d_pw, ffn_rows, d_cw = _allreduce_small(small)
    loss = gains[5, 0]

    outs = {}
    big_w = dict(zip(names, col_w + row_w))
    big_m = dict(w_in=m_w_in, w_kv=m_w_kv, w_up=m_w_up, w_out=m_w_out, w_q=m_w_q, w_o=m_w_o, w_down=m_w_down)
    big_v = dict(w_in=v_w_in, w_kv=v_w_kv, w_up=v_w_up, w_out=v_w_out, w_q=v_w_q, w_o=v_w_o, w_down=v_w_down)
    for n, g in zip(names, shard_grads):
        w = big_w[n]
        g2 = g.reshape(w.shape[1], w.shape[2])
        delta, new_m, new_v = _adamw_shard("adamw_" + n, w, g2, big_m[n], big_v[n])
        outs[n] = (g2.reshape(w.shape), delta, new_m, new_v)

    f2 = ffn_dw_b.shape[1]
    cs_c = conv_dw_w.shape[2]
    cs_f = ffn_dw_w.shape[2]
    g_cw = lax.dynamic_slice(d_cw, (0, chip * cs_c), (kw, cs_c)).reshape(conv_dw_w.shape)
    g_fw = lax.dynamic_slice(ffn_rows, (1, chip * cs_f), (ffn_dw_w.shape[1], cs_f)).reshape(ffn_dw_w.shape)
    small_params = [
        ("norm_mix_g", norm_mix_g, gains[0:1], m_norm_mix_g, v_norm_mix_g),
        ("conv_dw_w", conv_dw_w, g_cw, m_conv_dw_w, v_conv_dw_w),
        ("conv_dw_b", conv_dw_b, conv_rows[0:1], m_conv_dw_b, v_conv_dw_b),
        ("conv_ln_g", conv_ln_g, conv_rows[1:2], m_conv_ln_g, v_conv_ln_g),
        ("conv_ln_b", conv_ln_b, conv_rows[2:3], m_conv_ln_b, v_conv_ln_b),
        ("pool_w", pool_w, d_pw.reshape(pool_w.shape), m_pool_w, v_pool_w),
        ("pool_scale", pool_scale, conv_rows[3:4], m_pool_scale, v_pool_scale),
        ("norm_xattn_g", norm_xattn_g, gains[1:2], m_norm_xattn_g, v_norm_xattn_g),
        ("norm_mem_g", norm_mem_g, gains[2:3], m_norm_mem_g, v_norm_mem_g),
        ("norm_ffn_g", norm_ffn_g, gains[3:4], m_norm_ffn_g, v_norm_ffn_g),
        ("ffn_dw_w", ffn_dw_w, g_fw, m_ffn_dw_w, v_ffn_dw_w),
        ("ffn_dw_b", ffn_dw_b, ffn_rows[0:1, :f2], m_ffn_dw_b, v_ffn_dw_b),
        ("norm_final_g", norm_final_g.reshape(1, d), gains[4:5], m_norm_final_g.reshape(1, d), v_norm_final_g.reshape(1, d)),
    ]
    quads = []
    for _, w, g, m, v in small_params:
        shape2 = (-1, w.shape[-1])
        quads.append((w.reshape(shape2), g.reshape(shape2), m.reshape(shape2), v.reshape(shape2)))
    for (n, w, g, _, _), (delta, new_m, new_v) in zip(small_params, _adamw_small(quads)):
        shape = norm_final_g.shape if n == "norm_final_g" else w.shape
        outs[n] = (g.reshape(shape), delta.reshape(shape), new_m.reshape(shape), new_v.reshape(shape))

    order = ["norm_mix_g", "w_in", "conv_dw_w", "conv_dw_b", "conv_ln_g", "conv_ln_b", "pool_w", "pool_scale", "w_out",
             "norm_xattn_g", "norm_mem_g", "w_q", "w_kv", "w_o", "norm_ffn_g", "w_up", "ffn_dw_w", "ffn_dw_b", "w_down",
             "norm_final_g"]
    return (loss, grad_x.reshape(x.shape), *[outs[n][0] for n in order], *[outs[n][1] for n in order],
            *[outs[n][2] for n in order], *[outs[n][3] for n in order])
```

```python
import functools

import jax
import jax.numpy as jnp
from jax import lax
from jax.experimental import pallas as pl
from jax.experimental.pallas import tpu as pltpu

f32 = jnp.float32
_ACT = jnp.bfloat16

EPS = 1e-6
POOL_WINDOWS = (2, 4, 8, 16)
XATTN_HEADS = 4
ADAM_LR = 0.001
ADAM_B1 = 0.9
ADAM_B2 = 0.999
ADAM_EPS = 1e-08
ADAM_WD = 0.01
ADAM_STEP = 10

_VMEM_LIMIT_BYTES = 56 * 1024 * 1024
_MESH = pl.DeviceIdType.MESH
_ANY = pl.BlockSpec(memory_space=pl.ANY)
_VMEM = pl.BlockSpec(memory_space=pltpu.VMEM)

_NN = (((1,), (0,)), ((), ()))
_NT = (((1,), (1,)), ((), ()))
_TN = (((0,), (0,)), ((), ()))


def _params(n_grid):
    return pltpu.CompilerParams(dimension_semantics=("arbitrary",) * n_grid, vmem_limit_bytes=_VMEM_LIMIT_BYTES)


def _sigmoid(v):
    return 1.0 / (1.0 + jnp.exp(-v))


def _dot(a, b, dims):
    return lax.dot_general(a, b, dims, preferred_element_type=f32)


def _mm(name, a, b, *, dims, grid, a_spec, b_spec, o_spec, out_shape, nk, acc_shape=None, res=None, res_spec=None):
    def body(*refs):
        if res is None:
            a_ref, b_ref, o_ref, *scratch = refs
            r_ref = None
        else:
            a_ref, b_ref, r_ref, o_ref, *scratch = refs
        p = _dot(a_ref[...], b_ref[...], dims)

        def finish(v):
            if r_ref is not None:
                v = v + r_ref[...]
            o_ref[...] = v.astype(o_ref.dtype)

        if nk == 1:
            finish(p)
        else:
            acc = scratch[0]
            k = pl.program_id(2)

            @pl.when(k == 0)
            def _():
                acc[...] = p

            @pl.when(k > 0)
            def _():
                acc[...] += p

            @pl.when(k == nk - 1)
            def _():
                finish(acc[...])

    ins = [a, b] + ([] if res is None else [res])
    specs = [a_spec, b_spec] + ([] if res is None else [res_spec])
    return pl.pallas_call(
        body, name=name, grid=grid, in_specs=specs, out_specs=o_spec, out_shape=out_shape,
        scratch_shapes=[pltpu.VMEM(acc_shape, f32)] if nk > 1 else [], compiler_params=_params(3),
    )(*ins)


def _row_tile(m):
    return min(512, m)


def _mm_nn(name, a, b, out_dtype, tn, res=None, split_out=False):
    m, k = a.shape
    n = b.shape[1]
    tm = _row_tile(m)
    if split_out:
        out_shape = jax.ShapeDtypeStruct((n // tn, m, tn), out_dtype)
        o_spec = pl.BlockSpec((None, tm, tn), lambda j, i, kk: (j, i, 0))
    else:
        out_shape = jax.ShapeDtypeStruct((m, n), out_dtype)
        o_spec = pl.BlockSpec((tm, tn), lambda j, i, kk: (i, j))
    return _mm(
        name, a, b, dims=_NN, grid=(n // tn, m // tm, 1), nk=1,
        a_spec=pl.BlockSpec((tm, k), lambda j, i, kk: (i, 0)),
        b_spec=pl.BlockSpec((k, tn), lambda j, i, kk: (0, j)),
        o_spec=o_spec, out_shape=out_shape, res=res,
        res_spec=pl.BlockSpec((tm, tn), lambda j, i, kk: (i, j)),
    )


def _mm_nt(name, a, b, out_dtype, nk=1):
    n, kc = b.shape
    tk = kc // nk
    if a.ndim == 3:
        m = a.shape[1]
        tm = _row_tile(m)
        a_spec = pl.BlockSpec((None, tm, tk), lambda i, j, k: (k, i, 0))
    else:
        m = a.shape[0]
        tm = _row_tile(m)
        a_spec = pl.BlockSpec((tm, tk), lambda i, j, k: (i, k))
    return _mm(
        name, a, b, dims=_NT, grid=(m // tm, 1, nk), nk=nk, acc_shape=(tm, n),
        a_spec=a_spec, b_spec=pl.BlockSpec((n, tk), lambda i, j, k: (0, k)),
        o_spec=pl.BlockSpec((tm, n), lambda i, j, k: (i, 0)),
        out_shape=jax.ShapeDtypeStruct((m, n), out_dtype),
    )


def _mm_tn_rows(name, a, b, tt, tn):
    m, ka = a.shape
    nb = b.shape[1]
    nk = m // tt
    return _mm(
        name, a, b, dims=_TN, grid=(1, nb // tn, nk), nk=nk, acc_shape=(ka, tn),
        a_spec=pl.BlockSpec((tt, ka), lambda i, j, k: (k, 0)),
        b_spec=pl.BlockSpec((tt, tn), lambda i, j, k: (k, j)),
        o_spec=pl.BlockSpec((ka, tn), lambda i, j, k: (0, j)),
        out_shape=jax.ShapeDtypeStruct((ka, nb), _ACT),
    )


def _mm_tn_pieces(name, a, b, cs, tt):
    m, ka = a.shape
    nk = m // tt
    if b.ndim == 3:
        b_spec = pl.BlockSpec((None, tt, cs), lambda i, j, k: (j // 2, k, j % 2))
    else:
        b_spec = pl.BlockSpec((tt, cs), lambda i, j, k: (k, j))
    return _mm(
        name, a, b, dims=_TN, grid=(2, 4, nk), nk=nk, acc_shape=(ka // 2, cs),
        a_spec=pl.BlockSpec((tt, ka // 2), lambda i, j, k: (k, i)), b_spec=b_spec,
        o_spec=pl.BlockSpec((None, ka // 2, cs), lambda i, j, k: (2 * j + i, 0, 0)),
        out_shape=jax.ShapeDtypeStruct((8, ka // 2, cs), _ACT),
    )


def _rms_fwd(name, x, g):
    t, d = x.shape
    tm = _row_tile(t)

    def body(x_ref, g_ref, h_ref):
        xv = x_ref[...]
        r = lax.rsqrt(jnp.mean(xv * xv, axis=-1, keepdims=True) + EPS)
        h_ref[...] = (xv * r * g_ref[...]).astype(h_ref.dtype)

    return pl.pallas_call(
        body, name=name, grid=(t // tm,),
        in_specs=[pl.BlockSpec((tm, d), lambda i: (i, 0)), pl.BlockSpec((1, d), lambda i: (0, 0))],
        out_specs=pl.BlockSpec((tm, d), lambda i: (i, 0)), out_shape=jax.ShapeDtypeStruct((t, d), _ACT),
        compiler_params=_params(1),
    )(x, g)


def _rms_bwd(name, x, g, dh, dres):
    t, d = x.shape
    tm = _row_tile(t)

    def body(x_ref, g_ref, dh_ref, dres_ref, dx_ref, dxb_ref, dg_ref):
        @pl.when(pl.program_id(0) == 0)
        def _():
            dg_ref[...] = jnp.zeros_like(dg_ref)

        xv = x_ref[...]
        r = lax.rsqrt(jnp.mean(xv * xv, axis=-1, keepdims=True) + EPS)
        xn = xv * r
        dhv = dh_ref[...].astype(f32)
        dxn = dhv * g_ref[...]
        dx = r * (dxn - xn * jnp.mean(dxn * xn, axis=-1, keepdims=True)) + dres_ref[...]
        dx_ref[...] = dx
        dxb_ref[...] = dx.astype(dxb_ref.dtype)
        dg_ref[...] += jnp.sum(dhv * xn, axis=0, keepdims=True)

    row = pl.BlockSpec((tm, d), lambda i: (i, 0))
    vec = pl.BlockSpec((1, d), lambda i: (0, 0))
    return pl.pallas_call(
        body, name=name, grid=(t // tm,), in_specs=[row, vec, row, row], out_specs=[row, row, vec],
        out_shape=[jax.ShapeDtypeStruct((t, d), f32), jax.ShapeDtypeStruct((t, d), _ACT), jax.ShapeDtypeStruct((1, d), f32)],
        compiler_params=_params(1),
    )(x, g, dh, dres)


def _rms_gain_grad(name, x, dh):
    t, d = x.shape
    tm = _row_tile(t)

    def body(x_ref, dh_ref, dg_ref):
        @pl.when(pl.program_id(0) == 0)
        def _():
            dg_ref[...] = jnp.zeros_like(dg_ref)

        xv = x_ref[...]
        r = lax.rsqrt(jnp.mean(xv * xv, axis=-1, keepdims=True) + EPS)
        dg_ref[...] += jnp.sum(dh_ref[...] * (xv * r), axis=0, keepdims=True)

    row = pl.BlockSpec((tm, d), lambda i: (i, 0))
    return pl.pallas_call(
        body, name=name, grid=(t // tm,), in_specs=[row, row], out_specs=pl.BlockSpec((1, d), lambda i: (0, 0)),
        out_shape=jax.ShapeDtypeStruct((1, d), f32), compiler_params=_params(1),
    )(x, dh)


def _final_loss_bwd(x, g, tgt):
    t, d = x.shape
    tm = _row_tile(t)

    def body(x_ref, g_ref, t_ref, dx_ref, dxb_ref, dg_ref, loss_ref):
        @pl.when(pl.program_id(0) == 0)
        def _():
            dg_ref[...] = jnp.zeros_like(dg_ref)
            loss_ref[...] = jnp.zeros_like(loss_ref)

        xv = x_ref[...]
        gv = g_ref[...]
        r = lax.rsqrt(jnp.mean(xv * xv, axis=-1, keepdims=True) + EPS)
        xn = xv * r
        err = xn * gv - t_ref[...]
        loss_ref[...] += 0.5 * jnp.sum(jnp.mean(err * err, axis=-1, keepdims=True), axis=0, keepdims=True)
        dout = err * (1.0 / d)
        dxn = dout * gv
        dx = r * (dxn - xn * jnp.mean(dxn * xn, axis=-1, keepdims=True))
        dx_ref[...] = dx
        dxb_ref[...] = dx.astype(dxb_ref.dtype)
        dg_ref[...] += jnp.sum(dout * xn, axis=0, keepdims=True)

    row = pl.BlockSpec((tm, d), lambda i: (i, 0))
    vec = pl.BlockSpec((1, d), lambda i: (0, 0))
    one = pl.BlockSpec((1, 1), lambda i: (0, 0))
    return pl.pallas_call(
        body, name="final_loss_bwd", grid=(t // tm,), in_specs=[row, vec, row], out_specs=[row, row, vec, one],
        out_shape=[jax.ShapeDtypeStruct((t, d), f32), jax.ShapeDtypeStruct((t, d), _ACT),
                   jax.ShapeDtypeStruct((1, d), f32), jax.ShapeDtypeStruct((1, 1), f32)],
        compiler_params=_params(1),
    )(x, g, tgt)


_CONV_ROWS = 256
_CHUNK = 64
_HALO = 32


def _pool_counts(pos, w):
    return jnp.minimum(pos + 1.0, float(w))


def _mix_fwd(u, cw, cb, lg, lb, pw, ps, seq):
    t, c3 = u.shape
    c = c3 // 3
    kw = 31
    tm = min(_CONV_ROWS, seq)
    tps = seq // tm
    gd = c // len(POOL_WINDOWS)

    def body(u_ref, uh_ref, cw_ref, cb_ref, lg_ref, lb_ref, pw_ref, ps_ref, y_ref, hc_ref, hgbuf, pbuf):
        i = pl.program_id(0)
        keep = jnp.where(i % tps == 0, 0.0, 1.0)
        um = u_ref[...].astype(f32)
        uh = uh_ref[...].astype(f32) * keep
        hgbuf[0:_HALO, :] = uh[:, 0:c] * _sigmoid(uh[:, c:2 * c])
        hgbuf[_HALO:_HALO + tm, :] = um[:, 0:c] * _sigmoid(um[:, c:2 * c])
        pbuf[0:_HALO, :] = uh[:, 2 * c:]
        pbuf[_HALO:_HALO + tm, :] = um[:, 2 * c:]
        for r0 in range(0, tm, _CHUNK):
            acc = jnp.broadcast_to(cb_ref[...], (_CHUNK, c))
            for k in range(kw):
                off = r0 + _HALO - (kw - 1) + k
                acc = acc + cw_ref[k:k + 1, :] * hgbuf[off:off + _CHUNK, :]
            hc_ref[r0:r0 + _CHUNK, :] = acc
            mu = jnp.mean(acc, axis=-1, keepdims=True)
            xc = acc - mu
            var = jnp.mean(xc * xc, axis=-1, keepdims=True)
            hl = xc * lax.rsqrt(var + EPS) * lg_ref[...] + lb_ref[...]
            y_ref[r0:r0 + _CHUNK, 0:c] = (hl * _sigmoid(hl)).astype(y_ref.dtype)
        pos = ((i % tps) * tm).astype(f32) + lax.broadcasted_iota(jnp.int32, (tm, 1), 0).astype(f32)
        for gi, w in enumerate(POOL_WINDOWS):
            sl = slice(gi * gd, (gi + 1) * gd)
            v = pbuf[_HALO:_HALO + tm, sl]
            s = v
            for j in range(1, w):
                s = s + pbuf[_HALO - j:_HALO - j + tm, sl]
            pooled = s / _pool_counts(pos, w) - v
            mixed = _dot(pooled.astype(_ACT), pw_ref[gi].astype(_ACT), _NN)
            y_ref[:, c + gi * gd:c + (gi + 1) * gd] = (mixed * ps_ref[:, sl]).astype(y_ref.dtype)

    hb = tm // _HALO
    full = lambda shape: pl.BlockSpec(shape, lambda i: (0,) * len(shape))
    return pl.pallas_call(
        body, name="mix_fwd", grid=(t // tm,),
        in_specs=[pl.BlockSpec((tm, c3), lambda i: (i, 0)),
                  pl.BlockSpec((_HALO, c3), lambda i: (jnp.maximum(i * hb - 1, 0), 0)),
                  full((_HALO, c)), full((1, c)), full((1, c)), full((1, c)), full((len(POOL_WINDOWS), gd, gd)), full((1, c))],
        out_specs=[pl.BlockSpec((tm, 2 * c), lambda i: (i, 0)), pl.BlockSpec((tm, c), lambda i: (i, 0))],
        out_shape=[jax.ShapeDtypeStruct((t, 2 * c), _ACT), jax.ShapeDtypeStruct((t, c), f32)],
        scratch_shapes=[pltpu.VMEM((_HALO + tm, c), f32), pltpu.VMEM((_HALO + tm, c), f32)],
        compiler_params=_params(1),
    )(u, u, cw, cb, lg, lb, pw, ps)


def _mix_bwd_norm(hc, dy, lg, lb, seq):
    t, c = hc.shape
    tm = min(_CONV_ROWS, seq)

    def body(hc_ref, dy_ref, lg_ref, lb_ref, dhc_ref, sums_ref):
        @pl.when(pl.program_id(0) == 0)
        def _():
            sums_ref[...] = jnp.zeros_like(sums_ref)

        hcv = hc_ref[...]
        mu = jnp.mean(hcv, axis=-1, keepdims=True)
        xc = hcv - mu
        rstd = lax.rsqrt(jnp.mean(xc * xc, axis=-1, keepdims=True) + EPS)
        n = xc * rstd
        hl = n * lg_ref[...] + lb_ref[...]
        sg = _sigmoid(hl)
        dhl = dy_ref[...].astype(f32) * (sg * (1.0 + hl * (1.0 - sg)))
        dn = dhl * lg_ref[...]
        dhc = rstd * (dn - jnp.mean(dn, axis=-1, keepdims=True) - n * jnp.mean(dn * n, axis=-1, keepdims=True))
        dhc_ref[...] = dhc
        sums_ref[0:1, :] += jnp.sum(dhl * n, axis=0, keepdims=True)
        sums_ref[1:2, :] += jnp.sum(dhl, axis=0, keepdims=True)
        sums_ref[2:3, :] += jnp.sum(dhc, axis=0, keepdims=True)

    row = pl.BlockSpec((tm, c), lambda i: (i, 0))
    vec = pl.BlockSpec((1, c), lambda i: (0, 0))
    return pl.pallas_call(
        body, name="mix_bwd_norm", grid=(t // tm,), in_specs=[row, row, vec, vec],
        out_specs=[row, pl.BlockSpec((8, c), lambda i: (0, 0))],
        out_shape=[jax.ShapeDtypeStruct((t, c), f32), jax.ShapeDtypeStruct((8, c), f32)],
        compiler_params=_params(1),
    )(hc, dy, lg, lb)


def _mix_bwd_taps(u, dhc, dy, cw, pw, ps, seq):
    t, c3 = u.shape
    c = c3 // 3
    kw = 31
    tm = min(_CONV_ROWS, seq)
    tps = seq // tm
    ng = len(POOL_WINDOWS)
    gd = c // ng
    nh = 16

    def body(u_ref, uh_ref, dhc_ref, dhcn_ref, dy_ref, dyn_ref, cw_ref, pw_ref, ps_ref,
             du_ref, dcw_ref, dps_ref, dpw_ref, hgbuf, dcbuf, pbuf, dpbuf):
        i = pl.program_id(0)
        keep_prev = jnp.where(i % tps == 0, 0.0, 1.0)
        keep_next = jnp.where(i % tps == tps - 1, 0.0, 1.0)

        @pl.when(i == 0)
        def _():
            dcw_ref[...] = jnp.zeros_like(dcw_ref)
            dps_ref[...] = jnp.zeros_like(dps_ref)
            dpw_ref[...] = jnp.zeros_like(dpw_ref)

        uh = uh_ref[...].astype(f32) * keep_prev
        hgbuf[0:_HALO, :] = uh[:, 0:c] * _sigmoid(uh[:, c:2 * c])
        pbuf[0:_HALO, :] = uh[:, 2 * c:]
        um = u_ref[...].astype(f32)
        hgbuf[_HALO:_HALO + tm, :] = um[:, 0:c] * _sigmoid(um[:, c:2 * c])
        pbuf[_HALO:_HALO + tm, :] = um[:, 2 * c:]
        dcbuf[0:tm, :] = dhc_ref[...]
        dcbuf[tm:tm + _HALO, :] = dhcn_ref[...] * keep_next
        tap_sums = [None] * kw
        for r0 in range(0, tm, _CHUNK):
            dh = dcbuf[r0:r0 + _CHUNK, :]
            acc = jnp.zeros((_CHUNK, c), f32)
            for k in range(kw):
                off = r0 + _HALO - (kw - 1) + k
                part = jnp.sum(dh * hgbuf[off:off + _CHUNK, :], axis=0, keepdims=True)
                tap_sums[k] = part if tap_sums[k] is None else tap_sums[k] + part
                fwd = r0 + (kw - 1) - k
                acc = acc + cw_ref[k:k + 1, :] * dcbuf[fwd:fwd + _CHUNK, :]
            val = u_ref[r0:r0 + _CHUNK, 0:c].astype(f32)
            sg = _sigmoid(u_ref[r0:r0 + _CHUNK, c:2 * c].astype(f32))
            du_ref[r0:r0 + _CHUNK, 0:c] = (acc * sg).astype(du_ref.dtype)
            du_ref[r0:r0 + _CHUNK, c:2 * c] = (acc * val * sg * (1.0 - sg)).astype(du_ref.dtype)
        for k in range(kw):
            dcw_ref[k:k + 1, :] += tap_sums[k]
        base = ((i % tps) * tm).astype(f32)
        pos = base + lax.broadcasted_iota(jnp.int32, (tm, 1), 0).astype(f32)
        pos_next = base + float(tm) + lax.broadcasted_iota(jnp.int32, (nh, 1), 0).astype(f32)
        for gi, w in enumerate(POOL_WINDOWS):
            sl = slice(gi * gd, (gi + 1) * gd)
            v = pbuf[_HALO:_HALO + tm, sl]
            s = v
            for j in range(1, w):
                s = s + pbuf[_HALO - j:_HALO - j + tm, sl]
            cnt = _pool_counts(pos, w)
            pooled = (s / cnt - v).astype(_ACT)
            pwg = pw_ref[gi].astype(_ACT)
            mixed = _dot(pooled, pwg, _NN)
            dyp = dy_ref[:, sl].astype(f32)
            dps_ref[0:1, sl] += jnp.sum(dyp * mixed, axis=0, keepdims=True)
            dmix = (dyp * ps_ref[:, sl]).astype(_ACT)
            dpw_ref[gi] += _dot(pooled, dmix, _TN)
            dmix_next = (dyn_ref[:, sl].astype(f32) * ps_ref[:, sl] * keep_next).astype(_ACT)
            dpool = _dot(dmix, pwg, _NT)
            dpbuf[0:tm, sl] = dpool / cnt
            dpbuf[tm:tm + nh, sl] = _dot(dmix_next, pwg, _NT) / _pool_counts(pos_next, w)
            acc = -dpool
            for j in range(w):
                acc = acc + dpbuf[j:j + tm, sl]
            du_ref[:, 2 * c + gi * gd:2 * c + (gi + 1) * gd] = acc.astype(du_ref.dtype)

    hb = tm // _HALO
    n_halo = t // _HALO
    n_nh = t // nh
    full = lambda shape: pl.BlockSpec(shape, lambda i: (0,) * len(shape))
    return pl.pallas_call(
        body, name="mix_bwd_taps", grid=(t // tm,),
        in_specs=[pl.BlockSpec((tm, c3), lambda i: (i, 0)),
                  pl.BlockSpec((_HALO, c3), lambda i: (jnp.maximum(i * hb - 1, 0), 0)),
                  pl.BlockSpec((tm, c), lambda i: (i, 0)),
                  pl.BlockSpec((_HALO, c), lambda i: (jnp.minimum((i + 1) * hb, n_halo - 1), 0)),
                  pl.BlockSpec((tm, c), lambda i: (i, 1)),
                  pl.BlockSpec((nh, c), lambda i: (jnp.minimum((i + 1) * (tm // nh), n_nh - 1), 1)),
                  full((_HALO, c)), full((ng, gd, gd)), full((1, c))],
        out_specs=[pl.BlockSpec((tm, c3), lambda i: (i, 0)), full((_HALO, c)), full((8, c)), full((ng, gd, gd))],
        out_shape=[jax.ShapeDtypeStruct((t, c3), _ACT), jax.ShapeDtypeStruct((_HALO, c), f32),
                   jax.ShapeDtypeStruct((8, c), f32), jax.ShapeDtypeStruct((ng, gd, gd), f32)],
        scratch_shapes=[pltpu.VMEM((_HALO + tm, c), f32), pltpu.VMEM((tm + _HALO, c), f32),
                        pltpu.VMEM((_HALO + tm, c), f32), pltpu.VMEM((tm + nh, c), f32)],
        compiler_params=_params(1),
    )(u, u, dhc, dhc, dy, dy, cw, pw, ps)


def _attn_fwd(q, kv, n_seq, seq, n_mem):
    t, d = q.shape
    dh = d // XATTN_HEADS
    tq = min(512, seq)
    nq = seq // tq
    scale = dh ** -0.5

    def body(q_ref, k_ref, v_ref, o_ref):
        s = _dot(q_ref[...], k_ref[...], _NT) * scale
        e = jnp.exp(s - jnp.max(s, axis=-1, keepdims=True))
        p = e / jnp.sum(e, axis=-1, keepdims=True)
        o_ref[...] = _dot(p.astype(_ACT), v_ref[...], _NN).astype(o_ref.dtype)

    qs = pl.BlockSpec((tq, dh), lambda b, h, i: (b * nq + i, h))
    return pl.pallas_call(
        body, name="attn_fwd", grid=(n_seq, XATTN_HEADS, nq),
        in_specs=[qs, pl.BlockSpec((n_mem, dh), lambda b, h, i: (b, h)),
                  pl.BlockSpec((n_mem, dh), lambda b, h, i: (b, XATTN_HEADS + h))],
        out_specs=qs, out_shape=jax.ShapeDtypeStruct((t, d), _ACT), compiler_params=_params(3),
    )(q, kv, kv)


def _attn_bwd(q, kv, do, n_seq, seq, n_mem):
    t, d = q.shape
    dh = d // XATTN_HEADS
    tq = min(512, seq)
    nq = seq // tq
    scale = dh ** -0.5

    def body(q_ref, k_ref, v_ref, do_ref, dq_ref, dk_ref, dv_ref, dk_acc, dv_acc):
        i = pl.program_id(2)
        qv = q_ref[...]
        kvv = k_ref[...]
        dov = do_ref[...]
        s = _dot(qv, kvv, _NT) * scale
        e = jnp.exp(s - jnp.max(s, axis=-1, keepdims=True))
        p = e / jnp.sum(e, axis=-1, keepdims=True)
        dp = _dot(dov, v_ref[...], _NT)
        ds = (p * (dp - jnp.sum(dp * p, axis=-1, keepdims=True)) * scale).astype(_ACT)
        dq_ref[...] = _dot(ds, kvv, _NN).astype(dq_ref.dtype)
        dk_part = _dot(ds, qv, _TN)
        dv_part = _dot(p.astype(_ACT), dov, _TN)

        @pl.when(i == 0)
        def _():
            dk_acc[...] = dk_part
            dv_acc[...] = dv_part

        @pl.when(i > 0)
        def _():
            dk_acc[...] += dk_part
            dv_acc[...] += dv_part

        @pl.when(i == nq - 1)
        def _():
            dk_ref[...] = dk_acc[...].astype(dk_ref.dtype)
            dv_ref[...] = dv_acc[...].astype(dv_ref.dtype)

    qs = pl.BlockSpec((tq, dh), lambda b, h, i: (b * nq + i, h))
    ms = pl.BlockSpec((n_mem, dh), lambda b, h, i: (b, h))
    return pl.pallas_call(
        body, name="attn_bwd", grid=(n_seq, XATTN_HEADS, nq),
        in_specs=[qs, ms, pl.BlockSpec((n_mem, dh), lambda b, h, i: (b, XATTN_HEADS + h)), qs],
        out_specs=[qs, ms, ms],
        out_shape=[jax.ShapeDtypeStruct((t, d), _ACT), jax.ShapeDtypeStruct((n_seq * n_mem, d), _ACT),
                   jax.ShapeDtypeStruct((n_seq * n_mem, d), _ACT)],
        scratch_shapes=[pltpu.VMEM((n_mem, dh), f32), pltpu.VMEM((n_mem, dh), f32)],
        compiler_params=_params(3),
    )(q, kv, kv, do)


_FFN_COLS = 256
_FFN_HALO = 16


def _ffn_gate_fwd(up, fw, fb, seq):
    _, t, f = up.shape
    tm = min(_CONV_ROWS, seq)
    tps = seq // tm
    tc = _FFN_COLS
    nc = f // tc
    hl = _FFN_HALO

    def body(up_ref, uph_ref, wg_ref, wv_ref, bg_ref, bv_ref, a_ref, buf):
        i = pl.program_id(1)
        keep = jnp.where(i % tps == 0, 0.0, 1.0)
        buf[:, 0:hl, :] = uph_ref[...].astype(f32) * keep
        buf[:, hl:hl + tm, :] = up_ref[...].astype(f32)
        for r0 in range(0, tm, _CHUNK):
            conv = []
            for g, (w_ref, b_ref) in enumerate(((wg_ref, bg_ref), (wv_ref, bv_ref))):
                acc = jnp.broadcast_to(b_ref[...], (_CHUNK, tc))
                for k in range(3):
                    off = r0 + hl - 2 + k
                    acc = acc + w_ref[k:k + 1, :] * buf[g, off:off + _CHUNK, :]
                conv.append(acc)
            gate, val = conv
            a_ref[r0:r0 + _CHUNK, :] = (gate * _sigmoid(gate) * val).astype(a_ref.dtype)

    hb = tm // hl
    return pl.pallas_call(
        body, name="ffn_gate_fwd", grid=(nc, t // tm),
        in_specs=[pl.BlockSpec((2, tm, tc), lambda j, i: (0, i, j)),
                  pl.BlockSpec((2, hl, tc), lambda j, i: (0, jnp.maximum(i * hb - 1, 0), j)),
                  pl.BlockSpec((8, tc), lambda j, i: (0, j)), pl.BlockSpec((8, tc), lambda j, i: (0, nc + j)),
                  pl.BlockSpec((1, tc), lambda j, i: (0, j)), pl.BlockSpec((1, tc), lambda j, i: (0, nc + j))],
        out_specs=pl.BlockSpec((tm, tc), lambda j, i: (i, j)),
        out_shape=jax.ShapeDtypeStruct((t, f), _ACT),
        scratch_shapes=[pltpu.VMEM((2, hl + tm, tc), f32)], compiler_params=_params(2),
    )(up, up, fw, fw, fb, fb)


def _ffn_gate_bwd(up, da, fw, fb, seq):
    _, t, f = up.shape
    tm = min(_CONV_ROWS, seq)
    tps = seq // tm
    tc = _FFN_COLS
    nc = f // tc
    hl = _FFN_HALO

    def body(up_ref, uph_ref, upn_ref, da_ref, dan_ref, wg_ref, wv_ref, bg_ref, bv_ref,
             dup_ref, sg_ref, sv_ref, ubuf, dbuf):
        i = pl.program_id(1)
        keep_prev = jnp.where(i % tps == 0, 0.0, 1.0)
        keep_next = jnp.where(i % tps == tps - 1, 0.0, 1.0)

        @pl.when(i == 0)
        def _():
            sg_ref[...] = jnp.zeros_like(sg_ref)
            sv_ref[...] = jnp.zeros_like(sv_ref)

        ubuf[:, 0:hl, :] = uph_ref[...].astype(f32) * keep_prev
        ubuf[:, hl:hl + tm, :] = up_ref[...].astype(f32)
        ubuf[:, hl + tm:hl + tm + hl, :] = upn_ref[...].astype(f32) * keep_next
        w_refs = (wg_ref, wv_ref)
        b_refs = (bg_ref, bv_ref)

        def conv_rows(r0, rows):
            out = []
            for g in range(2):
                acc = jnp.broadcast_to(b_refs[g][...], (rows, tc))
                for k in range(3):
                    off = r0 + hl - 2 + k
                    acc = acc + w_refs[g][k:k + 1, :] * ubuf[g, off:off + rows, :]
                out.append(acc)
            return out

        def grads(r0, rows, dav):
            gate, val = conv_rows(r0, rows)
            sg = _sigmoid(gate)
            return dav * val * (sg * (1.0 + gate * (1.0 - sg))), dav * (gate * sg)

        for r0 in range(0, tm, _CHUNK):
            dg, dv = grads(r0, _CHUNK, da_ref[r0:r0 + _CHUNK, :].astype(f32))
            dbuf[0, r0:r0 + _CHUNK, :] = dg
            dbuf[1, r0:r0 + _CHUNK, :] = dv
        dg, dv = grads(tm, hl, dan_ref[...].astype(f32) * keep_next)
        dbuf[0, tm:tm + hl, :] = dg
        dbuf[1, tm:tm + hl, :] = dv
        for g, s_ref in enumerate((sg_ref, sv_ref)):
            sums = [None] * 4
            for r0 in range(0, tm, _CHUNK):
                d = dbuf[g, r0:r0 + _CHUNK, :]
                parts = [jnp.sum(d, axis=0, keepdims=True)]
                acc = jnp.zeros((_CHUNK, tc), f32)
                for k in range(3):
                    off = r0 + hl - 2 + k
                    parts.append(jnp.sum(d * ubuf[g, off:off + _CHUNK, :], axis=0, keepdims=True))
                    fwd = r0 + 2 - k
                    acc = acc + w_refs[g][k:k + 1, :] * dbuf[g, fwd:fwd + _CHUNK, :]
                dup_ref[g, r0:r0 + _CHUNK, :] = acc.astype(dup_ref.dtype)
                sums = [p if s is None else s + p for s, p in zip(sums, parts)]
            for r in range(4):
                s_ref[r:r + 1, :] += sums[r]

    hb = tm // hl
    n_halo = t // hl
    return pl.pallas_call(
        body, name="ffn_gate_bwd", grid=(nc, t // tm),
        in_specs=[pl.BlockSpec((2, tm, tc), lambda j, i: (0, i, j)),
                  pl.BlockSpec((2, hl, tc), lambda j, i: (0, jnp.maximum(i * hb - 1, 0), j)),
                  pl.BlockSpec((2, hl, tc), lambda j, i: (0, jnp.minimum((i + 1) * hb, n_halo - 1), j)),
                  pl.BlockSpec((tm, tc), lambda j, i: (i, j)),
                  pl.BlockSpec((hl, tc), lambda j, i: (jnp.minimum((i + 1) * hb, n_halo - 1), j)),
                  pl.BlockSpec((8, tc), lambda j, i: (0, j)), pl.BlockSpec((8, tc), lambda j, i: (0, nc + j)),
                  pl.BlockSpec((1, tc), lambda j, i: (0, j)), pl.BlockSpec((1, tc), lambda j, i: (0, nc + j))],
        out_specs=[pl.BlockSpec((2, tm, tc), lambda j, i: (0, i, j)),
                   pl.BlockSpec((8, tc), lambda j, i: (0, j)), pl.BlockSpec((8, tc), lambda j, i: (0, j))],
        out_shape=[jax.ShapeDtypeStruct((2, t, f), _ACT), jax.ShapeDtypeStruct((8, f), f32), jax.ShapeDtypeStruct((8, f), f32)],
        scratch_shapes=[pltpu.VMEM((2, hl + tm + hl, tc), f32), pltpu.VMEM((2, tm + hl, tc), f32)],
        compiler_params=_params(2),
    )(up, up, up, da, da, fw, fw, fb, fb)


def _adamw_math(w, g, m, v):
    m = ADAM_B1 * m + (1.0 - ADAM_B1) * g
    v = ADAM_B2 * v + (1.0 - ADAM_B2) * (g * g)
    m_hat = m / (1.0 - ADAM_B1 ** ADAM_STEP)
    v_hat = v / (1.0 - ADAM_B2 ** ADAM_STEP)
    delta = -ADAM_LR * (m_hat / (jnp.sqrt(v_hat) + ADAM_EPS) + ADAM_WD * w)
    return delta, m, v


def _adamw_shard(name, w, g, m, v):
    _, r, c = w.shape
    tr = next((cand for cand in (256, 176, 128, 64, 32, 16, 8) if r % cand == 0), r)

    def body(w_ref, g_ref, m_ref, v_ref, d_ref, mo_ref, vo_ref):
        d, mn, vn = _adamw_math(w_ref[...], g_ref[...], m_ref[...], v_ref[...])
        d_ref[...] = d
        mo_ref[...] = mn
        vo_ref[...] = vn

    s3 = pl.BlockSpec((None, tr, c), lambda i: (0, i, 0))
    s2 = pl.BlockSpec((tr, c), lambda i: (i, 0))
    shp = jax.ShapeDtypeStruct(w.shape, f32)
    return pl.pallas_call(
        body, name=name, grid=(r // tr,), in_specs=[s3, s2, s3, s3], out_specs=[s3, s3, s3], out_shape=[shp, shp, shp],
        compiler_params=_params(1),
    )(w, g, m, v)


def _adamw_small(quads):
    n = len(quads)

    def body(*refs):
        ins, outs = refs[:4 * n], refs[4 * n:]
        for p in range(n):
            w_ref, g_ref, m_ref, v_ref = ins[4 * p:4 * p + 4]
            d, mn, vn = _adamw_math(w_ref[...], g_ref[...], m_ref[...], v_ref[...])
            outs[3 * p][...] = d
            outs[3 * p + 1][...] = mn
            outs[3 * p + 2][...] = vn

    flat = [a for q in quads for a in q]
    shapes = [jax.ShapeDtypeStruct(q[0].shape, f32) for q in quads for _ in range(3)]
    outs = pl.pallas_call(
        body, name="adamw_small", in_specs=[_VMEM] * (4 * n), out_specs=[_VMEM] * (3 * n), out_shape=shapes,
        compiler_params=pltpu.CompilerParams(vmem_limit_bytes=_VMEM_LIMIT_BYTES),
    )(*flat)
    return [tuple(outs[3 * p:3 * p + 3]) for p in range(n)]


def _sum_pairs(name, place, grads, got):
    _, r, c = grads.shape

    def body(place_ref, a_ref, b_ref, o_ref):
        o_ref[...] = (a_ref[...].astype(f32) + b_ref[...].astype(f32)).astype(o_ref.dtype)

    grid_spec = pltpu.PrefetchScalarGridSpec(
        num_scalar_prefetch=1, grid=(4,),
        in_specs=[pl.BlockSpec((None, r, c), lambda i, p: (2 * i + p[1], 0, 0)), pl.BlockSpec((None, r, c), lambda i, p: (i, 0, 0))],
        out_specs=pl.BlockSpec((None, r, c), lambda i, p: (i, 0, 0)))
    return pl.pallas_call(body, name=name, grid_spec=grid_spec, out_shape=jax.ShapeDtypeStruct((4, r, c), _ACT),
                          compiler_params=_params(1))(place, grads, got)


def _sum_four(name, place, sums, got):
    _, r, c = sums.shape

    def body(place_ref, o_ref, g_ref, f_ref):
        s = o_ref[...].astype(f32) + g_ref[0].astype(f32)
        s = s + g_ref[1].astype(f32)
        f_ref[...] = s + g_ref[2].astype(f32)

    grid_spec = pltpu.PrefetchScalarGridSpec(
        num_scalar_prefetch=1, grid=(1,),
        in_specs=[pl.BlockSpec((None, r, c), lambda i, p: (p[0], 0, 0)), pl.BlockSpec((3, r, c), lambda i, p: (0, 0, 0))],
        out_specs=pl.BlockSpec((None, r, c), lambda i, p: (p[1], 0, 0)))
    return pl.pallas_call(body, name=name, grid_spec=grid_spec, out_shape=jax.ShapeDtypeStruct((2, r, c), f32),
                          compiler_params=_params(1))(place, sums, got)


def _place():
    return lax.axis_index("x"), lax.axis_index("y"), lax.axis_index("c")


def _other_chips(x, y):
    return [(1 - x, y), (x, 1 - y), (1 - x, 1 - y)]


def _remote(src, dst, send_sem, recv_sem, to):
    return pltpu.make_async_remote_copy(src_ref=src, dst_ref=dst, send_sem=send_sem, recv_sem=recv_sem,
                                        device_id=to, device_id_type=_MESH)


def _allgather_weights(fulls, col_sharded, smalls):
    nw = len(fulls)
    ns = len(smalls)

    def body(*refs):
        buf = refs[nw + ns:2 * nw + ns]
        sbuf = refs[2 * nw + ns:2 * (nw + ns)]
        ici_send, ici_recv, d2d_send, d2d_recv, s_send, s_recv = refs[2 * (nw + ns):]
        x, y, c = _place()
        me_chip = 2 * x + y
        chips = _other_chips(x, y)

        def piece(w, s, h):
            rows, cols = fulls[w].shape
            if col_sharded[w]:
                return buf[w].at[pl.ds(h * (rows // 2), rows // 2), pl.ds(s * (cols // 4), cols // 4)]
            return buf[w].at[pl.ds((2 * s + h) * (rows // 8), rows // 8), :]

        def small_piece(w, s):
            pc = smalls[w].shape[1] // 4
            return sbuf[w].at[:, pl.ds(s * pc, pc)]

        first = []
        for w in range(nw):
            for j, chip in enumerate(chips):
                mine = piece(w, me_chip, c)
                first.append(_remote(mine, mine, ici_send.at[3 * w + j], ici_recv.at[3 * w + j], (*chip, c)))
        for w in range(ns):
            for j, chip in enumerate(chips):
                mine = small_piece(w, me_chip)
                first.append(_remote(mine, mine, s_send.at[3 * w + j], s_recv.at[3 * w + j], (*chip, c)))
        for cp in first:
            cp.start()
        passed = []
        for w in range(nw):
            for j, chip in enumerate(chips):
                landed = piece(w, 2 * chip[0] + chip[1], c)
                _remote(landed, landed, ici_send.at[3 * w + j], ici_recv.at[3 * w + j], (x, y, c)).wait_recv()
                cp = _remote(landed, landed, d2d_send.at[3 * w + j], d2d_recv.at[3 * w + j], (x, y, 1 - c))
                cp.start()
                passed.append(cp)
        for w in range(nw):
            for j, chip in enumerate(chips):
                other = piece(w, 2 * chip[0] + chip[1], 1 - c)
                _remote(other, other, d2d_send.at[3 * w + j], d2d_recv.at[3 * w + j], (x, y, c)).wait_recv()
        for w in range(ns):
            for j, chip in enumerate(chips):
                landed = small_piece(w, 2 * chip[0] + chip[1])
                _remote(landed, landed, s_send.at[3 * w + j], s_recv.at[3 * w + j], (x, y, c)).wait_recv()
        for cp in first + passed:
            cp.wait_send()

    n = nw + ns
    return pl.pallas_call(
        body, name="allgather_weights", in_specs=[_ANY] * n, out_specs=[_ANY] * n,
        out_shape=[jax.ShapeDtypeStruct(a.shape, a.dtype) for a in list(fulls) + list(smalls)],
        input_output_aliases={i: i for i in range(n)},
        scratch_shapes=[pltpu.SemaphoreType.DMA((3 * nw,)), pltpu.SemaphoreType.DMA((3 * nw,)),
                        pltpu.SemaphoreType.DMA((3 * nw,)), pltpu.SemaphoreType.DMA((3 * nw,)),
                        pltpu.SemaphoreType.DMA((3 * ns,)), pltpu.SemaphoreType.DMA((3 * ns,))],
    )(*fulls, *smalls)


def _exchange_pair_halves(grads):
    nw = len(grads)

    def body(*refs):
        src = refs[:nw]
        got = refs[nw:2 * nw]
        send_sem, recv_sem = refs[2 * nw:]
        x, y, c = _place()
        sends = []
        for w in range(nw):
            for s in range(4):
                rc = _remote(src[w].at[2 * s + 1 - c], got[w].at[s], send_sem.at[4 * w + s], recv_sem.at[4 * w + s], (x, y, 1 - c))
                rc.start()
                sends.append(rc)
        for rc in sends:
            rc.wait_recv()
        for rc in sends:
            rc.wait_send()

    return pl.pallas_call(
        body, name="rs_pair_exchange", in_specs=[_ANY] * nw, out_specs=[_ANY] * nw,
        out_shape=[jax.ShapeDtypeStruct((4,) + g.shape[1:], g.dtype) for g in grads],
        scratch_shapes=[pltpu.SemaphoreType.DMA((4 * nw,)), pltpu.SemaphoreType.DMA((4 * nw,))],
    )(*grads)


def _exchange_chip_pieces(sums):
    nw = len(sums)

    def body(*refs):
        src = refs[:nw]
        got = refs[nw:2 * nw]
        send_sem, recv_sem = refs[2 * nw:]
        x, y, c = _place()
        sends = []
        for w in range(nw):
            for j, chip in enumerate(_other_chips(x, y)):
                s = 2 * chip[0] + chip[1]
                rc = _remote(src[w].at[s], got[w].at[j], send_sem.at[3 * w + j], recv_sem.at[3 * w + j], (*chip, c))
                rc.start()
                sends.append(rc)
        for rc in sends:
            rc.wait_recv()
        for rc in sends:
            rc.wait_send()

    return pl.pallas_call(
        body, name="rs_chip_exchange", in_specs=[_ANY] * nw, out_specs=[_ANY] * nw,
        out_shape=[jax.ShapeDtypeStruct((3,) + g.shape[1:], g.dtype) for g in sums],
        scratch_shapes=[pltpu.SemaphoreType.DMA((3 * nw,)), pltpu.SemaphoreType.DMA((3 * nw,))],
    )(*sums)


def _swap_halves(finals):
    nw = len(finals)

    def body(*refs):
        buf = refs[nw:2 * nw]
        send_sem, recv_sem = refs[2 * nw:]
        x, y, c = _place()
        sends = []
        for w in range(nw):
            rc = _remote(buf[w].at[c], buf[w].at[c], send_sem.at[w], recv_sem.at[w], (x, y, 1 - c))
            rc.start()
            sends.append(rc)
        for w in range(nw):
            _remote(buf[w].at[1 - c], buf[w].at[1 - c], send_sem.at[w], recv_sem.at[w], (x, y, c)).wait_recv()
        for rc in sends:
            rc.wait_send()

    return pl.pallas_call(
        body, name="rs_swap_halves", in_specs=[_ANY] * nw, out_specs=[_ANY] * nw,
        out_shape=[jax.ShapeDtypeStruct(g.shape, g.dtype) for g in finals],
        input_output_aliases={i: i for i in range(nw)},
        scratch_shapes=[pltpu.SemaphoreType.DMA((nw,)), pltpu.SemaphoreType.DMA((nw,))],
    )(*finals)


def _allreduce_small(parts):
    n = len(parts)

    def body(*refs):
        src = refs[:n]
        out = refs[n:2 * n]
        slots = refs[2 * n:3 * n]
        send_sem, recv_sem = refs[3 * n:]
        x, y, c = _place()
        me = 4 * x + 2 * y + c
        flips = [(bx, by, bc) for bx in (0, 1) for by in (0, 1) for bc in (0, 1)][1:]
        sends = []
        for a in range(n):
            slots[a][me] = src[a][...]
        for k, (bx, by, bc) in enumerate(flips):
            to = (1 - x if bx else x, 1 - y if by else y, 1 - c if bc else c)
            for a in range(n):
                rc = _remote(src[a], slots[a].at[me], send_sem.at[k, a], recv_sem.at[k, a], to)
                rc.start()
                sends.append(rc)
        for rc in sends:
            rc.wait_recv()
        for a in range(n):
            s = slots[a][0]
            for d in range(1, 8):
                s = s + slots[a][d]
            out[a][...] = s
        for rc in sends:
            rc.wait_send()

    return pl.pallas_call(
        body, name="allreduce_small", in_specs=[_VMEM] * n, out_specs=[_VMEM] * n,
        out_shape=[jax.ShapeDtypeStruct(p.shape, f32) for p in parts],
        scratch_shapes=[pltpu.VMEM((8,) + p.shape, f32) for p in parts] + [pltpu.SemaphoreType.DMA((7, n)), pltpu.SemaphoreType.DMA((7, n))],
        compiler_params=pltpu.CompilerParams(vmem_limit_bytes=_VMEM_LIMIT_BYTES),
    )(*parts)


def _local_step(x, mem, tgt, g_mix, g_xattn, g_mem, g_ffn, g_final, cb, lg, lb, pw, ps, fb,
                w_in, w_out, w_q, w_kv, w_o, w_up, w_down, cw, fw, n_seq, seq, n_mem):
    t, d = x.shape
    f = fb.shape[1] // 2
    c = cb.shape[1]
    h1 = _rms_fwd("norm_mix", x, g_mix)
    u = _mm_nn("proj_in", h1, w_in, _ACT, w_in.shape[1])
    y, hc = _mix_fwd(u, cw, cb, lg, lb, pw, ps, seq)
    x1 = _mm_nn("proj_out", y, w_out, f32, d, res=x)
    h2 = _rms_fwd("norm_xattn", x1, g_xattn)
    q = _mm_nn("proj_q", h2, w_q, _ACT, d)
    mem_n = _rms_fwd("norm_mem", mem, g_mem)
    kv = _mm_nn("proj_kv", mem_n, w_kv, _ACT, 2 * d)
    o = _attn_fwd(q, kv, n_seq, seq, n_mem)
    x2 = _mm_nn("proj_o", o, w_o, f32, d, res=x1)
    h3 = _rms_fwd("norm_ffn", x2, g_ffn)
    up = _mm_nn("proj_up", h3, w_up, _ACT, f, split_out=True)
    a = _ffn_gate_fwd(up, fw, fb, seq)
    x3 = _mm_nn("proj_down", a, w_down, f32, d, res=x2)
    dx3, dx3b, dg_final, loss = _final_loss_bwd(x3, g_final, tgt)
    da = _mm_nt("d_act", dx3b, w_down, _ACT)
    gw_down = _mm_tn_rows("dw_down", a, dx3b, min(512, t), 512)
    dup, sums_g, sums_v = _ffn_gate_bwd(up, da, fw, fb, seq)
    dh3 = _mm_nt("d_h3", dup, w_up, f32, nk=2)
    gw_up = _mm_tn_pieces("dw_up", h3, dup, f // 2, min(512, t))
    dx2, dx2b, dg_ffn = _rms_bwd("norm_ffn_bwd", x2, g_ffn, dh3, dx3)
    do = _mm_nt("d_o", dx2b, w_o, _ACT)
    gw_o = _mm_tn_rows("dw_o", o, dx2b, min(1024, t), d)
    dq, dk, dv = _attn_bwd(q, kv, do, n_seq, seq, n_mem)
    dkv = jnp.concatenate([dk, dv], axis=1)
    dh2 = _mm_nt("d_h2", dq, w_q, f32)
    gw_q = _mm_tn_rows("dw_q", h2, dq, min(1024, t), d)
    gw_kv = _mm_tn_pieces("dw_kv", mem_n, dkv, d // 2, mem.shape[0])
    dmem_n = _mm_nt("d_mem_n", dkv, w_kv, f32)
    dg_mem = _rms_gain_grad("norm_mem_bwd", mem, dmem_n)
    dx1, dx1b, dg_xattn = _rms_bwd("norm_xattn_bwd", x1, g_xattn, dh2, dx2)
    dy = _mm_nt("d_y", dx1b, w_out, _ACT)
    gw_out = _mm_tn_rows("dw_out", y, dx1b, min(1024, t), d)
    dhc, sums_norm = _mix_bwd_norm(hc, dy, lg, lb, seq)
    du, d_cw, d_ps, d_pw = _mix_bwd_taps(u, dhc, dy, cw, pw, ps, seq)
    dh1 = _mm_nt("d_h1", du, w_in, f32)
    gw_in = _mm_tn_pieces("dw_in", h1, du, c * 3 // 4, min(2048, t))
    grad_x, _, dg_mix = _rms_bwd("norm_mix_bwd", x, g_mix, dh1, dx1)
    zero_row = jnp.zeros((1, d), f32)
    gains = jnp.concatenate([dg_mix, dg_xattn, dg_mem, dg_ffn, dg_final, jnp.pad(loss, ((0, 0), (0, d - 1))), zero_row, zero_row], axis=0)
    conv_rows = jnp.concatenate([sums_norm[2:3], sums_norm[0:1], sums_norm[1:2], d_ps[0:1], jnp.zeros((4, c), f32)], axis=0)
    ffn_rows = jnp.concatenate([sums_g, sums_v], axis=1)
    big = [gw_in, gw_kv, gw_up,
           gw_out.reshape(8, -1, d), gw_q.reshape(8, -1, d), gw_o.reshape(8, -1, d), gw_down.reshape(8, -1, d)]
    small = [gains, conv_rows, d_pw.reshape(-1, d_pw.shape[-1]), ffn_rows, d_cw]
    return grad_x, big, small


def kernel(x, mem, norm_mix_g, w_in, conv_dw_w, conv_dw_b, conv_ln_g, conv_ln_b, pool_w, pool_scale, w_out, norm_xattn_g, norm_mem_g, w_q, w_kv, w_o, norm_ffn_g, w_up, ffn_dw_w, ffn_dw_b, w_down, norm_final_g, loss_target, m_norm_mix_g, m_w_in, m_conv_dw_w, m_conv_dw_b, m_conv_ln_g, m_conv_ln_b, m_pool_w, m_pool_scale, m_w_out, m_norm_xattn_g, m_norm_mem_g, m_w_q, m_w_kv, m_w_o, m_norm_ffn_g, m_w_up, m_ffn_dw_w, m_ffn_dw_b, m_w_down, m_norm_final_g, v_norm_mix_g, v_w_in, v_conv_dw_w, v_conv_dw_b, v_conv_ln_g, v_conv_ln_b, v_pool_w, v_pool_scale, v_w_out, v_norm_xattn_g, v_norm_mem_g, v_w_q, v_w_kv, v_w_o, v_norm_ffn_g, v_w_up, v_ffn_dw_w, v_ffn_dw_b, v_w_down, v_norm_final_g):
    n_seq, seq, d = x.shape
    n_mem = mem.shape[1]
    chip = 2 * lax.axis_index("x") + lax.axis_index("y")

    col_w = [w_in, w_kv, w_up]
    row_w = [w_out, w_q, w_o, w_down]
    def placed(shard, rows, col_sharded):
        r, cs = shard.shape
        if col_sharded:
            full = jnp.zeros((rows, 4 * cs), shard.dtype)
            return lax.dynamic_update_slice(full, shard, (0, chip * cs))
        full = jnp.zeros((4 * r, cs), shard.dtype)
        return lax.dynamic_update_slice(full, shard, (chip * r, 0))

    fulls = [placed(w[0].astype(_ACT), w.shape[1], True) for w in col_w]
    fulls += [placed(w[0].astype(_ACT), None, False) for w in row_w]
    kw = conv_dw_w.shape[1]
    smalls = [placed(conv_dw_w[0], _HALO, True), placed(ffn_dw_w[0], 8, True)]
    fw_in, fw_kv, fw_up, fw_out, fw_q, fw_o, fw_down, cw, fw = _allgather_weights(
        fulls, [True] * 3 + [False] * 4, smalls)

    grad_x, big, small = _local_step(
        x.reshape(n_seq * seq, d), mem.reshape(n_seq * n_mem, d), loss_target.reshape(n_seq * seq, d),
        norm_mix_g, norm_xattn_g, norm_mem_g, norm_ffn_g, norm_final_g.reshape(1, d),
        conv_dw_b, conv_ln_g, conv_ln_b, pool_w[0], pool_scale, ffn_dw_b,
        fw_in, fw_out, fw_q, fw_kv, fw_o, fw_up, fw_down, cw, fw, n_seq, seq, n_mem)

    names = ["w_in", "w_kv", "w_up", "w_out", "w_q", "w_o", "w_down"]
    place = jnp.stack([chip, lax.axis_index("c")]).astype(jnp.int32)
    got = _exchange_pair_halves(big)
    chip_sums = [_sum_pairs("rs_pair_sum_" + n, place, a, b) for n, a, b in zip(names, big, got)]
    others = _exchange_chip_pieces(chip_sums)
    finals = [_sum_four("rs_chip_sum_" + n, place, a, b) for n, a, b in zip(names, chip_sums, others)]
    shard_grads = _swap_halves(finals)

    gains, conv_rows, d_pw, ffn_rows, d_cw = _allreduce_small(small)
    loss = gains[5, 0]

    outs = {}
    big_w = dict(zip(names, col_w + row_w))
    big_m = dict(w_in=m_w_in, w_kv=m_w_kv, w_up=m_w_up, w_out=m_w_out, w_q=m_w_q, w_o=m_w_o, w_down=m_w_down)
    big_v = dict(w_in=v_w_in, w_kv=v_w_kv, w_up=v_w_up, w_out=v_w_out, w_q=v_w_q, w_o=v_w_o, w_down=v_w_down)
    for n, g in zip(names, shard_grads):
        w = big_w[n]
        g2 = g.reshape(w.shape[1], w.shape[2])
        delta, new_m, new_v = _adamw_shard("adamw_" + n, w, g2, big_m[n], big_v[n])
        outs[n] = (g2.reshape(w.shape), delta, new_m, new_v)

    f2 = ffn_dw_b.shape[1]
    cs_c = conv_dw_w.shape[2]
    cs_f = ffn_dw_w.shape[2]
    g_cw = lax.dynamic_slice(d_cw, (0, chip * cs_c), (kw, cs_c)).reshape(conv_dw_w.shape)
    g_fw = lax.dynamic_slice(ffn_rows, (1, chip * cs_f), (ffn_dw_w.shape[1], cs_f)).reshape(ffn_dw_w.shape)
    small_params = [
        ("norm_mix_g", norm_mix_g, gains[0:1], m_norm_mix_g, v_norm_mix_g),
        ("conv_dw_w", conv_dw_w, g_cw, m_conv_dw_w, v_conv_dw_w),
        ("conv_dw_b", conv_dw_b, conv_rows[0:1], m_conv_dw_b, v_conv_dw_b),
        ("conv_ln_g", conv_ln_g, conv_rows[1:2], m_conv_ln_g, v_conv_ln_g),
        ("conv_ln_b", conv_ln_b, conv_rows[2:3], m_conv_ln_b, v_conv_ln_b),
        ("pool_w", pool_w, d_pw.reshape(pool_w.shape), m_pool_w, v_pool_w),
        ("pool_scale", pool_scale, conv_rows[3:4], m_pool_scale, v_pool_scale),
        ("norm_xattn_g", norm_xattn_g, gains[1:2], m_norm_xattn_g, v_norm_xattn_g),
        ("norm_mem_g", norm_mem_g, gains[2:3], m_norm_mem_g, v_norm_mem_g),
        ("norm_ffn_g", norm_ffn_g, gains[3:4], m_norm_ffn_g, v_norm_ffn_g),
        ("ffn_dw_w", ffn_dw_w, g_fw, m_ffn_dw_w, v_ffn_dw_w),
        ("ffn_dw_b", ffn_dw_b, ffn_rows[0:1, :f2], m_ffn_dw_b, v_ffn_dw_b),
        ("norm_final_g", norm_final_g.reshape(1, d), gains[4:5], m_norm_final_g.reshape(1, d), v_norm_final_g.reshape(1, d)),
    ]
    quads = []
    for _, w, g, m, v in small_params:
        shape2 = (-1, w.shape[-1])
        quads.append((w.reshape(shape2), g.reshape(shape2), m.reshape(shape2), v.reshape(shape2)))
    for (n, w, g, _, _), (delta, new_m, new_v) in zip(small_params, _adamw_small(quads)):
        shape = norm_final_g.shape if n == "norm_final_g" else w.shape
        outs[n] = (g.reshape(shape), delta.reshape(shape), new_m.reshape(shape), new_v.reshape(shape))

    order = ["norm_mix_g", "w_in", "conv_dw_w", "conv_dw_b", "conv_ln_g", "conv_ln_b", "pool_w", "pool_scale", "w_out",
             "norm_xattn_g", "norm_mem_g", "w_q", "w_kv", "w_o", "norm_ffn_g", "w_up", "ffn_dw_w", "ffn_dw_b", "w_down",
             "norm_final_g"]
    return (loss, grad_x.reshape(x.shape), *[outs[n][0] for n in order], *[outs[n][1] for n in order],
            *[outs[n][2] for n in order], *[outs[n][3] for n in order])
```

```python
import functools

import jax
import jax.numpy as jnp
from jax import lax
from jax.experimental import pallas as pl
from jax.experimental.pallas import tpu as pltpu

f32 = jnp.float32
_ACT = jnp.bfloat16

EPS = 1e-6
POOL_WINDOWS = (2, 4, 8, 16)
XATTN_HEADS = 4
ADAM_LR = 0.001
ADAM_B1 = 0.9
ADAM_B2 = 0.999
ADAM_EPS = 1e-08
ADAM_WD = 0.01
ADAM_STEP = 10

_VMEM_LIMIT_BYTES = 56 * 1024 * 1024
_MESH = pl.DeviceIdType.MESH
_ANY = pl.BlockSpec(memory_space=pl.ANY)
_VMEM = pl.BlockSpec(memory_space=pltpu.VMEM)
_HBM = pl.BlockSpec(memory_space=pltpu.HBM)
_SEM = pl.BlockSpec(memory_space=pltpu.SEMAPHORE)
_EFFECT = pltpu.SideEffectType.DATAFLOW_SIDE_EFFECTING

_NN = (((1,), (0,)), ((), ()))
_NT = (((1,), (1,)), ((), ()))
_TN = (((0,), (0,)), ((), ()))


def _params(n_grid):
    return pltpu.CompilerParams(dimension_semantics=("arbitrary",) * n_grid, vmem_limit_bytes=_VMEM_LIMIT_BYTES)


def _sigmoid(v):
    return 1.0 / (1.0 + jnp.exp(-v))


def _dot(a, b, dims):
    return lax.dot_general(a, b, dims, preferred_element_type=f32)


def _mm(name, a, b, *, dims, grid, a_spec, b_spec, o_spec, out_shape, nk, acc_shape=None, res=None, res_spec=None):
    def body(*refs):
        if res is None:
            a_ref, b_ref, o_ref, *scratch = refs
            r_ref = None
        else:
            a_ref, b_ref, r_ref, o_ref, *scratch = refs
        p = _dot(a_ref[...], b_ref[...], dims)

        def finish(v):
            if r_ref is not None:
                v = v + r_ref[...]
            o_ref[...] = v.astype(o_ref.dtype)

        if nk == 1:
            finish(p)
        else:
            acc = scratch[0]
            k = pl.program_id(2)

            @pl.when(k == 0)
            def _():
                acc[...] = p

            @pl.when(k > 0)
            def _():
                acc[...] += p

            @pl.when(k == nk - 1)
            def _():
                finish(acc[...])

    ins = [a, b] + ([] if res is None else [res])
    specs = [a_spec, b_spec] + ([] if res is None else [res_spec])
    return pl.pallas_call(
        body, name=name, grid=grid, in_specs=specs, out_specs=o_spec, out_shape=out_shape,
        scratch_shapes=[pltpu.VMEM(acc_shape, f32)] if nk > 1 else [], compiler_params=_params(3),
    )(*ins)


def _row_tile(m):
    return min(512, m)


def _mm_nn(name, a, b, out_dtype, tn, res=None, split_out=False):
    m, k = a.shape
    n = b.shape[1]
    tm = _row_tile(m)
    if split_out:
        out_shape = jax.ShapeDtypeStruct((n // tn, m, tn), out_dtype)
        o_spec = pl.BlockSpec((None, tm, tn), lambda j, i, kk: (j, i, 0))
    else:
        out_shape = jax.ShapeDtypeStruct((m, n), out_dtype)
        o_spec = pl.BlockSpec((tm, tn), lambda j, i, kk: (i, j))
    return _mm(
        name, a, b, dims=_NN, grid=(n // tn, m // tm, 1), nk=1,
        a_spec=pl.BlockSpec((tm, k), lambda j, i, kk: (i, 0)),
        b_spec=pl.BlockSpec((k, tn), lambda j, i, kk: (0, j)),
        o_spec=o_spec, out_shape=out_shape, res=res,
        res_spec=pl.BlockSpec((tm, tn), lambda j, i, kk: (i, j)),
    )


def _mm_nt(name, a, b, out_dtype, nk=1):
    n, kc = b.shape
    tk = kc // nk
    if a.ndim == 3:
        m = a.shape[1]
        tm = _row_tile(m)
        a_spec = pl.BlockSpec((None, tm, tk), lambda i, j, k: (k, i, 0))
    else:
        m = a.shape[0]
        tm = _row_tile(m)
        a_spec = pl.BlockSpec((tm, tk), lambda i, j, k: (i, k))
    return _mm(
        name, a, b, dims=_NT, grid=(m // tm, 1, nk), nk=nk, acc_shape=(tm, n),
        a_spec=a_spec, b_spec=pl.BlockSpec((n, tk), lambda i, j, k: (0, k)),
        o_spec=pl.BlockSpec((tm, n), lambda i, j, k: (i, 0)),
        out_shape=jax.ShapeDtypeStruct((m, n), out_dtype),
    )


def _mm_tn_rows(name, a, b, tt, tn):
    m, ka = a.shape
    nb = b.shape[1]
    nk = m // tt
    return _mm(
        name, a, b, dims=_TN, grid=(1, nb // tn, nk), nk=nk, acc_shape=(ka, tn),
        a_spec=pl.BlockSpec((tt, ka), lambda i, j, k: (k, 0)),
        b_spec=pl.BlockSpec((tt, tn), lambda i, j, k: (k, j)),
        o_spec=pl.BlockSpec((ka, tn), lambda i, j, k: (0, j)),
        out_shape=jax.ShapeDtypeStruct((ka, nb), _ACT),
    )


def _mm_tn_pieces(name, a, b, cs, tt):
    m, ka = a.shape
    nk = m // tt
    if b.ndim == 3:
        b_spec = pl.BlockSpec((None, tt, cs), lambda i, j, k: (j // 2, k, j % 2))
    else:
        b_spec = pl.BlockSpec((tt, cs), lambda i, j, k: (k, j))
    return _mm(
        name, a, b, dims=_TN, grid=(2, 4, nk), nk=nk, acc_shape=(ka // 2, cs),
        a_spec=pl.BlockSpec((tt, ka // 2), lambda i, j, k: (k, i)), b_spec=b_spec,
        o_spec=pl.BlockSpec((None, ka // 2, cs), lambda i, j, k: (2 * j + i, 0, 0)),
        out_shape=jax.ShapeDtypeStruct((8, ka // 2, cs), _ACT),
    )


def _rms_fwd(name, x, g):
    t, d = x.shape
    tm = _row_tile(t)

    def body(x_ref, g_ref, h_ref):
        xv = x_ref[...]
        r = lax.rsqrt(jnp.mean(xv * xv, axis=-1, keepdims=True) + EPS)
        h_ref[...] = (xv * r * g_ref[...]).astype(h_ref.dtype)

    return pl.pallas_call(
        body, name=name, grid=(t // tm,),
        in_specs=[pl.BlockSpec((tm, d), lambda i: (i, 0)), pl.BlockSpec((1, d), lambda i: (0, 0))],
        out_specs=pl.BlockSpec((tm, d), lambda i: (i, 0)), out_shape=jax.ShapeDtypeStruct((t, d), _ACT),
        compiler_params=_params(1),
    )(x, g)


def _rms_bwd(name, x, g, dh, dres):
    t, d = x.shape
    tm = _row_tile(t)

    def body(x_ref, g_ref, dh_ref, dres_ref, dx_ref, dxb_ref, dg_ref):
        @pl.when(pl.program_id(0) == 0)
        def _():
            dg_ref[...] = jnp.zeros_like(dg_ref)

        xv = x_ref[...]
        r = lax.rsqrt(jnp.mean(xv * xv, axis=-1, keepdims=True) + EPS)
        xn = xv * r
        dhv = dh_ref[...].astype(f32)
        dxn = dhv * g_ref[...]
        dx = r * (dxn - xn * jnp.mean(dxn * xn, axis=-1, keepdims=True)) + dres_ref[...]
        dx_ref[...] = dx
        dxb_ref[...] = dx.astype(dxb_ref.dtype)
        dg_ref[...] += jnp.sum(dhv * xn, axis=0, keepdims=True)

    row = pl.BlockSpec((tm, d), lambda i: (i, 0))
    vec = pl.BlockSpec((1, d), lambda i: (0, 0))
    return pl.pallas_call(
        body, name=name, grid=(t // tm,), in_specs=[row, vec, row, row], out_specs=[row, row, vec],
        out_shape=[jax.ShapeDtypeStruct((t, d), f32), jax.ShapeDtypeStruct((t, d), _ACT), jax.ShapeDtypeStruct((1, d), f32)],
        compiler_params=_params(1),
    )(x, g, dh, dres)


def _rms_gain_grad(name, x, dh):
    t, d = x.shape
    tm = _row_tile(t)

    def body(x_ref, dh_ref, dg_ref):
        @pl.when(pl.program_id(0) == 0)
        def _():
            dg_ref[...] = jnp.zeros_like(dg_ref)

        xv = x_ref[...]
        r = lax.rsqrt(jnp.mean(xv * xv, axis=-1, keepdims=True) + EPS)
        dg_ref[...] += jnp.sum(dh_ref[...] * (xv * r), axis=0, keepdims=True)

    row = pl.BlockSpec((tm, d), lambda i: (i, 0))
    return pl.pallas_call(
        body, name=name, grid=(t // tm,), in_specs=[row, row], out_specs=pl.BlockSpec((1, d), lambda i: (0, 0)),
        out_shape=jax.ShapeDtypeStruct((1, d), f32), compiler_params=_params(1),
    )(x, dh)


def _final_loss_bwd(x, g, tgt):
    t, d = x.shape
    tm = _row_tile(t)

    def body(x_ref, g_ref, t_ref, dx_ref, dxb_ref, dg_ref, loss_ref):
        @pl.when(pl.program_id(0) == 0)
        def _():
            dg_ref[...] = jnp.zeros_like(dg_ref)
            loss_ref[...] = jnp.zeros_like(loss_ref)

        xv = x_ref[...]
        gv = g_ref[...]
        r = lax.rsqrt(jnp.mean(xv * xv, axis=-1, keepdims=True) + EPS)
        xn = xv * r
        err = xn * gv - t_ref[...]
        loss_ref[...] += 0.5 * jnp.sum(jnp.mean(err * err, axis=-1, keepdims=True), axis=0, keepdims=True)
        dout = err * (1.0 / d)
        dxn = dout * gv
        dx = r * (dxn - xn * jnp.mean(dxn * xn, axis=-1, keepdims=True))
        dx_ref[...] = dx
        dxb_ref[...] = dx.astype(dxb_ref.dtype)
        dg_ref[...] += jnp.sum(dout * xn, axis=0, keepdims=True)

    row = pl.BlockSpec((tm, d), lambda i: (i, 0))
    vec = pl.BlockSpec((1, d), lambda i: (0, 0))
    one = pl.BlockSpec((1, 1), lambda i: (0, 0))
    return pl.pallas_call(
        body, name="final_loss_bwd", grid=(t // tm,), in_specs=[row, vec, row], out_specs=[row, row, vec, one],
        out_shape=[jax.ShapeDtypeStruct((t, d), f32), jax.ShapeDtypeStruct((t, d), _ACT),
                   jax.ShapeDtypeStruct((1, d), f32), jax.ShapeDtypeStruct((1, 1), f32)],
        compiler_params=_params(1),
    )(x, g, tgt)


_CONV_ROWS = 256
_CHUNK = 64
_HALO = 32


def _pool_counts(pos, w):
    return jnp.minimum(pos + 1.0, float(w))


def _mix_fwd(u, cw, cb, lg, lb, pw, ps, seq):
    t, c3 = u.shape
    c = c3 // 3
    kw = 31
    tm = min(_CONV_ROWS, seq)
    tps = seq // tm
    gd = c // len(POOL_WINDOWS)

    def body(u_ref, uh_ref, cw_ref, cb_ref, lg_ref, lb_ref, pw_ref, ps_ref, y_ref, hc_ref, hgbuf, pbuf):
        i = pl.program_id(0)
        keep = jnp.where(i % tps == 0, 0.0, 1.0)
        um = u_ref[...].astype(f32)
        uh = uh_ref[...].astype(f32) * keep
        hgbuf[0:_HALO, :] = uh[:, 0:c] * _sigmoid(uh[:, c:2 * c])
        hgbuf[_HALO:_HALO + tm, :] = um[:, 0:c] * _sigmoid(um[:, c:2 * c])
        pbuf[0:_HALO, :] = uh[:, 2 * c:]
        pbuf[_HALO:_HALO + tm, :] = um[:, 2 * c:]
        for r0 in range(0, tm, _CHUNK):
            acc = jnp.broadcast_to(cb_ref[...], (_CHUNK, c))
            for k in range(kw):
                off = r0 + _HALO - (kw - 1) + k
                acc = acc + cw_ref[k:k + 1, :] * hgbuf[off:off + _CHUNK, :]
            hc_ref[r0:r0 + _CHUNK, :] = acc
            mu = jnp.mean(acc, axis=-1, keepdims=True)
            xc = acc - mu
            var = jnp.mean(xc * xc, axis=-1, keepdims=True)
            hl = xc * lax.rsqrt(var + EPS) * lg_ref[...] + lb_ref[...]
            y_ref[r0:r0 + _CHUNK, 0:c] = (hl * _sigmoid(hl)).astype(y_ref.dtype)
        pos = ((i % tps) * tm).astype(f32) + lax.broadcasted_iota(jnp.int32, (tm, 1), 0).astype(f32)
        for gi, w in enumerate(POOL_WINDOWS):
            sl = slice(gi * gd, (gi + 1) * gd)
            v = pbuf[_HALO:_HALO + tm, sl]
            s = v
            for j in range(1, w):
                s = s + pbuf[_HALO - j:_HALO - j + tm, sl]
            pooled = s / _pool_counts(pos, w) - v
            mixed = _dot(pooled.astype(_ACT), pw_ref[gi].astype(_ACT), _NN)
            y_ref[:, c + gi * gd:c + (gi + 1) * gd] = (mixed * ps_ref[:, sl]).astype(y_ref.dtype)

    hb = tm // _HALO
    full = lambda shape: pl.BlockSpec(shape, lambda i: (0,) * len(shape))
    return pl.pallas_call(
        body, name="mix_fwd", grid=(t // tm,),
        in_specs=[pl.BlockSpec((tm, c3), lambda i: (i, 0)),
                  pl.BlockSpec((_HALO, c3), lambda i: (jnp.maximum(i * hb - 1, 0), 0)),
                  full((_HALO, c)), full((1, c)), full((1, c)), full((1, c)), full((len(POOL_WINDOWS), gd, gd)), full((1, c))],
        out_specs=[pl.BlockSpec((tm, 2 * c), lambda i: (i, 0)), pl.BlockSpec((tm, c), lambda i: (i, 0))],
        out_shape=[jax.ShapeDtypeStruct((t, 2 * c), _ACT), jax.ShapeDtypeStruct((t, c), f32)],
        scratch_shapes=[pltpu.VMEM((_HALO + tm, c), f32), pltpu.VMEM((_HALO + tm, c), f32)],
        compiler_params=_params(1),
    )(u, u, cw, cb, lg, lb, pw, ps)


def _mix_bwd_norm(hc, dy, lg, lb, seq):
    t, c = hc.shape
    tm = min(_CONV_ROWS, seq)

    def body(hc_ref, dy_ref, lg_ref, lb_ref, dhc_ref, sums_ref):
        @pl.when(pl.program_id(0) == 0)
        def _():
            sums_ref[...] = jnp.zeros_like(sums_ref)

        hcv = hc_ref[...]
        mu = jnp.mean(hcv, axis=-1, keepdims=True)
        xc = hcv - mu
        rstd = lax.rsqrt(jnp.mean(xc * xc, axis=-1, keepdims=True) + EPS)
        n = xc * rstd
        hl = n * lg_ref[...] + lb_ref[...]
        sg = _sigmoid(hl)
        dhl = dy_ref[...].astype(f32) * (sg * (1.0 + hl * (1.0 - sg)))
        dn = dhl * lg_ref[...]
        dhc = rstd * (dn - jnp.mean(dn, axis=-1, keepdims=True) - n * jnp.mean(dn * n, axis=-1, keepdims=True))
        dhc_ref[...] = dhc
        sums_ref[0:1, :] += jnp.sum(dhl * n, axis=0, keepdims=True)
        sums_ref[1:2, :] += jnp.sum(dhl, axis=0, keepdims=True)
        sums_ref[2:3, :] += jnp.sum(dhc, axis=0, keepdims=True)

    row = pl.BlockSpec((tm, c), lambda i: (i, 0))
    vec = pl.BlockSpec((1, c), lambda i: (0, 0))
    return pl.pallas_call(
        body, name="mix_bwd_norm", grid=(t // tm,), in_specs=[row, row, vec, vec],
        out_specs=[row, pl.BlockSpec((8, c), lambda i: (0, 0))],
        out_shape=[jax.ShapeDtypeStruct((t, c), f32), jax.ShapeDtypeStruct((8, c), f32)],
        compiler_params=_params(1),
    )(hc, dy, lg, lb)


def _mix_bwd_taps(u, dhc, dy, cw, pw, ps, seq):
    t, c3 = u.shape
    c = c3 // 3
    kw = 31
    tm = min(_CONV_ROWS, seq)
    tps = seq // tm
    ng = len(POOL_WINDOWS)
    gd = c // ng
    nh = 16

    def body(u_ref, uh_ref, dhc_ref, dhcn_ref, dy_ref, dyn_ref, cw_ref, pw_ref, ps_ref,
             du_ref, dcw_ref, dps_ref, dpw_ref, hgbuf, dcbuf, pbuf, dpbuf):
        i = pl.program_id(0)
        keep_prev = jnp.where(i % tps == 0, 0.0, 1.0)
        keep_next = jnp.where(i % tps == tps - 1, 0.0, 1.0)

        @pl.when(i == 0)
        def _():
            dcw_ref[...] = jnp.zeros_like(dcw_ref)
            dps_ref[...] = jnp.zeros_like(dps_ref)
            dpw_ref[...] = jnp.zeros_like(dpw_ref)

        uh = uh_ref[...].astype(f32) * keep_prev
        hgbuf[0:_HALO, :] = uh[:, 0:c] * _sigmoid(uh[:, c:2 * c])
        pbuf[0:_HALO, :] = uh[:, 2 * c:]
        um = u_ref[...].astype(f32)
        hgbuf[_HALO:_HALO + tm, :] = um[:, 0:c] * _sigmoid(um[:, c:2 * c])
        pbuf[_HALO:_HALO + tm, :] = um[:, 2 * c:]
        dcbuf[0:tm, :] = dhc_ref[...]
        dcbuf[tm:tm + _HALO, :] = dhcn_ref[...] * keep_next
        tap_sums = [None] * kw
        for r0 in range(0, tm, _CHUNK):
            dh = dcbuf[r0:r0 + _CHUNK, :]
            acc = jnp.zeros((_CHUNK, c), f32)
            for k in range(kw):
                off = r0 + _HALO - (kw - 1) + k
                part = jnp.sum(dh * hgbuf[off:off + _CHUNK, :], axis=0, keepdims=True)
                tap_sums[k] = part if tap_sums[k] is None else tap_sums[k] + part
                fwd = r0 + (kw - 1) - k
                acc = acc + cw_ref[k:k + 1, :] * dcbuf[fwd:fwd + _CHUNK, :]
            val = u_ref[r0:r0 + _CHUNK, 0:c].astype(f32)
            sg = _sigmoid(u_ref[r0:r0 + _CHUNK, c:2 * c].astype(f32))
            du_ref[r0:r0 + _CHUNK, 0:c] = (acc * sg).astype(du_ref.dtype)
            du_ref[r0:r0 + _CHUNK, c:2 * c] = (acc * val * sg * (1.0 - sg)).astype(du_ref.dtype)
        for k in range(kw):
            dcw_ref[k:k + 1, :] += tap_sums[k]
        base = ((i % tps) * tm).astype(f32)
        pos = base + lax.broadcasted_iota(jnp.int32, (tm, 1), 0).astype(f32)
        pos_next = base + float(tm) + lax.broadcasted_iota(jnp.int32, (nh, 1), 0).astype(f32)
        for gi, w in enumerate(POOL_WINDOWS):
            sl = slice(gi * gd, (gi + 1) * gd)
            v = pbuf[_HALO:_HALO + tm, sl]
            s = v
            for j in range(1, w):
                s = s + pbuf[_HALO - j:_HALO - j + tm, sl]
            cnt = _pool_counts(pos, w)
            pooled = (s / cnt - v).astype(_ACT)
            pwg = pw_ref[gi].astype(_ACT)
            mixed = _dot(pooled, pwg, _NN)
            dyp = dy_ref[:, sl].astype(f32)
            dps_ref[0:1, sl] += jnp.sum(dyp * mixed, axis=0, keepdims=True)
            dmix = (dyp * ps_ref[:, sl]).astype(_ACT)
            dpw_ref[gi] += _dot(pooled, dmix, _TN)
            dmix_next = (dyn_ref[:, sl].astype(f32) * ps_ref[:, sl] * keep_next).astype(_ACT)
            dpool = _dot(dmix, pwg, _NT)
            dpbuf[0:tm, sl] = dpool / cnt
            dpbuf[tm:tm + nh, sl] = _dot(dmix_next, pwg, _NT) / _pool_counts(pos_next, w)
            acc = -dpool
            for j in range(w):
                acc = acc + dpbuf[j:j + tm, sl]
            du_ref[:, 2 * c + gi * gd:2 * c + (gi + 1) * gd] = acc.astype(du_ref.dtype)

    hb = tm // _HALO
    n_halo = t // _HALO
    n_nh = t // nh
    full = lambda shape: pl.BlockSpec(shape, lambda i: (0,) * len(shape))
    return pl.pallas_call(
        body, name="mix_bwd_taps", grid=(t // tm,),
        in_specs=[pl.BlockSpec((tm, c3), lambda i: (i, 0)),
                  pl.BlockSpec((_HALO, c3), lambda i: (jnp.maximum(i * hb - 1, 0), 0)),
                  pl.BlockSpec((tm, c), lambda i: (i, 0)),
                  pl.BlockSpec((_HALO, c), lambda i: (jnp.minimum((i + 1) * hb, n_halo - 1), 0)),
                  pl.BlockSpec((tm, c), lambda i: (i, 1)),
                  pl.BlockSpec((nh, c), lambda i: (jnp.minimum((i + 1) * (tm // nh), n_nh - 1), 1)),
                  full((_HALO, c)), full((ng, gd, gd)), full((1, c))],
        out_specs=[pl.BlockSpec((tm, c3), lambda i: (i, 0)), full((_HALO, c)), full((8, c)), full((ng, gd, gd))],
        out_shape=[jax.ShapeDtypeStruct((t, c3), _ACT), jax.ShapeDtypeStruct((_HALO, c), f32),
                   jax.ShapeDtypeStruct((8, c), f32), jax.ShapeDtypeStruct((ng, gd, gd), f32)],
        scratch_shapes=[pltpu.VMEM((_HALO + tm, c), f32), pltpu.VMEM((tm + _HALO, c), f32),
                        pltpu.VMEM((_HALO + tm, c), f32), pltpu.VMEM((tm + nh, c), f32)],
        compiler_params=_params(1),
    )(u, u, dhc, dhc, dy, dy, cw, pw, ps)


def _attn_fwd(q, kv, n_seq, seq, n_mem):
    t, d = q.shape
    dh = d // XATTN_HEADS
    tq = min(512, seq)
    nq = seq // tq
    scale = dh ** -0.5

    def body(q_ref, k_ref, v_ref, o_ref):
        s = _dot(q_ref[...], k_ref[...], _NT) * scale
        e = jnp.exp(s - jnp.max(s, axis=-1, keepdims=True))
        p = e / jnp.sum(e, axis=-1, keepdims=True)
        o_ref[...] = _dot(p.astype(_ACT), v_ref[...], _NN).astype(o_ref.dtype)

    qs = pl.BlockSpec((tq, dh), lambda b, h, i: (b * nq + i, h))
    return pl.pallas_call(
        body, name="attn_fwd", grid=(n_seq, XATTN_HEADS, nq),
        in_specs=[qs, pl.BlockSpec((n_mem, dh), lambda b, h, i: (b, h)),
                  pl.BlockSpec((n_mem, dh), lambda b, h, i: (b, XATTN_HEADS + h))],
        out_specs=qs, out_shape=jax.ShapeDtypeStruct((t, d), _ACT), compiler_params=_params(3),
    )(q, kv, kv)


def _attn_bwd(q, kv, do, n_seq, seq, n_mem):
    t, d = q.shape
    dh = d // XATTN_HEADS
    tq = min(512, seq)
    nq = seq // tq
    scale = dh ** -0.5

    def body(q_ref, k_ref, v_ref, do_ref, dq_ref, dk_ref, dv_ref, dk_acc, dv_acc):
        i = pl.program_id(2)
        qv = q_ref[...]
        kvv = k_ref[...]
        dov = do_ref[...]
        s = _dot(qv, kvv, _NT) * scale
        e = jnp.exp(s - jnp.max(s, axis=-1, keepdims=True))
        p = e / jnp.sum(e, axis=-1, keepdims=True)
        dp = _dot(dov, v_ref[...], _NT)
        ds = (p * (dp - jnp.sum(dp * p, axis=-1, keepdims=True)) * scale).astype(_ACT)
        dq_ref[...] = _dot(ds, kvv, _NN).astype(dq_ref.dtype)
        dk_part = _dot(ds, qv, _TN)
        dv_part = _dot(p.astype(_ACT), dov, _TN)

        @pl.when(i == 0)
        def _():
            dk_acc[...] = dk_part
            dv_acc[...] = dv_part

        @pl.when(i > 0)
        def _():
            dk_acc[...] += dk_part
            dv_acc[...] += dv_part

        @pl.when(i == nq - 1)
        def _():
            dk_ref[...] = dk_acc[...].astype(dk_ref.dtype)
            dv_ref[...] = dv_acc[...].astype(dv_ref.dtype)

    qs = pl.BlockSpec((tq, dh), lambda b, h, i: (b * nq + i, h))
    ms = pl.BlockSpec((n_mem, dh), lambda b, h, i: (b, h))
    return pl.pallas_call(
        body, name="attn_bwd", grid=(n_seq, XATTN_HEADS, nq),
        in_specs=[qs, ms, pl.BlockSpec((n_mem, dh), lambda b, h, i: (b, XATTN_HEADS + h)), qs],
        out_specs=[qs, ms, ms],
        out_shape=[jax.ShapeDtypeStruct((t, d), _ACT), jax.ShapeDtypeStruct((n_seq * n_mem, d), _ACT),
                   jax.ShapeDtypeStruct((n_seq * n_mem, d), _ACT)],
        scratch_shapes=[pltpu.VMEM((n_mem, dh), f32), pltpu.VMEM((n_mem, dh), f32)],
        compiler_params=_params(3),
    )(q, kv, kv, do)


_FFN_COLS = 256
_FFN_HALO = 16


def _ffn_gate_fwd(up, fw, fb, seq):
    _, t, f = up.shape
    tm = min(_CONV_ROWS, seq)
    tps = seq // tm
    tc = _FFN_COLS
    nc = f // tc
    hl = _FFN_HALO

    def body(up_ref, uph_ref, wg_ref, wv_ref, bg_ref, bv_ref, a_ref, buf):
        i = pl.program_id(1)
        keep = jnp.where(i % tps == 0, 0.0, 1.0)
        buf[:, 0:hl, :] = uph_ref[...].astype(f32) * keep
        buf[:, hl:hl + tm, :] = up_ref[...].astype(f32)
        for r0 in range(0, tm, _CHUNK):
            conv = []
            for g, (w_ref, b_ref) in enumerate(((wg_ref, bg_ref), (wv_ref, bv_ref))):
                acc = jnp.broadcast_to(b_ref[...], (_CHUNK, tc))
                for k in range(3):
                    off = r0 + hl - 2 + k
                    acc = acc + w_ref[k:k + 1, :] * buf[g, off:off + _CHUNK, :]
                conv.append(acc)
            gate, val = conv
            a_ref[r0:r0 + _CHUNK, :] = (gate * _sigmoid(gate) * val).astype(a_ref.dtype)

    hb = tm // hl
    return pl.pallas_call(
        body, name="ffn_gate_fwd", grid=(nc, t // tm),
        in_specs=[pl.BlockSpec((2, tm, tc), lambda j, i: (0, i, j)),
                  pl.BlockSpec((2, hl, tc), lambda j, i: (0, jnp.maximum(i * hb - 1, 0), j)),
                  pl.BlockSpec((8, tc), lambda j, i: (0, j)), pl.BlockSpec((8, tc), lambda j, i: (0, nc + j)),
                  pl.BlockSpec((1, tc), lambda j, i: (0, j)), pl.BlockSpec((1, tc), lambda j, i: (0, nc + j))],
        out_specs=pl.BlockSpec((tm, tc), lambda j, i: (i, j)),
        out_shape=jax.ShapeDtypeStruct((t, f), _ACT),
        scratch_shapes=[pltpu.VMEM((2, hl + tm, tc), f32)], compiler_params=_params(2),
    )(up, up, fw, fw, fb, fb)


def _ffn_gate_bwd(up, da, fw, fb, seq):
    _, t, f = up.shape
    tm = min(_CONV_ROWS, seq)
    tps = seq // tm
    tc = _FFN_COLS
    nc = f // tc
    hl = _FFN_HALO

    def body(up_ref, uph_ref, upn_ref, da_ref, dan_ref, wg_ref, wv_ref, bg_ref, bv_ref,
             dup_ref, sg_ref, sv_ref, ubuf, dbuf):
        i = pl.program_id(1)
        keep_prev = jnp.where(i % tps == 0, 0.0, 1.0)
        keep_next = jnp.where(i % tps == tps - 1, 0.0, 1.0)

        @pl.when(i == 0)
        def _():
            sg_ref[...] = jnp.zeros_like(sg_ref)
            sv_ref[...] = jnp.zeros_like(sv_ref)

        ubuf[:, 0:hl, :] = uph_ref[...].astype(f32) * keep_prev
        ubuf[:, hl:hl + tm, :] = up_ref[...].astype(f32)
        ubuf[:, hl + tm:hl + tm + hl, :] = upn_ref[...].astype(f32) * keep_next
        w_refs = (wg_ref, wv_ref)
        b_refs = (bg_ref, bv_ref)

        def conv_rows(r0, rows):
            out = []
            for g in range(2):
                acc = jnp.broadcast_to(b_refs[g][...], (rows, tc))
                for k in range(3):
                    off = r0 + hl - 2 + k
                    acc = acc + w_refs[g][k:k + 1, :] * ubuf[g, off:off + rows, :]
                out.append(acc)
            return out

        def grads(r0, rows, dav):
            gate, val = conv_rows(r0, rows)
            sg = _sigmoid(gate)
            return dav * val * (sg * (1.0 + gate * (1.0 - sg))), dav * (gate * sg)

        for r0 in range(0, tm, _CHUNK):
            dg, dv = grads(r0, _CHUNK, da_ref[r0:r0 + _CHUNK, :].astype(f32))
            dbuf[0, r0:r0 + _CHUNK, :] = dg
            dbuf[1, r0:r0 + _CHUNK, :] = dv
        dg, dv = grads(tm, hl, dan_ref[...].astype(f32) * keep_next)
        dbuf[0, tm:tm + hl, :] = dg
        dbuf[1, tm:tm + hl, :] = dv
        for g, s_ref in enumerate((sg_ref, sv_ref)):
            sums = [None] * 4
            for r0 in range(0, tm, _CHUNK):
                d = dbuf[g, r0:r0 + _CHUNK, :]
                parts = [jnp.sum(d, axis=0, keepdims=True)]
                acc = jnp.zeros((_CHUNK, tc), f32)
                for k in range(3):
                    off = r0 + hl - 2 + k
                    parts.append(jnp.sum(d * ubuf[g, off:off + _CHUNK, :], axis=0, keepdims=True))
                    fwd = r0 + 2 - k
                    acc = acc + w_refs[g][k:k + 1, :] * dbuf[g, fwd:fwd + _CHUNK, :]
                dup_ref[g, r0:r0 + _CHUNK, :] = acc.astype(dup_ref.dtype)
                sums = [p if s is None else s + p for s, p in zip(sums, parts)]
            for r in range(4):
                s_ref[r:r + 1, :] += sums[r]

    hb = tm // hl
    n_halo = t // hl
    return pl.pallas_call(
        body, name="ffn_gate_bwd", grid=(nc, t // tm),
        in_specs=[pl.BlockSpec((2, tm, tc), lambda j, i: (0, i, j)),
                  pl.BlockSpec((2, hl, tc), lambda j, i: (0, jnp.maximum(i * hb - 1, 0), j)),
                  pl.BlockSpec((2, hl, tc), lambda j, i: (0, jnp.minimum((i + 1) * hb, n_halo - 1), j)),
                  pl.BlockSpec((tm, tc), lambda j, i: (i, j)),
                  pl.BlockSpec((hl, tc), lambda j, i: (jnp.minimum((i + 1) * hb, n_halo - 1), j)),
                  pl.BlockSpec((8, tc), lambda j, i: (0, j)), pl.BlockSpec((8, tc), lambda j, i: (0, nc + j)),
                  pl.BlockSpec((1, tc), lambda j, i: (0, j)), pl.BlockSpec((1, tc), lambda j, i: (0, nc + j))],
        out_specs=[pl.BlockSpec((2, tm, tc), lambda j, i: (0, i, j)),
                   pl.BlockSpec((8, tc), lambda j, i: (0, j)), pl.BlockSpec((8, tc), lambda j, i: (0, j))],
        out_shape=[jax.ShapeDtypeStruct((2, t, f), _ACT), jax.ShapeDtypeStruct((8, f), f32), jax.ShapeDtypeStruct((8, f), f32)],
        scratch_shapes=[pltpu.VMEM((2, hl + tm + hl, tc), f32), pltpu.VMEM((2, tm + hl, tc), f32)],
        compiler_params=_params(2),
    )(up, up, up, da, da, fw, fw, fb, fb)


def _adamw_math(w, g, m, v):
    m = ADAM_B1 * m + (1.0 - ADAM_B1) * g
    v = ADAM_B2 * v + (1.0 - ADAM_B2) * (g * g)
    m_hat = m / (1.0 - ADAM_B1 ** ADAM_STEP)
    v_hat = v / (1.0 - ADAM_B2 ** ADAM_STEP)
    delta = -ADAM_LR * (m_hat / (jnp.sqrt(v_hat) + ADAM_EPS) + ADAM_WD * w)
    return delta, m, v


def _adamw_shard(name, w, g, m, v):
    _, r, c = w.shape
    tr = next((cand for cand in (256, 176, 128, 64, 32, 16, 8) if r % cand == 0), r)

    def body(w_ref, g_ref, m_ref, v_ref, d_ref, mo_ref, vo_ref):
        d, mn, vn = _adamw_math(w_ref[...], g_ref[...], m_ref[...], v_ref[...])
        d_ref[...] = d
        mo_ref[...] = mn
        vo_ref[...] = vn

    s3 = pl.BlockSpec((None, tr, c), lambda i: (0, i, 0))
    s2 = pl.BlockSpec((tr, c), lambda i: (i, 0))
    shp = jax.ShapeDtypeStruct(w.shape, f32)
    return pl.pallas_call(
        body, name=name, grid=(r // tr,), in_specs=[s3, s2, s3, s3], out_specs=[s3, s3, s3], out_shape=[shp, shp, shp],
        compiler_params=_params(1),
    )(w, g, m, v)


def _adamw_small(quads):
    n = len(quads)

    def body(*refs):
        ins, outs = refs[:4 * n], refs[4 * n:]
        for p in range(n):
            w_ref, g_ref, m_ref, v_ref = ins[4 * p:4 * p + 4]
            d, mn, vn = _adamw_math(w_ref[...], g_ref[...], m_ref[...], v_ref[...])
            outs[3 * p][...] = d
            outs[3 * p + 1][...] = mn
            outs[3 * p + 2][...] = vn

    flat = [a for q in quads for a in q]
    shapes = [jax.ShapeDtypeStruct(q[0].shape, f32) for q in quads for _ in range(3)]
    outs = pl.pallas_call(
        body, name="adamw_small", in_specs=[_VMEM] * (4 * n), out_specs=[_VMEM] * (3 * n), out_shape=shapes,
        compiler_params=pltpu.CompilerParams(vmem_limit_bytes=_VMEM_LIMIT_BYTES),
    )(*flat)
    return [tuple(outs[3 * p:3 * p + 3]) for p in range(n)]


def _sum_pairs(name, place, grads, got):
    _, r, c = grads.shape

    def body(place_ref, a_ref, b_ref, o_ref):
        o_ref[...] = (a_ref[...].astype(f32) + b_ref[...].astype(f32)).astype(o_ref.dtype)

    grid_spec = pltpu.PrefetchScalarGridSpec(
        num_scalar_prefetch=1, grid=(4,),
        in_specs=[pl.BlockSpec((None, r, c), lambda i, p: (2 * i + p[1], 0, 0)), pl.BlockSpec((None, r, c), lambda i, p: (i, 0, 0))],
        out_specs=pl.BlockSpec((None, r, c), lambda i, p: (i, 0, 0)))
    return pl.pallas_call(body, name=name, grid_spec=grid_spec, out_shape=jax.ShapeDtypeStruct((4, r, c), _ACT),
                          compiler_params=_params(1))(place, grads, got)


def _sum_four(name, place, sums, got):
    _, r, c = sums.shape

    def body(place_ref, o_ref, g_ref, f_ref):
        s = o_ref[...].astype(f32) + g_ref[0].astype(f32)
        s = s + g_ref[1].astype(f32)
        f_ref[...] = s + g_ref[2].astype(f32)

    grid_spec = pltpu.PrefetchScalarGridSpec(
        num_scalar_prefetch=1, grid=(1,),
        in_specs=[pl.BlockSpec((None, r, c), lambda i, p: (p[0], 0, 0)), pl.BlockSpec((3, r, c), lambda i, p: (0, 0, 0))],
        out_specs=pl.BlockSpec((None, r, c), lambda i, p: (p[1], 0, 0)))
    return pl.pallas_call(body, name=name, grid_spec=grid_spec, out_shape=jax.ShapeDtypeStruct((2, r, c), f32),
                          compiler_params=_params(1))(place, sums, got)


def _place():
    return lax.axis_index("x"), lax.axis_index("y"), lax.axis_index("c")


def _other_chips(x, y):
    return [(1 - x, y), (x, 1 - y), (1 - x, 1 - y)]


def _remote(src, dst, send_sem, recv_sem, to):
    return pltpu.make_async_remote_copy(src_ref=src, dst_ref=dst, send_sem=send_sem, recv_sem=recv_sem,
                                        device_id=to, device_id_type=_MESH)


def _place_shards(place, shards, col_sharded):
    n = len(shards)
    steps = 4

    def body(place_ref, *refs):
        for src, dst in zip(refs[:n], refs[n:]):
            dst[...] = src[...].astype(dst.dtype)

    in_specs, out_specs, out_shape = [], [], []
    for w, col in zip(shards, col_sharded):
        r, cs = w.shape
        tr = r // steps
        in_specs.append(pl.BlockSpec((tr, cs), lambda i, p: (i, 0)))
        if col:
            out_specs.append(pl.BlockSpec((tr, cs), lambda i, p: (i, p[0])))
            out_shape.append(jax.ShapeDtypeStruct((r, 4 * cs), _ACT))
        else:
            out_specs.append(pl.BlockSpec((tr, cs), lambda i, p: (p[0] * steps + i, 0)))
            out_shape.append(jax.ShapeDtypeStruct((4 * r, cs), _ACT))
    grid_spec = pltpu.PrefetchScalarGridSpec(num_scalar_prefetch=1, grid=(steps,), in_specs=in_specs, out_specs=out_specs)
    return pl.pallas_call(body, name="place_shards", grid_spec=grid_spec, out_shape=out_shape,
                          compiler_params=_params(1))(place, *shards)


def _shard_of(ref, col_sharded, s):
    rows, cols = ref.shape
    if col_sharded:
        return ref.at[:, pl.ds(s * (cols // 4), cols // 4)]
    return ref.at[pl.ds(s * (rows // 4), rows // 4), :]


def _allgather_start(bufs, col_sharded, groups):
    n = len(bufs)
    ng = len(groups)

    def body(*refs):
        out = refs[n:2 * n]
        sems = refs[2 * n:]
        x, y, c = _place()
        for g, members in enumerate(groups):
            for i, w in enumerate(members):
                mine = _shard_of(out[w], col_sharded[w], 2 * x + y)
                for j, chip in enumerate(_other_chips(x, y)):
                    _remote(mine, mine, sems[2 * g].at[3 * i + j], sems[2 * g + 1].at[3 * i + j], (*chip, c)).start()

    sem_shapes = [pltpu.SemaphoreType.DMA((3 * len(m),)) for m in groups for _ in range(2)]
    outs = pl.pallas_call(
        body, name="allgather_start", in_specs=[_HBM] * n, out_specs=[_HBM] * n + [_SEM] * (2 * ng),
        out_shape=[pltpu.HBM(b.shape, b.dtype) for b in bufs] + sem_shapes,
        input_output_aliases={i: i for i in range(n)},
        compiler_params=pltpu.CompilerParams(has_side_effects=_EFFECT),
    )(*[pltpu.with_memory_space_constraint(b, pltpu.HBM) for b in bufs])
    return list(outs[:n]), [(outs[n + 2 * g], outs[n + 2 * g + 1]) for g in range(ng)]


def _allgather_wait(name, bufs, col_sharded, sems, after):
    n = len(bufs)

    def body(*refs):
        buf = refs[:n]
        send, recv = refs[n], refs[n + 1]
        x, y, c = _place()
        for i in range(n):
            mine = _shard_of(buf[i], col_sharded[i], 2 * x + y)
            for j, chip in enumerate(_other_chips(x, y)):
                landed = _shard_of(buf[i], col_sharded[i], 2 * chip[0] + chip[1])
                cp = _remote(mine, landed, send.at[3 * i + j], recv.at[3 * i + j], (*chip, c))
                cp.wait_send()
                cp.wait_recv()

    return pl.pallas_call(
        body, name=name, in_specs=[_HBM] * n + [_SEM, _SEM, _ANY], out_specs=[_HBM] * n,
        out_shape=[pltpu.HBM(b.shape, b.dtype) for b in bufs],
        input_output_aliases={i: i for i in range(n)},
        compiler_params=pltpu.CompilerParams(has_side_effects=_EFFECT),
    )(*bufs, *sems, after)


def _exchange_pair_halves(grads):
    nw = len(grads)

    def body(*refs):
        src = refs[:nw]
        got = refs[nw:2 * nw]
        send_sem, recv_sem = refs[2 * nw:]
        x, y, c = _place()
        sends = []
        for w in range(nw):
            for s in range(4):
                rc = _remote(src[w].at[2 * s + 1 - c], got[w].at[s], send_sem.at[4 * w + s], recv_sem.at[4 * w + s], (x, y, 1 - c))
                rc.start()
                sends.append(rc)
        for rc in sends:
            rc.wait_recv()
        for rc in sends:
            rc.wait_send()

    return pl.pallas_call(
        body, name="rs_pair_exchange", in_specs=[_ANY] * nw, out_specs=[_ANY] * nw,
        out_shape=[jax.ShapeDtypeStruct((4,) + g.shape[1:], g.dtype) for g in grads],
        scratch_shapes=[pltpu.SemaphoreType.DMA((4 * nw,)), pltpu.SemaphoreType.DMA((4 * nw,))],
    )(*grads)


def _exchange_chip_pieces(sums):
    nw = len(sums)

    def body(*refs):
        src = refs[:nw]
        got = refs[nw:2 * nw]
        send_sem, recv_sem = refs[2 * nw:]
        x, y, c = _place()
        sends = []
        for w in range(nw):
            for j, chip in enumerate(_other_chips(x, y)):
                s = 2 * chip[0] + chip[1]
                rc = _remote(src[w].at[s], got[w].at[j], send_sem.at[3 * w + j], recv_sem.at[3 * w + j], (*chip, c))
                rc.start()
                sends.append(rc)
        for rc in sends:
            rc.wait_recv()
        for rc in sends:
            rc.wait_send()

    return pl.pallas_call(
        body, name="rs_chip_exchange", in_specs=[_ANY] * nw, out_specs=[_ANY] * nw,
        out_shape=[jax.ShapeDtypeStruct((3,) + g.shape[1:], g.dtype) for g in sums],
        scratch_shapes=[pltpu.SemaphoreType.DMA((3 * nw,)), pltpu.SemaphoreType.DMA((3 * nw,))],
    )(*sums)


def _swap_halves(finals):
    nw = len(finals)

    def body(*refs):
        buf = refs[nw:2 * nw]
        send_sem, recv_sem = refs[2 * nw:]
        x, y, c = _place()
        sends = []
        for w in range(nw):
            rc = _remote(buf[w].at[c], buf[w].at[c], send_sem.at[w], recv_sem.at[w], (x, y, 1 - c))
            rc.start()
            sends.append(rc)
        for w in range(nw):
            _remote(buf[w].at[1 - c], buf[w].at[1 - c], send_sem.at[w], recv_sem.at[w], (x, y, c)).wait_recv()
        for rc in sends:
            rc.wait_send()

    return pl.pallas_call(
        body, name="rs_swap_halves", in_specs=[_ANY] * nw, out_specs=[_ANY] * nw,
        out_shape=[jax.ShapeDtypeStruct(g.shape, g.dtype) for g in finals],
        input_output_aliases={i: i for i in range(nw)},
        scratch_shapes=[pltpu.SemaphoreType.DMA((nw,)), pltpu.SemaphoreType.DMA((nw,))],
    )(*finals)


def _allreduce_small(parts):
    n = len(parts)

    def body(*refs):
        src = refs[:n]
        out = refs[n:2 * n]
        slots = refs[2 * n:3 * n]
        send_sem, recv_sem = refs[3 * n:]
        x, y, c = _place()
        me = 4 * x + 2 * y + c
        flips = [(bx, by, bc) for bx in (0, 1) for by in (0, 1) for bc in (0, 1)][1:]
        sends = []
        for a in range(n):
            slots[a][me] = src[a][...]
        for k, (bx, by, bc) in enumerate(flips):
            to = (1 - x if bx else x, 1 - y if by else y, 1 - c if bc else c)
            for a in range(n):
                rc = _remote(src[a], slots[a].at[me], send_sem.at[k, a], recv_sem.at[k, a], to)
                rc.start()
                sends.append(rc)
        for rc in sends:
            rc.wait_recv()
        for a in range(n):
            s = slots[a][0]
            for d in range(1, 8):
                s = s + slots[a][d]
            out[a][...] = s
        for rc in sends:
            rc.wait_send()

    return pl.pallas_call(
        body, name="allreduce_small", in_specs=[_VMEM] * n, out_specs=[_VMEM] * n,
        out_shape=[jax.ShapeDtypeStruct(p.shape, f32) for p in parts],
        scratch_shapes=[pltpu.VMEM((8,) + p.shape, f32) for p in parts] + [pltpu.SemaphoreType.DMA((7, n)), pltpu.SemaphoreType.DMA((7, n))],
        compiler_params=pltpu.CompilerParams(vmem_limit_bytes=_VMEM_LIMIT_BYTES),
    )(*parts)


def _local_step(x, mem, tgt, g_mix, g_xattn, g_mem, g_ffn, g_final, cb, lg, lb, pw, ps, fb, weights, n_seq, seq, n_mem):
    t, d = x.shape
    f = fb.shape[1] // 2
    c = cb.shape[1]
    h1 = _rms_fwd("norm_mix", x, g_mix)
    w_in, cw, fw = weights(0, h1)
    u = _mm_nn("proj_in", h1, w_in, _ACT, w_in.shape[1])
    y, hc = _mix_fwd(u, cw, cb, lg, lb, pw, ps, seq)
    w_out, w_q, w_kv, w_o = weights(1, y)
    x1 = _mm_nn("proj_out", y, w_out, f32, d, res=x)
    h2 = _rms_fwd("norm_xattn", x1, g_xattn)
    q = _mm_nn("proj_q", h2, w_q, _ACT, d)
    mem_n = _rms_fwd("norm_mem", mem, g_mem)
    kv = _mm_nn("proj_kv", mem_n, w_kv, _ACT, 2 * d)
    o = _attn_fwd(q, kv, n_seq, seq, n_mem)
    x2 = _mm_nn("proj_o", o, w_o, f32, d, res=x1)
    h3 = _rms_fwd("norm_ffn", x2, g_ffn)
    w_up, w_down = weights(2, h3)
    up = _mm_nn("proj_up", h3, w_up, _ACT, f, split_out=True)
    a = _ffn_gate_fwd(up, fw, fb, seq)
    x3 = _mm_nn("proj_down", a, w_down, f32, d, res=x2)
    dx3, dx3b, dg_final, loss = _final_loss_bwd(x3, g_final, tgt)
    da = _mm_nt("d_act", dx3b, w_down, _ACT)
    gw_down = _mm_tn_rows("dw_down", a, dx3b, min(512, t), 512)
    dup, sums_g, sums_v = _ffn_gate_bwd(up, da, fw, fb, seq)
    dh3 = _mm_nt("d_h3", dup, w_up, f32, nk=2)
    gw_up = _mm_tn_pieces("dw_up", h3, dup, f // 2, min(512, t))
    dx2, dx2b, dg_ffn = _rms_bwd("norm_ffn_bwd", x2, g_ffn, dh3, dx3)
    do = _mm_nt("d_o", dx2b, w_o, _ACT)
    gw_o = _mm_tn_rows("dw_o", o, dx2b, min(1024, t), d)
    dq, dk, dv = _attn_bwd(q, kv, do, n_seq, seq, n_mem)
    dkv = jnp.concatenate([dk, dv], axis=1)
    dh2 = _mm_nt("d_h2", dq, w_q, f32)
    gw_q = _mm_tn_rows("dw_q", h2, dq, min(1024, t), d)
    gw_kv = _mm_tn_pieces("dw_kv", mem_n, dkv, d // 2, mem.shape[0])
    dmem_n = _mm_nt("d_mem_n", dkv, w_kv, f32)
    dg_mem = _rms_gain_grad("norm_mem_bwd", mem, dmem_n)
    dx1, dx1b, dg_xattn = _rms_bwd("norm_xattn_bwd", x1, g_xattn, dh2, dx2)
    dy = _mm_nt("d_y", dx1b, w_out, _ACT)
    gw_out = _mm_tn_rows("dw_out", y, dx1b, min(1024, t), d)
    dhc, sums_norm = _mix_bwd_norm(hc, dy, lg, lb, seq)
    du, d_cw, d_ps, d_pw = _mix_bwd_taps(u, dhc, dy, cw, pw, ps, seq)
    dh1 = _mm_nt("d_h1", du, w_in, f32)
    gw_in = _mm_tn_pieces("dw_in", h1, du, c * 3 // 4, min(2048, t))
    grad_x, _, dg_mix = _rms_bwd("norm_mix_bwd", x, g_mix, dh1, dx1)
    zero_row = jnp.zeros((1, d), f32)
    gains = jnp.concatenate([dg_mix, dg_xattn, dg_mem, dg_ffn, dg_final, jnp.pad(loss, ((0, 0), (0, d - 1))), zero_row, zero_row], axis=0)
    conv_rows = jnp.concatenate([sums_norm[2:3], sums_norm[0:1], sums_norm[1:2], d_ps[0:1], jnp.zeros((4, c), f32)], axis=0)
    ffn_rows = jnp.concatenate([sums_g, sums_v], axis=1)
    big = [gw_in, gw_kv, gw_up,
           gw_out.reshape(8, -1, d), gw_q.reshape(8, -1, d), gw_o.reshape(8, -1, d), gw_down.reshape(8, -1, d)]
    small = [gains, conv_rows, d_pw.reshape(-1, d_pw.shape[-1]), ffn_rows, d_cw]
    return grad_x, big, small


def kernel(x, mem, norm_mix_g, w_in, conv_dw_w, conv_dw_b, conv_ln_g, conv_ln_b, pool_w, pool_scale, w_out, norm_xattn_g, norm_mem_g, w_q, w_kv, w_o, norm_ffn_g, w_up, ffn_dw_w, ffn_dw_b, w_down, norm_final_g, loss_target, m_norm_mix_g, m_w_in, m_conv_dw_w, m_conv_dw_b, m_conv_ln_g, m_conv_ln_b, m_pool_w, m_pool_scale, m_w_out, m_norm_xattn_g, m_norm_mem_g, m_w_q, m_w_kv, m_w_o, m_norm_ffn_g, m_w_up, m_ffn_dw_w, m_ffn_dw_b, m_w_down, m_norm_final_g, v_norm_mix_g, v_w_in, v_conv_dw_w, v_conv_dw_b, v_conv_ln_g, v_conv_ln_b, v_pool_w, v_pool_scale, v_w_out, v_norm_xattn_g, v_norm_mem_g, v_w_q, v_w_kv, v_w_o, v_norm_ffn_g, v_w_up, v_ffn_dw_w, v_ffn_dw_b, v_w_down, v_norm_final_g):
    n_seq, seq, d = x.shape
    n_mem = mem.shape[1]
    chip = 2 * lax.axis_index("x") + lax.axis_index("y")

    place = jnp.stack([chip, lax.axis_index("c")]).astype(jnp.int32)

    col_w = [w_in, w_kv, w_up]
    row_w = [w_out, w_q, w_o, w_down]
    col_flags = [True] * 3 + [False] * 4 + [True] * 2
    kw = conv_dw_w.shape[1]

    def padded_in_place(shard, rows):
        full = jnp.zeros((rows, 4 * shard.shape[1]), shard.dtype)
        return lax.dynamic_update_slice(full, shard, (0, chip * shard.shape[1]))

    bufs = list(_place_shards(place, [w[0] for w in col_w + row_w], col_flags[:7]))
    bufs += [padded_in_place(conv_dw_w[0], _HALO), padded_in_place(ffn_dw_w[0], 8)]
    groups = [[0, 7, 8], [3, 4, 1, 5], [2, 6]]
    bufs, sems = _allgather_start(bufs, col_flags, groups)

    def weights(g, after):
        members = groups[g]
        return _allgather_wait("allgather_wait_%d" % g, [bufs[i] for i in members], [col_flags[i] for i in members], sems[g], after)

    grad_x, big, small = _local_step(
        x.reshape(n_seq * seq, d), mem.reshape(n_seq * n_mem, d), loss_target.reshape(n_seq * seq, d),
        norm_mix_g, norm_xattn_g, norm_mem_g, norm_ffn_g, norm_final_g.reshape(1, d),
        conv_dw_b, conv_ln_g, conv_ln_b, pool_w[0], pool_scale, ffn_dw_b, weights, n_seq, seq, n_mem)

    names = ["w_in", "w_kv", "w_up", "w_out", "w_q", "w_o", "w_down"]
    got = _exchange_pair_halves(big)
    chip_sums = [_sum_pairs("rs_pair_sum_" + n, place, a, b) for n, a, b in zip(names, big, got)]
    others = _exchange_chip_pieces(chip_sums)
    finals = [_sum_four("rs_chip_sum_" + n, place, a, b) for n, a, b in zip(names, chip_sums, others)]
    shard_grads = _swap_halves(finals)

    gains, conv_rows, d_pw, ffn_rows, d_cw = _allreduce_small(small)
    loss = gains[5, 0]

    outs = {}
    big_w = dict(zip(names, col_w + row_w))
    big_m = dict(w_in=m_w_in, w_kv=m_w_kv, w_up=m_w_up, w_out=m_w_out, w_q=m_w_q, w_o=m_w_o, w_down=m_w_down)
    big_v = dict(w_in=v_w_in, w_kv=v_w_kv, w_up=v_w_up, w_out=v_w_out, w_q=v_w_q, w_o=v_w_o, w_down=v_w_down)
    for n, g in zip(names, shard_grads):
        w = big_w[n]
        g2 = g.reshape(w.shape[1], w.shape[2])
        delta, new_m, new_v = _adamw_shard("adamw_" + n, w, g2, big_m[n], big_v[n])
        outs[n] = (g2.reshape(w.shape), delta, new_m, new_v)

    f2 = ffn_dw_b.shape[1]
    cs_c = conv_dw_w.shape[2]
    cs_f = ffn_dw_w.shape[2]
    g_cw = lax.dynamic_slice(d_cw, (0, chip * cs_c), (kw, cs_c)).reshape(conv_dw_w.shape)
    g_fw = lax.dynamic_slice(ffn_rows, (1, chip * cs_f), (ffn_dw_w.shape[1], cs_f)).reshape(ffn_dw_w.shape)
    small_params = [
        ("norm_mix_g", norm_mix_g, gains[0:1], m_norm_mix_g, v_norm_mix_g),
        ("conv_dw_w", conv_dw_w, g_cw, m_conv_dw_w, v_conv_dw_w),
        ("conv_dw_b", conv_dw_b, conv_rows[0:1], m_conv_dw_b, v_conv_dw_b),
        ("conv_ln_g", conv_ln_g, conv_rows[1:2], m_conv_ln_g, v_conv_ln_g),
        ("conv_ln_b", conv_ln_b, conv_rows[2:3], m_conv_ln_b, v_conv_ln_b),
        ("pool_w", pool_w, d_pw.reshape(pool_w.shape), m_pool_w, v_pool_w),
        ("pool_scale", pool_scale, conv_rows[3:4], m_pool_scale, v_pool_scale),
        ("norm_xattn_g", norm_xattn_g, gains[1:2], m_norm_xattn_g, v_norm_xattn_g),
        ("norm_mem_g", norm_mem_g, gains[2:3], m_norm_mem_g, v_norm_mem_g),
        ("norm_ffn_g", norm_ffn_g, gains[3:4], m_norm_ffn_g, v_norm_ffn_g),
        ("ffn_dw_w", ffn_dw_w, g_fw, m_ffn_dw_w, v_ffn_dw_w),
        ("ffn_dw_b", ffn_dw_b, ffn_rows[0:1, :f2], m_ffn_dw_b, v_ffn_dw_b),
        ("norm_final_g", norm_final_g.reshape(1, d), gains[4:5], m_norm_final_g.reshape(1, d), v_norm_final_g.reshape(1, d)),
    ]
    quads = []
    for _, w, g, m, v in small_params:
        shape2 = (-1, w.shape[-1])
        quads.append((w.reshape(shape2), g.reshape(shape2), m.reshape(shape2), v.reshape(shape2)))
    for (n, w, g, _, _), (delta, new_m, new_v) in zip(small_params, _adamw_small(quads)):
        shape = norm_final_g.shape if n == "norm_final_g" else w.shape
        outs[n] = (g.reshape(shape), delta.reshape(shape), new_m.reshape(shape), new_v.reshape(shape))

    order = ["norm_mix_g", "w_in", "conv_dw_w", "conv_dw_b", "conv_ln_g", "conv_ln_b", "pool_w", "pool_scale", "w_out",
             "norm_xattn_g", "norm_mem_g", "w_q", "w_kv", "w_o", "norm_ffn_g", "w_up", "ffn_dw_w", "ffn_dw_b", "w_down",
             "norm_final_g"]
    return (loss, grad_x.reshape(x.shape), *[outs[n][0] for n in order], *[outs[n][1] for n in order],
            *[outs[n][2] for n in order], *[outs[n][3] for n in order])
```

```python
import functools

import jax
import jax.numpy as jnp
from jax import lax
from jax.experimental import pallas as pl
from jax.experimental.pallas import tpu as pltpu

f32 = jnp.float32
_ACT = jnp.bfloat16

EPS = 1e-6
POOL_WINDOWS = (2, 4, 8, 16)
XATTN_HEADS = 4
ADAM_LR = 0.001
ADAM_B1 = 0.9
ADAM_B2 = 0.999
ADAM_EPS = 1e-08
ADAM_WD = 0.01
ADAM_STEP = 10

_VMEM_LIMIT_BYTES = 56 * 1024 * 1024
_MESH = pl.DeviceIdType.MESH
_ANY = pl.BlockSpec(memory_space=pl.ANY)
_VMEM = pl.BlockSpec(memory_space=pltpu.VMEM)
_HBM = pl.BlockSpec(memory_space=pltpu.HBM)
_SEM = pl.BlockSpec(memory_space=pltpu.SEMAPHORE)
_EFFECT = pltpu.SideEffectType.DATAFLOW_SIDE_EFFECTING

_NN = (((1,), (0,)), ((), ()))
_NT = (((1,), (1,)), ((), ()))
_TN = (((0,), (0,)), ((), ()))


def _params(n_grid):
    return pltpu.CompilerParams(dimension_semantics=("arbitrary",) * n_grid, vmem_limit_bytes=_VMEM_LIMIT_BYTES)


def _sigmoid(v):
    return 1.0 / (1.0 + jnp.exp(-v))


def _dot(a, b, dims):
    return lax.dot_general(a, b, dims, preferred_element_type=f32)


def _mm(name, a, b, *, dims, grid, a_spec, b_spec, o_spec, out_shape, nk, acc_shape=None, res=None, res_spec=None):
    def body(*refs):
        if res is None:
            a_ref, b_ref, o_ref, *scratch = refs
            r_ref = None
        else:
            a_ref, b_ref, r_ref, o_ref, *scratch = refs
        p = _dot(a_ref[...], b_ref[...], dims)

        def finish(v):
            if r_ref is not None:
                v = v + r_ref[...]
            o_ref[...] = v.astype(o_ref.dtype)

        if nk == 1:
            finish(p)
        else:
            acc = scratch[0]
            k = pl.program_id(2)

            @pl.when(k == 0)
            def _():
                acc[...] = p

            @pl.when(k > 0)
            def _():
                acc[...] += p

            @pl.when(k == nk - 1)
            def _():
                finish(acc[...])

    ins = [a, b] + ([] if res is None else [res])
    specs = [a_spec, b_spec] + ([] if res is None else [res_spec])
    return pl.pallas_call(
        body, name=name, grid=grid, in_specs=specs, out_specs=o_spec, out_shape=out_shape,
        scratch_shapes=[pltpu.VMEM(acc_shape, f32)] if nk > 1 else [], compiler_params=_params(3),
    )(*ins)


def _row_tile(m):
    return min(512, m)


def _mm_nn(name, a, b, out_dtype, tn, res=None, split_out=False):
    m, k = a.shape
    n = b.shape[1]
    tm = _row_tile(m)
    if split_out:
        out_shape = jax.ShapeDtypeStruct((n // tn, m, tn), out_dtype)
        o_spec = pl.BlockSpec((None, tm, tn), lambda j, i, kk: (j, i, 0))
    else:
        out_shape = jax.ShapeDtypeStruct((m, n), out_dtype)
        o_spec = pl.BlockSpec((tm, tn), lambda j, i, kk: (i, j))
    return _mm(
        name, a, b, dims=_NN, grid=(n // tn, m // tm, 1), nk=1,
        a_spec=pl.BlockSpec((tm, k), lambda j, i, kk: (i, 0)),
        b_spec=pl.BlockSpec((k, tn), lambda j, i, kk: (0, j)),
        o_spec=o_spec, out_shape=out_shape, res=res,
        res_spec=pl.BlockSpec((tm, tn), lambda j, i, kk: (i, j)),
    )


def _mm_nt(name, a, b, out_dtype, nk=1):
    n, kc = b.shape
    tk = kc // nk
    if a.ndim == 3:
        m = a.shape[1]
        tm = _row_tile(m)
        a_spec = pl.BlockSpec((None, tm, tk), lambda i, j, k: (k, i, 0))
    else:
        m = a.shape[0]
        tm = _row_tile(m)
        a_spec = pl.BlockSpec((tm, tk), lambda i, j, k: (i, k))
    return _mm(
        name, a, b, dims=_NT, grid=(m // tm, 1, nk), nk=nk, acc_shape=(tm, n),
        a_spec=a_spec, b_spec=pl.BlockSpec((n, tk), lambda i, j, k: (0, k)),
        o_spec=pl.BlockSpec((tm, n), lambda i, j, k: (i, 0)),
        out_shape=jax.ShapeDtypeStruct((m, n), out_dtype),
    )


def _mm_tn_rows(name, a, b, tt, tn):
    m, ka = a.shape
    nb = b.shape[1]
    nk = m // tt
    return _mm(
        name, a, b, dims=_TN, grid=(1, nb // tn, nk), nk=nk, acc_shape=(ka, tn),
        a_spec=pl.BlockSpec((tt, ka), lambda i, j, k: (k, 0)),
        b_spec=pl.BlockSpec((tt, tn), lambda i, j, k: (k, j)),
        o_spec=pl.BlockSpec((ka, tn), lambda i, j, k: (0, j)),
        out_shape=jax.ShapeDtypeStruct((ka, nb), _ACT),
    )


def _mm_tn_pieces(name, a, b, cs, tt):
    m, ka = a.shape
    nk = m // tt
    if b.ndim == 3:
        b_spec = pl.BlockSpec((None, tt, cs), lambda i, j, k: (j // 2, k, j % 2))
    else:
        b_spec = pl.BlockSpec((tt, cs), lambda i, j, k: (k, j))
    return _mm(
        name, a, b, dims=_TN, grid=(2, 4, nk), nk=nk, acc_shape=(ka // 2, cs),
        a_spec=pl.BlockSpec((tt, ka // 2), lambda i, j, k: (k, i)), b_spec=b_spec,
        o_spec=pl.BlockSpec((None, ka // 2, cs), lambda i, j, k: (2 * j + i, 0, 0)),
        out_shape=jax.ShapeDtypeStruct((8, ka // 2, cs), _ACT),
    )


def _rms_fwd(name, x, g):
    t, d = x.shape
    tm = _row_tile(t)

    def body(x_ref, g_ref, h_ref):
        xv = x_ref[...]
        r = lax.rsqrt(jnp.mean(xv * xv, axis=-1, keepdims=True) + EPS)
        h_ref[...] = (xv * r * g_ref[...]).astype(h_ref.dtype)

    return pl.pallas_call(
        body, name=name, grid=(t // tm,),
        in_specs=[pl.BlockSpec((tm, d), lambda i: (i, 0)), pl.BlockSpec((1, d), lambda i: (0, 0))],
        out_specs=pl.BlockSpec((tm, d), lambda i: (i, 0)), out_shape=jax.ShapeDtypeStruct((t, d), _ACT),
        compiler_params=_params(1),
    )(x, g)


def _rms_bwd(name, x, g, dh, dres):
    t, d = x.shape
    tm = _row_tile(t)

    def body(x_ref, g_ref, dh_ref, dres_ref, dx_ref, dxb_ref, dg_ref):
        @pl.when(pl.program_id(0) == 0)
        def _():
            dg_ref[...] = jnp.zeros_like(dg_ref)

        xv = x_ref[...]
        r = lax.rsqrt(jnp.mean(xv * xv, axis=-1, keepdims=True) + EPS)
        xn = xv * r
        dhv = dh_ref[...].astype(f32)
        dxn = dhv * g_ref[...]
        dx = r * (dxn - xn * jnp.mean(dxn * xn, axis=-1, keepdims=True)) + dres_ref[...]
        dx_ref[...] = dx
        dxb_ref[...] = dx.astype(dxb_ref.dtype)
        dg_ref[...] += jnp.sum(dhv * xn, axis=0, keepdims=True)

    row = pl.BlockSpec((tm, d), lambda i: (i, 0))
    vec = pl.BlockSpec((1, d), lambda i: (0, 0))
    return pl.pallas_call(
        body, name=name, grid=(t // tm,), in_specs=[row, vec, row, row], out_specs=[row, row, vec],
        out_shape=[jax.ShapeDtypeStruct((t, d), f32), jax.ShapeDtypeStruct((t, d), _ACT), jax.ShapeDtypeStruct((1, d), f32)],
        compiler_params=_params(1),
    )(x, g, dh, dres)


def _rms_gain_grad(name, x, dh):
    t, d = x.shape
    tm = _row_tile(t)

    def body(x_ref, dh_ref, dg_ref):
        @pl.when(pl.program_id(0) == 0)
        def _():
            dg_ref[...] = jnp.zeros_like(dg_ref)

        xv = x_ref[...]
        r = lax.rsqrt(jnp.mean(xv * xv, axis=-1, keepdims=True) + EPS)
        dg_ref[...] += jnp.sum(dh_ref[...] * (xv * r), axis=0, keepdims=True)

    row = pl.BlockSpec((tm, d), lambda i: (i, 0))
    return pl.pallas_call(
        body, name=name, grid=(t // tm,), in_specs=[row, row], out_specs=pl.BlockSpec((1, d), lambda i: (0, 0)),
        out_shape=jax.ShapeDtypeStruct((1, d), f32), compiler_params=_params(1),
    )(x, dh)


def _final_loss_bwd(x, g, tgt):
    t, d = x.shape
    tm = _row_tile(t)

    def body(x_ref, g_ref, t_ref, dx_ref, dxb_ref, dg_ref, loss_ref):
        @pl.when(pl.program_id(0) == 0)
        def _():
            dg_ref[...] = jnp.zeros_like(dg_ref)
            loss_ref[...] = jnp.zeros_like(loss_ref)

        xv = x_ref[...]
        gv = g_ref[...]
        r = lax.rsqrt(jnp.mean(xv * xv, axis=-1, keepdims=True) + EPS)
        xn = xv * r
        err = xn * gv - t_ref[...]
        loss_ref[...] += 0.5 * jnp.sum(jnp.mean(err * err, axis=-1, keepdims=True), axis=0, keepdims=True)
        dout = err * (1.0 / d)
        dxn = dout * gv
        dx = r * (dxn - xn * jnp.mean(dxn * xn, axis=-1, keepdims=True))
        dx_ref[...] = dx
        dxb_ref[...] = dx.astype(dxb_ref.dtype)
        dg_ref[...] += jnp.sum(dout * xn, axis=0, keepdims=True)

    row = pl.BlockSpec((tm, d), lambda i: (i, 0))
    vec = pl.BlockSpec((1, d), lambda i: (0, 0))
    one = pl.BlockSpec((1, 1), lambda i: (0, 0))
    return pl.pallas_call(
        body, name="final_loss_bwd", grid=(t // tm,), in_specs=[row, vec, row], out_specs=[row, row, vec, one],
        out_shape=[jax.ShapeDtypeStruct((t, d), f32), jax.ShapeDtypeStruct((t, d), _ACT),
                   jax.ShapeDtypeStruct((1, d), f32), jax.ShapeDtypeStruct((1, 1), f32)],
        compiler_params=_params(1),
    )(x, g, tgt)


_CONV_ROWS = 256
_CHUNK = 64
_HALO = 32


def _pool_counts(pos, w):
    return jnp.minimum(pos + 1.0, float(w))


def _mix_fwd(u, cw, cb, lg, lb, pw, ps, seq):
    t, c3 = u.shape
    c = c3 // 3
    kw = 31
    tm = min(_CONV_ROWS, seq)
    tps = seq // tm
    gd = c // len(POOL_WINDOWS)

    def body(u_ref, uh_ref, cw_ref, cb_ref, lg_ref, lb_ref, pw_ref, ps_ref, y_ref, hc_ref, hgbuf, pbuf):
        i = pl.program_id(0)
        keep = jnp.where(i % tps == 0, 0.0, 1.0)
        um = u_ref[...].astype(f32)
        uh = uh_ref[...].astype(f32) * keep
        hgbuf[0:_HALO, :] = uh[:, 0:c] * _sigmoid(uh[:, c:2 * c])
        hgbuf[_HALO:_HALO + tm, :] = um[:, 0:c] * _sigmoid(um[:, c:2 * c])
        pbuf[0:_HALO, :] = uh[:, 2 * c:]
        pbuf[_HALO:_HALO + tm, :] = um[:, 2 * c:]
        for r0 in range(0, tm, _CHUNK):
            acc = jnp.broadcast_to(cb_ref[...], (_CHUNK, c))
            for k in range(kw):
                off = r0 + _HALO - (kw - 1) + k
                acc = acc + cw_ref[k:k + 1, :] * hgbuf[off:off + _CHUNK, :]
            hc_ref[r0:r0 + _CHUNK, :] = acc
            mu = jnp.mean(acc, axis=-1, keepdims=True)
            xc = acc - mu
            var = jnp.mean(xc * xc, axis=-1, keepdims=True)
            hl = xc * lax.rsqrt(var + EPS) * lg_ref[...] + lb_ref[...]
            y_ref[r0:r0 + _CHUNK, 0:c] = (hl * _sigmoid(hl)).astype(y_ref.dtype)
        pos = ((i % tps) * tm).astype(f32) + lax.broadcasted_iota(jnp.int32, (tm, 1), 0).astype(f32)
        for gi, w in enumerate(POOL_WINDOWS):
            sl = slice(gi * gd, (gi + 1) * gd)
            v = pbuf[_HALO:_HALO + tm, sl]
            s = v
            for j in range(1, w):
                s = s + pbuf[_HALO - j:_HALO - j + tm, sl]
            pooled = s / _pool_counts(pos, w) - v
            mixed = _dot(pooled.astype(_ACT), pw_ref[gi].astype(_ACT), _NN)
            y_ref[:, c + gi * gd:c + (gi + 1) * gd] = (mixed * ps_ref[:, sl]).astype(y_ref.dtype)

    hb = tm // _HALO
    full = lambda shape: pl.BlockSpec(shape, lambda i: (0,) * len(shape))
    return pl.pallas_call(
        body, name="mix_fwd", grid=(t // tm,),
        in_specs=[pl.BlockSpec((tm, c3), lambda i: (i, 0)),
                  pl.BlockSpec((_HALO, c3), lambda i: (jnp.maximum(i * hb - 1, 0), 0)),
                  full((_HALO, c)), full((1, c)), full((1, c)), full((1, c)), full((len(POOL_WINDOWS), gd, gd)), full((1, c))],
        out_specs=[pl.BlockSpec((tm, 2 * c), lambda i: (i, 0)), pl.BlockSpec((tm, c), lambda i: (i, 0))],
        out_shape=[jax.ShapeDtypeStruct((t, 2 * c), _ACT), jax.ShapeDtypeStruct((t, c), f32)],
        scratch_shapes=[pltpu.VMEM((_HALO + tm, c), f32), pltpu.VMEM((_HALO + tm, c), f32)],
        compiler_params=_params(1),
    )(u, u, cw, cb, lg, lb, pw, ps)


def _mix_bwd_norm(hc, dy, lg, lb, seq):
    t, c = hc.shape
    tm = min(_CONV_ROWS, seq)

    def body(hc_ref, dy_ref, lg_ref, lb_ref, dhc_ref, sums_ref):
        @pl.when(pl.program_id(0) == 0)
        def _():
            sums_ref[...] = jnp.zeros_like(sums_ref)

        hcv = hc_ref[...]
        mu = jnp.mean(hcv, axis=-1, keepdims=True)
        xc = hcv - mu
        rstd = lax.rsqrt(jnp.mean(xc * xc, axis=-1, keepdims=True) + EPS)
        n = xc * rstd
        hl = n * lg_ref[...] + lb_ref[...]
        sg = _sigmoid(hl)
        dhl = dy_ref[...].astype(f32) * (sg * (1.0 + hl * (1.0 - sg)))
        dn = dhl * lg_ref[...]
        dhc = rstd * (dn - jnp.mean(dn, axis=-1, keepdims=True) - n * jnp.mean(dn * n, axis=-1, keepdims=True))
        dhc_ref[...] = dhc
        sums_ref[0:1, :] += jnp.sum(dhl * n, axis=0, keepdims=True)
        sums_ref[1:2, :] += jnp.sum(dhl, axis=0, keepdims=True)
        sums_ref[2:3, :] += jnp.sum(dhc, axis=0, keepdims=True)

    row = pl.BlockSpec((tm, c), lambda i: (i, 0))
    vec = pl.BlockSpec((1, c), lambda i: (0, 0))
    return pl.pallas_call(
        body, name="mix_bwd_norm", grid=(t // tm,), in_specs=[row, row, vec, vec],
        out_specs=[row, pl.BlockSpec((8, c), lambda i: (0, 0))],
        out_shape=[jax.ShapeDtypeStruct((t, c), f32), jax.ShapeDtypeStruct((8, c), f32)],
        compiler_params=_params(1),
    )(hc, dy, lg, lb)


def _mix_bwd_taps(u, dhc, dy, cw, pw, ps, seq):
    t, c3 = u.shape
    c = c3 // 3
    kw = 31
    tm = min(_CONV_ROWS, seq)
    tps = seq // tm
    ng = len(POOL_WINDOWS)
    gd = c // ng
    nh = 16

    def body(u_ref, uh_ref, dhc_ref, dhcn_ref, dy_ref, dyn_ref, cw_ref, pw_ref, ps_ref,
             du_ref, dcw_ref, dps_ref, dpw_ref, hgbuf, dcbuf, pbuf, dpbuf):
        i = pl.program_id(0)
        keep_prev = jnp.where(i % tps == 0, 0.0, 1.0)
        keep_next = jnp.where(i % tps == tps - 1, 0.0, 1.0)

        @pl.when(i == 0)
        def _():
            dcw_ref[...] = jnp.zeros_like(dcw_ref)
            dps_ref[...] = jnp.zeros_like(dps_ref)
            dpw_ref[...] = jnp.zeros_like(dpw_ref)

        uh = uh_ref[...].astype(f32) * keep_prev
        hgbuf[0:_HALO, :] = uh[:, 0:c] * _sigmoid(uh[:, c:2 * c])
        pbuf[0:_HALO, :] = uh[:, 2 * c:]
        um = u_ref[...].astype(f32)
        hgbuf[_HALO:_HALO + tm, :] = um[:, 0:c] * _sigmoid(um[:, c:2 * c])
        pbuf[_HALO:_HALO + tm, :] = um[:, 2 * c:]
        dcbuf[0:tm, :] = dhc_ref[...]
        dcbuf[tm:tm + _HALO, :] = dhcn_ref[...] * keep_next
        tap_sums = [None] * kw
        for r0 in range(0, tm, _CHUNK):
            dh = dcbuf[r0:r0 + _CHUNK, :]
            acc = jnp.zeros((_CHUNK, c), f32)
            for k in range(kw):
                off = r0 + _HALO - (kw - 1) + k
                part = jnp.sum(dh * hgbuf[off:off + _CHUNK, :], axis=0, keepdims=True)
                tap_sums[k] = part if tap_sums[k] is None else tap_sums[k] + part
                fwd = r0 + (kw - 1) - k
                acc = acc + cw_ref[k:k + 1, :] * dcbuf[fwd:fwd + _CHUNK, :]
            val = u_ref[r0:r0 + _CHUNK, 0:c].astype(f32)
            sg = _sigmoid(u_ref[r0:r0 + _CHUNK, c:2 * c].astype(f32))
            du_ref[r0:r0 + _CHUNK, 0:c] = (acc * sg).astype(du_ref.dtype)
            du_ref[r0:r0 + _CHUNK, c:2 * c] = (acc * val * sg * (1.0 - sg)).astype(du_ref.dtype)
        for k in range(kw):
            dcw_ref[k:k + 1, :] += tap_sums[k]
        base = ((i % tps) * tm).astype(f32)
        pos = base + lax.broadcasted_iota(jnp.int32, (tm, 1), 0).astype(f32)
        pos_next = base + float(tm) + lax.broadcasted_iota(jnp.int32, (nh, 1), 0).astype(f32)
        for gi, w in enumerate(POOL_WINDOWS):
            sl = slice(gi * gd, (gi + 1) * gd)
            v = pbuf[_HALO:_HALO + tm, sl]
            s = v
            for j in range(1, w):
                s = s + pbuf[_HALO - j:_HALO - j + tm, sl]
            cnt = _pool_counts(pos, w)
            pooled = (s / cnt - v).astype(_ACT)
            pwg = pw_ref[gi].astype(_ACT)
            mixed = _dot(pooled, pwg, _NN)
            dyp = dy_ref[:, sl].astype(f32)
            dps_ref[0:1, sl] += jnp.sum(dyp * mixed, axis=0, keepdims=True)
            dmix = (dyp * ps_ref[:, sl]).astype(_ACT)
            dpw_ref[gi] += _dot(pooled, dmix, _TN)
            dmix_next = (dyn_ref[:, sl].astype(f32) * ps_ref[:, sl] * keep_next).astype(_ACT)
            dpool = _dot(dmix, pwg, _NT)
            dpbuf[0:tm, sl] = dpool / cnt
            dpbuf[tm:tm + nh, sl] = _dot(dmix_next, pwg, _NT) / _pool_counts(pos_next, w)
            acc = -dpool
            for j in range(w):
                acc = acc + dpbuf[j:j + tm, sl]
            du_ref[:, 2 * c + gi * gd:2 * c + (gi + 1) * gd] = acc.astype(du_ref.dtype)

    hb = tm // _HALO
    n_halo = t // _HALO
    n_nh = t // nh
    full = lambda shape: pl.BlockSpec(shape, lambda i: (0,) * len(shape))
    return pl.pallas_call(
        body, name="mix_bwd_taps", grid=(t // tm,),
        in_specs=[pl.BlockSpec((tm, c3), lambda i: (i, 0)),
                  pl.BlockSpec((_HALO, c3), lambda i: (jnp.maximum(i * hb - 1, 0), 0)),
                  pl.BlockSpec((tm, c), lambda i: (i, 0)),
                  pl.BlockSpec((_HALO, c), lambda i: (jnp.minimum((i + 1) * hb, n_halo - 1), 0)),
                  pl.BlockSpec((tm, c), lambda i: (i, 1)),
                  pl.BlockSpec((nh, c), lambda i: (jnp.minimum((i + 1) * (tm // nh), n_nh - 1), 1)),
                  full((_HALO, c)), full((ng, gd, gd)), full((1, c))],
        out_specs=[pl.BlockSpec((tm, c3), lambda i: (i, 0)), full((_HALO, c)), full((8, c)), full((ng, gd, gd))],
        out_shape=[jax.ShapeDtypeStruct((t, c3), _ACT), jax.ShapeDtypeStruct((_HALO, c), f32),
                   jax.ShapeDtypeStruct((8, c), f32), jax.ShapeDtypeStruct((ng, gd, gd), f32)],
        scratch_shapes=[pltpu.VMEM((_HALO + tm, c), f32), pltpu.VMEM((tm + _HALO, c), f32),
                        pltpu.VMEM((_HALO + tm, c), f32), pltpu.VMEM((tm + nh, c), f32)],
        compiler_params=_params(1),
    )(u, u, dhc, dhc, dy, dy, cw, pw, ps)


def _attn_fwd(q, kv, n_seq, seq, n_mem):
    t, d = q.shape
    dh = d // XATTN_HEADS
    tq = min(512, seq)
    nq = seq // tq
    scale = dh ** -0.5

    def body(q_ref, k_ref, v_ref, o_ref):
        s = _dot(q_ref[...], k_ref[...], _NT) * scale
        e = jnp.exp(s - jnp.max(s, axis=-1, keepdims=True))
        p = e / jnp.sum(e, axis=-1, keepdims=True)
        o_ref[...] = _dot(p.astype(_ACT), v_ref[...], _NN).astype(o_ref.dtype)

    qs = pl.BlockSpec((tq, dh), lambda b, h, i: (b * nq + i, h))
    return pl.pallas_call(
        body, name="attn_fwd", grid=(n_seq, XATTN_HEADS, nq),
        in_specs=[qs, pl.BlockSpec((n_mem, dh), lambda b, h, i: (b, h)),
                  pl.BlockSpec((n_mem, dh), lambda b, h, i: (b, XATTN_HEADS + h))],
        out_specs=qs, out_shape=jax.ShapeDtypeStruct((t, d), _ACT), compiler_params=_params(3),
    )(q, kv, kv)


def _attn_bwd(q, kv, do, n_seq, seq, n_mem):
    t, d = q.shape
    dh = d // XATTN_HEADS
    tq = min(512, seq)
    nq = seq // tq
    scale = dh ** -0.5

    def body(q_ref, k_ref, v_ref, do_ref, dq_ref, dk_ref, dv_ref, dk_acc, dv_acc):
        i = pl.program_id(2)
        qv = q_ref[...]
        kvv = k_ref[...]
        dov = do_ref[...]
        s = _dot(qv, kvv, _NT) * scale
        e = jnp.exp(s - jnp.max(s, axis=-1, keepdims=True))
        p = e / jnp.sum(e, axis=-1, keepdims=True)
        dp = _dot(dov, v_ref[...], _NT)
        ds = (p * (dp - jnp.sum(dp * p, axis=-1, keepdims=True)) * scale).astype(_ACT)
        dq_ref[...] = _dot(ds, kvv, _NN).astype(dq_ref.dtype)
        dk_part = _dot(ds, qv, _TN)
        dv_part = _dot(p.astype(_ACT), dov, _TN)

        @pl.when(i == 0)
        def _():
            dk_acc[...] = dk_part
            dv_acc[...] = dv_part

        @pl.when(i > 0)
        def _():
            dk_acc[...] += dk_part
            dv_acc[...] += dv_part

        @pl.when(i == nq - 1)
        def _():
            dk_ref[...] = dk_acc[...].astype(dk_ref.dtype)
            dv_ref[...] = dv_acc[...].astype(dv_ref.dtype)

    qs = pl.BlockSpec((tq, dh), lambda b, h, i: (b * nq + i, h))
    ms = pl.BlockSpec((n_mem, dh), lambda b, h, i: (b, h))
    return pl.pallas_call(
        body, name="attn_bwd", grid=(n_seq, XATTN_HEADS, nq),
        in_specs=[qs, ms, pl.BlockSpec((n_mem, dh), lambda b, h, i: (b, XATTN_HEADS + h)), qs],
        out_specs=[qs, ms, ms],
        out_shape=[jax.ShapeDtypeStruct((t, d), _ACT), jax.ShapeDtypeStruct((n_seq * n_mem, d), _ACT),
                   jax.ShapeDtypeStruct((n_seq * n_mem, d), _ACT)],
        scratch_shapes=[pltpu.VMEM((n_mem, dh), f32), pltpu.VMEM((n_mem, dh), f32)],
        compiler_params=_params(3),
    )(q, kv, kv, do)


_FFN_COLS = 256
_FFN_HALO = 16


def _ffn_gate_fwd(up, fw, fb, seq):
    _, t, f = up.shape
    tm = min(_CONV_ROWS, seq)
    tps = seq // tm
    tc = _FFN_COLS
    nc = f // tc
    hl = _FFN_HALO

    def body(up_ref, uph_ref, wg_ref, wv_ref, bg_ref, bv_ref, a_ref, buf):
        i = pl.program_id(1)
        keep = jnp.where(i % tps == 0, 0.0, 1.0)
        buf[:, 0:hl, :] = uph_ref[...].astype(f32) * keep
        buf[:, hl:hl + tm, :] = up_ref[...].astype(f32)
        for r0 in range(0, tm, _CHUNK):
            conv = []
            for g, (w_ref, b_ref) in enumerate(((wg_ref, bg_ref), (wv_ref, bv_ref))):
                acc = jnp.broadcast_to(b_ref[...], (_CHUNK, tc))
                for k in range(3):
                    off = r0 + hl - 2 + k
                    acc = acc + w_ref[k:k + 1, :] * buf[g, off:off + _CHUNK, :]
                conv.append(acc)
            gate, val = conv
            a_ref[r0:r0 + _CHUNK, :] = (gate * _sigmoid(gate) * val).astype(a_ref.dtype)

    hb = tm // hl
    return pl.pallas_call(
        body, name="ffn_gate_fwd", grid=(nc, t // tm),
        in_specs=[pl.BlockSpec((2, tm, tc), lambda j, i: (0, i, j)),
                  pl.BlockSpec((2, hl, tc), lambda j, i: (0, jnp.maximum(i * hb - 1, 0), j)),
                  pl.BlockSpec((8, tc), lambda j, i: (0, j)), pl.BlockSpec((8, tc), lambda j, i: (0, nc + j)),
                  pl.BlockSpec((1, tc), lambda j, i: (0, j)), pl.BlockSpec((1, tc), lambda j, i: (0, nc + j))],
        out_specs=pl.BlockSpec((tm, tc), lambda j, i: (i, j)),
        out_shape=jax.ShapeDtypeStruct((t, f), _ACT),
        scratch_shapes=[pltpu.VMEM((2, hl + tm, tc), f32)], compiler_params=_params(2),
    )(up, up, fw, fw, fb, fb)


def _ffn_gate_bwd(up, da, fw, fb, seq):
    _, t, f = up.shape
    tm = min(_CONV_ROWS, seq)
    tps = seq // tm
    tc = _FFN_COLS
    nc = f // tc
    hl = _FFN_HALO

    def body(up_ref, uph_ref, upn_ref, da_ref, dan_ref, wg_ref, wv_ref, bg_ref, bv_ref,
             dup_ref, sg_ref, sv_ref, ubuf, dbuf):
        i = pl.program_id(1)
        keep_prev = jnp.where(i % tps == 0, 0.0, 1.0)
        keep_next = jnp.where(i % tps == tps - 1, 0.0, 1.0)

        @pl.when(i == 0)
        def _():
            sg_ref[...] = jnp.zeros_like(sg_ref)
            sv_ref[...] = jnp.zeros_like(sv_ref)

        ubuf[:, 0:hl, :] = uph_ref[...].astype(f32) * keep_prev
        ubuf[:, hl:hl + tm, :] = up_ref[...].astype(f32)
        ubuf[:, hl + tm:hl + tm + hl, :] = upn_ref[...].astype(f32) * keep_next
        w_refs = (wg_ref, wv_ref)
        b_refs = (bg_ref, bv_ref)

        def conv_rows(r0, rows):
            out = []
            for g in range(2):
                acc = jnp.broadcast_to(b_refs[g][...], (rows, tc))
                for k in range(3):
                    off = r0 + hl - 2 + k
                    acc = acc + w_refs[g][k:k + 1, :] * ubuf[g, off:off + rows, :]
                out.append(acc)
            return out

        def grads(r0, rows, dav):
            gate, val = conv_rows(r0, rows)
            sg = _sigmoid(gate)
            return dav * val * (sg * (1.0 + gate * (1.0 - sg))), dav * (gate * sg)

        for r0 in range(0, tm, _CHUNK):
            dg, dv = grads(r0, _CHUNK, da_ref[r0:r0 + _CHUNK, :].astype(f32))
            dbuf[0, r0:r0 + _CHUNK, :] = dg
            dbuf[1, r0:r0 + _CHUNK, :] = dv
        dg, dv = grads(tm, hl, dan_ref[...].astype(f32) * keep_next)
        dbuf[0, tm:tm + hl, :] = dg
        dbuf[1, tm:tm + hl, :] = dv
        for g, s_ref in enumerate((sg_ref, sv_ref)):
            sums = [None] * 4
            for r0 in range(0, tm, _CHUNK):
                d = dbuf[g, r0:r0 + _CHUNK, :]
                parts = [jnp.sum(d, axis=0, keepdims=True)]
                acc = jnp.zeros((_CHUNK, tc), f32)
                for k in range(3):
                    off = r0 + hl - 2 + k
                    parts.append(jnp.sum(d * ubuf[g, off:off + _CHUNK, :], axis=0, keepdims=True))
                    fwd = r0 + 2 - k
                    acc = acc + w_refs[g][k:k + 1, :] * dbuf[g, fwd:fwd + _CHUNK, :]
                dup_ref[g, r0:r0 + _CHUNK, :] = acc.astype(dup_ref.dtype)
                sums = [p if s is None else s + p for s, p in zip(sums, parts)]
            for r in range(4):
                s_ref[r:r + 1, :] += sums[r]

    hb = tm // hl
    n_halo = t // hl
    return pl.pallas_call(
        body, name="ffn_gate_bwd", grid=(nc, t // tm),
        in_specs=[pl.BlockSpec((2, tm, tc), lambda j, i: (0, i, j)),
                  pl.BlockSpec((2, hl, tc), lambda j, i: (0, jnp.maximum(i * hb - 1, 0), j)),
                  pl.BlockSpec((2, hl, tc), lambda j, i: (0, jnp.minimum((i + 1) * hb, n_halo - 1), j)),
                  pl.BlockSpec((tm, tc), lambda j, i: (i, j)),
                  pl.BlockSpec((hl, tc), lambda j, i: (jnp.minimum((i + 1) * hb, n_halo - 1), j)),
                  pl.BlockSpec((8, tc), lambda j, i: (0, j)), pl.BlockSpec((8, tc), lambda j, i: (0, nc + j)),
                  pl.BlockSpec((1, tc), lambda j, i: (0, j)), pl.BlockSpec((1, tc), lambda j, i: (0, nc + j))],
        out_specs=[pl.BlockSpec((2, tm, tc), lambda j, i: (0, i, j)),
                   pl.BlockSpec((8, tc), lambda j, i: (0, j)), pl.BlockSpec((8, tc), lambda j, i: (0, j))],
        out_shape=[jax.ShapeDtypeStruct((2, t, f), _ACT), jax.ShapeDtypeStruct((8, f), f32), jax.ShapeDtypeStruct((8, f), f32)],
        scratch_shapes=[pltpu.VMEM((2, hl + tm + hl, tc), f32), pltpu.VMEM((2, tm + hl, tc), f32)],
        compiler_params=_params(2),
    )(up, up, up, da, da, fw, fw, fb, fb)


def _adamw_math(w, g, m, v):
    m = ADAM_B1 * m + (1.0 - ADAM_B1) * g
    v = ADAM_B2 * v + (1.0 - ADAM_B2) * (g * g)
    m_hat = m / (1.0 - ADAM_B1 ** ADAM_STEP)
    v_hat = v / (1.0 - ADAM_B2 ** ADAM_STEP)
    delta = -ADAM_LR * (m_hat / (jnp.sqrt(v_hat) + ADAM_EPS) + ADAM_WD * w)
    return delta, m, v


def _adamw_shard(name, w, g, m, v):
    _, r, c = w.shape
    tr = next((cand for cand in (256, 176, 128, 64, 32, 16, 8) if r % cand == 0), r)

    def body(w_ref, g_ref, m_ref, v_ref, d_ref, mo_ref, vo_ref):
        d, mn, vn = _adamw_math(w_ref[...], g_ref[...], m_ref[...], v_ref[...])
        d_ref[...] = d
        mo_ref[...] = mn
        vo_ref[...] = vn

    s3 = pl.BlockSpec((None, tr, c), lambda i: (0, i, 0))
    s2 = pl.BlockSpec((tr, c), lambda i: (i, 0))
    shp = jax.ShapeDtypeStruct(w.shape, f32)
    return pl.pallas_call(
        body, name=name, grid=(r // tr,), in_specs=[s3, s2, s3, s3], out_specs=[s3, s3, s3], out_shape=[shp, shp, shp],
        compiler_params=_params(1),
    )(w, g, m, v)


def _adamw_small(quads):
    n = len(quads)

    def body(*refs):
        ins, outs = refs[:4 * n], refs[4 * n:]
        for p in range(n):
            w_ref, g_ref, m_ref, v_ref = ins[4 * p:4 * p + 4]
            d, mn, vn = _adamw_math(w_ref[...], g_ref[...], m_ref[...], v_ref[...])
            outs[3 * p][...] = d
            outs[3 * p + 1][...] = mn
            outs[3 * p + 2][...] = vn

    flat = [a for q in quads for a in q]
    shapes = [jax.ShapeDtypeStruct(q[0].shape, f32) for q in quads for _ in range(3)]
    outs = pl.pallas_call(
        body, name="adamw_small", in_specs=[_VMEM] * (4 * n), out_specs=[_VMEM] * (3 * n), out_shape=shapes,
        compiler_params=pltpu.CompilerParams(vmem_limit_bytes=_VMEM_LIMIT_BYTES),
    )(*flat)
    return [tuple(outs[3 * p:3 * p + 3]) for p in range(n)]


def _sum_pairs(name, place, grads, got):
    _, r, c = grads.shape

    def body(place_ref, a_ref, b_ref, o_ref):
        o_ref[...] = (a_ref[...].astype(f32) + b_ref[...].astype(f32)).astype(o_ref.dtype)

    grid_spec = pltpu.PrefetchScalarGridSpec(
        num_scalar_prefetch=1, grid=(4,),
        in_specs=[pl.BlockSpec((None, r, c), lambda i, p: (2 * i + p[1], 0, 0)), pl.BlockSpec((None, r, c), lambda i, p: (i, 0, 0))],
        out_specs=pl.BlockSpec((None, r, c), lambda i, p: (i, 0, 0)))
    return pl.pallas_call(body, name=name, grid_spec=grid_spec, out_shape=jax.ShapeDtypeStruct((4, r, c), _ACT),
                          compiler_params=_params(1))(place, grads, got)


def _sum_four(name, place, sums, got):
    _, r, c = sums.shape

    def body(place_ref, o_ref, g_ref, f_ref):
        s = o_ref[...].astype(f32) + g_ref[0].astype(f32)
        s = s + g_ref[1].astype(f32)
        f_ref[...] = s + g_ref[2].astype(f32)

    grid_spec = pltpu.PrefetchScalarGridSpec(
        num_scalar_prefetch=1, grid=(1,),
        in_specs=[pl.BlockSpec((None, r, c), lambda i, p: (p[0], 0, 0)), pl.BlockSpec((3, r, c), lambda i, p: (0, 0, 0))],
        out_specs=pl.BlockSpec((None, r, c), lambda i, p: (p[1], 0, 0)))
    return pl.pallas_call(body, name=name, grid_spec=grid_spec, out_shape=jax.ShapeDtypeStruct((2, r, c), f32),
                          compiler_params=_params(1))(place, sums, got)


def _place():
    return lax.axis_index("x"), lax.axis_index("y"), lax.axis_index("c")


def _other_chips(x, y):
    return [(1 - x, y), (x, 1 - y), (1 - x, 1 - y)]


def _remote(src, dst, send_sem, recv_sem, to):
    return pltpu.make_async_remote_copy(src_ref=src, dst_ref=dst, send_sem=send_sem, recv_sem=recv_sem,
                                        device_id=to, device_id_type=_MESH)


def _place_shards(place, shards, col_sharded):
    n = len(shards)
    steps = 4

    def body(place_ref, *refs):
        for src, dst in zip(refs[:n], refs[n:]):
            dst[...] = src[...].astype(dst.dtype)

    in_specs, out_specs, out_shape = [], [], []
    for w, col in zip(shards, col_sharded):
        r, cs = w.shape
        tr = r // steps
        in_specs.append(pl.BlockSpec((tr, cs), lambda i, p: (i, 0)))
        if col:
            out_specs.append(pl.BlockSpec((tr, cs), lambda i, p: (i, p[0])))
            out_shape.append(jax.ShapeDtypeStruct((r, 4 * cs), _ACT))
        else:
            out_specs.append(pl.BlockSpec((tr, cs), lambda i, p: (p[0] * steps + i, 0)))
            out_shape.append(jax.ShapeDtypeStruct((4 * r, cs), _ACT))
    grid_spec = pltpu.PrefetchScalarGridSpec(num_scalar_prefetch=1, grid=(steps,), in_specs=in_specs, out_specs=out_specs)
    return pl.pallas_call(body, name="place_shards", grid_spec=grid_spec, out_shape=out_shape,
                          compiler_params=_params(1))(place, *shards)


def _shard_of(ref, col_sharded, s):
    rows, cols = ref.shape
    if col_sharded:
        return ref.at[:, pl.ds(s * (cols // 4), cols // 4)]
    return ref.at[pl.ds(s * (rows // 4), rows // 4), :]


def _allgather_start(bufs, col_sharded, groups):
    n = len(bufs)
    ng = len(groups)

    def body(*refs):
        out = refs[n:2 * n]
        sems = refs[2 * n:]
        x, y, c = _place()
        for g, members in enumerate(groups):
            for i, w in enumerate(members):
                mine = _shard_of(out[w], col_sharded[w], 2 * x + y)
                for j, chip in enumerate(_other_chips(x, y)):
                    _remote(mine, mine, sems[2 * g].at[3 * i + j], sems[2 * g + 1].at[3 * i + j], (*chip, c)).start()

    sem_shapes = [pltpu.SemaphoreType.DMA((3 * len(m),)) for m in groups for _ in range(2)]
    outs = pl.pallas_call(
        body, name="allgather_start", in_specs=[_HBM] * n, out_specs=[_HBM] * n + [_SEM] * (2 * ng),
        out_shape=[pltpu.HBM(b.shape, b.dtype) for b in bufs] + sem_shapes,
        input_output_aliases={i: i for i in range(n)},
        compiler_params=pltpu.CompilerParams(has_side_effects=_EFFECT),
    )(*[pltpu.with_memory_space_constraint(b, pltpu.HBM) for b in bufs])
    return list(outs[:n]), [(outs[n + 2 * g], outs[n + 2 * g + 1]) for g in range(ng)]


def _allgather_wait(name, bufs, col_sharded, sems, after):
    n = len(bufs)

    def body(*refs):
        buf = refs[:n]
        send, recv = refs[n], refs[n + 1]
        x, y, c = _place()
        for i in range(n):
            mine = _shard_of(buf[i], col_sharded[i], 2 * x + y)
            for j, chip in enumerate(_other_chips(x, y)):
                landed = _shard_of(buf[i], col_sharded[i], 2 * chip[0] + chip[1])
                cp = _remote(mine, landed, send.at[3 * i + j], recv.at[3 * i + j], (*chip, c))
                cp.wait_send()
                cp.wait_recv()

    return pl.pallas_call(
        body, name=name, in_specs=[_HBM] * n + [_SEM, _SEM, _ANY], out_specs=[_HBM] * n,
        out_shape=[pltpu.HBM(b.shape, b.dtype) for b in bufs],
        input_output_aliases={i: i for i in range(n)},
        compiler_params=pltpu.CompilerParams(has_side_effects=_EFFECT),
    )(*bufs, *sems, after)


def _exchange_pair_halves(name, grads):
    nw = len(grads)

    def body(*refs):
        src = refs[:nw]
        got = refs[nw:2 * nw]
        send_sem, recv_sem = refs[2 * nw:]
        x, y, c = _place()
        sends = []
        for w in range(nw):
            for s in range(4):
                rc = _remote(src[w].at[2 * s + 1 - c], got[w].at[s], send_sem.at[4 * w + s], recv_sem.at[4 * w + s], (x, y, 1 - c))
                rc.start()
                sends.append(rc)
        for rc in sends:
            rc.wait_recv()
        for rc in sends:
            rc.wait_send()

    return pl.pallas_call(
        body, name=name, in_specs=[_ANY] * nw, out_specs=[_ANY] * nw,
        out_shape=[jax.ShapeDtypeStruct((4,) + g.shape[1:], g.dtype) for g in grads],
        scratch_shapes=[pltpu.SemaphoreType.DMA((4 * nw,)), pltpu.SemaphoreType.DMA((4 * nw,))],
    )(*grads)


def _chip_exchange_start(name, sums):
    nw = len(sums)
    lands = [lax.empty((3,) + s.shape[1:], s.dtype) for s in sums]

    def body(*refs):
        src = refs[2 * nw:3 * nw]
        got = refs[3 * nw:4 * nw]
        send, recv, token = refs[4 * nw:]
        x, y, c = _place()
        for w in range(nw):
            for j, chip in enumerate(_other_chips(x, y)):
                _remote(src[w].at[2 * chip[0] + chip[1]], got[w].at[j], send.at[3 * w + j], recv.at[3 * w + j], (*chip, c)).start()
        token[...] = jnp.zeros_like(token)

    outs = pl.pallas_call(
        body, name=name, in_specs=[_HBM] * (2 * nw), out_specs=[_HBM] * (2 * nw) + [_SEM, _SEM, _VMEM],
        out_shape=[pltpu.HBM(a.shape, a.dtype) for a in list(sums) + lands]
        + [pltpu.SemaphoreType.DMA((3 * nw,)), pltpu.SemaphoreType.DMA((3 * nw,)), jax.ShapeDtypeStruct((8, 128), f32)],
        input_output_aliases={i: i for i in range(2 * nw)},
        compiler_params=pltpu.CompilerParams(has_side_effects=_EFFECT),
    )(*[pltpu.with_memory_space_constraint(a, pltpu.HBM) for a in list(sums) + lands])
    return list(outs[:nw]), list(outs[nw:2 * nw]), (outs[2 * nw], outs[2 * nw + 1]), outs[2 * nw + 2]


def _chip_exchange_wait(name, sums, got, sems, after):
    nw = len(sums)

    def body(*refs):
        src = refs[:nw]
        land = refs[nw:2 * nw]
        send, recv = refs[2 * nw], refs[2 * nw + 1]
        x, y, c = _place()
        for w in range(nw):
            for j, chip in enumerate(_other_chips(x, y)):
                cp = _remote(src[w].at[2 * chip[0] + chip[1]], land[w].at[j], send.at[3 * w + j], recv.at[3 * w + j], (*chip, c))
                cp.wait_send()
                cp.wait_recv()

    outs = pl.pallas_call(
        body, name=name, in_specs=[_HBM] * (2 * nw) + [_SEM, _SEM, _ANY], out_specs=[_HBM] * (2 * nw),
        out_shape=[pltpu.HBM(a.shape, a.dtype) for a in list(sums) + list(got)],
        input_output_aliases={i: i for i in range(2 * nw)},
        compiler_params=pltpu.CompilerParams(has_side_effects=_EFFECT),
    )(*sums, *got, *sems, after)
    return list(outs[:nw]), list(outs[nw:])


def _swap_halves(finals):
    nw = len(finals)

    def body(*refs):
        buf = refs[nw:2 * nw]
        send_sem, recv_sem = refs[2 * nw:]
        x, y, c = _place()
        sends = []
        for w in range(nw):
            rc = _remote(buf[w].at[c], buf[w].at[c], send_sem.at[w], recv_sem.at[w], (x, y, 1 - c))
            rc.start()
            sends.append(rc)
        for w in range(nw):
            _remote(buf[w].at[1 - c], buf[w].at[1 - c], send_sem.at[w], recv_sem.at[w], (x, y, c)).wait_recv()
        for rc in sends:
            rc.wait_send()

    return pl.pallas_call(
        body, name="rs_swap_halves", in_specs=[_ANY] * nw, out_specs=[_ANY] * nw,
        out_shape=[jax.ShapeDtypeStruct(g.shape, g.dtype) for g in finals],
        input_output_aliases={i: i for i in range(nw)},
        scratch_shapes=[pltpu.SemaphoreType.DMA((nw,)), pltpu.SemaphoreType.DMA((nw,))],
    )(*finals)


def _allreduce_small(parts):
    n = len(parts)

    def body(*refs):
        src = refs[:n]
        out = refs[n:2 * n]
        slots = refs[2 * n:3 * n]
        send_sem, recv_sem = refs[3 * n:]
        x, y, c = _place()
        me = 4 * x + 2 * y + c
        flips = [(bx, by, bc) for bx in (0, 1) for by in (0, 1) for bc in (0, 1)][1:]
        sends = []
        for a in range(n):
            slots[a][me] = src[a][...]
        for k, (bx, by, bc) in enumerate(flips):
            to = (1 - x if bx else x, 1 - y if by else y, 1 - c if bc else c)
            for a in range(n):
                rc = _remote(src[a], slots[a].at[me], send_sem.at[k, a], recv_sem.at[k, a], to)
                rc.start()
                sends.append(rc)
        for rc in sends:
            rc.wait_recv()
        for a in range(n):
            s = slots[a][0]
            for d in range(1, 8):
                s = s + slots[a][d]
            out[a][...] = s
        for rc in sends:
            rc.wait_send()

    return pl.pallas_call(
        body, name="allreduce_small", in_specs=[_VMEM] * n, out_specs=[_VMEM] * n,
        out_shape=[jax.ShapeDtypeStruct(p.shape, f32) for p in parts],
        scratch_shapes=[pltpu.VMEM((8,) + p.shape, f32) for p in parts] + [pltpu.SemaphoreType.DMA((7, n)), pltpu.SemaphoreType.DMA((7, n))],
        compiler_params=pltpu.CompilerParams(vmem_limit_bytes=_VMEM_LIMIT_BYTES),
    )(*parts)


def _local_step(x, mem, tgt, g_mix, g_xattn, g_mem, g_ffn, g_final, cb, lg, lb, pw, ps, fb, weights, reduce, n_seq, seq, n_mem):
    t, d = x.shape
    f = fb.shape[1] // 2
    c = cb.shape[1]
    h1 = _rms_fwd("norm_mix", x, g_mix)
    w_in, cw, fw = weights(0, h1)
    u = _mm_nn("proj_in", h1, w_in, _ACT, w_in.shape[1])
    y, hc = _mix_fwd(u, cw, cb, lg, lb, pw, ps, seq)
    w_out, w_q, w_kv, w_o = weights(1, y)
    x1 = _mm_nn("proj_out", y, w_out, f32, d, res=x)
    h2 = _rms_fwd("norm_xattn", x1, g_xattn)
    q = _mm_nn("proj_q", h2, w_q, _ACT, d)
    mem_n = _rms_fwd("norm_mem", mem, g_mem)
    kv = _mm_nn("proj_kv", mem_n, w_kv, _ACT, 2 * d)
    o = _attn_fwd(q, kv, n_seq, seq, n_mem)
    x2 = _mm_nn("proj_o", o, w_o, f32, d, res=x1)
    h3 = _rms_fwd("norm_ffn", x2, g_ffn)
    w_up, w_down = weights(2, h3)
    up = _mm_nn("proj_up", h3, w_up, _ACT, f, split_out=True)
    a = _ffn_gate_fwd(up, fw, fb, seq)
    x3 = _mm_nn("proj_down", a, w_down, f32, d, res=x2)
    dx3, dx3b, dg_final, loss = _final_loss_bwd(x3, g_final, tgt)
    da = _mm_nt("d_act", dx3b, w_down, _ACT)
    gw_down = _mm_tn_rows("dw_down", a, dx3b, min(512, t), 512)
    dup, sums_g, sums_v = _ffn_gate_bwd(up, da, fw, fb, seq)
    dh3 = _mm_nt("d_h3", dup, w_up, f32, nk=2)
    gw_up = _mm_tn_pieces("dw_up", h3, dup, f // 2, min(512, t))
    token = reduce(0, [gw_down.reshape(8, -1, d), gw_up])
    dx2, dx2b, dg_ffn = _rms_bwd("norm_ffn_bwd", x2, g_ffn + token, dh3, dx3)
    do = _mm_nt("d_o", dx2b, w_o, _ACT)
    gw_o = _mm_tn_rows("dw_o", o, dx2b, min(1024, t), d)
    dq, dk, dv = _attn_bwd(q, kv, do, n_seq, seq, n_mem)
    dkv = jnp.concatenate([dk, dv], axis=1)
    dh2 = _mm_nt("d_h2", dq, w_q, f32)
    gw_q = _mm_tn_rows("dw_q", h2, dq, min(1024, t), d)
    gw_kv = _mm_tn_pieces("dw_kv", mem_n, dkv, d // 2, mem.shape[0])
    dmem_n = _mm_nt("d_mem_n", dkv, w_kv, f32)
    dg_mem = _rms_gain_grad("norm_mem_bwd", mem, dmem_n)
    token = reduce(1, [gw_o.reshape(8, -1, d), gw_q.reshape(8, -1, d), gw_kv])
    dx1, dx1b, dg_xattn = _rms_bwd("norm_xattn_bwd", x1, g_xattn + token, dh2, dx2)
    dy = _mm_nt("d_y", dx1b, w_out, _ACT)
    gw_out = _mm_tn_rows("dw_out", y, dx1b, min(1024, t), d)
    dhc, sums_norm = _mix_bwd_norm(hc, dy, lg, lb, seq)
    du, d_cw, d_ps, d_pw = _mix_bwd_taps(u, dhc, dy, cw, pw, ps, seq)
    dh1 = _mm_nt("d_h1", du, w_in, f32)
    gw_in = _mm_tn_pieces("dw_in", h1, du, c * 3 // 4, min(2048, t))
    reduce(2, [gw_out.reshape(8, -1, d), gw_in])
    grad_x, _, dg_mix = _rms_bwd("norm_mix_bwd", x, g_mix, dh1, dx1)
    zero_row = jnp.zeros((1, d), f32)
    gains = jnp.concatenate([dg_mix, dg_xattn, dg_mem, dg_ffn, dg_final, jnp.pad(loss, ((0, 0), (0, d - 1))), zero_row, zero_row], axis=0)
    conv_rows = jnp.concatenate([sums_norm[2:3], sums_norm[0:1], sums_norm[1:2], d_ps[0:1], jnp.zeros((4, c), f32)], axis=0)
    ffn_rows = jnp.concatenate([sums_g, sums_v], axis=1)
    small = [gains, conv_rows, d_pw.reshape(-1, d_pw.shape[-1]), ffn_rows, d_cw]
    return grad_x, small


def kernel(x, mem, norm_mix_g, w_in, conv_dw_w, conv_dw_b, conv_ln_g, conv_ln_b, pool_w, pool_scale, w_out, norm_xattn_g, norm_mem_g, w_q, w_kv, w_o, norm_ffn_g, w_up, ffn_dw_w, ffn_dw_b, w_down, norm_final_g, loss_target, m_norm_mix_g, m_w_in, m_conv_dw_w, m_conv_dw_b, m_conv_ln_g, m_conv_ln_b, m_pool_w, m_pool_scale, m_w_out, m_norm_xattn_g, m_norm_mem_g, m_w_q, m_w_kv, m_w_o, m_norm_ffn_g, m_w_up, m_ffn_dw_w, m_ffn_dw_b, m_w_down, m_norm_final_g, v_norm_mix_g, v_w_in, v_conv_dw_w, v_conv_dw_b, v_conv_ln_g, v_conv_ln_b, v_pool_w, v_pool_scale, v_w_out, v_norm_xattn_g, v_norm_mem_g, v_w_q, v_w_kv, v_w_o, v_norm_ffn_g, v_w_up, v_ffn_dw_w, v_ffn_dw_b, v_w_down, v_norm_final_g):
    n_seq, seq, d = x.shape
    n_mem = mem.shape[1]
    chip = 2 * lax.axis_index("x") + lax.axis_index("y")

    place = jnp.stack([chip, lax.axis_index("c")]).astype(jnp.int32)

    col_w = [w_in, w_kv, w_up]
    row_w = [w_out, w_q, w_o, w_down]
    col_flags = [True] * 3 + [False] * 4 + [True] * 2
    kw = conv_dw_w.shape[1]

    def padded_in_place(shard, rows):
        full = jnp.zeros((rows, 4 * shard.shape[1]), shard.dtype)
        return lax.dynamic_update_slice(full, shard, (0, chip * shard.shape[1]))

    bufs = list(_place_shards(place, [w[0] for w in col_w + row_w], col_flags[:7]))
    bufs += [padded_in_place(conv_dw_w[0], _HALO), padded_in_place(ffn_dw_w[0], 8)]
    groups = [[0, 7, 8], [3, 4, 1, 5], [2, 6]]
    bufs, sems = _allgather_start(bufs, col_flags, groups)

    def weights(g, after):
        members = groups[g]
        return _allgather_wait("allgather_wait_%d" % g, [bufs[i] for i in members], [col_flags[i] for i in members], sems[g], after)

    names = ["w_in", "w_kv", "w_up", "w_out", "w_q", "w_o", "w_down"]
    reduce_groups = [["w_down", "w_up"], ["w_o", "w_q", "w_kv"], ["w_out", "w_in"]]
    in_flight = {}

    def reduce(g, grads):
        members = reduce_groups[g]
        got = _exchange_pair_halves("rs_pair_exchange_%d" % g, grads)
        sums = [_sum_pairs("rs_pair_sum_" + n, place, a, b) for n, a, b in zip(members, grads, got)]
        sums, lands, rs_sems, token = _chip_exchange_start("rs_chip_start_%d" % g, sums)
        in_flight[g] = (sums, lands, rs_sems)
        return token[0:1, 0:1]

    grad_x, small = _local_step(
        x.reshape(n_seq * seq, d), mem.reshape(n_seq * n_mem, d), loss_target.reshape(n_seq * seq, d),
        norm_mix_g, norm_xattn_g, norm_mem_g, norm_ffn_g, norm_final_g.reshape(1, d),
        conv_dw_b, conv_ln_g, conv_ln_b, pool_w[0], pool_scale, ffn_dw_b, weights, reduce, n_seq, seq, n_mem)

    finals = {}
    for g, members in enumerate(reduce_groups):
        sums, lands, rs_sems = in_flight[g]
        sums, lands = _chip_exchange_wait("rs_chip_wait_%d" % g, sums, lands, rs_sems, grad_x)
        for n, a, b in zip(members, sums, lands):
            finals[n] = _sum_four("rs_chip_sum_" + n, place, a, b)
    shard_grads = _swap_halves([finals[n] for n in names])

    gains, conv_rows, d_pw, ffn_rows, d_cw = _allreduce_small(small)
    loss = gains[5, 0]

    outs = {}
    big_w = dict(zip(names, col_w + row_w))
    big_m = dict(w_in=m_w_in, w_kv=m_w_kv, w_up=m_w_up, w_out=m_w_out, w_q=m_w_q, w_o=m_w_o, w_down=m_w_down)
    big_v = dict(w_in=v_w_in, w_kv=v_w_kv, w_up=v_w_up, w_out=v_w_out, w_q=v_w_q, w_o=v_w_o, w_down=v_w_down)
    for n, g in zip(names, shard_grads):
        w = big_w[n]
        g2 = g.reshape(w.shape[1], w.shape[2])
        delta, new_m, new_v = _adamw_shard("adamw_" + n, w, g2, big_m[n], big_v[n])
        outs[n] = (g2.reshape(w.shape), delta, new_m, new_v)

    f2 = ffn_dw_b.shape[1]
    cs_c = conv_dw_w.shape[2]
    cs_f = ffn_dw_w.shape[2]
    g_cw = lax.dynamic_slice(d_cw, (0, chip * cs_c), (kw, cs_c)).reshape(conv_dw_w.shape)
    g_fw = lax.dynamic_slice(ffn_rows, (1, chip * cs_f), (ffn_dw_w.shape[1], cs_f)).reshape(ffn_dw_w.shape)
    small_params = [
        ("norm_mix_g", norm_mix_g, gains[0:1], m_norm_mix_g, v_norm_mix_g),
        ("conv_dw_w", conv_dw_w, g_cw, m_conv_dw_w, v_conv_dw_w),
        ("conv_dw_b", conv_dw_b, conv_rows[0:1], m_conv_dw_b, v_conv_dw_b),
        ("conv_ln_g", conv_ln_g, conv_rows[1:2], m_conv_ln_g, v_conv_ln_g),
        ("conv_ln_b", conv_ln_b, conv_rows[2:3], m_conv_ln_b, v_conv_ln_b),
        ("pool_w", pool_w, d_pw.reshape(pool_w.shape), m_pool_w, v_pool_w),
        ("pool_scale", pool_scale, conv_rows[3:4], m_pool_scale, v_pool_scale),
        ("norm_xattn_g", norm_xattn_g, gains[1:2], m_norm_xattn_g, v_norm_xattn_g),
        ("norm_mem_g", norm_mem_g, gains[2:3], m_norm_mem_g, v_norm_mem_g),
        ("norm_ffn_g", norm_ffn_g, gains[3:4], m_norm_ffn_g, v_norm_ffn_g),
        ("ffn_dw_w", ffn_dw_w, g_fw, m_ffn_dw_w, v_ffn_dw_w),
        ("ffn_dw_b", ffn_dw_b, ffn_rows[0:1, :f2], m_ffn_dw_b, v_ffn_dw_b),
        ("norm_final_g", norm_final_g.reshape(1, d), gains[4:5], m_norm_final_g.reshape(1, d), v_norm_final_g.reshape(1, d)),
    ]
    quads = []
    for _, w, g, m, v in small_params:
        shape2 = (-1, w.shape[-1])
        quads.append((w.reshape(shape2), g.reshape(shape2), m.reshape(shape2), v.reshape(shape2)))
    for (n, w, g, _, _), (delta, new_m, new_v) in zip(small_params, _adamw_small(quads)):
        shape = norm_final_g.shape if n == "norm_final_g" else w.shape
        outs[n] = (g.reshape(shape), delta.reshape(shape), new_m.reshape(shape), new_v.reshape(shape))

    order = ["norm_mix_g", "w_in", "conv_dw_w", "conv_dw_b", "conv_ln_g", "conv_ln_b", "pool_w", "pool_scale", "w_out",
             "norm_xattn_g", "norm_mem_g", "w_q", "w_kv", "w_o", "norm_ffn_g", "w_up", "ffn_dw_w", "ffn_dw_b", "w_down",
             "norm_final_g"]
    return (loss, grad_x.reshape(x.shape), *[outs[n][0] for n in order], *[outs[n][1] for n in order],
            *[outs[n][2] for n in order], *[outs[n][3] for n in order])
```

```python
import functools

import jax
import jax.numpy as jnp
from jax import lax
from jax.experimental import pallas as pl
from jax.experimental.pallas import tpu as pltpu

f32 = jnp.float32
_ACT = jnp.bfloat16

EPS = 1e-6
POOL_WINDOWS = (2, 4, 8, 16)
XATTN_HEADS = 4
ADAM_LR = 0.001
ADAM_B1 = 0.9
ADAM_B2 = 0.999
ADAM_EPS = 1e-08
ADAM_WD = 0.01
ADAM_STEP = 10

_VMEM_LIMIT_BYTES = 56 * 1024 * 1024
_MESH = pl.DeviceIdType.MESH
_ANY = pl.BlockSpec(memory_space=pl.ANY)
_VMEM = pl.BlockSpec(memory_space=pltpu.VMEM)
_HBM = pl.BlockSpec(memory_space=pltpu.HBM)
_SEM = pl.BlockSpec(memory_space=pltpu.SEMAPHORE)
_EFFECT = pltpu.SideEffectType.DATAFLOW_SIDE_EFFECTING

_NN = (((1,), (0,)), ((), ()))
_NT = (((1,), (1,)), ((), ()))
_TN = (((0,), (0,)), ((), ()))


def _params(n_grid):
    return pltpu.CompilerParams(dimension_semantics=("arbitrary",) * n_grid, vmem_limit_bytes=_VMEM_LIMIT_BYTES)


def _sigmoid(v):
    return 1.0 / (1.0 + jnp.exp(-v))


def _dot(a, b, dims):
    return lax.dot_general(a, b, dims, preferred_element_type=f32)


def _mm(name, a, b, *, dims, grid, a_spec, b_spec, o_spec, out_shape, nk, acc_shape=None, res=None, res_spec=None):
    def body(*refs):
        if res is None:
            a_ref, b_ref, o_ref, *scratch = refs
            r_ref = None
        else:
            a_ref, b_ref, r_ref, o_ref, *scratch = refs
        p = _dot(a_ref[...], b_ref[...], dims)

        def finish(v):
            if r_ref is not None:
                v = v + r_ref[...]
            o_ref[...] = v.astype(o_ref.dtype)

        if nk == 1:
            finish(p)
        else:
            acc = scratch[0]
            k = pl.program_id(2)

            @pl.when(k == 0)
            def _():
                acc[...] = p

            @pl.when(k > 0)
            def _():
                acc[...] += p

            @pl.when(k == nk - 1)
            def _():
                finish(acc[...])

    ins = [a, b] + ([] if res is None else [res])
    specs = [a_spec, b_spec] + ([] if res is None else [res_spec])
    return pl.pallas_call(
        body, name=name, grid=grid, in_specs=specs, out_specs=o_spec, out_shape=out_shape,
        scratch_shapes=[pltpu.VMEM(acc_shape, f32)] if nk > 1 else [], compiler_params=_params(3),
    )(*ins)


def _row_tile(m):
    return min(512, m)


def _mm_nn(name, a, b, out_dtype, tn, res=None, split_out=False):
    m, k = a.shape
    n = b.shape[1]
    tm = _row_tile(m)
    if split_out:
        out_shape = jax.ShapeDtypeStruct((n // tn, m, tn), out_dtype)
        o_spec = pl.BlockSpec((None, tm, tn), lambda j, i, kk: (j, i, 0))
    else:
        out_shape = jax.ShapeDtypeStruct((m, n), out_dtype)
        o_spec = pl.BlockSpec((tm, tn), lambda j, i, kk: (i, j))
    return _mm(
        name, a, b, dims=_NN, grid=(n // tn, m // tm, 1), nk=1,
        a_spec=pl.BlockSpec((tm, k), lambda j, i, kk: (i, 0)),
        b_spec=pl.BlockSpec((k, tn), lambda j, i, kk: (0, j)),
        o_spec=o_spec, out_shape=out_shape, res=res,
        res_spec=pl.BlockSpec((tm, tn), lambda j, i, kk: (i, j)),
    )


def _mm_nt(name, a, b, out_dtype):
    n, kc = b.shape
    m = a.shape[0]
    tm = _row_tile(m)
    return _mm(
        name, a, b, dims=_NT, grid=(m // tm, 1, 1), nk=1,
        a_spec=pl.BlockSpec((tm, kc), lambda i, j, k: (i, 0)), b_spec=pl.BlockSpec((n, kc), lambda i, j, k: (0, 0)),
        o_spec=pl.BlockSpec((tm, n), lambda i, j, k: (i, 0)),
        out_shape=jax.ShapeDtypeStruct((m, n), out_dtype),
    )


def _mm_nt_halves(name, a, b, out_dtype):
    nh, m, kh = a.shape
    n = b.shape[0]
    tm = _row_tile(m)

    def body(a_ref, b_ref, o_ref):
        p = _dot(a_ref[0], b_ref[:, 0:kh], _NT)
        for h in range(1, nh):
            p = p + _dot(a_ref[h], b_ref[:, h * kh:(h + 1) * kh], _NT)
        o_ref[...] = p.astype(o_ref.dtype)

    return pl.pallas_call(
        body, name=name, grid=(m // tm,),
        in_specs=[pl.BlockSpec((nh, tm, kh), lambda i: (0, i, 0)), pl.BlockSpec(b.shape, lambda i: (0, 0))],
        out_specs=pl.BlockSpec((tm, n), lambda i: (i, 0)), out_shape=jax.ShapeDtypeStruct((m, n), out_dtype),
        compiler_params=_params(1),
    )(a, b)


def _mm_tn_rows(name, a, b, tka, tn):
    m, ka = a.shape
    nb = b.shape[1]
    return _mm(
        name, a, b, dims=_TN, grid=(ka // tka, nb // tn, 1), nk=1,
        a_spec=pl.BlockSpec((m, tka), lambda i, j, k: (0, i)),
        b_spec=pl.BlockSpec((m, tn), lambda i, j, k: (0, j)),
        o_spec=pl.BlockSpec((tka, tn), lambda i, j, k: (i, j)),
        out_shape=jax.ShapeDtypeStruct((ka, nb), _ACT),
    )


def _mm_tn_pieces(name, a, b, cs, tt):
    m, ka = a.shape
    nk = m // tt
    if b.ndim == 3:
        b_spec = pl.BlockSpec((None, tt, cs), lambda i, j, k: (j // 2, k, j % 2))
    else:
        b_spec = pl.BlockSpec((tt, cs), lambda i, j, k: (k, j))
    return _mm(
        name, a, b, dims=_TN, grid=(2, 4, nk), nk=nk, acc_shape=(ka // 2, cs),
        a_spec=pl.BlockSpec((tt, ka // 2), lambda i, j, k: (k, i)), b_spec=b_spec,
        o_spec=pl.BlockSpec((None, ka // 2, cs), lambda i, j, k: (2 * j + i, 0, 0)),
        out_shape=jax.ShapeDtypeStruct((8, ka // 2, cs), _ACT),
    )


def _rms_fwd(name, x, g):
    t, d = x.shape
    tm = _row_tile(t)

    def body(x_ref, g_ref, h_ref):
        xv = x_ref[...]
        r = lax.rsqrt(jnp.mean(xv * xv, axis=-1, keepdims=True) + EPS)
        h_ref[...] = (xv * r * g_ref[...]).astype(h_ref.dtype)

    return pl.pallas_call(
        body, name=name, grid=(t // tm,),
        in_specs=[pl.BlockSpec((tm, d), lambda i: (i, 0)), pl.BlockSpec((1, d), lambda i: (0, 0))],
        out_specs=pl.BlockSpec((tm, d), lambda i: (i, 0)), out_shape=jax.ShapeDtypeStruct((t, d), _ACT),
        compiler_params=_params(1),
    )(x, g)


def _rms_bwd(name, x, g, dh, dres):
    t, d = x.shape
    tm = _row_tile(t)

    def body(x_ref, g_ref, dh_ref, dres_ref, dx_ref, dxb_ref, dg_ref):
        @pl.when(pl.program_id(0) == 0)
        def _():
            dg_ref[...] = jnp.zeros_like(dg_ref)

        xv = x_ref[...]
        r = lax.rsqrt(jnp.mean(xv * xv, axis=-1, keepdims=True) + EPS)
        xn = xv * r
        dhv = dh_ref[...].astype(f32)
        dxn = dhv * g_ref[...]
        dx = r * (dxn - xn * jnp.mean(dxn * xn, axis=-1, keepdims=True)) + dres_ref[...]
        dx_ref[...] = dx
        dxb_ref[...] = dx.astype(dxb_ref.dtype)
        dg_ref[...] += jnp.sum(dhv * xn, axis=0, keepdims=True)

    row = pl.BlockSpec((tm, d), lambda i: (i, 0))
    vec = pl.BlockSpec((1, d), lambda i: (0, 0))
    return pl.pallas_call(
        body, name=name, grid=(t // tm,), in_specs=[row, vec, row, row], out_specs=[row, row, vec],
        out_shape=[jax.ShapeDtypeStruct((t, d), f32), jax.ShapeDtypeStruct((t, d), _ACT), jax.ShapeDtypeStruct((1, d), f32)],
        compiler_params=_params(1),
    )(x, g, dh, dres)


def _rms_gain_grad(name, x, dh):
    t, d = x.shape
    tm = _row_tile(t)

    def body(x_ref, dh_ref, dg_ref):
        @pl.when(pl.program_id(0) == 0)
        def _():
            dg_ref[...] = jnp.zeros_like(dg_ref)

        xv = x_ref[...]
        r = lax.rsqrt(jnp.mean(xv * xv, axis=-1, keepdims=True) + EPS)
        dg_ref[...] += jnp.sum(dh_ref[...] * (xv * r), axis=0, keepdims=True)

    row = pl.BlockSpec((tm, d), lambda i: (i, 0))
    return pl.pallas_call(
        body, name=name, grid=(t // tm,), in_specs=[row, row], out_specs=pl.BlockSpec((1, d), lambda i: (0, 0)),
        out_shape=jax.ShapeDtypeStruct((1, d), f32), compiler_params=_params(1),
    )(x, dh)


def _final_loss_bwd(x, g, tgt):
    t, d = x.shape
    tm = _row_tile(t)

    def body(x_ref, g_ref, t_ref, dx_ref, dxb_ref, dg_ref, loss_ref):
        @pl.when(pl.program_id(0) == 0)
        def _():
            dg_ref[...] = jnp.zeros_like(dg_ref)
            loss_ref[...] = jnp.zeros_like(loss_ref)

        xv = x_ref[...]
        gv = g_ref[...]
        r = lax.rsqrt(jnp.mean(xv * xv, axis=-1, keepdims=True) + EPS)
        xn = xv * r
        err = xn * gv - t_ref[...]
        loss_ref[...] += 0.5 * jnp.sum(jnp.mean(err * err, axis=-1, keepdims=True), axis=0, keepdims=True)
        dout = err * (1.0 / d)
        dxn = dout * gv
        dx = r * (dxn - xn * jnp.mean(dxn * xn, axis=-1, keepdims=True))
        dx_ref[...] = dx
        dxb_ref[...] = dx.astype(dxb_ref.dtype)
        dg_ref[...] += jnp.sum(dout * xn, axis=0, keepdims=True)

    row = pl.BlockSpec((tm, d), lambda i: (i, 0))
    vec = pl.BlockSpec((1, d), lambda i: (0, 0))
    one = pl.BlockSpec((1, 1), lambda i: (0, 0))
    return pl.pallas_call(
        body, name="final_loss_bwd", grid=(t // tm,), in_specs=[row, vec, row], out_specs=[row, row, vec, one],
        out_shape=[jax.ShapeDtypeStruct((t, d), f32), jax.ShapeDtypeStruct((t, d), _ACT),
                   jax.ShapeDtypeStruct((1, d), f32), jax.ShapeDtypeStruct((1, 1), f32)],
        compiler_params=_params(1),
    )(x, g, tgt)


_CONV_ROWS = 256
_CHUNK = 64
_HALO = 32


def _pool_counts(pos, w):
    return jnp.minimum(pos + 1.0, float(w))


def _mix_fwd(u, cw, cb, lg, lb, pw, ps, seq):
    t, c3 = u.shape
    c = c3 // 3
    kw = 31
    tm = min(_CONV_ROWS, seq)
    tps = seq // tm
    gd = c // len(POOL_WINDOWS)

    def body(u_ref, uh_ref, cw_ref, cb_ref, lg_ref, lb_ref, pw_ref, ps_ref, y_ref, hc_ref, hgbuf, pbuf):
        i = pl.program_id(0)
        keep = jnp.where(i % tps == 0, 0.0, 1.0)
        um = u_ref[...].astype(f32)
        uh = uh_ref[...].astype(f32) * keep
        hgbuf[0:_HALO, :] = uh[:, 0:c] * _sigmoid(uh[:, c:2 * c])
        hgbuf[_HALO:_HALO + tm, :] = um[:, 0:c] * _sigmoid(um[:, c:2 * c])
        pbuf[0:_HALO, :] = uh[:, 2 * c:]
        pbuf[_HALO:_HALO + tm, :] = um[:, 2 * c:]
        for r0 in range(0, tm, _CHUNK):
            acc = jnp.broadcast_to(cb_ref[...], (_CHUNK, c))
            for k in range(kw):
                off = r0 + _HALO - (kw - 1) + k
                acc = acc + cw_ref[k:k + 1, :] * hgbuf[off:off + _CHUNK, :]
            hc_ref[r0:r0 + _CHUNK, :] = acc
            mu = jnp.mean(acc, axis=-1, keepdims=True)
            xc = acc - mu
            var = jnp.mean(xc * xc, axis=-1, keepdims=True)
            hl = xc * lax.rsqrt(var + EPS) * lg_ref[...] + lb_ref[...]
            y_ref[r0:r0 + _CHUNK, 0:c] = (hl * _sigmoid(hl)).astype(y_ref.dtype)
        pos = ((i % tps) * tm).astype(f32) + lax.broadcasted_iota(jnp.int32, (tm, 1), 0).astype(f32)
        for gi, w in enumerate(POOL_WINDOWS):
            sl = slice(gi * gd, (gi + 1) * gd)
            v = pbuf[_HALO:_HALO + tm, sl]
            s = v
            for j in range(1, w):
                s = s + pbuf[_HALO - j:_HALO - j + tm, sl]
            pooled = s / _pool_counts(pos, w) - v
            mixed = _dot(pooled.astype(_ACT), pw_ref[gi].astype(_ACT), _NN)
            y_ref[:, c + gi * gd:c + (gi + 1) * gd] = (mixed * ps_ref[:, sl]).astype(y_ref.dtype)

    hb = tm // _HALO
    full = lambda shape: pl.BlockSpec(shape, lambda i: (0,) * len(shape))
    return pl.pallas_call(
        body, name="mix_fwd", grid=(t // tm,),
        in_specs=[pl.BlockSpec((tm, c3), lambda i: (i, 0)),
                  pl.BlockSpec((_HALO, c3), lambda i: (jnp.maximum(i * hb - 1, 0), 0)),
                  full((_HALO, c)), full((1, c)), full((1, c)), full((1, c)), full((len(POOL_WINDOWS), gd, gd)), full((1, c))],
        out_specs=[pl.BlockSpec((tm, 2 * c), lambda i: (i, 0)), pl.BlockSpec((tm, c), lambda i: (i, 0))],
        out_shape=[jax.ShapeDtypeStruct((t, 2 * c), _ACT), jax.ShapeDtypeStruct((t, c), f32)],
        scratch_shapes=[pltpu.VMEM((_HALO + tm, c), f32), pltpu.VMEM((_HALO + tm, c), f32)],
        compiler_params=_params(1),
    )(u, u, cw, cb, lg, lb, pw, ps)


def _mix_bwd_norm(hc, dy, lg, lb, seq):
    t, c = hc.shape
    tm = min(_CONV_ROWS, seq)

    def body(hc_ref, dy_ref, lg_ref, lb_ref, dhc_ref, sums_ref):
        @pl.when(pl.program_id(0) == 0)
        def _():
            sums_ref[...] = jnp.zeros_like(sums_ref)

        hcv = hc_ref[...]
        mu = jnp.mean(hcv, axis=-1, keepdims=True)
        xc = hcv - mu
        rstd = lax.rsqrt(jnp.mean(xc * xc, axis=-1, keepdims=True) + EPS)
        n = xc * rstd
        hl = n * lg_ref[...] + lb_ref[...]
        sg = _sigmoid(hl)
        dhl = dy_ref[...].astype(f32) * (sg * (1.0 + hl * (1.0 - sg)))
        dn = dhl * lg_ref[...]
        dhc = rstd * (dn - jnp.mean(dn, axis=-1, keepdims=True) - n * jnp.mean(dn * n, axis=-1, keepdims=True))
        dhc_ref[...] = dhc
        sums_ref[0:1, :] += jnp.sum(dhl * n, axis=0, keepdims=True)
        sums_ref[1:2, :] += jnp.sum(dhl, axis=0, keepdims=True)
        sums_ref[2:3, :] += jnp.sum(dhc, axis=0, keepdims=True)

    row = pl.BlockSpec((tm, c), lambda i: (i, 0))
    vec = pl.BlockSpec((1, c), lambda i: (0, 0))
    return pl.pallas_call(
        body, name="mix_bwd_norm", grid=(t // tm,), in_specs=[row, row, vec, vec],
        out_specs=[row, pl.BlockSpec((8, c), lambda i: (0, 0))],
        out_shape=[jax.ShapeDtypeStruct((t, c), f32), jax.ShapeDtypeStruct((8, c), f32)],
        compiler_params=_params(1),
    )(hc, dy, lg, lb)


def _mix_bwd_taps(u, dhc, dy, cw, pw, ps, seq):
    t, c3 = u.shape
    c = c3 // 3
    kw = 31
    tm = min(_CONV_ROWS, seq)
    tps = seq // tm
    ng = len(POOL_WINDOWS)
    gd = c // ng
    nh = 16

    def body(u_ref, uh_ref, dhc_ref, dhcn_ref, dy_ref, dyn_ref, cw_ref, pw_ref, ps_ref,
             du_ref, dcw_ref, dps_ref, dpw_ref, hgbuf, dcbuf, pbuf, dpbuf):
        i = pl.program_id(0)
        keep_prev = jnp.where(i % tps == 0, 0.0, 1.0)
        keep_next = jnp.where(i % tps == tps - 1, 0.0, 1.0)

        @pl.when(i == 0)
        def _():
            dcw_ref[...] = jnp.zeros_like(dcw_ref)
            dps_ref[...] = jnp.zeros_like(dps_ref)
            dpw_ref[...] = jnp.zeros_like(dpw_ref)

        uh = uh_ref[...].astype(f32) * keep_prev
        hgbuf[0:_HALO, :] = uh[:, 0:c] * _sigmoid(uh[:, c:2 * c])
        pbuf[0:_HALO, :] = uh[:, 2 * c:]
        um = u_ref[...].astype(f32)
        hgbuf[_HALO:_HALO + tm, :] = um[:, 0:c] * _sigmoid(um[:, c:2 * c])
        pbuf[_HALO:_HALO + tm, :] = um[:, 2 * c:]
        dcbuf[0:tm, :] = dhc_ref[...]
        dcbuf[tm:tm + _HALO, :] = dhcn_ref[...] * keep_next
        tap_sums = [None] * kw
        for r0 in range(0, tm, _CHUNK):
            dh = dcbuf[r0:r0 + _CHUNK, :]
            acc = jnp.zeros((_CHUNK, c), f32)
            for k in range(kw):
                off = r0 + _HALO - (kw - 1) + k
                part = jnp.sum(dh * hgbuf[off:off + _CHUNK, :], axis=0, keepdims=True)
                tap_sums[k] = part if tap_sums[k] is None else tap_sums[k] + part
                fwd = r0 + (kw - 1) - k
                acc = acc + cw_ref[k:k + 1, :] * dcbuf[fwd:fwd + _CHUNK, :]
            val = u_ref[r0:r0 + _CHUNK, 0:c].astype(f32)
            sg = _sigmoid(u_ref[r0:r0 + _CHUNK, c:2 * c].astype(f32))
            du_ref[r0:r0 + _CHUNK, 0:c] = (acc * sg).astype(du_ref.dtype)
            du_ref[r0:r0 + _CHUNK, c:2 * c] = (acc * val * sg * (1.0 - sg)).astype(du_ref.dtype)
        for k in range(kw):
            dcw_ref[k:k + 1, :] += tap_sums[k]
        base = ((i % tps) * tm).astype(f32)
        pos = base + lax.broadcasted_iota(jnp.int32, (tm, 1), 0).astype(f32)
        pos_next = base + float(tm) + lax.broadcasted_iota(jnp.int32, (nh, 1), 0).astype(f32)
        for gi, w in enumerate(POOL_WINDOWS):
            sl = slice(gi * gd, (gi + 1) * gd)
            v = pbuf[_HALO:_HALO + tm, sl]
            s = v
            for j in range(1, w):
                s = s + pbuf[_HALO - j:_HALO - j + tm, sl]
            cnt = _pool_counts(pos, w)
            pooled = (s / cnt - v).astype(_ACT)
            pwg = pw_ref[gi].astype(_ACT)
            mixed = _dot(pooled, pwg, _NN)
            dyp = dy_ref[:, sl].astype(f32)
            dps_ref[0:1, sl] += jnp.sum(dyp * mixed, axis=0, keepdims=True)
            dmix = (dyp * ps_ref[:, sl]).astype(_ACT)
            dpw_ref[gi] += _dot(pooled, dmix, _TN)
            dmix_next = (dyn_ref[:, sl].astype(f32) * ps_ref[:, sl] * keep_next).astype(_ACT)
            dpool = _dot(dmix, pwg, _NT)
            dpbuf[0:tm, sl] = dpool / cnt
            dpbuf[tm:tm + nh, sl] = _dot(dmix_next, pwg, _NT) / _pool_counts(pos_next, w)
            acc = -dpool
            for j in range(w):
                acc = acc + dpbuf[j:j + tm, sl]
            du_ref[:, 2 * c + gi * gd:2 * c + (gi + 1) * gd] = acc.astype(du_ref.dtype)

    hb = tm // _HALO
    n_halo = t // _HALO
    n_nh = t // nh
    full = lambda shape: pl.BlockSpec(shape, lambda i: (0,) * len(shape))
    return pl.pallas_call(
        body, name="mix_bwd_taps", grid=(t // tm,),
        in_specs=[pl.BlockSpec((tm, c3), lambda i: (i, 0)),
                  pl.BlockSpec((_HALO, c3), lambda i: (jnp.maximum(i * hb - 1, 0), 0)),
                  pl.BlockSpec((tm, c), lambda i: (i, 0)),
                  pl.BlockSpec((_HALO, c), lambda i: (jnp.minimum((i + 1) * hb, n_halo - 1), 0)),
                  pl.BlockSpec((tm, c), lambda i: (i, 1)),
                  pl.BlockSpec((nh, c), lambda i: (jnp.minimum((i + 1) * (tm // nh), n_nh - 1), 1)),
                  full((_HALO, c)), full((ng, gd, gd)), full((1, c))],
        out_specs=[pl.BlockSpec((tm, c3), lambda i: (i, 0)), full((_HALO, c)), full((8, c)), full((ng, gd, gd))],
        out_shape=[jax.ShapeDtypeStruct((t, c3), _ACT), jax.ShapeDtypeStruct((_HALO, c), f32),
                   jax.ShapeDtypeStruct((8, c), f32), jax.ShapeDtypeStruct((ng, gd, gd), f32)],
        scratch_shapes=[pltpu.VMEM((_HALO + tm, c), f32), pltpu.VMEM((tm + _HALO, c), f32),
                        pltpu.VMEM((_HALO + tm, c), f32), pltpu.VMEM((tm + nh, c), f32)],
        compiler_params=_params(1),
    )(u, u, dhc, dhc, dy, dy, cw, pw, ps)


def _attn_fwd(q, kv, n_seq, seq, n_mem):
    t, d = q.shape
    dh = d // XATTN_HEADS
    tq = min(512, seq)
    nq = seq // tq
    scale = dh ** -0.5

    def body(q_ref, k_ref, v_ref, o_ref):
        s = _dot(q_ref[...], k_ref[...], _NT) * scale
        e = jnp.exp(s - jnp.max(s, axis=-1, keepdims=True))
        p = e / jnp.sum(e, axis=-1, keepdims=True)
        o_ref[...] = _dot(p.astype(_ACT), v_ref[...], _NN).astype(o_ref.dtype)

    qs = pl.BlockSpec((tq, dh), lambda b, h, i: (b * nq + i, h))
    return pl.pallas_call(
        body, name="attn_fwd", grid=(n_seq, XATTN_HEADS, nq),
        in_specs=[qs, pl.BlockSpec((n_mem, dh), lambda b, h, i: (b, h)),
                  pl.BlockSpec((n_mem, dh), lambda b, h, i: (b, XATTN_HEADS + h))],
        out_specs=qs, out_shape=jax.ShapeDtypeStruct((t, d), _ACT), compiler_params=_params(3),
    )(q, kv, kv)


def _attn_bwd(q, kv, do, n_seq, seq, n_mem):
    t, d = q.shape
    dh = d // XATTN_HEADS
    tq = min(512, seq)
    nq = seq // tq
    scale = dh ** -0.5

    def body(q_ref, k_ref, v_ref, do_ref, dq_ref, dk_ref, dv_ref, dk_acc, dv_acc):
        i = pl.program_id(2)
        qv = q_ref[...]
        kvv = k_ref[...]
        dov = do_ref[...]
        s = _dot(qv, kvv, _NT) * scale
        e = jnp.exp(s - jnp.max(s, axis=-1, keepdims=True))
        p = e / jnp.sum(e, axis=-1, keepdims=True)
        dp = _dot(dov, v_ref[...], _NT)
        ds = (p * (dp - jnp.sum(dp * p, axis=-1, keepdims=True)) * scale).astype(_ACT)
        dq_ref[...] = _dot(ds, kvv, _NN).astype(dq_ref.dtype)
        dk_part = _dot(ds, qv, _TN)
        dv_part = _dot(p.astype(_ACT), dov, _TN)

        @pl.when(i == 0)
        def _():
            dk_acc[...] = dk_part
            dv_acc[...] = dv_part

        @pl.when(i > 0)
        def _():
            dk_acc[...] += dk_part
            dv_acc[...] += dv_part

        @pl.when(i == nq - 1)
        def _():
            dk_ref[...] = dk_acc[...].astype(dk_ref.dtype)
            dv_ref[...] = dv_acc[...].astype(dv_ref.dtype)

    qs = pl.BlockSpec((tq, dh), lambda b, h, i: (b * nq + i, h))
    ms = pl.BlockSpec((n_mem, dh), lambda b, h, i: (b, h))
    return pl.pallas_call(
        body, name="attn_bwd", grid=(n_seq, XATTN_HEADS, nq),
        in_specs=[qs, ms, pl.BlockSpec((n_mem, dh), lambda b, h, i: (b, XATTN_HEADS + h)), qs],
        out_specs=[qs, ms, ms],
        out_shape=[jax.ShapeDtypeStruct((t, d), _ACT), jax.ShapeDtypeStruct((n_seq * n_mem, d), _ACT),
                   jax.ShapeDtypeStruct((n_seq * n_mem, d), _ACT)],
        scratch_shapes=[pltpu.VMEM((n_mem, dh), f32), pltpu.VMEM((n_mem, dh), f32)],
        compiler_params=_params(3),
    )(q, kv, kv, do)


_FFN_ROWS = 1024
_FFN_COLS = 256
_FFN_HALO = 16


def _window(buf, g, start, rows):
    return buf[g, pl.ds(start, rows + 8), :]


def _conv3(b_ref, w_ref, win, rows):
    acc = jnp.broadcast_to(b_ref[...], (rows, win.shape[1]))
    for k in range(3):
        acc = acc + w_ref[k:k + 1, :] * win[6 + k:6 + k + rows, :]
    return acc


def _ffn_gate_fwd(up, fw, fb, seq):
    _, t, f = up.shape
    tm = min(_FFN_ROWS, seq)
    tps = seq // tm
    tc = _FFN_COLS
    nc = f // tc
    hl = _FFN_HALO

    def body(up_ref, uph_ref, wg_ref, wv_ref, bg_ref, bv_ref, a_ref, buf):
        i = pl.program_id(1)
        keep = jnp.where(i % tps == 0, 0.0, 1.0)
        buf[:, 0:hl, :] = uph_ref[...].astype(f32) * keep
        buf[:, hl:hl + tm, :] = up_ref[...].astype(f32)

        def chunk(ci, carry):
            r0 = pl.multiple_of(ci * _CHUNK, _CHUNK)
            conv = []
            for g, (w_ref, b_ref) in enumerate(((wg_ref, bg_ref), (wv_ref, bv_ref))):
                conv.append(_conv3(b_ref, w_ref, _window(buf, g, r0 + hl - 8, _CHUNK), _CHUNK))
            gate, val = conv
            a_ref[pl.ds(r0, _CHUNK), :] = (gate * _sigmoid(gate) * val).astype(a_ref.dtype)
            return carry

        lax.fori_loop(0, tm // _CHUNK, chunk, 0)

    hb = tm // hl
    return pl.pallas_call(
        body, name="ffn_gate_fwd", grid=(nc, t // tm),
        in_specs=[pl.BlockSpec((2, tm, tc), lambda j, i: (0, i, j)),
                  pl.BlockSpec((2, hl, tc), lambda j, i: (0, jnp.maximum(i * hb - 1, 0), j)),
                  pl.BlockSpec((8, tc), lambda j, i: (0, j)), pl.BlockSpec((8, tc), lambda j, i: (0, nc + j)),
                  pl.BlockSpec((1, tc), lambda j, i: (0, j)), pl.BlockSpec((1, tc), lambda j, i: (0, nc + j))],
        out_specs=pl.BlockSpec((tm, tc), lambda j, i: (i, j)),
        out_shape=jax.ShapeDtypeStruct((t, f), _ACT),
        scratch_shapes=[pltpu.VMEM((2, hl + tm, tc), f32)], compiler_params=_params(2),
    )(up, up, fw, fw, fb, fb)


def _ffn_gate_bwd(up, da, fw, fb, seq):
    _, t, f = up.shape
    tm = min(_FFN_ROWS, seq)
    tps = seq // tm
    tc = _FFN_COLS
    nc = f // tc
    hl = _FFN_HALO

    def body(up_ref, uph_ref, upn_ref, da_ref, dan_ref, wg_ref, wv_ref, bg_ref, bv_ref,
             dup_ref, sg_ref, sv_ref, ubuf, dbuf, sums):
        i = pl.program_id(1)
        keep_prev = jnp.where(i % tps == 0, 0.0, 1.0)
        keep_next = jnp.where(i % tps == tps - 1, 0.0, 1.0)

        @pl.when(i == 0)
        def _():
            sg_ref[...] = jnp.zeros_like(sg_ref)
            sv_ref[...] = jnp.zeros_like(sv_ref)

        sums[...] = jnp.zeros_like(sums)
        ubuf[:, 0:hl, :] = uph_ref[...].astype(f32) * keep_prev
        ubuf[:, hl:hl + tm, :] = up_ref[...].astype(f32)
        ubuf[:, hl + tm:hl + tm + hl, :] = upn_ref[...].astype(f32) * keep_next
        w_refs = (wg_ref, wv_ref)
        b_refs = (bg_ref, bv_ref)

        def grads(r0, rows, dav, count):
            wins = [_window(ubuf, g, r0 + hl - 8, rows) for g in range(2)]
            gate, val = [_conv3(b_refs[g], w_refs[g], wins[g], rows) for g in range(2)]
            sg = _sigmoid(gate)
            douts = (dav * val * (sg * (1.0 + gate * (1.0 - sg))), dav * (gate * sg))
            for g in range(2):
                dbuf[g, pl.ds(r0, rows), :] = douts[g]
                if count:
                    sums[g, 0] += douts[g].reshape(rows // 8, 8, tc).sum(axis=0)
                    for k in range(3):
                        sums[g, 1 + k] += (douts[g] * wins[g][6 + k:6 + k + rows, :]).reshape(rows // 8, 8, tc).sum(axis=0)

        def first(ci, carry):
            r0 = pl.multiple_of(ci * _CHUNK, _CHUNK)
            grads(r0, _CHUNK, da_ref[pl.ds(r0, _CHUNK), :].astype(f32), True)
            return carry

        lax.fori_loop(0, tm // _CHUNK, first, 0)
        grads(tm, hl, dan_ref[...].astype(f32) * keep_next, False)

        def second(ci, carry):
            r0 = pl.multiple_of(ci * _CHUNK, _CHUNK)
            for g in range(2):
                win = _window(dbuf, g, r0, _CHUNK)
                acc = jnp.zeros((_CHUNK, tc), f32)
                for k in range(3):
                    acc = acc + w_refs[g][k:k + 1, :] * win[2 - k:2 - k + _CHUNK, :]
                dup_ref[g, pl.ds(r0, _CHUNK), :] = acc.astype(dup_ref.dtype)
            return carry

        lax.fori_loop(0, tm // _CHUNK, second, 0)
        for g, s_ref in enumerate((sg_ref, sv_ref)):
            for r in range(4):
                s_ref[r:r + 1, :] += jnp.sum(sums[g, r], axis=0, keepdims=True)

    hb = tm // hl
    n_halo = t // hl
    return pl.pallas_call(
        body, name="ffn_gate_bwd", grid=(nc, t // tm),
        in_specs=[pl.BlockSpec((2, tm, tc), lambda j, i: (0, i, j)),
                  pl.BlockSpec((2, hl, tc), lambda j, i: (0, jnp.maximum(i * hb - 1, 0), j)),
                  pl.BlockSpec((2, hl, tc), lambda j, i: (0, jnp.minimum((i + 1) * hb, n_halo - 1), j)),
                  pl.BlockSpec((tm, tc), lambda j, i: (i, j)),
                  pl.BlockSpec((hl, tc), lambda j, i: (jnp.minimum((i + 1) * hb, n_halo - 1), j)),
                  pl.BlockSpec((8, tc), lambda j, i: (0, j)), pl.BlockSpec((8, tc), lambda j, i: (0, nc + j)),
                  pl.BlockSpec((1, tc), lambda j, i: (0, j)), pl.BlockSpec((1, tc), lambda j, i: (0, nc + j))],
        out_specs=[pl.BlockSpec((2, tm, tc), lambda j, i: (0, i, j)),
                   pl.BlockSpec((8, tc), lambda j, i: (0, j)), pl.BlockSpec((8, tc), lambda j, i: (0, j))],
        out_shape=[jax.ShapeDtypeStruct((2, t, f), _ACT), jax.ShapeDtypeStruct((8, f), f32), jax.ShapeDtypeStruct((8, f), f32)],
        scratch_shapes=[pltpu.VMEM((2, hl + tm + hl, tc), f32), pltpu.VMEM((2, tm + hl, tc), f32),
                        pltpu.VMEM((2, 4, 8, tc), f32)],
        compiler_params=_params(2),
    )(up, up, up, da, da, fw, fw, fb, fb)


def _adamw_math(w, g, m, v):
    m = ADAM_B1 * m + (1.0 - ADAM_B1) * g
    v = ADAM_B2 * v + (1.0 - ADAM_B2) * (g * g)
    m_hat = m / (1.0 - ADAM_B1 ** ADAM_STEP)
    v_hat = v / (1.0 - ADAM_B2 ** ADAM_STEP)
    delta = -ADAM_LR * (m_hat / (jnp.sqrt(v_hat) + ADAM_EPS) + ADAM_WD * w)
    return delta, m, v


def _adamw_shard(name, w, g, m, v):
    _, r, c = w.shape
    tr = next((cand for cand in (256, 176, 128, 64, 32, 16, 8) if r % cand == 0), r)

    def body(w_ref, g_ref, m_ref, v_ref, d_ref, mo_ref, vo_ref):
        d, mn, vn = _adamw_math(w_ref[...], g_ref[...], m_ref[...], v_ref[...])
        d_ref[...] = d
        mo_ref[...] = mn
        vo_ref[...] = vn

    s3 = pl.BlockSpec((None, tr, c), lambda i: (0, i, 0))
    s2 = pl.BlockSpec((tr, c), lambda i: (i, 0))
    shp = jax.ShapeDtypeStruct(w.shape, f32)
    return pl.pallas_call(
        body, name=name, grid=(r // tr,), in_specs=[s3, s2, s3, s3], out_specs=[s3, s3, s3], out_shape=[shp, shp, shp],
        compiler_params=_params(1),
    )(w, g, m, v)


def _adamw_small(quads):
    n = len(quads)

    def body(*refs):
        ins, outs = refs[:4 * n], refs[4 * n:]
        for p in range(n):
            w_ref, g_ref, m_ref, v_ref = ins[4 * p:4 * p + 4]
            d, mn, vn = _adamw_math(w_ref[...], g_ref[...], m_ref[...], v_ref[...])
            outs[3 * p][...] = d
            outs[3 * p + 1][...] = mn
            outs[3 * p + 2][...] = vn

    flat = [a for q in quads for a in q]
    shapes = [jax.ShapeDtypeStruct(q[0].shape, f32) for q in quads for _ in range(3)]
    outs = pl.pallas_call(
        body, name="adamw_small", in_specs=[_VMEM] * (4 * n), out_specs=[_VMEM] * (3 * n), out_shape=shapes,
        compiler_params=pltpu.CompilerParams(vmem_limit_bytes=_VMEM_LIMIT_BYTES),
    )(*flat)
    return [tuple(outs[3 * p:3 * p + 3]) for p in range(n)]


def _sum_pairs(name, place, grads, got):
    _, r, c = grads.shape

    def body(place_ref, a_ref, b_ref, o_ref):
        o_ref[...] = (a_ref[...].astype(f32) + b_ref[...].astype(f32)).astype(o_ref.dtype)

    grid_spec = pltpu.PrefetchScalarGridSpec(
        num_scalar_prefetch=1, grid=(4,),
        in_specs=[pl.BlockSpec((None, r, c), lambda i, p: (2 * i + p[1], 0, 0)), pl.BlockSpec((None, r, c), lambda i, p: (i, 0, 0))],
        out_specs=pl.BlockSpec((None, r, c), lambda i, p: (i, 0, 0)))
    return pl.pallas_call(body, name=name, grid_spec=grid_spec, out_shape=jax.ShapeDtypeStruct((4, r, c), _ACT),
                          compiler_params=_params(1))(place, grads, got)


def _sum_four(name, place, sums, got):
    _, r, c = sums.shape

    def body(place_ref, o_ref, g_ref, f_ref):
        s = o_ref[...].astype(f32) + g_ref[0].astype(f32)
        s = s + g_ref[1].astype(f32)
        f_ref[...] = s + g_ref[2].astype(f32)

    grid_spec = pltpu.PrefetchScalarGridSpec(
        num_scalar_prefetch=1, grid=(1,),
        in_specs=[pl.BlockSpec((None, r, c), lambda i, p: (p[0], 0, 0)), pl.BlockSpec((3, r, c), lambda i, p: (0, 0, 0))],
        out_specs=pl.BlockSpec((None, r, c), lambda i, p: (p[1], 0, 0)))
    return pl.pallas_call(body, name=name, grid_spec=grid_spec, out_shape=jax.ShapeDtypeStruct((2, r, c), f32),
                          compiler_params=_params(1))(place, sums, got)


def _place():
    return lax.axis_index("x"), lax.axis_index("y"), lax.axis_index("c")


def _other_chips(x, y):
    return [(1 - x, y), (x, 1 - y), (1 - x, 1 - y)]


def _remote(src, dst, send_sem, recv_sem, to):
    return pltpu.make_async_remote_copy(src_ref=src, dst_ref=dst, send_sem=send_sem, recv_sem=recv_sem,
                                        device_id=to, device_id_type=_MESH)


def _place_shards(place, shards, col_sharded):
    n = len(shards)
    steps = 4

    def body(place_ref, *refs):
        for src, dst in zip(refs[:n], refs[n:]):
            dst[...] = src[...].astype(dst.dtype)

    in_specs, out_specs, out_shape = [], [], []
    for w, col in zip(shards, col_sharded):
        r, cs = w.shape
        tr = r // steps
        in_specs.append(pl.BlockSpec((tr, cs), lambda i, p: (i, 0)))
        if col:
            out_specs.append(pl.BlockSpec((tr, cs), lambda i, p: (i, p[0])))
            out_shape.append(jax.ShapeDtypeStruct((r, 4 * cs), _ACT))
        else:
            out_specs.append(pl.BlockSpec((tr, cs), lambda i, p: (p[0] * steps + i, 0)))
            out_shape.append(jax.ShapeDtypeStruct((4 * r, cs), _ACT))
    grid_spec = pltpu.PrefetchScalarGridSpec(num_scalar_prefetch=1, grid=(steps,), in_specs=in_specs, out_specs=out_specs)
    return pl.pallas_call(body, name="place_shards", grid_spec=grid_spec, out_shape=out_shape,
                          compiler_params=_params(1))(place, *shards)


def _shard_of(ref, col_sharded, s):
    rows, cols = ref.shape
    if col_sharded:
        return ref.at[:, pl.ds(s * (cols // 4), cols // 4)]
    return ref.at[pl.ds(s * (rows // 4), rows // 4), :]


def _allgather_start(bufs, col_sharded, groups):
    n = len(bufs)
    ng = len(groups)

    def body(*refs):
        out = refs[n:2 * n]
        sems = refs[2 * n:]
        x, y, c = _place()
        for g, members in enumerate(groups):
            for i, w in enumerate(members):
                mine = _shard_of(out[w], col_sharded[w], 2 * x + y)
                for j, chip in enumerate(_other_chips(x, y)):
                    _remote(mine, mine, sems[2 * g].at[3 * i + j], sems[2 * g + 1].at[3 * i + j], (*chip, c)).start()

    sem_shapes = [pltpu.SemaphoreType.DMA((3 * len(m),)) for m in groups for _ in range(2)]
    outs = pl.pallas_call(
        body, name="allgather_start", in_specs=[_HBM] * n, out_specs=[_HBM] * n + [_SEM] * (2 * ng),
        out_shape=[pltpu.HBM(b.shape, b.dtype) for b in bufs] + sem_shapes,
        input_output_aliases={i: i for i in range(n)},
        compiler_params=pltpu.CompilerParams(has_side_effects=_EFFECT),
    )(*[pltpu.with_memory_space_constraint(b, pltpu.HBM) for b in bufs])
    return list(outs[:n]), [(outs[n + 2 * g], outs[n + 2 * g + 1]) for g in range(ng)]


def _allgather_wait(name, bufs, col_sharded, sems, after):
    n = len(bufs)

    def body(*refs):
        buf = refs[:n]
        send, recv = refs[n], refs[n + 1]
        x, y, c = _place()
        for i in range(n):
            mine = _shard_of(buf[i], col_sharded[i], 2 * x + y)
            for j, chip in enumerate(_other_chips(x, y)):
                landed = _shard_of(buf[i], col_sharded[i], 2 * chip[0] + chip[1])
                cp = _remote(mine, landed, send.at[3 * i + j], recv.at[3 * i + j], (*chip, c))
                cp.wait_send()
                cp.wait_recv()

    return pl.pallas_call(
        body, name=name, in_specs=[_HBM] * n + [_SEM, _SEM, _ANY], out_specs=[_HBM] * n,
        out_shape=[pltpu.HBM(b.shape, b.dtype) for b in bufs],
        input_output_aliases={i: i for i in range(n)},
        compiler_params=pltpu.CompilerParams(has_side_effects=_EFFECT),
    )(*bufs, *sems, after)


def _exchange_pair_halves(name, grads):
    nw = len(grads)

    def body(*refs):
        src = refs[:nw]
        got = refs[nw:2 * nw]
        send_sem, recv_sem = refs[2 * nw:]
        x, y, c = _place()
        sends = []
        for w in range(nw):
            for s in range(4):
                rc = _remote(src[w].at[2 * s + 1 - c], got[w].at[s], send_sem.at[4 * w + s], recv_sem.at[4 * w + s], (x, y, 1 - c))
                rc.start()
                sends.append(rc)
        for rc in sends:
            rc.wait_recv()
        for rc in sends:
            rc.wait_send()

    return pl.pallas_call(
        body, name=name, in_specs=[_ANY] * nw, out_specs=[_ANY] * nw,
        out_shape=[jax.ShapeDtypeStruct((4,) + g.shape[1:], g.dtype) for g in grads],
        scratch_shapes=[pltpu.SemaphoreType.DMA((4 * nw,)), pltpu.SemaphoreType.DMA((4 * nw,))],
    )(*grads)


def _chip_exchange_start(name, sums):
    nw = len(sums)
    lands = [lax.empty((3,) + s.shape[1:], s.dtype) for s in sums]

    def body(*refs):
        src = refs[2 * nw:3 * nw]
        got = refs[3 * nw:4 * nw]
        send, recv, token = refs[4 * nw:]
        x, y, c = _place()
        for w in range(nw):
            for j, chip in enumerate(_other_chips(x, y)):
                _remote(src[w].at[2 * chip[0] + chip[1]], got[w].at[j], send.at[3 * w + j], recv.at[3 * w + j], (*chip, c)).start()
        token[...] = jnp.zeros_like(token)

    outs = pl.pallas_call(
        body, name=name, in_specs=[_HBM] * (2 * nw), out_specs=[_HBM] * (2 * nw) + [_SEM, _SEM, _VMEM],
        out_shape=[pltpu.HBM(a.shape, a.dtype) for a in list(sums) + lands]
        + [pltpu.SemaphoreType.DMA((3 * nw,)), pltpu.SemaphoreType.DMA((3 * nw,)), jax.ShapeDtypeStruct((8, 128), f32)],
        input_output_aliases={i: i for i in range(2 * nw)},
        compiler_params=pltpu.CompilerParams(has_side_effects=_EFFECT),
    )(*[pltpu.with_memory_space_constraint(a, pltpu.HBM) for a in list(sums) + lands])
    return list(outs[:nw]), list(outs[nw:2 * nw]), (outs[2 * nw], outs[2 * nw + 1]), outs[2 * nw + 2]


def _chip_exchange_wait(name, sums, got, sems, after):
    nw = len(sums)

    def body(*refs):
        src = refs[:nw]
        land = refs[nw:2 * nw]
        send, recv = refs[2 * nw], refs[2 * nw + 1]
        x, y, c = _place()
        for w in range(nw):
            for j, chip in enumerate(_other_chips(x, y)):
                cp = _remote(src[w].at[2 * chip[0] + chip[1]], land[w].at[j], send.at[3 * w + j], recv.at[3 * w + j], (*chip, c))
                cp.wait_send()
                cp.wait_recv()

    outs = pl.pallas_call(
        body, name=name, in_specs=[_HBM] * (2 * nw) + [_SEM, _SEM, _ANY], out_specs=[_HBM] * (2 * nw),
        out_shape=[pltpu.HBM(a.shape, a.dtype) for a in list(sums) + list(got)],
        input_output_aliases={i: i for i in range(2 * nw)},
        compiler_params=pltpu.CompilerParams(has_side_effects=_EFFECT),
    )(*sums, *got, *sems, after)
    return list(outs[:nw]), list(outs[nw:])


def _swap_halves(finals):
    nw = len(finals)

    def body(*refs):
        buf = refs[nw:2 * nw]
        send_sem, recv_sem = refs[2 * nw:]
        x, y, c = _place()
        sends = []
        for w in range(nw):
            rc = _remote(buf[w].at[c], buf[w].at[c], send_sem.at[w], recv_sem.at[w], (x, y, 1 - c))
            rc.start()
            sends.append(rc)
        for w in range(nw):
            _remote(buf[w].at[1 - c], buf[w].at[1 - c], send_sem.at[w], recv_sem.at[w], (x, y, c)).wait_recv()
        for rc in sends:
            rc.wait_send()

    return pl.pallas_call(
        body, name="rs_swap_halves", in_specs=[_ANY] * nw, out_specs=[_ANY] * nw,
        out_shape=[jax.ShapeDtypeStruct(g.shape, g.dtype) for g in finals],
        input_output_aliases={i: i for i in range(nw)},
        scratch_shapes=[pltpu.SemaphoreType.DMA((nw,)), pltpu.SemaphoreType.DMA((nw,))],
    )(*finals)


def _allreduce_small(parts):
    n = len(parts)

    def body(*refs):
        src = refs[:n]
        out = refs[n:2 * n]
        slots = refs[2 * n:3 * n]
        send_sem, recv_sem = refs[3 * n:]
        x, y, c = _place()
        me = 4 * x + 2 * y + c
        flips = [(bx, by, bc) for bx in (0, 1) for by in (0, 1) for bc in (0, 1)][1:]
        sends = []
        for a in range(n):
            slots[a][me] = src[a][...]
        for k, (bx, by, bc) in enumerate(flips):
            to = (1 - x if bx else x, 1 - y if by else y, 1 - c if bc else c)
            for a in range(n):
                rc = _remote(src[a], slots[a].at[me], send_sem.at[k, a], recv_sem.at[k, a], to)
                rc.start()
                sends.append(rc)
        for rc in sends:
            rc.wait_recv()
        for a in range(n):
            s = slots[a][0]
            for d in range(1, 8):
                s = s + slots[a][d]
            out[a][...] = s
        for rc in sends:
            rc.wait_send()

    return pl.pallas_call(
        body, name="allreduce_small", in_specs=[_VMEM] * n, out_specs=[_VMEM] * n,
        out_shape=[jax.ShapeDtypeStruct(p.shape, f32) for p in parts],
        scratch_shapes=[pltpu.VMEM((8,) + p.shape, f32) for p in parts] + [pltpu.SemaphoreType.DMA((7, n)), pltpu.SemaphoreType.DMA((7, n))],
        compiler_params=pltpu.CompilerParams(vmem_limit_bytes=_VMEM_LIMIT_BYTES),
    )(*parts)


def _local_step(x, mem, tgt, g_mix, g_xattn, g_mem, g_ffn, g_final, cb, lg, lb, pw, ps, fb, weights, reduce, n_seq, seq, n_mem):
    t, d = x.shape
    f = fb.shape[1] // 2
    c = cb.shape[1]
    h1 = _rms_fwd("norm_mix", x, g_mix)
    w_in, cw, fw = weights(0, h1)
    u = _mm_nn("proj_in", h1, w_in, _ACT, w_in.shape[1])
    y, hc = _mix_fwd(u, cw, cb, lg, lb, pw, ps, seq)
    w_out, w_q, w_kv, w_o = weights(1, y)
    x1 = _mm_nn("proj_out", y, w_out, f32, d, res=x)
    h2 = _rms_fwd("norm_xattn", x1, g_xattn)
    q = _mm_nn("proj_q", h2, w_q, _ACT, d)
    mem_n = _rms_fwd("norm_mem", mem, g_mem)
    kv = _mm_nn("proj_kv", mem_n, w_kv, _ACT, 2 * d)
    o = _attn_fwd(q, kv, n_seq, seq, n_mem)
    x2 = _mm_nn("proj_o", o, w_o, f32, d, res=x1)
    h3 = _rms_fwd("norm_ffn", x2, g_ffn)
    w_up, w_down = weights(2, h3)
    up = _mm_nn("proj_up", h3, w_up, _ACT, f, split_out=True)
    a = _ffn_gate_fwd(up, fw, fb, seq)
    x3 = _mm_nn("proj_down", a, w_down, f32, d, res=x2)
    dx3, dx3b, dg_final, loss = _final_loss_bwd(x3, g_final, tgt)
    da = _mm_nt("d_act", dx3b, w_down, _ACT)
    gw_down = _mm_tn_rows("dw_down", a, dx3b, f // 2, d // 2)
    dup, sums_g, sums_v = _ffn_gate_bwd(up, da, fw, fb, seq)
    dh3 = _mm_nt_halves("d_h3", dup, w_up, f32)
    gw_up = _mm_tn_pieces("dw_up", h3, dup, f // 2, t)
    token = reduce(0, [gw_down.reshape(8, -1, d), gw_up])
    dx2, dx2b, dg_ffn = _rms_bwd("norm_ffn_bwd", x2, g_ffn + token, dh3, dx3)
    do = _mm_nt("d_o", dx2b, w_o, _ACT)
    gw_o = _mm_tn_rows("dw_o", o, dx2b, d, d // 2)
    dq, dk, dv = _attn_bwd(q, kv, do, n_seq, seq, n_mem)
    dkv = jnp.concatenate([dk, dv], axis=1)
    dh2 = _mm_nt("d_h2", dq, w_q, f32)
    gw_q = _mm_tn_rows("dw_q", h2, dq, d, d // 2)
    gw_kv = _mm_tn_pieces("dw_kv", mem_n, dkv, d // 2, mem.shape[0])
    dmem_n = _mm_nt("d_mem_n", dkv, w_kv, f32)
    dg_mem = _rms_gain_grad("norm_mem_bwd", mem, dmem_n)
    token = reduce(1, [gw_o.reshape(8, -1, d), gw_q.reshape(8, -1, d), gw_kv])
    dx1, dx1b, dg_xattn = _rms_bwd("norm_xattn_bwd", x1, g_xattn + token, dh2, dx2)
    dy = _mm_nt("d_y", dx1b, w_out, _ACT)
    gw_out = _mm_tn_rows("dw_out", y, dx1b, d, d // 2)
    dhc, sums_norm = _mix_bwd_norm(hc, dy, lg, lb, seq)
    du, d_cw, d_ps, d_pw = _mix_bwd_taps(u, dhc, dy, cw, pw, ps, seq)
    dh1 = _mm_nt("d_h1", du, w_in, f32)
    gw_in = _mm_tn_pieces("dw_in", h1, du, c * 3 // 4, t)
    reduce(2, [gw_out.reshape(8, -1, d), gw_in])
    grad_x, _, dg_mix = _rms_bwd("norm_mix_bwd", x, g_mix, dh1, dx1)
    zero_row = jnp.zeros((1, d), f32)
    gains = jnp.concatenate([dg_mix, dg_xattn, dg_mem, dg_ffn, dg_final, jnp.pad(loss, ((0, 0), (0, d - 1))), zero_row, zero_row], axis=0)
    conv_rows = jnp.concatenate([sums_norm[2:3], sums_norm[0:1], sums_norm[1:2], d_ps[0:1], jnp.zeros((4, c), f32)], axis=0)
    ffn_rows = jnp.concatenate([sums_g, sums_v], axis=1)
    small = [gains, conv_rows, d_pw.reshape(-1, d_pw.shape[-1]), ffn_rows, d_cw]
    return grad_x, small


def kernel(x, mem, norm_mix_g, w_in, conv_dw_w, conv_dw_b, conv_ln_g, conv_ln_b, pool_w, pool_scale, w_out, norm_xattn_g, norm_mem_g, w_q, w_kv, w_o, norm_ffn_g, w_up, ffn_dw_w, ffn_dw_b, w_down, norm_final_g, loss_target, m_norm_mix_g, m_w_in, m_conv_dw_w, m_conv_dw_b, m_conv_ln_g, m_conv_ln_b, m_pool_w, m_pool_scale, m_w_out, m_norm_xattn_g, m_norm_mem_g, m_w_q, m_w_kv, m_w_o, m_norm_ffn_g, m_w_up, m_ffn_dw_w, m_ffn_dw_b, m_w_down, m_norm_final_g, v_norm_mix_g, v_w_in, v_conv_dw_w, v_conv_dw_b, v_conv_ln_g, v_conv_ln_b, v_pool_w, v_pool_scale, v_w_out, v_norm_xattn_g, v_norm_mem_g, v_w_q, v_w_kv, v_w_o, v_norm_ffn_g, v_w_up, v_ffn_dw_w, v_ffn_dw_b, v_w_down, v_norm_final_g):
    n_seq, seq, d = x.shape
    n_mem = mem.shape[1]
    chip = 2 * lax.axis_index("x") + lax.axis_index("y")

    place = jnp.stack([chip, lax.axis_index("c")]).astype(jnp.int32)

    col_w = [w_in, w_kv, w_up]
    row_w = [w_out, w_q, w_o, w_down]
    col_flags = [True] * 3 + [False] * 4 + [True] * 2
    kw = conv_dw_w.shape[1]

    def padded_in_place(shard, rows):
        full = jnp.zeros((rows, 4 * shard.shape[1]), shard.dtype)
        return lax.dynamic_update_slice(full, shard, (0, chip * shard.shape[1]))

    bufs = list(_place_shards(place, [w[0] for w in col_w + row_w], col_flags[:7]))
    bufs += [padded_in_place(conv_dw_w[0], _HALO), padded_in_place(ffn_dw_w[0], 8)]
    groups = [[0, 7, 8], [3, 4, 1, 5], [2, 6]]
    bufs, sems = _allgather_start(bufs, col_flags, groups)

    def weights(g, after):
        members = groups[g]
        return _allgather_wait("allgather_wait_%d" % g, [bufs[i] for i in members], [col_flags[i] for i in members], sems[g], after)

    names = ["w_in", "w_kv", "w_up", "w_out", "w_q", "w_o", "w_down"]
    reduce_groups = [["w_down", "w_up"], ["w_o", "w_q", "w_kv"], ["w_out", "w_in"]]
    in_flight = {}

    def reduce(g, grads):
        members = reduce_groups[g]
        got = _exchange_pair_halves("rs_pair_exchange_%d" % g, grads)
        sums = [_sum_pairs("rs_pair_sum_" + n, place, a, b) for n, a, b in zip(members, grads, got)]
        sums, lands, rs_sems, token = _chip_exchange_start("rs_chip_start_%d" % g, sums)
        in_flight[g] = (sums, lands, rs_sems)
        return token[0:1, 0:1]

    grad_x, small = _local_step(
        x.reshape(n_seq * seq, d), mem.reshape(n_seq * n_mem, d), loss_target.reshape(n_seq * seq, d),
        norm_mix_g, norm_xattn_g, norm_mem_g, norm_ffn_g, norm_final_g.reshape(1, d),
        conv_dw_b, conv_ln_g, conv_ln_b, pool_w[0], pool_scale, ffn_dw_b, weights, reduce, n_seq, seq, n_mem)

    finals = {}
    for g, members in enumerate(reduce_groups):
        sums, lands, rs_sems = in_flight[g]
        sums, lands = _chip_exchange_wait("rs_chip_wait_%d" % g, sums, lands, rs_sems, grad_x)
        for n, a, b in zip(members, sums, lands):
            finals[n] = _sum_four("rs_chip_sum_" + n, place, a, b)
    shard_grads = _swap_halves([finals[n] for n in names])

    gains, conv_rows, d_pw, ffn_rows, d_cw = _allreduce_small(small)
    loss = gains[5, 0]

    outs = {}
    big_w = dict(zip(names, col_w + row_w))
    big_m = dict(w_in=m_w_in, w_kv=m_w_kv, w_up=m_w_up, w_out=m_w_out, w_q=m_w_q, w_o=m_w_o, w_down=m_w_down)
    big_v = dict(w_in=v_w_in, w_kv=v_w_kv, w_up=v_w_up, w_out=v_w_out, w_q=v_w_q, w_o=v_w_o, w_down=v_w_down)
    for n, g in zip(names, shard_grads):
        w = big_w[n]
        g2 = g.reshape(w.shape[1], w.shape[2])
        delta, new_m, new_v = _adamw_shard("adamw_" + n, w, g2, big_m[n], big_v[n])
        outs[n] = (g2.reshape(w.shape), delta, new_m, new_v)

    f2 = ffn_dw_b.shape[1]
    cs_c = conv_dw_w.shape[2]
    cs_f = ffn_dw_w.shape[2]
    g_cw = lax.dynamic_slice(d_cw, (0, chip * cs_c), (kw, cs_c)).reshape(conv_dw_w.shape)
    g_fw = lax.dynamic_slice(ffn_rows, (1, chip * cs_f), (ffn_dw_w.shape[1], cs_f)).reshape(ffn_dw_w.shape)
    small_params = [
        ("norm_mix_g", norm_mix_g, gains[0:1], m_norm_mix_g, v_norm_mix_g),
        ("conv_dw_w", conv_dw_w, g_cw, m_conv_dw_w, v_conv_dw_w),
        ("conv_dw_b", conv_dw_b, conv_rows[0:1], m_conv_dw_b, v_conv_dw_b),
        ("conv_ln_g", conv_ln_g, conv_rows[1:2], m_conv_ln_g, v_conv_ln_g),
        ("conv_ln_b", conv_ln_b, conv_rows[2:3], m_conv_ln_b, v_conv_ln_b),
        ("pool_w", pool_w, d_pw.reshape(pool_w.shape), m_pool_w, v_pool_w),
        ("pool_scale", pool_scale, conv_rows[3:4], m_pool_scale, v_pool_scale),
        ("norm_xattn_g", norm_xattn_g, gains[1:2], m_norm_xattn_g, v_norm_xattn_g),
        ("norm_mem_g", norm_mem_g, gains[2:3], m_norm_mem_g, v_norm_mem_g),
        ("norm_ffn_g", norm_ffn_g, gains[3:4], m_norm_ffn_g, v_norm_ffn_g),
        ("ffn_dw_w", ffn_dw_w, g_fw, m_ffn_dw_w, v_ffn_dw_w),
        ("ffn_dw_b", ffn_dw_b, ffn_rows[0:1, :f2], m_ffn_dw_b, v_ffn_dw_b),
        ("norm_final_g", norm_final_g.reshape(1, d), gains[4:5], m_norm_final_g.reshape(1, d), v_norm_final_g.reshape(1, d)),
    ]
    quads = []
    for _, w, g, m, v in small_params:
        shape2 = (-1, w.shape[-1])
        quads.append((w.reshape(shape2), g.reshape(shape2), m.reshape(shape2), v.reshape(shape2)))
    for (n, w, g, _, _), (delta, new_m, new_v) in zip(small_params, _adamw_small(quads)):
        shape = norm_final_g.shape if n == "norm_final_g" else w.shape
        outs[n] = (g.reshape(shape), delta.reshape(shape), new_m.reshape(shape), new_v.reshape(shape))

    order = ["norm_mix_g", "w_in", "conv_dw_w", "conv_dw_b", "conv_ln_g", "conv_ln_b", "pool_w", "pool_scale", "w_out",
             "norm_xattn_g", "norm_mem_g", "w_q", "w_kv", "w_o", "norm_ffn_g", "w_up", "ffn_dw_w", "ffn_dw_b", "w_down",
             "norm_final_g"]
    return (loss, grad_x.reshape(x.shape), *[outs[n][0] for n in order], *[outs[n][1] for n in order],
            *[outs[n][2] for n in order], *[outs[n][3] for n in order])
```

```python
import functools

import jax
import jax.numpy as jnp
from jax import lax
from jax.experimental import pallas as pl
from jax.experimental.pallas import tpu as pltpu

f32 = jnp.float32
_ACT = jnp.bfloat16

EPS = 1e-6
POOL_WINDOWS = (2, 4, 8, 16)
XATTN_HEADS = 4
ADAM_LR = 0.001
ADAM_B1 = 0.9
ADAM_B2 = 0.999
ADAM_EPS = 1e-08
ADAM_WD = 0.01
ADAM_STEP = 10

_VMEM_LIMIT_BYTES = 56 * 1024 * 1024
_MESH = pl.DeviceIdType.MESH
_ANY = pl.BlockSpec(memory_space=pl.ANY)
_VMEM = pl.BlockSpec(memory_space=pltpu.VMEM)
_HBM = pl.BlockSpec(memory_space=pltpu.HBM)
_SEM = pl.BlockSpec(memory_space=pltpu.SEMAPHORE)
_EFFECT = pltpu.SideEffectType.DATAFLOW_SIDE_EFFECTING

_NN = (((1,), (0,)), ((), ()))
_NT = (((1,), (1,)), ((), ()))
_TN = (((0,), (0,)), ((), ()))


def _params(n_grid):
    return pltpu.CompilerParams(dimension_semantics=("arbitrary",) * n_grid, vmem_limit_bytes=_VMEM_LIMIT_BYTES)


def _sigmoid(v):
    return 1.0 / (1.0 + jnp.exp(-v))


def _dot(a, b, dims):
    return lax.dot_general(a, b, dims, preferred_element_type=f32)


def _mm(name, a, b, *, dims, grid, a_spec, b_spec, o_spec, out_shape, nk, acc_shape=None, res=None, res_spec=None):
    def body(*refs):
        if res is None:
            a_ref, b_ref, o_ref, *scratch = refs
            r_ref = None
        else:
            a_ref, b_ref, r_ref, o_ref, *scratch = refs
        p = _dot(a_ref[...], b_ref[...], dims)

        def finish(v):
            if r_ref is not None:
                v = v + r_ref[...]
            o_ref[...] = v.astype(o_ref.dtype)

        if nk == 1:
            finish(p)
        else:
            acc = scratch[0]
            k = pl.program_id(2)

            @pl.when(k == 0)
            def _():
                acc[...] = p

            @pl.when(k > 0)
            def _():
                acc[...] += p

            @pl.when(k == nk - 1)
            def _():
                finish(acc[...])

    ins = [a, b] + ([] if res is None else [res])
    specs = [a_spec, b_spec] + ([] if res is None else [res_spec])
    return pl.pallas_call(
        body, name=name, grid=grid, in_specs=specs, out_specs=o_spec, out_shape=out_shape,
        scratch_shapes=[pltpu.VMEM(acc_shape, f32)] if nk > 1 else [], compiler_params=_params(3),
    )(*ins)


def _row_tile(m):
    return min(512, m)


def _mm_nn(name, a, b, out_dtype, tn, res=None, split_out=False):
    m, k = a.shape
    n = b.shape[1]
    tm = _row_tile(m)
    if split_out:
        out_shape = jax.ShapeDtypeStruct((n // tn, m, tn), out_dtype)
        o_spec = pl.BlockSpec((None, tm, tn), lambda j, i, kk: (j, i, 0))
    else:
        out_shape = jax.ShapeDtypeStruct((m, n), out_dtype)
        o_spec = pl.BlockSpec((tm, tn), lambda j, i, kk: (i, j))
    return _mm(
        name, a, b, dims=_NN, grid=(n // tn, m // tm, 1), nk=1,
        a_spec=pl.BlockSpec((tm, k), lambda j, i, kk: (i, 0)),
        b_spec=pl.BlockSpec((k, tn), lambda j, i, kk: (0, j)),
        o_spec=o_spec, out_shape=out_shape, res=res,
        res_spec=pl.BlockSpec((tm, tn), lambda j, i, kk: (i, j)),
    )


def _mm_nt(name, a, b, out_dtype):
    n, kc = b.shape
    m = a.shape[0]
    tm = _row_tile(m)
    return _mm(
        name, a, b, dims=_NT, grid=(m // tm, 1, 1), nk=1,
        a_spec=pl.BlockSpec((tm, kc), lambda i, j, k: (i, 0)), b_spec=pl.BlockSpec((n, kc), lambda i, j, k: (0, 0)),
        o_spec=pl.BlockSpec((tm, n), lambda i, j, k: (i, 0)),
        out_shape=jax.ShapeDtypeStruct((m, n), out_dtype),
    )


def _mm_tn_rows(name, a, b, tka, tn):
    m, ka = a.shape
    nb = b.shape[1]
    return _mm(
        name, a, b, dims=_TN, grid=(ka // tka, nb // tn, 1), nk=1,
        a_spec=pl.BlockSpec((m, tka), lambda i, j, k: (0, i)),
        b_spec=pl.BlockSpec((m, tn), lambda i, j, k: (0, j)),
        o_spec=pl.BlockSpec((tka, tn), lambda i, j, k: (i, j)),
        out_shape=jax.ShapeDtypeStruct((ka, nb), _ACT),
    )


def _mm_tn_pieces(name, a, b, cs, tt):
    m, ka = a.shape
    nk = m // tt
    if b.ndim == 3:
        b_spec = pl.BlockSpec((None, tt, cs), lambda i, j, k: (j // 2, k, j % 2))
    else:
        b_spec = pl.BlockSpec((tt, cs), lambda i, j, k: (k, j))
    return _mm(
        name, a, b, dims=_TN, grid=(2, 4, nk), nk=nk, acc_shape=(ka // 2, cs),
        a_spec=pl.BlockSpec((tt, ka // 2), lambda i, j, k: (k, i)), b_spec=b_spec,
        o_spec=pl.BlockSpec((None, ka // 2, cs), lambda i, j, k: (2 * j + i, 0, 0)),
        out_shape=jax.ShapeDtypeStruct((8, ka // 2, cs), _ACT),
    )


def _rms_fwd(name, x, g):
    t, d = x.shape
    tm = _row_tile(t)

    def body(x_ref, g_ref, h_ref):
        xv = x_ref[...]
        r = lax.rsqrt(jnp.mean(xv * xv, axis=-1, keepdims=True) + EPS)
        h_ref[...] = (xv * r * g_ref[...]).astype(h_ref.dtype)

    return pl.pallas_call(
        body, name=name, grid=(t // tm,),
        in_specs=[pl.BlockSpec((tm, d), lambda i: (i, 0)), pl.BlockSpec((1, d), lambda i: (0, 0))],
        out_specs=pl.BlockSpec((tm, d), lambda i: (i, 0)), out_shape=jax.ShapeDtypeStruct((t, d), _ACT),
        compiler_params=_params(1),
    )(x, g)


def _fused_rows(name, a, b, product, a_spec, tm, extras, extra_specs, out_shape, out_specs, epilogue):
    ne = len(extras)

    def body(a_ref, b_ref, *refs):
        epilogue(product(a_ref, b_ref), refs[:ne], refs[ne:])

    m = extras[0].shape[0]
    return pl.pallas_call(
        body, name=name, grid=(m // tm,),
        in_specs=[a_spec, pl.BlockSpec(b.shape, lambda i: (0, 0)), *extra_specs], out_specs=out_specs, out_shape=out_shape,
        compiler_params=_params(1),
    )(a, b, *extras)


def _proj_residual_norm(name, a, b, res, g):
    m, k = a.shape
    d = b.shape[1]
    tm = _row_tile(m)

    def epilogue(p, ins, outs):
        xv = p + ins[0][...]
        outs[0][...] = xv
        r = lax.rsqrt(jnp.mean(xv * xv, axis=-1, keepdims=True) + EPS)
        outs[1][...] = (xv * r * ins[1][...]).astype(outs[1].dtype)

    row = pl.BlockSpec((tm, d), lambda i: (i, 0))
    return _fused_rows(
        name, a, b, lambda a_ref, b_ref: _dot(a_ref[...], b_ref[...], _NN), pl.BlockSpec((tm, k), lambda i: (i, 0)), tm,
        [res, g], [row, pl.BlockSpec((1, d), lambda i: (0, 0))],
        [jax.ShapeDtypeStruct((m, d), f32), jax.ShapeDtypeStruct((m, d), _ACT)], [row, row], epilogue)


def _dproj_rms_bwd(name, a, b, x, g, dres, storage_copy=True):
    m, d = x.shape
    if a.ndim == 3:
        nh, _, kh = a.shape
        tm = min(256, m)
        a_spec = pl.BlockSpec((nh, tm, kh), lambda i: (0, i, 0))

        def product(a_ref, b_ref):
            p = _dot(a_ref[0], b_ref[:, 0:kh], _NT)
            for h in range(1, nh):
                p = p + _dot(a_ref[h], b_ref[:, h * kh:(h + 1) * kh], _NT)
            return p
    else:
        tm = _row_tile(m)
        a_spec = pl.BlockSpec((tm, a.shape[1]), lambda i: (i, 0))

        def product(a_ref, b_ref):
            return _dot(a_ref[...], b_ref[...], _NT)

    def epilogue(dhv, ins, outs):
        x_ref, g_ref, dres_ref = ins
        dg_ref = outs[-1]

        @pl.when(pl.program_id(0) == 0)
        def _():
            dg_ref[...] = jnp.zeros_like(dg_ref)

        xv = x_ref[...]
        r = lax.rsqrt(jnp.mean(xv * xv, axis=-1, keepdims=True) + EPS)
        xn = xv * r
        dxn = dhv * g_ref[...]
        dx = r * (dxn - xn * jnp.mean(dxn * xn, axis=-1, keepdims=True)) + dres_ref[...]
        outs[0][...] = dx
        if storage_copy:
            outs[1][...] = dx.astype(outs[1].dtype)
        dg_ref[...] += jnp.sum(dhv * xn, axis=0, keepdims=True)

    row = pl.BlockSpec((tm, d), lambda i: (i, 0))
    vec = pl.BlockSpec((1, d), lambda i: (0, 0))
    copies = [jax.ShapeDtypeStruct((m, d), _ACT)] if storage_copy else []
    return _fused_rows(
        name, a, b, product, a_spec, tm, [x, g, dres], [row, vec, row],
        [jax.ShapeDtypeStruct((m, d), f32)] + copies + [jax.ShapeDtypeStruct((1, d), f32)],
        [row] * (1 + len(copies)) + [vec], epilogue)


def _proj_loss_bwd(name, a, b, res, g, tgt):
    m, k = a.shape
    d = b.shape[1]
    tm = _row_tile(m)

    def epilogue(p, ins, outs):
        res_ref, g_ref, t_ref = ins
        dx_ref, dxb_ref, dg_ref, loss_ref = outs

        @pl.when(pl.program_id(0) == 0)
        def _():
            dg_ref[...] = jnp.zeros_like(dg_ref)
            loss_ref[...] = jnp.zeros_like(loss_ref)

        xv = p + res_ref[...]
        gv = g_ref[...]
        r = lax.rsqrt(jnp.mean(xv * xv, axis=-1, keepdims=True) + EPS)
        xn = xv * r
        err = xn * gv - t_ref[...]
        loss_ref[...] += 0.5 * jnp.sum(jnp.mean(err * err, axis=-1, keepdims=True), axis=0, keepdims=True)
        dout = err * (1.0 / d)
        dxn = dout * gv
        dx = r * (dxn - xn * jnp.mean(dxn * xn, axis=-1, keepdims=True))
        dx_ref[...] = dx
        dxb_ref[...] = dx.astype(dxb_ref.dtype)
        dg_ref[...] += jnp.sum(dout * xn, axis=0, keepdims=True)

    row = pl.BlockSpec((tm, d), lambda i: (i, 0))
    vec = pl.BlockSpec((1, d), lambda i: (0, 0))
    return _fused_rows(
        name, a, b, lambda a_ref, b_ref: _dot(a_ref[...], b_ref[...], _NN), pl.BlockSpec((tm, k), lambda i: (i, 0)), tm,
        [res, g, tgt], [row, vec, row],
        [jax.ShapeDtypeStruct((m, d), f32), jax.ShapeDtypeStruct((m, d), _ACT), jax.ShapeDtypeStruct((1, d), f32),
         jax.ShapeDtypeStruct((1, 1), f32)],
        [row, row, vec, pl.BlockSpec((1, 1), lambda i: (0, 0))], epilogue)


def _rms_gain_grad(name, x, dh):
    t, d = x.shape
    tm = _row_tile(t)

    def body(x_ref, dh_ref, dg_ref):
        @pl.when(pl.program_id(0) == 0)
        def _():
            dg_ref[...] = jnp.zeros_like(dg_ref)

        xv = x_ref[...]
        r = lax.rsqrt(jnp.mean(xv * xv, axis=-1, keepdims=True) + EPS)
        dg_ref[...] += jnp.sum(dh_ref[...] * (xv * r), axis=0, keepdims=True)

    row = pl.BlockSpec((tm, d), lambda i: (i, 0))
    return pl.pallas_call(
        body, name=name, grid=(t // tm,), in_specs=[row, row], out_specs=pl.BlockSpec((1, d), lambda i: (0, 0)),
        out_shape=jax.ShapeDtypeStruct((1, d), f32), compiler_params=_params(1),
    )(x, dh)


_CONV_ROWS = 256
_CHUNK = 64
_HALO = 32


def _pool_counts(pos, w):
    return jnp.minimum(pos + 1.0, float(w))


def _mix_fwd(u, cw, cb, lg, lb, pw, ps, seq):
    t, c3 = u.shape
    c = c3 // 3
    kw = 31
    tm = min(_CONV_ROWS, seq)
    tps = seq // tm
    gd = c // len(POOL_WINDOWS)

    def body(u_ref, uh_ref, cw_ref, cb_ref, lg_ref, lb_ref, pw_ref, ps_ref, y_ref, hc_ref, hgbuf, pbuf):
        i = pl.program_id(0)
        keep = jnp.where(i % tps == 0, 0.0, 1.0)
        um = u_ref[...].astype(f32)
        uh = uh_ref[...].astype(f32) * keep
        hgbuf[0:_HALO, :] = uh[:, 0:c] * _sigmoid(uh[:, c:2 * c])
        hgbuf[_HALO:_HALO + tm, :] = um[:, 0:c] * _sigmoid(um[:, c:2 * c])
        pbuf[0:_HALO, :] = uh[:, 2 * c:]
        pbuf[_HALO:_HALO + tm, :] = um[:, 2 * c:]
        for r0 in range(0, tm, _CHUNK):
            acc = jnp.broadcast_to(cb_ref[...], (_CHUNK, c))
            for k in range(kw):
                off = r0 + _HALO - (kw - 1) + k
                acc = acc + cw_ref[k:k + 1, :] * hgbuf[off:off + _CHUNK, :]
            hc_ref[r0:r0 + _CHUNK, :] = acc
            mu = jnp.mean(acc, axis=-1, keepdims=True)
            xc = acc - mu
            var = jnp.mean(xc * xc, axis=-1, keepdims=True)
            hl = xc * lax.rsqrt(var + EPS) * lg_ref[...] + lb_ref[...]
            y_ref[r0:r0 + _CHUNK, 0:c] = (hl * _sigmoid(hl)).astype(y_ref.dtype)
        pos = ((i % tps) * tm).astype(f32) + lax.broadcasted_iota(jnp.int32, (tm, 1), 0).astype(f32)
        for gi, w in enumerate(POOL_WINDOWS):
            sl = slice(gi * gd, (gi + 1) * gd)
            v = pbuf[_HALO:_HALO + tm, sl]
            s = v
            for j in range(1, w):
                s = s + pbuf[_HALO - j:_HALO - j + tm, sl]
            pooled = s / _pool_counts(pos, w) - v
            mixed = _dot(pooled.astype(_ACT), pw_ref[gi].astype(_ACT), _NN)
            y_ref[:, c + gi * gd:c + (gi + 1) * gd] = (mixed * ps_ref[:, sl]).astype(y_ref.dtype)

    hb = tm // _HALO
    full = lambda shape: pl.BlockSpec(shape, lambda i: (0,) * len(shape))
    return pl.pallas_call(
        body, name="mix_fwd", grid=(t // tm,),
        in_specs=[pl.BlockSpec((tm, c3), lambda i: (i, 0)),
                  pl.BlockSpec((_HALO, c3), lambda i: (jnp.maximum(i * hb - 1, 0), 0)),
                  full((_HALO, c)), full((1, c)), full((1, c)), full((1, c)), full((len(POOL_WINDOWS), gd, gd)), full((1, c))],
        out_specs=[pl.BlockSpec((tm, 2 * c), lambda i: (i, 0)), pl.BlockSpec((tm, c), lambda i: (i, 0))],
        out_shape=[jax.ShapeDtypeStruct((t, 2 * c), _ACT), jax.ShapeDtypeStruct((t, c), f32)],
        scratch_shapes=[pltpu.VMEM((_HALO + tm, c), f32), pltpu.VMEM((_HALO + tm, c), f32)],
        compiler_params=_params(1),
    )(u, u, cw, cb, lg, lb, pw, ps)


def _mix_bwd_norm(hc, dy, lg, lb, seq):
    t, c = hc.shape
    tm = min(_CONV_ROWS, seq)

    def body(hc_ref, dy_ref, lg_ref, lb_ref, dhc_ref, sums_ref):
        @pl.when(pl.program_id(0) == 0)
        def _():
            sums_ref[...] = jnp.zeros_like(sums_ref)

        hcv = hc_ref[...]
        mu = jnp.mean(hcv, axis=-1, keepdims=True)
        xc = hcv - mu
        rstd = lax.rsqrt(jnp.mean(xc * xc, axis=-1, keepdims=True) + EPS)
        n = xc * rstd
        hl = n * lg_ref[...] + lb_ref[...]
        sg = _sigmoid(hl)
        dhl = dy_ref[...].astype(f32) * (sg * (1.0 + hl * (1.0 - sg)))
        dn = dhl * lg_ref[...]
        dhc = rstd * (dn - jnp.mean(dn, axis=-1, keepdims=True) - n * jnp.mean(dn * n, axis=-1, keepdims=True))
        dhc_ref[...] = dhc
        sums_ref[0:1, :] += jnp.sum(dhl * n, axis=0, keepdims=True)
        sums_ref[1:2, :] += jnp.sum(dhl, axis=0, keepdims=True)
        sums_ref[2:3, :] += jnp.sum(dhc, axis=0, keepdims=True)

    row = pl.BlockSpec((tm, c), lambda i: (i, 0))
    vec = pl.BlockSpec((1, c), lambda i: (0, 0))
    return pl.pallas_call(
        body, name="mix_bwd_norm", grid=(t // tm,), in_specs=[row, row, vec, vec],
        out_specs=[row, pl.BlockSpec((8, c), lambda i: (0, 0))],
        out_shape=[jax.ShapeDtypeStruct((t, c), f32), jax.ShapeDtypeStruct((8, c), f32)],
        compiler_params=_params(1),
    )(hc, dy, lg, lb)


def _mix_bwd_taps(u, dhc, dy, cw, pw, ps, seq):
    t, c3 = u.shape
    c = c3 // 3
    kw = 31
    tm = min(_CONV_ROWS, seq)
    tps = seq // tm
    ng = len(POOL_WINDOWS)
    gd = c // ng
    nh = 16

    def body(u_ref, uh_ref, dhc_ref, dhcn_ref, dy_ref, dyn_ref, cw_ref, pw_ref, ps_ref,
             du_ref, dcw_ref, dps_ref, dpw_ref, hgbuf, dcbuf, pbuf, dpbuf):
        i = pl.program_id(0)
        keep_prev = jnp.where(i % tps == 0, 0.0, 1.0)
        keep_next = jnp.where(i % tps == tps - 1, 0.0, 1.0)

        @pl.when(i == 0)
        def _():
            dcw_ref[...] = jnp.zeros_like(dcw_ref)
            dps_ref[...] = jnp.zeros_like(dps_ref)
            dpw_ref[...] = jnp.zeros_like(dpw_ref)

        uh = uh_ref[...].astype(f32) * keep_prev
        hgbuf[0:_HALO, :] = uh[:, 0:c] * _sigmoid(uh[:, c:2 * c])
        pbuf[0:_HALO, :] = uh[:, 2 * c:]
        um = u_ref[...].astype(f32)
        hgbuf[_HALO:_HALO + tm, :] = um[:, 0:c] * _sigmoid(um[:, c:2 * c])
        pbuf[_HALO:_HALO + tm, :] = um[:, 2 * c:]
        dcbuf[0:tm, :] = dhc_ref[...]
        dcbuf[tm:tm + _HALO, :] = dhcn_ref[...] * keep_next
        tap_sums = [None] * kw
        for r0 in range(0, tm, _CHUNK):
            dh = dcbuf[r0:r0 + _CHUNK, :]
            acc = jnp.zeros((_CHUNK, c), f32)
            for k in range(kw):
                off = r0 + _HALO - (kw - 1) + k
                part = jnp.sum(dh * hgbuf[off:off + _CHUNK, :], axis=0, keepdims=True)
                tap_sums[k] = part if tap_sums[k] is None else tap_sums[k] + part
                fwd = r0 + (kw - 1) - k
                acc = acc + cw_ref[k:k + 1, :] * dcbuf[fwd:fwd + _CHUNK, :]
            val = u_ref[r0:r0 + _CHUNK, 0:c].astype(f32)
            sg = _sigmoid(u_ref[r0:r0 + _CHUNK, c:2 * c].astype(f32))
            du_ref[r0:r0 + _CHUNK, 0:c] = (acc * sg).astype(du_ref.dtype)
            du_ref[r0:r0 + _CHUNK, c:2 * c] = (acc * val * sg * (1.0 - sg)).astype(du_ref.dtype)
        for k in range(kw):
            dcw_ref[k:k + 1, :] += tap_sums[k]
        base = ((i % tps) * tm).astype(f32)
        pos = base + lax.broadcasted_iota(jnp.int32, (tm, 1), 0).astype(f32)
        pos_next = base + float(tm) + lax.broadcasted_iota(jnp.int32, (nh, 1), 0).astype(f32)
        for gi, w in enumerate(POOL_WINDOWS):
            sl = slice(gi * gd, (gi + 1) * gd)
            v = pbuf[_HALO:_HALO + tm, sl]
            s = v
            for j in range(1, w):
                s = s + pbuf[_HALO - j:_HALO - j + tm, sl]
            cnt = _pool_counts(pos, w)
            pooled = (s / cnt - v).astype(_ACT)
            pwg = pw_ref[gi].astype(_ACT)
            mixed = _dot(pooled, pwg, _NN)
            dyp = dy_ref[:, sl].astype(f32)
            dps_ref[0:1, sl] += jnp.sum(dyp * mixed, axis=0, keepdims=True)
            dmix = (dyp * ps_ref[:, sl]).astype(_ACT)
            dpw_ref[gi] += _dot(pooled, dmix, _TN)
            dmix_next = (dyn_ref[:, sl].astype(f32) * ps_ref[:, sl] * keep_next).astype(_ACT)
            dpool = _dot(dmix, pwg, _NT)
            dpbuf[0:tm, sl] = dpool / cnt
            dpbuf[tm:tm + nh, sl] = _dot(dmix_next, pwg, _NT) / _pool_counts(pos_next, w)
            acc = -dpool
            for j in range(w):
                acc = acc + dpbuf[j:j + tm, sl]
            du_ref[:, 2 * c + gi * gd:2 * c + (gi + 1) * gd] = acc.astype(du_ref.dtype)

    hb = tm // _HALO
    n_halo = t // _HALO
    n_nh = t // nh
    full = lambda shape: pl.BlockSpec(shape, lambda i: (0,) * len(shape))
    return pl.pallas_call(
        body, name="mix_bwd_taps", grid=(t // tm,),
        in_specs=[pl.BlockSpec((tm, c3), lambda i: (i, 0)),
                  pl.BlockSpec((_HALO, c3), lambda i: (jnp.maximum(i * hb - 1, 0), 0)),
                  pl.BlockSpec((tm, c), lambda i: (i, 0)),
                  pl.BlockSpec((_HALO, c), lambda i: (jnp.minimum((i + 1) * hb, n_halo - 1), 0)),
                  pl.BlockSpec((tm, c), lambda i: (i, 1)),
                  pl.BlockSpec((nh, c), lambda i: (jnp.minimum((i + 1) * (tm // nh), n_nh - 1), 1)),
                  full((_HALO, c)), full((ng, gd, gd)), full((1, c))],
        out_specs=[pl.BlockSpec((tm, c3), lambda i: (i, 0)), full((_HALO, c)), full((8, c)), full((ng, gd, gd))],
        out_shape=[jax.ShapeDtypeStruct((t, c3), _ACT), jax.ShapeDtypeStruct((_HALO, c), f32),
                   jax.ShapeDtypeStruct((8, c), f32), jax.ShapeDtypeStruct((ng, gd, gd), f32)],
        scratch_shapes=[pltpu.VMEM((_HALO + tm, c), f32), pltpu.VMEM((tm + _HALO, c), f32),
                        pltpu.VMEM((_HALO + tm, c), f32), pltpu.VMEM((tm + nh, c), f32)],
        compiler_params=_params(1),
    )(u, u, dhc, dhc, dy, dy, cw, pw, ps)


def _attn_fwd(q, kv, n_seq, seq, n_mem):
    t, d = q.shape
    dh = d // XATTN_HEADS
    tq = min(512, seq)
    nq = seq // tq
    scale = dh ** -0.5

    def body(q_ref, k_ref, v_ref, o_ref):
        s = _dot(q_ref[...], k_ref[...], _NT) * scale
        e = jnp.exp(s - jnp.max(s, axis=-1, keepdims=True))
        p = e / jnp.sum(e, axis=-1, keepdims=True)
        o_ref[...] = _dot(p.astype(_ACT), v_ref[...], _NN).astype(o_ref.dtype)

    qs = pl.BlockSpec((tq, dh), lambda b, h, i: (b * nq + i, h))
    return pl.pallas_call(
        body, name="attn_fwd", grid=(n_seq, XATTN_HEADS, nq),
        in_specs=[qs, pl.BlockSpec((n_mem, dh), lambda b, h, i: (b, h)),
                  pl.BlockSpec((n_mem, dh), lambda b, h, i: (b, XATTN_HEADS + h))],
        out_specs=qs, out_shape=jax.ShapeDtypeStruct((t, d), _ACT), compiler_params=_params(3),
    )(q, kv, kv)


def _attn_bwd(q, kv, do, n_seq, seq, n_mem):
    t, d = q.shape
    dh = d // XATTN_HEADS
    tq = min(512, seq)
    nq = seq // tq
    scale = dh ** -0.5

    def body(q_ref, k_ref, v_ref, do_ref, dq_ref, dk_ref, dv_ref, dk_acc, dv_acc):
        i = pl.program_id(2)
        qv = q_ref[...]
        kvv = k_ref[...]
        dov = do_ref[...]
        s = _dot(qv, kvv, _NT) * scale
        e = jnp.exp(s - jnp.max(s, axis=-1, keepdims=True))
        p = e / jnp.sum(e, axis=-1, keepdims=True)
        dp = _dot(dov, v_ref[...], _NT)
        ds = (p * (dp - jnp.sum(dp * p, axis=-1, keepdims=True)) * scale).astype(_ACT)
        dq_ref[...] = _dot(ds, kvv, _NN).astype(dq_ref.dtype)
        dk_part = _dot(ds, qv, _TN)
        dv_part = _dot(p.astype(_ACT), dov, _TN)

        @pl.when(i == 0)
        def _():
            dk_acc[...] = dk_part
            dv_acc[...] = dv_part

        @pl.when(i > 0)
        def _():
            dk_acc[...] += dk_part
            dv_acc[...] += dv_part

        @pl.when(i == nq - 1)
        def _():
            dk_ref[...] = dk_acc[...].astype(dk_ref.dtype)
            dv_ref[...] = dv_acc[...].astype(dv_ref.dtype)

    qs = pl.BlockSpec((tq, dh), lambda b, h, i: (b * nq + i, h))
    ms = pl.BlockSpec((n_mem, dh), lambda b, h, i: (b, h))
    return pl.pallas_call(
        body, name="attn_bwd", grid=(n_seq, XATTN_HEADS, nq),
        in_specs=[qs, ms, pl.BlockSpec((n_mem, dh), lambda b, h, i: (b, XATTN_HEADS + h)), qs],
        out_specs=[qs, ms, ms],
        out_shape=[jax.ShapeDtypeStruct((t, d), _ACT), jax.ShapeDtypeStruct((n_seq * n_mem, d), _ACT),
                   jax.ShapeDtypeStruct((n_seq * n_mem, d), _ACT)],
        scratch_shapes=[pltpu.VMEM((n_mem, dh), f32), pltpu.VMEM((n_mem, dh), f32)],
        compiler_params=_params(3),
    )(q, kv, kv, do)


_FFN_ROWS = 1024
_FFN_COLS = 256
_FFN_HALO = 16


def _window(buf, g, start, rows):
    return buf[g, pl.ds(start, rows + 8), :]


def _conv3(b_ref, w_ref, win, rows):
    acc = jnp.broadcast_to(b_ref[...], (rows, win.shape[1]))
    for k in range(3):
        acc = acc + w_ref[k:k + 1, :] * win[6 + k:6 + k + rows, :]
    return acc


def _ffn_gate_fwd(up, fw, fb, seq):
    _, t, f = up.shape
    tm = min(_FFN_ROWS, seq)
    tps = seq // tm
    tc = _FFN_COLS
    nc = f // tc
    hl = _FFN_HALO

    def body(up_ref, uph_ref, wg_ref, wv_ref, bg_ref, bv_ref, a_ref, buf):
        i = pl.program_id(1)
        keep = jnp.where(i % tps == 0, 0.0, 1.0)
        buf[:, 0:hl, :] = uph_ref[...].astype(f32) * keep
        buf[:, hl:hl + tm, :] = up_ref[...].astype(f32)

        def chunk(ci, carry):
            r0 = pl.multiple_of(ci * _CHUNK, _CHUNK)
            conv = []
            for g, (w_ref, b_ref) in enumerate(((wg_ref, bg_ref), (wv_ref, bv_ref))):
                conv.append(_conv3(b_ref, w_ref, _window(buf, g, r0 + hl - 8, _CHUNK), _CHUNK))
            gate, val = conv
            a_ref[pl.ds(r0, _CHUNK), :] = (gate * _sigmoid(gate) * val).astype(a_ref.dtype)
            return carry

        lax.fori_loop(0, tm // _CHUNK, chunk, 0)

    hb = tm // hl
    return pl.pallas_call(
        body, name="ffn_gate_fwd", grid=(nc, t // tm),
        in_specs=[pl.BlockSpec((2, tm, tc), lambda j, i: (0, i, j)),
                  pl.BlockSpec((2, hl, tc), lambda j, i: (0, jnp.maximum(i * hb - 1, 0), j)),
                  pl.BlockSpec((8, tc), lambda j, i: (0, j)), pl.BlockSpec((8, tc), lambda j, i: (0, nc + j)),
                  pl.BlockSpec((1, tc), lambda j, i: (0, j)), pl.BlockSpec((1, tc), lambda j, i: (0, nc + j))],
        out_specs=pl.BlockSpec((tm, tc), lambda j, i: (i, j)),
        out_shape=jax.ShapeDtypeStruct((t, f), _ACT),
        scratch_shapes=[pltpu.VMEM((2, hl + tm, tc), f32)], compiler_params=_params(2),
    )(up, up, fw, fw, fb, fb)


def _ffn_gate_bwd(up, da, fw, fb, seq):
    _, t, f = up.shape
    tm = min(_FFN_ROWS, seq)
    tps = seq // tm
    tc = _FFN_COLS
    nc = f // tc
    hl = _FFN_HALO

    def body(up_ref, uph_ref, upn_ref, da_ref, dan_ref, wg_ref, wv_ref, bg_ref, bv_ref,
             dup_ref, sg_ref, sv_ref, ubuf, dbuf, sums):
        i = pl.program_id(1)
        keep_prev = jnp.where(i % tps == 0, 0.0, 1.0)
        keep_next = jnp.where(i % tps == tps - 1, 0.0, 1.0)

        @pl.when(i == 0)
        def _():
            sg_ref[...] = jnp.zeros_like(sg_ref)
            sv_ref[...] = jnp.zeros_like(sv_ref)

        sums[...] = jnp.zeros_like(sums)
        ubuf[:, 0:hl, :] = uph_ref[...].astype(f32) * keep_prev
        ubuf[:, hl:hl + tm, :] = up_ref[...].astype(f32)
        ubuf[:, hl + tm:hl + tm + hl, :] = upn_ref[...].astype(f32) * keep_next
        w_refs = (wg_ref, wv_ref)
        b_refs = (bg_ref, bv_ref)

        def grads(r0, rows, dav, count):
            wins = [_window(ubuf, g, r0 + hl - 8, rows) for g in range(2)]
            gate, val = [_conv3(b_refs[g], w_refs[g], wins[g], rows) for g in range(2)]
            sg = _sigmoid(gate)
            douts = (dav * val * (sg * (1.0 + gate * (1.0 - sg))), dav * (gate * sg))
            for g in range(2):
                dbuf[g, pl.ds(r0, rows), :] = douts[g]
                if count:
                    sums[g, 0] += douts[g].reshape(rows // 8, 8, tc).sum(axis=0)
                    for k in range(3):
                        sums[g, 1 + k] += (douts[g] * wins[g][6 + k:6 + k + rows, :]).reshape(rows // 8, 8, tc).sum(axis=0)

        def first(ci, carry):
            r0 = pl.multiple_of(ci * _CHUNK, _CHUNK)
            grads(r0, _CHUNK, da_ref[pl.ds(r0, _CHUNK), :].astype(f32), True)
            return carry

        lax.fori_loop(0, tm // _CHUNK, first, 0)
        grads(tm, hl, dan_ref[...].astype(f32) * keep_next, False)

        def second(ci, carry):
            r0 = pl.multiple_of(ci * _CHUNK, _CHUNK)
            for g in range(2):
                win = _window(dbuf, g, r0, _CHUNK)
                acc = jnp.zeros((_CHUNK, tc), f32)
                for k in range(3):
                    acc = acc + w_refs[g][k:k + 1, :] * win[2 - k:2 - k + _CHUNK, :]
                dup_ref[g, pl.ds(r0, _CHUNK), :] = acc.astype(dup_ref.dtype)
            return carry

        lax.fori_loop(0, tm // _CHUNK, second, 0)
        for g, s_ref in enumerate((sg_ref, sv_ref)):
            for r in range(4):
                s_ref[r:r + 1, :] += jnp.sum(sums[g, r], axis=0, keepdims=True)

    hb = tm // hl
    n_halo = t // hl
    return pl.pallas_call(
        body, name="ffn_gate_bwd", grid=(nc, t // tm),
        in_specs=[pl.BlockSpec((2, tm, tc), lambda j, i: (0, i, j)),
                  pl.BlockSpec((2, hl, tc), lambda j, i: (0, jnp.maximum(i * hb - 1, 0), j)),
                  pl.BlockSpec((2, hl, tc), lambda j, i: (0, jnp.minimum((i + 1) * hb, n_halo - 1), j)),
                  pl.BlockSpec((tm, tc), lambda j, i: (i, j)),
                  pl.BlockSpec((hl, tc), lambda j, i: (jnp.minimum((i + 1) * hb, n_halo - 1), j)),
                  pl.BlockSpec((8, tc), lambda j, i: (0, j)), pl.BlockSpec((8, tc), lambda j, i: (0, nc + j)),
                  pl.BlockSpec((1, tc), lambda j, i: (0, j)), pl.BlockSpec((1, tc), lambda j, i: (0, nc + j))],
        out_specs=[pl.BlockSpec((2, tm, tc), lambda j, i: (0, i, j)),
                   pl.BlockSpec((8, tc), lambda j, i: (0, j)), pl.BlockSpec((8, tc), lambda j, i: (0, j))],
        out_shape=[jax.ShapeDtypeStruct((2, t, f), _ACT), jax.ShapeDtypeStruct((8, f), f32), jax.ShapeDtypeStruct((8, f), f32)],
        scratch_shapes=[pltpu.VMEM((2, hl + tm + hl, tc), f32), pltpu.VMEM((2, tm + hl, tc), f32),
                        pltpu.VMEM((2, 4, 8, tc), f32)],
        compiler_params=_params(2),
    )(up, up, up, da, da, fw, fw, fb, fb)


def _adamw_math(w, g, m, v):
    m = ADAM_B1 * m + (1.0 - ADAM_B1) * g
    v = ADAM_B2 * v + (1.0 - ADAM_B2) * (g * g)
    m_hat = m / (1.0 - ADAM_B1 ** ADAM_STEP)
    v_hat = v / (1.0 - ADAM_B2 ** ADAM_STEP)
    delta = -ADAM_LR * (m_hat / (jnp.sqrt(v_hat) + ADAM_EPS) + ADAM_WD * w)
    return delta, m, v


def _adamw_shard(name, w, g, m, v):
    _, r, c = w.shape
    tr = next((cand for cand in (256, 176, 128, 64, 32, 16, 8) if r % cand == 0), r)

    def body(w_ref, g_ref, m_ref, v_ref, d_ref, mo_ref, vo_ref):
        d, mn, vn = _adamw_math(w_ref[...], g_ref[...], m_ref[...], v_ref[...])
        d_ref[...] = d
        mo_ref[...] = mn
        vo_ref[...] = vn

    s3 = pl.BlockSpec((None, tr, c), lambda i: (0, i, 0))
    s2 = pl.BlockSpec((tr, c), lambda i: (i, 0))
    shp = jax.ShapeDtypeStruct(w.shape, f32)
    return pl.pallas_call(
        body, name=name, grid=(r // tr,), in_specs=[s3, s2, s3, s3], out_specs=[s3, s3, s3], out_shape=[shp, shp, shp],
        compiler_params=_params(1),
    )(w, g, m, v)


def _adamw_small(quads):
    n = len(quads)

    def body(*refs):
        ins, outs = refs[:4 * n], refs[4 * n:]
        for p in range(n):
            w_ref, g_ref, m_ref, v_ref = ins[4 * p:4 * p + 4]
            d, mn, vn = _adamw_math(w_ref[...], g_ref[...], m_ref[...], v_ref[...])
            outs[3 * p][...] = d
            outs[3 * p + 1][...] = mn
            outs[3 * p + 2][...] = vn

    flat = [a for q in quads for a in q]
    shapes = [jax.ShapeDtypeStruct(q[0].shape, f32) for q in quads for _ in range(3)]
    outs = pl.pallas_call(
        body, name="adamw_small", in_specs=[_VMEM] * (4 * n), out_specs=[_VMEM] * (3 * n), out_shape=shapes,
        compiler_params=pltpu.CompilerParams(vmem_limit_bytes=_VMEM_LIMIT_BYTES),
    )(*flat)
    return [tuple(outs[3 * p:3 * p + 3]) for p in range(n)]


def _sum_pairs(name, place, grads, got):
    _, r, c = grads.shape

    def body(place_ref, a_ref, b_ref, o_ref):
        o_ref[...] = (a_ref[...].astype(f32) + b_ref[...].astype(f32)).astype(o_ref.dtype)

    grid_spec = pltpu.PrefetchScalarGridSpec(
        num_scalar_prefetch=1, grid=(4,),
        in_specs=[pl.BlockSpec((None, r, c), lambda i, p: (2 * i + p[1], 0, 0)), pl.BlockSpec((None, r, c), lambda i, p: (i, 0, 0))],
        out_specs=pl.BlockSpec((None, r, c), lambda i, p: (i, 0, 0)))
    return pl.pallas_call(body, name=name, grid_spec=grid_spec, out_shape=jax.ShapeDtypeStruct((4, r, c), _ACT),
                          compiler_params=_params(1))(place, grads, got)


def _sum_four(name, place, sums, got):
    _, r, c = sums.shape

    def body(place_ref, o_ref, g_ref, f_ref):
        s = o_ref[...].astype(f32) + g_ref[0].astype(f32)
        s = s + g_ref[1].astype(f32)
        f_ref[...] = s + g_ref[2].astype(f32)

    grid_spec = pltpu.PrefetchScalarGridSpec(
        num_scalar_prefetch=1, grid=(1,),
        in_specs=[pl.BlockSpec((None, r, c), lambda i, p: (p[0], 0, 0)), pl.BlockSpec((3, r, c), lambda i, p: (0, 0, 0))],
        out_specs=pl.BlockSpec((None, r, c), lambda i, p: (p[1], 0, 0)))
    return pl.pallas_call(body, name=name, grid_spec=grid_spec, out_shape=jax.ShapeDtypeStruct((2, r, c), f32),
                          compiler_params=_params(1))(place, sums, got)


def _place():
    return lax.axis_index("x"), lax.axis_index("y"), lax.axis_index("c")


def _other_chips(x, y):
    return [(1 - x, y), (x, 1 - y), (1 - x, 1 - y)]


def _remote(src, dst, send_sem, recv_sem, to):
    return pltpu.make_async_remote_copy(src_ref=src, dst_ref=dst, send_sem=send_sem, recv_sem=recv_sem,
                                        device_id=to, device_id_type=_MESH)


def _place_shards(place, shards, col_sharded):
    n = len(shards)
    steps = 4

    def body(place_ref, *refs):
        for src, dst in zip(refs[:n], refs[n:]):
            dst[...] = src[...].astype(dst.dtype)

    in_specs, out_specs, out_shape = [], [], []
    for w, col in zip(shards, col_sharded):
        r, cs = w.shape
        tr = r // steps
        in_specs.append(pl.BlockSpec((tr, cs), lambda i, p: (i, 0)))
        if col:
            out_specs.append(pl.BlockSpec((tr, cs), lambda i, p: (i, p[0])))
            out_shape.append(jax.ShapeDtypeStruct((r, 4 * cs), _ACT))
        else:
            out_specs.append(pl.BlockSpec((tr, cs), lambda i, p: (p[0] * steps + i, 0)))
            out_shape.append(jax.ShapeDtypeStruct((4 * r, cs), _ACT))
    grid_spec = pltpu.PrefetchScalarGridSpec(num_scalar_prefetch=1, grid=(steps,), in_specs=in_specs, out_specs=out_specs)
    return pl.pallas_call(body, name="place_shards", grid_spec=grid_spec, out_shape=out_shape,
                          compiler_params=_params(1))(place, *shards)


def _shard_of(ref, col_sharded, s):
    rows, cols = ref.shape
    if col_sharded:
        return ref.at[:, pl.ds(s * (cols // 4), cols // 4)]
    return ref.at[pl.ds(s * (rows // 4), rows // 4), :]


def _allgather_start(bufs, col_sharded, groups):
    n = len(bufs)
    ng = len(groups)

    def body(*refs):
        out = refs[n:2 * n]
        sems = refs[2 * n:]
        x, y, c = _place()
        for g, members in enumerate(groups):
            for i, w in enumerate(members):
                mine = _shard_of(out[w], col_sharded[w], 2 * x + y)
                for j, chip in enumerate(_other_chips(x, y)):
                    _remote(mine, mine, sems[2 * g].at[3 * i + j], sems[2 * g + 1].at[3 * i + j], (*chip, c)).start()

    sem_shapes = [pltpu.SemaphoreType.DMA((3 * len(m),)) for m in groups for _ in range(2)]
    outs = pl.pallas_call(
        body, name="allgather_start", in_specs=[_HBM] * n, out_specs=[_HBM] * n + [_SEM] * (2 * ng),
        out_shape=[pltpu.HBM(b.shape, b.dtype) for b in bufs] + sem_shapes,
        input_output_aliases={i: i for i in range(n)},
        compiler_params=pltpu.CompilerParams(has_side_effects=_EFFECT),
    )(*[pltpu.with_memory_space_constraint(b, pltpu.HBM) for b in bufs])
    return list(outs[:n]), [(outs[n + 2 * g], outs[n + 2 * g + 1]) for g in range(ng)]


def _allgather_wait(name, bufs, col_sharded, sems, after):
    n = len(bufs)

    def body(*refs):
        buf = refs[:n]
        send, recv = refs[n], refs[n + 1]
        x, y, c = _place()
        for i in range(n):
            mine = _shard_of(buf[i], col_sharded[i], 2 * x + y)
            for j, chip in enumerate(_other_chips(x, y)):
                landed = _shard_of(buf[i], col_sharded[i], 2 * chip[0] + chip[1])
                cp = _remote(mine, landed, send.at[3 * i + j], recv.at[3 * i + j], (*chip, c))
                cp.wait_send()
                cp.wait_recv()

    return pl.pallas_call(
        body, name=name, in_specs=[_HBM] * n + [_SEM, _SEM, _ANY], out_specs=[_HBM] * n,
        out_shape=[pltpu.HBM(b.shape, b.dtype) for b in bufs],
        input_output_aliases={i: i for i in range(n)},
        compiler_params=pltpu.CompilerParams(has_side_effects=_EFFECT),
    )(*bufs, *sems, after)


def _exchange_pair_halves(name, grads):
    nw = len(grads)

    def body(*refs):
        src = refs[:nw]
        got = refs[nw:2 * nw]
        send_sem, recv_sem = refs[2 * nw:]
        x, y, c = _place()
        sends = []
        for w in range(nw):
            for s in range(4):
                rc = _remote(src[w].at[2 * s + 1 - c], got[w].at[s], send_sem.at[4 * w + s], recv_sem.at[4 * w + s], (x, y, 1 - c))
                rc.start()
                sends.append(rc)
        for rc in sends:
            rc.wait_recv()
        for rc in sends:
            rc.wait_send()

    return pl.pallas_call(
        body, name=name, in_specs=[_ANY] * nw, out_specs=[_ANY] * nw,
        out_shape=[jax.ShapeDtypeStruct((4,) + g.shape[1:], g.dtype) for g in grads],
        scratch_shapes=[pltpu.SemaphoreType.DMA((4 * nw,)), pltpu.SemaphoreType.DMA((4 * nw,))],
    )(*grads)


def _chip_exchange_start(name, sums):
    nw = len(sums)
    lands = [lax.empty((3,) + s.shape[1:], s.dtype) for s in sums]

    def body(*refs):
        src = refs[2 * nw:3 * nw]
        got = refs[3 * nw:4 * nw]
        send, recv, token = refs[4 * nw:]
        x, y, c = _place()
        for w in range(nw):
            for j, chip in enumerate(_other_chips(x, y)):
                _remote(src[w].at[2 * chip[0] + chip[1]], got[w].at[j], send.at[3 * w + j], recv.at[3 * w + j], (*chip, c)).start()
        token[...] = jnp.zeros_like(token)

    outs = pl.pallas_call(
        body, name=name, in_specs=[_HBM] * (2 * nw), out_specs=[_HBM] * (2 * nw) + [_SEM, _SEM, _VMEM],
        out_shape=[pltpu.HBM(a.shape, a.dtype) for a in list(sums) + lands]
        + [pltpu.SemaphoreType.DMA((3 * nw,)), pltpu.SemaphoreType.DMA((3 * nw,)), jax.ShapeDtypeStruct((8, 128), f32)],
        input_output_aliases={i: i for i in range(2 * nw)},
        compiler_params=pltpu.CompilerParams(has_side_effects=_EFFECT),
    )(*[pltpu.with_memory_space_constraint(a, pltpu.HBM) for a in list(sums) + lands])
    return list(outs[:nw]), list(outs[nw:2 * nw]), (outs[2 * nw], outs[2 * nw + 1]), outs[2 * nw + 2]


def _chip_exchange_wait(name, sums, got, sems, after):
    nw = len(sums)

    def body(*refs):
        src = refs[:nw]
        land = refs[nw:2 * nw]
        send, recv = refs[2 * nw], refs[2 * nw + 1]
        x, y, c = _place()
        for w in range(nw):
            for j, chip in enumerate(_other_chips(x, y)):
                cp = _remote(src[w].at[2 * chip[0] + chip[1]], land[w].at[j], send.at[3 * w + j], recv.at[3 * w + j], (*chip, c))
                cp.wait_send()
                cp.wait_recv()

    outs = pl.pallas_call(
        body, name=name, in_specs=[_HBM] * (2 * nw) + [_SEM, _SEM, _ANY], out_specs=[_HBM] * (2 * nw),
        out_shape=[pltpu.HBM(a.shape, a.dtype) for a in list(sums) + list(got)],
        input_output_aliases={i: i for i in range(2 * nw)},
        compiler_params=pltpu.CompilerParams(has_side_effects=_EFFECT),
    )(*sums, *got, *sems, after)
    return list(outs[:nw]), list(outs[nw:])


def _swap_halves(finals):
    nw = len(finals)

    def body(*refs):
        buf = refs[nw:2 * nw]
        send_sem, recv_sem = refs[2 * nw:]
        x, y, c = _place()
        sends = []
        for w in range(nw):
            rc = _remote(buf[w].at[c], buf[w].at[c], send_sem.at[w], recv_sem.at[w], (x, y, 1 - c))
            rc.start()
            sends.append(rc)
        for w in range(nw):
            _remote(buf[w].at[1 - c], buf[w].at[1 - c], send_sem.at[w], recv_sem.at[w], (x, y, c)).wait_recv()
        for rc in sends:
            rc.wait_send()

    return pl.pallas_call(
        body, name="rs_swap_halves", in_specs=[_ANY] * nw, out_specs=[_ANY] * nw,
        out_shape=[jax.ShapeDtypeStruct(g.shape, g.dtype) for g in finals],
        input_output_aliases={i: i for i in range(nw)},
        scratch_shapes=[pltpu.SemaphoreType.DMA((nw,)), pltpu.SemaphoreType.DMA((nw,))],
    )(*finals)


def _allreduce_small(parts):
    n = len(parts)

    def body(*refs):
        src = refs[:n]
        out = refs[n:2 * n]
        slots = refs[2 * n:3 * n]
        send_sem, recv_sem = refs[3 * n:]
        x, y, c = _place()
        me = 4 * x + 2 * y + c
        flips = [(bx, by, bc) for bx in (0, 1) for by in (0, 1) for bc in (0, 1)][1:]
        sends = []
        for a in range(n):
            slots[a][me] = src[a][...]
        for k, (bx, by, bc) in enumerate(flips):
            to = (1 - x if bx else x, 1 - y if by else y, 1 - c if bc else c)
            for a in range(n):
                rc = _remote(src[a], slots[a].at[me], send_sem.at[k, a], recv_sem.at[k, a], to)
                rc.start()
                sends.append(rc)
        for rc in sends:
            rc.wait_recv()
        for a in range(n):
            s = slots[a][0]
            for d in range(1, 8):
                s = s + slots[a][d]
            out[a][...] = s
        for rc in sends:
            rc.wait_send()

    return pl.pallas_call(
        body, name="allreduce_small", in_specs=[_VMEM] * n, out_specs=[_VMEM] * n,
        out_shape=[jax.ShapeDtypeStruct(p.shape, f32) for p in parts],
        scratch_shapes=[pltpu.VMEM((8,) + p.shape, f32) for p in parts] + [pltpu.SemaphoreType.DMA((7, n)), pltpu.SemaphoreType.DMA((7, n))],
        compiler_params=pltpu.CompilerParams(vmem_limit_bytes=_VMEM_LIMIT_BYTES),
    )(*parts)


def _local_step(x, mem, tgt, g_mix, g_xattn, g_mem, g_ffn, g_final, cb, lg, lb, pw, ps, fb, weights, reduce, n_seq, seq, n_mem):
    t, d = x.shape
    f = fb.shape[1] // 2
    c = cb.shape[1]
    h1 = _rms_fwd("norm_mix", x, g_mix)
    w_in, cw, fw = weights(0, h1)
    u = _mm_nn("proj_in", h1, w_in, _ACT, w_in.shape[1])
    y, hc = _mix_fwd(u, cw, cb, lg, lb, pw, ps, seq)
    w_out, w_q, w_kv, w_o = weights(1, y)
    x1, h2 = _proj_residual_norm("proj_out", y, w_out, x, g_xattn)
    q = _mm_nn("proj_q", h2, w_q, _ACT, d)
    mem_n = _rms_fwd("norm_mem", mem, g_mem)
    kv = _mm_nn("proj_kv", mem_n, w_kv, _ACT, 2 * d)
    o = _attn_fwd(q, kv, n_seq, seq, n_mem)
    x2, h3 = _proj_residual_norm("proj_o", o, w_o, x1, g_ffn)
    w_up, w_down = weights(2, h3)
    up = _mm_nn("proj_up", h3, w_up, _ACT, f, split_out=True)
    a = _ffn_gate_fwd(up, fw, fb, seq)
    dx3, dx3b, dg_final, loss = _proj_loss_bwd("proj_down", a, w_down, x2, g_final, tgt)
    da = _mm_nt("d_act", dx3b, w_down, _ACT)
    gw_down = _mm_tn_rows("dw_down", a, dx3b, f // 2, d // 2)
    dup, sums_g, sums_v = _ffn_gate_bwd(up, da, fw, fb, seq)
    gw_up = _mm_tn_pieces("dw_up", h3, dup, f // 2, t)
    token = reduce(0, [gw_down.reshape(8, -1, d), gw_up])
    dx2, dx2b, dg_ffn = _dproj_rms_bwd("d_h3", dup, w_up, x2, g_ffn + token, dx3)
    do = _mm_nt("d_o", dx2b, w_o, _ACT)
    gw_o = _mm_tn_rows("dw_o", o, dx2b, d, d // 2)
    dq, dk, dv = _attn_bwd(q, kv, do, n_seq, seq, n_mem)
    dkv = jnp.concatenate([dk, dv], axis=1)
    gw_q = _mm_tn_rows("dw_q", h2, dq, d, d // 2)
    gw_kv = _mm_tn_pieces("dw_kv", mem_n, dkv, d // 2, mem.shape[0])
    dmem_n = _mm_nt("d_mem_n", dkv, w_kv, f32)
    dg_mem = _rms_gain_grad("norm_mem_bwd", mem, dmem_n)
    dx1, dx1b, dg_xattn = _dproj_rms_bwd("d_h2", dq, w_q, x1, g_xattn, dx2)
    dy = _mm_nt("d_y", dx1b, w_out, _ACT)
    gw_out = _mm_tn_rows("dw_out", y, dx1b, d, d // 2)
    token = reduce(1, [gw_o.reshape(8, -1, d), gw_q.reshape(8, -1, d), gw_kv, gw_out.reshape(8, -1, d)])
    dhc, sums_norm = _mix_bwd_norm(hc, dy, lg + token, lb, seq)
    du, d_cw, d_ps, d_pw = _mix_bwd_taps(u, dhc, dy, cw, pw, ps, seq)
    gw_in = _mm_tn_pieces("dw_in", h1, du, c * 3 // 4, t)
    reduce(2, [gw_in])
    grad_x, dg_mix = _dproj_rms_bwd("d_h1", du, w_in, x, g_mix, dx1, storage_copy=False)
    zero_row = jnp.zeros((1, d), f32)
    gains = jnp.concatenate([dg_mix, dg_xattn, dg_mem, dg_ffn, dg_final, jnp.pad(loss, ((0, 0), (0, d - 1))), zero_row, zero_row], axis=0)
    conv_rows = jnp.concatenate([sums_norm[2:3], sums_norm[0:1], sums_norm[1:2], d_ps[0:1], jnp.zeros((4, c), f32)], axis=0)
    ffn_rows = jnp.concatenate([sums_g, sums_v], axis=1)
    small = [gains, conv_rows, d_pw.reshape(-1, d_pw.shape[-1]), ffn_rows, d_cw]
    return grad_x, small


def kernel(x, mem, norm_mix_g, w_in, conv_dw_w, conv_dw_b, conv_ln_g, conv_ln_b, pool_w, pool_scale, w_out, norm_xattn_g, norm_mem_g, w_q, w_kv, w_o, norm_ffn_g, w_up, ffn_dw_w, ffn_dw_b, w_down, norm_final_g, loss_target, m_norm_mix_g, m_w_in, m_conv_dw_w, m_conv_dw_b, m_conv_ln_g, m_conv_ln_b, m_pool_w, m_pool_scale, m_w_out, m_norm_xattn_g, m_norm_mem_g, m_w_q, m_w_kv, m_w_o, m_norm_ffn_g, m_w_up, m_ffn_dw_w, m_ffn_dw_b, m_w_down, m_norm_final_g, v_norm_mix_g, v_w_in, v_conv_dw_w, v_conv_dw_b, v_conv_ln_g, v_conv_ln_b, v_pool_w, v_pool_scale, v_w_out, v_norm_xattn_g, v_norm_mem_g, v_w_q, v_w_kv, v_w_o, v_norm_ffn_g, v_w_up, v_ffn_dw_w, v_ffn_dw_b, v_w_down, v_norm_final_g):
    n_seq, seq, d = x.shape
    n_mem = mem.shape[1]
    chip = 2 * lax.axis_index("x") + lax.axis_index("y")

    place = jnp.stack([chip, lax.axis_index("c")]).astype(jnp.int32)

    col_w = [w_in, w_kv, w_up]
    row_w = [w_out, w_q, w_o, w_down]
    col_flags = [True] * 3 + [False] * 4 + [True] * 2
    kw = conv_dw_w.shape[1]

    def padded_in_place(shard, rows):
        full = jnp.zeros((rows, 4 * shard.shape[1]), shard.dtype)
        return lax.dynamic_update_slice(full, shard, (0, chip * shard.shape[1]))

    bufs = list(_place_shards(place, [w[0] for w in col_w + row_w], col_flags[:7]))
    bufs += [padded_in_place(conv_dw_w[0], _HALO), padded_in_place(ffn_dw_w[0], 8)]
    groups = [[0, 7, 8], [3, 4, 1, 5], [2, 6]]
    bufs, sems = _allgather_start(bufs, col_flags, groups)

    def weights(g, after):
        members = groups[g]
        return _allgather_wait("allgather_wait_%d" % g, [bufs[i] for i in members], [col_flags[i] for i in members], sems[g], after)

    names = ["w_in", "w_kv", "w_up", "w_out", "w_q", "w_o", "w_down"]
    reduce_groups = [["w_down", "w_up"], ["w_o", "w_q", "w_kv", "w_out"], ["w_in"]]
    in_flight = {}

    def reduce(g, grads):
        members = reduce_groups[g]
        got = _exchange_pair_halves("rs_pair_exchange_%d" % g, grads)
        sums = [_sum_pairs("rs_pair_sum_" + n, place, a, b) for n, a, b in zip(members, grads, got)]
        sums, lands, rs_sems, token = _chip_exchange_start("rs_chip_start_%d" % g, sums)
        in_flight[g] = (sums, lands, rs_sems)
        return token[0:1, 0:1]

    grad_x, small = _local_step(
        x.reshape(n_seq * seq, d), mem.reshape(n_seq * n_mem, d), loss_target.reshape(n_seq * seq, d),
        norm_mix_g, norm_xattn_g, norm_mem_g, norm_ffn_g, norm_final_g.reshape(1, d),
        conv_dw_b, conv_ln_g, conv_ln_b, pool_w[0], pool_scale, ffn_dw_b, weights, reduce, n_seq, seq, n_mem)

    finals = {}
    for g, members in enumerate(reduce_groups):
        sums, lands, rs_sems = in_flight[g]
        sums, lands = _chip_exchange_wait("rs_chip_wait_%d" % g, sums, lands, rs_sems, grad_x)
        for n, a, b in zip(members, sums, lands):
            finals[n] = _sum_four("rs_chip_sum_" + n, place, a, b)
    shard_grads = _swap_halves([finals[n] for n in names])

    gains, conv_rows, d_pw, ffn_rows, d_cw = _allreduce_small(small)
    loss = gains[5, 0]

    outs = {}
    big_w = dict(zip(names, col_w + row_w))
    big_m = dict(w_in=m_w_in, w_kv=m_w_kv, w_up=m_w_up, w_out=m_w_out, w_q=m_w_q, w_o=m_w_o, w_down=m_w_down)
    big_v = dict(w_in=v_w_in, w_kv=v_w_kv, w_up=v_w_up, w_out=v_w_out, w_q=v_w_q, w_o=v_w_o, w_down=v_w_down)
    for n, g in zip(names, shard_grads):
        w = big_w[n]
        g2 = g.reshape(w.shape[1], w.shape[2])
        delta, new_m, new_v = _adamw_shard("adamw_" + n, w, g2, big_m[n], big_v[n])
        outs[n] = (g2.reshape(w.shape), delta, new_m, new_v)

    f2 = ffn_dw_b.shape[1]
    cs_c = conv_dw_w.shape[2]
    cs_f = ffn_dw_w.shape[2]
    g_cw = lax.dynamic_slice(d_cw, (0, chip * cs_c), (kw, cs_c)).reshape(conv_dw_w.shape)
    g_fw = lax.dynamic_slice(ffn_rows, (1, chip * cs_f), (ffn_dw_w.shape[1], cs_f)).reshape(ffn_dw_w.shape)
    small_params = [
        ("norm_mix_g", norm_mix_g, gains[0:1], m_norm_mix_g, v_norm_mix_g),
        ("conv_dw_w", conv_dw_w, g_cw, m_conv_dw_w, v_conv_dw_w),
        ("conv_dw_b", conv_dw_b, conv_rows[0:1], m_conv_dw_b, v_conv_dw_b),
        ("conv_ln_g", conv_ln_g, conv_rows[1:2], m_conv_ln_g, v_conv_ln_g),
        ("conv_ln_b", conv_ln_b, conv_rows[2:3], m_conv_ln_b, v_conv_ln_b),
        ("pool_w", pool_w, d_pw.reshape(pool_w.shape), m_pool_w, v_pool_w),
        ("pool_scale", pool_scale, conv_rows[3:4], m_pool_scale, v_pool_scale),
        ("norm_xattn_g", norm_xattn_g, gains[1:2], m_norm_xattn_g, v_norm_xattn_g),
        ("norm_mem_g", norm_mem_g, gains[2:3], m_norm_mem_g, v_norm_mem_g),
        ("norm_ffn_g", norm_ffn_g, gains[3:4], m_norm_ffn_g, v_norm_ffn_g),
        ("ffn_dw_w", ffn_dw_w, g_fw, m_ffn_dw_w, v_ffn_dw_w),
        ("ffn_dw_b", ffn_dw_b, ffn_rows[0:1, :f2], m_ffn_dw_b, v_ffn_dw_b),
        ("norm_final_g", norm_final_g.reshape(1, d), gains[4:5], m_norm_final_g.reshape(1, d), v_norm_final_g.reshape(1, d)),
    ]
    quads = []
    for _, w, g, m, v in small_params:
        shape2 = (-1, w.shape[-1])
        quads.append((w.reshape(shape2), g.reshape(shape2), m.reshape(shape2), v.reshape(shape2)))
    for (n, w, g, _, _), (delta, new_m, new_v) in zip(small_params, _adamw_small(quads)):
        shape = norm_final_g.shape if n == "norm_final_g" else w.shape
        outs[n] = (g.reshape(shape), delta.reshape(shape), new_m.reshape(shape), new_v.reshape(shape))

    order = ["norm_mix_g", "w_in", "conv_dw_w", "conv_dw_b", "conv_ln_g", "conv_ln_b", "pool_w", "pool_scale", "w_out",
             "norm_xattn_g", "norm_mem_g", "w_q", "w_kv", "w_o", "norm_ffn_g", "w_up", "ffn_dw_w", "ffn_dw_b", "w_down",
             "norm_final_g"]
    return (loss, grad_x.reshape(x.shape), *[outs[n][0] for n in order], *[outs[n][1] for n in order],
            *[outs[n][2] for n in order], *[outs[n][3] for n in order])
```

```python
import functools

import jax
import jax.numpy as jnp
from jax import lax
from jax.experimental import pallas as pl
from jax.experimental.pallas import tpu as pltpu

f32 = jnp.float32
_ACT = jnp.bfloat16

EPS = 1e-6
POOL_WINDOWS = (2, 4, 8, 16)
XATTN_HEADS = 4
ADAM_LR = 0.001
ADAM_B1 = 0.9
ADAM_B2 = 0.999
ADAM_EPS = 1e-08
ADAM_WD = 0.01
ADAM_STEP = 10

_VMEM_LIMIT_BYTES = 56 * 1024 * 1024
_MESH = pl.DeviceIdType.MESH
_ANY = pl.BlockSpec(memory_space=pl.ANY)
_VMEM = pl.BlockSpec(memory_space=pltpu.VMEM)
_HBM = pl.BlockSpec(memory_space=pltpu.HBM)
_SEM = pl.BlockSpec(memory_space=pltpu.SEMAPHORE)
_EFFECT = pltpu.SideEffectType.DATAFLOW_SIDE_EFFECTING

_NN = (((1,), (0,)), ((), ()))
_NT = (((1,), (1,)), ((), ()))
_TN = (((0,), (0,)), ((), ()))


def _params(n_grid):
    return pltpu.CompilerParams(dimension_semantics=("arbitrary",) * n_grid, vmem_limit_bytes=_VMEM_LIMIT_BYTES)


def _sigmoid(v):
    return 1.0 / (1.0 + jnp.exp(-v))


def _dot(a, b, dims):
    return lax.dot_general(a, b, dims, preferred_element_type=f32)


def _mm(name, a, b, *, dims, grid, a_spec, b_spec, o_spec, out_shape, nk, acc_shape=None, res=None, res_spec=None):
    def body(*refs):
        if res is None:
            a_ref, b_ref, o_ref, *scratch = refs
            r_ref = None
        else:
            a_ref, b_ref, r_ref, o_ref, *scratch = refs
        p = _dot(a_ref[...], b_ref[...], dims)

        def finish(v):
            if r_ref is not None:
                v = v + r_ref[...]
            o_ref[...] = v.astype(o_ref.dtype)

        if nk == 1:
            finish(p)
        else:
            acc = scratch[0]
            k = pl.program_id(2)

            @pl.when(k == 0)
            def _():
                acc[...] = p

            @pl.when(k > 0)
            def _():
                acc[...] += p

            @pl.when(k == nk - 1)
            def _():
                finish(acc[...])

    ins = [a, b] + ([] if res is None else [res])
    specs = [a_spec, b_spec] + ([] if res is None else [res_spec])
    return pl.pallas_call(
        body, name=name, grid=grid, in_specs=specs, out_specs=o_spec, out_shape=out_shape,
        scratch_shapes=[pltpu.VMEM(acc_shape, f32)] if nk > 1 else [], compiler_params=_params(3),
    )(*ins)


def _row_tile(m):
    return min(512, m)


def _mm_nn(name, a, b, out_dtype, tn, res=None, split_out=False):
    m, k = a.shape
    n = b.shape[1]
    tm = _row_tile(m)
    if split_out:
        out_shape = jax.ShapeDtypeStruct((n // tn, m, tn), out_dtype)
        o_spec = pl.BlockSpec((None, tm, tn), lambda j, i, kk: (j, i, 0))
    else:
        out_shape = jax.ShapeDtypeStruct((m, n), out_dtype)
        o_spec = pl.BlockSpec((tm, tn), lambda j, i, kk: (i, j))
    return _mm(
        name, a, b, dims=_NN, grid=(n // tn, m // tm, 1), nk=1,
        a_spec=pl.BlockSpec((tm, k), lambda j, i, kk: (i, 0)),
        b_spec=pl.BlockSpec((k, tn), lambda j, i, kk: (0, j)),
        o_spec=o_spec, out_shape=out_shape, res=res,
        res_spec=pl.BlockSpec((tm, tn), lambda j, i, kk: (i, j)),
    )


def _mm_nt(name, a, b, out_dtype):
    n, kc = b.shape
    m = a.shape[0]
    tm = _row_tile(m)
    return _mm(
        name, a, b, dims=_NT, grid=(m // tm, 1, 1), nk=1,
        a_spec=pl.BlockSpec((tm, kc), lambda i, j, k: (i, 0)), b_spec=pl.BlockSpec((n, kc), lambda i, j, k: (0, 0)),
        o_spec=pl.BlockSpec((tm, n), lambda i, j, k: (i, 0)),
        out_shape=jax.ShapeDtypeStruct((m, n), out_dtype),
    )


def _mm_tn_rows(name, a, b, tka, tn):
    m, ka = a.shape
    nb = b.shape[1]
    return _mm(
        name, a, b, dims=_TN, grid=(ka // tka, nb // tn, 1), nk=1,
        a_spec=pl.BlockSpec((m, tka), lambda i, j, k: (0, i)),
        b_spec=pl.BlockSpec((m, tn), lambda i, j, k: (0, j)),
        o_spec=pl.BlockSpec((tka, tn), lambda i, j, k: (i, j)),
        out_shape=jax.ShapeDtypeStruct((ka, nb), _ACT),
    )


def _mm_tn_pieces(name, a, b, cs, tt):
    m, ka = a.shape
    nk = m // tt
    if b.ndim == 3:
        b_spec = pl.BlockSpec((None, tt, cs), lambda i, j, k: (j // 2, k, j % 2))
    else:
        b_spec = pl.BlockSpec((tt, cs), lambda i, j, k: (k, j))
    return _mm(
        name, a, b, dims=_TN, grid=(2, 4, nk), nk=nk, acc_shape=(ka // 2, cs),
        a_spec=pl.BlockSpec((tt, ka // 2), lambda i, j, k: (k, i)), b_spec=b_spec,
        o_spec=pl.BlockSpec((None, ka // 2, cs), lambda i, j, k: (2 * j + i, 0, 0)),
        out_shape=jax.ShapeDtypeStruct((8, ka // 2, cs), _ACT),
    )


def _rms_fwd(name, x, g):
    t, d = x.shape
    tm = _row_tile(t)

    def body(x_ref, g_ref, h_ref):
        xv = x_ref[...]
        r = lax.rsqrt(jnp.mean(xv * xv, axis=-1, keepdims=True) + EPS)
        h_ref[...] = (xv * r * g_ref[...]).astype(h_ref.dtype)

    return pl.pallas_call(
        body, name=name, grid=(t // tm,),
        in_specs=[pl.BlockSpec((tm, d), lambda i: (i, 0)), pl.BlockSpec((1, d), lambda i: (0, 0))],
        out_specs=pl.BlockSpec((tm, d), lambda i: (i, 0)), out_shape=jax.ShapeDtypeStruct((t, d), _ACT),
        compiler_params=_params(1),
    )(x, g)


def _fused_rows(name, a, b, product, a_spec, tm, extras, extra_specs, out_shape, out_specs, epilogue):
    ne = len(extras)

    def body(a_ref, b_ref, *refs):
        epilogue(product(a_ref, b_ref), refs[:ne], refs[ne:])

    m = extras[0].shape[0]
    return pl.pallas_call(
        body, name=name, grid=(m // tm,),
        in_specs=[a_spec, pl.BlockSpec(b.shape, lambda i: (0, 0)), *extra_specs], out_specs=out_specs, out_shape=out_shape,
        compiler_params=_params(1),
    )(a, b, *extras)


def _proj_residual_norm(name, a, b, res, g):
    m, k = a.shape
    d = b.shape[1]
    tm = _row_tile(m)

    def epilogue(p, ins, outs):
        xv = p + ins[0][...]
        outs[0][...] = xv
        r = lax.rsqrt(jnp.mean(xv * xv, axis=-1, keepdims=True) + EPS)
        outs[1][...] = (xv * r * ins[1][...]).astype(outs[1].dtype)

    row = pl.BlockSpec((tm, d), lambda i: (i, 0))
    return _fused_rows(
        name, a, b, lambda a_ref, b_ref: _dot(a_ref[...], b_ref[...], _NN), pl.BlockSpec((tm, k), lambda i: (i, 0)), tm,
        [res, g], [row, pl.BlockSpec((1, d), lambda i: (0, 0))],
        [jax.ShapeDtypeStruct((m, d), f32), jax.ShapeDtypeStruct((m, d), _ACT)], [row, row], epilogue)


def _dproj_rms_bwd(name, a, b, x, g, dres, storage_copy=True):
    m, d = x.shape
    if a.ndim == 3:
        nh, _, kh = a.shape
        tm = min(256, m)
        a_spec = pl.BlockSpec((nh, tm, kh), lambda i: (0, i, 0))

        def product(a_ref, b_ref):
            p = _dot(a_ref[0], b_ref[:, 0:kh], _NT)
            for h in range(1, nh):
                p = p + _dot(a_ref[h], b_ref[:, h * kh:(h + 1) * kh], _NT)
            return p
    else:
        tm = _row_tile(m)
        a_spec = pl.BlockSpec((tm, a.shape[1]), lambda i: (i, 0))

        def product(a_ref, b_ref):
            return _dot(a_ref[...], b_ref[...], _NT)

    def epilogue(dhv, ins, outs):
        x_ref, g_ref, dres_ref = ins
        dg_ref = outs[-1]

        @pl.when(pl.program_id(0) == 0)
        def _():
            dg_ref[...] = jnp.zeros_like(dg_ref)

        xv = x_ref[...]
        r = lax.rsqrt(jnp.mean(xv * xv, axis=-1, keepdims=True) + EPS)
        xn = xv * r
        dxn = dhv * g_ref[...]
        dx = r * (dxn - xn * jnp.mean(dxn * xn, axis=-1, keepdims=True)) + dres_ref[...]
        outs[0][...] = dx
        if storage_copy:
            outs[1][...] = dx.astype(outs[1].dtype)
        dg_ref[...] += jnp.sum(dhv * xn, axis=0, keepdims=True)

    row = pl.BlockSpec((tm, d), lambda i: (i, 0))
    vec = pl.BlockSpec((1, d), lambda i: (0, 0))
    copies = [jax.ShapeDtypeStruct((m, d), _ACT)] if storage_copy else []
    return _fused_rows(
        name, a, b, product, a_spec, tm, [x, g, dres], [row, vec, row],
        [jax.ShapeDtypeStruct((m, d), f32)] + copies + [jax.ShapeDtypeStruct((1, d), f32)],
        [row] * (1 + len(copies)) + [vec], epilogue)


def _proj_loss_bwd(name, a, b, res, g, tgt):
    m, k = a.shape
    d = b.shape[1]
    tm = _row_tile(m)

    def epilogue(p, ins, outs):
        res_ref, g_ref, t_ref = ins
        dx_ref, dxb_ref, dg_ref, loss_ref = outs

        @pl.when(pl.program_id(0) == 0)
        def _():
            dg_ref[...] = jnp.zeros_like(dg_ref)
            loss_ref[...] = jnp.zeros_like(loss_ref)

        xv = p + res_ref[...]
        gv = g_ref[...]
        r = lax.rsqrt(jnp.mean(xv * xv, axis=-1, keepdims=True) + EPS)
        xn = xv * r
        err = xn * gv - t_ref[...]
        loss_ref[...] += 0.5 * jnp.sum(jnp.mean(err * err, axis=-1, keepdims=True), axis=0, keepdims=True)
        dout = err * (1.0 / d)
        dxn = dout * gv
        dx = r * (dxn - xn * jnp.mean(dxn * xn, axis=-1, keepdims=True))
        dx_ref[...] = dx
        dxb_ref[...] = dx.astype(dxb_ref.dtype)
        dg_ref[...] += jnp.sum(dout * xn, axis=0, keepdims=True)

    row = pl.BlockSpec((tm, d), lambda i: (i, 0))
    vec = pl.BlockSpec((1, d), lambda i: (0, 0))
    return _fused_rows(
        name, a, b, lambda a_ref, b_ref: _dot(a_ref[...], b_ref[...], _NN), pl.BlockSpec((tm, k), lambda i: (i, 0)), tm,
        [res, g, tgt], [row, vec, row],
        [jax.ShapeDtypeStruct((m, d), f32), jax.ShapeDtypeStruct((m, d), _ACT), jax.ShapeDtypeStruct((1, d), f32),
         jax.ShapeDtypeStruct((1, 1), f32)],
        [row, row, vec, pl.BlockSpec((1, 1), lambda i: (0, 0))], epilogue)


def _rms_gain_grad(name, x, dh):
    t, d = x.shape
    tm = _row_tile(t)

    def body(x_ref, dh_ref, dg_ref):
        @pl.when(pl.program_id(0) == 0)
        def _():
            dg_ref[...] = jnp.zeros_like(dg_ref)

        xv = x_ref[...]
        r = lax.rsqrt(jnp.mean(xv * xv, axis=-1, keepdims=True) + EPS)
        dg_ref[...] += jnp.sum(dh_ref[...] * (xv * r), axis=0, keepdims=True)

    row = pl.BlockSpec((tm, d), lambda i: (i, 0))
    return pl.pallas_call(
        body, name=name, grid=(t // tm,), in_specs=[row, row], out_specs=pl.BlockSpec((1, d), lambda i: (0, 0)),
        out_shape=jax.ShapeDtypeStruct((1, d), f32), compiler_params=_params(1),
    )(x, dh)


_CONV_ROWS = 256
_CHUNK = 64
_HALO = 32


def _pool_counts(pos, w):
    return jnp.minimum(pos + 1.0, float(w))


def _mix_fwd(u, cw, cb, lg, lb, pw, ps, seq):
    t, c3 = u.shape
    c = c3 // 3
    kw = 31
    tm = min(_CONV_ROWS, seq)
    tps = seq // tm
    gd = c // len(POOL_WINDOWS)

    def body(u_ref, uh_ref, cw_ref, cb_ref, lg_ref, lb_ref, pw_ref, ps_ref, y_ref, hc_ref, hgbuf, pbuf):
        i = pl.program_id(0)
        keep = jnp.where(i % tps == 0, 0.0, 1.0)
        um = u_ref[...].astype(f32)
        uh = uh_ref[...].astype(f32) * keep
        hgbuf[0:_HALO, :] = uh[:, 0:c] * _sigmoid(uh[:, c:2 * c])
        hgbuf[_HALO:_HALO + tm, :] = um[:, 0:c] * _sigmoid(um[:, c:2 * c])
        pbuf[0:_HALO, :] = uh[:, 2 * c:]
        pbuf[_HALO:_HALO + tm, :] = um[:, 2 * c:]
        for r0 in range(0, tm, _CHUNK):
            acc = jnp.broadcast_to(cb_ref[...], (_CHUNK, c))
            for k in range(kw):
                off = r0 + _HALO - (kw - 1) + k
                acc = acc + cw_ref[k:k + 1, :] * hgbuf[off:off + _CHUNK, :]
            hc_ref[r0:r0 + _CHUNK, :] = acc
            mu = jnp.mean(acc, axis=-1, keepdims=True)
            xc = acc - mu
            var = jnp.mean(xc * xc, axis=-1, keepdims=True)
            hl = xc * lax.rsqrt(var + EPS) * lg_ref[...] + lb_ref[...]
            y_ref[r0:r0 + _CHUNK, 0:c] = (hl * _sigmoid(hl)).astype(y_ref.dtype)
        pos = ((i % tps) * tm).astype(f32) + lax.broadcasted_iota(jnp.int32, (tm, 1), 0).astype(f32)
        for gi, w in enumerate(POOL_WINDOWS):
            sl = slice(gi * gd, (gi + 1) * gd)
            v = pbuf[_HALO:_HALO + tm, sl]
            s = v
            for j in range(1, w):
                s = s + pbuf[_HALO - j:_HALO - j + tm, sl]
            pooled = s / _pool_counts(pos, w) - v
            mixed = _dot(pooled.astype(_ACT), pw_ref[gi].astype(_ACT), _NN)
            y_ref[:, c + gi * gd:c + (gi + 1) * gd] = (mixed * ps_ref[:, sl]).astype(y_ref.dtype)

    hb = tm // _HALO
    full = lambda shape: pl.BlockSpec(shape, lambda i: (0,) * len(shape))
    return pl.pallas_call(
        body, name="mix_fwd", grid=(t // tm,),
        in_specs=[pl.BlockSpec((tm, c3), lambda i: (i, 0)),
                  pl.BlockSpec((_HALO, c3), lambda i: (jnp.maximum(i * hb - 1, 0), 0)),
                  full((_HALO, c)), full((1, c)), full((1, c)), full((1, c)), full((len(POOL_WINDOWS), gd, gd)), full((1, c))],
        out_specs=[pl.BlockSpec((tm, 2 * c), lambda i: (i, 0)), pl.BlockSpec((tm, c), lambda i: (i, 0))],
        out_shape=[jax.ShapeDtypeStruct((t, 2 * c), _ACT), jax.ShapeDtypeStruct((t, c), f32)],
        scratch_shapes=[pltpu.VMEM((_HALO + tm, c), f32), pltpu.VMEM((_HALO + tm, c), f32)],
        compiler_params=_params(1),
    )(u, u, cw, cb, lg, lb, pw, ps)


def _mix_bwd_norm(hc, dy, lg, lb, seq):
    t, c = hc.shape
    tm = min(_CONV_ROWS, seq)

    def body(hc_ref, dy_ref, lg_ref, lb_ref, dhc_ref, sums_ref):
        @pl.when(pl.program_id(0) == 0)
        def _():
            sums_ref[...] = jnp.zeros_like(sums_ref)

        hcv = hc_ref[...]
        mu = jnp.mean(hcv, axis=-1, keepdims=True)
        xc = hcv - mu
        rstd = lax.rsqrt(jnp.mean(xc * xc, axis=-1, keepdims=True) + EPS)
        n = xc * rstd
        hl = n * lg_ref[...] + lb_ref[...]
        sg = _sigmoid(hl)
        dhl = dy_ref[...].astype(f32) * (sg * (1.0 + hl * (1.0 - sg)))
        dn = dhl * lg_ref[...]
        dhc = rstd * (dn - jnp.mean(dn, axis=-1, keepdims=True) - n * jnp.mean(dn * n, axis=-1, keepdims=True))
        dhc_ref[...] = dhc
        sums_ref[0:1, :] += jnp.sum(dhl * n, axis=0, keepdims=True)
        sums_ref[1:2, :] += jnp.sum(dhl, axis=0, keepdims=True)
        sums_ref[2:3, :] += jnp.sum(dhc, axis=0, keepdims=True)

    row = pl.BlockSpec((tm, c), lambda i: (i, 0))
    vec = pl.BlockSpec((1, c), lambda i: (0, 0))
    return pl.pallas_call(
        body, name="mix_bwd_norm", grid=(t // tm,), in_specs=[row, row, vec, vec],
        out_specs=[row, pl.BlockSpec((8, c), lambda i: (0, 0))],
        out_shape=[jax.ShapeDtypeStruct((t, c), f32), jax.ShapeDtypeStruct((8, c), f32)],
        compiler_params=_params(1),
    )(hc, dy, lg, lb)


def _mix_bwd_taps(u, dhc, dy, cw, pw, ps, seq):
    t, c3 = u.shape
    c = c3 // 3
    kw = 31
    tm = min(_CONV_ROWS, seq)
    tps = seq // tm
    ng = len(POOL_WINDOWS)
    gd = c // ng
    nh = 16

    def body(u_ref, uh_ref, dhc_ref, dhcn_ref, dy_ref, dyn_ref, cw_ref, pw_ref, ps_ref,
             du_ref, dcw_ref, dps_ref, dpw_ref, hgbuf, dcbuf, pbuf, dpbuf):
        i = pl.program_id(0)
        keep_prev = jnp.where(i % tps == 0, 0.0, 1.0)
        keep_next = jnp.where(i % tps == tps - 1, 0.0, 1.0)

        @pl.when(i == 0)
        def _():
            dcw_ref[...] = jnp.zeros_like(dcw_ref)
            dps_ref[...] = jnp.zeros_like(dps_ref)
            dpw_ref[...] = jnp.zeros_like(dpw_ref)

        uh = uh_ref[...].astype(f32) * keep_prev
        hgbuf[0:_HALO, :] = uh[:, 0:c] * _sigmoid(uh[:, c:2 * c])
        pbuf[0:_HALO, :] = uh[:, 2 * c:]
        um = u_ref[...].astype(f32)
        hgbuf[_HALO:_HALO + tm, :] = um[:, 0:c] * _sigmoid(um[:, c:2 * c])
        pbuf[_HALO:_HALO + tm, :] = um[:, 2 * c:]
        dcbuf[0:tm, :] = dhc_ref[...]
        dcbuf[tm:tm + _HALO, :] = dhcn_ref[...] * keep_next
        tap_sums = [None] * kw
        for r0 in range(0, tm, _CHUNK):
            dh = dcbuf[r0:r0 + _CHUNK, :]
            acc = jnp.zeros((_CHUNK, c), f32)
            for k in range(kw):
                off = r0 + _HALO - (kw - 1) + k
                part = jnp.sum(dh * hgbuf[off:off + _CHUNK, :], axis=0, keepdims=True)
                tap_sums[k] = part if tap_sums[k] is None else tap_sums[k] + part
                fwd = r0 + (kw - 1) - k
                acc = acc + cw_ref[k:k + 1, :] * dcbuf[fwd:fwd + _CHUNK, :]
            val = u_ref[r0:r0 + _CHUNK, 0:c].astype(f32)
            sg = _sigmoid(u_ref[r0:r0 + _CHUNK, c:2 * c].astype(f32))
            du_ref[r0:r0 + _CHUNK, 0:c] = (acc * sg).astype(du_ref.dtype)
            du_ref[r0:r0 + _CHUNK, c:2 * c] = (acc * val * sg * (1.0 - sg)).astype(du_ref.dtype)
        for k in range(kw):
            dcw_ref[k:k + 1, :] += tap_sums[k]
        base = ((i % tps) * tm).astype(f32)
        pos = base + lax.broadcasted_iota(jnp.int32, (tm, 1), 0).astype(f32)
        pos_next = base + float(tm) + lax.broadcasted_iota(jnp.int32, (nh, 1), 0).astype(f32)
        for gi, w in enumerate(POOL_WINDOWS):
            sl = slice(gi * gd, (gi + 1) * gd)
            v = pbuf[_HALO:_HALO + tm, sl]
            s = v
            for j in range(1, w):
                s = s + pbuf[_HALO - j:_HALO - j + tm, sl]
            cnt = _pool_counts(pos, w)
            pooled = (s / cnt - v).astype(_ACT)
            pwg = pw_ref[gi].astype(_ACT)
            mixed = _dot(pooled, pwg, _NN)
            dyp = dy_ref[:, sl].astype(f32)
            dps_ref[0:1, sl] += jnp.sum(dyp * mixed, axis=0, keepdims=True)
            dmix = (dyp * ps_ref[:, sl]).astype(_ACT)
            dpw_ref[gi] += _dot(pooled, dmix, _TN)
            dmix_next = (dyn_ref[:, sl].astype(f32) * ps_ref[:, sl] * keep_next).astype(_ACT)
            dpool = _dot(dmix, pwg, _NT)
            dpbuf[0:tm, sl] = dpool / cnt
            dpbuf[tm:tm + nh, sl] = _dot(dmix_next, pwg, _NT) / _pool_counts(pos_next, w)
            acc = -dpool
            for j in range(w):
                acc = acc + dpbuf[j:j + tm, sl]
            du_ref[:, 2 * c + gi * gd:2 * c + (gi + 1) * gd] = acc.astype(du_ref.dtype)

    hb = tm // _HALO
    n_halo = t // _HALO
    n_nh = t // nh
    full = lambda shape: pl.BlockSpec(shape, lambda i: (0,) * len(shape))
    return pl.pallas_call(
        body, name="mix_bwd_taps", grid=(t // tm,),
        in_specs=[pl.BlockSpec((tm, c3), lambda i: (i, 0)),
                  pl.BlockSpec((_HALO, c3), lambda i: (jnp.maximum(i * hb - 1, 0), 0)),
                  pl.BlockSpec((tm, c), lambda i: (i, 0)),
                  pl.BlockSpec((_HALO, c), lambda i: (jnp.minimum((i + 1) * hb, n_halo - 1), 0)),
                  pl.BlockSpec((tm, c), lambda i: (i, 1)),
                  pl.BlockSpec((nh, c), lambda i: (jnp.minimum((i + 1) * (tm // nh), n_nh - 1), 1)),
                  full((_HALO, c)), full((ng, gd, gd)), full((1, c))],
        out_specs=[pl.BlockSpec((tm, c3), lambda i: (i, 0)), full((_HALO, c)), full((8, c)), full((ng, gd, gd))],
        out_shape=[jax.ShapeDtypeStruct((t, c3), _ACT), jax.ShapeDtypeStruct((_HALO, c), f32),
                   jax.ShapeDtypeStruct((8, c), f32), jax.ShapeDtypeStruct((ng, gd, gd), f32)],
        scratch_shapes=[pltpu.VMEM((_HALO + tm, c), f32), pltpu.VMEM((tm + _HALO, c), f32),
                        pltpu.VMEM((_HALO + tm, c), f32), pltpu.VMEM((tm + nh, c), f32)],
        compiler_params=_params(1),
    )(u, u, dhc, dhc, dy, dy, cw, pw, ps)


def _attn_fwd(q, kv, n_seq, seq, n_mem):
    t, d = q.shape
    dh = d // XATTN_HEADS
    tq = min(512, seq)
    nq = seq // tq
    scale = dh ** -0.5

    def body(q_ref, k_ref, v_ref, o_ref):
        s = _dot(q_ref[...], k_ref[...], _NT) * scale
        e = jnp.exp(s - jnp.max(s, axis=-1, keepdims=True))
        p = e / jnp.sum(e, axis=-1, keepdims=True)
        o_ref[...] = _dot(p.astype(_ACT), v_ref[...], _NN).astype(o_ref.dtype)

    qs = pl.BlockSpec((tq, dh), lambda b, h, i: (b * nq + i, h))
    return pl.pallas_call(
        body, name="attn_fwd", grid=(n_seq, XATTN_HEADS, nq),
        in_specs=[qs, pl.BlockSpec((n_mem, dh), lambda b, h, i: (b, h)),
                  pl.BlockSpec((n_mem, dh), lambda b, h, i: (b, XATTN_HEADS + h))],
        out_specs=qs, out_shape=jax.ShapeDtypeStruct((t, d), _ACT), compiler_params=_params(3),
    )(q, kv, kv)


def _attn_bwd(q, kv, do, n_seq, seq, n_mem):
    t, d = q.shape
    dh = d // XATTN_HEADS
    tq = min(512, seq)
    nq = seq // tq
    scale = dh ** -0.5

    def body(q_ref, k_ref, v_ref, do_ref, dq_ref, dk_ref, dv_ref, dk_acc, dv_acc):
        i = pl.program_id(2)
        qv = q_ref[...]
        kvv = k_ref[...]
        dov = do_ref[...]
        s = _dot(qv, kvv, _NT) * scale
        e = jnp.exp(s - jnp.max(s, axis=-1, keepdims=True))
        p = e / jnp.sum(e, axis=-1, keepdims=True)
        dp = _dot(dov, v_ref[...], _NT)
        ds = (p * (dp - jnp.sum(dp * p, axis=-1, keepdims=True)) * scale).astype(_ACT)
        dq_ref[...] = _dot(ds, kvv, _NN).astype(dq_ref.dtype)
        dk_part = _dot(ds, qv, _TN)
        dv_part = _dot(p.astype(_ACT), dov, _TN)

        @pl.when(i == 0)
        def _():
            dk_acc[...] = dk_part
            dv_acc[...] = dv_part

        @pl.when(i > 0)
        def _():
            dk_acc[...] += dk_part
            dv_acc[...] += dv_part

        @pl.when(i == nq - 1)
        def _():
            dk_ref[...] = dk_acc[...].astype(dk_ref.dtype)
            dv_ref[...] = dv_acc[...].astype(dv_ref.dtype)

    qs = pl.BlockSpec((tq, dh), lambda b, h, i: (b * nq + i, h))
    ms = pl.BlockSpec((n_mem, dh), lambda b, h, i: (b, h))
    return pl.pallas_call(
        body, name="attn_bwd", grid=(n_seq, XATTN_HEADS, nq),
        in_specs=[qs, ms, pl.BlockSpec((n_mem, dh), lambda b, h, i: (b, XATTN_HEADS + h)), qs],
        out_specs=[qs, ms, ms],
        out_shape=[jax.ShapeDtypeStruct((t, d), _ACT), jax.ShapeDtypeStruct((n_seq * n_mem, d), _ACT),
                   jax.ShapeDtypeStruct((n_seq * n_mem, d), _ACT)],
        scratch_shapes=[pltpu.VMEM((n_mem, dh), f32), pltpu.VMEM((n_mem, dh), f32)],
        compiler_params=_params(3),
    )(q, kv, kv, do)


_FFN_ROWS = 1024
_FFN_COLS = 256
_FFN_HALO = 16


def _window(buf, g, start, rows):
    return buf[g, pl.ds(start, rows + 8), :]


def _conv3(b_ref, w_ref, win, rows):
    acc = jnp.broadcast_to(b_ref[...], (rows, win.shape[1]))
    for k in range(3):
        acc = acc + w_ref[k:k + 1, :] * win[6 + k:6 + k + rows, :]
    return acc


def _ffn_gate_fwd(up, fw, fb, seq):
    _, t, f = up.shape
    tm = min(_FFN_ROWS, seq)
    tps = seq // tm
    tc = _FFN_COLS
    nc = f // tc
    hl = _FFN_HALO

    def body(up_ref, uph_ref, wg_ref, wv_ref, bg_ref, bv_ref, a_ref, buf):
        i = pl.program_id(1)
        keep = jnp.where(i % tps == 0, 0.0, 1.0)
        buf[:, 0:hl, :] = uph_ref[...].astype(f32) * keep
        buf[:, hl:hl + tm, :] = up_ref[...].astype(f32)

        def chunk(ci, carry):
            r0 = pl.multiple_of(ci * _CHUNK, _CHUNK)
            conv = []
            for g, (w_ref, b_ref) in enumerate(((wg_ref, bg_ref), (wv_ref, bv_ref))):
                conv.append(_conv3(b_ref, w_ref, _window(buf, g, r0 + hl - 8, _CHUNK), _CHUNK))
            gate, val = conv
            a_ref[pl.ds(r0, _CHUNK), :] = (gate * _sigmoid(gate) * val).astype(a_ref.dtype)
            return carry

        lax.fori_loop(0, tm // _CHUNK, chunk, 0)

    hb = tm // hl
    return pl.pallas_call(
        body, name="ffn_gate_fwd", grid=(nc, t // tm),
        in_specs=[pl.BlockSpec((2, tm, tc), lambda j, i: (0, i, j)),
                  pl.BlockSpec((2, hl, tc), lambda j, i: (0, jnp.maximum(i * hb - 1, 0), j)),
                  pl.BlockSpec((8, tc), lambda j, i: (0, j)), pl.BlockSpec((8, tc), lambda j, i: (0, nc + j)),
                  pl.BlockSpec((1, tc), lambda j, i: (0, j)), pl.BlockSpec((1, tc), lambda j, i: (0, nc + j))],
        out_specs=pl.BlockSpec((tm, tc), lambda j, i: (i, j)),
        out_shape=jax.ShapeDtypeStruct((t, f), _ACT),
        scratch_shapes=[pltpu.VMEM((2, hl + tm, tc), f32)], compiler_params=_params(2),
    )(up, up, fw, fw, fb, fb)


def _ffn_gate_bwd(up, da, fw, fb, seq):
    _, t, f = up.shape
    tm = min(_FFN_ROWS, seq)
    tps = seq // tm
    tc = _FFN_COLS
    nc = f // tc
    hl = _FFN_HALO

    def body(up_ref, uph_ref, upn_ref, da_ref, dan_ref, wg_ref, wv_ref, bg_ref, bv_ref,
             dup_ref, sg_ref, sv_ref, ubuf, dbuf, sums):
        i = pl.program_id(1)
        keep_prev = jnp.where(i % tps == 0, 0.0, 1.0)
        keep_next = jnp.where(i % tps == tps - 1, 0.0, 1.0)

        @pl.when(i == 0)
        def _():
            sg_ref[...] = jnp.zeros_like(sg_ref)
            sv_ref[...] = jnp.zeros_like(sv_ref)

        sums[...] = jnp.zeros_like(sums)
        ubuf[:, 0:hl, :] = uph_ref[...].astype(f32) * keep_prev
        ubuf[:, hl:hl + tm, :] = up_ref[...].astype(f32)
        ubuf[:, hl + tm:hl + tm + hl, :] = upn_ref[...].astype(f32) * keep_next
        w_refs = (wg_ref, wv_ref)
        b_refs = (bg_ref, bv_ref)

        def grads(r0, rows, dav, count):
            wins = [_window(ubuf, g, r0 + hl - 8, rows) for g in range(2)]
            gate, val = [_conv3(b_refs[g], w_refs[g], wins[g], rows) for g in range(2)]
            sg = _sigmoid(gate)
            douts = (dav * val * (sg * (1.0 + gate * (1.0 - sg))), dav * (gate * sg))
            for g in range(2):
                dbuf[g, pl.ds(r0, rows), :] = douts[g]
                if count:
                    sums[g, 0] += douts[g].reshape(rows // 8, 8, tc).sum(axis=0)
                    for k in range(3):
                        sums[g, 1 + k] += (douts[g] * wins[g][6 + k:6 + k + rows, :]).reshape(rows // 8, 8, tc).sum(axis=0)

        def first(ci, carry):
            r0 = pl.multiple_of(ci * _CHUNK, _CHUNK)
            grads(r0, _CHUNK, da_ref[pl.ds(r0, _CHUNK), :].astype(f32), True)
            return carry

        lax.fori_loop(0, tm // _CHUNK, first, 0)
        grads(tm, hl, dan_ref[...].astype(f32) * keep_next, False)

        def second(ci, carry):
            r0 = pl.multiple_of(ci * _CHUNK, _CHUNK)
            for g in range(2):
                win = _window(dbuf, g, r0, _CHUNK)
                acc = jnp.zeros((_CHUNK, tc), f32)
                for k in range(3):
                    acc = acc + w_refs[g][k:k + 1, :] * win[2 - k:2 - k + _CHUNK, :]
                dup_ref[g, pl.ds(r0, _CHUNK), :] = acc.astype(dup_ref.dtype)
            return carry

        lax.fori_loop(0, tm // _CHUNK, second, 0)
        for g, s_ref in enumerate((sg_ref, sv_ref)):
            for r in range(4):
                s_ref[r:r + 1, :] += jnp.sum(sums[g, r], axis=0, keepdims=True)

    hb = tm // hl
    n_halo = t // hl
    return pl.pallas_call(
        body, name="ffn_gate_bwd", grid=(nc, t // tm),
        in_specs=[pl.BlockSpec((2, tm, tc), lambda j, i: (0, i, j)),
                  pl.BlockSpec((2, hl, tc), lambda j, i: (0, jnp.maximum(i * hb - 1, 0), j)),
                  pl.BlockSpec((2, hl, tc), lambda j, i: (0, jnp.minimum((i + 1) * hb, n_halo - 1), j)),
                  pl.BlockSpec((tm, tc), lambda j, i: (i, j)),
                  pl.BlockSpec((hl, tc), lambda j, i: (jnp.minimum((i + 1) * hb, n_halo - 1), j)),
                  pl.BlockSpec((8, tc), lambda j, i: (0, j)), pl.BlockSpec((8, tc), lambda j, i: (0, nc + j)),
                  pl.BlockSpec((1, tc), lambda j, i: (0, j)), pl.BlockSpec((1, tc), lambda j, i: (0, nc + j))],
        out_specs=[pl.BlockSpec((2, tm, tc), lambda j, i: (0, i, j)),
                   pl.BlockSpec((8, tc), lambda j, i: (0, j)), pl.BlockSpec((8, tc), lambda j, i: (0, j))],
        out_shape=[jax.ShapeDtypeStruct((2, t, f), _ACT), jax.ShapeDtypeStruct((8, f), f32), jax.ShapeDtypeStruct((8, f), f32)],
        scratch_shapes=[pltpu.VMEM((2, hl + tm + hl, tc), f32), pltpu.VMEM((2, tm + hl, tc), f32),
                        pltpu.VMEM((2, 4, 8, tc), f32)],
        compiler_params=_params(2),
    )(up, up, up, da, da, fw, fw, fb, fb)


def _adamw_math(w, g, m, v):
    m = ADAM_B1 * m + (1.0 - ADAM_B1) * g
    v = ADAM_B2 * v + (1.0 - ADAM_B2) * (g * g)
    m_hat = m / (1.0 - ADAM_B1 ** ADAM_STEP)
    v_hat = v / (1.0 - ADAM_B2 ** ADAM_STEP)
    delta = -ADAM_LR * (m_hat / (jnp.sqrt(v_hat) + ADAM_EPS) + ADAM_WD * w)
    return delta, m, v


def _adamw_shard(name, w, g, m, v):
    _, r, c = w.shape
    tr = next((cand for cand in (256, 176, 128, 64, 32, 16, 8) if r % cand == 0), r)

    def body(w_ref, g_ref, m_ref, v_ref, d_ref, mo_ref, vo_ref):
        d, mn, vn = _adamw_math(w_ref[...], g_ref[...], m_ref[...], v_ref[...])
        d_ref[...] = d
        mo_ref[...] = mn
        vo_ref[...] = vn

    s3 = pl.BlockSpec((None, tr, c), lambda i: (0, i, 0))
    s2 = pl.BlockSpec((tr, c), lambda i: (i, 0))
    shp = jax.ShapeDtypeStruct(w.shape, f32)
    return pl.pallas_call(
        body, name=name, grid=(r // tr,), in_specs=[s3, s2, s3, s3], out_specs=[s3, s3, s3], out_shape=[shp, shp, shp],
        compiler_params=_params(1),
    )(w, g, m, v)


def _adamw_small(quads):
    n = len(quads)

    def body(*refs):
        ins, outs = refs[:4 * n], refs[4 * n:]
        for p in range(n):
            w_ref, g_ref, m_ref, v_ref = ins[4 * p:4 * p + 4]
            d, mn, vn = _adamw_math(w_ref[...], g_ref[...], m_ref[...], v_ref[...])
            outs[3 * p][...] = d
            outs[3 * p + 1][...] = mn
            outs[3 * p + 2][...] = vn

    flat = [a for q in quads for a in q]
    shapes = [jax.ShapeDtypeStruct(q[0].shape, f32) for q in quads for _ in range(3)]
    outs = pl.pallas_call(
        body, name="adamw_small", in_specs=[_VMEM] * (4 * n), out_specs=[_VMEM] * (3 * n), out_shape=shapes,
        compiler_params=pltpu.CompilerParams(vmem_limit_bytes=_VMEM_LIMIT_BYTES),
    )(*flat)
    return [tuple(outs[3 * p:3 * p + 3]) for p in range(n)]


def _sum_pairs(name, place, grads, got):
    _, r, c = grads.shape

    def body(place_ref, a_ref, b_ref, o_ref):
        o_ref[...] = (a_ref[...].astype(f32) + b_ref[...].astype(f32)).astype(o_ref.dtype)

    grid_spec = pltpu.PrefetchScalarGridSpec(
        num_scalar_prefetch=1, grid=(4,),
        in_specs=[pl.BlockSpec((None, r, c), lambda i, p: (2 * i + p[1], 0, 0)), pl.BlockSpec((None, r, c), lambda i, p: (i, 0, 0))],
        out_specs=pl.BlockSpec((None, r, c), lambda i, p: (i, 0, 0)))
    return pl.pallas_call(body, name=name, grid_spec=grid_spec, out_shape=jax.ShapeDtypeStruct((4, r, c), _ACT),
                          compiler_params=_params(1))(place, grads, got)


def _sum_four(name, place, sums, got):
    _, r, c = sums.shape

    def body(place_ref, o_ref, g_ref, f_ref):
        s = o_ref[...].astype(f32) + g_ref[0].astype(f32)
        s = s + g_ref[1].astype(f32)
        f_ref[...] = s + g_ref[2].astype(f32)

    grid_spec = pltpu.PrefetchScalarGridSpec(
        num_scalar_prefetch=1, grid=(1,),
        in_specs=[pl.BlockSpec((None, r, c), lambda i, p: (p[0], 0, 0)), pl.BlockSpec((3, r, c), lambda i, p: (0, 0, 0))],
        out_specs=pl.BlockSpec((None, r, c), lambda i, p: (p[1], 0, 0)))
    return pl.pallas_call(body, name=name, grid_spec=grid_spec, out_shape=jax.ShapeDtypeStruct((2, r, c), f32),
                          compiler_params=_params(1))(place, sums, got)


def _place():
    return lax.axis_index("x"), lax.axis_index("y"), lax.axis_index("c")


def _other_chips(x, y):
    return [(1 - x, y), (x, 1 - y), (1 - x, 1 - y)]


def _remote(src, dst, send_sem, recv_sem, to):
    return pltpu.make_async_remote_copy(src_ref=src, dst_ref=dst, send_sem=send_sem, recv_sem=recv_sem,
                                        device_id=to, device_id_type=_MESH)


def _place_shards(place, shards, col_sharded):
    n = len(shards)
    steps = 4

    def body(place_ref, *refs):
        for src, dst in zip(refs[:n], refs[n:]):
            dst[...] = src[...].astype(dst.dtype)

    in_specs, out_specs, out_shape = [], [], []
    for w, col in zip(shards, col_sharded):
        r, cs = w.shape
        tr = r // steps
        in_specs.append(pl.BlockSpec((tr, cs), lambda i, p: (i, 0)))
        if col:
            out_specs.append(pl.BlockSpec((tr, cs), lambda i, p: (i, p[0])))
            out_shape.append(jax.ShapeDtypeStruct((r, 4 * cs), _ACT))
        else:
            out_specs.append(pl.BlockSpec((tr, cs), lambda i, p: (p[0] * steps + i, 0)))
            out_shape.append(jax.ShapeDtypeStruct((4 * r, cs), _ACT))
    grid_spec = pltpu.PrefetchScalarGridSpec(num_scalar_prefetch=1, grid=(steps,), in_specs=in_specs, out_specs=out_specs)
    return pl.pallas_call(body, name="place_shards", grid_spec=grid_spec, out_shape=out_shape,
                          compiler_params=_params(1))(place, *shards)


def _shard_of(ref, col_sharded, s):
    rows, cols = ref.shape
    if col_sharded:
        return ref.at[:, pl.ds(s * (cols // 4), cols // 4)]
    return ref.at[pl.ds(s * (rows // 4), rows // 4), :]


def _part_of(ref, col_sharded, whole, s, h):
    if whole:
        return _shard_of(ref, col_sharded, s)
    rows, cols = ref.shape
    if col_sharded:
        return ref.at[pl.ds(h * (rows // 2), rows // 2), pl.ds(s * (cols // 4), cols // 4)]
    return ref.at[pl.ds((2 * s + h) * (rows // 8), rows // 8), :]


def _allgather_start(bufs, col_sharded, whole, groups):
    n = len(bufs)
    ng = len(groups)

    def body(*refs):
        out = refs[n:2 * n]
        sems = refs[2 * n:]
        x, y, c = _place()
        for g, members in enumerate(groups):
            for i, w in enumerate(members):
                mine = _part_of(out[w], col_sharded[w], whole[w], 2 * x + y, c)
                for j, chip in enumerate(_other_chips(x, y)):
                    _remote(mine, mine, sems[2 * g].at[3 * i + j], sems[2 * g + 1].at[3 * i + j], (*chip, c)).start()

    sem_shapes = [pltpu.SemaphoreType.DMA((3 * len(m),)) for m in groups for _ in range(2)]
    outs = pl.pallas_call(
        body, name="allgather_start", in_specs=[_HBM] * n, out_specs=[_HBM] * n + [_SEM] * (2 * ng),
        out_shape=[pltpu.HBM(b.shape, b.dtype) for b in bufs] + sem_shapes,
        input_output_aliases={i: i for i in range(n)},
        compiler_params=pltpu.CompilerParams(has_side_effects=_EFFECT),
    )(*[pltpu.with_memory_space_constraint(b, pltpu.HBM) for b in bufs])
    return list(outs[:n]), [(outs[n + 2 * g], outs[n + 2 * g + 1]) for g in range(ng)]


def _allgather_relay(name, bufs, col_sharded, whole, sems, after):
    n = len(bufs)

    def body(*refs):
        buf = refs[:n]
        send, recv = refs[n], refs[n + 1]
        out = refs[n + 3:2 * n + 3]
        to_sibling, from_sibling = refs[2 * n + 3:]
        x, y, c = _place()
        for i in range(n):
            mine = _part_of(buf[i], col_sharded[i], whole[i], 2 * x + y, c)
            for j, chip in enumerate(_other_chips(x, y)):
                landed = _part_of(buf[i], col_sharded[i], whole[i], 2 * chip[0] + chip[1], c)
                cp = _remote(mine, landed, send.at[3 * i + j], recv.at[3 * i + j], (*chip, c))
                cp.wait_send()
                cp.wait_recv()
        for i in range(n):
            if not whole[i]:
                for j, chip in enumerate(_other_chips(x, y)):
                    landed = _part_of(out[i], col_sharded[i], False, 2 * chip[0] + chip[1], c)
                    _remote(landed, landed, to_sibling.at[3 * i + j], from_sibling.at[3 * i + j], (x, y, 1 - c)).start()

    outs = pl.pallas_call(
        body, name=name, in_specs=[_HBM] * n + [_SEM, _SEM, _ANY], out_specs=[_HBM] * n + [_SEM, _SEM],
        out_shape=[pltpu.HBM(b.shape, b.dtype) for b in bufs] + [pltpu.SemaphoreType.DMA((3 * n,))] * 2,
        input_output_aliases={i: i for i in range(n)},
        compiler_params=pltpu.CompilerParams(has_side_effects=_EFFECT),
    )(*bufs, *sems, after)
    return list(outs[:n]), (outs[n], outs[n + 1])


def _allgather_wait(name, bufs, col_sharded, whole, sems, after):
    n = len(bufs)

    def body(*refs):
        buf = refs[:n]
        to_sibling, from_sibling = refs[n], refs[n + 1]
        x, y, c = _place()
        for i in range(n):
            if not whole[i]:
                for j, chip in enumerate(_other_chips(x, y)):
                    sent = _part_of(buf[i], col_sharded[i], False, 2 * chip[0] + chip[1], c)
                    landed = _part_of(buf[i], col_sharded[i], False, 2 * chip[0] + chip[1], 1 - c)
                    cp = _remote(sent, landed, to_sibling.at[3 * i + j], from_sibling.at[3 * i + j], (x, y, 1 - c))
                    cp.wait_send()
                    cp.wait_recv()

    return pl.pallas_call(
        body, name=name, in_specs=[_HBM] * n + [_SEM, _SEM, _ANY], out_specs=[_HBM] * n,
        out_shape=[pltpu.HBM(b.shape, b.dtype) for b in bufs],
        input_output_aliases={i: i for i in range(n)},
        compiler_params=pltpu.CompilerParams(has_side_effects=_EFFECT),
    )(*bufs, *sems, after)


def _exchange_pair_halves(name, grads):
    nw = len(grads)

    def body(*refs):
        src = refs[:nw]
        got = refs[nw:2 * nw]
        send_sem, recv_sem = refs[2 * nw:]
        x, y, c = _place()
        sends = []
        for w in range(nw):
            for s in range(4):
                rc = _remote(src[w].at[2 * s + 1 - c], got[w].at[s], send_sem.at[4 * w + s], recv_sem.at[4 * w + s], (x, y, 1 - c))
                rc.start()
                sends.append(rc)
        for rc in sends:
            rc.wait_recv()
        for rc in sends:
            rc.wait_send()

    return pl.pallas_call(
        body, name=name, in_specs=[_ANY] * nw, out_specs=[_ANY] * nw,
        out_shape=[jax.ShapeDtypeStruct((4,) + g.shape[1:], g.dtype) for g in grads],
        scratch_shapes=[pltpu.SemaphoreType.DMA((4 * nw,)), pltpu.SemaphoreType.DMA((4 * nw,))],
    )(*grads)


def _chip_exchange_start(name, sums):
    nw = len(sums)
    lands = [lax.empty((3,) + s.shape[1:], s.dtype) for s in sums]

    def body(*refs):
        src = refs[2 * nw:3 * nw]
        got = refs[3 * nw:4 * nw]
        send, recv, token = refs[4 * nw:]
        x, y, c = _place()
        for w in range(nw):
            for j, chip in enumerate(_other_chips(x, y)):
                _remote(src[w].at[2 * chip[0] + chip[1]], got[w].at[j], send.at[3 * w + j], recv.at[3 * w + j], (*chip, c)).start()
        token[...] = jnp.zeros_like(token)

    outs = pl.pallas_call(
        body, name=name, in_specs=[_HBM] * (2 * nw), out_specs=[_HBM] * (2 * nw) + [_SEM, _SEM, _VMEM],
        out_shape=[pltpu.HBM(a.shape, a.dtype) for a in list(sums) + lands]
        + [pltpu.SemaphoreType.DMA((3 * nw,)), pltpu.SemaphoreType.DMA((3 * nw,)), jax.ShapeDtypeStruct((8, 128), f32)],
        input_output_aliases={i: i for i in range(2 * nw)},
        compiler_params=pltpu.CompilerParams(has_side_effects=_EFFECT),
    )(*[pltpu.with_memory_space_constraint(a, pltpu.HBM) for a in list(sums) + lands])
    return list(outs[:nw]), list(outs[nw:2 * nw]), (outs[2 * nw], outs[2 * nw + 1]), outs[2 * nw + 2]


def _chip_exchange_wait(name, sums, got, sems, after):
    nw = len(sums)

    def body(*refs):
        src = refs[:nw]
        land = refs[nw:2 * nw]
        send, recv = refs[2 * nw], refs[2 * nw + 1]
        x, y, c = _place()
        for w in range(nw):
            for j, chip in enumerate(_other_chips(x, y)):
                cp = _remote(src[w].at[2 * chip[0] + chip[1]], land[w].at[j], send.at[3 * w + j], recv.at[3 * w + j], (*chip, c))
                cp.wait_send()
                cp.wait_recv()

    outs = pl.pallas_call(
        body, name=name, in_specs=[_HBM] * (2 * nw) + [_SEM, _SEM, _ANY], out_specs=[_HBM] * (2 * nw),
        out_shape=[pltpu.HBM(a.shape, a.dtype) for a in list(sums) + list(got)],
        input_output_aliases={i: i for i in range(2 * nw)},
        compiler_params=pltpu.CompilerParams(has_side_effects=_EFFECT),
    )(*sums, *got, *sems, after)
    return list(outs[:nw]), list(outs[nw:])


def _swap_halves(finals):
    nw = len(finals)

    def body(*refs):
        buf = refs[nw:2 * nw]
        send_sem, recv_sem = refs[2 * nw:]
        x, y, c = _place()
        sends = []
        for w in range(nw):
            rc = _remote(buf[w].at[c], buf[w].at[c], send_sem.at[w], recv_sem.at[w], (x, y, 1 - c))
            rc.start()
            sends.append(rc)
        for w in range(nw):
            _remote(buf[w].at[1 - c], buf[w].at[1 - c], send_sem.at[w], recv_sem.at[w], (x, y, c)).wait_recv()
        for rc in sends:
            rc.wait_send()

    return pl.pallas_call(
        body, name="rs_swap_halves", in_specs=[_ANY] * nw, out_specs=[_ANY] * nw,
        out_shape=[jax.ShapeDtypeStruct(g.shape, g.dtype) for g in finals],
        input_output_aliases={i: i for i in range(nw)},
        scratch_shapes=[pltpu.SemaphoreType.DMA((nw,)), pltpu.SemaphoreType.DMA((nw,))],
    )(*finals)


def _allreduce_small(parts):
    n = len(parts)

    def body(*refs):
        src = refs[:n]
        out = refs[n:2 * n]
        slots = refs[2 * n:3 * n]
        send_sem, recv_sem = refs[3 * n:]
        x, y, c = _place()
        me = 4 * x + 2 * y + c
        flips = [(bx, by, bc) for bx in (0, 1) for by in (0, 1) for bc in (0, 1)][1:]
        sends = []
        for a in range(n):
            slots[a][me] = src[a][...]
        for k, (bx, by, bc) in enumerate(flips):
            to = (1 - x if bx else x, 1 - y if by else y, 1 - c if bc else c)
            for a in range(n):
                rc = _remote(src[a], slots[a].at[me], send_sem.at[k, a], recv_sem.at[k, a], to)
                rc.start()
                sends.append(rc)
        for rc in sends:
            rc.wait_recv()
        for a in range(n):
            s = slots[a][0]
            for d in range(1, 8):
                s = s + slots[a][d]
            out[a][...] = s
        for rc in sends:
            rc.wait_send()

    return pl.pallas_call(
        body, name="allreduce_small", in_specs=[_VMEM] * n, out_specs=[_VMEM] * n,
        out_shape=[jax.ShapeDtypeStruct(p.shape, f32) for p in parts],
        scratch_shapes=[pltpu.VMEM((8,) + p.shape, f32) for p in parts] + [pltpu.SemaphoreType.DMA((7, n)), pltpu.SemaphoreType.DMA((7, n))],
        compiler_params=pltpu.CompilerParams(vmem_limit_bytes=_VMEM_LIMIT_BYTES),
    )(*parts)


def _local_step(x, mem, tgt, g_mix, g_xattn, g_mem, g_ffn, g_final, cb, lg, lb, pw, ps, fb, relay, weights, reduce, n_seq, seq, n_mem):
    t, d = x.shape
    f = fb.shape[1] // 2
    c = cb.shape[1]
    h1 = _rms_fwd("norm_mix", x, g_mix)
    relay(0, h1)
    w_in, cw, fw = weights(0, h1)
    u = _mm_nn("proj_in", h1, w_in, _ACT, w_in.shape[1])
    relay(1, u)
    y, hc = _mix_fwd(u, cw, cb, lg, lb, pw, ps, seq)
    w_out, w_q, w_kv, w_o = weights(1, y)
    x1, h2 = _proj_residual_norm("proj_out", y, w_out, x, g_xattn)
    q = _mm_nn("proj_q", h2, w_q, _ACT, d)
    mem_n = _rms_fwd("norm_mem", mem, g_mem)
    kv = _mm_nn("proj_kv", mem_n, w_kv, _ACT, 2 * d)
    o = _attn_fwd(q, kv, n_seq, seq, n_mem)
    relay(2, o)
    x2, h3 = _proj_residual_norm("proj_o", o, w_o, x1, g_ffn)
    w_up, w_down = weights(2, h3)
    up = _mm_nn("proj_up", h3, w_up, _ACT, f, split_out=True)
    a = _ffn_gate_fwd(up, fw, fb, seq)
    dx3, dx3b, dg_final, loss = _proj_loss_bwd("proj_down", a, w_down, x2, g_final, tgt)
    da = _mm_nt("d_act", dx3b, w_down, _ACT)
    gw_down = _mm_tn_rows("dw_down", a, dx3b, f // 2, d // 2)
    dup, sums_g, sums_v = _ffn_gate_bwd(up, da, fw, fb, seq)
    gw_up = _mm_tn_pieces("dw_up", h3, dup, f // 2, t)
    token = reduce(0, [gw_down.reshape(8, -1, d), gw_up])
    dx2, dx2b, dg_ffn = _dproj_rms_bwd("d_h3", dup, w_up, x2, g_ffn + token, dx3)
    do = _mm_nt("d_o", dx2b, w_o, _ACT)
    gw_o = _mm_tn_rows("dw_o", o, dx2b, d, d // 2)
    dq, dk, dv = _attn_bwd(q, kv, do, n_seq, seq, n_mem)
    dkv = jnp.concatenate([dk, dv], axis=1)
    gw_q = _mm_tn_rows("dw_q", h2, dq, d, d // 2)
    gw_kv = _mm_tn_pieces("dw_kv", mem_n, dkv, d // 2, mem.shape[0])
    dmem_n = _mm_nt("d_mem_n", dkv, w_kv, f32)
    dg_mem = _rms_gain_grad("norm_mem_bwd", mem, dmem_n)
    dx1, dx1b, dg_xattn = _dproj_rms_bwd("d_h2", dq, w_q, x1, g_xattn, dx2)
    dy = _mm_nt("d_y", dx1b, w_out, _ACT)
    gw_out = _mm_tn_rows("dw_out", y, dx1b, d, d // 2)
    token = reduce(1, [gw_o.reshape(8, -1, d), gw_q.reshape(8, -1, d), gw_kv, gw_out.reshape(8, -1, d)])
    dhc, sums_norm = _mix_bwd_norm(hc, dy, lg + token, lb, seq)
    du, d_cw, d_ps, d_pw = _mix_bwd_taps(u, dhc, dy, cw, pw, ps, seq)
    gw_in = _mm_tn_pieces("dw_in", h1, du, c * 3 // 4, t)
    reduce(2, [gw_in])
    grad_x, dg_mix = _dproj_rms_bwd("d_h1", du, w_in, x, g_mix, dx1, storage_copy=False)
    zero_row = jnp.zeros((1, d), f32)
    gains = jnp.concatenate([dg_mix, dg_xattn, dg_mem, dg_ffn, dg_final, jnp.pad(loss, ((0, 0), (0, d - 1))), zero_row, zero_row], axis=0)
    conv_rows = jnp.concatenate([sums_norm[2:3], sums_norm[0:1], sums_norm[1:2], d_ps[0:1], jnp.zeros((4, c), f32)], axis=0)
    ffn_rows = jnp.concatenate([sums_g, sums_v], axis=1)
    small = [gains, conv_rows, d_pw.reshape(-1, d_pw.shape[-1]), ffn_rows, d_cw]
    return grad_x, small


def kernel(x, mem, norm_mix_g, w_in, conv_dw_w, conv_dw_b, conv_ln_g, conv_ln_b, pool_w, pool_scale, w_out, norm_xattn_g, norm_mem_g, w_q, w_kv, w_o, norm_ffn_g, w_up, ffn_dw_w, ffn_dw_b, w_down, norm_final_g, loss_target, m_norm_mix_g, m_w_in, m_conv_dw_w, m_conv_dw_b, m_conv_ln_g, m_conv_ln_b, m_pool_w, m_pool_scale, m_w_out, m_norm_xattn_g, m_norm_mem_g, m_w_q, m_w_kv, m_w_o, m_norm_ffn_g, m_w_up, m_ffn_dw_w, m_ffn_dw_b, m_w_down, m_norm_final_g, v_norm_mix_g, v_w_in, v_conv_dw_w, v_conv_dw_b, v_conv_ln_g, v_conv_ln_b, v_pool_w, v_pool_scale, v_w_out, v_norm_xattn_g, v_norm_mem_g, v_w_q, v_w_kv, v_w_o, v_norm_ffn_g, v_w_up, v_ffn_dw_w, v_ffn_dw_b, v_w_down, v_norm_final_g):
    n_seq, seq, d = x.shape
    n_mem = mem.shape[1]
    chip = 2 * lax.axis_index("x") + lax.axis_index("y")

    place = jnp.stack([chip, lax.axis_index("c")]).astype(jnp.int32)

    col_w = [w_in, w_kv, w_up]
    row_w = [w_out, w_q, w_o, w_down]
    col_flags = [True] * 3 + [False] * 4 + [True] * 2
    kw = conv_dw_w.shape[1]

    def padded_in_place(shard, rows):
        full = jnp.zeros((rows, 4 * shard.shape[1]), shard.dtype)
        return lax.dynamic_update_slice(full, shard, (0, chip * shard.shape[1]))

    bufs = list(_place_shards(place, [w[0] for w in col_w + row_w], col_flags[:7]))
    bufs += [padded_in_place(conv_dw_w[0], _HALO), padded_in_place(ffn_dw_w[0], 8)]
    groups = [[0, 7, 8], [3, 4, 1, 5], [2, 6]]
    whole = [False] * 7 + [True] * 2
    bufs, sems = _allgather_start(bufs, col_flags, whole, groups)
    relayed = {}

    def relay(g, after):
        members = groups[g]
        relayed[g] = _allgather_relay("allgather_relay_%d" % g, [bufs[i] for i in members], [col_flags[i] for i in members],
                                      [whole[i] for i in members], sems[g], after)

    def weights(g, after):
        members = groups[g]
        group_bufs, sibling_sems = relayed[g]
        return _allgather_wait("allgather_wait_%d" % g, group_bufs, [col_flags[i] for i in members],
                               [whole[i] for i in members], sibling_sems, after)

    names = ["w_in", "w_kv", "w_up", "w_out", "w_q", "w_o", "w_down"]
    reduce_groups = [["w_down", "w_up"], ["w_o", "w_q", "w_kv", "w_out"], ["w_in"]]
    in_flight = {}

    def reduce(g, grads):
        members = reduce_groups[g]
        got = _exchange_pair_halves("rs_pair_exchange_%d" % g, grads)
        sums = [_sum_pairs("rs_pair_sum_" + n, place, a, b) for n, a, b in zip(members, grads, got)]
        sums, lands, rs_sems, token = _chip_exchange_start("rs_chip_start_%d" % g, sums)
        in_flight[g] = (sums, lands, rs_sems)
        return token[0:1, 0:1]

    grad_x, small = _local_step(
        x.reshape(n_seq * seq, d), mem.reshape(n_seq * n_mem, d), loss_target.reshape(n_seq * seq, d),
        norm_mix_g, norm_xattn_g, norm_mem_g, norm_ffn_g, norm_final_g.reshape(1, d),
        conv_dw_b, conv_ln_g, conv_ln_b, pool_w[0], pool_scale, ffn_dw_b, relay, weights, reduce, n_seq, seq, n_mem)

    finals = {}
    for g, members in enumerate(reduce_groups):
        sums, lands, rs_sems = in_flight[g]
        sums, lands = _chip_exchange_wait("rs_chip_wait_%d" % g, sums, lands, rs_sems, grad_x)
        for n, a, b in zip(members, sums, lands):
            finals[n] = _sum_four("rs_chip_sum_" + n, place, a, b)
    shard_grads = _swap_halves([finals[n] for n in names])

    gains, conv_rows, d_pw, ffn_rows, d_cw = _allreduce_small(small)
    loss = gains[5, 0]

    outs = {}
    big_w = dict(zip(names, col_w + row_w))
    big_m = dict(w_in=m_w_in, w_kv=m_w_kv, w_up=m_w_up, w_out=m_w_out, w_q=m_w_q, w_o=m_w_o, w_down=m_w_down)
    big_v = dict(w_in=v_w_in, w_kv=v_w_kv, w_up=v_w_up, w_out=v_w_out, w_q=v_w_q, w_o=v_w_o, w_down=v_w_down)
    for n, g in zip(names, shard_grads):
        w = big_w[n]
        g2 = g.reshape(w.shape[1], w.shape[2])
        delta, new_m, new_v = _adamw_shard("adamw_" + n, w, g2, big_m[n], big_v[n])
        outs[n] = (g2.reshape(w.shape), delta, new_m, new_v)

    f2 = ffn_dw_b.shape[1]
    cs_c = conv_dw_w.shape[2]
    cs_f = ffn_dw_w.shape[2]
    g_cw = lax.dynamic_slice(d_cw, (0, chip * cs_c), (kw, cs_c)).reshape(conv_dw_w.shape)
    g_fw = lax.dynamic_slice(ffn_rows, (1, chip * cs_f), (ffn_dw_w.shape[1], cs_f)).reshape(ffn_dw_w.shape)
    small_params = [
        ("norm_mix_g", norm_mix_g, gains[0:1], m_norm_mix_g, v_norm_mix_g),
        ("conv_dw_w", conv_dw_w, g_cw, m_conv_dw_w, v_conv_dw_w),
        ("conv_dw_b", conv_dw_b, conv_rows[0:1], m_conv_dw_b, v_conv_dw_b),
        ("conv_ln_g", conv_ln_g, conv_rows[1:2], m_conv_ln_g, v_conv_ln_g),
        ("conv_ln_b", conv_ln_b, conv_rows[2:3], m_conv_ln_b, v_conv_ln_b),
        ("pool_w", pool_w, d_pw.reshape(pool_w.shape), m_pool_w, v_pool_w),
        ("pool_scale", pool_scale, conv_rows[3:4], m_pool_scale, v_pool_scale),
        ("norm_xattn_g", norm_xattn_g, gains[1:2], m_norm_xattn_g, v_norm_xattn_g),
        ("norm_mem_g", norm_mem_g, gains[2:3], m_norm_mem_g, v_norm_mem_g),
        ("norm_ffn_g", norm_ffn_g, gains[3:4], m_norm_ffn_g, v_norm_ffn_g),
        ("ffn_dw_w", ffn_dw_w, g_fw, m_ffn_dw_w, v_ffn_dw_w),
        ("ffn_dw_b", ffn_dw_b, ffn_rows[0:1, :f2], m_ffn_dw_b, v_ffn_dw_b),
        ("norm_final_g", norm_final_g.reshape(1, d), gains[4:5], m_norm_final_g.reshape(1, d), v_norm_final_g.reshape(1, d)),
    ]
    quads = []
    for _, w, g, m, v in small_params:
        shape2 = (-1, w.shape[-1])
        quads.append((w.reshape(shape2), g.reshape(shape2), m.reshape(shape2), v.reshape(shape2)))
    for (n, w, g, _, _), (delta, new_m, new_v) in zip(small_params, _adamw_small(quads)):
        shape = norm_final_g.shape if n == "norm_final_g" else w.shape
        outs[n] = (g.reshape(shape), delta.reshape(shape), new_m.reshape(shape), new_v.reshape(shape))

    order = ["norm_mix_g", "w_in", "conv_dw_w", "conv_dw_b", "conv_ln_g", "conv_ln_b", "pool_w", "pool_scale", "w_out",
             "norm_xattn_g", "norm_mem_g", "w_q", "w_kv", "w_o", "norm_ffn_g", "w_up", "ffn_dw_w", "ffn_dw_b", "w_down",
             "norm_final_g"]
    return (loss, grad_x.reshape(x.shape), *[outs[n][0] for n in order], *[outs[n][1] for n in order],
            *[outs[n][2] for n in order], *[outs[n][3] for n in order])
```

```python
import functools

import jax
import jax.numpy as jnp
from jax import lax
from jax.experimental import pallas as pl
from jax.experimental.pallas import tpu as pltpu

f32 = jnp.float32
_ACT = jnp.bfloat16

EPS = 1e-6
POOL_WINDOWS = (2, 4, 8, 16)
XATTN_HEADS = 4
ADAM_LR = 0.001
ADAM_B1 = 0.9
ADAM_B2 = 0.999
ADAM_EPS = 1e-08
ADAM_WD = 0.01
ADAM_STEP = 10

_VMEM_LIMIT_BYTES = 56 * 1024 * 1024
_MESH = pl.DeviceIdType.MESH
_ANY = pl.BlockSpec(memory_space=pl.ANY)
_VMEM = pl.BlockSpec(memory_space=pltpu.VMEM)
_HBM = pl.BlockSpec(memory_space=pltpu.HBM)
_SEM = pl.BlockSpec(memory_space=pltpu.SEMAPHORE)
_EFFECT = pltpu.SideEffectType.DATAFLOW_SIDE_EFFECTING

_NN = (((1,), (0,)), ((), ()))
_NT = (((1,), (1,)), ((), ()))
_TN = (((0,), (0,)), ((), ()))


def _params(n_grid):
    return pltpu.CompilerParams(dimension_semantics=("arbitrary",) * n_grid, vmem_limit_bytes=_VMEM_LIMIT_BYTES)


def _sigmoid(v):
    return 1.0 / (1.0 + jnp.exp(-v))


def _dot(a, b, dims):
    return lax.dot_general(a, b, dims, preferred_element_type=f32)


def _mm(name, a, b, *, dims, grid, a_spec, b_spec, o_spec, out_shape, nk, acc_shape=None, res=None, res_spec=None):
    def body(*refs):
        if res is None:
            a_ref, b_ref, o_ref, *scratch = refs
            r_ref = None
        else:
            a_ref, b_ref, r_ref, o_ref, *scratch = refs
        p = _dot(a_ref[...], b_ref[...], dims)

        def finish(v):
            if r_ref is not None:
                v = v + r_ref[...]
            o_ref[...] = v.astype(o_ref.dtype)

        if nk == 1:
            finish(p)
        else:
            acc = scratch[0]
            k = pl.program_id(2)

            @pl.when(k == 0)
            def _():
                acc[...] = p

            @pl.when(k > 0)
            def _():
                acc[...] += p

            @pl.when(k == nk - 1)
            def _():
                finish(acc[...])

    ins = [a, b] + ([] if res is None else [res])
    specs = [a_spec, b_spec] + ([] if res is None else [res_spec])
    return pl.pallas_call(
        body, name=name, grid=grid, in_specs=specs, out_specs=o_spec, out_shape=out_shape,
        scratch_shapes=[pltpu.VMEM(acc_shape, f32)] if nk > 1 else [], compiler_params=_params(3),
    )(*ins)


def _row_tile(m):
    return min(512, m)


def _mm_nn(name, a, b, out_dtype, tn, res=None, split_out=False):
    m, k = a.shape
    n = b.shape[1]
    tm = _row_tile(m)
    if split_out:
        out_shape = jax.ShapeDtypeStruct((n // tn, m, tn), out_dtype)
        o_spec = pl.BlockSpec((None, tm, tn), lambda j, i, kk: (j, i, 0))
    else:
        out_shape = jax.ShapeDtypeStruct((m, n), out_dtype)
        o_spec = pl.BlockSpec((tm, tn), lambda j, i, kk: (i, j))
    return _mm(
        name, a, b, dims=_NN, grid=(n // tn, m // tm, 1), nk=1,
        a_spec=pl.BlockSpec((tm, k), lambda j, i, kk: (i, 0)),
        b_spec=pl.BlockSpec((k, tn), lambda j, i, kk: (0, j)),
        o_spec=o_spec, out_shape=out_shape, res=res,
        res_spec=pl.BlockSpec((tm, tn), lambda j, i, kk: (i, j)),
    )


def _mm_nt(name, a, b, out_dtype):
    n, kc = b.shape
    m = a.shape[0]
    tm = _row_tile(m)
    return _mm(
        name, a, b, dims=_NT, grid=(m // tm, 1, 1), nk=1,
        a_spec=pl.BlockSpec((tm, kc), lambda i, j, k: (i, 0)), b_spec=pl.BlockSpec((n, kc), lambda i, j, k: (0, 0)),
        o_spec=pl.BlockSpec((tm, n), lambda i, j, k: (i, 0)),
        out_shape=jax.ShapeDtypeStruct((m, n), out_dtype),
    )


def _mm_tn_rows(name, a, b, tka, tn):
    m, ka = a.shape
    nb = b.shape[1]
    return _mm(
        name, a, b, dims=_TN, grid=(ka // tka, nb // tn, 1), nk=1,
        a_spec=pl.BlockSpec((m, tka), lambda i, j, k: (0, i)),
        b_spec=pl.BlockSpec((m, tn), lambda i, j, k: (0, j)),
        o_spec=pl.BlockSpec((tka, tn), lambda i, j, k: (i, j)),
        out_shape=jax.ShapeDtypeStruct((ka, nb), _ACT),
    )


def _mm_tn_pieces(name, a, b, cs, tt):
    m, ka = a.shape
    nk = m // tt
    if b.ndim == 3:
        b_spec = pl.BlockSpec((None, tt, cs), lambda i, j, k: (j // 2, k, j % 2))
    else:
        b_spec = pl.BlockSpec((tt, cs), lambda i, j, k: (k, j))
    return _mm(
        name, a, b, dims=_TN, grid=(2, 4, nk), nk=nk, acc_shape=(ka // 2, cs),
        a_spec=pl.BlockSpec((tt, ka // 2), lambda i, j, k: (k, i)), b_spec=b_spec,
        o_spec=pl.BlockSpec((None, ka // 2, cs), lambda i, j, k: (2 * j + i, 0, 0)),
        out_shape=jax.ShapeDtypeStruct((8, ka // 2, cs), _ACT),
    )


def _rms_fwd(name, x, g):
    t, d = x.shape
    tm = _row_tile(t)

    def body(x_ref, g_ref, h_ref):
        xv = x_ref[...]
        r = lax.rsqrt(jnp.mean(xv * xv, axis=-1, keepdims=True) + EPS)
        h_ref[...] = (xv * r * g_ref[...]).astype(h_ref.dtype)

    return pl.pallas_call(
        body, name=name, grid=(t // tm,),
        in_specs=[pl.BlockSpec((tm, d), lambda i: (i, 0)), pl.BlockSpec((1, d), lambda i: (0, 0))],
        out_specs=pl.BlockSpec((tm, d), lambda i: (i, 0)), out_shape=jax.ShapeDtypeStruct((t, d), _ACT),
        compiler_params=_params(1),
    )(x, g)


def _fused_rows(name, a, b, product, a_spec, tm, extras, extra_specs, out_shape, out_specs, epilogue):
    ne = len(extras)

    def body(a_ref, b_ref, *refs):
        epilogue(product(a_ref, b_ref), refs[:ne], refs[ne:])

    m = extras[0].shape[0]
    return pl.pallas_call(
        body, name=name, grid=(m // tm,),
        in_specs=[a_spec, pl.BlockSpec(b.shape, lambda i: (0, 0)), *extra_specs], out_specs=out_specs, out_shape=out_shape,
        compiler_params=_params(1),
    )(a, b, *extras)


def _proj_residual_norm(name, a, b, res, g):
    m, k = a.shape
    d = b.shape[1]
    tm = _row_tile(m)

    def epilogue(p, ins, outs):
        xv = p + ins[0][...]
        outs[0][...] = xv
        r = lax.rsqrt(jnp.mean(xv * xv, axis=-1, keepdims=True) + EPS)
        outs[1][...] = (xv * r * ins[1][...]).astype(outs[1].dtype)

    row = pl.BlockSpec((tm, d), lambda i: (i, 0))
    return _fused_rows(
        name, a, b, lambda a_ref, b_ref: _dot(a_ref[...], b_ref[...], _NN), pl.BlockSpec((tm, k), lambda i: (i, 0)), tm,
        [res, g], [row, pl.BlockSpec((1, d), lambda i: (0, 0))],
        [jax.ShapeDtypeStruct((m, d), f32), jax.ShapeDtypeStruct((m, d), _ACT)], [row, row], epilogue)


def _dproj_rms_bwd(name, a, b, x, g, dres, storage_copy=True):
    m, d = x.shape
    if a.ndim == 3:
        nh, _, kh = a.shape
        tm = min(256, m)
        a_spec = pl.BlockSpec((nh, tm, kh), lambda i: (0, i, 0))

        def product(a_ref, b_ref):
            p = _dot(a_ref[0], b_ref[:, 0:kh], _NT)
            for h in range(1, nh):
                p = p + _dot(a_ref[h], b_ref[:, h * kh:(h + 1) * kh], _NT)
            return p
    else:
        tm = _row_tile(m)
        a_spec = pl.BlockSpec((tm, a.shape[1]), lambda i: (i, 0))

        def product(a_ref, b_ref):
            return _dot(a_ref[...], b_ref[...], _NT)

    def epilogue(dhv, ins, outs):
        x_ref, g_ref, dres_ref = ins
        dg_ref = outs[-1]

        @pl.when(pl.program_id(0) == 0)
        def _():
            dg_ref[...] = jnp.zeros_like(dg_ref)

        xv = x_ref[...]
        r = lax.rsqrt(jnp.mean(xv * xv, axis=-1, keepdims=True) + EPS)
        xn = xv * r
        dxn = dhv * g_ref[...]
        dx = r * (dxn - xn * jnp.mean(dxn * xn, axis=-1, keepdims=True)) + dres_ref[...]
        outs[0][...] = dx
        if storage_copy:
            outs[1][...] = dx.astype(outs[1].dtype)
        dg_ref[...] += jnp.sum(dhv * xn, axis=0, keepdims=True)

    row = pl.BlockSpec((tm, d), lambda i: (i, 0))
    vec = pl.BlockSpec((1, d), lambda i: (0, 0))
    copies = [jax.ShapeDtypeStruct((m, d), _ACT)] if storage_copy else []
    return _fused_rows(
        name, a, b, product, a_spec, tm, [x, g, dres], [row, vec, row],
        [jax.ShapeDtypeStruct((m, d), f32)] + copies + [jax.ShapeDtypeStruct((1, d), f32)],
        [row] * (1 + len(copies)) + [vec], epilogue)


def _proj_loss_bwd(name, a, b, res, g, tgt):
    m, k = a.shape
    d = b.shape[1]
    tm = _row_tile(m)

    def epilogue(p, ins, outs):
        res_ref, g_ref, t_ref = ins
        dx_ref, dxb_ref, dg_ref, loss_ref = outs

        @pl.when(pl.program_id(0) == 0)
        def _():
            dg_ref[...] = jnp.zeros_like(dg_ref)
            loss_ref[...] = jnp.zeros_like(loss_ref)

        xv = p + res_ref[...]
        gv = g_ref[...]
        r = lax.rsqrt(jnp.mean(xv * xv, axis=-1, keepdims=True) + EPS)
        xn = xv * r
        err = xn * gv - t_ref[...]
        loss_ref[...] += 0.5 * jnp.sum(jnp.mean(err * err, axis=-1, keepdims=True), axis=0, keepdims=True)
        dout = err * (1.0 / d)
        dxn = dout * gv
        dx = r * (dxn - xn * jnp.mean(dxn * xn, axis=-1, keepdims=True))
        dx_ref[...] = dx
        dxb_ref[...] = dx.astype(dxb_ref.dtype)
        dg_ref[...] += jnp.sum(dout * xn, axis=0, keepdims=True)

    row = pl.BlockSpec((tm, d), lambda i: (i, 0))
    vec = pl.BlockSpec((1, d), lambda i: (0, 0))
    return _fused_rows(
        name, a, b, lambda a_ref, b_ref: _dot(a_ref[...], b_ref[...], _NN), pl.BlockSpec((tm, k), lambda i: (i, 0)), tm,
        [res, g, tgt], [row, vec, row],
        [jax.ShapeDtypeStruct((m, d), f32), jax.ShapeDtypeStruct((m, d), _ACT), jax.ShapeDtypeStruct((1, d), f32),
         jax.ShapeDtypeStruct((1, 1), f32)],
        [row, row, vec, pl.BlockSpec((1, 1), lambda i: (0, 0))], epilogue)


def _rms_gain_grad(name, x, dh):
    t, d = x.shape
    tm = _row_tile(t)

    def body(x_ref, dh_ref, dg_ref):
        @pl.when(pl.program_id(0) == 0)
        def _():
            dg_ref[...] = jnp.zeros_like(dg_ref)

        xv = x_ref[...]
        r = lax.rsqrt(jnp.mean(xv * xv, axis=-1, keepdims=True) + EPS)
        dg_ref[...] += jnp.sum(dh_ref[...] * (xv * r), axis=0, keepdims=True)

    row = pl.BlockSpec((tm, d), lambda i: (i, 0))
    return pl.pallas_call(
        body, name=name, grid=(t // tm,), in_specs=[row, row], out_specs=pl.BlockSpec((1, d), lambda i: (0, 0)),
        out_shape=jax.ShapeDtypeStruct((1, d), f32), compiler_params=_params(1),
    )(x, dh)


_CONV_ROWS = 256
_CHUNK = 64
_HALO = 32


def _pool_counts(pos, w):
    return jnp.minimum(pos + 1.0, float(w))


def _mix_fwd(u, cw, cb, lg, lb, pw, ps, seq):
    t, c3 = u.shape
    c = c3 // 3
    kw = 31
    tm = min(_CONV_ROWS, seq)
    tps = seq // tm
    gd = c // len(POOL_WINDOWS)

    def body(u_ref, uh_ref, cw_ref, cb_ref, lg_ref, lb_ref, pw_ref, ps_ref, y_ref, hc_ref, hgbuf, pbuf):
        i = pl.program_id(0)
        keep = jnp.where(i % tps == 0, 0.0, 1.0)
        um = u_ref[...].astype(f32)
        uh = uh_ref[...].astype(f32) * keep
        hgbuf[0:_HALO, :] = uh[:, 0:c] * _sigmoid(uh[:, c:2 * c])
        hgbuf[_HALO:_HALO + tm, :] = um[:, 0:c] * _sigmoid(um[:, c:2 * c])
        pbuf[0:_HALO, :] = uh[:, 2 * c:]
        pbuf[_HALO:_HALO + tm, :] = um[:, 2 * c:]
        for r0 in range(0, tm, _CHUNK):
            acc = jnp.broadcast_to(cb_ref[...], (_CHUNK, c))
            for k in range(kw):
                off = r0 + _HALO - (kw - 1) + k
                acc = acc + cw_ref[k:k + 1, :] * hgbuf[off:off + _CHUNK, :]
            hc_ref[r0:r0 + _CHUNK, :] = acc
            mu = jnp.mean(acc, axis=-1, keepdims=True)
            xc = acc - mu
            var = jnp.mean(xc * xc, axis=-1, keepdims=True)
            hl = xc * lax.rsqrt(var + EPS) * lg_ref[...] + lb_ref[...]
            y_ref[r0:r0 + _CHUNK, 0:c] = (hl * _sigmoid(hl)).astype(y_ref.dtype)
        pos = ((i % tps) * tm).astype(f32) + lax.broadcasted_iota(jnp.int32, (tm, 1), 0).astype(f32)
        for gi, w in enumerate(POOL_WINDOWS):
            sl = slice(gi * gd, (gi + 1) * gd)
            v = pbuf[_HALO:_HALO + tm, sl]
            s = v
            for j in range(1, w):
                s = s + pbuf[_HALO - j:_HALO - j + tm, sl]
            pooled = s / _pool_counts(pos, w) - v
            mixed = _dot(pooled.astype(_ACT), pw_ref[gi].astype(_ACT), _NN)
            y_ref[:, c + gi * gd:c + (gi + 1) * gd] = (mixed * ps_ref[:, sl]).astype(y_ref.dtype)

    hb = tm // _HALO
    full = lambda shape: pl.BlockSpec(shape, lambda i: (0,) * len(shape))
    return pl.pallas_call(
        body, name="mix_fwd", grid=(t // tm,),
        in_specs=[pl.BlockSpec((tm, c3), lambda i: (i, 0)),
                  pl.BlockSpec((_HALO, c3), lambda i: (jnp.maximum(i * hb - 1, 0), 0)),
                  full((_HALO, c)), full((1, c)), full((1, c)), full((1, c)), full((len(POOL_WINDOWS), gd, gd)), full((1, c))],
        out_specs=[pl.BlockSpec((tm, 2 * c), lambda i: (i, 0)), pl.BlockSpec((tm, c), lambda i: (i, 0))],
        out_shape=[jax.ShapeDtypeStruct((t, 2 * c), _ACT), jax.ShapeDtypeStruct((t, c), f32)],
        scratch_shapes=[pltpu.VMEM((_HALO + tm, c), f32), pltpu.VMEM((_HALO + tm, c), f32)],
        compiler_params=_params(1),
    )(u, u, cw, cb, lg, lb, pw, ps)


def _mix_bwd_norm(hc, dy, lg, lb, seq):
    t, c = hc.shape
    tm = min(_CONV_ROWS, seq)

    def body(hc_ref, dy_ref, lg_ref, lb_ref, dhc_ref, sums_ref):
        @pl.when(pl.program_id(0) == 0)
        def _():
            sums_ref[...] = jnp.zeros_like(sums_ref)

        hcv = hc_ref[...]
        mu = jnp.mean(hcv, axis=-1, keepdims=True)
        xc = hcv - mu
        rstd = lax.rsqrt(jnp.mean(xc * xc, axis=-1, keepdims=True) + EPS)
        n = xc * rstd
        hl = n * lg_ref[...] + lb_ref[...]
        sg = _sigmoid(hl)
        dhl = dy_ref[...].astype(f32) * (sg * (1.0 + hl * (1.0 - sg)))
        dn = dhl * lg_ref[...]
        dhc = rstd * (dn - jnp.mean(dn, axis=-1, keepdims=True) - n * jnp.mean(dn * n, axis=-1, keepdims=True))
        dhc_ref[...] = dhc
        sums_ref[0:1, :] += jnp.sum(dhl * n, axis=0, keepdims=True)
        sums_ref[1:2, :] += jnp.sum(dhl, axis=0, keepdims=True)
        sums_ref[2:3, :] += jnp.sum(dhc, axis=0, keepdims=True)

    row = pl.BlockSpec((tm, c), lambda i: (i, 0))
    vec = pl.BlockSpec((1, c), lambda i: (0, 0))
    return pl.pallas_call(
        body, name="mix_bwd_norm", grid=(t // tm,), in_specs=[row, row, vec, vec],
        out_specs=[row, pl.BlockSpec((8, c), lambda i: (0, 0))],
        out_shape=[jax.ShapeDtypeStruct((t, c), f32), jax.ShapeDtypeStruct((8, c), f32)],
        compiler_params=_params(1),
    )(hc, dy, lg, lb)


def _mix_bwd_taps(u, dhc, dy, cw, pw, ps, seq):
    t, c3 = u.shape
    c = c3 // 3
    kw = 31
    tm = min(_CONV_ROWS, seq)
    tps = seq // tm
    ng = len(POOL_WINDOWS)
    gd = c // ng
    nh = 16

    def body(u_ref, uh_ref, dhc_ref, dhcn_ref, dy_ref, dyn_ref, cw_ref, pw_ref, ps_ref,
             du_ref, dcw_ref, dps_ref, dpw_ref, hgbuf, dcbuf, pbuf, dpbuf):
        i = pl.program_id(0)
        keep_prev = jnp.where(i % tps == 0, 0.0, 1.0)
        keep_next = jnp.where(i % tps == tps - 1, 0.0, 1.0)

        @pl.when(i == 0)
        def _():
            dcw_ref[...] = jnp.zeros_like(dcw_ref)
            dps_ref[...] = jnp.zeros_like(dps_ref)
            dpw_ref[...] = jnp.zeros_like(dpw_ref)

        uh = uh_ref[...].astype(f32) * keep_prev
        hgbuf[0:_HALO, :] = uh[:, 0:c] * _sigmoid(uh[:, c:2 * c])
        pbuf[0:_HALO, :] = uh[:, 2 * c:]
        um = u_ref[...].astype(f32)
        hgbuf[_HALO:_HALO + tm, :] = um[:, 0:c] * _sigmoid(um[:, c:2 * c])
        pbuf[_HALO:_HALO + tm, :] = um[:, 2 * c:]
        dcbuf[0:tm, :] = dhc_ref[...]
        dcbuf[tm:tm + _HALO, :] = dhcn_ref[...] * keep_next
        tap_sums = [None] * kw
        for r0 in range(0, tm, _CHUNK):
            dh = dcbuf[r0:r0 + _CHUNK, :]
            acc = jnp.zeros((_CHUNK, c), f32)
            for k in range(kw):
                off = r0 + _HALO - (kw - 1) + k
                part = jnp.sum(dh * hgbuf[off:off + _CHUNK, :], axis=0, keepdims=True)
                tap_sums[k] = part if tap_sums[k] is None else tap_sums[k] + part
                fwd = r0 + (kw - 1) - k
                acc = acc + cw_ref[k:k + 1, :] * dcbuf[fwd:fwd + _CHUNK, :]
            val = u_ref[r0:r0 + _CHUNK, 0:c].astype(f32)
            sg = _sigmoid(u_ref[r0:r0 + _CHUNK, c:2 * c].astype(f32))
            du_ref[r0:r0 + _CHUNK, 0:c] = (acc * sg).astype(du_ref.dtype)
            du_ref[r0:r0 + _CHUNK, c:2 * c] = (acc * val * sg * (1.0 - sg)).astype(du_ref.dtype)
        for k in range(kw):
            dcw_ref[k:k + 1, :] += tap_sums[k]
        base = ((i % tps) * tm).astype(f32)
        pos = base + lax.broadcasted_iota(jnp.int32, (tm, 1), 0).astype(f32)
        pos_next = base + float(tm) + lax.broadcasted_iota(jnp.int32, (nh, 1), 0).astype(f32)
        for gi, w in enumerate(POOL_WINDOWS):
            sl = slice(gi * gd, (gi + 1) * gd)
            v = pbuf[_HALO:_HALO + tm, sl]
            s = v
            for j in range(1, w):
                s = s + pbuf[_HALO - j:_HALO - j + tm, sl]
            cnt = _pool_counts(pos, w)
            pooled = (s / cnt - v).astype(_ACT)
            pwg = pw_ref[gi].astype(_ACT)
            mixed = _dot(pooled, pwg, _NN)
            dyp = dy_ref[:, sl].astype(f32)
            dps_ref[0:1, sl] += jnp.sum(dyp * mixed, axis=0, keepdims=True)
            dmix = (dyp * ps_ref[:, sl]).astype(_ACT)
            dpw_ref[gi] += _dot(pooled, dmix, _TN)
            dmix_next = (dyn_ref[:, sl].astype(f32) * ps_ref[:, sl] * keep_next).astype(_ACT)
            dpool = _dot(dmix, pwg, _NT)
            dpbuf[0:tm, sl] = dpool / cnt
            dpbuf[tm:tm + nh, sl] = _dot(dmix_next, pwg, _NT) / _pool_counts(pos_next, w)
            acc = -dpool
            for j in range(w):
                acc = acc + dpbuf[j:j + tm, sl]
            du_ref[:, 2 * c + gi * gd:2 * c + (gi + 1) * gd] = acc.astype(du_ref.dtype)

    hb = tm // _HALO
    n_halo = t // _HALO
    n_nh = t // nh
    full = lambda shape: pl.BlockSpec(shape, lambda i: (0,) * len(shape))
    return pl.pallas_call(
        body, name="mix_bwd_taps", grid=(t // tm,),
        in_specs=[pl.BlockSpec((tm, c3), lambda i: (i, 0)),
                  pl.BlockSpec((_HALO, c3), lambda i: (jnp.maximum(i * hb - 1, 0), 0)),
                  pl.BlockSpec((tm, c), lambda i: (i, 0)),
                  pl.BlockSpec((_HALO, c), lambda i: (jnp.minimum((i + 1) * hb, n_halo - 1), 0)),
                  pl.BlockSpec((tm, c), lambda i: (i, 1)),
                  pl.BlockSpec((nh, c), lambda i: (jnp.minimum((i + 1) * (tm // nh), n_nh - 1), 1)),
                  full((_HALO, c)), full((ng, gd, gd)), full((1, c))],
        out_specs=[pl.BlockSpec((tm, c3), lambda i: (i, 0)), full((_HALO, c)), full((8, c)), full((ng, gd, gd))],
        out_shape=[jax.ShapeDtypeStruct((t, c3), _ACT), jax.ShapeDtypeStruct((_HALO, c), f32),
                   jax.ShapeDtypeStruct((8, c), f32), jax.ShapeDtypeStruct((ng, gd, gd), f32)],
        scratch_shapes=[pltpu.VMEM((_HALO + tm, c), f32), pltpu.VMEM((tm + _HALO, c), f32),
                        pltpu.VMEM((_HALO + tm, c), f32), pltpu.VMEM((tm + nh, c), f32)],
        compiler_params=_params(1),
    )(u, u, dhc, dhc, dy, dy, cw, pw, ps)


def _attn_fwd(q, kv, n_seq, seq, n_mem):
    t, d = q.shape
    dh = d // XATTN_HEADS
    tq = min(512, seq)
    nq = seq // tq
    scale = dh ** -0.5

    def body(q_ref, k_ref, v_ref, o_ref):
        s = _dot(q_ref[...], k_ref[...], _NT) * scale
        e = jnp.exp(s - jnp.max(s, axis=-1, keepdims=True))
        p = e / jnp.sum(e, axis=-1, keepdims=True)
        o_ref[...] = _dot(p.astype(_ACT), v_ref[...], _NN).astype(o_ref.dtype)

    qs = pl.BlockSpec((tq, dh), lambda b, h, i: (b * nq + i, h))
    return pl.pallas_call(
        body, name="attn_fwd", grid=(n_seq, XATTN_HEADS, nq),
        in_specs=[qs, pl.BlockSpec((n_mem, dh), lambda b, h, i: (b, h)),
                  pl.BlockSpec((n_mem, dh), lambda b, h, i: (b, XATTN_HEADS + h))],
        out_specs=qs, out_shape=jax.ShapeDtypeStruct((t, d), _ACT), compiler_params=_params(3),
    )(q, kv, kv)


def _attn_bwd(q, kv, do, n_seq, seq, n_mem):
    t, d = q.shape
    dh = d // XATTN_HEADS
    tq = min(512, seq)
    nq = seq // tq
    scale = dh ** -0.5

    def body(q_ref, k_ref, v_ref, do_ref, dq_ref, dk_ref, dv_ref, dk_acc, dv_acc):
        i = pl.program_id(2)
        qv = q_ref[...]
        kvv = k_ref[...]
        dov = do_ref[...]
        s = _dot(qv, kvv, _NT) * scale
        e = jnp.exp(s - jnp.max(s, axis=-1, keepdims=True))
        p = e / jnp.sum(e, axis=-1, keepdims=True)
        dp = _dot(dov, v_ref[...], _NT)
        ds = (p * (dp - jnp.sum(dp * p, axis=-1, keepdims=True)) * scale).astype(_ACT)
        dq_ref[...] = _dot(ds, kvv, _NN).astype(dq_ref.dtype)
        dk_part = _dot(ds, qv, _TN)
        dv_part = _dot(p.astype(_ACT), dov, _TN)

        @pl.when(i == 0)
        def _():
            dk_acc[...] = dk_part
            dv_acc[...] = dv_part

        @pl.when(i > 0)
        def _():
            dk_acc[...] += dk_part
            dv_acc[...] += dv_part

        @pl.when(i == nq - 1)
        def _():
            dk_ref[...] = dk_acc[...].astype(dk_ref.dtype)
            dv_ref[...] = dv_acc[...].astype(dv_ref.dtype)

    qs = pl.BlockSpec((tq, dh), lambda b, h, i: (b * nq + i, h))
    ms = pl.BlockSpec((n_mem, dh), lambda b, h, i: (b, h))
    return pl.pallas_call(
        body, name="attn_bwd", grid=(n_seq, XATTN_HEADS, nq),
        in_specs=[qs, ms, pl.BlockSpec((n_mem, dh), lambda b, h, i: (b, XATTN_HEADS + h)), qs],
        out_specs=[qs, ms, ms],
        out_shape=[jax.ShapeDtypeStruct((t, d), _ACT), jax.ShapeDtypeStruct((n_seq * n_mem, d), _ACT),
                   jax.ShapeDtypeStruct((n_seq * n_mem, d), _ACT)],
        scratch_shapes=[pltpu.VMEM((n_mem, dh), f32), pltpu.VMEM((n_mem, dh), f32)],
        compiler_params=_params(3),
    )(q, kv, kv, do)


_FFN_ROWS = 1024
_FFN_COLS = 256
_FFN_HALO = 16


def _window(buf, g, start, rows):
    return buf[g, pl.ds(start, rows + 8), :]


def _conv3(b_ref, w_ref, win, rows):
    acc = jnp.broadcast_to(b_ref[...], (rows, win.shape[1]))
    for k in range(3):
        acc = acc + w_ref[k:k + 1, :] * win[6 + k:6 + k + rows, :]
    return acc


def _ffn_gate_fwd(up, fw, fb, seq):
    _, t, f = up.shape
    tm = min(_FFN_ROWS, seq)
    tps = seq // tm
    tc = _FFN_COLS
    nc = f // tc
    hl = _FFN_HALO

    def body(up_ref, uph_ref, wg_ref, wv_ref, bg_ref, bv_ref, a_ref, buf):
        i = pl.program_id(1)
        keep = jnp.where(i % tps == 0, 0.0, 1.0)
        buf[:, 0:hl, :] = uph_ref[...].astype(f32) * keep
        buf[:, hl:hl + tm, :] = up_ref[...].astype(f32)

        def chunk(ci, carry):
            r0 = pl.multiple_of(ci * _CHUNK, _CHUNK)
            conv = []
            for g, (w_ref, b_ref) in enumerate(((wg_ref, bg_ref), (wv_ref, bv_ref))):
                conv.append(_conv3(b_ref, w_ref, _window(buf, g, r0 + hl - 8, _CHUNK), _CHUNK))
            gate, val = conv
            a_ref[pl.ds(r0, _CHUNK), :] = (gate * _sigmoid(gate) * val).astype(a_ref.dtype)
            return carry

        lax.fori_loop(0, tm // _CHUNK, chunk, 0)

    hb = tm // hl
    return pl.pallas_call(
        body, name="ffn_gate_fwd", grid=(nc, t // tm),
        in_specs=[pl.BlockSpec((2, tm, tc), lambda j, i: (0, i, j)),
                  pl.BlockSpec((2, hl, tc), lambda j, i: (0, jnp.maximum(i * hb - 1, 0), j)),
                  pl.BlockSpec((8, tc), lambda j, i: (0, j)), pl.BlockSpec((8, tc), lambda j, i: (0, nc + j)),
                  pl.BlockSpec((1, tc), lambda j, i: (0, j)), pl.BlockSpec((1, tc), lambda j, i: (0, nc + j))],
        out_specs=pl.BlockSpec((tm, tc), lambda j, i: (i, j)),
        out_shape=jax.ShapeDtypeStruct((t, f), _ACT),
        scratch_shapes=[pltpu.VMEM((2, hl + tm, tc), f32)], compiler_params=_params(2),
    )(up, up, fw, fw, fb, fb)


def _ffn_gate_bwd(up, da, fw, fb, seq):
    _, t, f = up.shape
    tm = min(_FFN_ROWS, seq)
    tps = seq // tm
    tc = _FFN_COLS
    nc = f // tc
    hl = _FFN_HALO

    def body(up_ref, uph_ref, upn_ref, da_ref, dan_ref, wg_ref, wv_ref, bg_ref, bv_ref,
             dup_ref, sg_ref, sv_ref, ubuf, dbuf, sums):
        i = pl.program_id(1)
        keep_prev = jnp.where(i % tps == 0, 0.0, 1.0)
        keep_next = jnp.where(i % tps == tps - 1, 0.0, 1.0)

        @pl.when(i == 0)
        def _():
            sg_ref[...] = jnp.zeros_like(sg_ref)
            sv_ref[...] = jnp.zeros_like(sv_ref)

        sums[...] = jnp.zeros_like(sums)
        ubuf[:, 0:hl, :] = uph_ref[...].astype(f32) * keep_prev
        ubuf[:, hl:hl + tm, :] = up_ref[...].astype(f32)
        ubuf[:, hl + tm:hl + tm + hl, :] = upn_ref[...].astype(f32) * keep_next
        w_refs = (wg_ref, wv_ref)
        b_refs = (bg_ref, bv_ref)

        def grads(r0, rows, dav, count):
            wins = [_window(ubuf, g, r0 + hl - 8, rows) for g in range(2)]
            gate, val = [_conv3(b_refs[g], w_refs[g], wins[g], rows) for g in range(2)]
            sg = _sigmoid(gate)
            douts = (dav * val * (sg * (1.0 + gate * (1.0 - sg))), dav * (gate * sg))
            for g in range(2):
                dbuf[g, pl.ds(r0, rows), :] = douts[g]
                if count:
                    sums[g, 0] += douts[g].reshape(rows // 8, 8, tc).sum(axis=0)
                    for k in range(3):
                        sums[g, 1 + k] += (douts[g] * wins[g][6 + k:6 + k + rows, :]).reshape(rows // 8, 8, tc).sum(axis=0)

        def first(ci, carry):
            r0 = pl.multiple_of(ci * _CHUNK, _CHUNK)
            grads(r0, _CHUNK, da_ref[pl.ds(r0, _CHUNK), :].astype(f32), True)
            return carry

        lax.fori_loop(0, tm // _CHUNK, first, 0)
        grads(tm, hl, dan_ref[...].astype(f32) * keep_next, False)

        def second(ci, carry):
            r0 = pl.multiple_of(ci * _CHUNK, _CHUNK)
            for g in range(2):
                win = _window(dbuf, g, r0, _CHUNK)
                acc = jnp.zeros((_CHUNK, tc), f32)
                for k in range(3):
                    acc = acc + w_refs[g][k:k + 1, :] * win[2 - k:2 - k + _CHUNK, :]
                dup_ref[g, pl.ds(r0, _CHUNK), :] = acc.astype(dup_ref.dtype)
            return carry

        lax.fori_loop(0, tm // _CHUNK, second, 0)
        for g, s_ref in enumerate((sg_ref, sv_ref)):
            for r in range(4):
                s_ref[r:r + 1, :] += jnp.sum(sums[g, r], axis=0, keepdims=True)

    hb = tm // hl
    n_halo = t // hl
    return pl.pallas_call(
        body, name="ffn_gate_bwd", grid=(nc, t // tm),
        in_specs=[pl.BlockSpec((2, tm, tc), lambda j, i: (0, i, j)),
                  pl.BlockSpec((2, hl, tc), lambda j, i: (0, jnp.maximum(i * hb - 1, 0), j)),
                  pl.BlockSpec((2, hl, tc), lambda j, i: (0, jnp.minimum((i + 1) * hb, n_halo - 1), j)),
                  pl.BlockSpec((tm, tc), lambda j, i: (i, j)),
                  pl.BlockSpec((hl, tc), lambda j, i: (jnp.minimum((i + 1) * hb, n_halo - 1), j)),
                  pl.BlockSpec((8, tc), lambda j, i: (0, j)), pl.BlockSpec((8, tc), lambda j, i: (0, nc + j)),
                  pl.BlockSpec((1, tc), lambda j, i: (0, j)), pl.BlockSpec((1, tc), lambda j, i: (0, nc + j))],
        out_specs=[pl.BlockSpec((2, tm, tc), lambda j, i: (0, i, j)),
                   pl.BlockSpec((8, tc), lambda j, i: (0, j)), pl.BlockSpec((8, tc), lambda j, i: (0, j))],
        out_shape=[jax.ShapeDtypeStruct((2, t, f), _ACT), jax.ShapeDtypeStruct((8, f), f32), jax.ShapeDtypeStruct((8, f), f32)],
        scratch_shapes=[pltpu.VMEM((2, hl + tm + hl, tc), f32), pltpu.VMEM((2, tm + hl, tc), f32),
                        pltpu.VMEM((2, 4, 8, tc), f32)],
        compiler_params=_params(2),
    )(up, up, up, da, da, fw, fw, fb, fb)


def _adamw_math(w, g, m, v):
    m = ADAM_B1 * m + (1.0 - ADAM_B1) * g
    v = ADAM_B2 * v + (1.0 - ADAM_B2) * (g * g)
    m_hat = m / (1.0 - ADAM_B1 ** ADAM_STEP)
    v_hat = v / (1.0 - ADAM_B2 ** ADAM_STEP)
    delta = -ADAM_LR * (m_hat / (jnp.sqrt(v_hat) + ADAM_EPS) + ADAM_WD * w)
    return delta, m, v


def _adamw_shard(name, w, g, m, v):
    _, r, c = w.shape
    tr = next((cand for cand in (256, 176, 128, 64, 32, 16, 8) if r % cand == 0), r)

    def body(w_ref, g_ref, m_ref, v_ref, d_ref, mo_ref, vo_ref):
        d, mn, vn = _adamw_math(w_ref[...], g_ref[...], m_ref[...], v_ref[...])
        d_ref[...] = d
        mo_ref[...] = mn
        vo_ref[...] = vn

    s3 = pl.BlockSpec((None, tr, c), lambda i: (0, i, 0))
    s2 = pl.BlockSpec((tr, c), lambda i: (i, 0))
    shp = jax.ShapeDtypeStruct(w.shape, f32)
    return pl.pallas_call(
        body, name=name, grid=(r // tr,), in_specs=[s3, s2, s3, s3], out_specs=[s3, s3, s3], out_shape=[shp, shp, shp],
        compiler_params=_params(1),
    )(w, g, m, v)


def _adamw_small(quads):
    n = len(quads)

    def body(*refs):
        ins, outs = refs[:4 * n], refs[4 * n:]
        for p in range(n):
            w_ref, g_ref, m_ref, v_ref = ins[4 * p:4 * p + 4]
            d, mn, vn = _adamw_math(w_ref[...], g_ref[...], m_ref[...], v_ref[...])
            outs[3 * p][...] = d
            outs[3 * p + 1][...] = mn
            outs[3 * p + 2][...] = vn

    flat = [a for q in quads for a in q]
    shapes = [jax.ShapeDtypeStruct(q[0].shape, f32) for q in quads for _ in range(3)]
    outs = pl.pallas_call(
        body, name="adamw_small", in_specs=[_VMEM] * (4 * n), out_specs=[_VMEM] * (3 * n), out_shape=shapes,
        compiler_params=pltpu.CompilerParams(vmem_limit_bytes=_VMEM_LIMIT_BYTES),
    )(*flat)
    return [tuple(outs[3 * p:3 * p + 3]) for p in range(n)]


def _sum_pairs(name, place, grads, got):
    _, r, c = grads.shape

    def body(place_ref, a_ref, b_ref, o_ref):
        o_ref[...] = (a_ref[...].astype(f32) + b_ref[...].astype(f32)).astype(o_ref.dtype)

    grid_spec = pltpu.PrefetchScalarGridSpec(
        num_scalar_prefetch=1, grid=(4,),
        in_specs=[pl.BlockSpec((None, r, c), lambda i, p: (2 * i + p[1], 0, 0)), pl.BlockSpec((None, r, c), lambda i, p: (i, 0, 0))],
        out_specs=pl.BlockSpec((None, r, c), lambda i, p: (i, 0, 0)))
    return pl.pallas_call(body, name=name, grid_spec=grid_spec, out_shape=jax.ShapeDtypeStruct((4, r, c), _ACT),
                          compiler_params=_params(1))(place, grads, got)


def _sum_four(name, place, sums, got):
    _, r, c = sums.shape

    def body(place_ref, o_ref, g_ref, f_ref):
        s = o_ref[...].astype(f32) + g_ref[0].astype(f32)
        s = s + g_ref[1].astype(f32)
        f_ref[...] = s + g_ref[2].astype(f32)

    grid_spec = pltpu.PrefetchScalarGridSpec(
        num_scalar_prefetch=1, grid=(1,),
        in_specs=[pl.BlockSpec((None, r, c), lambda i, p: (p[0], 0, 0)), pl.BlockSpec((3, r, c), lambda i, p: (0, 0, 0))],
        out_specs=pl.BlockSpec((None, r, c), lambda i, p: (p[1], 0, 0)))
    return pl.pallas_call(body, name=name, grid_spec=grid_spec, out_shape=jax.ShapeDtypeStruct((2, r, c), f32),
                          compiler_params=_params(1))(place, sums, got)


def _place():
    return lax.axis_index("x"), lax.axis_index("y"), lax.axis_index("c")


def _other_chips(x, y):
    return [(1 - x, y), (x, 1 - y), (1 - x, 1 - y)]


def _remote(src, dst, send_sem, recv_sem, to):
    return pltpu.make_async_remote_copy(src_ref=src, dst_ref=dst, send_sem=send_sem, recv_sem=recv_sem,
                                        device_id=to, device_id_type=_MESH)


def _place_shards(place, shards, col_sharded):
    n = len(shards)
    steps = 4

    def body(place_ref, *refs):
        for src, dst in zip(refs[:n], refs[n:]):
            dst[...] = src[...].astype(dst.dtype)

    in_specs, out_specs, out_shape = [], [], []
    for w, col in zip(shards, col_sharded):
        r, cs = w.shape
        tr = r // steps
        in_specs.append(pl.BlockSpec((tr, cs), lambda i, p: (i, 0)))
        if col:
            out_specs.append(pl.BlockSpec((tr, cs), lambda i, p: (i, p[0])))
            out_shape.append(jax.ShapeDtypeStruct((r, 4 * cs), _ACT))
        else:
            out_specs.append(pl.BlockSpec((tr, cs), lambda i, p: (p[0] * steps + i, 0)))
            out_shape.append(jax.ShapeDtypeStruct((4 * r, cs), _ACT))
    grid_spec = pltpu.PrefetchScalarGridSpec(num_scalar_prefetch=1, grid=(steps,), in_specs=in_specs, out_specs=out_specs)
    return pl.pallas_call(body, name="place_shards", grid_spec=grid_spec, out_shape=out_shape,
                          compiler_params=_params(1))(place, *shards)


def _shard_of(ref, col_sharded, s):
    rows, cols = ref.shape
    if col_sharded:
        return ref.at[:, pl.ds(s * (cols // 4), cols // 4)]
    return ref.at[pl.ds(s * (rows // 4), rows // 4), :]


def _part_of(ref, col_sharded, whole, s, h):
    if whole:
        return _shard_of(ref, col_sharded, s)
    rows, cols = ref.shape
    if col_sharded:
        return ref.at[pl.ds(h * (rows // 2), rows // 2), pl.ds(s * (cols // 4), cols // 4)]
    return ref.at[pl.ds((2 * s + h) * (rows // 8), rows // 8), :]


def _allgather_start(bufs, col_sharded, whole, groups):
    n = len(bufs)
    ng = len(groups)

    def body(*refs):
        out = refs[n:2 * n]
        sems = refs[2 * n:]
        x, y, c = _place()
        for g, members in enumerate(groups):
            for i, w in enumerate(members):
                mine = _part_of(out[w], col_sharded[w], whole[w], 2 * x + y, c)
                for j, chip in enumerate(_other_chips(x, y)):
                    _remote(mine, mine, sems[2 * g].at[3 * i + j], sems[2 * g + 1].at[3 * i + j], (*chip, c)).start()

    sem_shapes = [pltpu.SemaphoreType.DMA((3 * len(m),)) for m in groups for _ in range(2)]
    outs = pl.pallas_call(
        body, name="allgather_start", in_specs=[_HBM] * n, out_specs=[_HBM] * n + [_SEM] * (2 * ng),
        out_shape=[pltpu.HBM(b.shape, b.dtype) for b in bufs] + sem_shapes,
        input_output_aliases={i: i for i in range(n)},
        compiler_params=pltpu.CompilerParams(has_side_effects=_EFFECT),
    )(*[pltpu.with_memory_space_constraint(b, pltpu.HBM) for b in bufs])
    return list(outs[:n]), [(outs[n + 2 * g], outs[n + 2 * g + 1]) for g in range(ng)]


def _allgather_relay(name, bufs, col_sharded, whole, sems, after):
    n = len(bufs)

    def body(*refs):
        buf = refs[:n]
        send, recv = refs[n], refs[n + 1]
        out = refs[n + 3:2 * n + 3]
        to_sibling, from_sibling = refs[2 * n + 3:]
        x, y, c = _place()
        for i in range(n):
            mine = _part_of(buf[i], col_sharded[i], whole[i], 2 * x + y, c)
            for j, chip in enumerate(_other_chips(x, y)):
                landed = _part_of(buf[i], col_sharded[i], whole[i], 2 * chip[0] + chip[1], c)
                cp = _remote(mine, landed, send.at[3 * i + j], recv.at[3 * i + j], (*chip, c))
                cp.wait_send()
                cp.wait_recv()
        for i in range(n):
            if not whole[i]:
                for j, chip in enumerate(_other_chips(x, y)):
                    landed = _part_of(out[i], col_sharded[i], False, 2 * chip[0] + chip[1], c)
                    _remote(landed, landed, to_sibling.at[3 * i + j], from_sibling.at[3 * i + j], (x, y, 1 - c)).start()

    outs = pl.pallas_call(
        body, name=name, in_specs=[_HBM] * n + [_SEM, _SEM, _ANY], out_specs=[_HBM] * n + [_SEM, _SEM],
        out_shape=[pltpu.HBM(b.shape, b.dtype) for b in bufs] + [pltpu.SemaphoreType.DMA((3 * n,))] * 2,
        input_output_aliases={i: i for i in range(n)},
        compiler_params=pltpu.CompilerParams(has_side_effects=_EFFECT),
    )(*bufs, *sems, after)
    return list(outs[:n]), (outs[n], outs[n + 1])


def _allgather_wait(name, bufs, col_sharded, whole, sems, after):
    n = len(bufs)

    def body(*refs):
        buf = refs[:n]
        to_sibling, from_sibling = refs[n], refs[n + 1]
        x, y, c = _place()
        for i in range(n):
            if not whole[i]:
                for j, chip in enumerate(_other_chips(x, y)):
                    sent = _part_of(buf[i], col_sharded[i], False, 2 * chip[0] + chip[1], c)
                    landed = _part_of(buf[i], col_sharded[i], False, 2 * chip[0] + chip[1], 1 - c)
                    cp = _remote(sent, landed, to_sibling.at[3 * i + j], from_sibling.at[3 * i + j], (x, y, 1 - c))
                    cp.wait_send()
                    cp.wait_recv()

    return pl.pallas_call(
        body, name=name, in_specs=[_HBM] * n + [_SEM, _SEM, _ANY], out_specs=[_HBM] * n,
        out_shape=[pltpu.HBM(b.shape, b.dtype) for b in bufs],
        input_output_aliases={i: i for i in range(n)},
        compiler_params=pltpu.CompilerParams(has_side_effects=_EFFECT),
    )(*bufs, *sems, after)


def _exchange_pair_halves(name, grads):
    nw = len(grads)

    def body(*refs):
        src = refs[:nw]
        got = refs[nw:2 * nw]
        send_sem, recv_sem = refs[2 * nw:]
        x, y, c = _place()
        sends = []
        for w in range(nw):
            for s in range(4):
                rc = _remote(src[w].at[2 * s + 1 - c], got[w].at[s], send_sem.at[4 * w + s], recv_sem.at[4 * w + s], (x, y, 1 - c))
                rc.start()
                sends.append(rc)
        for rc in sends:
            rc.wait_recv()
        for rc in sends:
            rc.wait_send()

    return pl.pallas_call(
        body, name=name, in_specs=[_ANY] * nw, out_specs=[_ANY] * nw,
        out_shape=[jax.ShapeDtypeStruct((4,) + g.shape[1:], g.dtype) for g in grads],
        scratch_shapes=[pltpu.SemaphoreType.DMA((4 * nw,)), pltpu.SemaphoreType.DMA((4 * nw,))],
    )(*grads)


def _chip_exchange_start(name, sums):
    nw = len(sums)
    lands = [lax.empty((3,) + s.shape[1:], s.dtype) for s in sums]

    def body(*refs):
        src = refs[2 * nw:3 * nw]
        got = refs[3 * nw:4 * nw]
        send, recv, token = refs[4 * nw:]
        x, y, c = _place()
        for w in range(nw):
            for j, chip in enumerate(_other_chips(x, y)):
                _remote(src[w].at[2 * chip[0] + chip[1]], got[w].at[j], send.at[3 * w + j], recv.at[3 * w + j], (*chip, c)).start()
        token[...] = jnp.zeros_like(token)

    outs = pl.pallas_call(
        body, name=name, in_specs=[_HBM] * (2 * nw), out_specs=[_HBM] * (2 * nw) + [_SEM, _SEM, _VMEM],
        out_shape=[pltpu.HBM(a.shape, a.dtype) for a in list(sums) + lands]
        + [pltpu.SemaphoreType.DMA((3 * nw,)), pltpu.SemaphoreType.DMA((3 * nw,)), jax.ShapeDtypeStruct((8, 128), f32)],
        input_output_aliases={i: i for i in range(2 * nw)},
        compiler_params=pltpu.CompilerParams(has_side_effects=_EFFECT),
    )(*[pltpu.with_memory_space_constraint(a, pltpu.HBM) for a in list(sums) + lands])
    return list(outs[:nw]), list(outs[nw:2 * nw]), (outs[2 * nw], outs[2 * nw + 1]), outs[2 * nw + 2]


def _chip_exchange_wait(name, sums, got, sems, after):
    nw = len(sums)

    def body(*refs):
        src = refs[:nw]
        land = refs[nw:2 * nw]
        send, recv = refs[2 * nw], refs[2 * nw + 1]
        x, y, c = _place()
        for w in range(nw):
            for j, chip in enumerate(_other_chips(x, y)):
                cp = _remote(src[w].at[2 * chip[0] + chip[1]], land[w].at[j], send.at[3 * w + j], recv.at[3 * w + j], (*chip, c))
                cp.wait_send()
                cp.wait_recv()

    outs = pl.pallas_call(
        body, name=name, in_specs=[_HBM] * (2 * nw) + [_SEM, _SEM, _ANY], out_specs=[_HBM] * (2 * nw),
        out_shape=[pltpu.HBM(a.shape, a.dtype) for a in list(sums) + list(got)],
        input_output_aliases={i: i for i in range(2 * nw)},
        compiler_params=pltpu.CompilerParams(has_side_effects=_EFFECT),
    )(*sums, *got, *sems, after)
    return list(outs[:nw]), list(outs[nw:])


def _swap_halves(finals):
    nw = len(finals)

    def body(*refs):
        buf = refs[nw:2 * nw]
        send_sem, recv_sem = refs[2 * nw:]
        x, y, c = _place()
        sends = []
        for w in range(nw):
            rc = _remote(buf[w].at[c], buf[w].at[c], send_sem.at[w], recv_sem.at[w], (x, y, 1 - c))
            rc.start()
            sends.append(rc)
        for w in range(nw):
            _remote(buf[w].at[1 - c], buf[w].at[1 - c], send_sem.at[w], recv_sem.at[w], (x, y, c)).wait_recv()
        for rc in sends:
            rc.wait_send()

    return pl.pallas_call(
        body, name="rs_swap_halves", in_specs=[_ANY] * nw, out_specs=[_ANY] * nw,
        out_shape=[jax.ShapeDtypeStruct(g.shape, g.dtype) for g in finals],
        input_output_aliases={i: i for i in range(nw)},
        scratch_shapes=[pltpu.SemaphoreType.DMA((nw,)), pltpu.SemaphoreType.DMA((nw,))],
    )(*finals)


def _half_slices(shape, h):
    rows, cols = shape
    if cols % 256 == 0:
        return (slice(None), slice(h * (cols // 2), (h + 1) * (cols // 2)))
    return (slice(h * (rows // 2), (h + 1) * (rows // 2)), slice(None))


def _allreduce_small(parts):
    n = len(parts)

    def body(*refs):
        src = refs[:n]
        out = refs[n:2 * n]
        sib = refs[2 * n:3 * n]
        chip_sum = refs[3 * n:4 * n]
        slots = refs[4 * n:5 * n]
        pair_send, pair_recv, ici_send, ici_recv, swap_send, swap_recv = refs[5 * n:]
        x, y, c = _place()
        me_chip = 2 * x + y
        chips = _other_chips(x, y)
        pairs = [_remote(src[a], sib[a], pair_send.at[a], pair_recv.at[a], (x, y, 1 - c)) for a in range(n)]
        for rc in pairs:
            rc.start()
        for a in range(n):
            pairs[a].wait_recv()
            chip_sum[a][...] = src[a][...] + sib[a][...]
        for h in (0, 1):
            @pl.when(c == h)
            def _():
                sends = []
                for a in range(n):
                    idx = _half_slices(parts[a].shape, h)
                    for j, chip in enumerate(chips):
                        rc = _remote(chip_sum[a].at[idx], slots[a].at[me_chip].at[idx], ici_send.at[3 * a + j], ici_recv.at[3 * a + j], (*chip, h))
                        rc.start()
                        sends.append(rc)
                    slots[a][(me_chip,) + idx] = chip_sum[a][idx]
                for a in range(n):
                    idx = _half_slices(parts[a].shape, h)
                    for j, chip in enumerate(chips):
                        landed = slots[a].at[2 * chip[0] + chip[1]].at[idx]
                        _remote(landed, landed, ici_send.at[3 * a + j], ici_recv.at[3 * a + j], (x, y, c)).wait_recv()
                    total = slots[a][(0,) + idx]
                    for s in range(1, 4):
                        total = total + slots[a][(s,) + idx]
                    out[a][idx] = total
                    rc = _remote(out[a].at[idx], out[a].at[idx], swap_send.at[a], swap_recv.at[a], (x, y, 1 - h))
                    rc.start()
                    sends.append(rc)
                for a in range(n):
                    other = out[a].at[_half_slices(parts[a].shape, 1 - h)]
                    _remote(other, other, swap_send.at[a], swap_recv.at[a], (x, y, c)).wait_recv()
                for rc in sends:
                    rc.wait_send()
        for rc in pairs:
            rc.wait_send()

    return pl.pallas_call(
        body, name="allreduce_small", in_specs=[_VMEM] * n, out_specs=[_VMEM] * n,
        out_shape=[jax.ShapeDtypeStruct(p.shape, f32) for p in parts],
        scratch_shapes=[pltpu.VMEM(p.shape, f32) for p in parts] * 2 + [pltpu.VMEM((4,) + p.shape, f32) for p in parts]
        + [pltpu.SemaphoreType.DMA((n,)), pltpu.SemaphoreType.DMA((n,)), pltpu.SemaphoreType.DMA((3 * n,)),
           pltpu.SemaphoreType.DMA((3 * n,)), pltpu.SemaphoreType.DMA((n,)), pltpu.SemaphoreType.DMA((n,))],
        compiler_params=pltpu.CompilerParams(vmem_limit_bytes=_VMEM_LIMIT_BYTES),
    )(*parts)


def _local_step(x, mem, tgt, g_mix, g_xattn, g_mem, g_ffn, g_final, cb, lg, lb, pw, ps, fb, relay, weights, reduce, n_seq, seq, n_mem):
    t, d = x.shape
    f = fb.shape[1] // 2
    c = cb.shape[1]
    h1 = _rms_fwd("norm_mix", x, g_mix)
    relay(0, h1)
    w_in, cw, fw = weights(0, h1)
    u = _mm_nn("proj_in", h1, w_in, _ACT, w_in.shape[1])
    y, hc = _mix_fwd(u, cw, cb, lg, lb, pw, ps, seq)
    relay(1, y)
    w_out, w_q, w_kv, w_o = weights(1, y)
    x1, h2 = _proj_residual_norm("proj_out", y, w_out, x, g_xattn)
    q = _mm_nn("proj_q", h2, w_q, _ACT, d)
    mem_n = _rms_fwd("norm_mem", mem, g_mem)
    kv = _mm_nn("proj_kv", mem_n, w_kv, _ACT, 2 * d)
    o = _attn_fwd(q, kv, n_seq, seq, n_mem)
    relay(2, o)
    x2, h3 = _proj_residual_norm("proj_o", o, w_o, x1, g_ffn)
    w_up, w_down = weights(2, h3)
    up = _mm_nn("proj_up", h3, w_up, _ACT, f, split_out=True)
    a = _ffn_gate_fwd(up, fw, fb, seq)
    dx3, dx3b, dg_final, loss = _proj_loss_bwd("proj_down", a, w_down, x2, g_final, tgt)
    da = _mm_nt("d_act", dx3b, w_down, _ACT)
    gw_down = _mm_tn_rows("dw_down", a, dx3b, f // 2, d // 2)
    dup, sums_g, sums_v = _ffn_gate_bwd(up, da, fw, fb, seq)
    gw_up = _mm_tn_pieces("dw_up", h3, dup, f // 2, t)
    token = reduce(0, [gw_down.reshape(8, -1, d), gw_up])
    dx2, dx2b, dg_ffn = _dproj_rms_bwd("d_h3", dup, w_up, x2, g_ffn + token, dx3)
    do = _mm_nt("d_o", dx2b, w_o, _ACT)
    gw_o = _mm_tn_rows("dw_o", o, dx2b, d, d // 2)
    dq, dk, dv = _attn_bwd(q, kv, do, n_seq, seq, n_mem)
    dkv = jnp.concatenate([dk, dv], axis=1)
    gw_q = _mm_tn_rows("dw_q", h2, dq, d, d // 2)
    gw_kv = _mm_tn_pieces("dw_kv", mem_n, dkv, d // 2, mem.shape[0])
    dmem_n = _mm_nt("d_mem_n", dkv, w_kv, f32)
    dg_mem = _rms_gain_grad("norm_mem_bwd", mem, dmem_n)
    dx1, dx1b, dg_xattn = _dproj_rms_bwd("d_h2", dq, w_q, x1, g_xattn, dx2)
    dy = _mm_nt("d_y", dx1b, w_out, _ACT)
    gw_out = _mm_tn_rows("dw_out", y, dx1b, d, d // 2)
    token = reduce(1, [gw_o.reshape(8, -1, d), gw_q.reshape(8, -1, d), gw_kv, gw_out.reshape(8, -1, d)])
    dhc, sums_norm = _mix_bwd_norm(hc, dy, lg + token, lb, seq)
    du, d_cw, d_ps, d_pw = _mix_bwd_taps(u, dhc, dy, cw, pw, ps, seq)
    gw_in = _mm_tn_pieces("dw_in", h1, du, c * 3 // 4, t)
    reduce(2, [gw_in])
    grad_x, dg_mix = _dproj_rms_bwd("d_h1", du, w_in, x, g_mix, dx1, storage_copy=False)
    zero_row = jnp.zeros((1, d), f32)
    gains = jnp.concatenate([dg_mix, dg_xattn, dg_mem, dg_ffn, dg_final, jnp.pad(loss, ((0, 0), (0, d - 1))), zero_row, zero_row], axis=0)
    conv_rows = jnp.concatenate([sums_norm[2:3], sums_norm[0:1], sums_norm[1:2], d_ps[0:1], jnp.zeros((4, c), f32)], axis=0)
    ffn_rows = jnp.concatenate([sums_g, sums_v], axis=1)
    small = [gains, conv_rows, d_pw.reshape(-1, d_pw.shape[-1]), ffn_rows, d_cw]
    return grad_x, small


def kernel(x, mem, norm_mix_g, w_in, conv_dw_w, conv_dw_b, conv_ln_g, conv_ln_b, pool_w, pool_scale, w_out, norm_xattn_g, norm_mem_g, w_q, w_kv, w_o, norm_ffn_g, w_up, ffn_dw_w, ffn_dw_b, w_down, norm_final_g, loss_target, m_norm_mix_g, m_w_in, m_conv_dw_w, m_conv_dw_b, m_conv_ln_g, m_conv_ln_b, m_pool_w, m_pool_scale, m_w_out, m_norm_xattn_g, m_norm_mem_g, m_w_q, m_w_kv, m_w_o, m_norm_ffn_g, m_w_up, m_ffn_dw_w, m_ffn_dw_b, m_w_down, m_norm_final_g, v_norm_mix_g, v_w_in, v_conv_dw_w, v_conv_dw_b, v_conv_ln_g, v_conv_ln_b, v_pool_w, v_pool_scale, v_w_out, v_norm_xattn_g, v_norm_mem_g, v_w_q, v_w_kv, v_w_o, v_norm_ffn_g, v_w_up, v_ffn_dw_w, v_ffn_dw_b, v_w_down, v_norm_final_g):
    n_seq, seq, d = x.shape
    n_mem = mem.shape[1]
    chip = 2 * lax.axis_index("x") + lax.axis_index("y")

    place = jnp.stack([chip, lax.axis_index("c")]).astype(jnp.int32)

    col_w = [w_in, w_kv, w_up]
    row_w = [w_out, w_q, w_o, w_down]
    col_flags = [True] * 3 + [False] * 4 + [True] * 2
    kw = conv_dw_w.shape[1]

    def padded_in_place(shard, rows):
        full = jnp.zeros((rows, 4 * shard.shape[1]), shard.dtype)
        return lax.dynamic_update_slice(full, shard, (0, chip * shard.shape[1]))

    bufs = list(_place_shards(place, [w[0] for w in col_w + row_w], col_flags[:7]))
    bufs += [padded_in_place(conv_dw_w[0], _HALO), padded_in_place(ffn_dw_w[0], 8)]
    groups = [[0, 7, 8], [3, 4, 1, 5], [2, 6]]
    whole = [False] * 7 + [True] * 2
    bufs, sems = _allgather_start(bufs, col_flags, whole, groups)
    relayed = {}

    def relay(g, after):
        members = groups[g]
        relayed[g] = _allgather_relay("allgather_relay_%d" % g, [bufs[i] for i in members], [col_flags[i] for i in members],
                                      [whole[i] for i in members], sems[g], after)

    def weights(g, after):
        members = groups[g]
        group_bufs, sibling_sems = relayed[g]
        return _allgather_wait("allgather_wait_%d" % g, group_bufs, [col_flags[i] for i in members],
                               [whole[i] for i in members], sibling_sems, after)

    names = ["w_in", "w_kv", "w_up", "w_out", "w_q", "w_o", "w_down"]
    reduce_groups = [["w_down", "w_up"], ["w_o", "w_q", "w_kv", "w_out"], ["w_in"]]
    in_flight = {}

    def reduce(g, grads):
        members = reduce_groups[g]
        got = _exchange_pair_halves("rs_pair_exchange_%d" % g, grads)
        sums = [_sum_pairs("rs_pair_sum_" + n, place, a, b) for n, a, b in zip(members, grads, got)]
        sums, lands, rs_sems, token = _chip_exchange_start("rs_chip_start_%d" % g, sums)
        in_flight[g] = (sums, lands, rs_sems)
        return token[0:1, 0:1]

    grad_x, small = _local_step(
        x.reshape(n_seq * seq, d), mem.reshape(n_seq * n_mem, d), loss_target.reshape(n_seq * seq, d),
        norm_mix_g, norm_xattn_g, norm_mem_g, norm_ffn_g, norm_final_g.reshape(1, d),
        conv_dw_b, conv_ln_g, conv_ln_b, pool_w[0], pool_scale, ffn_dw_b, relay, weights, reduce, n_seq, seq, n_mem)

    finals = {}
    for g, members in enumerate(reduce_groups):
        sums, lands, rs_sems = in_flight[g]
        sums, lands = _chip_exchange_wait("rs_chip_wait_%d" % g, sums, lands, rs_sems, grad_x)
        for n, a, b in zip(members, sums, lands):
            finals[n] = _sum_four("rs_chip_sum_" + n, place, a, b)
    shard_grads = _swap_halves([finals[n] for n in names])

    gains, conv_rows, d_pw, ffn_rows, d_cw = _allreduce_small(small)
    loss = gains[5, 0]

    outs = {}
    big_w = dict(zip(names, col_w + row_w))
    big_m = dict(w_in=m_w_in, w_kv=m_w_kv, w_up=m_w_up, w_out=m_w_out, w_q=m_w_q, w_o=m_w_o, w_down=m_w_down)
    big_v = dict(w_in=v_w_in, w_kv=v_w_kv, w_up=v_w_up, w_out=v_w_out, w_q=v_w_q, w_o=v_w_o, w_down=v_w_down)
    for n, g in zip(names, shard_grads):
        w = big_w[n]
        g2 = g.reshape(w.shape[1], w.shape[2])
        delta, new_m, new_v = _adamw_shard("adamw_" + n, w, g2, big_m[n], big_v[n])
        outs[n] = (g2.reshape(w.shape), delta, new_m, new_v)

    f2 = ffn_dw_b.shape[1]
    cs_c = conv_dw_w.shape[2]
    cs_f = ffn_dw_w.shape[2]
    g_cw = lax.dynamic_slice(d_cw, (0, chip * cs_c), (kw, cs_c)).reshape(conv_dw_w.shape)
    g_fw = lax.dynamic_slice(ffn_rows, (1, chip * cs_f), (ffn_dw_w.shape[1], cs_f)).reshape(ffn_dw_w.shape)
    small_params = [
        ("norm_mix_g", norm_mix_g, gains[0:1], m_norm_mix_g, v_norm_mix_g),
        ("conv_dw_w", conv_dw_w, g_cw, m_conv_dw_w, v_conv_dw_w),
        ("conv_dw_b", conv_dw_b, conv_rows[0:1], m_conv_dw_b, v_conv_dw_b),
        ("conv_ln_g", conv_ln_g, conv_rows[1:2], m_conv_ln_g, v_conv_ln_g),
        ("conv_ln_b", conv_ln_b, conv_rows[2:3], m_conv_ln_b, v_conv_ln_b),
        ("pool_w", pool_w, d_pw.reshape(pool_w.shape), m_pool_w, v_pool_w),
        ("pool_scale", pool_scale, conv_rows[3:4], m_pool_scale, v_pool_scale),
        ("norm_xattn_g", norm_xattn_g, gains[1:2], m_norm_xattn_g, v_norm_xattn_g),
        ("norm_mem_g", norm_mem_g, gains[2:3], m_norm_mem_g, v_norm_mem_g),
        ("norm_ffn_g", norm_ffn_g, gains[3:4], m_norm_ffn_g, v_norm_ffn_g),
        ("ffn_dw_w", ffn_dw_w, g_fw, m_ffn_dw_w, v_ffn_dw_w),
        ("ffn_dw_b", ffn_dw_b, ffn_rows[0:1, :f2], m_ffn_dw_b, v_ffn_dw_b),
        ("norm_final_g", norm_final_g.reshape(1, d), gains[4:5], m_norm_final_g.reshape(1, d), v_norm_final_g.reshape(1, d)),
    ]
    quads = []
    for _, w, g, m, v in small_params:
        shape2 = (-1, w.shape[-1])
        quads.append((w.reshape(shape2), g.reshape(shape2), m.reshape(shape2), v.reshape(shape2)))
    for (n, w, g, _, _), (delta, new_m, new_v) in zip(small_params, _adamw_small(quads)):
        shape = norm_final_g.shape if n == "norm_final_g" else w.shape
        outs[n] = (g.reshape(shape), delta.reshape(shape), new_m.reshape(shape), new_v.reshape(shape))

    order = ["norm_mix_g", "w_in", "conv_dw_w", "conv_dw_b", "conv_ln_g", "conv_ln_b", "pool_w", "pool_scale", "w_out",
             "norm_xattn_g", "norm_mem_g", "w_q", "w_kv", "w_o", "norm_ffn_g", "w_up", "ffn_dw_w", "ffn_dw_b", "w_down",
             "norm_final_g"]
    return (loss, grad_x.reshape(x.shape), *[outs[n][0] for n in order], *[outs[n][1] for n in order],
            *[outs[n][2] for n in order], *[outs[n][3] for n in order])
```

```python
import functools

import jax
import jax.numpy as jnp
from jax import lax
from jax.experimental import pallas as pl
from jax.experimental.pallas import tpu as pltpu

f32 = jnp.float32
_ACT = jnp.bfloat16

EPS = 1e-6
POOL_WINDOWS = (2, 4, 8, 16)
XATTN_HEADS = 4
ADAM_LR = 0.001
ADAM_B1 = 0.9
ADAM_B2 = 0.999
ADAM_EPS = 1e-08
ADAM_WD = 0.01
ADAM_STEP = 10

_VMEM_LIMIT_BYTES = 56 * 1024 * 1024
_MESH = pl.DeviceIdType.MESH
_ANY = pl.BlockSpec(memory_space=pl.ANY)
_VMEM = pl.BlockSpec(memory_space=pltpu.VMEM)
_HBM = pl.BlockSpec(memory_space=pltpu.HBM)
_SEM = pl.BlockSpec(memory_space=pltpu.SEMAPHORE)
_EFFECT = pltpu.SideEffectType.DATAFLOW_SIDE_EFFECTING

_NN = (((1,), (0,)), ((), ()))
_NT = (((1,), (1,)), ((), ()))
_TN = (((0,), (0,)), ((), ()))


def _params(n_grid):
    return pltpu.CompilerParams(dimension_semantics=("arbitrary",) * n_grid, vmem_limit_bytes=_VMEM_LIMIT_BYTES)


def _sigmoid(v):
    return 1.0 / (1.0 + jnp.exp(-v))


def _dot(a, b, dims):
    return lax.dot_general(a, b, dims, preferred_element_type=f32)


def _mm(name, a, b, *, dims, grid, a_spec, b_spec, o_spec, out_shape, nk, acc_shape=None, res=None, res_spec=None):
    def body(*refs):
        if res is None:
            a_ref, b_ref, o_ref, *scratch = refs
            r_ref = None
        else:
            a_ref, b_ref, r_ref, o_ref, *scratch = refs
        p = _dot(a_ref[...], b_ref[...], dims)

        def finish(v):
            if r_ref is not None:
                v = v + r_ref[...]
            o_ref[...] = v.astype(o_ref.dtype)

        if nk == 1:
            finish(p)
        else:
            acc = scratch[0]
            k = pl.program_id(2)

            @pl.when(k == 0)
            def _():
                acc[...] = p

            @pl.when(k > 0)
            def _():
                acc[...] += p

            @pl.when(k == nk - 1)
            def _():
                finish(acc[...])

    ins = [a, b] + ([] if res is None else [res])
    specs = [a_spec, b_spec] + ([] if res is None else [res_spec])
    return pl.pallas_call(
        body, name=name, grid=grid, in_specs=specs, out_specs=o_spec, out_shape=out_shape,
        scratch_shapes=[pltpu.VMEM(acc_shape, f32)] if nk > 1 else [], compiler_params=_params(3),
    )(*ins)


def _row_tile(m):
    return min(512, m)


def _mm_nn(name, a, b, out_dtype, tn, res=None, split_out=False):
    m, k = a.shape
    n = b.shape[1]
    tm = _row_tile(m)
    if split_out:
        out_shape = jax.ShapeDtypeStruct((n // tn, m, tn), out_dtype)
        o_spec = pl.BlockSpec((None, tm, tn), lambda j, i, kk: (j, i, 0))
    else:
        out_shape = jax.ShapeDtypeStruct((m, n), out_dtype)
        o_spec = pl.BlockSpec((tm, tn), lambda j, i, kk: (i, j))
    return _mm(
        name, a, b, dims=_NN, grid=(n // tn, m // tm, 1), nk=1,
        a_spec=pl.BlockSpec((tm, k), lambda j, i, kk: (i, 0)),
        b_spec=pl.BlockSpec((k, tn), lambda j, i, kk: (0, j)),
        o_spec=o_spec, out_shape=out_shape, res=res,
        res_spec=pl.BlockSpec((tm, tn), lambda j, i, kk: (i, j)),
    )


def _mm_nt(name, a, b, out_dtype):
    n, kc = b.shape
    m = a.shape[0]
    tm = _row_tile(m)
    return _mm(
        name, a, b, dims=_NT, grid=(m // tm, 1, 1), nk=1,
        a_spec=pl.BlockSpec((tm, kc), lambda i, j, k: (i, 0)), b_spec=pl.BlockSpec((n, kc), lambda i, j, k: (0, 0)),
        o_spec=pl.BlockSpec((tm, n), lambda i, j, k: (i, 0)),
        out_shape=jax.ShapeDtypeStruct((m, n), out_dtype),
    )


def _mm_tn_rows(name, a, b, tka, tn):
    m, ka = a.shape
    nb = b.shape[1]
    return _mm(
        name, a, b, dims=_TN, grid=(ka // tka, nb // tn, 1), nk=1,
        a_spec=pl.BlockSpec((m, tka), lambda i, j, k: (0, i)),
        b_spec=pl.BlockSpec((m, tn), lambda i, j, k: (0, j)),
        o_spec=pl.BlockSpec((tka, tn), lambda i, j, k: (i, j)),
        out_shape=jax.ShapeDtypeStruct((ka, nb), _ACT),
    )


def _mm_tn_pieces(name, a, b, cs, tt):
    m, ka = a.shape
    nk = m // tt
    if b.ndim == 3:
        b_spec = pl.BlockSpec((None, tt, cs), lambda i, j, k: (j // 2, k, j % 2))
    else:
        b_spec = pl.BlockSpec((tt, cs), lambda i, j, k: (k, j))
    return _mm(
        name, a, b, dims=_TN, grid=(2, 4, nk), nk=nk, acc_shape=(ka // 2, cs),
        a_spec=pl.BlockSpec((tt, ka // 2), lambda i, j, k: (k, i)), b_spec=b_spec,
        o_spec=pl.BlockSpec((None, ka // 2, cs), lambda i, j, k: (2 * j + i, 0, 0)),
        out_shape=jax.ShapeDtypeStruct((8, ka // 2, cs), _ACT),
    )


def _rms_fwd(name, x, g):
    t, d = x.shape
    tm = _row_tile(t)

    def body(x_ref, g_ref, h_ref):
        xv = x_ref[...]
        r = lax.rsqrt(jnp.mean(xv * xv, axis=-1, keepdims=True) + EPS)
        h_ref[...] = (xv * r * g_ref[...]).astype(h_ref.dtype)

    return pl.pallas_call(
        body, name=name, grid=(t // tm,),
        in_specs=[pl.BlockSpec((tm, d), lambda i: (i, 0)), pl.BlockSpec((1, d), lambda i: (0, 0))],
        out_specs=pl.BlockSpec((tm, d), lambda i: (i, 0)), out_shape=jax.ShapeDtypeStruct((t, d), _ACT),
        compiler_params=_params(1),
    )(x, g)


def _fused_rows(name, a, b, product, a_spec, tm, extras, extra_specs, out_shape, out_specs, epilogue):
    ne = len(extras)

    def body(a_ref, b_ref, *refs):
        epilogue(product(a_ref, b_ref), refs[:ne], refs[ne:])

    m = extras[0].shape[0]
    return pl.pallas_call(
        body, name=name, grid=(m // tm,),
        in_specs=[a_spec, pl.BlockSpec(b.shape, lambda i: (0, 0)), *extra_specs], out_specs=out_specs, out_shape=out_shape,
        compiler_params=_params(1),
    )(a, b, *extras)


def _proj_residual_norm(name, a, b, res, g):
    m, k = a.shape
    d = b.shape[1]
    tm = _row_tile(m)

    def epilogue(p, ins, outs):
        xv = p + ins[0][...]
        outs[0][...] = xv
        r = lax.rsqrt(jnp.mean(xv * xv, axis=-1, keepdims=True) + EPS)
        outs[1][...] = (xv * r * ins[1][...]).astype(outs[1].dtype)

    row = pl.BlockSpec((tm, d), lambda i: (i, 0))
    return _fused_rows(
        name, a, b, lambda a_ref, b_ref: _dot(a_ref[...], b_ref[...], _NN), pl.BlockSpec((tm, k), lambda i: (i, 0)), tm,
        [res, g], [row, pl.BlockSpec((1, d), lambda i: (0, 0))],
        [jax.ShapeDtypeStruct((m, d), f32), jax.ShapeDtypeStruct((m, d), _ACT)], [row, row], epilogue)


def _dproj_rms_bwd(name, a, b, x, g, dres, storage_copy=True):
    m, d = x.shape
    if a.ndim == 3:
        nh, _, kh = a.shape
        tm = min(256, m)
        a_spec = pl.BlockSpec((nh, tm, kh), lambda i: (0, i, 0))

        def product(a_ref, b_ref):
            p = _dot(a_ref[0], b_ref[:, 0:kh], _NT)
            for h in range(1, nh):
                p = p + _dot(a_ref[h], b_ref[:, h * kh:(h + 1) * kh], _NT)
            return p
    else:
        tm = _row_tile(m)
        a_spec = pl.BlockSpec((tm, a.shape[1]), lambda i: (i, 0))

        def product(a_ref, b_ref):
            return _dot(a_ref[...], b_ref[...], _NT)

    def epilogue(dhv, ins, outs):
        x_ref, g_ref, dres_ref = ins
        dg_ref = outs[-1]

        @pl.when(pl.program_id(0) == 0)
        def _():
            dg_ref[...] = jnp.zeros_like(dg_ref)

        xv = x_ref[...]
        r = lax.rsqrt(jnp.mean(xv * xv, axis=-1, keepdims=True) + EPS)
        xn = xv * r
        dxn = dhv * g_ref[...]
        dx = r * (dxn - xn * jnp.mean(dxn * xn, axis=-1, keepdims=True)) + dres_ref[...]
        outs[0][...] = dx
        if storage_copy:
            outs[1][...] = dx.astype(outs[1].dtype)
        dg_ref[...] += jnp.sum(dhv * xn, axis=0, keepdims=True)

    row = pl.BlockSpec((tm, d), lambda i: (i, 0))
    vec = pl.BlockSpec((1, d), lambda i: (0, 0))
    copies = [jax.ShapeDtypeStruct((m, d), _ACT)] if storage_copy else []
    return _fused_rows(
        name, a, b, product, a_spec, tm, [x, g, dres], [row, vec, row],
        [jax.ShapeDtypeStruct((m, d), f32)] + copies + [jax.ShapeDtypeStruct((1, d), f32)],
        [row] * (1 + len(copies)) + [vec], epilogue)


def _proj_loss_bwd(name, a, b, res, g, tgt):
    m, k = a.shape
    d = b.shape[1]
    tm = _row_tile(m)

    def epilogue(p, ins, outs):
        res_ref, g_ref, t_ref = ins
        dx_ref, dxb_ref, dg_ref, loss_ref = outs

        @pl.when(pl.program_id(0) == 0)
        def _():
            dg_ref[...] = jnp.zeros_like(dg_ref)
            loss_ref[...] = jnp.zeros_like(loss_ref)

        xv = p + res_ref[...]
        gv = g_ref[...]
        r = lax.rsqrt(jnp.mean(xv * xv, axis=-1, keepdims=True) + EPS)
        xn = xv * r
        err = xn * gv - t_ref[...]
        loss_ref[...] += 0.5 * jnp.sum(jnp.mean(err * err, axis=-1, keepdims=True), axis=0, keepdims=True)
        dout = err * (1.0 / d)
        dxn = dout * gv
        dx = r * (dxn - xn * jnp.mean(dxn * xn, axis=-1, keepdims=True))
        dx_ref[...] = dx
        dxb_ref[...] = dx.astype(dxb_ref.dtype)
        dg_ref[...] += jnp.sum(dout * xn, axis=0, keepdims=True)

    row = pl.BlockSpec((tm, d), lambda i: (i, 0))
    vec = pl.BlockSpec((1, d), lambda i: (0, 0))
    return _fused_rows(
        name, a, b, lambda a_ref, b_ref: _dot(a_ref[...], b_ref[...], _NN), pl.BlockSpec((tm, k), lambda i: (i, 0)), tm,
        [res, g, tgt], [row, vec, row],
        [jax.ShapeDtypeStruct((m, d), f32), jax.ShapeDtypeStruct((m, d), _ACT), jax.ShapeDtypeStruct((1, d), f32),
         jax.ShapeDtypeStruct((1, 1), f32)],
        [row, row, vec, pl.BlockSpec((1, 1), lambda i: (0, 0))], epilogue)


def _rms_gain_grad(name, x, dh):
    t, d = x.shape
    tm = _row_tile(t)

    def body(x_ref, dh_ref, dg_ref):
        @pl.when(pl.program_id(0) == 0)
        def _():
            dg_ref[...] = jnp.zeros_like(dg_ref)

        xv = x_ref[...]
        r = lax.rsqrt(jnp.mean(xv * xv, axis=-1, keepdims=True) + EPS)
        dg_ref[...] += jnp.sum(dh_ref[...] * (xv * r), axis=0, keepdims=True)

    row = pl.BlockSpec((tm, d), lambda i: (i, 0))
    return pl.pallas_call(
        body, name=name, grid=(t // tm,), in_specs=[row, row], out_specs=pl.BlockSpec((1, d), lambda i: (0, 0)),
        out_shape=jax.ShapeDtypeStruct((1, d), f32), compiler_params=_params(1),
    )(x, dh)


_CONV_ROWS = 256
_CHUNK = 64
_HALO = 32


def _pool_counts(pos, w):
    return jnp.minimum(pos + 1.0, float(w))


def _mix_fwd(u, cw, cb, lg, lb, pw, ps, seq):
    t, c3 = u.shape
    c = c3 // 3
    kw = 31
    tm = min(_CONV_ROWS, seq)
    tps = seq // tm
    gd = c // len(POOL_WINDOWS)

    def body(u_ref, uh_ref, cw_ref, cb_ref, lg_ref, lb_ref, pw_ref, ps_ref, y_ref, hc_ref, hgbuf, pbuf):
        i = pl.program_id(0)
        keep = jnp.where(i % tps == 0, 0.0, 1.0)
        um = u_ref[...].astype(f32)
        uh = uh_ref[...].astype(f32) * keep
        hgbuf[0:_HALO, :] = uh[:, 0:c] * _sigmoid(uh[:, c:2 * c])
        hgbuf[_HALO:_HALO + tm, :] = um[:, 0:c] * _sigmoid(um[:, c:2 * c])
        pbuf[0:_HALO, :] = uh[:, 2 * c:]
        pbuf[_HALO:_HALO + tm, :] = um[:, 2 * c:]
        for r0 in range(0, tm, _CHUNK):
            acc = jnp.broadcast_to(cb_ref[...], (_CHUNK, c))
            for k in range(kw):
                off = r0 + _HALO - (kw - 1) + k
                acc = acc + cw_ref[k:k + 1, :] * hgbuf[off:off + _CHUNK, :]
            hc_ref[r0:r0 + _CHUNK, :] = acc
            mu = jnp.mean(acc, axis=-1, keepdims=True)
            xc = acc - mu
            var = jnp.mean(xc * xc, axis=-1, keepdims=True)
            hl = xc * lax.rsqrt(var + EPS) * lg_ref[...] + lb_ref[...]
            y_ref[r0:r0 + _CHUNK, 0:c] = (hl * _sigmoid(hl)).astype(y_ref.dtype)
        pos = ((i % tps) * tm).astype(f32) + lax.broadcasted_iota(jnp.int32, (tm, 1), 0).astype(f32)
        for gi, w in enumerate(POOL_WINDOWS):
            sl = slice(gi * gd, (gi + 1) * gd)
            v = pbuf[_HALO:_HALO + tm, sl]
            s = v
            for j in range(1, w):
                s = s + pbuf[_HALO - j:_HALO - j + tm, sl]
            pooled = s / _pool_counts(pos, w) - v
            mixed = _dot(pooled.astype(_ACT), pw_ref[gi].astype(_ACT), _NN)
            y_ref[:, c + gi * gd:c + (gi + 1) * gd] = (mixed * ps_ref[:, sl]).astype(y_ref.dtype)

    hb = tm // _HALO
    full = lambda shape: pl.BlockSpec(shape, lambda i: (0,) * len(shape))
    return pl.pallas_call(
        body, name="mix_fwd", grid=(t // tm,),
        in_specs=[pl.BlockSpec((tm, c3), lambda i: (i, 0)),
                  pl.BlockSpec((_HALO, c3), lambda i: (jnp.maximum(i * hb - 1, 0), 0)),
                  full((_HALO, c)), full((1, c)), full((1, c)), full((1, c)), full((len(POOL_WINDOWS), gd, gd)), full((1, c))],
        out_specs=[pl.BlockSpec((tm, 2 * c), lambda i: (i, 0)), pl.BlockSpec((tm, c), lambda i: (i, 0))],
        out_shape=[jax.ShapeDtypeStruct((t, 2 * c), _ACT), jax.ShapeDtypeStruct((t, c), f32)],
        scratch_shapes=[pltpu.VMEM((_HALO + tm, c), f32), pltpu.VMEM((_HALO + tm, c), f32)],
        compiler_params=_params(1),
    )(u, u, cw, cb, lg, lb, pw, ps)


def _mix_bwd_norm(hc, dy, lg, lb, seq):
    t, c = hc.shape
    tm = min(_CONV_ROWS, seq)

    def body(hc_ref, dy_ref, lg_ref, lb_ref, dhc_ref, sums_ref):
        @pl.when(pl.program_id(0) == 0)
        def _():
            sums_ref[...] = jnp.zeros_like(sums_ref)

        hcv = hc_ref[...]
        mu = jnp.mean(hcv, axis=-1, keepdims=True)
        xc = hcv - mu
        rstd = lax.rsqrt(jnp.mean(xc * xc, axis=-1, keepdims=True) + EPS)
        n = xc * rstd
        hl = n * lg_ref[...] + lb_ref[...]
        sg = _sigmoid(hl)
        dhl = dy_ref[...].astype(f32) * (sg * (1.0 + hl * (1.0 - sg)))
        dn = dhl * lg_ref[...]
        dhc = rstd * (dn - jnp.mean(dn, axis=-1, keepdims=True) - n * jnp.mean(dn * n, axis=-1, keepdims=True))
        dhc_ref[...] = dhc
        sums_ref[0:1, :] += jnp.sum(dhl * n, axis=0, keepdims=True)
        sums_ref[1:2, :] += jnp.sum(dhl, axis=0, keepdims=True)
        sums_ref[2:3, :] += jnp.sum(dhc, axis=0, keepdims=True)

    row = pl.BlockSpec((tm, c), lambda i: (i, 0))
    vec = pl.BlockSpec((1, c), lambda i: (0, 0))
    return pl.pallas_call(
        body, name="mix_bwd_norm", grid=(t // tm,), in_specs=[row, row, vec, vec],
        out_specs=[row, pl.BlockSpec((8, c), lambda i: (0, 0))],
        out_shape=[jax.ShapeDtypeStruct((t, c), f32), jax.ShapeDtypeStruct((8, c), f32)],
        compiler_params=_params(1),
    )(hc, dy, lg, lb)


def _mix_bwd_taps(u, dhc, dy, cw, pw, ps, seq):
    t, c3 = u.shape
    c = c3 // 3
    kw = 31
    tm = min(_CONV_ROWS, seq)
    tps = seq // tm
    ng = len(POOL_WINDOWS)
    gd = c // ng
    nh = 16

    def body(u_ref, uh_ref, dhc_ref, dhcn_ref, dy_ref, dyn_ref, cw_ref, pw_ref, ps_ref,
             du_ref, dcw_ref, dps_ref, dpw_ref, hgbuf, dcbuf, pbuf, dpbuf):
        i = pl.program_id(0)
        keep_prev = jnp.where(i % tps == 0, 0.0, 1.0)
        keep_next = jnp.where(i % tps == tps - 1, 0.0, 1.0)

        @pl.when(i == 0)
        def _():
            dcw_ref[...] = jnp.zeros_like(dcw_ref)
            dps_ref[...] = jnp.zeros_like(dps_ref)
            dpw_ref[...] = jnp.zeros_like(dpw_ref)

        uh = uh_ref[...].astype(f32) * keep_prev
        hgbuf[0:_HALO, :] = uh[:, 0:c] * _sigmoid(uh[:, c:2 * c])
        pbuf[0:_HALO, :] = uh[:, 2 * c:]
        um = u_ref[...].astype(f32)
        hgbuf[_HALO:_HALO + tm, :] = um[:, 0:c] * _sigmoid(um[:, c:2 * c])
        pbuf[_HALO:_HALO + tm, :] = um[:, 2 * c:]
        dcbuf[0:tm, :] = dhc_ref[...]
        dcbuf[tm:tm + _HALO, :] = dhcn_ref[...] * keep_next
        tap_sums = [None] * kw
        for r0 in range(0, tm, _CHUNK):
            dh = dcbuf[r0:r0 + _CHUNK, :]
            acc = jnp.zeros((_CHUNK, c), f32)
            for k in range(kw):
                off = r0 + _HALO - (kw - 1) + k
                part = jnp.sum(dh * hgbuf[off:off + _CHUNK, :], axis=0, keepdims=True)
                tap_sums[k] = part if tap_sums[k] is None else tap_sums[k] + part
                fwd = r0 + (kw - 1) - k
                acc = acc + cw_ref[k:k + 1, :] * dcbuf[fwd:fwd + _CHUNK, :]
            val = u_ref[r0:r0 + _CHUNK, 0:c].astype(f32)
            sg = _sigmoid(u_ref[r0:r0 + _CHUNK, c:2 * c].astype(f32))
            du_ref[r0:r0 + _CHUNK, 0:c] = (acc * sg).astype(du_ref.dtype)
            du_ref[r0:r0 + _CHUNK, c:2 * c] = (acc * val * sg * (1.0 - sg)).astype(du_ref.dtype)
        for k in range(kw):
            dcw_ref[k:k + 1, :] += tap_sums[k]
        base = ((i % tps) * tm).astype(f32)
        pos = base + lax.broadcasted_iota(jnp.int32, (tm, 1), 0).astype(f32)
        pos_next = base + float(tm) + lax.broadcasted_iota(jnp.int32, (nh, 1), 0).astype(f32)
        for gi, w in enumerate(POOL_WINDOWS):
            sl = slice(gi * gd, (gi + 1) * gd)
            v = pbuf[_HALO:_HALO + tm, sl]
            s = v
            for j in range(1, w):
                s = s + pbuf[_HALO - j:_HALO - j + tm, sl]
            cnt = _pool_counts(pos, w)
            pooled = (s / cnt - v).astype(_ACT)
            pwg = pw_ref[gi].astype(_ACT)
            mixed = _dot(pooled, pwg, _NN)
            dyp = dy_ref[:, sl].astype(f32)
            dps_ref[0:1, sl] += jnp.sum(dyp * mixed, axis=0, keepdims=True)
            dmix = (dyp * ps_ref[:, sl]).astype(_ACT)
            dpw_ref[gi] += _dot(pooled, dmix, _TN)
            dmix_next = (dyn_ref[:, sl].astype(f32) * ps_ref[:, sl] * keep_next).astype(_ACT)
            dpool = _dot(dmix, pwg, _NT)
            dpbuf[0:tm, sl] = dpool / cnt
            dpbuf[tm:tm + nh, sl] = _dot(dmix_next, pwg, _NT) / _pool_counts(pos_next, w)
            acc = -dpool
            for j in range(w):
                acc = acc + dpbuf[j:j + tm, sl]
            du_ref[:, 2 * c + gi * gd:2 * c + (gi + 1) * gd] = acc.astype(du_ref.dtype)

    hb = tm // _HALO
    n_halo = t // _HALO
    n_nh = t // nh
    full = lambda shape: pl.BlockSpec(shape, lambda i: (0,) * len(shape))
    return pl.pallas_call(
        body, name="mix_bwd_taps", grid=(t // tm,),
        in_specs=[pl.BlockSpec((tm, c3), lambda i: (i, 0)),
                  pl.BlockSpec((_HALO, c3), lambda i: (jnp.maximum(i * hb - 1, 0), 0)),
                  pl.BlockSpec((tm, c), lambda i: (i, 0)),
                  pl.BlockSpec((_HALO, c), lambda i: (jnp.minimum((i + 1) * hb, n_halo - 1), 0)),
                  pl.BlockSpec((tm, c), lambda i: (i, 1)),
                  pl.BlockSpec((nh, c), lambda i: (jnp.minimum((i + 1) * (tm // nh), n_nh - 1), 1)),
                  full((_HALO, c)), full((ng, gd, gd)), full((1, c))],
        out_specs=[pl.BlockSpec((tm, c3), lambda i: (i, 0)), full((_HALO, c)), full((8, c)), full((ng, gd, gd))],
        out_shape=[jax.ShapeDtypeStruct((t, c3), _ACT), jax.ShapeDtypeStruct((_HALO, c), f32),
                   jax.ShapeDtypeStruct((8, c), f32), jax.ShapeDtypeStruct((ng, gd, gd), f32)],
        scratch_shapes=[pltpu.VMEM((_HALO + tm, c), f32), pltpu.VMEM((tm + _HALO, c), f32),
                        pltpu.VMEM((_HALO + tm, c), f32), pltpu.VMEM((tm + nh, c), f32)],
        compiler_params=_params(1),
    )(u, u, dhc, dhc, dy, dy, cw, pw, ps)


def _attn_fwd(q, kv, n_seq, seq, n_mem):
    t, d = q.shape
    dh = d // XATTN_HEADS
    tq = min(512, seq)
    nq = seq // tq
    scale = dh ** -0.5

    def body(q_ref, k_ref, v_ref, o_ref):
        s = _dot(q_ref[...], k_ref[...], _NT) * scale
        e = jnp.exp(s - jnp.max(s, axis=-1, keepdims=True))
        p = e / jnp.sum(e, axis=-1, keepdims=True)
        o_ref[...] = _dot(p.astype(_ACT), v_ref[...], _NN).astype(o_ref.dtype)

    qs = pl.BlockSpec((tq, dh), lambda b, h, i: (b * nq + i, h))
    return pl.pallas_call(
        body, name="attn_fwd", grid=(n_seq, XATTN_HEADS, nq),
        in_specs=[qs, pl.BlockSpec((n_mem, dh), lambda b, h, i: (b, h)),
                  pl.BlockSpec((n_mem, dh), lambda b, h, i: (b, XATTN_HEADS + h))],
        out_specs=qs, out_shape=jax.ShapeDtypeStruct((t, d), _ACT), compiler_params=_params(3),
    )(q, kv, kv)


def _attn_bwd(q, kv, do, n_seq, seq, n_mem):
    t, d = q.shape
    dh = d // XATTN_HEADS
    tq = min(512, seq)
    nq = seq // tq
    scale = dh ** -0.5

    def body(q_ref, k_ref, v_ref, do_ref, dq_ref, dk_ref, dv_ref, dk_acc, dv_acc):
        i = pl.program_id(2)
        qv = q_ref[...]
        kvv = k_ref[...]
        dov = do_ref[...]
        s = _dot(qv, kvv, _NT) * scale
        e = jnp.exp(s - jnp.max(s, axis=-1, keepdims=True))
        p = e / jnp.sum(e, axis=-1, keepdims=True)
        dp = _dot(dov, v_ref[...], _NT)
        ds = (p * (dp - jnp.sum(dp * p, axis=-1, keepdims=True)) * scale).astype(_ACT)
        dq_ref[...] = _dot(ds, kvv, _NN).astype(dq_ref.dtype)
        dk_part = _dot(ds, qv, _TN)
        dv_part = _dot(p.astype(_ACT), dov, _TN)

        @pl.when(i == 0)
        def _():
            dk_acc[...] = dk_part
            dv_acc[...] = dv_part

        @pl.when(i > 0)
        def _():
            dk_acc[...] += dk_part
            dv_acc[...] += dv_part

        @pl.when(i == nq - 1)
        def _():
            dk_ref[...] = dk_acc[...].astype(dk_ref.dtype)
            dv_ref[...] = dv_acc[...].astype(dv_ref.dtype)

    qs = pl.BlockSpec((tq, dh), lambda b, h, i: (b * nq + i, h))
    ms = pl.BlockSpec((n_mem, dh), lambda b, h, i: (b, h))
    return pl.pallas_call(
        body, name="attn_bwd", grid=(n_seq, XATTN_HEADS, nq),
        in_specs=[qs, ms, pl.BlockSpec((n_mem, dh), lambda b, h, i: (b, XATTN_HEADS + h)), qs],
        out_specs=[qs, ms, ms],
        out_shape=[jax.ShapeDtypeStruct((t, d), _ACT), jax.ShapeDtypeStruct((n_seq * n_mem, d), _ACT),
                   jax.ShapeDtypeStruct((n_seq * n_mem, d), _ACT)],
        scratch_shapes=[pltpu.VMEM((n_mem, dh), f32), pltpu.VMEM((n_mem, dh), f32)],
        compiler_params=_params(3),
    )(q, kv, kv, do)


_FFN_ROWS = 1024
_FFN_COLS = 256
_FFN_HALO = 16


def _window(buf, g, start, rows):
    return buf[g, pl.ds(start, rows + 8), :]


def _shift_matrix(rows):
    r = jnp.arange(3 * rows)
    col = r % rows + _FFN_HALO - 2 + r // rows
    return (jnp.arange(rows + _FFN_HALO)[None, :] == col[:, None]).astype(_ACT)


def _taps(shift_ref, win, rows):
    shifted = _dot(shift_ref[...], win, _NN)
    return [shifted[k * rows:(k + 1) * rows, :] for k in range(3)]


def _conv3(b_ref, w_ref, taps):
    acc = b_ref[...] + w_ref[0:1, :] * taps[0]
    for k in (1, 2):
        acc = acc + w_ref[k:k + 1, :] * taps[k]
    return acc


def _ffn_gate_fwd(up, fw, fb, seq):
    _, t, f = up.shape
    tm = min(_FFN_ROWS, seq)
    tps = seq // tm
    tc = _FFN_COLS
    nc = f // tc
    hl = _FFN_HALO

    def body(up_ref, uph_ref, wg_ref, wv_ref, bg_ref, bv_ref, shift_ref, a_ref):
        i = pl.program_id(1)
        before = uph_ref[...]
        before = jnp.where(i % tps == 0, jnp.zeros_like(before), before)

        def chunk(r0, wins):
            gate = _conv3(bg_ref, wg_ref, _taps(shift_ref, wins[0], _CHUNK))
            val = _conv3(bv_ref, wv_ref, _taps(shift_ref, wins[1], _CHUNK))
            a_ref[pl.ds(r0, _CHUNK), :] = (gate * _sigmoid(gate) * val).astype(a_ref.dtype)

        chunk(0, [jnp.concatenate([before[g], up_ref[g, 0:_CHUNK, :]], axis=0) for g in range(2)])

        def later(ci, carry):
            r0 = pl.multiple_of(ci * _CHUNK, _CHUNK)
            chunk(r0, [up_ref[g, pl.ds(r0 - hl, _CHUNK + hl), :] for g in range(2)])
            return carry

        lax.fori_loop(1, tm // _CHUNK, later, 0)

    hb = tm // hl
    shift = _shift_matrix(_CHUNK)
    return pl.pallas_call(
        body, name="ffn_gate_fwd", grid=(nc, t // tm),
        in_specs=[pl.BlockSpec((2, tm, tc), lambda j, i: (0, i, j)),
                  pl.BlockSpec((2, hl, tc), lambda j, i: (0, jnp.maximum(i * hb - 1, 0), j)),
                  pl.BlockSpec((8, tc), lambda j, i: (0, j)), pl.BlockSpec((8, tc), lambda j, i: (0, nc + j)),
                  pl.BlockSpec((1, tc), lambda j, i: (0, j)), pl.BlockSpec((1, tc), lambda j, i: (0, nc + j)),
                  pl.BlockSpec(shift.shape, lambda j, i: (0, 0))],
        out_specs=pl.BlockSpec((tm, tc), lambda j, i: (i, j)),
        out_shape=jax.ShapeDtypeStruct((t, f), _ACT), compiler_params=_params(2),
    )(up, up, fw, fw, fb, fb, shift)


def _ffn_gate_bwd(up, da, fw, fb, seq):
    _, t, f = up.shape
    tm = min(_FFN_ROWS, seq)
    tps = seq // tm
    tc = _FFN_COLS
    nc = f // tc
    hl = _FFN_HALO

    def body(up_ref, uph_ref, upn_ref, da_ref, dan_ref, wg_ref, wv_ref, bg_ref, bv_ref, shift_ref, shift_end_ref,
             dup_ref, sg_ref, sv_ref, dbuf, sums):
        i = pl.program_id(1)
        at_start = i % tps == 0
        at_end = i % tps == tps - 1

        @pl.when(i == 0)
        def _():
            sg_ref[...] = jnp.zeros_like(sg_ref)
            sv_ref[...] = jnp.zeros_like(sv_ref)

        sums[...] = jnp.zeros_like(sums)
        before = uph_ref[...]
        before = jnp.where(at_start, jnp.zeros_like(before), before)
        after = upn_ref[...]
        after = jnp.where(at_end, jnp.zeros_like(after), after)
        w_refs = (wg_ref, wv_ref)
        b_refs = (bg_ref, bv_ref)

        def grads(r0, rows, wins, dav, s_ref, count):
            taps = [_taps(s_ref, wins[g], rows) for g in range(2)]
            gate, val = [_conv3(b_refs[g], w_refs[g], taps[g]) for g in range(2)]
            sg = _sigmoid(gate)
            douts = (dav * val * (sg * (1.0 + gate * (1.0 - sg))), dav * (gate * sg))
            for g in range(2):
                dbuf[g, pl.ds(r0, rows), :] = douts[g]
                if count:
                    sums[g, 0] += douts[g].reshape(rows // 8, 8, tc).sum(axis=0)
                    for k in range(3):
                        sums[g, 1 + k] += (douts[g] * taps[g][k]).reshape(rows // 8, 8, tc).sum(axis=0)

        grads(0, _CHUNK, [jnp.concatenate([before[g], up_ref[g, 0:_CHUNK, :]], axis=0) for g in range(2)],
              da_ref[0:_CHUNK, :].astype(f32), shift_ref, True)

        def first(ci, carry):
            r0 = pl.multiple_of(ci * _CHUNK, _CHUNK)
            grads(r0, _CHUNK, [up_ref[g, pl.ds(r0 - hl, _CHUNK + hl), :] for g in range(2)],
                  da_ref[pl.ds(r0, _CHUNK), :].astype(f32), shift_ref, True)
            return carry

        lax.fori_loop(1, tm // _CHUNK, first, 0)
        da_after = dan_ref[...].astype(f32)
        grads(tm, hl, [jnp.concatenate([up_ref[g, tm - hl:tm, :], after[g]], axis=0) for g in range(2)],
              jnp.where(at_end, jnp.zeros_like(da_after), da_after), shift_end_ref, False)

        def second(ci, carry):
            r0 = pl.multiple_of(ci * _CHUNK, _CHUNK)
            for g in range(2):
                win = _window(dbuf, g, r0, _CHUNK)
                acc = jnp.zeros((_CHUNK, tc), f32)
                for k in range(3):
                    acc = acc + w_refs[g][k:k + 1, :] * win[2 - k:2 - k + _CHUNK, :]
                dup_ref[g, pl.ds(r0, _CHUNK), :] = acc.astype(dup_ref.dtype)
            return carry

        lax.fori_loop(0, tm // _CHUNK, second, 0)
        for g, s_ref in enumerate((sg_ref, sv_ref)):
            for r in range(4):
                s_ref[r:r + 1, :] += jnp.sum(sums[g, r], axis=0, keepdims=True)

    hb = tm // hl
    n_halo = t // hl
    shift = _shift_matrix(_CHUNK)
    shift_end = _shift_matrix(hl)
    return pl.pallas_call(
        body, name="ffn_gate_bwd", grid=(nc, t // tm),
        in_specs=[pl.BlockSpec((2, tm, tc), lambda j, i: (0, i, j)),
                  pl.BlockSpec((2, hl, tc), lambda j, i: (0, jnp.maximum(i * hb - 1, 0), j)),
                  pl.BlockSpec((2, hl, tc), lambda j, i: (0, jnp.minimum((i + 1) * hb, n_halo - 1), j)),
                  pl.BlockSpec((tm, tc), lambda j, i: (i, j)),
                  pl.BlockSpec((hl, tc), lambda j, i: (jnp.minimum((i + 1) * hb, n_halo - 1), j)),
                  pl.BlockSpec((8, tc), lambda j, i: (0, j)), pl.BlockSpec((8, tc), lambda j, i: (0, nc + j)),
                  pl.BlockSpec((1, tc), lambda j, i: (0, j)), pl.BlockSpec((1, tc), lambda j, i: (0, nc + j)),
                  pl.BlockSpec(shift.shape, lambda j, i: (0, 0)), pl.BlockSpec(shift_end.shape, lambda j, i: (0, 0))],
        out_specs=[pl.BlockSpec((2, tm, tc), lambda j, i: (0, i, j)),
                   pl.BlockSpec((8, tc), lambda j, i: (0, j)), pl.BlockSpec((8, tc), lambda j, i: (0, j))],
        out_shape=[jax.ShapeDtypeStruct((2, t, f), _ACT), jax.ShapeDtypeStruct((8, f), f32), jax.ShapeDtypeStruct((8, f), f32)],
        scratch_shapes=[pltpu.VMEM((2, tm + hl, tc), f32), pltpu.VMEM((2, 4, 8, tc), f32)],
        compiler_params=_params(2),
    )(up, up, up, da, da, fw, fw, fb, fb, shift, shift_end)


def _adamw_math(w, g, m, v):
    m = ADAM_B1 * m + (1.0 - ADAM_B1) * g
    v = ADAM_B2 * v + (1.0 - ADAM_B2) * (g * g)
    m_hat = m / (1.0 - ADAM_B1 ** ADAM_STEP)
    v_hat = v / (1.0 - ADAM_B2 ** ADAM_STEP)
    delta = -ADAM_LR * (m_hat / (jnp.sqrt(v_hat) + ADAM_EPS) + ADAM_WD * w)
    return delta, m, v


def _adamw_shard(name, w, g, m, v):
    _, r, c = w.shape
    tr = next((cand for cand in (256, 176, 128, 64, 32, 16, 8) if r % cand == 0), r)

    def body(w_ref, g_ref, m_ref, v_ref, d_ref, mo_ref, vo_ref):
        d, mn, vn = _adamw_math(w_ref[...], g_ref[...], m_ref[...], v_ref[...])
        d_ref[...] = d
        mo_ref[...] = mn
        vo_ref[...] = vn

    s3 = pl.BlockSpec((None, tr, c), lambda i: (0, i, 0))
    s2 = pl.BlockSpec((tr, c), lambda i: (i, 0))
    shp = jax.ShapeDtypeStruct(w.shape, f32)
    return pl.pallas_call(
        body, name=name, grid=(r // tr,), in_specs=[s3, s2, s3, s3], out_specs=[s3, s3, s3], out_shape=[shp, shp, shp],
        compiler_params=_params(1),
    )(w, g, m, v)


def _adamw_small(quads):
    n = len(quads)

    def body(*refs):
        ins, outs = refs[:4 * n], refs[4 * n:]
        for p in range(n):
            w_ref, g_ref, m_ref, v_ref = ins[4 * p:4 * p + 4]
            d, mn, vn = _adamw_math(w_ref[...], g_ref[...], m_ref[...], v_ref[...])
            outs[3 * p][...] = d
            outs[3 * p + 1][...] = mn
            outs[3 * p + 2][...] = vn

    flat = [a for q in quads for a in q]
    shapes = [jax.ShapeDtypeStruct(q[0].shape, f32) for q in quads for _ in range(3)]
    outs = pl.pallas_call(
        body, name="adamw_small", in_specs=[_VMEM] * (4 * n), out_specs=[_VMEM] * (3 * n), out_shape=shapes,
        compiler_params=pltpu.CompilerParams(vmem_limit_bytes=_VMEM_LIMIT_BYTES),
    )(*flat)
    return [tuple(outs[3 * p:3 * p + 3]) for p in range(n)]


def _sum_pairs(name, place, grads, got):
    _, r, c = grads.shape

    def body(place_ref, a_ref, b_ref, o_ref):
        o_ref[...] = (a_ref[...].astype(f32) + b_ref[...].astype(f32)).astype(o_ref.dtype)

    grid_spec = pltpu.PrefetchScalarGridSpec(
        num_scalar_prefetch=1, grid=(4,),
        in_specs=[pl.BlockSpec((None, r, c), lambda i, p: (2 * i + p[1], 0, 0)), pl.BlockSpec((None, r, c), lambda i, p: (i, 0, 0))],
        out_specs=pl.BlockSpec((None, r, c), lambda i, p: (i, 0, 0)))
    return pl.pallas_call(body, name=name, grid_spec=grid_spec, out_shape=jax.ShapeDtypeStruct((4, r, c), _ACT),
                          compiler_params=_params(1))(place, grads, got)


def _sum_four(name, place, sums, got):
    _, r, c = sums.shape

    def body(place_ref, o_ref, g_ref, f_ref):
        s = o_ref[...].astype(f32) + g_ref[0].astype(f32)
        s = s + g_ref[1].astype(f32)
        f_ref[...] = s + g_ref[2].astype(f32)

    grid_spec = pltpu.PrefetchScalarGridSpec(
        num_scalar_prefetch=1, grid=(1,),
        in_specs=[pl.BlockSpec((None, r, c), lambda i, p: (p[0], 0, 0)), pl.BlockSpec((3, r, c), lambda i, p: (0, 0, 0))],
        out_specs=pl.BlockSpec((None, r, c), lambda i, p: (p[1], 0, 0)))
    return pl.pallas_call(body, name=name, grid_spec=grid_spec, out_shape=jax.ShapeDtypeStruct((2, r, c), f32),
                          compiler_params=_params(1))(place, sums, got)


def _place():
    return lax.axis_index("x"), lax.axis_index("y"), lax.axis_index("c")


def _other_chips(x, y):
    return [(1 - x, y), (x, 1 - y), (1 - x, 1 - y)]


def _remote(src, dst, send_sem, recv_sem, to):
    return pltpu.make_async_remote_copy(src_ref=src, dst_ref=dst, send_sem=send_sem, recv_sem=recv_sem,
                                        device_id=to, device_id_type=_MESH)


def _place_shards(place, shards, col_sharded):
    n = len(shards)
    steps = 4

    def body(place_ref, *refs):
        for src, dst in zip(refs[:n], refs[n:]):
            dst[...] = src[...].astype(dst.dtype)

    in_specs, out_specs, out_shape = [], [], []
    for w, col in zip(shards, col_sharded):
        r, cs = w.shape
        tr = r // steps
        in_specs.append(pl.BlockSpec((tr, cs), lambda i, p: (i, 0)))
        if col:
            out_specs.append(pl.BlockSpec((tr, cs), lambda i, p: (i, p[0])))
            out_shape.append(jax.ShapeDtypeStruct((r, 4 * cs), _ACT))
        else:
            out_specs.append(pl.BlockSpec((tr, cs), lambda i, p: (p[0] * steps + i, 0)))
            out_shape.append(jax.ShapeDtypeStruct((4 * r, cs), _ACT))
    grid_spec = pltpu.PrefetchScalarGridSpec(num_scalar_prefetch=1, grid=(steps,), in_specs=in_specs, out_specs=out_specs)
    return pl.pallas_call(body, name="place_shards", grid_spec=grid_spec, out_shape=out_shape,
                          compiler_params=_params(1))(place, *shards)


def _shard_of(ref, col_sharded, s):
    rows, cols = ref.shape
    if col_sharded:
        return ref.at[:, pl.ds(s * (cols // 4), cols // 4)]
    return ref.at[pl.ds(s * (rows // 4), rows // 4), :]


def _part_of(ref, col_sharded, whole, s, h):
    if whole:
        return _shard_of(ref, col_sharded, s)
    rows, cols = ref.shape
    if col_sharded:
        return ref.at[pl.ds(h * (rows // 2), rows // 2), pl.ds(s * (cols // 4), cols // 4)]
    return ref.at[pl.ds((2 * s + h) * (rows // 8), rows // 8), :]


def _allgather_start(bufs, col_sharded, whole, groups):
    n = len(bufs)
    ng = len(groups)

    def body(*refs):
        out = refs[n:2 * n]
        sems = refs[2 * n:]
        x, y, c = _place()
        for g, members in enumerate(groups):
            for i, w in enumerate(members):
                mine = _part_of(out[w], col_sharded[w], whole[w], 2 * x + y, c)
                for j, chip in enumerate(_other_chips(x, y)):
                    _remote(mine, mine, sems[2 * g].at[3 * i + j], sems[2 * g + 1].at[3 * i + j], (*chip, c)).start()

    sem_shapes = [pltpu.SemaphoreType.DMA((3 * len(m),)) for m in groups for _ in range(2)]
    outs = pl.pallas_call(
        body, name="allgather_start", in_specs=[_HBM] * n, out_specs=[_HBM] * n + [_SEM] * (2 * ng),
        out_shape=[pltpu.HBM(b.shape, b.dtype) for b in bufs] + sem_shapes,
        input_output_aliases={i: i for i in range(n)},
        compiler_params=pltpu.CompilerParams(has_side_effects=_EFFECT),
    )(*[pltpu.with_memory_space_constraint(b, pltpu.HBM) for b in bufs])
    return list(outs[:n]), [(outs[n + 2 * g], outs[n + 2 * g + 1]) for g in range(ng)]


def _allgather_relay(name, bufs, col_sharded, whole, sems, after):
    n = len(bufs)

    def body(*refs):
        buf = refs[:n]
        send, recv = refs[n], refs[n + 1]
        out = refs[n + 3:2 * n + 3]
        to_sibling, from_sibling = refs[2 * n + 3:]
        x, y, c = _place()
        for i in range(n):
            mine = _part_of(buf[i], col_sharded[i], whole[i], 2 * x + y, c)
            for j, chip in enumerate(_other_chips(x, y)):
                landed = _part_of(buf[i], col_sharded[i], whole[i], 2 * chip[0] + chip[1], c)
                cp = _remote(mine, landed, send.at[3 * i + j], recv.at[3 * i + j], (*chip, c))
                cp.wait_send()
                cp.wait_recv()
        for i in range(n):
            if not whole[i]:
                for j, chip in enumerate(_other_chips(x, y)):
                    landed = _part_of(out[i], col_sharded[i], False, 2 * chip[0] + chip[1], c)
                    _remote(landed, landed, to_sibling.at[3 * i + j], from_sibling.at[3 * i + j], (x, y, 1 - c)).start()

    outs = pl.pallas_call(
        body, name=name, in_specs=[_HBM] * n + [_SEM, _SEM, _ANY], out_specs=[_HBM] * n + [_SEM, _SEM],
        out_shape=[pltpu.HBM(b.shape, b.dtype) for b in bufs] + [pltpu.SemaphoreType.DMA((3 * n,))] * 2,
        input_output_aliases={i: i for i in range(n)},
        compiler_params=pltpu.CompilerParams(has_side_effects=_EFFECT),
    )(*bufs, *sems, after)
    return list(outs[:n]), (outs[n], outs[n + 1])


def _allgather_wait(name, bufs, col_sharded, whole, sems, after):
    n = len(bufs)

    def body(*refs):
        buf = refs[:n]
        to_sibling, from_sibling = refs[n], refs[n + 1]
        x, y, c = _place()
        for i in range(n):
            if not whole[i]:
                for j, chip in enumerate(_other_chips(x, y)):
                    sent = _part_of(buf[i], col_sharded[i], False, 2 * chip[0] + chip[1], c)
                    landed = _part_of(buf[i], col_sharded[i], False, 2 * chip[0] + chip[1], 1 - c)
                    cp = _remote(sent, landed, to_sibling.at[3 * i + j], from_sibling.at[3 * i + j], (x, y, 1 - c))
                    cp.wait_send()
                    cp.wait_recv()

    return pl.pallas_call(
        body, name=name, in_specs=[_HBM] * n + [_SEM, _SEM, _ANY], out_specs=[_HBM] * n,
        out_shape=[pltpu.HBM(b.shape, b.dtype) for b in bufs],
        input_output_aliases={i: i for i in range(n)},
        compiler_params=pltpu.CompilerParams(has_side_effects=_EFFECT),
    )(*bufs, *sems, after)


def _exchange_pair_halves(name, grads):
    nw = len(grads)

    def body(*refs):
        src = refs[:nw]
        got = refs[nw:2 * nw]
        send_sem, recv_sem = refs[2 * nw:]
        x, y, c = _place()
        sends = []
        for w in range(nw):
            for s in range(4):
                rc = _remote(src[w].at[2 * s + 1 - c], got[w].at[s], send_sem.at[4 * w + s], recv_sem.at[4 * w + s], (x, y, 1 - c))
                rc.start()
                sends.append(rc)
        for rc in sends:
            rc.wait_recv()
        for rc in sends:
            rc.wait_send()

    return pl.pallas_call(
        body, name=name, in_specs=[_ANY] * nw, out_specs=[_ANY] * nw,
        out_shape=[jax.ShapeDtypeStruct((4,) + g.shape[1:], g.dtype) for g in grads],
        scratch_shapes=[pltpu.SemaphoreType.DMA((4 * nw,)), pltpu.SemaphoreType.DMA((4 * nw,))],
    )(*grads)


def _chip_exchange_start(name, sums):
    nw = len(sums)
    lands = [lax.empty((3,) + s.shape[1:], s.dtype) for s in sums]

    def body(*refs):
        src = refs[2 * nw:3 * nw]
        got = refs[3 * nw:4 * nw]
        send, recv, token = refs[4 * nw:]
        x, y, c = _place()
        for w in range(nw):
            for j, chip in enumerate(_other_chips(x, y)):
                _remote(src[w].at[2 * chip[0] + chip[1]], got[w].at[j], send.at[3 * w + j], recv.at[3 * w + j], (*chip, c)).start()
        token[...] = jnp.zeros_like(token)

    outs = pl.pallas_call(
        body, name=name, in_specs=[_HBM] * (2 * nw), out_specs=[_HBM] * (2 * nw) + [_SEM, _SEM, _VMEM],
        out_shape=[pltpu.HBM(a.shape, a.dtype) for a in list(sums) + lands]
        + [pltpu.SemaphoreType.DMA((3 * nw,)), pltpu.SemaphoreType.DMA((3 * nw,)), jax.ShapeDtypeStruct((8, 128), f32)],
        input_output_aliases={i: i for i in range(2 * nw)},
        compiler_params=pltpu.CompilerParams(has_side_effects=_EFFECT),
    )(*[pltpu.with_memory_space_constraint(a, pltpu.HBM) for a in list(sums) + lands])
    return list(outs[:nw]), list(outs[nw:2 * nw]), (outs[2 * nw], outs[2 * nw + 1]), outs[2 * nw + 2]


def _chip_exchange_wait(name, sums, got, sems, after):
    nw = len(sums)

    def body(*refs):
        src = refs[:nw]
        land = refs[nw:2 * nw]
        send, recv = refs[2 * nw], refs[2 * nw + 1]
        x, y, c = _place()
        for w in range(nw):
            for j, chip in enumerate(_other_chips(x, y)):
                cp = _remote(src[w].at[2 * chip[0] + chip[1]], land[w].at[j], send.at[3 * w + j], recv.at[3 * w + j], (*chip, c))
                cp.wait_send()
                cp.wait_recv()

    outs = pl.pallas_call(
        body, name=name, in_specs=[_HBM] * (2 * nw) + [_SEM, _SEM, _ANY], out_specs=[_HBM] * (2 * nw),
        out_shape=[pltpu.HBM(a.shape, a.dtype) for a in list(sums) + list(got)],
        input_output_aliases={i: i for i in range(2 * nw)},
        compiler_params=pltpu.CompilerParams(has_side_effects=_EFFECT),
    )(*sums, *got, *sems, after)
    return list(outs[:nw]), list(outs[nw:])


def _swap_halves(finals):
    nw = len(finals)

    def body(*refs):
        buf = refs[nw:2 * nw]
        send_sem, recv_sem = refs[2 * nw:]
        x, y, c = _place()
        sends = []
        for w in range(nw):
            rc = _remote(buf[w].at[c], buf[w].at[c], send_sem.at[w], recv_sem.at[w], (x, y, 1 - c))
            rc.start()
            sends.append(rc)
        for w in range(nw):
            _remote(buf[w].at[1 - c], buf[w].at[1 - c], send_sem.at[w], recv_sem.at[w], (x, y, c)).wait_recv()
        for rc in sends:
            rc.wait_send()

    return pl.pallas_call(
        body, name="rs_swap_halves", in_specs=[_ANY] * nw, out_specs=[_ANY] * nw,
        out_shape=[jax.ShapeDtypeStruct(g.shape, g.dtype) for g in finals],
        input_output_aliases={i: i for i in range(nw)},
        scratch_shapes=[pltpu.SemaphoreType.DMA((nw,)), pltpu.SemaphoreType.DMA((nw,))],
    )(*finals)


def _half_slices(shape, h):
    rows, cols = shape
    if cols % 256 == 0:
        return (slice(None), slice(h * (cols // 2), (h + 1) * (cols // 2)))
    return (slice(h * (rows // 2), (h + 1) * (rows // 2)), slice(None))


def _allreduce_small(parts):
    n = len(parts)

    def body(*refs):
        src = refs[:n]
        out = refs[n:2 * n]
        sib = refs[2 * n:3 * n]
        chip_sum = refs[3 * n:4 * n]
        slots = refs[4 * n:5 * n]
        pair_send, pair_recv, ici_send, ici_recv, swap_send, swap_recv = refs[5 * n:]
        x, y, c = _place()
        me_chip = 2 * x + y
        chips = _other_chips(x, y)
        pairs = [_remote(src[a], sib[a], pair_send.at[a], pair_recv.at[a], (x, y, 1 - c)) for a in range(n)]
        for rc in pairs:
            rc.start()
        for a in range(n):
            pairs[a].wait_recv()
            chip_sum[a][...] = src[a][...] + sib[a][...]
        for h in (0, 1):
            @pl.when(c == h)
            def _():
                sends = []
                for a in range(n):
                    idx = _half_slices(parts[a].shape, h)
                    for j, chip in enumerate(chips):
                        rc = _remote(chip_sum[a].at[idx], slots[a].at[me_chip].at[idx], ici_send.at[3 * a + j], ici_recv.at[3 * a + j], (*chip, h))
                        rc.start()
                        sends.append(rc)
                    slots[a][(me_chip,) + idx] = chip_sum[a][idx]
                for a in range(n):
                    idx = _half_slices(parts[a].shape, h)
                    for j, chip in enumerate(chips):
                        landed = slots[a].at[2 * chip[0] + chip[1]].at[idx]
                        _remote(landed, landed, ici_send.at[3 * a + j], ici_recv.at[3 * a + j], (x, y, c)).wait_recv()
                    total = slots[a][(0,) + idx]
                    for s in range(1, 4):
                        total = total + slots[a][(s,) + idx]
                    out[a][idx] = total
                    rc = _remote(out[a].at[idx], out[a].at[idx], swap_send.at[a], swap_recv.at[a], (x, y, 1 - h))
                    rc.start()
                    sends.append(rc)
                for a in range(n):
                    other = out[a].at[_half_slices(parts[a].shape, 1 - h)]
                    _remote(other, other, swap_send.at[a], swap_recv.at[a], (x, y, c)).wait_recv()
                for rc in sends:
                    rc.wait_send()
        for rc in pairs:
            rc.wait_send()

    return pl.pallas_call(
        body, name="allreduce_small", in_specs=[_VMEM] * n, out_specs=[_VMEM] * n,
        out_shape=[jax.ShapeDtypeStruct(p.shape, f32) for p in parts],
        scratch_shapes=[pltpu.VMEM(p.shape, f32) for p in parts] * 2 + [pltpu.VMEM((4,) + p.shape, f32) for p in parts]
        + [pltpu.SemaphoreType.DMA((n,)), pltpu.SemaphoreType.DMA((n,)), pltpu.SemaphoreType.DMA((3 * n,)),
           pltpu.SemaphoreType.DMA((3 * n,)), pltpu.SemaphoreType.DMA((n,)), pltpu.SemaphoreType.DMA((n,))],
        compiler_params=pltpu.CompilerParams(vmem_limit_bytes=_VMEM_LIMIT_BYTES),
    )(*parts)


def _local_step(x, mem, tgt, g_mix, g_xattn, g_mem, g_ffn, g_final, cb, lg, lb, pw, ps, fb, relay, weights, reduce, n_seq, seq, n_mem):
    t, d = x.shape
    f = fb.shape[1] // 2
    c = cb.shape[1]
    h1 = _rms_fwd("norm_mix", x, g_mix)
    relay(0, h1)
    w_in, cw, fw = weights(0, h1)
    u = _mm_nn("proj_in", h1, w_in, _ACT, w_in.shape[1])
    y, hc = _mix_fwd(u, cw, cb, lg, lb, pw, ps, seq)
    relay(1, y)
    w_out, w_q, w_kv, w_o = weights(1, y)
    x1, h2 = _proj_residual_norm("proj_out", y, w_out, x, g_xattn)
    q = _mm_nn("proj_q", h2, w_q, _ACT, d)
    mem_n = _rms_fwd("norm_mem", mem, g_mem)
    kv = _mm_nn("proj_kv", mem_n, w_kv, _ACT, 2 * d)
    o = _attn_fwd(q, kv, n_seq, seq, n_mem)
    relay(2, o)
    x2, h3 = _proj_residual_norm("proj_o", o, w_o, x1, g_ffn)
    w_up, w_down = weights(2, h3)
    up = _mm_nn("proj_up", h3, w_up, _ACT, f, split_out=True)
    a = _ffn_gate_fwd(up, fw, fb, seq)
    dx3, dx3b, dg_final, loss = _proj_loss_bwd("proj_down", a, w_down, x2, g_final, tgt)
    da = _mm_nt("d_act", dx3b, w_down, _ACT)
    gw_down = _mm_tn_rows("dw_down", a, dx3b, f // 2, d // 2)
    dup, sums_g, sums_v = _ffn_gate_bwd(up, da, fw, fb, seq)
    gw_up = _mm_tn_pieces("dw_up", h3, dup, f // 2, t)
    token = reduce(0, [gw_down.reshape(8, -1, d), gw_up])
    dx2, dx2b, dg_ffn = _dproj_rms_bwd("d_h3", dup, w_up, x2, g_ffn + token, dx3)
    do = _mm_nt("d_o", dx2b, w_o, _ACT)
    gw_o = _mm_tn_rows("dw_o", o, dx2b, d, d // 2)
    dq, dk, dv = _attn_bwd(q, kv, do, n_seq, seq, n_mem)
    dkv = jnp.concatenate([dk, dv], axis=1)
    gw_q = _mm_tn_rows("dw_q", h2, dq, d, d // 2)
    gw_kv = _mm_tn_pieces("dw_kv", mem_n, dkv, d // 2, mem.shape[0])
    dmem_n = _mm_nt("d_mem_n", dkv, w_kv, f32)
    dg_mem = _rms_gain_grad("norm_mem_bwd", mem, dmem_n)
    dx1, dx1b, dg_xattn = _dproj_rms_bwd("d_h2", dq, w_q, x1, g_xattn, dx2)
    dy = _mm_nt("d_y", dx1b, w_out, _ACT)
    gw_out = _mm_tn_rows("dw_out", y, dx1b, d, d // 2)
    token = reduce(1, [gw_o.reshape(8, -1, d), gw_q.reshape(8, -1, d), gw_kv, gw_out.reshape(8, -1, d)])
    dhc, sums_norm = _mix_bwd_norm(hc, dy, lg + token, lb, seq)
    du, d_cw, d_ps, d_pw = _mix_bwd_taps(u, dhc, dy, cw, pw, ps, seq)
    gw_in = _mm_tn_pieces("dw_in", h1, du, c * 3 // 4, t)
    reduce(2, [gw_in])
    grad_x, dg_mix = _dproj_rms_bwd("d_h1", du, w_in, x, g_mix, dx1, storage_copy=False)
    zero_row = jnp.zeros((1, d), f32)
    gains = jnp.concatenate([dg_mix, dg_xattn, dg_mem, dg_ffn, dg_final, jnp.pad(loss, ((0, 0), (0, d - 1))), zero_row, zero_row], axis=0)
    conv_rows = jnp.concatenate([sums_norm[2:3], sums_norm[0:1], sums_norm[1:2], d_ps[0:1], jnp.zeros((4, c), f32)], axis=0)
    ffn_rows = jnp.concatenate([sums_g, sums_v], axis=1)
    small = [gains, conv_rows, d_pw.reshape(-1, d_pw.shape[-1]), ffn_rows, d_cw]
    return grad_x, small


def kernel(x, mem, norm_mix_g, w_in, conv_dw_w, conv_dw_b, conv_ln_g, conv_ln_b, pool_w, pool_scale, w_out, norm_xattn_g, norm_mem_g, w_q, w_kv, w_o, norm_ffn_g, w_up, ffn_dw_w, ffn_dw_b, w_down, norm_final_g, loss_target, m_norm_mix_g, m_w_in, m_conv_dw_w, m_conv_dw_b, m_conv_ln_g, m_conv_ln_b, m_pool_w, m_pool_scale, m_w_out, m_norm_xattn_g, m_norm_mem_g, m_w_q, m_w_kv, m_w_o, m_norm_ffn_g, m_w_up, m_ffn_dw_w, m_ffn_dw_b, m_w_down, m_norm_final_g, v_norm_mix_g, v_w_in, v_conv_dw_w, v_conv_dw_b, v_conv_ln_g, v_conv_ln_b, v_pool_w, v_pool_scale, v_w_out, v_norm_xattn_g, v_norm_mem_g, v_w_q, v_w_kv, v_w_o, v_norm_ffn_g, v_w_up, v_ffn_dw_w, v_ffn_dw_b, v_w_down, v_norm_final_g):
    n_seq, seq, d = x.shape
    n_mem = mem.shape[1]
    chip = 2 * lax.axis_index("x") + lax.axis_index("y")

    place = jnp.stack([chip, lax.axis_index("c")]).astype(jnp.int32)

    col_w = [w_in, w_kv, w_up]
    row_w = [w_out, w_q, w_o, w_down]
    col_flags = [True] * 3 + [False] * 4 + [True] * 2
    kw = conv_dw_w.shape[1]

    def padded_in_place(shard, rows):
        full = jnp.zeros((rows, 4 * shard.shape[1]), shard.dtype)
        return lax.dynamic_update_slice(full, shard, (0, chip * shard.shape[1]))

    bufs = list(_place_shards(place, [w[0] for w in col_w + row_w], col_flags[:7]))
    bufs += [padded_in_place(conv_dw_w[0], _HALO), padded_in_place(ffn_dw_w[0], 8)]
    groups = [[0, 7, 8], [3, 4, 1, 5], [2, 6]]
    whole = [False] * 7 + [True] * 2
    bufs, sems = _allgather_start(bufs, col_flags, whole, groups)
    relayed = {}

    def relay(g, after):
        members = groups[g]
        relayed[g] = _allgather_relay("allgather_relay_%d" % g, [bufs[i] for i in members], [col_flags[i] for i in members],
                                      [whole[i] for i in members], sems[g], after)

    def weights(g, after):
        members = groups[g]
        group_bufs, sibling_sems = relayed[g]
        return _allgather_wait("allgather_wait_%d" % g, group_bufs, [col_flags[i] for i in members],
                               [whole[i] for i in members], sibling_sems, after)

    names = ["w_in", "w_kv", "w_up", "w_out", "w_q", "w_o", "w_down"]
    reduce_groups = [["w_down", "w_up"], ["w_o", "w_q", "w_kv", "w_out"], ["w_in"]]
    in_flight = {}

    def reduce(g, grads):
        members = reduce_groups[g]
        got = _exchange_pair_halves("rs_pair_exchange_%d" % g, grads)
        sums = [_sum_pairs("rs_pair_sum_" + n, place, a, b) for n, a, b in zip(members, grads, got)]
        sums, lands, rs_sems, token = _chip_exchange_start("rs_chip_start_%d" % g, sums)
        in_flight[g] = (sums, lands, rs_sems)
        return token[0:1, 0:1]

    grad_x, small = _local_step(
        x.reshape(n_seq * seq, d), mem.reshape(n_seq * n_mem, d), loss_target.reshape(n_seq * seq, d),
        norm_mix_g, norm_xattn_g, norm_mem_g, norm_ffn_g, norm_final_g.reshape(1, d),
        conv_dw_b, conv_ln_g, conv_ln_b, pool_w[0], pool_scale, ffn_dw_b, relay, weights, reduce, n_seq, seq, n_mem)

    finals = {}
    for g, members in enumerate(reduce_groups):
        sums, lands, rs_sems = in_flight[g]
        sums, lands = _chip_exchange_wait("rs_chip_wait_%d" % g, sums, lands, rs_sems, grad_x)
        for n, a, b in zip(members, sums, lands):
            finals[n] = _sum_four("rs_chip_sum_" + n, place, a, b)
    shard_grads = _swap_halves([finals[n] for n in names])

    gains, conv_rows, d_pw, ffn_rows, d_cw = _allreduce_small(small)
    loss = gains[5, 0]

    outs = {}
    big_w = dict(zip(names, col_w + row_w))
    big_m = dict(w_in=m_w_in, w_kv=m_w_kv, w_up=m_w_up, w_out=m_w_out, w_q=m_w_q, w_o=m_w_o, w_down=m_w_down)
    big_v = dict(w_in=v_w_in, w_kv=v_w_kv, w_up=v_w_up, w_out=v_w_out, w_q=v_w_q, w_o=v_w_o, w_down=v_w_down)
    for n, g in zip(names, shard_grads):
        w = big_w[n]
        g2 = g.reshape(w.shape[1], w.shape[2])
        delta, new_m, new_v = _adamw_shard("adamw_" + n, w, g2, big_m[n], big_v[n])
        outs[n] = (g2.reshape(w.shape), delta, new_m, new_v)

    f2 = ffn_dw_b.shape[1]
    cs_c = conv_dw_w.shape[2]
    cs_f = ffn_dw_w.shape[2]
    g_cw = lax.dynamic_slice(d_cw, (0, chip * cs_c), (kw, cs_c)).reshape(conv_dw_w.shape)
    g_fw = lax.dynamic_slice(ffn_rows, (1, chip * cs_f), (ffn_dw_w.shape[1], cs_f)).reshape(ffn_dw_w.shape)
    small_params = [
        ("norm_mix_g", norm_mix_g, gains[0:1], m_norm_mix_g, v_norm_mix_g),
        ("conv_dw_w", conv_dw_w, g_cw, m_conv_dw_w, v_conv_dw_w),
        ("conv_dw_b", conv_dw_b, conv_rows[0:1], m_conv_dw_b, v_conv_dw_b),
        ("conv_ln_g", conv_ln_g, conv_rows[1:2], m_conv_ln_g, v_conv_ln_g),
        ("conv_ln_b", conv_ln_b, conv_rows[2:3], m_conv_ln_b, v_conv_ln_b),
        ("pool_w", pool_w, d_pw.reshape(pool_w.shape), m_pool_w, v_pool_w),
        ("pool_scale", pool_scale, conv_rows[3:4], m_pool_scale, v_pool_scale),
        ("norm_xattn_g", norm_xattn_g, gains[1:2], m_norm_xattn_g, v_norm_xattn_g),
        ("norm_mem_g", norm_mem_g, gains[2:3], m_norm_mem_g, v_norm_mem_g),
        ("norm_ffn_g", norm_ffn_g, gains[3:4], m_norm_ffn_g, v_norm_ffn_g),
        ("ffn_dw_w", ffn_dw_w, g_fw, m_ffn_dw_w, v_ffn_dw_w),
        ("ffn_dw_b", ffn_dw_b, ffn_rows[0:1, :f2], m_ffn_dw_b, v_ffn_dw_b),
        ("norm_final_g", norm_final_g.reshape(1, d), gains[4:5], m_norm_final_g.reshape(1, d), v_norm_final_g.reshape(1, d)),
    ]
    quads = []
    for _, w, g, m, v in small_params:
        shape2 = (-1, w.shape[-1])
        quads.append((w.reshape(shape2), g.reshape(shape2), m.reshape(shape2), v.reshape(shape2)))
    for (n, w, g, _, _), (delta, new_m, new_v) in zip(small_params, _adamw_small(quads)):
        shape = norm_final_g.shape if n == "norm_final_g" else w.shape
        outs[n] = (g.reshape(shape), delta.reshape(shape), new_m.reshape(shape), new_v.reshape(shape))

    order = ["norm_mix_g", "w_in", "conv_dw_w", "conv_dw_b", "conv_ln_g", "conv_ln_b", "pool_w", "pool_scale", "w_out",
             "norm_xattn_g", "norm_mem_g", "w_q", "w_kv", "w_o", "norm_ffn_g", "w_up", "ffn_dw_w", "ffn_dw_b", "w_down",
             "norm_final_g"]
    return (loss, grad_x.reshape(x.shape), *[outs[n][0] for n in order], *[outs[n][1] for n in order],
            *[outs[n][2] for n in order], *[outs[n][3] for n in order])
```

```python
import functools

import jax
import jax.numpy as jnp
from jax import lax
from jax.experimental import pallas as pl
from jax.experimental.pallas import tpu as pltpu

f32 = jnp.float32
_ACT = jnp.bfloat16

EPS = 1e-6
POOL_WINDOWS = (2, 4, 8, 16)
XATTN_HEADS = 4
ADAM_LR = 0.001
ADAM_B1 = 0.9
ADAM_B2 = 0.999
ADAM_EPS = 1e-08
ADAM_WD = 0.01
ADAM_STEP = 10

_VMEM_LIMIT_BYTES = 56 * 1024 * 1024
_MESH = pl.DeviceIdType.MESH
_ANY = pl.BlockSpec(memory_space=pl.ANY)
_VMEM = pl.BlockSpec(memory_space=pltpu.VMEM)
_HBM = pl.BlockSpec(memory_space=pltpu.HBM)
_SEM = pl.BlockSpec(memory_space=pltpu.SEMAPHORE)
_EFFECT = pltpu.SideEffectType.DATAFLOW_SIDE_EFFECTING

_NN = (((1,), (0,)), ((), ()))
_NT = (((1,), (1,)), ((), ()))
_TN = (((0,), (0,)), ((), ()))


def _params(n_grid):
    return pltpu.CompilerParams(dimension_semantics=("arbitrary",) * n_grid, vmem_limit_bytes=_VMEM_LIMIT_BYTES)


def _sigmoid(v):
    return 1.0 / (1.0 + jnp.exp(-v))


def _dot(a, b, dims):
    return lax.dot_general(a, b, dims, preferred_element_type=f32)


def _mm(name, a, b, *, dims, grid, a_spec, b_spec, o_spec, out_shape, nk, acc_shape=None, res=None, res_spec=None):
    def body(*refs):
        if res is None:
            a_ref, b_ref, o_ref, *scratch = refs
            r_ref = None
        else:
            a_ref, b_ref, r_ref, o_ref, *scratch = refs
        p = _dot(a_ref[...], b_ref[...], dims)

        def finish(v):
            if r_ref is not None:
                v = v + r_ref[...]
            o_ref[...] = v.astype(o_ref.dtype)

        if nk == 1:
            finish(p)
        else:
            acc = scratch[0]
            k = pl.program_id(2)

            @pl.when(k == 0)
            def _():
                acc[...] = p

            @pl.when(k > 0)
            def _():
                acc[...] += p

            @pl.when(k == nk - 1)
            def _():
                finish(acc[...])

    ins = [a, b] + ([] if res is None else [res])
    specs = [a_spec, b_spec] + ([] if res is None else [res_spec])
    return pl.pallas_call(
        body, name=name, grid=grid, in_specs=specs, out_specs=o_spec, out_shape=out_shape,
        scratch_shapes=[pltpu.VMEM(acc_shape, f32)] if nk > 1 else [], compiler_params=_params(3),
    )(*ins)


def _row_tile(m):
    return min(512, m)


def _mm_nn(name, a, b, out_dtype, tn, res=None, split_out=False):
    m, k = a.shape
    n = b.shape[1]
    tm = _row_tile(m)
    if split_out:
        out_shape = jax.ShapeDtypeStruct((n // tn, m, tn), out_dtype)
        o_spec = pl.BlockSpec((None, tm, tn), lambda j, i, kk: (j, i, 0))
    else:
        out_shape = jax.ShapeDtypeStruct((m, n), out_dtype)
        o_spec = pl.BlockSpec((tm, tn), lambda j, i, kk: (i, j))
    return _mm(
        name, a, b, dims=_NN, grid=(n // tn, m // tm, 1), nk=1,
        a_spec=pl.BlockSpec((tm, k), lambda j, i, kk: (i, 0)),
        b_spec=pl.BlockSpec((k, tn), lambda j, i, kk: (0, j)),
        o_spec=o_spec, out_shape=out_shape, res=res,
        res_spec=pl.BlockSpec((tm, tn), lambda j, i, kk: (i, j)),
    )


def _mm_nt(name, a, b, out_dtype):
    n, kc = b.shape
    m = a.shape[0]
    tm = _row_tile(m)
    return _mm(
        name, a, b, dims=_NT, grid=(m // tm, 1, 1), nk=1,
        a_spec=pl.BlockSpec((tm, kc), lambda i, j, k: (i, 0)), b_spec=pl.BlockSpec((n, kc), lambda i, j, k: (0, 0)),
        o_spec=pl.BlockSpec((tm, n), lambda i, j, k: (i, 0)),
        out_shape=jax.ShapeDtypeStruct((m, n), out_dtype),
    )


def _mm_tn_rows(name, a, b, tka, tn):
    m, ka = a.shape
    nb = b.shape[1]
    return _mm(
        name, a, b, dims=_TN, grid=(ka // tka, nb // tn, 1), nk=1,
        a_spec=pl.BlockSpec((m, tka), lambda i, j, k: (0, i)),
        b_spec=pl.BlockSpec((m, tn), lambda i, j, k: (0, j)),
        o_spec=pl.BlockSpec((tka, tn), lambda i, j, k: (i, j)),
        out_shape=jax.ShapeDtypeStruct((ka, nb), _ACT),
    )


def _mm_tn_pieces(name, a, b, cs, tt):
    m, ka = a.shape
    nk = m // tt
    if b.ndim == 3:
        b_spec = pl.BlockSpec((None, tt, cs), lambda i, j, k: (j // 2, k, j % 2))
    else:
        b_spec = pl.BlockSpec((tt, cs), lambda i, j, k: (k, j))
    return _mm(
        name, a, b, dims=_TN, grid=(2, 4, nk), nk=nk, acc_shape=(ka // 2, cs),
        a_spec=pl.BlockSpec((tt, ka // 2), lambda i, j, k: (k, i)), b_spec=b_spec,
        o_spec=pl.BlockSpec((None, ka // 2, cs), lambda i, j, k: (2 * j + i, 0, 0)),
        out_shape=jax.ShapeDtypeStruct((8, ka // 2, cs), _ACT),
    )


def _rms_fwd(name, x, g):
    t, d = x.shape
    tm = _row_tile(t)

    def body(x_ref, g_ref, h_ref):
        xv = x_ref[...]
        r = lax.rsqrt(jnp.mean(xv * xv, axis=-1, keepdims=True) + EPS)
        h_ref[...] = (xv * r * g_ref[...]).astype(h_ref.dtype)

    return pl.pallas_call(
        body, name=name, grid=(t // tm,),
        in_specs=[pl.BlockSpec((tm, d), lambda i: (i, 0)), pl.BlockSpec((1, d), lambda i: (0, 0))],
        out_specs=pl.BlockSpec((tm, d), lambda i: (i, 0)), out_shape=jax.ShapeDtypeStruct((t, d), _ACT),
        compiler_params=_params(1),
    )(x, g)


def _fused_rows(name, a, b, product, a_spec, tm, extras, extra_specs, out_shape, out_specs, epilogue):
    ne = len(extras)

    def body(a_ref, b_ref, *refs):
        epilogue(product(a_ref, b_ref), refs[:ne], refs[ne:])

    m = extras[0].shape[0]
    return pl.pallas_call(
        body, name=name, grid=(m // tm,),
        in_specs=[a_spec, pl.BlockSpec(b.shape, lambda i: (0, 0)), *extra_specs], out_specs=out_specs, out_shape=out_shape,
        compiler_params=_params(1),
    )(a, b, *extras)


def _proj_residual_norm(name, a, b, res, g):
    m, k = a.shape
    d = b.shape[1]
    tm = _row_tile(m)

    def epilogue(p, ins, outs):
        xv = p + ins[0][...]
        outs[0][...] = xv
        r = lax.rsqrt(jnp.mean(xv * xv, axis=-1, keepdims=True) + EPS)
        outs[1][...] = (xv * r * ins[1][...]).astype(outs[1].dtype)

    row = pl.BlockSpec((tm, d), lambda i: (i, 0))
    return _fused_rows(
        name, a, b, lambda a_ref, b_ref: _dot(a_ref[...], b_ref[...], _NN), pl.BlockSpec((tm, k), lambda i: (i, 0)), tm,
        [res, g], [row, pl.BlockSpec((1, d), lambda i: (0, 0))],
        [jax.ShapeDtypeStruct((m, d), f32), jax.ShapeDtypeStruct((m, d), _ACT)], [row, row], epilogue)


def _dproj_rms_bwd(name, a, b, x, g, dres, storage_copy=True):
    m, d = x.shape
    if a.ndim == 3:
        nh, _, kh = a.shape
        tm = min(256, m)
        a_spec = pl.BlockSpec((nh, tm, kh), lambda i: (0, i, 0))

        def product(a_ref, b_ref):
            p = _dot(a_ref[0], b_ref[:, 0:kh], _NT)
            for h in range(1, nh):
                p = p + _dot(a_ref[h], b_ref[:, h * kh:(h + 1) * kh], _NT)
            return p
    else:
        tm = _row_tile(m)
        a_spec = pl.BlockSpec((tm, a.shape[1]), lambda i: (i, 0))

        def product(a_ref, b_ref):
            return _dot(a_ref[...], b_ref[...], _NT)

    def epilogue(dhv, ins, outs):
        x_ref, g_ref, dres_ref = ins
        dg_ref = outs[-1]

        @pl.when(pl.program_id(0) == 0)
        def _():
            dg_ref[...] = jnp.zeros_like(dg_ref)

        xv = x_ref[...]
        r = lax.rsqrt(jnp.mean(xv * xv, axis=-1, keepdims=True) + EPS)
        xn = xv * r
        dxn = dhv * g_ref[...]
        dx = r * (dxn - xn * jnp.mean(dxn * xn, axis=-1, keepdims=True)) + dres_ref[...]
        outs[0][...] = dx
        if storage_copy:
            outs[1][...] = dx.astype(outs[1].dtype)
        dg_ref[...] += jnp.sum(dhv * xn, axis=0, keepdims=True)

    row = pl.BlockSpec((tm, d), lambda i: (i, 0))
    vec = pl.BlockSpec((1, d), lambda i: (0, 0))
    copies = [jax.ShapeDtypeStruct((m, d), _ACT)] if storage_copy else []
    return _fused_rows(
        name, a, b, product, a_spec, tm, [x, g, dres], [row, vec, row],
        [jax.ShapeDtypeStruct((m, d), f32)] + copies + [jax.ShapeDtypeStruct((1, d), f32)],
        [row] * (1 + len(copies)) + [vec], epilogue)


def _proj_loss_bwd(name, a, b, res, g, tgt):
    m, k = a.shape
    d = b.shape[1]
    tm = _row_tile(m)

    def epilogue(p, ins, outs):
        res_ref, g_ref, t_ref = ins
        dx_ref, dxb_ref, dg_ref, loss_ref = outs

        @pl.when(pl.program_id(0) == 0)
        def _():
            dg_ref[...] = jnp.zeros_like(dg_ref)
            loss_ref[...] = jnp.zeros_like(loss_ref)

        xv = p + res_ref[...]
        gv = g_ref[...]
        r = lax.rsqrt(jnp.mean(xv * xv, axis=-1, keepdims=True) + EPS)
        xn = xv * r
        err = xn * gv - t_ref[...]
        loss_ref[...] += 0.5 * jnp.sum(jnp.mean(err * err, axis=-1, keepdims=True), axis=0, keepdims=True)
        dout = err * (1.0 / d)
        dxn = dout * gv
        dx = r * (dxn - xn * jnp.mean(dxn * xn, axis=-1, keepdims=True))
        dx_ref[...] = dx
        dxb_ref[...] = dx.astype(dxb_ref.dtype)
        dg_ref[...] += jnp.sum(dout * xn, axis=0, keepdims=True)

    row = pl.BlockSpec((tm, d), lambda i: (i, 0))
    vec = pl.BlockSpec((1, d), lambda i: (0, 0))
    return _fused_rows(
        name, a, b, lambda a_ref, b_ref: _dot(a_ref[...], b_ref[...], _NN), pl.BlockSpec((tm, k), lambda i: (i, 0)), tm,
        [res, g, tgt], [row, vec, row],
        [jax.ShapeDtypeStruct((m, d), f32), jax.ShapeDtypeStruct((m, d), _ACT), jax.ShapeDtypeStruct((1, d), f32),
         jax.ShapeDtypeStruct((1, 1), f32)],
        [row, row, vec, pl.BlockSpec((1, 1), lambda i: (0, 0))], epilogue)


def _rms_gain_grad(name, x, dh):
    t, d = x.shape
    tm = _row_tile(t)

    def body(x_ref, dh_ref, dg_ref):
        @pl.when(pl.program_id(0) == 0)
        def _():
            dg_ref[...] = jnp.zeros_like(dg_ref)

        xv = x_ref[...]
        r = lax.rsqrt(jnp.mean(xv * xv, axis=-1, keepdims=True) + EPS)
        dg_ref[...] += jnp.sum(dh_ref[...] * (xv * r), axis=0, keepdims=True)

    row = pl.BlockSpec((tm, d), lambda i: (i, 0))
    return pl.pallas_call(
        body, name=name, grid=(t // tm,), in_specs=[row, row], out_specs=pl.BlockSpec((1, d), lambda i: (0, 0)),
        out_shape=jax.ShapeDtypeStruct((1, d), f32), compiler_params=_params(1),
    )(x, dh)


_CONV_ROWS = 256
_CHUNK = 64
_HALO = 32


def _pool_counts(pos, w):
    return jnp.minimum(pos + 1.0, float(w))


def _tap_rows(buf, starts, rows):
    for residue in range(8):
        group = [(k, s) for k, s in starts.items() if s % 8 == residue]
        if group:
            lo = min(s for _, s in group)
            hi = max(s for _, s in group)
            win = buf[lo:hi + rows, :]
            for k, s in group:
                yield k, win[s - lo:s - lo + rows, :]


def _mix_fwd(u, cw, cb, lg, lb, pw, ps, seq):
    t, c3 = u.shape
    c = c3 // 3
    kw = 31
    tm = min(_CONV_ROWS, seq)
    tps = seq // tm
    gd = c // len(POOL_WINDOWS)

    def body(u_ref, uh_ref, cw_ref, cb_ref, lg_ref, lb_ref, pw_ref, ps_ref, y_ref, hc_ref, hgbuf, pbuf):
        i = pl.program_id(0)
        keep = jnp.where(i % tps == 0, 0.0, 1.0)
        um = u_ref[...].astype(f32)
        uh = uh_ref[...].astype(f32) * keep
        hgbuf[0:_HALO, :] = uh[:, 0:c] * _sigmoid(uh[:, c:2 * c])
        hgbuf[_HALO:_HALO + tm, :] = um[:, 0:c] * _sigmoid(um[:, c:2 * c])
        pbuf[0:_HALO, :] = uh[:, 2 * c:]
        pbuf[_HALO:_HALO + tm, :] = um[:, 2 * c:]
        for r0 in range(0, tm, _CHUNK):
            acc = jnp.broadcast_to(cb_ref[...], (_CHUNK, c))
            for k, rows in _tap_rows(hgbuf, {k: r0 + _HALO - (kw - 1) + k for k in range(kw)}, _CHUNK):
                acc = acc + cw_ref[k:k + 1, :] * rows
            hc_ref[r0:r0 + _CHUNK, :] = acc
            mu = jnp.mean(acc, axis=-1, keepdims=True)
            xc = acc - mu
            var = jnp.mean(xc * xc, axis=-1, keepdims=True)
            hl = xc * lax.rsqrt(var + EPS) * lg_ref[...] + lb_ref[...]
            y_ref[r0:r0 + _CHUNK, 0:c] = (hl * _sigmoid(hl)).astype(y_ref.dtype)
        pos = ((i % tps) * tm).astype(f32) + lax.broadcasted_iota(jnp.int32, (tm, 1), 0).astype(f32)
        for gi, w in enumerate(POOL_WINDOWS):
            sl = slice(gi * gd, (gi + 1) * gd)
            v = pbuf[_HALO:_HALO + tm, sl]
            s = v
            for j in range(1, w):
                s = s + pbuf[_HALO - j:_HALO - j + tm, sl]
            pooled = s / _pool_counts(pos, w) - v
            mixed = _dot(pooled.astype(_ACT), pw_ref[gi].astype(_ACT), _NN)
            y_ref[:, c + gi * gd:c + (gi + 1) * gd] = (mixed * ps_ref[:, sl]).astype(y_ref.dtype)

    hb = tm // _HALO
    full = lambda shape: pl.BlockSpec(shape, lambda i: (0,) * len(shape))
    return pl.pallas_call(
        body, name="mix_fwd", grid=(t // tm,),
        in_specs=[pl.BlockSpec((tm, c3), lambda i: (i, 0)),
                  pl.BlockSpec((_HALO, c3), lambda i: (jnp.maximum(i * hb - 1, 0), 0)),
                  full((_HALO, c)), full((1, c)), full((1, c)), full((1, c)), full((len(POOL_WINDOWS), gd, gd)), full((1, c))],
        out_specs=[pl.BlockSpec((tm, 2 * c), lambda i: (i, 0)), pl.BlockSpec((tm, c), lambda i: (i, 0))],
        out_shape=[jax.ShapeDtypeStruct((t, 2 * c), _ACT), jax.ShapeDtypeStruct((t, c), f32)],
        scratch_shapes=[pltpu.VMEM((_HALO + tm, c), f32), pltpu.VMEM((_HALO + tm, c), f32)],
        compiler_params=_params(1),
    )(u, u, cw, cb, lg, lb, pw, ps)


def _mix_bwd_norm(hc, dy, lg, lb, seq):
    t, c = hc.shape
    tm = min(_CONV_ROWS, seq)

    def body(hc_ref, dy_ref, lg_ref, lb_ref, dhc_ref, sums_ref):
        @pl.when(pl.program_id(0) == 0)
        def _():
            sums_ref[...] = jnp.zeros_like(sums_ref)

        hcv = hc_ref[...]
        mu = jnp.mean(hcv, axis=-1, keepdims=True)
        xc = hcv - mu
        rstd = lax.rsqrt(jnp.mean(xc * xc, axis=-1, keepdims=True) + EPS)
        n = xc * rstd
        hl = n * lg_ref[...] + lb_ref[...]
        sg = _sigmoid(hl)
        dhl = dy_ref[...].astype(f32) * (sg * (1.0 + hl * (1.0 - sg)))
        dn = dhl * lg_ref[...]
        dhc = rstd * (dn - jnp.mean(dn, axis=-1, keepdims=True) - n * jnp.mean(dn * n, axis=-1, keepdims=True))
        dhc_ref[...] = dhc
        sums_ref[0:1, :] += jnp.sum(dhl * n, axis=0, keepdims=True)
        sums_ref[1:2, :] += jnp.sum(dhl, axis=0, keepdims=True)
        sums_ref[2:3, :] += jnp.sum(dhc, axis=0, keepdims=True)

    row = pl.BlockSpec((tm, c), lambda i: (i, 0))
    vec = pl.BlockSpec((1, c), lambda i: (0, 0))
    return pl.pallas_call(
        body, name="mix_bwd_norm", grid=(t // tm,), in_specs=[row, row, vec, vec],
        out_specs=[row, pl.BlockSpec((8, c), lambda i: (0, 0))],
        out_shape=[jax.ShapeDtypeStruct((t, c), f32), jax.ShapeDtypeStruct((8, c), f32)],
        compiler_params=_params(1),
    )(hc, dy, lg, lb)


def _mix_bwd_taps(u, dhc, dy, cw, pw, ps, seq):
    t, c3 = u.shape
    c = c3 // 3
    kw = 31
    tm = min(_CONV_ROWS, seq)
    tps = seq // tm
    ng = len(POOL_WINDOWS)
    gd = c // ng
    nh = 16

    def body(u_ref, uh_ref, dhc_ref, dhcn_ref, dy_ref, dyn_ref, cw_ref, pw_ref, ps_ref,
             du_ref, dcw_ref, dps_ref, dpw_ref, hgbuf, dcbuf, pbuf, dpbuf):
        i = pl.program_id(0)
        keep_prev = jnp.where(i % tps == 0, 0.0, 1.0)
        keep_next = jnp.where(i % tps == tps - 1, 0.0, 1.0)

        @pl.when(i == 0)
        def _():
            dcw_ref[...] = jnp.zeros_like(dcw_ref)
            dps_ref[...] = jnp.zeros_like(dps_ref)
            dpw_ref[...] = jnp.zeros_like(dpw_ref)

        uh = uh_ref[...].astype(f32) * keep_prev
        hgbuf[0:_HALO, :] = uh[:, 0:c] * _sigmoid(uh[:, c:2 * c])
        pbuf[0:_HALO, :] = uh[:, 2 * c:]
        um = u_ref[...].astype(f32)
        hgbuf[_HALO:_HALO + tm, :] = um[:, 0:c] * _sigmoid(um[:, c:2 * c])
        pbuf[_HALO:_HALO + tm, :] = um[:, 2 * c:]
        dcbuf[0:tm, :] = dhc_ref[...]
        dcbuf[tm:tm + _HALO, :] = dhcn_ref[...] * keep_next
        tap_sums = [None] * kw
        for r0 in range(0, tm, _CHUNK):
            dh = dcbuf[r0:r0 + _CHUNK, :]
            acc = jnp.zeros((_CHUNK, c), f32)
            for k, rows in _tap_rows(hgbuf, {k: r0 + _HALO - (kw - 1) + k for k in range(kw)}, _CHUNK):
                part = jnp.sum(dh * rows, axis=0, keepdims=True)
                tap_sums[k] = part if tap_sums[k] is None else tap_sums[k] + part
            for k, rows in _tap_rows(dcbuf, {k: r0 + (kw - 1) - k for k in range(kw)}, _CHUNK):
                acc = acc + cw_ref[k:k + 1, :] * rows
            val = u_ref[r0:r0 + _CHUNK, 0:c].astype(f32)
            sg = _sigmoid(u_ref[r0:r0 + _CHUNK, c:2 * c].astype(f32))
            du_ref[r0:r0 + _CHUNK, 0:c] = (acc * sg).astype(du_ref.dtype)
            du_ref[r0:r0 + _CHUNK, c:2 * c] = (acc * val * sg * (1.0 - sg)).astype(du_ref.dtype)
        for k in range(kw):
            dcw_ref[k:k + 1, :] += tap_sums[k]
        base = ((i % tps) * tm).astype(f32)
        pos = base + lax.broadcasted_iota(jnp.int32, (tm, 1), 0).astype(f32)
        pos_next = base + float(tm) + lax.broadcasted_iota(jnp.int32, (nh, 1), 0).astype(f32)
        for gi, w in enumerate(POOL_WINDOWS):
            sl = slice(gi * gd, (gi + 1) * gd)
            v = pbuf[_HALO:_HALO + tm, sl]
            s = v
            for j in range(1, w):
                s = s + pbuf[_HALO - j:_HALO - j + tm, sl]
            cnt = _pool_counts(pos, w)
            pooled = (s / cnt - v).astype(_ACT)
            pwg = pw_ref[gi].astype(_ACT)
            mixed = _dot(pooled, pwg, _NN)
            dyp = dy_ref[:, sl].astype(f32)
            dps_ref[0:1, sl] += jnp.sum(dyp * mixed, axis=0, keepdims=True)
            dmix = (dyp * ps_ref[:, sl]).astype(_ACT)
            dpw_ref[gi] += _dot(pooled, dmix, _TN)
            dmix_next = (dyn_ref[:, sl].astype(f32) * ps_ref[:, sl] * keep_next).astype(_ACT)
            dpool = _dot(dmix, pwg, _NT)
            dpbuf[0:tm, sl] = dpool / cnt
            dpbuf[tm:tm + nh, sl] = _dot(dmix_next, pwg, _NT) / _pool_counts(pos_next, w)
            acc = -dpool
            for j in range(w):
                acc = acc + dpbuf[j:j + tm, sl]
            du_ref[:, 2 * c + gi * gd:2 * c + (gi + 1) * gd] = acc.astype(du_ref.dtype)

    hb = tm // _HALO
    n_halo = t // _HALO
    n_nh = t // nh
    full = lambda shape: pl.BlockSpec(shape, lambda i: (0,) * len(shape))
    return pl.pallas_call(
        body, name="mix_bwd_taps", grid=(t // tm,),
        in_specs=[pl.BlockSpec((tm, c3), lambda i: (i, 0)),
                  pl.BlockSpec((_HALO, c3), lambda i: (jnp.maximum(i * hb - 1, 0), 0)),
                  pl.BlockSpec((tm, c), lambda i: (i, 0)),
                  pl.BlockSpec((_HALO, c), lambda i: (jnp.minimum((i + 1) * hb, n_halo - 1), 0)),
                  pl.BlockSpec((tm, c), lambda i: (i, 1)),
                  pl.BlockSpec((nh, c), lambda i: (jnp.minimum((i + 1) * (tm // nh), n_nh - 1), 1)),
                  full((_HALO, c)), full((ng, gd, gd)), full((1, c))],
        out_specs=[pl.BlockSpec((tm, c3), lambda i: (i, 0)), full((_HALO, c)), full((8, c)), full((ng, gd, gd))],
        out_shape=[jax.ShapeDtypeStruct((t, c3), _ACT), jax.ShapeDtypeStruct((_HALO, c), f32),
                   jax.ShapeDtypeStruct((8, c), f32), jax.ShapeDtypeStruct((ng, gd, gd), f32)],
        scratch_shapes=[pltpu.VMEM((_HALO + tm, c), f32), pltpu.VMEM((tm + _HALO, c), f32),
                        pltpu.VMEM((_HALO + tm, c), f32), pltpu.VMEM((tm + nh, c), f32)],
        compiler_params=_params(1),
    )(u, u, dhc, dhc, dy, dy, cw, pw, ps)


def _attn_fwd(q, kv, n_seq, seq, n_mem):
    t, d = q.shape
    dh = d // XATTN_HEADS
    tq = min(512, seq)
    nq = seq // tq
    scale = dh ** -0.5

    def body(q_ref, kv_ref, o_ref):
        for h in range(XATTN_HEADS):
            cols = slice(h * dh, (h + 1) * dh)
            s = _dot(q_ref[:, cols], kv_ref[:, cols], _NT) * scale
            e = jnp.exp(s - jnp.max(s, axis=-1, keepdims=True))
            p = e / jnp.sum(e, axis=-1, keepdims=True)
            o_ref[:, cols] = _dot(p.astype(_ACT), kv_ref[:, d + h * dh:d + (h + 1) * dh], _NN).astype(o_ref.dtype)

    qs = pl.BlockSpec((tq, d), lambda b, i: (b * nq + i, 0))
    return pl.pallas_call(
        body, name="attn_fwd", grid=(n_seq, nq), in_specs=[qs, pl.BlockSpec((n_mem, 2 * d), lambda b, i: (b, 0))],
        out_specs=qs, out_shape=jax.ShapeDtypeStruct((t, d), _ACT), compiler_params=_params(2),
    )(q, kv)


def _attn_bwd(q, kv, do, n_seq, seq, n_mem):
    t, d = q.shape
    dh = d // XATTN_HEADS
    tq = min(512, seq)
    nq = seq // tq
    scale = dh ** -0.5

    def body(q_ref, kv_ref, do_ref, dq_ref, dkv_ref, acc):
        i = pl.program_id(1)

        @pl.when(i == 0)
        def _():
            acc[...] = jnp.zeros_like(acc)

        for h in range(XATTN_HEADS):
            cols = slice(h * dh, (h + 1) * dh)
            vcols = slice(d + h * dh, d + (h + 1) * dh)
            qv = q_ref[:, cols]
            kh = kv_ref[:, cols]
            dov = do_ref[:, cols]
            s = _dot(qv, kh, _NT) * scale
            e = jnp.exp(s - jnp.max(s, axis=-1, keepdims=True))
            p = e / jnp.sum(e, axis=-1, keepdims=True)
            dp = _dot(dov, kv_ref[:, vcols], _NT)
            ds = (p * (dp - jnp.sum(dp * p, axis=-1, keepdims=True)) * scale).astype(_ACT)
            dq_ref[:, cols] = _dot(ds, kh, _NN).astype(dq_ref.dtype)
            acc[:, cols] += _dot(ds, qv, _TN)
            acc[:, vcols] += _dot(p.astype(_ACT), dov, _TN)

        @pl.when(i == nq - 1)
        def _():
            dkv_ref[...] = acc[...].astype(dkv_ref.dtype)

    qs = pl.BlockSpec((tq, d), lambda b, i: (b * nq + i, 0))
    ms = pl.BlockSpec((n_mem, 2 * d), lambda b, i: (b, 0))
    return pl.pallas_call(
        body, name="attn_bwd", grid=(n_seq, nq), in_specs=[qs, ms, qs], out_specs=[qs, ms],
        out_shape=[jax.ShapeDtypeStruct((t, d), _ACT), jax.ShapeDtypeStruct((n_seq * n_mem, 2 * d), _ACT)],
        scratch_shapes=[pltpu.VMEM((n_mem, 2 * d), f32)], compiler_params=_params(2),
    )(q, kv, do)


_FFN_ROWS = 1024
_FFN_COLS = 256
_FFN_HALO = 16


def _window(buf, g, start, rows):
    return buf[g, pl.ds(start, rows + 8), :]


def _conv3(b_ref, w_ref, win, rows):
    acc = jnp.broadcast_to(b_ref[...], (rows, win.shape[1]))
    for k in range(3):
        acc = acc + w_ref[k:k + 1, :] * win[6 + k:6 + k + rows, :]
    return acc


def _ffn_gate_fwd(up, fw, fb, seq):
    _, t, f = up.shape
    tm = min(_FFN_ROWS, seq)
    tps = seq // tm
    tc = _FFN_COLS
    nc = f // tc
    hl = _FFN_HALO

    def body(up_ref, uph_ref, wg_ref, wv_ref, bg_ref, bv_ref, a_ref, buf):
        i = pl.program_id(1)
        keep = jnp.where(i % tps == 0, 0.0, 1.0)
        buf[:, 0:hl, :] = uph_ref[...].astype(f32) * keep
        buf[:, hl:hl + tm, :] = up_ref[...].astype(f32)

        def chunk(ci, carry):
            r0 = pl.multiple_of(ci * _CHUNK, _CHUNK)
            conv = []
            for g, (w_ref, b_ref) in enumerate(((wg_ref, bg_ref), (wv_ref, bv_ref))):
                conv.append(_conv3(b_ref, w_ref, _window(buf, g, r0 + hl - 8, _CHUNK), _CHUNK))
            gate, val = conv
            a_ref[pl.ds(r0, _CHUNK), :] = (gate * _sigmoid(gate) * val).astype(a_ref.dtype)
            return carry

        lax.fori_loop(0, tm // _CHUNK, chunk, 0)

    hb = tm // hl
    return pl.pallas_call(
        body, name="ffn_gate_fwd", grid=(nc, t // tm),
        in_specs=[pl.BlockSpec((2, tm, tc), lambda j, i: (0, i, j)),
                  pl.BlockSpec((2, hl, tc), lambda j, i: (0, jnp.maximum(i * hb - 1, 0), j)),
                  pl.BlockSpec((8, tc), lambda j, i: (0, j)), pl.BlockSpec((8, tc), lambda j, i: (0, nc + j)),
                  pl.BlockSpec((1, tc), lambda j, i: (0, j)), pl.BlockSpec((1, tc), lambda j, i: (0, nc + j))],
        out_specs=pl.BlockSpec((tm, tc), lambda j, i: (i, j)),
        out_shape=jax.ShapeDtypeStruct((t, f), _ACT),
        scratch_shapes=[pltpu.VMEM((2, hl + tm, tc), f32)], compiler_params=_params(2),
    )(up, up, fw, fw, fb, fb)


def _ffn_gate_bwd(up, da, fw, fb, seq):
    _, t, f = up.shape
    tm = min(_FFN_ROWS, seq)
    tps = seq // tm
    tc = _FFN_COLS
    nc = f // tc
    hl = _FFN_HALO

    def body(up_ref, uph_ref, upn_ref, da_ref, dan_ref, wg_ref, wv_ref, bg_ref, bv_ref,
             dup_ref, sg_ref, sv_ref, ubuf, dbuf, sums):
        i = pl.program_id(1)
        keep_prev = jnp.where(i % tps == 0, 0.0, 1.0)
        keep_next = jnp.where(i % tps == tps - 1, 0.0, 1.0)

        @pl.when(i == 0)
        def _():
            sg_ref[...] = jnp.zeros_like(sg_ref)
            sv_ref[...] = jnp.zeros_like(sv_ref)

        sums[...] = jnp.zeros_like(sums)
        ubuf[:, 0:hl, :] = uph_ref[...].astype(f32) * keep_prev
        ubuf[:, hl:hl + tm, :] = up_ref[...].astype(f32)
        ubuf[:, hl + tm:hl + tm + hl, :] = upn_ref[...].astype(f32) * keep_next
        w_refs = (wg_ref, wv_ref)
        b_refs = (bg_ref, bv_ref)

        def grads(r0, rows, dav, count):
            wins = [_window(ubuf, g, r0 + hl - 8, rows) for g in range(2)]
            gate, val = [_conv3(b_refs[g], w_refs[g], wins[g], rows) for g in range(2)]
            sg = _sigmoid(gate)
            douts = (dav * val * (sg * (1.0 + gate * (1.0 - sg))), dav * (gate * sg))
            for g in range(2):
                dbuf[g, pl.ds(r0, rows), :] = douts[g]
                if count:
                    sums[g, 0] += douts[g].reshape(rows // 8, 8, tc).sum(axis=0)
                    for k in range(3):
                        sums[g, 1 + k] += (douts[g] * wins[g][6 + k:6 + k + rows, :]).reshape(rows // 8, 8, tc).sum(axis=0)

        def first(ci, carry):
            r0 = pl.multiple_of(ci * _CHUNK, _CHUNK)
            grads(r0, _CHUNK, da_ref[pl.ds(r0, _CHUNK), :].astype(f32), True)
            return carry

        lax.fori_loop(0, tm // _CHUNK, first, 0)
        grads(tm, hl, dan_ref[...].astype(f32) * keep_next, False)

        def second(ci, carry):
            r0 = pl.multiple_of(ci * _CHUNK, _CHUNK)
            for g in range(2):
                win = _window(dbuf, g, r0, _CHUNK)
                acc = jnp.zeros((_CHUNK, tc), f32)
                for k in range(3):
                    acc = acc + w_refs[g][k:k + 1, :] * win[2 - k:2 - k + _CHUNK, :]
                dup_ref[g, pl.ds(r0, _CHUNK), :] = acc.astype(dup_ref.dtype)
            return carry

        lax.fori_loop(0, tm // _CHUNK, second, 0)
        for g, s_ref in enumerate((sg_ref, sv_ref)):
            for r in range(4):
                s_ref[r:r + 1, :] += jnp.sum(sums[g, r], axis=0, keepdims=True)

    hb = tm // hl
    n_halo = t // hl
    return pl.pallas_call(
        body, name="ffn_gate_bwd", grid=(nc, t // tm),
        in_specs=[pl.BlockSpec((2, tm, tc), lambda j, i: (0, i, j)),
                  pl.BlockSpec((2, hl, tc), lambda j, i: (0, jnp.maximum(i * hb - 1, 0), j)),
                  pl.BlockSpec((2, hl, tc), lambda j, i: (0, jnp.minimum((i + 1) * hb, n_halo - 1), j)),
                  pl.BlockSpec((tm, tc), lambda j, i: (i, j)),
                  pl.BlockSpec((hl, tc), lambda j, i: (jnp.minimum((i + 1) * hb, n_halo - 1), j)),
                  pl.BlockSpec((8, tc), lambda j, i: (0, j)), pl.BlockSpec((8, tc), lambda j, i: (0, nc + j)),
                  pl.BlockSpec((1, tc), lambda j, i: (0, j)), pl.BlockSpec((1, tc), lambda j, i: (0, nc + j))],
        out_specs=[pl.BlockSpec((2, tm, tc), lambda j, i: (0, i, j)),
                   pl.BlockSpec((8, tc), lambda j, i: (0, j)), pl.BlockSpec((8, tc), lambda j, i: (0, j))],
        out_shape=[jax.ShapeDtypeStruct((2, t, f), _ACT), jax.ShapeDtypeStruct((8, f), f32), jax.ShapeDtypeStruct((8, f), f32)],
        scratch_shapes=[pltpu.VMEM((2, hl + tm + hl, tc), f32), pltpu.VMEM((2, tm + hl, tc), f32),
                        pltpu.VMEM((2, 4, 8, tc), f32)],
        compiler_params=_params(2),
    )(up, up, up, da, da, fw, fw, fb, fb)


def _adamw_math(w, g, m, v):
    m = ADAM_B1 * m + (1.0 - ADAM_B1) * g
    v = ADAM_B2 * v + (1.0 - ADAM_B2) * (g * g)
    m_hat = m / (1.0 - ADAM_B1 ** ADAM_STEP)
    v_hat = v / (1.0 - ADAM_B2 ** ADAM_STEP)
    delta = -ADAM_LR * (m_hat / (jnp.sqrt(v_hat) + ADAM_EPS) + ADAM_WD * w)
    return delta, m, v


def _adamw_shard(name, w, g, m, v):
    _, r, c = w.shape
    tr = next((cand for cand in (256, 176, 128, 64, 32, 16, 8) if r % cand == 0), r)

    def body(w_ref, g_ref, m_ref, v_ref, d_ref, mo_ref, vo_ref):
        d, mn, vn = _adamw_math(w_ref[...], g_ref[...], m_ref[...], v_ref[...])
        d_ref[...] = d
        mo_ref[...] = mn
        vo_ref[...] = vn

    s3 = pl.BlockSpec((None, tr, c), lambda i: (0, i, 0))
    s2 = pl.BlockSpec((tr, c), lambda i: (i, 0))
    shp = jax.ShapeDtypeStruct(w.shape, f32)
    return pl.pallas_call(
        body, name=name, grid=(r // tr,), in_specs=[s3, s2, s3, s3], out_specs=[s3, s3, s3], out_shape=[shp, shp, shp],
        compiler_params=_params(1),
    )(w, g, m, v)


def _adamw_small(quads):
    n = len(quads)

    def body(*refs):
        ins, outs = refs[:4 * n], refs[4 * n:]
        for p in range(n):
            w_ref, g_ref, m_ref, v_ref = ins[4 * p:4 * p + 4]
            d, mn, vn = _adamw_math(w_ref[...], g_ref[...], m_ref[...], v_ref[...])
            outs[3 * p][...] = d
            outs[3 * p + 1][...] = mn
            outs[3 * p + 2][...] = vn

    flat = [a for q in quads for a in q]
    shapes = [jax.ShapeDtypeStruct(q[0].shape, f32) for q in quads for _ in range(3)]
    outs = pl.pallas_call(
        body, name="adamw_small", in_specs=[_VMEM] * (4 * n), out_specs=[_VMEM] * (3 * n), out_shape=shapes,
        compiler_params=pltpu.CompilerParams(vmem_limit_bytes=_VMEM_LIMIT_BYTES),
    )(*flat)
    return [tuple(outs[3 * p:3 * p + 3]) for p in range(n)]


def _sum_pairs(name, place, grads, got):
    _, r, c = grads.shape

    def body(place_ref, a_ref, b_ref, o_ref):
        o_ref[...] = (a_ref[...].astype(f32) + b_ref[...].astype(f32)).astype(o_ref.dtype)

    grid_spec = pltpu.PrefetchScalarGridSpec(
        num_scalar_prefetch=1, grid=(4,),
        in_specs=[pl.BlockSpec((None, r, c), lambda i, p: (2 * i + p[1], 0, 0)), pl.BlockSpec((None, r, c), lambda i, p: (i, 0, 0))],
        out_specs=pl.BlockSpec((None, r, c), lambda i, p: (i, 0, 0)))
    return pl.pallas_call(body, name=name, grid_spec=grid_spec, out_shape=jax.ShapeDtypeStruct((4, r, c), _ACT),
                          compiler_params=_params(1))(place, grads, got)


def _sum_four(name, place, sums, got):
    _, r, c = sums.shape

    def body(place_ref, o_ref, g_ref, f_ref):
        s = o_ref[...].astype(f32) + g_ref[0].astype(f32)
        s = s + g_ref[1].astype(f32)
        f_ref[...] = s + g_ref[2].astype(f32)

    grid_spec = pltpu.PrefetchScalarGridSpec(
        num_scalar_prefetch=1, grid=(1,),
        in_specs=[pl.BlockSpec((None, r, c), lambda i, p: (p[0], 0, 0)), pl.BlockSpec((3, r, c), lambda i, p: (0, 0, 0))],
        out_specs=pl.BlockSpec((None, r, c), lambda i, p: (p[1], 0, 0)))
    return pl.pallas_call(body, name=name, grid_spec=grid_spec, out_shape=jax.ShapeDtypeStruct((2, r, c), f32),
                          compiler_params=_params(1))(place, sums, got)


def _place():
    return lax.axis_index("x"), lax.axis_index("y"), lax.axis_index("c")


def _other_chips(x, y):
    return [(1 - x, y), (x, 1 - y), (1 - x, 1 - y)]


def _remote(src, dst, send_sem, recv_sem, to):
    return pltpu.make_async_remote_copy(src_ref=src, dst_ref=dst, send_sem=send_sem, recv_sem=recv_sem,
                                        device_id=to, device_id_type=_MESH)


def _place_shards(place, shards, col_sharded):
    n = len(shards)
    steps = 4

    def body(place_ref, *refs):
        for src, dst in zip(refs[:n], refs[n:]):
            dst[...] = src[...].astype(dst.dtype)

    in_specs, out_specs, out_shape = [], [], []
    for w, col in zip(shards, col_sharded):
        r, cs = w.shape
        tr = r // steps
        in_specs.append(pl.BlockSpec((tr, cs), lambda i, p: (i, 0)))
        if col:
            out_specs.append(pl.BlockSpec((tr, cs), lambda i, p: (i, p[0])))
            out_shape.append(jax.ShapeDtypeStruct((r, 4 * cs), _ACT))
        else:
            out_specs.append(pl.BlockSpec((tr, cs), lambda i, p: (p[0] * steps + i, 0)))
            out_shape.append(jax.ShapeDtypeStruct((4 * r, cs), _ACT))
    grid_spec = pltpu.PrefetchScalarGridSpec(num_scalar_prefetch=1, grid=(steps,), in_specs=in_specs, out_specs=out_specs)
    return pl.pallas_call(body, name="place_shards", grid_spec=grid_spec, out_shape=out_shape,
                          compiler_params=_params(1))(place, *shards)


def _shard_of(ref, col_sharded, s):
    rows, cols = ref.shape
    if col_sharded:
        return ref.at[:, pl.ds(s * (cols // 4), cols // 4)]
    return ref.at[pl.ds(s * (rows // 4), rows // 4), :]


def _part_of(ref, col_sharded, whole, s, h):
    if whole:
        return _shard_of(ref, col_sharded, s)
    rows, cols = ref.shape
    if col_sharded:
        return ref.at[pl.ds(h * (rows // 2), rows // 2), pl.ds(s * (cols // 4), cols // 4)]
    return ref.at[pl.ds((2 * s + h) * (rows // 8), rows // 8), :]


def _allgather_start(bufs, col_sharded, whole, groups):
    n = len(bufs)
    ng = len(groups)

    def body(*refs):
        out = refs[n:2 * n]
        sems = refs[2 * n:]
        x, y, c = _place()
        for g, members in enumerate(groups):
            for i, w in enumerate(members):
                mine = _part_of(out[w], col_sharded[w], whole[w], 2 * x + y, c)
                for j, chip in enumerate(_other_chips(x, y)):
                    _remote(mine, mine, sems[2 * g].at[3 * i + j], sems[2 * g + 1].at[3 * i + j], (*chip, c)).start()

    sem_shapes = [pltpu.SemaphoreType.DMA((3 * len(m),)) for m in groups for _ in range(2)]
    outs = pl.pallas_call(
        body, name="allgather_start", in_specs=[_HBM] * n, out_specs=[_HBM] * n + [_SEM] * (2 * ng),
        out_shape=[pltpu.HBM(b.shape, b.dtype) for b in bufs] + sem_shapes,
        input_output_aliases={i: i for i in range(n)},
        compiler_params=pltpu.CompilerParams(has_side_effects=_EFFECT),
    )(*[pltpu.with_memory_space_constraint(b, pltpu.HBM) for b in bufs])
    return list(outs[:n]), [(outs[n + 2 * g], outs[n + 2 * g + 1]) for g in range(ng)]


def _allgather_relay(name, bufs, col_sharded, whole, sems, after):
    n = len(bufs)

    def body(*refs):
        buf = refs[:n]
        send, recv = refs[n], refs[n + 1]
        out = refs[n + 3:2 * n + 3]
        to_sibling, from_sibling = refs[2 * n + 3:]
        x, y, c = _place()
        for i in range(n):
            mine = _part_of(buf[i], col_sharded[i], whole[i], 2 * x + y, c)
            for j, chip in enumerate(_other_chips(x, y)):
                landed = _part_of(buf[i], col_sharded[i], whole[i], 2 * chip[0] + chip[1], c)
                cp = _remote(mine, landed, send.at[3 * i + j], recv.at[3 * i + j], (*chip, c))
                cp.wait_send()
                cp.wait_recv()
        for i in range(n):
            if not whole[i]:
                for j, chip in enumerate(_other_chips(x, y)):
                    landed = _part_of(out[i], col_sharded[i], False, 2 * chip[0] + chip[1], c)
                    _remote(landed, landed, to_sibling.at[3 * i + j], from_sibling.at[3 * i + j], (x, y, 1 - c)).start()

    outs = pl.pallas_call(
        body, name=name, in_specs=[_HBM] * n + [_SEM, _SEM, _ANY], out_specs=[_HBM] * n + [_SEM, _SEM],
        out_shape=[pltpu.HBM(b.shape, b.dtype) for b in bufs] + [pltpu.SemaphoreType.DMA((3 * n,))] * 2,
        input_output_aliases={i: i for i in range(n)},
        compiler_params=pltpu.CompilerParams(has_side_effects=_EFFECT),
    )(*bufs, *sems, after)
    return list(outs[:n]), (outs[n], outs[n + 1])


def _allgather_wait(name, bufs, col_sharded, whole, sems, after):
    n = len(bufs)

    def body(*refs):
        buf = refs[:n]
        to_sibling, from_sibling = refs[n], refs[n + 1]
        x, y, c = _place()
        for i in range(n):
            if not whole[i]:
                for j, chip in enumerate(_other_chips(x, y)):
                    sent = _part_of(buf[i], col_sharded[i], False, 2 * chip[0] + chip[1], c)
                    landed = _part_of(buf[i], col_sharded[i], False, 2 * chip[0] + chip[1], 1 - c)
                    cp = _remote(sent, landed, to_sibling.at[3 * i + j], from_sibling.at[3 * i + j], (x, y, 1 - c))
                    cp.wait_send()
                    cp.wait_recv()

    return pl.pallas_call(
        body, name=name, in_specs=[_HBM] * n + [_SEM, _SEM, _ANY], out_specs=[_HBM] * n,
        out_shape=[pltpu.HBM(b.shape, b.dtype) for b in bufs],
        input_output_aliases={i: i for i in range(n)},
        compiler_params=pltpu.CompilerParams(has_side_effects=_EFFECT),
    )(*bufs, *sems, after)


def _exchange_pair_halves(name, grads):
    nw = len(grads)

    def body(*refs):
        src = refs[:nw]
        got = refs[nw:2 * nw]
        send_sem, recv_sem = refs[2 * nw:]
        x, y, c = _place()
        sends = []
        for w in range(nw):
            for s in range(4):
                rc = _remote(src[w].at[2 * s + 1 - c], got[w].at[s], send_sem.at[4 * w + s], recv_sem.at[4 * w + s], (x, y, 1 - c))
                rc.start()
                sends.append(rc)
        for rc in sends:
            rc.wait_recv()
        for rc in sends:
            rc.wait_send()

    return pl.pallas_call(
        body, name=name, in_specs=[_ANY] * nw, out_specs=[_ANY] * nw,
        out_shape=[jax.ShapeDtypeStruct((4,) + g.shape[1:], g.dtype) for g in grads],
        scratch_shapes=[pltpu.SemaphoreType.DMA((4 * nw,)), pltpu.SemaphoreType.DMA((4 * nw,))],
    )(*grads)


def _chip_exchange_start(name, sums):
    nw = len(sums)
    lands = [lax.empty((3,) + s.shape[1:], s.dtype) for s in sums]

    def body(*refs):
        src = refs[2 * nw:3 * nw]
        got = refs[3 * nw:4 * nw]
        send, recv, token = refs[4 * nw:]
        x, y, c = _place()
        for w in range(nw):
            for j, chip in enumerate(_other_chips(x, y)):
                _remote(src[w].at[2 * chip[0] + chip[1]], got[w].at[j], send.at[3 * w + j], recv.at[3 * w + j], (*chip, c)).start()
        token[...] = jnp.zeros_like(token)

    outs = pl.pallas_call(
        body, name=name, in_specs=[_HBM] * (2 * nw), out_specs=[_HBM] * (2 * nw) + [_SEM, _SEM, _VMEM],
        out_shape=[pltpu.HBM(a.shape, a.dtype) for a in list(sums) + lands]
        + [pltpu.SemaphoreType.DMA((3 * nw,)), pltpu.SemaphoreType.DMA((3 * nw,)), jax.ShapeDtypeStruct((8, 128), f32)],
        input_output_aliases={i: i for i in range(2 * nw)},
        compiler_params=pltpu.CompilerParams(has_side_effects=_EFFECT),
    )(*[pltpu.with_memory_space_constraint(a, pltpu.HBM) for a in list(sums) + lands])
    return list(outs[:nw]), list(outs[nw:2 * nw]), (outs[2 * nw], outs[2 * nw + 1]), outs[2 * nw + 2]


def _chip_exchange_wait(name, sums, got, sems, after):
    nw = len(sums)

    def body(*refs):
        src = refs[:nw]
        land = refs[nw:2 * nw]
        send, recv = refs[2 * nw], refs[2 * nw + 1]
        x, y, c = _place()
        for w in range(nw):
            for j, chip in enumerate(_other_chips(x, y)):
                cp = _remote(src[w].at[2 * chip[0] + chip[1]], land[w].at[j], send.at[3 * w + j], recv.at[3 * w + j], (*chip, c))
                cp.wait_send()
                cp.wait_recv()

    outs = pl.pallas_call(
        body, name=name, in_specs=[_HBM] * (2 * nw) + [_SEM, _SEM, _ANY], out_specs=[_HBM] * (2 * nw),
        out_shape=[pltpu.HBM(a.shape, a.dtype) for a in list(sums) + list(got)],
        input_output_aliases={i: i for i in range(2 * nw)},
        compiler_params=pltpu.CompilerParams(has_side_effects=_EFFECT),
    )(*sums, *got, *sems, after)
    return list(outs[:nw]), list(outs[nw:])


def _swap_halves(finals):
    nw = len(finals)

    def body(*refs):
        buf = refs[nw:2 * nw]
        send_sem, recv_sem = refs[2 * nw:]
        x, y, c = _place()
        sends = []
        for w in range(nw):
            rc = _remote(buf[w].at[c], buf[w].at[c], send_sem.at[w], recv_sem.at[w], (x, y, 1 - c))
            rc.start()
            sends.append(rc)
        for w in range(nw):
            _remote(buf[w].at[1 - c], buf[w].at[1 - c], send_sem.at[w], recv_sem.at[w], (x, y, c)).wait_recv()
        for rc in sends:
            rc.wait_send()

    return pl.pallas_call(
        body, name="rs_swap_halves", in_specs=[_ANY] * nw, out_specs=[_ANY] * nw,
        out_shape=[jax.ShapeDtypeStruct(g.shape, g.dtype) for g in finals],
        input_output_aliases={i: i for i in range(nw)},
        scratch_shapes=[pltpu.SemaphoreType.DMA((nw,)), pltpu.SemaphoreType.DMA((nw,))],
    )(*finals)


def _half_slices(shape, h):
    rows, cols = shape
    if cols % 256 == 0:
        return (slice(None), slice(h * (cols // 2), (h + 1) * (cols // 2)))
    return (slice(h * (rows // 2), (h + 1) * (rows // 2)), slice(None))


def _allreduce_small(parts):
    n = len(parts)

    def body(*refs):
        src = refs[:n]
        out = refs[n:2 * n]
        sib = refs[2 * n:3 * n]
        chip_sum = refs[3 * n:4 * n]
        slots = refs[4 * n:5 * n]
        pair_send, pair_recv, ici_send, ici_recv, swap_send, swap_recv = refs[5 * n:]
        x, y, c = _place()
        me_chip = 2 * x + y
        chips = _other_chips(x, y)
        pairs = [_remote(src[a], sib[a], pair_send.at[a], pair_recv.at[a], (x, y, 1 - c)) for a in range(n)]
        for rc in pairs:
            rc.start()
        for a in range(n):
            pairs[a].wait_recv()
            chip_sum[a][...] = src[a][...] + sib[a][...]
        for h in (0, 1):
            @pl.when(c == h)
            def _():
                sends = []
                for a in range(n):
                    idx = _half_slices(parts[a].shape, h)
                    for j, chip in enumerate(chips):
                        rc = _remote(chip_sum[a].at[idx], slots[a].at[me_chip].at[idx], ici_send.at[3 * a + j], ici_recv.at[3 * a + j], (*chip, h))
                        rc.start()
                        sends.append(rc)
                    slots[a][(me_chip,) + idx] = chip_sum[a][idx]
                for a in range(n):
                    idx = _half_slices(parts[a].shape, h)
                    for j, chip in enumerate(chips):
                        landed = slots[a].at[2 * chip[0] + chip[1]].at[idx]
                        _remote(landed, landed, ici_send.at[3 * a + j], ici_recv.at[3 * a + j], (x, y, c)).wait_recv()
                    total = slots[a][(0,) + idx]
                    for s in range(1, 4):
                        total = total + slots[a][(s,) + idx]
                    out[a][idx] = total
                    rc = _remote(out[a].at[idx], out[a].at[idx], swap_send.at[a], swap_recv.at[a], (x, y, 1 - h))
                    rc.start()
                    sends.append(rc)
                for a in range(n):
                    other = out[a].at[_half_slices(parts[a].shape, 1 - h)]
                    _remote(other, other, swap_send.at[a], swap_recv.at[a], (x, y, c)).wait_recv()
                for rc in sends:
                    rc.wait_send()
        for rc in pairs:
            rc.wait_send()

    return pl.pallas_call(
        body, name="allreduce_small", in_specs=[_VMEM] * n, out_specs=[_VMEM] * n,
        out_shape=[jax.ShapeDtypeStruct(p.shape, f32) for p in parts],
        scratch_shapes=[pltpu.VMEM(p.shape, f32) for p in parts] * 2 + [pltpu.VMEM((4,) + p.shape, f32) for p in parts]
        + [pltpu.SemaphoreType.DMA((n,)), pltpu.SemaphoreType.DMA((n,)), pltpu.SemaphoreType.DMA((3 * n,)),
           pltpu.SemaphoreType.DMA((3 * n,)), pltpu.SemaphoreType.DMA((n,)), pltpu.SemaphoreType.DMA((n,))],
        compiler_params=pltpu.CompilerParams(vmem_limit_bytes=_VMEM_LIMIT_BYTES),
    )(*parts)


def _local_step(x, mem, tgt, g_mix, g_xattn, g_mem, g_ffn, g_final, cb, lg, lb, pw, ps, fb, relay, weights, reduce, n_seq, seq, n_mem):
    t, d = x.shape
    f = fb.shape[1] // 2
    c = cb.shape[1]
    h1 = _rms_fwd("norm_mix", x, g_mix)
    relay(0, h1)
    w_in, cw, fw = weights(0, h1)
    u = _mm_nn("proj_in", h1, w_in, _ACT, w_in.shape[1])
    y, hc = _mix_fwd(u, cw, cb, lg, lb, pw, ps, seq)
    relay(1, y)
    w_out, w_q, w_kv, w_o = weights(1, y)
    x1, h2 = _proj_residual_norm("proj_out", y, w_out, x, g_xattn)
    q = _mm_nn("proj_q", h2, w_q, _ACT, d)
    mem_n = _rms_fwd("norm_mem", mem, g_mem)
    kv = _mm_nn("proj_kv", mem_n, w_kv, _ACT, 2 * d)
    o = _attn_fwd(q, kv, n_seq, seq, n_mem)
    relay(2, o)
    x2, h3 = _proj_residual_norm("proj_o", o, w_o, x1, g_ffn)
    w_up, w_down = weights(2, h3)
    up = _mm_nn("proj_up", h3, w_up, _ACT, f, split_out=True)
    a = _ffn_gate_fwd(up, fw, fb, seq)
    dx3, dx3b, dg_final, loss = _proj_loss_bwd("proj_down", a, w_down, x2, g_final, tgt)
    da = _mm_nt("d_act", dx3b, w_down, _ACT)
    gw_down = _mm_tn_rows("dw_down", a, dx3b, f // 2, d // 2)
    dup, sums_g, sums_v = _ffn_gate_bwd(up, da, fw, fb, seq)
    gw_up = _mm_tn_pieces("dw_up", h3, dup, f // 2, t)
    token = reduce(0, [gw_down.reshape(8, -1, d), gw_up])
    dx2, dx2b, dg_ffn = _dproj_rms_bwd("d_h3", dup, w_up, x2, g_ffn + token, dx3)
    do = _mm_nt("d_o", dx2b, w_o, _ACT)
    gw_o = _mm_tn_rows("dw_o", o, dx2b, d, d // 2)
    dq, dkv = _attn_bwd(q, kv, do, n_seq, seq, n_mem)
    gw_q = _mm_tn_rows("dw_q", h2, dq, d, d // 2)
    gw_kv = _mm_tn_pieces("dw_kv", mem_n, dkv, d // 2, mem.shape[0])
    dmem_n = _mm_nt("d_mem_n", dkv, w_kv, f32)
    dg_mem = _rms_gain_grad("norm_mem_bwd", mem, dmem_n)
    dx1, dx1b, dg_xattn = _dproj_rms_bwd("d_h2", dq, w_q, x1, g_xattn, dx2)
    dy = _mm_nt("d_y", dx1b, w_out, _ACT)
    gw_out = _mm_tn_rows("dw_out", y, dx1b, d, d // 2)
    token = reduce(1, [gw_o.reshape(8, -1, d), gw_q.reshape(8, -1, d), gw_kv, gw_out.reshape(8, -1, d)])
    dhc, sums_norm = _mix_bwd_norm(hc, dy, lg + token, lb, seq)
    du, d_cw, d_ps, d_pw = _mix_bwd_taps(u, dhc, dy, cw, pw, ps, seq)
    gw_in = _mm_tn_pieces("dw_in", h1, du, c * 3 // 4, t)
    token = reduce(2, [gw_in])
    grad_x, dg_mix = _dproj_rms_bwd("d_h1", du, w_in, x, g_mix + token, dx1, storage_copy=False)
    zero_row = jnp.zeros((1, d), f32)
    gains = jnp.concatenate([dg_mix, dg_xattn, dg_mem, dg_ffn, dg_final, jnp.pad(loss, ((0, 0), (0, d - 1))), zero_row, zero_row], axis=0)
    conv_rows = jnp.concatenate([sums_norm[2:3], sums_norm[0:1], sums_norm[1:2], d_ps[0:1], jnp.zeros((4, c), f32)], axis=0)
    ffn_rows = jnp.concatenate([sums_g, sums_v], axis=1)
    small = [gains, conv_rows, d_pw.reshape(-1, d_pw.shape[-1]), ffn_rows, d_cw]
    return grad_x, small


def kernel(x, mem, norm_mix_g, w_in, conv_dw_w, conv_dw_b, conv_ln_g, conv_ln_b, pool_w, pool_scale, w_out, norm_xattn_g, norm_mem_g, w_q, w_kv, w_o, norm_ffn_g, w_up, ffn_dw_w, ffn_dw_b, w_down, norm_final_g, loss_target, m_norm_mix_g, m_w_in, m_conv_dw_w, m_conv_dw_b, m_conv_ln_g, m_conv_ln_b, m_pool_w, m_pool_scale, m_w_out, m_norm_xattn_g, m_norm_mem_g, m_w_q, m_w_kv, m_w_o, m_norm_ffn_g, m_w_up, m_ffn_dw_w, m_ffn_dw_b, m_w_down, m_norm_final_g, v_norm_mix_g, v_w_in, v_conv_dw_w, v_conv_dw_b, v_conv_ln_g, v_conv_ln_b, v_pool_w, v_pool_scale, v_w_out, v_norm_xattn_g, v_norm_mem_g, v_w_q, v_w_kv, v_w_o, v_norm_ffn_g, v_w_up, v_ffn_dw_w, v_ffn_dw_b, v_w_down, v_norm_final_g):
    n_seq, seq, d = x.shape
    n_mem = mem.shape[1]
    chip = 2 * lax.axis_index("x") + lax.axis_index("y")

    place = jnp.stack([chip, lax.axis_index("c")]).astype(jnp.int32)

    col_w = [w_in, w_kv, w_up]
    row_w = [w_out, w_q, w_o, w_down]
    col_flags = [True] * 3 + [False] * 4 + [True] * 2
    kw = conv_dw_w.shape[1]

    def padded_in_place(shard, rows):
        full = jnp.zeros((rows, 4 * shard.shape[1]), shard.dtype)
        return lax.dynamic_update_slice(full, shard, (0, chip * shard.shape[1]))

    bufs = list(_place_shards(place, [w[0] for w in col_w + row_w], col_flags[:7]))
    bufs += [padded_in_place(conv_dw_w[0], _HALO), padded_in_place(ffn_dw_w[0], 8)]
    groups = [[0, 7, 8], [3, 4, 1, 5], [2, 6]]
    whole = [False] * 7 + [True] * 2
    bufs, sems = _allgather_start(bufs, col_flags, whole, groups)
    relayed = {}

    def relay(g, after):
        members = groups[g]
        relayed[g] = _allgather_relay("allgather_relay_%d" % g, [bufs[i] for i in members], [col_flags[i] for i in members],
                                      [whole[i] for i in members], sems[g], after)

    def weights(g, after):
        members = groups[g]
        group_bufs, sibling_sems = relayed[g]
        return _allgather_wait("allgather_wait_%d" % g, group_bufs, [col_flags[i] for i in members],
                               [whole[i] for i in members], sibling_sems, after)

    names = ["w_in", "w_kv", "w_up", "w_out", "w_q", "w_o", "w_down"]
    reduce_groups = [["w_down", "w_up"], ["w_o", "w_q", "w_kv", "w_out"], ["w_in"]]
    in_flight = {}

    def reduce(g, grads):
        members = reduce_groups[g]
        got = _exchange_pair_halves("rs_pair_exchange_%d" % g, grads)
        sums = [_sum_pairs("rs_pair_sum_" + n, place, a, b) for n, a, b in zip(members, grads, got)]
        sums, lands, rs_sems, token = _chip_exchange_start("rs_chip_start_%d" % g, sums)
        in_flight[g] = (sums, lands, rs_sems)
        return token[0:1, 0:1]

    grad_x, small = _local_step(
        x.reshape(n_seq * seq, d), mem.reshape(n_seq * n_mem, d), loss_target.reshape(n_seq * seq, d),
        norm_mix_g, norm_xattn_g, norm_mem_g, norm_ffn_g, norm_final_g.reshape(1, d),
        conv_dw_b, conv_ln_g, conv_ln_b, pool_w[0], pool_scale, ffn_dw_b, relay, weights, reduce, n_seq, seq, n_mem)

    finals = {}
    for g, members in enumerate(reduce_groups):
        sums, lands, rs_sems = in_flight[g]
        sums, lands = _chip_exchange_wait("rs_chip_wait_%d" % g, sums, lands, rs_sems, grad_x)
        for n, a, b in zip(members, sums, lands):
            finals[n] = _sum_four("rs_chip_sum_" + n, place, a, b)
    shard_grads = _swap_halves([finals[n] for n in names])

    gains, conv_rows, d_pw, ffn_rows, d_cw = _allreduce_small(small)
    loss = gains[5, 0]

    outs = {}
    big_w = dict(zip(names, col_w + row_w))
    big_m = dict(w_in=m_w_in, w_kv=m_w_kv, w_up=m_w_up, w_out=m_w_out, w_q=m_w_q, w_o=m_w_o, w_down=m_w_down)
    big_v = dict(w_in=v_w_in, w_kv=v_w_kv, w_up=v_w_up, w_out=v_w_out, w_q=v_w_q, w_o=v_w_o, w_down=v_w_down)
    for n, g in zip(names, shard_grads):
        w = big_w[n]
        g2 = g.reshape(w.shape[1], w.shape[2])
        delta, new_m, new_v = _adamw_shard("adamw_" + n, w, g2, big_m[n], big_v[n])
        outs[n] = (g2.reshape(w.shape), delta, new_m, new_v)

    f2 = ffn_dw_b.shape[1]
    cs_c = conv_dw_w.shape[2]
    cs_f = ffn_dw_w.shape[2]
    g_cw = lax.dynamic_slice(d_cw, (0, chip * cs_c), (kw, cs_c)).reshape(conv_dw_w.shape)
    g_fw = lax.dynamic_slice(ffn_rows, (1, chip * cs_f), (ffn_dw_w.shape[1], cs_f)).reshape(ffn_dw_w.shape)
    small_params = [
        ("norm_mix_g", norm_mix_g, gains[0:1], m_norm_mix_g, v_norm_mix_g),
        ("conv_dw_w", conv_dw_w, g_cw, m_conv_dw_w, v_conv_dw_w),
        ("conv_dw_b", conv_dw_b, conv_rows[0:1], m_conv_dw_b, v_conv_dw_b),
        ("conv_ln_g", conv_ln_g, conv_rows[1:2], m_conv_ln_g, v_conv_ln_g),
        ("conv_ln_b", conv_ln_b, conv_rows[2:3], m_conv_ln_b, v_conv_ln_b),
        ("pool_w", pool_w, d_pw.reshape(pool_w.shape), m_pool_w, v_pool_w),
        ("pool_scale", pool_scale, conv_rows[3:4], m_pool_scale, v_pool_scale),
        ("norm_xattn_g", norm_xattn_g, gains[1:2], m_norm_xattn_g, v_norm_xattn_g),
        ("norm_mem_g", norm_mem_g, gains[2:3], m_norm_mem_g, v_norm_mem_g),
        ("norm_ffn_g", norm_ffn_g, gains[3:4], m_norm_ffn_g, v_norm_ffn_g),
        ("ffn_dw_w", ffn_dw_w, g_fw, m_ffn_dw_w, v_ffn_dw_w),
        ("ffn_dw_b", ffn_dw_b, ffn_rows[0:1, :f2], m_ffn_dw_b, v_ffn_dw_b),
        ("norm_final_g", norm_final_g.reshape(1, d), gains[4:5], m_norm_final_g.reshape(1, d), v_norm_final_g.reshape(1, d)),
    ]
    quads = []
    for _, w, g, m, v in small_params:
        shape2 = (-1, w.shape[-1])
        quads.append((w.reshape(shape2), g.reshape(shape2), m.reshape(shape2), v.reshape(shape2)))
    for (n, w, g, _, _), (delta, new_m, new_v) in zip(small_params, _adamw_small(quads)):
        shape = norm_final_g.shape if n == "norm_final_g" else w.shape
        outs[n] = (g.reshape(shape), delta.reshape(shape), new_m.reshape(shape), new_v.reshape(shape))

    order = ["norm_mix_g", "w_in", "conv_dw_w", "conv_dw_b", "conv_ln_g", "conv_ln_b", "pool_w", "pool_scale", "w_out",
             "norm_xattn_g", "norm_mem_g", "w_q", "w_kv", "w_o", "norm_ffn_g", "w_up", "ffn_dw_w", "ffn_dw_b", "w_down",
             "norm_final_g"]
    return (loss, grad_x.reshape(x.shape), *[outs[n][0] for n in order], *[outs[n][1] for n in order],
            *[outs[n][2] for n in order], *[outs[n][3] for n in order])
```

```python
import functools

import jax
import jax.numpy as jnp
from jax import lax
from jax.experimental import pallas as pl
from jax.experimental.pallas import tpu as pltpu

f32 = jnp.float32
_ACT = jnp.bfloat16

EPS = 1e-6
POOL_WINDOWS = (2, 4, 8, 16)
XATTN_HEADS = 4
ADAM_LR = 0.001
ADAM_B1 = 0.9
ADAM_B2 = 0.999
ADAM_EPS = 1e-08
ADAM_WD = 0.01
ADAM_STEP = 10

_VMEM_LIMIT_BYTES = 56 * 1024 * 1024
_MESH = pl.DeviceIdType.MESH
_ANY = pl.BlockSpec(memory_space=pl.ANY)
_VMEM = pl.BlockSpec(memory_space=pltpu.VMEM)
_HBM = pl.BlockSpec(memory_space=pltpu.HBM)
_SEM = pl.BlockSpec(memory_space=pltpu.SEMAPHORE)
_EFFECT = pltpu.SideEffectType.DATAFLOW_SIDE_EFFECTING

_NN = (((1,), (0,)), ((), ()))
_NT = (((1,), (1,)), ((), ()))
_TN = (((0,), (0,)), ((), ()))


def _params(n_grid):
    return pltpu.CompilerParams(dimension_semantics=("arbitrary",) * n_grid, vmem_limit_bytes=_VMEM_LIMIT_BYTES)


def _sigmoid(v):
    return 1.0 / (1.0 + jnp.exp(-v))


def _dot(a, b, dims):
    return lax.dot_general(a, b, dims, preferred_element_type=f32)


def _mm(name, a, b, *, dims, grid, a_spec, b_spec, o_spec, out_shape, nk, acc_shape=None, res=None, res_spec=None):
    def body(*refs):
        if res is None:
            a_ref, b_ref, o_ref, *scratch = refs
            r_ref = None
        else:
            a_ref, b_ref, r_ref, o_ref, *scratch = refs
        p = _dot(a_ref[...], b_ref[...], dims)

        def finish(v):
            if r_ref is not None:
                v = v + r_ref[...]
            o_ref[...] = v.astype(o_ref.dtype)

        if nk == 1:
            finish(p)
        else:
            acc = scratch[0]
            k = pl.program_id(2)

            @pl.when(k == 0)
            def _():
                acc[...] = p

            @pl.when(k > 0)
            def _():
                acc[...] += p

            @pl.when(k == nk - 1)
            def _():
                finish(acc[...])

    ins = [a, b] + ([] if res is None else [res])
    specs = [a_spec, b_spec] + ([] if res is None else [res_spec])
    return pl.pallas_call(
        body, name=name, grid=grid, in_specs=specs, out_specs=o_spec, out_shape=out_shape,
        scratch_shapes=[pltpu.VMEM(acc_shape, f32)] if nk > 1 else [], compiler_params=_params(3),
    )(*ins)


def _row_tile(m):
    return min(512, m)


def _mm_nn(name, a, b, out_dtype, tn, res=None, split_out=False):
    m, k = a.shape
    n = b.shape[1]
    tm = _row_tile(m)
    if split_out:
        out_shape = jax.ShapeDtypeStruct((n // tn, m, tn), out_dtype)
        o_spec = pl.BlockSpec((None, tm, tn), lambda j, i, kk: (j, i, 0))
    else:
        out_shape = jax.ShapeDtypeStruct((m, n), out_dtype)
        o_spec = pl.BlockSpec((tm, tn), lambda j, i, kk: (i, j))
    return _mm(
        name, a, b, dims=_NN, grid=(n // tn, m // tm, 1), nk=1,
        a_spec=pl.BlockSpec((tm, k), lambda j, i, kk: (i, 0)),
        b_spec=pl.BlockSpec((k, tn), lambda j, i, kk: (0, j)),
        o_spec=o_spec, out_shape=out_shape, res=res,
        res_spec=pl.BlockSpec((tm, tn), lambda j, i, kk: (i, j)),
    )


def _mm_nt(name, a, b, out_dtype):
    n, kc = b.shape
    m = a.shape[0]
    tm = _row_tile(m)
    return _mm(
        name, a, b, dims=_NT, grid=(m // tm, 1, 1), nk=1,
        a_spec=pl.BlockSpec((tm, kc), lambda i, j, k: (i, 0)), b_spec=pl.BlockSpec((n, kc), lambda i, j, k: (0, 0)),
        o_spec=pl.BlockSpec((tm, n), lambda i, j, k: (i, 0)),
        out_shape=jax.ShapeDtypeStruct((m, n), out_dtype),
    )


def _mm_tn_rows(name, a, b, tka, tn):
    m, ka = a.shape
    nb = b.shape[1]
    return _mm(
        name, a, b, dims=_TN, grid=(ka // tka, nb // tn, 1), nk=1,
        a_spec=pl.BlockSpec((m, tka), lambda i, j, k: (0, i)),
        b_spec=pl.BlockSpec((m, tn), lambda i, j, k: (0, j)),
        o_spec=pl.BlockSpec((tka, tn), lambda i, j, k: (i, j)),
        out_shape=jax.ShapeDtypeStruct((ka, nb), _ACT),
    )


def _mm_tn_pieces(name, a, b, cs, tt):
    m, ka = a.shape
    nk = m // tt
    if b.ndim == 3:
        b_spec = pl.BlockSpec((None, tt, cs), lambda i, j, k: (j // 2, k, j % 2))
    else:
        b_spec = pl.BlockSpec((tt, cs), lambda i, j, k: (k, j))
    return _mm(
        name, a, b, dims=_TN, grid=(2, 4, nk), nk=nk, acc_shape=(ka // 2, cs),
        a_spec=pl.BlockSpec((tt, ka // 2), lambda i, j, k: (k, i)), b_spec=b_spec,
        o_spec=pl.BlockSpec((None, ka // 2, cs), lambda i, j, k: (2 * j + i, 0, 0)),
        out_shape=jax.ShapeDtypeStruct((8, ka // 2, cs), _ACT),
    )


def _rms_fwd(name, x, g):
    t, d = x.shape
    tm = _row_tile(t)

    def body(x_ref, g_ref, h_ref):
        xv = x_ref[...]
        r = lax.rsqrt(jnp.mean(xv * xv, axis=-1, keepdims=True) + EPS)
        h_ref[...] = (xv * r * g_ref[...]).astype(h_ref.dtype)

    return pl.pallas_call(
        body, name=name, grid=(t // tm,),
        in_specs=[pl.BlockSpec((tm, d), lambda i: (i, 0)), pl.BlockSpec((1, d), lambda i: (0, 0))],
        out_specs=pl.BlockSpec((tm, d), lambda i: (i, 0)), out_shape=jax.ShapeDtypeStruct((t, d), _ACT),
        compiler_params=_params(1),
    )(x, g)


def _fused_rows(name, a, b, product, a_spec, tm, extras, extra_specs, out_shape, out_specs, epilogue):
    ne = len(extras)

    def body(a_ref, b_ref, *refs):
        epilogue(product(a_ref, b_ref), refs[:ne], refs[ne:])

    m = extras[0].shape[0]
    return pl.pallas_call(
        body, name=name, grid=(m // tm,),
        in_specs=[a_spec, pl.BlockSpec(b.shape, lambda i: (0, 0)), *extra_specs], out_specs=out_specs, out_shape=out_shape,
        compiler_params=_params(1),
    )(a, b, *extras)


def _proj_residual_norm(name, a, b, res, g):
    m, k = a.shape
    d = b.shape[1]
    tm = _row_tile(m)

    def epilogue(p, ins, outs):
        xv = p + ins[0][...]
        outs[0][...] = xv
        r = lax.rsqrt(jnp.mean(xv * xv, axis=-1, keepdims=True) + EPS)
        outs[1][...] = (xv * r * ins[1][...]).astype(outs[1].dtype)

    row = pl.BlockSpec((tm, d), lambda i: (i, 0))
    return _fused_rows(
        name, a, b, lambda a_ref, b_ref: _dot(a_ref[...], b_ref[...], _NN), pl.BlockSpec((tm, k), lambda i: (i, 0)), tm,
        [res, g], [row, pl.BlockSpec((1, d), lambda i: (0, 0))],
        [jax.ShapeDtypeStruct((m, d), f32), jax.ShapeDtypeStruct((m, d), _ACT)], [row, row], epilogue)


def _dproj_rms_bwd(name, a, b, x, g, dres, storage_copy=True):
    m, d = x.shape
    if a.ndim == 3:
        nh, _, kh = a.shape
        tm = min(256, m)
        a_spec = pl.BlockSpec((nh, tm, kh), lambda i: (0, i, 0))

        def product(a_ref, b_ref):
            p = _dot(a_ref[0], b_ref[:, 0:kh], _NT)
            for h in range(1, nh):
                p = p + _dot(a_ref[h], b_ref[:, h * kh:(h + 1) * kh], _NT)
            return p
    else:
        tm = _row_tile(m)
        a_spec = pl.BlockSpec((tm, a.shape[1]), lambda i: (i, 0))

        def product(a_ref, b_ref):
            return _dot(a_ref[...], b_ref[...], _NT)

    def epilogue(dhv, ins, outs):
        x_ref, g_ref, dres_ref = ins
        dg_ref = outs[-1]

        @pl.when(pl.program_id(0) == 0)
        def _():
            dg_ref[...] = jnp.zeros_like(dg_ref)

        xv = x_ref[...]
        r = lax.rsqrt(jnp.mean(xv * xv, axis=-1, keepdims=True) + EPS)
        xn = xv * r
        dxn = dhv * g_ref[...]
        dx = r * (dxn - xn * jnp.mean(dxn * xn, axis=-1, keepdims=True)) + dres_ref[...]
        outs[0][...] = dx
        if storage_copy:
            outs[1][...] = dx.astype(outs[1].dtype)
        dg_ref[...] += jnp.sum(dhv * xn, axis=0, keepdims=True)

    row = pl.BlockSpec((tm, d), lambda i: (i, 0))
    vec = pl.BlockSpec((1, d), lambda i: (0, 0))
    copies = [jax.ShapeDtypeStruct((m, d), _ACT)] if storage_copy else []
    return _fused_rows(
        name, a, b, product, a_spec, tm, [x, g, dres], [row, vec, row],
        [jax.ShapeDtypeStruct((m, d), f32)] + copies + [jax.ShapeDtypeStruct((1, d), f32)],
        [row] * (1 + len(copies)) + [vec], epilogue)


def _proj_loss_bwd(name, a, b, res, g, tgt):
    m, k = a.shape
    d = b.shape[1]
    tm = _row_tile(m)

    def epilogue(p, ins, outs):
        res_ref, g_ref, t_ref = ins
        dx_ref, dxb_ref, dg_ref, loss_ref = outs

        @pl.when(pl.program_id(0) == 0)
        def _():
            dg_ref[...] = jnp.zeros_like(dg_ref)
            loss_ref[...] = jnp.zeros_like(loss_ref)

        xv = p + res_ref[...]
        gv = g_ref[...]
        r = lax.rsqrt(jnp.mean(xv * xv, axis=-1, keepdims=True) + EPS)
        xn = xv * r
        err = xn * gv - t_ref[...]
        loss_ref[...] += 0.5 * jnp.sum(jnp.mean(err * err, axis=-1, keepdims=True), axis=0, keepdims=True)
        dout = err * (1.0 / d)
        dxn = dout * gv
        dx = r * (dxn - xn * jnp.mean(dxn * xn, axis=-1, keepdims=True))
        dx_ref[...] = dx
        dxb_ref[...] = dx.astype(dxb_ref.dtype)
        dg_ref[...] += jnp.sum(dout * xn, axis=0, keepdims=True)

    row = pl.BlockSpec((tm, d), lambda i: (i, 0))
    vec = pl.BlockSpec((1, d), lambda i: (0, 0))
    return _fused_rows(
        name, a, b, lambda a_ref, b_ref: _dot(a_ref[...], b_ref[...], _NN), pl.BlockSpec((tm, k), lambda i: (i, 0)), tm,
        [res, g, tgt], [row, vec, row],
        [jax.ShapeDtypeStruct((m, d), f32), jax.ShapeDtypeStruct((m, d), _ACT), jax.ShapeDtypeStruct((1, d), f32),
         jax.ShapeDtypeStruct((1, 1), f32)],
        [row, row, vec, pl.BlockSpec((1, 1), lambda i: (0, 0))], epilogue)


def _rms_gain_grad(name, x, dh):
    t, d = x.shape
    tm = _row_tile(t)

    def body(x_ref, dh_ref, dg_ref):
        @pl.when(pl.program_id(0) == 0)
        def _():
            dg_ref[...] = jnp.zeros_like(dg_ref)

        xv = x_ref[...]
        r = lax.rsqrt(jnp.mean(xv * xv, axis=-1, keepdims=True) + EPS)
        dg_ref[...] += jnp.sum(dh_ref[...] * (xv * r), axis=0, keepdims=True)

    row = pl.BlockSpec((tm, d), lambda i: (i, 0))
    return pl.pallas_call(
        body, name=name, grid=(t // tm,), in_specs=[row, row], out_specs=pl.BlockSpec((1, d), lambda i: (0, 0)),
        out_shape=jax.ShapeDtypeStruct((1, d), f32), compiler_params=_params(1),
    )(x, dh)


_CONV_ROWS = 256
_CHUNK = 64
_HALO = 32


def _pool_counts(pos, w):
    return jnp.minimum(pos + 1.0, float(w))


def _rows_from(win, start, rows):
    if start % 8 == 0:
        return win[start:start + rows, :]
    n = win.shape[0]
    return pltpu.roll(win, n - start % 8, axis=0)[start - start % 8:start - start % 8 + rows, :]


def _tap_rows(buf, starts, rows):
    for residue in range(8):
        group = [(k, s) for k, s in starts.items() if s % 8 == residue]
        if group:
            lo = min(s for _, s in group) - residue
            hi = max(s for _, s in group) - residue + rows + (8 if residue else 0)
            win = buf[lo:hi, :]
            if residue:
                win = pltpu.roll(win, hi - lo - residue, axis=0)
            for k, s in group:
                yield k, win[s - residue - lo:s - residue - lo + rows, :]


def _mix_fwd(u, cw, cb, lg, lb, pw, ps, seq):
    t, c3 = u.shape
    c = c3 // 3
    kw = 31
    tm = min(_CONV_ROWS, seq)
    tps = seq // tm
    gd = c // len(POOL_WINDOWS)

    def body(u_ref, uh_ref, cw_ref, cb_ref, lg_ref, lb_ref, pw_ref, ps_ref, y_ref, hc_ref, hgbuf, pbuf):
        i = pl.program_id(0)
        keep = jnp.where(i % tps == 0, 0.0, 1.0)
        um = u_ref[...].astype(f32)
        uh = uh_ref[...].astype(f32) * keep
        hgbuf[0:_HALO, :] = uh[:, 0:c] * _sigmoid(uh[:, c:2 * c])
        hgbuf[_HALO:_HALO + tm, :] = um[:, 0:c] * _sigmoid(um[:, c:2 * c])
        pbuf[0:_HALO, :] = uh[:, 2 * c:]
        pbuf[_HALO:_HALO + tm, :] = um[:, 2 * c:]
        for r0 in range(0, tm, _CHUNK):
            acc = jnp.broadcast_to(cb_ref[...], (_CHUNK, c))
            for k, rows in _tap_rows(hgbuf, {k: r0 + _HALO - (kw - 1) + k for k in range(kw)}, _CHUNK):
                acc = acc + cw_ref[k:k + 1, :] * rows
            hc_ref[r0:r0 + _CHUNK, :] = acc
            mu = jnp.mean(acc, axis=-1, keepdims=True)
            xc = acc - mu
            var = jnp.mean(xc * xc, axis=-1, keepdims=True)
            hl = xc * lax.rsqrt(var + EPS) * lg_ref[...] + lb_ref[...]
            y_ref[r0:r0 + _CHUNK, 0:c] = (hl * _sigmoid(hl)).astype(y_ref.dtype)
        pos = ((i % tps) * tm).astype(f32) + lax.broadcasted_iota(jnp.int32, (tm, 1), 0).astype(f32)
        for gi, w in enumerate(POOL_WINDOWS):
            sl = slice(gi * gd, (gi + 1) * gd)
            v = pbuf[_HALO:_HALO + tm, sl]
            s = v
            for j in range(1, w):
                s = s + pbuf[_HALO - j:_HALO - j + tm, sl]
            pooled = s / _pool_counts(pos, w) - v
            mixed = _dot(pooled.astype(_ACT), pw_ref[gi].astype(_ACT), _NN)
            y_ref[:, c + gi * gd:c + (gi + 1) * gd] = (mixed * ps_ref[:, sl]).astype(y_ref.dtype)

    hb = tm // _HALO
    full = lambda shape: pl.BlockSpec(shape, lambda i: (0,) * len(shape))
    return pl.pallas_call(
        body, name="mix_fwd", grid=(t // tm,),
        in_specs=[pl.BlockSpec((tm, c3), lambda i: (i, 0)),
                  pl.BlockSpec((_HALO, c3), lambda i: (jnp.maximum(i * hb - 1, 0), 0)),
                  full((_HALO, c)), full((1, c)), full((1, c)), full((1, c)), full((len(POOL_WINDOWS), gd, gd)), full((1, c))],
        out_specs=[pl.BlockSpec((tm, 2 * c), lambda i: (i, 0)), pl.BlockSpec((tm, c), lambda i: (i, 0))],
        out_shape=[jax.ShapeDtypeStruct((t, 2 * c), _ACT), jax.ShapeDtypeStruct((t, c), f32)],
        scratch_shapes=[pltpu.VMEM((_HALO + tm, c), f32), pltpu.VMEM((_HALO + tm, c), f32)],
        compiler_params=_params(1),
    )(u, u, cw, cb, lg, lb, pw, ps)


def _mix_bwd_norm(hc, dy, lg, lb, seq):
    t, c = hc.shape
    tm = min(_CONV_ROWS, seq)

    def body(hc_ref, dy_ref, lg_ref, lb_ref, dhc_ref, sums_ref):
        @pl.when(pl.program_id(0) == 0)
        def _():
            sums_ref[...] = jnp.zeros_like(sums_ref)

        hcv = hc_ref[...]
        mu = jnp.mean(hcv, axis=-1, keepdims=True)
        xc = hcv - mu
        rstd = lax.rsqrt(jnp.mean(xc * xc, axis=-1, keepdims=True) + EPS)
        n = xc * rstd
        hl = n * lg_ref[...] + lb_ref[...]
        sg = _sigmoid(hl)
        dhl = dy_ref[...].astype(f32) * (sg * (1.0 + hl * (1.0 - sg)))
        dn = dhl * lg_ref[...]
        dhc = rstd * (dn - jnp.mean(dn, axis=-1, keepdims=True) - n * jnp.mean(dn * n, axis=-1, keepdims=True))
        dhc_ref[...] = dhc
        sums_ref[0:1, :] += jnp.sum(dhl * n, axis=0, keepdims=True)
        sums_ref[1:2, :] += jnp.sum(dhl, axis=0, keepdims=True)
        sums_ref[2:3, :] += jnp.sum(dhc, axis=0, keepdims=True)

    row = pl.BlockSpec((tm, c), lambda i: (i, 0))
    vec = pl.BlockSpec((1, c), lambda i: (0, 0))
    return pl.pallas_call(
        body, name="mix_bwd_norm", grid=(t // tm,), in_specs=[row, row, vec, vec],
        out_specs=[row, pl.BlockSpec((8, c), lambda i: (0, 0))],
        out_shape=[jax.ShapeDtypeStruct((t, c), f32), jax.ShapeDtypeStruct((8, c), f32)],
        compiler_params=_params(1),
    )(hc, dy, lg, lb)


def _mix_bwd_taps(u, dhc, dy, cw, pw, ps, seq):
    t, c3 = u.shape
    c = c3 // 3
    kw = 31
    tm = min(_CONV_ROWS, seq)
    tps = seq // tm
    ng = len(POOL_WINDOWS)
    gd = c // ng
    nh = 16

    def body(u_ref, uh_ref, dhc_ref, dhcn_ref, dy_ref, dyn_ref, cw_ref, pw_ref, ps_ref,
             du_ref, dcw_ref, dps_ref, dpw_ref, hgbuf, dcbuf, pbuf, dpbuf):
        i = pl.program_id(0)
        keep_prev = jnp.where(i % tps == 0, 0.0, 1.0)
        keep_next = jnp.where(i % tps == tps - 1, 0.0, 1.0)

        @pl.when(i == 0)
        def _():
            dcw_ref[...] = jnp.zeros_like(dcw_ref)
            dps_ref[...] = jnp.zeros_like(dps_ref)
            dpw_ref[...] = jnp.zeros_like(dpw_ref)

        uh = uh_ref[...].astype(f32) * keep_prev
        hgbuf[0:_HALO, :] = uh[:, 0:c] * _sigmoid(uh[:, c:2 * c])
        pbuf[0:_HALO, :] = uh[:, 2 * c:]
        um = u_ref[...].astype(f32)
        hgbuf[_HALO:_HALO + tm, :] = um[:, 0:c] * _sigmoid(um[:, c:2 * c])
        pbuf[_HALO:_HALO + tm, :] = um[:, 2 * c:]
        dcbuf[0:tm, :] = dhc_ref[...]
        dcbuf[tm:tm + _HALO, :] = dhcn_ref[...] * keep_next
        tap_sums = [None] * kw
        for r0 in range(0, tm, _CHUNK):
            dh = dcbuf[r0:r0 + _CHUNK, :]
            acc = jnp.zeros((_CHUNK, c), f32)
            for k, rows in _tap_rows(hgbuf, {k: r0 + _HALO - (kw - 1) + k for k in range(kw)}, _CHUNK):
                part = (dh * rows).reshape(_CHUNK // 8, 8, c).sum(axis=0)
                tap_sums[k] = part if tap_sums[k] is None else tap_sums[k] + part
            for k, rows in _tap_rows(dcbuf, {k: r0 + (kw - 1) - k for k in range(kw)}, _CHUNK):
                acc = acc + cw_ref[k:k + 1, :] * rows
            val = u_ref[r0:r0 + _CHUNK, 0:c].astype(f32)
            sg = _sigmoid(u_ref[r0:r0 + _CHUNK, c:2 * c].astype(f32))
            du_ref[r0:r0 + _CHUNK, 0:c] = (acc * sg).astype(du_ref.dtype)
            du_ref[r0:r0 + _CHUNK, c:2 * c] = (acc * val * sg * (1.0 - sg)).astype(du_ref.dtype)
        for k in range(kw):
            dcw_ref[k:k + 1, :] += jnp.sum(tap_sums[k], axis=0, keepdims=True)
        base = ((i % tps) * tm).astype(f32)
        pos = base + lax.broadcasted_iota(jnp.int32, (tm, 1), 0).astype(f32)
        pos_next = base + float(tm) + lax.broadcasted_iota(jnp.int32, (nh, 1), 0).astype(f32)
        for gi, w in enumerate(POOL_WINDOWS):
            sl = slice(gi * gd, (gi + 1) * gd)
            v = pbuf[_HALO:_HALO + tm, sl]
            s = v
            for j in range(1, w):
                s = s + pbuf[_HALO - j:_HALO - j + tm, sl]
            cnt = _pool_counts(pos, w)
            pooled = (s / cnt - v).astype(_ACT)
            pwg = pw_ref[gi].astype(_ACT)
            mixed = _dot(pooled, pwg, _NN)
            dyp = dy_ref[:, sl].astype(f32)
            dps_ref[0:1, sl] += jnp.sum(dyp * mixed, axis=0, keepdims=True)
            dmix = (dyp * ps_ref[:, sl]).astype(_ACT)
            dpw_ref[gi] += _dot(pooled, dmix, _TN)
            dmix_next = (dyn_ref[:, sl].astype(f32) * ps_ref[:, sl] * keep_next).astype(_ACT)
            dpool = _dot(dmix, pwg, _NT)
            dpbuf[0:tm, sl] = dpool / cnt
            dpbuf[tm:tm + nh, sl] = _dot(dmix_next, pwg, _NT) / _pool_counts(pos_next, w)
            acc = -dpool
            for j in range(w):
                acc = acc + dpbuf[j:j + tm, sl]
            du_ref[:, 2 * c + gi * gd:2 * c + (gi + 1) * gd] = acc.astype(du_ref.dtype)

    hb = tm // _HALO
    n_halo = t // _HALO
    n_nh = t // nh
    full = lambda shape: pl.BlockSpec(shape, lambda i: (0,) * len(shape))
    return pl.pallas_call(
        body, name="mix_bwd_taps", grid=(t // tm,),
        in_specs=[pl.BlockSpec((tm, c3), lambda i: (i, 0)),
                  pl.BlockSpec((_HALO, c3), lambda i: (jnp.maximum(i * hb - 1, 0), 0)),
                  pl.BlockSpec((tm, c), lambda i: (i, 0)),
                  pl.BlockSpec((_HALO, c), lambda i: (jnp.minimum((i + 1) * hb, n_halo - 1), 0)),
                  pl.BlockSpec((tm, c), lambda i: (i, 1)),
                  pl.BlockSpec((nh, c), lambda i: (jnp.minimum((i + 1) * (tm // nh), n_nh - 1), 1)),
                  full((_HALO, c)), full((ng, gd, gd)), full((1, c))],
        out_specs=[pl.BlockSpec((tm, c3), lambda i: (i, 0)), full((_HALO, c)), full((8, c)), full((ng, gd, gd))],
        out_shape=[jax.ShapeDtypeStruct((t, c3), _ACT), jax.ShapeDtypeStruct((_HALO, c), f32),
                   jax.ShapeDtypeStruct((8, c), f32), jax.ShapeDtypeStruct((ng, gd, gd), f32)],
        scratch_shapes=[pltpu.VMEM((_HALO + tm, c), f32), pltpu.VMEM((tm + _HALO, c), f32),
                        pltpu.VMEM((_HALO + tm, c), f32), pltpu.VMEM((tm + nh, c), f32)],
        compiler_params=_params(1),
    )(u, u, dhc, dhc, dy, dy, cw, pw, ps)


def _attn_fwd(q, kv, n_seq, seq, n_mem):
    t, d = q.shape
    dh = d // XATTN_HEADS
    tq = min(512, seq)
    nq = seq // tq
    scale = dh ** -0.5

    def body(q_ref, kv_ref, o_ref):
        for h in range(XATTN_HEADS):
            cols = slice(h * dh, (h + 1) * dh)
            s = _dot(q_ref[:, cols], kv_ref[:, cols], _NT) * scale
            e = jnp.exp(s - jnp.max(s, axis=-1, keepdims=True))
            p = e / jnp.sum(e, axis=-1, keepdims=True)
            o_ref[:, cols] = _dot(p.astype(_ACT), kv_ref[:, d + h * dh:d + (h + 1) * dh], _NN).astype(o_ref.dtype)

    qs = pl.BlockSpec((tq, d), lambda b, i: (b * nq + i, 0))
    return pl.pallas_call(
        body, name="attn_fwd", grid=(n_seq, nq), in_specs=[qs, pl.BlockSpec((n_mem, 2 * d), lambda b, i: (b, 0))],
        out_specs=qs, out_shape=jax.ShapeDtypeStruct((t, d), _ACT), compiler_params=_params(2),
    )(q, kv)


def _attn_bwd(q, kv, do, n_seq, seq, n_mem):
    t, d = q.shape
    dh = d // XATTN_HEADS
    tq = min(512, seq)
    nq = seq // tq
    scale = dh ** -0.5

    def body(q_ref, kv_ref, do_ref, dq_ref, dkv_ref, acc):
        i = pl.program_id(1)

        @pl.when(i == 0)
        def _():
            acc[...] = jnp.zeros_like(acc)

        for h in range(XATTN_HEADS):
            cols = slice(h * dh, (h + 1) * dh)
            vcols = slice(d + h * dh, d + (h + 1) * dh)
            qv = q_ref[:, cols]
            kh = kv_ref[:, cols]
            dov = do_ref[:, cols]
            s = _dot(qv, kh, _NT) * scale
            e = jnp.exp(s - jnp.max(s, axis=-1, keepdims=True))
            p = e / jnp.sum(e, axis=-1, keepdims=True)
            dp = _dot(dov, kv_ref[:, vcols], _NT)
            ds = (p * (dp - jnp.sum(dp * p, axis=-1, keepdims=True)) * scale).astype(_ACT)
            dq_ref[:, cols] = _dot(ds, kh, _NN).astype(dq_ref.dtype)
            acc[:, cols] += _dot(ds, qv, _TN)
            acc[:, vcols] += _dot(p.astype(_ACT), dov, _TN)

        @pl.when(i == nq - 1)
        def _():
            dkv_ref[...] = acc[...].astype(dkv_ref.dtype)

    qs = pl.BlockSpec((tq, d), lambda b, i: (b * nq + i, 0))
    ms = pl.BlockSpec((n_mem, 2 * d), lambda b, i: (b, 0))
    return pl.pallas_call(
        body, name="attn_bwd", grid=(n_seq, nq), in_specs=[qs, ms, qs], out_specs=[qs, ms],
        out_shape=[jax.ShapeDtypeStruct((t, d), _ACT), jax.ShapeDtypeStruct((n_seq * n_mem, 2 * d), _ACT)],
        scratch_shapes=[pltpu.VMEM((n_mem, 2 * d), f32)], compiler_params=_params(2),
    )(q, kv, do)


_FFN_ROWS = 1024
_FFN_COLS = 256
_FFN_HALO = 16


def _window(buf, g, start, rows):
    return buf[g, pl.ds(start, rows + 8), :]


def _taps3(win, rows):
    return [_rows_from(win, 6 + k, rows) for k in range(3)]


def _conv3(b_ref, w_ref, taps):
    acc = b_ref[...] + w_ref[0:1, :] * taps[0]
    for k in (1, 2):
        acc = acc + w_ref[k:k + 1, :] * taps[k]
    return acc


def _ffn_gate_fwd(up, fw, fb, seq):
    _, t, f = up.shape
    tm = min(_FFN_ROWS, seq)
    tps = seq // tm
    tc = _FFN_COLS
    nc = f // tc
    hl = _FFN_HALO

    def body(up_ref, uph_ref, wg_ref, wv_ref, bg_ref, bv_ref, a_ref, buf):
        i = pl.program_id(1)
        keep = jnp.where(i % tps == 0, 0.0, 1.0)
        buf[:, 0:hl, :] = uph_ref[...].astype(f32) * keep
        buf[:, hl:hl + tm, :] = up_ref[...].astype(f32)

        def chunk(ci, carry):
            r0 = pl.multiple_of(ci * _CHUNK, _CHUNK)
            conv = []
            for g, (w_ref, b_ref) in enumerate(((wg_ref, bg_ref), (wv_ref, bv_ref))):
                conv.append(_conv3(b_ref, w_ref, _taps3(_window(buf, g, r0 + hl - 8, _CHUNK), _CHUNK)))
            gate, val = conv
            a_ref[pl.ds(r0, _CHUNK), :] = (gate * _sigmoid(gate) * val).astype(a_ref.dtype)
            return carry

        lax.fori_loop(0, tm // _CHUNK, chunk, 0)

    hb = tm // hl
    return pl.pallas_call(
        body, name="ffn_gate_fwd", grid=(nc, t // tm),
        in_specs=[pl.BlockSpec((2, tm, tc), lambda j, i: (0, i, j)),
                  pl.BlockSpec((2, hl, tc), lambda j, i: (0, jnp.maximum(i * hb - 1, 0), j)),
                  pl.BlockSpec((8, tc), lambda j, i: (0, j)), pl.BlockSpec((8, tc), lambda j, i: (0, nc + j)),
                  pl.BlockSpec((1, tc), lambda j, i: (0, j)), pl.BlockSpec((1, tc), lambda j, i: (0, nc + j))],
        out_specs=pl.BlockSpec((tm, tc), lambda j, i: (i, j)),
        out_shape=jax.ShapeDtypeStruct((t, f), _ACT),
        scratch_shapes=[pltpu.VMEM((2, hl + tm, tc), f32)], compiler_params=_params(2),
    )(up, up, fw, fw, fb, fb)


def _ffn_gate_bwd(up, da, fw, fb, seq):
    _, t, f = up.shape
    tm = min(_FFN_ROWS, seq)
    tps = seq // tm
    tc = _FFN_COLS
    nc = f // tc
    hl = _FFN_HALO

    def body(up_ref, uph_ref, upn_ref, da_ref, dan_ref, wg_ref, wv_ref, bg_ref, bv_ref,
             dup_ref, sg_ref, sv_ref, ubuf, dbuf, sums):
        i = pl.program_id(1)
        keep_prev = jnp.where(i % tps == 0, 0.0, 1.0)
        keep_next = jnp.where(i % tps == tps - 1, 0.0, 1.0)

        @pl.when(i == 0)
        def _():
            sg_ref[...] = jnp.zeros_like(sg_ref)
            sv_ref[...] = jnp.zeros_like(sv_ref)

        sums[...] = jnp.zeros_like(sums)
        ubuf[:, 0:hl, :] = uph_ref[...].astype(f32) * keep_prev
        ubuf[:, hl:hl + tm, :] = up_ref[...].astype(f32)
        ubuf[:, hl + tm:hl + tm + hl, :] = upn_ref[...].astype(f32) * keep_next
        w_refs = (wg_ref, wv_ref)
        b_refs = (bg_ref, bv_ref)

        def grads(r0, rows, dav, count):
            taps = [_taps3(_window(ubuf, g, r0 + hl - 8, rows), rows) for g in range(2)]
            gate, val = [_conv3(b_refs[g], w_refs[g], taps[g]) for g in range(2)]
            sg = _sigmoid(gate)
            douts = (dav * val * (sg * (1.0 + gate * (1.0 - sg))), dav * (gate * sg))
            for g in range(2):
                dbuf[g, pl.ds(r0, rows), :] = douts[g]
                if count:
                    sums[g, 0] += douts[g].reshape(rows // 8, 8, tc).sum(axis=0)
                    for k in range(3):
                        sums[g, 1 + k] += (douts[g] * taps[g][k]).reshape(rows // 8, 8, tc).sum(axis=0)

        def first(ci, carry):
            r0 = pl.multiple_of(ci * _CHUNK, _CHUNK)
            grads(r0, _CHUNK, da_ref[pl.ds(r0, _CHUNK), :].astype(f32), True)
            return carry

        lax.fori_loop(0, tm // _CHUNK, first, 0)
        grads(tm, hl, dan_ref[...].astype(f32) * keep_next, False)

        def second(ci, carry):
            r0 = pl.multiple_of(ci * _CHUNK, _CHUNK)
            for g in range(2):
                win = _window(dbuf, g, r0, _CHUNK)
                acc = jnp.zeros((_CHUNK, tc), f32)
                for k in range(3):
                    acc = acc + w_refs[g][k:k + 1, :] * _rows_from(win, 2 - k, _CHUNK)
                dup_ref[g, pl.ds(r0, _CHUNK), :] = acc.astype(dup_ref.dtype)
            return carry

        lax.fori_loop(0, tm // _CHUNK, second, 0)
        for g, s_ref in enumerate((sg_ref, sv_ref)):
            for r in range(4):
                s_ref[r:r + 1, :] += jnp.sum(sums[g, r], axis=0, keepdims=True)

    hb = tm // hl
    n_halo = t // hl
    return pl.pallas_call(
        body, name="ffn_gate_bwd", grid=(nc, t // tm),
        in_specs=[pl.BlockSpec((2, tm, tc), lambda j, i: (0, i, j)),
                  pl.BlockSpec((2, hl, tc), lambda j, i: (0, jnp.maximum(i * hb - 1, 0), j)),
                  pl.BlockSpec((2, hl, tc), lambda j, i: (0, jnp.minimum((i + 1) * hb, n_halo - 1), j)),
                  pl.BlockSpec((tm, tc), lambda j, i: (i, j)),
                  pl.BlockSpec((hl, tc), lambda j, i: (jnp.minimum((i + 1) * hb, n_halo - 1), j)),
                  pl.BlockSpec((8, tc), lambda j, i: (0, j)), pl.BlockSpec((8, tc), lambda j, i: (0, nc + j)),
                  pl.BlockSpec((1, tc), lambda j, i: (0, j)), pl.BlockSpec((1, tc), lambda j, i: (0, nc + j))],
        out_specs=[pl.BlockSpec((2, tm, tc), lambda j, i: (0, i, j)),
                   pl.BlockSpec((8, tc), lambda j, i: (0, j)), pl.BlockSpec((8, tc), lambda j, i: (0, j))],
        out_shape=[jax.ShapeDtypeStruct((2, t, f), _ACT), jax.ShapeDtypeStruct((8, f), f32), jax.ShapeDtypeStruct((8, f), f32)],
        scratch_shapes=[pltpu.VMEM((2, hl + tm + hl, tc), f32), pltpu.VMEM((2, tm + hl, tc), f32),
                        pltpu.VMEM((2, 4, 8, tc), f32)],
        compiler_params=_params(2),
    )(up, up, up, da, da, fw, fw, fb, fb)


def _adamw_math(w, g, m, v):
    m = ADAM_B1 * m + (1.0 - ADAM_B1) * g
    v = ADAM_B2 * v + (1.0 - ADAM_B2) * (g * g)
    m_hat = m / (1.0 - ADAM_B1 ** ADAM_STEP)
    v_hat = v / (1.0 - ADAM_B2 ** ADAM_STEP)
    delta = -ADAM_LR * (m_hat / (jnp.sqrt(v_hat) + ADAM_EPS) + ADAM_WD * w)
    return delta, m, v


def _adamw_shard(name, w, g, m, v):
    _, r, c = w.shape
    tr = next((cand for cand in (256, 176, 128, 64, 32, 16, 8) if r % cand == 0), r)

    def body(w_ref, g_ref, m_ref, v_ref, d_ref, mo_ref, vo_ref):
        d, mn, vn = _adamw_math(w_ref[...], g_ref[...], m_ref[...], v_ref[...])
        d_ref[...] = d
        mo_ref[...] = mn
        vo_ref[...] = vn

    s3 = pl.BlockSpec((None, tr, c), lambda i: (0, i, 0))
    s2 = pl.BlockSpec((tr, c), lambda i: (i, 0))
    shp = jax.ShapeDtypeStruct(w.shape, f32)
    return pl.pallas_call(
        body, name=name, grid=(r // tr,), in_specs=[s3, s2, s3, s3], out_specs=[s3, s3, s3], out_shape=[shp, shp, shp],
        compiler_params=_params(1),
    )(w, g, m, v)


def _adamw_small(quads):
    n = len(quads)

    def body(*refs):
        ins, outs = refs[:4 * n], refs[4 * n:]
        for p in range(n):
            w_ref, g_ref, m_ref, v_ref = ins[4 * p:4 * p + 4]
            d, mn, vn = _adamw_math(w_ref[...], g_ref[...], m_ref[...], v_ref[...])
            outs[3 * p][...] = d
            outs[3 * p + 1][...] = mn
            outs[3 * p + 2][...] = vn

    flat = [a for q in quads for a in q]
    shapes = [jax.ShapeDtypeStruct(q[0].shape, f32) for q in quads for _ in range(3)]
    outs = pl.pallas_call(
        body, name="adamw_small", in_specs=[_VMEM] * (4 * n), out_specs=[_VMEM] * (3 * n), out_shape=shapes,
        compiler_params=pltpu.CompilerParams(vmem_limit_bytes=_VMEM_LIMIT_BYTES),
    )(*flat)
    return [tuple(outs[3 * p:3 * p + 3]) for p in range(n)]


def _sum_pairs(name, place, grads, got):
    _, r, c = grads.shape

    def body(place_ref, a_ref, b_ref, o_ref):
        o_ref[...] = (a_ref[...].astype(f32) + b_ref[...].astype(f32)).astype(o_ref.dtype)

    grid_spec = pltpu.PrefetchScalarGridSpec(
        num_scalar_prefetch=1, grid=(4,),
        in_specs=[pl.BlockSpec((None, r, c), lambda i, p: (2 * i + p[1], 0, 0)), pl.BlockSpec((None, r, c), lambda i, p: (i, 0, 0))],
        out_specs=pl.BlockSpec((None, r, c), lambda i, p: (i, 0, 0)))
    return pl.pallas_call(body, name=name, grid_spec=grid_spec, out_shape=jax.ShapeDtypeStruct((4, r, c), _ACT),
                          compiler_params=_params(1))(place, grads, got)


def _sum_four(name, place, sums, got):
    _, r, c = sums.shape

    def body(place_ref, o_ref, g_ref, f_ref):
        s = o_ref[...].astype(f32) + g_ref[0].astype(f32)
        s = s + g_ref[1].astype(f32)
        f_ref[...] = s + g_ref[2].astype(f32)

    grid_spec = pltpu.PrefetchScalarGridSpec(
        num_scalar_prefetch=1, grid=(1,),
        in_specs=[pl.BlockSpec((None, r, c), lambda i, p: (p[0], 0, 0)), pl.BlockSpec((3, r, c), lambda i, p: (0, 0, 0))],
        out_specs=pl.BlockSpec((None, r, c), lambda i, p: (p[1], 0, 0)))
    return pl.pallas_call(body, name=name, grid_spec=grid_spec, out_shape=jax.ShapeDtypeStruct((2, r, c), f32),
                          compiler_params=_params(1))(place, sums, got)


def _place():
    return lax.axis_index("x"), lax.axis_index("y"), lax.axis_index("c")


def _other_chips(x, y):
    return [(1 - x, y), (x, 1 - y), (1 - x, 1 - y)]


def _remote(src, dst, send_sem, recv_sem, to):
    return pltpu.make_async_remote_copy(src_ref=src, dst_ref=dst, send_sem=send_sem, recv_sem=recv_sem,
                                        device_id=to, device_id_type=_MESH)


def _place_shards(place, shards, col_sharded):
    n = len(shards)
    steps = 4

    def body(place_ref, *refs):
        for src, dst in zip(refs[:n], refs[n:]):
            dst[...] = src[...].astype(dst.dtype)

    in_specs, out_specs, out_shape = [], [], []
    for w, col in zip(shards, col_sharded):
        r, cs = w.shape
        tr = r // steps
        in_specs.append(pl.BlockSpec((tr, cs), lambda i, p: (i, 0)))
        if col:
            out_specs.append(pl.BlockSpec((tr, cs), lambda i, p: (i, p[0])))
            out_shape.append(jax.ShapeDtypeStruct((r, 4 * cs), _ACT))
        else:
            out_specs.append(pl.BlockSpec((tr, cs), lambda i, p: (p[0] * steps + i, 0)))
            out_shape.append(jax.ShapeDtypeStruct((4 * r, cs), _ACT))
    grid_spec = pltpu.PrefetchScalarGridSpec(num_scalar_prefetch=1, grid=(steps,), in_specs=in_specs, out_specs=out_specs)
    return pl.pallas_call(body, name="place_shards", grid_spec=grid_spec, out_shape=out_shape,
                          compiler_params=_params(1))(place, *shards)


def _shard_of(ref, col_sharded, s):
    rows, cols = ref.shape
    if col_sharded:
        return ref.at[:, pl.ds(s * (cols // 4), cols // 4)]
    return ref.at[pl.ds(s * (rows // 4), rows // 4), :]


def _part_of(ref, col_sharded, whole, s, h):
    if whole:
        return _shard_of(ref, col_sharded, s)
    rows, cols = ref.shape
    if col_sharded:
        return ref.at[pl.ds(h * (rows // 2), rows // 2), pl.ds(s * (cols // 4), cols // 4)]
    return ref.at[pl.ds((2 * s + h) * (rows // 8), rows // 8), :]


def _allgather_start(bufs, col_sharded, whole, groups):
    n = len(bufs)
    ng = len(groups)

    def body(*refs):
        out = refs[n:2 * n]
        sems = refs[2 * n:]
        x, y, c = _place()
        for g, members in enumerate(groups):
            for i, w in enumerate(members):
                mine = _part_of(out[w], col_sharded[w], whole[w], 2 * x + y, c)
                for j, chip in enumerate(_other_chips(x, y)):
                    _remote(mine, mine, sems[2 * g].at[3 * i + j], sems[2 * g + 1].at[3 * i + j], (*chip, c)).start()

    sem_shapes = [pltpu.SemaphoreType.DMA((3 * len(m),)) for m in groups for _ in range(2)]
    outs = pl.pallas_call(
        body, name="allgather_start", in_specs=[_HBM] * n, out_specs=[_HBM] * n + [_SEM] * (2 * ng),
        out_shape=[pltpu.HBM(b.shape, b.dtype) for b in bufs] + sem_shapes,
        input_output_aliases={i: i for i in range(n)},
        compiler_params=pltpu.CompilerParams(has_side_effects=_EFFECT),
    )(*[pltpu.with_memory_space_constraint(b, pltpu.HBM) for b in bufs])
    return list(outs[:n]), [(outs[n + 2 * g], outs[n + 2 * g + 1]) for g in range(ng)]


def _allgather_relay(name, bufs, col_sharded, whole, sems, after):
    n = len(bufs)

    def body(*refs):
        buf = refs[:n]
        send, recv = refs[n], refs[n + 1]
        out = refs[n + 3:2 * n + 3]
        to_sibling, from_sibling = refs[2 * n + 3:]
        x, y, c = _place()
        for i in range(n):
            mine = _part_of(buf[i], col_sharded[i], whole[i], 2 * x + y, c)
            for j, chip in enumerate(_other_chips(x, y)):
                landed = _part_of(buf[i], col_sharded[i], whole[i], 2 * chip[0] + chip[1], c)
                cp = _remote(mine, landed, send.at[3 * i + j], recv.at[3 * i + j], (*chip, c))
                cp.wait_send()
                cp.wait_recv()
        for i in range(n):
            if not whole[i]:
                for j, chip in enumerate(_other_chips(x, y)):
                    landed = _part_of(out[i], col_sharded[i], False, 2 * chip[0] + chip[1], c)
                    _remote(landed, landed, to_sibling.at[3 * i + j], from_sibling.at[3 * i + j], (x, y, 1 - c)).start()

    outs = pl.pallas_call(
        body, name=name, in_specs=[_HBM] * n + [_SEM, _SEM, _ANY], out_specs=[_HBM] * n + [_SEM, _SEM],
        out_shape=[pltpu.HBM(b.shape, b.dtype) for b in bufs] + [pltpu.SemaphoreType.DMA((3 * n,))] * 2,
        input_output_aliases={i: i for i in range(n)},
        compiler_params=pltpu.CompilerParams(has_side_effects=_EFFECT),
    )(*bufs, *sems, after)
    return list(outs[:n]), (outs[n], outs[n + 1])


def _allgather_wait(name, bufs, col_sharded, whole, sems, after):
    n = len(bufs)

    def body(*refs):
        buf = refs[:n]
        to_sibling, from_sibling = refs[n], refs[n + 1]
        x, y, c = _place()
        for i in range(n):
            if not whole[i]:
                for j, chip in enumerate(_other_chips(x, y)):
                    sent = _part_of(buf[i], col_sharded[i], False, 2 * chip[0] + chip[1], c)
                    landed = _part_of(buf[i], col_sharded[i], False, 2 * chip[0] + chip[1], 1 - c)
                    cp = _remote(sent, landed, to_sibling.at[3 * i + j], from_sibling.at[3 * i + j], (x, y, 1 - c))
                    cp.wait_send()
                    cp.wait_recv()

    return pl.pallas_call(
        body, name=name, in_specs=[_HBM] * n + [_SEM, _SEM, _ANY], out_specs=[_HBM] * n,
        out_shape=[pltpu.HBM(b.shape, b.dtype) for b in bufs],
        input_output_aliases={i: i for i in range(n)},
        compiler_params=pltpu.CompilerParams(has_side_effects=_EFFECT),
    )(*bufs, *sems, after)


def _exchange_pair_halves(name, grads):
    nw = len(grads)

    def body(*refs):
        src = refs[:nw]
        got = refs[nw:2 * nw]
        send_sem, recv_sem = refs[2 * nw:]
        x, y, c = _place()
        sends = []
        for w in range(nw):
            for s in range(4):
                rc = _remote(src[w].at[2 * s + 1 - c], got[w].at[s], send_sem.at[4 * w + s], recv_sem.at[4 * w + s], (x, y, 1 - c))
                rc.start()
                sends.append(rc)
        for rc in sends:
            rc.wait_recv()
        for rc in sends:
            rc.wait_send()

    return pl.pallas_call(
        body, name=name, in_specs=[_ANY] * nw, out_specs=[_ANY] * nw,
        out_shape=[jax.ShapeDtypeStruct((4,) + g.shape[1:], g.dtype) for g in grads],
        scratch_shapes=[pltpu.SemaphoreType.DMA((4 * nw,)), pltpu.SemaphoreType.DMA((4 * nw,))],
    )(*grads)


def _chip_exchange_start(name, sums):
    nw = len(sums)
    lands = [lax.empty((3,) + s.shape[1:], s.dtype) for s in sums]

    def body(*refs):
        src = refs[2 * nw:3 * nw]
        got = refs[3 * nw:4 * nw]
        send, recv, token = refs[4 * nw:]
        x, y, c = _place()
        for w in range(nw):
            for j, chip in enumerate(_other_chips(x, y)):
                _remote(src[w].at[2 * chip[0] + chip[1]], got[w].at[j], send.at[3 * w + j], recv.at[3 * w + j], (*chip, c)).start()
        token[...] = jnp.zeros_like(token)

    outs = pl.pallas_call(
        body, name=name, in_specs=[_HBM] * (2 * nw), out_specs=[_HBM] * (2 * nw) + [_SEM, _SEM, _VMEM],
        out_shape=[pltpu.HBM(a.shape, a.dtype) for a in list(sums) + lands]
        + [pltpu.SemaphoreType.DMA((3 * nw,)), pltpu.SemaphoreType.DMA((3 * nw,)), jax.ShapeDtypeStruct((8, 128), f32)],
        input_output_aliases={i: i for i in range(2 * nw)},
        compiler_params=pltpu.CompilerParams(has_side_effects=_EFFECT),
    )(*[pltpu.with_memory_space_constraint(a, pltpu.HBM) for a in list(sums) + lands])
    return list(outs[:nw]), list(outs[nw:2 * nw]), (outs[2 * nw], outs[2 * nw + 1]), outs[2 * nw + 2]


def _chip_exchange_wait(name, sums, got, sems, after):
    nw = len(sums)

    def body(*refs):
        src = refs[:nw]
        land = refs[nw:2 * nw]
        send, recv = refs[2 * nw], refs[2 * nw + 1]
        x, y, c = _place()
        for w in range(nw):
            for j, chip in enumerate(_other_chips(x, y)):
                cp = _remote(src[w].at[2 * chip[0] + chip[1]], land[w].at[j], send.at[3 * w + j], recv.at[3 * w + j], (*chip, c))
                cp.wait_send()
                cp.wait_recv()

    outs = pl.pallas_call(
        body, name=name, in_specs=[_HBM] * (2 * nw) + [_SEM, _SEM, _ANY], out_specs=[_HBM] * (2 * nw),
        out_shape=[pltpu.HBM(a.shape, a.dtype) for a in list(sums) + list(got)],
        input_output_aliases={i: i for i in range(2 * nw)},
        compiler_params=pltpu.CompilerParams(has_side_effects=_EFFECT),
    )(*sums, *got, *sems, after)
    return list(outs[:nw]), list(outs[nw:])


def _swap_halves(finals):
    nw = len(finals)

    def body(*refs):
        buf = refs[nw:2 * nw]
        send_sem, recv_sem = refs[2 * nw:]
        x, y, c = _place()
        sends = []
        for w in range(nw):
            rc = _remote(buf[w].at[c], buf[w].at[c], send_sem.at[w], recv_sem.at[w], (x, y, 1 - c))
            rc.start()
            sends.append(rc)
        for w in range(nw):
            _remote(buf[w].at[1 - c], buf[w].at[1 - c], send_sem.at[w], recv_sem.at[w], (x, y, c)).wait_recv()
        for rc in sends:
            rc.wait_send()

    return pl.pallas_call(
        body, name="rs_swap_halves", in_specs=[_ANY] * nw, out_specs=[_ANY] * nw,
        out_shape=[jax.ShapeDtypeStruct(g.shape, g.dtype) for g in finals],
        input_output_aliases={i: i for i in range(nw)},
        scratch_shapes=[pltpu.SemaphoreType.DMA((nw,)), pltpu.SemaphoreType.DMA((nw,))],
    )(*finals)


def _half_slices(shape, h):
    rows, cols = shape
    if cols % 256 == 0:
        return (slice(None), slice(h * (cols // 2), (h + 1) * (cols // 2)))
    return (slice(h * (rows // 2), (h + 1) * (rows // 2)), slice(None))


def _allreduce_small(parts):
    n = len(parts)

    def body(*refs):
        src = refs[:n]
        out = refs[n:2 * n]
        sib = refs[2 * n:3 * n]
        chip_sum = refs[3 * n:4 * n]
        slots = refs[4 * n:5 * n]
        pair_send, pair_recv, ici_send, ici_recv, swap_send, swap_recv = refs[5 * n:]
        x, y, c = _place()
        me_chip = 2 * x + y
        chips = _other_chips(x, y)
        pairs = [_remote(src[a], sib[a], pair_send.at[a], pair_recv.at[a], (x, y, 1 - c)) for a in range(n)]
        for rc in pairs:
            rc.start()
        for a in range(n):
            pairs[a].wait_recv()
            chip_sum[a][...] = src[a][...] + sib[a][...]
        for h in (0, 1):
            @pl.when(c == h)
            def _():
                sends = []
                for a in range(n):
                    idx = _half_slices(parts[a].shape, h)
                    for j, chip in enumerate(chips):
                        rc = _remote(chip_sum[a].at[idx], slots[a].at[me_chip].at[idx], ici_send.at[3 * a + j], ici_recv.at[3 * a + j], (*chip, h))
                        rc.start()
                        sends.append(rc)
                    slots[a][(me_chip,) + idx] = chip_sum[a][idx]
                for a in range(n):
                    idx = _half_slices(parts[a].shape, h)
                    for j, chip in enumerate(chips):
                        landed = slots[a].at[2 * chip[0] + chip[1]].at[idx]
                        _remote(landed, landed, ici_send.at[3 * a + j], ici_recv.at[3 * a + j], (x, y, c)).wait_recv()
                    total = slots[a][(0,) + idx]
                    for s in range(1, 4):
                        total = total + slots[a][(s,) + idx]
                    out[a][idx] = total
                    rc = _remote(out[a].at[idx], out[a].at[idx], swap_send.at[a], swap_recv.at[a], (x, y, 1 - h))
                    rc.start()
                    sends.append(rc)
                for a in range(n):
                    other = out[a].at[_half_slices(parts[a].shape, 1 - h)]
                    _remote(other, other, swap_send.at[a], swap_recv.at[a], (x, y, c)).wait_recv()
                for rc in sends:
                    rc.wait_send()
        for rc in pairs:
            rc.wait_send()

    return pl.pallas_call(
        body, name="allreduce_small", in_specs=[_VMEM] * n, out_specs=[_VMEM] * n,
        out_shape=[jax.ShapeDtypeStruct(p.shape, f32) for p in parts],
        scratch_shapes=[pltpu.VMEM(p.shape, f32) for p in parts] * 2 + [pltpu.VMEM((4,) + p.shape, f32) for p in parts]
        + [pltpu.SemaphoreType.DMA((n,)), pltpu.SemaphoreType.DMA((n,)), pltpu.SemaphoreType.DMA((3 * n,)),
           pltpu.SemaphoreType.DMA((3 * n,)), pltpu.SemaphoreType.DMA((n,)), pltpu.SemaphoreType.DMA((n,))],
        compiler_params=pltpu.CompilerParams(vmem_limit_bytes=_VMEM_LIMIT_BYTES),
    )(*parts)


def _local_step(x, mem, tgt, g_mix, g_xattn, g_mem, g_ffn, g_final, cb, lg, lb, pw, ps, fb, relay, weights, reduce, n_seq, seq, n_mem):
    t, d = x.shape
    f = fb.shape[1] // 2
    c = cb.shape[1]
    h1 = _rms_fwd("norm_mix", x, g_mix)
    relay(0, h1)
    w_in, cw, fw = weights(0, h1)
    u = _mm_nn("proj_in", h1, w_in, _ACT, w_in.shape[1])
    y, hc = _mix_fwd(u, cw, cb, lg, lb, pw, ps, seq)
    relay(1, y)
    w_out, w_q, w_kv, w_o = weights(1, y)
    x1, h2 = _proj_residual_norm("proj_out", y, w_out, x, g_xattn)
    q = _mm_nn("proj_q", h2, w_q, _ACT, d)
    mem_n = _rms_fwd("norm_mem", mem, g_mem)
    kv = _mm_nn("proj_kv", mem_n, w_kv, _ACT, 2 * d)
    o = _attn_fwd(q, kv, n_seq, seq, n_mem)
    relay(2, o)
    x2, h3 = _proj_residual_norm("proj_o", o, w_o, x1, g_ffn)
    w_up, w_down = weights(2, h3)
    up = _mm_nn("proj_up", h3, w_up, _ACT, f, split_out=True)
    a = _ffn_gate_fwd(up, fw, fb, seq)
    dx3, dx3b, dg_final, loss = _proj_loss_bwd("proj_down", a, w_down, x2, g_final, tgt)
    da = _mm_nt("d_act", dx3b, w_down, _ACT)
    gw_down = _mm_tn_rows("dw_down", a, dx3b, f // 2, d // 2)
    dup, sums_g, sums_v = _ffn_gate_bwd(up, da, fw, fb, seq)
    gw_up = _mm_tn_pieces("dw_up", h3, dup, f // 2, t)
    token = reduce(0, [gw_down.reshape(8, -1, d), gw_up])
    dx2, dx2b, dg_ffn = _dproj_rms_bwd("d_h3", dup, w_up, x2, g_ffn + token, dx3)
    do = _mm_nt("d_o", dx2b, w_o, _ACT)
    gw_o = _mm_tn_rows("dw_o", o, dx2b, d, d // 2)
    dq, dkv = _attn_bwd(q, kv, do, n_seq, seq, n_mem)
    gw_q = _mm_tn_rows("dw_q", h2, dq, d, d // 2)
    gw_kv = _mm_tn_pieces("dw_kv", mem_n, dkv, d // 2, mem.shape[0])
    dmem_n = _mm_nt("d_mem_n", dkv, w_kv, f32)
    dg_mem = _rms_gain_grad("norm_mem_bwd", mem, dmem_n)
    dx1, dx1b, dg_xattn = _dproj_rms_bwd("d_h2", dq, w_q, x1, g_xattn, dx2)
    dy = _mm_nt("d_y", dx1b, w_out, _ACT)
    gw_out = _mm_tn_rows("dw_out", y, dx1b, d, d // 2)
    token = reduce(1, [gw_o.reshape(8, -1, d), gw_q.reshape(8, -1, d), gw_kv, gw_out.reshape(8, -1, d)])
    dhc, sums_norm = _mix_bwd_norm(hc, dy, lg + token, lb, seq)
    du, d_cw, d_ps, d_pw = _mix_bwd_taps(u, dhc, dy, cw, pw, ps, seq)
    gw_in = _mm_tn_pieces("dw_in", h1, du, c * 3 // 4, t)
    token = reduce(2, [gw_in])
    grad_x, dg_mix = _dproj_rms_bwd("d_h1", du, w_in, x, g_mix + token, dx1, storage_copy=False)
    zero_row = jnp.zeros((1, d), f32)
    gains = jnp.concatenate([dg_mix, dg_xattn, dg_mem, dg_ffn, dg_final, jnp.pad(loss, ((0, 0), (0, d - 1))), zero_row, zero_row], axis=0)
    conv_rows = jnp.concatenate([sums_norm[2:3], sums_norm[0:1], sums_norm[1:2], d_ps[0:1], jnp.zeros((4, c), f32)], axis=0)
    ffn_rows = jnp.concatenate([sums_g, sums_v], axis=1)
    small = [gains, conv_rows, d_pw.reshape(-1, d_pw.shape[-1]), ffn_rows, d_cw]
    return grad_x, small


def kernel(x, mem, norm_mix_g, w_in, conv_dw_w, conv_dw_b, conv_ln_g, conv_ln_b, pool_w, pool_scale, w_out, norm_xattn_g, norm_mem_g, w_q, w_kv, w_o, norm_ffn_g, w_up, ffn_dw_w, ffn_dw_b, w_down, norm_final_g, loss_target, m_norm_mix_g, m_w_in, m_conv_dw_w, m_conv_dw_b, m_conv_ln_g, m_conv_ln_b, m_pool_w, m_pool_scale, m_w_out, m_norm_xattn_g, m_norm_mem_g, m_w_q, m_w_kv, m_w_o, m_norm_ffn_g, m_w_up, m_ffn_dw_w, m_ffn_dw_b, m_w_down, m_norm_final_g, v_norm_mix_g, v_w_in, v_conv_dw_w, v_conv_dw_b, v_conv_ln_g, v_conv_ln_b, v_pool_w, v_pool_scale, v_w_out, v_norm_xattn_g, v_norm_mem_g, v_w_q, v_w_kv, v_w_o, v_norm_ffn_g, v_w_up, v_ffn_dw_w, v_ffn_dw_b, v_w_down, v_norm_final_g):
    n_seq, seq, d = x.shape
    n_mem = mem.shape[1]
    chip = 2 * lax.axis_index("x") + lax.axis_index("y")

    place = jnp.stack([chip, lax.axis_index("c")]).astype(jnp.int32)

    col_w = [w_in, w_kv, w_up]
    row_w = [w_out, w_q, w_o, w_down]
    col_flags = [True] * 3 + [False] * 4 + [True] * 2
    kw = conv_dw_w.shape[1]

    def padded_in_place(shard, rows):
        full = jnp.zeros((rows, 4 * shard.shape[1]), shard.dtype)
        return lax.dynamic_update_slice(full, shard, (0, chip * shard.shape[1]))

    bufs = list(_place_shards(place, [w[0] for w in col_w + row_w], col_flags[:7]))
    bufs += [padded_in_place(conv_dw_w[0], _HALO), padded_in_place(ffn_dw_w[0], 8)]
    groups = [[0, 7, 8], [3, 4, 1, 5], [2, 6]]
    whole = [False] * 7 + [True] * 2
    bufs, sems = _allgather_start(bufs, col_flags, whole, groups)
    relayed = {}

    def relay(g, after):
        members = groups[g]
        relayed[g] = _allgather_relay("allgather_relay_%d" % g, [bufs[i] for i in members], [col_flags[i] for i in members],
                                      [whole[i] for i in members], sems[g], after)

    def weights(g, after):
        members = groups[g]
        group_bufs, sibling_sems = relayed[g]
        return _allgather_wait("allgather_wait_%d" % g, group_bufs, [col_flags[i] for i in members],
                               [whole[i] for i in members], sibling_sems, after)

    names = ["w_in", "w_kv", "w_up", "w_out", "w_q", "w_o", "w_down"]
    reduce_groups = [["w_down", "w_up"], ["w_o", "w_q", "w_kv", "w_out"], ["w_in"]]
    in_flight = {}

    def reduce(g, grads):
        members = reduce_groups[g]
        got = _exchange_pair_halves("rs_pair_exchange_%d" % g, grads)
        sums = [_sum_pairs("rs_pair_sum_" + n, place, a, b) for n, a, b in zip(members, grads, got)]
        sums, lands, rs_sems, token = _chip_exchange_start("rs_chip_start_%d" % g, sums)
        in_flight[g] = (sums, lands, rs_sems)
        return token[0:1, 0:1]

    grad_x, small = _local_step(
        x.reshape(n_seq * seq, d), mem.reshape(n_seq * n_mem, d), loss_target.reshape(n_seq * seq, d),
        norm_mix_g, norm_xattn_g, norm_mem_g, norm_ffn_g, norm_final_g.reshape(1, d),
        conv_dw_b, conv_ln_g, conv_ln_b, pool_w[0], pool_scale, ffn_dw_b, relay, weights, reduce, n_seq, seq, n_mem)

    finals = {}
    for g, members in enumerate(reduce_groups):
        sums, lands, rs_sems = in_flight[g]
        sums, lands = _chip_exchange_wait("rs_chip_wait_%d" % g, sums, lands, rs_sems, grad_x)
        for n, a, b in zip(members, sums, lands):
            finals[n] = _sum_four("rs_chip_sum_" + n, place, a, b)
    shard_grads = _swap_halves([finals[n] for n in names])

    gains, conv_rows, d_pw, ffn_rows, d_cw = _allreduce_small(small)
    loss = gains[5, 0]

    outs = {}
    big_w = dict(zip(names, col_w + row_w))
    big_m = dict(w_in=m_w_in, w_kv=m_w_kv, w_up=m_w_up, w_out=m_w_out, w_q=m_w_q, w_o=m_w_o, w_down=m_w_down)
    big_v = dict(w_in=v_w_in, w_kv=v_w_kv, w_up=v_w_up, w_out=v_w_out, w_q=v_w_q, w_o=v_w_o, w_down=v_w_down)
    for n, g in zip(names, shard_grads):
        w = big_w[n]
        g2 = g.reshape(w.shape[1], w.shape[2])
        delta, new_m, new_v = _adamw_shard("adamw_" + n, w, g2, big_m[n], big_v[n])
        outs[n] = (g2.reshape(w.shape), delta, new_m, new_v)

    f2 = ffn_dw_b.shape[1]
    cs_c = conv_dw_w.shape[2]
    cs_f = ffn_dw_w.shape[2]
    g_cw = lax.dynamic_slice(d_cw, (0, chip * cs_c), (kw, cs_c)).reshape(conv_dw_w.shape)
    g_fw = lax.dynamic_slice(ffn_rows, (1, chip * cs_f), (ffn_dw_w.shape[1], cs_f)).reshape(ffn_dw_w.shape)
    small_params = [
        ("norm_mix_g", norm_mix_g, gains[0:1], m_norm_mix_g, v_norm_mix_g),
        ("conv_dw_w", conv_dw_w, g_cw, m_conv_dw_w, v_conv_dw_w),
        ("conv_dw_b", conv_dw_b, conv_rows[0:1], m_conv_dw_b, v_conv_dw_b),
        ("conv_ln_g", conv_ln_g, conv_rows[1:2], m_conv_ln_g, v_conv_ln_g),
        ("conv_ln_b", conv_ln_b, conv_rows[2:3], m_conv_ln_b, v_conv_ln_b),
        ("pool_w", pool_w, d_pw.reshape(pool_w.shape), m_pool_w, v_pool_w),
        ("pool_scale", pool_scale, conv_rows[3:4], m_pool_scale, v_pool_scale),
        ("norm_xattn_g", norm_xattn_g, gains[1:2], m_norm_xattn_g, v_norm_xattn_g),
        ("norm_mem_g", norm_mem_g, gains[2:3], m_norm_mem_g, v_norm_mem_g),
        ("norm_ffn_g", norm_ffn_g, gains[3:4], m_norm_ffn_g, v_norm_ffn_g),
        ("ffn_dw_w", ffn_dw_w, g_fw, m_ffn_dw_w, v_ffn_dw_w),
        ("ffn_dw_b", ffn_dw_b, ffn_rows[0:1, :f2], m_ffn_dw_b, v_ffn_dw_b),
        ("norm_final_g", norm_final_g.reshape(1, d), gains[4:5], m_norm_final_g.reshape(1, d), v_norm_final_g.reshape(1, d)),
    ]
    quads = []
    for _, w, g, m, v in small_params:
        shape2 = (-1, w.shape[-1])
        quads.append((w.reshape(shape2), g.reshape(shape2), m.reshape(shape2), v.reshape(shape2)))
    for (n, w, g, _, _), (delta, new_m, new_v) in zip(small_params, _adamw_small(quads)):
        shape = norm_final_g.shape if n == "norm_final_g" else w.shape
        outs[n] = (g.reshape(shape), delta.reshape(shape), new_m.reshape(shape), new_v.reshape(shape))

    order = ["norm_mix_g", "w_in", "conv_dw_w", "conv_dw_b", "conv_ln_g", "conv_ln_b", "pool_w", "pool_scale", "w_out",
             "norm_xattn_g", "norm_mem_g", "w_q", "w_kv", "w_o", "norm_ffn_g", "w_up", "ffn_dw_w", "ffn_dw_b", "w_down",
             "norm_final_g"]
    return (loss, grad_x.reshape(x.shape), *[outs[n][0] for n in order], *[outs[n][1] for n in order],
            *[outs[n][2] for n in order], *[outs[n][3] for n in order])
```

```python
import functools

import jax
import jax.numpy as jnp
from jax import lax
from jax.experimental import pallas as pl
from jax.experimental.pallas import tpu as pltpu

f32 = jnp.float32
_ACT = jnp.bfloat16

EPS = 1e-6
POOL_WINDOWS = (2, 4, 8, 16)
XATTN_HEADS = 4
ADAM_LR = 0.001
ADAM_B1 = 0.9
ADAM_B2 = 0.999
ADAM_EPS = 1e-08
ADAM_WD = 0.01
ADAM_STEP = 10

_VMEM_LIMIT_BYTES = 56 * 1024 * 1024
_MESH = pl.DeviceIdType.MESH
_ANY = pl.BlockSpec(memory_space=pl.ANY)
_VMEM = pl.BlockSpec(memory_space=pltpu.VMEM)
_HBM = pl.BlockSpec(memory_space=pltpu.HBM)
_SEM = pl.BlockSpec(memory_space=pltpu.SEMAPHORE)
_EFFECT = pltpu.SideEffectType.DATAFLOW_SIDE_EFFECTING

_NN = (((1,), (0,)), ((), ()))
_NT = (((1,), (1,)), ((), ()))
_TN = (((0,), (0,)), ((), ()))


def _params(n_grid):
    return pltpu.CompilerParams(dimension_semantics=("arbitrary",) * n_grid, vmem_limit_bytes=_VMEM_LIMIT_BYTES)


def _sigmoid(v):
    return 1.0 / (1.0 + jnp.exp(-v))


def _dot(a, b, dims):
    return lax.dot_general(a, b, dims, preferred_element_type=f32)


def _mm(name, a, b, *, dims, grid, a_spec, b_spec, o_spec, out_shape, nk, acc_shape=None, res=None, res_spec=None):
    def body(*refs):
        if res is None:
            a_ref, b_ref, o_ref, *scratch = refs
            r_ref = None
        else:
            a_ref, b_ref, r_ref, o_ref, *scratch = refs
        p = _dot(a_ref[...], b_ref[...], dims)

        def finish(v):
            if r_ref is not None:
                v = v + r_ref[...]
            o_ref[...] = v.astype(o_ref.dtype)

        if nk == 1:
            finish(p)
        else:
            acc = scratch[0]
            k = pl.program_id(2)

            @pl.when(k == 0)
            def _():
                acc[...] = p

            @pl.when(k > 0)
            def _():
                acc[...] += p

            @pl.when(k == nk - 1)
            def _():
                finish(acc[...])

    ins = [a, b] + ([] if res is None else [res])
    specs = [a_spec, b_spec] + ([] if res is None else [res_spec])
    return pl.pallas_call(
        body, name=name, grid=grid, in_specs=specs, out_specs=o_spec, out_shape=out_shape,
        scratch_shapes=[pltpu.VMEM(acc_shape, f32)] if nk > 1 else [], compiler_params=_params(3),
    )(*ins)


def _row_tile(m):
    return min(512, m)


def _mm_nn(name, a, b, out_dtype, tn, res=None, split_out=False):
    m, k = a.shape
    n = b.shape[1]
    tm = _row_tile(m)
    if split_out:
        out_shape = jax.ShapeDtypeStruct((n // tn, m, tn), out_dtype)
        o_spec = pl.BlockSpec((None, tm, tn), lambda j, i, kk: (j, i, 0))
    else:
        out_shape = jax.ShapeDtypeStruct((m, n), out_dtype)
        o_spec = pl.BlockSpec((tm, tn), lambda j, i, kk: (i, j))
    return _mm(
        name, a, b, dims=_NN, grid=(n // tn, m // tm, 1), nk=1,
        a_spec=pl.BlockSpec((tm, k), lambda j, i, kk: (i, 0)),
        b_spec=pl.BlockSpec((k, tn), lambda j, i, kk: (0, j)),
        o_spec=o_spec, out_shape=out_shape, res=res,
        res_spec=pl.BlockSpec((tm, tn), lambda j, i, kk: (i, j)),
    )


def _mm_nt(name, a, b, out_dtype):
    n, kc = b.shape
    m = a.shape[0]
    tm = _row_tile(m)
    return _mm(
        name, a, b, dims=_NT, grid=(m // tm, 1, 1), nk=1,
        a_spec=pl.BlockSpec((tm, kc), lambda i, j, k: (i, 0)), b_spec=pl.BlockSpec((n, kc), lambda i, j, k: (0, 0)),
        o_spec=pl.BlockSpec((tm, n), lambda i, j, k: (i, 0)),
        out_shape=jax.ShapeDtypeStruct((m, n), out_dtype),
    )


def _mm_tn_rows(name, a, b, tka, tn):
    m, ka = a.shape
    nb = b.shape[1]
    return _mm(
        name, a, b, dims=_TN, grid=(ka // tka, nb // tn, 1), nk=1,
        a_spec=pl.BlockSpec((m, tka), lambda i, j, k: (0, i)),
        b_spec=pl.BlockSpec((m, tn), lambda i, j, k: (0, j)),
        o_spec=pl.BlockSpec((tka, tn), lambda i, j, k: (i, j)),
        out_shape=jax.ShapeDtypeStruct((ka, nb), _ACT),
    )


def _mm_tn_pieces(name, a, b, cs, tt):
    m, ka = a.shape
    nk = m // tt
    if b.ndim == 3:
        b_spec = pl.BlockSpec((None, tt, cs), lambda i, j, k: (j // 2, k, j % 2))
    else:
        b_spec = pl.BlockSpec((tt, cs), lambda i, j, k: (k, j))
    return _mm(
        name, a, b, dims=_TN, grid=(2, 4, nk), nk=nk, acc_shape=(ka // 2, cs),
        a_spec=pl.BlockSpec((tt, ka // 2), lambda i, j, k: (k, i)), b_spec=b_spec,
        o_spec=pl.BlockSpec((None, ka // 2, cs), lambda i, j, k: (2 * j + i, 0, 0)),
        out_shape=jax.ShapeDtypeStruct((8, ka // 2, cs), _ACT),
    )


def _rms_fwd(name, x, g):
    t, d = x.shape
    tm = _row_tile(t)

    def body(x_ref, g_ref, h_ref):
        xv = x_ref[...]
        r = lax.rsqrt(jnp.mean(xv * xv, axis=-1, keepdims=True) + EPS)
        h_ref[...] = (xv * r * g_ref[...]).astype(h_ref.dtype)

    return pl.pallas_call(
        body, name=name, grid=(t // tm,),
        in_specs=[pl.BlockSpec((tm, d), lambda i: (i, 0)), pl.BlockSpec((1, d), lambda i: (0, 0))],
        out_specs=pl.BlockSpec((tm, d), lambda i: (i, 0)), out_shape=jax.ShapeDtypeStruct((t, d), _ACT),
        compiler_params=_params(1),
    )(x, g)


def _fused_rows(name, a, b, product, a_spec, tm, extras, extra_specs, out_shape, out_specs, epilogue):
    ne = len(extras)

    def body(a_ref, b_ref, *refs):
        epilogue(product(a_ref, b_ref), refs[:ne], refs[ne:])

    m = extras[0].shape[0]
    return pl.pallas_call(
        body, name=name, grid=(m // tm,),
        in_specs=[a_spec, pl.BlockSpec(b.shape, lambda i: (0, 0)), *extra_specs], out_specs=out_specs, out_shape=out_shape,
        compiler_params=_params(1),
    )(a, b, *extras)


def _proj_residual_norm(name, a, b, res, g):
    m, k = a.shape
    d = b.shape[1]
    tm = _row_tile(m)

    def epilogue(p, ins, outs):
        xv = p + ins[0][...]
        outs[0][...] = xv
        r = lax.rsqrt(jnp.mean(xv * xv, axis=-1, keepdims=True) + EPS)
        outs[1][...] = (xv * r * ins[1][...]).astype(outs[1].dtype)

    row = pl.BlockSpec((tm, d), lambda i: (i, 0))
    return _fused_rows(
        name, a, b, lambda a_ref, b_ref: _dot(a_ref[...], b_ref[...], _NN), pl.BlockSpec((tm, k), lambda i: (i, 0)), tm,
        [res, g], [row, pl.BlockSpec((1, d), lambda i: (0, 0))],
        [jax.ShapeDtypeStruct((m, d), f32), jax.ShapeDtypeStruct((m, d), _ACT)], [row, row], epilogue)


def _dproj_rms_bwd(name, a, b, x, g, dres, storage_copy=True):
    m, d = x.shape
    if a.ndim == 3:
        nh, _, kh = a.shape
        tm = min(256, m)
        a_spec = pl.BlockSpec((nh, tm, kh), lambda i: (0, i, 0))

        def product(a_ref, b_ref):
            p = _dot(a_ref[0], b_ref[:, 0:kh], _NT)
            for h in range(1, nh):
                p = p + _dot(a_ref[h], b_ref[:, h * kh:(h + 1) * kh], _NT)
            return p
    else:
        tm = _row_tile(m)
        a_spec = pl.BlockSpec((tm, a.shape[1]), lambda i: (i, 0))

        def product(a_ref, b_ref):
            return _dot(a_ref[...], b_ref[...], _NT)

    def epilogue(dhv, ins, outs):
        x_ref, g_ref, dres_ref = ins
        dg_ref = outs[-1]

        @pl.when(pl.program_id(0) == 0)
        def _():
            dg_ref[...] = jnp.zeros_like(dg_ref)

        xv = x_ref[...]
        r = lax.rsqrt(jnp.mean(xv * xv, axis=-1, keepdims=True) + EPS)
        xn = xv * r
        dxn = dhv * g_ref[...]
        dx = r * (dxn - xn * jnp.mean(dxn * xn, axis=-1, keepdims=True)) + dres_ref[...]
        outs[0][...] = dx
        if storage_copy:
            outs[1][...] = dx.astype(outs[1].dtype)
        dg_ref[...] += jnp.sum(dhv * xn, axis=0, keepdims=True)

    row = pl.BlockSpec((tm, d), lambda i: (i, 0))
    vec = pl.BlockSpec((1, d), lambda i: (0, 0))
    copies = [jax.ShapeDtypeStruct((m, d), _ACT)] if storage_copy else []
    return _fused_rows(
        name, a, b, product, a_spec, tm, [x, g, dres], [row, vec, row],
        [jax.ShapeDtypeStruct((m, d), f32)] + copies + [jax.ShapeDtypeStruct((1, d), f32)],
        [row] * (1 + len(copies)) + [vec], epilogue)


def _proj_loss_bwd(name, a, b, res, g, tgt):
    m, k = a.shape
    d = b.shape[1]
    tm = _row_tile(m)

    def epilogue(p, ins, outs):
        res_ref, g_ref, t_ref = ins
        dx_ref, dxb_ref, dg_ref, loss_ref = outs

        @pl.when(pl.program_id(0) == 0)
        def _():
            dg_ref[...] = jnp.zeros_like(dg_ref)
            loss_ref[...] = jnp.zeros_like(loss_ref)

        xv = p + res_ref[...]
        gv = g_ref[...]
        r = lax.rsqrt(jnp.mean(xv * xv, axis=-1, keepdims=True) + EPS)
        xn = xv * r
        err = xn * gv - t_ref[...]
        loss_ref[...] += 0.5 * jnp.sum(jnp.mean(err * err, axis=-1, keepdims=True), axis=0, keepdims=True)
        dout = err * (1.0 / d)
        dxn = dout * gv
        dx = r * (dxn - xn * jnp.mean(dxn * xn, axis=-1, keepdims=True))
        dx_ref[...] = dx
        dxb_ref[...] = dx.astype(dxb_ref.dtype)
        dg_ref[...] += jnp.sum(dout * xn, axis=0, keepdims=True)

    row = pl.BlockSpec((tm, d), lambda i: (i, 0))
    vec = pl.BlockSpec((1, d), lambda i: (0, 0))
    return _fused_rows(
        name, a, b, lambda a_ref, b_ref: _dot(a_ref[...], b_ref[...], _NN), pl.BlockSpec((tm, k), lambda i: (i, 0)), tm,
        [res, g, tgt], [row, vec, row],
        [jax.ShapeDtypeStruct((m, d), f32), jax.ShapeDtypeStruct((m, d), _ACT), jax.ShapeDtypeStruct((1, d), f32),
         jax.ShapeDtypeStruct((1, 1), f32)],
        [row, row, vec, pl.BlockSpec((1, 1), lambda i: (0, 0))], epilogue)


def _rms_gain_grad(name, x, dh):
    t, d = x.shape
    tm = _row_tile(t)

    def body(x_ref, dh_ref, dg_ref):
        @pl.when(pl.program_id(0) == 0)
        def _():
            dg_ref[...] = jnp.zeros_like(dg_ref)

        xv = x_ref[...]
        r = lax.rsqrt(jnp.mean(xv * xv, axis=-1, keepdims=True) + EPS)
        dg_ref[...] += jnp.sum(dh_ref[...] * (xv * r), axis=0, keepdims=True)

    row = pl.BlockSpec((tm, d), lambda i: (i, 0))
    return pl.pallas_call(
        body, name=name, grid=(t // tm,), in_specs=[row, row], out_specs=pl.BlockSpec((1, d), lambda i: (0, 0)),
        out_shape=jax.ShapeDtypeStruct((1, d), f32), compiler_params=_params(1),
    )(x, dh)


_CONV_ROWS = 256
_CHUNK = 64
_HALO = 32


def _pool_counts(pos, w):
    return jnp.minimum(pos + 1.0, float(w))


def _rows_from(win, start, rows):
    if start % 8 == 0:
        return win[start:start + rows, :]
    n = win.shape[0]
    return pltpu.roll(win, n - start % 8, axis=0)[start - start % 8:start - start % 8 + rows, :]


def _tap_rows(buf, starts, rows):
    for residue in range(8):
        group = [(k, s) for k, s in starts.items() if s % 8 == residue]
        if group:
            lo = min(s for _, s in group) - residue
            hi = max(s for _, s in group) - residue + rows + (8 if residue else 0)
            win = buf[lo:hi, :]
            if residue:
                win = pltpu.roll(win, hi - lo - residue, axis=0)
            for k, s in group:
                yield k, win[s - residue - lo:s - residue - lo + rows, :]


def _mix_fwd(u, cw, cb, lg, lb, pw, ps, seq):
    t, c3 = u.shape
    c = c3 // 3
    kw = 31
    tm = min(_CONV_ROWS, seq)
    tps = seq // tm
    gd = c // len(POOL_WINDOWS)

    def body(u_ref, uh_ref, cw_ref, cb_ref, lg_ref, lb_ref, pw_ref, ps_ref, y_ref, hc_ref, hgbuf, pbuf):
        i = pl.program_id(0)
        keep = jnp.where(i % tps == 0, 0.0, 1.0)
        um = u_ref[...].astype(f32)
        uh = uh_ref[...].astype(f32) * keep
        hgbuf[0:_HALO, :] = uh[:, 0:c] * _sigmoid(uh[:, c:2 * c])
        hgbuf[_HALO:_HALO + tm, :] = um[:, 0:c] * _sigmoid(um[:, c:2 * c])
        pbuf[0:_HALO, :] = uh[:, 2 * c:]
        pbuf[_HALO:_HALO + tm, :] = um[:, 2 * c:]
        for r0 in range(0, tm, _CHUNK):
            acc = jnp.broadcast_to(cb_ref[...], (_CHUNK, c))
            for k, rows in _tap_rows(hgbuf, {k: r0 + _HALO - (kw - 1) + k for k in range(kw)}, _CHUNK):
                acc = acc + cw_ref[k:k + 1, :] * rows
            hc_ref[r0:r0 + _CHUNK, :] = acc
            mu = jnp.mean(acc, axis=-1, keepdims=True)
            xc = acc - mu
            var = jnp.mean(xc * xc, axis=-1, keepdims=True)
            hl = xc * lax.rsqrt(var + EPS) * lg_ref[...] + lb_ref[...]
            y_ref[r0:r0 + _CHUNK, 0:c] = (hl * _sigmoid(hl)).astype(y_ref.dtype)
        pos = ((i % tps) * tm).astype(f32) + lax.broadcasted_iota(jnp.int32, (tm, 1), 0).astype(f32)
        for gi, w in enumerate(POOL_WINDOWS):
            sl = slice(gi * gd, (gi + 1) * gd)
            v = pbuf[_HALO:_HALO + tm, sl]
            s = v
            for j in range(1, w):
                s = s + pbuf[_HALO - j:_HALO - j + tm, sl]
            pooled = s / _pool_counts(pos, w) - v
            mixed = _dot(pooled.astype(_ACT), pw_ref[gi].astype(_ACT), _NN)
            y_ref[:, c + gi * gd:c + (gi + 1) * gd] = (mixed * ps_ref[:, sl]).astype(y_ref.dtype)

    hb = tm // _HALO
    full = lambda shape: pl.BlockSpec(shape, lambda i: (0,) * len(shape))
    return pl.pallas_call(
        body, name="mix_fwd", grid=(t // tm,),
        in_specs=[pl.BlockSpec((tm, c3), lambda i: (i, 0)),
                  pl.BlockSpec((_HALO, c3), lambda i: (jnp.maximum(i * hb - 1, 0), 0)),
                  full((_HALO, c)), full((1, c)), full((1, c)), full((1, c)), full((len(POOL_WINDOWS), gd, gd)), full((1, c))],
        out_specs=[pl.BlockSpec((tm, 2 * c), lambda i: (i, 0)), pl.BlockSpec((tm, c), lambda i: (i, 0))],
        out_shape=[jax.ShapeDtypeStruct((t, 2 * c), _ACT), jax.ShapeDtypeStruct((t, c), f32)],
        scratch_shapes=[pltpu.VMEM((_HALO + tm, c), f32), pltpu.VMEM((_HALO + tm, c), f32)],
        compiler_params=_params(1),
    )(u, u, cw, cb, lg, lb, pw, ps)


def _mix_bwd_norm(hc, dy, lg, lb, seq):
    t, c = hc.shape
    tm = min(_CONV_ROWS, seq)

    def body(hc_ref, dy_ref, lg_ref, lb_ref, dhc_ref, sums_ref):
        @pl.when(pl.program_id(0) == 0)
        def _():
            sums_ref[...] = jnp.zeros_like(sums_ref)

        hcv = hc_ref[...]
        mu = jnp.mean(hcv, axis=-1, keepdims=True)
        xc = hcv - mu
        rstd = lax.rsqrt(jnp.mean(xc * xc, axis=-1, keepdims=True) + EPS)
        n = xc * rstd
        hl = n * lg_ref[...] + lb_ref[...]
        sg = _sigmoid(hl)
        dhl = dy_ref[...].astype(f32) * (sg * (1.0 + hl * (1.0 - sg)))
        dn = dhl * lg_ref[...]
        dhc = rstd * (dn - jnp.mean(dn, axis=-1, keepdims=True) - n * jnp.mean(dn * n, axis=-1, keepdims=True))
        dhc_ref[...] = dhc
        sums_ref[0:1, :] += jnp.sum(dhl * n, axis=0, keepdims=True)
        sums_ref[1:2, :] += jnp.sum(dhl, axis=0, keepdims=True)
        sums_ref[2:3, :] += jnp.sum(dhc, axis=0, keepdims=True)

    row = pl.BlockSpec((tm, c), lambda i: (i, 0))
    vec = pl.BlockSpec((1, c), lambda i: (0, 0))
    return pl.pallas_call(
        body, name="mix_bwd_norm", grid=(t // tm,), in_specs=[row, row, vec, vec],
        out_specs=[row, pl.BlockSpec((8, c), lambda i: (0, 0))],
        out_shape=[jax.ShapeDtypeStruct((t, c), f32), jax.ShapeDtypeStruct((8, c), f32)],
        compiler_params=_params(1),
    )(hc, dy, lg, lb)


def _mix_bwd_taps(u, dhc, dy, cw, pw, ps, seq):
    t, c3 = u.shape
    c = c3 // 3
    kw = 31
    tm = min(_CONV_ROWS, seq)
    tps = seq // tm
    ng = len(POOL_WINDOWS)
    gd = c // ng
    nh = 16

    def body(u_ref, uh_ref, dhc_ref, dhcn_ref, dy_ref, dyn_ref, cw_ref, pw_ref, ps_ref,
             du_ref, dcw_ref, dps_ref, dpw_ref, hgbuf, dcbuf, pbuf, dpbuf):
        i = pl.program_id(0)
        keep_prev = jnp.where(i % tps == 0, 0.0, 1.0)
        keep_next = jnp.where(i % tps == tps - 1, 0.0, 1.0)

        @pl.when(i == 0)
        def _():
            dcw_ref[...] = jnp.zeros_like(dcw_ref)
            dps_ref[...] = jnp.zeros_like(dps_ref)
            dpw_ref[...] = jnp.zeros_like(dpw_ref)

        uh = uh_ref[...].astype(f32) * keep_prev
        hgbuf[0:_HALO, :] = uh[:, 0:c] * _sigmoid(uh[:, c:2 * c])
        pbuf[0:_HALO, :] = uh[:, 2 * c:]
        um = u_ref[...].astype(f32)
        hgbuf[_HALO:_HALO + tm, :] = um[:, 0:c] * _sigmoid(um[:, c:2 * c])
        pbuf[_HALO:_HALO + tm, :] = um[:, 2 * c:]
        dcbuf[0:tm, :] = dhc_ref[...]
        dcbuf[tm:tm + _HALO, :] = dhcn_ref[...] * keep_next
        tap_sums = [None] * kw
        for r0 in range(0, tm, _CHUNK):
            dh = dcbuf[r0:r0 + _CHUNK, :]
            acc = jnp.zeros((_CHUNK, c), f32)
            for k, rows in _tap_rows(hgbuf, {k: r0 + _HALO - (kw - 1) + k for k in range(kw)}, _CHUNK):
                part = (dh * rows).reshape(_CHUNK // 8, 8, c).sum(axis=0)
                tap_sums[k] = part if tap_sums[k] is None else tap_sums[k] + part
            for k, rows in _tap_rows(dcbuf, {k: r0 + (kw - 1) - k for k in range(kw)}, _CHUNK):
                acc = acc + cw_ref[k:k + 1, :] * rows
            val = u_ref[r0:r0 + _CHUNK, 0:c].astype(f32)
            sg = _sigmoid(u_ref[r0:r0 + _CHUNK, c:2 * c].astype(f32))
            du_ref[r0:r0 + _CHUNK, 0:c] = (acc * sg).astype(du_ref.dtype)
            du_ref[r0:r0 + _CHUNK, c:2 * c] = (acc * val * sg * (1.0 - sg)).astype(du_ref.dtype)
        for k in range(kw):
            dcw_ref[k:k + 1, :] += jnp.sum(tap_sums[k], axis=0, keepdims=True)
        base = ((i % tps) * tm).astype(f32)
        pos = base + lax.broadcasted_iota(jnp.int32, (tm, 1), 0).astype(f32)
        pos_next = base + float(tm) + lax.broadcasted_iota(jnp.int32, (nh, 1), 0).astype(f32)
        for gi, w in enumerate(POOL_WINDOWS):
            sl = slice(gi * gd, (gi + 1) * gd)
            v = pbuf[_HALO:_HALO + tm, sl]
            s = v
            for j in range(1, w):
                s = s + pbuf[_HALO - j:_HALO - j + tm, sl]
            cnt = _pool_counts(pos, w)
            pooled = (s / cnt - v).astype(_ACT)
            pwg = pw_ref[gi].astype(_ACT)
            mixed = _dot(pooled, pwg, _NN)
            dyp = dy_ref[:, sl].astype(f32)
            dps_ref[0:1, sl] += jnp.sum(dyp * mixed, axis=0, keepdims=True)
            dmix = (dyp * ps_ref[:, sl]).astype(_ACT)
            dpw_ref[gi] += _dot(pooled, dmix, _TN)
            dmix_next = (dyn_ref[:, sl].astype(f32) * ps_ref[:, sl] * keep_next).astype(_ACT)
            dpool = _dot(dmix, pwg, _NT)
            dpbuf[0:tm, sl] = dpool / cnt
            dpbuf[tm:tm + nh, sl] = _dot(dmix_next, pwg, _NT) / _pool_counts(pos_next, w)
            acc = -dpool
            for j in range(w):
                acc = acc + dpbuf[j:j + tm, sl]
            du_ref[:, 2 * c + gi * gd:2 * c + (gi + 1) * gd] = acc.astype(du_ref.dtype)

    hb = tm // _HALO
    n_halo = t // _HALO
    n_nh = t // nh
    full = lambda shape: pl.BlockSpec(shape, lambda i: (0,) * len(shape))
    return pl.pallas_call(
        body, name="mix_bwd_taps", grid=(t // tm,),
        in_specs=[pl.BlockSpec((tm, c3), lambda i: (i, 0)),
                  pl.BlockSpec((_HALO, c3), lambda i: (jnp.maximum(i * hb - 1, 0), 0)),
                  pl.BlockSpec((tm, c), lambda i: (i, 0)),
                  pl.BlockSpec((_HALO, c), lambda i: (jnp.minimum((i + 1) * hb, n_halo - 1), 0)),
                  pl.BlockSpec((tm, c), lambda i: (i, 1)),
                  pl.BlockSpec((nh, c), lambda i: (jnp.minimum((i + 1) * (tm // nh), n_nh - 1), 1)),
                  full((_HALO, c)), full((ng, gd, gd)), full((1, c))],
        out_specs=[pl.BlockSpec((tm, c3), lambda i: (i, 0)), full((_HALO, c)), full((8, c)), full((ng, gd, gd))],
        out_shape=[jax.ShapeDtypeStruct((t, c3), _ACT), jax.ShapeDtypeStruct((_HALO, c), f32),
                   jax.ShapeDtypeStruct((8, c), f32), jax.ShapeDtypeStruct((ng, gd, gd), f32)],
        scratch_shapes=[pltpu.VMEM((_HALO + tm, c), f32), pltpu.VMEM((tm + _HALO, c), f32),
                        pltpu.VMEM((_HALO + tm, c), f32), pltpu.VMEM((tm + nh, c), f32)],
        compiler_params=_params(1),
    )(u, u, dhc, dhc, dy, dy, cw, pw, ps)


def _attn_fwd(q, kv, n_seq, seq, n_mem):
    t, d = q.shape
    dh = d // XATTN_HEADS
    tq = min(512, seq)
    nq = seq // tq
    scale = dh ** -0.5

    def body(q_ref, kv_ref, o_ref):
        for h in range(XATTN_HEADS):
            cols = slice(h * dh, (h + 1) * dh)
            s = _dot(q_ref[:, cols], kv_ref[:, cols], _NT) * scale
            e = jnp.exp(s - jnp.max(s, axis=-1, keepdims=True))
            p = e / jnp.sum(e, axis=-1, keepdims=True)
            o_ref[:, cols] = _dot(p.astype(_ACT), kv_ref[:, d + h * dh:d + (h + 1) * dh], _NN).astype(o_ref.dtype)

    qs = pl.BlockSpec((tq, d), lambda b, i: (b * nq + i, 0))
    return pl.pallas_call(
        body, name="attn_fwd", grid=(n_seq, nq), in_specs=[qs, pl.BlockSpec((n_mem, 2 * d), lambda b, i: (b, 0))],
        out_specs=qs, out_shape=jax.ShapeDtypeStruct((t, d), _ACT), compiler_params=_params(2),
    )(q, kv)


def _attn_bwd(q, kv, do, n_seq, seq, n_mem):
    t, d = q.shape
    dh = d // XATTN_HEADS
    tq = min(512, seq)
    nq = seq // tq
    scale = dh ** -0.5

    def body(q_ref, kv_ref, do_ref, dq_ref, dkv_ref, acc):
        i = pl.program_id(1)

        @pl.when(i == 0)
        def _():
            acc[...] = jnp.zeros_like(acc)

        for h in range(XATTN_HEADS):
            cols = slice(h * dh, (h + 1) * dh)
            vcols = slice(d + h * dh, d + (h + 1) * dh)
            qv = q_ref[:, cols]
            kh = kv_ref[:, cols]
            dov = do_ref[:, cols]
            s = _dot(qv, kh, _NT) * scale
            e = jnp.exp(s - jnp.max(s, axis=-1, keepdims=True))
            p = e / jnp.sum(e, axis=-1, keepdims=True)
            dp = _dot(dov, kv_ref[:, vcols], _NT)
            ds = (p * (dp - jnp.sum(dp * p, axis=-1, keepdims=True)) * scale).astype(_ACT)
            dq_ref[:, cols] = _dot(ds, kh, _NN).astype(dq_ref.dtype)
            acc[:, cols] += _dot(ds, qv, _TN)
            acc[:, vcols] += _dot(p.astype(_ACT), dov, _TN)

        @pl.when(i == nq - 1)
        def _():
            dkv_ref[...] = acc[...].astype(dkv_ref.dtype)

    qs = pl.BlockSpec((tq, d), lambda b, i: (b * nq + i, 0))
    ms = pl.BlockSpec((n_mem, 2 * d), lambda b, i: (b, 0))
    return pl.pallas_call(
        body, name="attn_bwd", grid=(n_seq, nq), in_specs=[qs, ms, qs], out_specs=[qs, ms],
        out_shape=[jax.ShapeDtypeStruct((t, d), _ACT), jax.ShapeDtypeStruct((n_seq * n_mem, 2 * d), _ACT)],
        scratch_shapes=[pltpu.VMEM((n_mem, 2 * d), f32)], compiler_params=_params(2),
    )(q, kv, do)


_FFN_ROWS = 1024
_FFN_COLS = 256
_FFN_HALO = 16


def _window(buf, g, start, rows):
    return buf[g, pl.ds(start, rows + 8), :]


def _taps3(win, rows):
    return [_rows_from(win, 6 + k, rows) for k in range(3)]


def _conv3(b_ref, w_ref, taps):
    acc = b_ref[...] + w_ref[0:1, :] * taps[0]
    for k in (1, 2):
        acc = acc + w_ref[k:k + 1, :] * taps[k]
    return acc


def _ffn_gate_fwd(up, fw, fb, seq):
    _, t, f = up.shape
    tm = min(_FFN_ROWS, seq)
    tps = seq // tm
    tc = _FFN_COLS
    nc = f // tc
    hl = _FFN_HALO

    def body(up_ref, uph_ref, wg_ref, wv_ref, bg_ref, bv_ref, a_ref, buf):
        i = pl.program_id(1)
        keep = jnp.where(i % tps == 0, 0.0, 1.0)
        buf[:, 0:hl, :] = uph_ref[...].astype(f32) * keep
        buf[:, hl:hl + tm, :] = up_ref[...].astype(f32)

        def chunk(ci, carry):
            r0 = pl.multiple_of(ci * _CHUNK, _CHUNK)
            conv = []
            for g, (w_ref, b_ref) in enumerate(((wg_ref, bg_ref), (wv_ref, bv_ref))):
                conv.append(_conv3(b_ref, w_ref, _taps3(_window(buf, g, r0 + hl - 8, _CHUNK), _CHUNK)))
            gate, val = conv
            a_ref[pl.ds(r0, _CHUNK), :] = (gate * _sigmoid(gate) * val).astype(a_ref.dtype)
            return carry

        lax.fori_loop(0, tm // _CHUNK, chunk, 0)

    hb = tm // hl
    return pl.pallas_call(
        body, name="ffn_gate_fwd", grid=(nc, t // tm),
        in_specs=[pl.BlockSpec((2, tm, tc), lambda j, i: (0, i, j)),
                  pl.BlockSpec((2, hl, tc), lambda j, i: (0, jnp.maximum(i * hb - 1, 0), j)),
                  pl.BlockSpec((8, tc), lambda j, i: (0, j)), pl.BlockSpec((8, tc), lambda j, i: (0, nc + j)),
                  pl.BlockSpec((1, tc), lambda j, i: (0, j)), pl.BlockSpec((1, tc), lambda j, i: (0, nc + j))],
        out_specs=pl.BlockSpec((tm, tc), lambda j, i: (i, j)),
        out_shape=jax.ShapeDtypeStruct((t, f), _ACT),
        scratch_shapes=[pltpu.VMEM((2, hl + tm, tc), f32)], compiler_params=_params(2),
    )(up, up, fw, fw, fb, fb)


def _ffn_gate_bwd(up, da, fw, fb, seq):
    _, t, f = up.shape
    tm = min(_FFN_ROWS, seq)
    tps = seq // tm
    tc = _FFN_COLS
    nc = f // tc
    hl = _FFN_HALO

    def body(up_ref, uph_ref, upn_ref, da_ref, dan_ref, wg_ref, wv_ref, bg_ref, bv_ref,
             dup_ref, sg_ref, sv_ref, ubuf, dbuf, sums):
        i = pl.program_id(1)
        keep_prev = jnp.where(i % tps == 0, 0.0, 1.0)
        keep_next = jnp.where(i % tps == tps - 1, 0.0, 1.0)

        @pl.when(i == 0)
        def _():
            sg_ref[...] = jnp.zeros_like(sg_ref)
            sv_ref[...] = jnp.zeros_like(sv_ref)

        sums[...] = jnp.zeros_like(sums)
        ubuf[:, 0:hl, :] = uph_ref[...].astype(f32) * keep_prev
        ubuf[:, hl:hl + tm, :] = up_ref[...].astype(f32)
        ubuf[:, hl + tm:hl + tm + hl, :] = upn_ref[...].astype(f32) * keep_next
        w_refs = (wg_ref, wv_ref)
        b_refs = (bg_ref, bv_ref)

        def grads(r0, rows, dav, count):
            taps = [_taps3(_window(ubuf, g, r0 + hl - 8, rows), rows) for g in range(2)]
            gate, val = [_conv3(b_refs[g], w_refs[g], taps[g]) for g in range(2)]
            sg = _sigmoid(gate)
            douts = (dav * val * (sg * (1.0 + gate * (1.0 - sg))), dav * (gate * sg))
            for g in range(2):
                dbuf[g, pl.ds(r0, rows), :] = douts[g]
                if count:
                    sums[g, 0] += douts[g].reshape(rows // 8, 8, tc).sum(axis=0)
                    for k in range(3):
                        sums[g, 1 + k] += (douts[g] * taps[g][k]).reshape(rows // 8, 8, tc).sum(axis=0)

        def first(ci, carry):
            r0 = pl.multiple_of(ci * _CHUNK, _CHUNK)
            grads(r0, _CHUNK, da_ref[pl.ds(r0, _CHUNK), :].astype(f32), True)
            return carry

        lax.fori_loop(0, tm // _CHUNK, first, 0)
        grads(tm, hl, dan_ref[...].astype(f32) * keep_next, False)

        def second(ci, carry):
            r0 = pl.multiple_of(ci * _CHUNK, _CHUNK)
            for g in range(2):
                win = _window(dbuf, g, r0, _CHUNK)
                acc = jnp.zeros((_CHUNK, tc), f32)
                for k in range(3):
                    acc = acc + w_refs[g][k:k + 1, :] * _rows_from(win, 2 - k, _CHUNK)
                dup_ref[g, pl.ds(r0, _CHUNK), :] = acc.astype(dup_ref.dtype)
            return carry

        lax.fori_loop(0, tm // _CHUNK, second, 0)
        for g, s_ref in enumerate((sg_ref, sv_ref)):
            for r in range(4):
                s_ref[r:r + 1, :] += jnp.sum(sums[g, r], axis=0, keepdims=True)

    hb = tm // hl
    n_halo = t // hl
    return pl.pallas_call(
        body, name="ffn_gate_bwd", grid=(nc, t // tm),
        in_specs=[pl.BlockSpec((2, tm, tc), lambda j, i: (0, i, j)),
                  pl.BlockSpec((2, hl, tc), lambda j, i: (0, jnp.maximum(i * hb - 1, 0), j)),
                  pl.BlockSpec((2, hl, tc), lambda j, i: (0, jnp.minimum((i + 1) * hb, n_halo - 1), j)),
                  pl.BlockSpec((tm, tc), lambda j, i: (i, j)),
                  pl.BlockSpec((hl, tc), lambda j, i: (jnp.minimum((i + 1) * hb, n_halo - 1), j)),
                  pl.BlockSpec((8, tc), lambda j, i: (0, j)), pl.BlockSpec((8, tc), lambda j, i: (0, nc + j)),
                  pl.BlockSpec((1, tc), lambda j, i: (0, j)), pl.BlockSpec((1, tc), lambda j, i: (0, nc + j))],
        out_specs=[pl.BlockSpec((2, tm, tc), lambda j, i: (0, i, j)),
                   pl.BlockSpec((8, tc), lambda j, i: (0, j)), pl.BlockSpec((8, tc), lambda j, i: (0, j))],
        out_shape=[jax.ShapeDtypeStruct((2, t, f), _ACT), jax.ShapeDtypeStruct((8, f), f32), jax.ShapeDtypeStruct((8, f), f32)],
        scratch_shapes=[pltpu.VMEM((2, hl + tm + hl, tc), f32), pltpu.VMEM((2, tm + hl, tc), f32),
                        pltpu.VMEM((2, 4, 8, tc), f32)],
        compiler_params=_params(2),
    )(up, up, up, da, da, fw, fw, fb, fb)


def _adamw_math(w, g, m, v):
    m = ADAM_B1 * m + (1.0 - ADAM_B1) * g
    v = ADAM_B2 * v + (1.0 - ADAM_B2) * (g * g)
    m_hat = m / (1.0 - ADAM_B1 ** ADAM_STEP)
    v_hat = v / (1.0 - ADAM_B2 ** ADAM_STEP)
    delta = -ADAM_LR * (m_hat / (jnp.sqrt(v_hat) + ADAM_EPS) + ADAM_WD * w)
    return delta, m, v


def _adamw_shard(name, w, g, m, v):
    _, r, c = w.shape
    tr = next((cand for cand in (256, 176, 128, 64, 32, 16, 8) if r % cand == 0), r)

    def body(w_ref, g_ref, m_ref, v_ref, d_ref, mo_ref, vo_ref):
        d, mn, vn = _adamw_math(w_ref[...], g_ref[...], m_ref[...], v_ref[...])
        d_ref[...] = d
        mo_ref[...] = mn
        vo_ref[...] = vn

    s3 = pl.BlockSpec((None, tr, c), lambda i: (0, i, 0))
    s2 = pl.BlockSpec((tr, c), lambda i: (i, 0))
    shp = jax.ShapeDtypeStruct(w.shape, f32)
    return pl.pallas_call(
        body, name=name, grid=(r // tr,), in_specs=[s3, s2, s3, s3], out_specs=[s3, s3, s3], out_shape=[shp, shp, shp],
        compiler_params=_params(1),
    )(w, g, m, v)


def _adamw_small(quads):
    n = len(quads)

    def body(*refs):
        ins, outs = refs[:4 * n], refs[4 * n:]
        for p in range(n):
            w_ref, g_ref, m_ref, v_ref = ins[4 * p:4 * p + 4]
            d, mn, vn = _adamw_math(w_ref[...], g_ref[...], m_ref[...], v_ref[...])
            outs[3 * p][...] = d
            outs[3 * p + 1][...] = mn
            outs[3 * p + 2][...] = vn

    flat = [a for q in quads for a in q]
    shapes = [jax.ShapeDtypeStruct(q[0].shape, f32) for q in quads for _ in range(3)]
    outs = pl.pallas_call(
        body, name="adamw_small", in_specs=[_VMEM] * (4 * n), out_specs=[_VMEM] * (3 * n), out_shape=shapes,
        compiler_params=pltpu.CompilerParams(vmem_limit_bytes=_VMEM_LIMIT_BYTES),
    )(*flat)
    return [tuple(outs[3 * p:3 * p + 3]) for p in range(n)]


def _sum_partials(name, place, grads, got):
    _, r, c = grads.shape
    steps = 4 if r % 64 == 0 else 1
    tr = r // steps

    def body(place_ref, own_ref, got_ref, f_ref):
        s = own_ref[...].astype(f32)
        for k in range(got.shape[0]):
            s = s + got_ref[k].astype(f32)
        f_ref[...] = s

    grid_spec = pltpu.PrefetchScalarGridSpec(
        num_scalar_prefetch=1, grid=(steps,),
        in_specs=[pl.BlockSpec((None, tr, c), lambda i, p: (2 * p[0] + p[1], i, 0)),
                  pl.BlockSpec((got.shape[0], tr, c), lambda i, p: (0, i, 0))],
        out_specs=pl.BlockSpec((None, tr, c), lambda i, p: (p[1], i, 0)))
    return pl.pallas_call(body, name=name, grid_spec=grid_spec, out_shape=jax.ShapeDtypeStruct((2, r, c), f32),
                          compiler_params=_params(1))(place, grads, got)


def _place():
    return lax.axis_index("x"), lax.axis_index("y"), lax.axis_index("c")


def _other_chips(x, y):
    return [(1 - x, y), (x, 1 - y), (1 - x, 1 - y)]


def _remote(src, dst, send_sem, recv_sem, to):
    return pltpu.make_async_remote_copy(src_ref=src, dst_ref=dst, send_sem=send_sem, recv_sem=recv_sem,
                                        device_id=to, device_id_type=_MESH)


def _place_shards(place, shards, col_sharded):
    n = len(shards)
    steps = 4

    def body(place_ref, *refs):
        for src, dst in zip(refs[:n], refs[n:]):
            dst[...] = src[...].astype(dst.dtype)

    in_specs, out_specs, out_shape = [], [], []
    for w, col in zip(shards, col_sharded):
        r, cs = w.shape
        tr = r // steps
        in_specs.append(pl.BlockSpec((tr, cs), lambda i, p: (i, 0)))
        if col:
            out_specs.append(pl.BlockSpec((tr, cs), lambda i, p: (i, p[0])))
            out_shape.append(jax.ShapeDtypeStruct((r, 4 * cs), _ACT))
        else:
            out_specs.append(pl.BlockSpec((tr, cs), lambda i, p: (p[0] * steps + i, 0)))
            out_shape.append(jax.ShapeDtypeStruct((4 * r, cs), _ACT))
    grid_spec = pltpu.PrefetchScalarGridSpec(num_scalar_prefetch=1, grid=(steps,), in_specs=in_specs, out_specs=out_specs)
    return pl.pallas_call(body, name="place_shards", grid_spec=grid_spec, out_shape=out_shape,
                          compiler_params=_params(1))(place, *shards)


def _shard_of(ref, col_sharded, s):
    rows, cols = ref.shape
    if col_sharded:
        return ref.at[:, pl.ds(s * (cols // 4), cols // 4)]
    return ref.at[pl.ds(s * (rows // 4), rows // 4), :]


def _part_of(ref, col_sharded, whole, s, h):
    if whole:
        return _shard_of(ref, col_sharded, s)
    rows, cols = ref.shape
    if col_sharded:
        return ref.at[pl.ds(h * (rows // 2), rows // 2), pl.ds(s * (cols // 4), cols // 4)]
    return ref.at[pl.ds((2 * s + h) * (rows // 8), rows // 8), :]


def _allgather_start(bufs, col_sharded, whole, groups):
    n = len(bufs)
    ng = len(groups)

    def body(*refs):
        out = refs[n:2 * n]
        sems = refs[2 * n:]
        x, y, c = _place()
        for g, members in enumerate(groups):
            for i, w in enumerate(members):
                mine = _part_of(out[w], col_sharded[w], whole[w], 2 * x + y, c)
                for j, chip in enumerate(_other_chips(x, y)):
                    _remote(mine, mine, sems[2 * g].at[3 * i + j], sems[2 * g + 1].at[3 * i + j], (*chip, c)).start()

    sem_shapes = [pltpu.SemaphoreType.DMA((3 * len(m),)) for m in groups for _ in range(2)]
    outs = pl.pallas_call(
        body, name="allgather_start", in_specs=[_HBM] * n, out_specs=[_HBM] * n + [_SEM] * (2 * ng),
        out_shape=[pltpu.HBM(b.shape, b.dtype) for b in bufs] + sem_shapes,
        input_output_aliases={i: i for i in range(n)},
        compiler_params=pltpu.CompilerParams(has_side_effects=_EFFECT),
    )(*[pltpu.with_memory_space_constraint(b, pltpu.HBM) for b in bufs])
    return list(outs[:n]), [(outs[n + 2 * g], outs[n + 2 * g + 1]) for g in range(ng)]


def _allgather_relay(name, bufs, col_sharded, whole, sems, after):
    n = len(bufs)

    def body(*refs):
        buf = refs[:n]
        send, recv = refs[n], refs[n + 1]
        out = refs[n + 3:2 * n + 3]
        to_sibling, from_sibling = refs[2 * n + 3:]
        x, y, c = _place()
        for i in range(n):
            mine = _part_of(buf[i], col_sharded[i], whole[i], 2 * x + y, c)
            for j, chip in enumerate(_other_chips(x, y)):
                landed = _part_of(buf[i], col_sharded[i], whole[i], 2 * chip[0] + chip[1], c)
                cp = _remote(mine, landed, send.at[3 * i + j], recv.at[3 * i + j], (*chip, c))
                cp.wait_send()
                cp.wait_recv()
        for i in range(n):
            if not whole[i]:
                for j, chip in enumerate(_other_chips(x, y)):
                    landed = _part_of(out[i], col_sharded[i], False, 2 * chip[0] + chip[1], c)
                    _remote(landed, landed, to_sibling.at[3 * i + j], from_sibling.at[3 * i + j], (x, y, 1 - c)).start()

    outs = pl.pallas_call(
        body, name=name, in_specs=[_HBM] * n + [_SEM, _SEM, _ANY], out_specs=[_HBM] * n + [_SEM, _SEM],
        out_shape=[pltpu.HBM(b.shape, b.dtype) for b in bufs] + [pltpu.SemaphoreType.DMA((3 * n,))] * 2,
        input_output_aliases={i: i for i in range(n)},
        compiler_params=pltpu.CompilerParams(has_side_effects=_EFFECT),
    )(*bufs, *sems, after)
    return list(outs[:n]), (outs[n], outs[n + 1])


def _allgather_wait(name, bufs, col_sharded, whole, sems, after):
    n = len(bufs)

    def body(*refs):
        buf = refs[:n]
        to_sibling, from_sibling = refs[n], refs[n + 1]
        x, y, c = _place()
        for i in range(n):
            if not whole[i]:
                for j, chip in enumerate(_other_chips(x, y)):
                    sent = _part_of(buf[i], col_sharded[i], False, 2 * chip[0] + chip[1], c)
                    landed = _part_of(buf[i], col_sharded[i], False, 2 * chip[0] + chip[1], 1 - c)
                    cp = _remote(sent, landed, to_sibling.at[3 * i + j], from_sibling.at[3 * i + j], (x, y, 1 - c))
                    cp.wait_send()
                    cp.wait_recv()

    return pl.pallas_call(
        body, name=name, in_specs=[_HBM] * n + [_SEM, _SEM, _ANY], out_specs=[_HBM] * n,
        out_shape=[pltpu.HBM(b.shape, b.dtype) for b in bufs],
        input_output_aliases={i: i for i in range(n)},
        compiler_params=pltpu.CompilerParams(has_side_effects=_EFFECT),
    )(*bufs, *sems, after)


def _other_devices(x, y, c):
    flips = [(bx, by, bc) for bx in (0, 1) for by in (0, 1) for bc in (0, 1)][1:]
    return [(1 - x if bx else x, 1 - y if by else y, 1 - c if bc else c) for bx, by, bc in flips]


def _grad_exchange_start(name, grads):
    nw = len(grads)
    lands = [lax.empty((7,) + g.shape[1:], g.dtype) for g in grads]

    def body(*refs):
        src = refs[2 * nw:3 * nw]
        got = refs[3 * nw:4 * nw]
        send, recv, token = refs[4 * nw:]
        x, y, c = _place()
        for w in range(nw):
            for k, (px, py, pc) in enumerate(_other_devices(x, y, c)):
                _remote(src[w].at[4 * px + 2 * py + pc], got[w].at[k], send.at[7 * w + k], recv.at[7 * w + k], (px, py, pc)).start()
        token[...] = jnp.zeros_like(token)

    outs = pl.pallas_call(
        body, name=name, in_specs=[_HBM] * (2 * nw), out_specs=[_HBM] * (2 * nw) + [_SEM, _SEM, _VMEM],
        out_shape=[pltpu.HBM(a.shape, a.dtype) for a in list(grads) + lands]
        + [pltpu.SemaphoreType.DMA((7 * nw,)), pltpu.SemaphoreType.DMA((7 * nw,)), jax.ShapeDtypeStruct((8, 128), f32)],
        input_output_aliases={i: i for i in range(2 * nw)},
        compiler_params=pltpu.CompilerParams(has_side_effects=_EFFECT),
    )(*[pltpu.with_memory_space_constraint(a, pltpu.HBM) for a in list(grads) + lands])
    return list(outs[:nw]), list(outs[nw:2 * nw]), (outs[2 * nw], outs[2 * nw + 1]), outs[2 * nw + 2]


def _grad_exchange_wait(name, grads, got, sems, after):
    nw = len(grads)

    def body(*refs):
        src = refs[:nw]
        land = refs[nw:2 * nw]
        send, recv = refs[2 * nw], refs[2 * nw + 1]
        x, y, c = _place()
        for w in range(nw):
            for k, (px, py, pc) in enumerate(_other_devices(x, y, c)):
                cp = _remote(src[w].at[4 * px + 2 * py + pc], land[w].at[k], send.at[7 * w + k], recv.at[7 * w + k], (px, py, pc))
                cp.wait_send()
                cp.wait_recv()

    outs = pl.pallas_call(
        body, name=name, in_specs=[_HBM] * (2 * nw) + [_SEM, _SEM, _ANY], out_specs=[_HBM] * (2 * nw),
        out_shape=[pltpu.HBM(a.shape, a.dtype) for a in list(grads) + list(got)],
        input_output_aliases={i: i for i in range(2 * nw)},
        compiler_params=pltpu.CompilerParams(has_side_effects=_EFFECT),
    )(*grads, *got, *sems, after)
    return list(outs[:nw]), list(outs[nw:])


def _swap_halves(finals):
    nw = len(finals)

    def body(*refs):
        buf = refs[nw:2 * nw]
        send_sem, recv_sem = refs[2 * nw:]
        x, y, c = _place()
        sends = []
        for w in range(nw):
            rc = _remote(buf[w].at[c], buf[w].at[c], send_sem.at[w], recv_sem.at[w], (x, y, 1 - c))
            rc.start()
            sends.append(rc)
        for w in range(nw):
            _remote(buf[w].at[1 - c], buf[w].at[1 - c], send_sem.at[w], recv_sem.at[w], (x, y, c)).wait_recv()
        for rc in sends:
            rc.wait_send()

    return pl.pallas_call(
        body, name="rs_swap_halves", in_specs=[_ANY] * nw, out_specs=[_ANY] * nw,
        out_shape=[jax.ShapeDtypeStruct(g.shape, g.dtype) for g in finals],
        input_output_aliases={i: i for i in range(nw)},
        scratch_shapes=[pltpu.SemaphoreType.DMA((nw,)), pltpu.SemaphoreType.DMA((nw,))],
    )(*finals)


def _half_slices(shape, h):
    rows, cols = shape
    if cols % 256 == 0:
        return (slice(None), slice(h * (cols // 2), (h + 1) * (cols // 2)))
    return (slice(h * (rows // 2), (h + 1) * (rows // 2)), slice(None))


def _allreduce_small(parts):
    n = len(parts)

    def body(*refs):
        src = refs[:n]
        out = refs[n:2 * n]
        sib = refs[2 * n:3 * n]
        chip_sum = refs[3 * n:4 * n]
        slots = refs[4 * n:5 * n]
        pair_send, pair_recv, ici_send, ici_recv, swap_send, swap_recv = refs[5 * n:]
        x, y, c = _place()
        me_chip = 2 * x + y
        chips = _other_chips(x, y)
        pairs = [_remote(src[a], sib[a], pair_send.at[a], pair_recv.at[a], (x, y, 1 - c)) for a in range(n)]
        for rc in pairs:
            rc.start()
        for a in range(n):
            pairs[a].wait_recv()
            chip_sum[a][...] = src[a][...] + sib[a][...]
        for h in (0, 1):
            @pl.when(c == h)
            def _():
                sends = []
                for a in range(n):
                    idx = _half_slices(parts[a].shape, h)
                    for j, chip in enumerate(chips):
                        rc = _remote(chip_sum[a].at[idx], slots[a].at[me_chip].at[idx], ici_send.at[3 * a + j], ici_recv.at[3 * a + j], (*chip, h))
                        rc.start()
                        sends.append(rc)
                    slots[a][(me_chip,) + idx] = chip_sum[a][idx]
                for a in range(n):
                    idx = _half_slices(parts[a].shape, h)
                    for j, chip in enumerate(chips):
                        landed = slots[a].at[2 * chip[0] + chip[1]].at[idx]
                        _remote(landed, landed, ici_send.at[3 * a + j], ici_recv.at[3 * a + j], (x, y, c)).wait_recv()
                    total = slots[a][(0,) + idx]
                    for s in range(1, 4):
                        total = total + slots[a][(s,) + idx]
                    out[a][idx] = total
                    rc = _remote(out[a].at[idx], out[a].at[idx], swap_send.at[a], swap_recv.at[a], (x, y, 1 - h))
                    rc.start()
                    sends.append(rc)
                for a in range(n):
                    other = out[a].at[_half_slices(parts[a].shape, 1 - h)]
                    _remote(other, other, swap_send.at[a], swap_recv.at[a], (x, y, c)).wait_recv()
                for rc in sends:
                    rc.wait_send()
        for rc in pairs:
            rc.wait_send()

    return pl.pallas_call(
        body, name="allreduce_small", in_specs=[_VMEM] * n, out_specs=[_VMEM] * n,
        out_shape=[jax.ShapeDtypeStruct(p.shape, f32) for p in parts],
        scratch_shapes=[pltpu.VMEM(p.shape, f32) for p in parts] * 2 + [pltpu.VMEM((4,) + p.shape, f32) for p in parts]
        + [pltpu.SemaphoreType.DMA((n,)), pltpu.SemaphoreType.DMA((n,)), pltpu.SemaphoreType.DMA((3 * n,)),
           pltpu.SemaphoreType.DMA((3 * n,)), pltpu.SemaphoreType.DMA((n,)), pltpu.SemaphoreType.DMA((n,))],
        compiler_params=pltpu.CompilerParams(vmem_limit_bytes=_VMEM_LIMIT_BYTES),
    )(*parts)


def _local_step(x, mem, tgt, g_mix, g_xattn, g_mem, g_ffn, g_final, cb, lg, lb, pw, ps, fb, relay, weights, reduce, n_seq, seq, n_mem):
    t, d = x.shape
    f = fb.shape[1] // 2
    c = cb.shape[1]
    h1 = _rms_fwd("norm_mix", x, g_mix)
    relay(0, h1)
    w_in, cw, fw = weights(0, h1)
    u = _mm_nn("proj_in", h1, w_in, _ACT, w_in.shape[1])
    y, hc = _mix_fwd(u, cw, cb, lg, lb, pw, ps, seq)
    relay(1, y)
    w_out, w_q, w_kv, w_o = weights(1, y)
    x1, h2 = _proj_residual_norm("proj_out", y, w_out, x, g_xattn)
    q = _mm_nn("proj_q", h2, w_q, _ACT, d)
    mem_n = _rms_fwd("norm_mem", mem, g_mem)
    kv = _mm_nn("proj_kv", mem_n, w_kv, _ACT, 2 * d)
    o = _attn_fwd(q, kv, n_seq, seq, n_mem)
    relay(2, o)
    x2, h3 = _proj_residual_norm("proj_o", o, w_o, x1, g_ffn)
    w_up, w_down = weights(2, h3)
    up = _mm_nn("proj_up", h3, w_up, _ACT, f, split_out=True)
    a = _ffn_gate_fwd(up, fw, fb, seq)
    dx3, dx3b, dg_final, loss = _proj_loss_bwd("proj_down", a, w_down, x2, g_final, tgt)
    da = _mm_nt("d_act", dx3b, w_down, _ACT)
    gw_down = _mm_tn_rows("dw_down", a, dx3b, f // 2, d // 2)
    dup, sums_g, sums_v = _ffn_gate_bwd(up, da, fw, fb, seq)
    gw_up = _mm_tn_pieces("dw_up", h3, dup, f // 2, t)
    token = reduce(0, [gw_down.reshape(8, -1, d), gw_up])
    dx2, dx2b, dg_ffn = _dproj_rms_bwd("d_h3", dup, w_up, x2, g_ffn + token, dx3)
    do = _mm_nt("d_o", dx2b, w_o, _ACT)
    gw_o = _mm_tn_rows("dw_o", o, dx2b, d, d // 2)
    dq, dkv = _attn_bwd(q, kv, do, n_seq, seq, n_mem)
    gw_q = _mm_tn_rows("dw_q", h2, dq, d, d // 2)
    gw_kv = _mm_tn_pieces("dw_kv", mem_n, dkv, d // 2, mem.shape[0])
    dmem_n = _mm_nt("d_mem_n", dkv, w_kv, f32)
    dg_mem = _rms_gain_grad("norm_mem_bwd", mem, dmem_n)
    dx1, dx1b, dg_xattn = _dproj_rms_bwd("d_h2", dq, w_q, x1, g_xattn, dx2)
    dy = _mm_nt("d_y", dx1b, w_out, _ACT)
    gw_out = _mm_tn_rows("dw_out", y, dx1b, d, d // 2)
    token = reduce(1, [gw_o.reshape(8, -1, d), gw_q.reshape(8, -1, d), gw_kv, gw_out.reshape(8, -1, d)])
    dhc, sums_norm = _mix_bwd_norm(hc, dy, lg + token, lb, seq)
    du, d_cw, d_ps, d_pw = _mix_bwd_taps(u, dhc, dy, cw, pw, ps, seq)
    gw_in = _mm_tn_pieces("dw_in", h1, du, c * 3 // 4, t)
    token = reduce(2, [gw_in])
    grad_x, dg_mix = _dproj_rms_bwd("d_h1", du, w_in, x, g_mix + token, dx1, storage_copy=False)
    zero_row = jnp.zeros((1, d), f32)
    gains = jnp.concatenate([dg_mix, dg_xattn, dg_mem, dg_ffn, dg_final, jnp.pad(loss, ((0, 0), (0, d - 1))), zero_row, zero_row], axis=0)
    conv_rows = jnp.concatenate([sums_norm[2:3], sums_norm[0:1], sums_norm[1:2], d_ps[0:1], jnp.zeros((4, c), f32)], axis=0)
    ffn_rows = jnp.concatenate([sums_g, sums_v], axis=1)
    small = [gains, conv_rows, d_pw.reshape(-1, d_pw.shape[-1]), ffn_rows, d_cw]
    return grad_x, small


def kernel(x, mem, norm_mix_g, w_in, conv_dw_w, conv_dw_b, conv_ln_g, conv_ln_b, pool_w, pool_scale, w_out, norm_xattn_g, norm_mem_g, w_q, w_kv, w_o, norm_ffn_g, w_up, ffn_dw_w, ffn_dw_b, w_down, norm_final_g, loss_target, m_norm_mix_g, m_w_in, m_conv_dw_w, m_conv_dw_b, m_conv_ln_g, m_conv_ln_b, m_pool_w, m_pool_scale, m_w_out, m_norm_xattn_g, m_norm_mem_g, m_w_q, m_w_kv, m_w_o, m_norm_ffn_g, m_w_up, m_ffn_dw_w, m_ffn_dw_b, m_w_down, m_norm_final_g, v_norm_mix_g, v_w_in, v_conv_dw_w, v_conv_dw_b, v_conv_ln_g, v_conv_ln_b, v_pool_w, v_pool_scale, v_w_out, v_norm_xattn_g, v_norm_mem_g, v_w_q, v_w_kv, v_w_o, v_norm_ffn_g, v_w_up, v_ffn_dw_w, v_ffn_dw_b, v_w_down, v_norm_final_g):
    n_seq, seq, d = x.shape
    n_mem = mem.shape[1]
    chip = 2 * lax.axis_index("x") + lax.axis_index("y")

    place = jnp.stack([chip, lax.axis_index("c")]).astype(jnp.int32)

    col_w = [w_in, w_kv, w_up]
    row_w = [w_out, w_q, w_o, w_down]
    col_flags = [True] * 3 + [False] * 4 + [True] * 2
    kw = conv_dw_w.shape[1]

    def padded_in_place(shard, rows):
        full = jnp.zeros((rows, 4 * shard.shape[1]), shard.dtype)
        return lax.dynamic_update_slice(full, shard, (0, chip * shard.shape[1]))

    bufs = list(_place_shards(place, [w[0] for w in col_w + row_w], col_flags[:7]))
    bufs += [padded_in_place(conv_dw_w[0], _HALO), padded_in_place(ffn_dw_w[0], 8)]
    groups = [[0, 7, 8], [3, 4, 1, 5], [2, 6]]
    whole = [False] * 7 + [True] * 2
    bufs, sems = _allgather_start(bufs, col_flags, whole, groups)
    relayed = {}

    def relay(g, after):
        members = groups[g]
        relayed[g] = _allgather_relay("allgather_relay_%d" % g, [bufs[i] for i in members], [col_flags[i] for i in members],
                                      [whole[i] for i in members], sems[g], after)

    def weights(g, after):
        members = groups[g]
        group_bufs, sibling_sems = relayed[g]
        return _allgather_wait("allgather_wait_%d" % g, group_bufs, [col_flags[i] for i in members],
                               [whole[i] for i in members], sibling_sems, after)

    names = ["w_in", "w_kv", "w_up", "w_out", "w_q", "w_o", "w_down"]
    reduce_groups = [["w_down", "w_up"], ["w_o", "w_q", "w_kv", "w_out"], ["w_in"]]
    in_flight = {}

    def reduce(g, grads):
        grads, lands, rs_sems, token = _grad_exchange_start("rs_start_%d" % g, grads)
        in_flight[g] = (grads, lands, rs_sems)
        return token[0:1, 0:1]

    grad_x, small = _local_step(
        x.reshape(n_seq * seq, d), mem.reshape(n_seq * n_mem, d), loss_target.reshape(n_seq * seq, d),
        norm_mix_g, norm_xattn_g, norm_mem_g, norm_ffn_g, norm_final_g.reshape(1, d),
        conv_dw_b, conv_ln_g, conv_ln_b, pool_w[0], pool_scale, ffn_dw_b, relay, weights, reduce, n_seq, seq, n_mem)

    finals = {}
    for g, members in enumerate(reduce_groups):
        grads, lands, rs_sems = in_flight[g]
        grads, lands = _grad_exchange_wait("rs_wait_%d" % g, grads, lands, rs_sems, grad_x)
        for n, a, b in zip(members, grads, lands):
            finals[n] = _sum_partials("rs_sum_" + n, place, a, b)
    shard_grads = _swap_halves([finals[n] for n in names])

    gains, conv_rows, d_pw, ffn_rows, d_cw = _allreduce_small(small)
    loss = gains[5, 0]

    outs = {}
    big_w = dict(zip(names, col_w + row_w))
    big_m = dict(w_in=m_w_in, w_kv=m_w_kv, w_up=m_w_up, w_out=m_w_out, w_q=m_w_q, w_o=m_w_o, w_down=m_w_down)
    big_v = dict(w_in=v_w_in, w_kv=v_w_kv, w_up=v_w_up, w_out=v_w_out, w_q=v_w_q, w_o=v_w_o, w_down=v_w_down)
    for n, g in zip(names, shard_grads):
        w = big_w[n]
        g2 = g.reshape(w.shape[1], w.shape[2])
        delta, new_m, new_v = _adamw_shard("adamw_" + n, w, g2, big_m[n], big_v[n])
        outs[n] = (g2.reshape(w.shape), delta, new_m, new_v)

    f2 = ffn_dw_b.shape[1]
    cs_c = conv_dw_w.shape[2]
    cs_f = ffn_dw_w.shape[2]
    g_cw = lax.dynamic_slice(d_cw, (0, chip * cs_c), (kw, cs_c)).reshape(conv_dw_w.shape)
    g_fw = lax.dynamic_slice(ffn_rows, (1, chip * cs_f), (ffn_dw_w.shape[1], cs_f)).reshape(ffn_dw_w.shape)
    small_params = [
        ("norm_mix_g", norm_mix_g, gains[0:1], m_norm_mix_g, v_norm_mix_g),
        ("conv_dw_w", conv_dw_w, g_cw, m_conv_dw_w, v_conv_dw_w),
        ("conv_dw_b", conv_dw_b, conv_rows[0:1], m_conv_dw_b, v_conv_dw_b),
        ("conv_ln_g", conv_ln_g, conv_rows[1:2], m_conv_ln_g, v_conv_ln_g),
        ("conv_ln_b", conv_ln_b, conv_rows[2:3], m_conv_ln_b, v_conv_ln_b),
        ("pool_w", pool_w, d_pw.reshape(pool_w.shape), m_pool_w, v_pool_w),
        ("pool_scale", pool_scale, conv_rows[3:4], m_pool_scale, v_pool_scale),
        ("norm_xattn_g", norm_xattn_g, gains[1:2], m_norm_xattn_g, v_norm_xattn_g),
        ("norm_mem_g", norm_mem_g, gains[2:3], m_norm_mem_g, v_norm_mem_g),
        ("norm_ffn_g", norm_ffn_g, gains[3:4], m_norm_ffn_g, v_norm_ffn_g),
        ("ffn_dw_w", ffn_dw_w, g_fw, m_ffn_dw_w, v_ffn_dw_w),
        ("ffn_dw_b", ffn_dw_b, ffn_rows[0:1, :f2], m_ffn_dw_b, v_ffn_dw_b),
        ("norm_final_g", norm_final_g.reshape(1, d), gains[4:5], m_norm_final_g.reshape(1, d), v_norm_final_g.reshape(1, d)),
    ]
    quads = []
    for _, w, g, m, v in small_params:
        shape2 = (-1, w.shape[-1])
        quads.append((w.reshape(shape2), g.reshape(shape2), m.reshape(shape2), v.reshape(shape2)))
    for (n, w, g, _, _), (delta, new_m, new_v) in zip(small_params, _adamw_small(quads)):
        shape = norm_final_g.shape if n == "norm_final_g" else w.shape
        outs[n] = (g.reshape(shape), delta.reshape(shape), new_m.reshape(shape), new_v.reshape(shape))

    order = ["norm_mix_g", "w_in", "conv_dw_w", "conv_dw_b", "conv_ln_g", "conv_ln_b", "pool_w", "pool_scale", "w_out",
             "norm_xattn_g", "norm_mem_g", "w_q", "w_kv", "w_o", "norm_ffn_g", "w_up", "ffn_dw_w", "ffn_dw_b", "w_down",
             "norm_final_g"]
    return (loss, grad_x.reshape(x.shape), *[outs[n][0] for n in order], *[outs[n][1] for n in order],
            *[outs[n][2] for n in order], *[outs[n][3] for n in order])
```

```python
import functools

import jax
import jax.numpy as jnp
from jax import lax
from jax.experimental import pallas as pl
from jax.experimental.pallas import tpu as pltpu

f32 = jnp.float32
_ACT = jnp.bfloat16

EPS = 1e-6
POOL_WINDOWS = (2, 4, 8, 16)
XATTN_HEADS = 4
ADAM_LR = 0.001
ADAM_B1 = 0.9
ADAM_B2 = 0.999
ADAM_EPS = 1e-08
ADAM_WD = 0.01
ADAM_STEP = 10

_VMEM_LIMIT_BYTES = 56 * 1024 * 1024
_MESH = pl.DeviceIdType.MESH
_ANY = pl.BlockSpec(memory_space=pl.ANY)
_VMEM = pl.BlockSpec(memory_space=pltpu.VMEM)
_HBM = pl.BlockSpec(memory_space=pltpu.HBM)
_SEM = pl.BlockSpec(memory_space=pltpu.SEMAPHORE)
_EFFECT = pltpu.SideEffectType.DATAFLOW_SIDE_EFFECTING

_NN = (((1,), (0,)), ((), ()))
_NT = (((1,), (1,)), ((), ()))
_TN = (((0,), (0,)), ((), ()))


def _params(n_grid):
    return pltpu.CompilerParams(dimension_semantics=("arbitrary",) * n_grid, vmem_limit_bytes=_VMEM_LIMIT_BYTES)


def _sigmoid(v):
    return 1.0 / (1.0 + jnp.exp(-v))


def _dot(a, b, dims):
    return lax.dot_general(a, b, dims, preferred_element_type=f32)


def _mm(name, a, b, *, dims, grid, a_spec, b_spec, o_spec, out_shape, nk, acc_shape=None, res=None, res_spec=None):
    def body(*refs):
        if res is None:
            a_ref, b_ref, o_ref, *scratch = refs
            r_ref = None
        else:
            a_ref, b_ref, r_ref, o_ref, *scratch = refs
        p = _dot(a_ref[...], b_ref[...], dims)

        def finish(v):
            if r_ref is not None:
                v = v + r_ref[...]
            o_ref[...] = v.astype(o_ref.dtype)

        if nk == 1:
            finish(p)
        else:
            acc = scratch[0]
            k = pl.program_id(2)

            @pl.when(k == 0)
            def _():
                acc[...] = p

            @pl.when(k > 0)
            def _():
                acc[...] += p

            @pl.when(k == nk - 1)
            def _():
                finish(acc[...])

    ins = [a, b] + ([] if res is None else [res])
    specs = [a_spec, b_spec] + ([] if res is None else [res_spec])
    return pl.pallas_call(
        body, name=name, grid=grid, in_specs=specs, out_specs=o_spec, out_shape=out_shape,
        scratch_shapes=[pltpu.VMEM(acc_shape, f32)] if nk > 1 else [], compiler_params=_params(3),
    )(*ins)


_NARROW = 1536


def _row_tile(m, width=_NARROW + 1):
    return min(1024 if width <= _NARROW else 512, m)


def _mm_nn(name, a, b, out_dtype, tn, res=None, split_out=False):
    m, k = a.shape
    n = b.shape[1]
    tm = _row_tile(m, max(k, tn))
    if split_out:
        out_shape = jax.ShapeDtypeStruct((n // tn, m, tn), out_dtype)
        o_spec = pl.BlockSpec((None, tm, tn), lambda j, i, kk: (j, i, 0))
    else:
        out_shape = jax.ShapeDtypeStruct((m, n), out_dtype)
        o_spec = pl.BlockSpec((tm, tn), lambda j, i, kk: (i, j))
    return _mm(
        name, a, b, dims=_NN, grid=(n // tn, m // tm, 1), nk=1,
        a_spec=pl.BlockSpec((tm, k), lambda j, i, kk: (i, 0)),
        b_spec=pl.BlockSpec((k, tn), lambda j, i, kk: (0, j)),
        o_spec=o_spec, out_shape=out_shape, res=res,
        res_spec=pl.BlockSpec((tm, tn), lambda j, i, kk: (i, j)),
    )


def _mm_nt(name, a, b, out_dtype):
    n, kc = b.shape
    m = a.shape[0]
    tm = _row_tile(m, max(n, kc))
    return _mm(
        name, a, b, dims=_NT, grid=(m // tm, 1, 1), nk=1,
        a_spec=pl.BlockSpec((tm, kc), lambda i, j, k: (i, 0)), b_spec=pl.BlockSpec((n, kc), lambda i, j, k: (0, 0)),
        o_spec=pl.BlockSpec((tm, n), lambda i, j, k: (i, 0)),
        out_shape=jax.ShapeDtypeStruct((m, n), out_dtype),
    )


def _mm_tn_rows(name, a, b, tka, tn):
    m, ka = a.shape
    nb = b.shape[1]
    return _mm(
        name, a, b, dims=_TN, grid=(ka // tka, nb // tn, 1), nk=1,
        a_spec=pl.BlockSpec((m, tka), lambda i, j, k: (0, i)),
        b_spec=pl.BlockSpec((m, tn), lambda i, j, k: (0, j)),
        o_spec=pl.BlockSpec((tka, tn), lambda i, j, k: (i, j)),
        out_shape=jax.ShapeDtypeStruct((ka, nb), _ACT),
    )


def _mm_tn_pieces(name, a, b, cs, tt):
    m, ka = a.shape
    nk = m // tt
    if b.ndim == 3:
        b_spec = pl.BlockSpec((None, tt, cs), lambda i, j, k: (j // 2, k, j % 2))
    else:
        b_spec = pl.BlockSpec((tt, cs), lambda i, j, k: (k, j))
    return _mm(
        name, a, b, dims=_TN, grid=(2, 4, nk), nk=nk, acc_shape=(ka // 2, cs),
        a_spec=pl.BlockSpec((tt, ka // 2), lambda i, j, k: (k, i)), b_spec=b_spec,
        o_spec=pl.BlockSpec((None, ka // 2, cs), lambda i, j, k: (2 * j + i, 0, 0)),
        out_shape=jax.ShapeDtypeStruct((8, ka // 2, cs), _ACT),
    )


def _rms_fwd(name, x, g):
    t, d = x.shape
    tm = _row_tile(t, d)

    def body(x_ref, g_ref, h_ref):
        xv = x_ref[...]
        r = lax.rsqrt(jnp.mean(xv * xv, axis=-1, keepdims=True) + EPS)
        h_ref[...] = (xv * r * g_ref[...]).astype(h_ref.dtype)

    return pl.pallas_call(
        body, name=name, grid=(t // tm,),
        in_specs=[pl.BlockSpec((tm, d), lambda i: (i, 0)), pl.BlockSpec((1, d), lambda i: (0, 0))],
        out_specs=pl.BlockSpec((tm, d), lambda i: (i, 0)), out_shape=jax.ShapeDtypeStruct((t, d), _ACT),
        compiler_params=_params(1),
    )(x, g)


def _fused_rows(name, a, b, product, a_spec, tm, extras, extra_specs, out_shape, out_specs, epilogue):
    ne = len(extras)

    def body(a_ref, b_ref, *refs):
        epilogue(product(a_ref, b_ref), refs[:ne], refs[ne:])

    m = extras[0].shape[0]
    return pl.pallas_call(
        body, name=name, grid=(m // tm,),
        in_specs=[a_spec, pl.BlockSpec(b.shape, lambda i: (0, 0)), *extra_specs], out_specs=out_specs, out_shape=out_shape,
        compiler_params=_params(1),
    )(a, b, *extras)


def _proj_residual_norm(name, a, b, res, g):
    m, k = a.shape
    d = b.shape[1]
    tm = _row_tile(m, max(k, d))

    def epilogue(p, ins, outs):
        xv = p + ins[0][...]
        outs[0][...] = xv
        r = lax.rsqrt(jnp.mean(xv * xv, axis=-1, keepdims=True) + EPS)
        outs[1][...] = (xv * r * ins[1][...]).astype(outs[1].dtype)

    row = pl.BlockSpec((tm, d), lambda i: (i, 0))
    return _fused_rows(
        name, a, b, lambda a_ref, b_ref: _dot(a_ref[...], b_ref[...], _NN), pl.BlockSpec((tm, k), lambda i: (i, 0)), tm,
        [res, g], [row, pl.BlockSpec((1, d), lambda i: (0, 0))],
        [jax.ShapeDtypeStruct((m, d), f32), jax.ShapeDtypeStruct((m, d), _ACT)], [row, row], epilogue)


def _dproj_rms_bwd(name, a, b, x, g, dres, storage_copy=True):
    m, d = x.shape
    if a.ndim == 3:
        nh, _, kh = a.shape
        tm = min(256, m)
        a_spec = pl.BlockSpec((nh, tm, kh), lambda i: (0, i, 0))

        def product(a_ref, b_ref):
            p = _dot(a_ref[0], b_ref[:, 0:kh], _NT)
            for h in range(1, nh):
                p = p + _dot(a_ref[h], b_ref[:, h * kh:(h + 1) * kh], _NT)
            return p
    else:
        tm = _row_tile(m, max(a.shape[1], d))
        a_spec = pl.BlockSpec((tm, a.shape[1]), lambda i: (i, 0))

        def product(a_ref, b_ref):
            return _dot(a_ref[...], b_ref[...], _NT)

    def epilogue(dhv, ins, outs):
        x_ref, g_ref, dres_ref = ins
        dg_ref = outs[-1]

        @pl.when(pl.program_id(0) == 0)
        def _():
            dg_ref[...] = jnp.zeros_like(dg_ref)

        xv = x_ref[...]
        r = lax.rsqrt(jnp.mean(xv * xv, axis=-1, keepdims=True) + EPS)
        xn = xv * r
        dxn = dhv * g_ref[...]
        dx = r * (dxn - xn * jnp.mean(dxn * xn, axis=-1, keepdims=True)) + dres_ref[...]
        outs[0][...] = dx
        if storage_copy:
            outs[1][...] = dx.astype(outs[1].dtype)
        dg_ref[...] += jnp.sum(dhv * xn, axis=0, keepdims=True)

    row = pl.BlockSpec((tm, d), lambda i: (i, 0))
    vec = pl.BlockSpec((1, d), lambda i: (0, 0))
    copies = [jax.ShapeDtypeStruct((m, d), _ACT)] if storage_copy else []
    return _fused_rows(
        name, a, b, product, a_spec, tm, [x, g, dres], [row, vec, row],
        [jax.ShapeDtypeStruct((m, d), f32)] + copies + [jax.ShapeDtypeStruct((1, d), f32)],
        [row] * (1 + len(copies)) + [vec], epilogue)


def _proj_loss_bwd(name, a, b, res, g, tgt):
    m, k = a.shape
    d = b.shape[1]
    tm = _row_tile(m)

    def epilogue(p, ins, outs):
        res_ref, g_ref, t_ref = ins
        dx_ref, dxb_ref, dg_ref, loss_ref = outs

        @pl.when(pl.program_id(0) == 0)
        def _():
            dg_ref[...] = jnp.zeros_like(dg_ref)
            loss_ref[...] = jnp.zeros_like(loss_ref)

        xv = p + res_ref[...]
        gv = g_ref[...]
        r = lax.rsqrt(jnp.mean(xv * xv, axis=-1, keepdims=True) + EPS)
        xn = xv * r
        err = xn * gv - t_ref[...]
        loss_ref[...] += 0.5 * jnp.sum(jnp.mean(err * err, axis=-1, keepdims=True), axis=0, keepdims=True)
        dout = err * (1.0 / d)
        dxn = dout * gv
        dx = r * (dxn - xn * jnp.mean(dxn * xn, axis=-1, keepdims=True))
        dx_ref[...] = dx
        dxb_ref[...] = dx.astype(dxb_ref.dtype)
        dg_ref[...] += jnp.sum(dout * xn, axis=0, keepdims=True)

    row = pl.BlockSpec((tm, d), lambda i: (i, 0))
    vec = pl.BlockSpec((1, d), lambda i: (0, 0))
    return _fused_rows(
        name, a, b, lambda a_ref, b_ref: _dot(a_ref[...], b_ref[...], _NN), pl.BlockSpec((tm, k), lambda i: (i, 0)), tm,
        [res, g, tgt], [row, vec, row],
        [jax.ShapeDtypeStruct((m, d), f32), jax.ShapeDtypeStruct((m, d), _ACT), jax.ShapeDtypeStruct((1, d), f32),
         jax.ShapeDtypeStruct((1, 1), f32)],
        [row, row, vec, pl.BlockSpec((1, 1), lambda i: (0, 0))], epilogue)


def _rms_gain_grad(name, x, dh):
    t, d = x.shape
    tm = _row_tile(t)

    def body(x_ref, dh_ref, dg_ref):
        @pl.when(pl.program_id(0) == 0)
        def _():
            dg_ref[...] = jnp.zeros_like(dg_ref)

        xv = x_ref[...]
        r = lax.rsqrt(jnp.mean(xv * xv, axis=-1, keepdims=True) + EPS)
        dg_ref[...] += jnp.sum(dh_ref[...] * (xv * r), axis=0, keepdims=True)

    row = pl.BlockSpec((tm, d), lambda i: (i, 0))
    return pl.pallas_call(
        body, name=name, grid=(t // tm,), in_specs=[row, row], out_specs=pl.BlockSpec((1, d), lambda i: (0, 0)),
        out_shape=jax.ShapeDtypeStruct((1, d), f32), compiler_params=_params(1),
    )(x, dh)


_CONV_ROWS = 256
_CHUNK = 64
_HALO = 32


def _pool_counts(pos, w):
    return jnp.minimum(pos + 1.0, float(w))


def _rows_from(win, start, rows):
    if start % 8 == 0:
        return win[start:start + rows, :]
    n = win.shape[0]
    return pltpu.roll(win, n - start % 8, axis=0)[start - start % 8:start - start % 8 + rows, :]


def _tap_rows(buf, starts, rows):
    for residue in range(8):
        group = [(k, s) for k, s in starts.items() if s % 8 == residue]
        if group:
            lo = min(s for _, s in group) - residue
            hi = max(s for _, s in group) - residue + rows + (8 if residue else 0)
            win = buf[lo:hi, :]
            if residue:
                win = pltpu.roll(win, hi - lo - residue, axis=0)
            for k, s in group:
                yield k, win[s - residue - lo:s - residue - lo + rows, :]


def _mix_fwd(u, cw, cb, lg, lb, pw, ps, seq):
    t, c3 = u.shape
    c = c3 // 3
    kw = 31
    tm = min(_CONV_ROWS, seq)
    tps = seq // tm
    gd = c // len(POOL_WINDOWS)

    def body(u_ref, uh_ref, cw_ref, cb_ref, lg_ref, lb_ref, pw_ref, ps_ref, y_ref, hc_ref, hgbuf, pbuf):
        i = pl.program_id(0)
        keep = jnp.where(i % tps == 0, 0.0, 1.0)
        um = u_ref[...].astype(f32)
        uh = uh_ref[...].astype(f32) * keep
        hgbuf[0:_HALO, :] = uh[:, 0:c] * _sigmoid(uh[:, c:2 * c])
        hgbuf[_HALO:_HALO + tm, :] = um[:, 0:c] * _sigmoid(um[:, c:2 * c])
        pbuf[0:_HALO, :] = uh[:, 2 * c:]
        pbuf[_HALO:_HALO + tm, :] = um[:, 2 * c:]
        for r0 in range(0, tm, _CHUNK):
            acc = jnp.broadcast_to(cb_ref[...], (_CHUNK, c))
            for k, rows in _tap_rows(hgbuf, {k: r0 + _HALO - (kw - 1) + k for k in range(kw)}, _CHUNK):
                acc = acc + cw_ref[k:k + 1, :] * rows
            hc_ref[r0:r0 + _CHUNK, :] = acc
            mu = jnp.mean(acc, axis=-1, keepdims=True)
            xc = acc - mu
            var = jnp.mean(xc * xc, axis=-1, keepdims=True)
            hl = xc * lax.rsqrt(var + EPS) * lg_ref[...] + lb_ref[...]
            y_ref[r0:r0 + _CHUNK, 0:c] = (hl * _sigmoid(hl)).astype(y_ref.dtype)
        pos = ((i % tps) * tm).astype(f32) + lax.broadcasted_iota(jnp.int32, (tm, 1), 0).astype(f32)
        for gi, w in enumerate(POOL_WINDOWS):
            sl = slice(gi * gd, (gi + 1) * gd)
            v = pbuf[_HALO:_HALO + tm, sl]
            s = v
            for j in range(1, w):
                s = s + pbuf[_HALO - j:_HALO - j + tm, sl]
            pooled = s / _pool_counts(pos, w) - v
            mixed = _dot(pooled.astype(_ACT), pw_ref[gi].astype(_ACT), _NN)
            y_ref[:, c + gi * gd:c + (gi + 1) * gd] = (mixed * ps_ref[:, sl]).astype(y_ref.dtype)

    hb = tm // _HALO
    full = lambda shape: pl.BlockSpec(shape, lambda i: (0,) * len(shape))
    return pl.pallas_call(
        body, name="mix_fwd", grid=(t // tm,),
        in_specs=[pl.BlockSpec((tm, c3), lambda i: (i, 0)),
                  pl.BlockSpec((_HALO, c3), lambda i: (jnp.maximum(i * hb - 1, 0), 0)),
                  full((_HALO, c)), full((1, c)), full((1, c)), full((1, c)), full((len(POOL_WINDOWS), gd, gd)), full((1, c))],
        out_specs=[pl.BlockSpec((tm, 2 * c), lambda i: (i, 0)), pl.BlockSpec((tm, c), lambda i: (i, 0))],
        out_shape=[jax.ShapeDtypeStruct((t, 2 * c), _ACT), jax.ShapeDtypeStruct((t, c), f32)],
        scratch_shapes=[pltpu.VMEM((_HALO + tm, c), f32), pltpu.VMEM((_HALO + tm, c), f32)],
        compiler_params=_params(1),
    )(u, u, cw, cb, lg, lb, pw, ps)


def _mix_bwd_norm(hc, dy, lg, lb, seq):
    t, c = hc.shape
    tm = min(_CONV_ROWS, seq)

    def body(hc_ref, dy_ref, lg_ref, lb_ref, dhc_ref, sums_ref):
        @pl.when(pl.program_id(0) == 0)
        def _():
            sums_ref[...] = jnp.zeros_like(sums_ref)

        hcv = hc_ref[...]
        mu = jnp.mean(hcv, axis=-1, keepdims=True)
        xc = hcv - mu
        rstd = lax.rsqrt(jnp.mean(xc * xc, axis=-1, keepdims=True) + EPS)
        n = xc * rstd
        hl = n * lg_ref[...] + lb_ref[...]
        sg = _sigmoid(hl)
        dhl = dy_ref[...].astype(f32) * (sg * (1.0 + hl * (1.0 - sg)))
        dn = dhl * lg_ref[...]
        dhc = rstd * (dn - jnp.mean(dn, axis=-1, keepdims=True) - n * jnp.mean(dn * n, axis=-1, keepdims=True))
        dhc_ref[...] = dhc
        sums_ref[0:1, :] += jnp.sum(dhl * n, axis=0, keepdims=True)
        sums_ref[1:2, :] += jnp.sum(dhl, axis=0, keepdims=True)
        sums_ref[2:3, :] += jnp.sum(dhc, axis=0, keepdims=True)

    row = pl.BlockSpec((tm, c), lambda i: (i, 0))
    vec = pl.BlockSpec((1, c), lambda i: (0, 0))
    return pl.pallas_call(
        body, name="mix_bwd_norm", grid=(t // tm,), in_specs=[row, row, vec, vec],
        out_specs=[row, pl.BlockSpec((8, c), lambda i: (0, 0))],
        out_shape=[jax.ShapeDtypeStruct((t, c), f32), jax.ShapeDtypeStruct((8, c), f32)],
        compiler_params=_params(1),
    )(hc, dy, lg, lb)


def _mix_bwd_taps(u, dhc, dy, cw, pw, ps, seq):
    t, c3 = u.shape
    c = c3 // 3
    kw = 31
    tm = min(_CONV_ROWS, seq)
    tps = seq // tm
    ng = len(POOL_WINDOWS)
    gd = c // ng
    nh = 16

    def body(u_ref, uh_ref, dhc_ref, dhcn_ref, dy_ref, dyn_ref, cw_ref, pw_ref, ps_ref,
             du_ref, dcw_ref, dps_ref, dpw_ref, hgbuf, dcbuf, pbuf, dpbuf):
        i = pl.program_id(0)
        keep_prev = jnp.where(i % tps == 0, 0.0, 1.0)
        keep_next = jnp.where(i % tps == tps - 1, 0.0, 1.0)

        @pl.when(i == 0)
        def _():
            dcw_ref[...] = jnp.zeros_like(dcw_ref)
            dps_ref[...] = jnp.zeros_like(dps_ref)
            dpw_ref[...] = jnp.zeros_like(dpw_ref)

        uh = uh_ref[...].astype(f32) * keep_prev
        hgbuf[0:_HALO, :] = uh[:, 0:c] * _sigmoid(uh[:, c:2 * c])
        pbuf[0:_HALO, :] = uh[:, 2 * c:]
        um = u_ref[...].astype(f32)
        hgbuf[_HALO:_HALO + tm, :] = um[:, 0:c] * _sigmoid(um[:, c:2 * c])
        pbuf[_HALO:_HALO + tm, :] = um[:, 2 * c:]
        dcbuf[0:tm, :] = dhc_ref[...]
        dcbuf[tm:tm + _HALO, :] = dhcn_ref[...] * keep_next
        tap_sums = [None] * kw
        for r0 in range(0, tm, _CHUNK):
            dh = dcbuf[r0:r0 + _CHUNK, :]
            acc = jnp.zeros((_CHUNK, c), f32)
            for k, rows in _tap_rows(hgbuf, {k: r0 + _HALO - (kw - 1) + k for k in range(kw)}, _CHUNK):
                part = (dh * rows).reshape(_CHUNK // 8, 8, c).sum(axis=0)
                tap_sums[k] = part if tap_sums[k] is None else tap_sums[k] + part
            for k, rows in _tap_rows(dcbuf, {k: r0 + (kw - 1) - k for k in range(kw)}, _CHUNK):
                acc = acc + cw_ref[k:k + 1, :] * rows
            val = u_ref[r0:r0 + _CHUNK, 0:c].astype(f32)
            sg = _sigmoid(u_ref[r0:r0 + _CHUNK, c:2 * c].astype(f32))
            du_ref[r0:r0 + _CHUNK, 0:c] = (acc * sg).astype(du_ref.dtype)
            du_ref[r0:r0 + _CHUNK, c:2 * c] = (acc * val * sg * (1.0 - sg)).astype(du_ref.dtype)
        for k in range(kw):
            dcw_ref[k:k + 1, :] += jnp.sum(tap_sums[k], axis=0, keepdims=True)
        base = ((i % tps) * tm).astype(f32)
        pos = base + lax.broadcasted_iota(jnp.int32, (tm, 1), 0).astype(f32)
        pos_next = base + float(tm) + lax.broadcasted_iota(jnp.int32, (nh, 1), 0).astype(f32)
        for gi, w in enumerate(POOL_WINDOWS):
            sl = slice(gi * gd, (gi + 1) * gd)
            v = pbuf[_HALO:_HALO + tm, sl]
            s = v
            for j in range(1, w):
                s = s + pbuf[_HALO - j:_HALO - j + tm, sl]
            cnt = _pool_counts(pos, w)
            pooled = (s / cnt - v).astype(_ACT)
            pwg = pw_ref[gi].astype(_ACT)
            mixed = _dot(pooled, pwg, _NN)
            dyp = dy_ref[:, sl].astype(f32)
            dps_ref[0:1, sl] += jnp.sum(dyp * mixed, axis=0, keepdims=True)
            dmix = (dyp * ps_ref[:, sl]).astype(_ACT)
            dpw_ref[gi] += _dot(pooled, dmix, _TN)
            dmix_next = (dyn_ref[:, sl].astype(f32) * ps_ref[:, sl] * keep_next).astype(_ACT)
            dpool = _dot(dmix, pwg, _NT)
            dpbuf[0:tm, sl] = dpool / cnt
            dpbuf[tm:tm + nh, sl] = _dot(dmix_next, pwg, _NT) / _pool_counts(pos_next, w)
            acc = -dpool
            for j in range(w):
                acc = acc + dpbuf[j:j + tm, sl]
            du_ref[:, 2 * c + gi * gd:2 * c + (gi + 1) * gd] = acc.astype(du_ref.dtype)

    hb = tm // _HALO
    n_halo = t // _HALO
    n_nh = t // nh
    full = lambda shape: pl.BlockSpec(shape, lambda i: (0,) * len(shape))
    return pl.pallas_call(
        body, name="mix_bwd_taps", grid=(t // tm,),
        in_specs=[pl.BlockSpec((tm, c3), lambda i: (i, 0)),
                  pl.BlockSpec((_HALO, c3), lambda i: (jnp.maximum(i * hb - 1, 0), 0)),
                  pl.BlockSpec((tm, c), lambda i: (i, 0)),
                  pl.BlockSpec((_HALO, c), lambda i: (jnp.minimum((i + 1) * hb, n_halo - 1), 0)),
                  pl.BlockSpec((tm, c), lambda i: (i, 1)),
                  pl.BlockSpec((nh, c), lambda i: (jnp.minimum((i + 1) * (tm // nh), n_nh - 1), 1)),
                  full((_HALO, c)), full((ng, gd, gd)), full((1, c))],
        out_specs=[pl.BlockSpec((tm, c3), lambda i: (i, 0)), full((_HALO, c)), full((8, c)), full((ng, gd, gd))],
        out_shape=[jax.ShapeDtypeStruct((t, c3), _ACT), jax.ShapeDtypeStruct((_HALO, c), f32),
                   jax.ShapeDtypeStruct((8, c), f32), jax.ShapeDtypeStruct((ng, gd, gd), f32)],
        scratch_shapes=[pltpu.VMEM((_HALO + tm, c), f32), pltpu.VMEM((tm + _HALO, c), f32),
                        pltpu.VMEM((_HALO + tm, c), f32), pltpu.VMEM((tm + nh, c), f32)],
        compiler_params=_params(1),
    )(u, u, dhc, dhc, dy, dy, cw, pw, ps)


def _attn_fwd(q, kv, n_seq, seq, n_mem):
    t, d = q.shape
    dh = d // XATTN_HEADS
    tq = min(512, seq)
    nq = seq // tq
    scale = dh ** -0.5

    def body(q_ref, kv_ref, o_ref):
        for h in range(XATTN_HEADS):
            cols = slice(h * dh, (h + 1) * dh)
            s = _dot(q_ref[:, cols], kv_ref[:, cols], _NT) * scale
            e = jnp.exp(s - jnp.max(s, axis=-1, keepdims=True))
            p = e / jnp.sum(e, axis=-1, keepdims=True)
            o_ref[:, cols] = _dot(p.astype(_ACT), kv_ref[:, d + h * dh:d + (h + 1) * dh], _NN).astype(o_ref.dtype)

    qs = pl.BlockSpec((tq, d), lambda b, i: (b * nq + i, 0))
    return pl.pallas_call(
        body, name="attn_fwd", grid=(n_seq, nq), in_specs=[qs, pl.BlockSpec((n_mem, 2 * d), lambda b, i: (b, 0))],
        out_specs=qs, out_shape=jax.ShapeDtypeStruct((t, d), _ACT), compiler_params=_params(2),
    )(q, kv)


def _attn_bwd(q, kv, do, n_seq, seq, n_mem):
    t, d = q.shape
    dh = d // XATTN_HEADS
    tq = min(512, seq)
    nq = seq // tq
    scale = dh ** -0.5

    def body(q_ref, kv_ref, do_ref, dq_ref, dkv_ref, acc):
        i = pl.program_id(1)

        @pl.when(i == 0)
        def _():
            acc[...] = jnp.zeros_like(acc)

        for h in range(XATTN_HEADS):
            cols = slice(h * dh, (h + 1) * dh)
            vcols = slice(d + h * dh, d + (h + 1) * dh)
            qv = q_ref[:, cols]
            kh = kv_ref[:, cols]
            dov = do_ref[:, cols]
            s = _dot(qv, kh, _NT) * scale
            e = jnp.exp(s - jnp.max(s, axis=-1, keepdims=True))
            p = e / jnp.sum(e, axis=-1, keepdims=True)
            dp = _dot(dov, kv_ref[:, vcols], _NT)
            ds = (p * (dp - jnp.sum(dp * p, axis=-1, keepdims=True)) * scale).astype(_ACT)
            dq_ref[:, cols] = _dot(ds, kh, _NN).astype(dq_ref.dtype)
            acc[:, cols] += _dot(ds, qv, _TN)
            acc[:, vcols] += _dot(p.astype(_ACT), dov, _TN)

        @pl.when(i == nq - 1)
        def _():
            dkv_ref[...] = acc[...].astype(dkv_ref.dtype)

    qs = pl.BlockSpec((tq, d), lambda b, i: (b * nq + i, 0))
    ms = pl.BlockSpec((n_mem, 2 * d), lambda b, i: (b, 0))
    return pl.pallas_call(
        body, name="attn_bwd", grid=(n_seq, nq), in_specs=[qs, ms, qs], out_specs=[qs, ms],
        out_shape=[jax.ShapeDtypeStruct((t, d), _ACT), jax.ShapeDtypeStruct((n_seq * n_mem, 2 * d), _ACT)],
        scratch_shapes=[pltpu.VMEM((n_mem, 2 * d), f32)], compiler_params=_params(2),
    )(q, kv, do)


_FFN_ROWS = 2048
_FFN_COLS = 256
_FFN_HALO = 16


def _window(buf, g, start, rows):
    return buf[g, pl.ds(start, rows + 8), :]


def _taps3(win, rows):
    return [_rows_from(win, 6 + k, rows) for k in range(3)]


def _conv3(b_ref, w_ref, taps):
    acc = b_ref[...] + w_ref[0:1, :] * taps[0]
    for k in (1, 2):
        acc = acc + w_ref[k:k + 1, :] * taps[k]
    return acc


def _ffn_gate_fwd(up, fw, fb, seq):
    _, t, f = up.shape
    tm = min(_FFN_ROWS, seq)
    tps = seq // tm
    tc = _FFN_COLS
    nc = f // tc
    hl = _FFN_HALO

    def body(up_ref, uph_ref, wg_ref, wv_ref, bg_ref, bv_ref, a_ref, buf):
        i = pl.program_id(1)
        keep = jnp.where(i % tps == 0, 0.0, 1.0)
        buf[:, 0:hl, :] = uph_ref[...].astype(f32) * keep
        buf[:, hl:hl + tm, :] = up_ref[...].astype(f32)

        def chunk(ci, carry):
            r0 = pl.multiple_of(ci * _CHUNK, _CHUNK)
            conv = []
            for g, (w_ref, b_ref) in enumerate(((wg_ref, bg_ref), (wv_ref, bv_ref))):
                conv.append(_conv3(b_ref, w_ref, _taps3(_window(buf, g, r0 + hl - 8, _CHUNK), _CHUNK)))
            gate, val = conv
            a_ref[pl.ds(r0, _CHUNK), :] = (gate * _sigmoid(gate) * val).astype(a_ref.dtype)
            return carry

        lax.fori_loop(0, tm // _CHUNK, chunk, 0)

    hb = tm // hl
    return pl.pallas_call(
        body, name="ffn_gate_fwd", grid=(nc, t // tm),
        in_specs=[pl.BlockSpec((2, tm, tc), lambda j, i: (0, i, j)),
                  pl.BlockSpec((2, hl, tc), lambda j, i: (0, jnp.maximum(i * hb - 1, 0), j)),
                  pl.BlockSpec((8, tc), lambda j, i: (0, j)), pl.BlockSpec((8, tc), lambda j, i: (0, nc + j)),
                  pl.BlockSpec((1, tc), lambda j, i: (0, j)), pl.BlockSpec((1, tc), lambda j, i: (0, nc + j))],
        out_specs=pl.BlockSpec((tm, tc), lambda j, i: (i, j)),
        out_shape=jax.ShapeDtypeStruct((t, f), _ACT),
        scratch_shapes=[pltpu.VMEM((2, hl + tm, tc), f32)], compiler_params=_params(2),
    )(up, up, fw, fw, fb, fb)


def _ffn_gate_bwd(up, da, fw, fb, seq):
    _, t, f = up.shape
    tm = min(_FFN_ROWS, seq)
    tps = seq // tm
    tc = _FFN_COLS
    nc = f // tc
    hl = _FFN_HALO

    def body(up_ref, uph_ref, upn_ref, da_ref, dan_ref, wg_ref, wv_ref, bg_ref, bv_ref,
             dup_ref, sg_ref, sv_ref, ubuf, dbuf, sums):
        i = pl.program_id(1)
        keep_prev = jnp.where(i % tps == 0, 0.0, 1.0)
        keep_next = jnp.where(i % tps == tps - 1, 0.0, 1.0)

        @pl.when(i == 0)
        def _():
            sg_ref[...] = jnp.zeros_like(sg_ref)
            sv_ref[...] = jnp.zeros_like(sv_ref)

        sums[...] = jnp.zeros_like(sums)
        ubuf[:, 0:hl, :] = uph_ref[...].astype(f32) * keep_prev
        ubuf[:, hl:hl + tm, :] = up_ref[...].astype(f32)
        ubuf[:, hl + tm:hl + tm + hl, :] = upn_ref[...].astype(f32) * keep_next
        w_refs = (wg_ref, wv_ref)
        b_refs = (bg_ref, bv_ref)

        def grads(r0, rows, dav, count):
            taps = [_taps3(_window(ubuf, g, r0 + hl - 8, rows), rows) for g in range(2)]
            gate, val = [_conv3(b_refs[g], w_refs[g], taps[g]) for g in range(2)]
            sg = _sigmoid(gate)
            douts = (dav * val * (sg * (1.0 + gate * (1.0 - sg))), dav * (gate * sg))
            for g in range(2):
                dbuf[g, pl.ds(r0, rows), :] = douts[g]
                if count:
                    sums[g, 0] += douts[g].reshape(rows // 8, 8, tc).sum(axis=0)
                    for k in range(3):
                        sums[g, 1 + k] += (douts[g] * taps[g][k]).reshape(rows // 8, 8, tc).sum(axis=0)

        def first(ci, carry):
            r0 = pl.multiple_of(ci * _CHUNK, _CHUNK)
            grads(r0, _CHUNK, da_ref[pl.ds(r0, _CHUNK), :].astype(f32), True)
            return carry

        lax.fori_loop(0, tm // _CHUNK, first, 0)
        grads(tm, hl, dan_ref[...].astype(f32) * keep_next, False)

        def second(ci, carry):
            r0 = pl.multiple_of(ci * _CHUNK, _CHUNK)
            for g in range(2):
                win = _window(dbuf, g, r0, _CHUNK)
                acc = jnp.zeros((_CHUNK, tc), f32)
                for k in range(3):
                    acc = acc + w_refs[g][k:k + 1, :] * _rows_from(win, 2 - k, _CHUNK)
                dup_ref[g, pl.ds(r0, _CHUNK), :] = acc.astype(dup_ref.dtype)
            return carry

        lax.fori_loop(0, tm // _CHUNK, second, 0)
        for g, s_ref in enumerate((sg_ref, sv_ref)):
            for r in range(4):
                s_ref[r:r + 1, :] += jnp.sum(sums[g, r], axis=0, keepdims=True)

    hb = tm // hl
    n_halo = t // hl
    return pl.pallas_call(
        body, name="ffn_gate_bwd", grid=(nc, t // tm),
        in_specs=[pl.BlockSpec((2, tm, tc), lambda j, i: (0, i, j)),
                  pl.BlockSpec((2, hl, tc), lambda j, i: (0, jnp.maximum(i * hb - 1, 0), j)),
                  pl.BlockSpec((2, hl, tc), lambda j, i: (0, jnp.minimum((i + 1) * hb, n_halo - 1), j)),
                  pl.BlockSpec((tm, tc), lambda j, i: (i, j)),
                  pl.BlockSpec((hl, tc), lambda j, i: (jnp.minimum((i + 1) * hb, n_halo - 1), j)),
                  pl.BlockSpec((8, tc), lambda j, i: (0, j)), pl.BlockSpec((8, tc), lambda j, i: (0, nc + j)),
                  pl.BlockSpec((1, tc), lambda j, i: (0, j)), pl.BlockSpec((1, tc), lambda j, i: (0, nc + j))],
        out_specs=[pl.BlockSpec((2, tm, tc), lambda j, i: (0, i, j)),
                   pl.BlockSpec((8, tc), lambda j, i: (0, j)), pl.BlockSpec((8, tc), lambda j, i: (0, j))],
        out_shape=[jax.ShapeDtypeStruct((2, t, f), _ACT), jax.ShapeDtypeStruct((8, f), f32), jax.ShapeDtypeStruct((8, f), f32)],
        scratch_shapes=[pltpu.VMEM((2, hl + tm + hl, tc), f32), pltpu.VMEM((2, tm + hl, tc), f32),
                        pltpu.VMEM((2, 4, 8, tc), f32)],
        compiler_params=_params(2),
    )(up, up, up, da, da, fw, fw, fb, fb)


def _adamw_math(w, g, m, v):
    m = ADAM_B1 * m + (1.0 - ADAM_B1) * g
    v = ADAM_B2 * v + (1.0 - ADAM_B2) * (g * g)
    m_hat = m / (1.0 - ADAM_B1 ** ADAM_STEP)
    v_hat = v / (1.0 - ADAM_B2 ** ADAM_STEP)
    delta = -ADAM_LR * (m_hat / (jnp.sqrt(v_hat) + ADAM_EPS) + ADAM_WD * w)
    return delta, m, v


def _adamw_shard(name, w, g, m, v):
    _, r, c = w.shape
    tr = next((cand for cand in (256, 176, 128, 64, 32, 16, 8) if r % cand == 0), r)

    def body(w_ref, g_ref, m_ref, v_ref, go_ref, d_ref, mo_ref, vo_ref):
        gv = g_ref[...]
        d, mn, vn = _adamw_math(w_ref[...], gv, m_ref[...], v_ref[...])
        go_ref[...] = gv
        d_ref[...] = d
        mo_ref[...] = mn
        vo_ref[...] = vn

    s3 = pl.BlockSpec((None, tr, c), lambda i: (0, i, 0))
    s2 = pl.BlockSpec((tr, c), lambda i: (i, 0))
    shp = jax.ShapeDtypeStruct(w.shape, f32)
    return pl.pallas_call(
        body, name=name, grid=(r // tr,), in_specs=[s3, s2, s3, s3], out_specs=[s3] * 4, out_shape=[shp] * 4,
        compiler_params=_params(1),
    )(w, g, m, v)


def _adamw_small(quads):
    n = len(quads)

    def body(*refs):
        ins, outs = refs[:4 * n], refs[4 * n:]
        for p in range(n):
            w_ref, g_ref, m_ref, v_ref = ins[4 * p:4 * p + 4]
            d, mn, vn = _adamw_math(w_ref[...], g_ref[...], m_ref[...], v_ref[...])
            outs[3 * p][...] = d
            outs[3 * p + 1][...] = mn
            outs[3 * p + 2][...] = vn

    flat = [a for q in quads for a in q]
    shapes = [jax.ShapeDtypeStruct(q[0].shape, f32) for q in quads for _ in range(3)]
    outs = pl.pallas_call(
        body, name="adamw_small", in_specs=[_VMEM] * (4 * n), out_specs=[_VMEM] * (3 * n), out_shape=shapes,
        compiler_params=pltpu.CompilerParams(vmem_limit_bytes=_VMEM_LIMIT_BYTES),
    )(*flat)
    return [tuple(outs[3 * p:3 * p + 3]) for p in range(n)]


def _sum_partials(name, place, grads, got):
    _, r, c = grads.shape
    steps = 4 if r % 64 == 0 else 1
    tr = r // steps

    def body(place_ref, own_ref, got_ref, f_ref):
        s = own_ref[...].astype(f32)
        for k in range(got.shape[0]):
            s = s + got_ref[k].astype(f32)
        f_ref[...] = s

    grid_spec = pltpu.PrefetchScalarGridSpec(
        num_scalar_prefetch=1, grid=(steps,),
        in_specs=[pl.BlockSpec((None, tr, c), lambda i, p: (2 * p[0] + p[1], i, 0)),
                  pl.BlockSpec((got.shape[0], tr, c), lambda i, p: (0, i, 0))],
        out_specs=pl.BlockSpec((None, tr, c), lambda i, p: (p[1], i, 0)))
    return pl.pallas_call(body, name=name, grid_spec=grid_spec, out_shape=jax.ShapeDtypeStruct((2, r, c), f32),
                          compiler_params=_params(1))(place, grads, got)


def _place():
    return lax.axis_index("x"), lax.axis_index("y"), lax.axis_index("c")


def _other_chips(x, y):
    return [(1 - x, y), (x, 1 - y), (1 - x, 1 - y)]


def _remote(src, dst, send_sem, recv_sem, to):
    return pltpu.make_async_remote_copy(src_ref=src, dst_ref=dst, send_sem=send_sem, recv_sem=recv_sem,
                                        device_id=to, device_id_type=_MESH)


def _place_shards(place, shards, col_sharded):
    n = len(shards)
    steps = 4

    def body(place_ref, *refs):
        for src, dst in zip(refs[:n], refs[n:]):
            dst[...] = src[...].astype(dst.dtype)

    in_specs, out_specs, out_shape = [], [], []
    for w, col in zip(shards, col_sharded):
        r, cs = w.shape
        tr = r // steps
        in_specs.append(pl.BlockSpec((tr, cs), lambda i, p: (i, 0)))
        if col:
            out_specs.append(pl.BlockSpec((tr, cs), lambda i, p: (i, p[0])))
            out_shape.append(jax.ShapeDtypeStruct((r, 4 * cs), _ACT))
        else:
            out_specs.append(pl.BlockSpec((tr, cs), lambda i, p: (p[0] * steps + i, 0)))
            out_shape.append(jax.ShapeDtypeStruct((4 * r, cs), _ACT))
    grid_spec = pltpu.PrefetchScalarGridSpec(num_scalar_prefetch=1, grid=(steps,), in_specs=in_specs, out_specs=out_specs)
    return pl.pallas_call(body, name="place_shards", grid_spec=grid_spec, out_shape=out_shape,
                          compiler_params=_params(1))(place, *shards)


def _shard_of(ref, col_sharded, s):
    rows, cols = ref.shape
    if col_sharded:
        return ref.at[:, pl.ds(s * (cols // 4), cols // 4)]
    return ref.at[pl.ds(s * (rows // 4), rows // 4), :]


def _part_of(ref, col_sharded, whole, s, h):
    if whole:
        return _shard_of(ref, col_sharded, s)
    rows, cols = ref.shape
    if col_sharded:
        return ref.at[pl.ds(h * (rows // 2), rows // 2), pl.ds(s * (cols // 4), cols // 4)]
    return ref.at[pl.ds((2 * s + h) * (rows // 8), rows // 8), :]


def _allgather_start(bufs, col_sharded, whole, groups):
    n = len(bufs)
    ng = len(groups)

    def body(*refs):
        out = refs[n:2 * n]
        sems = refs[2 * n:]
        x, y, c = _place()
        for g, members in enumerate(groups):
            for i, w in enumerate(members):
                mine = _part_of(out[w], col_sharded[w], whole[w], 2 * x + y, c)
                for j, chip in enumerate(_other_chips(x, y)):
                    _remote(mine, mine, sems[2 * g].at[3 * i + j], sems[2 * g + 1].at[3 * i + j], (*chip, c)).start()

    sem_shapes = [pltpu.SemaphoreType.DMA((3 * len(m),)) for m in groups for _ in range(2)]
    outs = pl.pallas_call(
        body, name="allgather_start", in_specs=[_HBM] * n, out_specs=[_HBM] * n + [_SEM] * (2 * ng),
        out_shape=[pltpu.HBM(b.shape, b.dtype) for b in bufs] + sem_shapes,
        input_output_aliases={i: i for i in range(n)},
        compiler_params=pltpu.CompilerParams(has_side_effects=_EFFECT),
    )(*[pltpu.with_memory_space_constraint(b, pltpu.HBM) for b in bufs])
    return list(outs[:n]), [(outs[n + 2 * g], outs[n + 2 * g + 1]) for g in range(ng)]


def _allgather_relay(name, bufs, col_sharded, whole, sems, after):
    n = len(bufs)

    def body(*refs):
        buf = refs[:n]
        send, recv = refs[n], refs[n + 1]
        out = refs[n + 3:2 * n + 3]
        to_sibling, from_sibling = refs[2 * n + 3:]
        x, y, c = _place()
        for i in range(n):
            mine = _part_of(buf[i], col_sharded[i], whole[i], 2 * x + y, c)
            for j, chip in enumerate(_other_chips(x, y)):
                landed = _part_of(buf[i], col_sharded[i], whole[i], 2 * chip[0] + chip[1], c)
                cp = _remote(mine, landed, send.at[3 * i + j], recv.at[3 * i + j], (*chip, c))
                cp.wait_send()
                cp.wait_recv()
        for i in range(n):
            if not whole[i]:
                for j, chip in enumerate(_other_chips(x, y)):
                    landed = _part_of(out[i], col_sharded[i], False, 2 * chip[0] + chip[1], c)
                    _remote(landed, landed, to_sibling.at[3 * i + j], from_sibling.at[3 * i + j], (x, y, 1 - c)).start()

    outs = pl.pallas_call(
        body, name=name, in_specs=[_HBM] * n + [_SEM, _SEM, _ANY], out_specs=[_HBM] * n + [_SEM, _SEM],
        out_shape=[pltpu.HBM(b.shape, b.dtype) for b in bufs] + [pltpu.SemaphoreType.DMA((3 * n,))] * 2,
        input_output_aliases={i: i for i in range(n)},
        compiler_params=pltpu.CompilerParams(has_side_effects=_EFFECT),
    )(*bufs, *sems, after)
    return list(outs[:n]), (outs[n], outs[n + 1])


def _allgather_wait(name, bufs, col_sharded, whole, sems, after):
    n = len(bufs)

    def body(*refs):
        buf = refs[:n]
        to_sibling, from_sibling = refs[n], refs[n + 1]
        x, y, c = _place()
        for i in range(n):
            if not whole[i]:
                for j, chip in enumerate(_other_chips(x, y)):
                    sent = _part_of(buf[i], col_sharded[i], False, 2 * chip[0] + chip[1], c)
                    landed = _part_of(buf[i], col_sharded[i], False, 2 * chip[0] + chip[1], 1 - c)
                    cp = _remote(sent, landed, to_sibling.at[3 * i + j], from_sibling.at[3 * i + j], (x, y, 1 - c))
                    cp.wait_send()
                    cp.wait_recv()

    return pl.pallas_call(
        body, name=name, in_specs=[_HBM] * n + [_SEM, _SEM, _ANY], out_specs=[_HBM] * n,
        out_shape=[pltpu.HBM(b.shape, b.dtype) for b in bufs],
        input_output_aliases={i: i for i in range(n)},
        compiler_params=pltpu.CompilerParams(has_side_effects=_EFFECT),
    )(*bufs, *sems, after)


def _other_devices(x, y, c):
    flips = [(bx, by, bc) for bx in (0, 1) for by in (0, 1) for bc in (0, 1)][1:]
    return [(1 - x if bx else x, 1 - y if by else y, 1 - c if bc else c) for bx, by, bc in flips]


def _grad_exchange_start(name, grads):
    nw = len(grads)
    lands = [lax.empty((7,) + g.shape[1:], g.dtype) for g in grads]

    def body(*refs):
        src = refs[2 * nw:3 * nw]
        got = refs[3 * nw:4 * nw]
        send, recv, token = refs[4 * nw:]
        x, y, c = _place()
        for w in range(nw):
            for k, (px, py, pc) in enumerate(_other_devices(x, y, c)):
                _remote(src[w].at[4 * px + 2 * py + pc], got[w].at[k], send.at[7 * w + k], recv.at[7 * w + k], (px, py, pc)).start()
        token[...] = jnp.zeros_like(token)

    outs = pl.pallas_call(
        body, name=name, in_specs=[_HBM] * (2 * nw), out_specs=[_HBM] * (2 * nw) + [_SEM, _SEM, _VMEM],
        out_shape=[pltpu.HBM(a.shape, a.dtype) for a in list(grads) + lands]
        + [pltpu.SemaphoreType.DMA((7 * nw,)), pltpu.SemaphoreType.DMA((7 * nw,)), jax.ShapeDtypeStruct((8, 128), f32)],
        input_output_aliases={i: i for i in range(2 * nw)},
        compiler_params=pltpu.CompilerParams(has_side_effects=_EFFECT),
    )(*[pltpu.with_memory_space_constraint(a, pltpu.HBM) for a in list(grads) + lands])
    return list(outs[:nw]), list(outs[nw:2 * nw]), (outs[2 * nw], outs[2 * nw + 1]), outs[2 * nw + 2]


def _grad_exchange_wait(name, grads, got, sems, after):
    nw = len(grads)

    def body(*refs):
        src = refs[:nw]
        land = refs[nw:2 * nw]
        send, recv = refs[2 * nw], refs[2 * nw + 1]
        x, y, c = _place()
        for w in range(nw):
            for k, (px, py, pc) in enumerate(_other_devices(x, y, c)):
                cp = _remote(src[w].at[4 * px + 2 * py + pc], land[w].at[k], send.at[7 * w + k], recv.at[7 * w + k], (px, py, pc))
                cp.wait_send()
                cp.wait_recv()

    outs = pl.pallas_call(
        body, name=name, in_specs=[_HBM] * (2 * nw) + [_SEM, _SEM, _ANY], out_specs=[_HBM] * (2 * nw),
        out_shape=[pltpu.HBM(a.shape, a.dtype) for a in list(grads) + list(got)],
        input_output_aliases={i: i for i in range(2 * nw)},
        compiler_params=pltpu.CompilerParams(has_side_effects=_EFFECT),
    )(*grads, *got, *sems, after)
    return list(outs[:nw]), list(outs[nw:])


def _swap_halves(finals):
    nw = len(finals)

    def body(*refs):
        buf = refs[nw:2 * nw]
        send_sem, recv_sem = refs[2 * nw:]
        x, y, c = _place()
        sends = []
        for w in range(nw):
            rc = _remote(buf[w].at[c], buf[w].at[c], send_sem.at[w], recv_sem.at[w], (x, y, 1 - c))
            rc.start()
            sends.append(rc)
        for w in range(nw):
            _remote(buf[w].at[1 - c], buf[w].at[1 - c], send_sem.at[w], recv_sem.at[w], (x, y, c)).wait_recv()
        for rc in sends:
            rc.wait_send()

    return pl.pallas_call(
        body, name="rs_swap_halves", in_specs=[_ANY] * nw, out_specs=[_ANY] * nw,
        out_shape=[jax.ShapeDtypeStruct(g.shape, g.dtype) for g in finals],
        input_output_aliases={i: i for i in range(nw)},
        scratch_shapes=[pltpu.SemaphoreType.DMA((nw,)), pltpu.SemaphoreType.DMA((nw,))],
    )(*finals)


def _half_slices(shape, h):
    rows, cols = shape
    if cols % 256 == 0:
        return (slice(None), slice(h * (cols // 2), (h + 1) * (cols // 2)))
    return (slice(h * (rows // 2), (h + 1) * (rows // 2)), slice(None))


def _allreduce_small(parts):
    n = len(parts)

    def body(*refs):
        src = refs[:n]
        out = refs[n:2 * n]
        sib = refs[2 * n:3 * n]
        chip_sum = refs[3 * n:4 * n]
        slots = refs[4 * n:5 * n]
        pair_send, pair_recv, ici_send, ici_recv, swap_send, swap_recv = refs[5 * n:]
        x, y, c = _place()
        me_chip = 2 * x + y
        chips = _other_chips(x, y)
        pairs = [_remote(src[a], sib[a], pair_send.at[a], pair_recv.at[a], (x, y, 1 - c)) for a in range(n)]
        for rc in pairs:
            rc.start()
        for a in range(n):
            pairs[a].wait_recv()
            chip_sum[a][...] = src[a][...] + sib[a][...]
        for h in (0, 1):
            @pl.when(c == h)
            def _():
                sends = []
                for a in range(n):
                    idx = _half_slices(parts[a].shape, h)
                    for j, chip in enumerate(chips):
                        rc = _remote(chip_sum[a].at[idx], slots[a].at[me_chip].at[idx], ici_send.at[3 * a + j], ici_recv.at[3 * a + j], (*chip, h))
                        rc.start()
                        sends.append(rc)
                    slots[a][(me_chip,) + idx] = chip_sum[a][idx]
                for a in range(n):
                    idx = _half_slices(parts[a].shape, h)
                    for j, chip in enumerate(chips):
                        landed = slots[a].at[2 * chip[0] + chip[1]].at[idx]
                        _remote(landed, landed, ici_send.at[3 * a + j], ici_recv.at[3 * a + j], (x, y, c)).wait_recv()
                    total = slots[a][(0,) + idx]
                    for s in range(1, 4):
                        total = total + slots[a][(s,) + idx]
                    out[a][idx] = total
                    rc = _remote(out[a].at[idx], out[a].at[idx], swap_send.at[a], swap_recv.at[a], (x, y, 1 - h))
                    rc.start()
                    sends.append(rc)
                for a in range(n):
                    other = out[a].at[_half_slices(parts[a].shape, 1 - h)]
                    _remote(other, other, swap_send.at[a], swap_recv.at[a], (x, y, c)).wait_recv()
                for rc in sends:
                    rc.wait_send()
        for rc in pairs:
            rc.wait_send()

    return pl.pallas_call(
        body, name="allreduce_small", in_specs=[_VMEM] * n, out_specs=[_VMEM] * n,
        out_shape=[jax.ShapeDtypeStruct(p.shape, f32) for p in parts],
        scratch_shapes=[pltpu.VMEM(p.shape, f32) for p in parts] * 2 + [pltpu.VMEM((4,) + p.shape, f32) for p in parts]
        + [pltpu.SemaphoreType.DMA((n,)), pltpu.SemaphoreType.DMA((n,)), pltpu.SemaphoreType.DMA((3 * n,)),
           pltpu.SemaphoreType.DMA((3 * n,)), pltpu.SemaphoreType.DMA((n,)), pltpu.SemaphoreType.DMA((n,))],
        compiler_params=pltpu.CompilerParams(vmem_limit_bytes=_VMEM_LIMIT_BYTES),
    )(*parts)


def _local_step(x, mem, tgt, g_mix, g_xattn, g_mem, g_ffn, g_final, cb, lg, lb, pw, ps, fb, relay, weights, reduce, n_seq, seq, n_mem):
    t, d = x.shape
    f = fb.shape[1] // 2
    c = cb.shape[1]
    h1 = _rms_fwd("norm_mix", x, g_mix)
    relay(0, h1)
    w_in, cw, fw = weights(0, h1)
    u = _mm_nn("proj_in", h1, w_in, _ACT, w_in.shape[1])
    y, hc = _mix_fwd(u, cw, cb, lg, lb, pw, ps, seq)
    relay(1, y)
    w_out, w_q, w_kv, w_o = weights(1, y)
    x1, h2 = _proj_residual_norm("proj_out", y, w_out, x, g_xattn)
    q = _mm_nn("proj_q", h2, w_q, _ACT, d)
    mem_n = _rms_fwd("norm_mem", mem, g_mem)
    kv = _mm_nn("proj_kv", mem_n, w_kv, _ACT, 2 * d)
    o = _attn_fwd(q, kv, n_seq, seq, n_mem)
    relay(2, o)
    x2, h3 = _proj_residual_norm("proj_o", o, w_o, x1, g_ffn)
    w_up, w_down = weights(2, h3)
    up = _mm_nn("proj_up", h3, w_up, _ACT, f, split_out=True)
    a = _ffn_gate_fwd(up, fw, fb, seq)
    dx3, dx3b, dg_final, loss = _proj_loss_bwd("proj_down", a, w_down, x2, g_final, tgt)
    da = _mm_nt("d_act", dx3b, w_down, _ACT)
    gw_down = _mm_tn_rows("dw_down", a, dx3b, f // 2, d // 2)
    dup, sums_g, sums_v = _ffn_gate_bwd(up, da, fw, fb, seq)
    gw_up = _mm_tn_pieces("dw_up", h3, dup, f // 2, t)
    token = reduce(0, [gw_down.reshape(8, -1, d), gw_up])
    dx2, dx2b, dg_ffn = _dproj_rms_bwd("d_h3", dup, w_up, x2, g_ffn + token, dx3)
    do = _mm_nt("d_o", dx2b, w_o, _ACT)
    gw_o = _mm_tn_rows("dw_o", o, dx2b, d, d // 2)
    dq, dkv = _attn_bwd(q, kv, do, n_seq, seq, n_mem)
    gw_q = _mm_tn_rows("dw_q", h2, dq, d, d // 2)
    gw_kv = _mm_tn_pieces("dw_kv", mem_n, dkv, d // 2, mem.shape[0])
    dmem_n = _mm_nt("d_mem_n", dkv, w_kv, f32)
    dg_mem = _rms_gain_grad("norm_mem_bwd", mem, dmem_n)
    dx1, dx1b, dg_xattn = _dproj_rms_bwd("d_h2", dq, w_q, x1, g_xattn, dx2)
    dy = _mm_nt("d_y", dx1b, w_out, _ACT)
    gw_out = _mm_tn_rows("dw_out", y, dx1b, d, d // 2)
    token = reduce(1, [gw_o.reshape(8, -1, d), gw_q.reshape(8, -1, d), gw_kv, gw_out.reshape(8, -1, d)])
    dhc, sums_norm = _mix_bwd_norm(hc, dy, lg + token, lb, seq)
    du, d_cw, d_ps, d_pw = _mix_bwd_taps(u, dhc, dy, cw, pw, ps, seq)
    gw_in = _mm_tn_pieces("dw_in", h1, du, c * 3 // 4, t)
    token = reduce(2, [gw_in])
    grad_x, dg_mix = _dproj_rms_bwd("d_h1", du, w_in, x, g_mix + token, dx1, storage_copy=False)
    zero_row = jnp.zeros((1, d), f32)
    gains = jnp.concatenate([dg_mix, dg_xattn, dg_mem, dg_ffn, dg_final, jnp.pad(loss, ((0, 0), (0, d - 1))), zero_row, zero_row], axis=0)
    conv_rows = jnp.concatenate([sums_norm[2:3], sums_norm[0:1], sums_norm[1:2], d_ps[0:1], jnp.zeros((4, c), f32)], axis=0)
    ffn_rows = jnp.concatenate([sums_g, sums_v], axis=1)
    small = [gains, conv_rows, d_pw.reshape(-1, d_pw.shape[-1]), ffn_rows, d_cw]
    return grad_x, small


def kernel(x, mem, norm_mix_g, w_in, conv_dw_w, conv_dw_b, conv_ln_g, conv_ln_b, pool_w, pool_scale, w_out, norm_xattn_g, norm_mem_g, w_q, w_kv, w_o, norm_ffn_g, w_up, ffn_dw_w, ffn_dw_b, w_down, norm_final_g, loss_target, m_norm_mix_g, m_w_in, m_conv_dw_w, m_conv_dw_b, m_conv_ln_g, m_conv_ln_b, m_pool_w, m_pool_scale, m_w_out, m_norm_xattn_g, m_norm_mem_g, m_w_q, m_w_kv, m_w_o, m_norm_ffn_g, m_w_up, m_ffn_dw_w, m_ffn_dw_b, m_w_down, m_norm_final_g, v_norm_mix_g, v_w_in, v_conv_dw_w, v_conv_dw_b, v_conv_ln_g, v_conv_ln_b, v_pool_w, v_pool_scale, v_w_out, v_norm_xattn_g, v_norm_mem_g, v_w_q, v_w_kv, v_w_o, v_norm_ffn_g, v_w_up, v_ffn_dw_w, v_ffn_dw_b, v_w_down, v_norm_final_g):
    n_seq, seq, d = x.shape
    n_mem = mem.shape[1]
    chip = 2 * lax.axis_index("x") + lax.axis_index("y")

    place = jnp.stack([chip, lax.axis_index("c")]).astype(jnp.int32)

    col_w = [w_in, w_kv, w_up]
    row_w = [w_out, w_q, w_o, w_down]
    col_flags = [True] * 3 + [False] * 4 + [True] * 2
    kw = conv_dw_w.shape[1]

    def padded_in_place(shard, rows):
        full = jnp.zeros((rows, 4 * shard.shape[1]), shard.dtype)
        return lax.dynamic_update_slice(full, shard, (0, chip * shard.shape[1]))

    bufs = list(_place_shards(place, [w[0] for w in col_w + row_w], col_flags[:7]))
    bufs += [padded_in_place(conv_dw_w[0], _HALO), padded_in_place(ffn_dw_w[0], 8)]
    groups = [[0, 7, 8], [3, 4, 1, 5], [2, 6]]
    whole = [False] * 7 + [True] * 2
    bufs, sems = _allgather_start(bufs, col_flags, whole, groups)
    relayed = {}

    def relay(g, after):
        members = groups[g]
        relayed[g] = _allgather_relay("allgather_relay_%d" % g, [bufs[i] for i in members], [col_flags[i] for i in members],
                                      [whole[i] for i in members], sems[g], after)

    def weights(g, after):
        members = groups[g]
        group_bufs, sibling_sems = relayed[g]
        return _allgather_wait("allgather_wait_%d" % g, group_bufs, [col_flags[i] for i in members],
                               [whole[i] for i in members], sibling_sems, after)

    names = ["w_in", "w_kv", "w_up", "w_out", "w_q", "w_o", "w_down"]
    reduce_groups = [["w_down", "w_up"], ["w_o", "w_q", "w_kv", "w_out"], ["w_in"]]
    in_flight = {}

    def reduce(g, grads):
        grads, lands, rs_sems, token = _grad_exchange_start("rs_start_%d" % g, grads)
        in_flight[g] = (grads, lands, rs_sems)
        return token[0:1, 0:1]

    grad_x, small = _local_step(
        x.reshape(n_seq * seq, d), mem.reshape(n_seq * n_mem, d), loss_target.reshape(n_seq * seq, d),
        norm_mix_g, norm_xattn_g, norm_mem_g, norm_ffn_g, norm_final_g.reshape(1, d),
        conv_dw_b, conv_ln_g, conv_ln_b, pool_w[0], pool_scale, ffn_dw_b, relay, weights, reduce, n_seq, seq, n_mem)

    finals = {}
    for g, members in enumerate(reduce_groups):
        grads, lands, rs_sems = in_flight[g]
        grads, lands = _grad_exchange_wait("rs_wait_%d" % g, grads, lands, rs_sems, grad_x)
        for n, a, b in zip(members, grads, lands):
            finals[n] = _sum_partials("rs_sum_" + n, place, a, b)
    shard_grads = _swap_halves([finals[n] for n in names])

    gains, conv_rows, d_pw, ffn_rows, d_cw = _allreduce_small(small)
    loss = gains[5, 0]

    outs = {}
    big_w = dict(zip(names, col_w + row_w))
    big_m = dict(w_in=m_w_in, w_kv=m_w_kv, w_up=m_w_up, w_out=m_w_out, w_q=m_w_q, w_o=m_w_o, w_down=m_w_down)
    big_v = dict(w_in=v_w_in, w_kv=v_w_kv, w_up=v_w_up, w_out=v_w_out, w_q=v_w_q, w_o=v_w_o, w_down=v_w_down)
    for n, g in zip(names, shard_grads):
        w = big_w[n]
        g2 = g.reshape(w.shape[1], w.shape[2])
        outs[n] = tuple(_adamw_shard("adamw_" + n, w, g2, big_m[n], big_v[n]))

    f2 = ffn_dw_b.shape[1]
    cs_c = conv_dw_w.shape[2]
    cs_f = ffn_dw_w.shape[2]
    g_cw = lax.dynamic_slice(d_cw, (0, chip * cs_c), (kw, cs_c)).reshape(conv_dw_w.shape)
    g_fw = lax.dynamic_slice(ffn_rows, (1, chip * cs_f), (ffn_dw_w.shape[1], cs_f)).reshape(ffn_dw_w.shape)
    small_params = [
        ("norm_mix_g", norm_mix_g, gains[0:1], m_norm_mix_g, v_norm_mix_g),
        ("conv_dw_w", conv_dw_w, g_cw, m_conv_dw_w, v_conv_dw_w),
        ("conv_dw_b", conv_dw_b, conv_rows[0:1], m_conv_dw_b, v_conv_dw_b),
        ("conv_ln_g", conv_ln_g, conv_rows[1:2], m_conv_ln_g, v_conv_ln_g),
        ("conv_ln_b", conv_ln_b, conv_rows[2:3], m_conv_ln_b, v_conv_ln_b),
        ("pool_w", pool_w, d_pw.reshape(pool_w.shape), m_pool_w, v_pool_w),
        ("pool_scale", pool_scale, conv_rows[3:4], m_pool_scale, v_pool_scale),
        ("norm_xattn_g", norm_xattn_g, gains[1:2], m_norm_xattn_g, v_norm_xattn_g),
        ("norm_mem_g", norm_mem_g, gains[2:3], m_norm_mem_g, v_norm_mem_g),
        ("norm_ffn_g", norm_ffn_g, gains[3:4], m_norm_ffn_g, v_norm_ffn_g),
        ("ffn_dw_w", ffn_dw_w, g_fw, m_ffn_dw_w, v_ffn_dw_w),
        ("ffn_dw_b", ffn_dw_b, ffn_rows[0:1, :f2], m_ffn_dw_b, v_ffn_dw_b),
        ("norm_final_g", norm_final_g.reshape(1, d), gains[4:5], m_norm_final_g.reshape(1, d), v_norm_final_g.reshape(1, d)),
    ]
    quads = []
    for _, w, g, m, v in small_params:
        shape2 = (-1, w.shape[-1])
        quads.append((w.reshape(shape2), g.reshape(shape2), m.reshape(shape2), v.reshape(shape2)))
    for (n, w, g, _, _), (delta, new_m, new_v) in zip(small_params, _adamw_small(quads)):
        shape = norm_final_g.shape if n == "norm_final_g" else w.shape
        outs[n] = (g.reshape(shape), delta.reshape(shape), new_m.reshape(shape), new_v.reshape(shape))

    order = ["norm_mix_g", "w_in", "conv_dw_w", "conv_dw_b", "conv_ln_g", "conv_ln_b", "pool_w", "pool_scale", "w_out",
             "norm_xattn_g", "norm_mem_g", "w_q", "w_kv", "w_o", "norm_ffn_g", "w_up", "ffn_dw_w", "ffn_dw_b", "w_down",
             "norm_final_g"]
    return (loss, grad_x.reshape(x.shape), *[outs[n][0] for n in order], *[outs[n][1] for n in order],
            *[outs[n][2] for n in order], *[outs[n][3] for n in order])
```

```python
import functools

import jax
import jax.numpy as jnp
from jax import lax
from jax.experimental import pallas as pl
from jax.experimental.pallas import tpu as pltpu

f32 = jnp.float32
_ACT = jnp.bfloat16

EPS = 1e-6
POOL_WINDOWS = (2, 4, 8, 16)
XATTN_HEADS = 4
ADAM_LR = 0.001
ADAM_B1 = 0.9
ADAM_B2 = 0.999
ADAM_EPS = 1e-08
ADAM_WD = 0.01
ADAM_STEP = 10

_VMEM_LIMIT_BYTES = 56 * 1024 * 1024
_MESH = pl.DeviceIdType.MESH
_ANY = pl.BlockSpec(memory_space=pl.ANY)
_VMEM = pl.BlockSpec(memory_space=pltpu.VMEM)
_HBM = pl.BlockSpec(memory_space=pltpu.HBM)
_SEM = pl.BlockSpec(memory_space=pltpu.SEMAPHORE)
_EFFECT = pltpu.SideEffectType.DATAFLOW_SIDE_EFFECTING

_NN = (((1,), (0,)), ((), ()))
_NT = (((1,), (1,)), ((), ()))
_TN = (((0,), (0,)), ((), ()))


def _params(n_grid):
    return pltpu.CompilerParams(dimension_semantics=("arbitrary",) * n_grid, vmem_limit_bytes=_VMEM_LIMIT_BYTES)


def _sigmoid(v):
    return 1.0 / (1.0 + jnp.exp(-v))


def _dot(a, b, dims):
    return lax.dot_general(a, b, dims, preferred_element_type=f32)


def _mm(name, a, b, *, dims, grid, a_spec, b_spec, o_spec, out_shape, nk, acc_shape=None, res=None, res_spec=None):
    def body(*refs):
        if res is None:
            a_ref, b_ref, o_ref, *scratch = refs
            r_ref = None
        else:
            a_ref, b_ref, r_ref, o_ref, *scratch = refs
        p = _dot(a_ref[...], b_ref[...], dims)

        def finish(v):
            if r_ref is not None:
                v = v + r_ref[...]
            o_ref[...] = v.astype(o_ref.dtype)

        if nk == 1:
            finish(p)
        else:
            acc = scratch[0]
            k = pl.program_id(2)

            @pl.when(k == 0)
            def _():
                acc[...] = p

            @pl.when(k > 0)
            def _():
                acc[...] += p

            @pl.when(k == nk - 1)
            def _():
                finish(acc[...])

    ins = [a, b] + ([] if res is None else [res])
    specs = [a_spec, b_spec] + ([] if res is None else [res_spec])
    return pl.pallas_call(
        body, name=name, grid=grid, in_specs=specs, out_specs=o_spec, out_shape=out_shape,
        scratch_shapes=[pltpu.VMEM(acc_shape, f32)] if nk > 1 else [], compiler_params=_params(3),
    )(*ins)


_NARROW = 1536


def _row_tile(m, width=_NARROW + 1):
    return min(1024 if width <= _NARROW else 512, m)


def _mm_nn(name, a, b, out_dtype, tn, res=None, split_out=False):
    m, k = a.shape
    n = b.shape[1]
    tm = _row_tile(m, max(k, tn))
    if split_out:
        out_shape = jax.ShapeDtypeStruct((n // tn, m, tn), out_dtype)
        o_spec = pl.BlockSpec((None, tm, tn), lambda j, i, kk: (j, i, 0))
    else:
        out_shape = jax.ShapeDtypeStruct((m, n), out_dtype)
        o_spec = pl.BlockSpec((tm, tn), lambda j, i, kk: (i, j))
    return _mm(
        name, a, b, dims=_NN, grid=(n // tn, m // tm, 1), nk=1,
        a_spec=pl.BlockSpec((tm, k), lambda j, i, kk: (i, 0)),
        b_spec=pl.BlockSpec((k, tn), lambda j, i, kk: (0, j)),
        o_spec=o_spec, out_shape=out_shape, res=res,
        res_spec=pl.BlockSpec((tm, tn), lambda j, i, kk: (i, j)),
    )


def _mm_nt(name, a, b, out_dtype):
    n, kc = b.shape
    m = a.shape[0]
    tm = _row_tile(m, max(n, kc))
    return _mm(
        name, a, b, dims=_NT, grid=(m // tm, 1, 1), nk=1,
        a_spec=pl.BlockSpec((tm, kc), lambda i, j, k: (i, 0)), b_spec=pl.BlockSpec((n, kc), lambda i, j, k: (0, 0)),
        o_spec=pl.BlockSpec((tm, n), lambda i, j, k: (i, 0)),
        out_shape=jax.ShapeDtypeStruct((m, n), out_dtype),
    )


def _mm_tn_rows(name, a, b, tka, tn):
    m, ka = a.shape
    nb = b.shape[1]
    return _mm(
        name, a, b, dims=_TN, grid=(ka // tka, nb // tn, 1), nk=1,
        a_spec=pl.BlockSpec((m, tka), lambda i, j, k: (0, i)),
        b_spec=pl.BlockSpec((m, tn), lambda i, j, k: (0, j)),
        o_spec=pl.BlockSpec((tka, tn), lambda i, j, k: (i, j)),
        out_shape=jax.ShapeDtypeStruct((ka, nb), _ACT),
    )


def _mm_tn_pieces(name, a, b, cs, tt):
    m, ka = a.shape
    nk = m // tt
    if b.ndim == 3:
        b_spec = pl.BlockSpec((None, tt, cs), lambda i, j, k: (j // 2, k, j % 2))
    else:
        b_spec = pl.BlockSpec((tt, cs), lambda i, j, k: (k, j))
    return _mm(
        name, a, b, dims=_TN, grid=(2, 4, nk), nk=nk, acc_shape=(ka // 2, cs),
        a_spec=pl.BlockSpec((tt, ka // 2), lambda i, j, k: (k, i)), b_spec=b_spec,
        o_spec=pl.BlockSpec((None, ka // 2, cs), lambda i, j, k: (2 * j + i, 0, 0)),
        out_shape=jax.ShapeDtypeStruct((8, ka // 2, cs), _ACT),
    )


def _rms_fwd(name, x, g):
    t, d = x.shape
    tm = _row_tile(t, d)

    def body(x_ref, g_ref, h_ref):
        xv = x_ref[...]
        r = lax.rsqrt(jnp.mean(xv * xv, axis=-1, keepdims=True) + EPS)
        h_ref[...] = (xv * r * g_ref[...]).astype(h_ref.dtype)

    return pl.pallas_call(
        body, name=name, grid=(t // tm,),
        in_specs=[pl.BlockSpec((tm, d), lambda i: (i, 0)), pl.BlockSpec((1, d), lambda i: (0, 0))],
        out_specs=pl.BlockSpec((tm, d), lambda i: (i, 0)), out_shape=jax.ShapeDtypeStruct((t, d), _ACT),
        compiler_params=_params(1),
    )(x, g)


def _fused_rows(name, a, b, product, a_spec, tm, extras, extra_specs, out_shape, out_specs, epilogue):
    ne = len(extras)

    def body(a_ref, b_ref, *refs):
        epilogue(product(a_ref, b_ref), refs[:ne], refs[ne:])

    m = extras[0].shape[0]
    return pl.pallas_call(
        body, name=name, grid=(m // tm,),
        in_specs=[a_spec, pl.BlockSpec(b.shape, lambda i: (0, 0)), *extra_specs], out_specs=out_specs, out_shape=out_shape,
        compiler_params=_params(1),
    )(a, b, *extras)


def _proj_residual_norm(name, a, b, res, g):
    m, k = a.shape
    d = b.shape[1]
    tm = _row_tile(m, max(k, d))

    def epilogue(p, ins, outs):
        xv = p + ins[0][...]
        outs[0][...] = xv
        r = lax.rsqrt(jnp.mean(xv * xv, axis=-1, keepdims=True) + EPS)
        outs[1][...] = (xv * r * ins[1][...]).astype(outs[1].dtype)

    row = pl.BlockSpec((tm, d), lambda i: (i, 0))
    return _fused_rows(
        name, a, b, lambda a_ref, b_ref: _dot(a_ref[...], b_ref[...], _NN), pl.BlockSpec((tm, k), lambda i: (i, 0)), tm,
        [res, g], [row, pl.BlockSpec((1, d), lambda i: (0, 0))],
        [jax.ShapeDtypeStruct((m, d), f32), jax.ShapeDtypeStruct((m, d), _ACT)], [row, row], epilogue)


def _dproj_rms_bwd(name, a, b, x, g, dres, storage_copy=True):
    m, d = x.shape
    if a.ndim == 3:
        nh, _, kh = a.shape
        tm = min(256, m)
        a_spec = pl.BlockSpec((nh, tm, kh), lambda i: (0, i, 0))

        def product(a_ref, b_ref):
            p = _dot(a_ref[0], b_ref[:, 0:kh], _NT)
            for h in range(1, nh):
                p = p + _dot(a_ref[h], b_ref[:, h * kh:(h + 1) * kh], _NT)
            return p
    else:
        tm = _row_tile(m, max(a.shape[1], d))
        a_spec = pl.BlockSpec((tm, a.shape[1]), lambda i: (i, 0))

        def product(a_ref, b_ref):
            return _dot(a_ref[...], b_ref[...], _NT)

    def epilogue(dhv, ins, outs):
        x_ref, g_ref, dres_ref = ins
        dg_ref = outs[-1]

        @pl.when(pl.program_id(0) == 0)
        def _():
            dg_ref[...] = jnp.zeros_like(dg_ref)

        xv = x_ref[...]
        r = lax.rsqrt(jnp.mean(xv * xv, axis=-1, keepdims=True) + EPS)
        xn = xv * r
        dxn = dhv * g_ref[...]
        dx = r * (dxn - xn * jnp.mean(dxn * xn, axis=-1, keepdims=True)) + dres_ref[...]
        outs[0][...] = dx
        if storage_copy:
            outs[1][...] = dx.astype(outs[1].dtype)
        dg_ref[...] += jnp.sum(dhv * xn, axis=0, keepdims=True)

    row = pl.BlockSpec((tm, d), lambda i: (i, 0))
    vec = pl.BlockSpec((1, d), lambda i: (0, 0))
    copies = [jax.ShapeDtypeStruct((m, d), _ACT)] if storage_copy else []
    return _fused_rows(
        name, a, b, product, a_spec, tm, [x, g, dres], [row, vec, row],
        [jax.ShapeDtypeStruct((m, d), f32)] + copies + [jax.ShapeDtypeStruct((1, d), f32)],
        [row] * (1 + len(copies)) + [vec], epilogue)


def _proj_loss_bwd(name, a, b, res, g, tgt):
    m, k = a.shape
    d = b.shape[1]
    tm = _row_tile(m)

    def epilogue(p, ins, outs):
        res_ref, g_ref, t_ref = ins
        dx_ref, dxb_ref, dg_ref, loss_ref = outs

        @pl.when(pl.program_id(0) == 0)
        def _():
            dg_ref[...] = jnp.zeros_like(dg_ref)
            loss_ref[...] = jnp.zeros_like(loss_ref)

        xv = p + res_ref[...]
        gv = g_ref[...]
        r = lax.rsqrt(jnp.mean(xv * xv, axis=-1, keepdims=True) + EPS)
        xn = xv * r
        err = xn * gv - t_ref[...]
        loss_ref[...] += 0.5 * jnp.sum(jnp.mean(err * err, axis=-1, keepdims=True), axis=0, keepdims=True)
        dout = err * (1.0 / d)
        dxn = dout * gv
        dx = r * (dxn - xn * jnp.mean(dxn * xn, axis=-1, keepdims=True))
        dx_ref[...] = dx
        dxb_ref[...] = dx.astype(dxb_ref.dtype)
        dg_ref[...] += jnp.sum(dout * xn, axis=0, keepdims=True)

    row = pl.BlockSpec((tm, d), lambda i: (i, 0))
    vec = pl.BlockSpec((1, d), lambda i: (0, 0))
    return _fused_rows(
        name, a, b, lambda a_ref, b_ref: _dot(a_ref[...], b_ref[...], _NN), pl.BlockSpec((tm, k), lambda i: (i, 0)), tm,
        [res, g, tgt], [row, vec, row],
        [jax.ShapeDtypeStruct((m, d), f32), jax.ShapeDtypeStruct((m, d), _ACT), jax.ShapeDtypeStruct((1, d), f32),
         jax.ShapeDtypeStruct((1, 1), f32)],
        [row, row, vec, pl.BlockSpec((1, 1), lambda i: (0, 0))], epilogue)


def _rms_gain_grad(name, x, dh):
    t, d = x.shape
    tm = _row_tile(t)

    def body(x_ref, dh_ref, dg_ref):
        @pl.when(pl.program_id(0) == 0)
        def _():
            dg_ref[...] = jnp.zeros_like(dg_ref)

        xv = x_ref[...]
        r = lax.rsqrt(jnp.mean(xv * xv, axis=-1, keepdims=True) + EPS)
        dg_ref[...] += jnp.sum(dh_ref[...] * (xv * r), axis=0, keepdims=True)

    row = pl.BlockSpec((tm, d), lambda i: (i, 0))
    return pl.pallas_call(
        body, name=name, grid=(t // tm,), in_specs=[row, row], out_specs=pl.BlockSpec((1, d), lambda i: (0, 0)),
        out_shape=jax.ShapeDtypeStruct((1, d), f32), compiler_params=_params(1),
    )(x, dh)


_CONV_ROWS = 256
_CHUNK = 64
_HALO = 32


def _pool_counts(pos, w):
    return jnp.minimum(pos + 1.0, float(w))


def _rows_from(win, start, rows):
    if start % 8 == 0:
        return win[start:start + rows, :]
    n = win.shape[0]
    return pltpu.roll(win, n - start % 8, axis=0)[start - start % 8:start - start % 8 + rows, :]


def _tap_rows(buf, starts, rows):
    for residue in range(8):
        group = [(k, s) for k, s in starts.items() if s % 8 == residue]
        if group:
            lo = min(s for _, s in group) - residue
            hi = max(s for _, s in group) - residue + rows + (8 if residue else 0)
            win = buf[lo:hi, :]
            if residue:
                win = pltpu.roll(win, hi - lo - residue, axis=0)
            for k, s in group:
                yield k, win[s - residue - lo:s - residue - lo + rows, :]


def _mix_fwd(u, cw, cb, lg, lb, pw, ps, seq):
    t, c3 = u.shape
    c = c3 // 3
    kw = 31
    tm = min(_CONV_ROWS, seq)
    tps = seq // tm
    gd = c // len(POOL_WINDOWS)

    def body(u_ref, uh_ref, cw_ref, cb_ref, lg_ref, lb_ref, pw_ref, ps_ref, y_ref, hc_ref, hgbuf, pbuf):
        i = pl.program_id(0)
        keep = jnp.where(i % tps == 0, 0.0, 1.0)
        um = u_ref[...].astype(f32)
        uh = uh_ref[...].astype(f32) * keep
        hgbuf[0:_HALO, :] = uh[:, 0:c] * _sigmoid(uh[:, c:2 * c])
        hgbuf[_HALO:_HALO + tm, :] = um[:, 0:c] * _sigmoid(um[:, c:2 * c])
        pbuf[0:_HALO, :] = uh[:, 2 * c:]
        pbuf[_HALO:_HALO + tm, :] = um[:, 2 * c:]
        for r0 in range(0, tm, _CHUNK):
            acc = jnp.broadcast_to(cb_ref[...], (_CHUNK, c))
            for k, rows in _tap_rows(hgbuf, {k: r0 + _HALO - (kw - 1) + k for k in range(kw)}, _CHUNK):
                acc = acc + cw_ref[k:k + 1, :] * rows
            hc_ref[r0:r0 + _CHUNK, :] = acc
            mu = jnp.mean(acc, axis=-1, keepdims=True)
            xc = acc - mu
            var = jnp.mean(xc * xc, axis=-1, keepdims=True)
            hl = xc * lax.rsqrt(var + EPS) * lg_ref[...] + lb_ref[...]
            y_ref[r0:r0 + _CHUNK, 0:c] = (hl * _sigmoid(hl)).astype(y_ref.dtype)
        pos = ((i % tps) * tm).astype(f32) + lax.broadcasted_iota(jnp.int32, (tm, 1), 0).astype(f32)
        for gi, w in enumerate(POOL_WINDOWS):
            sl = slice(gi * gd, (gi + 1) * gd)
            v = pbuf[_HALO:_HALO + tm, sl]
            s = v
            for j in range(1, w):
                s = s + pbuf[_HALO - j:_HALO - j + tm, sl]
            pooled = s / _pool_counts(pos, w) - v
            mixed = _dot(pooled.astype(_ACT), pw_ref[gi].astype(_ACT), _NN)
            y_ref[:, c + gi * gd:c + (gi + 1) * gd] = (mixed * ps_ref[:, sl]).astype(y_ref.dtype)

    hb = tm // _HALO
    full = lambda shape: pl.BlockSpec(shape, lambda i: (0,) * len(shape))
    return pl.pallas_call(
        body, name="mix_fwd", grid=(t // tm,),
        in_specs=[pl.BlockSpec((tm, c3), lambda i: (i, 0)),
                  pl.BlockSpec((_HALO, c3), lambda i: (jnp.maximum(i * hb - 1, 0), 0)),
                  full((_HALO, c)), full((1, c)), full((1, c)), full((1, c)), full((len(POOL_WINDOWS), gd, gd)), full((1, c))],
        out_specs=[pl.BlockSpec((tm, 2 * c), lambda i: (i, 0)), pl.BlockSpec((tm, c), lambda i: (i, 0))],
        out_shape=[jax.ShapeDtypeStruct((t, 2 * c), _ACT), jax.ShapeDtypeStruct((t, c), f32)],
        scratch_shapes=[pltpu.VMEM((_HALO + tm, c), f32), pltpu.VMEM((_HALO + tm, c), f32)],
        compiler_params=_params(1),
    )(u, u, cw, cb, lg, lb, pw, ps)


def _mix_bwd_norm(hc, dy, lg, lb, seq):
    t, c = hc.shape
    tm = min(_CONV_ROWS, seq)

    def body(hc_ref, dy_ref, lg_ref, lb_ref, dhc_ref, sums_ref):
        @pl.when(pl.program_id(0) == 0)
        def _():
            sums_ref[...] = jnp.zeros_like(sums_ref)

        hcv = hc_ref[...]
        mu = jnp.mean(hcv, axis=-1, keepdims=True)
        xc = hcv - mu
        rstd = lax.rsqrt(jnp.mean(xc * xc, axis=-1, keepdims=True) + EPS)
        n = xc * rstd
        hl = n * lg_ref[...] + lb_ref[...]
        sg = _sigmoid(hl)
        dhl = dy_ref[...].astype(f32) * (sg * (1.0 + hl * (1.0 - sg)))
        dn = dhl * lg_ref[...]
        dhc = rstd * (dn - jnp.mean(dn, axis=-1, keepdims=True) - n * jnp.mean(dn * n, axis=-1, keepdims=True))
        dhc_ref[...] = dhc
        sums_ref[0:1, :] += jnp.sum(dhl * n, axis=0, keepdims=True)
        sums_ref[1:2, :] += jnp.sum(dhl, axis=0, keepdims=True)
        sums_ref[2:3, :] += jnp.sum(dhc, axis=0, keepdims=True)

    row = pl.BlockSpec((tm, c), lambda i: (i, 0))
    vec = pl.BlockSpec((1, c), lambda i: (0, 0))
    return pl.pallas_call(
        body, name="mix_bwd_norm", grid=(t // tm,), in_specs=[row, row, vec, vec],
        out_specs=[row, pl.BlockSpec((8, c), lambda i: (0, 0))],
        out_shape=[jax.ShapeDtypeStruct((t, c), f32), jax.ShapeDtypeStruct((8, c), f32)],
        compiler_params=_params(1),
    )(hc, dy, lg, lb)


def _mix_bwd_taps(u, dhc, dy, cw, pw, ps, seq):
    t, c3 = u.shape
    c = c3 // 3
    kw = 31
    tm = min(_CONV_ROWS, seq)
    tps = seq // tm
    ng = len(POOL_WINDOWS)
    gd = c // ng
    nh = 16

    def body(u_ref, uh_ref, dhc_ref, dhcn_ref, dy_ref, dyn_ref, cw_ref, pw_ref, ps_ref,
             du_ref, dcw_ref, dps_ref, dpw_ref, hgbuf, dcbuf, pbuf, dpbuf):
        i = pl.program_id(0)
        keep_prev = jnp.where(i % tps == 0, 0.0, 1.0)
        keep_next = jnp.where(i % tps == tps - 1, 0.0, 1.0)

        @pl.when(i == 0)
        def _():
            dcw_ref[...] = jnp.zeros_like(dcw_ref)
            dps_ref[...] = jnp.zeros_like(dps_ref)
            dpw_ref[...] = jnp.zeros_like(dpw_ref)

        uh = uh_ref[...].astype(f32) * keep_prev
        hgbuf[0:_HALO, :] = uh[:, 0:c] * _sigmoid(uh[:, c:2 * c])
        pbuf[0:_HALO, :] = uh[:, 2 * c:]
        um = u_ref[...].astype(f32)
        hgbuf[_HALO:_HALO + tm, :] = um[:, 0:c] * _sigmoid(um[:, c:2 * c])
        pbuf[_HALO:_HALO + tm, :] = um[:, 2 * c:]
        dcbuf[0:tm, :] = dhc_ref[...]
        dcbuf[tm:tm + _HALO, :] = dhcn_ref[...] * keep_next
        tap_sums = [None] * kw
        for r0 in range(0, tm, _CHUNK):
            dh = dcbuf[r0:r0 + _CHUNK, :]
            acc = jnp.zeros((_CHUNK, c), f32)
            for k, rows in _tap_rows(hgbuf, {k: r0 + _HALO - (kw - 1) + k for k in range(kw)}, _CHUNK):
                part = (dh * rows).reshape(_CHUNK // 8, 8, c).sum(axis=0)
                tap_sums[k] = part if tap_sums[k] is None else tap_sums[k] + part
            for k, rows in _tap_rows(dcbuf, {k: r0 + (kw - 1) - k for k in range(kw)}, _CHUNK):
                acc = acc + cw_ref[k:k + 1, :] * rows
            val = u_ref[r0:r0 + _CHUNK, 0:c].astype(f32)
            sg = _sigmoid(u_ref[r0:r0 + _CHUNK, c:2 * c].astype(f32))
            du_ref[r0:r0 + _CHUNK, 0:c] = (acc * sg).astype(du_ref.dtype)
            du_ref[r0:r0 + _CHUNK, c:2 * c] = (acc * val * sg * (1.0 - sg)).astype(du_ref.dtype)
        for k in range(kw):
            dcw_ref[k:k + 1, :] += jnp.sum(tap_sums[k], axis=0, keepdims=True)
        base = ((i % tps) * tm).astype(f32)
        pos = base + lax.broadcasted_iota(jnp.int32, (tm, 1), 0).astype(f32)
        pos_next = base + float(tm) + lax.broadcasted_iota(jnp.int32, (nh, 1), 0).astype(f32)
        for gi, w in enumerate(POOL_WINDOWS):
            sl = slice(gi * gd, (gi + 1) * gd)
            v = pbuf[_HALO:_HALO + tm, sl]
            s = v
            for j in range(1, w):
                s = s + pbuf[_HALO - j:_HALO - j + tm, sl]
            cnt = _pool_counts(pos, w)
            pooled = (s / cnt - v).astype(_ACT)
            pwg = pw_ref[gi].astype(_ACT)
            mixed = _dot(pooled, pwg, _NN)
            dyp = dy_ref[:, sl].astype(f32)
            dps_ref[0:1, sl] += jnp.sum(dyp * mixed, axis=0, keepdims=True)
            dmix = (dyp * ps_ref[:, sl]).astype(_ACT)
            dpw_ref[gi] += _dot(pooled, dmix, _TN)
            dmix_next = (dyn_ref[:, sl].astype(f32) * ps_ref[:, sl] * keep_next).astype(_ACT)
            dpool = _dot(dmix, pwg, _NT)
            dpbuf[0:tm, sl] = dpool / cnt
            dpbuf[tm:tm + nh, sl] = _dot(dmix_next, pwg, _NT) / _pool_counts(pos_next, w)
            acc = -dpool
            for j in range(w):
                acc = acc + dpbuf[j:j + tm, sl]
            du_ref[:, 2 * c + gi * gd:2 * c + (gi + 1) * gd] = acc.astype(du_ref.dtype)

    hb = tm // _HALO
    n_halo = t // _HALO
    n_nh = t // nh
    full = lambda shape: pl.BlockSpec(shape, lambda i: (0,) * len(shape))
    return pl.pallas_call(
        body, name="mix_bwd_taps", grid=(t // tm,),
        in_specs=[pl.BlockSpec((tm, c3), lambda i: (i, 0)),
                  pl.BlockSpec((_HALO, c3), lambda i: (jnp.maximum(i * hb - 1, 0), 0)),
                  pl.BlockSpec((tm, c), lambda i: (i, 0)),
                  pl.BlockSpec((_HALO, c), lambda i: (jnp.minimum((i + 1) * hb, n_halo - 1), 0)),
                  pl.BlockSpec((tm, c), lambda i: (i, 1)),
                  pl.BlockSpec((nh, c), lambda i: (jnp.minimum((i + 1) * (tm // nh), n_nh - 1), 1)),
                  full((_HALO, c)), full((ng, gd, gd)), full((1, c))],
        out_specs=[pl.BlockSpec((tm, c3), lambda i: (i, 0)), full((_HALO, c)), full((8, c)), full((ng, gd, gd))],
        out_shape=[jax.ShapeDtypeStruct((t, c3), _ACT), jax.ShapeDtypeStruct((_HALO, c), f32),
                   jax.ShapeDtypeStruct((8, c), f32), jax.ShapeDtypeStruct((ng, gd, gd), f32)],
        scratch_shapes=[pltpu.VMEM((_HALO + tm, c), f32), pltpu.VMEM((tm + _HALO, c), f32),
                        pltpu.VMEM((_HALO + tm, c), f32), pltpu.VMEM((tm + nh, c), f32)],
        compiler_params=_params(1),
    )(u, u, dhc, dhc, dy, dy, cw, pw, ps)


def _attn_fwd(q, kv, n_seq, seq, n_mem):
    t, d = q.shape
    dh = d // XATTN_HEADS
    tq = min(512, seq)
    nq = seq // tq
    scale = dh ** -0.5

    def body(q_ref, kv_ref, o_ref):
        for h in range(XATTN_HEADS):
            cols = slice(h * dh, (h + 1) * dh)
            s = _dot(q_ref[:, cols], kv_ref[:, cols], _NT) * scale
            e = jnp.exp(s - jnp.max(s, axis=-1, keepdims=True))
            p = e / jnp.sum(e, axis=-1, keepdims=True)
            o_ref[:, cols] = _dot(p.astype(_ACT), kv_ref[:, d + h * dh:d + (h + 1) * dh], _NN).astype(o_ref.dtype)

    qs = pl.BlockSpec((tq, d), lambda b, i: (b * nq + i, 0))
    return pl.pallas_call(
        body, name="attn_fwd", grid=(n_seq, nq), in_specs=[qs, pl.BlockSpec((n_mem, 2 * d), lambda b, i: (b, 0))],
        out_specs=qs, out_shape=jax.ShapeDtypeStruct((t, d), _ACT), compiler_params=_params(2),
    )(q, kv)


def _attn_bwd(q, kv, do, n_seq, seq, n_mem):
    t, d = q.shape
    dh = d // XATTN_HEADS
    tq = min(512, seq)
    nq = seq // tq
    scale = dh ** -0.5

    def body(q_ref, kv_ref, do_ref, dq_ref, dkv_ref, acc):
        i = pl.program_id(1)

        @pl.when(i == 0)
        def _():
            acc[...] = jnp.zeros_like(acc)

        for h in range(XATTN_HEADS):
            cols = slice(h * dh, (h + 1) * dh)
            vcols = slice(d + h * dh, d + (h + 1) * dh)
            qv = q_ref[:, cols]
            kh = kv_ref[:, cols]
            dov = do_ref[:, cols]
            s = _dot(qv, kh, _NT) * scale
            e = jnp.exp(s - jnp.max(s, axis=-1, keepdims=True))
            p = e / jnp.sum(e, axis=-1, keepdims=True)
            dp = _dot(dov, kv_ref[:, vcols], _NT)
            ds = (p * (dp - jnp.sum(dp * p, axis=-1, keepdims=True)) * scale).astype(_ACT)
            dq_ref[:, cols] = _dot(ds, kh, _NN).astype(dq_ref.dtype)
            acc[:, cols] += _dot(ds, qv, _TN)
            acc[:, vcols] += _dot(p.astype(_ACT), dov, _TN)

        @pl.when(i == nq - 1)
        def _():
            dkv_ref[...] = acc[...].astype(dkv_ref.dtype)

    qs = pl.BlockSpec((tq, d), lambda b, i: (b * nq + i, 0))
    ms = pl.BlockSpec((n_mem, 2 * d), lambda b, i: (b, 0))
    return pl.pallas_call(
        body, name="attn_bwd", grid=(n_seq, nq), in_specs=[qs, ms, qs], out_specs=[qs, ms],
        out_shape=[jax.ShapeDtypeStruct((t, d), _ACT), jax.ShapeDtypeStruct((n_seq * n_mem, 2 * d), _ACT)],
        scratch_shapes=[pltpu.VMEM((n_mem, 2 * d), f32)], compiler_params=_params(2),
    )(q, kv, do)


_FFN_ROWS = 2048
_FFN_COLS = 256
_FFN_HALO = 16


def _window(buf, g, start, rows):
    return buf[g, pl.ds(start, rows + 8), :]


def _taps3(win, rows):
    return [_rows_from(win, 6 + k, rows) for k in range(3)]


def _conv3(b_ref, w_ref, taps):
    acc = b_ref[...] + w_ref[0:1, :] * taps[0]
    for k in (1, 2):
        acc = acc + w_ref[k:k + 1, :] * taps[k]
    return acc


def _ffn_gate_fwd(up, fw, fb, seq):
    _, t, f = up.shape
    tm = min(_FFN_ROWS, seq)
    tps = seq // tm
    tc = _FFN_COLS
    nc = f // tc
    hl = _FFN_HALO

    def body(up_ref, uph_ref, wg_ref, wv_ref, bg_ref, bv_ref, a_ref):
        i = pl.program_id(1)
        before = uph_ref[...]
        before = jnp.where(i % tps == 0, jnp.zeros_like(before), before)

        def chunk(r0, wins):
            conv = []
            for g, (w_ref, b_ref) in enumerate(((wg_ref, bg_ref), (wv_ref, bv_ref))):
                conv.append(_conv3(b_ref, w_ref, _taps3(wins[g].astype(f32)[hl - 8:, :], _CHUNK)))
            gate, val = conv
            a_ref[pl.ds(r0, _CHUNK), :] = (gate * _sigmoid(gate) * val).astype(a_ref.dtype)

        chunk(0, [jnp.concatenate([before[g], up_ref[g, 0:_CHUNK, :]], axis=0) for g in range(2)])

        def later(ci, carry):
            r0 = pl.multiple_of(ci * _CHUNK, _CHUNK)
            chunk(r0, [up_ref[g, pl.ds(r0 - hl, _CHUNK + hl), :] for g in range(2)])
            return carry

        lax.fori_loop(1, tm // _CHUNK, later, 0)

    hb = tm // hl
    return pl.pallas_call(
        body, name="ffn_gate_fwd", grid=(nc, t // tm),
        in_specs=[pl.BlockSpec((2, tm, tc), lambda j, i: (0, i, j)),
                  pl.BlockSpec((2, hl, tc), lambda j, i: (0, jnp.maximum(i * hb - 1, 0), j)),
                  pl.BlockSpec((8, tc), lambda j, i: (0, j)), pl.BlockSpec((8, tc), lambda j, i: (0, nc + j)),
                  pl.BlockSpec((1, tc), lambda j, i: (0, j)), pl.BlockSpec((1, tc), lambda j, i: (0, nc + j))],
        out_specs=pl.BlockSpec((tm, tc), lambda j, i: (i, j)),
        out_shape=jax.ShapeDtypeStruct((t, f), _ACT), compiler_params=_params(2),
    )(up, up, fw, fw, fb, fb)


def _ffn_gate_bwd(up, da, fw, fb, seq):
    _, t, f = up.shape
    tm = min(_FFN_ROWS, seq)
    tps = seq // tm
    tc = _FFN_COLS
    nc = f // tc
    hl = _FFN_HALO

    def body(up_ref, uph_ref, upn_ref, da_ref, dan_ref, wg_ref, wv_ref, bg_ref, bv_ref,
             dup_ref, sg_ref, sv_ref, dbuf, sums):
        i = pl.program_id(1)
        at_end = i % tps == tps - 1

        @pl.when(i == 0)
        def _():
            sg_ref[...] = jnp.zeros_like(sg_ref)
            sv_ref[...] = jnp.zeros_like(sv_ref)

        sums[...] = jnp.zeros_like(sums)
        before = uph_ref[...]
        before = jnp.where(i % tps == 0, jnp.zeros_like(before), before)
        after = upn_ref[...]
        after = jnp.where(at_end, jnp.zeros_like(after), after)
        w_refs = (wg_ref, wv_ref)
        b_refs = (bg_ref, bv_ref)

        def grads(r0, rows, wins, dav, count):
            taps = [_taps3(wins[g].astype(f32)[hl - 8:, :], rows) for g in range(2)]
            gate, val = [_conv3(b_refs[g], w_refs[g], taps[g]) for g in range(2)]
            sg = _sigmoid(gate)
            douts = (dav * val * (sg * (1.0 + gate * (1.0 - sg))), dav * (gate * sg))
            for g in range(2):
                dbuf[g, pl.ds(r0, rows), :] = douts[g]
                if count:
                    sums[g, 0] += douts[g].reshape(rows // 8, 8, tc).sum(axis=0)
                    for k in range(3):
                        sums[g, 1 + k] += (douts[g] * taps[g][k]).reshape(rows // 8, 8, tc).sum(axis=0)

        grads(0, _CHUNK, [jnp.concatenate([before[g], up_ref[g, 0:_CHUNK, :]], axis=0) for g in range(2)],
              da_ref[0:_CHUNK, :].astype(f32), True)

        def first(ci, carry):
            r0 = pl.multiple_of(ci * _CHUNK, _CHUNK)
            grads(r0, _CHUNK, [up_ref[g, pl.ds(r0 - hl, _CHUNK + hl), :] for g in range(2)],
                  da_ref[pl.ds(r0, _CHUNK), :].astype(f32), True)
            return carry

        lax.fori_loop(1, tm // _CHUNK, first, 0)
        da_after = dan_ref[...].astype(f32)
        grads(tm, hl, [jnp.concatenate([up_ref[g, tm - hl:tm, :], after[g]], axis=0) for g in range(2)],
              jnp.where(at_end, jnp.zeros_like(da_after), da_after), False)

        def second(ci, carry):
            r0 = pl.multiple_of(ci * _CHUNK, _CHUNK)
            for g in range(2):
                win = _window(dbuf, g, r0, _CHUNK)
                acc = jnp.zeros((_CHUNK, tc), f32)
                for k in range(3):
                    acc = acc + w_refs[g][k:k + 1, :] * _rows_from(win, 2 - k, _CHUNK)
                dup_ref[g, pl.ds(r0, _CHUNK), :] = acc.astype(dup_ref.dtype)
            return carry

        lax.fori_loop(0, tm // _CHUNK, second, 0)
        for g, s_ref in enumerate((sg_ref, sv_ref)):
            for r in range(4):
                s_ref[r:r + 1, :] += jnp.sum(sums[g, r], axis=0, keepdims=True)

    hb = tm // hl
    n_halo = t // hl
    return pl.pallas_call(
        body, name="ffn_gate_bwd", grid=(nc, t // tm),
        in_specs=[pl.BlockSpec((2, tm, tc), lambda j, i: (0, i, j)),
                  pl.BlockSpec((2, hl, tc), lambda j, i: (0, jnp.maximum(i * hb - 1, 0), j)),
                  pl.BlockSpec((2, hl, tc), lambda j, i: (0, jnp.minimum((i + 1) * hb, n_halo - 1), j)),
                  pl.BlockSpec((tm, tc), lambda j, i: (i, j)),
                  pl.BlockSpec((hl, tc), lambda j, i: (jnp.minimum((i + 1) * hb, n_halo - 1), j)),
                  pl.BlockSpec((8, tc), lambda j, i: (0, j)), pl.BlockSpec((8, tc), lambda j, i: (0, nc + j)),
                  pl.BlockSpec((1, tc), lambda j, i: (0, j)), pl.BlockSpec((1, tc), lambda j, i: (0, nc + j))],
        out_specs=[pl.BlockSpec((2, tm, tc), lambda j, i: (0, i, j)),
                   pl.BlockSpec((8, tc), lambda j, i: (0, j)), pl.BlockSpec((8, tc), lambda j, i: (0, j))],
        out_shape=[jax.ShapeDtypeStruct((2, t, f), _ACT), jax.ShapeDtypeStruct((8, f), f32), jax.ShapeDtypeStruct((8, f), f32)],
        scratch_shapes=[pltpu.VMEM((2, tm + hl, tc), f32), pltpu.VMEM((2, 4, 8, tc), f32)],
        compiler_params=_params(2),
    )(up, up, up, da, da, fw, fw, fb, fb)


def _adamw_math(w, g, m, v):
    m = ADAM_B1 * m + (1.0 - ADAM_B1) * g
    v = ADAM_B2 * v + (1.0 - ADAM_B2) * (g * g)
    m_hat = m / (1.0 - ADAM_B1 ** ADAM_STEP)
    v_hat = v / (1.0 - ADAM_B2 ** ADAM_STEP)
    delta = -ADAM_LR * (m_hat / (jnp.sqrt(v_hat) + ADAM_EPS) + ADAM_WD * w)
    return delta, m, v


def _adamw_shard(name, w, g, m, v):
    _, r, c = w.shape
    tr = next((cand for cand in (256, 176, 128, 64, 32, 16, 8) if r % cand == 0), r)

    def body(w_ref, g_ref, m_ref, v_ref, go_ref, d_ref, mo_ref, vo_ref):
        gv = g_ref[...]
        d, mn, vn = _adamw_math(w_ref[...], gv, m_ref[...], v_ref[...])
        go_ref[...] = gv
        d_ref[...] = d
        mo_ref[...] = mn
        vo_ref[...] = vn

    s3 = pl.BlockSpec((None, tr, c), lambda i: (0, i, 0))
    s2 = pl.BlockSpec((tr, c), lambda i: (i, 0))
    shp = jax.ShapeDtypeStruct(w.shape, f32)
    return pl.pallas_call(
        body, name=name, grid=(r // tr,), in_specs=[s3, s2, s3, s3], out_specs=[s3] * 4, out_shape=[shp] * 4,
        compiler_params=_params(1),
    )(w, g, m, v)


def _adamw_small(quads):
    n = len(quads)

    def body(*refs):
        ins, outs = refs[:4 * n], refs[4 * n:]
        for p in range(n):
            w_ref, g_ref, m_ref, v_ref = ins[4 * p:4 * p + 4]
            d, mn, vn = _adamw_math(w_ref[...], g_ref[...], m_ref[...], v_ref[...])
            outs[3 * p][...] = d
            outs[3 * p + 1][...] = mn
            outs[3 * p + 2][...] = vn

    flat = [a for q in quads for a in q]
    shapes = [jax.ShapeDtypeStruct(q[0].shape, f32) for q in quads for _ in range(3)]
    outs = pl.pallas_call(
        body, name="adamw_small", in_specs=[_VMEM] * (4 * n), out_specs=[_VMEM] * (3 * n), out_shape=shapes,
        compiler_params=pltpu.CompilerParams(vmem_limit_bytes=_VMEM_LIMIT_BYTES),
    )(*flat)
    return [tuple(outs[3 * p:3 * p + 3]) for p in range(n)]


def _sum_partials(name, place, grads, got):
    _, r, c = grads.shape
    steps = 4 if r % 64 == 0 else 1
    tr = r // steps

    def body(place_ref, own_ref, got_ref, f_ref):
        s = own_ref[...].astype(f32)
        for k in range(got.shape[0]):
            s = s + got_ref[k].astype(f32)
        f_ref[...] = s

    grid_spec = pltpu.PrefetchScalarGridSpec(
        num_scalar_prefetch=1, grid=(steps,),
        in_specs=[pl.BlockSpec((None, tr, c), lambda i, p: (2 * p[0] + p[1], i, 0)),
                  pl.BlockSpec((got.shape[0], tr, c), lambda i, p: (0, i, 0))],
        out_specs=pl.BlockSpec((None, tr, c), lambda i, p: (p[1], i, 0)))
    return pl.pallas_call(body, name=name, grid_spec=grid_spec, out_shape=jax.ShapeDtypeStruct((2, r, c), f32),
                          compiler_params=_params(1))(place, grads, got)


def _place():
    return lax.axis_index("x"), lax.axis_index("y"), lax.axis_index("c")


def _other_chips(x, y):
    return [(1 - x, y), (x, 1 - y), (1 - x, 1 - y)]


def _remote(src, dst, send_sem, recv_sem, to):
    return pltpu.make_async_remote_copy(src_ref=src, dst_ref=dst, send_sem=send_sem, recv_sem=recv_sem,
                                        device_id=to, device_id_type=_MESH)


def _place_shards(place, shards, col_sharded):
    n = len(shards)
    steps = 4

    def body(place_ref, *refs):
        for src, dst in zip(refs[:n], refs[n:]):
            dst[...] = src[...].astype(dst.dtype)

    in_specs, out_specs, out_shape = [], [], []
    for w, col in zip(shards, col_sharded):
        r, cs = w.shape
        tr = r // steps
        in_specs.append(pl.BlockSpec((tr, cs), lambda i, p: (i, 0)))
        if col:
            out_specs.append(pl.BlockSpec((tr, cs), lambda i, p: (i, p[0])))
            out_shape.append(jax.ShapeDtypeStruct((r, 4 * cs), _ACT))
        else:
            out_specs.append(pl.BlockSpec((tr, cs), lambda i, p: (p[0] * steps + i, 0)))
            out_shape.append(jax.ShapeDtypeStruct((4 * r, cs), _ACT))
    grid_spec = pltpu.PrefetchScalarGridSpec(num_scalar_prefetch=1, grid=(steps,), in_specs=in_specs, out_specs=out_specs)
    return pl.pallas_call(body, name="place_shards", grid_spec=grid_spec, out_shape=out_shape,
                          compiler_params=_params(1))(place, *shards)


def _shard_of(ref, col_sharded, s):
    rows, cols = ref.shape
    if col_sharded:
        return ref.at[:, pl.ds(s * (cols // 4), cols // 4)]
    return ref.at[pl.ds(s * (rows // 4), rows // 4), :]


def _part_of(ref, col_sharded, whole, s, h):
    if whole:
        return _shard_of(ref, col_sharded, s)
    rows, cols = ref.shape
    if col_sharded:
        return ref.at[pl.ds(h * (rows // 2), rows // 2), pl.ds(s * (cols // 4), cols // 4)]
    return ref.at[pl.ds((2 * s + h) * (rows // 8), rows // 8), :]


def _allgather_start(bufs, col_sharded, whole, groups):
    n = len(bufs)
    ng = len(groups)

    def body(*refs):
        out = refs[n:2 * n]
        sems = refs[2 * n:]
        x, y, c = _place()
        for g, members in enumerate(groups):
            for i, w in enumerate(members):
                mine = _part_of(out[w], col_sharded[w], whole[w], 2 * x + y, c)
                for j, chip in enumerate(_other_chips(x, y)):
                    _remote(mine, mine, sems[2 * g].at[3 * i + j], sems[2 * g + 1].at[3 * i + j], (*chip, c)).start()

    sem_shapes = [pltpu.SemaphoreType.DMA((3 * len(m),)) for m in groups for _ in range(2)]
    outs = pl.pallas_call(
        body, name="allgather_start", in_specs=[_HBM] * n, out_specs=[_HBM] * n + [_SEM] * (2 * ng),
        out_shape=[pltpu.HBM(b.shape, b.dtype) for b in bufs] + sem_shapes,
        input_output_aliases={i: i for i in range(n)},
        compiler_params=pltpu.CompilerParams(has_side_effects=_EFFECT),
    )(*[pltpu.with_memory_space_constraint(b, pltpu.HBM) for b in bufs])
    return list(outs[:n]), [(outs[n + 2 * g], outs[n + 2 * g + 1]) for g in range(ng)]


def _allgather_relay(name, bufs, col_sharded, whole, sems, after):
    n = len(bufs)

    def body(*refs):
        buf = refs[:n]
        send, recv = refs[n], refs[n + 1]
        out = refs[n + 3:2 * n + 3]
        to_sibling, from_sibling = refs[2 * n + 3:]
        x, y, c = _place()
        for i in range(n):
            mine = _part_of(buf[i], col_sharded[i], whole[i], 2 * x + y, c)
            for j, chip in enumerate(_other_chips(x, y)):
                landed = _part_of(buf[i], col_sharded[i], whole[i], 2 * chip[0] + chip[1], c)
                cp = _remote(mine, landed, send.at[3 * i + j], recv.at[3 * i + j], (*chip, c))
                cp.wait_send()
                cp.wait_recv()
        for i in range(n):
            if not whole[i]:
                for j, chip in enumerate(_other_chips(x, y)):
                    landed = _part_of(out[i], col_sharded[i], False, 2 * chip[0] + chip[1], c)
                    _remote(landed, landed, to_sibling.at[3 * i + j], from_sibling.at[3 * i + j], (x, y, 1 - c)).start()

    outs = pl.pallas_call(
        body, name=name, in_specs=[_HBM] * n + [_SEM, _SEM, _ANY], out_specs=[_HBM] * n + [_SEM, _SEM],
        out_shape=[pltpu.HBM(b.shape, b.dtype) for b in bufs] + [pltpu.SemaphoreType.DMA((3 * n,))] * 2,
        input_output_aliases={i: i for i in range(n)},
        compiler_params=pltpu.CompilerParams(has_side_effects=_EFFECT),
    )(*bufs, *sems, after)
    return list(outs[:n]), (outs[n], outs[n + 1])


def _allgather_wait(name, bufs, col_sharded, whole, sems, after):
    n = len(bufs)

    def body(*refs):
        buf = refs[:n]
        to_sibling, from_sibling = refs[n], refs[n + 1]
        x, y, c = _place()
        for i in range(n):
            if not whole[i]:
                for j, chip in enumerate(_other_chips(x, y)):
                    sent = _part_of(buf[i], col_sharded[i], False, 2 * chip[0] + chip[1], c)
                    landed = _part_of(buf[i], col_sharded[i], False, 2 * chip[0] + chip[1], 1 - c)
                    cp = _remote(sent, landed, to_sibling.at[3 * i + j], from_sibling.at[3 * i + j], (x, y, 1 - c))
                    cp.wait_send()
                    cp.wait_recv()

    return pl.pallas_call(
        body, name=name, in_specs=[_HBM] * n + [_SEM, _SEM, _ANY], out_specs=[_HBM] * n,
        out_shape=[pltpu.HBM(b.shape, b.dtype) for b in bufs],
        input_output_aliases={i: i for i in range(n)},
        compiler_params=pltpu.CompilerParams(has_side_effects=_EFFECT),
    )(*bufs, *sems, after)


def _other_devices(x, y, c):
    flips = [(bx, by, bc) for bx in (0, 1) for by in (0, 1) for bc in (0, 1)][1:]
    return [(1 - x if bx else x, 1 - y if by else y, 1 - c if bc else c) for bx, by, bc in flips]


def _grad_exchange_start(name, grads):
    nw = len(grads)
    lands = [lax.empty((7,) + g.shape[1:], g.dtype) for g in grads]

    def body(*refs):
        src = refs[2 * nw:3 * nw]
        got = refs[3 * nw:4 * nw]
        send, recv, token = refs[4 * nw:]
        x, y, c = _place()
        for w in range(nw):
            for k, (px, py, pc) in enumerate(_other_devices(x, y, c)):
                _remote(src[w].at[4 * px + 2 * py + pc], got[w].at[k], send.at[7 * w + k], recv.at[7 * w + k], (px, py, pc)).start()
        token[...] = jnp.zeros_like(token)

    outs = pl.pallas_call(
        body, name=name, in_specs=[_HBM] * (2 * nw), out_specs=[_HBM] * (2 * nw) + [_SEM, _SEM, _VMEM],
        out_shape=[pltpu.HBM(a.shape, a.dtype) for a in list(grads) + lands]
        + [pltpu.SemaphoreType.DMA((7 * nw,)), pltpu.SemaphoreType.DMA((7 * nw,)), jax.ShapeDtypeStruct((8, 128), f32)],
        input_output_aliases={i: i for i in range(2 * nw)},
        compiler_params=pltpu.CompilerParams(has_side_effects=_EFFECT),
    )(*[pltpu.with_memory_space_constraint(a, pltpu.HBM) for a in list(grads) + lands])
    return list(outs[:nw]), list(outs[nw:2 * nw]), (outs[2 * nw], outs[2 * nw + 1]), outs[2 * nw + 2]


def _grad_exchange_wait(name, grads, got, sems, after):
    nw = len(grads)

    def body(*refs):
        src = refs[:nw]
        land = refs[nw:2 * nw]
        send, recv = refs[2 * nw], refs[2 * nw + 1]
        x, y, c = _place()
        for w in range(nw):
            for k, (px, py, pc) in enumerate(_other_devices(x, y, c)):
                cp = _remote(src[w].at[4 * px + 2 * py + pc], land[w].at[k], send.at[7 * w + k], recv.at[7 * w + k], (px, py, pc))
                cp.wait_send()
                cp.wait_recv()

    outs = pl.pallas_call(
        body, name=name, in_specs=[_HBM] * (2 * nw) + [_SEM, _SEM, _ANY], out_specs=[_HBM] * (2 * nw),
        out_shape=[pltpu.HBM(a.shape, a.dtype) for a in list(grads) + list(got)],
        input_output_aliases={i: i for i in range(2 * nw)},
        compiler_params=pltpu.CompilerParams(has_side_effects=_EFFECT),
    )(*grads, *got, *sems, after)
    return list(outs[:nw]), list(outs[nw:])


def _swap_halves(finals):
    nw = len(finals)

    def body(*refs):
        buf = refs[nw:2 * nw]
        send_sem, recv_sem = refs[2 * nw:]
        x, y, c = _place()
        sends = []
        for w in range(nw):
            rc = _remote(buf[w].at[c], buf[w].at[c], send_sem.at[w], recv_sem.at[w], (x, y, 1 - c))
            rc.start()
            sends.append(rc)
        for w in range(nw):
            _remote(buf[w].at[1 - c], buf[w].at[1 - c], send_sem.at[w], recv_sem.at[w], (x, y, c)).wait_recv()
        for rc in sends:
            rc.wait_send()

    return pl.pallas_call(
        body, name="rs_swap_halves", in_specs=[_ANY] * nw, out_specs=[_ANY] * nw,
        out_shape=[jax.ShapeDtypeStruct(g.shape, g.dtype) for g in finals],
        input_output_aliases={i: i for i in range(nw)},
        scratch_shapes=[pltpu.SemaphoreType.DMA((nw,)), pltpu.SemaphoreType.DMA((nw,))],
    )(*finals)


def _half_slices(shape, h):
    rows, cols = shape
    if cols % 256 == 0:
        return (slice(None), slice(h * (cols // 2), (h + 1) * (cols // 2)))
    return (slice(h * (rows // 2), (h + 1) * (rows // 2)), slice(None))


def _allreduce_small(parts):
    n = len(parts)

    def body(*refs):
        src = refs[:n]
        out = refs[n:2 * n]
        sib = refs[2 * n:3 * n]
        chip_sum = refs[3 * n:4 * n]
        slots = refs[4 * n:5 * n]
        pair_send, pair_recv, ici_send, ici_recv, swap_send, swap_recv = refs[5 * n:]
        x, y, c = _place()
        me_chip = 2 * x + y
        chips = _other_chips(x, y)
        pairs = [_remote(src[a], sib[a], pair_send.at[a], pair_recv.at[a], (x, y, 1 - c)) for a in range(n)]
        for rc in pairs:
            rc.start()
        for a in range(n):
            pairs[a].wait_recv()
            chip_sum[a][...] = src[a][...] + sib[a][...]
        for h in (0, 1):
            @pl.when(c == h)
            def _():
                sends = []
                for a in range(n):
                    idx = _half_slices(parts[a].shape, h)
                    for j, chip in enumerate(chips):
                        rc = _remote(chip_sum[a].at[idx], slots[a].at[me_chip].at[idx], ici_send.at[3 * a + j], ici_recv.at[3 * a + j], (*chip, h))
                        rc.start()
                        sends.append(rc)
                    slots[a][(me_chip,) + idx] = chip_sum[a][idx]
                for a in range(n):
                    idx = _half_slices(parts[a].shape, h)
                    for j, chip in enumerate(chips):
                        landed = slots[a].at[2 * chip[0] + chip[1]].at[idx]
                        _remote(landed, landed, ici_send.at[3 * a + j], ici_recv.at[3 * a + j], (x, y, c)).wait_recv()
                    total = slots[a][(0,) + idx]
                    for s in range(1, 4):
                        total = total + slots[a][(s,) + idx]
                    out[a][idx] = total
                    rc = _remote(out[a].at[idx], out[a].at[idx], swap_send.at[a], swap_recv.at[a], (x, y, 1 - h))
                    rc.start()
                    sends.append(rc)
                for a in range(n):
                    other = out[a].at[_half_slices(parts[a].shape, 1 - h)]
                    _remote(other, other, swap_send.at[a], swap_recv.at[a], (x, y, c)).wait_recv()
                for rc in sends:
                    rc.wait_send()
        for rc in pairs:
            rc.wait_send()

    return pl.pallas_call(
        body, name="allreduce_small", in_specs=[_VMEM] * n, out_specs=[_VMEM] * n,
        out_shape=[jax.ShapeDtypeStruct(p.shape, f32) for p in parts],
        scratch_shapes=[pltpu.VMEM(p.shape, f32) for p in parts] * 2 + [pltpu.VMEM((4,) + p.shape, f32) for p in parts]
        + [pltpu.SemaphoreType.DMA((n,)), pltpu.SemaphoreType.DMA((n,)), pltpu.SemaphoreType.DMA((3 * n,)),
           pltpu.SemaphoreType.DMA((3 * n,)), pltpu.SemaphoreType.DMA((n,)), pltpu.SemaphoreType.DMA((n,))],
        compiler_params=pltpu.CompilerParams(vmem_limit_bytes=_VMEM_LIMIT_BYTES),
    )(*parts)


def _local_step(x, mem, tgt, g_mix, g_xattn, g_mem, g_ffn, g_final, cb, lg, lb, pw, ps, fb, relay, weights, reduce, n_seq, seq, n_mem):
    t, d = x.shape
    f = fb.shape[1] // 2
    c = cb.shape[1]
    h1 = _rms_fwd("norm_mix", x, g_mix)
    relay(0, h1)
    w_in, cw, fw = weights(0, h1)
    u = _mm_nn("proj_in", h1, w_in, _ACT, w_in.shape[1])
    y, hc = _mix_fwd(u, cw, cb, lg, lb, pw, ps, seq)
    relay(1, y)
    w_out, w_q, w_kv, w_o = weights(1, y)
    x1, h2 = _proj_residual_norm("proj_out", y, w_out, x, g_xattn)
    q = _mm_nn("proj_q", h2, w_q, _ACT, d)
    mem_n = _rms_fwd("norm_mem", mem, g_mem)
    kv = _mm_nn("proj_kv", mem_n, w_kv, _ACT, 2 * d)
    o = _attn_fwd(q, kv, n_seq, seq, n_mem)
    relay(2, o)
    x2, h3 = _proj_residual_norm("proj_o", o, w_o, x1, g_ffn)
    w_up, w_down = weights(2, h3)
    up = _mm_nn("proj_up", h3, w_up, _ACT, f, split_out=True)
    a = _ffn_gate_fwd(up, fw, fb, seq)
    dx3, dx3b, dg_final, loss = _proj_loss_bwd("proj_down", a, w_down, x2, g_final, tgt)
    da = _mm_nt("d_act", dx3b, w_down, _ACT)
    gw_down = _mm_tn_rows("dw_down", a, dx3b, f // 2, d // 2)
    dup, sums_g, sums_v = _ffn_gate_bwd(up, da, fw, fb, seq)
    gw_up = _mm_tn_pieces("dw_up", h3, dup, f // 2, t)
    token = reduce(0, [gw_down.reshape(8, -1, d), gw_up])
    dx2, dx2b, dg_ffn = _dproj_rms_bwd("d_h3", dup, w_up, x2, g_ffn + token, dx3)
    do = _mm_nt("d_o", dx2b, w_o, _ACT)
    gw_o = _mm_tn_rows("dw_o", o, dx2b, d, d // 2)
    dq, dkv = _attn_bwd(q, kv, do, n_seq, seq, n_mem)
    gw_q = _mm_tn_rows("dw_q", h2, dq, d, d // 2)
    gw_kv = _mm_tn_pieces("dw_kv", mem_n, dkv, d // 2, mem.shape[0])
    dmem_n = _mm_nt("d_mem_n", dkv, w_kv, f32)
    dg_mem = _rms_gain_grad("norm_mem_bwd", mem, dmem_n)
    dx1, dx1b, dg_xattn = _dproj_rms_bwd("d_h2", dq, w_q, x1, g_xattn, dx2)
    dy = _mm_nt("d_y", dx1b, w_out, _ACT)
    gw_out = _mm_tn_rows("dw_out", y, dx1b, d, d // 2)
    token = reduce(1, [gw_o.reshape(8, -1, d), gw_q.reshape(8, -1, d), gw_kv, gw_out.reshape(8, -1, d)])
    dhc, sums_norm = _mix_bwd_norm(hc, dy, lg + token, lb, seq)
    du, d_cw, d_ps, d_pw = _mix_bwd_taps(u, dhc, dy, cw, pw, ps, seq)
    gw_in = _mm_tn_pieces("dw_in", h1, du, c * 3 // 4, t)
    token = reduce(2, [gw_in])
    grad_x, dg_mix = _dproj_rms_bwd("d_h1", du, w_in, x, g_mix + token, dx1, storage_copy=False)
    zero_row = jnp.zeros((1, d), f32)
    gains = jnp.concatenate([dg_mix, dg_xattn, dg_mem, dg_ffn, dg_final, jnp.pad(loss, ((0, 0), (0, d - 1))), zero_row, zero_row], axis=0)
    conv_rows = jnp.concatenate([sums_norm[2:3], sums_norm[0:1], sums_norm[1:2], d_ps[0:1], jnp.zeros((4, c), f32)], axis=0)
    ffn_rows = jnp.concatenate([sums_g, sums_v], axis=1)
    small = [gains, conv_rows, d_pw.reshape(-1, d_pw.shape[-1]), ffn_rows, d_cw]
    return grad_x, small


def kernel(x, mem, norm_mix_g, w_in, conv_dw_w, conv_dw_b, conv_ln_g, conv_ln_b, pool_w, pool_scale, w_out, norm_xattn_g, norm_mem_g, w_q, w_kv, w_o, norm_ffn_g, w_up, ffn_dw_w, ffn_dw_b, w_down, norm_final_g, loss_target, m_norm_mix_g, m_w_in, m_conv_dw_w, m_conv_dw_b, m_conv_ln_g, m_conv_ln_b, m_pool_w, m_pool_scale, m_w_out, m_norm_xattn_g, m_norm_mem_g, m_w_q, m_w_kv, m_w_o, m_norm_ffn_g, m_w_up, m_ffn_dw_w, m_ffn_dw_b, m_w_down, m_norm_final_g, v_norm_mix_g, v_w_in, v_conv_dw_w, v_conv_dw_b, v_conv_ln_g, v_conv_ln_b, v_pool_w, v_pool_scale, v_w_out, v_norm_xattn_g, v_norm_mem_g, v_w_q, v_w_kv, v_w_o, v_norm_ffn_g, v_w_up, v_ffn_dw_w, v_ffn_dw_b, v_w_down, v_norm_final_g):
    n_seq, seq, d = x.shape
    n_mem = mem.shape[1]
    chip = 2 * lax.axis_index("x") + lax.axis_index("y")

    place = jnp.stack([chip, lax.axis_index("c")]).astype(jnp.int32)

    col_w = [w_in, w_kv, w_up]
    row_w = [w_out, w_q, w_o, w_down]
    col_flags = [True] * 3 + [False] * 4 + [True] * 2
    kw = conv_dw_w.shape[1]

    def padded_in_place(shard, rows):
        full = jnp.zeros((rows, 4 * shard.shape[1]), shard.dtype)
        return lax.dynamic_update_slice(full, shard, (0, chip * shard.shape[1]))

    bufs = list(_place_shards(place, [w[0] for w in col_w + row_w], col_flags[:7]))
    bufs += [padded_in_place(conv_dw_w[0], _HALO), padded_in_place(ffn_dw_w[0], 8)]
    groups = [[0, 7, 8], [3, 4, 1, 5], [2, 6]]
    whole = [False] * 7 + [True] * 2
    bufs, sems = _allgather_start(bufs, col_flags, whole, groups)
    relayed = {}

    def relay(g, after):
        members = groups[g]
        relayed[g] = _allgather_relay("allgather_relay_%d" % g, [bufs[i] for i in members], [col_flags[i] for i in members],
                                      [whole[i] for i in members], sems[g], after)

    def weights(g, after):
        members = groups[g]
        group_bufs, sibling_sems = relayed[g]
        return _allgather_wait("allgather_wait_%d" % g, group_bufs, [col_flags[i] for i in members],
                               [whole[i] for i in members], sibling_sems, after)

    names = ["w_in", "w_kv", "w_up", "w_out", "w_q", "w_o", "w_down"]
    reduce_groups = [["w_down", "w_up"], ["w_o", "w_q", "w_kv", "w_out"], ["w_in"]]
    in_flight = {}

    def reduce(g, grads):
        grads, lands, rs_sems, token = _grad_exchange_start("rs_start_%d" % g, grads)
        in_flight[g] = (grads, lands, rs_sems)
        return token[0:1, 0:1]

    grad_x, small = _local_step(
        x.reshape(n_seq * seq, d), mem.reshape(n_seq * n_mem, d), loss_target.reshape(n_seq * seq, d),
        norm_mix_g, norm_xattn_g, norm_mem_g, norm_ffn_g, norm_final_g.reshape(1, d),
        conv_dw_b, conv_ln_g, conv_ln_b, pool_w[0], pool_scale, ffn_dw_b, relay, weights, reduce, n_seq, seq, n_mem)

    finals = {}
    for g, members in enumerate(reduce_groups):
        grads, lands, rs_sems = in_flight[g]
        grads, lands = _grad_exchange_wait("rs_wait_%d" % g, grads, lands, rs_sems, grad_x)
        for n, a, b in zip(members, grads, lands):
            finals[n] = _sum_partials("rs_sum_" + n, place, a, b)
    shard_grads = _swap_halves([finals[n] for n in names])

    gains, conv_rows, d_pw, ffn_rows, d_cw = _allreduce_small(small)
    loss = gains[5, 0]

    outs = {}
    big_w = dict(zip(names, col_w + row_w))
    big_m = dict(w_in=m_w_in, w_kv=m_w_kv, w_up=m_w_up, w_out=m_w_out, w_q=m_w_q, w_o=m_w_o, w_down=m_w_down)
    big_v = dict(w_in=v_w_in, w_kv=v_w_kv, w_up=v_w_up, w_out=v_w_out, w_q=v_w_q, w_o=v_w_o, w_down=v_w_down)
    for n, g in zip(names, shard_grads):
        w = big_w[n]
        g2 = g.reshape(w.shape[1], w.shape[2])
        outs[n] = tuple(_adamw_shard("adamw_" + n, w, g2, big_m[n], big_v[n]))

    f2 = ffn_dw_b.shape[1]
    cs_c = conv_dw_w.shape[2]
    cs_f = ffn_dw_w.shape[2]
    g_cw = lax.dynamic_slice(d_cw, (0, chip * cs_c), (kw, cs_c)).reshape(conv_dw_w.shape)
    g_fw = lax.dynamic_slice(ffn_rows, (1, chip * cs_f), (ffn_dw_w.shape[1], cs_f)).reshape(ffn_dw_w.shape)
    small_params = [
        ("norm_mix_g", norm_mix_g, gains[0:1], m_norm_mix_g, v_norm_mix_g),
        ("conv_dw_w", conv_dw_w, g_cw, m_conv_dw_w, v_conv_dw_w),
        ("conv_dw_b", conv_dw_b, conv_rows[0:1], m_conv_dw_b, v_conv_dw_b),
        ("conv_ln_g", conv_ln_g, conv_rows[1:2], m_conv_ln_g, v_conv_ln_g),
        ("conv_ln_b", conv_ln_b, conv_rows[2:3], m_conv_ln_b, v_conv_ln_b),
        ("pool_w", pool_w, d_pw.reshape(pool_w.shape), m_pool_w, v_pool_w),
        ("pool_scale", pool_scale, conv_rows[3:4], m_pool_scale, v_pool_scale),
        ("norm_xattn_g", norm_xattn_g, gains[1:2], m_norm_xattn_g, v_norm_xattn_g),
        ("norm_mem_g", norm_mem_g, gains[2:3], m_norm_mem_g, v_norm_mem_g),
        ("norm_ffn_g", norm_ffn_g, gains[3:4], m_norm_ffn_g, v_norm_ffn_g),
        ("ffn_dw_w", ffn_dw_w, g_fw, m_ffn_dw_w, v_ffn_dw_w),
        ("ffn_dw_b", ffn_dw_b, ffn_rows[0:1, :f2], m_ffn_dw_b, v_ffn_dw_b),
        ("norm_final_g", norm_final_g.reshape(1, d), gains[4:5], m_norm_final_g.reshape(1, d), v_norm_final_g.reshape(1, d)),
    ]
    quads = []
    for _, w, g, m, v in small_params:
        shape2 = (-1, w.shape[-1])
        quads.append((w.reshape(shape2), g.reshape(shape2), m.reshape(shape2), v.reshape(shape2)))
    for (n, w, g, _, _), (delta, new_m, new_v) in zip(small_params, _adamw_small(quads)):
        shape = norm_final_g.shape if n == "norm_final_g" else w.shape
        outs[n] = (g.reshape(shape), delta.reshape(shape), new_m.reshape(shape), new_v.reshape(shape))

    order = ["norm_mix_g", "w_in", "conv_dw_w", "conv_dw_b", "conv_ln_g", "conv_ln_b", "pool_w", "pool_scale", "w_out",
             "norm_xattn_g", "norm_mem_g", "w_q", "w_kv", "w_o", "norm_ffn_g", "w_up", "ffn_dw_w", "ffn_dw_b", "w_down",
             "norm_final_g"]
    return (loss, grad_x.reshape(x.shape), *[outs[n][0] for n in order], *[outs[n][1] for n in order],
            *[outs[n][2] for n in order], *[outs[n][3] for n in order])
```

```python
import functools

import jax
import jax.numpy as jnp
from jax import lax
from jax.experimental import pallas as pl
from jax.experimental.pallas import tpu as pltpu

f32 = jnp.float32
_ACT = jnp.bfloat16

EPS = 1e-6
POOL_WINDOWS = (2, 4, 8, 16)
XATTN_HEADS = 4
ADAM_LR = 0.001
ADAM_B1 = 0.9
ADAM_B2 = 0.999
ADAM_EPS = 1e-08
ADAM_WD = 0.01
ADAM_STEP = 10

_VMEM_LIMIT_BYTES = 56 * 1024 * 1024
_MESH = pl.DeviceIdType.MESH
_ANY = pl.BlockSpec(memory_space=pl.ANY)
_VMEM = pl.BlockSpec(memory_space=pltpu.VMEM)
_HBM = pl.BlockSpec(memory_space=pltpu.HBM)
_SEM = pl.BlockSpec(memory_space=pltpu.SEMAPHORE)
_EFFECT = pltpu.SideEffectType.DATAFLOW_SIDE_EFFECTING

_NN = (((1,), (0,)), ((), ()))
_NT = (((1,), (1,)), ((), ()))
_TN = (((0,), (0,)), ((), ()))


def _params(n_grid):
    return pltpu.CompilerParams(dimension_semantics=("arbitrary",) * n_grid, vmem_limit_bytes=_VMEM_LIMIT_BYTES)


def _sigmoid(v):
    return 1.0 / (1.0 + jnp.exp(-v))


def _dot(a, b, dims):
    return lax.dot_general(a, b, dims, preferred_element_type=f32)


def _mm(name, a, b, *, dims, grid, a_spec, b_spec, o_spec, out_shape, nk, acc_shape=None, res=None, res_spec=None):
    def body(*refs):
        if res is None:
            a_ref, b_ref, o_ref, *scratch = refs
            r_ref = None
        else:
            a_ref, b_ref, r_ref, o_ref, *scratch = refs
        p = _dot(a_ref[...], b_ref[...], dims)

        def finish(v):
            if r_ref is not None:
                v = v + r_ref[...]
            o_ref[...] = v.astype(o_ref.dtype)

        if nk == 1:
            finish(p)
        else:
            acc = scratch[0]
            k = pl.program_id(2)

            @pl.when(k == 0)
            def _():
                acc[...] = p

            @pl.when(k > 0)
            def _():
                acc[...] += p

            @pl.when(k == nk - 1)
            def _():
                finish(acc[...])

    ins = [a, b] + ([] if res is None else [res])
    specs = [a_spec, b_spec] + ([] if res is None else [res_spec])
    return pl.pallas_call(
        body, name=name, grid=grid, in_specs=specs, out_specs=o_spec, out_shape=out_shape,
        scratch_shapes=[pltpu.VMEM(acc_shape, f32)] if nk > 1 else [], compiler_params=_params(3),
    )(*ins)


_NARROW = 1536


def _row_tile(m, width=_NARROW + 1):
    return min(1024 if width <= _NARROW else 512, m)


def _mm_nn(name, a, b, out_dtype, tn, res=None, split_out=False):
    m, k = a.shape
    n = b.shape[1]
    tm = _row_tile(m, max(k, tn))
    if split_out:
        out_shape = jax.ShapeDtypeStruct((n // tn, m, tn), out_dtype)
        o_spec = pl.BlockSpec((None, tm, tn), lambda j, i, kk: (j, i, 0))
    else:
        out_shape = jax.ShapeDtypeStruct((m, n), out_dtype)
        o_spec = pl.BlockSpec((tm, tn), lambda j, i, kk: (i, j))
    return _mm(
        name, a, b, dims=_NN, grid=(n // tn, m // tm, 1), nk=1,
        a_spec=pl.BlockSpec((tm, k), lambda j, i, kk: (i, 0)),
        b_spec=pl.BlockSpec((k, tn), lambda j, i, kk: (0, j)),
        o_spec=o_spec, out_shape=out_shape, res=res,
        res_spec=pl.BlockSpec((tm, tn), lambda j, i, kk: (i, j)),
    )


def _mm_nt(name, a, b, out_dtype):
    n, kc = b.shape
    m = a.shape[0]
    tm = _row_tile(m, max(n, kc))
    return _mm(
        name, a, b, dims=_NT, grid=(m // tm, 1, 1), nk=1,
        a_spec=pl.BlockSpec((tm, kc), lambda i, j, k: (i, 0)), b_spec=pl.BlockSpec((n, kc), lambda i, j, k: (0, 0)),
        o_spec=pl.BlockSpec((tm, n), lambda i, j, k: (i, 0)),
        out_shape=jax.ShapeDtypeStruct((m, n), out_dtype),
    )


def _mm_tn_rows(name, a, b, tka, tn):
    m, ka = a.shape
    nb = b.shape[1]
    return _mm(
        name, a, b, dims=_TN, grid=(ka // tka, nb // tn, 1), nk=1,
        a_spec=pl.BlockSpec((m, tka), lambda i, j, k: (0, i)),
        b_spec=pl.BlockSpec((m, tn), lambda i, j, k: (0, j)),
        o_spec=pl.BlockSpec((tka, tn), lambda i, j, k: (i, j)),
        out_shape=jax.ShapeDtypeStruct((ka, nb), _ACT),
    )


def _mm_tn_pieces(name, a, b, cs, tt):
    m, ka = a.shape
    nk = m // tt
    if b.ndim == 3:
        b_spec = pl.BlockSpec((None, tt, cs), lambda i, j, k: (j // 2, k, j % 2))
    else:
        b_spec = pl.BlockSpec((tt, cs), lambda i, j, k: (k, j))
    return _mm(
        name, a, b, dims=_TN, grid=(2, 4, nk), nk=nk, acc_shape=(ka // 2, cs),
        a_spec=pl.BlockSpec((tt, ka // 2), lambda i, j, k: (k, i)), b_spec=b_spec,
        o_spec=pl.BlockSpec((None, ka // 2, cs), lambda i, j, k: (2 * j + i, 0, 0)),
        out_shape=jax.ShapeDtypeStruct((8, ka // 2, cs), _ACT),
    )


def _rms_fwd(name, x, g):
    t, d = x.shape
    tm = _row_tile(t, d)

    def body(x_ref, g_ref, h_ref):
        xv = x_ref[...]
        r = lax.rsqrt(jnp.mean(xv * xv, axis=-1, keepdims=True) + EPS)
        h_ref[...] = (xv * r * g_ref[...]).astype(h_ref.dtype)

    return pl.pallas_call(
        body, name=name, grid=(t // tm,),
        in_specs=[pl.BlockSpec((tm, d), lambda i: (i, 0)), pl.BlockSpec((1, d), lambda i: (0, 0))],
        out_specs=pl.BlockSpec((tm, d), lambda i: (i, 0)), out_shape=jax.ShapeDtypeStruct((t, d), _ACT),
        compiler_params=_params(1),
    )(x, g)


def _fused_rows(name, a, b, product, a_spec, tm, extras, extra_specs, out_shape, out_specs, epilogue):
    ne = len(extras)
    halves = 2 if tm % 32 == 0 else 1
    th = tm // halves

    def body(a_ref, b_ref, *refs):
        for s in range(halves):
            rows = pl.ds(s * th, th)

            def part(ref):
                if len(ref.shape) == 3 and ref.shape[1] == tm:
                    return ref.at[:, rows, :]
                if len(ref.shape) == 2 and ref.shape[0] == tm and tm > 1:
                    return ref.at[rows, :]
                return ref

            epilogue(product(part(a_ref), b_ref), [part(r) for r in refs[:ne]], [part(r) for r in refs[ne:]], s == 0)

    m = extras[0].shape[0]
    return pl.pallas_call(
        body, name=name, grid=(m // tm,),
        in_specs=[a_spec, pl.BlockSpec(b.shape, lambda i: (0, 0)), *extra_specs], out_specs=out_specs, out_shape=out_shape,
        compiler_params=_params(1),
    )(a, b, *extras)


def _proj_residual_norm(name, a, b, res, g):
    m, k = a.shape
    d = b.shape[1]
    tm = _row_tile(m, max(k, d))

    def epilogue(p, ins, outs, first):
        xv = p + ins[0][...]
        outs[0][...] = xv
        r = lax.rsqrt(jnp.mean(xv * xv, axis=-1, keepdims=True) + EPS)
        outs[1][...] = (xv * r * ins[1][...]).astype(outs[1].dtype)

    row = pl.BlockSpec((tm, d), lambda i: (i, 0))
    return _fused_rows(
        name, a, b, lambda a_ref, b_ref: _dot(a_ref[...], b_ref[...], _NN), pl.BlockSpec((tm, k), lambda i: (i, 0)), tm,
        [res, g], [row, pl.BlockSpec((1, d), lambda i: (0, 0))],
        [jax.ShapeDtypeStruct((m, d), f32), jax.ShapeDtypeStruct((m, d), _ACT)], [row, row], epilogue)


def _dproj_rms_bwd(name, a, b, x, g, dres, storage_copy=True):
    m, d = x.shape
    if a.ndim == 3:
        nh, _, kh = a.shape
        tm = min(256, m)
        a_spec = pl.BlockSpec((nh, tm, kh), lambda i: (0, i, 0))

        def product(a_ref, b_ref):
            p = _dot(a_ref[0], b_ref[:, 0:kh], _NT)
            for h in range(1, nh):
                p = p + _dot(a_ref[h], b_ref[:, h * kh:(h + 1) * kh], _NT)
            return p
    else:
        tm = _row_tile(m, max(a.shape[1], d))
        a_spec = pl.BlockSpec((tm, a.shape[1]), lambda i: (i, 0))

        def product(a_ref, b_ref):
            return _dot(a_ref[...], b_ref[...], _NT)

    def epilogue(dhv, ins, outs, first):
        x_ref, g_ref, dres_ref = ins
        dg_ref = outs[-1]

        if first:
            @pl.when(pl.program_id(0) == 0)
            def _():
                dg_ref[...] = jnp.zeros_like(dg_ref)

        xv = x_ref[...]
        r = lax.rsqrt(jnp.mean(xv * xv, axis=-1, keepdims=True) + EPS)
        xn = xv * r
        dxn = dhv * g_ref[...]
        dx = r * (dxn - xn * jnp.mean(dxn * xn, axis=-1, keepdims=True)) + dres_ref[...]
        outs[0][...] = dx
        if storage_copy:
            outs[1][...] = dx.astype(outs[1].dtype)
        dg_ref[...] += jnp.sum(dhv * xn, axis=0, keepdims=True)

    row = pl.BlockSpec((tm, d), lambda i: (i, 0))
    vec = pl.BlockSpec((1, d), lambda i: (0, 0))
    copies = [jax.ShapeDtypeStruct((m, d), _ACT)] if storage_copy else []
    return _fused_rows(
        name, a, b, product, a_spec, tm, [x, g, dres], [row, vec, row],
        [jax.ShapeDtypeStruct((m, d), f32)] + copies + [jax.ShapeDtypeStruct((1, d), f32)],
        [row] * (1 + len(copies)) + [vec], epilogue)


def _proj_loss_bwd(name, a, b, res, g, tgt):
    m, k = a.shape
    d = b.shape[1]
    tm = _row_tile(m)

    def epilogue(p, ins, outs, first):
        res_ref, g_ref, t_ref = ins
        dx_ref, dxb_ref, dg_ref, loss_ref = outs

        if first:
            @pl.when(pl.program_id(0) == 0)
            def _():
                dg_ref[...] = jnp.zeros_like(dg_ref)
                loss_ref[...] = jnp.zeros_like(loss_ref)

        xv = p + res_ref[...]
        gv = g_ref[...]
        r = lax.rsqrt(jnp.mean(xv * xv, axis=-1, keepdims=True) + EPS)
        xn = xv * r
        err = xn * gv - t_ref[...]
        loss_ref[...] += 0.5 * jnp.sum(jnp.mean(err * err, axis=-1, keepdims=True), axis=0, keepdims=True)
        dout = err * (1.0 / d)
        dxn = dout * gv
        dx = r * (dxn - xn * jnp.mean(dxn * xn, axis=-1, keepdims=True))
        dx_ref[...] = dx
        dxb_ref[...] = dx.astype(dxb_ref.dtype)
        dg_ref[...] += jnp.sum(dout * xn, axis=0, keepdims=True)

    row = pl.BlockSpec((tm, d), lambda i: (i, 0))
    vec = pl.BlockSpec((1, d), lambda i: (0, 0))
    return _fused_rows(
        name, a, b, lambda a_ref, b_ref: _dot(a_ref[...], b_ref[...], _NN), pl.BlockSpec((tm, k), lambda i: (i, 0)), tm,
        [res, g, tgt], [row, vec, row],
        [jax.ShapeDtypeStruct((m, d), f32), jax.ShapeDtypeStruct((m, d), _ACT), jax.ShapeDtypeStruct((1, d), f32),
         jax.ShapeDtypeStruct((1, 1), f32)],
        [row, row, vec, pl.BlockSpec((1, 1), lambda i: (0, 0))], epilogue)


def _rms_gain_grad(name, x, dh):
    t, d = x.shape
    tm = _row_tile(t)

    def body(x_ref, dh_ref, dg_ref):
        @pl.when(pl.program_id(0) == 0)
        def _():
            dg_ref[...] = jnp.zeros_like(dg_ref)

        xv = x_ref[...]
        r = lax.rsqrt(jnp.mean(xv * xv, axis=-1, keepdims=True) + EPS)
        dg_ref[...] += jnp.sum(dh_ref[...] * (xv * r), axis=0, keepdims=True)

    row = pl.BlockSpec((tm, d), lambda i: (i, 0))
    return pl.pallas_call(
        body, name=name, grid=(t // tm,), in_specs=[row, row], out_specs=pl.BlockSpec((1, d), lambda i: (0, 0)),
        out_shape=jax.ShapeDtypeStruct((1, d), f32), compiler_params=_params(1),
    )(x, dh)


_CONV_ROWS = 256
_CHUNK = 64
_HALO = 32


def _pool_counts(pos, w):
    return jnp.minimum(pos + 1.0, float(w))


def _rows_from(win, start, rows):
    if start % 8 == 0:
        return win[start:start + rows, :]
    n = win.shape[0]
    return pltpu.roll(win, n - start % 8, axis=0)[start - start % 8:start - start % 8 + rows, :]


def _tap_rows(buf, starts, rows):
    for residue in range(8):
        group = [(k, s) for k, s in starts.items() if s % 8 == residue]
        if group:
            lo = min(s for _, s in group) - residue
            hi = max(s for _, s in group) - residue + rows + (8 if residue else 0)
            win = buf[lo:hi, :]
            if residue:
                win = pltpu.roll(win, hi - lo - residue, axis=0)
            for k, s in group:
                yield k, win[s - residue - lo:s - residue - lo + rows, :]


def _mix_fwd(u, cw, cb, lg, lb, pw, ps, seq):
    t, c3 = u.shape
    c = c3 // 3
    kw = 31
    tm = min(_CONV_ROWS, seq)
    tps = seq // tm
    gd = c // len(POOL_WINDOWS)

    def body(u_ref, uh_ref, cw_ref, cb_ref, lg_ref, lb_ref, pw_ref, ps_ref, y_ref, hc_ref, hgbuf, pbuf):
        i = pl.program_id(0)
        keep = jnp.where(i % tps == 0, 0.0, 1.0)
        um = u_ref[...].astype(f32)
        uh = uh_ref[...].astype(f32) * keep
        hgbuf[0:_HALO, :] = uh[:, 0:c] * _sigmoid(uh[:, c:2 * c])
        hgbuf[_HALO:_HALO + tm, :] = um[:, 0:c] * _sigmoid(um[:, c:2 * c])
        pbuf[0:_HALO, :] = uh[:, 2 * c:]
        pbuf[_HALO:_HALO + tm, :] = um[:, 2 * c:]
        for r0 in range(0, tm, _CHUNK):
            acc = jnp.broadcast_to(cb_ref[...], (_CHUNK, c))
            for k, rows in _tap_rows(hgbuf, {k: r0 + _HALO - (kw - 1) + k for k in range(kw)}, _CHUNK):
                acc = acc + cw_ref[k:k + 1, :] * rows
            hc_ref[r0:r0 + _CHUNK, :] = acc
            mu = jnp.mean(acc, axis=-1, keepdims=True)
            xc = acc - mu
            var = jnp.mean(xc * xc, axis=-1, keepdims=True)
            hl = xc * lax.rsqrt(var + EPS) * lg_ref[...] + lb_ref[...]
            y_ref[r0:r0 + _CHUNK, 0:c] = (hl * _sigmoid(hl)).astype(y_ref.dtype)
        pos = ((i % tps) * tm).astype(f32) + lax.broadcasted_iota(jnp.int32, (tm, 1), 0).astype(f32)
        for gi, w in enumerate(POOL_WINDOWS):
            sl = slice(gi * gd, (gi + 1) * gd)
            v = pbuf[_HALO:_HALO + tm, sl]
            s = v
            for j in range(1, w):
                s = s + pbuf[_HALO - j:_HALO - j + tm, sl]
            pooled = s / _pool_counts(pos, w) - v
            mixed = _dot(pooled.astype(_ACT), pw_ref[gi].astype(_ACT), _NN)
            y_ref[:, c + gi * gd:c + (gi + 1) * gd] = (mixed * ps_ref[:, sl]).astype(y_ref.dtype)

    hb = tm // _HALO
    full = lambda shape: pl.BlockSpec(shape, lambda i: (0,) * len(shape))
    return pl.pallas_call(
        body, name="mix_fwd", grid=(t // tm,),
        in_specs=[pl.BlockSpec((tm, c3), lambda i: (i, 0)),
                  pl.BlockSpec((_HALO, c3), lambda i: (jnp.maximum(i * hb - 1, 0), 0)),
                  full((_HALO, c)), full((1, c)), full((1, c)), full((1, c)), full((len(POOL_WINDOWS), gd, gd)), full((1, c))],
        out_specs=[pl.BlockSpec((tm, 2 * c), lambda i: (i, 0)), pl.BlockSpec((tm, c), lambda i: (i, 0))],
        out_shape=[jax.ShapeDtypeStruct((t, 2 * c), _ACT), jax.ShapeDtypeStruct((t, c), f32)],
        scratch_shapes=[pltpu.VMEM((_HALO + tm, c), f32), pltpu.VMEM((_HALO + tm, c), f32)],
        compiler_params=_params(1),
    )(u, u, cw, cb, lg, lb, pw, ps)


def _mix_bwd_norm(hc, dy, lg, lb, seq):
    t, c = hc.shape
    tm = min(_CONV_ROWS, seq)

    def body(hc_ref, dy_ref, lg_ref, lb_ref, dhc_ref, sums_ref):
        @pl.when(pl.program_id(0) == 0)
        def _():
            sums_ref[...] = jnp.zeros_like(sums_ref)

        hcv = hc_ref[...]
        mu = jnp.mean(hcv, axis=-1, keepdims=True)
        xc = hcv - mu
        rstd = lax.rsqrt(jnp.mean(xc * xc, axis=-1, keepdims=True) + EPS)
        n = xc * rstd
        hl = n * lg_ref[...] + lb_ref[...]
        sg = _sigmoid(hl)
        dhl = dy_ref[...].astype(f32) * (sg * (1.0 + hl * (1.0 - sg)))
        dn = dhl * lg_ref[...]
        dhc = rstd * (dn - jnp.mean(dn, axis=-1, keepdims=True) - n * jnp.mean(dn * n, axis=-1, keepdims=True))
        dhc_ref[...] = dhc
        sums_ref[0:1, :] += jnp.sum(dhl * n, axis=0, keepdims=True)
        sums_ref[1:2, :] += jnp.sum(dhl, axis=0, keepdims=True)
        sums_ref[2:3, :] += jnp.sum(dhc, axis=0, keepdims=True)

    row = pl.BlockSpec((tm, c), lambda i: (i, 0))
    vec = pl.BlockSpec((1, c), lambda i: (0, 0))
    return pl.pallas_call(
        body, name="mix_bwd_norm", grid=(t // tm,), in_specs=[row, row, vec, vec],
        out_specs=[row, pl.BlockSpec((8, c), lambda i: (0, 0))],
        out_shape=[jax.ShapeDtypeStruct((t, c), f32), jax.ShapeDtypeStruct((8, c), f32)],
        compiler_params=_params(1),
    )(hc, dy, lg, lb)


def _mix_bwd_taps(u, dhc, dy, cw, pw, ps, seq):
    t, c3 = u.shape
    c = c3 // 3
    kw = 31
    tm = min(_CONV_ROWS, seq)
    tps = seq // tm
    ng = len(POOL_WINDOWS)
    gd = c // ng
    nh = 16

    def body(u_ref, uh_ref, dhc_ref, dhcn_ref, dy_ref, dyn_ref, cw_ref, pw_ref, ps_ref,
             du_ref, dcw_ref, dps_ref, dpw_ref, hgbuf, dcbuf, pbuf, dpbuf):
        i = pl.program_id(0)
        keep_prev = jnp.where(i % tps == 0, 0.0, 1.0)
        keep_next = jnp.where(i % tps == tps - 1, 0.0, 1.0)

        @pl.when(i == 0)
        def _():
            dcw_ref[...] = jnp.zeros_like(dcw_ref)
            dps_ref[...] = jnp.zeros_like(dps_ref)
            dpw_ref[...] = jnp.zeros_like(dpw_ref)

        uh = uh_ref[...].astype(f32) * keep_prev
        hgbuf[0:_HALO, :] = uh[:, 0:c] * _sigmoid(uh[:, c:2 * c])
        pbuf[0:_HALO, :] = uh[:, 2 * c:]
        um = u_ref[...].astype(f32)
        hgbuf[_HALO:_HALO + tm, :] = um[:, 0:c] * _sigmoid(um[:, c:2 * c])
        pbuf[_HALO:_HALO + tm, :] = um[:, 2 * c:]
        dcbuf[0:tm, :] = dhc_ref[...]
        dcbuf[tm:tm + _HALO, :] = dhcn_ref[...] * keep_next
        tap_sums = [None] * kw
        for r0 in range(0, tm, _CHUNK):
            dh = dcbuf[r0:r0 + _CHUNK, :]
            acc = jnp.zeros((_CHUNK, c), f32)
            for k, rows in _tap_rows(hgbuf, {k: r0 + _HALO - (kw - 1) + k for k in range(kw)}, _CHUNK):
                part = (dh * rows).reshape(_CHUNK // 8, 8, c).sum(axis=0)
                tap_sums[k] = part if tap_sums[k] is None else tap_sums[k] + part
            for k, rows in _tap_rows(dcbuf, {k: r0 + (kw - 1) - k for k in range(kw)}, _CHUNK):
                acc = acc + cw_ref[k:k + 1, :] * rows
            val = u_ref[r0:r0 + _CHUNK, 0:c].astype(f32)
            sg = _sigmoid(u_ref[r0:r0 + _CHUNK, c:2 * c].astype(f32))
            du_ref[r0:r0 + _CHUNK, 0:c] = (acc * sg).astype(du_ref.dtype)
            du_ref[r0:r0 + _CHUNK, c:2 * c] = (acc * val * sg * (1.0 - sg)).astype(du_ref.dtype)
        for k in range(kw):
            dcw_ref[k:k + 1, :] += jnp.sum(tap_sums[k], axis=0, keepdims=True)
        base = ((i % tps) * tm).astype(f32)
        pos = base + lax.broadcasted_iota(jnp.int32, (tm, 1), 0).astype(f32)
        pos_next = base + float(tm) + lax.broadcasted_iota(jnp.int32, (nh, 1), 0).astype(f32)
        for gi, w in enumerate(POOL_WINDOWS):
            sl = slice(gi * gd, (gi + 1) * gd)
            v = pbuf[_HALO:_HALO + tm, sl]
            s = v
            for j in range(1, w):
                s = s + pbuf[_HALO - j:_HALO - j + tm, sl]
            cnt = _pool_counts(pos, w)
            pooled = (s / cnt - v).astype(_ACT)
            pwg = pw_ref[gi].astype(_ACT)
            mixed = _dot(pooled, pwg, _NN)
            dyp = dy_ref[:, sl].astype(f32)
            dps_ref[0:1, sl] += jnp.sum(dyp * mixed, axis=0, keepdims=True)
            dmix = (dyp * ps_ref[:, sl]).astype(_ACT)
            dpw_ref[gi] += _dot(pooled, dmix, _TN)
            dmix_next = (dyn_ref[:, sl].astype(f32) * ps_ref[:, sl] * keep_next).astype(_ACT)
            dpool = _dot(dmix, pwg, _NT)
            dpbuf[0:tm, sl] = dpool / cnt
            dpbuf[tm:tm + nh, sl] = _dot(dmix_next, pwg, _NT) / _pool_counts(pos_next, w)
            acc = -dpool
            for j in range(w):
                acc = acc + dpbuf[j:j + tm, sl]
            du_ref[:, 2 * c + gi * gd:2 * c + (gi + 1) * gd] = acc.astype(du_ref.dtype)

    hb = tm // _HALO
    n_halo = t // _HALO
    n_nh = t // nh
    full = lambda shape: pl.BlockSpec(shape, lambda i: (0,) * len(shape))
    return pl.pallas_call(
        body, name="mix_bwd_taps", grid=(t // tm,),
        in_specs=[pl.BlockSpec((tm, c3), lambda i: (i, 0)),
                  pl.BlockSpec((_HALO, c3), lambda i: (jnp.maximum(i * hb - 1, 0), 0)),
                  pl.BlockSpec((tm, c), lambda i: (i, 0)),
                  pl.BlockSpec((_HALO, c), lambda i: (jnp.minimum((i + 1) * hb, n_halo - 1), 0)),
                  pl.BlockSpec((tm, c), lambda i: (i, 1)),
                  pl.BlockSpec((nh, c), lambda i: (jnp.minimum((i + 1) * (tm // nh), n_nh - 1), 1)),
                  full((_HALO, c)), full((ng, gd, gd)), full((1, c))],
        out_specs=[pl.BlockSpec((tm, c3), lambda i: (i, 0)), full((_HALO, c)), full((8, c)), full((ng, gd, gd))],
        out_shape=[jax.ShapeDtypeStruct((t, c3), _ACT), jax.ShapeDtypeStruct((_HALO, c), f32),
                   jax.ShapeDtypeStruct((8, c), f32), jax.ShapeDtypeStruct((ng, gd, gd), f32)],
        scratch_shapes=[pltpu.VMEM((_HALO + tm, c), f32), pltpu.VMEM((tm + _HALO, c), f32),
                        pltpu.VMEM((_HALO + tm, c), f32), pltpu.VMEM((tm + nh, c), f32)],
        compiler_params=_params(1),
    )(u, u, dhc, dhc, dy, dy, cw, pw, ps)


def _attn_fwd(q, kv, n_seq, seq, n_mem):
    t, d = q.shape
    dh = d // XATTN_HEADS
    tq = min(512, seq)
    nq = seq // tq
    scale = dh ** -0.5

    def body(q_ref, kv_ref, o_ref):
        for h in range(XATTN_HEADS):
            cols = slice(h * dh, (h + 1) * dh)
            s = _dot(q_ref[:, cols], kv_ref[:, cols], _NT) * scale
            e = jnp.exp(s - jnp.max(s, axis=-1, keepdims=True))
            p = e / jnp.sum(e, axis=-1, keepdims=True)
            o_ref[:, cols] = _dot(p.astype(_ACT), kv_ref[:, d + h * dh:d + (h + 1) * dh], _NN).astype(o_ref.dtype)

    qs = pl.BlockSpec((tq, d), lambda b, i: (b * nq + i, 0))
    return pl.pallas_call(
        body, name="attn_fwd", grid=(n_seq, nq), in_specs=[qs, pl.BlockSpec((n_mem, 2 * d), lambda b, i: (b, 0))],
        out_specs=qs, out_shape=jax.ShapeDtypeStruct((t, d), _ACT), compiler_params=_params(2),
    )(q, kv)


def _attn_bwd(q, kv, do, n_seq, seq, n_mem):
    t, d = q.shape
    dh = d // XATTN_HEADS
    tq = min(512, seq)
    nq = seq // tq
    scale = dh ** -0.5

    def body(q_ref, kv_ref, do_ref, dq_ref, dkv_ref, acc):
        i = pl.program_id(1)

        @pl.when(i == 0)
        def _():
            acc[...] = jnp.zeros_like(acc)

        for h in range(XATTN_HEADS):
            cols = slice(h * dh, (h + 1) * dh)
            vcols = slice(d + h * dh, d + (h + 1) * dh)
            qv = q_ref[:, cols]
            kh = kv_ref[:, cols]
            dov = do_ref[:, cols]
            s = _dot(qv, kh, _NT) * scale
            e = jnp.exp(s - jnp.max(s, axis=-1, keepdims=True))
            p = e / jnp.sum(e, axis=-1, keepdims=True)
            dp = _dot(dov, kv_ref[:, vcols], _NT)
            ds = (p * (dp - jnp.sum(dp * p, axis=-1, keepdims=True)) * scale).astype(_ACT)
            dq_ref[:, cols] = _dot(ds, kh, _NN).astype(dq_ref.dtype)
            acc[:, cols] += _dot(ds, qv, _TN)
            acc[:, vcols] += _dot(p.astype(_ACT), dov, _TN)

        @pl.when(i == nq - 1)
        def _():
            dkv_ref[...] = acc[...].astype(dkv_ref.dtype)

    qs = pl.BlockSpec((tq, d), lambda b, i: (b * nq + i, 0))
    ms = pl.BlockSpec((n_mem, 2 * d), lambda b, i: (b, 0))
    return pl.pallas_call(
        body, name="attn_bwd", grid=(n_seq, nq), in_specs=[qs, ms, qs], out_specs=[qs, ms],
        out_shape=[jax.ShapeDtypeStruct((t, d), _ACT), jax.ShapeDtypeStruct((n_seq * n_mem, 2 * d), _ACT)],
        scratch_shapes=[pltpu.VMEM((n_mem, 2 * d), f32)], compiler_params=_params(2),
    )(q, kv, do)


_FFN_ROWS = 2048
_FFN_COLS = 256
_FFN_HALO = 16


def _window(buf, g, start, rows):
    return buf[g, pl.ds(start, rows + 8), :]


def _taps3(win, rows):
    return [_rows_from(win, 6 + k, rows) for k in range(3)]


def _conv3(b_ref, w_ref, taps):
    acc = b_ref[...] + w_ref[0:1, :] * taps[0]
    for k in (1, 2):
        acc = acc + w_ref[k:k + 1, :] * taps[k]
    return acc


def _ffn_gate_fwd(up, fw, fb, seq):
    _, t, f = up.shape
    tm = min(_FFN_ROWS, seq)
    tps = seq // tm
    tc = _FFN_COLS
    nc = f // tc
    hl = _FFN_HALO

    def body(up_ref, uph_ref, wg_ref, wv_ref, bg_ref, bv_ref, a_ref):
        i = pl.program_id(1)
        before = uph_ref[...]
        before = jnp.where(i % tps == 0, jnp.zeros_like(before), before)

        def chunk(r0, wins):
            conv = []
            for g, (w_ref, b_ref) in enumerate(((wg_ref, bg_ref), (wv_ref, bv_ref))):
                conv.append(_conv3(b_ref, w_ref, _taps3(wins[g].astype(f32)[hl - 8:, :], _CHUNK)))
            gate, val = conv
            a_ref[pl.ds(r0, _CHUNK), :] = (gate * _sigmoid(gate) * val).astype(a_ref.dtype)

        chunk(0, [jnp.concatenate([before[g], up_ref[g, 0:_CHUNK, :]], axis=0) for g in range(2)])

        def later(ci, carry):
            r0 = pl.multiple_of(ci * _CHUNK, _CHUNK)
            chunk(r0, [up_ref[g, pl.ds(r0 - hl, _CHUNK + hl), :] for g in range(2)])
            return carry

        lax.fori_loop(1, tm // _CHUNK, later, 0)

    hb = tm // hl
    return pl.pallas_call(
        body, name="ffn_gate_fwd", grid=(nc, t // tm),
        in_specs=[pl.BlockSpec((2, tm, tc), lambda j, i: (0, i, j)),
                  pl.BlockSpec((2, hl, tc), lambda j, i: (0, jnp.maximum(i * hb - 1, 0), j)),
                  pl.BlockSpec((8, tc), lambda j, i: (0, j)), pl.BlockSpec((8, tc), lambda j, i: (0, nc + j)),
                  pl.BlockSpec((1, tc), lambda j, i: (0, j)), pl.BlockSpec((1, tc), lambda j, i: (0, nc + j))],
        out_specs=pl.BlockSpec((tm, tc), lambda j, i: (i, j)),
        out_shape=jax.ShapeDtypeStruct((t, f), _ACT), compiler_params=_params(2),
    )(up, up, fw, fw, fb, fb)


def _ffn_gate_bwd(up, da, fw, fb, seq):
    _, t, f = up.shape
    tm = min(_FFN_ROWS, seq)
    tps = seq // tm
    tc = _FFN_COLS
    nc = f // tc
    hl = _FFN_HALO

    def body(up_ref, uph_ref, upn_ref, da_ref, dan_ref, wg_ref, wv_ref, bg_ref, bv_ref,
             dup_ref, sg_ref, sv_ref, dbuf, sums):
        i = pl.program_id(1)
        at_end = i % tps == tps - 1

        @pl.when(i == 0)
        def _():
            sg_ref[...] = jnp.zeros_like(sg_ref)
            sv_ref[...] = jnp.zeros_like(sv_ref)

        sums[...] = jnp.zeros_like(sums)
        before = uph_ref[...]
        before = jnp.where(i % tps == 0, jnp.zeros_like(before), before)
        after = upn_ref[...]
        after = jnp.where(at_end, jnp.zeros_like(after), after)
        w_refs = (wg_ref, wv_ref)
        b_refs = (bg_ref, bv_ref)

        def grads(r0, rows, wins, dav, count):
            taps = [_taps3(wins[g].astype(f32)[hl - 8:, :], rows) for g in range(2)]
            gate, val = [_conv3(b_refs[g], w_refs[g], taps[g]) for g in range(2)]
            sg = _sigmoid(gate)
            douts = (dav * val * (sg * (1.0 + gate * (1.0 - sg))), dav * (gate * sg))
            for g in range(2):
                dbuf[g, pl.ds(r0, rows), :] = douts[g]
                if count:
                    sums[g, 0] += douts[g].reshape(rows // 8, 8, tc).sum(axis=0)
                    for k in range(3):
                        sums[g, 1 + k] += (douts[g] * taps[g][k]).reshape(rows // 8, 8, tc).sum(axis=0)

        grads(0, _CHUNK, [jnp.concatenate([before[g], up_ref[g, 0:_CHUNK, :]], axis=0) for g in range(2)],
              da_ref[0:_CHUNK, :].astype(f32), True)

        def first(ci, carry):
            r0 = pl.multiple_of(ci * _CHUNK, _CHUNK)
            grads(r0, _CHUNK, [up_ref[g, pl.ds(r0 - hl, _CHUNK + hl), :] for g in range(2)],
                  da_ref[pl.ds(r0, _CHUNK), :].astype(f32), True)
            return carry

        lax.fori_loop(1, tm // _CHUNK, first, 0)
        da_after = dan_ref[...].astype(f32)
        grads(tm, hl, [jnp.concatenate([up_ref[g, tm - hl:tm, :], after[g]], axis=0) for g in range(2)],
              jnp.where(at_end, jnp.zeros_like(da_after), da_after), False)

        def second(ci, carry):
            r0 = pl.multiple_of(ci * _CHUNK, _CHUNK)
            for g in range(2):
                win = _window(dbuf, g, r0, _CHUNK)
                acc = jnp.zeros((_CHUNK, tc), f32)
                for k in range(3):
                    acc = acc + w_refs[g][k:k + 1, :] * _rows_from(win, 2 - k, _CHUNK)
                dup_ref[g, pl.ds(r0, _CHUNK), :] = acc.astype(dup_ref.dtype)
            return carry

        lax.fori_loop(0, tm // _CHUNK, second, 0)
        for g, s_ref in enumerate((sg_ref, sv_ref)):
            for r in range(4):
                s_ref[r:r + 1, :] += jnp.sum(sums[g, r], axis=0, keepdims=True)

    hb = tm // hl
    n_halo = t // hl
    return pl.pallas_call(
        body, name="ffn_gate_bwd", grid=(nc, t // tm),
        in_specs=[pl.BlockSpec((2, tm, tc), lambda j, i: (0, i, j)),
                  pl.BlockSpec((2, hl, tc), lambda j, i: (0, jnp.maximum(i * hb - 1, 0), j)),
                  pl.BlockSpec((2, hl, tc), lambda j, i: (0, jnp.minimum((i + 1) * hb, n_halo - 1), j)),
                  pl.BlockSpec((tm, tc), lambda j, i: (i, j)),
                  pl.BlockSpec((hl, tc), lambda j, i: (jnp.minimum((i + 1) * hb, n_halo - 1), j)),
                  pl.BlockSpec((8, tc), lambda j, i: (0, j)), pl.BlockSpec((8, tc), lambda j, i: (0, nc + j)),
                  pl.BlockSpec((1, tc), lambda j, i: (0, j)), pl.BlockSpec((1, tc), lambda j, i: (0, nc + j))],
        out_specs=[pl.BlockSpec((2, tm, tc), lambda j, i: (0, i, j)),
                   pl.BlockSpec((8, tc), lambda j, i: (0, j)), pl.BlockSpec((8, tc), lambda j, i: (0, j))],
        out_shape=[jax.ShapeDtypeStruct((2, t, f), _ACT), jax.ShapeDtypeStruct((8, f), f32), jax.ShapeDtypeStruct((8, f), f32)],
        scratch_shapes=[pltpu.VMEM((2, tm + hl, tc), f32), pltpu.VMEM((2, 4, 8, tc), f32)],
        compiler_params=_params(2),
    )(up, up, up, da, da, fw, fw, fb, fb)


def _adamw_math(w, g, m, v):
    m = ADAM_B1 * m + (1.0 - ADAM_B1) * g
    v = ADAM_B2 * v + (1.0 - ADAM_B2) * (g * g)
    m_hat = m / (1.0 - ADAM_B1 ** ADAM_STEP)
    v_hat = v / (1.0 - ADAM_B2 ** ADAM_STEP)
    delta = -ADAM_LR * (m_hat / (jnp.sqrt(v_hat) + ADAM_EPS) + ADAM_WD * w)
    return delta, m, v


def _adamw_shard(name, w, g, m, v):
    _, r, c = w.shape
    tr = next((cand for cand in (256, 176, 128, 64, 32, 16, 8) if r % cand == 0), r)

    def body(w_ref, g_ref, m_ref, v_ref, go_ref, d_ref, mo_ref, vo_ref):
        gv = g_ref[...]
        d, mn, vn = _adamw_math(w_ref[...], gv, m_ref[...], v_ref[...])
        go_ref[...] = gv
        d_ref[...] = d
        mo_ref[...] = mn
        vo_ref[...] = vn

    s3 = pl.BlockSpec((None, tr, c), lambda i: (0, i, 0))
    s2 = pl.BlockSpec((tr, c), lambda i: (i, 0))
    shp = jax.ShapeDtypeStruct(w.shape, f32)
    return pl.pallas_call(
        body, name=name, grid=(r // tr,), in_specs=[s3, s2, s3, s3], out_specs=[s3] * 4, out_shape=[shp] * 4,
        compiler_params=_params(1),
    )(w, g, m, v)


def _adamw_small(quads):
    n = len(quads)

    def body(*refs):
        ins, outs = refs[:4 * n], refs[4 * n:]
        for p in range(n):
            w_ref, g_ref, m_ref, v_ref = ins[4 * p:4 * p + 4]
            d, mn, vn = _adamw_math(w_ref[...], g_ref[...], m_ref[...], v_ref[...])
            outs[3 * p][...] = d
            outs[3 * p + 1][...] = mn
            outs[3 * p + 2][...] = vn

    flat = [a for q in quads for a in q]
    shapes = [jax.ShapeDtypeStruct(q[0].shape, f32) for q in quads for _ in range(3)]
    outs = pl.pallas_call(
        body, name="adamw_small", in_specs=[_VMEM] * (4 * n), out_specs=[_VMEM] * (3 * n), out_shape=shapes,
        compiler_params=pltpu.CompilerParams(vmem_limit_bytes=_VMEM_LIMIT_BYTES),
    )(*flat)
    return [tuple(outs[3 * p:3 * p + 3]) for p in range(n)]


def _sum_partials(name, place, grads, got):
    _, r, c = grads.shape
    steps = 4 if r % 64 == 0 else 1
    tr = r // steps

    def body(place_ref, own_ref, got_ref, f_ref):
        s = own_ref[...].astype(f32)
        for k in range(got.shape[0]):
            s = s + got_ref[k].astype(f32)
        f_ref[...] = s

    grid_spec = pltpu.PrefetchScalarGridSpec(
        num_scalar_prefetch=1, grid=(steps,),
        in_specs=[pl.BlockSpec((None, tr, c), lambda i, p: (2 * p[0] + p[1], i, 0)),
                  pl.BlockSpec((got.shape[0], tr, c), lambda i, p: (0, i, 0))],
        out_specs=pl.BlockSpec((None, tr, c), lambda i, p: (p[1], i, 0)))
    return pl.pallas_call(body, name=name, grid_spec=grid_spec, out_shape=jax.ShapeDtypeStruct((2, r, c), f32),
                          compiler_params=_params(1))(place, grads, got)


def _place():
    return lax.axis_index("x"), lax.axis_index("y"), lax.axis_index("c")


def _other_chips(x, y):
    return [(1 - x, y), (x, 1 - y), (1 - x, 1 - y)]


def _remote(src, dst, send_sem, recv_sem, to):
    return pltpu.make_async_remote_copy(src_ref=src, dst_ref=dst, send_sem=send_sem, recv_sem=recv_sem,
                                        device_id=to, device_id_type=_MESH)


def _place_shards(place, shards, col_sharded):
    n = len(shards)
    steps = 4

    def body(place_ref, *refs):
        for src, dst in zip(refs[:n], refs[n:]):
            dst[...] = src[...].astype(dst.dtype)

    in_specs, out_specs, out_shape = [], [], []
    for w, col in zip(shards, col_sharded):
        r, cs = w.shape
        tr = r // steps
        in_specs.append(pl.BlockSpec((tr, cs), lambda i, p: (i, 0)))
        if col:
            out_specs.append(pl.BlockSpec((tr, cs), lambda i, p: (i, p[0])))
            out_shape.append(jax.ShapeDtypeStruct((r, 4 * cs), _ACT))
        else:
            out_specs.append(pl.BlockSpec((tr, cs), lambda i, p: (p[0] * steps + i, 0)))
            out_shape.append(jax.ShapeDtypeStruct((4 * r, cs), _ACT))
    grid_spec = pltpu.PrefetchScalarGridSpec(num_scalar_prefetch=1, grid=(steps,), in_specs=in_specs, out_specs=out_specs)
    return pl.pallas_call(body, name="place_shards", grid_spec=grid_spec, out_shape=out_shape,
                          compiler_params=_params(1))(place, *shards)


def _shard_of(ref, col_sharded, s):
    rows, cols = ref.shape
    if col_sharded:
        return ref.at[:, pl.ds(s * (cols // 4), cols // 4)]
    return ref.at[pl.ds(s * (rows // 4), rows // 4), :]


def _part_of(ref, col_sharded, whole, s, h):
    if whole:
        return _shard_of(ref, col_sharded, s)
    rows, cols = ref.shape
    if col_sharded:
        return ref.at[pl.ds(h * (rows // 2), rows // 2), pl.ds(s * (cols // 4), cols // 4)]
    return ref.at[pl.ds((2 * s + h) * (rows // 8), rows // 8), :]


def _allgather_start(bufs, col_sharded, whole, groups):
    n = len(bufs)
    ng = len(groups)

    def body(*refs):
        out = refs[n:2 * n]
        sems = refs[2 * n:]
        x, y, c = _place()
        for g, members in enumerate(groups):
            for i, w in enumerate(members):
                mine = _part_of(out[w], col_sharded[w], whole[w], 2 * x + y, c)
                for j, chip in enumerate(_other_chips(x, y)):
                    _remote(mine, mine, sems[2 * g].at[3 * i + j], sems[2 * g + 1].at[3 * i + j], (*chip, c)).start()

    sem_shapes = [pltpu.SemaphoreType.DMA((3 * len(m),)) for m in groups for _ in range(2)]
    outs = pl.pallas_call(
        body, name="allgather_start", in_specs=[_HBM] * n, out_specs=[_HBM] * n + [_SEM] * (2 * ng),
        out_shape=[pltpu.HBM(b.shape, b.dtype) for b in bufs] + sem_shapes,
        input_output_aliases={i: i for i in range(n)},
        compiler_params=pltpu.CompilerParams(has_side_effects=_EFFECT),
    )(*[pltpu.with_memory_space_constraint(b, pltpu.HBM) for b in bufs])
    return list(outs[:n]), [(outs[n + 2 * g], outs[n + 2 * g + 1]) for g in range(ng)]


def _allgather_relay(name, bufs, col_sharded, whole, sems, after):
    n = len(bufs)

    def body(*refs):
        buf = refs[:n]
        send, recv = refs[n], refs[n + 1]
        out = refs[n + 3:2 * n + 3]
        to_sibling, from_sibling = refs[2 * n + 3:]
        x, y, c = _place()
        for i in range(n):
            mine = _part_of(buf[i], col_sharded[i], whole[i], 2 * x + y, c)
            for j, chip in enumerate(_other_chips(x, y)):
                landed = _part_of(buf[i], col_sharded[i], whole[i], 2 * chip[0] + chip[1], c)
                cp = _remote(mine, landed, send.at[3 * i + j], recv.at[3 * i + j], (*chip, c))
                cp.wait_send()
                cp.wait_recv()
        for i in range(n):
            if not whole[i]:
                for j, chip in enumerate(_other_chips(x, y)):
                    landed = _part_of(out[i], col_sharded[i], False, 2 * chip[0] + chip[1], c)
                    _remote(landed, landed, to_sibling.at[3 * i + j], from_sibling.at[3 * i + j], (x, y, 1 - c)).start()

    outs = pl.pallas_call(
        body, name=name, in_specs=[_HBM] * n + [_SEM, _SEM, _ANY], out_specs=[_HBM] * n + [_SEM, _SEM],
        out_shape=[pltpu.HBM(b.shape, b.dtype) for b in bufs] + [pltpu.SemaphoreType.DMA((3 * n,))] * 2,
        input_output_aliases={i: i for i in range(n)},
        compiler_params=pltpu.CompilerParams(has_side_effects=_EFFECT),
    )(*bufs, *sems, after)
    return list(outs[:n]), (outs[n], outs[n + 1])


def _allgather_wait(name, bufs, col_sharded, whole, sems, after):
    n = len(bufs)

    def body(*refs):
        buf = refs[:n]
        to_sibling, from_sibling = refs[n], refs[n + 1]
        x, y, c = _place()
        for i in range(n):
            if not whole[i]:
                for j, chip in enumerate(_other_chips(x, y)):
                    sent = _part_of(buf[i], col_sharded[i], False, 2 * chip[0] + chip[1], c)
                    landed = _part_of(buf[i], col_sharded[i], False, 2 * chip[0] + chip[1], 1 - c)
                    cp = _remote(sent, landed, to_sibling.at[3 * i + j], from_sibling.at[3 * i + j], (x, y, 1 - c))
                    cp.wait_send()
                    cp.wait_recv()

    return pl.pallas_call(
        body, name=name, in_specs=[_HBM] * n + [_SEM, _SEM, _ANY], out_specs=[_HBM] * n,
        out_shape=[pltpu.HBM(b.shape, b.dtype) for b in bufs],
        input_output_aliases={i: i for i in range(n)},
        compiler_params=pltpu.CompilerParams(has_side_effects=_EFFECT),
    )(*bufs, *sems, after)


def _other_devices(x, y, c):
    flips = [(bx, by, bc) for bx in (0, 1) for by in (0, 1) for bc in (0, 1)][1:]
    return [(1 - x if bx else x, 1 - y if by else y, 1 - c if bc else c) for bx, by, bc in flips]


def _grad_exchange_start(name, grads):
    nw = len(grads)
    lands = [lax.empty((7,) + g.shape[1:], g.dtype) for g in grads]

    def body(*refs):
        src = refs[2 * nw:3 * nw]
        got = refs[3 * nw:4 * nw]
        send, recv, token = refs[4 * nw:]
        x, y, c = _place()
        for w in range(nw):
            for k, (px, py, pc) in enumerate(_other_devices(x, y, c)):
                _remote(src[w].at[4 * px + 2 * py + pc], got[w].at[k], send.at[7 * w + k], recv.at[7 * w + k], (px, py, pc)).start()
        token[...] = jnp.zeros_like(token)

    outs = pl.pallas_call(
        body, name=name, in_specs=[_HBM] * (2 * nw), out_specs=[_HBM] * (2 * nw) + [_SEM, _SEM, _VMEM],
        out_shape=[pltpu.HBM(a.shape, a.dtype) for a in list(grads) + lands]
        + [pltpu.SemaphoreType.DMA((7 * nw,)), pltpu.SemaphoreType.DMA((7 * nw,)), jax.ShapeDtypeStruct((8, 128), f32)],
        input_output_aliases={i: i for i in range(2 * nw)},
        compiler_params=pltpu.CompilerParams(has_side_effects=_EFFECT),
    )(*[pltpu.with_memory_space_constraint(a, pltpu.HBM) for a in list(grads) + lands])
    return list(outs[:nw]), list(outs[nw:2 * nw]), (outs[2 * nw], outs[2 * nw + 1]), outs[2 * nw + 2]


def _grad_exchange_wait(name, grads, got, sems, after):
    nw = len(grads)

    def body(*refs):
        src = refs[:nw]
        land = refs[nw:2 * nw]
        send, recv = refs[2 * nw], refs[2 * nw + 1]
        x, y, c = _place()
        for w in range(nw):
            for k, (px, py, pc) in enumerate(_other_devices(x, y, c)):
                cp = _remote(src[w].at[4 * px + 2 * py + pc], land[w].at[k], send.at[7 * w + k], recv.at[7 * w + k], (px, py, pc))
                cp.wait_send()
                cp.wait_recv()

    outs = pl.pallas_call(
        body, name=name, in_specs=[_HBM] * (2 * nw) + [_SEM, _SEM, _ANY], out_specs=[_HBM] * (2 * nw),
        out_shape=[pltpu.HBM(a.shape, a.dtype) for a in list(grads) + list(got)],
        input_output_aliases={i: i for i in range(2 * nw)},
        compiler_params=pltpu.CompilerParams(has_side_effects=_EFFECT),
    )(*grads, *got, *sems, after)
    return list(outs[:nw]), list(outs[nw:])


def _swap_halves(finals):
    nw = len(finals)

    def body(*refs):
        buf = refs[nw:2 * nw]
        send_sem, recv_sem = refs[2 * nw:]
        x, y, c = _place()
        sends = []
        for w in range(nw):
            rc = _remote(buf[w].at[c], buf[w].at[c], send_sem.at[w], recv_sem.at[w], (x, y, 1 - c))
            rc.start()
            sends.append(rc)
        for w in range(nw):
            _remote(buf[w].at[1 - c], buf[w].at[1 - c], send_sem.at[w], recv_sem.at[w], (x, y, c)).wait_recv()
        for rc in sends:
            rc.wait_send()

    return pl.pallas_call(
        body, name="rs_swap_halves", in_specs=[_ANY] * nw, out_specs=[_ANY] * nw,
        out_shape=[jax.ShapeDtypeStruct(g.shape, g.dtype) for g in finals],
        input_output_aliases={i: i for i in range(nw)},
        scratch_shapes=[pltpu.SemaphoreType.DMA((nw,)), pltpu.SemaphoreType.DMA((nw,))],
    )(*finals)


def _half_slices(shape, h):
    rows, cols = shape
    if cols % 256 == 0:
        return (slice(None), slice(h * (cols // 2), (h + 1) * (cols // 2)))
    return (slice(h * (rows // 2), (h + 1) * (rows // 2)), slice(None))


def _allreduce_small(parts):
    n = len(parts)

    def body(*refs):
        src = refs[:n]
        out = refs[n:2 * n]
        sib = refs[2 * n:3 * n]
        chip_sum = refs[3 * n:4 * n]
        slots = refs[4 * n:5 * n]
        pair_send, pair_recv, ici_send, ici_recv, swap_send, swap_recv = refs[5 * n:]
        x, y, c = _place()
        me_chip = 2 * x + y
        chips = _other_chips(x, y)
        pairs = [_remote(src[a], sib[a], pair_send.at[a], pair_recv.at[a], (x, y, 1 - c)) for a in range(n)]
        for rc in pairs:
            rc.start()
        for a in range(n):
            pairs[a].wait_recv()
            chip_sum[a][...] = src[a][...] + sib[a][...]
        for h in (0, 1):
            @pl.when(c == h)
            def _():
                sends = []
                for a in range(n):
                    idx = _half_slices(parts[a].shape, h)
                    for j, chip in enumerate(chips):
                        rc = _remote(chip_sum[a].at[idx], slots[a].at[me_chip].at[idx], ici_send.at[3 * a + j], ici_recv.at[3 * a + j], (*chip, h))
                        rc.start()
                        sends.append(rc)
                    slots[a][(me_chip,) + idx] = chip_sum[a][idx]
                for a in range(n):
                    idx = _half_slices(parts[a].shape, h)
                    for j, chip in enumerate(chips):
                        landed = slots[a].at[2 * chip[0] + chip[1]].at[idx]
                        _remote(landed, landed, ici_send.at[3 * a + j], ici_recv.at[3 * a + j], (x, y, c)).wait_recv()
                    total = slots[a][(0,) + idx]
                    for s in range(1, 4):
                        total = total + slots[a][(s,) + idx]
                    out[a][idx] = total
                    rc = _remote(out[a].at[idx], out[a].at[idx], swap_send.at[a], swap_recv.at[a], (x, y, 1 - h))
                    rc.start()
                    sends.append(rc)
                for a in range(n):
                    other = out[a].at[_half_slices(parts[a].shape, 1 - h)]
                    _remote(other, other, swap_send.at[a], swap_recv.at[a], (x, y, c)).wait_recv()
                for rc in sends:
                    rc.wait_send()
        for rc in pairs:
            rc.wait_send()

    return pl.pallas_call(
        body, name="allreduce_small", in_specs=[_VMEM] * n, out_specs=[_VMEM] * n,
        out_shape=[jax.ShapeDtypeStruct(p.shape, f32) for p in parts],
        scratch_shapes=[pltpu.VMEM(p.shape, f32) for p in parts] * 2 + [pltpu.VMEM((4,) + p.shape, f32) for p in parts]
        + [pltpu.SemaphoreType.DMA((n,)), pltpu.SemaphoreType.DMA((n,)), pltpu.SemaphoreType.DMA((3 * n,)),
           pltpu.SemaphoreType.DMA((3 * n,)), pltpu.SemaphoreType.DMA((n,)), pltpu.SemaphoreType.DMA((n,))],
        compiler_params=pltpu.CompilerParams(vmem_limit_bytes=_VMEM_LIMIT_BYTES),
    )(*parts)


def _local_step(x, mem, tgt, g_mix, g_xattn, g_mem, g_ffn, g_final, cb, lg, lb, pw, ps, fb, relay, weights, reduce, n_seq, seq, n_mem):
    t, d = x.shape
    f = fb.shape[1] // 2
    c = cb.shape[1]
    h1 = _rms_fwd("norm_mix", x, g_mix)
    relay(0, h1)
    w_in, cw, fw = weights(0, h1)
    u = _mm_nn("proj_in", h1, w_in, _ACT, w_in.shape[1])
    y, hc = _mix_fwd(u, cw, cb, lg, lb, pw, ps, seq)
    relay(1, y)
    w_out, w_q, w_kv, w_o = weights(1, y)
    x1, h2 = _proj_residual_norm("proj_out", y, w_out, x, g_xattn)
    q = _mm_nn("proj_q", h2, w_q, _ACT, d)
    mem_n = _rms_fwd("norm_mem", mem, g_mem)
    kv = _mm_nn("proj_kv", mem_n, w_kv, _ACT, 2 * d)
    o = _attn_fwd(q, kv, n_seq, seq, n_mem)
    relay(2, o)
    x2, h3 = _proj_residual_norm("proj_o", o, w_o, x1, g_ffn)
    w_up, w_down = weights(2, h3)
    up = _mm_nn("proj_up", h3, w_up, _ACT, f, split_out=True)
    a = _ffn_gate_fwd(up, fw, fb, seq)
    dx3, dx3b, dg_final, loss = _proj_loss_bwd("proj_down", a, w_down, x2, g_final, tgt)
    da = _mm_nt("d_act", dx3b, w_down, _ACT)
    gw_down = _mm_tn_rows("dw_down", a, dx3b, f // 2, d // 2)
    dup, sums_g, sums_v = _ffn_gate_bwd(up, da, fw, fb, seq)
    gw_up = _mm_tn_pieces("dw_up", h3, dup, f // 2, t)
    token = reduce(0, [gw_down.reshape(8, -1, d), gw_up])
    dx2, dx2b, dg_ffn = _dproj_rms_bwd("d_h3", dup, w_up, x2, g_ffn + token, dx3)
    do = _mm_nt("d_o", dx2b, w_o, _ACT)
    gw_o = _mm_tn_rows("dw_o", o, dx2b, d, d // 2)
    dq, dkv = _attn_bwd(q, kv, do, n_seq, seq, n_mem)
    gw_q = _mm_tn_rows("dw_q", h2, dq, d, d // 2)
    gw_kv = _mm_tn_pieces("dw_kv", mem_n, dkv, d // 2, mem.shape[0])
    dmem_n = _mm_nt("d_mem_n", dkv, w_kv, f32)
    dg_mem = _rms_gain_grad("norm_mem_bwd", mem, dmem_n)
    dx1, dx1b, dg_xattn = _dproj_rms_bwd("d_h2", dq, w_q, x1, g_xattn, dx2)
    dy = _mm_nt("d_y", dx1b, w_out, _ACT)
    gw_out = _mm_tn_rows("dw_out", y, dx1b, d, d // 2)
    token = reduce(1, [gw_o.reshape(8, -1, d), gw_q.reshape(8, -1, d), gw_kv, gw_out.reshape(8, -1, d)])
    dhc, sums_norm = _mix_bwd_norm(hc, dy, lg + token, lb, seq)
    du, d_cw, d_ps, d_pw = _mix_bwd_taps(u, dhc, dy, cw, pw, ps, seq)
    gw_in = _mm_tn_pieces("dw_in", h1, du, c * 3 // 4, t)
    token = reduce(2, [gw_in])
    grad_x, dg_mix = _dproj_rms_bwd("d_h1", du, w_in, x, g_mix + token, dx1, storage_copy=False)
    zero_row = jnp.zeros((1, d), f32)
    gains = jnp.concatenate([dg_mix, dg_xattn, dg_mem, dg_ffn, dg_final, jnp.pad(loss, ((0, 0), (0, d - 1))), zero_row, zero_row], axis=0)
    conv_rows = jnp.concatenate([sums_norm[2:3], sums_norm[0:1], sums_norm[1:2], d_ps[0:1], jnp.zeros((4, c), f32)], axis=0)
    ffn_rows = jnp.concatenate([sums_g, sums_v], axis=1)
    small = [gains, conv_rows, d_pw.reshape(-1, d_pw.shape[-1]), ffn_rows, d_cw]
    return grad_x, small


def kernel(x, mem, norm_mix_g, w_in, conv_dw_w, conv_dw_b, conv_ln_g, conv_ln_b, pool_w, pool_scale, w_out, norm_xattn_g, norm_mem_g, w_q, w_kv, w_o, norm_ffn_g, w_up, ffn_dw_w, ffn_dw_b, w_down, norm_final_g, loss_target, m_norm_mix_g, m_w_in, m_conv_dw_w, m_conv_dw_b, m_conv_ln_g, m_conv_ln_b, m_pool_w, m_pool_scale, m_w_out, m_norm_xattn_g, m_norm_mem_g, m_w_q, m_w_kv, m_w_o, m_norm_ffn_g, m_w_up, m_ffn_dw_w, m_ffn_dw_b, m_w_down, m_norm_final_g, v_norm_mix_g, v_w_in, v_conv_dw_w, v_conv_dw_b, v_conv_ln_g, v_conv_ln_b, v_pool_w, v_pool_scale, v_w_out, v_norm_xattn_g, v_norm_mem_g, v_w_q, v_w_kv, v_w_o, v_norm_ffn_g, v_w_up, v_ffn_dw_w, v_ffn_dw_b, v_w_down, v_norm_final_g):
    n_seq, seq, d = x.shape
    n_mem = mem.shape[1]
    chip = 2 * lax.axis_index("x") + lax.axis_index("y")

    place = jnp.stack([chip, lax.axis_index("c")]).astype(jnp.int32)

    col_w = [w_in, w_kv, w_up]
    row_w = [w_out, w_q, w_o, w_down]
    col_flags = [True] * 3 + [False] * 4 + [True] * 2
    kw = conv_dw_w.shape[1]

    def padded_in_place(shard, rows):
        full = jnp.zeros((rows, 4 * shard.shape[1]), shard.dtype)
        return lax.dynamic_update_slice(full, shard, (0, chip * shard.shape[1]))

    bufs = list(_place_shards(place, [w[0] for w in col_w + row_w], col_flags[:7]))
    bufs += [padded_in_place(conv_dw_w[0], _HALO), padded_in_place(ffn_dw_w[0], 8)]
    groups = [[0, 7, 8], [3, 4, 1, 5], [2, 6]]
    whole = [False] * 7 + [True] * 2
    bufs, sems = _allgather_start(bufs, col_flags, whole, groups)
    relayed = {}

    def relay(g, after):
        members = groups[g]
        relayed[g] = _allgather_relay("allgather_relay_%d" % g, [bufs[i] for i in members], [col_flags[i] for i in members],
                                      [whole[i] for i in members], sems[g], after)

    def weights(g, after):
        members = groups[g]
        group_bufs, sibling_sems = relayed[g]
        return _allgather_wait("allgather_wait_%d" % g, group_bufs, [col_flags[i] for i in members],
                               [whole[i] for i in members], sibling_sems, after)

    names = ["w_in", "w_kv", "w_up", "w_out", "w_q", "w_o", "w_down"]
    reduce_groups = [["w_down", "w_up"], ["w_o", "w_q", "w_kv", "w_out"], ["w_in"]]
    in_flight = {}

    def reduce(g, grads):
        grads, lands, rs_sems, token = _grad_exchange_start("rs_start_%d" % g, grads)
        in_flight[g] = (grads, lands, rs_sems)
        return token[0:1, 0:1]

    grad_x, small = _local_step(
        x.reshape(n_seq * seq, d), mem.reshape(n_seq * n_mem, d), loss_target.reshape(n_seq * seq, d),
        norm_mix_g, norm_xattn_g, norm_mem_g, norm_ffn_g, norm_final_g.reshape(1, d),
        conv_dw_b, conv_ln_g, conv_ln_b, pool_w[0], pool_scale, ffn_dw_b, relay, weights, reduce, n_seq, seq, n_mem)

    finals = {}
    for g, members in enumerate(reduce_groups):
        grads, lands, rs_sems = in_flight[g]
        grads, lands = _grad_exchange_wait("rs_wait_%d" % g, grads, lands, rs_sems, grad_x)
        for n, a, b in zip(members, grads, lands):
            finals[n] = _sum_partials("rs_sum_" + n, place, a, b)
    shard_grads = _swap_halves([finals[n] for n in names])

    gains, conv_rows, d_pw, ffn_rows, d_cw = _allreduce_small(small)
    loss = gains[5, 0]

    outs = {}
    big_w = dict(zip(names, col_w + row_w))
    big_m = dict(w_in=m_w_in, w_kv=m_w_kv, w_up=m_w_up, w_out=m_w_out, w_q=m_w_q, w_o=m_w_o, w_down=m_w_down)
    big_v = dict(w_in=v_w_in, w_kv=v_w_kv, w_up=v_w_up, w_out=v_w_out, w_q=v_w_q, w_o=v_w_o, w_down=v_w_down)
    for n, g in zip(names, shard_grads):
        w = big_w[n]
        g2 = g.reshape(w.shape[1], w.shape[2])
        outs[n] = tuple(_adamw_shard("adamw_" + n, w, g2, big_m[n], big_v[n]))

    f2 = ffn_dw_b.shape[1]
    cs_c = conv_dw_w.shape[2]
    cs_f = ffn_dw_w.shape[2]
    g_cw = lax.dynamic_slice(d_cw, (0, chip * cs_c), (kw, cs_c)).reshape(conv_dw_w.shape)
    g_fw = lax.dynamic_slice(ffn_rows, (1, chip * cs_f), (ffn_dw_w.shape[1], cs_f)).reshape(ffn_dw_w.shape)
    small_params = [
        ("norm_mix_g", norm_mix_g, gains[0:1], m_norm_mix_g, v_norm_mix_g),
        ("conv_dw_w", conv_dw_w, g_cw, m_conv_dw_w, v_conv_dw_w),
        ("conv_dw_b", conv_dw_b, conv_rows[0:1], m_conv_dw_b, v_conv_dw_b),
        ("conv_ln_g", conv_ln_g, conv_rows[1:2], m_conv_ln_g, v_conv_ln_g),
        ("conv_ln_b", conv_ln_b, conv_rows[2:3], m_conv_ln_b, v_conv_ln_b),
        ("pool_w", pool_w, d_pw.reshape(pool_w.shape), m_pool_w, v_pool_w),
        ("pool_scale", pool_scale, conv_rows[3:4], m_pool_scale, v_pool_scale),
        ("norm_xattn_g", norm_xattn_g, gains[1:2], m_norm_xattn_g, v_norm_xattn_g),
        ("norm_mem_g", norm_mem_g, gains[2:3], m_norm_mem_g, v_norm_mem_g),
        ("norm_ffn_g", norm_ffn_g, gains[3:4], m_norm_ffn_g, v_norm_ffn_g),
        ("ffn_dw_w", ffn_dw_w, g_fw, m_ffn_dw_w, v_ffn_dw_w),
        ("ffn_dw_b", ffn_dw_b, ffn_rows[0:1, :f2], m_ffn_dw_b, v_ffn_dw_b),
        ("norm_final_g", norm_final_g.reshape(1, d), gains[4:5], m_norm_final_g.reshape(1, d), v_norm_final_g.reshape(1, d)),
    ]
    quads = []
    for _, w, g, m, v in small_params:
        shape2 = (-1, w.shape[-1])
        quads.append((w.reshape(shape2), g.reshape(shape2), m.reshape(shape2), v.reshape(shape2)))
    for (n, w, g, _, _), (delta, new_m, new_v) in zip(small_params, _adamw_small(quads)):
        shape = norm_final_g.shape if n == "norm_final_g" else w.shape
        outs[n] = (g.reshape(shape), delta.reshape(shape), new_m.reshape(shape), new_v.reshape(shape))

    order = ["norm_mix_g", "w_in", "conv_dw_w", "conv_dw_b", "conv_ln_g", "conv_ln_b", "pool_w", "pool_scale", "w_out",
             "norm_xattn_g", "norm_mem_g", "w_q", "w_kv", "w_o", "norm_ffn_g", "w_up", "ffn_dw_w", "ffn_dw_b", "w_down",
             "norm_final_g"]
    return (loss, grad_x.reshape(x.shape), *[outs[n][0] for n in order], *[outs[n][1] for n in order],
            *[outs[n][2] for n in order], *[outs[n][3] for n in order])
```

```python
import functools

import jax
import jax.numpy as jnp
from jax import lax
from jax.experimental import pallas as pl
from jax.experimental.pallas import tpu as pltpu

f32 = jnp.float32
_ACT = jnp.bfloat16

EPS = 1e-6
POOL_WINDOWS = (2, 4, 8, 16)
XATTN_HEADS = 4
ADAM_LR = 0.001
ADAM_B1 = 0.9
ADAM_B2 = 0.999
ADAM_EPS = 1e-08
ADAM_WD = 0.01
ADAM_STEP = 10

_VMEM_LIMIT_BYTES = 56 * 1024 * 1024
_MESH = pl.DeviceIdType.MESH
_ANY = pl.BlockSpec(memory_space=pl.ANY)
_VMEM = pl.BlockSpec(memory_space=pltpu.VMEM)
_HBM = pl.BlockSpec(memory_space=pltpu.HBM)
_SEM = pl.BlockSpec(memory_space=pltpu.SEMAPHORE)
_EFFECT = pltpu.SideEffectType.DATAFLOW_SIDE_EFFECTING

_NN = (((1,), (0,)), ((), ()))
_NT = (((1,), (1,)), ((), ()))
_TN = (((0,), (0,)), ((), ()))


def _params(n_grid):
    return pltpu.CompilerParams(dimension_semantics=("arbitrary",) * n_grid, vmem_limit_bytes=_VMEM_LIMIT_BYTES)


def _sigmoid(v):
    return 1.0 / (1.0 + jnp.exp(-v))


def _dot(a, b, dims):
    return lax.dot_general(a, b, dims, preferred_element_type=f32)


def _mm(name, a, b, *, dims, grid, a_spec, b_spec, o_spec, out_shape, nk, acc_shape=None, res=None, res_spec=None):
    def body(*refs):
        if res is None:
            a_ref, b_ref, o_ref, *scratch = refs
            r_ref = None
        else:
            a_ref, b_ref, r_ref, o_ref, *scratch = refs
        p = _dot(a_ref[...], b_ref[...], dims)

        def finish(v):
            if r_ref is not None:
                v = v + r_ref[...]
            o_ref[...] = v.astype(o_ref.dtype)

        if nk == 1:
            finish(p)
        else:
            acc = scratch[0]
            k = pl.program_id(2)

            @pl.when(k == 0)
            def _():
                acc[...] = p

            @pl.when(k > 0)
            def _():
                acc[...] += p

            @pl.when(k == nk - 1)
            def _():
                finish(acc[...])

    ins = [a, b] + ([] if res is None else [res])
    specs = [a_spec, b_spec] + ([] if res is None else [res_spec])
    return pl.pallas_call(
        body, name=name, grid=grid, in_specs=specs, out_specs=o_spec, out_shape=out_shape,
        scratch_shapes=[pltpu.VMEM(acc_shape, f32)] if nk > 1 else [], compiler_params=_params(3),
    )(*ins)


_NARROW = 2816


def _row_tile(m, width=_NARROW + 1):
    return min(1024 if width <= _NARROW else 512, m)


def _mm_nn(name, a, b, out_dtype, tn, res=None, split_out=False):
    m, k = a.shape
    n = b.shape[1]
    tm = _row_tile(m, max(k, tn))
    if split_out:
        out_shape = jax.ShapeDtypeStruct((n // tn, m, tn), out_dtype)
        o_spec = pl.BlockSpec((None, tm, tn), lambda j, i, kk: (j, i, 0))
    else:
        out_shape = jax.ShapeDtypeStruct((m, n), out_dtype)
        o_spec = pl.BlockSpec((tm, tn), lambda j, i, kk: (i, j))
    return _mm(
        name, a, b, dims=_NN, grid=(n // tn, m // tm, 1), nk=1,
        a_spec=pl.BlockSpec((tm, k), lambda j, i, kk: (i, 0)),
        b_spec=pl.BlockSpec((k, tn), lambda j, i, kk: (0, j)),
        o_spec=o_spec, out_shape=out_shape, res=res,
        res_spec=pl.BlockSpec((tm, tn), lambda j, i, kk: (i, j)),
    )


def _mm_nt(name, a, b, out_dtype):
    n, kc = b.shape
    m = a.shape[0]
    tm = _row_tile(m, max(n, kc))
    return _mm(
        name, a, b, dims=_NT, grid=(m // tm, 1, 1), nk=1,
        a_spec=pl.BlockSpec((tm, kc), lambda i, j, k: (i, 0)),
        b_spec=pl.BlockSpec((n, kc), lambda i, j, k: (0, 0), pipeline_mode=pl.Buffered(1)),
        o_spec=pl.BlockSpec((tm, n), lambda i, j, k: (i, 0)),
        out_shape=jax.ShapeDtypeStruct((m, n), out_dtype),
    )


def _mm_tn_rows(name, a, b, tka, tn):
    m, ka = a.shape
    nb = b.shape[1]
    return _mm(
        name, a, b, dims=_TN, grid=(ka // tka, nb // tn, 1), nk=1,
        a_spec=pl.BlockSpec((m, tka), lambda i, j, k: (0, i)),
        b_spec=pl.BlockSpec((m, tn), lambda i, j, k: (0, j)),
        o_spec=pl.BlockSpec((tka, tn), lambda i, j, k: (i, j)),
        out_shape=jax.ShapeDtypeStruct((ka, nb), _ACT),
    )


def _mm_tn_pieces(name, a, b, cs, tt):
    m, ka = a.shape
    nk = m // tt
    if b.ndim == 3:
        b_spec = pl.BlockSpec((None, tt, cs), lambda i, j, k: (j // 2, k, j % 2))
    else:
        b_spec = pl.BlockSpec((tt, cs), lambda i, j, k: (k, j))
    return _mm(
        name, a, b, dims=_TN, grid=(2, 4, nk), nk=nk, acc_shape=(ka // 2, cs),
        a_spec=pl.BlockSpec((tt, ka // 2), lambda i, j, k: (k, i)), b_spec=b_spec,
        o_spec=pl.BlockSpec((None, ka // 2, cs), lambda i, j, k: (2 * j + i, 0, 0)),
        out_shape=jax.ShapeDtypeStruct((8, ka // 2, cs), _ACT),
    )


def _rms_fwd(name, x, g):
    t, d = x.shape
    tm = _row_tile(t, d)

    def body(x_ref, g_ref, h_ref):
        xv = x_ref[...]
        r = lax.rsqrt(jnp.mean(xv * xv, axis=-1, keepdims=True) + EPS)
        h_ref[...] = (xv * r * g_ref[...]).astype(h_ref.dtype)

    return pl.pallas_call(
        body, name=name, grid=(t // tm,),
        in_specs=[pl.BlockSpec((tm, d), lambda i: (i, 0)), pl.BlockSpec((1, d), lambda i: (0, 0))],
        out_specs=pl.BlockSpec((tm, d), lambda i: (i, 0)), out_shape=jax.ShapeDtypeStruct((t, d), _ACT),
        compiler_params=_params(1),
    )(x, g)


def _fused_rows(name, a, b, product, a_spec, tm, extras, extra_specs, out_shape, out_specs, epilogue):
    ne = len(extras)

    def body(a_ref, b_ref, *refs):
        epilogue(product(a_ref, b_ref), refs[:ne], refs[ne:])

    m = extras[0].shape[0]
    return pl.pallas_call(
        body, name=name, grid=(m // tm,),
        in_specs=[a_spec, pl.BlockSpec(b.shape, lambda i: (0, 0), pipeline_mode=pl.Buffered(1)), *extra_specs],
        out_specs=out_specs, out_shape=out_shape, compiler_params=_params(1),
    )(a, b, *extras)


def _proj_residual_norm(name, a, b, res, g):
    m, k = a.shape
    d = b.shape[1]
    tm = _row_tile(m, max(k, d))

    def epilogue(p, ins, outs):
        xv = p + ins[0][...]
        outs[0][...] = xv
        r = lax.rsqrt(jnp.mean(xv * xv, axis=-1, keepdims=True) + EPS)
        outs[1][...] = (xv * r * ins[1][...]).astype(outs[1].dtype)

    row = pl.BlockSpec((tm, d), lambda i: (i, 0))
    return _fused_rows(
        name, a, b, lambda a_ref, b_ref: _dot(a_ref[...], b_ref[...], _NN), pl.BlockSpec((tm, k), lambda i: (i, 0)), tm,
        [res, g], [row, pl.BlockSpec((1, d), lambda i: (0, 0))],
        [jax.ShapeDtypeStruct((m, d), f32), jax.ShapeDtypeStruct((m, d), _ACT)], [row, row], epilogue)


def _dproj_rms_bwd(name, a, b, x, g, dres, storage_copy=True):
    m, d = x.shape
    if a.ndim == 3:
        nh, _, kh = a.shape
        tm = _row_tile(m, nh * kh)
        a_spec = pl.BlockSpec((nh, tm, kh), lambda i: (0, i, 0))

        def product(a_ref, b_ref):
            p = _dot(a_ref[0], b_ref[:, 0:kh], _NT)
            for h in range(1, nh):
                p = p + _dot(a_ref[h], b_ref[:, h * kh:(h + 1) * kh], _NT)
            return p
    else:
        tm = _row_tile(m, max(a.shape[1], d))
        a_spec = pl.BlockSpec((tm, a.shape[1]), lambda i: (i, 0))

        def product(a_ref, b_ref):
            return _dot(a_ref[...], b_ref[...], _NT)

    def epilogue(dhv, ins, outs):
        x_ref, g_ref, dres_ref = ins
        dg_ref = outs[-1]

        @pl.when(pl.program_id(0) == 0)
        def _():
            dg_ref[...] = jnp.zeros_like(dg_ref)

        xv = x_ref[...]
        r = lax.rsqrt(jnp.mean(xv * xv, axis=-1, keepdims=True) + EPS)
        xn = xv * r
        dxn = dhv * g_ref[...]
        dx = r * (dxn - xn * jnp.mean(dxn * xn, axis=-1, keepdims=True)) + dres_ref[...]
        outs[0][...] = dx
        if storage_copy:
            outs[1][...] = dx.astype(outs[1].dtype)
        dg_ref[...] += jnp.sum(dhv * xn, axis=0, keepdims=True)

    row = pl.BlockSpec((tm, d), lambda i: (i, 0))
    vec = pl.BlockSpec((1, d), lambda i: (0, 0))
    copies = [jax.ShapeDtypeStruct((m, d), _ACT)] if storage_copy else []
    return _fused_rows(
        name, a, b, product, a_spec, tm, [x, g, dres], [row, vec, row],
        [jax.ShapeDtypeStruct((m, d), f32)] + copies + [jax.ShapeDtypeStruct((1, d), f32)],
        [row] * (1 + len(copies)) + [vec], epilogue)


def _proj_loss_bwd(name, a, b, res, g, tgt):
    m, k = a.shape
    d = b.shape[1]
    tm = _row_tile(m, max(k, d))

    def epilogue(p, ins, outs):
        res_ref, g_ref, t_ref = ins
        dx_ref, dxb_ref, dg_ref, loss_ref = outs

        @pl.when(pl.program_id(0) == 0)
        def _():
            dg_ref[...] = jnp.zeros_like(dg_ref)
            loss_ref[...] = jnp.zeros_like(loss_ref)

        xv = p + res_ref[...]
        gv = g_ref[...]
        r = lax.rsqrt(jnp.mean(xv * xv, axis=-1, keepdims=True) + EPS)
        xn = xv * r
        err = xn * gv - t_ref[...]
        loss_ref[...] += 0.5 * jnp.sum(jnp.mean(err * err, axis=-1, keepdims=True), axis=0, keepdims=True)
        dout = err * (1.0 / d)
        dxn = dout * gv
        dx = r * (dxn - xn * jnp.mean(dxn * xn, axis=-1, keepdims=True))
        dx_ref[...] = dx
        dxb_ref[...] = dx.astype(dxb_ref.dtype)
        dg_ref[...] += jnp.sum(dout * xn, axis=0, keepdims=True)

    row = pl.BlockSpec((tm, d), lambda i: (i, 0))
    vec = pl.BlockSpec((1, d), lambda i: (0, 0))
    return _fused_rows(
        name, a, b, lambda a_ref, b_ref: _dot(a_ref[...], b_ref[...], _NN), pl.BlockSpec((tm, k), lambda i: (i, 0)), tm,
        [res, g, tgt], [row, vec, row],
        [jax.ShapeDtypeStruct((m, d), f32), jax.ShapeDtypeStruct((m, d), _ACT), jax.ShapeDtypeStruct((1, d), f32),
         jax.ShapeDtypeStruct((1, 1), f32)],
        [row, row, vec, pl.BlockSpec((1, 1), lambda i: (0, 0))], epilogue)


def _rms_gain_grad(name, x, dh):
    t, d = x.shape
    tm = _row_tile(t)

    def body(x_ref, dh_ref, dg_ref):
        @pl.when(pl.program_id(0) == 0)
        def _():
            dg_ref[...] = jnp.zeros_like(dg_ref)

        xv = x_ref[...]
        r = lax.rsqrt(jnp.mean(xv * xv, axis=-1, keepdims=True) + EPS)
        dg_ref[...] += jnp.sum(dh_ref[...] * (xv * r), axis=0, keepdims=True)

    row = pl.BlockSpec((tm, d), lambda i: (i, 0))
    return pl.pallas_call(
        body, name=name, grid=(t // tm,), in_specs=[row, row], out_specs=pl.BlockSpec((1, d), lambda i: (0, 0)),
        out_shape=jax.ShapeDtypeStruct((1, d), f32), compiler_params=_params(1),
    )(x, dh)


_CONV_ROWS = 256
_CHUNK = 64
_HALO = 32


def _pool_counts(pos, w):
    return jnp.minimum(pos + 1.0, float(w))


def _rows_from(win, start, rows):
    if start % 8 == 0:
        return win[start:start + rows, :]
    n = win.shape[0]
    return pltpu.roll(win, n - start % 8, axis=0)[start - start % 8:start - start % 8 + rows, :]


def _tap_rows(buf, starts, rows):
    for residue in range(8):
        group = [(k, s) for k, s in starts.items() if s % 8 == residue]
        if group:
            lo = min(s for _, s in group) - residue
            hi = max(s for _, s in group) - residue + rows + (8 if residue else 0)
            win = buf[lo:hi, :]
            if residue:
                win = pltpu.roll(win, hi - lo - residue, axis=0)
            for k, s in group:
                yield k, win[s - residue - lo:s - residue - lo + rows, :]


def _mix_fwd(u, cw, cb, lg, lb, pw, ps, seq):
    t, c3 = u.shape
    c = c3 // 3
    kw = 31
    tm = min(_CONV_ROWS, seq)
    tps = seq // tm
    gd = c // len(POOL_WINDOWS)

    def body(u_ref, uh_ref, cw_ref, cb_ref, lg_ref, lb_ref, pw_ref, ps_ref, y_ref, hc_ref, hgbuf, pbuf):
        i = pl.program_id(0)
        keep = jnp.where(i % tps == 0, 0.0, 1.0)
        um = u_ref[...].astype(f32)
        uh = uh_ref[...].astype(f32) * keep
        hgbuf[0:_HALO, :] = uh[:, 0:c] * _sigmoid(uh[:, c:2 * c])
        hgbuf[_HALO:_HALO + tm, :] = um[:, 0:c] * _sigmoid(um[:, c:2 * c])
        pbuf[0:_HALO, :] = uh[:, 2 * c:]
        pbuf[_HALO:_HALO + tm, :] = um[:, 2 * c:]
        for r0 in range(0, tm, _CHUNK):
            acc = jnp.broadcast_to(cb_ref[...], (_CHUNK, c))
            for k, rows in _tap_rows(hgbuf, {k: r0 + _HALO - (kw - 1) + k for k in range(kw)}, _CHUNK):
                acc = acc + cw_ref[k:k + 1, :] * rows
            hc_ref[r0:r0 + _CHUNK, :] = acc
            mu = jnp.mean(acc, axis=-1, keepdims=True)
            xc = acc - mu
            var = jnp.mean(xc * xc, axis=-1, keepdims=True)
            hl = xc * lax.rsqrt(var + EPS) * lg_ref[...] + lb_ref[...]
            y_ref[r0:r0 + _CHUNK, 0:c] = (hl * _sigmoid(hl)).astype(y_ref.dtype)
        pos = ((i % tps) * tm).astype(f32) + lax.broadcasted_iota(jnp.int32, (tm, 1), 0).astype(f32)
        for gi, w in enumerate(POOL_WINDOWS):
            sl = slice(gi * gd, (gi + 1) * gd)
            v = pbuf[_HALO:_HALO + tm, sl]
            s = v
            for j in range(1, w):
                s = s + pbuf[_HALO - j:_HALO - j + tm, sl]
            pooled = s / _pool_counts(pos, w) - v
            mixed = _dot(pooled.astype(_ACT), pw_ref[gi].astype(_ACT), _NN)
            y_ref[:, c + gi * gd:c + (gi + 1) * gd] = (mixed * ps_ref[:, sl]).astype(y_ref.dtype)

    hb = tm // _HALO
    full = lambda shape: pl.BlockSpec(shape, lambda i: (0,) * len(shape))
    return pl.pallas_call(
        body, name="mix_fwd", grid=(t // tm,),
        in_specs=[pl.BlockSpec((tm, c3), lambda i: (i, 0)),
                  pl.BlockSpec((_HALO, c3), lambda i: (jnp.maximum(i * hb - 1, 0), 0)),
                  full((_HALO, c)), full((1, c)), full((1, c)), full((1, c)), full((len(POOL_WINDOWS), gd, gd)), full((1, c))],
        out_specs=[pl.BlockSpec((tm, 2 * c), lambda i: (i, 0)), pl.BlockSpec((tm, c), lambda i: (i, 0))],
        out_shape=[jax.ShapeDtypeStruct((t, 2 * c), _ACT), jax.ShapeDtypeStruct((t, c), f32)],
        scratch_shapes=[pltpu.VMEM((_HALO + tm, c), f32), pltpu.VMEM((_HALO + tm, c), f32)],
        compiler_params=_params(1),
    )(u, u, cw, cb, lg, lb, pw, ps)


def _mix_bwd_norm(hc, dy, lg, lb, seq):
    t, c = hc.shape
    tm = min(_CONV_ROWS, seq)

    def body(hc_ref, dy_ref, lg_ref, lb_ref, dhc_ref, sums_ref):
        @pl.when(pl.program_id(0) == 0)
        def _():
            sums_ref[...] = jnp.zeros_like(sums_ref)

        hcv = hc_ref[...]
        mu = jnp.mean(hcv, axis=-1, keepdims=True)
        xc = hcv - mu
        rstd = lax.rsqrt(jnp.mean(xc * xc, axis=-1, keepdims=True) + EPS)
        n = xc * rstd
        hl = n * lg_ref[...] + lb_ref[...]
        sg = _sigmoid(hl)
        dhl = dy_ref[...].astype(f32) * (sg * (1.0 + hl * (1.0 - sg)))
        dn = dhl * lg_ref[...]
        dhc = rstd * (dn - jnp.mean(dn, axis=-1, keepdims=True) - n * jnp.mean(dn * n, axis=-1, keepdims=True))
        dhc_ref[...] = dhc
        sums_ref[0:1, :] += jnp.sum(dhl * n, axis=0, keepdims=True)
        sums_ref[1:2, :] += jnp.sum(dhl, axis=0, keepdims=True)
        sums_ref[2:3, :] += jnp.sum(dhc, axis=0, keepdims=True)

    row = pl.BlockSpec((tm, c), lambda i: (i, 0))
    vec = pl.BlockSpec((1, c), lambda i: (0, 0))
    return pl.pallas_call(
        body, name="mix_bwd_norm", grid=(t // tm,), in_specs=[row, row, vec, vec],
        out_specs=[row, pl.BlockSpec((8, c), lambda i: (0, 0))],
        out_shape=[jax.ShapeDtypeStruct((t, c), f32), jax.ShapeDtypeStruct((8, c), f32)],
        compiler_params=_params(1),
    )(hc, dy, lg, lb)


def _mix_bwd_taps(u, dhc, dy, cw, pw, ps, seq):
    t, c3 = u.shape
    c = c3 // 3
    kw = 31
    tm = min(_CONV_ROWS, seq)
    tps = seq // tm
    ng = len(POOL_WINDOWS)
    gd = c // ng
    nh = 16

    def body(u_ref, uh_ref, dhc_ref, dhcn_ref, dy_ref, dyn_ref, cw_ref, pw_ref, ps_ref,
             du_ref, dcw_ref, dps_ref, dpw_ref, hgbuf, dcbuf, pbuf, dpbuf):
        i = pl.program_id(0)
        keep_prev = jnp.where(i % tps == 0, 0.0, 1.0)
        keep_next = jnp.where(i % tps == tps - 1, 0.0, 1.0)

        @pl.when(i == 0)
        def _():
            dcw_ref[...] = jnp.zeros_like(dcw_ref)
            dps_ref[...] = jnp.zeros_like(dps_ref)
            dpw_ref[...] = jnp.zeros_like(dpw_ref)

        uh = uh_ref[...].astype(f32) * keep_prev
        hgbuf[0:_HALO, :] = uh[:, 0:c] * _sigmoid(uh[:, c:2 * c])
        pbuf[0:_HALO, :] = uh[:, 2 * c:]
        um = u_ref[...].astype(f32)
        hgbuf[_HALO:_HALO + tm, :] = um[:, 0:c] * _sigmoid(um[:, c:2 * c])
        pbuf[_HALO:_HALO + tm, :] = um[:, 2 * c:]
        dcbuf[0:tm, :] = dhc_ref[...]
        dcbuf[tm:tm + _HALO, :] = dhcn_ref[...] * keep_next
        tap_sums = [None] * kw
        for r0 in range(0, tm, _CHUNK):
            dh = dcbuf[r0:r0 + _CHUNK, :]
            acc = jnp.zeros((_CHUNK, c), f32)
            for k, rows in _tap_rows(hgbuf, {k: r0 + _HALO - (kw - 1) + k for k in range(kw)}, _CHUNK):
                part = (dh * rows).reshape(_CHUNK // 8, 8, c).sum(axis=0)
                tap_sums[k] = part if tap_sums[k] is None else tap_sums[k] + part
            for k, rows in _tap_rows(dcbuf, {k: r0 + (kw - 1) - k for k in range(kw)}, _CHUNK):
                acc = acc + cw_ref[k:k + 1, :] * rows
            val = u_ref[r0:r0 + _CHUNK, 0:c].astype(f32)
            sg = _sigmoid(u_ref[r0:r0 + _CHUNK, c:2 * c].astype(f32))
            du_ref[r0:r0 + _CHUNK, 0:c] = (acc * sg).astype(du_ref.dtype)
            du_ref[r0:r0 + _CHUNK, c:2 * c] = (acc * val * sg * (1.0 - sg)).astype(du_ref.dtype)
        for k in range(kw):
            dcw_ref[k:k + 1, :] += jnp.sum(tap_sums[k], axis=0, keepdims=True)
        base = ((i % tps) * tm).astype(f32)
        pos = base + lax.broadcasted_iota(jnp.int32, (tm, 1), 0).astype(f32)
        pos_next = base + float(tm) + lax.broadcasted_iota(jnp.int32, (nh, 1), 0).astype(f32)
        for gi, w in enumerate(POOL_WINDOWS):
            sl = slice(gi * gd, (gi + 1) * gd)
            v = pbuf[_HALO:_HALO + tm, sl]
            s = v
            for j in range(1, w):
                s = s + pbuf[_HALO - j:_HALO - j + tm, sl]
            cnt = _pool_counts(pos, w)
            pooled = (s / cnt - v).astype(_ACT)
            pwg = pw_ref[gi].astype(_ACT)
            mixed = _dot(pooled, pwg, _NN)
            dyp = dy_ref[:, sl].astype(f32)
            dps_ref[0:1, sl] += jnp.sum(dyp * mixed, axis=0, keepdims=True)
            dmix = (dyp * ps_ref[:, sl]).astype(_ACT)
            dpw_ref[gi] += _dot(pooled, dmix, _TN)
            dmix_next = (dyn_ref[:, sl].astype(f32) * ps_ref[:, sl] * keep_next).astype(_ACT)
            dpool = _dot(dmix, pwg, _NT)
            dpbuf[0:tm, sl] = dpool / cnt
            dpbuf[tm:tm + nh, sl] = _dot(dmix_next, pwg, _NT) / _pool_counts(pos_next, w)
            acc = -dpool
            for j in range(w):
                acc = acc + dpbuf[j:j + tm, sl]
            du_ref[:, 2 * c + gi * gd:2 * c + (gi + 1) * gd] = acc.astype(du_ref.dtype)

    hb = tm // _HALO
    n_halo = t // _HALO
    n_nh = t // nh
    full = lambda shape: pl.BlockSpec(shape, lambda i: (0,) * len(shape))
    return pl.pallas_call(
        body, name="mix_bwd_taps", grid=(t // tm,),
        in_specs=[pl.BlockSpec((tm, c3), lambda i: (i, 0)),
                  pl.BlockSpec((_HALO, c3), lambda i: (jnp.maximum(i * hb - 1, 0), 0)),
                  pl.BlockSpec((tm, c), lambda i: (i, 0)),
                  pl.BlockSpec((_HALO, c), lambda i: (jnp.minimum((i + 1) * hb, n_halo - 1), 0)),
                  pl.BlockSpec((tm, c), lambda i: (i, 1)),
                  pl.BlockSpec((nh, c), lambda i: (jnp.minimum((i + 1) * (tm // nh), n_nh - 1), 1)),
                  full((_HALO, c)), full((ng, gd, gd)), full((1, c))],
        out_specs=[pl.BlockSpec((tm, c3), lambda i: (i, 0)), full((_HALO, c)), full((8, c)), full((ng, gd, gd))],
        out_shape=[jax.ShapeDtypeStruct((t, c3), _ACT), jax.ShapeDtypeStruct((_HALO, c), f32),
                   jax.ShapeDtypeStruct((8, c), f32), jax.ShapeDtypeStruct((ng, gd, gd), f32)],
        scratch_shapes=[pltpu.VMEM((_HALO + tm, c), f32), pltpu.VMEM((tm + _HALO, c), f32),
                        pltpu.VMEM((_HALO + tm, c), f32), pltpu.VMEM((tm + nh, c), f32)],
        compiler_params=_params(1),
    )(u, u, dhc, dhc, dy, dy, cw, pw, ps)


def _attn_fwd(q, kv, n_seq, seq, n_mem):
    t, d = q.shape
    dh = d // XATTN_HEADS
    tq = min(512, seq)
    nq = seq // tq
    scale = dh ** -0.5

    def body(q_ref, kv_ref, o_ref):
        for h in range(XATTN_HEADS):
            cols = slice(h * dh, (h + 1) * dh)
            s = _dot(q_ref[:, cols], kv_ref[:, cols], _NT) * scale
            e = jnp.exp(s - jnp.max(s, axis=-1, keepdims=True))
            p = e / jnp.sum(e, axis=-1, keepdims=True)
            o_ref[:, cols] = _dot(p.astype(_ACT), kv_ref[:, d + h * dh:d + (h + 1) * dh], _NN).astype(o_ref.dtype)

    qs = pl.BlockSpec((tq, d), lambda b, i: (b * nq + i, 0))
    return pl.pallas_call(
        body, name="attn_fwd", grid=(n_seq, nq), in_specs=[qs, pl.BlockSpec((n_mem, 2 * d), lambda b, i: (b, 0))],
        out_specs=qs, out_shape=jax.ShapeDtypeStruct((t, d), _ACT), compiler_params=_params(2),
    )(q, kv)


def _attn_bwd(q, kv, do, n_seq, seq, n_mem):
    t, d = q.shape
    dh = d // XATTN_HEADS
    tq = min(512, seq)
    nq = seq // tq
    scale = dh ** -0.5

    def body(q_ref, kv_ref, do_ref, dq_ref, dkv_ref, acc):
        i = pl.program_id(1)

        @pl.when(i == 0)
        def _():
            acc[...] = jnp.zeros_like(acc)

        for h in range(XATTN_HEADS):
            cols = slice(h * dh, (h + 1) * dh)
            vcols = slice(d + h * dh, d + (h + 1) * dh)
            qv = q_ref[:, cols]
            kh = kv_ref[:, cols]
            dov = do_ref[:, cols]
            s = _dot(qv, kh, _NT) * scale
            e = jnp.exp(s - jnp.max(s, axis=-1, keepdims=True))
            p = e / jnp.sum(e, axis=-1, keepdims=True)
            dp = _dot(dov, kv_ref[:, vcols], _NT)
            ds = (p * (dp - jnp.sum(dp * p, axis=-1, keepdims=True)) * scale).astype(_ACT)
            dq_ref[:, cols] = _dot(ds, kh, _NN).astype(dq_ref.dtype)
            acc[:, cols] += _dot(ds, qv, _TN)
            acc[:, vcols] += _dot(p.astype(_ACT), dov, _TN)

        @pl.when(i == nq - 1)
        def _():
            dkv_ref[...] = acc[...].astype(dkv_ref.dtype)

    qs = pl.BlockSpec((tq, d), lambda b, i: (b * nq + i, 0))
    ms = pl.BlockSpec((n_mem, 2 * d), lambda b, i: (b, 0))
    return pl.pallas_call(
        body, name="attn_bwd", grid=(n_seq, nq), in_specs=[qs, ms, qs], out_specs=[qs, ms],
        out_shape=[jax.ShapeDtypeStruct((t, d), _ACT), jax.ShapeDtypeStruct((n_seq * n_mem, 2 * d), _ACT)],
        scratch_shapes=[pltpu.VMEM((n_mem, 2 * d), f32)], compiler_params=_params(2),
    )(q, kv, do)


_FFN_ROWS = 2048
_FFN_COLS = 256
_FFN_HALO = 16


def _window(buf, g, start, rows):
    return buf[g, pl.ds(start, rows + 8), :]


def _taps3(win, rows):
    return [_rows_from(win, 6 + k, rows) for k in range(3)]


def _conv3(b_ref, w_ref, taps):
    acc = b_ref[...] + w_ref[0:1, :] * taps[0]
    for k in (1, 2):
        acc = acc + w_ref[k:k + 1, :] * taps[k]
    return acc


def _ffn_gate_fwd(up, fw, fb, seq):
    _, t, f = up.shape
    tm = min(_FFN_ROWS, seq)
    tps = seq // tm
    tc = _FFN_COLS
    nc = f // tc
    hl = _FFN_HALO

    def body(up_ref, uph_ref, wg_ref, wv_ref, bg_ref, bv_ref, a_ref):
        i = pl.program_id(1)
        before = uph_ref[...]
        before = jnp.where(i % tps == 0, jnp.zeros_like(before), before)

        def chunk(r0, wins):
            conv = []
            for g, (w_ref, b_ref) in enumerate(((wg_ref, bg_ref), (wv_ref, bv_ref))):
                conv.append(_conv3(b_ref, w_ref, _taps3(wins[g].astype(f32)[hl - 8:, :], _CHUNK)))
            gate, val = conv
            a_ref[pl.ds(r0, _CHUNK), :] = (gate * _sigmoid(gate) * val).astype(a_ref.dtype)

        chunk(0, [jnp.concatenate([before[g], up_ref[g, 0:_CHUNK, :]], axis=0) for g in range(2)])

        def later(ci, carry):
            r0 = pl.multiple_of(ci * _CHUNK, _CHUNK)
            chunk(r0, [up_ref[g, pl.ds(r0 - hl, _CHUNK + hl), :] for g in range(2)])
            return carry

        lax.fori_loop(1, tm // _CHUNK, later, 0)

    hb = tm // hl
    return pl.pallas_call(
        body, name="ffn_gate_fwd", grid=(nc, t // tm),
        in_specs=[pl.BlockSpec((2, tm, tc), lambda j, i: (0, i, j)),
                  pl.BlockSpec((2, hl, tc), lambda j, i: (0, jnp.maximum(i * hb - 1, 0), j)),
                  pl.BlockSpec((8, tc), lambda j, i: (0, j)), pl.BlockSpec((8, tc), lambda j, i: (0, nc + j)),
                  pl.BlockSpec((1, tc), lambda j, i: (0, j)), pl.BlockSpec((1, tc), lambda j, i: (0, nc + j))],
        out_specs=pl.BlockSpec((tm, tc), lambda j, i: (i, j)),
        out_shape=jax.ShapeDtypeStruct((t, f), _ACT), compiler_params=_params(2),
    )(up, up, fw, fw, fb, fb)


def _ffn_gate_bwd(up, da, fw, fb, seq):
    _, t, f = up.shape
    tm = min(_FFN_ROWS, seq)
    tps = seq // tm
    tc = _FFN_COLS
    nc = f // tc
    hl = _FFN_HALO

    def body(up_ref, uph_ref, upn_ref, da_ref, dan_ref, wg_ref, wv_ref, bg_ref, bv_ref,
             dup_ref, sg_ref, sv_ref, dbuf, sums):
        i = pl.program_id(1)
        at_end = i % tps == tps - 1

        @pl.when(i == 0)
        def _():
            sg_ref[...] = jnp.zeros_like(sg_ref)
            sv_ref[...] = jnp.zeros_like(sv_ref)

        sums[...] = jnp.zeros_like(sums)
        before = uph_ref[...]
        before = jnp.where(i % tps == 0, jnp.zeros_like(before), before)
        after = upn_ref[...]
        after = jnp.where(at_end, jnp.zeros_like(after), after)
        w_refs = (wg_ref, wv_ref)
        b_refs = (bg_ref, bv_ref)

        def grads(r0, rows, wins, dav, count):
            taps = [_taps3(wins[g].astype(f32)[hl - 8:, :], rows) for g in range(2)]
            gate, val = [_conv3(b_refs[g], w_refs[g], taps[g]) for g in range(2)]
            sg = _sigmoid(gate)
            douts = (dav * val * (sg * (1.0 + gate * (1.0 - sg))), dav * (gate * sg))
            for g in range(2):
                dbuf[g, pl.ds(r0, rows), :] = douts[g]
                if count:
                    sums[g, 0] += douts[g].reshape(rows // 8, 8, tc).sum(axis=0)
                    for k in range(3):
                        sums[g, 1 + k] += (douts[g] * taps[g][k]).reshape(rows // 8, 8, tc).sum(axis=0)

        grads(0, _CHUNK, [jnp.concatenate([before[g], up_ref[g, 0:_CHUNK, :]], axis=0) for g in range(2)],
              da_ref[0:_CHUNK, :].astype(f32), True)

        def first(ci, carry):
            r0 = pl.multiple_of(ci * _CHUNK, _CHUNK)
            grads(r0, _CHUNK, [up_ref[g, pl.ds(r0 - hl, _CHUNK + hl), :] for g in range(2)],
                  da_ref[pl.ds(r0, _CHUNK), :].astype(f32), True)
            return carry

        lax.fori_loop(1, tm // _CHUNK, first, 0)
        da_after = dan_ref[...].astype(f32)
        grads(tm, hl, [jnp.concatenate([up_ref[g, tm - hl:tm, :], after[g]], axis=0) for g in range(2)],
              jnp.where(at_end, jnp.zeros_like(da_after), da_after), False)

        def second(ci, carry):
            r0 = pl.multiple_of(ci * _CHUNK, _CHUNK)
            for g in range(2):
                win = _window(dbuf, g, r0, _CHUNK)
                acc = jnp.zeros((_CHUNK, tc), f32)
                for k in range(3):
                    acc = acc + w_refs[g][k:k + 1, :] * _rows_from(win, 2 - k, _CHUNK)
                dup_ref[g, pl.ds(r0, _CHUNK), :] = acc.astype(dup_ref.dtype)
            return carry

        lax.fori_loop(0, tm // _CHUNK, second, 0)
        for g, s_ref in enumerate((sg_ref, sv_ref)):
            for r in range(4):
                s_ref[r:r + 1, :] += jnp.sum(sums[g, r], axis=0, keepdims=True)

    hb = tm // hl
    n_halo = t // hl
    return pl.pallas_call(
        body, name="ffn_gate_bwd", grid=(nc, t // tm),
        in_specs=[pl.BlockSpec((2, tm, tc), lambda j, i: (0, i, j)),
                  pl.BlockSpec((2, hl, tc), lambda j, i: (0, jnp.maximum(i * hb - 1, 0), j)),
                  pl.BlockSpec((2, hl, tc), lambda j, i: (0, jnp.minimum((i + 1) * hb, n_halo - 1), j)),
                  pl.BlockSpec((tm, tc), lambda j, i: (i, j)),
                  pl.BlockSpec((hl, tc), lambda j, i: (jnp.minimum((i + 1) * hb, n_halo - 1), j)),
                  pl.BlockSpec((8, tc), lambda j, i: (0, j)), pl.BlockSpec((8, tc), lambda j, i: (0, nc + j)),
                  pl.BlockSpec((1, tc), lambda j, i: (0, j)), pl.BlockSpec((1, tc), lambda j, i: (0, nc + j))],
        out_specs=[pl.BlockSpec((2, tm, tc), lambda j, i: (0, i, j)),
                   pl.BlockSpec((8, tc), lambda j, i: (0, j)), pl.BlockSpec((8, tc), lambda j, i: (0, j))],
        out_shape=[jax.ShapeDtypeStruct((2, t, f), _ACT), jax.ShapeDtypeStruct((8, f), f32), jax.ShapeDtypeStruct((8, f), f32)],
        scratch_shapes=[pltpu.VMEM((2, tm + hl, tc), f32), pltpu.VMEM((2, 4, 8, tc), f32)],
        compiler_params=_params(2),
    )(up, up, up, da, da, fw, fw, fb, fb)


def _adamw_math(w, g, m, v):
    m = ADAM_B1 * m + (1.0 - ADAM_B1) * g
    v = ADAM_B2 * v + (1.0 - ADAM_B2) * (g * g)
    m_hat = m / (1.0 - ADAM_B1 ** ADAM_STEP)
    v_hat = v / (1.0 - ADAM_B2 ** ADAM_STEP)
    delta = -ADAM_LR * (m_hat / (jnp.sqrt(v_hat) + ADAM_EPS) + ADAM_WD * w)
    return delta, m, v


def _adamw_shard(name, w, g, m, v):
    _, r, c = w.shape
    tr = next((cand for cand in (256, 176, 128, 64, 32, 16, 8) if r % cand == 0), r)

    def body(w_ref, g_ref, m_ref, v_ref, go_ref, d_ref, mo_ref, vo_ref):
        gv = g_ref[...]
        d, mn, vn = _adamw_math(w_ref[...], gv, m_ref[...], v_ref[...])
        go_ref[...] = gv
        d_ref[...] = d
        mo_ref[...] = mn
        vo_ref[...] = vn

    s3 = pl.BlockSpec((None, tr, c), lambda i: (0, i, 0))
    s2 = pl.BlockSpec((tr, c), lambda i: (i, 0))
    shp = jax.ShapeDtypeStruct(w.shape, f32)
    return pl.pallas_call(
        body, name=name, grid=(r // tr,), in_specs=[s3, s2, s3, s3], out_specs=[s3] * 4, out_shape=[shp] * 4,
        compiler_params=_params(1),
    )(w, g, m, v)


def _adamw_small(quads):
    n = len(quads)

    def body(*refs):
        ins, outs = refs[:4 * n], refs[4 * n:]
        for p in range(n):
            w_ref, g_ref, m_ref, v_ref = ins[4 * p:4 * p + 4]
            d, mn, vn = _adamw_math(w_ref[...], g_ref[...], m_ref[...], v_ref[...])
            outs[3 * p][...] = d
            outs[3 * p + 1][...] = mn
            outs[3 * p + 2][...] = vn

    flat = [a for q in quads for a in q]
    shapes = [jax.ShapeDtypeStruct(q[0].shape, f32) for q in quads for _ in range(3)]
    outs = pl.pallas_call(
        body, name="adamw_small", in_specs=[_VMEM] * (4 * n), out_specs=[_VMEM] * (3 * n), out_shape=shapes,
        compiler_params=pltpu.CompilerParams(vmem_limit_bytes=_VMEM_LIMIT_BYTES),
    )(*flat)
    return [tuple(outs[3 * p:3 * p + 3]) for p in range(n)]


def _sum_partials(name, place, grads, got):
    _, r, c = grads.shape
    steps = 4 if r % 64 == 0 else 1
    tr = r // steps

    def body(place_ref, own_ref, got_ref, f_ref):
        s = own_ref[...].astype(f32)
        for k in range(got.shape[0]):
            s = s + got_ref[k].astype(f32)
        f_ref[...] = s

    grid_spec = pltpu.PrefetchScalarGridSpec(
        num_scalar_prefetch=1, grid=(steps,),
        in_specs=[pl.BlockSpec((None, tr, c), lambda i, p: (2 * p[0] + p[1], i, 0)),
                  pl.BlockSpec((got.shape[0], tr, c), lambda i, p: (0, i, 0))],
        out_specs=pl.BlockSpec((None, tr, c), lambda i, p: (p[1], i, 0)))
    return pl.pallas_call(body, name=name, grid_spec=grid_spec, out_shape=jax.ShapeDtypeStruct((2, r, c), f32),
                          compiler_params=_params(1))(place, grads, got)


def _place():
    return lax.axis_index("x"), lax.axis_index("y"), lax.axis_index("c")


def _other_chips(x, y):
    return [(1 - x, y), (x, 1 - y), (1 - x, 1 - y)]


def _remote(src, dst, send_sem, recv_sem, to):
    return pltpu.make_async_remote_copy(src_ref=src, dst_ref=dst, send_sem=send_sem, recv_sem=recv_sem,
                                        device_id=to, device_id_type=_MESH)


def _place_shards(place, shards, col_sharded):
    n = len(shards)
    steps = 4

    def body(place_ref, *refs):
        for src, dst in zip(refs[:n], refs[n:]):
            dst[...] = src[...].astype(dst.dtype)

    in_specs, out_specs, out_shape = [], [], []
    for w, col in zip(shards, col_sharded):
        r, cs = w.shape
        tr = r // steps
        in_specs.append(pl.BlockSpec((tr, cs), lambda i, p: (i, 0)))
        if col:
            out_specs.append(pl.BlockSpec((tr, cs), lambda i, p: (i, p[0])))
            out_shape.append(jax.ShapeDtypeStruct((r, 4 * cs), _ACT))
        else:
            out_specs.append(pl.BlockSpec((tr, cs), lambda i, p: (p[0] * steps + i, 0)))
            out_shape.append(jax.ShapeDtypeStruct((4 * r, cs), _ACT))
    grid_spec = pltpu.PrefetchScalarGridSpec(num_scalar_prefetch=1, grid=(steps,), in_specs=in_specs, out_specs=out_specs)
    return pl.pallas_call(body, name="place_shards", grid_spec=grid_spec, out_shape=out_shape,
                          compiler_params=_params(1))(place, *shards)


def _shard_of(ref, col_sharded, s):
    rows, cols = ref.shape
    if col_sharded:
        return ref.at[:, pl.ds(s * (cols // 4), cols // 4)]
    return ref.at[pl.ds(s * (rows // 4), rows // 4), :]


def _part_of(ref, col_sharded, whole, s, h):
    if whole:
        return _shard_of(ref, col_sharded, s)
    rows, cols = ref.shape
    if col_sharded:
        return ref.at[pl.ds(h * (rows // 2), rows // 2), pl.ds(s * (cols // 4), cols // 4)]
    return ref.at[pl.ds((2 * s + h) * (rows // 8), rows // 8), :]


def _allgather_start(bufs, col_sharded, whole, groups):
    n = len(bufs)
    ng = len(groups)

    def body(*refs):
        out = refs[n:2 * n]
        sems = refs[2 * n:]
        x, y, c = _place()
        for g, members in enumerate(groups):
            for i, w in enumerate(members):
                mine = _part_of(out[w], col_sharded[w], whole[w], 2 * x + y, c)
                for j, chip in enumerate(_other_chips(x, y)):
                    _remote(mine, mine, sems[2 * g].at[3 * i + j], sems[2 * g + 1].at[3 * i + j], (*chip, c)).start()

    sem_shapes = [pltpu.SemaphoreType.DMA((3 * len(m),)) for m in groups for _ in range(2)]
    outs = pl.pallas_call(
        body, name="allgather_start", in_specs=[_HBM] * n, out_specs=[_HBM] * n + [_SEM] * (2 * ng),
        out_shape=[pltpu.HBM(b.shape, b.dtype) for b in bufs] + sem_shapes,
        input_output_aliases={i: i for i in range(n)},
        compiler_params=pltpu.CompilerParams(has_side_effects=_EFFECT),
    )(*[pltpu.with_memory_space_constraint(b, pltpu.HBM) for b in bufs])
    return list(outs[:n]), [(outs[n + 2 * g], outs[n + 2 * g + 1]) for g in range(ng)]


def _allgather_relay(name, bufs, col_sharded, whole, sems, after):
    n = len(bufs)

    def body(*refs):
        buf = refs[:n]
        send, recv = refs[n], refs[n + 1]
        out = refs[n + 3:2 * n + 3]
        to_sibling, from_sibling = refs[2 * n + 3:]
        x, y, c = _place()
        for i in range(n):
            mine = _part_of(buf[i], col_sharded[i], whole[i], 2 * x + y, c)
            for j, chip in enumerate(_other_chips(x, y)):
                landed = _part_of(buf[i], col_sharded[i], whole[i], 2 * chip[0] + chip[1], c)
                cp = _remote(mine, landed, send.at[3 * i + j], recv.at[3 * i + j], (*chip, c))
                cp.wait_send()
                cp.wait_recv()
        for i in range(n):
            if not whole[i]:
                for j, chip in enumerate(_other_chips(x, y)):
                    landed = _part_of(out[i], col_sharded[i], False, 2 * chip[0] + chip[1], c)
                    _remote(landed, landed, to_sibling.at[3 * i + j], from_sibling.at[3 * i + j], (x, y, 1 - c)).start()

    outs = pl.pallas_call(
        body, name=name, in_specs=[_HBM] * n + [_SEM, _SEM, _ANY], out_specs=[_HBM] * n + [_SEM, _SEM],
        out_shape=[pltpu.HBM(b.shape, b.dtype) for b in bufs] + [pltpu.SemaphoreType.DMA((3 * n,))] * 2,
        input_output_aliases={i: i for i in range(n)},
        compiler_params=pltpu.CompilerParams(has_side_effects=_EFFECT),
    )(*bufs, *sems, after)
    return list(outs[:n]), (outs[n], outs[n + 1])


def _allgather_wait(name, bufs, col_sharded, whole, sems, after):
    n = len(bufs)

    def body(*refs):
        buf = refs[:n]
        to_sibling, from_sibling = refs[n], refs[n + 1]
        x, y, c = _place()
        for i in range(n):
            if not whole[i]:
                for j, chip in enumerate(_other_chips(x, y)):
                    sent = _part_of(buf[i], col_sharded[i], False, 2 * chip[0] + chip[1], c)
                    landed = _part_of(buf[i], col_sharded[i], False, 2 * chip[0] + chip[1], 1 - c)
                    cp = _remote(sent, landed, to_sibling.at[3 * i + j], from_sibling.at[3 * i + j], (x, y, 1 - c))
                    cp.wait_send()
                    cp.wait_recv()

    return pl.pallas_call(
        body, name=name, in_specs=[_HBM] * n + [_SEM, _SEM, _ANY], out_specs=[_HBM] * n,
        out_shape=[pltpu.HBM(b.shape, b.dtype) for b in bufs],
        input_output_aliases={i: i for i in range(n)},
        compiler_params=pltpu.CompilerParams(has_side_effects=_EFFECT),
    )(*bufs, *sems, after)


def _other_devices(x, y, c):
    flips = [(bx, by, bc) for bx in (0, 1) for by in (0, 1) for bc in (0, 1)][1:]
    return [(1 - x if bx else x, 1 - y if by else y, 1 - c if bc else c) for bx, by, bc in flips]


def _grad_exchange_start(name, grads):
    nw = len(grads)
    lands = [lax.empty((7,) + g.shape[1:], g.dtype) for g in grads]

    def body(*refs):
        src = refs[2 * nw:3 * nw]
        got = refs[3 * nw:4 * nw]
        send, recv, token = refs[4 * nw:]
        x, y, c = _place()
        for w in range(nw):
            for k, (px, py, pc) in enumerate(_other_devices(x, y, c)):
                _remote(src[w].at[4 * px + 2 * py + pc], got[w].at[k], send.at[7 * w + k], recv.at[7 * w + k], (px, py, pc)).start()
        token[...] = jnp.zeros_like(token)

    outs = pl.pallas_call(
        body, name=name, in_specs=[_HBM] * (2 * nw), out_specs=[_HBM] * (2 * nw) + [_SEM, _SEM, _VMEM],
        out_shape=[pltpu.HBM(a.shape, a.dtype) for a in list(grads) + lands]
        + [pltpu.SemaphoreType.DMA((7 * nw,)), pltpu.SemaphoreType.DMA((7 * nw,)), jax.ShapeDtypeStruct((8, 128), f32)],
        input_output_aliases={i: i for i in range(2 * nw)},
        compiler_params=pltpu.CompilerParams(has_side_effects=_EFFECT),
    )(*[pltpu.with_memory_space_constraint(a, pltpu.HBM) for a in list(grads) + lands])
    return list(outs[:nw]), list(outs[nw:2 * nw]), (outs[2 * nw], outs[2 * nw + 1]), outs[2 * nw + 2]


def _grad_exchange_wait(name, grads, got, sems, after):
    nw = len(grads)

    def body(*refs):
        src = refs[:nw]
        land = refs[nw:2 * nw]
        send, recv = refs[2 * nw], refs[2 * nw + 1]
        x, y, c = _place()
        for w in range(nw):
            for k, (px, py, pc) in enumerate(_other_devices(x, y, c)):
                cp = _remote(src[w].at[4 * px + 2 * py + pc], land[w].at[k], send.at[7 * w + k], recv.at[7 * w + k], (px, py, pc))
                cp.wait_send()
                cp.wait_recv()

    outs = pl.pallas_call(
        body, name=name, in_specs=[_HBM] * (2 * nw) + [_SEM, _SEM, _ANY], out_specs=[_HBM] * (2 * nw),
        out_shape=[pltpu.HBM(a.shape, a.dtype) for a in list(grads) + list(got)],
        input_output_aliases={i: i for i in range(2 * nw)},
        compiler_params=pltpu.CompilerParams(has_side_effects=_EFFECT),
    )(*grads, *got, *sems, after)
    return list(outs[:nw]), list(outs[nw:])


def _swap_halves(finals):
    nw = len(finals)

    def body(*refs):
        buf = refs[nw:2 * nw]
        send_sem, recv_sem = refs[2 * nw:]
        x, y, c = _place()
        sends = []
        for w in range(nw):
            rc = _remote(buf[w].at[c], buf[w].at[c], send_sem.at[w], recv_sem.at[w], (x, y, 1 - c))
            rc.start()
            sends.append(rc)
        for w in range(nw):
            _remote(buf[w].at[1 - c], buf[w].at[1 - c], send_sem.at[w], recv_sem.at[w], (x, y, c)).wait_recv()
        for rc in sends:
            rc.wait_send()

    return pl.pallas_call(
        body, name="rs_swap_halves", in_specs=[_ANY] * nw, out_specs=[_ANY] * nw,
        out_shape=[jax.ShapeDtypeStruct(g.shape, g.dtype) for g in finals],
        input_output_aliases={i: i for i in range(nw)},
        scratch_shapes=[pltpu.SemaphoreType.DMA((nw,)), pltpu.SemaphoreType.DMA((nw,))],
    )(*finals)


def _half_slices(shape, h):
    rows, cols = shape
    if cols % 256 == 0:
        return (slice(None), slice(h * (cols // 2), (h + 1) * (cols // 2)))
    return (slice(h * (rows // 2), (h + 1) * (rows // 2)), slice(None))


def _allreduce_small(parts):
    n = len(parts)

    def body(*refs):
        src = refs[:n]
        out = refs[n:2 * n]
        sib = refs[2 * n:3 * n]
        chip_sum = refs[3 * n:4 * n]
        slots = refs[4 * n:5 * n]
        pair_send, pair_recv, ici_send, ici_recv, swap_send, swap_recv = refs[5 * n:]
        x, y, c = _place()
        me_chip = 2 * x + y
        chips = _other_chips(x, y)
        pairs = [_remote(src[a], sib[a], pair_send.at[a], pair_recv.at[a], (x, y, 1 - c)) for a in range(n)]
        for rc in pairs:
            rc.start()
        for a in range(n):
            pairs[a].wait_recv()
            chip_sum[a][...] = src[a][...] + sib[a][...]
        for h in (0, 1):
            @pl.when(c == h)
            def _():
                sends = []
                for a in range(n):
                    idx = _half_slices(parts[a].shape, h)
                    for j, chip in enumerate(chips):
                        rc = _remote(chip_sum[a].at[idx], slots[a].at[me_chip].at[idx], ici_send.at[3 * a + j], ici_recv.at[3 * a + j], (*chip, h))
                        rc.start()
                        sends.append(rc)
                    slots[a][(me_chip,) + idx] = chip_sum[a][idx]
                for a in range(n):
                    idx = _half_slices(parts[a].shape, h)
                    for j, chip in enumerate(chips):
                        landed = slots[a].at[2 * chip[0] + chip[1]].at[idx]
                        _remote(landed, landed, ici_send.at[3 * a + j], ici_recv.at[3 * a + j], (x, y, c)).wait_recv()
                    total = slots[a][(0,) + idx]
                    for s in range(1, 4):
                        total = total + slots[a][(s,) + idx]
                    out[a][idx] = total
                    rc = _remote(out[a].at[idx], out[a].at[idx], swap_send.at[a], swap_recv.at[a], (x, y, 1 - h))
                    rc.start()
                    sends.append(rc)
                for a in range(n):
                    other = out[a].at[_half_slices(parts[a].shape, 1 - h)]
                    _remote(other, other, swap_send.at[a], swap_recv.at[a], (x, y, c)).wait_recv()
                for rc in sends:
                    rc.wait_send()
        for rc in pairs:
            rc.wait_send()

    return pl.pallas_call(
        body, name="allreduce_small", in_specs=[_VMEM] * n, out_specs=[_VMEM] * n,
        out_shape=[jax.ShapeDtypeStruct(p.shape, f32) for p in parts],
        scratch_shapes=[pltpu.VMEM(p.shape, f32) for p in parts] * 2 + [pltpu.VMEM((4,) + p.shape, f32) for p in parts]
        + [pltpu.SemaphoreType.DMA((n,)), pltpu.SemaphoreType.DMA((n,)), pltpu.SemaphoreType.DMA((3 * n,)),
           pltpu.SemaphoreType.DMA((3 * n,)), pltpu.SemaphoreType.DMA((n,)), pltpu.SemaphoreType.DMA((n,))],
        compiler_params=pltpu.CompilerParams(vmem_limit_bytes=_VMEM_LIMIT_BYTES),
    )(*parts)


def _local_step(x, mem, tgt, g_mix, g_xattn, g_mem, g_ffn, g_final, cb, lg, lb, pw, ps, fb, relay, weights, reduce, n_seq, seq, n_mem):
    t, d = x.shape
    f = fb.shape[1] // 2
    c = cb.shape[1]
    h1 = _rms_fwd("norm_mix", x, g_mix)
    relay(0, h1)
    w_in, cw, fw = weights(0, h1)
    u = _mm_nn("proj_in", h1, w_in, _ACT, w_in.shape[1])
    y, hc = _mix_fwd(u, cw, cb, lg, lb, pw, ps, seq)
    relay(1, y)
    w_out, w_q, w_kv, w_o = weights(1, y)
    x1, h2 = _proj_residual_norm("proj_out", y, w_out, x, g_xattn)
    q = _mm_nn("proj_q", h2, w_q, _ACT, d)
    mem_n = _rms_fwd("norm_mem", mem, g_mem)
    kv = _mm_nn("proj_kv", mem_n, w_kv, _ACT, 2 * d)
    o = _attn_fwd(q, kv, n_seq, seq, n_mem)
    relay(2, o)
    x2, h3 = _proj_residual_norm("proj_o", o, w_o, x1, g_ffn)
    w_up, w_down = weights(2, h3)
    up = _mm_nn("proj_up", h3, w_up, _ACT, f, split_out=True)
    a = _ffn_gate_fwd(up, fw, fb, seq)
    dx3, dx3b, dg_final, loss = _proj_loss_bwd("proj_down", a, w_down, x2, g_final, tgt)
    da = _mm_nt("d_act", dx3b, w_down, _ACT)
    gw_down = _mm_tn_rows("dw_down", a, dx3b, f // 2, d // 2)
    dup, sums_g, sums_v = _ffn_gate_bwd(up, da, fw, fb, seq)
    gw_up = _mm_tn_pieces("dw_up", h3, dup, f // 2, t)
    token = reduce(0, [gw_down.reshape(8, -1, d), gw_up])
    dx2, dx2b, dg_ffn = _dproj_rms_bwd("d_h3", dup, w_up, x2, g_ffn + token, dx3)
    do = _mm_nt("d_o", dx2b, w_o, _ACT)
    gw_o = _mm_tn_rows("dw_o", o, dx2b, d, d // 2)
    dq, dkv = _attn_bwd(q, kv, do, n_seq, seq, n_mem)
    gw_q = _mm_tn_rows("dw_q", h2, dq, d, d // 2)
    gw_kv = _mm_tn_pieces("dw_kv", mem_n, dkv, d // 2, mem.shape[0])
    dmem_n = _mm_nt("d_mem_n", dkv, w_kv, f32)
    dg_mem = _rms_gain_grad("norm_mem_bwd", mem, dmem_n)
    dx1, dx1b, dg_xattn = _dproj_rms_bwd("d_h2", dq, w_q, x1, g_xattn, dx2)
    dy = _mm_nt("d_y", dx1b, w_out, _ACT)
    gw_out = _mm_tn_rows("dw_out", y, dx1b, d, d // 2)
    token = reduce(1, [gw_o.reshape(8, -1, d), gw_q.reshape(8, -1, d), gw_kv, gw_out.reshape(8, -1, d)])
    dhc, sums_norm = _mix_bwd_norm(hc, dy, lg + token, lb, seq)
    du, d_cw, d_ps, d_pw = _mix_bwd_taps(u, dhc, dy, cw, pw, ps, seq)
    gw_in = _mm_tn_pieces("dw_in", h1, du, c * 3 // 4, t)
    token = reduce(2, [gw_in])
    grad_x, dg_mix = _dproj_rms_bwd("d_h1", du, w_in, x, g_mix + token, dx1, storage_copy=False)
    zero_row = jnp.zeros((1, d), f32)
    gains = jnp.concatenate([dg_mix, dg_xattn, dg_mem, dg_ffn, dg_final, jnp.pad(loss, ((0, 0), (0, d - 1))), zero_row, zero_row], axis=0)
    conv_rows = jnp.concatenate([sums_norm[2:3], sums_norm[0:1], sums_norm[1:2], d_ps[0:1], jnp.zeros((4, c), f32)], axis=0)
    ffn_rows = jnp.concatenate([sums_g, sums_v], axis=1)
    small = [gains, conv_rows, d_pw.reshape(-1, d_pw.shape[-1]), ffn_rows, d_cw]
    return grad_x, small


def kernel(x, mem, norm_mix_g, w_in, conv_dw_w, conv_dw_b, conv_ln_g, conv_ln_b, pool_w, pool_scale, w_out, norm_xattn_g, norm_mem_g, w_q, w_kv, w_o, norm_ffn_g, w_up, ffn_dw_w, ffn_dw_b, w_down, norm_final_g, loss_target, m_norm_mix_g, m_w_in, m_conv_dw_w, m_conv_dw_b, m_conv_ln_g, m_conv_ln_b, m_pool_w, m_pool_scale, m_w_out, m_norm_xattn_g, m_norm_mem_g, m_w_q, m_w_kv, m_w_o, m_norm_ffn_g, m_w_up, m_ffn_dw_w, m_ffn_dw_b, m_w_down, m_norm_final_g, v_norm_mix_g, v_w_in, v_conv_dw_w, v_conv_dw_b, v_conv_ln_g, v_conv_ln_b, v_pool_w, v_pool_scale, v_w_out, v_norm_xattn_g, v_norm_mem_g, v_w_q, v_w_kv, v_w_o, v_norm_ffn_g, v_w_up, v_ffn_dw_w, v_ffn_dw_b, v_w_down, v_norm_final_g):
    n_seq, seq, d = x.shape
    n_mem = mem.shape[1]
    chip = 2 * lax.axis_index("x") + lax.axis_index("y")

    place = jnp.stack([chip, lax.axis_index("c")]).astype(jnp.int32)

    col_w = [w_in, w_kv, w_up]
    row_w = [w_out, w_q, w_o, w_down]
    col_flags = [True] * 3 + [False] * 4 + [True] * 2
    kw = conv_dw_w.shape[1]

    def padded_in_place(shard, rows):
        full = jnp.zeros((rows, 4 * shard.shape[1]), shard.dtype)
        return lax.dynamic_update_slice(full, shard, (0, chip * shard.shape[1]))

    bufs = list(_place_shards(place, [w[0] for w in col_w + row_w], col_flags[:7]))
    bufs += [padded_in_place(conv_dw_w[0], _HALO), padded_in_place(ffn_dw_w[0], 8)]
    groups = [[0, 7, 8], [3, 4, 1, 5], [2, 6]]
    whole = [False] * 7 + [True] * 2
    bufs, sems = _allgather_start(bufs, col_flags, whole, groups)
    relayed = {}

    def relay(g, after):
        members = groups[g]
        relayed[g] = _allgather_relay("allgather_relay_%d" % g, [bufs[i] for i in members], [col_flags[i] for i in members],
                                      [whole[i] for i in members], sems[g], after)

    def weights(g, after):
        members = groups[g]
        group_bufs, sibling_sems = relayed[g]
        return _allgather_wait("allgather_wait_%d" % g, group_bufs, [col_flags[i] for i in members],
                               [whole[i] for i in members], sibling_sems, after)

    names = ["w_in", "w_kv", "w_up", "w_out", "w_q", "w_o", "w_down"]
    reduce_groups = [["w_down", "w_up"], ["w_o", "w_q", "w_kv", "w_out"], ["w_in"]]
    in_flight = {}

    def reduce(g, grads):
        grads, lands, rs_sems, token = _grad_exchange_start("rs_start_%d" % g, grads)
        in_flight[g] = (grads, lands, rs_sems)
        return token[0:1, 0:1]

    grad_x, small = _local_step(
        x.reshape(n_seq * seq, d), mem.reshape(n_seq * n_mem, d), loss_target.reshape(n_seq * seq, d),
        norm_mix_g, norm_xattn_g, norm_mem_g, norm_ffn_g, norm_final_g.reshape(1, d),
        conv_dw_b, conv_ln_g, conv_ln_b, pool_w[0], pool_scale, ffn_dw_b, relay, weights, reduce, n_seq, seq, n_mem)

    finals = {}
    for g, members in enumerate(reduce_groups):
        grads, lands, rs_sems = in_flight[g]
        grads, lands = _grad_exchange_wait("rs_wait_%d" % g, grads, lands, rs_sems, grad_x)
        for n, a, b in zip(members, grads, lands):
            finals[n] = _sum_partials("rs_sum_" + n, place, a, b)
    shard_grads = _swap_halves([finals[n] for n in names])

    gains, conv_rows, d_pw, ffn_rows, d_cw = _allreduce_small(small)
    loss = gains[5, 0]

    outs = {}
    big_w = dict(zip(names, col_w + row_w))
    big_m = dict(w_in=m_w_in, w_kv=m_w_kv, w_up=m_w_up, w_out=m_w_out, w_q=m_w_q, w_o=m_w_o, w_down=m_w_down)
    big_v = dict(w_in=v_w_in, w_kv=v_w_kv, w_up=v_w_up, w_out=v_w_out, w_q=v_w_q, w_o=v_w_o, w_down=v_w_down)
    for n, g in zip(names, shard_grads):
        w = big_w[n]
        g2 = g.reshape(w.shape[1], w.shape[2])
        outs[n] = tuple(_adamw_shard("adamw_" + n, w, g2, big_m[n], big_v[n]))

    f2 = ffn_dw_b.shape[1]
    cs_c = conv_dw_w.shape[2]
    cs_f = ffn_dw_w.shape[2]
    g_cw = lax.dynamic_slice(d_cw, (0, chip * cs_c), (kw, cs_c)).reshape(conv_dw_w.shape)
    g_fw = lax.dynamic_slice(ffn_rows, (1, chip * cs_f), (ffn_dw_w.shape[1], cs_f)).reshape(ffn_dw_w.shape)
    small_params = [
        ("norm_mix_g", norm_mix_g, gains[0:1], m_norm_mix_g, v_norm_mix_g),
        ("conv_dw_w", conv_dw_w, g_cw, m_conv_dw_w, v_conv_dw_w),
        ("conv_dw_b", conv_dw_b, conv_rows[0:1], m_conv_dw_b, v_conv_dw_b),
        ("conv_ln_g", conv_ln_g, conv_rows[1:2], m_conv_ln_g, v_conv_ln_g),
        ("conv_ln_b", conv_ln_b, conv_rows[2:3], m_conv_ln_b, v_conv_ln_b),
        ("pool_w", pool_w, d_pw.reshape(pool_w.shape), m_pool_w, v_pool_w),
        ("pool_scale", pool_scale, conv_rows[3:4], m_pool_scale, v_pool_scale),
        ("norm_xattn_g", norm_xattn_g, gains[1:2], m_norm_xattn_g, v_norm_xattn_g),
        ("norm_mem_g", norm_mem_g, gains[2:3], m_norm_mem_g, v_norm_mem_g),
        ("norm_ffn_g", norm_ffn_g, gains[3:4], m_norm_ffn_g, v_norm_ffn_g),
        ("ffn_dw_w", ffn_dw_w, g_fw, m_ffn_dw_w, v_ffn_dw_w),
        ("ffn_dw_b", ffn_dw_b, ffn_rows[0:1, :f2], m_ffn_dw_b, v_ffn_dw_b),
        ("norm_final_g", norm_final_g.reshape(1, d), gains[4:5], m_norm_final_g.reshape(1, d), v_norm_final_g.reshape(1, d)),
    ]
    quads = []
    for _, w, g, m, v in small_params:
        shape2 = (-1, w.shape[-1])
        quads.append((w.reshape(shape2), g.reshape(shape2), m.reshape(shape2), v.reshape(shape2)))
    for (n, w, g, _, _), (delta, new_m, new_v) in zip(small_params, _adamw_small(quads)):
        shape = norm_final_g.shape if n == "norm_final_g" else w.shape
        outs[n] = (g.reshape(shape), delta.reshape(shape), new_m.reshape(shape), new_v.reshape(shape))

    order = ["norm_mix_g", "w_in", "conv_dw_w", "conv_dw_b", "conv_ln_g", "conv_ln_b", "pool_w", "pool_scale", "w_out",
             "norm_xattn_g", "norm_mem_g", "w_q", "w_kv", "w_o", "norm_ffn_g", "w_up", "ffn_dw_w", "ffn_dw_b", "w_down",
             "norm_final_g"]
    return (loss, grad_x.reshape(x.shape), *[outs[n][0] for n in order], *[outs[n][1] for n in order],
            *[outs[n][2] for n in order], *[outs[n][3] for n in order])
```

```python
import functools

import jax
import jax.numpy as jnp
from jax import lax
from jax.experimental import pallas as pl
from jax.experimental.pallas import tpu as pltpu

f32 = jnp.float32
_ACT = jnp.bfloat16

EPS = 1e-6
POOL_WINDOWS = (2, 4, 8, 16)
XATTN_HEADS = 4
ADAM_LR = 0.001
ADAM_B1 = 0.9
ADAM_B2 = 0.999
ADAM_EPS = 1e-08
ADAM_WD = 0.01
ADAM_STEP = 10

_VMEM_LIMIT_BYTES = 56 * 1024 * 1024
_MESH = pl.DeviceIdType.MESH
_ANY = pl.BlockSpec(memory_space=pl.ANY)
_VMEM = pl.BlockSpec(memory_space=pltpu.VMEM)
_HBM = pl.BlockSpec(memory_space=pltpu.HBM)
_SEM = pl.BlockSpec(memory_space=pltpu.SEMAPHORE)
_EFFECT = pltpu.SideEffectType.DATAFLOW_SIDE_EFFECTING

_NN = (((1,), (0,)), ((), ()))
_NT = (((1,), (1,)), ((), ()))
_TN = (((0,), (0,)), ((), ()))


def _params(n_grid):
    return pltpu.CompilerParams(dimension_semantics=("arbitrary",) * n_grid, vmem_limit_bytes=_VMEM_LIMIT_BYTES)


def _sigmoid(v):
    return 1.0 / (1.0 + jnp.exp(-v))


def _dot(a, b, dims):
    return lax.dot_general(a, b, dims, preferred_element_type=f32)


def _mm(name, a, b, *, dims, grid, a_spec, b_spec, o_spec, out_shape, nk, acc_shape=None, res=None, res_spec=None):
    def body(*refs):
        if res is None:
            a_ref, b_ref, o_ref, *scratch = refs
            r_ref = None
        else:
            a_ref, b_ref, r_ref, o_ref, *scratch = refs
        p = _dot(a_ref[...], b_ref[...], dims)

        def finish(v):
            if r_ref is not None:
                v = v + r_ref[...]
            o_ref[...] = v.astype(o_ref.dtype)

        if nk == 1:
            finish(p)
        else:
            acc = scratch[0]
            k = pl.program_id(2)

            @pl.when(k == 0)
            def _():
                acc[...] = p

            @pl.when(k > 0)
            def _():
                acc[...] += p

            @pl.when(k == nk - 1)
            def _():
                finish(acc[...])

    ins = [a, b] + ([] if res is None else [res])
    specs = [a_spec, b_spec] + ([] if res is None else [res_spec])
    return pl.pallas_call(
        body, name=name, grid=grid, in_specs=specs, out_specs=o_spec, out_shape=out_shape,
        scratch_shapes=[pltpu.VMEM(acc_shape, f32)] if nk > 1 else [], compiler_params=_params(3),
    )(*ins)


_NARROW = 2816


def _row_tile(m, width=_NARROW + 1):
    return min(1024 if width <= _NARROW else 512, m)


def _mm_nn(name, a, b, out_dtype, tn, res=None, split_out=False):
    m, k = a.shape
    n = b.shape[1]
    tm = _row_tile(m, max(k, tn))
    if split_out:
        out_shape = jax.ShapeDtypeStruct((n // tn, m, tn), out_dtype)
        o_spec = pl.BlockSpec((None, tm, tn), lambda j, i, kk: (j, i, 0))
    else:
        out_shape = jax.ShapeDtypeStruct((m, n), out_dtype)
        o_spec = pl.BlockSpec((tm, tn), lambda j, i, kk: (i, j))
    return _mm(
        name, a, b, dims=_NN, grid=(n // tn, m // tm, 1), nk=1,
        a_spec=pl.BlockSpec((tm, k), lambda j, i, kk: (i, 0)),
        b_spec=pl.BlockSpec((k, tn), lambda j, i, kk: (0, j)),
        o_spec=o_spec, out_shape=out_shape, res=res,
        res_spec=pl.BlockSpec((tm, tn), lambda j, i, kk: (i, j)),
    )


def _mm_nt(name, a, b, out_dtype):
    n, kc = b.shape
    m = a.shape[0]
    tm = _row_tile(m, max(n, kc))
    return _mm(
        name, a, b, dims=_NT, grid=(m // tm, 1, 1), nk=1,
        a_spec=pl.BlockSpec((tm, kc), lambda i, j, k: (i, 0)),
        b_spec=pl.BlockSpec((n, kc), lambda i, j, k: (0, 0), pipeline_mode=pl.Buffered(1)),
        o_spec=pl.BlockSpec((tm, n), lambda i, j, k: (i, 0)),
        out_shape=jax.ShapeDtypeStruct((m, n), out_dtype),
    )


def _mm_tn_rows(name, a, b, tka, tn):
    m, ka = a.shape
    nb = b.shape[1]
    return _mm(
        name, a, b, dims=_TN, grid=(ka // tka, nb // tn, 1), nk=1,
        a_spec=pl.BlockSpec((m, tka), lambda i, j, k: (0, i)),
        b_spec=pl.BlockSpec((m, tn), lambda i, j, k: (0, j)),
        o_spec=pl.BlockSpec((tka, tn), lambda i, j, k: (i, j)),
        out_shape=jax.ShapeDtypeStruct((ka, nb), _ACT),
    )


def _mm_tn_pieces(name, a, b, cs, tt):
    m, ka = a.shape
    nk = m // tt
    if b.ndim == 3:
        b_spec = pl.BlockSpec((None, tt, cs), lambda i, j, k: (j // 2, k, j % 2))
    else:
        b_spec = pl.BlockSpec((tt, cs), lambda i, j, k: (k, j))
    return _mm(
        name, a, b, dims=_TN, grid=(2, 4, nk), nk=nk, acc_shape=(ka // 2, cs),
        a_spec=pl.BlockSpec((tt, ka // 2), lambda i, j, k: (k, i)), b_spec=b_spec,
        o_spec=pl.BlockSpec((None, ka // 2, cs), lambda i, j, k: (2 * j + i, 0, 0)),
        out_shape=jax.ShapeDtypeStruct((8, ka // 2, cs), _ACT),
    )


def _rms_fwd(name, x, g):
    t, d = x.shape
    tm = _row_tile(t, d)

    def body(x_ref, g_ref, h_ref):
        xv = x_ref[...]
        r = lax.rsqrt(jnp.mean(xv * xv, axis=-1, keepdims=True) + EPS)
        h_ref[...] = (xv * r * g_ref[...]).astype(h_ref.dtype)

    return pl.pallas_call(
        body, name=name, grid=(t // tm,),
        in_specs=[pl.BlockSpec((tm, d), lambda i: (i, 0)), pl.BlockSpec((1, d), lambda i: (0, 0))],
        out_specs=pl.BlockSpec((tm, d), lambda i: (i, 0)), out_shape=jax.ShapeDtypeStruct((t, d), _ACT),
        compiler_params=_params(1),
    )(x, g)


def _fused_rows(name, a, b, product, a_spec, tm, extras, extra_specs, out_shape, out_specs, epilogue):
    ne = len(extras)

    def body(a_ref, b_ref, *refs):
        epilogue(product(a_ref, b_ref), refs[:ne], refs[ne:])

    m = extras[0].shape[0]
    return pl.pallas_call(
        body, name=name, grid=(m // tm,),
        in_specs=[a_spec, pl.BlockSpec(b.shape, lambda i: (0, 0), pipeline_mode=pl.Buffered(1)), *extra_specs],
        out_specs=out_specs, out_shape=out_shape, compiler_params=_params(1),
    )(a, b, *extras)


def _proj_residual_norm(name, a, b, res, g):
    m, k = a.shape
    d = b.shape[1]
    tm = _row_tile(m, max(k, d))

    def epilogue(p, ins, outs):
        xv = p + ins[0][...]
        outs[0][...] = xv
        r = lax.rsqrt(jnp.mean(xv * xv, axis=-1, keepdims=True) + EPS)
        outs[1][...] = (xv * r * ins[1][...]).astype(outs[1].dtype)

    row = pl.BlockSpec((tm, d), lambda i: (i, 0))
    return _fused_rows(
        name, a, b, lambda a_ref, b_ref: _dot(a_ref[...], b_ref[...], _NN), pl.BlockSpec((tm, k), lambda i: (i, 0)), tm,
        [res, g], [row, pl.BlockSpec((1, d), lambda i: (0, 0))],
        [jax.ShapeDtypeStruct((m, d), f32), jax.ShapeDtypeStruct((m, d), _ACT)], [row, row], epilogue)


def _dproj_rms_bwd(name, a, b, x, g, dres, storage_copy=True):
    m, d = x.shape
    if a.ndim == 3:
        nh, _, kh = a.shape
        tm = _row_tile(m, nh * kh)
        a_spec = pl.BlockSpec((nh, tm, kh), lambda i: (0, i, 0))

        def product(a_ref, b_ref):
            p = _dot(a_ref[0], b_ref[:, 0:kh], _NT)
            for h in range(1, nh):
                p = p + _dot(a_ref[h], b_ref[:, h * kh:(h + 1) * kh], _NT)
            return p
    else:
        tm = _row_tile(m, max(a.shape[1], d))
        a_spec = pl.BlockSpec((tm, a.shape[1]), lambda i: (i, 0))

        def product(a_ref, b_ref):
            return _dot(a_ref[...], b_ref[...], _NT)

    def epilogue(dhv, ins, outs):
        x_ref, g_ref, dres_ref = ins
        dg_ref = outs[-1]

        @pl.when(pl.program_id(0) == 0)
        def _():
            dg_ref[...] = jnp.zeros_like(dg_ref)

        xv = x_ref[...]
        r = lax.rsqrt(jnp.mean(xv * xv, axis=-1, keepdims=True) + EPS)
        xn = xv * r
        dxn = dhv * g_ref[...]
        dx = r * (dxn - xn * jnp.mean(dxn * xn, axis=-1, keepdims=True)) + dres_ref[...]
        outs[0][...] = dx
        if storage_copy:
            outs[1][...] = dx.astype(outs[1].dtype)
        dg_ref[...] += jnp.sum(dhv * xn, axis=0, keepdims=True)

    row = pl.BlockSpec((tm, d), lambda i: (i, 0))
    vec = pl.BlockSpec((1, d), lambda i: (0, 0))
    copies = [jax.ShapeDtypeStruct((m, d), _ACT)] if storage_copy else []
    return _fused_rows(
        name, a, b, product, a_spec, tm, [x, g, dres], [row, vec, row],
        [jax.ShapeDtypeStruct((m, d), f32)] + copies + [jax.ShapeDtypeStruct((1, d), f32)],
        [row] * (1 + len(copies)) + [vec], epilogue)


def _proj_loss_bwd(name, a, b, res, g, tgt):
    m, k = a.shape
    d = b.shape[1]
    tm = _row_tile(m, max(k, d))

    def epilogue(p, ins, outs):
        res_ref, g_ref, t_ref = ins
        dx_ref, dxb_ref, dg_ref, loss_ref = outs

        @pl.when(pl.program_id(0) == 0)
        def _():
            dg_ref[...] = jnp.zeros_like(dg_ref)
            loss_ref[...] = jnp.zeros_like(loss_ref)

        xv = p + res_ref[...]
        gv = g_ref[...]
        r = lax.rsqrt(jnp.mean(xv * xv, axis=-1, keepdims=True) + EPS)
        xn = xv * r
        err = xn * gv - t_ref[...]
        loss_ref[...] += 0.5 * jnp.sum(jnp.mean(err * err, axis=-1, keepdims=True), axis=0, keepdims=True)
        dout = err * (1.0 / d)
        dxn = dout * gv
        dx = r * (dxn - xn * jnp.mean(dxn * xn, axis=-1, keepdims=True))
        dx_ref[...] = dx
        dxb_ref[...] = dx.astype(dxb_ref.dtype)
        dg_ref[...] += jnp.sum(dout * xn, axis=0, keepdims=True)

    row = pl.BlockSpec((tm, d), lambda i: (i, 0))
    vec = pl.BlockSpec((1, d), lambda i: (0, 0))
    return _fused_rows(
        name, a, b, lambda a_ref, b_ref: _dot(a_ref[...], b_ref[...], _NN), pl.BlockSpec((tm, k), lambda i: (i, 0)), tm,
        [res, g, tgt], [row, vec, row],
        [jax.ShapeDtypeStruct((m, d), f32), jax.ShapeDtypeStruct((m, d), _ACT), jax.ShapeDtypeStruct((1, d), f32),
         jax.ShapeDtypeStruct((1, 1), f32)],
        [row, row, vec, pl.BlockSpec((1, 1), lambda i: (0, 0))], epilogue)


def _rms_gain_grad(name, x, dh):
    t, d = x.shape
    tm = _row_tile(t)

    def body(x_ref, dh_ref, dg_ref):
        @pl.when(pl.program_id(0) == 0)
        def _():
            dg_ref[...] = jnp.zeros_like(dg_ref)

        xv = x_ref[...]
        r = lax.rsqrt(jnp.mean(xv * xv, axis=-1, keepdims=True) + EPS)
        dg_ref[...] += jnp.sum(dh_ref[...] * (xv * r), axis=0, keepdims=True)

    row = pl.BlockSpec((tm, d), lambda i: (i, 0))
    return pl.pallas_call(
        body, name=name, grid=(t // tm,), in_specs=[row, row], out_specs=pl.BlockSpec((1, d), lambda i: (0, 0)),
        out_shape=jax.ShapeDtypeStruct((1, d), f32), compiler_params=_params(1),
    )(x, dh)


_CONV_ROWS = 256
_CHUNK = 64
_HALO = 32


def _pool_counts(pos, w):
    return jnp.minimum(pos + 1.0, float(w))


def _rows_from(win, start, rows):
    if start % 8 == 0:
        return win[start:start + rows, :]
    n = win.shape[0]
    return pltpu.roll(win, n - start % 8, axis=0)[start - start % 8:start - start % 8 + rows, :]


def _tap_rows(buf, starts, rows):
    for residue in range(8):
        group = [(k, s) for k, s in starts.items() if s % 8 == residue]
        if group:
            lo = min(s for _, s in group) - residue
            hi = max(s for _, s in group) - residue + rows + (8 if residue else 0)
            win = buf[lo:hi, :]
            if residue:
                win = pltpu.roll(win, hi - lo - residue, axis=0)
            for k, s in group:
                yield k, win[s - residue - lo:s - residue - lo + rows, :]


def _mix_fwd(u, cw, cb, lg, lb, pw, ps, seq):
    t, c3 = u.shape
    c = c3 // 3
    kw = 31
    tm = min(_CONV_ROWS, seq)
    tps = seq // tm
    gd = c // len(POOL_WINDOWS)

    def body(u_ref, uh_ref, cw_ref, cb_ref, lg_ref, lb_ref, pw_ref, ps_ref, y_ref, hc_ref, hgbuf, pbuf):
        i = pl.program_id(0)
        keep = jnp.where(i % tps == 0, 0.0, 1.0)
        um = u_ref[...].astype(f32)
        uh = uh_ref[...].astype(f32) * keep
        hgbuf[0:_HALO, :] = uh[:, 0:c] * _sigmoid(uh[:, c:2 * c])
        hgbuf[_HALO:_HALO + tm, :] = um[:, 0:c] * _sigmoid(um[:, c:2 * c])
        pbuf[0:_HALO, :] = uh[:, 2 * c:]
        pbuf[_HALO:_HALO + tm, :] = um[:, 2 * c:]
        for r0 in range(0, tm, _CHUNK):
            acc = jnp.broadcast_to(cb_ref[...], (_CHUNK, c))
            for k, rows in _tap_rows(hgbuf, {k: r0 + _HALO - (kw - 1) + k for k in range(kw)}, _CHUNK):
                acc = acc + cw_ref[k:k + 1, :] * rows
            hc_ref[r0:r0 + _CHUNK, :] = acc
            mu = jnp.mean(acc, axis=-1, keepdims=True)
            xc = acc - mu
            var = jnp.mean(xc * xc, axis=-1, keepdims=True)
            hl = xc * lax.rsqrt(var + EPS) * lg_ref[...] + lb_ref[...]
            y_ref[r0:r0 + _CHUNK, 0:c] = (hl * _sigmoid(hl)).astype(y_ref.dtype)
        pos = ((i % tps) * tm).astype(f32) + lax.broadcasted_iota(jnp.int32, (tm, 1), 0).astype(f32)
        for gi, w in enumerate(POOL_WINDOWS):
            sl = slice(gi * gd, (gi + 1) * gd)
            v = pbuf[_HALO:_HALO + tm, sl]
            s = v
            for j in range(1, w):
                s = s + pbuf[_HALO - j:_HALO - j + tm, sl]
            pooled = s / _pool_counts(pos, w) - v
            mixed = _dot(pooled.astype(_ACT), pw_ref[gi].astype(_ACT), _NN)
            y_ref[:, c + gi * gd:c + (gi + 1) * gd] = (mixed * ps_ref[:, sl]).astype(y_ref.dtype)

    hb = tm // _HALO
    full = lambda shape: pl.BlockSpec(shape, lambda i: (0,) * len(shape))
    return pl.pallas_call(
        body, name="mix_fwd", grid=(t // tm,),
        in_specs=[pl.BlockSpec((tm, c3), lambda i: (i, 0)),
                  pl.BlockSpec((_HALO, c3), lambda i: (jnp.maximum(i * hb - 1, 0), 0)),
                  full((_HALO, c)), full((1, c)), full((1, c)), full((1, c)), full((len(POOL_WINDOWS), gd, gd)), full((1, c))],
        out_specs=[pl.BlockSpec((tm, 2 * c), lambda i: (i, 0)), pl.BlockSpec((tm, c), lambda i: (i, 0))],
        out_shape=[jax.ShapeDtypeStruct((t, 2 * c), _ACT), jax.ShapeDtypeStruct((t, c), f32)],
        scratch_shapes=[pltpu.VMEM((_HALO + tm, c), f32), pltpu.VMEM((_HALO + tm, c), f32)],
        compiler_params=_params(1),
    )(u, u, cw, cb, lg, lb, pw, ps)


def _mix_bwd_norm(hc, dy, lg, lb, seq):
    t, c = hc.shape
    tm = _row_tile(t, c)

    def body(hc_ref, dy_ref, lg_ref, lb_ref, dhc_ref, sums_ref):
        @pl.when(pl.program_id(0) == 0)
        def _():
            sums_ref[...] = jnp.zeros_like(sums_ref)

        hcv = hc_ref[...]
        mu = jnp.mean(hcv, axis=-1, keepdims=True)
        xc = hcv - mu
        rstd = lax.rsqrt(jnp.mean(xc * xc, axis=-1, keepdims=True) + EPS)
        n = xc * rstd
        hl = n * lg_ref[...] + lb_ref[...]
        sg = _sigmoid(hl)
        dhl = dy_ref[...].astype(f32) * (sg * (1.0 + hl * (1.0 - sg)))
        dn = dhl * lg_ref[...]
        dhc = rstd * (dn - jnp.mean(dn, axis=-1, keepdims=True) - n * jnp.mean(dn * n, axis=-1, keepdims=True))
        dhc_ref[...] = dhc
        sums_ref[0:1, :] += jnp.sum(dhl * n, axis=0, keepdims=True)
        sums_ref[1:2, :] += jnp.sum(dhl, axis=0, keepdims=True)
        sums_ref[2:3, :] += jnp.sum(dhc, axis=0, keepdims=True)

    row = pl.BlockSpec((tm, c), lambda i: (i, 0))
    vec = pl.BlockSpec((1, c), lambda i: (0, 0))
    return pl.pallas_call(
        body, name="mix_bwd_norm", grid=(t // tm,), in_specs=[row, row, vec, vec],
        out_specs=[row, pl.BlockSpec((8, c), lambda i: (0, 0))],
        out_shape=[jax.ShapeDtypeStruct((t, c), f32), jax.ShapeDtypeStruct((8, c), f32)],
        compiler_params=_params(1),
    )(hc, dy, lg, lb)


def _mix_bwd_taps(u, dhc, dy, cw, pw, ps, seq):
    t, c3 = u.shape
    c = c3 // 3
    kw = 31
    tm = min(_CONV_ROWS, seq)
    tps = seq // tm
    ng = len(POOL_WINDOWS)
    gd = c // ng
    nh = 16

    def body(u_ref, uh_ref, dhc_ref, dhcn_ref, dy_ref, dyn_ref, cw_ref, pw_ref, ps_ref,
             du_ref, dcw_ref, dps_ref, dpw_ref, hgbuf, dcbuf, pbuf, dpbuf):
        i = pl.program_id(0)
        keep_prev = jnp.where(i % tps == 0, 0.0, 1.0)
        keep_next = jnp.where(i % tps == tps - 1, 0.0, 1.0)

        @pl.when(i == 0)
        def _():
            dcw_ref[...] = jnp.zeros_like(dcw_ref)
            dps_ref[...] = jnp.zeros_like(dps_ref)
            dpw_ref[...] = jnp.zeros_like(dpw_ref)

        uh = uh_ref[...].astype(f32) * keep_prev
        hgbuf[0:_HALO, :] = uh[:, 0:c] * _sigmoid(uh[:, c:2 * c])
        pbuf[0:_HALO, :] = uh[:, 2 * c:]
        um = u_ref[...].astype(f32)
        hgbuf[_HALO:_HALO + tm, :] = um[:, 0:c] * _sigmoid(um[:, c:2 * c])
        pbuf[_HALO:_HALO + tm, :] = um[:, 2 * c:]
        dcbuf[0:tm, :] = dhc_ref[...]
        dcbuf[tm:tm + _HALO, :] = dhcn_ref[...] * keep_next
        tap_sums = [None] * kw
        for r0 in range(0, tm, _CHUNK):
            dh = dcbuf[r0:r0 + _CHUNK, :]
            acc = jnp.zeros((_CHUNK, c), f32)
            for k, rows in _tap_rows(hgbuf, {k: r0 + _HALO - (kw - 1) + k for k in range(kw)}, _CHUNK):
                part = (dh * rows).reshape(_CHUNK // 8, 8, c).sum(axis=0)
                tap_sums[k] = part if tap_sums[k] is None else tap_sums[k] + part
            for k, rows in _tap_rows(dcbuf, {k: r0 + (kw - 1) - k for k in range(kw)}, _CHUNK):
                acc = acc + cw_ref[k:k + 1, :] * rows
            val = u_ref[r0:r0 + _CHUNK, 0:c].astype(f32)
            sg = _sigmoid(u_ref[r0:r0 + _CHUNK, c:2 * c].astype(f32))
            du_ref[r0:r0 + _CHUNK, 0:c] = (acc * sg).astype(du_ref.dtype)
            du_ref[r0:r0 + _CHUNK, c:2 * c] = (acc * val * sg * (1.0 - sg)).astype(du_ref.dtype)
        for k in range(kw):
            dcw_ref[k:k + 1, :] += jnp.sum(tap_sums[k], axis=0, keepdims=True)
        base = ((i % tps) * tm).astype(f32)
        pos = base + lax.broadcasted_iota(jnp.int32, (tm, 1), 0).astype(f32)
        pos_next = base + float(tm) + lax.broadcasted_iota(jnp.int32, (nh, 1), 0).astype(f32)
        for gi, w in enumerate(POOL_WINDOWS):
            sl = slice(gi * gd, (gi + 1) * gd)
            v = pbuf[_HALO:_HALO + tm, sl]
            s = v
            for j in range(1, w):
                s = s + pbuf[_HALO - j:_HALO - j + tm, sl]
            cnt = _pool_counts(pos, w)
            pooled = (s / cnt - v).astype(_ACT)
            pwg = pw_ref[gi].astype(_ACT)
            mixed = _dot(pooled, pwg, _NN)
            dyp = dy_ref[:, sl].astype(f32)
            dps_ref[0:1, sl] += jnp.sum(dyp * mixed, axis=0, keepdims=True)
            dmix = (dyp * ps_ref[:, sl]).astype(_ACT)
            dpw_ref[gi] += _dot(pooled, dmix, _TN)
            dmix_next = (dyn_ref[:, sl].astype(f32) * ps_ref[:, sl] * keep_next).astype(_ACT)
            dpool = _dot(dmix, pwg, _NT)
            dpbuf[0:tm, sl] = dpool / cnt
            dpbuf[tm:tm + nh, sl] = _dot(dmix_next, pwg, _NT) / _pool_counts(pos_next, w)
            acc = -dpool
            for j in range(w):
                acc = acc + dpbuf[j:j + tm, sl]
            du_ref[:, 2 * c + gi * gd:2 * c + (gi + 1) * gd] = acc.astype(du_ref.dtype)

    hb = tm // _HALO
    n_halo = t // _HALO
    n_nh = t // nh
    full = lambda shape: pl.BlockSpec(shape, lambda i: (0,) * len(shape))
    return pl.pallas_call(
        body, name="mix_bwd_taps", grid=(t // tm,),
        in_specs=[pl.BlockSpec((tm, c3), lambda i: (i, 0)),
                  pl.BlockSpec((_HALO, c3), lambda i: (jnp.maximum(i * hb - 1, 0), 0)),
                  pl.BlockSpec((tm, c), lambda i: (i, 0)),
                  pl.BlockSpec((_HALO, c), lambda i: (jnp.minimum((i + 1) * hb, n_halo - 1), 0)),
                  pl.BlockSpec((tm, c), lambda i: (i, 1)),
                  pl.BlockSpec((nh, c), lambda i: (jnp.minimum((i + 1) * (tm // nh), n_nh - 1), 1)),
                  full((_HALO, c)), full((ng, gd, gd)), full((1, c))],
        out_specs=[pl.BlockSpec((tm, c3), lambda i: (i, 0)), full((_HALO, c)), full((8, c)), full((ng, gd, gd))],
        out_shape=[jax.ShapeDtypeStruct((t, c3), _ACT), jax.ShapeDtypeStruct((_HALO, c), f32),
                   jax.ShapeDtypeStruct((8, c), f32), jax.ShapeDtypeStruct((ng, gd, gd), f32)],
        scratch_shapes=[pltpu.VMEM((_HALO + tm, c), f32), pltpu.VMEM((tm + _HALO, c), f32),
                        pltpu.VMEM((_HALO + tm, c), f32), pltpu.VMEM((tm + nh, c), f32)],
        compiler_params=_params(1),
    )(u, u, dhc, dhc, dy, dy, cw, pw, ps)


def _attn_fwd(q, kv, n_seq, seq, n_mem):
    t, d = q.shape
    dh = d // XATTN_HEADS
    tq = min(512, seq)
    nq = seq // tq
    scale = dh ** -0.5

    def body(q_ref, kv_ref, o_ref):
        for h in range(XATTN_HEADS):
            cols = slice(h * dh, (h + 1) * dh)
            s = _dot(q_ref[:, cols], kv_ref[:, cols], _NT) * scale
            e = jnp.exp(s - jnp.max(s, axis=-1, keepdims=True))
            p = e / jnp.sum(e, axis=-1, keepdims=True)
            o_ref[:, cols] = _dot(p.astype(_ACT), kv_ref[:, d + h * dh:d + (h + 1) * dh], _NN).astype(o_ref.dtype)

    qs = pl.BlockSpec((tq, d), lambda b, i: (b * nq + i, 0))
    return pl.pallas_call(
        body, name="attn_fwd", grid=(n_seq, nq), in_specs=[qs, pl.BlockSpec((n_mem, 2 * d), lambda b, i: (b, 0))],
        out_specs=qs, out_shape=jax.ShapeDtypeStruct((t, d), _ACT), compiler_params=_params(2),
    )(q, kv)


def _attn_bwd(q, kv, do, n_seq, seq, n_mem):
    t, d = q.shape
    dh = d // XATTN_HEADS
    tq = min(512, seq)
    nq = seq // tq
    scale = dh ** -0.5

    def body(q_ref, kv_ref, do_ref, dq_ref, dkv_ref, acc):
        i = pl.program_id(1)

        @pl.when(i == 0)
        def _():
            acc[...] = jnp.zeros_like(acc)

        for h in range(XATTN_HEADS):
            cols = slice(h * dh, (h + 1) * dh)
            vcols = slice(d + h * dh, d + (h + 1) * dh)
            qv = q_ref[:, cols]
            kh = kv_ref[:, cols]
            dov = do_ref[:, cols]
            s = _dot(qv, kh, _NT) * scale
            e = jnp.exp(s - jnp.max(s, axis=-1, keepdims=True))
            p = e / jnp.sum(e, axis=-1, keepdims=True)
            dp = _dot(dov, kv_ref[:, vcols], _NT)
            ds = (p * (dp - jnp.sum(dp * p, axis=-1, keepdims=True)) * scale).astype(_ACT)
            dq_ref[:, cols] = _dot(ds, kh, _NN).astype(dq_ref.dtype)
            acc[:, cols] += _dot(ds, qv, _TN)
            acc[:, vcols] += _dot(p.astype(_ACT), dov, _TN)

        @pl.when(i == nq - 1)
        def _():
            dkv_ref[...] = acc[...].astype(dkv_ref.dtype)

    qs = pl.BlockSpec((tq, d), lambda b, i: (b * nq + i, 0))
    ms = pl.BlockSpec((n_mem, 2 * d), lambda b, i: (b, 0))
    return pl.pallas_call(
        body, name="attn_bwd", grid=(n_seq, nq), in_specs=[qs, ms, qs], out_specs=[qs, ms],
        out_shape=[jax.ShapeDtypeStruct((t, d), _ACT), jax.ShapeDtypeStruct((n_seq * n_mem, 2 * d), _ACT)],
        scratch_shapes=[pltpu.VMEM((n_mem, 2 * d), f32)], compiler_params=_params(2),
    )(q, kv, do)


_FFN_ROWS = 2048
_FFN_COLS = 256
_FFN_HALO = 16


def _window(buf, g, start, rows):
    return buf[g, pl.ds(start, rows + 8), :]


def _taps3(win, rows):
    return [_rows_from(win, 6 + k, rows) for k in range(3)]


def _conv3(b_ref, w_ref, taps):
    acc = b_ref[...] + w_ref[0:1, :] * taps[0]
    for k in (1, 2):
        acc = acc + w_ref[k:k + 1, :] * taps[k]
    return acc


def _ffn_gate_fwd(up, fw, fb, seq):
    _, t, f = up.shape
    tm = min(_FFN_ROWS, seq)
    tps = seq // tm
    tc = _FFN_COLS
    nc = f // tc
    hl = _FFN_HALO

    def body(up_ref, uph_ref, wg_ref, wv_ref, bg_ref, bv_ref, a_ref):
        i = pl.program_id(1)
        before = uph_ref[...]
        before = jnp.where(i % tps == 0, jnp.zeros_like(before), before)

        def chunk(r0, wins):
            conv = []
            for g, (w_ref, b_ref) in enumerate(((wg_ref, bg_ref), (wv_ref, bv_ref))):
                conv.append(_conv3(b_ref, w_ref, _taps3(wins[g].astype(f32)[hl - 8:, :], _CHUNK)))
            gate, val = conv
            a_ref[pl.ds(r0, _CHUNK), :] = (gate * _sigmoid(gate) * val).astype(a_ref.dtype)

        chunk(0, [jnp.concatenate([before[g], up_ref[g, 0:_CHUNK, :]], axis=0) for g in range(2)])

        def later(ci, carry):
            r0 = pl.multiple_of(ci * _CHUNK, _CHUNK)
            chunk(r0, [up_ref[g, pl.ds(r0 - hl, _CHUNK + hl), :] for g in range(2)])
            return carry

        lax.fori_loop(1, tm // _CHUNK, later, 0)

    hb = tm // hl
    return pl.pallas_call(
        body, name="ffn_gate_fwd", grid=(nc, t // tm),
        in_specs=[pl.BlockSpec((2, tm, tc), lambda j, i: (0, i, j)),
                  pl.BlockSpec((2, hl, tc), lambda j, i: (0, jnp.maximum(i * hb - 1, 0), j)),
                  pl.BlockSpec((8, tc), lambda j, i: (0, j)), pl.BlockSpec((8, tc), lambda j, i: (0, nc + j)),
                  pl.BlockSpec((1, tc), lambda j, i: (0, j)), pl.BlockSpec((1, tc), lambda j, i: (0, nc + j))],
        out_specs=pl.BlockSpec((tm, tc), lambda j, i: (i, j)),
        out_shape=jax.ShapeDtypeStruct((t, f), _ACT), compiler_params=_params(2),
    )(up, up, fw, fw, fb, fb)


def _ffn_gate_bwd(up, da, fw, fb, seq):
    _, t, f = up.shape
    tm = min(_FFN_ROWS, seq)
    tps = seq // tm
    tc = _FFN_COLS
    nc = f // tc
    hl = _FFN_HALO

    def body(up_ref, uph_ref, upn_ref, da_ref, dan_ref, wg_ref, wv_ref, bg_ref, bv_ref,
             dup_ref, sg_ref, sv_ref, dbuf, sums):
        i = pl.program_id(1)
        at_end = i % tps == tps - 1

        @pl.when(i == 0)
        def _():
            sg_ref[...] = jnp.zeros_like(sg_ref)
            sv_ref[...] = jnp.zeros_like(sv_ref)

        sums[...] = jnp.zeros_like(sums)
        before = uph_ref[...]
        before = jnp.where(i % tps == 0, jnp.zeros_like(before), before)
        after = upn_ref[...]
        after = jnp.where(at_end, jnp.zeros_like(after), after)
        w_refs = (wg_ref, wv_ref)
        b_refs = (bg_ref, bv_ref)

        def grads(r0, rows, wins, dav, count):
            taps = [_taps3(wins[g].astype(f32)[hl - 8:, :], rows) for g in range(2)]
            gate, val = [_conv3(b_refs[g], w_refs[g], taps[g]) for g in range(2)]
            sg = _sigmoid(gate)
            douts = (dav * val * (sg * (1.0 + gate * (1.0 - sg))), dav * (gate * sg))
            for g in range(2):
                dbuf[g, pl.ds(r0, rows), :] = douts[g]
                if count:
                    sums[g, 0] += douts[g].reshape(rows // 8, 8, tc).sum(axis=0)
                    for k in range(3):
                        sums[g, 1 + k] += (douts[g] * taps[g][k]).reshape(rows // 8, 8, tc).sum(axis=0)

        grads(0, _CHUNK, [jnp.concatenate([before[g], up_ref[g, 0:_CHUNK, :]], axis=0) for g in range(2)],
              da_ref[0:_CHUNK, :].astype(f32), True)

        def first(ci, carry):
            r0 = pl.multiple_of(ci * _CHUNK, _CHUNK)
            grads(r0, _CHUNK, [up_ref[g, pl.ds(r0 - hl, _CHUNK + hl), :] for g in range(2)],
                  da_ref[pl.ds(r0, _CHUNK), :].astype(f32), True)
            return carry

        lax.fori_loop(1, tm // _CHUNK, first, 0)
        da_after = dan_ref[...].astype(f32)
        grads(tm, hl, [jnp.concatenate([up_ref[g, tm - hl:tm, :], after[g]], axis=0) for g in range(2)],
              jnp.where(at_end, jnp.zeros_like(da_after), da_after), False)

        def second(ci, carry):
            r0 = pl.multiple_of(ci * _CHUNK, _CHUNK)
            for g in range(2):
                win = _window(dbuf, g, r0, _CHUNK)
                acc = jnp.zeros((_CHUNK, tc), f32)
                for k in range(3):
                    acc = acc + w_refs[g][k:k + 1, :] * _rows_from(win, 2 - k, _CHUNK)
                dup_ref[g, pl.ds(r0, _CHUNK), :] = acc.astype(dup_ref.dtype)
            return carry

        lax.fori_loop(0, tm // _CHUNK, second, 0)
        for g, s_ref in enumerate((sg_ref, sv_ref)):
            for r in range(4):
                s_ref[r:r + 1, :] += jnp.sum(sums[g, r], axis=0, keepdims=True)

    hb = tm // hl
    n_halo = t // hl
    return pl.pallas_call(
        body, name="ffn_gate_bwd", grid=(nc, t // tm),
        in_specs=[pl.BlockSpec((2, tm, tc), lambda j, i: (0, i, j)),
                  pl.BlockSpec((2, hl, tc), lambda j, i: (0, jnp.maximum(i * hb - 1, 0), j)),
                  pl.BlockSpec((2, hl, tc), lambda j, i: (0, jnp.minimum((i + 1) * hb, n_halo - 1), j)),
                  pl.BlockSpec((tm, tc), lambda j, i: (i, j)),
                  pl.BlockSpec((hl, tc), lambda j, i: (jnp.minimum((i + 1) * hb, n_halo - 1), j)),
                  pl.BlockSpec((8, tc), lambda j, i: (0, j)), pl.BlockSpec((8, tc), lambda j, i: (0, nc + j)),
                  pl.BlockSpec((1, tc), lambda j, i: (0, j)), pl.BlockSpec((1, tc), lambda j, i: (0, nc + j))],
        out_specs=[pl.BlockSpec((2, tm, tc), lambda j, i: (0, i, j)),
                   pl.BlockSpec((8, tc), lambda j, i: (0, j)), pl.BlockSpec((8, tc), lambda j, i: (0, j))],
        out_shape=[jax.ShapeDtypeStruct((2, t, f), _ACT), jax.ShapeDtypeStruct((8, f), f32), jax.ShapeDtypeStruct((8, f), f32)],
        scratch_shapes=[pltpu.VMEM((2, tm + hl, tc), f32), pltpu.VMEM((2, 4, 8, tc), f32)],
        compiler_params=_params(2),
    )(up, up, up, da, da, fw, fw, fb, fb)


def _adamw_math(w, g, m, v):
    m = ADAM_B1 * m + (1.0 - ADAM_B1) * g
    v = ADAM_B2 * v + (1.0 - ADAM_B2) * (g * g)
    m_hat = m / (1.0 - ADAM_B1 ** ADAM_STEP)
    v_hat = v / (1.0 - ADAM_B2 ** ADAM_STEP)
    delta = -ADAM_LR * (m_hat / (jnp.sqrt(v_hat) + ADAM_EPS) + ADAM_WD * w)
    return delta, m, v


def _adamw_shard(name, w, g, m, v):
    _, r, c = w.shape
    tr = next((cand for cand in (256, 176, 128, 64, 32, 16, 8) if r % cand == 0), r)

    def body(w_ref, g_ref, m_ref, v_ref, go_ref, d_ref, mo_ref, vo_ref):
        gv = g_ref[...]
        d, mn, vn = _adamw_math(w_ref[...], gv, m_ref[...], v_ref[...])
        go_ref[...] = gv
        d_ref[...] = d
        mo_ref[...] = mn
        vo_ref[...] = vn

    s3 = pl.BlockSpec((None, tr, c), lambda i: (0, i, 0))
    s2 = pl.BlockSpec((tr, c), lambda i: (i, 0))
    shp = jax.ShapeDtypeStruct(w.shape, f32)
    return pl.pallas_call(
        body, name=name, grid=(r // tr,), in_specs=[s3, s2, s3, s3], out_specs=[s3] * 4, out_shape=[shp] * 4,
        compiler_params=_params(1),
    )(w, g, m, v)


def _adamw_small(quads):
    n = len(quads)

    def body(*refs):
        ins, outs = refs[:4 * n], refs[4 * n:]
        for p in range(n):
            w_ref, g_ref, m_ref, v_ref = ins[4 * p:4 * p + 4]
            d, mn, vn = _adamw_math(w_ref[...], g_ref[...], m_ref[...], v_ref[...])
            outs[3 * p][...] = d
            outs[3 * p + 1][...] = mn
            outs[3 * p + 2][...] = vn

    flat = [a for q in quads for a in q]
    shapes = [jax.ShapeDtypeStruct(q[0].shape, f32) for q in quads for _ in range(3)]
    outs = pl.pallas_call(
        body, name="adamw_small", in_specs=[_VMEM] * (4 * n), out_specs=[_VMEM] * (3 * n), out_shape=shapes,
        compiler_params=pltpu.CompilerParams(vmem_limit_bytes=_VMEM_LIMIT_BYTES),
    )(*flat)
    return [tuple(outs[3 * p:3 * p + 3]) for p in range(n)]


def _sum_partials(name, place, grads, got):
    nw = len(grads)
    steps = 2

    def body(place_ref, *refs):
        for w in range(nw):
            own_ref, got_ref, f_ref = refs[w], refs[nw + w], refs[2 * nw + w]
            s = own_ref[...].astype(f32)
            for k in range(got[w].shape[0]):
                s = s + got_ref[k].astype(f32)
            f_ref[...] = s

    own_specs, got_specs, out_specs, out_shape = [], [], [], []
    for g, l in zip(grads, got):
        _, r, c = g.shape
        tr = r // steps
        own_specs.append(pl.BlockSpec((None, tr, c), lambda i, p: (2 * p[0] + p[1], i, 0)))
        got_specs.append(pl.BlockSpec((l.shape[0], tr, c), lambda i, p: (0, i, 0)))
        out_specs.append(pl.BlockSpec((None, tr, c), lambda i, p: (p[1], i, 0)))
        out_shape.append(jax.ShapeDtypeStruct((2, r, c), f32))
    grid_spec = pltpu.PrefetchScalarGridSpec(num_scalar_prefetch=1, grid=(steps,), in_specs=own_specs + got_specs, out_specs=out_specs)
    return pl.pallas_call(body, name=name, grid_spec=grid_spec, out_shape=out_shape,
                          compiler_params=_params(1))(place, *grads, *got)


def _place():
    return lax.axis_index("x"), lax.axis_index("y"), lax.axis_index("c")


def _other_chips(x, y):
    return [(1 - x, y), (x, 1 - y), (1 - x, 1 - y)]


def _remote(src, dst, send_sem, recv_sem, to):
    return pltpu.make_async_remote_copy(src_ref=src, dst_ref=dst, send_sem=send_sem, recv_sem=recv_sem,
                                        device_id=to, device_id_type=_MESH)


def _place_shards(place, shards, col_sharded):
    n = len(shards)
    steps = 4

    def body(place_ref, *refs):
        for src, dst in zip(refs[:n], refs[n:]):
            dst[...] = src[...].astype(dst.dtype)

    in_specs, out_specs, out_shape = [], [], []
    for w, col in zip(shards, col_sharded):
        r, cs = w.shape
        tr = r // steps
        in_specs.append(pl.BlockSpec((tr, cs), lambda i, p: (i, 0)))
        if col:
            out_specs.append(pl.BlockSpec((tr, cs), lambda i, p: (i, p[0])))
            out_shape.append(jax.ShapeDtypeStruct((r, 4 * cs), _ACT))
        else:
            out_specs.append(pl.BlockSpec((tr, cs), lambda i, p: (p[0] * steps + i, 0)))
            out_shape.append(jax.ShapeDtypeStruct((4 * r, cs), _ACT))
    grid_spec = pltpu.PrefetchScalarGridSpec(num_scalar_prefetch=1, grid=(steps,), in_specs=in_specs, out_specs=out_specs)
    return pl.pallas_call(body, name="place_shards", grid_spec=grid_spec, out_shape=out_shape,
                          compiler_params=_params(1))(place, *shards)


def _shard_of(ref, col_sharded, s):
    rows, cols = ref.shape
    if col_sharded:
        return ref.at[:, pl.ds(s * (cols // 4), cols // 4)]
    return ref.at[pl.ds(s * (rows // 4), rows // 4), :]


def _part_of(ref, col_sharded, whole, s, h):
    if whole:
        return _shard_of(ref, col_sharded, s)
    rows, cols = ref.shape
    if col_sharded:
        return ref.at[pl.ds(h * (rows // 2), rows // 2), pl.ds(s * (cols // 4), cols // 4)]
    return ref.at[pl.ds((2 * s + h) * (rows // 8), rows // 8), :]


def _allgather_start(bufs, col_sharded, whole, groups):
    n = len(bufs)
    ng = len(groups)

    def body(*refs):
        out = refs[n:2 * n]
        sems = refs[2 * n:]
        x, y, c = _place()
        for g, members in enumerate(groups):
            for i, w in enumerate(members):
                mine = _part_of(out[w], col_sharded[w], whole[w], 2 * x + y, c)
                for j, chip in enumerate(_other_chips(x, y)):
                    _remote(mine, mine, sems[2 * g].at[3 * i + j], sems[2 * g + 1].at[3 * i + j], (*chip, c)).start()

    sem_shapes = [pltpu.SemaphoreType.DMA((3 * len(m),)) for m in groups for _ in range(2)]
    outs = pl.pallas_call(
        body, name="allgather_start", in_specs=[_HBM] * n, out_specs=[_HBM] * n + [_SEM] * (2 * ng),
        out_shape=[pltpu.HBM(b.shape, b.dtype) for b in bufs] + sem_shapes,
        input_output_aliases={i: i for i in range(n)},
        compiler_params=pltpu.CompilerParams(has_side_effects=_EFFECT),
    )(*[pltpu.with_memory_space_constraint(b, pltpu.HBM) for b in bufs])
    return list(outs[:n]), [(outs[n + 2 * g], outs[n + 2 * g + 1]) for g in range(ng)]


def _allgather_relay(name, bufs, col_sharded, whole, sems, after):
    n = len(bufs)

    def body(*refs):
        buf = refs[:n]
        send, recv = refs[n], refs[n + 1]
        out = refs[n + 3:2 * n + 3]
        to_sibling, from_sibling = refs[2 * n + 3:]
        x, y, c = _place()
        for i in range(n):
            mine = _part_of(buf[i], col_sharded[i], whole[i], 2 * x + y, c)
            for j, chip in enumerate(_other_chips(x, y)):
                landed = _part_of(buf[i], col_sharded[i], whole[i], 2 * chip[0] + chip[1], c)
                cp = _remote(mine, landed, send.at[3 * i + j], recv.at[3 * i + j], (*chip, c))
                cp.wait_send()
                cp.wait_recv()
        for i in range(n):
            if not whole[i]:
                for j, chip in enumerate(_other_chips(x, y)):
                    landed = _part_of(out[i], col_sharded[i], False, 2 * chip[0] + chip[1], c)
                    _remote(landed, landed, to_sibling.at[3 * i + j], from_sibling.at[3 * i + j], (x, y, 1 - c)).start()

    outs = pl.pallas_call(
        body, name=name, in_specs=[_HBM] * n + [_SEM, _SEM, _ANY], out_specs=[_HBM] * n + [_SEM, _SEM],
        out_shape=[pltpu.HBM(b.shape, b.dtype) for b in bufs] + [pltpu.SemaphoreType.DMA((3 * n,))] * 2,
        input_output_aliases={i: i for i in range(n)},
        compiler_params=pltpu.CompilerParams(has_side_effects=_EFFECT),
    )(*bufs, *sems, after)
    return list(outs[:n]), (outs[n], outs[n + 1])


def _allgather_wait(name, bufs, col_sharded, whole, sems, after):
    n = len(bufs)

    def body(*refs):
        buf = refs[:n]
        to_sibling, from_sibling = refs[n], refs[n + 1]
        x, y, c = _place()
        for i in range(n):
            if not whole[i]:
                for j, chip in enumerate(_other_chips(x, y)):
                    sent = _part_of(buf[i], col_sharded[i], False, 2 * chip[0] + chip[1], c)
                    landed = _part_of(buf[i], col_sharded[i], False, 2 * chip[0] + chip[1], 1 - c)
                    cp = _remote(sent, landed, to_sibling.at[3 * i + j], from_sibling.at[3 * i + j], (x, y, 1 - c))
                    cp.wait_send()
                    cp.wait_recv()

    return pl.pallas_call(
        body, name=name, in_specs=[_HBM] * n + [_SEM, _SEM, _ANY], out_specs=[_HBM] * n,
        out_shape=[pltpu.HBM(b.shape, b.dtype) for b in bufs],
        input_output_aliases={i: i for i in range(n)},
        compiler_params=pltpu.CompilerParams(has_side_effects=_EFFECT),
    )(*bufs, *sems, after)


def _other_devices(x, y, c):
    flips = [(bx, by, bc) for bx in (0, 1) for by in (0, 1) for bc in (0, 1)][1:]
    return [(1 - x if bx else x, 1 - y if by else y, 1 - c if bc else c) for bx, by, bc in flips]


def _grad_exchange_start(name, grads):
    nw = len(grads)
    lands = [lax.empty((7,) + g.shape[1:], g.dtype) for g in grads]

    def body(*refs):
        src = refs[2 * nw:3 * nw]
        got = refs[3 * nw:4 * nw]
        send, recv, token = refs[4 * nw:]
        x, y, c = _place()
        for w in range(nw):
            for k, (px, py, pc) in enumerate(_other_devices(x, y, c)):
                _remote(src[w].at[4 * px + 2 * py + pc], got[w].at[k], send.at[7 * w + k], recv.at[7 * w + k], (px, py, pc)).start()
        token[...] = jnp.zeros_like(token)

    outs = pl.pallas_call(
        body, name=name, in_specs=[_HBM] * (2 * nw), out_specs=[_HBM] * (2 * nw) + [_SEM, _SEM, _VMEM],
        out_shape=[pltpu.HBM(a.shape, a.dtype) for a in list(grads) + lands]
        + [pltpu.SemaphoreType.DMA((7 * nw,)), pltpu.SemaphoreType.DMA((7 * nw,)), jax.ShapeDtypeStruct((8, 128), f32)],
        input_output_aliases={i: i for i in range(2 * nw)},
        compiler_params=pltpu.CompilerParams(has_side_effects=_EFFECT),
    )(*[pltpu.with_memory_space_constraint(a, pltpu.HBM) for a in list(grads) + lands])
    return list(outs[:nw]), list(outs[nw:2 * nw]), (outs[2 * nw], outs[2 * nw + 1]), outs[2 * nw + 2]


def _grad_exchange_wait(name, grads, got, sems, after):
    nw = len(grads)

    def body(*refs):
        src = refs[:nw]
        land = refs[nw:2 * nw]
        send, recv = refs[2 * nw], refs[2 * nw + 1]
        x, y, c = _place()
        for w in range(nw):
            for k, (px, py, pc) in enumerate(_other_devices(x, y, c)):
                cp = _remote(src[w].at[4 * px + 2 * py + pc], land[w].at[k], send.at[7 * w + k], recv.at[7 * w + k], (px, py, pc))
                cp.wait_send()
                cp.wait_recv()

    outs = pl.pallas_call(
        body, name=name, in_specs=[_HBM] * (2 * nw) + [_SEM, _SEM, _ANY], out_specs=[_HBM] * (2 * nw),
        out_shape=[pltpu.HBM(a.shape, a.dtype) for a in list(grads) + list(got)],
        input_output_aliases={i: i for i in range(2 * nw)},
        compiler_params=pltpu.CompilerParams(has_side_effects=_EFFECT),
    )(*grads, *got, *sems, after)
    return list(outs[:nw]), list(outs[nw:])


def _swap_halves_start(finals):
    nw = len(finals)

    def body(*refs):
        buf = refs[nw:2 * nw]
        send, recv = refs[2 * nw:]
        x, y, c = _place()
        for w in range(nw):
            _remote(buf[w].at[c], buf[w].at[c], send.at[w], recv.at[w], (x, y, 1 - c)).start()

    outs = pl.pallas_call(
        body, name="rs_swap_start", in_specs=[_HBM] * nw, out_specs=[_HBM] * nw + [_SEM, _SEM],
        out_shape=[pltpu.HBM(g.shape, g.dtype) for g in finals] + [pltpu.SemaphoreType.DMA((nw,))] * 2,
        input_output_aliases={i: i for i in range(nw)},
        compiler_params=pltpu.CompilerParams(has_side_effects=_EFFECT),
    )(*[pltpu.with_memory_space_constraint(g, pltpu.HBM) for g in finals])
    return list(outs[:nw]), (outs[nw], outs[nw + 1])


def _swap_halves_wait(bufs, sems, after):
    nw = len(bufs)

    def body(*refs):
        buf = refs[:nw]
        send, recv = refs[nw], refs[nw + 1]
        x, y, c = _place()
        for w in range(nw):
            cp = _remote(buf[w].at[c], buf[w].at[1 - c], send.at[w], recv.at[w], (x, y, 1 - c))
            cp.wait_send()
            cp.wait_recv()

    return pl.pallas_call(
        body, name="rs_swap_wait", in_specs=[_HBM] * nw + [_SEM, _SEM, _ANY], out_specs=[_HBM] * nw,
        out_shape=[pltpu.HBM(g.shape, g.dtype) for g in bufs],
        input_output_aliases={i: i for i in range(nw)},
        compiler_params=pltpu.CompilerParams(has_side_effects=_EFFECT),
    )(*bufs, *sems, after)


def _half_slices(shape, h):
    rows, cols = shape
    if cols % 256 == 0:
        return (slice(None), slice(h * (cols // 2), (h + 1) * (cols // 2)))
    return (slice(h * (rows // 2), (h + 1) * (rows // 2)), slice(None))


def _allreduce_small(parts):
    n = len(parts)

    def body(*refs):
        src = refs[:n]
        out = refs[n:2 * n]
        sib = refs[2 * n:3 * n]
        chip_sum = refs[3 * n:4 * n]
        slots = refs[4 * n:5 * n]
        pair_send, pair_recv, ici_send, ici_recv, swap_send, swap_recv = refs[5 * n:]
        x, y, c = _place()
        me_chip = 2 * x + y
        chips = _other_chips(x, y)
        pairs = [_remote(src[a], sib[a], pair_send.at[a], pair_recv.at[a], (x, y, 1 - c)) for a in range(n)]
        for rc in pairs:
            rc.start()
        for a in range(n):
            pairs[a].wait_recv()
            chip_sum[a][...] = src[a][...] + sib[a][...]
        for h in (0, 1):
            @pl.when(c == h)
            def _():
                sends = []
                for a in range(n):
                    idx = _half_slices(parts[a].shape, h)
                    for j, chip in enumerate(chips):
                        rc = _remote(chip_sum[a].at[idx], slots[a].at[me_chip].at[idx], ici_send.at[3 * a + j], ici_recv.at[3 * a + j], (*chip, h))
                        rc.start()
                        sends.append(rc)
                    slots[a][(me_chip,) + idx] = chip_sum[a][idx]
                for a in range(n):
                    idx = _half_slices(parts[a].shape, h)
                    for j, chip in enumerate(chips):
                        landed = slots[a].at[2 * chip[0] + chip[1]].at[idx]
                        _remote(landed, landed, ici_send.at[3 * a + j], ici_recv.at[3 * a + j], (x, y, c)).wait_recv()
                    total = slots[a][(0,) + idx]
                    for s in range(1, 4):
                        total = total + slots[a][(s,) + idx]
                    out[a][idx] = total
                    rc = _remote(out[a].at[idx], out[a].at[idx], swap_send.at[a], swap_recv.at[a], (x, y, 1 - h))
                    rc.start()
                    sends.append(rc)
                for a in range(n):
                    other = out[a].at[_half_slices(parts[a].shape, 1 - h)]
                    _remote(other, other, swap_send.at[a], swap_recv.at[a], (x, y, c)).wait_recv()
                for rc in sends:
                    rc.wait_send()
        for rc in pairs:
            rc.wait_send()

    return pl.pallas_call(
        body, name="allreduce_small", in_specs=[_VMEM] * n, out_specs=[_VMEM] * n,
        out_shape=[jax.ShapeDtypeStruct(p.shape, f32) for p in parts],
        scratch_shapes=[pltpu.VMEM(p.shape, f32) for p in parts] * 2 + [pltpu.VMEM((4,) + p.shape, f32) for p in parts]
        + [pltpu.SemaphoreType.DMA((n,)), pltpu.SemaphoreType.DMA((n,)), pltpu.SemaphoreType.DMA((3 * n,)),
           pltpu.SemaphoreType.DMA((3 * n,)), pltpu.SemaphoreType.DMA((n,)), pltpu.SemaphoreType.DMA((n,))],
        compiler_params=pltpu.CompilerParams(vmem_limit_bytes=_VMEM_LIMIT_BYTES),
    )(*parts)


def _local_step(x, mem, tgt, g_mix, g_xattn, g_mem, g_ffn, g_final, cb, lg, lb, pw, ps, fb, relay, weights, reduce, n_seq, seq, n_mem):
    t, d = x.shape
    f = fb.shape[1] // 2
    c = cb.shape[1]
    h1 = _rms_fwd("norm_mix", x, g_mix)
    relay(0, h1)
    w_in, cw, fw = weights(0, h1)
    u = _mm_nn("proj_in", h1, w_in, _ACT, w_in.shape[1])
    y, hc = _mix_fwd(u, cw, cb, lg, lb, pw, ps, seq)
    relay(1, y)
    w_out, w_q, w_kv, w_o = weights(1, y)
    x1, h2 = _proj_residual_norm("proj_out", y, w_out, x, g_xattn)
    q = _mm_nn("proj_q", h2, w_q, _ACT, d)
    mem_n = _rms_fwd("norm_mem", mem, g_mem)
    kv = _mm_nn("proj_kv", mem_n, w_kv, _ACT, 2 * d)
    o = _attn_fwd(q, kv, n_seq, seq, n_mem)
    relay(2, o)
    x2, h3 = _proj_residual_norm("proj_o", o, w_o, x1, g_ffn)
    w_up, w_down = weights(2, h3)
    up = _mm_nn("proj_up", h3, w_up, _ACT, f, split_out=True)
    a = _ffn_gate_fwd(up, fw, fb, seq)
    dx3, dx3b, dg_final, loss = _proj_loss_bwd("proj_down", a, w_down, x2, g_final, tgt)
    da = _mm_nt("d_act", dx3b, w_down, _ACT)
    gw_down = _mm_tn_rows("dw_down", a, dx3b, f // 2, d // 2)
    dup, sums_g, sums_v = _ffn_gate_bwd(up, da, fw, fb, seq)
    gw_up = _mm_tn_pieces("dw_up", h3, dup, f // 2, t)
    token = reduce(0, [gw_down.reshape(8, -1, d), gw_up])
    dx2, dx2b, dg_ffn = _dproj_rms_bwd("d_h3", dup, w_up, x2, g_ffn + token, dx3)
    do = _mm_nt("d_o", dx2b, w_o, _ACT)
    gw_o = _mm_tn_rows("dw_o", o, dx2b, d, d // 2)
    dq, dkv = _attn_bwd(q, kv, do, n_seq, seq, n_mem)
    gw_q = _mm_tn_rows("dw_q", h2, dq, d, d // 2)
    gw_kv = _mm_tn_pieces("dw_kv", mem_n, dkv, d // 2, mem.shape[0])
    dmem_n = _mm_nt("d_mem_n", dkv, w_kv, f32)
    dg_mem = _rms_gain_grad("norm_mem_bwd", mem, dmem_n)
    dx1, dx1b, dg_xattn = _dproj_rms_bwd("d_h2", dq, w_q, x1, g_xattn, dx2)
    dy = _mm_nt("d_y", dx1b, w_out, _ACT)
    gw_out = _mm_tn_rows("dw_out", y, dx1b, d, d // 2)
    token = reduce(1, [gw_o.reshape(8, -1, d), gw_q.reshape(8, -1, d), gw_kv, gw_out.reshape(8, -1, d)])
    dhc, sums_norm = _mix_bwd_norm(hc, dy, lg + token, lb, seq)
    du, d_cw, d_ps, d_pw = _mix_bwd_taps(u, dhc, dy, cw, pw, ps, seq)
    gw_in = _mm_tn_pieces("dw_in", h1, du, c * 3 // 4, t)
    token = reduce(2, [gw_in])
    grad_x, dg_mix = _dproj_rms_bwd("d_h1", du, w_in, x, g_mix + token, dx1, storage_copy=False)
    zero_row = jnp.zeros((1, d), f32)
    gains = jnp.concatenate([dg_mix, dg_xattn, dg_mem, dg_ffn, dg_final, jnp.pad(loss, ((0, 0), (0, d - 1))), zero_row, zero_row], axis=0)
    conv_rows = jnp.concatenate([sums_norm[2:3], sums_norm[0:1], sums_norm[1:2], d_ps[0:1], jnp.zeros((4, c), f32)], axis=0)
    ffn_rows = jnp.concatenate([sums_g, sums_v], axis=1)
    small = [gains, conv_rows, d_pw.reshape(-1, d_pw.shape[-1]), ffn_rows, d_cw]
    return grad_x, small


def kernel(x, mem, norm_mix_g, w_in, conv_dw_w, conv_dw_b, conv_ln_g, conv_ln_b, pool_w, pool_scale, w_out, norm_xattn_g, norm_mem_g, w_q, w_kv, w_o, norm_ffn_g, w_up, ffn_dw_w, ffn_dw_b, w_down, norm_final_g, loss_target, m_norm_mix_g, m_w_in, m_conv_dw_w, m_conv_dw_b, m_conv_ln_g, m_conv_ln_b, m_pool_w, m_pool_scale, m_w_out, m_norm_xattn_g, m_norm_mem_g, m_w_q, m_w_kv, m_w_o, m_norm_ffn_g, m_w_up, m_ffn_dw_w, m_ffn_dw_b, m_w_down, m_norm_final_g, v_norm_mix_g, v_w_in, v_conv_dw_w, v_conv_dw_b, v_conv_ln_g, v_conv_ln_b, v_pool_w, v_pool_scale, v_w_out, v_norm_xattn_g, v_norm_mem_g, v_w_q, v_w_kv, v_w_o, v_norm_ffn_g, v_w_up, v_ffn_dw_w, v_ffn_dw_b, v_w_down, v_norm_final_g):
    n_seq, seq, d = x.shape
    n_mem = mem.shape[1]
    chip = 2 * lax.axis_index("x") + lax.axis_index("y")

    place = jnp.stack([chip, lax.axis_index("c")]).astype(jnp.int32)

    col_w = [w_in, w_kv, w_up]
    row_w = [w_out, w_q, w_o, w_down]
    col_flags = [True] * 3 + [False] * 4 + [True] * 2
    kw = conv_dw_w.shape[1]

    def padded_in_place(shard, rows):
        full = jnp.zeros((rows, 4 * shard.shape[1]), shard.dtype)
        return lax.dynamic_update_slice(full, shard, (0, chip * shard.shape[1]))

    bufs = list(_place_shards(place, [w[0] for w in col_w + row_w], col_flags[:7]))
    bufs += [padded_in_place(conv_dw_w[0], _HALO), padded_in_place(ffn_dw_w[0], 8)]
    groups = [[0, 7, 8], [3, 4, 1, 5], [2, 6]]
    whole = [False] * 7 + [True] * 2
    bufs, sems = _allgather_start(bufs, col_flags, whole, groups)
    relayed = {}

    def relay(g, after):
        members = groups[g]
        relayed[g] = _allgather_relay("allgather_relay_%d" % g, [bufs[i] for i in members], [col_flags[i] for i in members],
                                      [whole[i] for i in members], sems[g], after)

    def weights(g, after):
        members = groups[g]
        group_bufs, sibling_sems = relayed[g]
        return _allgather_wait("allgather_wait_%d" % g, group_bufs, [col_flags[i] for i in members],
                               [whole[i] for i in members], sibling_sems, after)

    names = ["w_in", "w_kv", "w_up", "w_out", "w_q", "w_o", "w_down"]
    reduce_groups = [["w_down", "w_up"], ["w_o", "w_q", "w_kv", "w_out"], ["w_in"]]
    in_flight = {}

    def reduce(g, grads):
        grads, lands, rs_sems, token = _grad_exchange_start("rs_start_%d" % g, grads)
        in_flight[g] = (grads, lands, rs_sems)
        return token[0:1, 0:1]

    grad_x, small = _local_step(
        x.reshape(n_seq * seq, d), mem.reshape(n_seq * n_mem, d), loss_target.reshape(n_seq * seq, d),
        norm_mix_g, norm_xattn_g, norm_mem_g, norm_ffn_g, norm_final_g.reshape(1, d),
        conv_dw_b, conv_ln_g, conv_ln_b, pool_w[0], pool_scale, ffn_dw_b, relay, weights, reduce, n_seq, seq, n_mem)

    landed = {}
    for g, members in enumerate(reduce_groups):
        grads, lands, rs_sems = in_flight[g]
        grads, lands = _grad_exchange_wait("rs_wait_%d" % g, grads, lands, rs_sems, grad_x)
        landed.update(zip(members, zip(grads, lands)))
    finals = _sum_partials("rs_sum", place, [landed[n][0] for n in names], [landed[n][1] for n in names])
    finals, swap_sems = _swap_halves_start(finals)

    gains, conv_rows, d_pw, ffn_rows, d_cw = _allreduce_small(small)
    loss = gains[5, 0]
    shard_grads = _swap_halves_wait(finals, swap_sems, gains)

    outs = {}
    big_w = dict(zip(names, col_w + row_w))
    big_m = dict(w_in=m_w_in, w_kv=m_w_kv, w_up=m_w_up, w_out=m_w_out, w_q=m_w_q, w_o=m_w_o, w_down=m_w_down)
    big_v = dict(w_in=v_w_in, w_kv=v_w_kv, w_up=v_w_up, w_out=v_w_out, w_q=v_w_q, w_o=v_w_o, w_down=v_w_down)
    for n, g in zip(names, shard_grads):
        w = big_w[n]
        g2 = g.reshape(w.shape[1], w.shape[2])
        outs[n] = tuple(_adamw_shard("adamw_" + n, w, g2, big_m[n], big_v[n]))

    f2 = ffn_dw_b.shape[1]
    cs_c = conv_dw_w.shape[2]
    cs_f = ffn_dw_w.shape[2]
    g_cw = lax.dynamic_slice(d_cw, (0, chip * cs_c), (kw, cs_c)).reshape(conv_dw_w.shape)
    g_fw = lax.dynamic_slice(ffn_rows, (1, chip * cs_f), (ffn_dw_w.shape[1], cs_f)).reshape(ffn_dw_w.shape)
    small_params = [
        ("norm_mix_g", norm_mix_g, gains[0:1], m_norm_mix_g, v_norm_mix_g),
        ("conv_dw_w", conv_dw_w, g_cw, m_conv_dw_w, v_conv_dw_w),
        ("conv_dw_b", conv_dw_b, conv_rows[0:1], m_conv_dw_b, v_conv_dw_b),
        ("conv_ln_g", conv_ln_g, conv_rows[1:2], m_conv_ln_g, v_conv_ln_g),
        ("conv_ln_b", conv_ln_b, conv_rows[2:3], m_conv_ln_b, v_conv_ln_b),
        ("pool_w", pool_w, d_pw.reshape(pool_w.shape), m_pool_w, v_pool_w),
        ("pool_scale", pool_scale, conv_rows[3:4], m_pool_scale, v_pool_scale),
        ("norm_xattn_g", norm_xattn_g, gains[1:2], m_norm_xattn_g, v_norm_xattn_g),
        ("norm_mem_g", norm_mem_g, gains[2:3], m_norm_mem_g, v_norm_mem_g),
        ("norm_ffn_g", norm_ffn_g, gains[3:4], m_norm_ffn_g, v_norm_ffn_g),
        ("ffn_dw_w", ffn_dw_w, g_fw, m_ffn_dw_w, v_ffn_dw_w),
        ("ffn_dw_b", ffn_dw_b, ffn_rows[0:1, :f2], m_ffn_dw_b, v_ffn_dw_b),
        ("norm_final_g", norm_final_g.reshape(1, d), gains[4:5], m_norm_final_g.reshape(1, d), v_norm_final_g.reshape(1, d)),
    ]
    quads = []
    for _, w, g, m, v in small_params:
        shape2 = (-1, w.shape[-1])
        quads.append((w.reshape(shape2), g.reshape(shape2), m.reshape(shape2), v.reshape(shape2)))
    for (n, w, g, _, _), (delta, new_m, new_v) in zip(small_params, _adamw_small(quads)):
        shape = norm_final_g.shape if n == "norm_final_g" else w.shape
        outs[n] = (g.reshape(shape), delta.reshape(shape), new_m.reshape(shape), new_v.reshape(shape))

    order = ["norm_mix_g", "w_in", "conv_dw_w", "conv_dw_b", "conv_ln_g", "conv_ln_b", "pool_w", "pool_scale", "w_out",
             "norm_xattn_g", "norm_mem_g", "w_q", "w_kv", "w_o", "norm_ffn_g", "w_up", "ffn_dw_w", "ffn_dw_b", "w_down",
             "norm_final_g"]
    return (loss, grad_x.reshape(x.shape), *[outs[n][0] for n in order], *[outs[n][1] for n in order],
            *[outs[n][2] for n in order], *[outs[n][3] for n in order])
```

```python
import functools

import jax
import jax.numpy as jnp
from jax import lax
from jax.experimental import pallas as pl
from jax.experimental.pallas import tpu as pltpu

f32 = jnp.float32
_ACT = jnp.bfloat16

EPS = 1e-6
POOL_WINDOWS = (2, 4, 8, 16)
XATTN_HEADS = 4
ADAM_LR = 0.001
ADAM_B1 = 0.9
ADAM_B2 = 0.999
ADAM_EPS = 1e-08
ADAM_WD = 0.01
ADAM_STEP = 10

_VMEM_LIMIT_BYTES = 56 * 1024 * 1024
_MESH = pl.DeviceIdType.MESH
_ANY = pl.BlockSpec(memory_space=pl.ANY)
_VMEM = pl.BlockSpec(memory_space=pltpu.VMEM)
_HBM = pl.BlockSpec(memory_space=pltpu.HBM)
_SEM = pl.BlockSpec(memory_space=pltpu.SEMAPHORE)
_EFFECT = pltpu.SideEffectType.DATAFLOW_SIDE_EFFECTING

_NN = (((1,), (0,)), ((), ()))
_NT = (((1,), (1,)), ((), ()))
_TN = (((0,), (0,)), ((), ()))


def _params(n_grid):
    return pltpu.CompilerParams(dimension_semantics=("arbitrary",) * n_grid, vmem_limit_bytes=_VMEM_LIMIT_BYTES)


def _sigmoid(v):
    return 1.0 / (1.0 + jnp.exp(-v))


def _dot(a, b, dims):
    return lax.dot_general(a, b, dims, preferred_element_type=f32)


def _mm(name, a, b, *, dims, grid, a_spec, b_spec, o_spec, out_shape, nk, acc_shape=None, res=None, res_spec=None):
    def body(*refs):
        if res is None:
            a_ref, b_ref, o_ref, *scratch = refs
            r_ref = None
        else:
            a_ref, b_ref, r_ref, o_ref, *scratch = refs
        p = _dot(a_ref[...], b_ref[...], dims)

        def finish(v):
            if r_ref is not None:
                v = v + r_ref[...]
            o_ref[...] = v.astype(o_ref.dtype)

        if nk == 1:
            finish(p)
        else:
            acc = scratch[0]
            k = pl.program_id(2)

            @pl.when(k == 0)
            def _():
                acc[...] = p

            @pl.when(k > 0)
            def _():
                acc[...] += p

            @pl.when(k == nk - 1)
            def _():
                finish(acc[...])

    ins = [a, b] + ([] if res is None else [res])
    specs = [a_spec, b_spec] + ([] if res is None else [res_spec])
    return pl.pallas_call(
        body, name=name, grid=grid, in_specs=specs, out_specs=o_spec, out_shape=out_shape,
        scratch_shapes=[pltpu.VMEM(acc_shape, f32)] if nk > 1 else [], compiler_params=_params(3),
    )(*ins)


_NARROW = 2816


def _row_tile(m, width=_NARROW + 1):
    return min(1024 if width <= _NARROW else 512, m)


def _mm_nn(name, a, b, out_dtype, tn, res=None, split_out=False):
    m, k = a.shape
    n = b.shape[1]
    tm = _row_tile(m, max(k, tn))
    if split_out:
        out_shape = jax.ShapeDtypeStruct((n // tn, m, tn), out_dtype)
        o_spec = pl.BlockSpec((None, tm, tn), lambda j, i, kk: (j, i, 0))
    else:
        out_shape = jax.ShapeDtypeStruct((m, n), out_dtype)
        o_spec = pl.BlockSpec((tm, tn), lambda j, i, kk: (i, j))
    return _mm(
        name, a, b, dims=_NN, grid=(n // tn, m // tm, 1), nk=1,
        a_spec=pl.BlockSpec((tm, k), lambda j, i, kk: (i, 0)),
        b_spec=pl.BlockSpec((k, tn), lambda j, i, kk: (0, j)),
        o_spec=o_spec, out_shape=out_shape, res=res,
        res_spec=pl.BlockSpec((tm, tn), lambda j, i, kk: (i, j)),
    )


def _mm_nt(name, a, b, out_dtype):
    n, kc = b.shape
    m = a.shape[0]
    tm = _row_tile(m, max(n, kc))
    return _mm(
        name, a, b, dims=_NT, grid=(m // tm, 1, 1), nk=1,
        a_spec=pl.BlockSpec((tm, kc), lambda i, j, k: (i, 0)),
        b_spec=pl.BlockSpec((n, kc), lambda i, j, k: (0, 0), pipeline_mode=pl.Buffered(1)),
        o_spec=pl.BlockSpec((tm, n), lambda i, j, k: (i, 0)),
        out_shape=jax.ShapeDtypeStruct((m, n), out_dtype),
    )


def _mm_tn_rows(name, a, b, tka, tn):
    m, ka = a.shape
    nb = b.shape[1]
    return _mm(
        name, a, b, dims=_TN, grid=(ka // tka, nb // tn, 1), nk=1,
        a_spec=pl.BlockSpec((m, tka), lambda i, j, k: (0, i)),
        b_spec=pl.BlockSpec((m, tn), lambda i, j, k: (0, j)),
        o_spec=pl.BlockSpec((tka, tn), lambda i, j, k: (i, j)),
        out_shape=jax.ShapeDtypeStruct((ka, nb), _ACT),
    )


def _mm_tn_pieces(name, a, b, cs, tt):
    m, ka = a.shape
    nk = m // tt
    if b.ndim == 3:
        b_spec = pl.BlockSpec((None, tt, cs), lambda i, j, k: (j // 2, k, j % 2))
    else:
        b_spec = pl.BlockSpec((tt, cs), lambda i, j, k: (k, j))
    return _mm(
        name, a, b, dims=_TN, grid=(2, 4, nk), nk=nk, acc_shape=(ka // 2, cs),
        a_spec=pl.BlockSpec((tt, ka // 2), lambda i, j, k: (k, i)), b_spec=b_spec,
        o_spec=pl.BlockSpec((None, ka // 2, cs), lambda i, j, k: (2 * j + i, 0, 0)),
        out_shape=jax.ShapeDtypeStruct((8, ka // 2, cs), _ACT),
    )


def _rms_fwd(name, x, g):
    t, d = x.shape
    tm = _row_tile(t, d)

    def body(x_ref, g_ref, h_ref):
        xv = x_ref[...]
        r = lax.rsqrt(jnp.mean(xv * xv, axis=-1, keepdims=True) + EPS)
        h_ref[...] = (xv * r * g_ref[...]).astype(h_ref.dtype)

    return pl.pallas_call(
        body, name=name, grid=(t // tm,),
        in_specs=[pl.BlockSpec((tm, d), lambda i: (i, 0)), pl.BlockSpec((1, d), lambda i: (0, 0))],
        out_specs=pl.BlockSpec((tm, d), lambda i: (i, 0)), out_shape=jax.ShapeDtypeStruct((t, d), _ACT),
        compiler_params=_params(1),
    )(x, g)


def _fused_rows(name, a, b, product, a_spec, tm, extras, extra_specs, out_shape, out_specs, epilogue):
    ne = len(extras)

    def body(a_ref, b_ref, *refs):
        epilogue(product(a_ref, b_ref), refs[:ne], refs[ne:])

    m = extras[0].shape[0]
    return pl.pallas_call(
        body, name=name, grid=(m // tm,),
        in_specs=[a_spec, pl.BlockSpec(b.shape, lambda i: (0, 0), pipeline_mode=pl.Buffered(1)), *extra_specs],
        out_specs=out_specs, out_shape=out_shape, compiler_params=_params(1),
    )(a, b, *extras)


def _proj_residual_norm(name, a, b, res, g):
    m, k = a.shape
    d = b.shape[1]
    tm = _row_tile(m, max(k, d))

    def epilogue(p, ins, outs):
        xv = p + ins[0][...]
        outs[0][...] = xv
        r = lax.rsqrt(jnp.mean(xv * xv, axis=-1, keepdims=True) + EPS)
        outs[1][...] = (xv * r * ins[1][...]).astype(outs[1].dtype)

    row = pl.BlockSpec((tm, d), lambda i: (i, 0))
    return _fused_rows(
        name, a, b, lambda a_ref, b_ref: _dot(a_ref[...], b_ref[...], _NN), pl.BlockSpec((tm, k), lambda i: (i, 0)), tm,
        [res, g], [row, pl.BlockSpec((1, d), lambda i: (0, 0))],
        [jax.ShapeDtypeStruct((m, d), f32), jax.ShapeDtypeStruct((m, d), _ACT)], [row, row], epilogue)


def _dproj_rms_bwd(name, a, b, x, g, dres, storage_copy=True):
    m, d = x.shape
    if a.ndim == 3:
        nh, _, kh = a.shape
        tm = _row_tile(m, nh * kh)
        a_spec = pl.BlockSpec((nh, tm, kh), lambda i: (0, i, 0))

        def product(a_ref, b_ref):
            p = _dot(a_ref[0], b_ref[:, 0:kh], _NT)
            for h in range(1, nh):
                p = p + _dot(a_ref[h], b_ref[:, h * kh:(h + 1) * kh], _NT)
            return p
    else:
        tm = _row_tile(m, max(a.shape[1], d))
        a_spec = pl.BlockSpec((tm, a.shape[1]), lambda i: (i, 0))

        def product(a_ref, b_ref):
            return _dot(a_ref[...], b_ref[...], _NT)

    def epilogue(dhv, ins, outs):
        x_ref, g_ref, dres_ref = ins
        dg_ref = outs[-1]

        @pl.when(pl.program_id(0) == 0)
        def _():
            dg_ref[...] = jnp.zeros_like(dg_ref)

        xv = x_ref[...]
        r = lax.rsqrt(jnp.mean(xv * xv, axis=-1, keepdims=True) + EPS)
        xn = xv * r
        dxn = dhv * g_ref[...]
        dx = r * (dxn - xn * jnp.mean(dxn * xn, axis=-1, keepdims=True)) + dres_ref[...]
        outs[0][...] = dx
        if storage_copy:
            outs[1][...] = dx.astype(outs[1].dtype)
        dg_ref[...] += jnp.sum(dhv * xn, axis=0, keepdims=True)

    row = pl.BlockSpec((tm, d), lambda i: (i, 0))
    vec = pl.BlockSpec((1, d), lambda i: (0, 0))
    copies = [jax.ShapeDtypeStruct((m, d), _ACT)] if storage_copy else []
    return _fused_rows(
        name, a, b, product, a_spec, tm, [x, g, dres], [row, vec, row],
        [jax.ShapeDtypeStruct((m, d), f32)] + copies + [jax.ShapeDtypeStruct((1, d), f32)],
        [row] * (1 + len(copies)) + [vec], epilogue)


def _proj_loss_bwd(name, a, b, res, g, tgt):
    m, k = a.shape
    d = b.shape[1]
    tm = _row_tile(m, max(k, d))

    def epilogue(p, ins, outs):
        res_ref, g_ref, t_ref = ins
        dx_ref, dxb_ref, dg_ref, loss_ref = outs

        @pl.when(pl.program_id(0) == 0)
        def _():
            dg_ref[...] = jnp.zeros_like(dg_ref)
            loss_ref[...] = jnp.zeros_like(loss_ref)

        xv = p + res_ref[...]
        gv = g_ref[...]
        r = lax.rsqrt(jnp.mean(xv * xv, axis=-1, keepdims=True) + EPS)
        xn = xv * r
        err = xn * gv - t_ref[...]
        loss_ref[...] += 0.5 * jnp.sum(jnp.mean(err * err, axis=-1, keepdims=True), axis=0, keepdims=True)
        dout = err * (1.0 / d)
        dxn = dout * gv
        dx = r * (dxn - xn * jnp.mean(dxn * xn, axis=-1, keepdims=True))
        dx_ref[...] = dx
        dxb_ref[...] = dx.astype(dxb_ref.dtype)
        dg_ref[...] += jnp.sum(dout * xn, axis=0, keepdims=True)

    row = pl.BlockSpec((tm, d), lambda i: (i, 0))
    vec = pl.BlockSpec((1, d), lambda i: (0, 0))
    return _fused_rows(
        name, a, b, lambda a_ref, b_ref: _dot(a_ref[...], b_ref[...], _NN), pl.BlockSpec((tm, k), lambda i: (i, 0)), tm,
        [res, g, tgt], [row, vec, row],
        [jax.ShapeDtypeStruct((m, d), f32), jax.ShapeDtypeStruct((m, d), _ACT), jax.ShapeDtypeStruct((1, d), f32),
         jax.ShapeDtypeStruct((1, 1), f32)],
        [row, row, vec, pl.BlockSpec((1, 1), lambda i: (0, 0))], epilogue)


def _rms_gain_grad(name, x, dh):
    t, d = x.shape
    tm = _row_tile(t)

    def body(x_ref, dh_ref, dg_ref):
        @pl.when(pl.program_id(0) == 0)
        def _():
            dg_ref[...] = jnp.zeros_like(dg_ref)

        xv = x_ref[...]
        r = lax.rsqrt(jnp.mean(xv * xv, axis=-1, keepdims=True) + EPS)
        dg_ref[...] += jnp.sum(dh_ref[...] * (xv * r), axis=0, keepdims=True)

    row = pl.BlockSpec((tm, d), lambda i: (i, 0))
    return pl.pallas_call(
        body, name=name, grid=(t // tm,), in_specs=[row, row], out_specs=pl.BlockSpec((1, d), lambda i: (0, 0)),
        out_shape=jax.ShapeDtypeStruct((1, d), f32), compiler_params=_params(1),
    )(x, dh)


_CONV_ROWS = 256
_CHUNK = 64
_HALO = 32


def _pool_counts(pos, w):
    return jnp.minimum(pos + 1.0, float(w))


def _rows_from(win, start, rows):
    if start % 8 == 0:
        return win[start:start + rows, :]
    n = win.shape[0]
    return pltpu.roll(win, n - start % 8, axis=0)[start - start % 8:start - start % 8 + rows, :]


def _tap_rows(buf, starts, rows):
    for residue in range(8):
        group = [(k, s) for k, s in starts.items() if s % 8 == residue]
        if group:
            lo = min(s for _, s in group) - residue
            hi = max(s for _, s in group) - residue + rows + (8 if residue else 0)
            win = buf[lo:hi, :]
            if residue:
                win = pltpu.roll(win, hi - lo - residue, axis=0)
            for k, s in group:
                yield k, win[s - residue - lo:s - residue - lo + rows, :]


def _mix_fwd(u, cw, cb, lg, lb, pw, ps, seq):
    t, c3 = u.shape
    c = c3 // 3
    kw = 31
    tm = min(_CONV_ROWS, seq)
    tps = seq // tm
    gd = c // len(POOL_WINDOWS)

    def body(u_ref, uh_ref, cw_ref, cb_ref, lg_ref, lb_ref, pw_ref, ps_ref, y_ref, hc_ref, hgbuf, pbuf):
        i = pl.program_id(0)
        keep = jnp.where(i % tps == 0, 0.0, 1.0)
        um = u_ref[...].astype(f32)
        uh = uh_ref[...].astype(f32) * keep
        hgbuf[0:_HALO, :] = uh[:, 0:c] * _sigmoid(uh[:, c:2 * c])
        hgbuf[_HALO:_HALO + tm, :] = um[:, 0:c] * _sigmoid(um[:, c:2 * c])
        pbuf[0:_HALO, :] = uh[:, 2 * c:]
        pbuf[_HALO:_HALO + tm, :] = um[:, 2 * c:]
        for r0 in range(0, tm, _CHUNK):
            acc = jnp.broadcast_to(cb_ref[...], (_CHUNK, c))
            for k, rows in _tap_rows(hgbuf, {k: r0 + _HALO - (kw - 1) + k for k in range(kw)}, _CHUNK):
                acc = acc + cw_ref[k:k + 1, :] * rows
            hc_ref[r0:r0 + _CHUNK, :] = acc
            mu = jnp.mean(acc, axis=-1, keepdims=True)
            xc = acc - mu
            var = jnp.mean(xc * xc, axis=-1, keepdims=True)
            hl = xc * lax.rsqrt(var + EPS) * lg_ref[...] + lb_ref[...]
            y_ref[r0:r0 + _CHUNK, 0:c] = (hl * _sigmoid(hl)).astype(y_ref.dtype)
        pos = ((i % tps) * tm).astype(f32) + lax.broadcasted_iota(jnp.int32, (tm, 1), 0).astype(f32)
        for gi, w in enumerate(POOL_WINDOWS):
            sl = slice(gi * gd, (gi + 1) * gd)
            v = pbuf[_HALO:_HALO + tm, sl]
            s = v
            for j in range(1, w):
                s = s + pbuf[_HALO - j:_HALO - j + tm, sl]
            pooled = s / _pool_counts(pos, w) - v
            mixed = _dot(pooled.astype(_ACT), pw_ref[gi].astype(_ACT), _NN)
            y_ref[:, c + gi * gd:c + (gi + 1) * gd] = (mixed * ps_ref[:, sl]).astype(y_ref.dtype)

    hb = tm // _HALO
    full = lambda shape: pl.BlockSpec(shape, lambda i: (0,) * len(shape))
    return pl.pallas_call(
        body, name="mix_fwd", grid=(t // tm,),
        in_specs=[pl.BlockSpec((tm, c3), lambda i: (i, 0)),
                  pl.BlockSpec((_HALO, c3), lambda i: (jnp.maximum(i * hb - 1, 0), 0)),
                  full((_HALO, c)), full((1, c)), full((1, c)), full((1, c)), full((len(POOL_WINDOWS), gd, gd)), full((1, c))],
        out_specs=[pl.BlockSpec((tm, 2 * c), lambda i: (i, 0)), pl.BlockSpec((tm, c), lambda i: (i, 0))],
        out_shape=[jax.ShapeDtypeStruct((t, 2 * c), _ACT), jax.ShapeDtypeStruct((t, c), f32)],
        scratch_shapes=[pltpu.VMEM((_HALO + tm, c), f32), pltpu.VMEM((_HALO + tm, c), f32)],
        compiler_params=_params(1),
    )(u, u, cw, cb, lg, lb, pw, ps)


def _mix_bwd_norm(hc, dy, lg, lb, seq):
    t, c = hc.shape
    tm = _row_tile(t, c)

    def body(hc_ref, dy_ref, lg_ref, lb_ref, dhc_ref, sums_ref):
        @pl.when(pl.program_id(0) == 0)
        def _():
            sums_ref[...] = jnp.zeros_like(sums_ref)

        hcv = hc_ref[...]
        mu = jnp.mean(hcv, axis=-1, keepdims=True)
        xc = hcv - mu
        rstd = lax.rsqrt(jnp.mean(xc * xc, axis=-1, keepdims=True) + EPS)
        n = xc * rstd
        hl = n * lg_ref[...] + lb_ref[...]
        sg = _sigmoid(hl)
        dhl = dy_ref[...].astype(f32) * (sg * (1.0 + hl * (1.0 - sg)))
        dn = dhl * lg_ref[...]
        dhc = rstd * (dn - jnp.mean(dn, axis=-1, keepdims=True) - n * jnp.mean(dn * n, axis=-1, keepdims=True))
        dhc_ref[...] = dhc
        sums_ref[0:1, :] += jnp.sum(dhl * n, axis=0, keepdims=True)
        sums_ref[1:2, :] += jnp.sum(dhl, axis=0, keepdims=True)
        sums_ref[2:3, :] += jnp.sum(dhc, axis=0, keepdims=True)

    row = pl.BlockSpec((tm, c), lambda i: (i, 0))
    vec = pl.BlockSpec((1, c), lambda i: (0, 0))
    return pl.pallas_call(
        body, name="mix_bwd_norm", grid=(t // tm,), in_specs=[row, row, vec, vec],
        out_specs=[row, pl.BlockSpec((8, c), lambda i: (0, 0))],
        out_shape=[jax.ShapeDtypeStruct((t, c), f32), jax.ShapeDtypeStruct((8, c), f32)],
        compiler_params=_params(1),
    )(hc, dy, lg, lb)


def _mix_bwd_taps(u, dhc, dy, cw, pw, ps, seq):
    t, c3 = u.shape
    c = c3 // 3
    kw = 31
    tm = min(_CONV_ROWS, seq)
    tps = seq // tm
    ng = len(POOL_WINDOWS)
    gd = c // ng
    nh = 16

    def body(u_ref, uh_ref, dhc_ref, dhcn_ref, dy_ref, dyn_ref, cw_ref, pw_ref, ps_ref,
             du_ref, dcw_ref, dps_ref, dpw_ref, hgbuf, dcbuf, pbuf, dpbuf):
        i = pl.program_id(0)
        keep_prev = jnp.where(i % tps == 0, 0.0, 1.0)
        keep_next = jnp.where(i % tps == tps - 1, 0.0, 1.0)

        @pl.when(i == 0)
        def _():
            dcw_ref[...] = jnp.zeros_like(dcw_ref)
            dps_ref[...] = jnp.zeros_like(dps_ref)
            dpw_ref[...] = jnp.zeros_like(dpw_ref)

        uh = uh_ref[...].astype(f32) * keep_prev
        hgbuf[0:_HALO, :] = uh[:, 0:c] * _sigmoid(uh[:, c:2 * c])
        pbuf[0:_HALO, :] = uh[:, 2 * c:]
        um = u_ref[...].astype(f32)
        hgbuf[_HALO:_HALO + tm, :] = um[:, 0:c] * _sigmoid(um[:, c:2 * c])
        pbuf[_HALO:_HALO + tm, :] = um[:, 2 * c:]
        dcbuf[0:tm, :] = dhc_ref[...]
        dcbuf[tm:tm + _HALO, :] = dhcn_ref[...] * keep_next
        tap_sums = [None] * kw
        for r0 in range(0, tm, _CHUNK):
            dh = dcbuf[r0:r0 + _CHUNK, :]
            acc = jnp.zeros((_CHUNK, c), f32)
            for k, rows in _tap_rows(hgbuf, {k: r0 + _HALO - (kw - 1) + k for k in range(kw)}, _CHUNK):
                part = (dh * rows).reshape(_CHUNK // 8, 8, c).sum(axis=0)
                tap_sums[k] = part if tap_sums[k] is None else tap_sums[k] + part
            for k, rows in _tap_rows(dcbuf, {k: r0 + (kw - 1) - k for k in range(kw)}, _CHUNK):
                acc = acc + cw_ref[k:k + 1, :] * rows
            val = u_ref[r0:r0 + _CHUNK, 0:c].astype(f32)
            sg = _sigmoid(u_ref[r0:r0 + _CHUNK, c:2 * c].astype(f32))
            du_ref[r0:r0 + _CHUNK, 0:c] = (acc * sg).astype(du_ref.dtype)
            du_ref[r0:r0 + _CHUNK, c:2 * c] = (acc * val * sg * (1.0 - sg)).astype(du_ref.dtype)
        for k in range(kw):
            dcw_ref[k:k + 1, :] += jnp.sum(tap_sums[k], axis=0, keepdims=True)
        base = ((i % tps) * tm).astype(f32)
        pos = base + lax.broadcasted_iota(jnp.int32, (tm, 1), 0).astype(f32)
        pos_next = base + float(tm) + lax.broadcasted_iota(jnp.int32, (nh, 1), 0).astype(f32)
        for gi, w in enumerate(POOL_WINDOWS):
            sl = slice(gi * gd, (gi + 1) * gd)
            v = pbuf[_HALO:_HALO + tm, sl]
            s = v
            for j in range(1, w):
                s = s + pbuf[_HALO - j:_HALO - j + tm, sl]
            cnt = _pool_counts(pos, w)
            pooled = (s / cnt - v).astype(_ACT)
            pwg = pw_ref[gi].astype(_ACT)
            mixed = _dot(pooled, pwg, _NN)
            dyp = dy_ref[:, sl].astype(f32)
            dps_ref[0:1, sl] += jnp.sum(dyp * mixed, axis=0, keepdims=True)
            dmix = (dyp * ps_ref[:, sl]).astype(_ACT)
            dpw_ref[gi] += _dot(pooled, dmix, _TN)
            dmix_next = (dyn_ref[:, sl].astype(f32) * ps_ref[:, sl] * keep_next).astype(_ACT)
            dpool = _dot(dmix, pwg, _NT)
            dpbuf[0:tm, sl] = dpool / cnt
            dpbuf[tm:tm + nh, sl] = _dot(dmix_next, pwg, _NT) / _pool_counts(pos_next, w)
            acc = -dpool
            for j in range(w):
                acc = acc + dpbuf[j:j + tm, sl]
            du_ref[:, 2 * c + gi * gd:2 * c + (gi + 1) * gd] = acc.astype(du_ref.dtype)

    hb = tm // _HALO
    n_halo = t // _HALO
    n_nh = t // nh
    full = lambda shape: pl.BlockSpec(shape, lambda i: (0,) * len(shape))
    return pl.pallas_call(
        body, name="mix_bwd_taps", grid=(t // tm,),
        in_specs=[pl.BlockSpec((tm, c3), lambda i: (i, 0)),
                  pl.BlockSpec((_HALO, c3), lambda i: (jnp.maximum(i * hb - 1, 0), 0)),
                  pl.BlockSpec((tm, c), lambda i: (i, 0)),
                  pl.BlockSpec((_HALO, c), lambda i: (jnp.minimum((i + 1) * hb, n_halo - 1), 0)),
                  pl.BlockSpec((tm, c), lambda i: (i, 1)),
                  pl.BlockSpec((nh, c), lambda i: (jnp.minimum((i + 1) * (tm // nh), n_nh - 1), 1)),
                  full((_HALO, c)), full((ng, gd, gd)), full((1, c))],
        out_specs=[pl.BlockSpec((tm, c3), lambda i: (i, 0)), full((_HALO, c)), full((8, c)), full((ng, gd, gd))],
        out_shape=[jax.ShapeDtypeStruct((t, c3), _ACT), jax.ShapeDtypeStruct((_HALO, c), f32),
                   jax.ShapeDtypeStruct((8, c), f32), jax.ShapeDtypeStruct((ng, gd, gd), f32)],
        scratch_shapes=[pltpu.VMEM((_HALO + tm, c), f32), pltpu.VMEM((tm + _HALO, c), f32),
                        pltpu.VMEM((_HALO + tm, c), f32), pltpu.VMEM((tm + nh, c), f32)],
        compiler_params=_params(1),
    )(u, u, dhc, dhc, dy, dy, cw, pw, ps)


def _attn_fwd(q, kv, n_seq, seq, n_mem):
    t, d = q.shape
    dh = d // XATTN_HEADS
    tq = min(512, seq)
    nq = seq // tq
    scale = dh ** -0.5

    def body(q_ref, kv_ref, o_ref):
        for h in range(XATTN_HEADS):
            cols = slice(h * dh, (h + 1) * dh)
            s = _dot(q_ref[:, cols], kv_ref[:, cols], _NT) * scale
            e = jnp.exp(s - jnp.max(s, axis=-1, keepdims=True))
            p = e / jnp.sum(e, axis=-1, keepdims=True)
            o_ref[:, cols] = _dot(p.astype(_ACT), kv_ref[:, d + h * dh:d + (h + 1) * dh], _NN).astype(o_ref.dtype)

    qs = pl.BlockSpec((tq, d), lambda b, i: (b * nq + i, 0))
    return pl.pallas_call(
        body, name="attn_fwd", grid=(n_seq, nq), in_specs=[qs, pl.BlockSpec((n_mem, 2 * d), lambda b, i: (b, 0))],
        out_specs=qs, out_shape=jax.ShapeDtypeStruct((t, d), _ACT), compiler_params=_params(2),
    )(q, kv)


def _attn_bwd(q, kv, do, n_seq, seq, n_mem):
    t, d = q.shape
    dh = d // XATTN_HEADS
    tq = min(512, seq)
    nq = seq // tq
    scale = dh ** -0.5

    def body(q_ref, kv_ref, do_ref, dq_ref, dkv_ref, acc):
        i = pl.program_id(1)

        @pl.when(i == 0)
        def _():
            acc[...] = jnp.zeros_like(acc)

        for h in range(XATTN_HEADS):
            cols = slice(h * dh, (h + 1) * dh)
            vcols = slice(d + h * dh, d + (h + 1) * dh)
            qv = q_ref[:, cols]
            kh = kv_ref[:, cols]
            dov = do_ref[:, cols]
            s = _dot(qv, kh, _NT) * scale
            e = jnp.exp(s - jnp.max(s, axis=-1, keepdims=True))
            p = e / jnp.sum(e, axis=-1, keepdims=True)
            dp = _dot(dov, kv_ref[:, vcols], _NT)
            ds = (p * (dp - jnp.sum(dp * p, axis=-1, keepdims=True)) * scale).astype(_ACT)
            dq_ref[:, cols] = _dot(ds, kh, _NN).astype(dq_ref.dtype)
            acc[:, cols] += _dot(ds, qv, _TN)
            acc[:, vcols] += _dot(p.astype(_ACT), dov, _TN)

        @pl.when(i == nq - 1)
        def _():
            dkv_ref[...] = acc[...].astype(dkv_ref.dtype)

    qs = pl.BlockSpec((tq, d), lambda b, i: (b * nq + i, 0))
    ms = pl.BlockSpec((n_mem, 2 * d), lambda b, i: (b, 0))
    return pl.pallas_call(
        body, name="attn_bwd", grid=(n_seq, nq), in_specs=[qs, ms, qs], out_specs=[qs, ms],
        out_shape=[jax.ShapeDtypeStruct((t, d), _ACT), jax.ShapeDtypeStruct((n_seq * n_mem, 2 * d), _ACT)],
        scratch_shapes=[pltpu.VMEM((n_mem, 2 * d), f32)], compiler_params=_params(2),
    )(q, kv, do)


_FFN_ROWS = 2048
_FFN_COLS = 256
_FFN_HALO = 16


def _window(buf, g, start, rows):
    return buf[g, pl.ds(start, rows + 8), :]


def _taps3(win, rows):
    return [_rows_from(win, 6 + k, rows) for k in range(3)]


def _conv3(b_ref, w_ref, taps):
    acc = b_ref[...] + w_ref[0:1, :] * taps[0]
    for k in (1, 2):
        acc = acc + w_ref[k:k + 1, :] * taps[k]
    return acc


def _ffn_gate_fwd(up, fw, fb, seq):
    _, t, f = up.shape
    tm = min(_FFN_ROWS, seq)
    tps = seq // tm
    tc = _FFN_COLS
    nc = f // tc
    hl = _FFN_HALO

    def body(up_ref, uph_ref, wg_ref, wv_ref, bg_ref, bv_ref, a_ref):
        i = pl.program_id(1)
        before = uph_ref[...]
        before = jnp.where(i % tps == 0, jnp.zeros_like(before), before)

        def chunk(r0, wins):
            conv = []
            for g, (w_ref, b_ref) in enumerate(((wg_ref, bg_ref), (wv_ref, bv_ref))):
                conv.append(_conv3(b_ref, w_ref, _taps3(wins[g].astype(f32)[hl - 8:, :], _CHUNK)))
            gate, val = conv
            a_ref[pl.ds(r0, _CHUNK), :] = (gate * _sigmoid(gate) * val).astype(a_ref.dtype)

        chunk(0, [jnp.concatenate([before[g], up_ref[g, 0:_CHUNK, :]], axis=0) for g in range(2)])

        def later(ci, carry):
            r0 = pl.multiple_of(ci * _CHUNK, _CHUNK)
            chunk(r0, [up_ref[g, pl.ds(r0 - hl, _CHUNK + hl), :] for g in range(2)])
            return carry

        lax.fori_loop(1, tm // _CHUNK, later, 0)

    hb = tm // hl
    return pl.pallas_call(
        body, name="ffn_gate_fwd", grid=(nc, t // tm),
        in_specs=[pl.BlockSpec((2, tm, tc), lambda j, i: (0, i, j)),
                  pl.BlockSpec((2, hl, tc), lambda j, i: (0, jnp.maximum(i * hb - 1, 0), j)),
                  pl.BlockSpec((8, tc), lambda j, i: (0, j)), pl.BlockSpec((8, tc), lambda j, i: (0, nc + j)),
                  pl.BlockSpec((1, tc), lambda j, i: (0, j)), pl.BlockSpec((1, tc), lambda j, i: (0, nc + j))],
        out_specs=pl.BlockSpec((tm, tc), lambda j, i: (i, j)),
        out_shape=jax.ShapeDtypeStruct((t, f), _ACT), compiler_params=_params(2),
    )(up, up, fw, fw, fb, fb)


def _ffn_gate_bwd(up, da, fw, fb, seq):
    _, t, f = up.shape
    tm = min(_FFN_ROWS, seq)
    tps = seq // tm
    tc = _FFN_COLS
    nc = f // tc
    hl = _FFN_HALO

    def body(up_ref, uph_ref, upn_ref, da_ref, dan_ref, wg_ref, wv_ref, bg_ref, bv_ref,
             dup_ref, sg_ref, sv_ref, dbuf, sums):
        i = pl.program_id(1)
        at_end = i % tps == tps - 1

        @pl.when(i == 0)
        def _():
            sg_ref[...] = jnp.zeros_like(sg_ref)
            sv_ref[...] = jnp.zeros_like(sv_ref)

        sums[...] = jnp.zeros_like(sums)
        before = uph_ref[...]
        before = jnp.where(i % tps == 0, jnp.zeros_like(before), before)
        after = upn_ref[...]
        after = jnp.where(at_end, jnp.zeros_like(after), after)
        w_refs = (wg_ref, wv_ref)
        b_refs = (bg_ref, bv_ref)

        def grads(r0, rows, wins, dav, count):
            taps = [_taps3(wins[g].astype(f32)[hl - 8:, :], rows) for g in range(2)]
            gate, val = [_conv3(b_refs[g], w_refs[g], taps[g]) for g in range(2)]
            sg = _sigmoid(gate)
            douts = (dav * val * (sg * (1.0 + gate * (1.0 - sg))), dav * (gate * sg))
            for g in range(2):
                dbuf[g, pl.ds(r0, rows), :] = douts[g]
                if count:
                    sums[g, 0] += douts[g].reshape(rows // 8, 8, tc).sum(axis=0)
                    for k in range(3):
                        sums[g, 1 + k] += (douts[g] * taps[g][k]).reshape(rows // 8, 8, tc).sum(axis=0)

        grads(0, _CHUNK, [jnp.concatenate([before[g], up_ref[g, 0:_CHUNK, :]], axis=0) for g in range(2)],
              da_ref[0:_CHUNK, :].astype(f32), True)

        def first(ci, carry):
            r0 = pl.multiple_of(ci * _CHUNK, _CHUNK)
            grads(r0, _CHUNK, [up_ref[g, pl.ds(r0 - hl, _CHUNK + hl), :] for g in range(2)],
                  da_ref[pl.ds(r0, _CHUNK), :].astype(f32), True)
            return carry

        lax.fori_loop(1, tm // _CHUNK, first, 0)
        da_after = dan_ref[...].astype(f32)
        grads(tm, hl, [jnp.concatenate([up_ref[g, tm - hl:tm, :], after[g]], axis=0) for g in range(2)],
              jnp.where(at_end, jnp.zeros_like(da_after), da_after), False)

        def second(ci, carry):
            r0 = pl.multiple_of(ci * _CHUNK, _CHUNK)
            for g in range(2):
                win = _window(dbuf, g, r0, _CHUNK)
                acc = jnp.zeros((_CHUNK, tc), f32)
                for k in range(3):
                    acc = acc + w_refs[g][k:k + 1, :] * _rows_from(win, 2 - k, _CHUNK)
                dup_ref[g, pl.ds(r0, _CHUNK), :] = acc.astype(dup_ref.dtype)
            return carry

        lax.fori_loop(0, tm // _CHUNK, second, 0)
        for g, s_ref in enumerate((sg_ref, sv_ref)):
            for r in range(4):
                s_ref[r:r + 1, :] += jnp.sum(sums[g, r], axis=0, keepdims=True)

    hb = tm // hl
    n_halo = t // hl
    return pl.pallas_call(
        body, name="ffn_gate_bwd", grid=(nc, t // tm),
        in_specs=[pl.BlockSpec((2, tm, tc), lambda j, i: (0, i, j)),
                  pl.BlockSpec((2, hl, tc), lambda j, i: (0, jnp.maximum(i * hb - 1, 0), j)),
                  pl.BlockSpec((2, hl, tc), lambda j, i: (0, jnp.minimum((i + 1) * hb, n_halo - 1), j)),
                  pl.BlockSpec((tm, tc), lambda j, i: (i, j)),
                  pl.BlockSpec((hl, tc), lambda j, i: (jnp.minimum((i + 1) * hb, n_halo - 1), j)),
                  pl.BlockSpec((8, tc), lambda j, i: (0, j)), pl.BlockSpec((8, tc), lambda j, i: (0, nc + j)),
                  pl.BlockSpec((1, tc), lambda j, i: (0, j)), pl.BlockSpec((1, tc), lambda j, i: (0, nc + j))],
        out_specs=[pl.BlockSpec((2, tm, tc), lambda j, i: (0, i, j)),
                   pl.BlockSpec((8, tc), lambda j, i: (0, j)), pl.BlockSpec((8, tc), lambda j, i: (0, j))],
        out_shape=[jax.ShapeDtypeStruct((2, t, f), _ACT), jax.ShapeDtypeStruct((8, f), f32), jax.ShapeDtypeStruct((8, f), f32)],
        scratch_shapes=[pltpu.VMEM((2, tm + hl, tc), f32), pltpu.VMEM((2, 4, 8, tc), f32)],
        compiler_params=_params(2),
    )(up, up, up, da, da, fw, fw, fb, fb)


def _adamw_math(w, g, m, v):
    m = ADAM_B1 * m + (1.0 - ADAM_B1) * g
    v = ADAM_B2 * v + (1.0 - ADAM_B2) * (g * g)
    m_hat = m / (1.0 - ADAM_B1 ** ADAM_STEP)
    v_hat = v / (1.0 - ADAM_B2 ** ADAM_STEP)
    delta = -ADAM_LR * (m_hat / (jnp.sqrt(v_hat) + ADAM_EPS) + ADAM_WD * w)
    return delta, m, v


def _adamw_shards(quads):
    n = len(quads)
    steps = 8

    def body(*refs):
        for p in range(n):
            w_ref, g_ref, m_ref, v_ref = refs[4 * p:4 * p + 4]
            go_ref, d_ref, mo_ref, vo_ref = refs[4 * n + 4 * p:4 * n + 4 * p + 4]
            gv = g_ref[...]
            d, mn, vn = _adamw_math(w_ref[...], gv, m_ref[...], v_ref[...])
            go_ref[...] = gv
            d_ref[...] = d
            mo_ref[...] = mn
            vo_ref[...] = vn

    in_specs, out_specs, out_shape = [], [], []
    for w, _, _, _ in quads:
        _, r, c = w.shape
        s3 = pl.BlockSpec((None, r // steps, c), lambda i: (0, i, 0))
        in_specs += [s3, pl.BlockSpec((r // steps, c), lambda i: (i, 0)), s3, s3]
        out_specs += [s3] * 4
        out_shape += [jax.ShapeDtypeStruct(w.shape, f32)] * 4
    outs = pl.pallas_call(
        body, name="adamw_shards", grid=(steps,), in_specs=in_specs, out_specs=out_specs, out_shape=out_shape,
        compiler_params=_params(1),
    )(*[a for q in quads for a in q])
    return [tuple(outs[4 * p:4 * p + 4]) for p in range(n)]


def _adamw_small(quads):
    n = len(quads)

    def body(*refs):
        ins, outs = refs[:4 * n], refs[4 * n:]
        for p in range(n):
            w_ref, g_ref, m_ref, v_ref = ins[4 * p:4 * p + 4]
            d, mn, vn = _adamw_math(w_ref[...], g_ref[...], m_ref[...], v_ref[...])
            outs[3 * p][...] = d
            outs[3 * p + 1][...] = mn
            outs[3 * p + 2][...] = vn

    flat = [a for q in quads for a in q]
    shapes = [jax.ShapeDtypeStruct(q[0].shape, f32) for q in quads for _ in range(3)]
    outs = pl.pallas_call(
        body, name="adamw_small", in_specs=[_VMEM] * (4 * n), out_specs=[_VMEM] * (3 * n), out_shape=shapes,
        compiler_params=pltpu.CompilerParams(vmem_limit_bytes=_VMEM_LIMIT_BYTES),
    )(*flat)
    return [tuple(outs[3 * p:3 * p + 3]) for p in range(n)]


def _sum_partials(name, place, grads, got):
    nw = len(grads)
    steps = 2

    def body(place_ref, *refs):
        for w in range(nw):
            own_ref, got_ref, f_ref = refs[w], refs[nw + w], refs[2 * nw + w]
            s = own_ref[...].astype(f32)
            for k in range(got[w].shape[0]):
                s = s + got_ref[k].astype(f32)
            f_ref[...] = s

    own_specs, got_specs, out_specs, out_shape = [], [], [], []
    for g, l in zip(grads, got):
        _, r, c = g.shape
        tr = r // steps
        own_specs.append(pl.BlockSpec((None, tr, c), lambda i, p: (2 * p[0] + p[1], i, 0)))
        got_specs.append(pl.BlockSpec((l.shape[0], tr, c), lambda i, p: (0, i, 0)))
        out_specs.append(pl.BlockSpec((None, tr, c), lambda i, p: (p[1], i, 0)))
        out_shape.append(jax.ShapeDtypeStruct((2, r, c), f32))
    grid_spec = pltpu.PrefetchScalarGridSpec(num_scalar_prefetch=1, grid=(steps,), in_specs=own_specs + got_specs, out_specs=out_specs)
    return pl.pallas_call(body, name=name, grid_spec=grid_spec, out_shape=out_shape,
                          compiler_params=_params(1))(place, *grads, *got)


def _place():
    return lax.axis_index("x"), lax.axis_index("y"), lax.axis_index("c")


def _other_chips(x, y):
    return [(1 - x, y), (x, 1 - y), (1 - x, 1 - y)]


def _remote(src, dst, send_sem, recv_sem, to):
    return pltpu.make_async_remote_copy(src_ref=src, dst_ref=dst, send_sem=send_sem, recv_sem=recv_sem,
                                        device_id=to, device_id_type=_MESH)


def _place_shards(place, shards, col_sharded):
    n = len(shards)
    steps = 4

    def body(place_ref, *refs):
        for src, dst in zip(refs[:n], refs[n:]):
            dst[...] = src[...].astype(dst.dtype)

    in_specs, out_specs, out_shape = [], [], []
    for w, col in zip(shards, col_sharded):
        r, cs = w.shape
        tr = r // steps
        in_specs.append(pl.BlockSpec((tr, cs), lambda i, p: (i, 0)))
        if col:
            out_specs.append(pl.BlockSpec((tr, cs), lambda i, p: (i, p[0])))
            out_shape.append(jax.ShapeDtypeStruct((r, 4 * cs), _ACT))
        else:
            out_specs.append(pl.BlockSpec((tr, cs), lambda i, p: (p[0] * steps + i, 0)))
            out_shape.append(jax.ShapeDtypeStruct((4 * r, cs), _ACT))
    grid_spec = pltpu.PrefetchScalarGridSpec(num_scalar_prefetch=1, grid=(steps,), in_specs=in_specs, out_specs=out_specs)
    return pl.pallas_call(body, name="place_shards", grid_spec=grid_spec, out_shape=out_shape,
                          compiler_params=_params(1))(place, *shards)


def _shard_of(ref, col_sharded, s):
    rows, cols = ref.shape
    if col_sharded:
        return ref.at[:, pl.ds(s * (cols // 4), cols // 4)]
    return ref.at[pl.ds(s * (rows // 4), rows // 4), :]


def _part_of(ref, col_sharded, whole, s, h):
    if whole:
        return _shard_of(ref, col_sharded, s)
    rows, cols = ref.shape
    if col_sharded:
        return ref.at[pl.ds(h * (rows // 2), rows // 2), pl.ds(s * (cols // 4), cols // 4)]
    return ref.at[pl.ds((2 * s + h) * (rows // 8), rows // 8), :]


def _allgather_start(bufs, col_sharded, whole, groups):
    n = len(bufs)
    ng = len(groups)

    def body(*refs):
        out = refs[n:2 * n]
        sems = refs[2 * n:]
        x, y, c = _place()
        for g, members in enumerate(groups):
            for i, w in enumerate(members):
                mine = _part_of(out[w], col_sharded[w], whole[w], 2 * x + y, c)
                for j, chip in enumerate(_other_chips(x, y)):
                    _remote(mine, mine, sems[2 * g].at[3 * i + j], sems[2 * g + 1].at[3 * i + j], (*chip, c)).start()

    sem_shapes = [pltpu.SemaphoreType.DMA((3 * len(m),)) for m in groups for _ in range(2)]
    outs = pl.pallas_call(
        body, name="allgather_start", in_specs=[_HBM] * n, out_specs=[_HBM] * n + [_SEM] * (2 * ng),
        out_shape=[pltpu.HBM(b.shape, b.dtype) for b in bufs] + sem_shapes,
        input_output_aliases={i: i for i in range(n)},
        compiler_params=pltpu.CompilerParams(has_side_effects=_EFFECT),
    )(*[pltpu.with_memory_space_constraint(b, pltpu.HBM) for b in bufs])
    return list(outs[:n]), [(outs[n + 2 * g], outs[n + 2 * g + 1]) for g in range(ng)]


def _allgather_relay(name, bufs, col_sharded, whole, sems, after):
    n = len(bufs)

    def body(*refs):
        buf = refs[:n]
        send, recv = refs[n], refs[n + 1]
        out = refs[n + 3:2 * n + 3]
        to_sibling, from_sibling = refs[2 * n + 3:]
        x, y, c = _place()
        for i in range(n):
            mine = _part_of(buf[i], col_sharded[i], whole[i], 2 * x + y, c)
            for j, chip in enumerate(_other_chips(x, y)):
                landed = _part_of(buf[i], col_sharded[i], whole[i], 2 * chip[0] + chip[1], c)
                cp = _remote(mine, landed, send.at[3 * i + j], recv.at[3 * i + j], (*chip, c))
                cp.wait_send()
                cp.wait_recv()
        for i in range(n):
            if not whole[i]:
                for j, chip in enumerate(_other_chips(x, y)):
                    landed = _part_of(out[i], col_sharded[i], False, 2 * chip[0] + chip[1], c)
                    _remote(landed, landed, to_sibling.at[3 * i + j], from_sibling.at[3 * i + j], (x, y, 1 - c)).start()

    outs = pl.pallas_call(
        body, name=name, in_specs=[_HBM] * n + [_SEM, _SEM, _ANY], out_specs=[_HBM] * n + [_SEM, _SEM],
        out_shape=[pltpu.HBM(b.shape, b.dtype) for b in bufs] + [pltpu.SemaphoreType.DMA((3 * n,))] * 2,
        input_output_aliases={i: i for i in range(n)},
        compiler_params=pltpu.CompilerParams(has_side_effects=_EFFECT),
    )(*bufs, *sems, after)
    return list(outs[:n]), (outs[n], outs[n + 1])


def _allgather_wait(name, bufs, col_sharded, whole, sems, after):
    n = len(bufs)

    def body(*refs):
        buf = refs[:n]
        to_sibling, from_sibling = refs[n], refs[n + 1]
        x, y, c = _place()
        for i in range(n):
            if not whole[i]:
                for j, chip in enumerate(_other_chips(x, y)):
                    sent = _part_of(buf[i], col_sharded[i], False, 2 * chip[0] + chip[1], c)
                    landed = _part_of(buf[i], col_sharded[i], False, 2 * chip[0] + chip[1], 1 - c)
                    cp = _remote(sent, landed, to_sibling.at[3 * i + j], from_sibling.at[3 * i + j], (x, y, 1 - c))
                    cp.wait_send()
                    cp.wait_recv()

    return pl.pallas_call(
        body, name=name, in_specs=[_HBM] * n + [_SEM, _SEM, _ANY], out_specs=[_HBM] * n,
        out_shape=[pltpu.HBM(b.shape, b.dtype) for b in bufs],
        input_output_aliases={i: i for i in range(n)},
        compiler_params=pltpu.CompilerParams(has_side_effects=_EFFECT),
    )(*bufs, *sems, after)


def _other_devices(x, y, c):
    flips = [(bx, by, bc) for bx in (0, 1) for by in (0, 1) for bc in (0, 1)][1:]
    return [(1 - x if bx else x, 1 - y if by else y, 1 - c if bc else c) for bx, by, bc in flips]


def _grad_exchange_start(name, grads):
    nw = len(grads)
    lands = [lax.empty((7,) + g.shape[1:], g.dtype) for g in grads]

    def body(*refs):
        src = refs[2 * nw:3 * nw]
        got = refs[3 * nw:4 * nw]
        send, recv, token = refs[4 * nw:]
        x, y, c = _place()
        for w in range(nw):
            for k, (px, py, pc) in enumerate(_other_devices(x, y, c)):
                _remote(src[w].at[4 * px + 2 * py + pc], got[w].at[k], send.at[7 * w + k], recv.at[7 * w + k], (px, py, pc)).start()
        token[...] = jnp.zeros_like(token)

    outs = pl.pallas_call(
        body, name=name, in_specs=[_HBM] * (2 * nw), out_specs=[_HBM] * (2 * nw) + [_SEM, _SEM, _VMEM],
        out_shape=[pltpu.HBM(a.shape, a.dtype) for a in list(grads) + lands]
        + [pltpu.SemaphoreType.DMA((7 * nw,)), pltpu.SemaphoreType.DMA((7 * nw,)), jax.ShapeDtypeStruct((8, 128), f32)],
        input_output_aliases={i: i for i in range(2 * nw)},
        compiler_params=pltpu.CompilerParams(has_side_effects=_EFFECT),
    )(*[pltpu.with_memory_space_constraint(a, pltpu.HBM) for a in list(grads) + lands])
    return list(outs[:nw]), list(outs[nw:2 * nw]), (outs[2 * nw], outs[2 * nw + 1]), outs[2 * nw + 2]


def _grad_exchange_wait(name, grads, got, sems, after):
    nw = len(grads)

    def body(*refs):
        src = refs[:nw]
        land = refs[nw:2 * nw]
        send, recv = refs[2 * nw], refs[2 * nw + 1]
        x, y, c = _place()
        for w in range(nw):
            for k, (px, py, pc) in enumerate(_other_devices(x, y, c)):
                cp = _remote(src[w].at[4 * px + 2 * py + pc], land[w].at[k], send.at[7 * w + k], recv.at[7 * w + k], (px, py, pc))
                cp.wait_send()
                cp.wait_recv()

    outs = pl.pallas_call(
        body, name=name, in_specs=[_HBM] * (2 * nw) + [_SEM, _SEM, _ANY], out_specs=[_HBM] * (2 * nw),
        out_shape=[pltpu.HBM(a.shape, a.dtype) for a in list(grads) + list(got)],
        input_output_aliases={i: i for i in range(2 * nw)},
        compiler_params=pltpu.CompilerParams(has_side_effects=_EFFECT),
    )(*grads, *got, *sems, after)
    return list(outs[:nw]), list(outs[nw:])


def _swap_halves_start(finals):
    nw = len(finals)

    def body(*refs):
        buf = refs[nw:2 * nw]
        send, recv = refs[2 * nw:]
        x, y, c = _place()
        for w in range(nw):
            _remote(buf[w].at[c], buf[w].at[c], send.at[w], recv.at[w], (x, y, 1 - c)).start()

    outs = pl.pallas_call(
        body, name="rs_swap_start", in_specs=[_HBM] * nw, out_specs=[_HBM] * nw + [_SEM, _SEM],
        out_shape=[pltpu.HBM(g.shape, g.dtype) for g in finals] + [pltpu.SemaphoreType.DMA((nw,))] * 2,
        input_output_aliases={i: i for i in range(nw)},
        compiler_params=pltpu.CompilerParams(has_side_effects=_EFFECT),
    )(*[pltpu.with_memory_space_constraint(g, pltpu.HBM) for g in finals])
    return list(outs[:nw]), (outs[nw], outs[nw + 1])


def _swap_halves_wait(bufs, sems, after):
    nw = len(bufs)

    def body(*refs):
        buf = refs[:nw]
        send, recv = refs[nw], refs[nw + 1]
        x, y, c = _place()
        for w in range(nw):
            cp = _remote(buf[w].at[c], buf[w].at[1 - c], send.at[w], recv.at[w], (x, y, 1 - c))
            cp.wait_send()
            cp.wait_recv()

    return pl.pallas_call(
        body, name="rs_swap_wait", in_specs=[_HBM] * nw + [_SEM, _SEM, _ANY], out_specs=[_HBM] * nw,
        out_shape=[pltpu.HBM(g.shape, g.dtype) for g in bufs],
        input_output_aliases={i: i for i in range(nw)},
        compiler_params=pltpu.CompilerParams(has_side_effects=_EFFECT),
    )(*bufs, *sems, after)


def _half_slices(shape, h):
    rows, cols = shape
    if cols % 256 == 0:
        return (slice(None), slice(h * (cols // 2), (h + 1) * (cols // 2)))
    return (slice(h * (rows // 2), (h + 1) * (rows // 2)), slice(None))


def _allreduce_small(parts):
    n = len(parts)

    def body(*refs):
        src = refs[:n]
        out = refs[n:2 * n]
        sib = refs[2 * n:3 * n]
        chip_sum = refs[3 * n:4 * n]
        slots = refs[4 * n:5 * n]
        pair_send, pair_recv, ici_send, ici_recv, swap_send, swap_recv = refs[5 * n:]
        x, y, c = _place()
        me_chip = 2 * x + y
        chips = _other_chips(x, y)
        pairs = [_remote(src[a], sib[a], pair_send.at[a], pair_recv.at[a], (x, y, 1 - c)) for a in range(n)]
        for rc in pairs:
            rc.start()
        for a in range(n):
            pairs[a].wait_recv()
            chip_sum[a][...] = src[a][...] + sib[a][...]
        for h in (0, 1):
            @pl.when(c == h)
            def _():
                sends = []
                for a in range(n):
                    idx = _half_slices(parts[a].shape, h)
                    for j, chip in enumerate(chips):
                        rc = _remote(chip_sum[a].at[idx], slots[a].at[me_chip].at[idx], ici_send.at[3 * a + j], ici_recv.at[3 * a + j], (*chip, h))
                        rc.start()
                        sends.append(rc)
                    slots[a][(me_chip,) + idx] = chip_sum[a][idx]
                for a in range(n):
                    idx = _half_slices(parts[a].shape, h)
                    for j, chip in enumerate(chips):
                        landed = slots[a].at[2 * chip[0] + chip[1]].at[idx]
                        _remote(landed, landed, ici_send.at[3 * a + j], ici_recv.at[3 * a + j], (x, y, c)).wait_recv()
                    total = slots[a][(0,) + idx]
                    for s in range(1, 4):
                        total = total + slots[a][(s,) + idx]
                    out[a][idx] = total
                    rc = _remote(out[a].at[idx], out[a].at[idx], swap_send.at[a], swap_recv.at[a], (x, y, 1 - h))
                    rc.start()
                    sends.append(rc)
                for a in range(n):
                    other = out[a].at[_half_slices(parts[a].shape, 1 - h)]
                    _remote(other, other, swap_send.at[a], swap_recv.at[a], (x, y, c)).wait_recv()
                for rc in sends:
                    rc.wait_send()
        for rc in pairs:
            rc.wait_send()

    return pl.pallas_call(
        body, name="allreduce_small", in_specs=[_VMEM] * n, out_specs=[_VMEM] * n,
        out_shape=[jax.ShapeDtypeStruct(p.shape, f32) for p in parts],
        scratch_shapes=[pltpu.VMEM(p.shape, f32) for p in parts] * 2 + [pltpu.VMEM((4,) + p.shape, f32) for p in parts]
        + [pltpu.SemaphoreType.DMA((n,)), pltpu.SemaphoreType.DMA((n,)), pltpu.SemaphoreType.DMA((3 * n,)),
           pltpu.SemaphoreType.DMA((3 * n,)), pltpu.SemaphoreType.DMA((n,)), pltpu.SemaphoreType.DMA((n,))],
        compiler_params=pltpu.CompilerParams(vmem_limit_bytes=_VMEM_LIMIT_BYTES),
    )(*parts)


def _local_step(x, mem, tgt, g_mix, g_xattn, g_mem, g_ffn, g_final, cb, lg, lb, pw, ps, fb, relay, weights, reduce, n_seq, seq, n_mem):
    t, d = x.shape
    f = fb.shape[1] // 2
    c = cb.shape[1]
    h1 = _rms_fwd("norm_mix", x, g_mix)
    relay(0, h1)
    w_in, cw, fw = weights(0, h1)
    u = _mm_nn("proj_in", h1, w_in, _ACT, w_in.shape[1])
    y, hc = _mix_fwd(u, cw, cb, lg, lb, pw, ps, seq)
    relay(1, y)
    w_out, w_q, w_kv, w_o = weights(1, y)
    x1, h2 = _proj_residual_norm("proj_out", y, w_out, x, g_xattn)
    q = _mm_nn("proj_q", h2, w_q, _ACT, d)
    mem_n = _rms_fwd("norm_mem", mem, g_mem)
    kv = _mm_nn("proj_kv", mem_n, w_kv, _ACT, 2 * d)
    o = _attn_fwd(q, kv, n_seq, seq, n_mem)
    relay(2, o)
    x2, h3 = _proj_residual_norm("proj_o", o, w_o, x1, g_ffn)
    w_up, w_down = weights(2, h3)
    up = _mm_nn("proj_up", h3, w_up, _ACT, f, split_out=True)
    a = _ffn_gate_fwd(up, fw, fb, seq)
    dx3, dx3b, dg_final, loss = _proj_loss_bwd("proj_down", a, w_down, x2, g_final, tgt)
    da = _mm_nt("d_act", dx3b, w_down, _ACT)
    gw_down = _mm_tn_rows("dw_down", a, dx3b, f // 2, d // 2)
    dup, sums_g, sums_v = _ffn_gate_bwd(up, da, fw, fb, seq)
    gw_up = _mm_tn_pieces("dw_up", h3, dup, f // 2, t)
    token = reduce(0, [gw_down.reshape(8, -1, d), gw_up])
    dx2, dx2b, dg_ffn = _dproj_rms_bwd("d_h3", dup, w_up, x2, g_ffn + token, dx3)
    do = _mm_nt("d_o", dx2b, w_o, _ACT)
    gw_o = _mm_tn_rows("dw_o", o, dx2b, d, d // 2)
    dq, dkv = _attn_bwd(q, kv, do, n_seq, seq, n_mem)
    gw_q = _mm_tn_rows("dw_q", h2, dq, d, d // 2)
    gw_kv = _mm_tn_pieces("dw_kv", mem_n, dkv, d // 2, mem.shape[0])
    dmem_n = _mm_nt("d_mem_n", dkv, w_kv, f32)
    dg_mem = _rms_gain_grad("norm_mem_bwd", mem, dmem_n)
    dx1, dx1b, dg_xattn = _dproj_rms_bwd("d_h2", dq, w_q, x1, g_xattn, dx2)
    dy = _mm_nt("d_y", dx1b, w_out, _ACT)
    gw_out = _mm_tn_rows("dw_out", y, dx1b, d, d // 2)
    token = reduce(1, [gw_o.reshape(8, -1, d), gw_q.reshape(8, -1, d), gw_kv, gw_out.reshape(8, -1, d)])
    dhc, sums_norm = _mix_bwd_norm(hc, dy, lg + token, lb, seq)
    du, d_cw, d_ps, d_pw = _mix_bwd_taps(u, dhc, dy, cw, pw, ps, seq)
    gw_in = _mm_tn_pieces("dw_in", h1, du, c * 3 // 4, t)
    token = reduce(2, [gw_in])
    grad_x, dg_mix = _dproj_rms_bwd("d_h1", du, w_in, x, g_mix + token, dx1, storage_copy=False)
    zero_row = jnp.zeros((1, d), f32)
    gains = jnp.concatenate([dg_mix, dg_xattn, dg_mem, dg_ffn, dg_final, jnp.pad(loss, ((0, 0), (0, d - 1))), zero_row, zero_row], axis=0)
    conv_rows = jnp.concatenate([sums_norm[2:3], sums_norm[0:1], sums_norm[1:2], d_ps[0:1], jnp.zeros((4, c), f32)], axis=0)
    ffn_rows = jnp.concatenate([sums_g, sums_v], axis=1)
    small = [gains, conv_rows, d_pw.reshape(-1, d_pw.shape[-1]), ffn_rows, d_cw]
    return grad_x, small


def kernel(x, mem, norm_mix_g, w_in, conv_dw_w, conv_dw_b, conv_ln_g, conv_ln_b, pool_w, pool_scale, w_out, norm_xattn_g, norm_mem_g, w_q, w_kv, w_o, norm_ffn_g, w_up, ffn_dw_w, ffn_dw_b, w_down, norm_final_g, loss_target, m_norm_mix_g, m_w_in, m_conv_dw_w, m_conv_dw_b, m_conv_ln_g, m_conv_ln_b, m_pool_w, m_pool_scale, m_w_out, m_norm_xattn_g, m_norm_mem_g, m_w_q, m_w_kv, m_w_o, m_norm_ffn_g, m_w_up, m_ffn_dw_w, m_ffn_dw_b, m_w_down, m_norm_final_g, v_norm_mix_g, v_w_in, v_conv_dw_w, v_conv_dw_b, v_conv_ln_g, v_conv_ln_b, v_pool_w, v_pool_scale, v_w_out, v_norm_xattn_g, v_norm_mem_g, v_w_q, v_w_kv, v_w_o, v_norm_ffn_g, v_w_up, v_ffn_dw_w, v_ffn_dw_b, v_w_down, v_norm_final_g):
    n_seq, seq, d = x.shape
    n_mem = mem.shape[1]
    chip = 2 * lax.axis_index("x") + lax.axis_index("y")

    place = jnp.stack([chip, lax.axis_index("c")]).astype(jnp.int32)

    col_w = [w_in, w_kv, w_up]
    row_w = [w_out, w_q, w_o, w_down]
    col_flags = [True] * 3 + [False] * 4 + [True] * 2
    kw = conv_dw_w.shape[1]

    def padded_in_place(shard, rows):
        full = jnp.zeros((rows, 4 * shard.shape[1]), shard.dtype)
        return lax.dynamic_update_slice(full, shard, (0, chip * shard.shape[1]))

    bufs = list(_place_shards(place, [w[0] for w in col_w + row_w], col_flags[:7]))
    bufs += [padded_in_place(conv_dw_w[0], _HALO), padded_in_place(ffn_dw_w[0], 8)]
    groups = [[0, 7, 8], [3, 4, 1, 5], [2, 6]]
    whole = [False] * 7 + [True] * 2
    bufs, sems = _allgather_start(bufs, col_flags, whole, groups)
    relayed = {}

    def relay(g, after):
        members = groups[g]
        relayed[g] = _allgather_relay("allgather_relay_%d" % g, [bufs[i] for i in members], [col_flags[i] for i in members],
                                      [whole[i] for i in members], sems[g], after)

    def weights(g, after):
        members = groups[g]
        group_bufs, sibling_sems = relayed[g]
        return _allgather_wait("allgather_wait_%d" % g, group_bufs, [col_flags[i] for i in members],
                               [whole[i] for i in members], sibling_sems, after)

    names = ["w_in", "w_kv", "w_up", "w_out", "w_q", "w_o", "w_down"]
    reduce_groups = [["w_down", "w_up"], ["w_o", "w_q", "w_kv", "w_out"], ["w_in"]]
    in_flight = {}

    def reduce(g, grads):
        grads, lands, rs_sems, token = _grad_exchange_start("rs_start_%d" % g, grads)
        in_flight[g] = (grads, lands, rs_sems)
        return token[0:1, 0:1]

    grad_x, small = _local_step(
        x.reshape(n_seq * seq, d), mem.reshape(n_seq * n_mem, d), loss_target.reshape(n_seq * seq, d),
        norm_mix_g, norm_xattn_g, norm_mem_g, norm_ffn_g, norm_final_g.reshape(1, d),
        conv_dw_b, conv_ln_g, conv_ln_b, pool_w[0], pool_scale, ffn_dw_b, relay, weights, reduce, n_seq, seq, n_mem)

    landed = {}
    for g, members in enumerate(reduce_groups):
        grads, lands, rs_sems = in_flight[g]
        grads, lands = _grad_exchange_wait("rs_wait_%d" % g, grads, lands, rs_sems, grad_x)
        landed.update(zip(members, zip(grads, lands)))
    finals = _sum_partials("rs_sum", place, [landed[n][0] for n in names], [landed[n][1] for n in names])
    finals, swap_sems = _swap_halves_start(finals)

    gains, conv_rows, d_pw, ffn_rows, d_cw = _allreduce_small(small)
    loss = gains[5, 0]
    shard_grads = _swap_halves_wait(finals, swap_sems, gains)

    outs = {}
    big_w = dict(zip(names, col_w + row_w))
    big_m = dict(w_in=m_w_in, w_kv=m_w_kv, w_up=m_w_up, w_out=m_w_out, w_q=m_w_q, w_o=m_w_o, w_down=m_w_down)
    big_v = dict(w_in=v_w_in, w_kv=v_w_kv, w_up=v_w_up, w_out=v_w_out, w_q=v_w_q, w_o=v_w_o, w_down=v_w_down)
    big_quads = [(big_w[n], g.reshape(big_w[n].shape[1:]), big_m[n], big_v[n]) for n, g in zip(names, shard_grads)]
    outs.update(zip(names, _adamw_shards(big_quads)))

    f2 = ffn_dw_b.shape[1]
    cs_c = conv_dw_w.shape[2]
    cs_f = ffn_dw_w.shape[2]
    g_cw = lax.dynamic_slice(d_cw, (0, chip * cs_c), (kw, cs_c)).reshape(conv_dw_w.shape)
    g_fw = lax.dynamic_slice(ffn_rows, (1, chip * cs_f), (ffn_dw_w.shape[1], cs_f)).reshape(ffn_dw_w.shape)
    small_params = [
        ("norm_mix_g", norm_mix_g, gains[0:1], m_norm_mix_g, v_norm_mix_g),
        ("conv_dw_w", conv_dw_w, g_cw, m_conv_dw_w, v_conv_dw_w),
        ("conv_dw_b", conv_dw_b, conv_rows[0:1], m_conv_dw_b, v_conv_dw_b),
        ("conv_ln_g", conv_ln_g, conv_rows[1:2], m_conv_ln_g, v_conv_ln_g),
        ("conv_ln_b", conv_ln_b, conv_rows[2:3], m_conv_ln_b, v_conv_ln_b),
        ("pool_w", pool_w, d_pw.reshape(pool_w.shape), m_pool_w, v_pool_w),
        ("pool_scale", pool_scale, conv_rows[3:4], m_pool_scale, v_pool_scale),
        ("norm_xattn_g", norm_xattn_g, gains[1:2], m_norm_xattn_g, v_norm_xattn_g),
        ("norm_mem_g", norm_mem_g, gains[2:3], m_norm_mem_g, v_norm_mem_g),
        ("norm_ffn_g", norm_ffn_g, gains[3:4], m_norm_ffn_g, v_norm_ffn_g),
        ("ffn_dw_w", ffn_dw_w, g_fw, m_ffn_dw_w, v_ffn_dw_w),
        ("ffn_dw_b", ffn_dw_b, ffn_rows[0:1, :f2], m_ffn_dw_b, v_ffn_dw_b),
        ("norm_final_g", norm_final_g.reshape(1, d), gains[4:5], m_norm_final_g.reshape(1, d), v_norm_final_g.reshape(1, d)),
    ]
    quads = []
    for _, w, g, m, v in small_params:
        shape2 = (-1, w.shape[-1])
        quads.append((w.reshape(shape2), g.reshape(shape2), m.reshape(shape2), v.reshape(shape2)))
    for (n, w, g, _, _), (delta, new_m, new_v) in zip(small_params, _adamw_small(quads)):
        shape = norm_final_g.shape if n == "norm_final_g" else w.shape
        outs[n] = (g.reshape(shape), delta.reshape(shape), new_m.reshape(shape), new_v.reshape(shape))

    order = ["norm_mix_g", "w_in", "conv_dw_w", "conv_dw_b", "conv_ln_g", "conv_ln_b", "pool_w", "pool_scale", "w_out",
             "norm_xattn_g", "norm_mem_g", "w_q", "w_kv", "w_o", "norm_ffn_g", "w_up", "ffn_dw_w", "ffn_dw_b", "w_down",
             "norm_final_g"]
    return (loss, grad_x.reshape(x.shape), *[outs[n][0] for n in order], *[outs[n][1] for n in order],
            *[outs[n][2] for n in order], *[outs[n][3] for n in order])
```

```python
import functools

import jax
import jax.numpy as jnp
from jax import lax
from jax.experimental import pallas as pl
from jax.experimental.pallas import tpu as pltpu

f32 = jnp.float32
_ACT = jnp.bfloat16

EPS = 1e-6
POOL_WINDOWS = (2, 4, 8, 16)
XATTN_HEADS = 4
ADAM_LR = 0.001
ADAM_B1 = 0.9
ADAM_B2 = 0.999
ADAM_EPS = 1e-08
ADAM_WD = 0.01
ADAM_STEP = 10

_VMEM_LIMIT_BYTES = 56 * 1024 * 1024
_MESH = pl.DeviceIdType.MESH
_ANY = pl.BlockSpec(memory_space=pl.ANY)
_VMEM = pl.BlockSpec(memory_space=pltpu.VMEM)
_HBM = pl.BlockSpec(memory_space=pltpu.HBM)
_SEM = pl.BlockSpec(memory_space=pltpu.SEMAPHORE)
_EFFECT = pltpu.SideEffectType.DATAFLOW_SIDE_EFFECTING

_NN = (((1,), (0,)), ((), ()))
_NT = (((1,), (1,)), ((), ()))
_TN = (((0,), (0,)), ((), ()))


def _params(n_grid):
    return pltpu.CompilerParams(dimension_semantics=("arbitrary",) * n_grid, vmem_limit_bytes=_VMEM_LIMIT_BYTES)


def _sigmoid(v):
    return 1.0 / (1.0 + jnp.exp(-v))


def _dot(a, b, dims):
    return lax.dot_general(a, b, dims, preferred_element_type=f32)


def _mm(name, a, b, *, dims, grid, a_spec, b_spec, o_spec, out_shape, nk, acc_shape=None, res=None, res_spec=None):
    def body(*refs):
        if res is None:
            a_ref, b_ref, o_ref, *scratch = refs
            r_ref = None
        else:
            a_ref, b_ref, r_ref, o_ref, *scratch = refs
        p = _dot(a_ref[...], b_ref[...], dims)

        def finish(v):
            if r_ref is not None:
                v = v + r_ref[...]
            o_ref[...] = v.astype(o_ref.dtype)

        if nk == 1:
            finish(p)
        else:
            acc = scratch[0]
            k = pl.program_id(2)

            @pl.when(k == 0)
            def _():
                acc[...] = p

            @pl.when(k > 0)
            def _():
                acc[...] += p

            @pl.when(k == nk - 1)
            def _():
                finish(acc[...])

    ins = [a, b] + ([] if res is None else [res])
    specs = [a_spec, b_spec] + ([] if res is None else [res_spec])
    return pl.pallas_call(
        body, name=name, grid=grid, in_specs=specs, out_specs=o_spec, out_shape=out_shape,
        scratch_shapes=[pltpu.VMEM(acc_shape, f32)] if nk > 1 else [], compiler_params=_params(3),
    )(*ins)


_NARROW = 2816


def _row_tile(m, width=_NARROW + 1):
    return min(1024 if width <= _NARROW else 512, m)


def _mm_nn(name, a, b, out_dtype, tn, res=None, split_out=False):
    m, k = a.shape
    n = b.shape[1]
    tm = _row_tile(m, max(k, tn))
    if split_out:
        out_shape = jax.ShapeDtypeStruct((n // tn, m, tn), out_dtype)
        o_spec = pl.BlockSpec((None, tm, tn), lambda j, i, kk: (j, i, 0))
    else:
        out_shape = jax.ShapeDtypeStruct((m, n), out_dtype)
        o_spec = pl.BlockSpec((tm, tn), lambda j, i, kk: (i, j))
    return _mm(
        name, a, b, dims=_NN, grid=(n // tn, m // tm, 1), nk=1,
        a_spec=pl.BlockSpec((tm, k), lambda j, i, kk: (i, 0)),
        b_spec=pl.BlockSpec((k, tn), lambda j, i, kk: (0, j)),
        o_spec=o_spec, out_shape=out_shape, res=res,
        res_spec=pl.BlockSpec((tm, tn), lambda j, i, kk: (i, j)),
    )


def _mm_nt(name, a, b, out_dtype):
    n, kc = b.shape
    m = a.shape[0]
    tm = _row_tile(m, max(n, kc))
    return _mm(
        name, a, b, dims=_NT, grid=(m // tm, 1, 1), nk=1,
        a_spec=pl.BlockSpec((tm, kc), lambda i, j, k: (i, 0)),
        b_spec=pl.BlockSpec((n, kc), lambda i, j, k: (0, 0), pipeline_mode=pl.Buffered(1)),
        o_spec=pl.BlockSpec((tm, n), lambda i, j, k: (i, 0)),
        out_shape=jax.ShapeDtypeStruct((m, n), out_dtype),
    )


def _mm_tn_rows(name, a, b, tka, tn):
    m, ka = a.shape
    nb = b.shape[1]
    return _mm(
        name, a, b, dims=_TN, grid=(ka // tka, nb // tn, 1), nk=1,
        a_spec=pl.BlockSpec((m, tka), lambda i, j, k: (0, i)),
        b_spec=pl.BlockSpec((m, tn), lambda i, j, k: (0, j)),
        o_spec=pl.BlockSpec((tka, tn), lambda i, j, k: (i, j)),
        out_shape=jax.ShapeDtypeStruct((ka, nb), _ACT),
    )


def _mm_tn_pieces(name, a, b, cs, tt):
    m, ka = a.shape
    nk = m // tt
    if b.ndim == 3:
        b_spec = pl.BlockSpec((None, tt, cs), lambda i, j, k: (j // 2, k, j % 2))
    else:
        b_spec = pl.BlockSpec((tt, cs), lambda i, j, k: (k, j))
    return _mm(
        name, a, b, dims=_TN, grid=(2, 4, nk), nk=nk, acc_shape=(ka // 2, cs),
        a_spec=pl.BlockSpec((tt, ka // 2), lambda i, j, k: (k, i)), b_spec=b_spec,
        o_spec=pl.BlockSpec((None, ka // 2, cs), lambda i, j, k: (2 * j + i, 0, 0)),
        out_shape=jax.ShapeDtypeStruct((8, ka // 2, cs), _ACT),
    )


def _rms_fwd(name, x, g):
    t, d = x.shape
    tm = _row_tile(t, d)

    def body(x_ref, g_ref, h_ref):
        xv = x_ref[...]
        r = lax.rsqrt(jnp.mean(xv * xv, axis=-1, keepdims=True) + EPS)
        h_ref[...] = (xv * r * g_ref[...]).astype(h_ref.dtype)

    return pl.pallas_call(
        body, name=name, grid=(t // tm,),
        in_specs=[pl.BlockSpec((tm, d), lambda i: (i, 0)), pl.BlockSpec((1, d), lambda i: (0, 0))],
        out_specs=pl.BlockSpec((tm, d), lambda i: (i, 0)), out_shape=jax.ShapeDtypeStruct((t, d), _ACT),
        compiler_params=_params(1),
    )(x, g)


def _fused_rows(name, a, b, product, a_spec, tm, extras, extra_specs, out_shape, out_specs, epilogue):
    ne = len(extras)

    def body(a_ref, b_ref, *refs):
        epilogue(product(a_ref, b_ref), refs[:ne], refs[ne:])

    m = extras[0].shape[0]
    return pl.pallas_call(
        body, name=name, grid=(m // tm,),
        in_specs=[a_spec, pl.BlockSpec(b.shape, lambda i: (0, 0), pipeline_mode=pl.Buffered(1)), *extra_specs],
        out_specs=out_specs, out_shape=out_shape, compiler_params=_params(1),
    )(a, b, *extras)


def _proj_residual_norm(name, a, b, res, g):
    m, k = a.shape
    d = b.shape[1]
    tm = _row_tile(m, max(k, d))

    def epilogue(p, ins, outs):
        xv = p + ins[0][...]
        outs[0][...] = xv
        r = lax.rsqrt(jnp.mean(xv * xv, axis=-1, keepdims=True) + EPS)
        outs[1][...] = (xv * r * ins[1][...]).astype(outs[1].dtype)

    row = pl.BlockSpec((tm, d), lambda i: (i, 0))
    return _fused_rows(
        name, a, b, lambda a_ref, b_ref: _dot(a_ref[...], b_ref[...], _NN), pl.BlockSpec((tm, k), lambda i: (i, 0)), tm,
        [res, g], [row, pl.BlockSpec((1, d), lambda i: (0, 0))],
        [jax.ShapeDtypeStruct((m, d), f32), jax.ShapeDtypeStruct((m, d), _ACT)], [row, row], epilogue)


def _dproj_rms_bwd(name, a, b, x, g, dres, storage_copy=True):
    m, d = x.shape
    if a.ndim == 3:
        nh, _, kh = a.shape
        tm = _row_tile(m, nh * kh)
        a_spec = pl.BlockSpec((nh, tm, kh), lambda i: (0, i, 0))

        def product(a_ref, b_ref):
            p = _dot(a_ref[0], b_ref[:, 0:kh], _NT)
            for h in range(1, nh):
                p = p + _dot(a_ref[h], b_ref[:, h * kh:(h + 1) * kh], _NT)
            return p
    else:
        tm = _row_tile(m, max(a.shape[1], d))
        a_spec = pl.BlockSpec((tm, a.shape[1]), lambda i: (i, 0))

        def product(a_ref, b_ref):
            return _dot(a_ref[...], b_ref[...], _NT)

    def epilogue(dhv, ins, outs):
        x_ref, g_ref, dres_ref = ins
        dg_ref = outs[-1]

        @pl.when(pl.program_id(0) == 0)
        def _():
            dg_ref[...] = jnp.zeros_like(dg_ref)

        xv = x_ref[...]
        r = lax.rsqrt(jnp.mean(xv * xv, axis=-1, keepdims=True) + EPS)
        xn = xv * r
        dxn = dhv * g_ref[...]
        dx = r * (dxn - xn * jnp.mean(dxn * xn, axis=-1, keepdims=True)) + dres_ref[...]
        outs[0][...] = dx
        if storage_copy:
            outs[1][...] = dx.astype(outs[1].dtype)
        dg_ref[...] += jnp.sum(dhv * xn, axis=0, keepdims=True)

    row = pl.BlockSpec((tm, d), lambda i: (i, 0))
    vec = pl.BlockSpec((1, d), lambda i: (0, 0))
    copies = [jax.ShapeDtypeStruct((m, d), _ACT)] if storage_copy else []
    return _fused_rows(
        name, a, b, product, a_spec, tm, [x, g, dres], [row, vec, row],
        [jax.ShapeDtypeStruct((m, d), f32)] + copies + [jax.ShapeDtypeStruct((1, d), f32)],
        [row] * (1 + len(copies)) + [vec], epilogue)


def _proj_loss_bwd(name, a, b, res, g, tgt):
    m, k = a.shape
    d = b.shape[1]
    tm = _row_tile(m, max(k, d))

    def epilogue(p, ins, outs):
        res_ref, g_ref, t_ref = ins
        dx_ref, dxb_ref, dg_ref, loss_ref = outs

        @pl.when(pl.program_id(0) == 0)
        def _():
            dg_ref[...] = jnp.zeros_like(dg_ref)
            loss_ref[...] = jnp.zeros_like(loss_ref)

        xv = p + res_ref[...]
        gv = g_ref[...]
        r = lax.rsqrt(jnp.mean(xv * xv, axis=-1, keepdims=True) + EPS)
        xn = xv * r
        err = xn * gv - t_ref[...]
        loss_ref[...] += 0.5 * jnp.sum(jnp.mean(err * err, axis=-1, keepdims=True), axis=0, keepdims=True)
        dout = err * (1.0 / d)
        dxn = dout * gv
        dx = r * (dxn - xn * jnp.mean(dxn * xn, axis=-1, keepdims=True))
        dx_ref[...] = dx
        dxb_ref[...] = dx.astype(dxb_ref.dtype)
        dg_ref[...] += jnp.sum(dout * xn, axis=0, keepdims=True)

    row = pl.BlockSpec((tm, d), lambda i: (i, 0))
    vec = pl.BlockSpec((1, d), lambda i: (0, 0))
    return _fused_rows(
        name, a, b, lambda a_ref, b_ref: _dot(a_ref[...], b_ref[...], _NN), pl.BlockSpec((tm, k), lambda i: (i, 0)), tm,
        [res, g, tgt], [row, vec, row],
        [jax.ShapeDtypeStruct((m, d), f32), jax.ShapeDtypeStruct((m, d), _ACT), jax.ShapeDtypeStruct((1, d), f32),
         jax.ShapeDtypeStruct((1, 1), f32)],
        [row, row, vec, pl.BlockSpec((1, 1), lambda i: (0, 0))], epilogue)


def _rms_gain_grad(name, x, dh):
    t, d = x.shape
    tm = _row_tile(t)

    def body(x_ref, dh_ref, dg_ref):
        @pl.when(pl.program_id(0) == 0)
        def _():
            dg_ref[...] = jnp.zeros_like(dg_ref)

        xv = x_ref[...]
        r = lax.rsqrt(jnp.mean(xv * xv, axis=-1, keepdims=True) + EPS)
        dg_ref[...] += jnp.sum(dh_ref[...] * (xv * r), axis=0, keepdims=True)

    row = pl.BlockSpec((tm, d), lambda i: (i, 0))
    return pl.pallas_call(
        body, name=name, grid=(t // tm,), in_specs=[row, row], out_specs=pl.BlockSpec((1, d), lambda i: (0, 0)),
        out_shape=jax.ShapeDtypeStruct((1, d), f32), compiler_params=_params(1),
    )(x, dh)


_CONV_ROWS = 512
_CHUNK = 64
_HALO = 32


def _pool_counts(pos, w):
    return jnp.minimum(pos + 1.0, float(w))


def _rows_from(win, start, rows):
    if start % 8 == 0:
        return win[start:start + rows, :]
    n = win.shape[0]
    return pltpu.roll(win, n - start % 8, axis=0)[start - start % 8:start - start % 8 + rows, :]


def _tap_rows(buf, starts, rows):
    for residue in range(8):
        group = [(k, s) for k, s in starts.items() if s % 8 == residue]
        if group:
            lo = min(s for _, s in group) - residue
            hi = max(s for _, s in group) - residue + rows + (8 if residue else 0)
            win = buf[lo:hi, :]
            if residue:
                win = pltpu.roll(win, hi - lo - residue, axis=0)
            for k, s in group:
                yield k, win[s - residue - lo:s - residue - lo + rows, :]


def _mix_fwd(u, cw, cb, lg, lb, pw, ps, seq):
    t, c3 = u.shape
    c = c3 // 3
    kw = 31
    tm = min(_CONV_ROWS, seq)
    tps = seq // tm
    gd = c // len(POOL_WINDOWS)

    def body(u_ref, uh_ref, cw_ref, cb_ref, lg_ref, lb_ref, pw_ref, ps_ref, y_ref, hc_ref, hgbuf, pbuf):
        i = pl.program_id(0)
        keep = jnp.where(i % tps == 0, 0.0, 1.0)
        um = u_ref[...].astype(f32)
        uh = uh_ref[...].astype(f32) * keep
        hgbuf[0:_HALO, :] = uh[:, 0:c] * _sigmoid(uh[:, c:2 * c])
        hgbuf[_HALO:_HALO + tm, :] = um[:, 0:c] * _sigmoid(um[:, c:2 * c])
        pbuf[0:_HALO, :] = uh[:, 2 * c:]
        pbuf[_HALO:_HALO + tm, :] = um[:, 2 * c:]
        for r0 in range(0, tm, _CHUNK):
            acc = jnp.broadcast_to(cb_ref[...], (_CHUNK, c))
            for k, rows in _tap_rows(hgbuf, {k: r0 + _HALO - (kw - 1) + k for k in range(kw)}, _CHUNK):
                acc = acc + cw_ref[k:k + 1, :] * rows
            hc_ref[r0:r0 + _CHUNK, :] = acc
            mu = jnp.mean(acc, axis=-1, keepdims=True)
            xc = acc - mu
            var = jnp.mean(xc * xc, axis=-1, keepdims=True)
            hl = xc * lax.rsqrt(var + EPS) * lg_ref[...] + lb_ref[...]
            y_ref[r0:r0 + _CHUNK, 0:c] = (hl * _sigmoid(hl)).astype(y_ref.dtype)
        pos = ((i % tps) * tm).astype(f32) + lax.broadcasted_iota(jnp.int32, (tm, 1), 0).astype(f32)
        for gi, w in enumerate(POOL_WINDOWS):
            sl = slice(gi * gd, (gi + 1) * gd)
            v = pbuf[_HALO:_HALO + tm, sl]
            s = v
            for j in range(1, w):
                s = s + pbuf[_HALO - j:_HALO - j + tm, sl]
            pooled = s / _pool_counts(pos, w) - v
            mixed = _dot(pooled.astype(_ACT), pw_ref[gi].astype(_ACT), _NN)
            y_ref[:, c + gi * gd:c + (gi + 1) * gd] = (mixed * ps_ref[:, sl]).astype(y_ref.dtype)

    hb = tm // _HALO
    full = lambda shape: pl.BlockSpec(shape, lambda i: (0,) * len(shape))
    return pl.pallas_call(
        body, name="mix_fwd", grid=(t // tm,),
        in_specs=[pl.BlockSpec((tm, c3), lambda i: (i, 0)),
                  pl.BlockSpec((_HALO, c3), lambda i: (jnp.maximum(i * hb - 1, 0), 0)),
                  full((_HALO, c)), full((1, c)), full((1, c)), full((1, c)), full((len(POOL_WINDOWS), gd, gd)), full((1, c))],
        out_specs=[pl.BlockSpec((tm, 2 * c), lambda i: (i, 0)), pl.BlockSpec((tm, c), lambda i: (i, 0))],
        out_shape=[jax.ShapeDtypeStruct((t, 2 * c), _ACT), jax.ShapeDtypeStruct((t, c), f32)],
        scratch_shapes=[pltpu.VMEM((_HALO + tm, c), f32), pltpu.VMEM((_HALO + tm, c), f32)],
        compiler_params=_params(1),
    )(u, u, cw, cb, lg, lb, pw, ps)


def _mix_bwd_norm(hc, dy, lg, lb, seq):
    t, c = hc.shape
    tm = _row_tile(t, c)

    def body(hc_ref, dy_ref, lg_ref, lb_ref, dhc_ref, sums_ref):
        @pl.when(pl.program_id(0) == 0)
        def _():
            sums_ref[...] = jnp.zeros_like(sums_ref)

        hcv = hc_ref[...]
        mu = jnp.mean(hcv, axis=-1, keepdims=True)
        xc = hcv - mu
        rstd = lax.rsqrt(jnp.mean(xc * xc, axis=-1, keepdims=True) + EPS)
        n = xc * rstd
        hl = n * lg_ref[...] + lb_ref[...]
        sg = _sigmoid(hl)
        dhl = dy_ref[...].astype(f32) * (sg * (1.0 + hl * (1.0 - sg)))
        dn = dhl * lg_ref[...]
        dhc = rstd * (dn - jnp.mean(dn, axis=-1, keepdims=True) - n * jnp.mean(dn * n, axis=-1, keepdims=True))
        dhc_ref[...] = dhc
        sums_ref[0:1, :] += jnp.sum(dhl * n, axis=0, keepdims=True)
        sums_ref[1:2, :] += jnp.sum(dhl, axis=0, keepdims=True)
        sums_ref[2:3, :] += jnp.sum(dhc, axis=0, keepdims=True)

    row = pl.BlockSpec((tm, c), lambda i: (i, 0))
    vec = pl.BlockSpec((1, c), lambda i: (0, 0))
    return pl.pallas_call(
        body, name="mix_bwd_norm", grid=(t // tm,), in_specs=[row, row, vec, vec],
        out_specs=[row, pl.BlockSpec((8, c), lambda i: (0, 0))],
        out_shape=[jax.ShapeDtypeStruct((t, c), f32), jax.ShapeDtypeStruct((8, c), f32)],
        compiler_params=_params(1),
    )(hc, dy, lg, lb)


def _mix_bwd_taps(u, dhc, dy, cw, pw, ps, seq):
    t, c3 = u.shape
    c = c3 // 3
    kw = 31
    tm = min(_CONV_ROWS, seq)
    tps = seq // tm
    ng = len(POOL_WINDOWS)
    gd = c // ng
    nh = 16

    def body(u_ref, uh_ref, dhc_ref, dhcn_ref, dy_ref, dyn_ref, cw_ref, pw_ref, ps_ref,
             du_ref, dcw_ref, dps_ref, dpw_ref, hgbuf, dcbuf, pbuf, dpbuf):
        i = pl.program_id(0)
        keep_prev = jnp.where(i % tps == 0, 0.0, 1.0)
        keep_next = jnp.where(i % tps == tps - 1, 0.0, 1.0)

        @pl.when(i == 0)
        def _():
            dcw_ref[...] = jnp.zeros_like(dcw_ref)
            dps_ref[...] = jnp.zeros_like(dps_ref)
            dpw_ref[...] = jnp.zeros_like(dpw_ref)

        uh = uh_ref[...].astype(f32) * keep_prev
        hgbuf[0:_HALO, :] = uh[:, 0:c] * _sigmoid(uh[:, c:2 * c])
        pbuf[0:_HALO, :] = uh[:, 2 * c:]
        um = u_ref[...].astype(f32)
        hgbuf[_HALO:_HALO + tm, :] = um[:, 0:c] * _sigmoid(um[:, c:2 * c])
        pbuf[_HALO:_HALO + tm, :] = um[:, 2 * c:]
        dcbuf[0:tm, :] = dhc_ref[...]
        dcbuf[tm:tm + _HALO, :] = dhcn_ref[...] * keep_next
        tap_sums = [None] * kw
        for r0 in range(0, tm, _CHUNK):
            dh = dcbuf[r0:r0 + _CHUNK, :]
            acc = jnp.zeros((_CHUNK, c), f32)
            for k, rows in _tap_rows(hgbuf, {k: r0 + _HALO - (kw - 1) + k for k in range(kw)}, _CHUNK):
                part = (dh * rows).reshape(_CHUNK // 8, 8, c).sum(axis=0)
                tap_sums[k] = part if tap_sums[k] is None else tap_sums[k] + part
            for k, rows in _tap_rows(dcbuf, {k: r0 + (kw - 1) - k for k in range(kw)}, _CHUNK):
                acc = acc + cw_ref[k:k + 1, :] * rows
            val = u_ref[r0:r0 + _CHUNK, 0:c].astype(f32)
            sg = _sigmoid(u_ref[r0:r0 + _CHUNK, c:2 * c].astype(f32))
            du_ref[r0:r0 + _CHUNK, 0:c] = (acc * sg).astype(du_ref.dtype)
            du_ref[r0:r0 + _CHUNK, c:2 * c] = (acc * val * sg * (1.0 - sg)).astype(du_ref.dtype)
        for k in range(kw):
            dcw_ref[k:k + 1, :] += jnp.sum(tap_sums[k], axis=0, keepdims=True)
        base = ((i % tps) * tm).astype(f32)
        pos = base + lax.broadcasted_iota(jnp.int32, (tm, 1), 0).astype(f32)
        pos_next = base + float(tm) + lax.broadcasted_iota(jnp.int32, (nh, 1), 0).astype(f32)
        for gi, w in enumerate(POOL_WINDOWS):
            sl = slice(gi * gd, (gi + 1) * gd)
            v = pbuf[_HALO:_HALO + tm, sl]
            s = v
            for j in range(1, w):
                s = s + pbuf[_HALO - j:_HALO - j + tm, sl]
            cnt = _pool_counts(pos, w)
            pooled = (s / cnt - v).astype(_ACT)
            pwg = pw_ref[gi].astype(_ACT)
            mixed = _dot(pooled, pwg, _NN)
            dyp = dy_ref[:, sl].astype(f32)
            dps_ref[0:1, sl] += jnp.sum(dyp * mixed, axis=0, keepdims=True)
            dmix = (dyp * ps_ref[:, sl]).astype(_ACT)
            dpw_ref[gi] += _dot(pooled, dmix, _TN)
            dmix_next = (dyn_ref[:, sl].astype(f32) * ps_ref[:, sl] * keep_next).astype(_ACT)
            dpool = _dot(dmix, pwg, _NT)
            dpbuf[0:tm, sl] = dpool / cnt
            dpbuf[tm:tm + nh, sl] = _dot(dmix_next, pwg, _NT) / _pool_counts(pos_next, w)
            acc = -dpool
            for j in range(w):
                acc = acc + dpbuf[j:j + tm, sl]
            du_ref[:, 2 * c + gi * gd:2 * c + (gi + 1) * gd] = acc.astype(du_ref.dtype)

    hb = tm // _HALO
    n_halo = t // _HALO
    n_nh = t // nh
    full = lambda shape: pl.BlockSpec(shape, lambda i: (0,) * len(shape))
    return pl.pallas_call(
        body, name="mix_bwd_taps", grid=(t // tm,),
        in_specs=[pl.BlockSpec((tm, c3), lambda i: (i, 0)),
                  pl.BlockSpec((_HALO, c3), lambda i: (jnp.maximum(i * hb - 1, 0), 0)),
                  pl.BlockSpec((tm, c), lambda i: (i, 0)),
                  pl.BlockSpec((_HALO, c), lambda i: (jnp.minimum((i + 1) * hb, n_halo - 1), 0)),
                  pl.BlockSpec((tm, c), lambda i: (i, 1)),
                  pl.BlockSpec((nh, c), lambda i: (jnp.minimum((i + 1) * (tm // nh), n_nh - 1), 1)),
                  full((_HALO, c)), full((ng, gd, gd)), full((1, c))],
        out_specs=[pl.BlockSpec((tm, c3), lambda i: (i, 0)), full((_HALO, c)), full((8, c)), full((ng, gd, gd))],
        out_shape=[jax.ShapeDtypeStruct((t, c3), _ACT), jax.ShapeDtypeStruct((_HALO, c), f32),
                   jax.ShapeDtypeStruct((8, c), f32), jax.ShapeDtypeStruct((ng, gd, gd), f32)],
        scratch_shapes=[pltpu.VMEM((_HALO + tm, c), f32), pltpu.VMEM((tm + _HALO, c), f32),
                        pltpu.VMEM((_HALO + tm, c), f32), pltpu.VMEM((tm + nh, c), f32)],
        compiler_params=_params(1),
    )(u, u, dhc, dhc, dy, dy, cw, pw, ps)


def _attn_fwd(q, kv, n_seq, seq, n_mem):
    t, d = q.shape
    dh = d // XATTN_HEADS
    tq = min(1024, seq)
    nq = seq // tq
    scale = dh ** -0.5

    def body(q_ref, kv_ref, o_ref):
        for h in range(XATTN_HEADS):
            cols = slice(h * dh, (h + 1) * dh)
            s = _dot(q_ref[:, cols], kv_ref[:, cols], _NT) * scale
            e = jnp.exp(s - jnp.max(s, axis=-1, keepdims=True))
            p = e / jnp.sum(e, axis=-1, keepdims=True)
            o_ref[:, cols] = _dot(p.astype(_ACT), kv_ref[:, d + h * dh:d + (h + 1) * dh], _NN).astype(o_ref.dtype)

    qs = pl.BlockSpec((tq, d), lambda b, i: (b * nq + i, 0))
    return pl.pallas_call(
        body, name="attn_fwd", grid=(n_seq, nq), in_specs=[qs, pl.BlockSpec((n_mem, 2 * d), lambda b, i: (b, 0))],
        out_specs=qs, out_shape=jax.ShapeDtypeStruct((t, d), _ACT), compiler_params=_params(2),
    )(q, kv)


def _attn_bwd(q, kv, do, n_seq, seq, n_mem):
    t, d = q.shape
    dh = d // XATTN_HEADS
    tq = min(1024, seq)
    nq = seq // tq
    scale = dh ** -0.5

    def body(q_ref, kv_ref, do_ref, dq_ref, dkv_ref, acc):
        i = pl.program_id(1)

        @pl.when(i == 0)
        def _():
            acc[...] = jnp.zeros_like(acc)

        for h in range(XATTN_HEADS):
            cols = slice(h * dh, (h + 1) * dh)
            vcols = slice(d + h * dh, d + (h + 1) * dh)
            qv = q_ref[:, cols]
            kh = kv_ref[:, cols]
            dov = do_ref[:, cols]
            s = _dot(qv, kh, _NT) * scale
            e = jnp.exp(s - jnp.max(s, axis=-1, keepdims=True))
            p = e / jnp.sum(e, axis=-1, keepdims=True)
            dp = _dot(dov, kv_ref[:, vcols], _NT)
            ds = (p * (dp - jnp.sum(dp * p, axis=-1, keepdims=True)) * scale).astype(_ACT)
            dq_ref[:, cols] = _dot(ds, kh, _NN).astype(dq_ref.dtype)
            acc[:, cols] += _dot(ds, qv, _TN)
            acc[:, vcols] += _dot(p.astype(_ACT), dov, _TN)

        @pl.when(i == nq - 1)
        def _():
            dkv_ref[...] = acc[...].astype(dkv_ref.dtype)

    qs = pl.BlockSpec((tq, d), lambda b, i: (b * nq + i, 0))
    ms = pl.BlockSpec((n_mem, 2 * d), lambda b, i: (b, 0))
    return pl.pallas_call(
        body, name="attn_bwd", grid=(n_seq, nq), in_specs=[qs, ms, qs], out_specs=[qs, ms],
        out_shape=[jax.ShapeDtypeStruct((t, d), _ACT), jax.ShapeDtypeStruct((n_seq * n_mem, 2 * d), _ACT)],
        scratch_shapes=[pltpu.VMEM((n_mem, 2 * d), f32)], compiler_params=_params(2),
    )(q, kv, do)


_FFN_ROWS = 2048
_FFN_COLS = 256
_FFN_HALO = 16


def _window(buf, g, start, rows):
    return buf[g, pl.ds(start, rows + 8), :]


def _taps3(win, rows):
    return [_rows_from(win, 6 + k, rows) for k in range(3)]


def _conv3(b_ref, w_ref, taps):
    acc = b_ref[...] + w_ref[0:1, :] * taps[0]
    for k in (1, 2):
        acc = acc + w_ref[k:k + 1, :] * taps[k]
    return acc


def _ffn_gate_fwd(up, fw, fb, seq):
    _, t, f = up.shape
    tm = min(_FFN_ROWS, seq)
    tps = seq // tm
    tc = _FFN_COLS
    nc = f // tc
    hl = _FFN_HALO

    def body(up_ref, uph_ref, wg_ref, wv_ref, bg_ref, bv_ref, a_ref):
        i = pl.program_id(1)
        before = uph_ref[...]
        before = jnp.where(i % tps == 0, jnp.zeros_like(before), before)

        def chunk(r0, wins):
            conv = []
            for g, (w_ref, b_ref) in enumerate(((wg_ref, bg_ref), (wv_ref, bv_ref))):
                conv.append(_conv3(b_ref, w_ref, _taps3(wins[g].astype(f32)[hl - 8:, :], _CHUNK)))
            gate, val = conv
            a_ref[pl.ds(r0, _CHUNK), :] = (gate * _sigmoid(gate) * val).astype(a_ref.dtype)

        chunk(0, [jnp.concatenate([before[g], up_ref[g, 0:_CHUNK, :]], axis=0) for g in range(2)])

        def later(ci, carry):
            r0 = pl.multiple_of(ci * _CHUNK, _CHUNK)
            chunk(r0, [up_ref[g, pl.ds(r0 - hl, _CHUNK + hl), :] for g in range(2)])
            return carry

        lax.fori_loop(1, tm // _CHUNK, later, 0)

    hb = tm // hl
    return pl.pallas_call(
        body, name="ffn_gate_fwd", grid=(nc, t // tm),
        in_specs=[pl.BlockSpec((2, tm, tc), lambda j, i: (0, i, j)),
                  pl.BlockSpec((2, hl, tc), lambda j, i: (0, jnp.maximum(i * hb - 1, 0), j)),
                  pl.BlockSpec((8, tc), lambda j, i: (0, j)), pl.BlockSpec((8, tc), lambda j, i: (0, nc + j)),
                  pl.BlockSpec((1, tc), lambda j, i: (0, j)), pl.BlockSpec((1, tc), lambda j, i: (0, nc + j))],
        out_specs=pl.BlockSpec((tm, tc), lambda j, i: (i, j)),
        out_shape=jax.ShapeDtypeStruct((t, f), _ACT), compiler_params=_params(2),
    )(up, up, fw, fw, fb, fb)


def _ffn_gate_bwd(up, da, fw, fb, seq):
    _, t, f = up.shape
    tm = min(_FFN_ROWS, seq)
    tps = seq // tm
    tc = _FFN_COLS
    nc = f // tc
    hl = _FFN_HALO

    def body(up_ref, uph_ref, upn_ref, da_ref, dan_ref, wg_ref, wv_ref, bg_ref, bv_ref,
             dup_ref, sg_ref, sv_ref, dbuf, sums):
        i = pl.program_id(1)
        at_end = i % tps == tps - 1

        @pl.when(i == 0)
        def _():
            sg_ref[...] = jnp.zeros_like(sg_ref)
            sv_ref[...] = jnp.zeros_like(sv_ref)

        sums[...] = jnp.zeros_like(sums)
        before = uph_ref[...]
        before = jnp.where(i % tps == 0, jnp.zeros_like(before), before)
        after = upn_ref[...]
        after = jnp.where(at_end, jnp.zeros_like(after), after)
        w_refs = (wg_ref, wv_ref)
        b_refs = (bg_ref, bv_ref)

        def grads(r0, rows, wins, dav, count):
            taps = [_taps3(wins[g].astype(f32)[hl - 8:, :], rows) for g in range(2)]
            gate, val = [_conv3(b_refs[g], w_refs[g], taps[g]) for g in range(2)]
            sg = _sigmoid(gate)
            douts = (dav * val * (sg * (1.0 + gate * (1.0 - sg))), dav * (gate * sg))
            for g in range(2):
                dbuf[g, pl.ds(r0, rows), :] = douts[g]
                if count:
                    sums[g, 0] += douts[g].reshape(rows // 8, 8, tc).sum(axis=0)
                    for k in range(3):
                        sums[g, 1 + k] += (douts[g] * taps[g][k]).reshape(rows // 8, 8, tc).sum(axis=0)

        grads(0, _CHUNK, [jnp.concatenate([before[g], up_ref[g, 0:_CHUNK, :]], axis=0) for g in range(2)],
              da_ref[0:_CHUNK, :].astype(f32), True)

        def first(ci, carry):
            r0 = pl.multiple_of(ci * _CHUNK, _CHUNK)
            grads(r0, _CHUNK, [up_ref[g, pl.ds(r0 - hl, _CHUNK + hl), :] for g in range(2)],
                  da_ref[pl.ds(r0, _CHUNK), :].astype(f32), True)
            return carry

        lax.fori_loop(1, tm // _CHUNK, first, 0)
        da_after = dan_ref[...].astype(f32)
        grads(tm, hl, [jnp.concatenate([up_ref[g, tm - hl:tm, :], after[g]], axis=0) for g in range(2)],
              jnp.where(at_end, jnp.zeros_like(da_after), da_after), False)

        def second(ci, carry):
            r0 = pl.multiple_of(ci * _CHUNK, _CHUNK)
            for g in range(2):
                win = _window(dbuf, g, r0, _CHUNK)
                acc = jnp.zeros((_CHUNK, tc), f32)
                for k in range(3):
                    acc = acc + w_refs[g][k:k + 1, :] * _rows_from(win, 2 - k, _CHUNK)
                dup_ref[g, pl.ds(r0, _CHUNK), :] = acc.astype(dup_ref.dtype)
            return carry

        lax.fori_loop(0, tm // _CHUNK, second, 0)
        for g, s_ref in enumerate((sg_ref, sv_ref)):
            for r in range(4):
                s_ref[r:r + 1, :] += jnp.sum(sums[g, r], axis=0, keepdims=True)

    hb = tm // hl
    n_halo = t // hl
    return pl.pallas_call(
        body, name="ffn_gate_bwd", grid=(nc, t // tm),
        in_specs=[pl.BlockSpec((2, tm, tc), lambda j, i: (0, i, j)),
                  pl.BlockSpec((2, hl, tc), lambda j, i: (0, jnp.maximum(i * hb - 1, 0), j)),
                  pl.BlockSpec((2, hl, tc), lambda j, i: (0, jnp.minimum((i + 1) * hb, n_halo - 1), j)),
                  pl.BlockSpec((tm, tc), lambda j, i: (i, j)),
                  pl.BlockSpec((hl, tc), lambda j, i: (jnp.minimum((i + 1) * hb, n_halo - 1), j)),
                  pl.BlockSpec((8, tc), lambda j, i: (0, j)), pl.BlockSpec((8, tc), lambda j, i: (0, nc + j)),
                  pl.BlockSpec((1, tc), lambda j, i: (0, j)), pl.BlockSpec((1, tc), lambda j, i: (0, nc + j))],
        out_specs=[pl.BlockSpec((2, tm, tc), lambda j, i: (0, i, j)),
                   pl.BlockSpec((8, tc), lambda j, i: (0, j)), pl.BlockSpec((8, tc), lambda j, i: (0, j))],
        out_shape=[jax.ShapeDtypeStruct((2, t, f), _ACT), jax.ShapeDtypeStruct((8, f), f32), jax.ShapeDtypeStruct((8, f), f32)],
        scratch_shapes=[pltpu.VMEM((2, tm + hl, tc), f32), pltpu.VMEM((2, 4, 8, tc), f32)],
        compiler_params=_params(2),
    )(up, up, up, da, da, fw, fw, fb, fb)


def _adamw_math(w, g, m, v):
    m = ADAM_B1 * m + (1.0 - ADAM_B1) * g
    v = ADAM_B2 * v + (1.0 - ADAM_B2) * (g * g)
    m_hat = m / (1.0 - ADAM_B1 ** ADAM_STEP)
    v_hat = v / (1.0 - ADAM_B2 ** ADAM_STEP)
    delta = -ADAM_LR * (m_hat / (jnp.sqrt(v_hat) + ADAM_EPS) + ADAM_WD * w)
    return delta, m, v


def _adamw_shards(quads):
    n = len(quads)
    steps = 8

    def body(*refs):
        for p in range(n):
            w_ref, g_ref, m_ref, v_ref = refs[4 * p:4 * p + 4]
            go_ref, d_ref, mo_ref, vo_ref = refs[4 * n + 4 * p:4 * n + 4 * p + 4]
            gv = g_ref[...]
            d, mn, vn = _adamw_math(w_ref[...], gv, m_ref[...], v_ref[...])
            go_ref[...] = gv
            d_ref[...] = d
            mo_ref[...] = mn
            vo_ref[...] = vn

    in_specs, out_specs, out_shape = [], [], []
    for w, _, _, _ in quads:
        _, r, c = w.shape
        s3 = pl.BlockSpec((None, r // steps, c), lambda i: (0, i, 0))
        in_specs += [s3, pl.BlockSpec((r // steps, c), lambda i: (i, 0)), s3, s3]
        out_specs += [s3] * 4
        out_shape += [jax.ShapeDtypeStruct(w.shape, f32)] * 4
    outs = pl.pallas_call(
        body, name="adamw_shards", grid=(steps,), in_specs=in_specs, out_specs=out_specs, out_shape=out_shape,
        compiler_params=_params(1),
    )(*[a for q in quads for a in q])
    return [tuple(outs[4 * p:4 * p + 4]) for p in range(n)]


def _adamw_small(quads):
    n = len(quads)

    def body(*refs):
        ins, outs = refs[:4 * n], refs[4 * n:]
        for p in range(n):
            w_ref, g_ref, m_ref, v_ref = ins[4 * p:4 * p + 4]
            d, mn, vn = _adamw_math(w_ref[...], g_ref[...], m_ref[...], v_ref[...])
            outs[3 * p][...] = d
            outs[3 * p + 1][...] = mn
            outs[3 * p + 2][...] = vn

    flat = [a for q in quads for a in q]
    shapes = [jax.ShapeDtypeStruct(q[0].shape, f32) for q in quads for _ in range(3)]
    outs = pl.pallas_call(
        body, name="adamw_small", in_specs=[_VMEM] * (4 * n), out_specs=[_VMEM] * (3 * n), out_shape=shapes,
        compiler_params=pltpu.CompilerParams(vmem_limit_bytes=_VMEM_LIMIT_BYTES),
    )(*flat)
    return [tuple(outs[3 * p:3 * p + 3]) for p in range(n)]


def _sum_partials(name, place, grads, got):
    nw = len(grads)
    steps = 2

    def body(place_ref, *refs):
        for w in range(nw):
            own_ref, got_ref, f_ref = refs[w], refs[nw + w], refs[2 * nw + w]
            s = own_ref[...].astype(f32)
            for k in range(got[w].shape[0]):
                s = s + got_ref[k].astype(f32)
            f_ref[...] = s

    own_specs, got_specs, out_specs, out_shape = [], [], [], []
    for g, l in zip(grads, got):
        _, r, c = g.shape
        tr = r // steps
        own_specs.append(pl.BlockSpec((None, tr, c), lambda i, p: (2 * p[0] + p[1], i, 0)))
        got_specs.append(pl.BlockSpec((l.shape[0], tr, c), lambda i, p: (0, i, 0)))
        out_specs.append(pl.BlockSpec((None, tr, c), lambda i, p: (p[1], i, 0)))
        out_shape.append(jax.ShapeDtypeStruct((2, r, c), f32))
    grid_spec = pltpu.PrefetchScalarGridSpec(num_scalar_prefetch=1, grid=(steps,), in_specs=own_specs + got_specs, out_specs=out_specs)
    return pl.pallas_call(body, name=name, grid_spec=grid_spec, out_shape=out_shape,
                          compiler_params=_params(1))(place, *grads, *got)


def _place():
    return lax.axis_index("x"), lax.axis_index("y"), lax.axis_index("c")


def _other_chips(x, y):
    return [(1 - x, y), (x, 1 - y), (1 - x, 1 - y)]


def _remote(src, dst, send_sem, recv_sem, to):
    return pltpu.make_async_remote_copy(src_ref=src, dst_ref=dst, send_sem=send_sem, recv_sem=recv_sem,
                                        device_id=to, device_id_type=_MESH)


def _place_shards(place, shards, col_sharded):
    n = len(shards)
    steps = 4

    def body(place_ref, *refs):
        for src, dst in zip(refs[:n], refs[n:]):
            dst[...] = src[...].astype(dst.dtype)

    in_specs, out_specs, out_shape = [], [], []
    for w, col in zip(shards, col_sharded):
        r, cs = w.shape
        tr = r // steps
        in_specs.append(pl.BlockSpec((tr, cs), lambda i, p: (i, 0)))
        if col:
            out_specs.append(pl.BlockSpec((tr, cs), lambda i, p: (i, p[0])))
            out_shape.append(jax.ShapeDtypeStruct((r, 4 * cs), _ACT))
        else:
            out_specs.append(pl.BlockSpec((tr, cs), lambda i, p: (p[0] * steps + i, 0)))
            out_shape.append(jax.ShapeDtypeStruct((4 * r, cs), _ACT))
    grid_spec = pltpu.PrefetchScalarGridSpec(num_scalar_prefetch=1, grid=(steps,), in_specs=in_specs, out_specs=out_specs)
    return pl.pallas_call(body, name="place_shards", grid_spec=grid_spec, out_shape=out_shape,
                          compiler_params=_params(1))(place, *shards)


def _shard_of(ref, col_sharded, s):
    rows, cols = ref.shape
    if col_sharded:
        return ref.at[:, pl.ds(s * (cols // 4), cols // 4)]
    return ref.at[pl.ds(s * (rows // 4), rows // 4), :]


def _part_of(ref, col_sharded, whole, s, h):
    if whole:
        return _shard_of(ref, col_sharded, s)
    rows, cols = ref.shape
    if col_sharded:
        return ref.at[pl.ds(h * (rows // 2), rows // 2), pl.ds(s * (cols // 4), cols // 4)]
    return ref.at[pl.ds((2 * s + h) * (rows // 8), rows // 8), :]


def _allgather_start(bufs, col_sharded, whole, groups):
    n = len(bufs)
    ng = len(groups)

    def body(*refs):
        out = refs[n:2 * n]
        sems = refs[2 * n:]
        x, y, c = _place()
        for g, members in enumerate(groups):
            for i, w in enumerate(members):
                mine = _part_of(out[w], col_sharded[w], whole[w], 2 * x + y, c)
                for j, chip in enumerate(_other_chips(x, y)):
                    _remote(mine, mine, sems[2 * g].at[3 * i + j], sems[2 * g + 1].at[3 * i + j], (*chip, c)).start()

    sem_shapes = [pltpu.SemaphoreType.DMA((3 * len(m),)) for m in groups for _ in range(2)]
    outs = pl.pallas_call(
        body, name="allgather_start", in_specs=[_HBM] * n, out_specs=[_HBM] * n + [_SEM] * (2 * ng),
        out_shape=[pltpu.HBM(b.shape, b.dtype) for b in bufs] + sem_shapes,
        input_output_aliases={i: i for i in range(n)},
        compiler_params=pltpu.CompilerParams(has_side_effects=_EFFECT),
    )(*[pltpu.with_memory_space_constraint(b, pltpu.HBM) for b in bufs])
    return list(outs[:n]), [(outs[n + 2 * g], outs[n + 2 * g + 1]) for g in range(ng)]


def _allgather_relay(name, bufs, col_sharded, whole, sems, after):
    n = len(bufs)

    def body(*refs):
        buf = refs[:n]
        send, recv = refs[n], refs[n + 1]
        out = refs[n + 3:2 * n + 3]
        to_sibling, from_sibling = refs[2 * n + 3:]
        x, y, c = _place()
        for i in range(n):
            mine = _part_of(buf[i], col_sharded[i], whole[i], 2 * x + y, c)
            for j, chip in enumerate(_other_chips(x, y)):
                landed = _part_of(buf[i], col_sharded[i], whole[i], 2 * chip[0] + chip[1], c)
                cp = _remote(mine, landed, send.at[3 * i + j], recv.at[3 * i + j], (*chip, c))
                cp.wait_send()
                cp.wait_recv()
        for i in range(n):
            if not whole[i]:
                for j, chip in enumerate(_other_chips(x, y)):
                    landed = _part_of(out[i], col_sharded[i], False, 2 * chip[0] + chip[1], c)
                    _remote(landed, landed, to_sibling.at[3 * i + j], from_sibling.at[3 * i + j], (x, y, 1 - c)).start()

    outs = pl.pallas_call(
        body, name=name, in_specs=[_HBM] * n + [_SEM, _SEM, _ANY], out_specs=[_HBM] * n + [_SEM, _SEM],
        out_shape=[pltpu.HBM(b.shape, b.dtype) for b in bufs] + [pltpu.SemaphoreType.DMA((3 * n,))] * 2,
        input_output_aliases={i: i for i in range(n)},
        compiler_params=pltpu.CompilerParams(has_side_effects=_EFFECT),
    )(*bufs, *sems, after)
    return list(outs[:n]), (outs[n], outs[n + 1])


def _allgather_wait(name, bufs, col_sharded, whole, sems, after):
    n = len(bufs)

    def body(*refs):
        buf = refs[:n]
        to_sibling, from_sibling = refs[n], refs[n + 1]
        x, y, c = _place()
        for i in range(n):
            if not whole[i]:
                for j, chip in enumerate(_other_chips(x, y)):
                    sent = _part_of(buf[i], col_sharded[i], False, 2 * chip[0] + chip[1], c)
                    landed = _part_of(buf[i], col_sharded[i], False, 2 * chip[0] + chip[1], 1 - c)
                    cp = _remote(sent, landed, to_sibling.at[3 * i + j], from_sibling.at[3 * i + j], (x, y, 1 - c))
                    cp.wait_send()
                    cp.wait_recv()

    return pl.pallas_call(
        body, name=name, in_specs=[_HBM] * n + [_SEM, _SEM, _ANY], out_specs=[_HBM] * n,
        out_shape=[pltpu.HBM(b.shape, b.dtype) for b in bufs],
        input_output_aliases={i: i for i in range(n)},
        compiler_params=pltpu.CompilerParams(has_side_effects=_EFFECT),
    )(*bufs, *sems, after)


def _other_devices(x, y, c):
    flips = [(bx, by, bc) for bx in (0, 1) for by in (0, 1) for bc in (0, 1)][1:]
    return [(1 - x if bx else x, 1 - y if by else y, 1 - c if bc else c) for bx, by, bc in flips]


def _grad_exchange_start(name, grads):
    nw = len(grads)
    lands = [lax.empty((7,) + g.shape[1:], g.dtype) for g in grads]

    def body(*refs):
        src = refs[2 * nw:3 * nw]
        got = refs[3 * nw:4 * nw]
        send, recv, token = refs[4 * nw:]
        x, y, c = _place()
        for w in range(nw):
            for k, (px, py, pc) in enumerate(_other_devices(x, y, c)):
                _remote(src[w].at[4 * px + 2 * py + pc], got[w].at[k], send.at[7 * w + k], recv.at[7 * w + k], (px, py, pc)).start()
        token[...] = jnp.zeros_like(token)

    outs = pl.pallas_call(
        body, name=name, in_specs=[_HBM] * (2 * nw), out_specs=[_HBM] * (2 * nw) + [_SEM, _SEM, _VMEM],
        out_shape=[pltpu.HBM(a.shape, a.dtype) for a in list(grads) + lands]
        + [pltpu.SemaphoreType.DMA((7 * nw,)), pltpu.SemaphoreType.DMA((7 * nw,)), jax.ShapeDtypeStruct((8, 128), f32)],
        input_output_aliases={i: i for i in range(2 * nw)},
        compiler_params=pltpu.CompilerParams(has_side_effects=_EFFECT),
    )(*[pltpu.with_memory_space_constraint(a, pltpu.HBM) for a in list(grads) + lands])
    return list(outs[:nw]), list(outs[nw:2 * nw]), (outs[2 * nw], outs[2 * nw + 1]), outs[2 * nw + 2]


def _grad_exchange_wait(name, grads, got, sems, after):
    nw = len(grads)

    def body(*refs):
        src = refs[:nw]
        land = refs[nw:2 * nw]
        send, recv = refs[2 * nw], refs[2 * nw + 1]
        x, y, c = _place()
        for w in range(nw):
            for k, (px, py, pc) in enumerate(_other_devices(x, y, c)):
                cp = _remote(src[w].at[4 * px + 2 * py + pc], land[w].at[k], send.at[7 * w + k], recv.at[7 * w + k], (px, py, pc))
                cp.wait_send()
                cp.wait_recv()

    outs = pl.pallas_call(
        body, name=name, in_specs=[_HBM] * (2 * nw) + [_SEM, _SEM, _ANY], out_specs=[_HBM] * (2 * nw),
        out_shape=[pltpu.HBM(a.shape, a.dtype) for a in list(grads) + list(got)],
        input_output_aliases={i: i for i in range(2 * nw)},
        compiler_params=pltpu.CompilerParams(has_side_effects=_EFFECT),
    )(*grads, *got, *sems, after)
    return list(outs[:nw]), list(outs[nw:])


def _swap_halves_start(finals):
    nw = len(finals)

    def body(*refs):
        buf = refs[nw:2 * nw]
        send, recv, token = refs[2 * nw:]
        x, y, c = _place()
        for w in range(nw):
            _remote(buf[w].at[c], buf[w].at[c], send.at[w], recv.at[w], (x, y, 1 - c)).start()
        token[...] = jnp.zeros_like(token)

    outs = pl.pallas_call(
        body, name="rs_swap_start", in_specs=[_HBM] * nw, out_specs=[_HBM] * nw + [_SEM, _SEM, _VMEM],
        out_shape=[pltpu.HBM(g.shape, g.dtype) for g in finals] + [pltpu.SemaphoreType.DMA((nw,))] * 2
        + [jax.ShapeDtypeStruct((8, 128), f32)],
        input_output_aliases={i: i for i in range(nw)},
        compiler_params=pltpu.CompilerParams(has_side_effects=_EFFECT),
    )(*[pltpu.with_memory_space_constraint(g, pltpu.HBM) for g in finals])
    return list(outs[:nw]), (outs[nw], outs[nw + 1]), outs[nw + 2]


def _swap_halves_wait(bufs, sems, after):
    nw = len(bufs)

    def body(*refs):
        buf = refs[:nw]
        send, recv = refs[nw], refs[nw + 1]
        x, y, c = _place()
        for w in range(nw):
            cp = _remote(buf[w].at[c], buf[w].at[1 - c], send.at[w], recv.at[w], (x, y, 1 - c))
            cp.wait_send()
            cp.wait_recv()

    return pl.pallas_call(
        body, name="rs_swap_wait", in_specs=[_HBM] * nw + [_SEM, _SEM, _ANY], out_specs=[_HBM] * nw,
        out_shape=[pltpu.HBM(g.shape, g.dtype) for g in bufs],
        input_output_aliases={i: i for i in range(nw)},
        compiler_params=pltpu.CompilerParams(has_side_effects=_EFFECT),
    )(*bufs, *sems, after)


def _half_slices(shape, h):
    rows, cols = shape
    if cols % 256 == 0:
        return (slice(None), slice(h * (cols // 2), (h + 1) * (cols // 2)))
    return (slice(h * (rows // 2), (h + 1) * (rows // 2)), slice(None))


def _allreduce_small(parts):
    n = len(parts)

    def body(*refs):
        src = refs[:n]
        out = refs[n:2 * n]
        sib = refs[2 * n:3 * n]
        chip_sum = refs[3 * n:4 * n]
        slots = refs[4 * n:5 * n]
        pair_send, pair_recv, ici_send, ici_recv, swap_send, swap_recv = refs[5 * n:]
        x, y, c = _place()
        me_chip = 2 * x + y
        chips = _other_chips(x, y)
        pairs = [_remote(src[a], sib[a], pair_send.at[a], pair_recv.at[a], (x, y, 1 - c)) for a in range(n)]
        for rc in pairs:
            rc.start()
        for a in range(n):
            pairs[a].wait_recv()
            chip_sum[a][...] = src[a][...] + sib[a][...]
        for h in (0, 1):
            @pl.when(c == h)
            def _():
                sends = []
                for a in range(n):
                    idx = _half_slices(parts[a].shape, h)
                    for j, chip in enumerate(chips):
                        rc = _remote(chip_sum[a].at[idx], slots[a].at[me_chip].at[idx], ici_send.at[3 * a + j], ici_recv.at[3 * a + j], (*chip, h))
                        rc.start()
                        sends.append(rc)
                    slots[a][(me_chip,) + idx] = chip_sum[a][idx]
                for a in range(n):
                    idx = _half_slices(parts[a].shape, h)
                    for j, chip in enumerate(chips):
                        landed = slots[a].at[2 * chip[0] + chip[1]].at[idx]
                        _remote(landed, landed, ici_send.at[3 * a + j], ici_recv.at[3 * a + j], (x, y, c)).wait_recv()
                    total = slots[a][(0,) + idx]
                    for s in range(1, 4):
                        total = total + slots[a][(s,) + idx]
                    out[a][idx] = total
                    rc = _remote(out[a].at[idx], out[a].at[idx], swap_send.at[a], swap_recv.at[a], (x, y, 1 - h))
                    rc.start()
                    sends.append(rc)
                for a in range(n):
                    other = out[a].at[_half_slices(parts[a].shape, 1 - h)]
                    _remote(other, other, swap_send.at[a], swap_recv.at[a], (x, y, c)).wait_recv()
                for rc in sends:
                    rc.wait_send()
        for rc in pairs:
            rc.wait_send()

    return pl.pallas_call(
        body, name="allreduce_small", in_specs=[_VMEM] * n, out_specs=[_VMEM] * n,
        out_shape=[jax.ShapeDtypeStruct(p.shape, f32) for p in parts],
        scratch_shapes=[pltpu.VMEM(p.shape, f32) for p in parts] * 2 + [pltpu.VMEM((4,) + p.shape, f32) for p in parts]
        + [pltpu.SemaphoreType.DMA((n,)), pltpu.SemaphoreType.DMA((n,)), pltpu.SemaphoreType.DMA((3 * n,)),
           pltpu.SemaphoreType.DMA((3 * n,)), pltpu.SemaphoreType.DMA((n,)), pltpu.SemaphoreType.DMA((n,))],
        compiler_params=pltpu.CompilerParams(vmem_limit_bytes=_VMEM_LIMIT_BYTES),
    )(*parts)


def _local_step(x, mem, tgt, g_mix, g_xattn, g_mem, g_ffn, g_final, cb, lg, lb, pw, ps, fb, relay, weights, reduce, n_seq, seq, n_mem):
    t, d = x.shape
    f = fb.shape[1] // 2
    c = cb.shape[1]
    h1 = _rms_fwd("norm_mix", x, g_mix)
    relay(0, h1)
    w_in, cw, fw = weights(0, h1)
    u = _mm_nn("proj_in", h1, w_in, _ACT, w_in.shape[1])
    y, hc = _mix_fwd(u, cw, cb, lg, lb, pw, ps, seq)
    relay(1, y)
    w_out, w_q, w_kv, w_o = weights(1, y)
    x1, h2 = _proj_residual_norm("proj_out", y, w_out, x, g_xattn)
    q = _mm_nn("proj_q", h2, w_q, _ACT, d)
    mem_n = _rms_fwd("norm_mem", mem, g_mem)
    kv = _mm_nn("proj_kv", mem_n, w_kv, _ACT, 2 * d)
    o = _attn_fwd(q, kv, n_seq, seq, n_mem)
    relay(2, o)
    x2, h3 = _proj_residual_norm("proj_o", o, w_o, x1, g_ffn)
    w_up, w_down = weights(2, h3)
    up = _mm_nn("proj_up", h3, w_up, _ACT, f, split_out=True)
    a = _ffn_gate_fwd(up, fw, fb, seq)
    dx3, dx3b, dg_final, loss = _proj_loss_bwd("proj_down", a, w_down, x2, g_final, tgt)
    da = _mm_nt("d_act", dx3b, w_down, _ACT)
    gw_down = _mm_tn_rows("dw_down", a, dx3b, f // 2, d // 2)
    dup, sums_g, sums_v = _ffn_gate_bwd(up, da, fw, fb, seq)
    gw_up = _mm_tn_pieces("dw_up", h3, dup, f // 2, t)
    token = reduce(0, [gw_down.reshape(8, -1, d), gw_up])
    dx2, dx2b, dg_ffn = _dproj_rms_bwd("d_h3", dup, w_up, x2, g_ffn + token, dx3)
    do = _mm_nt("d_o", dx2b, w_o, _ACT)
    gw_o = _mm_tn_rows("dw_o", o, dx2b, d, d // 2)
    dq, dkv = _attn_bwd(q, kv, do, n_seq, seq, n_mem)
    gw_q = _mm_tn_rows("dw_q", h2, dq, d, d // 2)
    gw_kv = _mm_tn_pieces("dw_kv", mem_n, dkv, d // 2, mem.shape[0])
    dmem_n = _mm_nt("d_mem_n", dkv, w_kv, f32)
    dg_mem = _rms_gain_grad("norm_mem_bwd", mem, dmem_n)
    dx1, dx1b, dg_xattn = _dproj_rms_bwd("d_h2", dq, w_q, x1, g_xattn, dx2)
    dy = _mm_nt("d_y", dx1b, w_out, _ACT)
    gw_out = _mm_tn_rows("dw_out", y, dx1b, d, d // 2)
    token = reduce(1, [gw_o.reshape(8, -1, d), gw_q.reshape(8, -1, d), gw_kv, gw_out.reshape(8, -1, d)])
    dhc, sums_norm = _mix_bwd_norm(hc, dy, lg + token, lb, seq)
    du, d_cw, d_ps, d_pw = _mix_bwd_taps(u, dhc, dy, cw, pw, ps, seq)
    gw_in = _mm_tn_pieces("dw_in", h1, du, c * 3 // 4, t)
    token = reduce(2, [gw_in])
    grad_x, dg_mix = _dproj_rms_bwd("d_h1", du, w_in, x, g_mix + token, dx1, storage_copy=False)
    zero_row = jnp.zeros((1, d), f32)
    gains = jnp.concatenate([dg_mix, dg_xattn, dg_mem, dg_ffn, dg_final, jnp.pad(loss, ((0, 0), (0, d - 1))), zero_row, zero_row], axis=0)
    conv_rows = jnp.concatenate([sums_norm[2:3], sums_norm[0:1], sums_norm[1:2], d_ps[0:1], jnp.zeros((4, c), f32)], axis=0)
    ffn_rows = jnp.concatenate([sums_g, sums_v], axis=1)
    small = [gains, conv_rows, d_pw.reshape(-1, d_pw.shape[-1]), ffn_rows, d_cw]
    return grad_x, small


def kernel(x, mem, norm_mix_g, w_in, conv_dw_w, conv_dw_b, conv_ln_g, conv_ln_b, pool_w, pool_scale, w_out, norm_xattn_g, norm_mem_g, w_q, w_kv, w_o, norm_ffn_g, w_up, ffn_dw_w, ffn_dw_b, w_down, norm_final_g, loss_target, m_norm_mix_g, m_w_in, m_conv_dw_w, m_conv_dw_b, m_conv_ln_g, m_conv_ln_b, m_pool_w, m_pool_scale, m_w_out, m_norm_xattn_g, m_norm_mem_g, m_w_q, m_w_kv, m_w_o, m_norm_ffn_g, m_w_up, m_ffn_dw_w, m_ffn_dw_b, m_w_down, m_norm_final_g, v_norm_mix_g, v_w_in, v_conv_dw_w, v_conv_dw_b, v_conv_ln_g, v_conv_ln_b, v_pool_w, v_pool_scale, v_w_out, v_norm_xattn_g, v_norm_mem_g, v_w_q, v_w_kv, v_w_o, v_norm_ffn_g, v_w_up, v_ffn_dw_w, v_ffn_dw_b, v_w_down, v_norm_final_g):
    n_seq, seq, d = x.shape
    n_mem = mem.shape[1]
    chip = 2 * lax.axis_index("x") + lax.axis_index("y")

    place = jnp.stack([chip, lax.axis_index("c")]).astype(jnp.int32)

    col_w = [w_in, w_kv, w_up]
    row_w = [w_out, w_q, w_o, w_down]
    col_flags = [True] * 3 + [False] * 4 + [True] * 2
    kw = conv_dw_w.shape[1]

    def padded_in_place(shard, rows):
        full = jnp.zeros((rows, 4 * shard.shape[1]), shard.dtype)
        return lax.dynamic_update_slice(full, shard, (0, chip * shard.shape[1]))

    bufs = list(_place_shards(place, [w[0] for w in col_w + row_w], col_flags[:7]))
    bufs += [padded_in_place(conv_dw_w[0], _HALO), padded_in_place(ffn_dw_w[0], 8)]
    groups = [[0, 7, 8], [3, 4, 1, 5], [2, 6]]
    whole = [False] * 7 + [True] * 2
    bufs, sems = _allgather_start(bufs, col_flags, whole, groups)
    relayed = {}

    def relay(g, after):
        members = groups[g]
        relayed[g] = _allgather_relay("allgather_relay_%d" % g, [bufs[i] for i in members], [col_flags[i] for i in members],
                                      [whole[i] for i in members], sems[g], after)

    def weights(g, after):
        members = groups[g]
        group_bufs, sibling_sems = relayed[g]
        return _allgather_wait("allgather_wait_%d" % g, group_bufs, [col_flags[i] for i in members],
                               [whole[i] for i in members], sibling_sems, after)

    names = ["w_in", "w_kv", "w_up", "w_out", "w_q", "w_o", "w_down"]
    reduce_groups = [["w_down", "w_up"], ["w_o", "w_q", "w_kv", "w_out"], ["w_in"]]
    in_flight = {}

    def reduce(g, grads):
        grads, lands, rs_sems, token = _grad_exchange_start("rs_start_%d" % g, grads)
        in_flight[g] = (grads, lands, rs_sems)
        return token[0:1, 0:1]

    grad_x, small = _local_step(
        x.reshape(n_seq * seq, d), mem.reshape(n_seq * n_mem, d), loss_target.reshape(n_seq * seq, d),
        norm_mix_g, norm_xattn_g, norm_mem_g, norm_ffn_g, norm_final_g.reshape(1, d),
        conv_dw_b, conv_ln_g, conv_ln_b, pool_w[0], pool_scale, ffn_dw_b, relay, weights, reduce, n_seq, seq, n_mem)

    landed = {}
    for g, members in enumerate(reduce_groups):
        grads, lands, rs_sems = in_flight[g]
        grads, lands = _grad_exchange_wait("rs_wait_%d" % g, grads, lands, rs_sems, grad_x)
        landed.update(zip(members, zip(grads, lands)))
    finals = _sum_partials("rs_sum", place, [landed[n][0] for n in names], [landed[n][1] for n in names])
    finals, swap_sems, token = _swap_halves_start(finals)

    gains, conv_rows, d_pw, ffn_rows, d_cw = _allreduce_small([small[0] + token[0:1, 0:1]] + small[1:])
    loss = gains[5, 0]
    shard_grads = _swap_halves_wait(finals, swap_sems, gains)

    outs = {}
    big_w = dict(zip(names, col_w + row_w))
    big_m = dict(w_in=m_w_in, w_kv=m_w_kv, w_up=m_w_up, w_out=m_w_out, w_q=m_w_q, w_o=m_w_o, w_down=m_w_down)
    big_v = dict(w_in=v_w_in, w_kv=v_w_kv, w_up=v_w_up, w_out=v_w_out, w_q=v_w_q, w_o=v_w_o, w_down=v_w_down)
    big_quads = [(big_w[n], g.reshape(big_w[n].shape[1:]), big_m[n], big_v[n]) for n, g in zip(names, shard_grads)]
    outs.update(zip(names, _adamw_shards(big_quads)))

    f2 = ffn_dw_b.shape[1]
    cs_c = conv_dw_w.shape[2]
    cs_f = ffn_dw_w.shape[2]
    g_cw = lax.dynamic_slice(d_cw, (0, chip * cs_c), (kw, cs_c)).reshape(conv_dw_w.shape)
    g_fw = lax.dynamic_slice(ffn_rows, (1, chip * cs_f), (ffn_dw_w.shape[1], cs_f)).reshape(ffn_dw_w.shape)
    small_params = [
        ("norm_mix_g", norm_mix_g, gains[0:1], m_norm_mix_g, v_norm_mix_g),
        ("conv_dw_w", conv_dw_w, g_cw, m_conv_dw_w, v_conv_dw_w),
        ("conv_dw_b", conv_dw_b, conv_rows[0:1], m_conv_dw_b, v_conv_dw_b),
        ("conv_ln_g", conv_ln_g, conv_rows[1:2], m_conv_ln_g, v_conv_ln_g),
        ("conv_ln_b", conv_ln_b, conv_rows[2:3], m_conv_ln_b, v_conv_ln_b),
        ("pool_w", pool_w, d_pw.reshape(pool_w.shape), m_pool_w, v_pool_w),
        ("pool_scale", pool_scale, conv_rows[3:4], m_pool_scale, v_pool_scale),
        ("norm_xattn_g", norm_xattn_g, gains[1:2], m_norm_xattn_g, v_norm_xattn_g),
        ("norm_mem_g", norm_mem_g, gains[2:3], m_norm_mem_g, v_norm_mem_g),
        ("norm_ffn_g", norm_ffn_g, gains[3:4], m_norm_ffn_g, v_norm_ffn_g),
        ("ffn_dw_w", ffn_dw_w, g_fw, m_ffn_dw_w, v_ffn_dw_w),
        ("ffn_dw_b", ffn_dw_b, ffn_rows[0:1, :f2], m_ffn_dw_b, v_ffn_dw_b),
        ("norm_final_g", norm_final_g.reshape(1, d), gains[4:5], m_norm_final_g.reshape(1, d), v_norm_final_g.reshape(1, d)),
    ]
    quads = []
    for _, w, g, m, v in small_params:
        shape2 = (-1, w.shape[-1])
        quads.append((w.reshape(shape2), g.reshape(shape2), m.reshape(shape2), v.reshape(shape2)))
    for (n, w, g, _, _), (delta, new_m, new_v) in zip(small_params, _adamw_small(quads)):
        shape = norm_final_g.shape if n == "norm_final_g" else w.shape
        outs[n] = (g.reshape(shape), delta.reshape(shape), new_m.reshape(shape), new_v.reshape(shape))

    order = ["norm_mix_g", "w_in", "conv_dw_w", "conv_dw_b", "conv_ln_g", "conv_ln_b", "pool_w", "pool_scale", "w_out",
             "norm_xattn_g", "norm_mem_g", "w_q", "w_kv", "w_o", "norm_ffn_g", "w_up", "ffn_dw_w", "ffn_dw_b", "w_down",
             "norm_final_g"]
    return (loss, grad_x.reshape(x.shape), *[outs[n][0] for n in order], *[outs[n][1] for n in order],
            *[outs[n][2] for n in order], *[outs[n][3] for n in order])
```

```python
import functools

import jax
import jax.numpy as jnp
from jax import lax
from jax.experimental import pallas as pl
from jax.experimental.pallas import tpu as pltpu

f32 = jnp.float32
_ACT = jnp.bfloat16

EPS = 1e-6
POOL_WINDOWS = (2, 4, 8, 16)
XATTN_HEADS = 4
ADAM_LR = 0.001
ADAM_B1 = 0.9
ADAM_B2 = 0.999
ADAM_EPS = 1e-08
ADAM_WD = 0.01
ADAM_STEP = 10

_VMEM_LIMIT_BYTES = 56 * 1024 * 1024
_MESH = pl.DeviceIdType.MESH
_ANY = pl.BlockSpec(memory_space=pl.ANY)
_VMEM = pl.BlockSpec(memory_space=pltpu.VMEM)
_HBM = pl.BlockSpec(memory_space=pltpu.HBM)
_SEM = pl.BlockSpec(memory_space=pltpu.SEMAPHORE)
_EFFECT = pltpu.SideEffectType.DATAFLOW_SIDE_EFFECTING

_NN = (((1,), (0,)), ((), ()))
_NT = (((1,), (1,)), ((), ()))
_TN = (((0,), (0,)), ((), ()))


def _params(n_grid):
    return pltpu.CompilerParams(dimension_semantics=("arbitrary",) * n_grid, vmem_limit_bytes=_VMEM_LIMIT_BYTES)


def _sigmoid(v):
    return 1.0 / (1.0 + jnp.exp(-v))


def _dot(a, b, dims):
    return lax.dot_general(a, b, dims, preferred_element_type=f32)


def _mm(name, a, b, *, dims, grid, a_spec, b_spec, o_spec, out_shape, nk, acc_shape=None, res=None, res_spec=None):
    def body(*refs):
        if res is None:
            a_ref, b_ref, o_ref, *scratch = refs
            r_ref = None
        else:
            a_ref, b_ref, r_ref, o_ref, *scratch = refs
        p = _dot(a_ref[...], b_ref[...], dims)

        def finish(v):
            if r_ref is not None:
                v = v + r_ref[...]
            o_ref[...] = v.astype(o_ref.dtype)

        if nk == 1:
            finish(p)
        else:
            acc = scratch[0]
            k = pl.program_id(2)

            @pl.when(k == 0)
            def _():
                acc[...] = p

            @pl.when(k > 0)
            def _():
                acc[...] += p

            @pl.when(k == nk - 1)
            def _():
                finish(acc[...])

    ins = [a, b] + ([] if res is None else [res])
    specs = [a_spec, b_spec] + ([] if res is None else [res_spec])
    return pl.pallas_call(
        body, name=name, grid=grid, in_specs=specs, out_specs=o_spec, out_shape=out_shape,
        scratch_shapes=[pltpu.VMEM(acc_shape, f32)] if nk > 1 else [], compiler_params=_params(3),
    )(*ins)


_NARROW = 2816


def _row_tile(m, width=_NARROW + 1):
    return min(1024 if width <= _NARROW else 512, m)


def _mm_nn(name, a, b, out_dtype, tn, res=None, split_out=False):
    m, k = a.shape
    n = b.shape[1]
    tm = _row_tile(m, max(k, tn))
    if split_out:
        out_shape = jax.ShapeDtypeStruct((n // tn, m, tn), out_dtype)
        o_spec = pl.BlockSpec((None, tm, tn), lambda j, i, kk: (j, i, 0))
    else:
        out_shape = jax.ShapeDtypeStruct((m, n), out_dtype)
        o_spec = pl.BlockSpec((tm, tn), lambda j, i, kk: (i, j))
    return _mm(
        name, a, b, dims=_NN, grid=(n // tn, m // tm, 1), nk=1,
        a_spec=pl.BlockSpec((tm, k), lambda j, i, kk: (i, 0)),
        b_spec=pl.BlockSpec((k, tn), lambda j, i, kk: (0, j)),
        o_spec=o_spec, out_shape=out_shape, res=res,
        res_spec=pl.BlockSpec((tm, tn), lambda j, i, kk: (i, j)),
    )


def _mm_nt(name, a, b, out_dtype):
    n, kc = b.shape
    m = a.shape[0]
    tm = _row_tile(m, max(n, kc))
    return _mm(
        name, a, b, dims=_NT, grid=(m // tm, 1, 1), nk=1,
        a_spec=pl.BlockSpec((tm, kc), lambda i, j, k: (i, 0)),
        b_spec=pl.BlockSpec((n, kc), lambda i, j, k: (0, 0), pipeline_mode=pl.Buffered(1)),
        o_spec=pl.BlockSpec((tm, n), lambda i, j, k: (i, 0)),
        out_shape=jax.ShapeDtypeStruct((m, n), out_dtype),
    )


def _mm_tn_rows(name, a, b, tka, tn):
    m, ka = a.shape
    nb = b.shape[1]
    return _mm(
        name, a, b, dims=_TN, grid=(ka // tka, nb // tn, 1), nk=1,
        a_spec=pl.BlockSpec((m, tka), lambda i, j, k: (0, i)),
        b_spec=pl.BlockSpec((m, tn), lambda i, j, k: (0, j)),
        o_spec=pl.BlockSpec((tka, tn), lambda i, j, k: (i, j)),
        out_shape=jax.ShapeDtypeStruct((ka, nb), _ACT),
    )


def _mm_tn_pieces(name, a, b, cs, tt):
    m, ka = a.shape
    nk = m // tt
    if b.ndim == 3:
        b_spec = pl.BlockSpec((None, tt, cs), lambda i, j, k: (j // 2, k, j % 2))
    else:
        b_spec = pl.BlockSpec((tt, cs), lambda i, j, k: (k, j))
    return _mm(
        name, a, b, dims=_TN, grid=(2, 4, nk), nk=nk, acc_shape=(ka // 2, cs),
        a_spec=pl.BlockSpec((tt, ka // 2), lambda i, j, k: (k, i)), b_spec=b_spec,
        o_spec=pl.BlockSpec((None, ka // 2, cs), lambda i, j, k: (2 * j + i, 0, 0)),
        out_shape=jax.ShapeDtypeStruct((8, ka // 2, cs), _ACT),
    )


def _rms_fwd(name, x, g):
    t, d = x.shape
    tm = _row_tile(t, d)

    def body(x_ref, g_ref, h_ref):
        xv = x_ref[...]
        r = lax.rsqrt(jnp.mean(xv * xv, axis=-1, keepdims=True) + EPS)
        h_ref[...] = (xv * r * g_ref[...]).astype(h_ref.dtype)

    return pl.pallas_call(
        body, name=name, grid=(t // tm,),
        in_specs=[pl.BlockSpec((tm, d), lambda i: (i, 0)), pl.BlockSpec((1, d), lambda i: (0, 0))],
        out_specs=pl.BlockSpec((tm, d), lambda i: (i, 0)), out_shape=jax.ShapeDtypeStruct((t, d), _ACT),
        compiler_params=_params(1),
    )(x, g)


def _fused_rows(name, a, b, product, a_spec, tm, extras, extra_specs, out_shape, out_specs, epilogue):
    ne = len(extras)

    def body(a_ref, b_ref, *refs):
        epilogue(product(a_ref, b_ref), refs[:ne], refs[ne:])

    m = extras[0].shape[0]
    return pl.pallas_call(
        body, name=name, grid=(m // tm,),
        in_specs=[a_spec, pl.BlockSpec(b.shape, lambda i: (0, 0), pipeline_mode=pl.Buffered(1)), *extra_specs],
        out_specs=out_specs, out_shape=out_shape, compiler_params=_params(1),
    )(a, b, *extras)


def _proj_residual_norm(name, a, b, res, g):
    m, k = a.shape
    d = b.shape[1]
    tm = _row_tile(m, max(k, d))

    def epilogue(p, ins, outs):
        xv = p + ins[0][...]
        outs[0][...] = xv
        r = lax.rsqrt(jnp.mean(xv * xv, axis=-1, keepdims=True) + EPS)
        outs[1][...] = (xv * r * ins[1][...]).astype(outs[1].dtype)

    row = pl.BlockSpec((tm, d), lambda i: (i, 0))
    return _fused_rows(
        name, a, b, lambda a_ref, b_ref: _dot(a_ref[...], b_ref[...], _NN), pl.BlockSpec((tm, k), lambda i: (i, 0)), tm,
        [res, g], [row, pl.BlockSpec((1, d), lambda i: (0, 0))],
        [jax.ShapeDtypeStruct((m, d), f32), jax.ShapeDtypeStruct((m, d), _ACT)], [row, row], epilogue)


def _dproj_rms_bwd(name, a, b, x, g, dres, storage_copy=True):
    m, d = x.shape
    if a.ndim == 3:
        nh, _, kh = a.shape
        tm = _row_tile(m, nh * kh)
        a_spec = pl.BlockSpec((nh, tm, kh), lambda i: (0, i, 0))

        def product(a_ref, b_ref):
            p = _dot(a_ref[0], b_ref[:, 0:kh], _NT)
            for h in range(1, nh):
                p = p + _dot(a_ref[h], b_ref[:, h * kh:(h + 1) * kh], _NT)
            return p
    else:
        tm = _row_tile(m, max(a.shape[1], d))
        a_spec = pl.BlockSpec((tm, a.shape[1]), lambda i: (i, 0))

        def product(a_ref, b_ref):
            return _dot(a_ref[...], b_ref[...], _NT)

    def epilogue(dhv, ins, outs):
        x_ref, g_ref, dres_ref = ins
        dg_ref = outs[-1]

        @pl.when(pl.program_id(0) == 0)
        def _():
            dg_ref[...] = jnp.zeros_like(dg_ref)

        xv = x_ref[...]
        r = lax.rsqrt(jnp.mean(xv * xv, axis=-1, keepdims=True) + EPS)
        xn = xv * r
        dxn = dhv * g_ref[...]
        dx = r * (dxn - xn * jnp.mean(dxn * xn, axis=-1, keepdims=True)) + dres_ref[...]
        outs[0][...] = dx
        if storage_copy:
            outs[1][...] = dx.astype(outs[1].dtype)
        dg_ref[...] += jnp.sum(dhv * xn, axis=0, keepdims=True)

    row = pl.BlockSpec((tm, d), lambda i: (i, 0))
    vec = pl.BlockSpec((1, d), lambda i: (0, 0))
    copies = [jax.ShapeDtypeStruct((m, d), _ACT)] if storage_copy else []
    return _fused_rows(
        name, a, b, product, a_spec, tm, [x, g, dres], [row, vec, row],
        [jax.ShapeDtypeStruct((m, d), f32)] + copies + [jax.ShapeDtypeStruct((1, d), f32)],
        [row] * (1 + len(copies)) + [vec], epilogue)


def _proj_loss_bwd(name, a, b, res, g, tgt):
    m, k = a.shape
    d = b.shape[1]
    tm = _row_tile(m, max(k, d))

    def epilogue(p, ins, outs):
        res_ref, g_ref, t_ref = ins
        dx_ref, dxb_ref, dg_ref, loss_ref = outs

        @pl.when(pl.program_id(0) == 0)
        def _():
            dg_ref[...] = jnp.zeros_like(dg_ref)
            loss_ref[...] = jnp.zeros_like(loss_ref)

        xv = p + res_ref[...]
        gv = g_ref[...]
        r = lax.rsqrt(jnp.mean(xv * xv, axis=-1, keepdims=True) + EPS)
        xn = xv * r
        err = xn * gv - t_ref[...]
        loss_ref[...] += 0.5 * jnp.sum(jnp.mean(err * err, axis=-1, keepdims=True), axis=0, keepdims=True)
        dout = err * (1.0 / d)
        dxn = dout * gv
        dx = r * (dxn - xn * jnp.mean(dxn * xn, axis=-1, keepdims=True))
        dx_ref[...] = dx
        dxb_ref[...] = dx.astype(dxb_ref.dtype)
        dg_ref[...] += jnp.sum(dout * xn, axis=0, keepdims=True)

    row = pl.BlockSpec((tm, d), lambda i: (i, 0))
    vec = pl.BlockSpec((1, d), lambda i: (0, 0))
    return _fused_rows(
        name, a, b, lambda a_ref, b_ref: _dot(a_ref[...], b_ref[...], _NN), pl.BlockSpec((tm, k), lambda i: (i, 0)), tm,
        [res, g, tgt], [row, vec, row],
        [jax.ShapeDtypeStruct((m, d), f32), jax.ShapeDtypeStruct((m, d), _ACT), jax.ShapeDtypeStruct((1, d), f32),
         jax.ShapeDtypeStruct((1, 1), f32)],
        [row, row, vec, pl.BlockSpec((1, 1), lambda i: (0, 0))], epilogue)


def _rms_gain_grad(name, x, dh):
    t, d = x.shape
    tm = _row_tile(t)

    def body(x_ref, dh_ref, dg_ref):
        @pl.when(pl.program_id(0) == 0)
        def _():
            dg_ref[...] = jnp.zeros_like(dg_ref)

        xv = x_ref[...]
        r = lax.rsqrt(jnp.mean(xv * xv, axis=-1, keepdims=True) + EPS)
        dg_ref[...] += jnp.sum(dh_ref[...] * (xv * r), axis=0, keepdims=True)

    row = pl.BlockSpec((tm, d), lambda i: (i, 0))
    return pl.pallas_call(
        body, name=name, grid=(t // tm,), in_specs=[row, row], out_specs=pl.BlockSpec((1, d), lambda i: (0, 0)),
        out_shape=jax.ShapeDtypeStruct((1, d), f32), compiler_params=_params(1),
    )(x, dh)


_CONV_ROWS = 512
_CHUNK = 64
_HALO = 32


def _pool_counts(pos, w):
    return jnp.minimum(pos + 1.0, float(w))


def _rows_from(win, start, rows):
    if start % 8 == 0:
        return win[start:start + rows, :]
    n = win.shape[0]
    return pltpu.roll(win, n - start % 8, axis=0)[start - start % 8:start - start % 8 + rows, :]


def _tap_rows(buf, starts, rows):
    for residue in range(8):
        group = [(k, s) for k, s in starts.items() if s % 8 == residue]
        if group:
            lo = min(s for _, s in group) - residue
            hi = max(s for _, s in group) - residue + rows + (8 if residue else 0)
            win = buf[lo:hi, :]
            if residue:
                win = pltpu.roll(win, hi - lo - residue, axis=0)
            for k, s in group:
                yield k, win[s - residue - lo:s - residue - lo + rows, :]


def _mix_fwd(u, cw, cb, lg, lb, pw, ps, seq):
    t, c3 = u.shape
    c = c3 // 3
    kw = 31
    tm = min(_CONV_ROWS, seq)
    tps = seq // tm
    gd = c // len(POOL_WINDOWS)

    def body(u_ref, uh_ref, cw_ref, cb_ref, lg_ref, lb_ref, pw_ref, ps_ref, y_ref, hc_ref, hgbuf, pbuf):
        i = pl.program_id(0)
        keep = jnp.where(i % tps == 0, 0.0, 1.0)
        um = u_ref[...].astype(f32)
        uh = uh_ref[...].astype(f32) * keep
        hgbuf[0:_HALO, :] = uh[:, 0:c] * _sigmoid(uh[:, c:2 * c])
        hgbuf[_HALO:_HALO + tm, :] = um[:, 0:c] * _sigmoid(um[:, c:2 * c])
        pbuf[0:_HALO, :] = uh[:, 2 * c:]
        pbuf[_HALO:_HALO + tm, :] = um[:, 2 * c:]
        for r0 in range(0, tm, _CHUNK):
            acc = jnp.broadcast_to(cb_ref[...], (_CHUNK, c))
            for k, rows in _tap_rows(hgbuf, {k: r0 + _HALO - (kw - 1) + k for k in range(kw)}, _CHUNK):
                acc = acc + cw_ref[k:k + 1, :] * rows
            hc_ref[r0:r0 + _CHUNK, :] = acc
            mu = jnp.mean(acc, axis=-1, keepdims=True)
            xc = acc - mu
            var = jnp.mean(xc * xc, axis=-1, keepdims=True)
            hl = xc * lax.rsqrt(var + EPS) * lg_ref[...] + lb_ref[...]
            y_ref[r0:r0 + _CHUNK, 0:c] = (hl * _sigmoid(hl)).astype(y_ref.dtype)
        pos = ((i % tps) * tm).astype(f32) + lax.broadcasted_iota(jnp.int32, (tm, 1), 0).astype(f32)
        for gi, w in enumerate(POOL_WINDOWS):
            sl = slice(gi * gd, (gi + 1) * gd)
            v = pbuf[_HALO:_HALO + tm, sl]
            s = v
            for j in range(1, w):
                s = s + pbuf[_HALO - j:_HALO - j + tm, sl]
            pooled = s / _pool_counts(pos, w) - v
            mixed = _dot(pooled.astype(_ACT), pw_ref[gi].astype(_ACT), _NN)
            y_ref[:, c + gi * gd:c + (gi + 1) * gd] = (mixed * ps_ref[:, sl]).astype(y_ref.dtype)

    hb = tm // _HALO
    full = lambda shape: pl.BlockSpec(shape, lambda i: (0,) * len(shape))
    return pl.pallas_call(
        body, name="mix_fwd", grid=(t // tm,),
        in_specs=[pl.BlockSpec((tm, c3), lambda i: (i, 0)),
                  pl.BlockSpec((_HALO, c3), lambda i: (jnp.maximum(i * hb - 1, 0), 0)),
                  full((_HALO, c)), full((1, c)), full((1, c)), full((1, c)), full((len(POOL_WINDOWS), gd, gd)), full((1, c))],
        out_specs=[pl.BlockSpec((tm, 2 * c), lambda i: (i, 0)), pl.BlockSpec((tm, c), lambda i: (i, 0))],
        out_shape=[jax.ShapeDtypeStruct((t, 2 * c), _ACT), jax.ShapeDtypeStruct((t, c), f32)],
        scratch_shapes=[pltpu.VMEM((_HALO + tm, c), f32), pltpu.VMEM((_HALO + tm, c), f32)],
        compiler_params=_params(1),
    )(u, u, cw, cb, lg, lb, pw, ps)


def _mix_bwd_norm(hc, dy, lg, lb, seq):
    t, c = hc.shape
    tm = _row_tile(t, c)

    def body(hc_ref, dy_ref, lg_ref, lb_ref, dhc_ref, sums_ref):
        @pl.when(pl.program_id(0) == 0)
        def _():
            sums_ref[...] = jnp.zeros_like(sums_ref)

        hcv = hc_ref[...]
        mu = jnp.mean(hcv, axis=-1, keepdims=True)
        xc = hcv - mu
        rstd = lax.rsqrt(jnp.mean(xc * xc, axis=-1, keepdims=True) + EPS)
        n = xc * rstd
        hl = n * lg_ref[...] + lb_ref[...]
        sg = _sigmoid(hl)
        dhl = dy_ref[...].astype(f32) * (sg * (1.0 + hl * (1.0 - sg)))
        dn = dhl * lg_ref[...]
        dhc = rstd * (dn - jnp.mean(dn, axis=-1, keepdims=True) - n * jnp.mean(dn * n, axis=-1, keepdims=True))
        dhc_ref[...] = dhc
        sums_ref[0:1, :] += jnp.sum(dhl * n, axis=0, keepdims=True)
        sums_ref[1:2, :] += jnp.sum(dhl, axis=0, keepdims=True)
        sums_ref[2:3, :] += jnp.sum(dhc, axis=0, keepdims=True)

    row = pl.BlockSpec((tm, c), lambda i: (i, 0))
    vec = pl.BlockSpec((1, c), lambda i: (0, 0))
    return pl.pallas_call(
        body, name="mix_bwd_norm", grid=(t // tm,), in_specs=[row, row, vec, vec],
        out_specs=[row, pl.BlockSpec((8, c), lambda i: (0, 0))],
        out_shape=[jax.ShapeDtypeStruct((t, c), f32), jax.ShapeDtypeStruct((8, c), f32)],
        compiler_params=_params(1),
    )(hc, dy, lg, lb)


def _mix_bwd_taps(u, dhc, dy, cw, pw, ps, seq):
    t, c3 = u.shape
    c = c3 // 3
    kw = 31
    tm = min(_CONV_ROWS, seq)
    tps = seq // tm
    ng = len(POOL_WINDOWS)
    gd = c // ng
    nh = 16

    def body(u_ref, uh_ref, dhc_ref, dhcn_ref, dy_ref, dyn_ref, cw_ref, pw_ref, ps_ref,
             du_ref, dcw_ref, dps_ref, dpw_ref, hgbuf, dcbuf, pbuf, dpbuf):
        i = pl.program_id(0)
        keep_prev = jnp.where(i % tps == 0, 0.0, 1.0)
        keep_next = jnp.where(i % tps == tps - 1, 0.0, 1.0)

        @pl.when(i == 0)
        def _():
            dcw_ref[...] = jnp.zeros_like(dcw_ref)
            dps_ref[...] = jnp.zeros_like(dps_ref)
            dpw_ref[...] = jnp.zeros_like(dpw_ref)

        uh = uh_ref[...].astype(f32) * keep_prev
        hgbuf[0:_HALO, :] = uh[:, 0:c] * _sigmoid(uh[:, c:2 * c])
        pbuf[0:_HALO, :] = uh[:, 2 * c:]
        um = u_ref[...].astype(f32)
        hgbuf[_HALO:_HALO + tm, :] = um[:, 0:c] * _sigmoid(um[:, c:2 * c])
        pbuf[_HALO:_HALO + tm, :] = um[:, 2 * c:]
        dcbuf[0:tm, :] = dhc_ref[...]
        dcbuf[tm:tm + _HALO, :] = dhcn_ref[...] * keep_next
        tap_sums = [None] * kw
        for r0 in range(0, tm, _CHUNK):
            dh = dcbuf[r0:r0 + _CHUNK, :]
            acc = jnp.zeros((_CHUNK, c), f32)
            for k, rows in _tap_rows(hgbuf, {k: r0 + _HALO - (kw - 1) + k for k in range(kw)}, _CHUNK):
                part = (dh * rows).reshape(_CHUNK // 8, 8, c).sum(axis=0)
                tap_sums[k] = part if tap_sums[k] is None else tap_sums[k] + part
            for k, rows in _tap_rows(dcbuf, {k: r0 + (kw - 1) - k for k in range(kw)}, _CHUNK):
                acc = acc + cw_ref[k:k + 1, :] * rows
            val = u_ref[r0:r0 + _CHUNK, 0:c].astype(f32)
            sg = _sigmoid(u_ref[r0:r0 + _CHUNK, c:2 * c].astype(f32))
            du_ref[r0:r0 + _CHUNK, 0:c] = (acc * sg).astype(du_ref.dtype)
            du_ref[r0:r0 + _CHUNK, c:2 * c] = (acc * val * sg * (1.0 - sg)).astype(du_ref.dtype)
        for k in range(kw):
            dcw_ref[k:k + 1, :] += jnp.sum(tap_sums[k], axis=0, keepdims=True)
        base = ((i % tps) * tm).astype(f32)
        pos = base + lax.broadcasted_iota(jnp.int32, (tm, 1), 0).astype(f32)
        pos_next = base + float(tm) + lax.broadcasted_iota(jnp.int32, (nh, 1), 0).astype(f32)
        for gi, w in enumerate(POOL_WINDOWS):
            sl = slice(gi * gd, (gi + 1) * gd)
            v = pbuf[_HALO:_HALO + tm, sl]
            s = v
            for j in range(1, w):
                s = s + pbuf[_HALO - j:_HALO - j + tm, sl]
            cnt = _pool_counts(pos, w)
            pooled = (s / cnt - v).astype(_ACT)
            pwg = pw_ref[gi].astype(_ACT)
            mixed = _dot(pooled, pwg, _NN)
            dyp = dy_ref[:, sl].astype(f32)
            dps_ref[0:1, sl] += jnp.sum(dyp * mixed, axis=0, keepdims=True)
            dmix = (dyp * ps_ref[:, sl]).astype(_ACT)
            dpw_ref[gi] += _dot(pooled, dmix, _TN)
            dmix_next = (dyn_ref[:, sl].astype(f32) * ps_ref[:, sl] * keep_next).astype(_ACT)
            dpool = _dot(dmix, pwg, _NT)
            dpbuf[0:tm, sl] = dpool / cnt
            dpbuf[tm:tm + nh, sl] = _dot(dmix_next, pwg, _NT) / _pool_counts(pos_next, w)
            acc = -dpool
            for j in range(w):
                acc = acc + dpbuf[j:j + tm, sl]
            du_ref[:, 2 * c + gi * gd:2 * c + (gi + 1) * gd] = acc.astype(du_ref.dtype)

    hb = tm // _HALO
    n_halo = t // _HALO
    n_nh = t // nh
    full = lambda shape: pl.BlockSpec(shape, lambda i: (0,) * len(shape))
    return pl.pallas_call(
        body, name="mix_bwd_taps", grid=(t // tm,),
        in_specs=[pl.BlockSpec((tm, c3), lambda i: (i, 0)),
                  pl.BlockSpec((_HALO, c3), lambda i: (jnp.maximum(i * hb - 1, 0), 0)),
                  pl.BlockSpec((tm, c), lambda i: (i, 0)),
                  pl.BlockSpec((_HALO, c), lambda i: (jnp.minimum((i + 1) * hb, n_halo - 1), 0)),
                  pl.BlockSpec((tm, c), lambda i: (i, 1)),
                  pl.BlockSpec((nh, c), lambda i: (jnp.minimum((i + 1) * (tm // nh), n_nh - 1), 1)),
                  full((_HALO, c)), full((ng, gd, gd)), full((1, c))],
        out_specs=[pl.BlockSpec((tm, c3), lambda i: (i, 0)), full((_HALO, c)), full((8, c)), full((ng, gd, gd))],
        out_shape=[jax.ShapeDtypeStruct((t, c3), _ACT), jax.ShapeDtypeStruct((_HALO, c), f32),
                   jax.ShapeDtypeStruct((8, c), f32), jax.ShapeDtypeStruct((ng, gd, gd), f32)],
        scratch_shapes=[pltpu.VMEM((_HALO + tm, c), f32), pltpu.VMEM((tm + _HALO, c), f32),
                        pltpu.VMEM((_HALO + tm, c), f32), pltpu.VMEM((tm + nh, c), f32)],
        compiler_params=_params(1),
    )(u, u, dhc, dhc, dy, dy, cw, pw, ps)


def _attn_fwd(q, kv, n_seq, seq, n_mem):
    t, d = q.shape
    dh = d // XATTN_HEADS
    tq = min(1024, seq)
    nq = seq // tq
    scale = dh ** -0.5

    def body(q_ref, kv_ref, o_ref):
        for h in range(XATTN_HEADS):
            cols = slice(h * dh, (h + 1) * dh)
            s = _dot(q_ref[:, cols], kv_ref[:, cols], _NT) * scale
            e = jnp.exp(s - jnp.max(s, axis=-1, keepdims=True))
            p = e / jnp.sum(e, axis=-1, keepdims=True)
            o_ref[:, cols] = _dot(p.astype(_ACT), kv_ref[:, d + h * dh:d + (h + 1) * dh], _NN).astype(o_ref.dtype)

    qs = pl.BlockSpec((tq, d), lambda b, i: (b * nq + i, 0))
    return pl.pallas_call(
        body, name="attn_fwd", grid=(n_seq, nq), in_specs=[qs, pl.BlockSpec((n_mem, 2 * d), lambda b, i: (b, 0))],
        out_specs=qs, out_shape=jax.ShapeDtypeStruct((t, d), _ACT), compiler_params=_params(2),
    )(q, kv)


def _attn_bwd(q, kv, do, n_seq, seq, n_mem):
    t, d = q.shape
    dh = d // XATTN_HEADS
    tq = min(1024, seq)
    nq = seq // tq
    scale = dh ** -0.5

    def body(q_ref, kv_ref, do_ref, dq_ref, dkv_ref, acc):
        i = pl.program_id(1)

        @pl.when(i == 0)
        def _():
            acc[...] = jnp.zeros_like(acc)

        for h in range(XATTN_HEADS):
            cols = slice(h * dh, (h + 1) * dh)
            vcols = slice(d + h * dh, d + (h + 1) * dh)
            qv = q_ref[:, cols]
            kh = kv_ref[:, cols]
            dov = do_ref[:, cols]
            s = _dot(qv, kh, _NT) * scale
            e = jnp.exp(s - jnp.max(s, axis=-1, keepdims=True))
            p = e / jnp.sum(e, axis=-1, keepdims=True)
            dp = _dot(dov, kv_ref[:, vcols], _NT)
            ds = (p * (dp - jnp.sum(dp * p, axis=-1, keepdims=True)) * scale).astype(_ACT)
            dq_ref[:, cols] = _dot(ds, kh, _NN).astype(dq_ref.dtype)
            acc[:, cols] += _dot(ds, qv, _TN)
            acc[:, vcols] += _dot(p.astype(_ACT), dov, _TN)

        @pl.when(i == nq - 1)
        def _():
            dkv_ref[...] = acc[...].astype(dkv_ref.dtype)

    qs = pl.BlockSpec((tq, d), lambda b, i: (b * nq + i, 0))
    ms = pl.BlockSpec((n_mem, 2 * d), lambda b, i: (b, 0))
    return pl.pallas_call(
        body, name="attn_bwd", grid=(n_seq, nq), in_specs=[qs, ms, qs], out_specs=[qs, ms],
        out_shape=[jax.ShapeDtypeStruct((t, d), _ACT), jax.ShapeDtypeStruct((n_seq * n_mem, 2 * d), _ACT)],
        scratch_shapes=[pltpu.VMEM((n_mem, 2 * d), f32)], compiler_params=_params(2),
    )(q, kv, do)


_FFN_ROWS = 2048
_FFN_COLS = 256
_FFN_HALO = 16


def _window(buf, g, start, rows):
    return buf[g, pl.ds(start, rows + 8), :]


def _taps3(win, rows):
    return [_rows_from(win, 6 + k, rows) for k in range(3)]


def _conv3(b_ref, w_ref, taps):
    acc = b_ref[...] + w_ref[0:1, :] * taps[0]
    for k in (1, 2):
        acc = acc + w_ref[k:k + 1, :] * taps[k]
    return acc


def _ffn_gate_fwd(up, fw, fb, seq):
    _, t, f = up.shape
    tm = min(_FFN_ROWS, seq)
    tps = seq // tm
    tc = _FFN_COLS
    nc = f // tc
    hl = _FFN_HALO

    def body(up_ref, uph_ref, wg_ref, wv_ref, bg_ref, bv_ref, a_ref):
        i = pl.program_id(1)
        before = uph_ref[...]
        before = jnp.where(i % tps == 0, jnp.zeros_like(before), before)

        def chunk(r0, wins):
            conv = []
            for g, (w_ref, b_ref) in enumerate(((wg_ref, bg_ref), (wv_ref, bv_ref))):
                conv.append(_conv3(b_ref, w_ref, _taps3(wins[g].astype(f32)[hl - 8:, :], _CHUNK)))
            gate, val = conv
            a_ref[pl.ds(r0, _CHUNK), :] = (gate * _sigmoid(gate) * val).astype(a_ref.dtype)

        chunk(0, [jnp.concatenate([before[g], up_ref[g, 0:_CHUNK, :]], axis=0) for g in range(2)])

        def later(ci, carry):
            r0 = pl.multiple_of(ci * _CHUNK, _CHUNK)
            chunk(r0, [up_ref[g, pl.ds(r0 - hl, _CHUNK + hl), :] for g in range(2)])
            return carry

        lax.fori_loop(1, tm // _CHUNK, later, 0)

    hb = tm // hl
    return pl.pallas_call(
        body, name="ffn_gate_fwd", grid=(nc, t // tm),
        in_specs=[pl.BlockSpec((2, tm, tc), lambda j, i: (0, i, j)),
                  pl.BlockSpec((2, hl, tc), lambda j, i: (0, jnp.maximum(i * hb - 1, 0), j)),
                  pl.BlockSpec((8, tc), lambda j, i: (0, j)), pl.BlockSpec((8, tc), lambda j, i: (0, nc + j)),
                  pl.BlockSpec((1, tc), lambda j, i: (0, j)), pl.BlockSpec((1, tc), lambda j, i: (0, nc + j))],
        out_specs=pl.BlockSpec((tm, tc), lambda j, i: (i, j)),
        out_shape=jax.ShapeDtypeStruct((t, f), _ACT), compiler_params=_params(2),
    )(up, up, fw, fw, fb, fb)


def _ffn_gate_bwd(up, da, fw, fb, seq):
    _, t, f = up.shape
    tm = min(_FFN_ROWS, seq)
    tps = seq // tm
    tc = _FFN_COLS
    nc = f // tc
    hl = _FFN_HALO

    def body(up_ref, uph_ref, upn_ref, da_ref, dan_ref, wg_ref, wv_ref, bg_ref, bv_ref,
             dup_ref, sg_ref, sv_ref, dbuf, sums):
        i = pl.program_id(1)
        at_end = i % tps == tps - 1

        @pl.when(i == 0)
        def _():
            sg_ref[...] = jnp.zeros_like(sg_ref)
            sv_ref[...] = jnp.zeros_like(sv_ref)

        sums[...] = jnp.zeros_like(sums)
        before = uph_ref[...]
        before = jnp.where(i % tps == 0, jnp.zeros_like(before), before)
        after = upn_ref[...]
        after = jnp.where(at_end, jnp.zeros_like(after), after)
        w_refs = (wg_ref, wv_ref)
        b_refs = (bg_ref, bv_ref)

        def grads(r0, rows, wins, dav, count):
            taps = [_taps3(wins[g].astype(f32)[hl - 8:, :], rows) for g in range(2)]
            gate, val = [_conv3(b_refs[g], w_refs[g], taps[g]) for g in range(2)]
            sg = _sigmoid(gate)
            douts = (dav * val * (sg * (1.0 + gate * (1.0 - sg))), dav * (gate * sg))
            for g in range(2):
                dbuf[g, pl.ds(r0, rows), :] = douts[g]
                if count:
                    sums[g, 0] += douts[g].reshape(rows // 8, 8, tc).sum(axis=0)
                    for k in range(3):
                        sums[g, 1 + k] += (douts[g] * taps[g][k]).reshape(rows // 8, 8, tc).sum(axis=0)

        grads(0, _CHUNK, [jnp.concatenate([before[g], up_ref[g, 0:_CHUNK, :]], axis=0) for g in range(2)],
              da_ref[0:_CHUNK, :].astype(f32), True)

        def first(ci, carry):
            r0 = pl.multiple_of(ci * _CHUNK, _CHUNK)
            grads(r0, _CHUNK, [up_ref[g, pl.ds(r0 - hl, _CHUNK + hl), :] for g in range(2)],
                  da_ref[pl.ds(r0, _CHUNK), :].astype(f32), True)
            return carry

        lax.fori_loop(1, tm // _CHUNK, first, 0)
        da_after = dan_ref[...].astype(f32)
        grads(tm, hl, [jnp.concatenate([up_ref[g, tm - hl:tm, :], after[g]], axis=0) for g in range(2)],
              jnp.where(at_end, jnp.zeros_like(da_after), da_after), False)

        def second(ci, carry):
            r0 = pl.multiple_of(ci * _CHUNK, _CHUNK)
            for g in range(2):
                win = _window(dbuf, g, r0, _CHUNK)
                acc = jnp.zeros((_CHUNK, tc), f32)
                for k in range(3):
                    acc = acc + w_refs[g][k:k + 1, :] * _rows_from(win, 2 - k, _CHUNK)
                dup_ref[g, pl.ds(r0, _CHUNK), :] = acc.astype(dup_ref.dtype)
            return carry

        lax.fori_loop(0, tm // _CHUNK, second, 0)
        for g, s_ref in enumerate((sg_ref, sv_ref)):
            for r in range(4):
                s_ref[r:r + 1, :] += jnp.sum(sums[g, r], axis=0, keepdims=True)

    hb = tm // hl
    n_halo = t // hl
    return pl.pallas_call(
        body, name="ffn_gate_bwd", grid=(nc, t // tm),
        in_specs=[pl.BlockSpec((2, tm, tc), lambda j, i: (0, i, j)),
                  pl.BlockSpec((2, hl, tc), lambda j, i: (0, jnp.maximum(i * hb - 1, 0), j)),
                  pl.BlockSpec((2, hl, tc), lambda j, i: (0, jnp.minimum((i + 1) * hb, n_halo - 1), j)),
                  pl.BlockSpec((tm, tc), lambda j, i: (i, j)),
                  pl.BlockSpec((hl, tc), lambda j, i: (jnp.minimum((i + 1) * hb, n_halo - 1), j)),
                  pl.BlockSpec((8, tc), lambda j, i: (0, j)), pl.BlockSpec((8, tc), lambda j, i: (0, nc + j)),
                  pl.BlockSpec((1, tc), lambda j, i: (0, j)), pl.BlockSpec((1, tc), lambda j, i: (0, nc + j))],
        out_specs=[pl.BlockSpec((2, tm, tc), lambda j, i: (0, i, j)),
                   pl.BlockSpec((8, tc), lambda j, i: (0, j)), pl.BlockSpec((8, tc), lambda j, i: (0, j))],
        out_shape=[jax.ShapeDtypeStruct((2, t, f), _ACT), jax.ShapeDtypeStruct((8, f), f32), jax.ShapeDtypeStruct((8, f), f32)],
        scratch_shapes=[pltpu.VMEM((2, tm + hl, tc), f32), pltpu.VMEM((2, 4, 8, tc), f32)],
        compiler_params=_params(2),
    )(up, up, up, da, da, fw, fw, fb, fb)


def _adamw_math(w, g, m, v):
    m = ADAM_B1 * m + (1.0 - ADAM_B1) * g
    v = ADAM_B2 * v + (1.0 - ADAM_B2) * (g * g)
    m_hat = m / (1.0 - ADAM_B1 ** ADAM_STEP)
    v_hat = v / (1.0 - ADAM_B2 ** ADAM_STEP)
    delta = -ADAM_LR * (m_hat / (jnp.sqrt(v_hat) + ADAM_EPS) + ADAM_WD * w)
    return delta, m, v


def _adamw_shards(quads):
    n = len(quads)
    steps = 8

    def body(*refs):
        for p in range(n):
            w_ref, g_ref, m_ref, v_ref = refs[4 * p:4 * p + 4]
            go_ref, d_ref, mo_ref, vo_ref = refs[4 * n + 4 * p:4 * n + 4 * p + 4]
            gv = g_ref[...]
            d, mn, vn = _adamw_math(w_ref[...], gv, m_ref[...], v_ref[...])
            go_ref[...] = gv
            d_ref[...] = d
            mo_ref[...] = mn
            vo_ref[...] = vn

    in_specs, out_specs, out_shape = [], [], []
    for w, _, _, _ in quads:
        _, r, c = w.shape
        s3 = pl.BlockSpec((None, r // steps, c), lambda i: (0, i, 0))
        in_specs += [s3, pl.BlockSpec((r // steps, c), lambda i: (i, 0)), s3, s3]
        out_specs += [s3] * 4
        out_shape += [jax.ShapeDtypeStruct(w.shape, f32)] * 4
    outs = pl.pallas_call(
        body, name="adamw_shards", grid=(steps,), in_specs=in_specs, out_specs=out_specs, out_shape=out_shape,
        compiler_params=_params(1),
    )(*[a for q in quads for a in q])
    return [tuple(outs[4 * p:4 * p + 4]) for p in range(n)]


def _adamw_small(quads):
    n = len(quads)

    def body(*refs):
        ins, outs = refs[:4 * n], refs[4 * n:]
        for p in range(n):
            w_ref, g_ref, m_ref, v_ref = ins[4 * p:4 * p + 4]
            d, mn, vn = _adamw_math(w_ref[...], g_ref[...], m_ref[...], v_ref[...])
            outs[3 * p][...] = d
            outs[3 * p + 1][...] = mn
            outs[3 * p + 2][...] = vn

    flat = [a for q in quads for a in q]
    shapes = [jax.ShapeDtypeStruct(q[0].shape, f32) for q in quads for _ in range(3)]
    outs = pl.pallas_call(
        body, name="adamw_small", in_specs=[_VMEM] * (4 * n), out_specs=[_VMEM] * (3 * n), out_shape=shapes,
        compiler_params=pltpu.CompilerParams(vmem_limit_bytes=_VMEM_LIMIT_BYTES),
    )(*flat)
    return [tuple(outs[3 * p:3 * p + 3]) for p in range(n)]


def _sum_partials(name, place, grads, got):
    nw = len(grads)
    steps = 2

    def body(place_ref, *refs):
        for w in range(nw):
            own_ref, got_ref, f_ref = refs[w], refs[nw + w], refs[2 * nw + w]
            s = own_ref[...].astype(f32)
            for k in range(got[w].shape[0]):
                s = s + got_ref[k].astype(f32)
            f_ref[...] = s

    own_specs, got_specs, out_specs, out_shape = [], [], [], []
    for g, l in zip(grads, got):
        _, r, c = g.shape
        tr = r // steps
        own_specs.append(pl.BlockSpec((None, tr, c), lambda i, p: (2 * p[0] + p[1], i, 0)))
        got_specs.append(pl.BlockSpec((l.shape[0], tr, c), lambda i, p: (0, i, 0)))
        out_specs.append(pl.BlockSpec((None, tr, c), lambda i, p: (p[1], i, 0)))
        out_shape.append(jax.ShapeDtypeStruct((2, r, c), f32))
    grid_spec = pltpu.PrefetchScalarGridSpec(num_scalar_prefetch=1, grid=(steps,), in_specs=own_specs + got_specs, out_specs=out_specs)
    return pl.pallas_call(body, name=name, grid_spec=grid_spec, out_shape=out_shape,
                          compiler_params=_params(1))(place, *grads, *got)


def _place():
    return lax.axis_index("x"), lax.axis_index("y"), lax.axis_index("c")


def _other_chips(x, y):
    return [(1 - x, y), (x, 1 - y), (1 - x, 1 - y)]


def _remote(src, dst, send_sem, recv_sem, to):
    return pltpu.make_async_remote_copy(src_ref=src, dst_ref=dst, send_sem=send_sem, recv_sem=recv_sem,
                                        device_id=to, device_id_type=_MESH)


def _place_shards(name, place, shards, col_sharded):
    n = len(shards)
    steps = 4

    def body(place_ref, *refs):
        for src, dst in zip(refs[:n], refs[n:]):
            dst[...] = src[...].astype(dst.dtype)

    in_specs, out_specs, out_shape = [], [], []
    for w, col in zip(shards, col_sharded):
        r, cs = w.shape
        tr = r // steps
        in_specs.append(pl.BlockSpec((tr, cs), lambda i, p: (i, 0)))
        if col:
            out_specs.append(pl.BlockSpec((tr, cs), lambda i, p: (i, p[0])))
            out_shape.append(jax.ShapeDtypeStruct((r, 4 * cs), _ACT))
        else:
            out_specs.append(pl.BlockSpec((tr, cs), lambda i, p: (p[0] * steps + i, 0)))
            out_shape.append(jax.ShapeDtypeStruct((4 * r, cs), _ACT))
    grid_spec = pltpu.PrefetchScalarGridSpec(num_scalar_prefetch=1, grid=(steps,), in_specs=in_specs, out_specs=out_specs)
    return pl.pallas_call(body, name=name, grid_spec=grid_spec, out_shape=out_shape,
                          compiler_params=_params(1))(place, *shards)


def _shard_of(ref, col_sharded, s):
    rows, cols = ref.shape
    if col_sharded:
        return ref.at[:, pl.ds(s * (cols // 4), cols // 4)]
    return ref.at[pl.ds(s * (rows // 4), rows // 4), :]


def _part_of(ref, col_sharded, whole, s, h):
    if whole:
        return _shard_of(ref, col_sharded, s)
    rows, cols = ref.shape
    if col_sharded:
        return ref.at[pl.ds(h * (rows // 2), rows // 2), pl.ds(s * (cols // 4), cols // 4)]
    return ref.at[pl.ds((2 * s + h) * (rows // 8), rows // 8), :]


def _allgather_start(name, bufs, col_sharded, whole, groups):
    n = len(bufs)
    ng = len(groups)

    def body(*refs):
        out = refs[n:2 * n]
        sems = refs[2 * n:2 * n + 2 * ng]
        token = refs[2 * n + 2 * ng]
        x, y, c = _place()
        for g, members in enumerate(groups):
            for i, w in enumerate(members):
                mine = _part_of(out[w], col_sharded[w], whole[w], 2 * x + y, c)
                for j, chip in enumerate(_other_chips(x, y)):
                    _remote(mine, mine, sems[2 * g].at[3 * i + j], sems[2 * g + 1].at[3 * i + j], (*chip, c)).start()
        token[...] = jnp.zeros_like(token)

    sem_shapes = [pltpu.SemaphoreType.DMA((3 * len(m),)) for m in groups for _ in range(2)]
    outs = pl.pallas_call(
        body, name=name, in_specs=[_HBM] * n, out_specs=[_HBM] * n + [_SEM] * (2 * ng) + [_VMEM],
        out_shape=[pltpu.HBM(b.shape, b.dtype) for b in bufs] + sem_shapes + [jax.ShapeDtypeStruct((8, 128), f32)],
        input_output_aliases={i: i for i in range(n)},
        compiler_params=pltpu.CompilerParams(has_side_effects=_EFFECT),
    )(*[pltpu.with_memory_space_constraint(b, pltpu.HBM) for b in bufs])
    return list(outs[:n]), [(outs[n + 2 * g], outs[n + 2 * g + 1]) for g in range(ng)], outs[n + 2 * ng]


def _allgather_relay(name, bufs, col_sharded, whole, sems, after):
    n = len(bufs)

    def body(*refs):
        buf = refs[:n]
        send, recv = refs[n], refs[n + 1]
        out = refs[n + 3:2 * n + 3]
        to_sibling, from_sibling, token = refs[2 * n + 3:]
        token[...] = jnp.zeros_like(token)
        x, y, c = _place()
        for i in range(n):
            mine = _part_of(buf[i], col_sharded[i], whole[i], 2 * x + y, c)
            for j, chip in enumerate(_other_chips(x, y)):
                landed = _part_of(buf[i], col_sharded[i], whole[i], 2 * chip[0] + chip[1], c)
                cp = _remote(mine, landed, send.at[3 * i + j], recv.at[3 * i + j], (*chip, c))
                cp.wait_send()
                cp.wait_recv()
        for i in range(n):
            if not whole[i]:
                for j, chip in enumerate(_other_chips(x, y)):
                    landed = _part_of(out[i], col_sharded[i], False, 2 * chip[0] + chip[1], c)
                    _remote(landed, landed, to_sibling.at[3 * i + j], from_sibling.at[3 * i + j], (x, y, 1 - c)).start()

    outs = pl.pallas_call(
        body, name=name, in_specs=[_HBM] * n + [_SEM, _SEM, _ANY], out_specs=[_HBM] * n + [_SEM, _SEM, _VMEM],
        out_shape=[pltpu.HBM(b.shape, b.dtype) for b in bufs] + [pltpu.SemaphoreType.DMA((3 * n,))] * 2
        + [jax.ShapeDtypeStruct((8, 128), f32)],
        input_output_aliases={i: i for i in range(n)},
        compiler_params=pltpu.CompilerParams(has_side_effects=_EFFECT),
    )(*bufs, *sems, after)
    return list(outs[:n]), (outs[n], outs[n + 1]), outs[n + 2]


def _allgather_wait(name, bufs, col_sharded, whole, sems, after):
    n = len(bufs)

    def body(*refs):
        buf = refs[:n]
        to_sibling, from_sibling = refs[n], refs[n + 1]
        x, y, c = _place()
        for i in range(n):
            if not whole[i]:
                for j, chip in enumerate(_other_chips(x, y)):
                    sent = _part_of(buf[i], col_sharded[i], False, 2 * chip[0] + chip[1], c)
                    landed = _part_of(buf[i], col_sharded[i], False, 2 * chip[0] + chip[1], 1 - c)
                    cp = _remote(sent, landed, to_sibling.at[3 * i + j], from_sibling.at[3 * i + j], (x, y, 1 - c))
                    cp.wait_send()
                    cp.wait_recv()

    return pl.pallas_call(
        body, name=name, in_specs=[_HBM] * n + [_SEM, _SEM, _ANY], out_specs=[_HBM] * n,
        out_shape=[pltpu.HBM(b.shape, b.dtype) for b in bufs],
        input_output_aliases={i: i for i in range(n)},
        compiler_params=pltpu.CompilerParams(has_side_effects=_EFFECT),
    )(*bufs, *sems, after)


def _other_devices(x, y, c):
    flips = [(bx, by, bc) for bx in (0, 1) for by in (0, 1) for bc in (0, 1)][1:]
    return [(1 - x if bx else x, 1 - y if by else y, 1 - c if bc else c) for bx, by, bc in flips]


def _grad_exchange_start(name, grads):
    nw = len(grads)
    lands = [lax.empty((7,) + g.shape[1:], g.dtype) for g in grads]

    def body(*refs):
        src = refs[2 * nw:3 * nw]
        got = refs[3 * nw:4 * nw]
        send, recv, token = refs[4 * nw:]
        x, y, c = _place()
        for w in range(nw):
            for k, (px, py, pc) in enumerate(_other_devices(x, y, c)):
                _remote(src[w].at[4 * px + 2 * py + pc], got[w].at[k], send.at[7 * w + k], recv.at[7 * w + k], (px, py, pc)).start()
        token[...] = jnp.zeros_like(token)

    outs = pl.pallas_call(
        body, name=name, in_specs=[_HBM] * (2 * nw), out_specs=[_HBM] * (2 * nw) + [_SEM, _SEM, _VMEM],
        out_shape=[pltpu.HBM(a.shape, a.dtype) for a in list(grads) + lands]
        + [pltpu.SemaphoreType.DMA((7 * nw,)), pltpu.SemaphoreType.DMA((7 * nw,)), jax.ShapeDtypeStruct((8, 128), f32)],
        input_output_aliases={i: i for i in range(2 * nw)},
        compiler_params=pltpu.CompilerParams(has_side_effects=_EFFECT),
    )(*[pltpu.with_memory_space_constraint(a, pltpu.HBM) for a in list(grads) + lands])
    return list(outs[:nw]), list(outs[nw:2 * nw]), (outs[2 * nw], outs[2 * nw + 1]), outs[2 * nw + 2]


def _grad_exchange_wait(name, grads, got, sems, after):
    nw = len(grads)

    def body(*refs):
        src = refs[:nw]
        land = refs[nw:2 * nw]
        send, recv = refs[2 * nw], refs[2 * nw + 1]
        x, y, c = _place()
        for w in range(nw):
            for k, (px, py, pc) in enumerate(_other_devices(x, y, c)):
                cp = _remote(src[w].at[4 * px + 2 * py + pc], land[w].at[k], send.at[7 * w + k], recv.at[7 * w + k], (px, py, pc))
                cp.wait_send()
                cp.wait_recv()

    outs = pl.pallas_call(
        body, name=name, in_specs=[_HBM] * (2 * nw) + [_SEM, _SEM, _ANY], out_specs=[_HBM] * (2 * nw),
        out_shape=[pltpu.HBM(a.shape, a.dtype) for a in list(grads) + list(got)],
        input_output_aliases={i: i for i in range(2 * nw)},
        compiler_params=pltpu.CompilerParams(has_side_effects=_EFFECT),
    )(*grads, *got, *sems, after)
    return list(outs[:nw]), list(outs[nw:])


def _swap_halves_start(finals):
    nw = len(finals)

    def body(*refs):
        buf = refs[nw:2 * nw]
        send, recv, token = refs[2 * nw:]
        x, y, c = _place()
        for w in range(nw):
            _remote(buf[w].at[c], buf[w].at[c], send.at[w], recv.at[w], (x, y, 1 - c)).start()
        token[...] = jnp.zeros_like(token)

    outs = pl.pallas_call(
        body, name="rs_swap_start", in_specs=[_HBM] * nw, out_specs=[_HBM] * nw + [_SEM, _SEM, _VMEM],
        out_shape=[pltpu.HBM(g.shape, g.dtype) for g in finals] + [pltpu.SemaphoreType.DMA((nw,))] * 2
        + [jax.ShapeDtypeStruct((8, 128), f32)],
        input_output_aliases={i: i for i in range(nw)},
        compiler_params=pltpu.CompilerParams(has_side_effects=_EFFECT),
    )(*[pltpu.with_memory_space_constraint(g, pltpu.HBM) for g in finals])
    return list(outs[:nw]), (outs[nw], outs[nw + 1]), outs[nw + 2]


def _swap_halves_wait(bufs, sems, after):
    nw = len(bufs)

    def body(*refs):
        buf = refs[:nw]
        send, recv = refs[nw], refs[nw + 1]
        x, y, c = _place()
        for w in range(nw):
            cp = _remote(buf[w].at[c], buf[w].at[1 - c], send.at[w], recv.at[w], (x, y, 1 - c))
            cp.wait_send()
            cp.wait_recv()

    return pl.pallas_call(
        body, name="rs_swap_wait", in_specs=[_HBM] * nw + [_SEM, _SEM, _ANY], out_specs=[_HBM] * nw,
        out_shape=[pltpu.HBM(g.shape, g.dtype) for g in bufs],
        input_output_aliases={i: i for i in range(nw)},
        compiler_params=pltpu.CompilerParams(has_side_effects=_EFFECT),
    )(*bufs, *sems, after)


def _half_slices(shape, h):
    rows, cols = shape
    if cols % 256 == 0:
        return (slice(None), slice(h * (cols // 2), (h + 1) * (cols // 2)))
    return (slice(h * (rows // 2), (h + 1) * (rows // 2)), slice(None))


def _allreduce_small(parts):
    n = len(parts)

    def body(*refs):
        src = refs[:n]
        out = refs[n:2 * n]
        sib = refs[2 * n:3 * n]
        chip_sum = refs[3 * n:4 * n]
        slots = refs[4 * n:5 * n]
        pair_send, pair_recv, ici_send, ici_recv, swap_send, swap_recv = refs[5 * n:]
        x, y, c = _place()
        me_chip = 2 * x + y
        chips = _other_chips(x, y)
        pairs = [_remote(src[a], sib[a], pair_send.at[a], pair_recv.at[a], (x, y, 1 - c)) for a in range(n)]
        for rc in pairs:
            rc.start()
        for a in range(n):
            pairs[a].wait_recv()
            chip_sum[a][...] = src[a][...] + sib[a][...]
        for h in (0, 1):
            @pl.when(c == h)
            def _():
                sends = []
                for a in range(n):
                    idx = _half_slices(parts[a].shape, h)
                    for j, chip in enumerate(chips):
                        rc = _remote(chip_sum[a].at[idx], slots[a].at[me_chip].at[idx], ici_send.at[3 * a + j], ici_recv.at[3 * a + j], (*chip, h))
                        rc.start()
                        sends.append(rc)
                    slots[a][(me_chip,) + idx] = chip_sum[a][idx]
                for a in range(n):
                    idx = _half_slices(parts[a].shape, h)
                    for j, chip in enumerate(chips):
                        landed = slots[a].at[2 * chip[0] + chip[1]].at[idx]
                        _remote(landed, landed, ici_send.at[3 * a + j], ici_recv.at[3 * a + j], (x, y, c)).wait_recv()
                    total = slots[a][(0,) + idx]
                    for s in range(1, 4):
                        total = total + slots[a][(s,) + idx]
                    out[a][idx] = total
                    rc = _remote(out[a].at[idx], out[a].at[idx], swap_send.at[a], swap_recv.at[a], (x, y, 1 - h))
                    rc.start()
                    sends.append(rc)
                for a in range(n):
                    other = out[a].at[_half_slices(parts[a].shape, 1 - h)]
                    _remote(other, other, swap_send.at[a], swap_recv.at[a], (x, y, c)).wait_recv()
                for rc in sends:
                    rc.wait_send()
        for rc in pairs:
            rc.wait_send()

    return pl.pallas_call(
        body, name="allreduce_small", in_specs=[_VMEM] * n, out_specs=[_VMEM] * n,
        out_shape=[jax.ShapeDtypeStruct(p.shape, f32) for p in parts],
        scratch_shapes=[pltpu.VMEM(p.shape, f32) for p in parts] * 2 + [pltpu.VMEM((4,) + p.shape, f32) for p in parts]
        + [pltpu.SemaphoreType.DMA((n,)), pltpu.SemaphoreType.DMA((n,)), pltpu.SemaphoreType.DMA((3 * n,)),
           pltpu.SemaphoreType.DMA((3 * n,)), pltpu.SemaphoreType.DMA((n,)), pltpu.SemaphoreType.DMA((n,))],
        compiler_params=pltpu.CompilerParams(vmem_limit_bytes=_VMEM_LIMIT_BYTES),
    )(*parts)


def _local_step(x, mem, tgt, g_mix, g_xattn, g_mem, g_ffn, g_final, cb, lg, lb, pw, ps, fb, relay, weights, reduce, n_seq, seq, n_mem):
    t, d = x.shape
    f = fb.shape[1] // 2
    c = cb.shape[1]
    h1 = _rms_fwd("norm_mix", x, g_mix)
    relay(0, h1)
    w_in, cw, fw = weights(0, h1)
    u = _mm_nn("proj_in", h1, w_in, _ACT, w_in.shape[1])
    y, hc = _mix_fwd(u, cw, cb, lg, lb, pw, ps, seq)
    relay(1, y)
    w_out, w_q, w_kv, w_o = weights(1, y)
    x1, h2 = _proj_residual_norm("proj_out", y, w_out, x, g_xattn)
    q = _mm_nn("proj_q", h2, w_q, _ACT, d)
    mem_n = _rms_fwd("norm_mem", mem, g_mem)
    kv = _mm_nn("proj_kv", mem_n, w_kv, _ACT, 2 * d)
    o = _attn_fwd(q, kv, n_seq, seq, n_mem)
    token = relay(2, o)
    x2, h3 = _proj_residual_norm("proj_o", o, w_o, x1, g_ffn + token)
    w_up, w_down = weights(2, h3)
    up = _mm_nn("proj_up", h3, w_up, _ACT, f, split_out=True)
    a = _ffn_gate_fwd(up, fw, fb, seq)
    dx3, dx3b, dg_final, loss = _proj_loss_bwd("proj_down", a, w_down, x2, g_final, tgt)
    da = _mm_nt("d_act", dx3b, w_down, _ACT)
    gw_down = _mm_tn_rows("dw_down", a, dx3b, f // 2, d // 2)
    dup, sums_g, sums_v = _ffn_gate_bwd(up, da, fw, fb, seq)
    gw_up = _mm_tn_pieces("dw_up", h3, dup, f // 2, t)
    token = reduce(0, [gw_down.reshape(8, -1, d), gw_up])
    dx2, dx2b, dg_ffn = _dproj_rms_bwd("d_h3", dup, w_up, x2, g_ffn + token, dx3)
    do = _mm_nt("d_o", dx2b, w_o, _ACT)
    gw_o = _mm_tn_rows("dw_o", o, dx2b, d, d // 2)
    dq, dkv = _attn_bwd(q, kv, do, n_seq, seq, n_mem)
    gw_q = _mm_tn_rows("dw_q", h2, dq, d, d // 2)
    gw_kv = _mm_tn_pieces("dw_kv", mem_n, dkv, d // 2, mem.shape[0])
    dmem_n = _mm_nt("d_mem_n", dkv, w_kv, f32)
    dg_mem = _rms_gain_grad("norm_mem_bwd", mem, dmem_n)
    dx1, dx1b, dg_xattn = _dproj_rms_bwd("d_h2", dq, w_q, x1, g_xattn, dx2)
    dy = _mm_nt("d_y", dx1b, w_out, _ACT)
    gw_out = _mm_tn_rows("dw_out", y, dx1b, d, d // 2)
    token = reduce(1, [gw_o.reshape(8, -1, d), gw_q.reshape(8, -1, d), gw_kv, gw_out.reshape(8, -1, d)])
    dhc, sums_norm = _mix_bwd_norm(hc, dy, lg + token, lb, seq)
    du, d_cw, d_ps, d_pw = _mix_bwd_taps(u, dhc, dy, cw, pw, ps, seq)
    gw_in = _mm_tn_pieces("dw_in", h1, du, c * 3 // 4, t)
    token = reduce(2, [gw_in])
    grad_x, dg_mix = _dproj_rms_bwd("d_h1", du, w_in, x, g_mix + token, dx1, storage_copy=False)
    zero_row = jnp.zeros((1, d), f32)
    gains = jnp.concatenate([dg_mix, dg_xattn, dg_mem, dg_ffn, dg_final, jnp.pad(loss, ((0, 0), (0, d - 1))), zero_row, zero_row], axis=0)
    conv_rows = jnp.concatenate([sums_norm[2:3], sums_norm[0:1], sums_norm[1:2], d_ps[0:1], jnp.zeros((4, c), f32)], axis=0)
    ffn_rows = jnp.concatenate([sums_g, sums_v], axis=1)
    small = [gains, conv_rows, d_pw.reshape(-1, d_pw.shape[-1]), ffn_rows, d_cw]
    return grad_x, small


def kernel(x, mem, norm_mix_g, w_in, conv_dw_w, conv_dw_b, conv_ln_g, conv_ln_b, pool_w, pool_scale, w_out, norm_xattn_g, norm_mem_g, w_q, w_kv, w_o, norm_ffn_g, w_up, ffn_dw_w, ffn_dw_b, w_down, norm_final_g, loss_target, m_norm_mix_g, m_w_in, m_conv_dw_w, m_conv_dw_b, m_conv_ln_g, m_conv_ln_b, m_pool_w, m_pool_scale, m_w_out, m_norm_xattn_g, m_norm_mem_g, m_w_q, m_w_kv, m_w_o, m_norm_ffn_g, m_w_up, m_ffn_dw_w, m_ffn_dw_b, m_w_down, m_norm_final_g, v_norm_mix_g, v_w_in, v_conv_dw_w, v_conv_dw_b, v_conv_ln_g, v_conv_ln_b, v_pool_w, v_pool_scale, v_w_out, v_norm_xattn_g, v_norm_mem_g, v_w_q, v_w_kv, v_w_o, v_norm_ffn_g, v_w_up, v_ffn_dw_w, v_ffn_dw_b, v_w_down, v_norm_final_g):
    n_seq, seq, d = x.shape
    n_mem = mem.shape[1]
    chip = 2 * lax.axis_index("x") + lax.axis_index("y")

    place = jnp.stack([chip, lax.axis_index("c")]).astype(jnp.int32)

    col_w = [w_in, w_kv, w_up]
    row_w = [w_out, w_q, w_o, w_down]
    kw = conv_dw_w.shape[1]

    def padded_in_place(shard, rows):
        full = jnp.zeros((rows, 4 * shard.shape[1]), shard.dtype)
        return lax.dynamic_update_slice(full, shard, (0, chip * shard.shape[1]))

    first = list(_place_shards("place_w_in", place, [w_in[0]], [True]))
    first += [padded_in_place(conv_dw_w[0], _HALO), padded_in_place(ffn_dw_w[0], 8)]
    first, first_sems, token = _allgather_start("allgather_start_0", first, [True] * 3, [False, True, True], [[0, 1, 2]])
    rest = [w_kv, w_up, w_out, w_q, w_o, w_down]
    rest_flags = [True, True, False, False, False, False]
    rest = list(_place_shards("place_rest", place + token[0, 0].astype(jnp.int32), [w[0] for w in rest], rest_flags))
    rest, rest_sems, token = _allgather_start("allgather_start_1", rest, rest_flags, [False] * 6, [[2, 3, 0, 4], [1, 5]])
    started = [(first, [True] * 3, [False, True, True], first_sems[0]),
               ([rest[i] for i in (2, 3, 0, 4)], [False, False, True, False], [False] * 4, rest_sems[0]),
               ([rest[i] for i in (1, 5)], [True, False], [False] * 2, rest_sems[1])]
    norm_mix_g_late = norm_mix_g + token[0:1, 0:1]
    relayed = {}

    def relay(g, after):
        group_bufs, flags, wholes, group_sems = started[g]
        group_bufs, sibling_sems, relay_token = _allgather_relay("allgather_relay_%d" % g, group_bufs, flags, wholes, group_sems, after)
        relayed[g] = (group_bufs, sibling_sems)
        return relay_token[0:1, 0:1]

    def weights(g, after):
        group_bufs, sibling_sems = relayed[g]
        return _allgather_wait("allgather_wait_%d" % g, group_bufs, started[g][1], started[g][2], sibling_sems, after)

    names = ["w_in", "w_kv", "w_up", "w_out", "w_q", "w_o", "w_down"]
    reduce_groups = [["w_down", "w_up"], ["w_o", "w_q", "w_kv", "w_out"], ["w_in"]]
    in_flight = {}

    def reduce(g, grads):
        grads, lands, rs_sems, token = _grad_exchange_start("rs_start_%d" % g, grads)
        in_flight[g] = (grads, lands, rs_sems)
        return token[0:1, 0:1]

    grad_x, small = _local_step(
        x.reshape(n_seq * seq, d), mem.reshape(n_seq * n_mem, d), loss_target.reshape(n_seq * seq, d),
        norm_mix_g_late, norm_xattn_g, norm_mem_g, norm_ffn_g, norm_final_g.reshape(1, d),
        conv_dw_b, conv_ln_g, conv_ln_b, pool_w[0], pool_scale, ffn_dw_b, relay, weights, reduce, n_seq, seq, n_mem)

    landed = {}
    for g, members in enumerate(reduce_groups):
        grads, lands, rs_sems = in_flight[g]
        grads, lands = _grad_exchange_wait("rs_wait_%d" % g, grads, lands, rs_sems, grad_x)
        landed.update(zip(members, zip(grads, lands)))
    finals = _sum_partials("rs_sum", place, [landed[n][0] for n in names], [landed[n][1] for n in names])
    finals, swap_sems, token = _swap_halves_start(finals)

    gains, conv_rows, d_pw, ffn_rows, d_cw = _allreduce_small([small[0] + token[0:1, 0:1]] + small[1:])
    loss = gains[5, 0]
    shard_grads = _swap_halves_wait(finals, swap_sems, gains)

    outs = {}
    big_w = dict(zip(names, col_w + row_w))
    big_m = dict(w_in=m_w_in, w_kv=m_w_kv, w_up=m_w_up, w_out=m_w_out, w_q=m_w_q, w_o=m_w_o, w_down=m_w_down)
    big_v = dict(w_in=v_w_in, w_kv=v_w_kv, w_up=v_w_up, w_out=v_w_out, w_q=v_w_q, w_o=v_w_o, w_down=v_w_down)
    big_quads = [(big_w[n], g.reshape(big_w[n].shape[1:]), big_m[n], big_v[n]) for n, g in zip(names, shard_grads)]
    outs.update(zip(names, _adamw_shards(big_quads)))

    f2 = ffn_dw_b.shape[1]
    cs_c = conv_dw_w.shape[2]
    cs_f = ffn_dw_w.shape[2]
    g_cw = lax.dynamic_slice(d_cw, (0, chip * cs_c), (kw, cs_c)).reshape(conv_dw_w.shape)
    g_fw = lax.dynamic_slice(ffn_rows, (1, chip * cs_f), (ffn_dw_w.shape[1], cs_f)).reshape(ffn_dw_w.shape)
    small_params = [
        ("norm_mix_g", norm_mix_g, gains[0:1], m_norm_mix_g, v_norm_mix_g),
        ("conv_dw_w", conv_dw_w, g_cw, m_conv_dw_w, v_conv_dw_w),
        ("conv_dw_b", conv_dw_b, conv_rows[0:1], m_conv_dw_b, v_conv_dw_b),
        ("conv_ln_g", conv_ln_g, conv_rows[1:2], m_conv_ln_g, v_conv_ln_g),
        ("conv_ln_b", conv_ln_b, conv_rows[2:3], m_conv_ln_b, v_conv_ln_b),
        ("pool_w", pool_w, d_pw.reshape(pool_w.shape), m_pool_w, v_pool_w),
        ("pool_scale", pool_scale, conv_rows[3:4], m_pool_scale, v_pool_scale),
        ("norm_xattn_g", norm_xattn_g, gains[1:2], m_norm_xattn_g, v_norm_xattn_g),
        ("norm_mem_g", norm_mem_g, gains[2:3], m_norm_mem_g, v_norm_mem_g),
        ("norm_ffn_g", norm_ffn_g, gains[3:4], m_norm_ffn_g, v_norm_ffn_g),
        ("ffn_dw_w", ffn_dw_w, g_fw, m_ffn_dw_w, v_ffn_dw_w),
        ("ffn_dw_b", ffn_dw_b, ffn_rows[0:1, :f2], m_ffn_dw_b, v_ffn_dw_b),
        ("norm_final_g", norm_final_g.reshape(1, d), gains[4:5], m_norm_final_g.reshape(1, d), v_norm_final_g.reshape(1, d)),
    ]
    quads = []
    for _, w, g, m, v in small_params:
        shape2 = (-1, w.shape[-1])
        quads.append((w.reshape(shape2), g.reshape(shape2), m.reshape(shape2), v.reshape(shape2)))
    for (n, w, g, _, _), (delta, new_m, new_v) in zip(small_params, _adamw_small(quads)):
        shape = norm_final_g.shape if n == "norm_final_g" else w.shape
        outs[n] = (g.reshape(shape), delta.reshape(shape), new_m.reshape(shape), new_v.reshape(shape))

    order = ["norm_mix_g", "w_in", "conv_dw_w", "conv_dw_b", "conv_ln_g", "conv_ln_b", "pool_w", "pool_scale", "w_out",
             "norm_xattn_g", "norm_mem_g", "w_q", "w_kv", "w_o", "norm_ffn_g", "w_up", "ffn_dw_w", "ffn_dw_b", "w_down",
             "norm_final_g"]
    return (loss, grad_x.reshape(x.shape), *[outs[n][0] for n in order], *[outs[n][1] for n in order],
            *[outs[n][2] for n in order], *[outs[n][3] for n in order])
```

```python
import jax
import jax.numpy as jnp
from jax import lax
from jax.experimental import pallas as pl
from jax.experimental.pallas import tpu as pltpu

f32 = jnp.float32
_ACT = jnp.bfloat16

EPS = 1e-6
POOL_WINDOWS = (2, 4, 8, 16)
XATTN_HEADS = 4
ADAM_LR = 0.001
ADAM_B1 = 0.9
ADAM_B2 = 0.999
ADAM_EPS = 1e-08
ADAM_WD = 0.01
ADAM_STEP = 10

_VMEM_LIMIT_BYTES = 56 * 1024 * 1024
_MESH = pl.DeviceIdType.MESH
_ANY = pl.BlockSpec(memory_space=pl.ANY)
_VMEM = pl.BlockSpec(memory_space=pltpu.VMEM)
_HBM = pl.BlockSpec(memory_space=pltpu.HBM)
_SEM = pl.BlockSpec(memory_space=pltpu.SEMAPHORE)
_EFFECT = pltpu.SideEffectType.DATAFLOW_SIDE_EFFECTING

_NN = (((1,), (0,)), ((), ()))
_NT = (((1,), (1,)), ((), ()))
_TN = (((0,), (0,)), ((), ()))


def _params(n_grid):
    return pltpu.CompilerParams(dimension_semantics=("arbitrary",) * n_grid, vmem_limit_bytes=_VMEM_LIMIT_BYTES)


def _sigmoid(v):
    return 1.0 / (1.0 + jnp.exp(-v))


def _dot(a, b, dims):
    return lax.dot_general(a, b, dims, preferred_element_type=f32)


def _mm(name, a, b, *, dims, grid, a_spec, b_spec, o_spec, out_shape):
    def body(a_ref, b_ref, o_ref):
        o_ref[...] = _dot(a_ref[...], b_ref[...], dims).astype(o_ref.dtype)

    return pl.pallas_call(
        body, name=name, grid=grid, in_specs=[a_spec, b_spec], out_specs=o_spec, out_shape=out_shape,
        compiler_params=_params(len(grid)),
    )(a, b)


_NARROW = 2816


def _row_tile(m, width=_NARROW + 1):
    return min(1024 if width <= _NARROW else 512, m)


def _mm_nn(name, a, b, out_dtype, tn, split_out=False):
    m, k = a.shape
    n = b.shape[1]
    tm = _row_tile(m, max(k, tn))
    if split_out:
        out_shape = jax.ShapeDtypeStruct((n // tn, m, tn), out_dtype)
        o_spec = pl.BlockSpec((None, tm, tn), lambda j, i: (j, i, 0))
    else:
        out_shape = jax.ShapeDtypeStruct((m, n), out_dtype)
        o_spec = pl.BlockSpec((tm, tn), lambda j, i: (i, j))
    return _mm(
        name, a, b, dims=_NN, grid=(n // tn, m // tm),
        a_spec=pl.BlockSpec((tm, k), lambda j, i: (i, 0)), b_spec=pl.BlockSpec((k, tn), lambda j, i: (0, j)),
        o_spec=o_spec, out_shape=out_shape,
    )


def _mm_nt(name, a, b, out_dtype):
    n, kc = b.shape
    m = a.shape[0]
    tm = _row_tile(m, max(n, kc))
    return _mm(
        name, a, b, dims=_NT, grid=(m // tm,),
        a_spec=pl.BlockSpec((tm, kc), lambda i: (i, 0)),
        b_spec=pl.BlockSpec((n, kc), lambda i: (0, 0), pipeline_mode=pl.Buffered(1)),
        o_spec=pl.BlockSpec((tm, n), lambda i: (i, 0)),
        out_shape=jax.ShapeDtypeStruct((m, n), out_dtype),
    )


def _mm_tn_rows(name, a, b, tka, tn):
    m, ka = a.shape
    nb = b.shape[1]
    return _mm(
        name, a, b, dims=_TN, grid=(ka // tka, nb // tn),
        a_spec=pl.BlockSpec((m, tka), lambda i, j: (0, i)), b_spec=pl.BlockSpec((m, tn), lambda i, j: (0, j)),
        o_spec=pl.BlockSpec((tka, tn), lambda i, j: (i, j)),
        out_shape=jax.ShapeDtypeStruct((ka, nb), _ACT),
    )


def _mm_tn_pieces(name, a, b, cs):
    m, ka = a.shape
    if b.ndim == 3:
        b_spec = pl.BlockSpec((None, m, cs), lambda i, j: (j // 2, 0, j % 2))
    else:
        b_spec = pl.BlockSpec((m, cs), lambda i, j: (0, j))
    return _mm(
        name, a, b, dims=_TN, grid=(2, 4),
        a_spec=pl.BlockSpec((m, ka // 2), lambda i, j: (0, i)), b_spec=b_spec,
        o_spec=pl.BlockSpec((None, ka // 2, cs), lambda i, j: (2 * j + i, 0, 0)),
        out_shape=jax.ShapeDtypeStruct((8, ka // 2, cs), _ACT),
    )


def _after(after):
    return ([], []) if after is None else ([after], [_ANY])


def _rms_fwd(name, x, g, after=None):
    t, d = x.shape
    tm = _row_tile(t, d)
    more, more_specs = _after(after)

    def body(x_ref, g_ref, *refs):
        h_ref = refs[-1]
        xv = x_ref[...]
        r = lax.rsqrt(jnp.mean(xv * xv, axis=-1, keepdims=True) + EPS)
        h_ref[...] = (xv * r * g_ref[...]).astype(h_ref.dtype)

    return pl.pallas_call(
        body, name=name, grid=(t // tm,),
        in_specs=[pl.BlockSpec((tm, d), lambda i: (i, 0)), pl.BlockSpec((1, d), lambda i: (0, 0))] + more_specs,
        out_specs=pl.BlockSpec((tm, d), lambda i: (i, 0)), out_shape=jax.ShapeDtypeStruct((t, d), _ACT),
        compiler_params=_params(1),
    )(x, g, *more)


def _fused_rows(name, a, b, product, a_spec, tm, extras, extra_specs, out_shape, out_specs, epilogue):
    ne = len(extras)

    def body(a_ref, b_ref, *refs):
        epilogue(product(a_ref, b_ref), refs[:ne], refs[ne:])

    m = extras[0].shape[0]
    return pl.pallas_call(
        body, name=name, grid=(m // tm,),
        in_specs=[a_spec, pl.BlockSpec(b.shape, lambda i: (0, 0), pipeline_mode=pl.Buffered(1)), *extra_specs],
        out_specs=out_specs, out_shape=out_shape, compiler_params=_params(1),
    )(a, b, *extras)


def _proj_residual_norm(name, a, b, res, g, after=None):
    m, k = a.shape
    d = b.shape[1]
    tm = _row_tile(m, max(k, d))

    def epilogue(p, ins, outs):
        xv = p + ins[0][...]
        outs[0][...] = xv
        r = lax.rsqrt(jnp.mean(xv * xv, axis=-1, keepdims=True) + EPS)
        outs[1][...] = (xv * r * ins[1][...]).astype(outs[1].dtype)

    row = pl.BlockSpec((tm, d), lambda i: (i, 0))
    return _fused_rows(
        name, a, b, lambda a_ref, b_ref: _dot(a_ref[...], b_ref[...], _NN), pl.BlockSpec((tm, k), lambda i: (i, 0)), tm,
        [res, g] + _after(after)[0], [row, pl.BlockSpec((1, d), lambda i: (0, 0))] + _after(after)[1],
        [jax.ShapeDtypeStruct((m, d), f32), jax.ShapeDtypeStruct((m, d), _ACT)], [row, row], epilogue)


def _dproj_rms_bwd(name, a, b, x, g, dres, storage_copy=True, after=None):
    m, d = x.shape
    if a.ndim == 3:
        nh, _, kh = a.shape
        tm = _row_tile(m, nh * kh)
        a_spec = pl.BlockSpec((nh, tm, kh), lambda i: (0, i, 0))

        def product(a_ref, b_ref):
            p = _dot(a_ref[0], b_ref[:, 0:kh], _NT)
            for h in range(1, nh):
                p = p + _dot(a_ref[h], b_ref[:, h * kh:(h + 1) * kh], _NT)
            return p
    else:
        tm = _row_tile(m, max(a.shape[1], d))
        a_spec = pl.BlockSpec((tm, a.shape[1]), lambda i: (i, 0))

        def product(a_ref, b_ref):
            return _dot(a_ref[...], b_ref[...], _NT)

    def epilogue(dhv, ins, outs):
        x_ref, g_ref, dres_ref = ins[:3]
        dg_ref = outs[-1]

        @pl.when(pl.program_id(0) == 0)
        def _():
            dg_ref[...] = jnp.zeros_like(dg_ref)

        xv = x_ref[...]
        r = lax.rsqrt(jnp.mean(xv * xv, axis=-1, keepdims=True) + EPS)
        xn = xv * r
        dxn = dhv * g_ref[...]
        dx = r * (dxn - xn * jnp.mean(dxn * xn, axis=-1, keepdims=True)) + dres_ref[...]
        outs[0][...] = dx
        if storage_copy:
            outs[1][...] = dx.astype(outs[1].dtype)
        dg_ref[...] += jnp.sum(dhv * xn, axis=0, keepdims=True)

    row = pl.BlockSpec((tm, d), lambda i: (i, 0))
    vec = pl.BlockSpec((1, d), lambda i: (0, 0))
    copies = [jax.ShapeDtypeStruct((m, d), _ACT)] if storage_copy else []
    return _fused_rows(
        name, a, b, product, a_spec, tm, [x, g, dres] + _after(after)[0], [row, vec, row] + _after(after)[1],
        [jax.ShapeDtypeStruct((m, d), f32)] + copies + [jax.ShapeDtypeStruct((1, d), f32)],
        [row] * (1 + len(copies)) + [vec], epilogue)


def _proj_loss_bwd(name, a, b, res, g, tgt):
    m, k = a.shape
    d = b.shape[1]
    tm = _row_tile(m, max(k, d))

    def epilogue(p, ins, outs):
        res_ref, g_ref, t_ref = ins
        dx_ref, dxb_ref, dg_ref, loss_ref = outs

        @pl.when(pl.program_id(0) == 0)
        def _():
            dg_ref[...] = jnp.zeros_like(dg_ref)
            loss_ref[...] = jnp.zeros_like(loss_ref)

        xv = p + res_ref[...]
        gv = g_ref[...]
        r = lax.rsqrt(jnp.mean(xv * xv, axis=-1, keepdims=True) + EPS)
        xn = xv * r
        err = xn * gv - t_ref[...]
        loss_ref[...] += 0.5 * jnp.sum(jnp.mean(err * err, axis=-1, keepdims=True), axis=0, keepdims=True)
        dout = err * (1.0 / d)
        dxn = dout * gv
        dx = r * (dxn - xn * jnp.mean(dxn * xn, axis=-1, keepdims=True))
        dx_ref[...] = dx
        dxb_ref[...] = dx.astype(dxb_ref.dtype)
        dg_ref[...] += jnp.sum(dout * xn, axis=0, keepdims=True)

    row = pl.BlockSpec((tm, d), lambda i: (i, 0))
    vec = pl.BlockSpec((1, d), lambda i: (0, 0))
    return _fused_rows(
        name, a, b, lambda a_ref, b_ref: _dot(a_ref[...], b_ref[...], _NN), pl.BlockSpec((tm, k), lambda i: (i, 0)), tm,
        [res, g, tgt], [row, vec, row],
        [jax.ShapeDtypeStruct((m, d), f32), jax.ShapeDtypeStruct((m, d), _ACT), jax.ShapeDtypeStruct((1, d), f32),
         jax.ShapeDtypeStruct((1, 1), f32)],
        [row, row, vec, pl.BlockSpec((1, 1), lambda i: (0, 0))], epilogue)


def _rms_gain_grad(name, x, dh):
    t, d = x.shape
    tm = _row_tile(t)

    def body(x_ref, dh_ref, dg_ref):
        @pl.when(pl.program_id(0) == 0)
        def _():
            dg_ref[...] = jnp.zeros_like(dg_ref)

        xv = x_ref[...]
        r = lax.rsqrt(jnp.mean(xv * xv, axis=-1, keepdims=True) + EPS)
        dg_ref[...] += jnp.sum(dh_ref[...] * (xv * r), axis=0, keepdims=True)

    row = pl.BlockSpec((tm, d), lambda i: (i, 0))
    return pl.pallas_call(
        body, name=name, grid=(t // tm,), in_specs=[row, row], out_specs=pl.BlockSpec((1, d), lambda i: (0, 0)),
        out_shape=jax.ShapeDtypeStruct((1, d), f32), compiler_params=_params(1),
    )(x, dh)


_CONV_ROWS = 512
_CHUNK = 64
_HALO = 32


def _pool_counts(pos, w):
    return jnp.minimum(pos + 1.0, float(w))


def _rows_from(win, start, rows):
    if start % 8 == 0:
        return win[start:start + rows, :]
    n = win.shape[0]
    return pltpu.roll(win, n - start % 8, axis=0)[start - start % 8:start - start % 8 + rows, :]


def _tap_rows(buf, starts, rows):
    for residue in range(8):
        group = [(k, s) for k, s in starts.items() if s % 8 == residue]
        if group:
            lo = min(s for _, s in group) - residue
            hi = max(s for _, s in group) - residue + rows + (8 if residue else 0)
            win = buf[lo:hi, :]
            if residue:
                win = pltpu.roll(win, hi - lo - residue, axis=0)
            for k, s in group:
                yield k, win[s - residue - lo:s - residue - lo + rows, :]


def _mix_fwd(u, cw, cb, lg, lb, pw, ps, seq):
    t, c3 = u.shape
    c = c3 // 3
    kw = 31
    tm = min(_CONV_ROWS, seq)
    tps = seq // tm
    gd = c // len(POOL_WINDOWS)

    def body(u_ref, uh_ref, cw_ref, cb_ref, lg_ref, lb_ref, pw_ref, ps_ref, y_ref, hc_ref, hgbuf, pbuf):
        i = pl.program_id(0)
        keep = jnp.where(i % tps == 0, 0.0, 1.0)
        um = u_ref[...].astype(f32)
        uh = uh_ref[...].astype(f32) * keep
        hgbuf[0:_HALO, :] = uh[:, 0:c] * _sigmoid(uh[:, c:2 * c])
        hgbuf[_HALO:_HALO + tm, :] = um[:, 0:c] * _sigmoid(um[:, c:2 * c])
        pbuf[0:_HALO, :] = uh[:, 2 * c:]
        pbuf[_HALO:_HALO + tm, :] = um[:, 2 * c:]
        for r0 in range(0, tm, _CHUNK):
            acc = jnp.broadcast_to(cb_ref[...], (_CHUNK, c))
            for k, rows in _tap_rows(hgbuf, {k: r0 + _HALO - (kw - 1) + k for k in range(kw)}, _CHUNK):
                acc = acc + cw_ref[k:k + 1, :] * rows
            hc_ref[r0:r0 + _CHUNK, :] = acc
            mu = jnp.mean(acc, axis=-1, keepdims=True)
            xc = acc - mu
            var = jnp.mean(xc * xc, axis=-1, keepdims=True)
            hl = xc * lax.rsqrt(var + EPS) * lg_ref[...] + lb_ref[...]
            y_ref[r0:r0 + _CHUNK, 0:c] = (hl * _sigmoid(hl)).astype(y_ref.dtype)
        pos = ((i % tps) * tm).astype(f32) + lax.broadcasted_iota(jnp.int32, (tm, 1), 0).astype(f32)
        for gi, w in enumerate(POOL_WINDOWS):
            sl = slice(gi * gd, (gi + 1) * gd)
            v = pbuf[_HALO:_HALO + tm, sl]
            s = v
            for j in range(1, w):
                s = s + pbuf[_HALO - j:_HALO - j + tm, sl]
            pooled = s / _pool_counts(pos, w) - v
            mixed = _dot(pooled.astype(_ACT), pw_ref[gi].astype(_ACT), _NN)
            y_ref[:, c + gi * gd:c + (gi + 1) * gd] = (mixed * ps_ref[:, sl]).astype(y_ref.dtype)

    hb = tm // _HALO
    full = lambda shape: pl.BlockSpec(shape, lambda i: (0,) * len(shape))
    return pl.pallas_call(
        body, name="mix_fwd", grid=(t // tm,),
        in_specs=[pl.BlockSpec((tm, c3), lambda i: (i, 0)),
                  pl.BlockSpec((_HALO, c3), lambda i: (jnp.maximum(i * hb - 1, 0), 0)),
                  full((_HALO, c)), full((1, c)), full((1, c)), full((1, c)), full((len(POOL_WINDOWS), gd, gd)), full((1, c))],
        out_specs=[pl.BlockSpec((tm, 2 * c), lambda i: (i, 0)), pl.BlockSpec((tm, c), lambda i: (i, 0))],
        out_shape=[jax.ShapeDtypeStruct((t, 2 * c), _ACT), jax.ShapeDtypeStruct((t, c), f32)],
        scratch_shapes=[pltpu.VMEM((_HALO + tm, c), f32), pltpu.VMEM((_HALO + tm, c), f32)],
        compiler_params=_params(1),
    )(u, u, cw, cb, lg, lb, pw, ps)


def _mix_bwd_norm(hc, dy, lg, lb, after):
    t, c = hc.shape
    tm = _row_tile(t, c)

    def body(hc_ref, dy_ref, lg_ref, lb_ref, after_ref, dhc_ref, sums_ref):
        @pl.when(pl.program_id(0) == 0)
        def _():
            sums_ref[...] = jnp.zeros_like(sums_ref)

        hcv = hc_ref[...]
        mu = jnp.mean(hcv, axis=-1, keepdims=True)
        xc = hcv - mu
        rstd = lax.rsqrt(jnp.mean(xc * xc, axis=-1, keepdims=True) + EPS)
        n = xc * rstd
        hl = n * lg_ref[...] + lb_ref[...]
        sg = _sigmoid(hl)
        dhl = dy_ref[...].astype(f32) * (sg * (1.0 + hl * (1.0 - sg)))
        dn = dhl * lg_ref[...]
        dhc = rstd * (dn - jnp.mean(dn, axis=-1, keepdims=True) - n * jnp.mean(dn * n, axis=-1, keepdims=True))
        dhc_ref[...] = dhc
        sums_ref[0:1, :] += jnp.sum(dhl * n, axis=0, keepdims=True)
        sums_ref[1:2, :] += jnp.sum(dhl, axis=0, keepdims=True)
        sums_ref[2:3, :] += jnp.sum(dhc, axis=0, keepdims=True)

    row = pl.BlockSpec((tm, c), lambda i: (i, 0))
    vec = pl.BlockSpec((1, c), lambda i: (0, 0))
    return pl.pallas_call(
        body, name="mix_bwd_norm", grid=(t // tm,), in_specs=[row, row, vec, vec, _ANY],
        out_specs=[row, pl.BlockSpec((8, c), lambda i: (0, 0))],
        out_shape=[jax.ShapeDtypeStruct((t, c), f32), jax.ShapeDtypeStruct((8, c), f32)],
        compiler_params=_params(1),
    )(hc, dy, lg, lb, after)


def _mix_bwd_taps(u, dhc, dy, cw, pw, ps, seq):
    t, c3 = u.shape
    c = c3 // 3
    kw = 31
    tm = min(_CONV_ROWS, seq)
    tps = seq // tm
    ng = len(POOL_WINDOWS)
    gd = c // ng
    nh = 16

    def body(u_ref, uh_ref, dhc_ref, dhcn_ref, dy_ref, dyn_ref, cw_ref, pw_ref, ps_ref,
             du_ref, dcw_ref, dps_ref, dpw_ref, hgbuf, dcbuf, pbuf, dpbuf):
        i = pl.program_id(0)
        keep_prev = jnp.where(i % tps == 0, 0.0, 1.0)
        keep_next = jnp.where(i % tps == tps - 1, 0.0, 1.0)

        @pl.when(i == 0)
        def _():
            dcw_ref[...] = jnp.zeros_like(dcw_ref)
            dps_ref[...] = jnp.zeros_like(dps_ref)
            dpw_ref[...] = jnp.zeros_like(dpw_ref)

        uh = uh_ref[...].astype(f32) * keep_prev
        hgbuf[0:_HALO, :] = uh[:, 0:c] * _sigmoid(uh[:, c:2 * c])
        pbuf[0:_HALO, :] = uh[:, 2 * c:]
        um = u_ref[...].astype(f32)
        hgbuf[_HALO:_HALO + tm, :] = um[:, 0:c] * _sigmoid(um[:, c:2 * c])
        pbuf[_HALO:_HALO + tm, :] = um[:, 2 * c:]
        dcbuf[0:tm, :] = dhc_ref[...]
        dcbuf[tm:tm + _HALO, :] = dhcn_ref[...] * keep_next
        tap_sums = [None] * kw
        for r0 in range(0, tm, _CHUNK):
            dh = dcbuf[r0:r0 + _CHUNK, :]
            acc = jnp.zeros((_CHUNK, c), f32)
            for k, rows in _tap_rows(hgbuf, {k: r0 + _HALO - (kw - 1) + k for k in range(kw)}, _CHUNK):
                part = (dh * rows).reshape(_CHUNK // 8, 8, c).sum(axis=0)
                tap_sums[k] = part if tap_sums[k] is None else tap_sums[k] + part
            for k, rows in _tap_rows(dcbuf, {k: r0 + (kw - 1) - k for k in range(kw)}, _CHUNK):
                acc = acc + cw_ref[k:k + 1, :] * rows
            val = u_ref[r0:r0 + _CHUNK, 0:c].astype(f32)
            sg = _sigmoid(u_ref[r0:r0 + _CHUNK, c:2 * c].astype(f32))
            du_ref[r0:r0 + _CHUNK, 0:c] = (acc * sg).astype(du_ref.dtype)
            du_ref[r0:r0 + _CHUNK, c:2 * c] = (acc * val * sg * (1.0 - sg)).astype(du_ref.dtype)
        for k in range(kw):
            dcw_ref[k:k + 1, :] += jnp.sum(tap_sums[k], axis=0, keepdims=True)
        base = ((i % tps) * tm).astype(f32)
        pos = base + lax.broadcasted_iota(jnp.int32, (tm, 1), 0).astype(f32)
        pos_next = base + float(tm) + lax.broadcasted_iota(jnp.int32, (nh, 1), 0).astype(f32)
        for gi, w in enumerate(POOL_WINDOWS):
            sl = slice(gi * gd, (gi + 1) * gd)
            v = pbuf[_HALO:_HALO + tm, sl]
            s = v
            for j in range(1, w):
                s = s + pbuf[_HALO - j:_HALO - j + tm, sl]
            cnt = _pool_counts(pos, w)
            pooled = (s / cnt - v).astype(_ACT)
            pwg = pw_ref[gi].astype(_ACT)
            mixed = _dot(pooled, pwg, _NN)
            dyp = dy_ref[:, sl].astype(f32)
            dps_ref[0:1, sl] += jnp.sum(dyp * mixed, axis=0, keepdims=True)
            dmix = (dyp * ps_ref[:, sl]).astype(_ACT)
            dpw_ref[gi] += _dot(pooled, dmix, _TN)
            dmix_next = (dyn_ref[:, sl].astype(f32) * ps_ref[:, sl] * keep_next).astype(_ACT)
            dpool = _dot(dmix, pwg, _NT)
            dpbuf[0:tm, sl] = dpool / cnt
            dpbuf[tm:tm + nh, sl] = _dot(dmix_next, pwg, _NT) / _pool_counts(pos_next, w)
            acc = -dpool
            for j in range(w):
                acc = acc + dpbuf[j:j + tm, sl]
            du_ref[:, 2 * c + gi * gd:2 * c + (gi + 1) * gd] = acc.astype(du_ref.dtype)

    hb = tm // _HALO
    n_halo = t // _HALO
    n_nh = t // nh
    full = lambda shape: pl.BlockSpec(shape, lambda i: (0,) * len(shape))
    return pl.pallas_call(
        body, name="mix_bwd_taps", grid=(t // tm,),
        in_specs=[pl.BlockSpec((tm, c3), lambda i: (i, 0)),
                  pl.BlockSpec((_HALO, c3), lambda i: (jnp.maximum(i * hb - 1, 0), 0)),
                  pl.BlockSpec((tm, c), lambda i: (i, 0)),
                  pl.BlockSpec((_HALO, c), lambda i: (jnp.minimum((i + 1) * hb, n_halo - 1), 0)),
                  pl.BlockSpec((tm, c), lambda i: (i, 1)),
                  pl.BlockSpec((nh, c), lambda i: (jnp.minimum((i + 1) * (tm // nh), n_nh - 1), 1)),
                  full((_HALO, c)), full((ng, gd, gd)), full((1, c))],
        out_specs=[pl.BlockSpec((tm, c3), lambda i: (i, 0)), full((_HALO, c)), full((8, c)), full((ng, gd, gd))],
        out_shape=[jax.ShapeDtypeStruct((t, c3), _ACT), jax.ShapeDtypeStruct((_HALO, c), f32),
                   jax.ShapeDtypeStruct((8, c), f32), jax.ShapeDtypeStruct((ng, gd, gd), f32)],
        scratch_shapes=[pltpu.VMEM((_HALO + tm, c), f32), pltpu.VMEM((tm + _HALO, c), f32),
                        pltpu.VMEM((_HALO + tm, c), f32), pltpu.VMEM((tm + nh, c), f32)],
        compiler_params=_params(1),
    )(u, u, dhc, dhc, dy, dy, cw, pw, ps)


def _attn_fwd(q, kv, n_seq, seq, n_mem):
    t, d = q.shape
    dh = d // XATTN_HEADS
    tq = min(1024, seq)
    nq = seq // tq
    scale = dh ** -0.5

    def body(q_ref, kv_ref, o_ref):
        for h in range(XATTN_HEADS):
            cols = slice(h * dh, (h + 1) * dh)
            s = _dot(q_ref[:, cols], kv_ref[:, cols], _NT) * scale
            e = jnp.exp(s - jnp.max(s, axis=-1, keepdims=True))
            p = e / jnp.sum(e, axis=-1, keepdims=True)
            o_ref[:, cols] = _dot(p.astype(_ACT), kv_ref[:, d + h * dh:d + (h + 1) * dh], _NN).astype(o_ref.dtype)

    qs = pl.BlockSpec((tq, d), lambda b, i: (b * nq + i, 0))
    return pl.pallas_call(
        body, name="attn_fwd", grid=(n_seq, nq), in_specs=[qs, pl.BlockSpec((n_mem, 2 * d), lambda b, i: (b, 0))],
        out_specs=qs, out_shape=jax.ShapeDtypeStruct((t, d), _ACT), compiler_params=_params(2),
    )(q, kv)


def _attn_bwd(q, kv, do, n_seq, seq, n_mem):
    t, d = q.shape
    dh = d // XATTN_HEADS
    tq = min(1024, seq)
    nq = seq // tq
    scale = dh ** -0.5

    def body(q_ref, kv_ref, do_ref, dq_ref, dkv_ref, acc):
        i = pl.program_id(1)

        @pl.when(i == 0)
        def _():
            acc[...] = jnp.zeros_like(acc)

        for h in range(XATTN_HEADS):
            cols = slice(h * dh, (h + 1) * dh)
            vcols = slice(d + h * dh, d + (h + 1) * dh)
            qv = q_ref[:, cols]
            kh = kv_ref[:, cols]
            dov = do_ref[:, cols]
            s = _dot(qv, kh, _NT) * scale
            e = jnp.exp(s - jnp.max(s, axis=-1, keepdims=True))
            p = e / jnp.sum(e, axis=-1, keepdims=True)
            dp = _dot(dov, kv_ref[:, vcols], _NT)
            ds = (p * (dp - jnp.sum(dp * p, axis=-1, keepdims=True)) * scale).astype(_ACT)
            dq_ref[:, cols] = _dot(ds, kh, _NN).astype(dq_ref.dtype)
            acc[:, cols] += _dot(ds, qv, _TN)
            acc[:, vcols] += _dot(p.astype(_ACT), dov, _TN)

        @pl.when(i == nq - 1)
        def _():
            dkv_ref[...] = acc[...].astype(dkv_ref.dtype)

    qs = pl.BlockSpec((tq, d), lambda b, i: (b * nq + i, 0))
    ms = pl.BlockSpec((n_mem, 2 * d), lambda b, i: (b, 0))
    return pl.pallas_call(
        body, name="attn_bwd", grid=(n_seq, nq), in_specs=[qs, ms, qs], out_specs=[qs, ms],
        out_shape=[jax.ShapeDtypeStruct((t, d), _ACT), jax.ShapeDtypeStruct((n_seq * n_mem, 2 * d), _ACT)],
        scratch_shapes=[pltpu.VMEM((n_mem, 2 * d), f32)], compiler_params=_params(2),
    )(q, kv, do)


_FFN_ROWS = 2048
_FFN_COLS = 256
_FFN_HALO = 16


def _window(buf, g, start, rows):
    return buf[g, pl.ds(start, rows + 8), :]


def _taps3(win, rows):
    return [_rows_from(win, 6 + k, rows) for k in range(3)]


def _conv3(b_ref, w_ref, taps):
    acc = b_ref[...] + w_ref[0:1, :] * taps[0]
    for k in (1, 2):
        acc = acc + w_ref[k:k + 1, :] * taps[k]
    return acc


def _ffn_gate_fwd(up, fw, fb, seq):
    _, t, f = up.shape
    tm = min(_FFN_ROWS, seq)
    tps = seq // tm
    tc = _FFN_COLS
    nc = f // tc
    hl = _FFN_HALO

    def body(up_ref, uph_ref, wg_ref, wv_ref, bg_ref, bv_ref, a_ref):
        i = pl.program_id(1)
        before = uph_ref[...]
        before = jnp.where(i % tps == 0, jnp.zeros_like(before), before)

        def chunk(r0, wins):
            conv = []
            for g, (w_ref, b_ref) in enumerate(((wg_ref, bg_ref), (wv_ref, bv_ref))):
                conv.append(_conv3(b_ref, w_ref, _taps3(wins[g].astype(f32)[hl - 8:, :], _CHUNK)))
            gate, val = conv
            a_ref[pl.ds(r0, _CHUNK), :] = (gate * _sigmoid(gate) * val).astype(a_ref.dtype)

        chunk(0, [jnp.concatenate([before[g], up_ref[g, 0:_CHUNK, :]], axis=0) for g in range(2)])

        def later(ci, carry):
            r0 = pl.multiple_of(ci * _CHUNK, _CHUNK)
            chunk(r0, [up_ref[g, pl.ds(r0 - hl, _CHUNK + hl), :] for g in range(2)])
            return carry

        lax.fori_loop(1, tm // _CHUNK, later, 0)

    hb = tm // hl
    return pl.pallas_call(
        body, name="ffn_gate_fwd", grid=(nc, t // tm),
        in_specs=[pl.BlockSpec((2, tm, tc), lambda j, i: (0, i, j)),
                  pl.BlockSpec((2, hl, tc), lambda j, i: (0, jnp.maximum(i * hb - 1, 0), j)),
                  pl.BlockSpec((8, tc), lambda j, i: (0, j)), pl.BlockSpec((8, tc), lambda j, i: (0, nc + j)),
                  pl.BlockSpec((1, tc), lambda j, i: (0, j)), pl.BlockSpec((1, tc), lambda j, i: (0, nc + j))],
        out_specs=pl.BlockSpec((tm, tc), lambda j, i: (i, j)),
        out_shape=jax.ShapeDtypeStruct((t, f), _ACT), compiler_params=_params(2),
    )(up, up, fw, fw, fb, fb)


def _ffn_gate_bwd(up, da, fw, fb, seq):
    _, t, f = up.shape
    tm = min(_FFN_ROWS, seq)
    tps = seq // tm
    tc = _FFN_COLS
    nc = f // tc
    hl = _FFN_HALO

    def body(up_ref, uph_ref, upn_ref, da_ref, dan_ref, wg_ref, wv_ref, bg_ref, bv_ref,
             dup_ref, sg_ref, sv_ref, dbuf, sums):
        i = pl.program_id(1)
        at_end = i % tps == tps - 1

        @pl.when(i == 0)
        def _():
            sg_ref[...] = jnp.zeros_like(sg_ref)
            sv_ref[...] = jnp.zeros_like(sv_ref)

        sums[...] = jnp.zeros_like(sums)
        before = uph_ref[...]
        before = jnp.where(i % tps == 0, jnp.zeros_like(before), before)
        after = upn_ref[...]
        after = jnp.where(at_end, jnp.zeros_like(after), after)
        w_refs = (wg_ref, wv_ref)
        b_refs = (bg_ref, bv_ref)

        def grads(r0, rows, wins, dav, count):
            taps = [_taps3(wins[g].astype(f32)[hl - 8:, :], rows) for g in range(2)]
            gate, val = [_conv3(b_refs[g], w_refs[g], taps[g]) for g in range(2)]
            sg = _sigmoid(gate)
            douts = (dav * val * (sg * (1.0 + gate * (1.0 - sg))), dav * (gate * sg))
            for g in range(2):
                dbuf[g, pl.ds(r0, rows), :] = douts[g]
                if count:
                    sums[g, 0] += douts[g].reshape(rows // 8, 8, tc).sum(axis=0)
                    for k in range(3):
                        sums[g, 1 + k] += (douts[g] * taps[g][k]).reshape(rows // 8, 8, tc).sum(axis=0)

        grads(0, _CHUNK, [jnp.concatenate([before[g], up_ref[g, 0:_CHUNK, :]], axis=0) for g in range(2)],
              da_ref[0:_CHUNK, :].astype(f32), True)

        def first(ci, carry):
            r0 = pl.multiple_of(ci * _CHUNK, _CHUNK)
            grads(r0, _CHUNK, [up_ref[g, pl.ds(r0 - hl, _CHUNK + hl), :] for g in range(2)],
                  da_ref[pl.ds(r0, _CHUNK), :].astype(f32), True)
            return carry

        lax.fori_loop(1, tm // _CHUNK, first, 0)
        da_after = dan_ref[...].astype(f32)
        grads(tm, hl, [jnp.concatenate([up_ref[g, tm - hl:tm, :], after[g]], axis=0) for g in range(2)],
              jnp.where(at_end, jnp.zeros_like(da_after), da_after), False)

        def second(ci, carry):
            r0 = pl.multiple_of(ci * _CHUNK, _CHUNK)
            for g in range(2):
                win = _window(dbuf, g, r0, _CHUNK)
                acc = jnp.zeros((_CHUNK, tc), f32)
                for k in range(3):
                    acc = acc + w_refs[g][k:k + 1, :] * _rows_from(win, 2 - k, _CHUNK)
                dup_ref[g, pl.ds(r0, _CHUNK), :] = acc.astype(dup_ref.dtype)
            return carry

        lax.fori_loop(0, tm // _CHUNK, second, 0)
        for g, s_ref in enumerate((sg_ref, sv_ref)):
            for r in range(4):
                s_ref[r:r + 1, :] += jnp.sum(sums[g, r], axis=0, keepdims=True)

    hb = tm // hl
    n_halo = t // hl
    return pl.pallas_call(
        body, name="ffn_gate_bwd", grid=(nc, t // tm),
        in_specs=[pl.BlockSpec((2, tm, tc), lambda j, i: (0, i, j)),
                  pl.BlockSpec((2, hl, tc), lambda j, i: (0, jnp.maximum(i * hb - 1, 0), j)),
                  pl.BlockSpec((2, hl, tc), lambda j, i: (0, jnp.minimum((i + 1) * hb, n_halo - 1), j)),
                  pl.BlockSpec((tm, tc), lambda j, i: (i, j)),
                  pl.BlockSpec((hl, tc), lambda j, i: (jnp.minimum((i + 1) * hb, n_halo - 1), j)),
                  pl.BlockSpec((8, tc), lambda j, i: (0, j)), pl.BlockSpec((8, tc), lambda j, i: (0, nc + j)),
                  pl.BlockSpec((1, tc), lambda j, i: (0, j)), pl.BlockSpec((1, tc), lambda j, i: (0, nc + j))],
        out_specs=[pl.BlockSpec((2, tm, tc), lambda j, i: (0, i, j)),
                   pl.BlockSpec((8, tc), lambda j, i: (0, j)), pl.BlockSpec((8, tc), lambda j, i: (0, j))],
        out_shape=[jax.ShapeDtypeStruct((2, t, f), _ACT), jax.ShapeDtypeStruct((8, f), f32), jax.ShapeDtypeStruct((8, f), f32)],
        scratch_shapes=[pltpu.VMEM((2, tm + hl, tc), f32), pltpu.VMEM((2, 4, 8, tc), f32)],
        compiler_params=_params(2),
    )(up, up, up, da, da, fw, fw, fb, fb)


def _adamw_math(w, g, m, v):
    m = ADAM_B1 * m + (1.0 - ADAM_B1) * g
    v = ADAM_B2 * v + (1.0 - ADAM_B2) * (g * g)
    m_hat = m / (1.0 - ADAM_B1 ** ADAM_STEP)
    v_hat = v / (1.0 - ADAM_B2 ** ADAM_STEP)
    delta = -ADAM_LR * (m_hat / (jnp.sqrt(v_hat) + ADAM_EPS) + ADAM_WD * w)
    return delta, m, v


def _adamw_shards(quads):
    n = len(quads)
    steps = 8

    def body(*refs):
        for p in range(n):
            w_ref, g_ref, m_ref, v_ref = refs[4 * p:4 * p + 4]
            go_ref, d_ref, mo_ref, vo_ref = refs[4 * n + 4 * p:4 * n + 4 * p + 4]
            gv = g_ref[...]
            d, mn, vn = _adamw_math(w_ref[...], gv, m_ref[...], v_ref[...])
            go_ref[...] = gv
            d_ref[...] = d
            mo_ref[...] = mn
            vo_ref[...] = vn

    in_specs, out_specs, out_shape = [], [], []
    for w, _, _, _ in quads:
        _, r, c = w.shape
        s3 = pl.BlockSpec((None, r // steps, c), lambda i: (0, i, 0))
        in_specs += [s3, pl.BlockSpec((r // steps, c), lambda i: (i, 0)), s3, s3]
        out_specs += [s3] * 4
        out_shape += [jax.ShapeDtypeStruct(w.shape, f32)] * 4
    outs = pl.pallas_call(
        body, name="adamw_shards", grid=(steps,), in_specs=in_specs, out_specs=out_specs, out_shape=out_shape,
        compiler_params=_params(1),
    )(*[a for q in quads for a in q])
    return [tuple(outs[4 * p:4 * p + 4]) for p in range(n)]


def _adamw_small(quads):
    n = len(quads)

    def body(*refs):
        ins, outs = refs[:4 * n], refs[4 * n:]
        for p in range(n):
            w_ref, g_ref, m_ref, v_ref = ins[4 * p:4 * p + 4]
            d, mn, vn = _adamw_math(w_ref[...], g_ref[...], m_ref[...], v_ref[...])
            outs[3 * p][...] = d
            outs[3 * p + 1][...] = mn
            outs[3 * p + 2][...] = vn

    flat = [a for q in quads for a in q]
    shapes = [jax.ShapeDtypeStruct(q[0].shape, f32) for q in quads for _ in range(3)]
    outs = pl.pallas_call(
        body, name="adamw_small", in_specs=[_VMEM] * (4 * n), out_specs=[_VMEM] * (3 * n), out_shape=shapes,
        compiler_params=pltpu.CompilerParams(vmem_limit_bytes=_VMEM_LIMIT_BYTES),
    )(*flat)
    return [tuple(outs[3 * p:3 * p + 3]) for p in range(n)]


def _sum_partials(name, place, grads, got):
    nw = len(grads)
    steps = 2

    def body(place_ref, *refs):
        for w in range(nw):
            own_ref, got_ref, f_ref = refs[w], refs[nw + w], refs[2 * nw + w]
            s = own_ref[...].astype(f32)
            for k in range(got[w].shape[0]):
                s = s + got_ref[k].astype(f32)
            f_ref[...] = s

    own_specs, got_specs, out_specs, out_shape = [], [], [], []
    for g, l in zip(grads, got):
        _, r, c = g.shape
        tr = r // steps
        own_specs.append(pl.BlockSpec((None, tr, c), lambda i, p: (2 * p[0] + p[1], i, 0)))
        got_specs.append(pl.BlockSpec((l.shape[0], tr, c), lambda i, p: (0, i, 0)))
        out_specs.append(pl.BlockSpec((None, tr, c), lambda i, p: (p[1], i, 0)))
        out_shape.append(jax.ShapeDtypeStruct((2, r, c), f32))
    grid_spec = pltpu.PrefetchScalarGridSpec(num_scalar_prefetch=1, grid=(steps,), in_specs=own_specs + got_specs, out_specs=out_specs)
    return pl.pallas_call(body, name=name, grid_spec=grid_spec, out_shape=out_shape,
                          compiler_params=_params(1))(place, *grads, *got)


def _place():
    return lax.axis_index("x"), lax.axis_index("y"), lax.axis_index("c")


def _other_chips(x, y):
    return [(1 - x, y), (x, 1 - y), (1 - x, 1 - y)]


def _remote(src, dst, send_sem, recv_sem, to):
    return pltpu.make_async_remote_copy(src_ref=src, dst_ref=dst, send_sem=send_sem, recv_sem=recv_sem,
                                        device_id=to, device_id_type=_MESH)


def _place_shards(name, place, shards, col_sharded, after=None):
    n = len(shards)
    steps = 4
    more, more_specs = _after(after)

    def body(place_ref, *refs):
        for src, dst in zip(refs[:n], refs[n + len(more):]):
            dst[...] = src[...].astype(dst.dtype)

    in_specs, out_specs, out_shape = [], [], []
    for w, col in zip(shards, col_sharded):
        r, cs = w.shape
        tr = r // steps
        in_specs.append(pl.BlockSpec((tr, cs), lambda i, p: (i, 0)))
        if col:
            out_specs.append(pl.BlockSpec((tr, cs), lambda i, p: (i, p[0])))
            out_shape.append(jax.ShapeDtypeStruct((r, 4 * cs), _ACT))
        else:
            out_specs.append(pl.BlockSpec((tr, cs), lambda i, p: (p[0] * steps + i, 0)))
            out_shape.append(jax.ShapeDtypeStruct((4 * r, cs), _ACT))
    grid_spec = pltpu.PrefetchScalarGridSpec(num_scalar_prefetch=1, grid=(steps,), in_specs=in_specs + more_specs,
                                            out_specs=out_specs)
    return pl.pallas_call(body, name=name, grid_spec=grid_spec, out_shape=out_shape,
                          compiler_params=_params(1))(place, *shards, *more)


def _shard_of(ref, col_sharded, s):
    rows, cols = ref.shape
    if col_sharded:
        return ref.at[:, pl.ds(s * (cols // 4), cols // 4)]
    return ref.at[pl.ds(s * (rows // 4), rows // 4), :]


def _part_of(ref, col_sharded, whole, s, h):
    if whole:
        return _shard_of(ref, col_sharded, s)
    rows, cols = ref.shape
    if col_sharded:
        return ref.at[pl.ds(h * (rows // 2), rows // 2), pl.ds(s * (cols // 4), cols // 4)]
    return ref.at[pl.ds((2 * s + h) * (rows // 8), rows // 8), :]


def _allgather_start(name, bufs, col_sharded, whole, groups):
    n = len(bufs)
    ng = len(groups)

    def body(*refs):
        out = refs[n:2 * n]
        sems = refs[2 * n:2 * n + 2 * ng]
        token = refs[2 * n + 2 * ng]
        x, y, c = _place()
        for g, members in enumerate(groups):
            for i, w in enumerate(members):
                mine = _part_of(out[w], col_sharded[w], whole[w], 2 * x + y, c)
                for j, chip in enumerate(_other_chips(x, y)):
                    _remote(mine, mine, sems[2 * g].at[3 * i + j], sems[2 * g + 1].at[3 * i + j], (*chip, c)).start()
        token[...] = jnp.zeros_like(token)

    sem_shapes = [pltpu.SemaphoreType.DMA((3 * len(m),)) for m in groups for _ in range(2)]
    outs = pl.pallas_call(
        body, name=name, in_specs=[_HBM] * n, out_specs=[_HBM] * n + [_SEM] * (2 * ng) + [_VMEM],
        out_shape=[pltpu.HBM(b.shape, b.dtype) for b in bufs] + sem_shapes + [jax.ShapeDtypeStruct((8, 128), f32)],
        input_output_aliases={i: i for i in range(n)},
        compiler_params=pltpu.CompilerParams(has_side_effects=_EFFECT),
    )(*[pltpu.with_memory_space_constraint(b, pltpu.HBM) for b in bufs])
    return list(outs[:n]), [(outs[n + 2 * g], outs[n + 2 * g + 1]) for g in range(ng)], outs[n + 2 * ng]


def _allgather_relay(name, bufs, col_sharded, whole, sems, after):
    n = len(bufs)

    def body(*refs):
        buf = refs[:n]
        send, recv = refs[n], refs[n + 1]
        out = refs[n + 3:2 * n + 3]
        to_sibling, from_sibling, token = refs[2 * n + 3:]
        token[...] = jnp.zeros_like(token)
        x, y, c = _place()
        for i in range(n):
            mine = _part_of(buf[i], col_sharded[i], whole[i], 2 * x + y, c)
            for j, chip in enumerate(_other_chips(x, y)):
                landed = _part_of(buf[i], col_sharded[i], whole[i], 2 * chip[0] + chip[1], c)
                cp = _remote(mine, landed, send.at[3 * i + j], recv.at[3 * i + j], (*chip, c))
                cp.wait_send()
                cp.wait_recv()
        for i in range(n):
            if not whole[i]:
                for j, chip in enumerate(_other_chips(x, y)):
                    landed = _part_of(out[i], col_sharded[i], False, 2 * chip[0] + chip[1], c)
                    _remote(landed, landed, to_sibling.at[3 * i + j], from_sibling.at[3 * i + j], (x, y, 1 - c)).start()

    outs = pl.pallas_call(
        body, name=name, in_specs=[_HBM] * n + [_SEM, _SEM, _ANY], out_specs=[_HBM] * n + [_SEM, _SEM, _VMEM],
        out_shape=[pltpu.HBM(b.shape, b.dtype) for b in bufs] + [pltpu.SemaphoreType.DMA((3 * n,))] * 2
        + [jax.ShapeDtypeStruct((8, 128), f32)],
        input_output_aliases={i: i for i in range(n)},
        compiler_params=pltpu.CompilerParams(has_side_effects=_EFFECT),
    )(*bufs, *sems, after)
    return list(outs[:n]), (outs[n], outs[n + 1]), outs[n + 2]


def _allgather_wait(name, bufs, col_sharded, whole, sems, after):
    n = len(bufs)

    def body(*refs):
        buf = refs[:n]
        to_sibling, from_sibling = refs[n], refs[n + 1]
        x, y, c = _place()
        for i in range(n):
            if not whole[i]:
                for j, chip in enumerate(_other_chips(x, y)):
                    sent = _part_of(buf[i], col_sharded[i], False, 2 * chip[0] + chip[1], c)
                    landed = _part_of(buf[i], col_sharded[i], False, 2 * chip[0] + chip[1], 1 - c)
                    cp = _remote(sent, landed, to_sibling.at[3 * i + j], from_sibling.at[3 * i + j], (x, y, 1 - c))
                    cp.wait_send()
                    cp.wait_recv()

    return pl.pallas_call(
        body, name=name, in_specs=[_HBM] * n + [_SEM, _SEM, _ANY], out_specs=[_HBM] * n,
        out_shape=[pltpu.HBM(b.shape, b.dtype) for b in bufs],
        input_output_aliases={i: i for i in range(n)},
        compiler_params=pltpu.CompilerParams(has_side_effects=_EFFECT),
    )(*bufs, *sems, after)


def _other_devices(x, y, c):
    flips = [(bx, by, bc) for bx in (0, 1) for by in (0, 1) for bc in (0, 1)][1:]
    return [(1 - x if bx else x, 1 - y if by else y, 1 - c if bc else c) for bx, by, bc in flips]


def _grad_exchange_start(name, grads):
    nw = len(grads)
    lands = [lax.empty((7,) + g.shape[1:], g.dtype) for g in grads]

    def body(*refs):
        src = refs[2 * nw:3 * nw]
        got = refs[3 * nw:4 * nw]
        send, recv, token = refs[4 * nw:]
        x, y, c = _place()
        for w in range(nw):
            for k, (px, py, pc) in enumerate(_other_devices(x, y, c)):
                _remote(src[w].at[4 * px + 2 * py + pc], got[w].at[k], send.at[7 * w + k], recv.at[7 * w + k], (px, py, pc)).start()
        token[...] = jnp.zeros_like(token)

    outs = pl.pallas_call(
        body, name=name, in_specs=[_HBM] * (2 * nw), out_specs=[_HBM] * (2 * nw) + [_SEM, _SEM, _VMEM],
        out_shape=[pltpu.HBM(a.shape, a.dtype) for a in list(grads) + lands]
        + [pltpu.SemaphoreType.DMA((7 * nw,)), pltpu.SemaphoreType.DMA((7 * nw,)), jax.ShapeDtypeStruct((8, 128), f32)],
        input_output_aliases={i: i for i in range(2 * nw)},
        compiler_params=pltpu.CompilerParams(has_side_effects=_EFFECT),
    )(*[pltpu.with_memory_space_constraint(a, pltpu.HBM) for a in list(grads) + lands])
    return list(outs[:nw]), list(outs[nw:2 * nw]), (outs[2 * nw], outs[2 * nw + 1]), outs[2 * nw + 2]


def _grad_exchange_wait(name, grads, got, sems, after):
    nw = len(grads)

    def body(*refs):
        src = refs[:nw]
        land = refs[nw:2 * nw]
        send, recv = refs[2 * nw], refs[2 * nw + 1]
        x, y, c = _place()
        for w in range(nw):
            for k, (px, py, pc) in enumerate(_other_devices(x, y, c)):
                cp = _remote(src[w].at[4 * px + 2 * py + pc], land[w].at[k], send.at[7 * w + k], recv.at[7 * w + k], (px, py, pc))
                cp.wait_send()
                cp.wait_recv()

    outs = pl.pallas_call(
        body, name=name, in_specs=[_HBM] * (2 * nw) + [_SEM, _SEM, _ANY], out_specs=[_HBM] * (2 * nw),
        out_shape=[pltpu.HBM(a.shape, a.dtype) for a in list(grads) + list(got)],
        input_output_aliases={i: i for i in range(2 * nw)},
        compiler_params=pltpu.CompilerParams(has_side_effects=_EFFECT),
    )(*grads, *got, *sems, after)
    return list(outs[:nw]), list(outs[nw:])


def _swap_halves_start(finals):
    nw = len(finals)

    def body(*refs):
        buf = refs[nw:2 * nw]
        send, recv, token = refs[2 * nw:]
        x, y, c = _place()
        for w in range(nw):
            _remote(buf[w].at[c], buf[w].at[c], send.at[w], recv.at[w], (x, y, 1 - c)).start()
        token[...] = jnp.zeros_like(token)

    outs = pl.pallas_call(
        body, name="rs_swap_start", in_specs=[_HBM] * nw, out_specs=[_HBM] * nw + [_SEM, _SEM, _VMEM],
        out_shape=[pltpu.HBM(g.shape, g.dtype) for g in finals] + [pltpu.SemaphoreType.DMA((nw,))] * 2
        + [jax.ShapeDtypeStruct((8, 128), f32)],
        input_output_aliases={i: i for i in range(nw)},
        compiler_params=pltpu.CompilerParams(has_side_effects=_EFFECT),
    )(*[pltpu.with_memory_space_constraint(g, pltpu.HBM) for g in finals])
    return list(outs[:nw]), (outs[nw], outs[nw + 1]), outs[nw + 2]


def _swap_halves_wait(bufs, sems, after):
    nw = len(bufs)

    def body(*refs):
        buf = refs[:nw]
        send, recv = refs[nw], refs[nw + 1]
        x, y, c = _place()
        for w in range(nw):
            cp = _remote(buf[w].at[c], buf[w].at[1 - c], send.at[w], recv.at[w], (x, y, 1 - c))
            cp.wait_send()
            cp.wait_recv()

    return pl.pallas_call(
        body, name="rs_swap_wait", in_specs=[_HBM] * nw + [_SEM, _SEM, _ANY], out_specs=[_HBM] * nw,
        out_shape=[pltpu.HBM(g.shape, g.dtype) for g in bufs],
        input_output_aliases={i: i for i in range(nw)},
        compiler_params=pltpu.CompilerParams(has_side_effects=_EFFECT),
    )(*bufs, *sems, after)


def _half_slices(shape, h):
    rows, cols = shape
    if cols % 256 == 0:
        return (slice(None), slice(h * (cols // 2), (h + 1) * (cols // 2)))
    return (slice(h * (rows // 2), (h + 1) * (rows // 2)), slice(None))


def _allreduce_small(parts, after):
    n = len(parts)

    def body(*refs):
        src = refs[:n]
        refs = refs[n + 1:]
        out = refs[:n]
        sib = refs[n:2 * n]
        chip_sum = refs[2 * n:3 * n]
        slots = refs[3 * n:4 * n]
        pair_send, pair_recv, ici_send, ici_recv, swap_send, swap_recv = refs[4 * n:]
        x, y, c = _place()
        me_chip = 2 * x + y
        chips = _other_chips(x, y)
        pairs = [_remote(src[a], sib[a], pair_send.at[a], pair_recv.at[a], (x, y, 1 - c)) for a in range(n)]
        for rc in pairs:
            rc.start()
        for a in range(n):
            pairs[a].wait_recv()
            chip_sum[a][...] = src[a][...] + sib[a][...]
        for h in (0, 1):
            @pl.when(c == h)
            def _():
                sends = []
                for a in range(n):
                    idx = _half_slices(parts[a].shape, h)
                    for j, chip in enumerate(chips):
                        rc = _remote(chip_sum[a].at[idx], slots[a].at[me_chip].at[idx], ici_send.at[3 * a + j], ici_recv.at[3 * a + j], (*chip, h))
                        rc.start()
                        sends.append(rc)
                    slots[a][(me_chip,) + idx] = chip_sum[a][idx]
                for a in range(n):
                    idx = _half_slices(parts[a].shape, h)
                    for j, chip in enumerate(chips):
                        landed = slots[a].at[2 * chip[0] + chip[1]].at[idx]
                        _remote(landed, landed, ici_send.at[3 * a + j], ici_recv.at[3 * a + j], (x, y, c)).wait_recv()
                    total = slots[a][(0,) + idx]
                    for s in range(1, 4):
                        total = total + slots[a][(s,) + idx]
                    out[a][idx] = total
                    rc = _remote(out[a].at[idx], out[a].at[idx], swap_send.at[a], swap_recv.at[a], (x, y, 1 - h))
                    rc.start()
                    sends.append(rc)
                for a in range(n):
                    other = out[a].at[_half_slices(parts[a].shape, 1 - h)]
                    _remote(other, other, swap_send.at[a], swap_recv.at[a], (x, y, c)).wait_recv()
                for rc in sends:
                    rc.wait_send()
        for rc in pairs:
            rc.wait_send()

    return pl.pallas_call(
        body, name="allreduce_small", in_specs=[_VMEM] * n + [_ANY], out_specs=[_VMEM] * n,
        out_shape=[jax.ShapeDtypeStruct(p.shape, f32) for p in parts],
        scratch_shapes=[pltpu.VMEM(p.shape, f32) for p in parts] * 2 + [pltpu.VMEM((4,) + p.shape, f32) for p in parts]
        + [pltpu.SemaphoreType.DMA((n,)), pltpu.SemaphoreType.DMA((n,)), pltpu.SemaphoreType.DMA((3 * n,)),
           pltpu.SemaphoreType.DMA((3 * n,)), pltpu.SemaphoreType.DMA((n,)), pltpu.SemaphoreType.DMA((n,))],
        compiler_params=pltpu.CompilerParams(vmem_limit_bytes=_VMEM_LIMIT_BYTES),
    )(*parts, after)


def _local_step(x, mem, tgt, g_mix, g_xattn, g_mem, g_ffn, g_final, cb, lg, lb, pw, ps, fb, started, relay, weights, reduce,
                n_seq, seq, n_mem):
    t, d = x.shape
    f = fb.shape[1] // 2
    c = cb.shape[1]
    h1 = _rms_fwd("norm_mix", x, g_mix, after=started)
    relay(0, h1)
    w_in, cw, fw = weights(0, h1)
    u = _mm_nn("proj_in", h1, w_in, _ACT, w_in.shape[1])
    y, hc = _mix_fwd(u, cw, cb, lg, lb, pw, ps, seq)
    relay(1, y)
    w_out, w_q, w_kv, w_o = weights(1, y)
    x1, h2 = _proj_residual_norm("proj_out", y, w_out, x, g_xattn)
    q = _mm_nn("proj_q", h2, w_q, _ACT, d)
    mem_n = _rms_fwd("norm_mem", mem, g_mem)
    kv = _mm_nn("proj_kv", mem_n, w_kv, _ACT, 2 * d)
    o = _attn_fwd(q, kv, n_seq, seq, n_mem)
    x2, h3 = _proj_residual_norm("proj_o", o, w_o, x1, g_ffn, after=relay(2, o))
    w_up, w_down = weights(2, h3)
    up = _mm_nn("proj_up", h3, w_up, _ACT, f, split_out=True)
    a = _ffn_gate_fwd(up, fw, fb, seq)
    dx3, dx3b, dg_final, loss = _proj_loss_bwd("proj_down", a, w_down, x2, g_final, tgt)
    da = _mm_nt("d_act", dx3b, w_down, _ACT)
    gw_down = _mm_tn_rows("dw_down", a, dx3b, f // 2, d // 2)
    dup, sums_g, sums_v = _ffn_gate_bwd(up, da, fw, fb, seq)
    gw_up = _mm_tn_pieces("dw_up", h3, dup, f // 2)
    token = reduce(0, [gw_down.reshape(8, -1, d), gw_up])
    dx2, dx2b, dg_ffn = _dproj_rms_bwd("d_h3", dup, w_up, x2, g_ffn, dx3, after=token)
    do = _mm_nt("d_o", dx2b, w_o, _ACT)
    gw_o = _mm_tn_rows("dw_o", o, dx2b, d, d // 2)
    dq, dkv = _attn_bwd(q, kv, do, n_seq, seq, n_mem)
    gw_q = _mm_tn_rows("dw_q", h2, dq, d, d // 2)
    gw_kv = _mm_tn_pieces("dw_kv", mem_n, dkv, d // 2)
    dmem_n = _mm_nt("d_mem_n", dkv, w_kv, f32)
    dg_mem = _rms_gain_grad("norm_mem_bwd", mem, dmem_n)
    dx1, dx1b, dg_xattn = _dproj_rms_bwd("d_h2", dq, w_q, x1, g_xattn, dx2)
    dy = _mm_nt("d_y", dx1b, w_out, _ACT)
    gw_out = _mm_tn_rows("dw_out", y, dx1b, d, d // 2)
    token = reduce(1, [gw_o.reshape(8, -1, d), gw_q.reshape(8, -1, d), gw_kv, gw_out.reshape(8, -1, d)])
    dhc, sums_norm = _mix_bwd_norm(hc, dy, lg, lb, token)
    du, d_cw, d_ps, d_pw = _mix_bwd_taps(u, dhc, dy, cw, pw, ps, seq)
    gw_in = _mm_tn_pieces("dw_in", h1, du, c * 3 // 4)
    token = reduce(2, [gw_in])
    grad_x, dg_mix = _dproj_rms_bwd("d_h1", du, w_in, x, g_mix, dx1, storage_copy=False, after=token)
    zero_row = jnp.zeros((1, d), f32)
    gains = jnp.concatenate([dg_mix, dg_xattn, dg_mem, dg_ffn, dg_final, jnp.pad(loss, ((0, 0), (0, d - 1))), zero_row, zero_row], axis=0)
    conv_rows = jnp.concatenate([sums_norm[2:3], sums_norm[0:1], sums_norm[1:2], d_ps[0:1], jnp.zeros((4, c), f32)], axis=0)
    ffn_rows = jnp.concatenate([sums_g, sums_v], axis=1)
    small = [gains, conv_rows, d_pw.reshape(-1, d_pw.shape[-1]), ffn_rows, d_cw]
    return grad_x, small


def kernel(x, mem, norm_mix_g, w_in, conv_dw_w, conv_dw_b, conv_ln_g, conv_ln_b, pool_w, pool_scale, w_out, norm_xattn_g, norm_mem_g, w_q, w_kv, w_o, norm_ffn_g, w_up, ffn_dw_w, ffn_dw_b, w_down, norm_final_g, loss_target, m_norm_mix_g, m_w_in, m_conv_dw_w, m_conv_dw_b, m_conv_ln_g, m_conv_ln_b, m_pool_w, m_pool_scale, m_w_out, m_norm_xattn_g, m_norm_mem_g, m_w_q, m_w_kv, m_w_o, m_norm_ffn_g, m_w_up, m_ffn_dw_w, m_ffn_dw_b, m_w_down, m_norm_final_g, v_norm_mix_g, v_w_in, v_conv_dw_w, v_conv_dw_b, v_conv_ln_g, v_conv_ln_b, v_pool_w, v_pool_scale, v_w_out, v_norm_xattn_g, v_norm_mem_g, v_w_q, v_w_kv, v_w_o, v_norm_ffn_g, v_w_up, v_ffn_dw_w, v_ffn_dw_b, v_w_down, v_norm_final_g):
    n_seq, seq, d = x.shape
    n_mem = mem.shape[1]
    chip = 2 * lax.axis_index("x") + lax.axis_index("y")

    place = jnp.stack([chip, lax.axis_index("c")]).astype(jnp.int32)

    col_w = [w_in, w_kv, w_up]
    row_w = [w_out, w_q, w_o, w_down]
    kw = conv_dw_w.shape[1]

    def padded_in_place(shard, rows):
        full = jnp.zeros((rows, 4 * shard.shape[1]), shard.dtype)
        return lax.dynamic_update_slice(full, shard, (0, chip * shard.shape[1]))

    first = list(_place_shards("place_w_in", place, [w_in[0]], [True]))
    first += [padded_in_place(conv_dw_w[0], _HALO), padded_in_place(ffn_dw_w[0], 8)]
    first, first_sems, token = _allgather_start("allgather_start_0", first, [True] * 3, [False, True, True], [[0, 1, 2]])
    rest = [w_kv, w_up, w_out, w_q, w_o, w_down]
    rest_flags = [True, True, False, False, False, False]
    rest = list(_place_shards("place_rest", place, [w[0] for w in rest], rest_flags, after=token))
    rest, rest_sems, all_started = _allgather_start("allgather_start_1", rest, rest_flags, [False] * 6, [[2, 3, 0, 4], [1, 5]])
    started = [(first, [True] * 3, [False, True, True], first_sems[0]),
               ([rest[i] for i in (2, 3, 0, 4)], [False, False, True, False], [False] * 4, rest_sems[0]),
               ([rest[i] for i in (1, 5)], [True, False], [False] * 2, rest_sems[1])]
    relayed = {}

    def relay(g, after):
        group_bufs, flags, wholes, group_sems = started[g]
        group_bufs, sibling_sems, relay_token = _allgather_relay("allgather_relay_%d" % g, group_bufs, flags, wholes, group_sems, after)
        relayed[g] = (group_bufs, sibling_sems)
        return relay_token

    def weights(g, after):
        group_bufs, sibling_sems = relayed[g]
        return _allgather_wait("allgather_wait_%d" % g, group_bufs, started[g][1], started[g][2], sibling_sems, after)

    names = ["w_in", "w_kv", "w_up", "w_out", "w_q", "w_o", "w_down"]
    reduce_groups = [["w_down", "w_up"], ["w_o", "w_q", "w_kv", "w_out"], ["w_in"]]
    in_flight = {}

    def reduce(g, grads):
        grads, lands, rs_sems, token = _grad_exchange_start("rs_start_%d" % g, grads)
        in_flight[g] = (grads, lands, rs_sems)
        return token

    grad_x, small = _local_step(
        x.reshape(n_seq * seq, d), mem.reshape(n_seq * n_mem, d), loss_target.reshape(n_seq * seq, d),
        norm_mix_g, norm_xattn_g, norm_mem_g, norm_ffn_g, norm_final_g.reshape(1, d),
        conv_dw_b, conv_ln_g, conv_ln_b, pool_w[0], pool_scale, ffn_dw_b, all_started, relay, weights, reduce,
        n_seq, seq, n_mem)

    landed = {}
    for g, members in enumerate(reduce_groups):
        grads, lands, rs_sems = in_flight[g]
        grads, lands = _grad_exchange_wait("rs_wait_%d" % g, grads, lands, rs_sems, grad_x)
        landed.update(zip(members, zip(grads, lands)))
    finals = _sum_partials("rs_sum", place, [landed[n][0] for n in names], [landed[n][1] for n in names])
    finals, swap_sems, token = _swap_halves_start(finals)

    gains, conv_rows, d_pw, ffn_rows, d_cw = _allreduce_small(small, token)
    loss = gains[5, 0]
    shard_grads = _swap_halves_wait(finals, swap_sems, gains)

    outs = {}
    big_w = dict(zip(names, col_w + row_w))
    big_m = dict(w_in=m_w_in, w_kv=m_w_kv, w_up=m_w_up, w_out=m_w_out, w_q=m_w_q, w_o=m_w_o, w_down=m_w_down)
    big_v = dict(w_in=v_w_in, w_kv=v_w_kv, w_up=v_w_up, w_out=v_w_out, w_q=v_w_q, w_o=v_w_o, w_down=v_w_down)
    big_quads = [(big_w[n], g.reshape(big_w[n].shape[1:]), big_m[n], big_v[n]) for n, g in zip(names, shard_grads)]
    outs.update(zip(names, _adamw_shards(big_quads)))

    f2 = ffn_dw_b.shape[1]
    cs_c = conv_dw_w.shape[2]
    cs_f = ffn_dw_w.shape[2]
    g_cw = lax.dynamic_slice(d_cw, (0, chip * cs_c), (kw, cs_c)).reshape(conv_dw_w.shape)
    g_fw = lax.dynamic_slice(ffn_rows, (1, chip * cs_f), (ffn_dw_w.shape[1], cs_f)).reshape(ffn_dw_w.shape)
    small_params = [
        ("norm_mix_g", norm_mix_g, gains[0:1], m_norm_mix_g, v_norm_mix_g),
        ("conv_dw_w", conv_dw_w, g_cw, m_conv_dw_w, v_conv_dw_w),
        ("conv_dw_b", conv_dw_b, conv_rows[0:1], m_conv_dw_b, v_conv_dw_b),
        ("conv_ln_g", conv_ln_g, conv_rows[1:2], m_conv_ln_g, v_conv_ln_g),
        ("conv_ln_b", conv_ln_b, conv_rows[2:3], m_conv_ln_b, v_conv_ln_b),
        ("pool_w", pool_w, d_pw.reshape(pool_w.shape), m_pool_w, v_pool_w),
        ("pool_scale", pool_scale, conv_rows[3:4], m_pool_scale, v_pool_scale),
        ("norm_xattn_g", norm_xattn_g, gains[1:2], m_norm_xattn_g, v_norm_xattn_g),
        ("norm_mem_g", norm_mem_g, gains[2:3], m_norm_mem_g, v_norm_mem_g),
        ("norm_ffn_g", norm_ffn_g, gains[3:4], m_norm_ffn_g, v_norm_ffn_g),
        ("ffn_dw_w", ffn_dw_w, g_fw, m_ffn_dw_w, v_ffn_dw_w),
        ("ffn_dw_b", ffn_dw_b, ffn_rows[0:1, :f2], m_ffn_dw_b, v_ffn_dw_b),
        ("norm_final_g", norm_final_g.reshape(1, d), gains[4:5], m_norm_final_g.reshape(1, d), v_norm_final_g.reshape(1, d)),
    ]
    quads = []
    for _, w, g, m, v in small_params:
        shape2 = (-1, w.shape[-1])
        quads.append((w.reshape(shape2), g.reshape(shape2), m.reshape(shape2), v.reshape(shape2)))
    for (n, w, g, _, _), (delta, new_m, new_v) in zip(small_params, _adamw_small(quads)):
        shape = norm_final_g.shape if n == "norm_final_g" else w.shape
        outs[n] = (g.reshape(shape), delta.reshape(shape), new_m.reshape(shape), new_v.reshape(shape))

    order = ["norm_mix_g", "w_in", "conv_dw_w", "conv_dw_b", "conv_ln_g", "conv_ln_b", "pool_w", "pool_scale", "w_out",
             "norm_xattn_g", "norm_mem_g", "w_q", "w_kv", "w_o", "norm_ffn_g", "w_up", "ffn_dw_w", "ffn_dw_b", "w_down",
             "norm_final_g"]
    return (loss, grad_x.reshape(x.shape), *[outs[n][0] for n in order], *[outs[n][1] for n in order],
            *[outs[n][2] for n in order], *[outs[n][3] for n in order])
```

```python
import jax
import jax.numpy as jnp
from jax import lax
from jax.experimental import pallas as pl
from jax.experimental.pallas import tpu as pltpu

f32 = jnp.float32
_ACT = jnp.bfloat16

EPS = 1e-6
POOL_WINDOWS = (2, 4, 8, 16)
XATTN_HEADS = 4
ADAM_LR = 0.001
ADAM_B1 = 0.9
ADAM_B2 = 0.999
ADAM_EPS = 1e-08
ADAM_WD = 0.01
ADAM_STEP = 10

_VMEM_LIMIT_BYTES = 56 * 1024 * 1024
_MESH = pl.DeviceIdType.MESH
_ANY = pl.BlockSpec(memory_space=pl.ANY)
_VMEM = pl.BlockSpec(memory_space=pltpu.VMEM)
_HBM = pl.BlockSpec(memory_space=pltpu.HBM)
_SEM = pl.BlockSpec(memory_space=pltpu.SEMAPHORE)
_EFFECT = pltpu.SideEffectType.DATAFLOW_SIDE_EFFECTING

_NN = (((1,), (0,)), ((), ()))
_NT = (((1,), (1,)), ((), ()))
_TN = (((0,), (0,)), ((), ()))


def _params(n_grid):
    return pltpu.CompilerParams(dimension_semantics=("arbitrary",) * n_grid, vmem_limit_bytes=_VMEM_LIMIT_BYTES)


def _sigmoid(v):
    return 1.0 / (1.0 + jnp.exp(-v))


def _dot(a, b, dims):
    return lax.dot_general(a, b, dims, preferred_element_type=f32)


def _mm(name, a, b, *, dims, grid, a_spec, b_spec, o_spec, out_shape):
    def body(a_ref, b_ref, o_ref):
        o_ref[...] = _dot(a_ref[...], b_ref[...], dims).astype(o_ref.dtype)

    return pl.pallas_call(
        body, name=name, grid=grid, in_specs=[a_spec, b_spec], out_specs=o_spec, out_shape=out_shape,
        compiler_params=_params(len(grid)),
    )(a, b)


_NARROW = 2816


def _row_tile(m, width=_NARROW + 1):
    return min(1024 if width <= _NARROW else 512, m)


def _mm_nn(name, a, b, out_dtype, tn, split_out=False):
    m, k = a.shape
    n = b.shape[1]
    tm = _row_tile(m, max(k, tn))
    if split_out:
        out_shape = jax.ShapeDtypeStruct((n // tn, m, tn), out_dtype)
        o_spec = pl.BlockSpec((None, tm, tn), lambda j, i: (j, i, 0))
    else:
        out_shape = jax.ShapeDtypeStruct((m, n), out_dtype)
        o_spec = pl.BlockSpec((tm, tn), lambda j, i: (i, j))
    return _mm(
        name, a, b, dims=_NN, grid=(n // tn, m // tm),
        a_spec=pl.BlockSpec((tm, k), lambda j, i: (i, 0)), b_spec=pl.BlockSpec((k, tn), lambda j, i: (0, j)),
        o_spec=o_spec, out_shape=out_shape,
    )


def _mm_nt(name, a, b, out_dtype):
    n, kc = b.shape
    m = a.shape[0]
    tm = _row_tile(m, max(n, kc))
    return _mm(
        name, a, b, dims=_NT, grid=(m // tm,),
        a_spec=pl.BlockSpec((tm, kc), lambda i: (i, 0)),
        b_spec=pl.BlockSpec((n, kc), lambda i: (0, 0), pipeline_mode=pl.Buffered(1)),
        o_spec=pl.BlockSpec((tm, n), lambda i: (i, 0)),
        out_shape=jax.ShapeDtypeStruct((m, n), out_dtype),
    )


def _mm_tn_rows(name, a, b, tka, tn):
    m, ka = a.shape
    nb = b.shape[1]
    return _mm(
        name, a, b, dims=_TN, grid=(ka // tka, nb // tn),
        a_spec=pl.BlockSpec((m, tka), lambda i, j: (0, i)), b_spec=pl.BlockSpec((m, tn), lambda i, j: (0, j)),
        o_spec=pl.BlockSpec((tka, tn), lambda i, j: (i, j)),
        out_shape=jax.ShapeDtypeStruct((ka, nb), _ACT),
    )


def _mm_tn_pieces(name, a, b, cs):
    m, ka = a.shape
    if b.ndim == 3:
        b_spec = pl.BlockSpec((None, m, cs), lambda i, j: (j // 2, 0, j % 2))
    else:
        b_spec = pl.BlockSpec((m, cs), lambda i, j: (0, j))
    return _mm(
        name, a, b, dims=_TN, grid=(2, 4),
        a_spec=pl.BlockSpec((m, ka // 2), lambda i, j: (0, i)), b_spec=b_spec,
        o_spec=pl.BlockSpec((None, ka // 2, cs), lambda i, j: (2 * j + i, 0, 0)),
        out_shape=jax.ShapeDtypeStruct((8, ka // 2, cs), _ACT),
    )


def _after(after):
    return ([], []) if after is None else ([after], [_ANY])


def _rms_fwd(name, x, g, after=None):
    t, d = x.shape
    tm = _row_tile(t, d)
    more, more_specs = _after(after)

    def body(x_ref, g_ref, *refs):
        h_ref = refs[-1]
        xv = x_ref[...]
        r = lax.rsqrt(jnp.mean(xv * xv, axis=-1, keepdims=True) + EPS)
        h_ref[...] = (xv * r * g_ref[...]).astype(h_ref.dtype)

    return pl.pallas_call(
        body, name=name, grid=(t // tm,),
        in_specs=[pl.BlockSpec((tm, d), lambda i: (i, 0)), pl.BlockSpec((1, d), lambda i: (0, 0))] + more_specs,
        out_specs=pl.BlockSpec((tm, d), lambda i: (i, 0)), out_shape=jax.ShapeDtypeStruct((t, d), _ACT),
        compiler_params=_params(1),
    )(x, g, *more)


def _fused_rows(name, a, b, product, a_spec, tm, extras, extra_specs, out_shape, out_specs, epilogue):
    ne = len(extras)
    m, d = extras[0].shape
    n = m // tm

    def lagged(spec, shift):
        if spec.index_map is None:
            return spec
        return pl.BlockSpec(spec.block_shape, lambda i, f=spec.index_map: f(shift(i)))

    def body(a_ref, b_ref, *refs):
        waiting = refs[-1]

        @pl.when(pl.program_id(0) == 0)
        def _():
            waiting[...] = jnp.zeros_like(waiting)

        product_of_last = waiting[...]
        product_now = product(a_ref, b_ref)
        epilogue(product_of_last, refs[:ne], refs[ne:-1], pl.program_id(0))
        waiting[...] = product_now

    return pl.pallas_call(
        body, name=name, grid=(n + 1,),
        in_specs=[lagged(a_spec, lambda i: jnp.minimum(i, n - 1)),
                  pl.BlockSpec(b.shape, lambda i: (0, 0), pipeline_mode=pl.Buffered(1)),
                  *[lagged(s, lambda i: jnp.maximum(i - 1, 0)) for s in extra_specs]],
        out_specs=[lagged(s, lambda i: jnp.maximum(i - 1, 0)) for s in out_specs], out_shape=out_shape,
        scratch_shapes=[pltpu.VMEM((tm, d), f32)], compiler_params=_params(1),
    )(a, b, *extras)


def _proj_residual_norm(name, a, b, res, g, after=None):
    m, k = a.shape
    d = b.shape[1]
    tm = _row_tile(m, max(k, d))

    def epilogue(p, ins, outs, step):
        xv = p + ins[0][...]
        outs[0][...] = xv
        r = lax.rsqrt(jnp.mean(xv * xv, axis=-1, keepdims=True) + EPS)
        outs[1][...] = (xv * r * ins[1][...]).astype(outs[1].dtype)

    row = pl.BlockSpec((tm, d), lambda i: (i, 0))
    return _fused_rows(
        name, a, b, lambda a_ref, b_ref: _dot(a_ref[...], b_ref[...], _NN), pl.BlockSpec((tm, k), lambda i: (i, 0)), tm,
        [res, g] + _after(after)[0], [row, pl.BlockSpec((1, d), lambda i: (0, 0))] + _after(after)[1],
        [jax.ShapeDtypeStruct((m, d), f32), jax.ShapeDtypeStruct((m, d), _ACT)], [row, row], epilogue)


def _dproj_rms_bwd(name, a, b, x, g, dres, storage_copy=True, after=None):
    m, d = x.shape
    if a.ndim == 3:
        nh, _, kh = a.shape
        tm = _row_tile(m, nh * kh)
        a_spec = pl.BlockSpec((nh, tm, kh), lambda i: (0, i, 0))

        def product(a_ref, b_ref):
            p = _dot(a_ref[0], b_ref[:, 0:kh], _NT)
            for h in range(1, nh):
                p = p + _dot(a_ref[h], b_ref[:, h * kh:(h + 1) * kh], _NT)
            return p
    else:
        tm = _row_tile(m, max(a.shape[1], d))
        a_spec = pl.BlockSpec((tm, a.shape[1]), lambda i: (i, 0))

        def product(a_ref, b_ref):
            return _dot(a_ref[...], b_ref[...], _NT)

    def epilogue(dhv, ins, outs, step):
        x_ref, g_ref, dres_ref = ins[:3]
        dg_ref = outs[-1]

        @pl.when(step == 0)
        def _():
            dg_ref[...] = jnp.zeros_like(dg_ref)

        xv = x_ref[...]
        r = lax.rsqrt(jnp.mean(xv * xv, axis=-1, keepdims=True) + EPS)
        xn = xv * r
        dxn = dhv * g_ref[...]
        dx = r * (dxn - xn * jnp.mean(dxn * xn, axis=-1, keepdims=True)) + dres_ref[...]
        outs[0][...] = dx
        if storage_copy:
            outs[1][...] = dx.astype(outs[1].dtype)
        dg_ref[...] += jnp.sum(dhv * xn, axis=0, keepdims=True)

    row = pl.BlockSpec((tm, d), lambda i: (i, 0))
    vec = pl.BlockSpec((1, d), lambda i: (0, 0))
    copies = [jax.ShapeDtypeStruct((m, d), _ACT)] if storage_copy else []
    return _fused_rows(
        name, a, b, product, a_spec, tm, [x, g, dres] + _after(after)[0], [row, vec, row] + _after(after)[1],
        [jax.ShapeDtypeStruct((m, d), f32)] + copies + [jax.ShapeDtypeStruct((1, d), f32)],
        [row] * (1 + len(copies)) + [vec], epilogue)


def _proj_loss_bwd(name, a, b, res, g, tgt):
    m, k = a.shape
    d = b.shape[1]
    tm = _row_tile(m)

    def epilogue(p, ins, outs, step):
        res_ref, g_ref, t_ref = ins
        dx_ref, dxb_ref, dg_ref, loss_ref = outs

        @pl.when(step == 0)
        def _():
            dg_ref[...] = jnp.zeros_like(dg_ref)
            loss_ref[...] = jnp.zeros_like(loss_ref)

        counts = jnp.where(step > 0, 1.0, 0.0)
        xv = p + res_ref[...]
        gv = g_ref[...]
        r = lax.rsqrt(jnp.mean(xv * xv, axis=-1, keepdims=True) + EPS)
        xn = xv * r
        err = xn * gv - t_ref[...]
        loss_ref[...] += counts * 0.5 * jnp.sum(jnp.mean(err * err, axis=-1, keepdims=True), axis=0, keepdims=True)
        dout = err * (1.0 / d)
        dxn = dout * gv
        dx = r * (dxn - xn * jnp.mean(dxn * xn, axis=-1, keepdims=True))
        dx_ref[...] = dx
        dxb_ref[...] = dx.astype(dxb_ref.dtype)
        dg_ref[...] += counts * jnp.sum(dout * xn, axis=0, keepdims=True)

    row = pl.BlockSpec((tm, d), lambda i: (i, 0))
    vec = pl.BlockSpec((1, d), lambda i: (0, 0))
    return _fused_rows(
        name, a, b, lambda a_ref, b_ref: _dot(a_ref[...], b_ref[...], _NN), pl.BlockSpec((tm, k), lambda i: (i, 0)), tm,
        [res, g, tgt], [row, vec, row],
        [jax.ShapeDtypeStruct((m, d), f32), jax.ShapeDtypeStruct((m, d), _ACT), jax.ShapeDtypeStruct((1, d), f32),
         jax.ShapeDtypeStruct((1, 1), f32)],
        [row, row, vec, pl.BlockSpec((1, 1), lambda i: (0, 0))], epilogue)


def _rms_gain_grad(name, x, dh):
    t, d = x.shape
    tm = _row_tile(t)

    def body(x_ref, dh_ref, dg_ref):
        @pl.when(pl.program_id(0) == 0)
        def _():
            dg_ref[...] = jnp.zeros_like(dg_ref)

        xv = x_ref[...]
        r = lax.rsqrt(jnp.mean(xv * xv, axis=-1, keepdims=True) + EPS)
        dg_ref[...] += jnp.sum(dh_ref[...] * (xv * r), axis=0, keepdims=True)

    row = pl.BlockSpec((tm, d), lambda i: (i, 0))
    return pl.pallas_call(
        body, name=name, grid=(t // tm,), in_specs=[row, row], out_specs=pl.BlockSpec((1, d), lambda i: (0, 0)),
        out_shape=jax.ShapeDtypeStruct((1, d), f32), compiler_params=_params(1),
    )(x, dh)


_CONV_ROWS = 512
_CHUNK = 64
_HALO = 32


def _pool_counts(pos, w):
    return jnp.minimum(pos + 1.0, float(w))


def _rows_from(win, start, rows):
    if start % 8 == 0:
        return win[start:start + rows, :]
    n = win.shape[0]
    return pltpu.roll(win, n - start % 8, axis=0)[start - start % 8:start - start % 8 + rows, :]


def _tap_rows(buf, starts, rows):
    for residue in range(8):
        group = [(k, s) for k, s in starts.items() if s % 8 == residue]
        if group:
            lo = min(s for _, s in group) - residue
            hi = max(s for _, s in group) - residue + rows + (8 if residue else 0)
            win = buf[lo:hi, :]
            if residue:
                win = pltpu.roll(win, hi - lo - residue, axis=0)
            for k, s in group:
                yield k, win[s - residue - lo:s - residue - lo + rows, :]


def _mix_fwd(u, cw, cb, lg, lb, pw, ps, seq):
    t, c3 = u.shape
    c = c3 // 3
    kw = 31
    tm = min(_CONV_ROWS, seq)
    tps = seq // tm
    gd = c // len(POOL_WINDOWS)

    def body(u_ref, uh_ref, cw_ref, cb_ref, lg_ref, lb_ref, pw_ref, ps_ref, y_ref, hc_ref, hgbuf, pbuf):
        i = pl.program_id(0)
        keep = jnp.where(i % tps == 0, 0.0, 1.0)
        um = u_ref[...].astype(f32)
        uh = uh_ref[...].astype(f32) * keep
        hgbuf[0:_HALO, :] = uh[:, 0:c] * _sigmoid(uh[:, c:2 * c])
        hgbuf[_HALO:_HALO + tm, :] = um[:, 0:c] * _sigmoid(um[:, c:2 * c])
        pbuf[0:_HALO, :] = uh[:, 2 * c:]
        pbuf[_HALO:_HALO + tm, :] = um[:, 2 * c:]
        for r0 in range(0, tm, _CHUNK):
            acc = jnp.broadcast_to(cb_ref[...], (_CHUNK, c))
            for k, rows in _tap_rows(hgbuf, {k: r0 + _HALO - (kw - 1) + k for k in range(kw)}, _CHUNK):
                acc = acc + cw_ref[k:k + 1, :] * rows
            hc_ref[r0:r0 + _CHUNK, :] = acc
            mu = jnp.mean(acc, axis=-1, keepdims=True)
            xc = acc - mu
            var = jnp.mean(xc * xc, axis=-1, keepdims=True)
            hl = xc * lax.rsqrt(var + EPS) * lg_ref[...] + lb_ref[...]
            y_ref[r0:r0 + _CHUNK, 0:c] = (hl * _sigmoid(hl)).astype(y_ref.dtype)
        pos = ((i % tps) * tm).astype(f32) + lax.broadcasted_iota(jnp.int32, (tm, 1), 0).astype(f32)
        for gi, w in enumerate(POOL_WINDOWS):
            sl = slice(gi * gd, (gi + 1) * gd)
            v = pbuf[_HALO:_HALO + tm, sl]
            s = v
            for j in range(1, w):
                s = s + pbuf[_HALO - j:_HALO - j + tm, sl]
            pooled = s / _pool_counts(pos, w) - v
            mixed = _dot(pooled.astype(_ACT), pw_ref[gi].astype(_ACT), _NN)
            y_ref[:, c + gi * gd:c + (gi + 1) * gd] = (mixed * ps_ref[:, sl]).astype(y_ref.dtype)

    hb = tm // _HALO
    full = lambda shape: pl.BlockSpec(shape, lambda i: (0,) * len(shape))
    return pl.pallas_call(
        body, name="mix_fwd", grid=(t // tm,),
        in_specs=[pl.BlockSpec((tm, c3), lambda i: (i, 0)),
                  pl.BlockSpec((_HALO, c3), lambda i: (jnp.maximum(i * hb - 1, 0), 0)),
                  full((_HALO, c)), full((1, c)), full((1, c)), full((1, c)), full((len(POOL_WINDOWS), gd, gd)), full((1, c))],
        out_specs=[pl.BlockSpec((tm, 2 * c), lambda i: (i, 0)), pl.BlockSpec((tm, c), lambda i: (i, 0))],
        out_shape=[jax.ShapeDtypeStruct((t, 2 * c), _ACT), jax.ShapeDtypeStruct((t, c), f32)],
        scratch_shapes=[pltpu.VMEM((_HALO + tm, c), f32), pltpu.VMEM((_HALO + tm, c), f32)],
        compiler_params=_params(1),
    )(u, u, cw, cb, lg, lb, pw, ps)


def _mix_bwd_norm(hc, dy, lg, lb, after):
    t, c = hc.shape
    tm = _row_tile(t, c)

    def body(hc_ref, dy_ref, lg_ref, lb_ref, after_ref, dhc_ref, sums_ref):
        @pl.when(pl.program_id(0) == 0)
        def _():
            sums_ref[...] = jnp.zeros_like(sums_ref)

        hcv = hc_ref[...]
        mu = jnp.mean(hcv, axis=-1, keepdims=True)
        xc = hcv - mu
        rstd = lax.rsqrt(jnp.mean(xc * xc, axis=-1, keepdims=True) + EPS)
        n = xc * rstd
        hl = n * lg_ref[...] + lb_ref[...]
        sg = _sigmoid(hl)
        dhl = dy_ref[...].astype(f32) * (sg * (1.0 + hl * (1.0 - sg)))
        dn = dhl * lg_ref[...]
        dhc = rstd * (dn - jnp.mean(dn, axis=-1, keepdims=True) - n * jnp.mean(dn * n, axis=-1, keepdims=True))
        dhc_ref[...] = dhc
        sums_ref[0:1, :] += jnp.sum(dhl * n, axis=0, keepdims=True)
        sums_ref[1:2, :] += jnp.sum(dhl, axis=0, keepdims=True)
        sums_ref[2:3, :] += jnp.sum(dhc, axis=0, keepdims=True)

    row = pl.BlockSpec((tm, c), lambda i: (i, 0))
    vec = pl.BlockSpec((1, c), lambda i: (0, 0))
    return pl.pallas_call(
        body, name="mix_bwd_norm", grid=(t // tm,), in_specs=[row, row, vec, vec, _ANY],
        out_specs=[row, pl.BlockSpec((8, c), lambda i: (0, 0))],
        out_shape=[jax.ShapeDtypeStruct((t, c), f32), jax.ShapeDtypeStruct((8, c), f32)],
        compiler_params=_params(1),
    )(hc, dy, lg, lb, after)


def _mix_bwd_taps(u, dhc, dy, cw, pw, ps, seq):
    t, c3 = u.shape
    c = c3 // 3
    kw = 31
    tm = min(_CONV_ROWS, seq)
    tps = seq // tm
    ng = len(POOL_WINDOWS)
    gd = c // ng
    nh = 16

    def body(u_ref, uh_ref, dhc_ref, dhcn_ref, dy_ref, dyn_ref, cw_ref, pw_ref, ps_ref,
             du_ref, dcw_ref, dps_ref, dpw_ref, hgbuf, dcbuf, pbuf, dpbuf):
        i = pl.program_id(0)
        keep_prev = jnp.where(i % tps == 0, 0.0, 1.0)
        keep_next = jnp.where(i % tps == tps - 1, 0.0, 1.0)

        @pl.when(i == 0)
        def _():
            dcw_ref[...] = jnp.zeros_like(dcw_ref)
            dps_ref[...] = jnp.zeros_like(dps_ref)
            dpw_ref[...] = jnp.zeros_like(dpw_ref)

        uh = uh_ref[...].astype(f32) * keep_prev
        hgbuf[0:_HALO, :] = uh[:, 0:c] * _sigmoid(uh[:, c:2 * c])
        pbuf[0:_HALO, :] = uh[:, 2 * c:]
        um = u_ref[...].astype(f32)
        hgbuf[_HALO:_HALO + tm, :] = um[:, 0:c] * _sigmoid(um[:, c:2 * c])
        pbuf[_HALO:_HALO + tm, :] = um[:, 2 * c:]
        dcbuf[0:tm, :] = dhc_ref[...]
        dcbuf[tm:tm + _HALO, :] = dhcn_ref[...] * keep_next
        tap_sums = [None] * kw
        for r0 in range(0, tm, _CHUNK):
            dh = dcbuf[r0:r0 + _CHUNK, :]
            acc = jnp.zeros((_CHUNK, c), f32)
            for k, rows in _tap_rows(hgbuf, {k: r0 + _HALO - (kw - 1) + k for k in range(kw)}, _CHUNK):
                part = (dh * rows).reshape(_CHUNK // 8, 8, c).sum(axis=0)
                tap_sums[k] = part if tap_sums[k] is None else tap_sums[k] + part
            for k, rows in _tap_rows(dcbuf, {k: r0 + (kw - 1) - k for k in range(kw)}, _CHUNK):
                acc = acc + cw_ref[k:k + 1, :] * rows
            val = u_ref[r0:r0 + _CHUNK, 0:c].astype(f32)
            sg = _sigmoid(u_ref[r0:r0 + _CHUNK, c:2 * c].astype(f32))
            du_ref[r0:r0 + _CHUNK, 0:c] = (acc * sg).astype(du_ref.dtype)
            du_ref[r0:r0 + _CHUNK, c:2 * c] = (acc * val * sg * (1.0 - sg)).astype(du_ref.dtype)
        for k in range(kw):
            dcw_ref[k:k + 1, :] += jnp.sum(tap_sums[k], axis=0, keepdims=True)
        base = ((i % tps) * tm).astype(f32)
        pos = base + lax.broadcasted_iota(jnp.int32, (tm, 1), 0).astype(f32)
        pos_next = base + float(tm) + lax.broadcasted_iota(jnp.int32, (nh, 1), 0).astype(f32)
        for gi, w in enumerate(POOL_WINDOWS):
            sl = slice(gi * gd, (gi + 1) * gd)
            v = pbuf[_HALO:_HALO + tm, sl]
            s = v
            for j in range(1, w):
                s = s + pbuf[_HALO - j:_HALO - j + tm, sl]
            cnt = _pool_counts(pos, w)
            pooled = (s / cnt - v).astype(_ACT)
            pwg = pw_ref[gi].astype(_ACT)
            mixed = _dot(pooled, pwg, _NN)
            dyp = dy_ref[:, sl].astype(f32)
            dps_ref[0:1, sl] += jnp.sum(dyp * mixed, axis=0, keepdims=True)
            dmix = (dyp * ps_ref[:, sl]).astype(_ACT)
            dpw_ref[gi] += _dot(pooled, dmix, _TN)
            dmix_next = (dyn_ref[:, sl].astype(f32) * ps_ref[:, sl] * keep_next).astype(_ACT)
            dpool = _dot(dmix, pwg, _NT)
            dpbuf[0:tm, sl] = dpool / cnt
            dpbuf[tm:tm + nh, sl] = _dot(dmix_next, pwg, _NT) / _pool_counts(pos_next, w)
            acc = -dpool
            for j in range(w):
                acc = acc + dpbuf[j:j + tm, sl]
            du_ref[:, 2 * c + gi * gd:2 * c + (gi + 1) * gd] = acc.astype(du_ref.dtype)

    hb = tm // _HALO
    n_halo = t // _HALO
    n_nh = t // nh
    full = lambda shape: pl.BlockSpec(shape, lambda i: (0,) * len(shape))
    return pl.pallas_call(
        body, name="mix_bwd_taps", grid=(t // tm,),
        in_specs=[pl.BlockSpec((tm, c3), lambda i: (i, 0)),
                  pl.BlockSpec((_HALO, c3), lambda i: (jnp.maximum(i * hb - 1, 0), 0)),
                  pl.BlockSpec((tm, c), lambda i: (i, 0)),
                  pl.BlockSpec((_HALO, c), lambda i: (jnp.minimum((i + 1) * hb, n_halo - 1), 0)),
                  pl.BlockSpec((tm, c), lambda i: (i, 1)),
                  pl.BlockSpec((nh, c), lambda i: (jnp.minimum((i + 1) * (tm // nh), n_nh - 1), 1)),
                  full((_HALO, c)), full((ng, gd, gd)), full((1, c))],
        out_specs=[pl.BlockSpec((tm, c3), lambda i: (i, 0)), full((_HALO, c)), full((8, c)), full((ng, gd, gd))],
        out_shape=[jax.ShapeDtypeStruct((t, c3), _ACT), jax.ShapeDtypeStruct((_HALO, c), f32),
                   jax.ShapeDtypeStruct((8, c), f32), jax.ShapeDtypeStruct((ng, gd, gd), f32)],
        scratch_shapes=[pltpu.VMEM((_HALO + tm, c), f32), pltpu.VMEM((tm + _HALO, c), f32),
                        pltpu.VMEM((_HALO + tm, c), f32), pltpu.VMEM((tm + nh, c), f32)],
        compiler_params=_params(1),
    )(u, u, dhc, dhc, dy, dy, cw, pw, ps)


def _attn_fwd(q, kv, n_seq, seq, n_mem):
    t, d = q.shape
    dh = d // XATTN_HEADS
    tq = min(1024, seq)
    nq = seq // tq
    scale = dh ** -0.5

    def body(q_ref, kv_ref, o_ref):
        for h in range(XATTN_HEADS):
            cols = slice(h * dh, (h + 1) * dh)
            s = _dot(q_ref[:, cols], kv_ref[:, cols], _NT) * scale
            e = jnp.exp(s - jnp.max(s, axis=-1, keepdims=True))
            p = e / jnp.sum(e, axis=-1, keepdims=True)
            o_ref[:, cols] = _dot(p.astype(_ACT), kv_ref[:, d + h * dh:d + (h + 1) * dh], _NN).astype(o_ref.dtype)

    qs = pl.BlockSpec((tq, d), lambda b, i: (b * nq + i, 0))
    return pl.pallas_call(
        body, name="attn_fwd", grid=(n_seq, nq), in_specs=[qs, pl.BlockSpec((n_mem, 2 * d), lambda b, i: (b, 0))],
        out_specs=qs, out_shape=jax.ShapeDtypeStruct((t, d), _ACT), compiler_params=_params(2),
    )(q, kv)


def _attn_bwd(q, kv, do, n_seq, seq, n_mem):
    t, d = q.shape
    dh = d // XATTN_HEADS
    tq = min(1024, seq)
    nq = seq // tq
    scale = dh ** -0.5

    def body(q_ref, kv_ref, do_ref, dq_ref, dkv_ref, acc):
        i = pl.program_id(1)

        @pl.when(i == 0)
        def _():
            acc[...] = jnp.zeros_like(acc)

        for h in range(XATTN_HEADS):
            cols = slice(h * dh, (h + 1) * dh)
            vcols = slice(d + h * dh, d + (h + 1) * dh)
            qv = q_ref[:, cols]
            kh = kv_ref[:, cols]
            dov = do_ref[:, cols]
            s = _dot(qv, kh, _NT) * scale
            e = jnp.exp(s - jnp.max(s, axis=-1, keepdims=True))
            p = e / jnp.sum(e, axis=-1, keepdims=True)
            dp = _dot(dov, kv_ref[:, vcols], _NT)
            ds = (p * (dp - jnp.sum(dp * p, axis=-1, keepdims=True)) * scale).astype(_ACT)
            dq_ref[:, cols] = _dot(ds, kh, _NN).astype(dq_ref.dtype)
            acc[:, cols] += _dot(ds, qv, _TN)
            acc[:, vcols] += _dot(p.astype(_ACT), dov, _TN)

        @pl.when(i == nq - 1)
        def _():
            dkv_ref[...] = acc[...].astype(dkv_ref.dtype)

    qs = pl.BlockSpec((tq, d), lambda b, i: (b * nq + i, 0))
    ms = pl.BlockSpec((n_mem, 2 * d), lambda b, i: (b, 0))
    return pl.pallas_call(
        body, name="attn_bwd", grid=(n_seq, nq), in_specs=[qs, ms, qs], out_specs=[qs, ms],
        out_shape=[jax.ShapeDtypeStruct((t, d), _ACT), jax.ShapeDtypeStruct((n_seq * n_mem, 2 * d), _ACT)],
        scratch_shapes=[pltpu.VMEM((n_mem, 2 * d), f32)], compiler_params=_params(2),
    )(q, kv, do)


_FFN_ROWS = 2048
_FFN_COLS = 256
_FFN_HALO = 16


def _window(buf, g, start, rows):
    return buf[g, pl.ds(start, rows + 8), :]


def _taps3(win, rows):
    return [_rows_from(win, 6 + k, rows) for k in range(3)]


def _conv3(b_ref, w_ref, taps):
    acc = b_ref[...] + w_ref[0:1, :] * taps[0]
    for k in (1, 2):
        acc = acc + w_ref[k:k + 1, :] * taps[k]
    return acc


def _ffn_gate_fwd(up, fw, fb, seq):
    _, t, f = up.shape
    tm = min(_FFN_ROWS, seq)
    tps = seq // tm
    tc = _FFN_COLS
    nc = f // tc
    hl = _FFN_HALO

    def body(up_ref, uph_ref, wg_ref, wv_ref, bg_ref, bv_ref, a_ref):
        i = pl.program_id(1)
        before = uph_ref[...]
        before = jnp.where(i % tps == 0, jnp.zeros_like(before), before)

        def chunk(r0, wins):
            conv = []
            for g, (w_ref, b_ref) in enumerate(((wg_ref, bg_ref), (wv_ref, bv_ref))):
                conv.append(_conv3(b_ref, w_ref, _taps3(wins[g].astype(f32)[hl - 8:, :], _CHUNK)))
            gate, val = conv
            a_ref[pl.ds(r0, _CHUNK), :] = (gate * _sigmoid(gate) * val).astype(a_ref.dtype)

        chunk(0, [jnp.concatenate([before[g], up_ref[g, 0:_CHUNK, :]], axis=0) for g in range(2)])

        def later(ci, carry):
            r0 = pl.multiple_of(ci * _CHUNK, _CHUNK)
            chunk(r0, [up_ref[g, pl.ds(r0 - hl, _CHUNK + hl), :] for g in range(2)])
            return carry

        lax.fori_loop(1, tm // _CHUNK, later, 0)

    hb = tm // hl
    return pl.pallas_call(
        body, name="ffn_gate_fwd", grid=(nc, t // tm),
        in_specs=[pl.BlockSpec((2, tm, tc), lambda j, i: (0, i, j)),
                  pl.BlockSpec((2, hl, tc), lambda j, i: (0, jnp.maximum(i * hb - 1, 0), j)),
                  pl.BlockSpec((8, tc), lambda j, i: (0, j)), pl.BlockSpec((8, tc), lambda j, i: (0, nc + j)),
                  pl.BlockSpec((1, tc), lambda j, i: (0, j)), pl.BlockSpec((1, tc), lambda j, i: (0, nc + j))],
        out_specs=pl.BlockSpec((tm, tc), lambda j, i: (i, j)),
        out_shape=jax.ShapeDtypeStruct((t, f), _ACT), compiler_params=_params(2),
    )(up, up, fw, fw, fb, fb)


def _ffn_gate_bwd(up, da, fw, fb, seq):
    _, t, f = up.shape
    tm = min(_FFN_ROWS, seq)
    tps = seq // tm
    tc = _FFN_COLS
    nc = f // tc
    hl = _FFN_HALO

    def body(up_ref, uph_ref, upn_ref, da_ref, dan_ref, wg_ref, wv_ref, bg_ref, bv_ref,
             dup_ref, sg_ref, sv_ref, dbuf, sums):
        i = pl.program_id(1)
        at_end = i % tps == tps - 1

        @pl.when(i == 0)
        def _():
            sg_ref[...] = jnp.zeros_like(sg_ref)
            sv_ref[...] = jnp.zeros_like(sv_ref)

        sums[...] = jnp.zeros_like(sums)
        before = uph_ref[...]
        before = jnp.where(i % tps == 0, jnp.zeros_like(before), before)
        after = upn_ref[...]
        after = jnp.where(at_end, jnp.zeros_like(after), after)
        w_refs = (wg_ref, wv_ref)
        b_refs = (bg_ref, bv_ref)

        def grads(r0, rows, wins, dav, count):
            taps = [_taps3(wins[g].astype(f32)[hl - 8:, :], rows) for g in range(2)]
            gate, val = [_conv3(b_refs[g], w_refs[g], taps[g]) for g in range(2)]
            sg = _sigmoid(gate)
            douts = (dav * val * (sg * (1.0 + gate * (1.0 - sg))), dav * (gate * sg))
            for g in range(2):
                dbuf[g, pl.ds(r0, rows), :] = douts[g]
                if count:
                    sums[g, 0] += douts[g].reshape(rows // 8, 8, tc).sum(axis=0)
                    for k in range(3):
                        sums[g, 1 + k] += (douts[g] * taps[g][k]).reshape(rows // 8, 8, tc).sum(axis=0)

        grads(0, _CHUNK, [jnp.concatenate([before[g], up_ref[g, 0:_CHUNK, :]], axis=0) for g in range(2)],
              da_ref[0:_CHUNK, :].astype(f32), True)

        def first(ci, carry):
            r0 = pl.multiple_of(ci * _CHUNK, _CHUNK)
            grads(r0, _CHUNK, [up_ref[g, pl.ds(r0 - hl, _CHUNK + hl), :] for g in range(2)],
                  da_ref[pl.ds(r0, _CHUNK), :].astype(f32), True)
            return carry

        lax.fori_loop(1, tm // _CHUNK, first, 0)
        da_after = dan_ref[...].astype(f32)
        grads(tm, hl, [jnp.concatenate([up_ref[g, tm - hl:tm, :], after[g]], axis=0) for g in range(2)],
              jnp.where(at_end, jnp.zeros_like(da_after), da_after), False)

        def second(ci, carry):
            r0 = pl.multiple_of(ci * _CHUNK, _CHUNK)
            for g in range(2):
                win = _window(dbuf, g, r0, _CHUNK)
                acc = jnp.zeros((_CHUNK, tc), f32)
                for k in range(3):
                    acc = acc + w_refs[g][k:k + 1, :] * _rows_from(win, 2 - k, _CHUNK)
                dup_ref[g, pl.ds(r0, _CHUNK), :] = acc.astype(dup_ref.dtype)
            return carry

        lax.fori_loop(0, tm // _CHUNK, second, 0)
        for g, s_ref in enumerate((sg_ref, sv_ref)):
            for r in range(4):
                s_ref[r:r + 1, :] += jnp.sum(sums[g, r], axis=0, keepdims=True)

    hb = tm // hl
    n_halo = t // hl
    return pl.pallas_call(
        body, name="ffn_gate_bwd", grid=(nc, t // tm),
        in_specs=[pl.BlockSpec((2, tm, tc), lambda j, i: (0, i, j)),
                  pl.BlockSpec((2, hl, tc), lambda j, i: (0, jnp.maximum(i * hb - 1, 0), j)),
                  pl.BlockSpec((2, hl, tc), lambda j, i: (0, jnp.minimum((i + 1) * hb, n_halo - 1), j)),
                  pl.BlockSpec((tm, tc), lambda j, i: (i, j)),
                  pl.BlockSpec((hl, tc), lambda j, i: (jnp.minimum((i + 1) * hb, n_halo - 1), j)),
                  pl.BlockSpec((8, tc), lambda j, i: (0, j)), pl.BlockSpec((8, tc), lambda j, i: (0, nc + j)),
                  pl.BlockSpec((1, tc), lambda j, i: (0, j)), pl.BlockSpec((1, tc), lambda j, i: (0, nc + j))],
        out_specs=[pl.BlockSpec((2, tm, tc), lambda j, i: (0, i, j)),
                   pl.BlockSpec((8, tc), lambda j, i: (0, j)), pl.BlockSpec((8, tc), lambda j, i: (0, j))],
        out_shape=[jax.ShapeDtypeStruct((2, t, f), _ACT), jax.ShapeDtypeStruct((8, f), f32), jax.ShapeDtypeStruct((8, f), f32)],
        scratch_shapes=[pltpu.VMEM((2, tm + hl, tc), f32), pltpu.VMEM((2, 4, 8, tc), f32)],
        compiler_params=_params(2),
    )(up, up, up, da, da, fw, fw, fb, fb)


def _adamw_math(w, g, m, v):
    m = ADAM_B1 * m + (1.0 - ADAM_B1) * g
    v = ADAM_B2 * v + (1.0 - ADAM_B2) * (g * g)
    m_hat = m / (1.0 - ADAM_B1 ** ADAM_STEP)
    v_hat = v / (1.0 - ADAM_B2 ** ADAM_STEP)
    delta = -ADAM_LR * (m_hat / (jnp.sqrt(v_hat) + ADAM_EPS) + ADAM_WD * w)
    return delta, m, v


def _adamw_shards(quads):
    n = len(quads)
    steps = 8

    def body(*refs):
        for p in range(n):
            w_ref, g_ref, m_ref, v_ref = refs[4 * p:4 * p + 4]
            go_ref, d_ref, mo_ref, vo_ref = refs[4 * n + 4 * p:4 * n + 4 * p + 4]
            gv = g_ref[...]
            d, mn, vn = _adamw_math(w_ref[...], gv, m_ref[...], v_ref[...])
            go_ref[...] = gv
            d_ref[...] = d
            mo_ref[...] = mn
            vo_ref[...] = vn

    in_specs, out_specs, out_shape = [], [], []
    for w, _, _, _ in quads:
        _, r, c = w.shape
        s3 = pl.BlockSpec((None, r // steps, c), lambda i: (0, i, 0))
        in_specs += [s3, pl.BlockSpec((r // steps, c), lambda i: (i, 0)), s3, s3]
        out_specs += [s3] * 4
        out_shape += [jax.ShapeDtypeStruct(w.shape, f32)] * 4
    outs = pl.pallas_call(
        body, name="adamw_shards", grid=(steps,), in_specs=in_specs, out_specs=out_specs, out_shape=out_shape,
        compiler_params=_params(1),
    )(*[a for q in quads for a in q])
    return [tuple(outs[4 * p:4 * p + 4]) for p in range(n)]


def _adamw_small(quads):
    n = len(quads)

    def body(*refs):
        ins, outs = refs[:4 * n], refs[4 * n:]
        for p in range(n):
            w_ref, g_ref, m_ref, v_ref = ins[4 * p:4 * p + 4]
            d, mn, vn = _adamw_math(w_ref[...], g_ref[...], m_ref[...], v_ref[...])
            outs[3 * p][...] = d
            outs[3 * p + 1][...] = mn
            outs[3 * p + 2][...] = vn

    flat = [a for q in quads for a in q]
    shapes = [jax.ShapeDtypeStruct(q[0].shape, f32) for q in quads for _ in range(3)]
    outs = pl.pallas_call(
        body, name="adamw_small", in_specs=[_VMEM] * (4 * n), out_specs=[_VMEM] * (3 * n), out_shape=shapes,
        compiler_params=pltpu.CompilerParams(vmem_limit_bytes=_VMEM_LIMIT_BYTES),
    )(*flat)
    return [tuple(outs[3 * p:3 * p + 3]) for p in range(n)]


def _sum_partials(name, place, grads, got):
    nw = len(grads)
    steps = 2

    def body(place_ref, *refs):
        for w in range(nw):
            own_ref, got_ref, f_ref = refs[w], refs[nw + w], refs[2 * nw + w]
            s = own_ref[...].astype(f32)
            for k in range(got[w].shape[0]):
                s = s + got_ref[k].astype(f32)
            f_ref[...] = s

    own_specs, got_specs, out_specs, out_shape = [], [], [], []
    for g, l in zip(grads, got):
        _, r, c = g.shape
        tr = r // steps
        own_specs.append(pl.BlockSpec((None, tr, c), lambda i, p: (2 * p[0] + p[1], i, 0)))
        got_specs.append(pl.BlockSpec((l.shape[0], tr, c), lambda i, p: (0, i, 0)))
        out_specs.append(pl.BlockSpec((None, tr, c), lambda i, p: (p[1], i, 0)))
        out_shape.append(jax.ShapeDtypeStruct((2, r, c), f32))
    grid_spec = pltpu.PrefetchScalarGridSpec(num_scalar_prefetch=1, grid=(steps,), in_specs=own_specs + got_specs, out_specs=out_specs)
    return pl.pallas_call(body, name=name, grid_spec=grid_spec, out_shape=out_shape,
                          compiler_params=_params(1))(place, *grads, *got)


def _place():
    return lax.axis_index("x"), lax.axis_index("y"), lax.axis_index("c")


def _other_chips(x, y):
    return [(1 - x, y), (x, 1 - y), (1 - x, 1 - y)]


def _remote(src, dst, send_sem, recv_sem, to):
    return pltpu.make_async_remote_copy(src_ref=src, dst_ref=dst, send_sem=send_sem, recv_sem=recv_sem,
                                        device_id=to, device_id_type=_MESH)


def _place_shards(name, place, shards, col_sharded, after=None):
    n = len(shards)
    steps = 4
    more, more_specs = _after(after)

    def body(place_ref, *refs):
        for src, dst in zip(refs[:n], refs[n + len(more):]):
            dst[...] = src[...].astype(dst.dtype)

    in_specs, out_specs, out_shape = [], [], []
    for w, col in zip(shards, col_sharded):
        r, cs = w.shape
        tr = r // steps
        in_specs.append(pl.BlockSpec((tr, cs), lambda i, p: (i, 0)))
        if col:
            out_specs.append(pl.BlockSpec((tr, cs), lambda i, p: (i, p[0])))
            out_shape.append(jax.ShapeDtypeStruct((r, 4 * cs), _ACT))
        else:
            out_specs.append(pl.BlockSpec((tr, cs), lambda i, p: (p[0] * steps + i, 0)))
            out_shape.append(jax.ShapeDtypeStruct((4 * r, cs), _ACT))
    grid_spec = pltpu.PrefetchScalarGridSpec(num_scalar_prefetch=1, grid=(steps,), in_specs=in_specs + more_specs,
                                            out_specs=out_specs)
    return pl.pallas_call(body, name=name, grid_spec=grid_spec, out_shape=out_shape,
                          compiler_params=_params(1))(place, *shards, *more)


def _shard_of(ref, col_sharded, s):
    rows, cols = ref.shape
    if col_sharded:
        return ref.at[:, pl.ds(s * (cols // 4), cols // 4)]
    return ref.at[pl.ds(s * (rows // 4), rows // 4), :]


def _part_of(ref, col_sharded, whole, s, h):
    if whole:
        return _shard_of(ref, col_sharded, s)
    rows, cols = ref.shape
    if col_sharded:
        return ref.at[pl.ds(h * (rows // 2), rows // 2), pl.ds(s * (cols // 4), cols // 4)]
    return ref.at[pl.ds((2 * s + h) * (rows // 8), rows // 8), :]


def _allgather_start(name, bufs, col_sharded, whole, groups):
    n = len(bufs)
    ng = len(groups)

    def body(*refs):
        out = refs[n:2 * n]
        sems = refs[2 * n:2 * n + 2 * ng]
        token = refs[2 * n + 2 * ng]
        x, y, c = _place()
        for g, members in enumerate(groups):
            for i, w in enumerate(members):
                mine = _part_of(out[w], col_sharded[w], whole[w], 2 * x + y, c)
                for j, chip in enumerate(_other_chips(x, y)):
                    _remote(mine, mine, sems[2 * g].at[3 * i + j], sems[2 * g + 1].at[3 * i + j], (*chip, c)).start()
        token[...] = jnp.zeros_like(token)

    sem_shapes = [pltpu.SemaphoreType.DMA((3 * len(m),)) for m in groups for _ in range(2)]
    outs = pl.pallas_call(
        body, name=name, in_specs=[_HBM] * n, out_specs=[_HBM] * n + [_SEM] * (2 * ng) + [_VMEM],
        out_shape=[pltpu.HBM(b.shape, b.dtype) for b in bufs] + sem_shapes + [jax.ShapeDtypeStruct((8, 128), f32)],
        input_output_aliases={i: i for i in range(n)},
        compiler_params=pltpu.CompilerParams(has_side_effects=_EFFECT),
    )(*[pltpu.with_memory_space_constraint(b, pltpu.HBM) for b in bufs])
    return list(outs[:n]), [(outs[n + 2 * g], outs[n + 2 * g + 1]) for g in range(ng)], outs[n + 2 * ng]


def _allgather_relay(name, bufs, col_sharded, whole, sems, after):
    n = len(bufs)

    def body(*refs):
        buf = refs[:n]
        send, recv = refs[n], refs[n + 1]
        out = refs[n + 3:2 * n + 3]
        to_sibling, from_sibling, token = refs[2 * n + 3:]
        token[...] = jnp.zeros_like(token)
        x, y, c = _place()
        for i in range(n):
            mine = _part_of(buf[i], col_sharded[i], whole[i], 2 * x + y, c)
            for j, chip in enumerate(_other_chips(x, y)):
                landed = _part_of(buf[i], col_sharded[i], whole[i], 2 * chip[0] + chip[1], c)
                cp = _remote(mine, landed, send.at[3 * i + j], recv.at[3 * i + j], (*chip, c))
                cp.wait_send()
                cp.wait_recv()
        for i in range(n):
            if not whole[i]:
                for j, chip in enumerate(_other_chips(x, y)):
                    landed = _part_of(out[i], col_sharded[i], False, 2 * chip[0] + chip[1], c)
                    _remote(landed, landed, to_sibling.at[3 * i + j], from_sibling.at[3 * i + j], (x, y, 1 - c)).start()

    outs = pl.pallas_call(
        body, name=name, in_specs=[_HBM] * n + [_SEM, _SEM, _ANY], out_specs=[_HBM] * n + [_SEM, _SEM, _VMEM],
        out_shape=[pltpu.HBM(b.shape, b.dtype) for b in bufs] + [pltpu.SemaphoreType.DMA((3 * n,))] * 2
        + [jax.ShapeDtypeStruct((8, 128), f32)],
        input_output_aliases={i: i for i in range(n)},
        compiler_params=pltpu.CompilerParams(has_side_effects=_EFFECT),
    )(*bufs, *sems, after)
    return list(outs[:n]), (outs[n], outs[n + 1]), outs[n + 2]


def _allgather_wait(name, bufs, col_sharded, whole, sems, after):
    n = len(bufs)

    def body(*refs):
        buf = refs[:n]
        to_sibling, from_sibling = refs[n], refs[n + 1]
        x, y, c = _place()
        for i in range(n):
            if not whole[i]:
                for j, chip in enumerate(_other_chips(x, y)):
                    sent = _part_of(buf[i], col_sharded[i], False, 2 * chip[0] + chip[1], c)
                    landed = _part_of(buf[i], col_sharded[i], False, 2 * chip[0] + chip[1], 1 - c)
                    cp = _remote(sent, landed, to_sibling.at[3 * i + j], from_sibling.at[3 * i + j], (x, y, 1 - c))
                    cp.wait_send()
                    cp.wait_recv()

    return pl.pallas_call(
        body, name=name, in_specs=[_HBM] * n + [_SEM, _SEM, _ANY], out_specs=[_HBM] * n,
        out_shape=[pltpu.HBM(b.shape, b.dtype) for b in bufs],
        input_output_aliases={i: i for i in range(n)},
        compiler_params=pltpu.CompilerParams(has_side_effects=_EFFECT),
    )(*bufs, *sems, after)


def _other_devices(x, y, c):
    flips = [(bx, by, bc) for bx in (0, 1) for by in (0, 1) for bc in (0, 1)][1:]
    return [(1 - x if bx else x, 1 - y if by else y, 1 - c if bc else c) for bx, by, bc in flips]


def _grad_exchange_start(name, grads):
    nw = len(grads)
    lands = [lax.empty((7,) + g.shape[1:], g.dtype) for g in grads]

    def body(*refs):
        src = refs[2 * nw:3 * nw]
        got = refs[3 * nw:4 * nw]
        send, recv, token = refs[4 * nw:]
        x, y, c = _place()
        for w in range(nw):
            for k, (px, py, pc) in enumerate(_other_devices(x, y, c)):
                _remote(src[w].at[4 * px + 2 * py + pc], got[w].at[k], send.at[7 * w + k], recv.at[7 * w + k], (px, py, pc)).start()
        token[...] = jnp.zeros_like(token)

    outs = pl.pallas_call(
        body, name=name, in_specs=[_HBM] * (2 * nw), out_specs=[_HBM] * (2 * nw) + [_SEM, _SEM, _VMEM],
        out_shape=[pltpu.HBM(a.shape, a.dtype) for a in list(grads) + lands]
        + [pltpu.SemaphoreType.DMA((7 * nw,)), pltpu.SemaphoreType.DMA((7 * nw,)), jax.ShapeDtypeStruct((8, 128), f32)],
        input_output_aliases={i: i for i in range(2 * nw)},
        compiler_params=pltpu.CompilerParams(has_side_effects=_EFFECT),
    )(*[pltpu.with_memory_space_constraint(a, pltpu.HBM) for a in list(grads) + lands])
    return list(outs[:nw]), list(outs[nw:2 * nw]), (outs[2 * nw], outs[2 * nw + 1]), outs[2 * nw + 2]


def _grad_exchange_wait(name, grads, got, sems, after):
    nw = len(grads)

    def body(*refs):
        src = refs[:nw]
        land = refs[nw:2 * nw]
        send, recv = refs[2 * nw], refs[2 * nw + 1]
        x, y, c = _place()
        for w in range(nw):
            for k, (px, py, pc) in enumerate(_other_devices(x, y, c)):
                cp = _remote(src[w].at[4 * px + 2 * py + pc], land[w].at[k], send.at[7 * w + k], recv.at[7 * w + k], (px, py, pc))
                cp.wait_send()
                cp.wait_recv()

    outs = pl.pallas_call(
        body, name=name, in_specs=[_HBM] * (2 * nw) + [_SEM, _SEM, _ANY], out_specs=[_HBM] * (2 * nw),
        out_shape=[pltpu.HBM(a.shape, a.dtype) for a in list(grads) + list(got)],
        input_output_aliases={i: i for i in range(2 * nw)},
        compiler_params=pltpu.CompilerParams(has_side_effects=_EFFECT),
    )(*grads, *got, *sems, after)
    return list(outs[:nw]), list(outs[nw:])


def _swap_halves_start(finals):
    nw = len(finals)

    def body(*refs):
        buf = refs[nw:2 * nw]
        send, recv, token = refs[2 * nw:]
        x, y, c = _place()
        for w in range(nw):
            _remote(buf[w].at[c], buf[w].at[c], send.at[w], recv.at[w], (x, y, 1 - c)).start()
        token[...] = jnp.zeros_like(token)

    outs = pl.pallas_call(
        body, name="rs_swap_start", in_specs=[_HBM] * nw, out_specs=[_HBM] * nw + [_SEM, _SEM, _VMEM],
        out_shape=[pltpu.HBM(g.shape, g.dtype) for g in finals] + [pltpu.SemaphoreType.DMA((nw,))] * 2
        + [jax.ShapeDtypeStruct((8, 128), f32)],
        input_output_aliases={i: i for i in range(nw)},
        compiler_params=pltpu.CompilerParams(has_side_effects=_EFFECT),
    )(*[pltpu.with_memory_space_constraint(g, pltpu.HBM) for g in finals])
    return list(outs[:nw]), (outs[nw], outs[nw + 1]), outs[nw + 2]


def _swap_halves_wait(bufs, sems, after):
    nw = len(bufs)

    def body(*refs):
        buf = refs[:nw]
        send, recv = refs[nw], refs[nw + 1]
        x, y, c = _place()
        for w in range(nw):
            cp = _remote(buf[w].at[c], buf[w].at[1 - c], send.at[w], recv.at[w], (x, y, 1 - c))
            cp.wait_send()
            cp.wait_recv()

    return pl.pallas_call(
        body, name="rs_swap_wait", in_specs=[_HBM] * nw + [_SEM, _SEM, _ANY], out_specs=[_HBM] * nw,
        out_shape=[pltpu.HBM(g.shape, g.dtype) for g in bufs],
        input_output_aliases={i: i for i in range(nw)},
        compiler_params=pltpu.CompilerParams(has_side_effects=_EFFECT),
    )(*bufs, *sems, after)


def _half_slices(shape, h):
    rows, cols = shape
    if cols % 256 == 0:
        return (slice(None), slice(h * (cols // 2), (h + 1) * (cols // 2)))
    return (slice(h * (rows // 2), (h + 1) * (rows // 2)), slice(None))


def _allreduce_small(parts, after):
    n = len(parts)

    def body(*refs):
        src = refs[:n]
        refs = refs[n + 1:]
        out = refs[:n]
        sib = refs[n:2 * n]
        chip_sum = refs[2 * n:3 * n]
        slots = refs[3 * n:4 * n]
        pair_send, pair_recv, ici_send, ici_recv, swap_send, swap_recv = refs[4 * n:]
        x, y, c = _place()
        me_chip = 2 * x + y
        chips = _other_chips(x, y)
        pairs = [_remote(src[a], sib[a], pair_send.at[a], pair_recv.at[a], (x, y, 1 - c)) for a in range(n)]
        for rc in pairs:
            rc.start()
        for a in range(n):
            pairs[a].wait_recv()
            chip_sum[a][...] = src[a][...] + sib[a][...]
        for h in (0, 1):
            @pl.when(c == h)
            def _():
                sends = []
                for a in range(n):
                    idx = _half_slices(parts[a].shape, h)
                    for j, chip in enumerate(chips):
                        rc = _remote(chip_sum[a].at[idx], slots[a].at[me_chip].at[idx], ici_send.at[3 * a + j], ici_recv.at[3 * a + j], (*chip, h))
                        rc.start()
                        sends.append(rc)
                    slots[a][(me_chip,) + idx] = chip_sum[a][idx]
                for a in range(n):
                    idx = _half_slices(parts[a].shape, h)
                    for j, chip in enumerate(chips):
                        landed = slots[a].at[2 * chip[0] + chip[1]].at[idx]
                        _remote(landed, landed, ici_send.at[3 * a + j], ici_recv.at[3 * a + j], (x, y, c)).wait_recv()
                    total = slots[a][(0,) + idx]
                    for s in range(1, 4):
                        total = total + slots[a][(s,) + idx]
                    out[a][idx] = total
                    rc = _remote(out[a].at[idx], out[a].at[idx], swap_send.at[a], swap_recv.at[a], (x, y, 1 - h))
                    rc.start()
                    sends.append(rc)
                for a in range(n):
                    other = out[a].at[_half_slices(parts[a].shape, 1 - h)]
                    _remote(other, other, swap_send.at[a], swap_recv.at[a], (x, y, c)).wait_recv()
                for rc in sends:
                    rc.wait_send()
        for rc in pairs:
            rc.wait_send()

    return pl.pallas_call(
        body, name="allreduce_small", in_specs=[_VMEM] * n + [_ANY], out_specs=[_VMEM] * n,
        out_shape=[jax.ShapeDtypeStruct(p.shape, f32) for p in parts],
        scratch_shapes=[pltpu.VMEM(p.shape, f32) for p in parts] * 2 + [pltpu.VMEM((4,) + p.shape, f32) for p in parts]
        + [pltpu.SemaphoreType.DMA((n,)), pltpu.SemaphoreType.DMA((n,)), pltpu.SemaphoreType.DMA((3 * n,)),
           pltpu.SemaphoreType.DMA((3 * n,)), pltpu.SemaphoreType.DMA((n,)), pltpu.SemaphoreType.DMA((n,))],
        compiler_params=pltpu.CompilerParams(vmem_limit_bytes=_VMEM_LIMIT_BYTES),
    )(*parts, after)


def _local_step(x, mem, tgt, g_mix, g_xattn, g_mem, g_ffn, g_final, cb, lg, lb, pw, ps, fb, started, relay, weights, reduce,
                n_seq, seq, n_mem):
    t, d = x.shape
    f = fb.shape[1] // 2
    c = cb.shape[1]
    h1 = _rms_fwd("norm_mix", x, g_mix, after=started)
    relay(0, h1)
    w_in, cw, fw = weights(0, h1)
    u = _mm_nn("proj_in", h1, w_in, _ACT, w_in.shape[1])
    y, hc = _mix_fwd(u, cw, cb, lg, lb, pw, ps, seq)
    relay(1, y)
    w_out, w_q, w_kv, w_o = weights(1, y)
    x1, h2 = _proj_residual_norm("proj_out", y, w_out, x, g_xattn)
    q = _mm_nn("proj_q", h2, w_q, _ACT, d)
    mem_n = _rms_fwd("norm_mem", mem, g_mem)
    kv = _mm_nn("proj_kv", mem_n, w_kv, _ACT, 2 * d)
    o = _attn_fwd(q, kv, n_seq, seq, n_mem)
    x2, h3 = _proj_residual_norm("proj_o", o, w_o, x1, g_ffn, after=relay(2, o))
    w_up, w_down = weights(2, h3)
    up = _mm_nn("proj_up", h3, w_up, _ACT, f, split_out=True)
    a = _ffn_gate_fwd(up, fw, fb, seq)
    dx3, dx3b, dg_final, loss = _proj_loss_bwd("proj_down", a, w_down, x2, g_final, tgt)
    da = _mm_nt("d_act", dx3b, w_down, _ACT)
    gw_down = _mm_tn_rows("dw_down", a, dx3b, f // 2, d // 2)
    dup, sums_g, sums_v = _ffn_gate_bwd(up, da, fw, fb, seq)
    gw_up = _mm_tn_pieces("dw_up", h3, dup, f // 2)
    token = reduce(0, [gw_down.reshape(8, -1, d), gw_up])
    dx2, dx2b, dg_ffn = _dproj_rms_bwd("d_h3", dup, w_up, x2, g_ffn, dx3, after=token)
    do = _mm_nt("d_o", dx2b, w_o, _ACT)
    gw_o = _mm_tn_rows("dw_o", o, dx2b, d, d // 2)
    dq, dkv = _attn_bwd(q, kv, do, n_seq, seq, n_mem)
    gw_q = _mm_tn_rows("dw_q", h2, dq, d, d // 2)
    gw_kv = _mm_tn_pieces("dw_kv", mem_n, dkv, d // 2)
    dmem_n = _mm_nt("d_mem_n", dkv, w_kv, f32)
    dg_mem = _rms_gain_grad("norm_mem_bwd", mem, dmem_n)
    dx1, dx1b, dg_xattn = _dproj_rms_bwd("d_h2", dq, w_q, x1, g_xattn, dx2)
    dy = _mm_nt("d_y", dx1b, w_out, _ACT)
    gw_out = _mm_tn_rows("dw_out", y, dx1b, d, d // 2)
    token = reduce(1, [gw_o.reshape(8, -1, d), gw_q.reshape(8, -1, d), gw_kv, gw_out.reshape(8, -1, d)])
    dhc, sums_norm = _mix_bwd_norm(hc, dy, lg, lb, token)
    du, d_cw, d_ps, d_pw = _mix_bwd_taps(u, dhc, dy, cw, pw, ps, seq)
    gw_in = _mm_tn_pieces("dw_in", h1, du, c * 3 // 4)
    token = reduce(2, [gw_in])
    grad_x, dg_mix = _dproj_rms_bwd("d_h1", du, w_in, x, g_mix, dx1, storage_copy=False, after=token)
    zero_row = jnp.zeros((1, d), f32)
    gains = jnp.concatenate([dg_mix, dg_xattn, dg_mem, dg_ffn, dg_final, jnp.pad(loss, ((0, 0), (0, d - 1))), zero_row, zero_row], axis=0)
    conv_rows = jnp.concatenate([sums_norm[2:3], sums_norm[0:1], sums_norm[1:2], d_ps[0:1], jnp.zeros((4, c), f32)], axis=0)
    ffn_rows = jnp.concatenate([sums_g, sums_v], axis=1)
    small = [gains, conv_rows, d_pw.reshape(-1, d_pw.shape[-1]), ffn_rows, d_cw]
    return grad_x, small


def kernel(x, mem, norm_mix_g, w_in, conv_dw_w, conv_dw_b, conv_ln_g, conv_ln_b, pool_w, pool_scale, w_out, norm_xattn_g, norm_mem_g, w_q, w_kv, w_o, norm_ffn_g, w_up, ffn_dw_w, ffn_dw_b, w_down, norm_final_g, loss_target, m_norm_mix_g, m_w_in, m_conv_dw_w, m_conv_dw_b, m_conv_ln_g, m_conv_ln_b, m_pool_w, m_pool_scale, m_w_out, m_norm_xattn_g, m_norm_mem_g, m_w_q, m_w_kv, m_w_o, m_norm_ffn_g, m_w_up, m_ffn_dw_w, m_ffn_dw_b, m_w_down, m_norm_final_g, v_norm_mix_g, v_w_in, v_conv_dw_w, v_conv_dw_b, v_conv_ln_g, v_conv_ln_b, v_pool_w, v_pool_scale, v_w_out, v_norm_xattn_g, v_norm_mem_g, v_w_q, v_w_kv, v_w_o, v_norm_ffn_g, v_w_up, v_ffn_dw_w, v_ffn_dw_b, v_w_down, v_norm_final_g):
    n_seq, seq, d = x.shape
    n_mem = mem.shape[1]
    chip = 2 * lax.axis_index("x") + lax.axis_index("y")

    place = jnp.stack([chip, lax.axis_index("c")]).astype(jnp.int32)

    col_w = [w_in, w_kv, w_up]
    row_w = [w_out, w_q, w_o, w_down]
    kw = conv_dw_w.shape[1]

    def padded_in_place(shard, rows):
        full = jnp.zeros((rows, 4 * shard.shape[1]), shard.dtype)
        return lax.dynamic_update_slice(full, shard, (0, chip * shard.shape[1]))

    first = list(_place_shards("place_w_in", place, [w_in[0]], [True]))
    first += [padded_in_place(conv_dw_w[0], _HALO), padded_in_place(ffn_dw_w[0], 8)]
    first, first_sems, token = _allgather_start("allgather_start_0", first, [True] * 3, [False, True, True], [[0, 1, 2]])
    rest = [w_kv, w_up, w_out, w_q, w_o, w_down]
    rest_flags = [True, True, False, False, False, False]
    rest = list(_place_shards("place_rest", place, [w[0] for w in rest], rest_flags, after=token))
    rest, rest_sems, all_started = _allgather_start("allgather_start_1", rest, rest_flags, [False] * 6, [[2, 3, 0, 4], [1, 5]])
    started = [(first, [True] * 3, [False, True, True], first_sems[0]),
               ([rest[i] for i in (2, 3, 0, 4)], [False, False, True, False], [False] * 4, rest_sems[0]),
               ([rest[i] for i in (1, 5)], [True, False], [False] * 2, rest_sems[1])]
    relayed = {}

    def relay(g, after):
        group_bufs, flags, wholes, group_sems = started[g]
        group_bufs, sibling_sems, relay_token = _allgather_relay("allgather_relay_%d" % g, group_bufs, flags, wholes, group_sems, after)
        relayed[g] = (group_bufs, sibling_sems)
        return relay_token

    def weights(g, after):
        group_bufs, sibling_sems = relayed[g]
        return _allgather_wait("allgather_wait_%d" % g, group_bufs, started[g][1], started[g][2], sibling_sems, after)

    names = ["w_in", "w_kv", "w_up", "w_out", "w_q", "w_o", "w_down"]
    reduce_groups = [["w_down", "w_up"], ["w_o", "w_q", "w_kv", "w_out"], ["w_in"]]
    in_flight = {}

    def reduce(g, grads):
        grads, lands, rs_sems, token = _grad_exchange_start("rs_start_%d" % g, grads)
        in_flight[g] = (grads, lands, rs_sems)
        return token

    grad_x, small = _local_step(
        x.reshape(n_seq * seq, d), mem.reshape(n_seq * n_mem, d), loss_target.reshape(n_seq * seq, d),
        norm_mix_g, norm_xattn_g, norm_mem_g, norm_ffn_g, norm_final_g.reshape(1, d),
        conv_dw_b, conv_ln_g, conv_ln_b, pool_w[0], pool_scale, ffn_dw_b, all_started, relay, weights, reduce,
        n_seq, seq, n_mem)

    landed = {}
    for g, members in enumerate(reduce_groups):
        grads, lands, rs_sems = in_flight[g]
        grads, lands = _grad_exchange_wait("rs_wait_%d" % g, grads, lands, rs_sems, grad_x)
        landed.update(zip(members, zip(grads, lands)))
    finals = _sum_partials("rs_sum", place, [landed[n][0] for n in names], [landed[n][1] for n in names])
    finals, swap_sems, token = _swap_halves_start(finals)

    gains, conv_rows, d_pw, ffn_rows, d_cw = _allreduce_small(small, token)
    loss = gains[5, 0]
    shard_grads = _swap_halves_wait(finals, swap_sems, gains)

    outs = {}
    big_w = dict(zip(names, col_w + row_w))
    big_m = dict(w_in=m_w_in, w_kv=m_w_kv, w_up=m_w_up, w_out=m_w_out, w_q=m_w_q, w_o=m_w_o, w_down=m_w_down)
    big_v = dict(w_in=v_w_in, w_kv=v_w_kv, w_up=v_w_up, w_out=v_w_out, w_q=v_w_q, w_o=v_w_o, w_down=v_w_down)
    big_quads = [(big_w[n], g.reshape(big_w[n].shape[1:]), big_m[n], big_v[n]) for n, g in zip(names, shard_grads)]
    outs.update(zip(names, _adamw_shards(big_quads)))

    f2 = ffn_dw_b.shape[1]
    cs_c = conv_dw_w.shape[2]
    cs_f = ffn_dw_w.shape[2]
    g_cw = lax.dynamic_slice(d_cw, (0, chip * cs_c), (kw, cs_c)).reshape(conv_dw_w.shape)
    g_fw = lax.dynamic_slice(ffn_rows, (1, chip * cs_f), (ffn_dw_w.shape[1], cs_f)).reshape(ffn_dw_w.shape)
    small_params = [
        ("norm_mix_g", norm_mix_g, gains[0:1], m_norm_mix_g, v_norm_mix_g),
        ("conv_dw_w", conv_dw_w, g_cw, m_conv_dw_w, v_conv_dw_w),
        ("conv_dw_b", conv_dw_b, conv_rows[0:1], m_conv_dw_b, v_conv_dw_b),
        ("conv_ln_g", conv_ln_g, conv_rows[1:2], m_conv_ln_g, v_conv_ln_g),
        ("conv_ln_b", conv_ln_b, conv_rows[2:3], m_conv_ln_b, v_conv_ln_b),
        ("pool_w", pool_w, d_pw.reshape(pool_w.shape), m_pool_w, v_pool_w),
        ("pool_scale", pool_scale, conv_rows[3:4], m_pool_scale, v_pool_scale),
        ("norm_xattn_g", norm_xattn_g, gains[1:2], m_norm_xattn_g, v_norm_xattn_g),
        ("norm_mem_g", norm_mem_g, gains[2:3], m_norm_mem_g, v_norm_mem_g),
        ("norm_ffn_g", norm_ffn_g, gains[3:4], m_norm_ffn_g, v_norm_ffn_g),
        ("ffn_dw_w", ffn_dw_w, g_fw, m_ffn_dw_w, v_ffn_dw_w),
        ("ffn_dw_b", ffn_dw_b, ffn_rows[0:1, :f2], m_ffn_dw_b, v_ffn_dw_b),
        ("norm_final_g", norm_final_g.reshape(1, d), gains[4:5], m_norm_final_g.reshape(1, d), v_norm_final_g.reshape(1, d)),
    ]
    quads = []
    for _, w, g, m, v in small_params:
        shape2 = (-1, w.shape[-1])
        quads.append((w.reshape(shape2), g.reshape(shape2), m.reshape(shape2), v.reshape(shape2)))
    for (n, w, g, _, _), (delta, new_m, new_v) in zip(small_params, _adamw_small(quads)):
        shape = norm_final_g.shape if n == "norm_final_g" else w.shape
        outs[n] = (g.reshape(shape), delta.reshape(shape), new_m.reshape(shape), new_v.reshape(shape))

    order = ["norm_mix_g", "w_in", "conv_dw_w", "conv_dw_b", "conv_ln_g", "conv_ln_b", "pool_w", "pool_scale", "w_out",
             "norm_xattn_g", "norm_mem_g", "w_q", "w_kv", "w_o", "norm_ffn_g", "w_up", "ffn_dw_w", "ffn_dw_b", "w_down",
             "norm_final_g"]
    return (loss, grad_x.reshape(x.shape), *[outs[n][0] for n in order], *[outs[n][1] for n in order],
            *[outs[n][2] for n in order], *[outs[n][3] for n in order])
```

```python
import jax
import jax.numpy as jnp
from jax import lax
from jax.experimental import pallas as pl
from jax.experimental.pallas import tpu as pltpu

f32 = jnp.float32
_ACT = jnp.bfloat16

EPS = 1e-6
POOL_WINDOWS = (2, 4, 8, 16)
XATTN_HEADS = 4
ADAM_LR = 0.001
ADAM_B1 = 0.9
ADAM_B2 = 0.999
ADAM_EPS = 1e-08
ADAM_WD = 0.01
ADAM_STEP = 10

_VMEM_LIMIT_BYTES = 56 * 1024 * 1024
_MESH = pl.DeviceIdType.MESH
_ANY = pl.BlockSpec(memory_space=pl.ANY)
_VMEM = pl.BlockSpec(memory_space=pltpu.VMEM)
_HBM = pl.BlockSpec(memory_space=pltpu.HBM)
_SEM = pl.BlockSpec(memory_space=pltpu.SEMAPHORE)
_EFFECT = pltpu.SideEffectType.DATAFLOW_SIDE_EFFECTING

_NN = (((1,), (0,)), ((), ()))
_NT = (((1,), (1,)), ((), ()))
_TN = (((0,), (0,)), ((), ()))


def _params(n_grid):
    return pltpu.CompilerParams(dimension_semantics=("arbitrary",) * n_grid, vmem_limit_bytes=_VMEM_LIMIT_BYTES)


def _sigmoid(v):
    return 1.0 / (1.0 + jnp.exp(-v))


def _dot(a, b, dims):
    return lax.dot_general(a, b, dims, preferred_element_type=f32)


def _mm(name, a, b, *, dims, grid, a_spec, b_spec, o_spec, out_shape):
    def body(a_ref, b_ref, o_ref):
        o_ref[...] = _dot(a_ref[...], b_ref[...], dims).astype(o_ref.dtype)

    return pl.pallas_call(
        body, name=name, grid=grid, in_specs=[a_spec, b_spec], out_specs=o_spec, out_shape=out_shape,
        compiler_params=_params(len(grid)),
    )(a, b)


_NARROW = 2816


def _row_tile(m, width=_NARROW + 1):
    return min(1024 if width <= _NARROW else 512, m)


def _mm_nn(name, a, b, out_dtype, tn, split_out=False):
    m, k = a.shape
    n = b.shape[1]
    tm = _row_tile(m, max(k, tn))
    if split_out:
        out_shape = jax.ShapeDtypeStruct((n // tn, m, tn), out_dtype)
        o_spec = pl.BlockSpec((None, tm, tn), lambda j, i: (j, i, 0))
    else:
        out_shape = jax.ShapeDtypeStruct((m, n), out_dtype)
        o_spec = pl.BlockSpec((tm, tn), lambda j, i: (i, j))
    return _mm(
        name, a, b, dims=_NN, grid=(n // tn, m // tm),
        a_spec=pl.BlockSpec((tm, k), lambda j, i: (i, 0)), b_spec=pl.BlockSpec((k, tn), lambda j, i: (0, j)),
        o_spec=o_spec, out_shape=out_shape,
    )


def _mm_nt(name, a, b, out_dtype):
    n, kc = b.shape
    m = a.shape[0]
    tm = _row_tile(m, max(n, kc))
    return _mm(
        name, a, b, dims=_NT, grid=(m // tm,),
        a_spec=pl.BlockSpec((tm, kc), lambda i: (i, 0)),
        b_spec=pl.BlockSpec((n, kc), lambda i: (0, 0), pipeline_mode=pl.Buffered(1)),
        o_spec=pl.BlockSpec((tm, n), lambda i: (i, 0)),
        out_shape=jax.ShapeDtypeStruct((m, n), out_dtype),
    )


def _mm_tn_rows(name, a, b, tka, tn):
    m, ka = a.shape
    nb = b.shape[1]
    return _mm(
        name, a, b, dims=_TN, grid=(ka // tka, nb // tn),
        a_spec=pl.BlockSpec((m, tka), lambda i, j: (0, i)), b_spec=pl.BlockSpec((m, tn), lambda i, j: (0, j)),
        o_spec=pl.BlockSpec((tka, tn), lambda i, j: (i, j)),
        out_shape=jax.ShapeDtypeStruct((ka, nb), _ACT),
    )


def _mm_tn_pieces(name, a, b, cs):
    m, ka = a.shape
    if b.ndim == 3:
        b_spec = pl.BlockSpec((None, m, cs), lambda i, j: (j // 2, 0, j % 2))
    else:
        b_spec = pl.BlockSpec((m, cs), lambda i, j: (0, j))
    return _mm(
        name, a, b, dims=_TN, grid=(2, 4),
        a_spec=pl.BlockSpec((m, ka // 2), lambda i, j: (0, i)), b_spec=b_spec,
        o_spec=pl.BlockSpec((None, ka // 2, cs), lambda i, j: (2 * j + i, 0, 0)),
        out_shape=jax.ShapeDtypeStruct((8, ka // 2, cs), _ACT),
    )


def _after(after):
    return ([], []) if after is None else ([after], [_ANY])


def _rms_fwd(name, x, g, after=None):
    t, d = x.shape
    tm = _row_tile(t, d)
    more, more_specs = _after(after)

    def body(x_ref, g_ref, *refs):
        h_ref = refs[-1]
        xv = x_ref[...]
        r = lax.rsqrt(jnp.mean(xv * xv, axis=-1, keepdims=True) + EPS)
        h_ref[...] = (xv * r * g_ref[...]).astype(h_ref.dtype)

    return pl.pallas_call(
        body, name=name, grid=(t // tm,),
        in_specs=[pl.BlockSpec((tm, d), lambda i: (i, 0)), pl.BlockSpec((1, d), lambda i: (0, 0))] + more_specs,
        out_specs=pl.BlockSpec((tm, d), lambda i: (i, 0)), out_shape=jax.ShapeDtypeStruct((t, d), _ACT),
        compiler_params=_params(1),
    )(x, g, *more)


def _fused_rows(name, a, b, product, a_spec, tm, extras, extra_specs, out_shape, out_specs, epilogue):
    ne = len(extras)

    def body(a_ref, b_ref, *refs):
        epilogue(product(a_ref, b_ref), refs[:ne], refs[ne:])

    m = extras[0].shape[0]
    return pl.pallas_call(
        body, name=name, grid=(m // tm,),
        in_specs=[a_spec, pl.BlockSpec(b.shape, lambda i: (0, 0), pipeline_mode=pl.Buffered(1)), *extra_specs],
        out_specs=out_specs, out_shape=out_shape, compiler_params=_params(1),
    )(a, b, *extras)


def _proj_residual_norm(name, a, b, res, g, after=None):
    m, k = a.shape
    d = b.shape[1]
    tm = _row_tile(m, max(k, d))

    def epilogue(p, ins, outs):
        xv = p + ins[0][...]
        outs[0][...] = xv
        r = lax.rsqrt(jnp.mean(xv * xv, axis=-1, keepdims=True) + EPS)
        outs[1][...] = (xv * r * ins[1][...]).astype(outs[1].dtype)

    row = pl.BlockSpec((tm, d), lambda i: (i, 0))
    return _fused_rows(
        name, a, b, lambda a_ref, b_ref: _dot(a_ref[...], b_ref[...], _NN), pl.BlockSpec((tm, k), lambda i: (i, 0)), tm,
        [res, g] + _after(after)[0], [row, pl.BlockSpec((1, d), lambda i: (0, 0))] + _after(after)[1],
        [jax.ShapeDtypeStruct((m, d), f32), jax.ShapeDtypeStruct((m, d), _ACT)], [row, row], epilogue)


def _dproj_rms_bwd(name, a, b, x, g, dres, storage_copy=True, after=None):
    m, d = x.shape
    if a.ndim == 3:
        nh, _, kh = a.shape
        tm = _row_tile(m, nh * kh)
        a_spec = pl.BlockSpec((nh, tm, kh), lambda i: (0, i, 0))

        def product(a_ref, b_ref):
            p = _dot(a_ref[0], b_ref[:, 0:kh], _NT)
            for h in range(1, nh):
                p = p + _dot(a_ref[h], b_ref[:, h * kh:(h + 1) * kh], _NT)
            return p
    else:
        tm = _row_tile(m, max(a.shape[1], d))
        a_spec = pl.BlockSpec((tm, a.shape[1]), lambda i: (i, 0))

        def product(a_ref, b_ref):
            return _dot(a_ref[...], b_ref[...], _NT)

    def epilogue(dhv, ins, outs):
        x_ref, g_ref, dres_ref = ins[:3]
        dg_ref = outs[-1]

        @pl.when(pl.program_id(0) == 0)
        def _():
            dg_ref[...] = jnp.zeros_like(dg_ref)

        xv = x_ref[...]
        r = lax.rsqrt(jnp.mean(xv * xv, axis=-1, keepdims=True) + EPS)
        xn = xv * r
        dxn = dhv * g_ref[...]
        dx = r * (dxn - xn * jnp.mean(dxn * xn, axis=-1, keepdims=True)) + dres_ref[...]
        outs[0][...] = dx
        if storage_copy:
            outs[1][...] = dx.astype(outs[1].dtype)
        dg_ref[...] += jnp.sum(dhv * xn, axis=0, keepdims=True)

    row = pl.BlockSpec((tm, d), lambda i: (i, 0))
    vec = pl.BlockSpec((1, d), lambda i: (0, 0))
    copies = [jax.ShapeDtypeStruct((m, d), _ACT)] if storage_copy else []
    return _fused_rows(
        name, a, b, product, a_spec, tm, [x, g, dres] + _after(after)[0], [row, vec, row] + _after(after)[1],
        [jax.ShapeDtypeStruct((m, d), f32)] + copies + [jax.ShapeDtypeStruct((1, d), f32)],
        [row] * (1 + len(copies)) + [vec], epilogue)


def _proj_loss_bwd(name, a, b, res, g, tgt):
    m, k = a.shape
    d = b.shape[1]
    tm = _row_tile(m, max(k, d))

    def epilogue(p, ins, outs):
        res_ref, g_ref, t_ref = ins
        dx_ref, dxb_ref, dg_ref, loss_ref = outs

        @pl.when(pl.program_id(0) == 0)
        def _():
            dg_ref[...] = jnp.zeros_like(dg_ref)
            loss_ref[...] = jnp.zeros_like(loss_ref)

        xv = p + res_ref[...]
        gv = g_ref[...]
        r = lax.rsqrt(jnp.mean(xv * xv, axis=-1, keepdims=True) + EPS)
        xn = xv * r
        err = xn * gv - t_ref[...]
        loss_ref[...] += 0.5 * jnp.sum(jnp.mean(err * err, axis=-1, keepdims=True), axis=0, keepdims=True)
        dout = err * (1.0 / d)
        dxn = dout * gv
        dx = r * (dxn - xn * jnp.mean(dxn * xn, axis=-1, keepdims=True))
        dx_ref[...] = dx
        dxb_ref[...] = dx.astype(dxb_ref.dtype)
        dg_ref[...] += jnp.sum(dout * xn, axis=0, keepdims=True)

    row = pl.BlockSpec((tm, d), lambda i: (i, 0))
    vec = pl.BlockSpec((1, d), lambda i: (0, 0))
    return _fused_rows(
        name, a, b, lambda a_ref, b_ref: _dot(a_ref[...], b_ref[...], _NN), pl.BlockSpec((tm, k), lambda i: (i, 0)), tm,
        [res, g, tgt], [row, vec, row],
        [jax.ShapeDtypeStruct((m, d), f32), jax.ShapeDtypeStruct((m, d), _ACT), jax.ShapeDtypeStruct((1, d), f32),
         jax.ShapeDtypeStruct((1, 1), f32)],
        [row, row, vec, pl.BlockSpec((1, 1), lambda i: (0, 0))], epilogue)


def _rms_gain_grad(name, x, dh):
    t, d = x.shape
    tm = _row_tile(t)

    def body(x_ref, dh_ref, dg_ref):
        @pl.when(pl.program_id(0) == 0)
        def _():
            dg_ref[...] = jnp.zeros_like(dg_ref)

        xv = x_ref[...]
        r = lax.rsqrt(jnp.mean(xv * xv, axis=-1, keepdims=True) + EPS)
        dg_ref[...] += jnp.sum(dh_ref[...] * (xv * r), axis=0, keepdims=True)

    row = pl.BlockSpec((tm, d), lambda i: (i, 0))
    return pl.pallas_call(
        body, name=name, grid=(t // tm,), in_specs=[row, row], out_specs=pl.BlockSpec((1, d), lambda i: (0, 0)),
        out_shape=jax.ShapeDtypeStruct((1, d), f32), compiler_params=_params(1),
    )(x, dh)


_CONV_ROWS = 512
_CHUNK = 64
_HALO = 32


def _pool_counts(pos, w):
    return jnp.minimum(pos + 1.0, float(w))


def _rows_from(win, start, rows):
    if start % 8 == 0:
        return win[start:start + rows, :]
    n = win.shape[0]
    return pltpu.roll(win, n - start % 8, axis=0)[start - start % 8:start - start % 8 + rows, :]


def _tap_rows(buf, starts, rows):
    for residue in range(8):
        group = [(k, s) for k, s in starts.items() if s % 8 == residue]
        if group:
            lo = min(s for _, s in group) - residue
            hi = max(s for _, s in group) - residue + rows + (8 if residue else 0)
            win = buf[lo:hi, :]
            if residue:
                win = pltpu.roll(win, hi - lo - residue, axis=0)
            for k, s in group:
                yield k, win[s - residue - lo:s - residue - lo + rows, :]


def _mix_fwd(u, cw, cb, lg, lb, pw, ps, seq):
    t, c3 = u.shape
    c = c3 // 3
    kw = 31
    tm = min(_CONV_ROWS, seq)
    tps = seq // tm
    gd = c // len(POOL_WINDOWS)

    def body(u_ref, uh_ref, cw_ref, cb_ref, lg_ref, lb_ref, pw_ref, ps_ref, y_ref, hc_ref, hgbuf, pbuf):
        i = pl.program_id(0)
        keep = jnp.where(i % tps == 0, 0.0, 1.0)
        um = u_ref[...].astype(f32)
        uh = uh_ref[...].astype(f32) * keep
        hgbuf[0:_HALO, :] = uh[:, 0:c] * _sigmoid(uh[:, c:2 * c])
        hgbuf[_HALO:_HALO + tm, :] = um[:, 0:c] * _sigmoid(um[:, c:2 * c])
        pbuf[0:_HALO, :] = uh[:, 2 * c:]
        pbuf[_HALO:_HALO + tm, :] = um[:, 2 * c:]
        for r0 in range(0, tm, _CHUNK):
            acc = jnp.broadcast_to(cb_ref[...], (_CHUNK, c))
            for k, rows in _tap_rows(hgbuf, {k: r0 + _HALO - (kw - 1) + k for k in range(kw)}, _CHUNK):
                acc = acc + cw_ref[k:k + 1, :] * rows
            hc_ref[r0:r0 + _CHUNK, :] = acc
            mu = jnp.mean(acc, axis=-1, keepdims=True)
            xc = acc - mu
            var = jnp.mean(xc * xc, axis=-1, keepdims=True)
            hl = xc * lax.rsqrt(var + EPS) * lg_ref[...] + lb_ref[...]
            y_ref[r0:r0 + _CHUNK, 0:c] = (hl * _sigmoid(hl)).astype(y_ref.dtype)
        pos = ((i % tps) * tm).astype(f32) + lax.broadcasted_iota(jnp.int32, (tm, 1), 0).astype(f32)
        for gi, w in enumerate(POOL_WINDOWS):
            sl = slice(gi * gd, (gi + 1) * gd)
            v = pbuf[_HALO:_HALO + tm, sl]
            s = v
            for _, rows in _tap_rows(pbuf.at[:, sl], {j: _HALO - j for j in range(1, w)}, tm):
                s = s + rows
            pooled = s / _pool_counts(pos, w) - v
            mixed = _dot(pooled.astype(_ACT), pw_ref[gi].astype(_ACT), _NN)
            y_ref[:, c + gi * gd:c + (gi + 1) * gd] = (mixed * ps_ref[:, sl]).astype(y_ref.dtype)

    hb = tm // _HALO
    full = lambda shape: pl.BlockSpec(shape, lambda i: (0,) * len(shape))
    return pl.pallas_call(
        body, name="mix_fwd", grid=(t // tm,),
        in_specs=[pl.BlockSpec((tm, c3), lambda i: (i, 0)),
                  pl.BlockSpec((_HALO, c3), lambda i: (jnp.maximum(i * hb - 1, 0), 0)),
                  full((_HALO, c)), full((1, c)), full((1, c)), full((1, c)), full((len(POOL_WINDOWS), gd, gd)), full((1, c))],
        out_specs=[pl.BlockSpec((tm, 2 * c), lambda i: (i, 0)), pl.BlockSpec((tm, c), lambda i: (i, 0))],
        out_shape=[jax.ShapeDtypeStruct((t, 2 * c), _ACT), jax.ShapeDtypeStruct((t, c), f32)],
        scratch_shapes=[pltpu.VMEM((_HALO + tm, c), f32), pltpu.VMEM((_HALO + tm, c), f32)],
        compiler_params=_params(1),
    )(u, u, cw, cb, lg, lb, pw, ps)


def _mix_bwd_norm(hc, dy, lg, lb, after):
    t, c = hc.shape
    tm = _row_tile(t, c)

    def body(hc_ref, dy_ref, lg_ref, lb_ref, after_ref, dhc_ref, sums_ref):
        @pl.when(pl.program_id(0) == 0)
        def _():
            sums_ref[...] = jnp.zeros_like(sums_ref)

        hcv = hc_ref[...]
        mu = jnp.mean(hcv, axis=-1, keepdims=True)
        xc = hcv - mu
        rstd = lax.rsqrt(jnp.mean(xc * xc, axis=-1, keepdims=True) + EPS)
        n = xc * rstd
        hl = n * lg_ref[...] + lb_ref[...]
        sg = _sigmoid(hl)
        dhl = dy_ref[...].astype(f32) * (sg * (1.0 + hl * (1.0 - sg)))
        dn = dhl * lg_ref[...]
        dhc = rstd * (dn - jnp.mean(dn, axis=-1, keepdims=True) - n * jnp.mean(dn * n, axis=-1, keepdims=True))
        dhc_ref[...] = dhc
        sums_ref[0:1, :] += jnp.sum(dhl * n, axis=0, keepdims=True)
        sums_ref[1:2, :] += jnp.sum(dhl, axis=0, keepdims=True)
        sums_ref[2:3, :] += jnp.sum(dhc, axis=0, keepdims=True)

    row = pl.BlockSpec((tm, c), lambda i: (i, 0))
    vec = pl.BlockSpec((1, c), lambda i: (0, 0))
    return pl.pallas_call(
        body, name="mix_bwd_norm", grid=(t // tm,), in_specs=[row, row, vec, vec, _ANY],
        out_specs=[row, pl.BlockSpec((8, c), lambda i: (0, 0))],
        out_shape=[jax.ShapeDtypeStruct((t, c), f32), jax.ShapeDtypeStruct((8, c), f32)],
        compiler_params=_params(1),
    )(hc, dy, lg, lb, after)


def _mix_bwd_taps(u, dhc, dy, cw, pw, ps, seq):
    t, c3 = u.shape
    c = c3 // 3
    kw = 31
    tm = min(_CONV_ROWS, seq)
    tps = seq // tm
    ng = len(POOL_WINDOWS)
    gd = c // ng
    nh = 16

    def body(u_ref, uh_ref, dhc_ref, dhcn_ref, dy_ref, dyn_ref, cw_ref, pw_ref, ps_ref,
             du_ref, dcw_ref, dps_ref, dpw_ref, hgbuf, dcbuf, pbuf, dpbuf):
        i = pl.program_id(0)
        keep_prev = jnp.where(i % tps == 0, 0.0, 1.0)
        keep_next = jnp.where(i % tps == tps - 1, 0.0, 1.0)

        @pl.when(i == 0)
        def _():
            dcw_ref[...] = jnp.zeros_like(dcw_ref)
            dps_ref[...] = jnp.zeros_like(dps_ref)
            dpw_ref[...] = jnp.zeros_like(dpw_ref)

        uh = uh_ref[...].astype(f32) * keep_prev
        hgbuf[0:_HALO, :] = uh[:, 0:c] * _sigmoid(uh[:, c:2 * c])
        pbuf[0:_HALO, :] = uh[:, 2 * c:]
        um = u_ref[...].astype(f32)
        hgbuf[_HALO:_HALO + tm, :] = um[:, 0:c] * _sigmoid(um[:, c:2 * c])
        pbuf[_HALO:_HALO + tm, :] = um[:, 2 * c:]
        dcbuf[0:tm, :] = dhc_ref[...]
        dcbuf[tm:tm + _HALO, :] = dhcn_ref[...] * keep_next
        tap_sums = [None] * kw
        for r0 in range(0, tm, _CHUNK):
            dh = dcbuf[r0:r0 + _CHUNK, :]
            acc = jnp.zeros((_CHUNK, c), f32)
            for k, rows in _tap_rows(hgbuf, {k: r0 + _HALO - (kw - 1) + k for k in range(kw)}, _CHUNK):
                part = (dh * rows).reshape(_CHUNK // 8, 8, c).sum(axis=0)
                tap_sums[k] = part if tap_sums[k] is None else tap_sums[k] + part
            for k, rows in _tap_rows(dcbuf, {k: r0 + (kw - 1) - k for k in range(kw)}, _CHUNK):
                acc = acc + cw_ref[k:k + 1, :] * rows
            val = u_ref[r0:r0 + _CHUNK, 0:c].astype(f32)
            sg = _sigmoid(u_ref[r0:r0 + _CHUNK, c:2 * c].astype(f32))
            du_ref[r0:r0 + _CHUNK, 0:c] = (acc * sg).astype(du_ref.dtype)
            du_ref[r0:r0 + _CHUNK, c:2 * c] = (acc * val * sg * (1.0 - sg)).astype(du_ref.dtype)
        for k in range(kw):
            dcw_ref[k:k + 1, :] += jnp.sum(tap_sums[k], axis=0, keepdims=True)
        base = ((i % tps) * tm).astype(f32)
        pos = base + lax.broadcasted_iota(jnp.int32, (tm, 1), 0).astype(f32)
        pos_next = base + float(tm) + lax.broadcasted_iota(jnp.int32, (nh, 1), 0).astype(f32)
        for gi, w in enumerate(POOL_WINDOWS):
            sl = slice(gi * gd, (gi + 1) * gd)
            v = pbuf[_HALO:_HALO + tm, sl]
            s = v
            for _, rows in _tap_rows(pbuf.at[:, sl], {j: _HALO - j for j in range(1, w)}, tm):
                s = s + rows
            cnt = _pool_counts(pos, w)
            pooled = (s / cnt - v).astype(_ACT)
            pwg = pw_ref[gi].astype(_ACT)
            mixed = _dot(pooled, pwg, _NN)
            dyp = dy_ref[:, sl].astype(f32)
            dps_ref[0:1, sl] += jnp.sum(dyp * mixed, axis=0, keepdims=True)
            dmix = (dyp * ps_ref[:, sl]).astype(_ACT)
            dpw_ref[gi] += _dot(pooled, dmix, _TN)
            dmix_next = (dyn_ref[:, sl].astype(f32) * ps_ref[:, sl] * keep_next).astype(_ACT)
            dpool = _dot(dmix, pwg, _NT)
            dpbuf[0:tm, sl] = dpool / cnt
            dpbuf[tm:tm + nh, sl] = _dot(dmix_next, pwg, _NT) / _pool_counts(pos_next, w)
            acc = -dpool
            for _, rows in _tap_rows(dpbuf.at[:, sl], {j: j for j in range(w)}, tm):
                acc = acc + rows
            du_ref[:, 2 * c + gi * gd:2 * c + (gi + 1) * gd] = acc.astype(du_ref.dtype)

    hb = tm // _HALO
    n_halo = t // _HALO
    n_nh = t // nh
    full = lambda shape: pl.BlockSpec(shape, lambda i: (0,) * len(shape))
    return pl.pallas_call(
        body, name="mix_bwd_taps", grid=(t // tm,),
        in_specs=[pl.BlockSpec((tm, c3), lambda i: (i, 0)),
                  pl.BlockSpec((_HALO, c3), lambda i: (jnp.maximum(i * hb - 1, 0), 0)),
                  pl.BlockSpec((tm, c), lambda i: (i, 0)),
                  pl.BlockSpec((_HALO, c), lambda i: (jnp.minimum((i + 1) * hb, n_halo - 1), 0)),
                  pl.BlockSpec((tm, c), lambda i: (i, 1)),
                  pl.BlockSpec((nh, c), lambda i: (jnp.minimum((i + 1) * (tm // nh), n_nh - 1), 1)),
                  full((_HALO, c)), full((ng, gd, gd)), full((1, c))],
        out_specs=[pl.BlockSpec((tm, c3), lambda i: (i, 0)), full((_HALO, c)), full((8, c)), full((ng, gd, gd))],
        out_shape=[jax.ShapeDtypeStruct((t, c3), _ACT), jax.ShapeDtypeStruct((_HALO, c), f32),
                   jax.ShapeDtypeStruct((8, c), f32), jax.ShapeDtypeStruct((ng, gd, gd), f32)],
        scratch_shapes=[pltpu.VMEM((_HALO + tm, c), f32), pltpu.VMEM((tm + _HALO, c), f32),
                        pltpu.VMEM((_HALO + tm, c), f32), pltpu.VMEM((tm + nh, c), f32)],
        compiler_params=_params(1),
    )(u, u, dhc, dhc, dy, dy, cw, pw, ps)


def _attn_fwd(q, kv, n_seq, seq, n_mem):
    t, d = q.shape
    dh = d // XATTN_HEADS
    tq = min(1024, seq)
    nq = seq // tq
    scale = dh ** -0.5

    def body(q_ref, kv_ref, o_ref):
        for h in range(XATTN_HEADS):
            cols = slice(h * dh, (h + 1) * dh)
            s = _dot(q_ref[:, cols], kv_ref[:, cols], _NT) * scale
            e = jnp.exp(s - jnp.max(s, axis=-1, keepdims=True))
            p = e / jnp.sum(e, axis=-1, keepdims=True)
            o_ref[:, cols] = _dot(p.astype(_ACT), kv_ref[:, d + h * dh:d + (h + 1) * dh], _NN).astype(o_ref.dtype)

    qs = pl.BlockSpec((tq, d), lambda b, i: (b * nq + i, 0))
    return pl.pallas_call(
        body, name="attn_fwd", grid=(n_seq, nq), in_specs=[qs, pl.BlockSpec((n_mem, 2 * d), lambda b, i: (b, 0))],
        out_specs=qs, out_shape=jax.ShapeDtypeStruct((t, d), _ACT), compiler_params=_params(2),
    )(q, kv)


def _attn_bwd(q, kv, do, n_seq, seq, n_mem):
    t, d = q.shape
    dh = d // XATTN_HEADS
    tq = min(1024, seq)
    nq = seq // tq
    scale = dh ** -0.5

    def body(q_ref, kv_ref, do_ref, dq_ref, dkv_ref, acc):
        i = pl.program_id(1)

        @pl.when(i == 0)
        def _():
            acc[...] = jnp.zeros_like(acc)

        for h in range(XATTN_HEADS):
            cols = slice(h * dh, (h + 1) * dh)
            vcols = slice(d + h * dh, d + (h + 1) * dh)
            qv = q_ref[:, cols]
            kh = kv_ref[:, cols]
            dov = do_ref[:, cols]
            s = _dot(qv, kh, _NT) * scale
            e = jnp.exp(s - jnp.max(s, axis=-1, keepdims=True))
            p = e / jnp.sum(e, axis=-1, keepdims=True)
            dp = _dot(dov, kv_ref[:, vcols], _NT)
            ds = (p * (dp - jnp.sum(dp * p, axis=-1, keepdims=True)) * scale).astype(_ACT)
            dq_ref[:, cols] = _dot(ds, kh, _NN).astype(dq_ref.dtype)
            acc[:, cols] += _dot(ds, qv, _TN)
            acc[:, vcols] += _dot(p.astype(_ACT), dov, _TN)

        @pl.when(i == nq - 1)
        def _():
            dkv_ref[...] = acc[...].astype(dkv_ref.dtype)

    qs = pl.BlockSpec((tq, d), lambda b, i: (b * nq + i, 0))
    ms = pl.BlockSpec((n_mem, 2 * d), lambda b, i: (b, 0))
    return pl.pallas_call(
        body, name="attn_bwd", grid=(n_seq, nq), in_specs=[qs, ms, qs], out_specs=[qs, ms],
        out_shape=[jax.ShapeDtypeStruct((t, d), _ACT), jax.ShapeDtypeStruct((n_seq * n_mem, 2 * d), _ACT)],
        scratch_shapes=[pltpu.VMEM((n_mem, 2 * d), f32)], compiler_params=_params(2),
    )(q, kv, do)


_FFN_ROWS = 2048
_FFN_COLS = 256
_FFN_HALO = 16


def _window(buf, g, start, rows):
    return buf[g, pl.ds(start, rows + 8), :]


def _taps3(win, rows):
    return [_rows_from(win, 6 + k, rows) for k in range(3)]


def _conv3(b_ref, w_ref, taps):
    acc = b_ref[...] + w_ref[0:1, :] * taps[0]
    for k in (1, 2):
        acc = acc + w_ref[k:k + 1, :] * taps[k]
    return acc


def _ffn_gate_fwd(up, fw, fb, seq):
    _, t, f = up.shape
    tm = min(_FFN_ROWS, seq)
    tps = seq // tm
    tc = _FFN_COLS
    nc = f // tc
    hl = _FFN_HALO

    def body(up_ref, uph_ref, wg_ref, wv_ref, bg_ref, bv_ref, a_ref):
        i = pl.program_id(1)
        before = uph_ref[...]
        before = jnp.where(i % tps == 0, jnp.zeros_like(before), before)

        def chunk(r0, wins):
            conv = []
            for g, (w_ref, b_ref) in enumerate(((wg_ref, bg_ref), (wv_ref, bv_ref))):
                conv.append(_conv3(b_ref, w_ref, _taps3(wins[g].astype(f32)[hl - 8:, :], _CHUNK)))
            gate, val = conv
            a_ref[pl.ds(r0, _CHUNK), :] = (gate * _sigmoid(gate) * val).astype(a_ref.dtype)

        chunk(0, [jnp.concatenate([before[g], up_ref[g, 0:_CHUNK, :]], axis=0) for g in range(2)])

        def later(ci, carry):
            r0 = pl.multiple_of(ci * _CHUNK, _CHUNK)
            chunk(r0, [up_ref[g, pl.ds(r0 - hl, _CHUNK + hl), :] for g in range(2)])
            return carry

        lax.fori_loop(1, tm // _CHUNK, later, 0)

    hb = tm // hl
    return pl.pallas_call(
        body, name="ffn_gate_fwd", grid=(nc, t // tm),
        in_specs=[pl.BlockSpec((2, tm, tc), lambda j, i: (0, i, j)),
                  pl.BlockSpec((2, hl, tc), lambda j, i: (0, jnp.maximum(i * hb - 1, 0), j)),
                  pl.BlockSpec((8, tc), lambda j, i: (0, j)), pl.BlockSpec((8, tc), lambda j, i: (0, nc + j)),
                  pl.BlockSpec((1, tc), lambda j, i: (0, j)), pl.BlockSpec((1, tc), lambda j, i: (0, nc + j))],
        out_specs=pl.BlockSpec((tm, tc), lambda j, i: (i, j)),
        out_shape=jax.ShapeDtypeStruct((t, f), _ACT), compiler_params=_params(2),
    )(up, up, fw, fw, fb, fb)


def _ffn_gate_bwd(up, da, fw, fb, seq):
    _, t, f = up.shape
    tm = min(_FFN_ROWS, seq)
    tps = seq // tm
    tc = _FFN_COLS
    nc = f // tc
    hl = _FFN_HALO

    def body(up_ref, uph_ref, upn_ref, da_ref, dan_ref, wg_ref, wv_ref, bg_ref, bv_ref,
             dup_ref, sg_ref, sv_ref, dbuf, sums):
        i = pl.program_id(1)
        at_end = i % tps == tps - 1

        @pl.when(i == 0)
        def _():
            sg_ref[...] = jnp.zeros_like(sg_ref)
            sv_ref[...] = jnp.zeros_like(sv_ref)

        sums[...] = jnp.zeros_like(sums)
        before = uph_ref[...]
        before = jnp.where(i % tps == 0, jnp.zeros_like(before), before)
        after = upn_ref[...]
        after = jnp.where(at_end, jnp.zeros_like(after), after)
        w_refs = (wg_ref, wv_ref)
        b_refs = (bg_ref, bv_ref)

        def grads(r0, rows, wins, dav, count):
            taps = [_taps3(wins[g].astype(f32)[hl - 8:, :], rows) for g in range(2)]
            gate, val = [_conv3(b_refs[g], w_refs[g], taps[g]) for g in range(2)]
            sg = _sigmoid(gate)
            douts = (dav * val * (sg * (1.0 + gate * (1.0 - sg))), dav * (gate * sg))
            for g in range(2):
                dbuf[g, pl.ds(r0, rows), :] = douts[g]
                if count:
                    sums[g, 0] += douts[g].reshape(rows // 8, 8, tc).sum(axis=0)
                    for k in range(3):
                        sums[g, 1 + k] += (douts[g] * taps[g][k]).reshape(rows // 8, 8, tc).sum(axis=0)

        grads(0, _CHUNK, [jnp.concatenate([before[g], up_ref[g, 0:_CHUNK, :]], axis=0) for g in range(2)],
              da_ref[0:_CHUNK, :].astype(f32), True)

        def first(ci, carry):
            r0 = pl.multiple_of(ci * _CHUNK, _CHUNK)
            grads(r0, _CHUNK, [up_ref[g, pl.ds(r0 - hl, _CHUNK + hl), :] for g in range(2)],
                  da_ref[pl.ds(r0, _CHUNK), :].astype(f32), True)
            return carry

        lax.fori_loop(1, tm // _CHUNK, first, 0)
        da_after = dan_ref[...].astype(f32)
        grads(tm, hl, [jnp.concatenate([up_ref[g, tm - hl:tm, :], after[g]], axis=0) for g in range(2)],
              jnp.where(at_end, jnp.zeros_like(da_after), da_after), False)

        def second(ci, carry):
            r0 = pl.multiple_of(ci * _CHUNK, _CHUNK)
            for g in range(2):
                win = _window(dbuf, g, r0, _CHUNK)
                acc = jnp.zeros((_CHUNK, tc), f32)
                for k in range(3):
                    acc = acc + w_refs[g][k:k + 1, :] * _rows_from(win, 2 - k, _CHUNK)
                dup_ref[g, pl.ds(r0, _CHUNK), :] = acc.astype(dup_ref.dtype)
            return carry

        lax.fori_loop(0, tm // _CHUNK, second, 0)
        for g, s_ref in enumerate((sg_ref, sv_ref)):
            for r in range(4):
                s_ref[r:r + 1, :] += jnp.sum(sums[g, r], axis=0, keepdims=True)

    hb = tm // hl
    n_halo = t // hl
    return pl.pallas_call(
        body, name="ffn_gate_bwd", grid=(nc, t // tm),
        in_specs=[pl.BlockSpec((2, tm, tc), lambda j, i: (0, i, j)),
                  pl.BlockSpec((2, hl, tc), lambda j, i: (0, jnp.maximum(i * hb - 1, 0), j)),
                  pl.BlockSpec((2, hl, tc), lambda j, i: (0, jnp.minimum((i + 1) * hb, n_halo - 1), j)),
                  pl.BlockSpec((tm, tc), lambda j, i: (i, j)),
                  pl.BlockSpec((hl, tc), lambda j, i: (jnp.minimum((i + 1) * hb, n_halo - 1), j)),
                  pl.BlockSpec((8, tc), lambda j, i: (0, j)), pl.BlockSpec((8, tc), lambda j, i: (0, nc + j)),
                  pl.BlockSpec((1, tc), lambda j, i: (0, j)), pl.BlockSpec((1, tc), lambda j, i: (0, nc + j))],
        out_specs=[pl.BlockSpec((2, tm, tc), lambda j, i: (0, i, j)),
                   pl.BlockSpec((8, tc), lambda j, i: (0, j)), pl.BlockSpec((8, tc), lambda j, i: (0, j))],
        out_shape=[jax.ShapeDtypeStruct((2, t, f), _ACT), jax.ShapeDtypeStruct((8, f), f32), jax.ShapeDtypeStruct((8, f), f32)],
        scratch_shapes=[pltpu.VMEM((2, tm + hl, tc), f32), pltpu.VMEM((2, 4, 8, tc), f32)],
        compiler_params=_params(2),
    )(up, up, up, da, da, fw, fw, fb, fb)


def _adamw_math(w, g, m, v):
    m = ADAM_B1 * m + (1.0 - ADAM_B1) * g
    v = ADAM_B2 * v + (1.0 - ADAM_B2) * (g * g)
    m_hat = m / (1.0 - ADAM_B1 ** ADAM_STEP)
    v_hat = v / (1.0 - ADAM_B2 ** ADAM_STEP)
    delta = -ADAM_LR * (m_hat / (jnp.sqrt(v_hat) + ADAM_EPS) + ADAM_WD * w)
    return delta, m, v


def _adamw_shards(quads):
    n = len(quads)
    steps = 8

    def body(*refs):
        for p in range(n):
            w_ref, g_ref, m_ref, v_ref = refs[4 * p:4 * p + 4]
            go_ref, d_ref, mo_ref, vo_ref = refs[4 * n + 4 * p:4 * n + 4 * p + 4]
            gv = g_ref[...]
            d, mn, vn = _adamw_math(w_ref[...], gv, m_ref[...], v_ref[...])
            go_ref[...] = gv
            d_ref[...] = d
            mo_ref[...] = mn
            vo_ref[...] = vn

    in_specs, out_specs, out_shape = [], [], []
    for w, _, _, _ in quads:
        _, r, c = w.shape
        s3 = pl.BlockSpec((None, r // steps, c), lambda i: (0, i, 0))
        in_specs += [s3, pl.BlockSpec((r // steps, c), lambda i: (i, 0)), s3, s3]
        out_specs += [s3] * 4
        out_shape += [jax.ShapeDtypeStruct(w.shape, f32)] * 4
    outs = pl.pallas_call(
        body, name="adamw_shards", grid=(steps,), in_specs=in_specs, out_specs=out_specs, out_shape=out_shape,
        compiler_params=_params(1),
    )(*[a for q in quads for a in q])
    return [tuple(outs[4 * p:4 * p + 4]) for p in range(n)]


def _adamw_small(quads):
    n = len(quads)

    def body(*refs):
        ins, outs = refs[:4 * n], refs[4 * n:]
        for p in range(n):
            w_ref, g_ref, m_ref, v_ref = ins[4 * p:4 * p + 4]
            d, mn, vn = _adamw_math(w_ref[...], g_ref[...], m_ref[...], v_ref[...])
            outs[3 * p][...] = d
            outs[3 * p + 1][...] = mn
            outs[3 * p + 2][...] = vn

    flat = [a for q in quads for a in q]
    shapes = [jax.ShapeDtypeStruct(q[0].shape, f32) for q in quads for _ in range(3)]
    outs = pl.pallas_call(
        body, name="adamw_small", in_specs=[_VMEM] * (4 * n), out_specs=[_VMEM] * (3 * n), out_shape=shapes,
        compiler_params=pltpu.CompilerParams(vmem_limit_bytes=_VMEM_LIMIT_BYTES),
    )(*flat)
    return [tuple(outs[3 * p:3 * p + 3]) for p in range(n)]


def _sum_partials(name, place, grads, got):
    nw = len(grads)
    steps = 2

    def body(place_ref, *refs):
        for w in range(nw):
            own_ref, got_ref, f_ref = refs[w], refs[nw + w], refs[2 * nw + w]
            s = own_ref[...].astype(f32)
            for k in range(got[w].shape[0]):
                s = s + got_ref[k].astype(f32)
            f_ref[...] = s

    own_specs, got_specs, out_specs, out_shape = [], [], [], []
    for g, l in zip(grads, got):
        _, r, c = g.shape
        tr = r // steps
        own_specs.append(pl.BlockSpec((None, tr, c), lambda i, p: (2 * p[0] + p[1], i, 0)))
        got_specs.append(pl.BlockSpec((l.shape[0], tr, c), lambda i, p: (0, i, 0)))
        out_specs.append(pl.BlockSpec((None, tr, c), lambda i, p: (p[1], i, 0)))
        out_shape.append(jax.ShapeDtypeStruct((2, r, c), f32))
    grid_spec = pltpu.PrefetchScalarGridSpec(num_scalar_prefetch=1, grid=(steps,), in_specs=own_specs + got_specs, out_specs=out_specs)
    return pl.pallas_call(body, name=name, grid_spec=grid_spec, out_shape=out_shape,
                          compiler_params=_params(1))(place, *grads, *got)


def _place():
    return lax.axis_index("x"), lax.axis_index("y"), lax.axis_index("c")


def _other_chips(x, y):
    return [(1 - x, y), (x, 1 - y), (1 - x, 1 - y)]


def _remote(src, dst, send_sem, recv_sem, to):
    return pltpu.make_async_remote_copy(src_ref=src, dst_ref=dst, send_sem=send_sem, recv_sem=recv_sem,
                                        device_id=to, device_id_type=_MESH)


def _place_shards(name, place, shards, col_sharded, after=None):
    n = len(shards)
    steps = 4
    more, more_specs = _after(after)

    def body(place_ref, *refs):
        for src, dst in zip(refs[:n], refs[n + len(more):]):
            dst[...] = src[...].astype(dst.dtype)

    in_specs, out_specs, out_shape = [], [], []
    for w, col in zip(shards, col_sharded):
        r, cs = w.shape
        tr = r // steps
        in_specs.append(pl.BlockSpec((tr, cs), lambda i, p: (i, 0)))
        if col:
            out_specs.append(pl.BlockSpec((tr, cs), lambda i, p: (i, p[0])))
            out_shape.append(jax.ShapeDtypeStruct((r, 4 * cs), _ACT))
        else:
            out_specs.append(pl.BlockSpec((tr, cs), lambda i, p: (p[0] * steps + i, 0)))
            out_shape.append(jax.ShapeDtypeStruct((4 * r, cs), _ACT))
    grid_spec = pltpu.PrefetchScalarGridSpec(num_scalar_prefetch=1, grid=(steps,), in_specs=in_specs + more_specs,
                                            out_specs=out_specs)
    return pl.pallas_call(body, name=name, grid_spec=grid_spec, out_shape=out_shape,
                          compiler_params=_params(1))(place, *shards, *more)


def _shard_of(ref, col_sharded, s):
    rows, cols = ref.shape
    if col_sharded:
        return ref.at[:, pl.ds(s * (cols // 4), cols // 4)]
    return ref.at[pl.ds(s * (rows // 4), rows // 4), :]


def _part_of(ref, col_sharded, whole, s, h):
    if whole:
        return _shard_of(ref, col_sharded, s)
    rows, cols = ref.shape
    if col_sharded:
        return ref.at[pl.ds(h * (rows // 2), rows // 2), pl.ds(s * (cols // 4), cols // 4)]
    return ref.at[pl.ds((2 * s + h) * (rows // 8), rows // 8), :]


def _allgather_start(name, bufs, col_sharded, whole, groups):
    n = len(bufs)
    ng = len(groups)

    def body(*refs):
        out = refs[n:2 * n]
        sems = refs[2 * n:2 * n + 2 * ng]
        token = refs[2 * n + 2 * ng]
        x, y, c = _place()
        for g, members in enumerate(groups):
            for i, w in enumerate(members):
                mine = _part_of(out[w], col_sharded[w], whole[w], 2 * x + y, c)
                for j, chip in enumerate(_other_chips(x, y)):
                    _remote(mine, mine, sems[2 * g].at[3 * i + j], sems[2 * g + 1].at[3 * i + j], (*chip, c)).start()
        token[...] = jnp.zeros_like(token)

    sem_shapes = [pltpu.SemaphoreType.DMA((3 * len(m),)) for m in groups for _ in range(2)]
    outs = pl.pallas_call(
        body, name=name, in_specs=[_HBM] * n, out_specs=[_HBM] * n + [_SEM] * (2 * ng) + [_VMEM],
        out_shape=[pltpu.HBM(b.shape, b.dtype) for b in bufs] + sem_shapes + [jax.ShapeDtypeStruct((8, 128), f32)],
        input_output_aliases={i: i for i in range(n)},
        compiler_params=pltpu.CompilerParams(has_side_effects=_EFFECT),
    )(*[pltpu.with_memory_space_constraint(b, pltpu.HBM) for b in bufs])
    return list(outs[:n]), [(outs[n + 2 * g], outs[n + 2 * g + 1]) for g in range(ng)], outs[n + 2 * ng]


def _allgather_relay(name, bufs, col_sharded, whole, sems, after):
    n = len(bufs)

    def body(*refs):
        buf = refs[:n]
        send, recv = refs[n], refs[n + 1]
        out = refs[n + 3:2 * n + 3]
        to_sibling, from_sibling, token = refs[2 * n + 3:]
        token[...] = jnp.zeros_like(token)
        x, y, c = _place()
        for i in range(n):
            mine = _part_of(buf[i], col_sharded[i], whole[i], 2 * x + y, c)
            for j, chip in enumerate(_other_chips(x, y)):
                landed = _part_of(buf[i], col_sharded[i], whole[i], 2 * chip[0] + chip[1], c)
                cp = _remote(mine, landed, send.at[3 * i + j], recv.at[3 * i + j], (*chip, c))
                cp.wait_send()
                cp.wait_recv()
        for i in range(n):
            if not whole[i]:
                for j, chip in enumerate(_other_chips(x, y)):
                    landed = _part_of(out[i], col_sharded[i], False, 2 * chip[0] + chip[1], c)
                    _remote(landed, landed, to_sibling.at[3 * i + j], from_sibling.at[3 * i + j], (x, y, 1 - c)).start()

    outs = pl.pallas_call(
        body, name=name, in_specs=[_HBM] * n + [_SEM, _SEM, _ANY], out_specs=[_HBM] * n + [_SEM, _SEM, _VMEM],
        out_shape=[pltpu.HBM(b.shape, b.dtype) for b in bufs] + [pltpu.SemaphoreType.DMA((3 * n,))] * 2
        + [jax.ShapeDtypeStruct((8, 128), f32)],
        input_output_aliases={i: i for i in range(n)},
        compiler_params=pltpu.CompilerParams(has_side_effects=_EFFECT),
    )(*bufs, *sems, after)
    return list(outs[:n]), (outs[n], outs[n + 1]), outs[n + 2]


def _allgather_wait(name, bufs, col_sharded, whole, sems, after):
    n = len(bufs)

    def body(*refs):
        buf = refs[:n]
        to_sibling, from_sibling = refs[n], refs[n + 1]
        x, y, c = _place()
        for i in range(n):
            if not whole[i]:
                for j, chip in enumerate(_other_chips(x, y)):
                    sent = _part_of(buf[i], col_sharded[i], False, 2 * chip[0] + chip[1], c)
                    landed = _part_of(buf[i], col_sharded[i], False, 2 * chip[0] + chip[1], 1 - c)
                    cp = _remote(sent, landed, to_sibling.at[3 * i + j], from_sibling.at[3 * i + j], (x, y, 1 - c))
                    cp.wait_send()
                    cp.wait_recv()

    return pl.pallas_call(
        body, name=name, in_specs=[_HBM] * n + [_SEM, _SEM, _ANY], out_specs=[_HBM] * n,
        out_shape=[pltpu.HBM(b.shape, b.dtype) for b in bufs],
        input_output_aliases={i: i for i in range(n)},
        compiler_params=pltpu.CompilerParams(has_side_effects=_EFFECT),
    )(*bufs, *sems, after)


def _other_devices(x, y, c):
    flips = [(bx, by, bc) for bx in (0, 1) for by in (0, 1) for bc in (0, 1)][1:]
    return [(1 - x if bx else x, 1 - y if by else y, 1 - c if bc else c) for bx, by, bc in flips]


def _grad_exchange_start(name, grads):
    nw = len(grads)
    lands = [lax.empty((7,) + g.shape[1:], g.dtype) for g in grads]

    def body(*refs):
        src = refs[2 * nw:3 * nw]
        got = refs[3 * nw:4 * nw]
        send, recv, token = refs[4 * nw:]
        x, y, c = _place()
        for w in range(nw):
            for k, (px, py, pc) in enumerate(_other_devices(x, y, c)):
                _remote(src[w].at[4 * px + 2 * py + pc], got[w].at[k], send.at[7 * w + k], recv.at[7 * w + k], (px, py, pc)).start()
        token[...] = jnp.zeros_like(token)

    outs = pl.pallas_call(
        body, name=name, in_specs=[_HBM] * (2 * nw), out_specs=[_HBM] * (2 * nw) + [_SEM, _SEM, _VMEM],
        out_shape=[pltpu.HBM(a.shape, a.dtype) for a in list(grads) + lands]
        + [pltpu.SemaphoreType.DMA((7 * nw,)), pltpu.SemaphoreType.DMA((7 * nw,)), jax.ShapeDtypeStruct((8, 128), f32)],
        input_output_aliases={i: i for i in range(2 * nw)},
        compiler_params=pltpu.CompilerParams(has_side_effects=_EFFECT),
    )(*[pltpu.with_memory_space_constraint(a, pltpu.HBM) for a in list(grads) + lands])
    return list(outs[:nw]), list(outs[nw:2 * nw]), (outs[2 * nw], outs[2 * nw + 1]), outs[2 * nw + 2]


def _grad_exchange_wait(name, grads, got, sems, after):
    nw = len(grads)

    def body(*refs):
        src = refs[:nw]
        land = refs[nw:2 * nw]
        send, recv = refs[2 * nw], refs[2 * nw + 1]
        x, y, c = _place()
        for w in range(nw):
            for k, (px, py, pc) in enumerate(_other_devices(x, y, c)):
                cp = _remote(src[w].at[4 * px + 2 * py + pc], land[w].at[k], send.at[7 * w + k], recv.at[7 * w + k], (px, py, pc))
                cp.wait_send()
                cp.wait_recv()

    outs = pl.pallas_call(
        body, name=name, in_specs=[_HBM] * (2 * nw) + [_SEM, _SEM, _ANY], out_specs=[_HBM] * (2 * nw),
        out_shape=[pltpu.HBM(a.shape, a.dtype) for a in list(grads) + list(got)],
        input_output_aliases={i: i for i in range(2 * nw)},
        compiler_params=pltpu.CompilerParams(has_side_effects=_EFFECT),
    )(*grads, *got, *sems, after)
    return list(outs[:nw]), list(outs[nw:])


def _swap_halves_start(finals):
    nw = len(finals)

    def body(*refs):
        buf = refs[nw:2 * nw]
        send, recv, token = refs[2 * nw:]
        x, y, c = _place()
        for w in range(nw):
            _remote(buf[w].at[c], buf[w].at[c], send.at[w], recv.at[w], (x, y, 1 - c)).start()
        token[...] = jnp.zeros_like(token)

    outs = pl.pallas_call(
        body, name="rs_swap_start", in_specs=[_HBM] * nw, out_specs=[_HBM] * nw + [_SEM, _SEM, _VMEM],
        out_shape=[pltpu.HBM(g.shape, g.dtype) for g in finals] + [pltpu.SemaphoreType.DMA((nw,))] * 2
        + [jax.ShapeDtypeStruct((8, 128), f32)],
        input_output_aliases={i: i for i in range(nw)},
        compiler_params=pltpu.CompilerParams(has_side_effects=_EFFECT),
    )(*[pltpu.with_memory_space_constraint(g, pltpu.HBM) for g in finals])
    return list(outs[:nw]), (outs[nw], outs[nw + 1]), outs[nw + 2]


def _swap_halves_wait(bufs, sems, after):
    nw = len(bufs)

    def body(*refs):
        buf = refs[:nw]
        send, recv = refs[nw], refs[nw + 1]
        x, y, c = _place()
        for w in range(nw):
            cp = _remote(buf[w].at[c], buf[w].at[1 - c], send.at[w], recv.at[w], (x, y, 1 - c))
            cp.wait_send()
            cp.wait_recv()

    return pl.pallas_call(
        body, name="rs_swap_wait", in_specs=[_HBM] * nw + [_SEM, _SEM, _ANY], out_specs=[_HBM] * nw,
        out_shape=[pltpu.HBM(g.shape, g.dtype) for g in bufs],
        input_output_aliases={i: i for i in range(nw)},
        compiler_params=pltpu.CompilerParams(has_side_effects=_EFFECT),
    )(*bufs, *sems, after)


def _half_slices(shape, h):
    rows, cols = shape
    if cols % 256 == 0:
        return (slice(None), slice(h * (cols // 2), (h + 1) * (cols // 2)))
    return (slice(h * (rows // 2), (h + 1) * (rows // 2)), slice(None))


def _allreduce_small(parts, after):
    n = len(parts)

    def body(*refs):
        src = refs[:n]
        refs = refs[n + 1:]
        out = refs[:n]
        sib = refs[n:2 * n]
        chip_sum = refs[2 * n:3 * n]
        slots = refs[3 * n:4 * n]
        pair_send, pair_recv, ici_send, ici_recv, swap_send, swap_recv = refs[4 * n:]
        x, y, c = _place()
        me_chip = 2 * x + y
        chips = _other_chips(x, y)
        pairs = [_remote(src[a], sib[a], pair_send.at[a], pair_recv.at[a], (x, y, 1 - c)) for a in range(n)]
        for rc in pairs:
            rc.start()
        for a in range(n):
            pairs[a].wait_recv()
            chip_sum[a][...] = src[a][...] + sib[a][...]
        for h in (0, 1):
            @pl.when(c == h)
            def _():
                sends = []
                for a in range(n):
                    idx = _half_slices(parts[a].shape, h)
                    for j, chip in enumerate(chips):
                        rc = _remote(chip_sum[a].at[idx], slots[a].at[me_chip].at[idx], ici_send.at[3 * a + j], ici_recv.at[3 * a + j], (*chip, h))
                        rc.start()
                        sends.append(rc)
                    slots[a][(me_chip,) + idx] = chip_sum[a][idx]
                for a in range(n):
                    idx = _half_slices(parts[a].shape, h)
                    for j, chip in enumerate(chips):
                        landed = slots[a].at[2 * chip[0] + chip[1]].at[idx]
                        _remote(landed, landed, ici_send.at[3 * a + j], ici_recv.at[3 * a + j], (x, y, c)).wait_recv()
                    total = slots[a][(0,) + idx]
                    for s in range(1, 4):
                        total = total + slots[a][(s,) + idx]
                    out[a][idx] = total
                    rc = _remote(out[a].at[idx], out[a].at[idx], swap_send.at[a], swap_recv.at[a], (x, y, 1 - h))
                    rc.start()
                    sends.append(rc)
                for a in range(n):
                    other = out[a].at[_half_slices(parts[a].shape, 1 - h)]
                    _remote(other, other, swap_send.at[a], swap_recv.at[a], (x, y, c)).wait_recv()
                for rc in sends:
                    rc.wait_send()
        for rc in pairs:
            rc.wait_send()

    return pl.pallas_call(
        body, name="allreduce_small", in_specs=[_VMEM] * n + [_ANY], out_specs=[_VMEM] * n,
        out_shape=[jax.ShapeDtypeStruct(p.shape, f32) for p in parts],
        scratch_shapes=[pltpu.VMEM(p.shape, f32) for p in parts] * 2 + [pltpu.VMEM((4,) + p.shape, f32) for p in parts]
        + [pltpu.SemaphoreType.DMA((n,)), pltpu.SemaphoreType.DMA((n,)), pltpu.SemaphoreType.DMA((3 * n,)),
           pltpu.SemaphoreType.DMA((3 * n,)), pltpu.SemaphoreType.DMA((n,)), pltpu.SemaphoreType.DMA((n,))],
        compiler_params=pltpu.CompilerParams(vmem_limit_bytes=_VMEM_LIMIT_BYTES),
    )(*parts, after)


def _local_step(x, mem, tgt, g_mix, g_xattn, g_mem, g_ffn, g_final, cb, lg, lb, pw, ps, fb, started, relay, weights, reduce,
                n_seq, seq, n_mem):
    t, d = x.shape
    f = fb.shape[1] // 2
    c = cb.shape[1]
    h1 = _rms_fwd("norm_mix", x, g_mix, after=started)
    relay(0, h1)
    w_in, cw, fw = weights(0, h1)
    u = _mm_nn("proj_in", h1, w_in, _ACT, w_in.shape[1])
    y, hc = _mix_fwd(u, cw, cb, lg, lb, pw, ps, seq)
    relay(1, y)
    w_out, w_q, w_kv, w_o = weights(1, y)
    x1, h2 = _proj_residual_norm("proj_out", y, w_out, x, g_xattn)
    q = _mm_nn("proj_q", h2, w_q, _ACT, d)
    mem_n = _rms_fwd("norm_mem", mem, g_mem)
    kv = _mm_nn("proj_kv", mem_n, w_kv, _ACT, 2 * d)
    o = _attn_fwd(q, kv, n_seq, seq, n_mem)
    x2, h3 = _proj_residual_norm("proj_o", o, w_o, x1, g_ffn, after=relay(2, o))
    w_up, w_down = weights(2, h3)
    up = _mm_nn("proj_up", h3, w_up, _ACT, f, split_out=True)
    a = _ffn_gate_fwd(up, fw, fb, seq)
    dx3, dx3b, dg_final, loss = _proj_loss_bwd("proj_down", a, w_down, x2, g_final, tgt)
    da = _mm_nt("d_act", dx3b, w_down, _ACT)
    gw_down = _mm_tn_rows("dw_down", a, dx3b, f // 2, d // 2)
    dup, sums_g, sums_v = _ffn_gate_bwd(up, da, fw, fb, seq)
    gw_up = _mm_tn_pieces("dw_up", h3, dup, f // 2)
    token = reduce(0, [gw_down.reshape(8, -1, d), gw_up])
    dx2, dx2b, dg_ffn = _dproj_rms_bwd("d_h3", dup, w_up, x2, g_ffn, dx3, after=token)
    do = _mm_nt("d_o", dx2b, w_o, _ACT)
    gw_o = _mm_tn_rows("dw_o", o, dx2b, d, d // 2)
    dq, dkv = _attn_bwd(q, kv, do, n_seq, seq, n_mem)
    gw_q = _mm_tn_rows("dw_q", h2, dq, d, d // 2)
    gw_kv = _mm_tn_pieces("dw_kv", mem_n, dkv, d // 2)
    dmem_n = _mm_nt("d_mem_n", dkv, w_kv, f32)
    dg_mem = _rms_gain_grad("norm_mem_bwd", mem, dmem_n)
    dx1, dx1b, dg_xattn = _dproj_rms_bwd("d_h2", dq, w_q, x1, g_xattn, dx2)
    dy = _mm_nt("d_y", dx1b, w_out, _ACT)
    gw_out = _mm_tn_rows("dw_out", y, dx1b, d, d // 2)
    token = reduce(1, [gw_o.reshape(8, -1, d), gw_q.reshape(8, -1, d), gw_kv, gw_out.reshape(8, -1, d)])
    dhc, sums_norm = _mix_bwd_norm(hc, dy, lg, lb, token)
    du, d_cw, d_ps, d_pw = _mix_bwd_taps(u, dhc, dy, cw, pw, ps, seq)
    gw_in = _mm_tn_pieces("dw_in", h1, du, c * 3 // 4)
    token = reduce(2, [gw_in])
    grad_x, dg_mix = _dproj_rms_bwd("d_h1", du, w_in, x, g_mix, dx1, storage_copy=False, after=token)
    zero_row = jnp.zeros((1, d), f32)
    gains = jnp.concatenate([dg_mix, dg_xattn, dg_mem, dg_ffn, dg_final, jnp.pad(loss, ((0, 0), (0, d - 1))), zero_row, zero_row], axis=0)
    conv_rows = jnp.concatenate([sums_norm[2:3], sums_norm[0:1], sums_norm[1:2], d_ps[0:1], jnp.zeros((4, c), f32)], axis=0)
    ffn_rows = jnp.concatenate([sums_g, sums_v], axis=1)
    small = [gains, conv_rows, d_pw.reshape(-1, d_pw.shape[-1]), ffn_rows, d_cw]
    return grad_x, small


def kernel(x, mem, norm_mix_g, w_in, conv_dw_w, conv_dw_b, conv_ln_g, conv_ln_b, pool_w, pool_scale, w_out, norm_xattn_g, norm_mem_g, w_q, w_kv, w_o, norm_ffn_g, w_up, ffn_dw_w, ffn_dw_b, w_down, norm_final_g, loss_target, m_norm_mix_g, m_w_in, m_conv_dw_w, m_conv_dw_b, m_conv_ln_g, m_conv_ln_b, m_pool_w, m_pool_scale, m_w_out, m_norm_xattn_g, m_norm_mem_g, m_w_q, m_w_kv, m_w_o, m_norm_ffn_g, m_w_up, m_ffn_dw_w, m_ffn_dw_b, m_w_down, m_norm_final_g, v_norm_mix_g, v_w_in, v_conv_dw_w, v_conv_dw_b, v_conv_ln_g, v_conv_ln_b, v_pool_w, v_pool_scale, v_w_out, v_norm_xattn_g, v_norm_mem_g, v_w_q, v_w_kv, v_w_o, v_norm_ffn_g, v_w_up, v_ffn_dw_w, v_ffn_dw_b, v_w_down, v_norm_final_g):
    n_seq, seq, d = x.shape
    n_mem = mem.shape[1]
    chip = 2 * lax.axis_index("x") + lax.axis_index("y")

    place = jnp.stack([chip, lax.axis_index("c")]).astype(jnp.int32)

    col_w = [w_in, w_kv, w_up]
    row_w = [w_out, w_q, w_o, w_down]
    kw = conv_dw_w.shape[1]

    def padded_in_place(shard, rows):
        full = jnp.zeros((rows, 4 * shard.shape[1]), shard.dtype)
        return lax.dynamic_update_slice(full, shard, (0, chip * shard.shape[1]))

    first = list(_place_shards("place_w_in", place, [w_in[0]], [True]))
    first += [padded_in_place(conv_dw_w[0], _HALO), padded_in_place(ffn_dw_w[0], 8)]
    first, first_sems, token = _allgather_start("allgather_start_0", first, [True] * 3, [False, True, True], [[0, 1, 2]])
    rest = [w_kv, w_up, w_out, w_q, w_o, w_down]
    rest_flags = [True, True, False, False, False, False]
    rest = list(_place_shards("place_rest", place, [w[0] for w in rest], rest_flags, after=token))
    rest, rest_sems, all_started = _allgather_start("allgather_start_1", rest, rest_flags, [False] * 6, [[2, 3, 0, 4], [1, 5]])
    started = [(first, [True] * 3, [False, True, True], first_sems[0]),
               ([rest[i] for i in (2, 3, 0, 4)], [False, False, True, False], [False] * 4, rest_sems[0]),
               ([rest[i] for i in (1, 5)], [True, False], [False] * 2, rest_sems[1])]
    relayed = {}

    def relay(g, after):
        group_bufs, flags, wholes, group_sems = started[g]
        group_bufs, sibling_sems, relay_token = _allgather_relay("allgather_relay_%d" % g, group_bufs, flags, wholes, group_sems, after)
        relayed[g] = (group_bufs, sibling_sems)
        return relay_token

    def weights(g, after):
        group_bufs, sibling_sems = relayed[g]
        return _allgather_wait("allgather_wait_%d" % g, group_bufs, started[g][1], started[g][2], sibling_sems, after)

    names = ["w_in", "w_kv", "w_up", "w_out", "w_q", "w_o", "w_down"]
    reduce_groups = [["w_down", "w_up"], ["w_o", "w_q", "w_kv", "w_out"], ["w_in"]]
    in_flight = {}

    def reduce(g, grads):
        grads, lands, rs_sems, token = _grad_exchange_start("rs_start_%d" % g, grads)
        in_flight[g] = (grads, lands, rs_sems)
        return token

    grad_x, small = _local_step(
        x.reshape(n_seq * seq, d), mem.reshape(n_seq * n_mem, d), loss_target.reshape(n_seq * seq, d),
        norm_mix_g, norm_xattn_g, norm_mem_g, norm_ffn_g, norm_final_g.reshape(1, d),
        conv_dw_b, conv_ln_g, conv_ln_b, pool_w[0], pool_scale, ffn_dw_b, all_started, relay, weights, reduce,
        n_seq, seq, n_mem)

    landed = {}
    for g, members in enumerate(reduce_groups):
        grads, lands, rs_sems = in_flight[g]
        grads, lands = _grad_exchange_wait("rs_wait_%d" % g, grads, lands, rs_sems, grad_x)
        landed.update(zip(members, zip(grads, lands)))
    finals = _sum_partials("rs_sum", place, [landed[n][0] for n in names], [landed[n][1] for n in names])
    finals, swap_sems, token = _swap_halves_start(finals)

    gains, conv_rows, d_pw, ffn_rows, d_cw = _allreduce_small(small, token)
    loss = gains[5, 0]
    shard_grads = _swap_halves_wait(finals, swap_sems, gains)

    outs = {}
    big_w = dict(zip(names, col_w + row_w))
    big_m = dict(w_in=m_w_in, w_kv=m_w_kv, w_up=m_w_up, w_out=m_w_out, w_q=m_w_q, w_o=m_w_o, w_down=m_w_down)
    big_v = dict(w_in=v_w_in, w_kv=v_w_kv, w_up=v_w_up, w_out=v_w_out, w_q=v_w_q, w_o=v_w_o, w_down=v_w_down)
    big_quads = [(big_w[n], g.reshape(big_w[n].shape[1:]), big_m[n], big_v[n]) for n, g in zip(names, shard_grads)]
    outs.update(zip(names, _adamw_shards(big_quads)))

    f2 = ffn_dw_b.shape[1]
    cs_c = conv_dw_w.shape[2]
    cs_f = ffn_dw_w.shape[2]
    g_cw = lax.dynamic_slice(d_cw, (0, chip * cs_c), (kw, cs_c)).reshape(conv_dw_w.shape)
    g_fw = lax.dynamic_slice(ffn_rows, (1, chip * cs_f), (ffn_dw_w.shape[1], cs_f)).reshape(ffn_dw_w.shape)
    small_params = [
        ("norm_mix_g", norm_mix_g, gains[0:1], m_norm_mix_g, v_norm_mix_g),
        ("conv_dw_w", conv_dw_w, g_cw, m_conv_dw_w, v_conv_dw_w),
        ("conv_dw_b", conv_dw_b, conv_rows[0:1], m_conv_dw_b, v_conv_dw_b),
        ("conv_ln_g", conv_ln_g, conv_rows[1:2], m_conv_ln_g, v_conv_ln_g),
        ("conv_ln_b", conv_ln_b, conv_rows[2:3], m_conv_ln_b, v_conv_ln_b),
        ("pool_w", pool_w, d_pw.reshape(pool_w.shape), m_pool_w, v_pool_w),
        ("pool_scale", pool_scale, conv_rows[3:4], m_pool_scale, v_pool_scale),
        ("norm_xattn_g", norm_xattn_g, gains[1:2], m_norm_xattn_g, v_norm_xattn_g),
        ("norm_mem_g", norm_mem_g, gains[2:3], m_norm_mem_g, v_norm_mem_g),
        ("norm_ffn_g", norm_ffn_g, gains[3:4], m_norm_ffn_g, v_norm_ffn_g),
        ("ffn_dw_w", ffn_dw_w, g_fw, m_ffn_dw_w, v_ffn_dw_w),
        ("ffn_dw_b", ffn_dw_b, ffn_rows[0:1, :f2], m_ffn_dw_b, v_ffn_dw_b),
        ("norm_final_g", norm_final_g.reshape(1, d), gains[4:5], m_norm_final_g.reshape(1, d), v_norm_final_g.reshape(1, d)),
    ]
    quads = []
    for _, w, g, m, v in small_params:
        shape2 = (-1, w.shape[-1])
        quads.append((w.reshape(shape2), g.reshape(shape2), m.reshape(shape2), v.reshape(shape2)))
    for (n, w, g, _, _), (delta, new_m, new_v) in zip(small_params, _adamw_small(quads)):
        shape = norm_final_g.shape if n == "norm_final_g" else w.shape
        outs[n] = (g.reshape(shape), delta.reshape(shape), new_m.reshape(shape), new_v.reshape(shape))

    order = ["norm_mix_g", "w_in", "conv_dw_w", "conv_dw_b", "conv_ln_g", "conv_ln_b", "pool_w", "pool_scale", "w_out",
             "norm_xattn_g", "norm_mem_g", "w_q", "w_kv", "w_o", "norm_ffn_g", "w_up", "ffn_dw_w", "ffn_dw_b", "w_down",
             "norm_final_g"]
    return (loss, grad_x.reshape(x.shape), *[outs[n][0] for n in order], *[outs[n][1] for n in order],
            *[outs[n][2] for n in order], *[outs[n][3] for n in order])
```

```python
import jax
import jax.numpy as jnp
from jax import lax
from jax.experimental import pallas as pl
from jax.experimental.pallas import tpu as pltpu

f32 = jnp.float32
_ACT = jnp.bfloat16

EPS = 1e-6
POOL_WINDOWS = (2, 4, 8, 16)
XATTN_HEADS = 4
ADAM_LR = 0.001
ADAM_B1 = 0.9
ADAM_B2 = 0.999
ADAM_EPS = 1e-08
ADAM_WD = 0.01
ADAM_STEP = 10

_VMEM_LIMIT_BYTES = 56 * 1024 * 1024
_MESH = pl.DeviceIdType.MESH
_ANY = pl.BlockSpec(memory_space=pl.ANY)
_VMEM = pl.BlockSpec(memory_space=pltpu.VMEM)
_HBM = pl.BlockSpec(memory_space=pltpu.HBM)
_SEM = pl.BlockSpec(memory_space=pltpu.SEMAPHORE)
_EFFECT = pltpu.SideEffectType.DATAFLOW_SIDE_EFFECTING

_NN = (((1,), (0,)), ((), ()))
_NT = (((1,), (1,)), ((), ()))
_TN = (((0,), (0,)), ((), ()))


def _params(n_grid):
    return pltpu.CompilerParams(dimension_semantics=("arbitrary",) * n_grid, vmem_limit_bytes=_VMEM_LIMIT_BYTES)


def _sigmoid(v):
    return 1.0 / (1.0 + jnp.exp(-v))


def _dot(a, b, dims):
    return lax.dot_general(a, b, dims, preferred_element_type=f32)


def _mm(name, a, b, *, dims, grid, a_spec, b_spec, o_spec, out_shape):
    def body(a_ref, b_ref, o_ref):
        o_ref[...] = _dot(a_ref[...], b_ref[...], dims).astype(o_ref.dtype)

    return pl.pallas_call(
        body, name=name, grid=grid, in_specs=[a_spec, b_spec], out_specs=o_spec, out_shape=out_shape,
        compiler_params=_params(len(grid)),
    )(a, b)


_NARROW = 2816


def _row_tile(m, width=_NARROW + 1):
    return min(1024 if width <= _NARROW else 512, m)


def _mm_nn(name, a, b, out_dtype, tn, split_out=False):
    m, k = a.shape
    n = b.shape[1]
    tm = _row_tile(m, max(k, tn))
    if split_out:
        out_shape = jax.ShapeDtypeStruct((n // tn, m, tn), out_dtype)
        o_spec = pl.BlockSpec((None, tm, tn), lambda j, i: (j, i, 0))
    else:
        out_shape = jax.ShapeDtypeStruct((m, n), out_dtype)
        o_spec = pl.BlockSpec((tm, tn), lambda j, i: (i, j))
    return _mm(
        name, a, b, dims=_NN, grid=(n // tn, m // tm),
        a_spec=pl.BlockSpec((tm, k), lambda j, i: (i, 0)), b_spec=pl.BlockSpec((k, tn), lambda j, i: (0, j)),
        o_spec=o_spec, out_shape=out_shape,
    )


def _mm_nt(name, a, b, out_dtype):
    n, kc = b.shape
    m = a.shape[0]
    tm = _row_tile(m, max(n, kc))
    return _mm(
        name, a, b, dims=_NT, grid=(m // tm,),
        a_spec=pl.BlockSpec((tm, kc), lambda i: (i, 0)),
        b_spec=pl.BlockSpec((n, kc), lambda i: (0, 0), pipeline_mode=pl.Buffered(1)),
        o_spec=pl.BlockSpec((tm, n), lambda i: (i, 0)),
        out_shape=jax.ShapeDtypeStruct((m, n), out_dtype),
    )


def _mm_tn_rows(name, a, b, tka, tn):
    m, ka = a.shape
    nb = b.shape[1]
    return _mm(
        name, a, b, dims=_TN, grid=(ka // tka, nb // tn),
        a_spec=pl.BlockSpec((m, tka), lambda i, j: (0, i)), b_spec=pl.BlockSpec((m, tn), lambda i, j: (0, j)),
        o_spec=pl.BlockSpec((tka, tn), lambda i, j: (i, j)),
        out_shape=jax.ShapeDtypeStruct((ka, nb), _ACT),
    )


def _mm_tn_pieces(name, a, b, cs):
    m, ka = a.shape
    if b.ndim == 3:
        b_spec = pl.BlockSpec((None, m, cs), lambda i, j: (j // 2, 0, j % 2))
    else:
        b_spec = pl.BlockSpec((m, cs), lambda i, j: (0, j))
    return _mm(
        name, a, b, dims=_TN, grid=(2, 4),
        a_spec=pl.BlockSpec((m, ka // 2), lambda i, j: (0, i)), b_spec=b_spec,
        o_spec=pl.BlockSpec((None, ka // 2, cs), lambda i, j: (2 * j + i, 0, 0)),
        out_shape=jax.ShapeDtypeStruct((8, ka // 2, cs), _ACT),
    )


def _after(after):
    return ([], []) if after is None else ([after], [_ANY])


def _rms_fwd(name, x, g, after=None):
    t, d = x.shape
    tm = _row_tile(t, d)
    more, more_specs = _after(after)

    def body(x_ref, g_ref, *refs):
        h_ref = refs[-1]
        xv = x_ref[...]
        r = lax.rsqrt(jnp.mean(xv * xv, axis=-1, keepdims=True) + EPS)
        h_ref[...] = (xv * r * g_ref[...]).astype(h_ref.dtype)

    return pl.pallas_call(
        body, name=name, grid=(t // tm,),
        in_specs=[pl.BlockSpec((tm, d), lambda i: (i, 0)), pl.BlockSpec((1, d), lambda i: (0, 0))] + more_specs,
        out_specs=pl.BlockSpec((tm, d), lambda i: (i, 0)), out_shape=jax.ShapeDtypeStruct((t, d), _ACT),
        compiler_params=_params(1),
    )(x, g, *more)


def _fused_rows(name, a, b, product, a_spec, tm, extras, extra_specs, out_shape, out_specs, epilogue):
    ne = len(extras)

    def body(a_ref, b_ref, *refs):
        epilogue(product(a_ref, b_ref), refs[:ne], refs[ne:])

    m = extras[0].shape[0]
    return pl.pallas_call(
        body, name=name, grid=(m // tm,),
        in_specs=[a_spec, pl.BlockSpec(b.shape, lambda i: (0, 0), pipeline_mode=pl.Buffered(1)), *extra_specs],
        out_specs=out_specs, out_shape=out_shape, compiler_params=_params(1),
    )(a, b, *extras)


def _proj_residual_norm(name, a, b, res, g, after=None):
    m, k = a.shape
    d = b.shape[1]
    tm = _row_tile(m, max(k, d))

    def epilogue(p, ins, outs):
        xv = p + ins[0][...]
        outs[0][...] = xv
        r = lax.rsqrt(jnp.mean(xv * xv, axis=-1, keepdims=True) + EPS)
        outs[1][...] = (xv * r * ins[1][...]).astype(outs[1].dtype)

    row = pl.BlockSpec((tm, d), lambda i: (i, 0))
    return _fused_rows(
        name, a, b, lambda a_ref, b_ref: _dot(a_ref[...], b_ref[...], _NN), pl.BlockSpec((tm, k), lambda i: (i, 0)), tm,
        [res, g] + _after(after)[0], [row, pl.BlockSpec((1, d), lambda i: (0, 0))] + _after(after)[1],
        [jax.ShapeDtypeStruct((m, d), f32), jax.ShapeDtypeStruct((m, d), _ACT)], [row, row], epilogue)


def _dproj_rms_bwd(name, a, b, x, g, dres, storage_copy=True, after=None):
    m, d = x.shape
    if a.ndim == 3:
        nh, _, kh = a.shape
        tm = _row_tile(m, nh * kh)
        a_spec = pl.BlockSpec((nh, tm, kh), lambda i: (0, i, 0))

        def product(a_ref, b_ref):
            p = _dot(a_ref[0], b_ref[:, 0:kh], _NT)
            for h in range(1, nh):
                p = p + _dot(a_ref[h], b_ref[:, h * kh:(h + 1) * kh], _NT)
            return p
    else:
        tm = _row_tile(m, max(a.shape[1], d))
        a_spec = pl.BlockSpec((tm, a.shape[1]), lambda i: (i, 0))

        def product(a_ref, b_ref):
            return _dot(a_ref[...], b_ref[...], _NT)

    def epilogue(dhv, ins, outs):
        x_ref, g_ref, dres_ref = ins[:3]
        dg_ref = outs[-1]

        @pl.when(pl.program_id(0) == 0)
        def _():
            dg_ref[...] = jnp.zeros_like(dg_ref)

        xv = x_ref[...]
        r = lax.rsqrt(jnp.mean(xv * xv, axis=-1, keepdims=True) + EPS)
        xn = xv * r
        dxn = dhv * g_ref[...]
        dx = r * (dxn - xn * jnp.mean(dxn * xn, axis=-1, keepdims=True)) + dres_ref[...]
        outs[0][...] = dx
        if storage_copy:
            outs[1][...] = dx.astype(outs[1].dtype)
        dg_ref[...] += jnp.sum(dhv * xn, axis=0, keepdims=True)

    row = pl.BlockSpec((tm, d), lambda i: (i, 0))
    vec = pl.BlockSpec((1, d), lambda i: (0, 0))
    copies = [jax.ShapeDtypeStruct((m, d), _ACT)] if storage_copy else []
    return _fused_rows(
        name, a, b, product, a_spec, tm, [x, g, dres] + _after(after)[0], [row, vec, row] + _after(after)[1],
        [jax.ShapeDtypeStruct((m, d), f32)] + copies + [jax.ShapeDtypeStruct((1, d), f32)],
        [row] * (1 + len(copies)) + [vec], epilogue)


def _proj_loss_bwd(name, a, b, res, g, tgt):
    m, k = a.shape
    d = b.shape[1]
    tm = _row_tile(m, max(k, d))

    def epilogue(p, ins, outs):
        res_ref, g_ref, t_ref = ins
        dx_ref, dxb_ref, dg_ref, loss_ref = outs

        @pl.when(pl.program_id(0) == 0)
        def _():
            dg_ref[...] = jnp.zeros_like(dg_ref)
            loss_ref[...] = jnp.zeros_like(loss_ref)

        xv = p + res_ref[...]
        gv = g_ref[...]
        r = lax.rsqrt(jnp.mean(xv * xv, axis=-1, keepdims=True) + EPS)
        xn = xv * r
        err = xn * gv - t_ref[...]
        loss_ref[...] += 0.5 * jnp.sum(jnp.mean(err * err, axis=-1, keepdims=True), axis=0, keepdims=True)
        dout = err * (1.0 / d)
        dxn = dout * gv
        dx = r * (dxn - xn * jnp.mean(dxn * xn, axis=-1, keepdims=True))
        dx_ref[...] = dx
        dxb_ref[...] = dx.astype(dxb_ref.dtype)
        dg_ref[...] += jnp.sum(dout * xn, axis=0, keepdims=True)

    row = pl.BlockSpec((tm, d), lambda i: (i, 0))
    vec = pl.BlockSpec((1, d), lambda i: (0, 0))
    return _fused_rows(
        name, a, b, lambda a_ref, b_ref: _dot(a_ref[...], b_ref[...], _NN), pl.BlockSpec((tm, k), lambda i: (i, 0)), tm,
        [res, g, tgt], [row, vec, row],
        [jax.ShapeDtypeStruct((m, d), f32), jax.ShapeDtypeStruct((m, d), _ACT), jax.ShapeDtypeStruct((1, d), f32),
         jax.ShapeDtypeStruct((1, 1), f32)],
        [row, row, vec, pl.BlockSpec((1, 1), lambda i: (0, 0))], epilogue)


def _rms_gain_grad(name, x, dh):
    t, d = x.shape
    tm = _row_tile(t)

    def body(x_ref, dh_ref, dg_ref):
        @pl.when(pl.program_id(0) == 0)
        def _():
            dg_ref[...] = jnp.zeros_like(dg_ref)

        xv = x_ref[...]
        r = lax.rsqrt(jnp.mean(xv * xv, axis=-1, keepdims=True) + EPS)
        dg_ref[...] += jnp.sum(dh_ref[...] * (xv * r), axis=0, keepdims=True)

    row = pl.BlockSpec((tm, d), lambda i: (i, 0))
    return pl.pallas_call(
        body, name=name, grid=(t // tm,), in_specs=[row, row], out_specs=pl.BlockSpec((1, d), lambda i: (0, 0)),
        out_shape=jax.ShapeDtypeStruct((1, d), f32), compiler_params=_params(1),
    )(x, dh)


_CONV_ROWS = 512
_CHUNK = 64
_HALO = 32


def _pool_counts(pos, w):
    return jnp.minimum(pos + 1.0, float(w))


def _rows_from(win, start, rows):
    if start % 8 == 0:
        return win[start:start + rows, :]
    n = win.shape[0]
    return pltpu.roll(win, n - start % 8, axis=0)[start - start % 8:start - start % 8 + rows, :]


def _tap_rows(buf, starts, rows):
    for residue in range(8):
        group = [(k, s) for k, s in starts.items() if s % 8 == residue]
        if group:
            lo = min(s for _, s in group) - residue
            hi = max(s for _, s in group) - residue + rows + (8 if residue else 0)
            win = buf[lo:hi, :]
            if residue:
                win = pltpu.roll(win, hi - lo - residue, axis=0)
            for k, s in group:
                yield k, win[s - residue - lo:s - residue - lo + rows, :]


def _mix_fwd(u, cw, cb, lg, lb, pw, ps, seq):
    t, c3 = u.shape
    c = c3 // 3
    kw = 31
    tm = min(_CONV_ROWS, seq)
    tps = seq // tm
    gd = c // len(POOL_WINDOWS)

    def body(u_ref, uh_ref, cw_ref, cb_ref, lg_ref, lb_ref, pw_ref, ps_ref, y_ref, hc_ref, hgbuf, pbuf):
        i = pl.program_id(0)
        keep = jnp.where(i % tps == 0, 0.0, 1.0)
        um = u_ref[...].astype(f32)
        uh = uh_ref[...].astype(f32) * keep
        hgbuf[0:_HALO, :] = uh[:, 0:c] * _sigmoid(uh[:, c:2 * c])
        hgbuf[_HALO:_HALO + tm, :] = um[:, 0:c] * _sigmoid(um[:, c:2 * c])
        pbuf[0:_HALO, :] = uh[:, 2 * c:]
        pbuf[_HALO:_HALO + tm, :] = um[:, 2 * c:]
        for r0 in range(0, tm, _CHUNK):
            acc = jnp.broadcast_to(cb_ref[...], (_CHUNK, c))
            for k, rows in _tap_rows(hgbuf, {k: r0 + _HALO - (kw - 1) + k for k in range(kw)}, _CHUNK):
                acc = acc + cw_ref[k:k + 1, :] * rows
            hc_ref[r0:r0 + _CHUNK, :] = acc
            mu = jnp.mean(acc, axis=-1, keepdims=True)
            xc = acc - mu
            var = jnp.mean(xc * xc, axis=-1, keepdims=True)
            hl = xc * lax.rsqrt(var + EPS) * lg_ref[...] + lb_ref[...]
            y_ref[r0:r0 + _CHUNK, 0:c] = (hl * _sigmoid(hl)).astype(y_ref.dtype)
        pos = ((i % tps) * tm).astype(f32) + lax.broadcasted_iota(jnp.int32, (tm, 1), 0).astype(f32)
        for gi, w in enumerate(POOL_WINDOWS):
            sl = slice(gi * gd, (gi + 1) * gd)
            v = pbuf[_HALO:_HALO + tm, sl]
            s = v
            for j in range(1, w):
                s = s + pbuf[_HALO - j:_HALO - j + tm, sl]
            pooled = s / _pool_counts(pos, w) - v
            mixed = _dot(pooled.astype(_ACT), pw_ref[gi].astype(_ACT), _NN)
            y_ref[:, c + gi * gd:c + (gi + 1) * gd] = (mixed * ps_ref[:, sl]).astype(y_ref.dtype)

    hb = tm // _HALO
    full = lambda shape: pl.BlockSpec(shape, lambda i: (0,) * len(shape))
    return pl.pallas_call(
        body, name="mix_fwd", grid=(t // tm,),
        in_specs=[pl.BlockSpec((tm, c3), lambda i: (i, 0)),
                  pl.BlockSpec((_HALO, c3), lambda i: (jnp.maximum(i * hb - 1, 0), 0)),
                  full((_HALO, c)), full((1, c)), full((1, c)), full((1, c)), full((len(POOL_WINDOWS), gd, gd)), full((1, c))],
        out_specs=[pl.BlockSpec((tm, 2 * c), lambda i: (i, 0)), pl.BlockSpec((tm, c), lambda i: (i, 0))],
        out_shape=[jax.ShapeDtypeStruct((t, 2 * c), _ACT), jax.ShapeDtypeStruct((t, c), f32)],
        scratch_shapes=[pltpu.VMEM((_HALO + tm, c), f32), pltpu.VMEM((_HALO + tm, c), f32)],
        compiler_params=_params(1),
    )(u, u, cw, cb, lg, lb, pw, ps)


def _mix_bwd_norm(hc, dy, lg, lb, after):
    t, c = hc.shape
    tm = _row_tile(t, c)

    def body(hc_ref, dy_ref, lg_ref, lb_ref, after_ref, dhc_ref, sums_ref):
        @pl.when(pl.program_id(0) == 0)
        def _():
            sums_ref[...] = jnp.zeros_like(sums_ref)

        hcv = hc_ref[...]
        mu = jnp.mean(hcv, axis=-1, keepdims=True)
        xc = hcv - mu
        rstd = lax.rsqrt(jnp.mean(xc * xc, axis=-1, keepdims=True) + EPS)
        n = xc * rstd
        hl = n * lg_ref[...] + lb_ref[...]
        sg = _sigmoid(hl)
        dhl = dy_ref[...].astype(f32) * (sg * (1.0 + hl * (1.0 - sg)))
        dn = dhl * lg_ref[...]
        dhc = rstd * (dn - jnp.mean(dn, axis=-1, keepdims=True) - n * jnp.mean(dn * n, axis=-1, keepdims=True))
        dhc_ref[...] = dhc
        sums_ref[0:1, :] += jnp.sum(dhl * n, axis=0, keepdims=True)
        sums_ref[1:2, :] += jnp.sum(dhl, axis=0, keepdims=True)
        sums_ref[2:3, :] += jnp.sum(dhc, axis=0, keepdims=True)

    row = pl.BlockSpec((tm, c), lambda i: (i, 0))
    vec = pl.BlockSpec((1, c), lambda i: (0, 0))
    return pl.pallas_call(
        body, name="mix_bwd_norm", grid=(t // tm,), in_specs=[row, row, vec, vec, _ANY],
        out_specs=[row, pl.BlockSpec((8, c), lambda i: (0, 0))],
        out_shape=[jax.ShapeDtypeStruct((t, c), f32), jax.ShapeDtypeStruct((8, c), f32)],
        compiler_params=_params(1),
    )(hc, dy, lg, lb, after)


def _mix_bwd_taps(u, dhc, dy, cw, pw, ps, seq):
    t, c3 = u.shape
    c = c3 // 3
    kw = 31
    tm = min(_CONV_ROWS, seq)
    tps = seq // tm
    ng = len(POOL_WINDOWS)
    gd = c // ng
    nh = 16

    def body(u_ref, uh_ref, dhc_ref, dhcn_ref, dy_ref, dyn_ref, cw_ref, pw_ref, ps_ref,
             du_ref, dcw_ref, dps_ref, dpw_ref, hgbuf, dcbuf, pbuf, dpbuf):
        i = pl.program_id(0)
        keep_prev = jnp.where(i % tps == 0, 0.0, 1.0)
        keep_next = jnp.where(i % tps == tps - 1, 0.0, 1.0)

        @pl.when(i == 0)
        def _():
            dcw_ref[...] = jnp.zeros_like(dcw_ref)
            dps_ref[...] = jnp.zeros_like(dps_ref)
            dpw_ref[...] = jnp.zeros_like(dpw_ref)

        uh = uh_ref[...].astype(f32) * keep_prev
        hgbuf[0:_HALO, :] = uh[:, 0:c] * _sigmoid(uh[:, c:2 * c])
        pbuf[0:_HALO, :] = uh[:, 2 * c:]
        um = u_ref[...].astype(f32)
        hgbuf[_HALO:_HALO + tm, :] = um[:, 0:c] * _sigmoid(um[:, c:2 * c])
        pbuf[_HALO:_HALO + tm, :] = um[:, 2 * c:]
        dcbuf[0:tm, :] = dhc_ref[...]
        dcbuf[tm:tm + _HALO, :] = dhcn_ref[...] * keep_next
        tap_sums = [None] * kw
        for r0 in range(0, tm, _CHUNK):
            dh = dcbuf[r0:r0 + _CHUNK, :]
            acc = jnp.zeros((_CHUNK, c), f32)
            for k, rows in _tap_rows(hgbuf, {k: r0 + _HALO - (kw - 1) + k for k in range(kw)}, _CHUNK):
                part = (dh * rows).reshape(_CHUNK // 8, 8, c).sum(axis=0)
                tap_sums[k] = part if tap_sums[k] is None else tap_sums[k] + part
            for k, rows in _tap_rows(dcbuf, {k: r0 + (kw - 1) - k for k in range(kw)}, _CHUNK):
                acc = acc + cw_ref[k:k + 1, :] * rows
            val = u_ref[r0:r0 + _CHUNK, 0:c].astype(f32)
            sg = _sigmoid(u_ref[r0:r0 + _CHUNK, c:2 * c].astype(f32))
            du_ref[r0:r0 + _CHUNK, 0:c] = (acc * sg).astype(du_ref.dtype)
            du_ref[r0:r0 + _CHUNK, c:2 * c] = (acc * val * sg * (1.0 - sg)).astype(du_ref.dtype)
        for k in range(kw):
            dcw_ref[k:k + 1, :] += jnp.sum(tap_sums[k], axis=0, keepdims=True)
        base = ((i % tps) * tm).astype(f32)
        pos = base + lax.broadcasted_iota(jnp.int32, (tm, 1), 0).astype(f32)
        pos_next = base + float(tm) + lax.broadcasted_iota(jnp.int32, (nh, 1), 0).astype(f32)
        for gi, w in enumerate(POOL_WINDOWS):
            sl = slice(gi * gd, (gi + 1) * gd)
            v = pbuf[_HALO:_HALO + tm, sl]
            s = v
            for j in range(1, w):
                s = s + pbuf[_HALO - j:_HALO - j + tm, sl]
            cnt = _pool_counts(pos, w)
            pooled = (s / cnt - v).astype(_ACT)
            pwg = pw_ref[gi].astype(_ACT)
            mixed = _dot(pooled, pwg, _NN)
            dyp = dy_ref[:, sl].astype(f32)
            dps_ref[0:1, sl] += jnp.sum(dyp * mixed, axis=0, keepdims=True)
            dmix = (dyp * ps_ref[:, sl]).astype(_ACT)
            dpw_ref[gi] += _dot(pooled, dmix, _TN)
            dmix_next = (dyn_ref[:, sl].astype(f32) * ps_ref[:, sl] * keep_next).astype(_ACT)
            dpool = _dot(dmix, pwg, _NT)
            dpbuf[0:tm, sl] = dpool / cnt
            dpbuf[tm:tm + nh, sl] = _dot(dmix_next, pwg, _NT) / _pool_counts(pos_next, w)
            acc = -dpool
            for j in range(w):
                acc = acc + dpbuf[j:j + tm, sl]
            du_ref[:, 2 * c + gi * gd:2 * c + (gi + 1) * gd] = acc.astype(du_ref.dtype)

    hb = tm // _HALO
    n_halo = t // _HALO
    n_nh = t // nh
    full = lambda shape: pl.BlockSpec(shape, lambda i: (0,) * len(shape))
    return pl.pallas_call(
        body, name="mix_bwd_taps", grid=(t // tm,),
        in_specs=[pl.BlockSpec((tm, c3), lambda i: (i, 0)),
                  pl.BlockSpec((_HALO, c3), lambda i: (jnp.maximum(i * hb - 1, 0), 0)),
                  pl.BlockSpec((tm, c), lambda i: (i, 0)),
                  pl.BlockSpec((_HALO, c), lambda i: (jnp.minimum((i + 1) * hb, n_halo - 1), 0)),
                  pl.BlockSpec((tm, c), lambda i: (i, 1)),
                  pl.BlockSpec((nh, c), lambda i: (jnp.minimum((i + 1) * (tm // nh), n_nh - 1), 1)),
                  full((_HALO, c)), full((ng, gd, gd)), full((1, c))],
        out_specs=[pl.BlockSpec((tm, c3), lambda i: (i, 0)), full((_HALO, c)), full((8, c)), full((ng, gd, gd))],
        out_shape=[jax.ShapeDtypeStruct((t, c3), _ACT), jax.ShapeDtypeStruct((_HALO, c), f32),
                   jax.ShapeDtypeStruct((8, c), f32), jax.ShapeDtypeStruct((ng, gd, gd), f32)],
        scratch_shapes=[pltpu.VMEM((_HALO + tm, c), f32), pltpu.VMEM((tm + _HALO, c), f32),
                        pltpu.VMEM((_HALO + tm, c), f32), pltpu.VMEM((tm + nh, c), f32)],
        compiler_params=_params(1),
    )(u, u, dhc, dhc, dy, dy, cw, pw, ps)


def _attn_fwd(q, kv, n_seq, seq, n_mem):
    t, d = q.shape
    dh = d // XATTN_HEADS
    tq = min(1024, seq)
    nq = seq // tq
    scale = dh ** -0.5

    def body(q_ref, kv_ref, o_ref):
        for h in range(XATTN_HEADS):
            cols = slice(h * dh, (h + 1) * dh)
            s = _dot(q_ref[:, cols], kv_ref[:, cols], _NT) * scale
            e = jnp.exp(s - jnp.max(s, axis=-1, keepdims=True))
            p = e / jnp.sum(e, axis=-1, keepdims=True)
            o_ref[:, cols] = _dot(p.astype(_ACT), kv_ref[:, d + h * dh:d + (h + 1) * dh], _NN).astype(o_ref.dtype)

    qs = pl.BlockSpec((tq, d), lambda b, i: (b * nq + i, 0))
    return pl.pallas_call(
        body, name="attn_fwd", grid=(n_seq, nq), in_specs=[qs, pl.BlockSpec((n_mem, 2 * d), lambda b, i: (b, 0))],
        out_specs=qs, out_shape=jax.ShapeDtypeStruct((t, d), _ACT), compiler_params=_params(2),
    )(q, kv)


def _attn_bwd(q, kv, do, n_seq, seq, n_mem):
    t, d = q.shape
    dh = d // XATTN_HEADS
    tq = min(1024, seq)
    nq = seq // tq
    scale = dh ** -0.5

    def body(q_ref, kv_ref, do_ref, dq_ref, dkv_ref, acc):
        i = pl.program_id(1)

        @pl.when(i == 0)
        def _():
            acc[...] = jnp.zeros_like(acc)

        for h in range(XATTN_HEADS):
            cols = slice(h * dh, (h + 1) * dh)
            vcols = slice(d + h * dh, d + (h + 1) * dh)
            qv = q_ref[:, cols]
            kh = kv_ref[:, cols]
            dov = do_ref[:, cols]
            s = _dot(qv, kh, _NT) * scale
            e = jnp.exp(s - jnp.max(s, axis=-1, keepdims=True))
            p = e / jnp.sum(e, axis=-1, keepdims=True)
            dp = _dot(dov, kv_ref[:, vcols], _NT)
            ds = (p * (dp - jnp.sum(dp * p, axis=-1, keepdims=True)) * scale).astype(_ACT)
            dq_ref[:, cols] = _dot(ds, kh, _NN).astype(dq_ref.dtype)
            acc[:, cols] += _dot(ds, qv, _TN)
            acc[:, vcols] += _dot(p.astype(_ACT), dov, _TN)

        @pl.when(i == nq - 1)
        def _():
            dkv_ref[...] = acc[...].astype(dkv_ref.dtype)

    qs = pl.BlockSpec((tq, d), lambda b, i: (b * nq + i, 0))
    ms = pl.BlockSpec((n_mem, 2 * d), lambda b, i: (b, 0))
    return pl.pallas_call(
        body, name="attn_bwd", grid=(n_seq, nq), in_specs=[qs, ms, qs], out_specs=[qs, ms],
        out_shape=[jax.ShapeDtypeStruct((t, d), _ACT), jax.ShapeDtypeStruct((n_seq * n_mem, 2 * d), _ACT)],
        scratch_shapes=[pltpu.VMEM((n_mem, 2 * d), f32)], compiler_params=_params(2),
    )(q, kv, do)


_FFN_ROWS = 2048
_FFN_COLS = 256
_FFN_HALO = 16


def _window(buf, g, start, rows):
    return buf[g, pl.ds(start, rows + 8), :]


def _taps3(win, rows):
    return [_rows_from(win, 6 + k, rows) for k in range(3)]


def _conv3(b_ref, w_ref, taps):
    acc = b_ref[...] + w_ref[0:1, :] * taps[0]
    for k in (1, 2):
        acc = acc + w_ref[k:k + 1, :] * taps[k]
    return acc


def _ffn_gate_fwd(up, fw, fb, seq):
    _, t, f = up.shape
    tm = min(_FFN_ROWS, seq)
    tps = seq // tm
    tc = _FFN_COLS
    nc = f // tc
    hl = _FFN_HALO

    def body(up_ref, uph_ref, wg_ref, wv_ref, bg_ref, bv_ref, a_ref, uc_ref):
        i = pl.program_id(1)
        before = uph_ref[...]
        before = jnp.where(i % tps == 0, jnp.zeros_like(before), before)

        def chunk(r0, wins):
            conv = []
            for g, (w_ref, b_ref) in enumerate(((wg_ref, bg_ref), (wv_ref, bv_ref))):
                conv.append(_conv3(b_ref, w_ref, _taps3(wins[g].astype(f32)[hl - 8:, :], _CHUNK)))
                uc_ref[g, pl.ds(r0, _CHUNK), :] = conv[g].astype(uc_ref.dtype)
            gate, val = conv
            a_ref[pl.ds(r0, _CHUNK), :] = (gate * _sigmoid(gate) * val).astype(a_ref.dtype)

        chunk(0, [jnp.concatenate([before[g], up_ref[g, 0:_CHUNK, :]], axis=0) for g in range(2)])

        def later(ci, carry):
            r0 = pl.multiple_of(ci * _CHUNK, _CHUNK)
            chunk(r0, [up_ref[g, pl.ds(r0 - hl, _CHUNK + hl), :] for g in range(2)])
            return carry

        lax.fori_loop(1, tm // _CHUNK, later, 0)

    hb = tm // hl
    return pl.pallas_call(
        body, name="ffn_gate_fwd", grid=(nc, t // tm),
        in_specs=[pl.BlockSpec((2, tm, tc), lambda j, i: (0, i, j)),
                  pl.BlockSpec((2, hl, tc), lambda j, i: (0, jnp.maximum(i * hb - 1, 0), j)),
                  pl.BlockSpec((8, tc), lambda j, i: (0, j)), pl.BlockSpec((8, tc), lambda j, i: (0, nc + j)),
                  pl.BlockSpec((1, tc), lambda j, i: (0, j)), pl.BlockSpec((1, tc), lambda j, i: (0, nc + j))],
        out_specs=[pl.BlockSpec((tm, tc), lambda j, i: (i, j)), pl.BlockSpec((2, tm, tc), lambda j, i: (0, i, j))],
        out_shape=[jax.ShapeDtypeStruct((t, f), _ACT), jax.ShapeDtypeStruct((2, t, f), _ACT)], compiler_params=_params(2),
    )(up, up, fw, fw, fb, fb)


def _ffn_gate_bwd(up, uc, da, fw, seq):
    _, t, f = up.shape
    tm = min(_FFN_ROWS, seq)
    tps = seq // tm
    tc = _FFN_COLS
    nc = f // tc
    hl = _FFN_HALO

    def body(up_ref, uph_ref, uc_ref, ucn_ref, da_ref, dan_ref, wg_ref, wv_ref, dup_ref, sg_ref, sv_ref, dbuf, sums):
        i = pl.program_id(1)
        at_end = i % tps == tps - 1

        @pl.when(i == 0)
        def _():
            sg_ref[...] = jnp.zeros_like(sg_ref)
            sv_ref[...] = jnp.zeros_like(sv_ref)

        sums[...] = jnp.zeros_like(sums)
        before = uph_ref[...]
        before = jnp.where(i % tps == 0, jnp.zeros_like(before), before)
        w_refs = (wg_ref, wv_ref)

        def grads(r0, rows, conv, dav):
            gate, val = [v.astype(f32) for v in conv]
            sg = _sigmoid(gate)
            douts = (dav * val * (sg * (1.0 + gate * (1.0 - sg))), dav * (gate * sg))
            for g in range(2):
                dbuf[g, pl.ds(r0, rows), :] = douts[g]
            return douts

        def count(douts, wins):
            for g in range(2):
                taps = _taps3(wins[g].astype(f32)[hl - 8:, :], _CHUNK)
                sums[g, 0] += douts[g].reshape(_CHUNK // 8, 8, tc).sum(axis=0)
                for k in range(3):
                    sums[g, 1 + k] += (douts[g] * taps[k]).reshape(_CHUNK // 8, 8, tc).sum(axis=0)

        count(grads(0, _CHUNK, [uc_ref[g, 0:_CHUNK, :] for g in range(2)], da_ref[0:_CHUNK, :].astype(f32)),
              [jnp.concatenate([before[g], up_ref[g, 0:_CHUNK, :]], axis=0) for g in range(2)])

        def first(ci, carry):
            r0 = pl.multiple_of(ci * _CHUNK, _CHUNK)
            douts = grads(r0, _CHUNK, [uc_ref[g, pl.ds(r0, _CHUNK), :] for g in range(2)],
                          da_ref[pl.ds(r0, _CHUNK), :].astype(f32))
            count(douts, [up_ref[g, pl.ds(r0 - hl, _CHUNK + hl), :] for g in range(2)])
            return carry

        lax.fori_loop(1, tm // _CHUNK, first, 0)
        da_after = dan_ref[...].astype(f32)
        grads(tm, hl, [ucn_ref[g] for g in range(2)], jnp.where(at_end, jnp.zeros_like(da_after), da_after))

        def second(ci, carry):
            r0 = pl.multiple_of(ci * _CHUNK, _CHUNK)
            for g in range(2):
                win = _window(dbuf, g, r0, _CHUNK)
                acc = jnp.zeros((_CHUNK, tc), f32)
                for k in range(3):
                    acc = acc + w_refs[g][k:k + 1, :] * _rows_from(win, 2 - k, _CHUNK)
                dup_ref[g, pl.ds(r0, _CHUNK), :] = acc.astype(dup_ref.dtype)
            return carry

        lax.fori_loop(0, tm // _CHUNK, second, 0)
        for g, s_ref in enumerate((sg_ref, sv_ref)):
            for r in range(4):
                s_ref[r:r + 1, :] += jnp.sum(sums[g, r], axis=0, keepdims=True)

    hb = tm // hl
    n_halo = t // hl
    return pl.pallas_call(
        body, name="ffn_gate_bwd", grid=(nc, t // tm),
        in_specs=[pl.BlockSpec((2, tm, tc), lambda j, i: (0, i, j)),
                  pl.BlockSpec((2, hl, tc), lambda j, i: (0, jnp.maximum(i * hb - 1, 0), j)),
                  pl.BlockSpec((2, tm, tc), lambda j, i: (0, i, j)),
                  pl.BlockSpec((2, hl, tc), lambda j, i: (0, jnp.minimum((i + 1) * hb, n_halo - 1), j)),
                  pl.BlockSpec((tm, tc), lambda j, i: (i, j)),
                  pl.BlockSpec((hl, tc), lambda j, i: (jnp.minimum((i + 1) * hb, n_halo - 1), j)),
                  pl.BlockSpec((8, tc), lambda j, i: (0, j)), pl.BlockSpec((8, tc), lambda j, i: (0, nc + j))],
        out_specs=[pl.BlockSpec((2, tm, tc), lambda j, i: (0, i, j)),
                   pl.BlockSpec((8, tc), lambda j, i: (0, j)), pl.BlockSpec((8, tc), lambda j, i: (0, j))],
        out_shape=[jax.ShapeDtypeStruct((2, t, f), _ACT), jax.ShapeDtypeStruct((8, f), f32), jax.ShapeDtypeStruct((8, f), f32)],
        scratch_shapes=[pltpu.VMEM((2, tm + hl, tc), f32), pltpu.VMEM((2, 4, 8, tc), f32)],
        compiler_params=_params(2),
    )(up, up, uc, uc, da, da, fw, fw)


def _adamw_math(w, g, m, v):
    m = ADAM_B1 * m + (1.0 - ADAM_B1) * g
    v = ADAM_B2 * v + (1.0 - ADAM_B2) * (g * g)
    m_hat = m / (1.0 - ADAM_B1 ** ADAM_STEP)
    v_hat = v / (1.0 - ADAM_B2 ** ADAM_STEP)
    delta = -ADAM_LR * (m_hat / (jnp.sqrt(v_hat) + ADAM_EPS) + ADAM_WD * w)
    return delta, m, v


def _adamw_shards(quads):
    n = len(quads)
    steps = 8

    def body(*refs):
        for p in range(n):
            w_ref, g_ref, m_ref, v_ref = refs[4 * p:4 * p + 4]
            go_ref, d_ref, mo_ref, vo_ref = refs[4 * n + 4 * p:4 * n + 4 * p + 4]
            gv = g_ref[...]
            d, mn, vn = _adamw_math(w_ref[...], gv, m_ref[...], v_ref[...])
            go_ref[...] = gv
            d_ref[...] = d
            mo_ref[...] = mn
            vo_ref[...] = vn

    in_specs, out_specs, out_shape = [], [], []
    for w, _, _, _ in quads:
        _, r, c = w.shape
        s3 = pl.BlockSpec((None, r // steps, c), lambda i: (0, i, 0))
        in_specs += [s3, pl.BlockSpec((r // steps, c), lambda i: (i, 0)), s3, s3]
        out_specs += [s3] * 4
        out_shape += [jax.ShapeDtypeStruct(w.shape, f32)] * 4
    outs = pl.pallas_call(
        body, name="adamw_shards", grid=(steps,), in_specs=in_specs, out_specs=out_specs, out_shape=out_shape,
        compiler_params=_params(1),
    )(*[a for q in quads for a in q])
    return [tuple(outs[4 * p:4 * p + 4]) for p in range(n)]


def _adamw_small(quads):
    n = len(quads)

    def body(*refs):
        ins, outs = refs[:4 * n], refs[4 * n:]
        for p in range(n):
            w_ref, g_ref, m_ref, v_ref = ins[4 * p:4 * p + 4]
            d, mn, vn = _adamw_math(w_ref[...], g_ref[...], m_ref[...], v_ref[...])
            outs[3 * p][...] = d
            outs[3 * p + 1][...] = mn
            outs[3 * p + 2][...] = vn

    flat = [a for q in quads for a in q]
    shapes = [jax.ShapeDtypeStruct(q[0].shape, f32) for q in quads for _ in range(3)]
    outs = pl.pallas_call(
        body, name="adamw_small", in_specs=[_VMEM] * (4 * n), out_specs=[_VMEM] * (3 * n), out_shape=shapes,
        compiler_params=pltpu.CompilerParams(vmem_limit_bytes=_VMEM_LIMIT_BYTES),
    )(*flat)
    return [tuple(outs[3 * p:3 * p + 3]) for p in range(n)]


def _sum_partials(name, place, grads, got):
    nw = len(grads)
    steps = 2

    def body(place_ref, *refs):
        for w in range(nw):
            own_ref, got_ref, f_ref = refs[w], refs[nw + w], refs[2 * nw + w]
            s = own_ref[...].astype(f32)
            for k in range(got[w].shape[0]):
                s = s + got_ref[k].astype(f32)
            f_ref[...] = s

    own_specs, got_specs, out_specs, out_shape = [], [], [], []
    for g, l in zip(grads, got):
        _, r, c = g.shape
        tr = r // steps
        own_specs.append(pl.BlockSpec((None, tr, c), lambda i, p: (2 * p[0] + p[1], i, 0)))
        got_specs.append(pl.BlockSpec((l.shape[0], tr, c), lambda i, p: (0, i, 0)))
        out_specs.append(pl.BlockSpec((None, tr, c), lambda i, p: (p[1], i, 0)))
        out_shape.append(jax.ShapeDtypeStruct((2, r, c), f32))
    grid_spec = pltpu.PrefetchScalarGridSpec(num_scalar_prefetch=1, grid=(steps,), in_specs=own_specs + got_specs, out_specs=out_specs)
    return pl.pallas_call(body, name=name, grid_spec=grid_spec, out_shape=out_shape,
                          compiler_params=_params(1))(place, *grads, *got)


def _place():
    return lax.axis_index("x"), lax.axis_index("y"), lax.axis_index("c")


def _other_chips(x, y):
    return [(1 - x, y), (x, 1 - y), (1 - x, 1 - y)]


def _remote(src, dst, send_sem, recv_sem, to):
    return pltpu.make_async_remote_copy(src_ref=src, dst_ref=dst, send_sem=send_sem, recv_sem=recv_sem,
                                        device_id=to, device_id_type=_MESH)


def _place_shards(name, place, shards, col_sharded, after=None):
    n = len(shards)
    steps = 4
    more, more_specs = _after(after)

    def body(place_ref, *refs):
        for src, dst in zip(refs[:n], refs[n + len(more):]):
            dst[...] = src[...].astype(dst.dtype)

    in_specs, out_specs, out_shape = [], [], []
    for w, col in zip(shards, col_sharded):
        r, cs = w.shape
        tr = r // steps
        in_specs.append(pl.BlockSpec((tr, cs), lambda i, p: (i, 0)))
        if col:
            out_specs.append(pl.BlockSpec((tr, cs), lambda i, p: (i, p[0])))
            out_shape.append(jax.ShapeDtypeStruct((r, 4 * cs), _ACT))
        else:
            out_specs.append(pl.BlockSpec((tr, cs), lambda i, p: (p[0] * steps + i, 0)))
            out_shape.append(jax.ShapeDtypeStruct((4 * r, cs), _ACT))
    grid_spec = pltpu.PrefetchScalarGridSpec(num_scalar_prefetch=1, grid=(steps,), in_specs=in_specs + more_specs,
                                            out_specs=out_specs)
    return pl.pallas_call(body, name=name, grid_spec=grid_spec, out_shape=out_shape,
                          compiler_params=_params(1))(place, *shards, *more)


def _shard_of(ref, col_sharded, s):
    rows, cols = ref.shape
    if col_sharded:
        return ref.at[:, pl.ds(s * (cols // 4), cols // 4)]
    return ref.at[pl.ds(s * (rows // 4), rows // 4), :]


def _part_of(ref, col_sharded, whole, s, h):
    if whole:
        return _shard_of(ref, col_sharded, s)
    rows, cols = ref.shape
    if col_sharded:
        return ref.at[pl.ds(h * (rows // 2), rows // 2), pl.ds(s * (cols // 4), cols // 4)]
    return ref.at[pl.ds((2 * s + h) * (rows // 8), rows // 8), :]


def _allgather_start(name, bufs, col_sharded, whole, groups):
    n = len(bufs)
    ng = len(groups)

    def body(*refs):
        out = refs[n:2 * n]
        sems = refs[2 * n:2 * n + 2 * ng]
        token = refs[2 * n + 2 * ng]
        x, y, c = _place()
        for g, members in enumerate(groups):
            for i, w in enumerate(members):
                mine = _part_of(out[w], col_sharded[w], whole[w], 2 * x + y, c)
                for j, chip in enumerate(_other_chips(x, y)):
                    _remote(mine, mine, sems[2 * g].at[3 * i + j], sems[2 * g + 1].at[3 * i + j], (*chip, c)).start()
        token[...] = jnp.zeros_like(token)

    sem_shapes = [pltpu.SemaphoreType.DMA((3 * len(m),)) for m in groups for _ in range(2)]
    outs = pl.pallas_call(
        body, name=name, in_specs=[_HBM] * n, out_specs=[_HBM] * n + [_SEM] * (2 * ng) + [_VMEM],
        out_shape=[pltpu.HBM(b.shape, b.dtype) for b in bufs] + sem_shapes + [jax.ShapeDtypeStruct((8, 128), f32)],
        input_output_aliases={i: i for i in range(n)},
        compiler_params=pltpu.CompilerParams(has_side_effects=_EFFECT),
    )(*[pltpu.with_memory_space_constraint(b, pltpu.HBM) for b in bufs])
    return list(outs[:n]), [(outs[n + 2 * g], outs[n + 2 * g + 1]) for g in range(ng)], outs[n + 2 * ng]


def _allgather_relay(name, bufs, col_sharded, whole, sems, after):
    n = len(bufs)

    def body(*refs):
        buf = refs[:n]
        send, recv = refs[n], refs[n + 1]
        out = refs[n + 3:2 * n + 3]
        to_sibling, from_sibling, token = refs[2 * n + 3:]
        token[...] = jnp.zeros_like(token)
        x, y, c = _place()
        for i in range(n):
            mine = _part_of(buf[i], col_sharded[i], whole[i], 2 * x + y, c)
            for j, chip in enumerate(_other_chips(x, y)):
                landed = _part_of(buf[i], col_sharded[i], whole[i], 2 * chip[0] + chip[1], c)
                cp = _remote(mine, landed, send.at[3 * i + j], recv.at[3 * i + j], (*chip, c))
                cp.wait_send()
                cp.wait_recv()
        for i in range(n):
            if not whole[i]:
                for j, chip in enumerate(_other_chips(x, y)):
                    landed = _part_of(out[i], col_sharded[i], False, 2 * chip[0] + chip[1], c)
                    _remote(landed, landed, to_sibling.at[3 * i + j], from_sibling.at[3 * i + j], (x, y, 1 - c)).start()

    outs = pl.pallas_call(
        body, name=name, in_specs=[_HBM] * n + [_SEM, _SEM, _ANY], out_specs=[_HBM] * n + [_SEM, _SEM, _VMEM],
        out_shape=[pltpu.HBM(b.shape, b.dtype) for b in bufs] + [pltpu.SemaphoreType.DMA((3 * n,))] * 2
        + [jax.ShapeDtypeStruct((8, 128), f32)],
        input_output_aliases={i: i for i in range(n)},
        compiler_params=pltpu.CompilerParams(has_side_effects=_EFFECT),
    )(*bufs, *sems, after)
    return list(outs[:n]), (outs[n], outs[n + 1]), outs[n + 2]


def _allgather_wait(name, bufs, col_sharded, whole, sems, after):
    n = len(bufs)

    def body(*refs):
        buf = refs[:n]
        to_sibling, from_sibling = refs[n], refs[n + 1]
        x, y, c = _place()
        for i in range(n):
            if not whole[i]:
                for j, chip in enumerate(_other_chips(x, y)):
                    sent = _part_of(buf[i], col_sharded[i], False, 2 * chip[0] + chip[1], c)
                    landed = _part_of(buf[i], col_sharded[i], False, 2 * chip[0] + chip[1], 1 - c)
                    cp = _remote(sent, landed, to_sibling.at[3 * i + j], from_sibling.at[3 * i + j], (x, y, 1 - c))
                    cp.wait_send()
                    cp.wait_recv()

    return pl.pallas_call(
        body, name=name, in_specs=[_HBM] * n + [_SEM, _SEM, _ANY], out_specs=[_HBM] * n,
        out_shape=[pltpu.HBM(b.shape, b.dtype) for b in bufs],
        input_output_aliases={i: i for i in range(n)},
        compiler_params=pltpu.CompilerParams(has_side_effects=_EFFECT),
    )(*bufs, *sems, after)


def _other_devices(x, y, c):
    flips = [(bx, by, bc) for bx in (0, 1) for by in (0, 1) for bc in (0, 1)][1:]
    return [(1 - x if bx else x, 1 - y if by else y, 1 - c if bc else c) for bx, by, bc in flips]


def _grad_exchange_start(name, grads):
    nw = len(grads)
    lands = [lax.empty((7,) + g.shape[1:], g.dtype) for g in grads]

    def body(*refs):
        src = refs[2 * nw:3 * nw]
        got = refs[3 * nw:4 * nw]
        send, recv, token = refs[4 * nw:]
        x, y, c = _place()
        for w in range(nw):
            for k, (px, py, pc) in enumerate(_other_devices(x, y, c)):
                _remote(src[w].at[4 * px + 2 * py + pc], got[w].at[k], send.at[7 * w + k], recv.at[7 * w + k], (px, py, pc)).start()
        token[...] = jnp.zeros_like(token)

    outs = pl.pallas_call(
        body, name=name, in_specs=[_HBM] * (2 * nw), out_specs=[_HBM] * (2 * nw) + [_SEM, _SEM, _VMEM],
        out_shape=[pltpu.HBM(a.shape, a.dtype) for a in list(grads) + lands]
        + [pltpu.SemaphoreType.DMA((7 * nw,)), pltpu.SemaphoreType.DMA((7 * nw,)), jax.ShapeDtypeStruct((8, 128), f32)],
        input_output_aliases={i: i for i in range(2 * nw)},
        compiler_params=pltpu.CompilerParams(has_side_effects=_EFFECT),
    )(*[pltpu.with_memory_space_constraint(a, pltpu.HBM) for a in list(grads) + lands])
    return list(outs[:nw]), list(outs[nw:2 * nw]), (outs[2 * nw], outs[2 * nw + 1]), outs[2 * nw + 2]


def _grad_exchange_wait(name, grads, got, sems, after):
    nw = len(grads)

    def body(*refs):
        src = refs[:nw]
        land = refs[nw:2 * nw]
        send, recv = refs[2 * nw], refs[2 * nw + 1]
        x, y, c = _place()
        for w in range(nw):
            for k, (px, py, pc) in enumerate(_other_devices(x, y, c)):
                cp = _remote(src[w].at[4 * px + 2 * py + pc], land[w].at[k], send.at[7 * w + k], recv.at[7 * w + k], (px, py, pc))
                cp.wait_send()
                cp.wait_recv()

    outs = pl.pallas_call(
        body, name=name, in_specs=[_HBM] * (2 * nw) + [_SEM, _SEM, _ANY], out_specs=[_HBM] * (2 * nw),
        out_shape=[pltpu.HBM(a.shape, a.dtype) for a in list(grads) + list(got)],
        input_output_aliases={i: i for i in range(2 * nw)},
        compiler_params=pltpu.CompilerParams(has_side_effects=_EFFECT),
    )(*grads, *got, *sems, after)
    return list(outs[:nw]), list(outs[nw:])


def _swap_halves_start(finals):
    nw = len(finals)

    def body(*refs):
        buf = refs[nw:2 * nw]
        send, recv, token = refs[2 * nw:]
        x, y, c = _place()
        for w in range(nw):
            _remote(buf[w].at[c], buf[w].at[c], send.at[w], recv.at[w], (x, y, 1 - c)).start()
        token[...] = jnp.zeros_like(token)

    outs = pl.pallas_call(
        body, name="rs_swap_start", in_specs=[_HBM] * nw, out_specs=[_HBM] * nw + [_SEM, _SEM, _VMEM],
        out_shape=[pltpu.HBM(g.shape, g.dtype) for g in finals] + [pltpu.SemaphoreType.DMA((nw,))] * 2
        + [jax.ShapeDtypeStruct((8, 128), f32)],
        input_output_aliases={i: i for i in range(nw)},
        compiler_params=pltpu.CompilerParams(has_side_effects=_EFFECT),
    )(*[pltpu.with_memory_space_constraint(g, pltpu.HBM) for g in finals])
    return list(outs[:nw]), (outs[nw], outs[nw + 1]), outs[nw + 2]


def _swap_halves_wait(bufs, sems, after):
    nw = len(bufs)

    def body(*refs):
        buf = refs[:nw]
        send, recv = refs[nw], refs[nw + 1]
        x, y, c = _place()
        for w in range(nw):
            cp = _remote(buf[w].at[c], buf[w].at[1 - c], send.at[w], recv.at[w], (x, y, 1 - c))
            cp.wait_send()
            cp.wait_recv()

    return pl.pallas_call(
        body, name="rs_swap_wait", in_specs=[_HBM] * nw + [_SEM, _SEM, _ANY], out_specs=[_HBM] * nw,
        out_shape=[pltpu.HBM(g.shape, g.dtype) for g in bufs],
        input_output_aliases={i: i for i in range(nw)},
        compiler_params=pltpu.CompilerParams(has_side_effects=_EFFECT),
    )(*bufs, *sems, after)


def _half_slices(shape, h):
    rows, cols = shape
    if cols % 256 == 0:
        return (slice(None), slice(h * (cols // 2), (h + 1) * (cols // 2)))
    return (slice(h * (rows // 2), (h + 1) * (rows // 2)), slice(None))


def _allreduce_small(parts, after):
    n = len(parts)

    def body(*refs):
        src = refs[:n]
        refs = refs[n + 1:]
        out = refs[:n]
        sib = refs[n:2 * n]
        chip_sum = refs[2 * n:3 * n]
        slots = refs[3 * n:4 * n]
        pair_send, pair_recv, ici_send, ici_recv, swap_send, swap_recv = refs[4 * n:]
        x, y, c = _place()
        me_chip = 2 * x + y
        chips = _other_chips(x, y)
        pairs = [_remote(src[a], sib[a], pair_send.at[a], pair_recv.at[a], (x, y, 1 - c)) for a in range(n)]
        for rc in pairs:
            rc.start()
        for a in range(n):
            pairs[a].wait_recv()
            chip_sum[a][...] = src[a][...] + sib[a][...]
        for h in (0, 1):
            @pl.when(c == h)
            def _():
                sends = []
                for a in range(n):
                    idx = _half_slices(parts[a].shape, h)
                    for j, chip in enumerate(chips):
                        rc = _remote(chip_sum[a].at[idx], slots[a].at[me_chip].at[idx], ici_send.at[3 * a + j], ici_recv.at[3 * a + j], (*chip, h))
                        rc.start()
                        sends.append(rc)
                    slots[a][(me_chip,) + idx] = chip_sum[a][idx]
                for a in range(n):
                    idx = _half_slices(parts[a].shape, h)
                    for j, chip in enumerate(chips):
                        landed = slots[a].at[2 * chip[0] + chip[1]].at[idx]
                        _remote(landed, landed, ici_send.at[3 * a + j], ici_recv.at[3 * a + j], (x, y, c)).wait_recv()
                    total = slots[a][(0,) + idx]
                    for s in range(1, 4):
                        total = total + slots[a][(s,) + idx]
                    out[a][idx] = total
                    rc = _remote(out[a].at[idx], out[a].at[idx], swap_send.at[a], swap_recv.at[a], (x, y, 1 - h))
                    rc.start()
                    sends.append(rc)
                for a in range(n):
                    other = out[a].at[_half_slices(parts[a].shape, 1 - h)]
                    _remote(other, other, swap_send.at[a], swap_recv.at[a], (x, y, c)).wait_recv()
                for rc in sends:
                    rc.wait_send()
        for rc in pairs:
            rc.wait_send()

    return pl.pallas_call(
        body, name="allreduce_small", in_specs=[_VMEM] * n + [_ANY], out_specs=[_VMEM] * n,
        out_shape=[jax.ShapeDtypeStruct(p.shape, f32) for p in parts],
        scratch_shapes=[pltpu.VMEM(p.shape, f32) for p in parts] * 2 + [pltpu.VMEM((4,) + p.shape, f32) for p in parts]
        + [pltpu.SemaphoreType.DMA((n,)), pltpu.SemaphoreType.DMA((n,)), pltpu.SemaphoreType.DMA((3 * n,)),
           pltpu.SemaphoreType.DMA((3 * n,)), pltpu.SemaphoreType.DMA((n,)), pltpu.SemaphoreType.DMA((n,))],
        compiler_params=pltpu.CompilerParams(vmem_limit_bytes=_VMEM_LIMIT_BYTES),
    )(*parts, after)


def _local_step(x, mem, tgt, g_mix, g_xattn, g_mem, g_ffn, g_final, cb, lg, lb, pw, ps, fb, started, relay, weights, reduce,
                n_seq, seq, n_mem):
    t, d = x.shape
    f = fb.shape[1] // 2
    c = cb.shape[1]
    h1 = _rms_fwd("norm_mix", x, g_mix, after=started)
    relay(0, h1)
    w_in, cw, fw = weights(0, h1)
    u = _mm_nn("proj_in", h1, w_in, _ACT, w_in.shape[1])
    y, hc = _mix_fwd(u, cw, cb, lg, lb, pw, ps, seq)
    relay(1, y)
    w_out, w_q, w_kv, w_o = weights(1, y)
    x1, h2 = _proj_residual_norm("proj_out", y, w_out, x, g_xattn)
    q = _mm_nn("proj_q", h2, w_q, _ACT, d)
    mem_n = _rms_fwd("norm_mem", mem, g_mem)
    kv = _mm_nn("proj_kv", mem_n, w_kv, _ACT, 2 * d)
    o = _attn_fwd(q, kv, n_seq, seq, n_mem)
    x2, h3 = _proj_residual_norm("proj_o", o, w_o, x1, g_ffn, after=relay(2, o))
    w_up, w_down = weights(2, h3)
    up = _mm_nn("proj_up", h3, w_up, _ACT, f, split_out=True)
    a, uc = _ffn_gate_fwd(up, fw, fb, seq)
    dx3, dx3b, dg_final, loss = _proj_loss_bwd("proj_down", a, w_down, x2, g_final, tgt)
    da = _mm_nt("d_act", dx3b, w_down, _ACT)
    gw_down = _mm_tn_rows("dw_down", a, dx3b, f // 2, d // 2)
    dup, sums_g, sums_v = _ffn_gate_bwd(up, uc, da, fw, seq)
    gw_up = _mm_tn_pieces("dw_up", h3, dup, f // 2)
    token = reduce(0, [gw_down.reshape(8, -1, d), gw_up])
    dx2, dx2b, dg_ffn = _dproj_rms_bwd("d_h3", dup, w_up, x2, g_ffn, dx3, after=token)
    do = _mm_nt("d_o", dx2b, w_o, _ACT)
    gw_o = _mm_tn_rows("dw_o", o, dx2b, d, d // 2)
    dq, dkv = _attn_bwd(q, kv, do, n_seq, seq, n_mem)
    gw_q = _mm_tn_rows("dw_q", h2, dq, d, d // 2)
    gw_kv = _mm_tn_pieces("dw_kv", mem_n, dkv, d // 2)
    dmem_n = _mm_nt("d_mem_n", dkv, w_kv, f32)
    dg_mem = _rms_gain_grad("norm_mem_bwd", mem, dmem_n)
    dx1, dx1b, dg_xattn = _dproj_rms_bwd("d_h2", dq, w_q, x1, g_xattn, dx2)
    dy = _mm_nt("d_y", dx1b, w_out, _ACT)
    gw_out = _mm_tn_rows("dw_out", y, dx1b, d, d // 2)
    token = reduce(1, [gw_o.reshape(8, -1, d), gw_q.reshape(8, -1, d), gw_kv, gw_out.reshape(8, -1, d)])
    dhc, sums_norm = _mix_bwd_norm(hc, dy, lg, lb, token)
    du, d_cw, d_ps, d_pw = _mix_bwd_taps(u, dhc, dy, cw, pw, ps, seq)
    gw_in = _mm_tn_pieces("dw_in", h1, du, c * 3 // 4)
    token = reduce(2, [gw_in])
    grad_x, dg_mix = _dproj_rms_bwd("d_h1", du, w_in, x, g_mix, dx1, storage_copy=False, after=token)
    zero_row = jnp.zeros((1, d), f32)
    gains = jnp.concatenate([dg_mix, dg_xattn, dg_mem, dg_ffn, dg_final, jnp.pad(loss, ((0, 0), (0, d - 1))), zero_row, zero_row], axis=0)
    conv_rows = jnp.concatenate([sums_norm[2:3], sums_norm[0:1], sums_norm[1:2], d_ps[0:1], jnp.zeros((4, c), f32)], axis=0)
    ffn_rows = jnp.concatenate([sums_g, sums_v], axis=1)
    small = [gains, conv_rows, d_pw.reshape(-1, d_pw.shape[-1]), ffn_rows, d_cw]
    return grad_x, small


def kernel(x, mem, norm_mix_g, w_in, conv_dw_w, conv_dw_b, conv_ln_g, conv_ln_b, pool_w, pool_scale, w_out, norm_xattn_g, norm_mem_g, w_q, w_kv, w_o, norm_ffn_g, w_up, ffn_dw_w, ffn_dw_b, w_down, norm_final_g, loss_target, m_norm_mix_g, m_w_in, m_conv_dw_w, m_conv_dw_b, m_conv_ln_g, m_conv_ln_b, m_pool_w, m_pool_scale, m_w_out, m_norm_xattn_g, m_norm_mem_g, m_w_q, m_w_kv, m_w_o, m_norm_ffn_g, m_w_up, m_ffn_dw_w, m_ffn_dw_b, m_w_down, m_norm_final_g, v_norm_mix_g, v_w_in, v_conv_dw_w, v_conv_dw_b, v_conv_ln_g, v_conv_ln_b, v_pool_w, v_pool_scale, v_w_out, v_norm_xattn_g, v_norm_mem_g, v_w_q, v_w_kv, v_w_o, v_norm_ffn_g, v_w_up, v_ffn_dw_w, v_ffn_dw_b, v_w_down, v_norm_final_g):
    n_seq, seq, d = x.shape
    n_mem = mem.shape[1]
    chip = 2 * lax.axis_index("x") + lax.axis_index("y")

    place = jnp.stack([chip, lax.axis_index("c")]).astype(jnp.int32)

    col_w = [w_in, w_kv, w_up]
    row_w = [w_out, w_q, w_o, w_down]
    kw = conv_dw_w.shape[1]

    def padded_in_place(shard, rows):
        full = jnp.zeros((rows, 4 * shard.shape[1]), shard.dtype)
        return lax.dynamic_update_slice(full, shard, (0, chip * shard.shape[1]))

    first = list(_place_shards("place_w_in", place, [w_in[0]], [True]))
    first += [padded_in_place(conv_dw_w[0], _HALO), padded_in_place(ffn_dw_w[0], 8)]
    first, first_sems, token = _allgather_start("allgather_start_0", first, [True] * 3, [False, True, True], [[0, 1, 2]])
    rest = [w_kv, w_up, w_out, w_q, w_o, w_down]
    rest_flags = [True, True, False, False, False, False]
    rest = list(_place_shards("place_rest", place, [w[0] for w in rest], rest_flags, after=token))
    rest, rest_sems, all_started = _allgather_start("allgather_start_1", rest, rest_flags, [False] * 6, [[2, 3, 0, 4], [1, 5]])
    started = [(first, [True] * 3, [False, True, True], first_sems[0]),
               ([rest[i] for i in (2, 3, 0, 4)], [False, False, True, False], [False] * 4, rest_sems[0]),
               ([rest[i] for i in (1, 5)], [True, False], [False] * 2, rest_sems[1])]
    relayed = {}

    def relay(g, after):
        group_bufs, flags, wholes, group_sems = started[g]
        group_bufs, sibling_sems, relay_token = _allgather_relay("allgather_relay_%d" % g, group_bufs, flags, wholes, group_sems, after)
        relayed[g] = (group_bufs, sibling_sems)
        return relay_token

    def weights(g, after):
        group_bufs, sibling_sems = relayed[g]
        return _allgather_wait("allgather_wait_%d" % g, group_bufs, started[g][1], started[g][2], sibling_sems, after)

    names = ["w_in", "w_kv", "w_up", "w_out", "w_q", "w_o", "w_down"]
    reduce_groups = [["w_down", "w_up"], ["w_o", "w_q", "w_kv", "w_out"], ["w_in"]]
    in_flight = {}

    def reduce(g, grads):
        grads, lands, rs_sems, token = _grad_exchange_start("rs_start_%d" % g, grads)
        in_flight[g] = (grads, lands, rs_sems)
        return token

    grad_x, small = _local_step(
        x.reshape(n_seq * seq, d), mem.reshape(n_seq * n_mem, d), loss_target.reshape(n_seq * seq, d),
        norm_mix_g, norm_xattn_g, norm_mem_g, norm_ffn_g, norm_final_g.reshape(1, d),
        conv_dw_b, conv_ln_g, conv_ln_b, pool_w[0], pool_scale, ffn_dw_b, all_started, relay, weights, reduce,
        n_seq, seq, n_mem)

    landed = {}
    for g, members in enumerate(reduce_groups):
        grads, lands, rs_sems = in_flight[g]
        grads, lands = _grad_exchange_wait("rs_wait_%d" % g, grads, lands, rs_sems, grad_x)
        landed.update(zip(members, zip(grads, lands)))
    finals = _sum_partials("rs_sum", place, [landed[n][0] for n in names], [landed[n][1] for n in names])
    finals, swap_sems, token = _swap_halves_start(finals)

    gains, conv_rows, d_pw, ffn_rows, d_cw = _allreduce_small(small, token)
    loss = gains[5, 0]
    shard_grads = _swap_halves_wait(finals, swap_sems, gains)

    outs = {}
    big_w = dict(zip(names, col_w + row_w))
    big_m = dict(w_in=m_w_in, w_kv=m_w_kv, w_up=m_w_up, w_out=m_w_out, w_q=m_w_q, w_o=m_w_o, w_down=m_w_down)
    big_v = dict(w_in=v_w_in, w_kv=v_w_kv, w_up=v_w_up, w_out=v_w_out, w_q=v_w_q, w_o=v_w_o, w_down=v_w_down)
    big_quads = [(big_w[n], g.reshape(big_w[n].shape[1:]), big_m[n], big_v[n]) for n, g in zip(names, shard_grads)]
    outs.update(zip(names, _adamw_shards(big_quads)))

    f2 = ffn_dw_b.shape[1]
    cs_c = conv_dw_w.shape[2]
    cs_f = ffn_dw_w.shape[2]
    g_cw = lax.dynamic_slice(d_cw, (0, chip * cs_c), (kw, cs_c)).reshape(conv_dw_w.shape)
    g_fw = lax.dynamic_slice(ffn_rows, (1, chip * cs_f), (ffn_dw_w.shape[1], cs_f)).reshape(ffn_dw_w.shape)
    small_params = [
        ("norm_mix_g", norm_mix_g, gains[0:1], m_norm_mix_g, v_norm_mix_g),
        ("conv_dw_w", conv_dw_w, g_cw, m_conv_dw_w, v_conv_dw_w),
        ("conv_dw_b", conv_dw_b, conv_rows[0:1], m_conv_dw_b, v_conv_dw_b),
        ("conv_ln_g", conv_ln_g, conv_rows[1:2], m_conv_ln_g, v_conv_ln_g),
        ("conv_ln_b", conv_ln_b, conv_rows[2:3], m_conv_ln_b, v_conv_ln_b),
        ("pool_w", pool_w, d_pw.reshape(pool_w.shape), m_pool_w, v_pool_w),
        ("pool_scale", pool_scale, conv_rows[3:4], m_pool_scale, v_pool_scale),
        ("norm_xattn_g", norm_xattn_g, gains[1:2], m_norm_xattn_g, v_norm_xattn_g),
        ("norm_mem_g", norm_mem_g, gains[2:3], m_norm_mem_g, v_norm_mem_g),
        ("norm_ffn_g", norm_ffn_g, gains[3:4], m_norm_ffn_g, v_norm_ffn_g),
        ("ffn_dw_w", ffn_dw_w, g_fw, m_ffn_dw_w, v_ffn_dw_w),
        ("ffn_dw_b", ffn_dw_b, ffn_rows[0:1, :f2], m_ffn_dw_b, v_ffn_dw_b),
        ("norm_final_g", norm_final_g.reshape(1, d), gains[4:5], m_norm_final_g.reshape(1, d), v_norm_final_g.reshape(1, d)),
    ]
    quads = []
    for _, w, g, m, v in small_params:
        shape2 = (-1, w.shape[-1])
        quads.append((w.reshape(shape2), g.reshape(shape2), m.reshape(shape2), v.reshape(shape2)))
    for (n, w, g, _, _), (delta, new_m, new_v) in zip(small_params, _adamw_small(quads)):
        shape = norm_final_g.shape if n == "norm_final_g" else w.shape
        outs[n] = (g.reshape(shape), delta.reshape(shape), new_m.reshape(shape), new_v.reshape(shape))

    order = ["norm_mix_g", "w_in", "conv_dw_w", "conv_dw_b", "conv_ln_g", "conv_ln_b", "pool_w", "pool_scale", "w_out",
             "norm_xattn_g", "norm_mem_g", "w_q", "w_kv", "w_o", "norm_ffn_g", "w_up", "ffn_dw_w", "ffn_dw_b", "w_down",
             "norm_final_g"]
    return (loss, grad_x.reshape(x.shape), *[outs[n][0] for n in order], *[outs[n][1] for n in order],
            *[outs[n][2] for n in order], *[outs[n][3] for n in order])
```

```python
import jax
import jax.numpy as jnp
from jax import lax
from jax.experimental import pallas as pl
from jax.experimental.pallas import tpu as pltpu

f32 = jnp.float32
_ACT = jnp.bfloat16

EPS = 1e-6
POOL_WINDOWS = (2, 4, 8, 16)
XATTN_HEADS = 4
ADAM_LR = 0.001
ADAM_B1 = 0.9
ADAM_B2 = 0.999
ADAM_EPS = 1e-08
ADAM_WD = 0.01
ADAM_STEP = 10

_VMEM_LIMIT_BYTES = 56 * 1024 * 1024
_MESH = pl.DeviceIdType.MESH
_ANY = pl.BlockSpec(memory_space=pl.ANY)
_VMEM = pl.BlockSpec(memory_space=pltpu.VMEM)
_HBM = pl.BlockSpec(memory_space=pltpu.HBM)
_SEM = pl.BlockSpec(memory_space=pltpu.SEMAPHORE)
_EFFECT = pltpu.SideEffectType.DATAFLOW_SIDE_EFFECTING

_NN = (((1,), (0,)), ((), ()))
_NT = (((1,), (1,)), ((), ()))
_TN = (((0,), (0,)), ((), ()))


def _params(n_grid):
    return pltpu.CompilerParams(dimension_semantics=("arbitrary",) * n_grid, vmem_limit_bytes=_VMEM_LIMIT_BYTES)


def _sigmoid(v):
    return 1.0 / (1.0 + jnp.exp(-v))


def _dot(a, b, dims):
    return lax.dot_general(a, b, dims, preferred_element_type=f32)


def _mm(name, a, b, *, dims, grid, a_spec, b_spec, o_spec, out_shape):
    def body(a_ref, b_ref, o_ref):
        o_ref[...] = _dot(a_ref[...], b_ref[...], dims).astype(o_ref.dtype)

    return pl.pallas_call(
        body, name=name, grid=grid, in_specs=[a_spec, b_spec], out_specs=o_spec, out_shape=out_shape,
        compiler_params=_params(len(grid)),
    )(a, b)


_NARROW = 2816


def _row_tile(m, width=_NARROW + 1):
    return min(1024 if width <= _NARROW else 512, m)


def _mm_nn(name, a, b, out_dtype, tn, split_out=False):
    m, k = a.shape
    n = b.shape[1]
    tm = _row_tile(m, max(k, tn))
    if split_out:
        out_shape = jax.ShapeDtypeStruct((n // tn, m, tn), out_dtype)
        o_spec = pl.BlockSpec((None, tm, tn), lambda j, i: (j, i, 0))
    else:
        out_shape = jax.ShapeDtypeStruct((m, n), out_dtype)
        o_spec = pl.BlockSpec((tm, tn), lambda j, i: (i, j))
    return _mm(
        name, a, b, dims=_NN, grid=(n // tn, m // tm),
        a_spec=pl.BlockSpec((tm, k), lambda j, i: (i, 0)), b_spec=pl.BlockSpec((k, tn), lambda j, i: (0, j)),
        o_spec=o_spec, out_shape=out_shape,
    )


def _mm_nt(name, a, b, out_dtype):
    n, kc = b.shape
    m = a.shape[0]
    tm = _row_tile(m, max(n, kc))
    return _mm(
        name, a, b, dims=_NT, grid=(m // tm,),
        a_spec=pl.BlockSpec((tm, kc), lambda i: (i, 0)),
        b_spec=pl.BlockSpec((n, kc), lambda i: (0, 0), pipeline_mode=pl.Buffered(1)),
        o_spec=pl.BlockSpec((tm, n), lambda i: (i, 0)),
        out_shape=jax.ShapeDtypeStruct((m, n), out_dtype),
    )


def _mm_tn_rows(name, a, b, tka, tn):
    m, ka = a.shape
    nb = b.shape[1]
    return _mm(
        name, a, b, dims=_TN, grid=(ka // tka, nb // tn),
        a_spec=pl.BlockSpec((m, tka), lambda i, j: (0, i)), b_spec=pl.BlockSpec((m, tn), lambda i, j: (0, j)),
        o_spec=pl.BlockSpec((tka, tn), lambda i, j: (i, j)),
        out_shape=jax.ShapeDtypeStruct((ka, nb), _ACT),
    )


def _mm_tn_pieces(name, a, b, cs):
    m, ka = a.shape
    if b.ndim == 3:
        b_spec = pl.BlockSpec((None, m, cs), lambda i, j: (j // 2, 0, j % 2))
    else:
        b_spec = pl.BlockSpec((m, cs), lambda i, j: (0, j))
    return _mm(
        name, a, b, dims=_TN, grid=(2, 4),
        a_spec=pl.BlockSpec((m, ka // 2), lambda i, j: (0, i)), b_spec=b_spec,
        o_spec=pl.BlockSpec((None, ka // 2, cs), lambda i, j: (2 * j + i, 0, 0)),
        out_shape=jax.ShapeDtypeStruct((8, ka // 2, cs), _ACT),
    )


def _after(after):
    return ([], []) if after is None else ([after], [_ANY])


def _rms_fwd(name, x, g, after=None):
    t, d = x.shape
    tm = _row_tile(t, d)
    more, more_specs = _after(after)

    def body(x_ref, g_ref, *refs):
        h_ref = refs[-1]
        xv = x_ref[...]
        r = lax.rsqrt(jnp.mean(xv * xv, axis=-1, keepdims=True) + EPS)
        h_ref[...] = (xv * r * g_ref[...]).astype(h_ref.dtype)

    return pl.pallas_call(
        body, name=name, grid=(t // tm,),
        in_specs=[pl.BlockSpec((tm, d), lambda i: (i, 0)), pl.BlockSpec((1, d), lambda i: (0, 0))] + more_specs,
        out_specs=pl.BlockSpec((tm, d), lambda i: (i, 0)), out_shape=jax.ShapeDtypeStruct((t, d), _ACT),
        compiler_params=_params(1),
    )(x, g, *more)


_EPILOGUE_ROWS = 32


def _fused_rows(name, a, b, product, a_spec, tm, extras, extra_specs, out_shape, out_specs, epilogue, n_sums=0):
    ne = len(extras)
    d = out_shape[0].shape[1]

    def body(a_ref, b_ref, *refs):
        ins, outs, p_ref = refs[:ne], refs[ne:-1], refs[-1]

        @pl.when(pl.program_id(0) == 0)
        def _():
            for o_ref in outs[len(outs) - n_sums:]:
                o_ref[...] = jnp.zeros_like(o_ref)

        p_ref[...] = product(a_ref, b_ref)

        def rows_of(ref, r0):
            return ref.at[pl.ds(r0, _EPILOGUE_ROWS), :] if ref.shape[0] == tm else ref

        def chunk(ci, carry):
            r0 = pl.multiple_of(ci * _EPILOGUE_ROWS, _EPILOGUE_ROWS)
            epilogue(p_ref[pl.ds(r0, _EPILOGUE_ROWS), :], [rows_of(r, r0) for r in ins], [rows_of(r, r0) for r in outs])
            return carry

        lax.fori_loop(0, tm // _EPILOGUE_ROWS, chunk, 0)

    m = extras[0].shape[0]
    return pl.pallas_call(
        body, name=name, grid=(m // tm,),
        in_specs=[a_spec, pl.BlockSpec(b.shape, lambda i: (0, 0), pipeline_mode=pl.Buffered(1)), *extra_specs],
        out_specs=out_specs, out_shape=out_shape, scratch_shapes=[pltpu.VMEM((tm, d), f32)], compiler_params=_params(1),
    )(a, b, *extras)


def _proj_residual_norm(name, a, b, res, g, after=None):
    m, k = a.shape
    d = b.shape[1]
    tm = _row_tile(m, max(k, d))

    def epilogue(p, ins, outs):
        xv = p + ins[0][...]
        outs[0][...] = xv
        r = lax.rsqrt(jnp.mean(xv * xv, axis=-1, keepdims=True) + EPS)
        outs[1][...] = (xv * r * ins[1][...]).astype(outs[1].dtype)

    row = pl.BlockSpec((tm, d), lambda i: (i, 0))
    return _fused_rows(
        name, a, b, lambda a_ref, b_ref: _dot(a_ref[...], b_ref[...], _NN), pl.BlockSpec((tm, k), lambda i: (i, 0)), tm,
        [res, g] + _after(after)[0], [row, pl.BlockSpec((1, d), lambda i: (0, 0))] + _after(after)[1],
        [jax.ShapeDtypeStruct((m, d), f32), jax.ShapeDtypeStruct((m, d), _ACT)], [row, row], epilogue)


def _dproj_rms_bwd(name, a, b, x, g, dres, storage_copy=True, after=None):
    m, d = x.shape
    if a.ndim == 3:
        nh, _, kh = a.shape
        tm = _row_tile(m, nh * kh)
        a_spec = pl.BlockSpec((nh, tm, kh), lambda i: (0, i, 0))

        def product(a_ref, b_ref):
            p = _dot(a_ref[0], b_ref[:, 0:kh], _NT)
            for h in range(1, nh):
                p = p + _dot(a_ref[h], b_ref[:, h * kh:(h + 1) * kh], _NT)
            return p
    else:
        tm = _row_tile(m, max(a.shape[1], d))
        a_spec = pl.BlockSpec((tm, a.shape[1]), lambda i: (i, 0))

        def product(a_ref, b_ref):
            return _dot(a_ref[...], b_ref[...], _NT)

    def epilogue(dhv, ins, outs):
        x_ref, g_ref, dres_ref = ins[:3]
        dg_ref = outs[-1]
        xv = x_ref[...]
        r = lax.rsqrt(jnp.mean(xv * xv, axis=-1, keepdims=True) + EPS)
        xn = xv * r
        dxn = dhv * g_ref[...]
        dx = r * (dxn - xn * jnp.mean(dxn * xn, axis=-1, keepdims=True)) + dres_ref[...]
        outs[0][...] = dx
        if storage_copy:
            outs[1][...] = dx.astype(outs[1].dtype)
        dg_ref[...] += jnp.sum(dhv * xn, axis=0, keepdims=True)

    row = pl.BlockSpec((tm, d), lambda i: (i, 0))
    vec = pl.BlockSpec((1, d), lambda i: (0, 0))
    copies = [jax.ShapeDtypeStruct((m, d), _ACT)] if storage_copy else []
    return _fused_rows(
        name, a, b, product, a_spec, tm, [x, g, dres] + _after(after)[0], [row, vec, row] + _after(after)[1],
        [jax.ShapeDtypeStruct((m, d), f32)] + copies + [jax.ShapeDtypeStruct((1, d), f32)],
        [row] * (1 + len(copies)) + [vec], epilogue, n_sums=1)


def _proj_loss_bwd(name, a, b, res, g, tgt):
    m, k = a.shape
    d = b.shape[1]
    tm = _row_tile(m, max(k, d))

    def epilogue(p, ins, outs):
        res_ref, g_ref, t_ref = ins
        dx_ref, dxb_ref, dg_ref, loss_ref = outs
        xv = p + res_ref[...]
        gv = g_ref[...]
        r = lax.rsqrt(jnp.mean(xv * xv, axis=-1, keepdims=True) + EPS)
        xn = xv * r
        err = xn * gv - t_ref[...]
        loss_ref[...] += 0.5 * jnp.sum(jnp.mean(err * err, axis=-1, keepdims=True), axis=0, keepdims=True)
        dout = err * (1.0 / d)
        dxn = dout * gv
        dx = r * (dxn - xn * jnp.mean(dxn * xn, axis=-1, keepdims=True))
        dx_ref[...] = dx
        dxb_ref[...] = dx.astype(dxb_ref.dtype)
        dg_ref[...] += jnp.sum(dout * xn, axis=0, keepdims=True)

    row = pl.BlockSpec((tm, d), lambda i: (i, 0))
    vec = pl.BlockSpec((1, d), lambda i: (0, 0))
    return _fused_rows(
        name, a, b, lambda a_ref, b_ref: _dot(a_ref[...], b_ref[...], _NN), pl.BlockSpec((tm, k), lambda i: (i, 0)), tm,
        [res, g, tgt], [row, vec, row],
        [jax.ShapeDtypeStruct((m, d), f32), jax.ShapeDtypeStruct((m, d), _ACT), jax.ShapeDtypeStruct((1, d), f32),
         jax.ShapeDtypeStruct((1, 1), f32)],
        [row, row, vec, pl.BlockSpec((1, 1), lambda i: (0, 0))], epilogue, n_sums=2)


def _rms_gain_grad(name, x, dh):
    t, d = x.shape
    tm = _row_tile(t)

    def body(x_ref, dh_ref, dg_ref):
        @pl.when(pl.program_id(0) == 0)
        def _():
            dg_ref[...] = jnp.zeros_like(dg_ref)

        xv = x_ref[...]
        r = lax.rsqrt(jnp.mean(xv * xv, axis=-1, keepdims=True) + EPS)
        dg_ref[...] += jnp.sum(dh_ref[...] * (xv * r), axis=0, keepdims=True)

    row = pl.BlockSpec((tm, d), lambda i: (i, 0))
    return pl.pallas_call(
        body, name=name, grid=(t // tm,), in_specs=[row, row], out_specs=pl.BlockSpec((1, d), lambda i: (0, 0)),
        out_shape=jax.ShapeDtypeStruct((1, d), f32), compiler_params=_params(1),
    )(x, dh)


_CONV_ROWS = 512
_CHUNK = 64
_HALO = 32


def _pool_counts(pos, w):
    return jnp.minimum(pos + 1.0, float(w))


def _rows_from(win, start, rows):
    if start % 8 == 0:
        return win[start:start + rows, :]
    n = win.shape[0]
    return pltpu.roll(win, n - start % 8, axis=0)[start - start % 8:start - start % 8 + rows, :]


def _tap_rows(buf, starts, rows):
    for residue in range(8):
        group = [(k, s) for k, s in starts.items() if s % 8 == residue]
        if group:
            lo = min(s for _, s in group) - residue
            hi = max(s for _, s in group) - residue + rows + (8 if residue else 0)
            win = buf[lo:hi, :]
            if residue:
                win = pltpu.roll(win, hi - lo - residue, axis=0)
            for k, s in group:
                yield k, win[s - residue - lo:s - residue - lo + rows, :]


def _mix_fwd(u, cw, cb, lg, lb, pw, ps, seq):
    t, c3 = u.shape
    c = c3 // 3
    kw = 31
    tm = min(_CONV_ROWS, seq)
    tps = seq // tm
    gd = c // len(POOL_WINDOWS)

    def body(u_ref, uh_ref, cw_ref, cb_ref, lg_ref, lb_ref, pw_ref, ps_ref, y_ref, hc_ref, hgbuf, pbuf):
        i = pl.program_id(0)
        keep = jnp.where(i % tps == 0, 0.0, 1.0)
        um = u_ref[...].astype(f32)
        uh = uh_ref[...].astype(f32) * keep
        hgbuf[0:_HALO, :] = uh[:, 0:c] * _sigmoid(uh[:, c:2 * c])
        hgbuf[_HALO:_HALO + tm, :] = um[:, 0:c] * _sigmoid(um[:, c:2 * c])
        pbuf[0:_HALO, :] = uh[:, 2 * c:]
        pbuf[_HALO:_HALO + tm, :] = um[:, 2 * c:]
        for r0 in range(0, tm, _CHUNK):
            acc = jnp.broadcast_to(cb_ref[...], (_CHUNK, c))
            for k, rows in _tap_rows(hgbuf, {k: r0 + _HALO - (kw - 1) + k for k in range(kw)}, _CHUNK):
                acc = acc + cw_ref[k:k + 1, :] * rows
            hc_ref[r0:r0 + _CHUNK, :] = acc
            mu = jnp.mean(acc, axis=-1, keepdims=True)
            xc = acc - mu
            var = jnp.mean(xc * xc, axis=-1, keepdims=True)
            hl = xc * lax.rsqrt(var + EPS) * lg_ref[...] + lb_ref[...]
            y_ref[r0:r0 + _CHUNK, 0:c] = (hl * _sigmoid(hl)).astype(y_ref.dtype)
        pos = ((i % tps) * tm).astype(f32) + lax.broadcasted_iota(jnp.int32, (tm, 1), 0).astype(f32)
        for gi, w in enumerate(POOL_WINDOWS):
            sl = slice(gi * gd, (gi + 1) * gd)
            v = pbuf[_HALO:_HALO + tm, sl]
            s = v
            for j in range(1, w):
                s = s + pbuf[_HALO - j:_HALO - j + tm, sl]
            pooled = s / _pool_counts(pos, w) - v
            mixed = _dot(pooled.astype(_ACT), pw_ref[gi].astype(_ACT), _NN)
            y_ref[:, c + gi * gd:c + (gi + 1) * gd] = (mixed * ps_ref[:, sl]).astype(y_ref.dtype)

    hb = tm // _HALO
    full = lambda shape: pl.BlockSpec(shape, lambda i: (0,) * len(shape))
    return pl.pallas_call(
        body, name="mix_fwd", grid=(t // tm,),
        in_specs=[pl.BlockSpec((tm, c3), lambda i: (i, 0)),
                  pl.BlockSpec((_HALO, c3), lambda i: (jnp.maximum(i * hb - 1, 0), 0)),
                  full((_HALO, c)), full((1, c)), full((1, c)), full((1, c)), full((len(POOL_WINDOWS), gd, gd)), full((1, c))],
        out_specs=[pl.BlockSpec((tm, 2 * c), lambda i: (i, 0)), pl.BlockSpec((tm, c), lambda i: (i, 0))],
        out_shape=[jax.ShapeDtypeStruct((t, 2 * c), _ACT), jax.ShapeDtypeStruct((t, c), f32)],
        scratch_shapes=[pltpu.VMEM((_HALO + tm, c), f32), pltpu.VMEM((_HALO + tm, c), f32)],
        compiler_params=_params(1),
    )(u, u, cw, cb, lg, lb, pw, ps)


def _mix_bwd_norm(hc, dy, lg, lb, after):
    t, c = hc.shape
    tm = _row_tile(t, c)

    def body(hc_ref, dy_ref, lg_ref, lb_ref, after_ref, dhc_ref, sums_ref):
        @pl.when(pl.program_id(0) == 0)
        def _():
            sums_ref[...] = jnp.zeros_like(sums_ref)

        hcv = hc_ref[...]
        mu = jnp.mean(hcv, axis=-1, keepdims=True)
        xc = hcv - mu
        rstd = lax.rsqrt(jnp.mean(xc * xc, axis=-1, keepdims=True) + EPS)
        n = xc * rstd
        hl = n * lg_ref[...] + lb_ref[...]
        sg = _sigmoid(hl)
        dhl = dy_ref[...].astype(f32) * (sg * (1.0 + hl * (1.0 - sg)))
        dn = dhl * lg_ref[...]
        dhc = rstd * (dn - jnp.mean(dn, axis=-1, keepdims=True) - n * jnp.mean(dn * n, axis=-1, keepdims=True))
        dhc_ref[...] = dhc
        sums_ref[0:1, :] += jnp.sum(dhl * n, axis=0, keepdims=True)
        sums_ref[1:2, :] += jnp.sum(dhl, axis=0, keepdims=True)
        sums_ref[2:3, :] += jnp.sum(dhc, axis=0, keepdims=True)

    row = pl.BlockSpec((tm, c), lambda i: (i, 0))
    vec = pl.BlockSpec((1, c), lambda i: (0, 0))
    return pl.pallas_call(
        body, name="mix_bwd_norm", grid=(t // tm,), in_specs=[row, row, vec, vec, _ANY],
        out_specs=[row, pl.BlockSpec((8, c), lambda i: (0, 0))],
        out_shape=[jax.ShapeDtypeStruct((t, c), f32), jax.ShapeDtypeStruct((8, c), f32)],
        compiler_params=_params(1),
    )(hc, dy, lg, lb, after)


def _mix_bwd_taps(u, dhc, dy, cw, pw, ps, seq):
    t, c3 = u.shape
    c = c3 // 3
    kw = 31
    tm = min(_CONV_ROWS, seq)
    tps = seq // tm
    ng = len(POOL_WINDOWS)
    gd = c // ng
    nh = 16

    def body(u_ref, uh_ref, dhc_ref, dhcn_ref, dy_ref, dyn_ref, cw_ref, pw_ref, ps_ref,
             du_ref, dcw_ref, dps_ref, dpw_ref, hgbuf, dcbuf, pbuf, dpbuf):
        i = pl.program_id(0)
        keep_prev = jnp.where(i % tps == 0, 0.0, 1.0)
        keep_next = jnp.where(i % tps == tps - 1, 0.0, 1.0)

        @pl.when(i == 0)
        def _():
            dcw_ref[...] = jnp.zeros_like(dcw_ref)
            dps_ref[...] = jnp.zeros_like(dps_ref)
            dpw_ref[...] = jnp.zeros_like(dpw_ref)

        uh = uh_ref[...].astype(f32) * keep_prev
        hgbuf[0:_HALO, :] = uh[:, 0:c] * _sigmoid(uh[:, c:2 * c])
        pbuf[0:_HALO, :] = uh[:, 2 * c:]
        um = u_ref[...].astype(f32)
        hgbuf[_HALO:_HALO + tm, :] = um[:, 0:c] * _sigmoid(um[:, c:2 * c])
        pbuf[_HALO:_HALO + tm, :] = um[:, 2 * c:]
        dcbuf[0:tm, :] = dhc_ref[...]
        dcbuf[tm:tm + _HALO, :] = dhcn_ref[...] * keep_next
        tap_sums = [None] * kw
        for r0 in range(0, tm, _CHUNK):
            dh = dcbuf[r0:r0 + _CHUNK, :]
            acc = jnp.zeros((_CHUNK, c), f32)
            for k, rows in _tap_rows(hgbuf, {k: r0 + _HALO - (kw - 1) + k for k in range(kw)}, _CHUNK):
                part = (dh * rows).reshape(_CHUNK // 8, 8, c).sum(axis=0)
                tap_sums[k] = part if tap_sums[k] is None else tap_sums[k] + part
            for k, rows in _tap_rows(dcbuf, {k: r0 + (kw - 1) - k for k in range(kw)}, _CHUNK):
                acc = acc + cw_ref[k:k + 1, :] * rows
            val = u_ref[r0:r0 + _CHUNK, 0:c].astype(f32)
            sg = _sigmoid(u_ref[r0:r0 + _CHUNK, c:2 * c].astype(f32))
            du_ref[r0:r0 + _CHUNK, 0:c] = (acc * sg).astype(du_ref.dtype)
            du_ref[r0:r0 + _CHUNK, c:2 * c] = (acc * val * sg * (1.0 - sg)).astype(du_ref.dtype)
        for k in range(kw):
            dcw_ref[k:k + 1, :] += jnp.sum(tap_sums[k], axis=0, keepdims=True)
        base = ((i % tps) * tm).astype(f32)
        pos = base + lax.broadcasted_iota(jnp.int32, (tm, 1), 0).astype(f32)
        pos_next = base + float(tm) + lax.broadcasted_iota(jnp.int32, (nh, 1), 0).astype(f32)
        for gi, w in enumerate(POOL_WINDOWS):
            sl = slice(gi * gd, (gi + 1) * gd)
            v = pbuf[_HALO:_HALO + tm, sl]
            s = v
            for j in range(1, w):
                s = s + pbuf[_HALO - j:_HALO - j + tm, sl]
            cnt = _pool_counts(pos, w)
            pooled = (s / cnt - v).astype(_ACT)
            pwg = pw_ref[gi].astype(_ACT)
            mixed = _dot(pooled, pwg, _NN)
            dyp = dy_ref[:, sl].astype(f32)
            dps_ref[0:1, sl] += jnp.sum(dyp * mixed, axis=0, keepdims=True)
            dmix = (dyp * ps_ref[:, sl]).astype(_ACT)
            dpw_ref[gi] += _dot(pooled, dmix, _TN)
            dmix_next = (dyn_ref[:, sl].astype(f32) * ps_ref[:, sl] * keep_next).astype(_ACT)
            dpool = _dot(dmix, pwg, _NT)
            dpbuf[0:tm, sl] = dpool / cnt
            dpbuf[tm:tm + nh, sl] = _dot(dmix_next, pwg, _NT) / _pool_counts(pos_next, w)
            acc = -dpool
            for j in range(w):
                acc = acc + dpbuf[j:j + tm, sl]
            du_ref[:, 2 * c + gi * gd:2 * c + (gi + 1) * gd] = acc.astype(du_ref.dtype)

    hb = tm // _HALO
    n_halo = t // _HALO
    n_nh = t // nh
    full = lambda shape: pl.BlockSpec(shape, lambda i: (0,) * len(shape))
    return pl.pallas_call(
        body, name="mix_bwd_taps", grid=(t // tm,),
        in_specs=[pl.BlockSpec((tm, c3), lambda i: (i, 0)),
                  pl.BlockSpec((_HALO, c3), lambda i: (jnp.maximum(i * hb - 1, 0), 0)),
                  pl.BlockSpec((tm, c), lambda i: (i, 0)),
                  pl.BlockSpec((_HALO, c), lambda i: (jnp.minimum((i + 1) * hb, n_halo - 1), 0)),
                  pl.BlockSpec((tm, c), lambda i: (i, 1)),
                  pl.BlockSpec((nh, c), lambda i: (jnp.minimum((i + 1) * (tm // nh), n_nh - 1), 1)),
                  full((_HALO, c)), full((ng, gd, gd)), full((1, c))],
        out_specs=[pl.BlockSpec((tm, c3), lambda i: (i, 0)), full((_HALO, c)), full((8, c)), full((ng, gd, gd))],
        out_shape=[jax.ShapeDtypeStruct((t, c3), _ACT), jax.ShapeDtypeStruct((_HALO, c), f32),
                   jax.ShapeDtypeStruct((8, c), f32), jax.ShapeDtypeStruct((ng, gd, gd), f32)],
        scratch_shapes=[pltpu.VMEM((_HALO + tm, c), f32), pltpu.VMEM((tm + _HALO, c), f32),
                        pltpu.VMEM((_HALO + tm, c), f32), pltpu.VMEM((tm + nh, c), f32)],
        compiler_params=_params(1),
    )(u, u, dhc, dhc, dy, dy, cw, pw, ps)


def _attn_fwd(q, kv, n_seq, seq, n_mem):
    t, d = q.shape
    dh = d // XATTN_HEADS
    tq = min(1024, seq)
    nq = seq // tq
    scale = dh ** -0.5

    def body(q_ref, kv_ref, o_ref):
        for h in range(XATTN_HEADS):
            cols = slice(h * dh, (h + 1) * dh)
            s = _dot(q_ref[:, cols], kv_ref[:, cols], _NT) * scale
            e = jnp.exp(s - jnp.max(s, axis=-1, keepdims=True))
            p = e / jnp.sum(e, axis=-1, keepdims=True)
            o_ref[:, cols] = _dot(p.astype(_ACT), kv_ref[:, d + h * dh:d + (h + 1) * dh], _NN).astype(o_ref.dtype)

    qs = pl.BlockSpec((tq, d), lambda b, i: (b * nq + i, 0))
    return pl.pallas_call(
        body, name="attn_fwd", grid=(n_seq, nq), in_specs=[qs, pl.BlockSpec((n_mem, 2 * d), lambda b, i: (b, 0))],
        out_specs=qs, out_shape=jax.ShapeDtypeStruct((t, d), _ACT), compiler_params=_params(2),
    )(q, kv)


def _attn_bwd(q, kv, do, n_seq, seq, n_mem):
    t, d = q.shape
    dh = d // XATTN_HEADS
    tq = min(1024, seq)
    nq = seq // tq
    scale = dh ** -0.5

    def body(q_ref, kv_ref, do_ref, dq_ref, dkv_ref, acc):
        i = pl.program_id(1)

        @pl.when(i == 0)
        def _():
            acc[...] = jnp.zeros_like(acc)

        for h in range(XATTN_HEADS):
            cols = slice(h * dh, (h + 1) * dh)
            vcols = slice(d + h * dh, d + (h + 1) * dh)
            qv = q_ref[:, cols]
            kh = kv_ref[:, cols]
            dov = do_ref[:, cols]
            s = _dot(qv, kh, _NT) * scale
            e = jnp.exp(s - jnp.max(s, axis=-1, keepdims=True))
            p = e / jnp.sum(e, axis=-1, keepdims=True)
            dp = _dot(dov, kv_ref[:, vcols], _NT)
            ds = (p * (dp - jnp.sum(dp * p, axis=-1, keepdims=True)) * scale).astype(_ACT)
            dq_ref[:, cols] = _dot(ds, kh, _NN).astype(dq_ref.dtype)
            acc[:, cols] += _dot(ds, qv, _TN)
            acc[:, vcols] += _dot(p.astype(_ACT), dov, _TN)

        @pl.when(i == nq - 1)
        def _():
            dkv_ref[...] = acc[...].astype(dkv_ref.dtype)

    qs = pl.BlockSpec((tq, d), lambda b, i: (b * nq + i, 0))
    ms = pl.BlockSpec((n_mem, 2 * d), lambda b, i: (b, 0))
    return pl.pallas_call(
        body, name="attn_bwd", grid=(n_seq, nq), in_specs=[qs, ms, qs], out_specs=[qs, ms],
        out_shape=[jax.ShapeDtypeStruct((t, d), _ACT), jax.ShapeDtypeStruct((n_seq * n_mem, 2 * d), _ACT)],
        scratch_shapes=[pltpu.VMEM((n_mem, 2 * d), f32)], compiler_params=_params(2),
    )(q, kv, do)


_FFN_ROWS = 2048
_FFN_COLS = 256
_FFN_HALO = 16


def _window(buf, g, start, rows):
    return buf[g, pl.ds(start, rows + 8), :]


def _taps3(win, rows):
    return [_rows_from(win, 6 + k, rows) for k in range(3)]


def _conv3(b_ref, w_ref, taps):
    acc = b_ref[...] + w_ref[0:1, :] * taps[0]
    for k in (1, 2):
        acc = acc + w_ref[k:k + 1, :] * taps[k]
    return acc


def _ffn_gate_fwd(up, fw, fb, seq):
    _, t, f = up.shape
    tm = min(_FFN_ROWS, seq)
    tps = seq // tm
    tc = _FFN_COLS
    nc = f // tc
    hl = _FFN_HALO

    def body(up_ref, uph_ref, wg_ref, wv_ref, bg_ref, bv_ref, a_ref, uc_ref):
        i = pl.program_id(1)
        before = uph_ref[...]
        before = jnp.where(i % tps == 0, jnp.zeros_like(before), before)

        def chunk(r0, wins):
            conv = []
            for g, (w_ref, b_ref) in enumerate(((wg_ref, bg_ref), (wv_ref, bv_ref))):
                conv.append(_conv3(b_ref, w_ref, _taps3(wins[g].astype(f32)[hl - 8:, :], _CHUNK)))
                uc_ref[g, pl.ds(r0, _CHUNK), :] = conv[g].astype(uc_ref.dtype)
            gate, val = conv
            a_ref[pl.ds(r0, _CHUNK), :] = (gate * _sigmoid(gate) * val).astype(a_ref.dtype)

        chunk(0, [jnp.concatenate([before[g], up_ref[g, 0:_CHUNK, :]], axis=0) for g in range(2)])

        def later(ci, carry):
            r0 = pl.multiple_of(ci * _CHUNK, _CHUNK)
            chunk(r0, [up_ref[g, pl.ds(r0 - hl, _CHUNK + hl), :] for g in range(2)])
            return carry

        lax.fori_loop(1, tm // _CHUNK, later, 0)

    hb = tm // hl
    return pl.pallas_call(
        body, name="ffn_gate_fwd", grid=(nc, t // tm),
        in_specs=[pl.BlockSpec((2, tm, tc), lambda j, i: (0, i, j)),
                  pl.BlockSpec((2, hl, tc), lambda j, i: (0, jnp.maximum(i * hb - 1, 0), j)),
                  pl.BlockSpec((8, tc), lambda j, i: (0, j)), pl.BlockSpec((8, tc), lambda j, i: (0, nc + j)),
                  pl.BlockSpec((1, tc), lambda j, i: (0, j)), pl.BlockSpec((1, tc), lambda j, i: (0, nc + j))],
        out_specs=[pl.BlockSpec((tm, tc), lambda j, i: (i, j)), pl.BlockSpec((2, tm, tc), lambda j, i: (0, i, j))],
        out_shape=[jax.ShapeDtypeStruct((t, f), _ACT), jax.ShapeDtypeStruct((2, t, f), _ACT)], compiler_params=_params(2),
    )(up, up, fw, fw, fb, fb)


def _ffn_gate_bwd(up, uc, da, fw, seq):
    _, t, f = up.shape
    tm = min(_FFN_ROWS, seq)
    tps = seq // tm
    tc = _FFN_COLS
    nc = f // tc
    hl = _FFN_HALO

    def body(up_ref, uph_ref, uc_ref, ucn_ref, da_ref, dan_ref, wg_ref, wv_ref, dup_ref, sg_ref, sv_ref, dbuf, sums):
        i = pl.program_id(1)
        at_end = i % tps == tps - 1

        @pl.when(i == 0)
        def _():
            sg_ref[...] = jnp.zeros_like(sg_ref)
            sv_ref[...] = jnp.zeros_like(sv_ref)

        sums[...] = jnp.zeros_like(sums)
        before = uph_ref[...]
        before = jnp.where(i % tps == 0, jnp.zeros_like(before), before)
        w_refs = (wg_ref, wv_ref)

        def grads(r0, rows, conv, dav):
            gate, val = [v.astype(f32) for v in conv]
            sg = _sigmoid(gate)
            douts = (dav * val * (sg * (1.0 + gate * (1.0 - sg))), dav * (gate * sg))
            for g in range(2):
                dbuf[g, pl.ds(r0, rows), :] = douts[g]
            return douts

        def count(douts, wins):
            for g in range(2):
                taps = _taps3(wins[g].astype(f32)[hl - 8:, :], _CHUNK)
                sums[g, 0] += douts[g].reshape(_CHUNK // 8, 8, tc).sum(axis=0)
                for k in range(3):
                    sums[g, 1 + k] += (douts[g] * taps[k]).reshape(_CHUNK // 8, 8, tc).sum(axis=0)

        count(grads(0, _CHUNK, [uc_ref[g, 0:_CHUNK, :] for g in range(2)], da_ref[0:_CHUNK, :].astype(f32)),
              [jnp.concatenate([before[g], up_ref[g, 0:_CHUNK, :]], axis=0) for g in range(2)])

        def first(ci, carry):
            r0 = pl.multiple_of(ci * _CHUNK, _CHUNK)
            douts = grads(r0, _CHUNK, [uc_ref[g, pl.ds(r0, _CHUNK), :] for g in range(2)],
                          da_ref[pl.ds(r0, _CHUNK), :].astype(f32))
            count(douts, [up_ref[g, pl.ds(r0 - hl, _CHUNK + hl), :] for g in range(2)])
            return carry

        lax.fori_loop(1, tm // _CHUNK, first, 0)
        da_after = dan_ref[...].astype(f32)
        grads(tm, hl, [ucn_ref[g] for g in range(2)], jnp.where(at_end, jnp.zeros_like(da_after), da_after))

        def second(ci, carry):
            r0 = pl.multiple_of(ci * _CHUNK, _CHUNK)
            for g in range(2):
                win = _window(dbuf, g, r0, _CHUNK)
                acc = jnp.zeros((_CHUNK, tc), f32)
                for k in range(3):
                    acc = acc + w_refs[g][k:k + 1, :] * _rows_from(win, 2 - k, _CHUNK)
                dup_ref[g, pl.ds(r0, _CHUNK), :] = acc.astype(dup_ref.dtype)
            return carry

        lax.fori_loop(0, tm // _CHUNK, second, 0)
        for g, s_ref in enumerate((sg_ref, sv_ref)):
            for r in range(4):
                s_ref[r:r + 1, :] += jnp.sum(sums[g, r], axis=0, keepdims=True)

    hb = tm // hl
    n_halo = t // hl
    return pl.pallas_call(
        body, name="ffn_gate_bwd", grid=(nc, t // tm),
        in_specs=[pl.BlockSpec((2, tm, tc), lambda j, i: (0, i, j)),
                  pl.BlockSpec((2, hl, tc), lambda j, i: (0, jnp.maximum(i * hb - 1, 0), j)),
                  pl.BlockSpec((2, tm, tc), lambda j, i: (0, i, j)),
                  pl.BlockSpec((2, hl, tc), lambda j, i: (0, jnp.minimum((i + 1) * hb, n_halo - 1), j)),
                  pl.BlockSpec((tm, tc), lambda j, i: (i, j)),
                  pl.BlockSpec((hl, tc), lambda j, i: (jnp.minimum((i + 1) * hb, n_halo - 1), j)),
                  pl.BlockSpec((8, tc), lambda j, i: (0, j)), pl.BlockSpec((8, tc), lambda j, i: (0, nc + j))],
        out_specs=[pl.BlockSpec((2, tm, tc), lambda j, i: (0, i, j)),
                   pl.BlockSpec((8, tc), lambda j, i: (0, j)), pl.BlockSpec((8, tc), lambda j, i: (0, j))],
        out_shape=[jax.ShapeDtypeStruct((2, t, f), _ACT), jax.ShapeDtypeStruct((8, f), f32), jax.ShapeDtypeStruct((8, f), f32)],
        scratch_shapes=[pltpu.VMEM((2, tm + hl, tc), f32), pltpu.VMEM((2, 4, 8, tc), f32)],
        compiler_params=_params(2),
    )(up, up, uc, uc, da, da, fw, fw)


def _adamw_math(w, g, m, v):
    m = ADAM_B1 * m + (1.0 - ADAM_B1) * g
    v = ADAM_B2 * v + (1.0 - ADAM_B2) * (g * g)
    m_hat = m / (1.0 - ADAM_B1 ** ADAM_STEP)
    v_hat = v / (1.0 - ADAM_B2 ** ADAM_STEP)
    delta = -ADAM_LR * (m_hat / (jnp.sqrt(v_hat) + ADAM_EPS) + ADAM_WD * w)
    return delta, m, v


def _adamw_shards(quads):
    n = len(quads)
    steps = 8

    def body(*refs):
        for p in range(n):
            w_ref, g_ref, m_ref, v_ref = refs[4 * p:4 * p + 4]
            go_ref, d_ref, mo_ref, vo_ref = refs[4 * n + 4 * p:4 * n + 4 * p + 4]
            gv = g_ref[...]
            d, mn, vn = _adamw_math(w_ref[...], gv, m_ref[...], v_ref[...])
            go_ref[...] = gv
            d_ref[...] = d
            mo_ref[...] = mn
            vo_ref[...] = vn

    in_specs, out_specs, out_shape = [], [], []
    for w, _, _, _ in quads:
        _, r, c = w.shape
        s3 = pl.BlockSpec((None, r // steps, c), lambda i: (0, i, 0))
        in_specs += [s3, pl.BlockSpec((r // steps, c), lambda i: (i, 0)), s3, s3]
        out_specs += [s3] * 4
        out_shape += [jax.ShapeDtypeStruct(w.shape, f32)] * 4
    outs = pl.pallas_call(
        body, name="adamw_shards", grid=(steps,), in_specs=in_specs, out_specs=out_specs, out_shape=out_shape,
        compiler_params=_params(1),
    )(*[a for q in quads for a in q])
    return [tuple(outs[4 * p:4 * p + 4]) for p in range(n)]


def _adamw_small(quads):
    n = len(quads)

    def body(*refs):
        ins, outs = refs[:4 * n], refs[4 * n:]
        for p in range(n):
            w_ref, g_ref, m_ref, v_ref = ins[4 * p:4 * p + 4]
            d, mn, vn = _adamw_math(w_ref[...], g_ref[...], m_ref[...], v_ref[...])
            outs[3 * p][...] = d
            outs[3 * p + 1][...] = mn
            outs[3 * p + 2][...] = vn

    flat = [a for q in quads for a in q]
    shapes = [jax.ShapeDtypeStruct(q[0].shape, f32) for q in quads for _ in range(3)]
    outs = pl.pallas_call(
        body, name="adamw_small", in_specs=[_VMEM] * (4 * n), out_specs=[_VMEM] * (3 * n), out_shape=shapes,
        compiler_params=pltpu.CompilerParams(vmem_limit_bytes=_VMEM_LIMIT_BYTES),
    )(*flat)
    return [tuple(outs[3 * p:3 * p + 3]) for p in range(n)]


def _sum_partials(name, place, grads, got):
    nw = len(grads)
    steps = 2

    def body(place_ref, *refs):
        for w in range(nw):
            own_ref, got_ref, f_ref = refs[w], refs[nw + w], refs[2 * nw + w]
            s = own_ref[...].astype(f32)
            for k in range(got[w].shape[0]):
                s = s + got_ref[k].astype(f32)
            f_ref[...] = s

    own_specs, got_specs, out_specs, out_shape = [], [], [], []
    for g, l in zip(grads, got):
        _, r, c = g.shape
        tr = r // steps
        own_specs.append(pl.BlockSpec((None, tr, c), lambda i, p: (2 * p[0] + p[1], i, 0)))
        got_specs.append(pl.BlockSpec((l.shape[0], tr, c), lambda i, p: (0, i, 0)))
        out_specs.append(pl.BlockSpec((None, tr, c), lambda i, p: (p[1], i, 0)))
        out_shape.append(jax.ShapeDtypeStruct((2, r, c), f32))
    grid_spec = pltpu.PrefetchScalarGridSpec(num_scalar_prefetch=1, grid=(steps,), in_specs=own_specs + got_specs, out_specs=out_specs)
    return pl.pallas_call(body, name=name, grid_spec=grid_spec, out_shape=out_shape,
                          compiler_params=_params(1))(place, *grads, *got)


def _place():
    return lax.axis_index("x"), lax.axis_index("y"), lax.axis_index("c")


def _other_chips(x, y):
    return [(1 - x, y), (x, 1 - y), (1 - x, 1 - y)]


def _remote(src, dst, send_sem, recv_sem, to):
    return pltpu.make_async_remote_copy(src_ref=src, dst_ref=dst, send_sem=send_sem, recv_sem=recv_sem,
                                        device_id=to, device_id_type=_MESH)


def _place_shards(name, place, shards, col_sharded, after=None):
    n = len(shards)
    steps = 4
    more, more_specs = _after(after)

    def body(place_ref, *refs):
        for src, dst in zip(refs[:n], refs[n + len(more):]):
            dst[...] = src[...].astype(dst.dtype)

    in_specs, out_specs, out_shape = [], [], []
    for w, col in zip(shards, col_sharded):
        r, cs = w.shape
        tr = r // steps
        in_specs.append(pl.BlockSpec((tr, cs), lambda i, p: (i, 0)))
        if col:
            out_specs.append(pl.BlockSpec((tr, cs), lambda i, p: (i, p[0])))
            out_shape.append(jax.ShapeDtypeStruct((r, 4 * cs), _ACT))
        else:
            out_specs.append(pl.BlockSpec((tr, cs), lambda i, p: (p[0] * steps + i, 0)))
            out_shape.append(jax.ShapeDtypeStruct((4 * r, cs), _ACT))
    grid_spec = pltpu.PrefetchScalarGridSpec(num_scalar_prefetch=1, grid=(steps,), in_specs=in_specs + more_specs,
                                            out_specs=out_specs)
    return pl.pallas_call(body, name=name, grid_spec=grid_spec, out_shape=out_shape,
                          compiler_params=_params(1))(place, *shards, *more)


def _shard_of(ref, col_sharded, s):
    rows, cols = ref.shape
    if col_sharded:
        return ref.at[:, pl.ds(s * (cols // 4), cols // 4)]
    return ref.at[pl.ds(s * (rows // 4), rows // 4), :]


def _part_of(ref, col_sharded, whole, s, h):
    if whole:
        return _shard_of(ref, col_sharded, s)
    rows, cols = ref.shape
    if col_sharded:
        return ref.at[pl.ds(h * (rows // 2), rows // 2), pl.ds(s * (cols // 4), cols // 4)]
    return ref.at[pl.ds((2 * s + h) * (rows // 8), rows // 8), :]


def _allgather_start(name, bufs, col_sharded, whole, groups):
    n = len(bufs)
    ng = len(groups)

    def body(*refs):
        out = refs[n:2 * n]
        sems = refs[2 * n:2 * n + 2 * ng]
        token = refs[2 * n + 2 * ng]
        x, y, c = _place()
        for g, members in enumerate(groups):
            for i, w in enumerate(members):
                mine = _part_of(out[w], col_sharded[w], whole[w], 2 * x + y, c)
                for j, chip in enumerate(_other_chips(x, y)):
                    _remote(mine, mine, sems[2 * g].at[3 * i + j], sems[2 * g + 1].at[3 * i + j], (*chip, c)).start()
        token[...] = jnp.zeros_like(token)

    sem_shapes = [pltpu.SemaphoreType.DMA((3 * len(m),)) for m in groups for _ in range(2)]
    outs = pl.pallas_call(
        body, name=name, in_specs=[_HBM] * n, out_specs=[_HBM] * n + [_SEM] * (2 * ng) + [_VMEM],
        out_shape=[pltpu.HBM(b.shape, b.dtype) for b in bufs] + sem_shapes + [jax.ShapeDtypeStruct((8, 128), f32)],
        input_output_aliases={i: i for i in range(n)},
        compiler_params=pltpu.CompilerParams(has_side_effects=_EFFECT),
    )(*[pltpu.with_memory_space_constraint(b, pltpu.HBM) for b in bufs])
    return list(outs[:n]), [(outs[n + 2 * g], outs[n + 2 * g + 1]) for g in range(ng)], outs[n + 2 * ng]


def _allgather_relay(name, bufs, col_sharded, whole, sems, after):
    n = len(bufs)

    def body(*refs):
        buf = refs[:n]
        send, recv = refs[n], refs[n + 1]
        out = refs[n + 3:2 * n + 3]
        to_sibling, from_sibling, token = refs[2 * n + 3:]
        token[...] = jnp.zeros_like(token)
        x, y, c = _place()
        for i in range(n):
            mine = _part_of(buf[i], col_sharded[i], whole[i], 2 * x + y, c)
            for j, chip in enumerate(_other_chips(x, y)):
                landed = _part_of(buf[i], col_sharded[i], whole[i], 2 * chip[0] + chip[1], c)
                cp = _remote(mine, landed, send.at[3 * i + j], recv.at[3 * i + j], (*chip, c))
                cp.wait_send()
                cp.wait_recv()
        for i in range(n):
            if not whole[i]:
                for j, chip in enumerate(_other_chips(x, y)):
                    landed = _part_of(out[i], col_sharded[i], False, 2 * chip[0] + chip[1], c)
                    _remote(landed, landed, to_sibling.at[3 * i + j], from_sibling.at[3 * i + j], (x, y, 1 - c)).start()

    outs = pl.pallas_call(
        body, name=name, in_specs=[_HBM] * n + [_SEM, _SEM, _ANY], out_specs=[_HBM] * n + [_SEM, _SEM, _VMEM],
        out_shape=[pltpu.HBM(b.shape, b.dtype) for b in bufs] + [pltpu.SemaphoreType.DMA((3 * n,))] * 2
        + [jax.ShapeDtypeStruct((8, 128), f32)],
        input_output_aliases={i: i for i in range(n)},
        compiler_params=pltpu.CompilerParams(has_side_effects=_EFFECT),
    )(*bufs, *sems, after)
    return list(outs[:n]), (outs[n], outs[n + 1]), outs[n + 2]


def _allgather_wait(name, bufs, col_sharded, whole, sems, after):
    n = len(bufs)

    def body(*refs):
        buf = refs[:n]
        to_sibling, from_sibling = refs[n], refs[n + 1]
        x, y, c = _place()
        for i in range(n):
            if not whole[i]:
                for j, chip in enumerate(_other_chips(x, y)):
                    sent = _part_of(buf[i], col_sharded[i], False, 2 * chip[0] + chip[1], c)
                    landed = _part_of(buf[i], col_sharded[i], False, 2 * chip[0] + chip[1], 1 - c)
                    cp = _remote(sent, landed, to_sibling.at[3 * i + j], from_sibling.at[3 * i + j], (x, y, 1 - c))
                    cp.wait_send()
                    cp.wait_recv()

    return pl.pallas_call(
        body, name=name, in_specs=[_HBM] * n + [_SEM, _SEM, _ANY], out_specs=[_HBM] * n,
        out_shape=[pltpu.HBM(b.shape, b.dtype) for b in bufs],
        input_output_aliases={i: i for i in range(n)},
        compiler_params=pltpu.CompilerParams(has_side_effects=_EFFECT),
    )(*bufs, *sems, after)


def _other_devices(x, y, c):
    flips = [(bx, by, bc) for bx in (0, 1) for by in (0, 1) for bc in (0, 1)][1:]
    return [(1 - x if bx else x, 1 - y if by else y, 1 - c if bc else c) for bx, by, bc in flips]


def _grad_exchange_start(name, grads):
    nw = len(grads)
    lands = [lax.empty((7,) + g.shape[1:], g.dtype) for g in grads]

    def body(*refs):
        src = refs[2 * nw:3 * nw]
        got = refs[3 * nw:4 * nw]
        send, recv, token = refs[4 * nw:]
        x, y, c = _place()
        for w in range(nw):
            for k, (px, py, pc) in enumerate(_other_devices(x, y, c)):
                _remote(src[w].at[4 * px + 2 * py + pc], got[w].at[k], send.at[7 * w + k], recv.at[7 * w + k], (px, py, pc)).start()
        token[...] = jnp.zeros_like(token)

    outs = pl.pallas_call(
        body, name=name, in_specs=[_HBM] * (2 * nw), out_specs=[_HBM] * (2 * nw) + [_SEM, _SEM, _VMEM],
        out_shape=[pltpu.HBM(a.shape, a.dtype) for a in list(grads) + lands]
        + [pltpu.SemaphoreType.DMA((7 * nw,)), pltpu.SemaphoreType.DMA((7 * nw,)), jax.ShapeDtypeStruct((8, 128), f32)],
        input_output_aliases={i: i for i in range(2 * nw)},
        compiler_params=pltpu.CompilerParams(has_side_effects=_EFFECT),
    )(*[pltpu.with_memory_space_constraint(a, pltpu.HBM) for a in list(grads) + lands])
    return list(outs[:nw]), list(outs[nw:2 * nw]), (outs[2 * nw], outs[2 * nw + 1]), outs[2 * nw + 2]


def _grad_exchange_wait(name, grads, got, sems, after):
    nw = len(grads)

    def body(*refs):
        src = refs[:nw]
        land = refs[nw:2 * nw]
        send, recv = refs[2 * nw], refs[2 * nw + 1]
        x, y, c = _place()
        for w in range(nw):
            for k, (px, py, pc) in enumerate(_other_devices(x, y, c)):
                cp = _remote(src[w].at[4 * px + 2 * py + pc], land[w].at[k], send.at[7 * w + k], recv.at[7 * w + k], (px, py, pc))
                cp.wait_send()
                cp.wait_recv()

    outs = pl.pallas_call(
        body, name=name, in_specs=[_HBM] * (2 * nw) + [_SEM, _SEM, _ANY], out_specs=[_HBM] * (2 * nw),
        out_shape=[pltpu.HBM(a.shape, a.dtype) for a in list(grads) + list(got)],
        input_output_aliases={i: i for i in range(2 * nw)},
        compiler_params=pltpu.CompilerParams(has_side_effects=_EFFECT),
    )(*grads, *got, *sems, after)
    return list(outs[:nw]), list(outs[nw:])


def _swap_halves_start(finals):
    nw = len(finals)

    def body(*refs):
        buf = refs[nw:2 * nw]
        send, recv, token = refs[2 * nw:]
        x, y, c = _place()
        for w in range(nw):
            _remote(buf[w].at[c], buf[w].at[c], send.at[w], recv.at[w], (x, y, 1 - c)).start()
        token[...] = jnp.zeros_like(token)

    outs = pl.pallas_call(
        body, name="rs_swap_start", in_specs=[_HBM] * nw, out_specs=[_HBM] * nw + [_SEM, _SEM, _VMEM],
        out_shape=[pltpu.HBM(g.shape, g.dtype) for g in finals] + [pltpu.SemaphoreType.DMA((nw,))] * 2
        + [jax.ShapeDtypeStruct((8, 128), f32)],
        input_output_aliases={i: i for i in range(nw)},
        compiler_params=pltpu.CompilerParams(has_side_effects=_EFFECT),
    )(*[pltpu.with_memory_space_constraint(g, pltpu.HBM) for g in finals])
    return list(outs[:nw]), (outs[nw], outs[nw + 1]), outs[nw + 2]


def _swap_halves_wait(bufs, sems, after):
    nw = len(bufs)

    def body(*refs):
        buf = refs[:nw]
        send, recv = refs[nw], refs[nw + 1]
        x, y, c = _place()
        for w in range(nw):
            cp = _remote(buf[w].at[c], buf[w].at[1 - c], send.at[w], recv.at[w], (x, y, 1 - c))
            cp.wait_send()
            cp.wait_recv()

    return pl.pallas_call(
        body, name="rs_swap_wait", in_specs=[_HBM] * nw + [_SEM, _SEM, _ANY], out_specs=[_HBM] * nw,
        out_shape=[pltpu.HBM(g.shape, g.dtype) for g in bufs],
        input_output_aliases={i: i for i in range(nw)},
        compiler_params=pltpu.CompilerParams(has_side_effects=_EFFECT),
    )(*bufs, *sems, after)


def _half_slices(shape, h):
    rows, cols = shape
    if cols % 256 == 0:
        return (slice(None), slice(h * (cols // 2), (h + 1) * (cols // 2)))
    return (slice(h * (rows // 2), (h + 1) * (rows // 2)), slice(None))


def _allreduce_small(parts, after):
    n = len(parts)

    def body(*refs):
        src = refs[:n]
        refs = refs[n + 1:]
        out = refs[:n]
        sib = refs[n:2 * n]
        chip_sum = refs[2 * n:3 * n]
        slots = refs[3 * n:4 * n]
        pair_send, pair_recv, ici_send, ici_recv, swap_send, swap_recv = refs[4 * n:]
        x, y, c = _place()
        me_chip = 2 * x + y
        chips = _other_chips(x, y)
        pairs = [_remote(src[a], sib[a], pair_send.at[a], pair_recv.at[a], (x, y, 1 - c)) for a in range(n)]
        for rc in pairs:
            rc.start()
        for a in range(n):
            pairs[a].wait_recv()
            chip_sum[a][...] = src[a][...] + sib[a][...]
        for h in (0, 1):
            @pl.when(c == h)
            def _():
                sends = []
                for a in range(n):
                    idx = _half_slices(parts[a].shape, h)
                    for j, chip in enumerate(chips):
                        rc = _remote(chip_sum[a].at[idx], slots[a].at[me_chip].at[idx], ici_send.at[3 * a + j], ici_recv.at[3 * a + j], (*chip, h))
                        rc.start()
                        sends.append(rc)
                    slots[a][(me_chip,) + idx] = chip_sum[a][idx]
                for a in range(n):
                    idx = _half_slices(parts[a].shape, h)
                    for j, chip in enumerate(chips):
                        landed = slots[a].at[2 * chip[0] + chip[1]].at[idx]
                        _remote(landed, landed, ici_send.at[3 * a + j], ici_recv.at[3 * a + j], (x, y, c)).wait_recv()
                    total = slots[a][(0,) + idx]
                    for s in range(1, 4):
                        total = total + slots[a][(s,) + idx]
                    out[a][idx] = total
                    rc = _remote(out[a].at[idx], out[a].at[idx], swap_send.at[a], swap_recv.at[a], (x, y, 1 - h))
                    rc.start()
                    sends.append(rc)
                for a in range(n):
                    other = out[a].at[_half_slices(parts[a].shape, 1 - h)]
                    _remote(other, other, swap_send.at[a], swap_recv.at[a], (x, y, c)).wait_recv()
                for rc in sends:
                    rc.wait_send()
        for rc in pairs:
            rc.wait_send()

    return pl.pallas_call(
        body, name="allreduce_small", in_specs=[_VMEM] * n + [_ANY], out_specs=[_VMEM] * n,
        out_shape=[jax.ShapeDtypeStruct(p.shape, f32) for p in parts],
        scratch_shapes=[pltpu.VMEM(p.shape, f32) for p in parts] * 2 + [pltpu.VMEM((4,) + p.shape, f32) for p in parts]
        + [pltpu.SemaphoreType.DMA((n,)), pltpu.SemaphoreType.DMA((n,)), pltpu.SemaphoreType.DMA((3 * n,)),
           pltpu.SemaphoreType.DMA((3 * n,)), pltpu.SemaphoreType.DMA((n,)), pltpu.SemaphoreType.DMA((n,))],
        compiler_params=pltpu.CompilerParams(vmem_limit_bytes=_VMEM_LIMIT_BYTES),
    )(*parts, after)


def _local_step(x, mem, tgt, g_mix, g_xattn, g_mem, g_ffn, g_final, cb, lg, lb, pw, ps, fb, started, relay, weights, reduce,
                n_seq, seq, n_mem):
    t, d = x.shape
    f = fb.shape[1] // 2
    c = cb.shape[1]
    h1 = _rms_fwd("norm_mix", x, g_mix, after=started)
    relay(0, h1)
    w_in, cw, fw = weights(0, h1)
    u = _mm_nn("proj_in", h1, w_in, _ACT, w_in.shape[1])
    y, hc = _mix_fwd(u, cw, cb, lg, lb, pw, ps, seq)
    relay(1, y)
    w_out, w_q, w_kv, w_o = weights(1, y)
    x1, h2 = _proj_residual_norm("proj_out", y, w_out, x, g_xattn)
    q = _mm_nn("proj_q", h2, w_q, _ACT, d)
    mem_n = _rms_fwd("norm_mem", mem, g_mem)
    kv = _mm_nn("proj_kv", mem_n, w_kv, _ACT, 2 * d)
    o = _attn_fwd(q, kv, n_seq, seq, n_mem)
    x2, h3 = _proj_residual_norm("proj_o", o, w_o, x1, g_ffn, after=relay(2, o))
    w_up, w_down = weights(2, h3)
    up = _mm_nn("proj_up", h3, w_up, _ACT, f, split_out=True)
    a, uc = _ffn_gate_fwd(up, fw, fb, seq)
    dx3, dx3b, dg_final, loss = _proj_loss_bwd("proj_down", a, w_down, x2, g_final, tgt)
    da = _mm_nt("d_act", dx3b, w_down, _ACT)
    gw_down = _mm_tn_rows("dw_down", a, dx3b, f // 2, d // 2)
    dup, sums_g, sums_v = _ffn_gate_bwd(up, uc, da, fw, seq)
    gw_up = _mm_tn_pieces("dw_up", h3, dup, f // 2)
    token = reduce(0, [gw_down.reshape(8, -1, d), gw_up])
    dx2, dx2b, dg_ffn = _dproj_rms_bwd("d_h3", dup, w_up, x2, g_ffn, dx3, after=token)
    do = _mm_nt("d_o", dx2b, w_o, _ACT)
    gw_o = _mm_tn_rows("dw_o", o, dx2b, d, d // 2)
    dq, dkv = _attn_bwd(q, kv, do, n_seq, seq, n_mem)
    gw_q = _mm_tn_rows("dw_q", h2, dq, d, d // 2)
    gw_kv = _mm_tn_pieces("dw_kv", mem_n, dkv, d // 2)
    dmem_n = _mm_nt("d_mem_n", dkv, w_kv, f32)
    dg_mem = _rms_gain_grad("norm_mem_bwd", mem, dmem_n)
    dx1, dx1b, dg_xattn = _dproj_rms_bwd("d_h2", dq, w_q, x1, g_xattn, dx2)
    dy = _mm_nt("d_y", dx1b, w_out, _ACT)
    gw_out = _mm_tn_rows("dw_out", y, dx1b, d, d // 2)
    token = reduce(1, [gw_o.reshape(8, -1, d), gw_q.reshape(8, -1, d), gw_kv, gw_out.reshape(8, -1, d)])
    dhc, sums_norm = _mix_bwd_norm(hc, dy, lg, lb, token)
    du, d_cw, d_ps, d_pw = _mix_bwd_taps(u, dhc, dy, cw, pw, ps, seq)
    gw_in = _mm_tn_pieces("dw_in", h1, du, c * 3 // 4)
    token = reduce(2, [gw_in])
    grad_x, dg_mix = _dproj_rms_bwd("d_h1", du, w_in, x, g_mix, dx1, storage_copy=False, after=token)
    zero_row = jnp.zeros((1, d), f32)
    gains = jnp.concatenate([dg_mix, dg_xattn, dg_mem, dg_ffn, dg_final, jnp.pad(loss, ((0, 0), (0, d - 1))), zero_row, zero_row], axis=0)
    conv_rows = jnp.concatenate([sums_norm[2:3], sums_norm[0:1], sums_norm[1:2], d_ps[0:1], jnp.zeros((4, c), f32)], axis=0)
    ffn_rows = jnp.concatenate([sums_g, sums_v], axis=1)
    small = [gains, conv_rows, d_pw.reshape(-1, d_pw.shape[-1]), ffn_rows, d_cw]
    return grad_x, small


def kernel(x, mem, norm_mix_g, w_in, conv_dw_w, conv_dw_b, conv_ln_g, conv_ln_b, pool_w, pool_scale, w_out, norm_xattn_g, norm_mem_g, w_q, w_kv, w_o, norm_ffn_g, w_up, ffn_dw_w, ffn_dw_b, w_down, norm_final_g, loss_target, m_norm_mix_g, m_w_in, m_conv_dw_w, m_conv_dw_b, m_conv_ln_g, m_conv_ln_b, m_pool_w, m_pool_scale, m_w_out, m_norm_xattn_g, m_norm_mem_g, m_w_q, m_w_kv, m_w_o, m_norm_ffn_g, m_w_up, m_ffn_dw_w, m_ffn_dw_b, m_w_down, m_norm_final_g, v_norm_mix_g, v_w_in, v_conv_dw_w, v_conv_dw_b, v_conv_ln_g, v_conv_ln_b, v_pool_w, v_pool_scale, v_w_out, v_norm_xattn_g, v_norm_mem_g, v_w_q, v_w_kv, v_w_o, v_norm_ffn_g, v_w_up, v_ffn_dw_w, v_ffn_dw_b, v_w_down, v_norm_final_g):
    n_seq, seq, d = x.shape
    n_mem = mem.shape[1]
    chip = 2 * lax.axis_index("x") + lax.axis_index("y")

    place = jnp.stack([chip, lax.axis_index("c")]).astype(jnp.int32)

    col_w = [w_in, w_kv, w_up]
    row_w = [w_out, w_q, w_o, w_down]
    kw = conv_dw_w.shape[1]

    def padded_in_place(shard, rows):
        full = jnp.zeros((rows, 4 * shard.shape[1]), shard.dtype)
        return lax.dynamic_update_slice(full, shard, (0, chip * shard.shape[1]))

    first = list(_place_shards("place_w_in", place, [w_in[0]], [True]))
    first += [padded_in_place(conv_dw_w[0], _HALO), padded_in_place(ffn_dw_w[0], 8)]
    first, first_sems, token = _allgather_start("allgather_start_0", first, [True] * 3, [False, True, True], [[0, 1, 2]])
    rest = [w_kv, w_up, w_out, w_q, w_o, w_down]
    rest_flags = [True, True, False, False, False, False]
    rest = list(_place_shards("place_rest", place, [w[0] for w in rest], rest_flags, after=token))
    rest, rest_sems, all_started = _allgather_start("allgather_start_1", rest, rest_flags, [False] * 6, [[2, 3, 0, 4], [1, 5]])
    started = [(first, [True] * 3, [False, True, True], first_sems[0]),
               ([rest[i] for i in (2, 3, 0, 4)], [False, False, True, False], [False] * 4, rest_sems[0]),
               ([rest[i] for i in (1, 5)], [True, False], [False] * 2, rest_sems[1])]
    relayed = {}

    def relay(g, after):
        group_bufs, flags, wholes, group_sems = started[g]
        group_bufs, sibling_sems, relay_token = _allgather_relay("allgather_relay_%d" % g, group_bufs, flags, wholes, group_sems, after)
        relayed[g] = (group_bufs, sibling_sems)
        return relay_token

    def weights(g, after):
        group_bufs, sibling_sems = relayed[g]
        return _allgather_wait("allgather_wait_%d" % g, group_bufs, started[g][1], started[g][2], sibling_sems, after)

    names = ["w_in", "w_kv", "w_up", "w_out", "w_q", "w_o", "w_down"]
    reduce_groups = [["w_down", "w_up"], ["w_o", "w_q", "w_kv", "w_out"], ["w_in"]]
    in_flight = {}

    def reduce(g, grads):
        grads, lands, rs_sems, token = _grad_exchange_start("rs_start_%d" % g, grads)
        in_flight[g] = (grads, lands, rs_sems)
        return token

    grad_x, small = _local_step(
        x.reshape(n_seq * seq, d), mem.reshape(n_seq * n_mem, d), loss_target.reshape(n_seq * seq, d),
        norm_mix_g, norm_xattn_g, norm_mem_g, norm_ffn_g, norm_final_g.reshape(1, d),
        conv_dw_b, conv_ln_g, conv_ln_b, pool_w[0], pool_scale, ffn_dw_b, all_started, relay, weights, reduce,
        n_seq, seq, n_mem)

    landed = {}
    for g, members in enumerate(reduce_groups):
        grads, lands, rs_sems = in_flight[g]
        grads, lands = _grad_exchange_wait("rs_wait_%d" % g, grads, lands, rs_sems, grad_x)
        landed.update(zip(members, zip(grads, lands)))
    finals = _sum_partials("rs_sum", place, [landed[n][0] for n in names], [landed[n][1] for n in names])
    finals, swap_sems, token = _swap_halves_start(finals)

    gains, conv_rows, d_pw, ffn_rows, d_cw = _allreduce_small(small, token)
    loss = gains[5, 0]
    shard_grads = _swap_halves_wait(finals, swap_sems, gains)

    outs = {}
    big_w = dict(zip(names, col_w + row_w))
    big_m = dict(w_in=m_w_in, w_kv=m_w_kv, w_up=m_w_up, w_out=m_w_out, w_q=m_w_q, w_o=m_w_o, w_down=m_w_down)
    big_v = dict(w_in=v_w_in, w_kv=v_w_kv, w_up=v_w_up, w_out=v_w_out, w_q=v_w_q, w_o=v_w_o, w_down=v_w_down)
    big_quads = [(big_w[n], g.reshape(big_w[n].shape[1:]), big_m[n], big_v[n]) for n, g in zip(names, shard_grads)]
    outs.update(zip(names, _adamw_shards(big_quads)))

    f2 = ffn_dw_b.shape[1]
    cs_c = conv_dw_w.shape[2]
    cs_f = ffn_dw_w.shape[2]
    g_cw = lax.dynamic_slice(d_cw, (0, chip * cs_c), (kw, cs_c)).reshape(conv_dw_w.shape)
    g_fw = lax.dynamic_slice(ffn_rows, (1, chip * cs_f), (ffn_dw_w.shape[1], cs_f)).reshape(ffn_dw_w.shape)
    small_params = [
        ("norm_mix_g", norm_mix_g, gains[0:1], m_norm_mix_g, v_norm_mix_g),
        ("conv_dw_w", conv_dw_w, g_cw, m_conv_dw_w, v_conv_dw_w),
        ("conv_dw_b", conv_dw_b, conv_rows[0:1], m_conv_dw_b, v_conv_dw_b),
        ("conv_ln_g", conv_ln_g, conv_rows[1:2], m_conv_ln_g, v_conv_ln_g),
        ("conv_ln_b", conv_ln_b, conv_rows[2:3], m_conv_ln_b, v_conv_ln_b),
        ("pool_w", pool_w, d_pw.reshape(pool_w.shape), m_pool_w, v_pool_w),
        ("pool_scale", pool_scale, conv_rows[3:4], m_pool_scale, v_pool_scale),
        ("norm_xattn_g", norm_xattn_g, gains[1:2], m_norm_xattn_g, v_norm_xattn_g),
        ("norm_mem_g", norm_mem_g, gains[2:3], m_norm_mem_g, v_norm_mem_g),
        ("norm_ffn_g", norm_ffn_g, gains[3:4], m_norm_ffn_g, v_norm_ffn_g),
        ("ffn_dw_w", ffn_dw_w, g_fw, m_ffn_dw_w, v_ffn_dw_w),
        ("ffn_dw_b", ffn_dw_b, ffn_rows[0:1, :f2], m_ffn_dw_b, v_ffn_dw_b),
        ("norm_final_g", norm_final_g.reshape(1, d), gains[4:5], m_norm_final_g.reshape(1, d), v_norm_final_g.reshape(1, d)),
    ]
    quads = []
    for _, w, g, m, v in small_params:
        shape2 = (-1, w.shape[-1])
        quads.append((w.reshape(shape2), g.reshape(shape2), m.reshape(shape2), v.reshape(shape2)))
    for (n, w, g, _, _), (delta, new_m, new_v) in zip(small_params, _adamw_small(quads)):
        shape = norm_final_g.shape if n == "norm_final_g" else w.shape
        outs[n] = (g.reshape(shape), delta.reshape(shape), new_m.reshape(shape), new_v.reshape(shape))

    order = ["norm_mix_g", "w_in", "conv_dw_w", "conv_dw_b", "conv_ln_g", "conv_ln_b", "pool_w", "pool_scale", "w_out",
             "norm_xattn_g", "norm_mem_g", "w_q", "w_kv", "w_o", "norm_ffn_g", "w_up", "ffn_dw_w", "ffn_dw_b", "w_down",
             "norm_final_g"]
    return (loss, grad_x.reshape(x.shape), *[outs[n][0] for n in order], *[outs[n][1] for n in order],
            *[outs[n][2] for n in order], *[outs[n][3] for n in order])
```

```python
import jax
import jax.numpy as jnp
from jax import lax
from jax.experimental import pallas as pl
from jax.experimental.pallas import tpu as pltpu

f32 = jnp.float32
_ACT = jnp.bfloat16

EPS = 1e-6
POOL_WINDOWS = (2, 4, 8, 16)
XATTN_HEADS = 4
ADAM_LR = 0.001
ADAM_B1 = 0.9
ADAM_B2 = 0.999
ADAM_EPS = 1e-08
ADAM_WD = 0.01
ADAM_STEP = 10

_VMEM_LIMIT_BYTES = 56 * 1024 * 1024
_MESH = pl.DeviceIdType.MESH
_ANY = pl.BlockSpec(memory_space=pl.ANY)
_VMEM = pl.BlockSpec(memory_space=pltpu.VMEM)
_HBM = pl.BlockSpec(memory_space=pltpu.HBM)
_SEM = pl.BlockSpec(memory_space=pltpu.SEMAPHORE)
_EFFECT = pltpu.SideEffectType.DATAFLOW_SIDE_EFFECTING

_NN = (((1,), (0,)), ((), ()))
_NT = (((1,), (1,)), ((), ()))
_TN = (((0,), (0,)), ((), ()))


def _params(n_grid):
    return pltpu.CompilerParams(dimension_semantics=("arbitrary",) * n_grid, vmem_limit_bytes=_VMEM_LIMIT_BYTES)


def _sigmoid(v):
    return 0.5 * jnp.tanh(0.5 * v) + 0.5


def _dot(a, b, dims):
    return lax.dot_general(a, b, dims, preferred_element_type=f32)


def _mm(name, a, b, *, dims, grid, a_spec, b_spec, o_spec, out_shape):
    def body(a_ref, b_ref, o_ref):
        o_ref[...] = _dot(a_ref[...], b_ref[...], dims).astype(o_ref.dtype)

    return pl.pallas_call(
        body, name=name, grid=grid, in_specs=[a_spec, b_spec], out_specs=o_spec, out_shape=out_shape,
        compiler_params=_params(len(grid)),
    )(a, b)


_NARROW = 2816


def _row_tile(m, width=_NARROW + 1):
    return min(1024 if width <= _NARROW else 512, m)


def _mm_nn(name, a, b, out_dtype, tn, split_out=False):
    m, k = a.shape
    n = b.shape[1]
    tm = _row_tile(m, max(k, tn))
    if split_out:
        out_shape = jax.ShapeDtypeStruct((n // tn, m, tn), out_dtype)
        o_spec = pl.BlockSpec((None, tm, tn), lambda j, i: (j, i, 0))
    else:
        out_shape = jax.ShapeDtypeStruct((m, n), out_dtype)
        o_spec = pl.BlockSpec((tm, tn), lambda j, i: (i, j))
    return _mm(
        name, a, b, dims=_NN, grid=(n // tn, m // tm),
        a_spec=pl.BlockSpec((tm, k), lambda j, i: (i, 0)), b_spec=pl.BlockSpec((k, tn), lambda j, i: (0, j)),
        o_spec=o_spec, out_shape=out_shape,
    )


def _mm_nt(name, a, b, out_dtype):
    n, kc = b.shape
    m = a.shape[0]
    tm = _row_tile(m, max(n, kc))
    return _mm(
        name, a, b, dims=_NT, grid=(m // tm,),
        a_spec=pl.BlockSpec((tm, kc), lambda i: (i, 0)),
        b_spec=pl.BlockSpec((n, kc), lambda i: (0, 0), pipeline_mode=pl.Buffered(1)),
        o_spec=pl.BlockSpec((tm, n), lambda i: (i, 0)),
        out_shape=jax.ShapeDtypeStruct((m, n), out_dtype),
    )


def _mm_tn_rows(name, a, b, tka, tn):
    m, ka = a.shape
    nb = b.shape[1]
    return _mm(
        name, a, b, dims=_TN, grid=(ka // tka, nb // tn),
        a_spec=pl.BlockSpec((m, tka), lambda i, j: (0, i)), b_spec=pl.BlockSpec((m, tn), lambda i, j: (0, j)),
        o_spec=pl.BlockSpec((tka, tn), lambda i, j: (i, j)),
        out_shape=jax.ShapeDtypeStruct((ka, nb), _ACT),
    )


def _mm_tn_pieces(name, a, b, cs):
    m, ka = a.shape
    if b.ndim == 3:
        b_spec = pl.BlockSpec((None, m, cs), lambda i, j: (j // 2, 0, j % 2))
    else:
        b_spec = pl.BlockSpec((m, cs), lambda i, j: (0, j))
    return _mm(
        name, a, b, dims=_TN, grid=(2, 4),
        a_spec=pl.BlockSpec((m, ka // 2), lambda i, j: (0, i)), b_spec=b_spec,
        o_spec=pl.BlockSpec((None, ka // 2, cs), lambda i, j: (2 * j + i, 0, 0)),
        out_shape=jax.ShapeDtypeStruct((8, ka // 2, cs), _ACT),
    )


def _after(after):
    return ([], []) if after is None else ([after], [_ANY])


def _rms_fwd(name, x, g, after=None):
    t, d = x.shape
    tm = _row_tile(t, d)
    more, more_specs = _after(after)

    def body(x_ref, g_ref, *refs):
        h_ref = refs[-1]
        xv = x_ref[...]
        r = lax.rsqrt(jnp.mean(xv * xv, axis=-1, keepdims=True) + EPS)
        h_ref[...] = (xv * r * g_ref[...]).astype(h_ref.dtype)

    return pl.pallas_call(
        body, name=name, grid=(t // tm,),
        in_specs=[pl.BlockSpec((tm, d), lambda i: (i, 0)), pl.BlockSpec((1, d), lambda i: (0, 0))] + more_specs,
        out_specs=pl.BlockSpec((tm, d), lambda i: (i, 0)), out_shape=jax.ShapeDtypeStruct((t, d), _ACT),
        compiler_params=_params(1),
    )(x, g, *more)


def _fused_rows(name, a, b, product, a_spec, tm, extras, extra_specs, out_shape, out_specs, epilogue):
    ne = len(extras)

    def body(a_ref, b_ref, *refs):
        epilogue(product(a_ref, b_ref), refs[:ne], refs[ne:])

    m = extras[0].shape[0]
    return pl.pallas_call(
        body, name=name, grid=(m // tm,),
        in_specs=[a_spec, pl.BlockSpec(b.shape, lambda i: (0, 0), pipeline_mode=pl.Buffered(1)), *extra_specs],
        out_specs=out_specs, out_shape=out_shape, compiler_params=_params(1),
    )(a, b, *extras)


def _proj_residual_norm(name, a, b, res, g, after=None):
    m, k = a.shape
    d = b.shape[1]
    tm = _row_tile(m, max(k, d))

    def epilogue(p, ins, outs):
        xv = p + ins[0][...]
        outs[0][...] = xv
        r = lax.rsqrt(jnp.mean(xv * xv, axis=-1, keepdims=True) + EPS)
        outs[1][...] = (xv * r * ins[1][...]).astype(outs[1].dtype)

    row = pl.BlockSpec((tm, d), lambda i: (i, 0))
    return _fused_rows(
        name, a, b, lambda a_ref, b_ref: _dot(a_ref[...], b_ref[...], _NN), pl.BlockSpec((tm, k), lambda i: (i, 0)), tm,
        [res, g] + _after(after)[0], [row, pl.BlockSpec((1, d), lambda i: (0, 0))] + _after(after)[1],
        [jax.ShapeDtypeStruct((m, d), f32), jax.ShapeDtypeStruct((m, d), _ACT)], [row, row], epilogue)


def _dproj_rms_bwd(name, a, b, x, g, dres, storage_copy=True, after=None):
    m, d = x.shape
    if a.ndim == 3:
        nh, _, kh = a.shape
        tm = _row_tile(m, nh * kh)
        a_spec = pl.BlockSpec((nh, tm, kh), lambda i: (0, i, 0))

        def product(a_ref, b_ref):
            p = _dot(a_ref[0], b_ref[:, 0:kh], _NT)
            for h in range(1, nh):
                p = p + _dot(a_ref[h], b_ref[:, h * kh:(h + 1) * kh], _NT)
            return p
    else:
        tm = _row_tile(m, max(a.shape[1], d))
        a_spec = pl.BlockSpec((tm, a.shape[1]), lambda i: (i, 0))

        def product(a_ref, b_ref):
            return _dot(a_ref[...], b_ref[...], _NT)

    def epilogue(dhv, ins, outs):
        x_ref, g_ref, dres_ref = ins[:3]
        dg_ref = outs[-1]

        @pl.when(pl.program_id(0) == 0)
        def _():
            dg_ref[...] = jnp.zeros_like(dg_ref)

        xv = x_ref[...]
        r = lax.rsqrt(jnp.mean(xv * xv, axis=-1, keepdims=True) + EPS)
        xn = xv * r
        dxn = dhv * g_ref[...]
        dx = r * (dxn - xn * jnp.mean(dxn * xn, axis=-1, keepdims=True)) + dres_ref[...]
        outs[0][...] = dx
        if storage_copy:
            outs[1][...] = dx.astype(outs[1].dtype)
        dg_ref[...] += jnp.sum(dhv * xn, axis=0, keepdims=True)

    row = pl.BlockSpec((tm, d), lambda i: (i, 0))
    vec = pl.BlockSpec((1, d), lambda i: (0, 0))
    copies = [jax.ShapeDtypeStruct((m, d), _ACT)] if storage_copy else []
    return _fused_rows(
        name, a, b, product, a_spec, tm, [x, g, dres] + _after(after)[0], [row, vec, row] + _after(after)[1],
        [jax.ShapeDtypeStruct((m, d), f32)] + copies + [jax.ShapeDtypeStruct((1, d), f32)],
        [row] * (1 + len(copies)) + [vec], epilogue)


def _proj_loss_bwd(name, a, b, res, g, tgt):
    m, k = a.shape
    d = b.shape[1]
    tm = _row_tile(m, max(k, d))

    def epilogue(p, ins, outs):
        res_ref, g_ref, t_ref = ins
        dx_ref, dxb_ref, dg_ref, loss_ref = outs

        @pl.when(pl.program_id(0) == 0)
        def _():
            dg_ref[...] = jnp.zeros_like(dg_ref)
            loss_ref[...] = jnp.zeros_like(loss_ref)

        xv = p + res_ref[...]
        gv = g_ref[...]
        r = lax.rsqrt(jnp.mean(xv * xv, axis=-1, keepdims=True) + EPS)
        xn = xv * r
        err = xn * gv - t_ref[...]
        loss_ref[...] += 0.5 * jnp.sum(jnp.mean(err * err, axis=-1, keepdims=True), axis=0, keepdims=True)
        dout = err * (1.0 / d)
        dxn = dout * gv
        dx = r * (dxn - xn * jnp.mean(dxn * xn, axis=-1, keepdims=True))
        dx_ref[...] = dx
        dxb_ref[...] = dx.astype(dxb_ref.dtype)
        dg_ref[...] += jnp.sum(dout * xn, axis=0, keepdims=True)

    row = pl.BlockSpec((tm, d), lambda i: (i, 0))
    vec = pl.BlockSpec((1, d), lambda i: (0, 0))
    return _fused_rows(
        name, a, b, lambda a_ref, b_ref: _dot(a_ref[...], b_ref[...], _NN), pl.BlockSpec((tm, k), lambda i: (i, 0)), tm,
        [res, g, tgt], [row, vec, row],
        [jax.ShapeDtypeStruct((m, d), f32), jax.ShapeDtypeStruct((m, d), _ACT), jax.ShapeDtypeStruct((1, d), f32),
         jax.ShapeDtypeStruct((1, 1), f32)],
        [row, row, vec, pl.BlockSpec((1, 1), lambda i: (0, 0))], epilogue)


def _rms_gain_grad(name, x, dh):
    t, d = x.shape
    tm = _row_tile(t)

    def body(x_ref, dh_ref, dg_ref):
        @pl.when(pl.program_id(0) == 0)
        def _():
            dg_ref[...] = jnp.zeros_like(dg_ref)

        xv = x_ref[...]
        r = lax.rsqrt(jnp.mean(xv * xv, axis=-1, keepdims=True) + EPS)
        dg_ref[...] += jnp.sum(dh_ref[...] * (xv * r), axis=0, keepdims=True)

    row = pl.BlockSpec((tm, d), lambda i: (i, 0))
    return pl.pallas_call(
        body, name=name, grid=(t // tm,), in_specs=[row, row], out_specs=pl.BlockSpec((1, d), lambda i: (0, 0)),
        out_shape=jax.ShapeDtypeStruct((1, d), f32), compiler_params=_params(1),
    )(x, dh)


_CONV_ROWS = 512
_CHUNK = 64
_HALO = 32


def _pool_counts(pos, w):
    return jnp.minimum(pos + 1.0, float(w))


def _rows_from(win, start, rows):
    if start % 8 == 0:
        return win[start:start + rows, :]
    n = win.shape[0]
    return pltpu.roll(win, n - start % 8, axis=0)[start - start % 8:start - start % 8 + rows, :]


def _tap_rows(buf, starts, rows):
    for residue in range(8):
        group = [(k, s) for k, s in starts.items() if s % 8 == residue]
        if group:
            lo = min(s for _, s in group) - residue
            hi = max(s for _, s in group) - residue + rows + (8 if residue else 0)
            win = buf[lo:hi, :]
            if residue:
                win = pltpu.roll(win, hi - lo - residue, axis=0)
            for k, s in group:
                yield k, win[s - residue - lo:s - residue - lo + rows, :]


def _mix_fwd(u, cw, cb, lg, lb, pw, ps, seq):
    t, c3 = u.shape
    c = c3 // 3
    kw = 31
    tm = min(_CONV_ROWS, seq)
    tps = seq // tm
    gd = c // len(POOL_WINDOWS)

    def body(u_ref, uh_ref, cw_ref, cb_ref, lg_ref, lb_ref, pw_ref, ps_ref, y_ref, hc_ref, hgbuf, pbuf):
        i = pl.program_id(0)
        keep = jnp.where(i % tps == 0, 0.0, 1.0)
        um = u_ref[...].astype(f32)
        uh = uh_ref[...].astype(f32) * keep
        hgbuf[0:_HALO, :] = uh[:, 0:c] * _sigmoid(uh[:, c:2 * c])
        hgbuf[_HALO:_HALO + tm, :] = um[:, 0:c] * _sigmoid(um[:, c:2 * c])
        pbuf[0:_HALO, :] = uh[:, 2 * c:]
        pbuf[_HALO:_HALO + tm, :] = um[:, 2 * c:]
        for r0 in range(0, tm, _CHUNK):
            acc = jnp.broadcast_to(cb_ref[...], (_CHUNK, c))
            for k, rows in _tap_rows(hgbuf, {k: r0 + _HALO - (kw - 1) + k for k in range(kw)}, _CHUNK):
                acc = acc + cw_ref[k:k + 1, :] * rows
            hc_ref[r0:r0 + _CHUNK, :] = acc
            mu = jnp.mean(acc, axis=-1, keepdims=True)
            xc = acc - mu
            var = jnp.mean(xc * xc, axis=-1, keepdims=True)
            hl = xc * lax.rsqrt(var + EPS) * lg_ref[...] + lb_ref[...]
            y_ref[r0:r0 + _CHUNK, 0:c] = (hl * _sigmoid(hl)).astype(y_ref.dtype)
        pos = ((i % tps) * tm).astype(f32) + lax.broadcasted_iota(jnp.int32, (tm, 1), 0).astype(f32)
        for gi, w in enumerate(POOL_WINDOWS):
            sl = slice(gi * gd, (gi + 1) * gd)
            v = pbuf[_HALO:_HALO + tm, sl]
            s = v
            for j in range(1, w):
                s = s + pbuf[_HALO - j:_HALO - j + tm, sl]
            pooled = s / _pool_counts(pos, w) - v
            mixed = _dot(pooled.astype(_ACT), pw_ref[gi].astype(_ACT), _NN)
            y_ref[:, c + gi * gd:c + (gi + 1) * gd] = (mixed * ps_ref[:, sl]).astype(y_ref.dtype)

    hb = tm // _HALO
    full = lambda shape: pl.BlockSpec(shape, lambda i: (0,) * len(shape))
    return pl.pallas_call(
        body, name="mix_fwd", grid=(t // tm,),
        in_specs=[pl.BlockSpec((tm, c3), lambda i: (i, 0)),
                  pl.BlockSpec((_HALO, c3), lambda i: (jnp.maximum(i * hb - 1, 0), 0)),
                  full((_HALO, c)), full((1, c)), full((1, c)), full((1, c)), full((len(POOL_WINDOWS), gd, gd)), full((1, c))],
        out_specs=[pl.BlockSpec((tm, 2 * c), lambda i: (i, 0)), pl.BlockSpec((tm, c), lambda i: (i, 0))],
        out_shape=[jax.ShapeDtypeStruct((t, 2 * c), _ACT), jax.ShapeDtypeStruct((t, c), f32)],
        scratch_shapes=[pltpu.VMEM((_HALO + tm, c), f32), pltpu.VMEM((_HALO + tm, c), f32)],
        compiler_params=_params(1),
    )(u, u, cw, cb, lg, lb, pw, ps)


def _mix_bwd_norm(hc, dy, lg, lb, after):
    t, c = hc.shape
    tm = _row_tile(t, c)

    def body(hc_ref, dy_ref, lg_ref, lb_ref, after_ref, dhc_ref, sums_ref):
        @pl.when(pl.program_id(0) == 0)
        def _():
            sums_ref[...] = jnp.zeros_like(sums_ref)

        hcv = hc_ref[...]
        mu = jnp.mean(hcv, axis=-1, keepdims=True)
        xc = hcv - mu
        rstd = lax.rsqrt(jnp.mean(xc * xc, axis=-1, keepdims=True) + EPS)
        n = xc * rstd
        hl = n * lg_ref[...] + lb_ref[...]
        sg = _sigmoid(hl)
        dhl = dy_ref[...].astype(f32) * (sg * (1.0 + hl * (1.0 - sg)))
        dn = dhl * lg_ref[...]
        dhc = rstd * (dn - jnp.mean(dn, axis=-1, keepdims=True) - n * jnp.mean(dn * n, axis=-1, keepdims=True))
        dhc_ref[...] = dhc
        sums_ref[0:1, :] += jnp.sum(dhl * n, axis=0, keepdims=True)
        sums_ref[1:2, :] += jnp.sum(dhl, axis=0, keepdims=True)
        sums_ref[2:3, :] += jnp.sum(dhc, axis=0, keepdims=True)

    row = pl.BlockSpec((tm, c), lambda i: (i, 0))
    vec = pl.BlockSpec((1, c), lambda i: (0, 0))
    return pl.pallas_call(
        body, name="mix_bwd_norm", grid=(t // tm,), in_specs=[row, row, vec, vec, _ANY],
        out_specs=[row, pl.BlockSpec((8, c), lambda i: (0, 0))],
        out_shape=[jax.ShapeDtypeStruct((t, c), f32), jax.ShapeDtypeStruct((8, c), f32)],
        compiler_params=_params(1),
    )(hc, dy, lg, lb, after)


def _mix_bwd_taps(u, dhc, dy, cw, pw, ps, seq):
    t, c3 = u.shape
    c = c3 // 3
    kw = 31
    tm = min(_CONV_ROWS, seq)
    tps = seq // tm
    ng = len(POOL_WINDOWS)
    gd = c // ng
    nh = 16

    def body(u_ref, uh_ref, dhc_ref, dhcn_ref, dy_ref, dyn_ref, cw_ref, pw_ref, ps_ref,
             du_ref, dcw_ref, dps_ref, dpw_ref, hgbuf, dcbuf, pbuf, dpbuf):
        i = pl.program_id(0)
        keep_prev = jnp.where(i % tps == 0, 0.0, 1.0)
        keep_next = jnp.where(i % tps == tps - 1, 0.0, 1.0)

        @pl.when(i == 0)
        def _():
            dcw_ref[...] = jnp.zeros_like(dcw_ref)
            dps_ref[...] = jnp.zeros_like(dps_ref)
            dpw_ref[...] = jnp.zeros_like(dpw_ref)

        uh = uh_ref[...].astype(f32) * keep_prev
        hgbuf[0:_HALO, :] = uh[:, 0:c] * _sigmoid(uh[:, c:2 * c])
        pbuf[0:_HALO, :] = uh[:, 2 * c:]
        um = u_ref[...].astype(f32)
        hgbuf[_HALO:_HALO + tm, :] = um[:, 0:c] * _sigmoid(um[:, c:2 * c])
        pbuf[_HALO:_HALO + tm, :] = um[:, 2 * c:]
        dcbuf[0:tm, :] = dhc_ref[...]
        dcbuf[tm:tm + _HALO, :] = dhcn_ref[...] * keep_next
        tap_sums = [None] * kw
        for r0 in range(0, tm, _CHUNK):
            dh = dcbuf[r0:r0 + _CHUNK, :]
            acc = jnp.zeros((_CHUNK, c), f32)
            for k, rows in _tap_rows(hgbuf, {k: r0 + _HALO - (kw - 1) + k for k in range(kw)}, _CHUNK):
                part = (dh * rows).reshape(_CHUNK // 8, 8, c).sum(axis=0)
                tap_sums[k] = part if tap_sums[k] is None else tap_sums[k] + part
            for k, rows in _tap_rows(dcbuf, {k: r0 + (kw - 1) - k for k in range(kw)}, _CHUNK):
                acc = acc + cw_ref[k:k + 1, :] * rows
            val = u_ref[r0:r0 + _CHUNK, 0:c].astype(f32)
            sg = _sigmoid(u_ref[r0:r0 + _CHUNK, c:2 * c].astype(f32))
            du_ref[r0:r0 + _CHUNK, 0:c] = (acc * sg).astype(du_ref.dtype)
            du_ref[r0:r0 + _CHUNK, c:2 * c] = (acc * val * sg * (1.0 - sg)).astype(du_ref.dtype)
        for k in range(kw):
            dcw_ref[k:k + 1, :] += jnp.sum(tap_sums[k], axis=0, keepdims=True)
        base = ((i % tps) * tm).astype(f32)
        pos = base + lax.broadcasted_iota(jnp.int32, (tm, 1), 0).astype(f32)
        pos_next = base + float(tm) + lax.broadcasted_iota(jnp.int32, (nh, 1), 0).astype(f32)
        for gi, w in enumerate(POOL_WINDOWS):
            sl = slice(gi * gd, (gi + 1) * gd)
            v = pbuf[_HALO:_HALO + tm, sl]
            s = v
            for j in range(1, w):
                s = s + pbuf[_HALO - j:_HALO - j + tm, sl]
            cnt = _pool_counts(pos, w)
            pooled = (s / cnt - v).astype(_ACT)
            pwg = pw_ref[gi].astype(_ACT)
            mixed = _dot(pooled, pwg, _NN)
            dyp = dy_ref[:, sl].astype(f32)
            dps_ref[0:1, sl] += jnp.sum(dyp * mixed, axis=0, keepdims=True)
            dmix = (dyp * ps_ref[:, sl]).astype(_ACT)
            dpw_ref[gi] += _dot(pooled, dmix, _TN)
            dmix_next = (dyn_ref[:, sl].astype(f32) * ps_ref[:, sl] * keep_next).astype(_ACT)
            dpool = _dot(dmix, pwg, _NT)
            dpbuf[0:tm, sl] = dpool / cnt
            dpbuf[tm:tm + nh, sl] = _dot(dmix_next, pwg, _NT) / _pool_counts(pos_next, w)
            acc = -dpool
            for j in range(w):
                acc = acc + dpbuf[j:j + tm, sl]
            du_ref[:, 2 * c + gi * gd:2 * c + (gi + 1) * gd] = acc.astype(du_ref.dtype)

    hb = tm // _HALO
    n_halo = t // _HALO
    n_nh = t // nh
    full = lambda shape: pl.BlockSpec(shape, lambda i: (0,) * len(shape))
    return pl.pallas_call(
        body, name="mix_bwd_taps", grid=(t // tm,),
        in_specs=[pl.BlockSpec((tm, c3), lambda i: (i, 0)),
                  pl.BlockSpec((_HALO, c3), lambda i: (jnp.maximum(i * hb - 1, 0), 0)),
                  pl.BlockSpec((tm, c), lambda i: (i, 0)),
                  pl.BlockSpec((_HALO, c), lambda i: (jnp.minimum((i + 1) * hb, n_halo - 1), 0)),
                  pl.BlockSpec((tm, c), lambda i: (i, 1)),
                  pl.BlockSpec((nh, c), lambda i: (jnp.minimum((i + 1) * (tm // nh), n_nh - 1), 1)),
                  full((_HALO, c)), full((ng, gd, gd)), full((1, c))],
        out_specs=[pl.BlockSpec((tm, c3), lambda i: (i, 0)), full((_HALO, c)), full((8, c)), full((ng, gd, gd))],
        out_shape=[jax.ShapeDtypeStruct((t, c3), _ACT), jax.ShapeDtypeStruct((_HALO, c), f32),
                   jax.ShapeDtypeStruct((8, c), f32), jax.ShapeDtypeStruct((ng, gd, gd), f32)],
        scratch_shapes=[pltpu.VMEM((_HALO + tm, c), f32), pltpu.VMEM((tm + _HALO, c), f32),
                        pltpu.VMEM((_HALO + tm, c), f32), pltpu.VMEM((tm + nh, c), f32)],
        compiler_params=_params(1),
    )(u, u, dhc, dhc, dy, dy, cw, pw, ps)


def _attn_fwd(q, kv, n_seq, seq, n_mem):
    t, d = q.shape
    dh = d // XATTN_HEADS
    tq = min(1024, seq)
    nq = seq // tq
    scale = dh ** -0.5

    def body(q_ref, kv_ref, o_ref):
        for h in range(XATTN_HEADS):
            cols = slice(h * dh, (h + 1) * dh)
            s = _dot(q_ref[:, cols], kv_ref[:, cols], _NT) * scale
            e = jnp.exp(s - jnp.max(s, axis=-1, keepdims=True))
            p = e / jnp.sum(e, axis=-1, keepdims=True)
            o_ref[:, cols] = _dot(p.astype(_ACT), kv_ref[:, d + h * dh:d + (h + 1) * dh], _NN).astype(o_ref.dtype)

    qs = pl.BlockSpec((tq, d), lambda b, i: (b * nq + i, 0))
    return pl.pallas_call(
        body, name="attn_fwd", grid=(n_seq, nq), in_specs=[qs, pl.BlockSpec((n_mem, 2 * d), lambda b, i: (b, 0))],
        out_specs=qs, out_shape=jax.ShapeDtypeStruct((t, d), _ACT), compiler_params=_params(2),
    )(q, kv)


def _attn_bwd(q, kv, do, n_seq, seq, n_mem):
    t, d = q.shape
    dh = d // XATTN_HEADS
    tq = min(1024, seq)
    nq = seq // tq
    scale = dh ** -0.5

    def body(q_ref, kv_ref, do_ref, dq_ref, dkv_ref, acc):
        i = pl.program_id(1)

        @pl.when(i == 0)
        def _():
            acc[...] = jnp.zeros_like(acc)

        for h in range(XATTN_HEADS):
            cols = slice(h * dh, (h + 1) * dh)
            vcols = slice(d + h * dh, d + (h + 1) * dh)
            qv = q_ref[:, cols]
            kh = kv_ref[:, cols]
            dov = do_ref[:, cols]
            s = _dot(qv, kh, _NT) * scale
            e = jnp.exp(s - jnp.max(s, axis=-1, keepdims=True))
            p = e / jnp.sum(e, axis=-1, keepdims=True)
            dp = _dot(dov, kv_ref[:, vcols], _NT)
            ds = (p * (dp - jnp.sum(dp * p, axis=-1, keepdims=True)) * scale).astype(_ACT)
            dq_ref[:, cols] = _dot(ds, kh, _NN).astype(dq_ref.dtype)
            acc[:, cols] += _dot(ds, qv, _TN)
            acc[:, vcols] += _dot(p.astype(_ACT), dov, _TN)

        @pl.when(i == nq - 1)
        def _():
            dkv_ref[...] = acc[...].astype(dkv_ref.dtype)

    qs = pl.BlockSpec((tq, d), lambda b, i: (b * nq + i, 0))
    ms = pl.BlockSpec((n_mem, 2 * d), lambda b, i: (b, 0))
    return pl.pallas_call(
        body, name="attn_bwd", grid=(n_seq, nq), in_specs=[qs, ms, qs], out_specs=[qs, ms],
        out_shape=[jax.ShapeDtypeStruct((t, d), _ACT), jax.ShapeDtypeStruct((n_seq * n_mem, 2 * d), _ACT)],
        scratch_shapes=[pltpu.VMEM((n_mem, 2 * d), f32)], compiler_params=_params(2),
    )(q, kv, do)


_FFN_ROWS = 2048
_FFN_COLS = 256
_FFN_HALO = 16


def _window(buf, g, start, rows):
    return buf[g, pl.ds(start, rows + 8), :]


def _taps3(win, rows):
    return [_rows_from(win, 6 + k, rows) for k in range(3)]


def _conv3(b_ref, w_ref, taps):
    acc = b_ref[...] + w_ref[0:1, :] * taps[0]
    for k in (1, 2):
        acc = acc + w_ref[k:k + 1, :] * taps[k]
    return acc


def _ffn_gate_fwd(up, fw, fb, seq):
    _, t, f = up.shape
    tm = min(_FFN_ROWS, seq)
    tps = seq // tm
    tc = _FFN_COLS
    nc = f // tc
    hl = _FFN_HALO

    def body(up_ref, uph_ref, wg_ref, wv_ref, bg_ref, bv_ref, a_ref, uc_ref):
        i = pl.program_id(1)
        before = uph_ref[...]
        before = jnp.where(i % tps == 0, jnp.zeros_like(before), before)

        def chunk(r0, wins):
            conv = []
            for g, (w_ref, b_ref) in enumerate(((wg_ref, bg_ref), (wv_ref, bv_ref))):
                conv.append(_conv3(b_ref, w_ref, _taps3(wins[g].astype(f32)[hl - 8:, :], _CHUNK)))
                uc_ref[g, pl.ds(r0, _CHUNK), :] = conv[g].astype(uc_ref.dtype)
            gate, val = conv
            a_ref[pl.ds(r0, _CHUNK), :] = (gate * _sigmoid(gate) * val).astype(a_ref.dtype)

        chunk(0, [jnp.concatenate([before[g], up_ref[g, 0:_CHUNK, :]], axis=0) for g in range(2)])

        def later(ci, carry):
            r0 = pl.multiple_of(ci * _CHUNK, _CHUNK)
            chunk(r0, [up_ref[g, pl.ds(r0 - hl, _CHUNK + hl), :] for g in range(2)])
            return carry

        lax.fori_loop(1, tm // _CHUNK, later, 0)

    hb = tm // hl
    return pl.pallas_call(
        body, name="ffn_gate_fwd", grid=(nc, t // tm),
        in_specs=[pl.BlockSpec((2, tm, tc), lambda j, i: (0, i, j)),
                  pl.BlockSpec((2, hl, tc), lambda j, i: (0, jnp.maximum(i * hb - 1, 0), j)),
                  pl.BlockSpec((8, tc), lambda j, i: (0, j)), pl.BlockSpec((8, tc), lambda j, i: (0, nc + j)),
                  pl.BlockSpec((1, tc), lambda j, i: (0, j)), pl.BlockSpec((1, tc), lambda j, i: (0, nc + j))],
        out_specs=[pl.BlockSpec((tm, tc), lambda j, i: (i, j)), pl.BlockSpec((2, tm, tc), lambda j, i: (0, i, j))],
        out_shape=[jax.ShapeDtypeStruct((t, f), _ACT), jax.ShapeDtypeStruct((2, t, f), _ACT)], compiler_params=_params(2),
    )(up, up, fw, fw, fb, fb)


def _ffn_gate_bwd(up, uc, da, fw, seq):
    _, t, f = up.shape
    tm = min(_FFN_ROWS, seq)
    tps = seq // tm
    tc = _FFN_COLS
    nc = f // tc
    hl = _FFN_HALO

    def body(up_ref, uph_ref, uc_ref, ucn_ref, da_ref, dan_ref, wg_ref, wv_ref, dup_ref, sg_ref, sv_ref, dbuf, sums):
        i = pl.program_id(1)
        at_end = i % tps == tps - 1

        @pl.when(i == 0)
        def _():
            sg_ref[...] = jnp.zeros_like(sg_ref)
            sv_ref[...] = jnp.zeros_like(sv_ref)

        sums[...] = jnp.zeros_like(sums)
        before = uph_ref[...]
        before = jnp.where(i % tps == 0, jnp.zeros_like(before), before)
        w_refs = (wg_ref, wv_ref)

        def grads(r0, rows, conv, dav):
            gate, val = [v.astype(f32) for v in conv]
            sg = _sigmoid(gate)
            douts = (dav * val * (sg * (1.0 + gate * (1.0 - sg))), dav * (gate * sg))
            for g in range(2):
                dbuf[g, pl.ds(r0, rows), :] = douts[g]
            return douts

        def count(douts, wins):
            for g in range(2):
                taps = _taps3(wins[g].astype(f32)[hl - 8:, :], _CHUNK)
                sums[g, 0] += douts[g].reshape(_CHUNK // 8, 8, tc).sum(axis=0)
                for k in range(3):
                    sums[g, 1 + k] += (douts[g] * taps[k]).reshape(_CHUNK // 8, 8, tc).sum(axis=0)

        count(grads(0, _CHUNK, [uc_ref[g, 0:_CHUNK, :] for g in range(2)], da_ref[0:_CHUNK, :].astype(f32)),
              [jnp.concatenate([before[g], up_ref[g, 0:_CHUNK, :]], axis=0) for g in range(2)])

        def first(ci, carry):
            r0 = pl.multiple_of(ci * _CHUNK, _CHUNK)
            douts = grads(r0, _CHUNK, [uc_ref[g, pl.ds(r0, _CHUNK), :] for g in range(2)],
                          da_ref[pl.ds(r0, _CHUNK), :].astype(f32))
            count(douts, [up_ref[g, pl.ds(r0 - hl, _CHUNK + hl), :] for g in range(2)])
            return carry

        lax.fori_loop(1, tm // _CHUNK, first, 0)
        da_after = dan_ref[...].astype(f32)
        grads(tm, hl, [ucn_ref[g] for g in range(2)], jnp.where(at_end, jnp.zeros_like(da_after), da_after))

        def second(ci, carry):
            r0 = pl.multiple_of(ci * _CHUNK, _CHUNK)
            for g in range(2):
                win = _window(dbuf, g, r0, _CHUNK)
                acc = jnp.zeros((_CHUNK, tc), f32)
                for k in range(3):
                    acc = acc + w_refs[g][k:k + 1, :] * _rows_from(win, 2 - k, _CHUNK)
                dup_ref[g, pl.ds(r0, _CHUNK), :] = acc.astype(dup_ref.dtype)
            return carry

        lax.fori_loop(0, tm // _CHUNK, second, 0)
        for g, s_ref in enumerate((sg_ref, sv_ref)):
            for r in range(4):
                s_ref[r:r + 1, :] += jnp.sum(sums[g, r], axis=0, keepdims=True)

    hb = tm // hl
    n_halo = t // hl
    return pl.pallas_call(
        body, name="ffn_gate_bwd", grid=(nc, t // tm),
        in_specs=[pl.BlockSpec((2, tm, tc), lambda j, i: (0, i, j)),
                  pl.BlockSpec((2, hl, tc), lambda j, i: (0, jnp.maximum(i * hb - 1, 0), j)),
                  pl.BlockSpec((2, tm, tc), lambda j, i: (0, i, j)),
                  pl.BlockSpec((2, hl, tc), lambda j, i: (0, jnp.minimum((i + 1) * hb, n_halo - 1), j)),
                  pl.BlockSpec((tm, tc), lambda j, i: (i, j)),
                  pl.BlockSpec((hl, tc), lambda j, i: (jnp.minimum((i + 1) * hb, n_halo - 1), j)),
                  pl.BlockSpec((8, tc), lambda j, i: (0, j)), pl.BlockSpec((8, tc), lambda j, i: (0, nc + j))],
        out_specs=[pl.BlockSpec((2, tm, tc), lambda j, i: (0, i, j)),
                   pl.BlockSpec((8, tc), lambda j, i: (0, j)), pl.BlockSpec((8, tc), lambda j, i: (0, j))],
        out_shape=[jax.ShapeDtypeStruct((2, t, f), _ACT), jax.ShapeDtypeStruct((8, f), f32), jax.ShapeDtypeStruct((8, f), f32)],
        scratch_shapes=[pltpu.VMEM((2, tm + hl, tc), f32), pltpu.VMEM((2, 4, 8, tc), f32)],
        compiler_params=_params(2),
    )(up, up, uc, uc, da, da, fw, fw)


def _adamw_math(w, g, m, v):
    m = ADAM_B1 * m + (1.0 - ADAM_B1) * g
    v = ADAM_B2 * v + (1.0 - ADAM_B2) * (g * g)
    m_hat = m / (1.0 - ADAM_B1 ** ADAM_STEP)
    v_hat = v / (1.0 - ADAM_B2 ** ADAM_STEP)
    delta = -ADAM_LR * (m_hat / (jnp.sqrt(v_hat) + ADAM_EPS) + ADAM_WD * w)
    return delta, m, v


def _adamw_shards(quads):
    n = len(quads)
    steps = 8

    def body(*refs):
        for p in range(n):
            w_ref, g_ref, m_ref, v_ref = refs[4 * p:4 * p + 4]
            go_ref, d_ref, mo_ref, vo_ref = refs[4 * n + 4 * p:4 * n + 4 * p + 4]
            gv = g_ref[...]
            d, mn, vn = _adamw_math(w_ref[...], gv, m_ref[...], v_ref[...])
            go_ref[...] = gv
            d_ref[...] = d
            mo_ref[...] = mn
            vo_ref[...] = vn

    in_specs, out_specs, out_shape = [], [], []
    for w, _, _, _ in quads:
        _, r, c = w.shape
        s3 = pl.BlockSpec((None, r // steps, c), lambda i: (0, i, 0))
        in_specs += [s3, pl.BlockSpec((r // steps, c), lambda i: (i, 0)), s3, s3]
        out_specs += [s3] * 4
        out_shape += [jax.ShapeDtypeStruct(w.shape, f32)] * 4
    outs = pl.pallas_call(
        body, name="adamw_shards", grid=(steps,), in_specs=in_specs, out_specs=out_specs, out_shape=out_shape,
        compiler_params=_params(1),
    )(*[a for q in quads for a in q])
    return [tuple(outs[4 * p:4 * p + 4]) for p in range(n)]


def _adamw_small(quads):
    n = len(quads)

    def body(*refs):
        ins, outs = refs[:4 * n], refs[4 * n:]
        for p in range(n):
            w_ref, g_ref, m_ref, v_ref = ins[4 * p:4 * p + 4]
            d, mn, vn = _adamw_math(w_ref[...], g_ref[...], m_ref[...], v_ref[...])
            outs[3 * p][...] = d
            outs[3 * p + 1][...] = mn
            outs[3 * p + 2][...] = vn

    flat = [a for q in quads for a in q]
    shapes = [jax.ShapeDtypeStruct(q[0].shape, f32) for q in quads for _ in range(3)]
    outs = pl.pallas_call(
        body, name="adamw_small", in_specs=[_VMEM] * (4 * n), out_specs=[_VMEM] * (3 * n), out_shape=shapes,
        compiler_params=pltpu.CompilerParams(vmem_limit_bytes=_VMEM_LIMIT_BYTES),
    )(*flat)
    return [tuple(outs[3 * p:3 * p + 3]) for p in range(n)]


def _sum_partials(name, place, grads, got):
    nw = len(grads)
    steps = 2

    def body(place_ref, *refs):
        for w in range(nw):
            own_ref, got_ref, f_ref = refs[w], refs[nw + w], refs[2 * nw + w]
            s = own_ref[...].astype(f32)
            for k in range(got[w].shape[0]):
                s = s + got_ref[k].astype(f32)
            f_ref[...] = s

    own_specs, got_specs, out_specs, out_shape = [], [], [], []
    for g, l in zip(grads, got):
        _, r, c = g.shape
        tr = r // steps
        own_specs.append(pl.BlockSpec((None, tr, c), lambda i, p: (2 * p[0] + p[1], i, 0)))
        got_specs.append(pl.BlockSpec((l.shape[0], tr, c), lambda i, p: (0, i, 0)))
        out_specs.append(pl.BlockSpec((None, tr, c), lambda i, p: (p[1], i, 0)))
        out_shape.append(jax.ShapeDtypeStruct((2, r, c), f32))
    grid_spec = pltpu.PrefetchScalarGridSpec(num_scalar_prefetch=1, grid=(steps,), in_specs=own_specs + got_specs, out_specs=out_specs)
    return pl.pallas_call(body, name=name, grid_spec=grid_spec, out_shape=out_shape,
                          compiler_params=_params(1))(place, *grads, *got)


def _place():
    return lax.axis_index("x"), lax.axis_index("y"), lax.axis_index("c")


def _other_chips(x, y):
    return [(1 - x, y), (x, 1 - y), (1 - x, 1 - y)]


def _remote(src, dst, send_sem, recv_sem, to):
    return pltpu.make_async_remote_copy(src_ref=src, dst_ref=dst, send_sem=send_sem, recv_sem=recv_sem,
                                        device_id=to, device_id_type=_MESH)


def _place_shards(name, place, shards, col_sharded, after=None):
    n = len(shards)
    steps = 4
    more, more_specs = _after(after)

    def body(place_ref, *refs):
        for src, dst in zip(refs[:n], refs[n + len(more):]):
            dst[...] = src[...].astype(dst.dtype)

    in_specs, out_specs, out_shape = [], [], []
    for w, col in zip(shards, col_sharded):
        r, cs = w.shape
        tr = r // steps
        in_specs.append(pl.BlockSpec((tr, cs), lambda i, p: (i, 0)))
        if col:
            out_specs.append(pl.BlockSpec((tr, cs), lambda i, p: (i, p[0])))
            out_shape.append(jax.ShapeDtypeStruct((r, 4 * cs), _ACT))
        else:
            out_specs.append(pl.BlockSpec((tr, cs), lambda i, p: (p[0] * steps + i, 0)))
            out_shape.append(jax.ShapeDtypeStruct((4 * r, cs), _ACT))
    grid_spec = pltpu.PrefetchScalarGridSpec(num_scalar_prefetch=1, grid=(steps,), in_specs=in_specs + more_specs,
                                            out_specs=out_specs)
    return pl.pallas_call(body, name=name, grid_spec=grid_spec, out_shape=out_shape,
                          compiler_params=_params(1))(place, *shards, *more)


def _shard_of(ref, col_sharded, s):
    rows, cols = ref.shape
    if col_sharded:
        return ref.at[:, pl.ds(s * (cols // 4), cols // 4)]
    return ref.at[pl.ds(s * (rows // 4), rows // 4), :]


def _part_of(ref, col_sharded, whole, s, h):
    if whole:
        return _shard_of(ref, col_sharded, s)
    rows, cols = ref.shape
    if col_sharded:
        return ref.at[pl.ds(h * (rows // 2), rows // 2), pl.ds(s * (cols // 4), cols // 4)]
    return ref.at[pl.ds((2 * s + h) * (rows // 8), rows // 8), :]


def _allgather_start(name, bufs, col_sharded, whole, groups):
    n = len(bufs)
    ng = len(groups)

    def body(*refs):
        out = refs[n:2 * n]
        sems = refs[2 * n:2 * n + 2 * ng]
        token = refs[2 * n + 2 * ng]
        x, y, c = _place()
        for g, members in enumerate(groups):
            for i, w in enumerate(members):
                mine = _part_of(out[w], col_sharded[w], whole[w], 2 * x + y, c)
                for j, chip in enumerate(_other_chips(x, y)):
                    _remote(mine, mine, sems[2 * g].at[3 * i + j], sems[2 * g + 1].at[3 * i + j], (*chip, c)).start()
        token[...] = jnp.zeros_like(token)

    sem_shapes = [pltpu.SemaphoreType.DMA((3 * len(m),)) for m in groups for _ in range(2)]
    outs = pl.pallas_call(
        body, name=name, in_specs=[_HBM] * n, out_specs=[_HBM] * n + [_SEM] * (2 * ng) + [_VMEM],
        out_shape=[pltpu.HBM(b.shape, b.dtype) for b in bufs] + sem_shapes + [jax.ShapeDtypeStruct((8, 128), f32)],
        input_output_aliases={i: i for i in range(n)},
        compiler_params=pltpu.CompilerParams(has_side_effects=_EFFECT),
    )(*[pltpu.with_memory_space_constraint(b, pltpu.HBM) for b in bufs])
    return list(outs[:n]), [(outs[n + 2 * g], outs[n + 2 * g + 1]) for g in range(ng)], outs[n + 2 * ng]


def _allgather_relay(name, bufs, col_sharded, whole, sems, after):
    n = len(bufs)

    def body(*refs):
        buf = refs[:n]
        send, recv = refs[n], refs[n + 1]
        out = refs[n + 3:2 * n + 3]
        to_sibling, from_sibling, token = refs[2 * n + 3:]
        token[...] = jnp.zeros_like(token)
        x, y, c = _place()
        for i in range(n):
            mine = _part_of(buf[i], col_sharded[i], whole[i], 2 * x + y, c)
            for j, chip in enumerate(_other_chips(x, y)):
                landed = _part_of(buf[i], col_sharded[i], whole[i], 2 * chip[0] + chip[1], c)
                cp = _remote(mine, landed, send.at[3 * i + j], recv.at[3 * i + j], (*chip, c))
                cp.wait_send()
                cp.wait_recv()
        for i in range(n):
            if not whole[i]:
                for j, chip in enumerate(_other_chips(x, y)):
                    landed = _part_of(out[i], col_sharded[i], False, 2 * chip[0] + chip[1], c)
                    _remote(landed, landed, to_sibling.at[3 * i + j], from_sibling.at[3 * i + j], (x, y, 1 - c)).start()

    outs = pl.pallas_call(
        body, name=name, in_specs=[_HBM] * n + [_SEM, _SEM, _ANY], out_specs=[_HBM] * n + [_SEM, _SEM, _VMEM],
        out_shape=[pltpu.HBM(b.shape, b.dtype) for b in bufs] + [pltpu.SemaphoreType.DMA((3 * n,))] * 2
        + [jax.ShapeDtypeStruct((8, 128), f32)],
        input_output_aliases={i: i for i in range(n)},
        compiler_params=pltpu.CompilerParams(has_side_effects=_EFFECT),
    )(*bufs, *sems, after)
    return list(outs[:n]), (outs[n], outs[n + 1]), outs[n + 2]


def _allgather_wait(name, bufs, col_sharded, whole, sems, after):
    n = len(bufs)

    def body(*refs):
        buf = refs[:n]
        to_sibling, from_sibling = refs[n], refs[n + 1]
        x, y, c = _place()
        for i in range(n):
            if not whole[i]:
                for j, chip in enumerate(_other_chips(x, y)):
                    sent = _part_of(buf[i], col_sharded[i], False, 2 * chip[0] + chip[1], c)
                    landed = _part_of(buf[i], col_sharded[i], False, 2 * chip[0] + chip[1], 1 - c)
                    cp = _remote(sent, landed, to_sibling.at[3 * i + j], from_sibling.at[3 * i + j], (x, y, 1 - c))
                    cp.wait_send()
                    cp.wait_recv()

    return pl.pallas_call(
        body, name=name, in_specs=[_HBM] * n + [_SEM, _SEM, _ANY], out_specs=[_HBM] * n,
        out_shape=[pltpu.HBM(b.shape, b.dtype) for b in bufs],
        input_output_aliases={i: i for i in range(n)},
        compiler_params=pltpu.CompilerParams(has_side_effects=_EFFECT),
    )(*bufs, *sems, after)


def _other_devices(x, y, c):
    flips = [(bx, by, bc) for bx in (0, 1) for by in (0, 1) for bc in (0, 1)][1:]
    return [(1 - x if bx else x, 1 - y if by else y, 1 - c if bc else c) for bx, by, bc in flips]


def _grad_exchange_start(name, grads):
    nw = len(grads)
    lands = [lax.empty((7,) + g.shape[1:], g.dtype) for g in grads]

    def body(*refs):
        src = refs[2 * nw:3 * nw]
        got = refs[3 * nw:4 * nw]
        send, recv, token = refs[4 * nw:]
        x, y, c = _place()
        for w in range(nw):
            for k, (px, py, pc) in enumerate(_other_devices(x, y, c)):
                _remote(src[w].at[4 * px + 2 * py + pc], got[w].at[k], send.at[7 * w + k], recv.at[7 * w + k], (px, py, pc)).start()
        token[...] = jnp.zeros_like(token)

    outs = pl.pallas_call(
        body, name=name, in_specs=[_HBM] * (2 * nw), out_specs=[_HBM] * (2 * nw) + [_SEM, _SEM, _VMEM],
        out_shape=[pltpu.HBM(a.shape, a.dtype) for a in list(grads) + lands]
        + [pltpu.SemaphoreType.DMA((7 * nw,)), pltpu.SemaphoreType.DMA((7 * nw,)), jax.ShapeDtypeStruct((8, 128), f32)],
        input_output_aliases={i: i for i in range(2 * nw)},
        compiler_params=pltpu.CompilerParams(has_side_effects=_EFFECT),
    )(*[pltpu.with_memory_space_constraint(a, pltpu.HBM) for a in list(grads) + lands])
    return list(outs[:nw]), list(outs[nw:2 * nw]), (outs[2 * nw], outs[2 * nw + 1]), outs[2 * nw + 2]


def _grad_exchange_wait(name, grads, got, sems, after):
    nw = len(grads)

    def body(*refs):
        src = refs[:nw]
        land = refs[nw:2 * nw]
        send, recv = refs[2 * nw], refs[2 * nw + 1]
        x, y, c = _place()
        for w in range(nw):
            for k, (px, py, pc) in enumerate(_other_devices(x, y, c)):
                cp = _remote(src[w].at[4 * px + 2 * py + pc], land[w].at[k], send.at[7 * w + k], recv.at[7 * w + k], (px, py, pc))
                cp.wait_send()
                cp.wait_recv()

    outs = pl.pallas_call(
        body, name=name, in_specs=[_HBM] * (2 * nw) + [_SEM, _SEM, _ANY], out_specs=[_HBM] * (2 * nw),
        out_shape=[pltpu.HBM(a.shape, a.dtype) for a in list(grads) + list(got)],
        input_output_aliases={i: i for i in range(2 * nw)},
        compiler_params=pltpu.CompilerParams(has_side_effects=_EFFECT),
    )(*grads, *got, *sems, after)
    return list(outs[:nw]), list(outs[nw:])


def _swap_halves_start(finals):
    nw = len(finals)

    def body(*refs):
        buf = refs[nw:2 * nw]
        send, recv, token = refs[2 * nw:]
        x, y, c = _place()
        for w in range(nw):
            _remote(buf[w].at[c], buf[w].at[c], send.at[w], recv.at[w], (x, y, 1 - c)).start()
        token[...] = jnp.zeros_like(token)

    outs = pl.pallas_call(
        body, name="rs_swap_start", in_specs=[_HBM] * nw, out_specs=[_HBM] * nw + [_SEM, _SEM, _VMEM],
        out_shape=[pltpu.HBM(g.shape, g.dtype) for g in finals] + [pltpu.SemaphoreType.DMA((nw,))] * 2
        + [jax.ShapeDtypeStruct((8, 128), f32)],
        input_output_aliases={i: i for i in range(nw)},
        compiler_params=pltpu.CompilerParams(has_side_effects=_EFFECT),
    )(*[pltpu.with_memory_space_constraint(g, pltpu.HBM) for g in finals])
    return list(outs[:nw]), (outs[nw], outs[nw + 1]), outs[nw + 2]


def _swap_halves_wait(bufs, sems, after):
    nw = len(bufs)

    def body(*refs):
        buf = refs[:nw]
        send, recv = refs[nw], refs[nw + 1]
        x, y, c = _place()
        for w in range(nw):
            cp = _remote(buf[w].at[c], buf[w].at[1 - c], send.at[w], recv.at[w], (x, y, 1 - c))
            cp.wait_send()
            cp.wait_recv()

    return pl.pallas_call(
        body, name="rs_swap_wait", in_specs=[_HBM] * nw + [_SEM, _SEM, _ANY], out_specs=[_HBM] * nw,
        out_shape=[pltpu.HBM(g.shape, g.dtype) for g in bufs],
        input_output_aliases={i: i for i in range(nw)},
        compiler_params=pltpu.CompilerParams(has_side_effects=_EFFECT),
    )(*bufs, *sems, after)


def _half_slices(shape, h):
    rows, cols = shape
    if cols % 256 == 0:
        return (slice(None), slice(h * (cols // 2), (h + 1) * (cols // 2)))
    return (slice(h * (rows // 2), (h + 1) * (rows // 2)), slice(None))


def _allreduce_small(parts, after):
    n = len(parts)

    def body(*refs):
        src = refs[:n]
        refs = refs[n + 1:]
        out = refs[:n]
        sib = refs[n:2 * n]
        chip_sum = refs[2 * n:3 * n]
        slots = refs[3 * n:4 * n]
        pair_send, pair_recv, ici_send, ici_recv, swap_send, swap_recv = refs[4 * n:]
        x, y, c = _place()
        me_chip = 2 * x + y
        chips = _other_chips(x, y)
        pairs = [_remote(src[a], sib[a], pair_send.at[a], pair_recv.at[a], (x, y, 1 - c)) for a in range(n)]
        for rc in pairs:
            rc.start()
        for a in range(n):
            pairs[a].wait_recv()
            chip_sum[a][...] = src[a][...] + sib[a][...]
        for h in (0, 1):
            @pl.when(c == h)
            def _():
                sends = []
                for a in range(n):
                    idx = _half_slices(parts[a].shape, h)
                    for j, chip in enumerate(chips):
                        rc = _remote(chip_sum[a].at[idx], slots[a].at[me_chip].at[idx], ici_send.at[3 * a + j], ici_recv.at[3 * a + j], (*chip, h))
                        rc.start()
                        sends.append(rc)
                    slots[a][(me_chip,) + idx] = chip_sum[a][idx]
                for a in range(n):
                    idx = _half_slices(parts[a].shape, h)
                    for j, chip in enumerate(chips):
                        landed = slots[a].at[2 * chip[0] + chip[1]].at[idx]
                        _remote(landed, landed, ici_send.at[3 * a + j], ici_recv.at[3 * a + j], (x, y, c)).wait_recv()
                    total = slots[a][(0,) + idx]
                    for s in range(1, 4):
                        total = total + slots[a][(s,) + idx]
                    out[a][idx] = total
                    rc = _remote(out[a].at[idx], out[a].at[idx], swap_send.at[a], swap_recv.at[a], (x, y, 1 - h))
                    rc.start()
                    sends.append(rc)
                for a in range(n):
                    other = out[a].at[_half_slices(parts[a].shape, 1 - h)]
                    _remote(other, other, swap_send.at[a], swap_recv.at[a], (x, y, c)).wait_recv()
                for rc in sends:
                    rc.wait_send()
        for rc in pairs:
            rc.wait_send()

    return pl.pallas_call(
        body, name="allreduce_small", in_specs=[_VMEM] * n + [_ANY], out_specs=[_VMEM] * n,
        out_shape=[jax.ShapeDtypeStruct(p.shape, f32) for p in parts],
        scratch_shapes=[pltpu.VMEM(p.shape, f32) for p in parts] * 2 + [pltpu.VMEM((4,) + p.shape, f32) for p in parts]
        + [pltpu.SemaphoreType.DMA((n,)), pltpu.SemaphoreType.DMA((n,)), pltpu.SemaphoreType.DMA((3 * n,)),
           pltpu.SemaphoreType.DMA((3 * n,)), pltpu.SemaphoreType.DMA((n,)), pltpu.SemaphoreType.DMA((n,))],
        compiler_params=pltpu.CompilerParams(vmem_limit_bytes=_VMEM_LIMIT_BYTES),
    )(*parts, after)


def _local_step(x, mem, tgt, g_mix, g_xattn, g_mem, g_ffn, g_final, cb, lg, lb, pw, ps, fb, started, relay, weights, reduce,
                n_seq, seq, n_mem):
    t, d = x.shape
    f = fb.shape[1] // 2
    c = cb.shape[1]
    h1 = _rms_fwd("norm_mix", x, g_mix, after=started)
    relay(0, h1)
    w_in, cw, fw = weights(0, h1)
    u = _mm_nn("proj_in", h1, w_in, _ACT, w_in.shape[1])
    y, hc = _mix_fwd(u, cw, cb, lg, lb, pw, ps, seq)
    relay(1, y)
    w_out, w_q, w_kv, w_o = weights(1, y)
    x1, h2 = _proj_residual_norm("proj_out", y, w_out, x, g_xattn)
    q = _mm_nn("proj_q", h2, w_q, _ACT, d)
    mem_n = _rms_fwd("norm_mem", mem, g_mem)
    kv = _mm_nn("proj_kv", mem_n, w_kv, _ACT, 2 * d)
    o = _attn_fwd(q, kv, n_seq, seq, n_mem)
    x2, h3 = _proj_residual_norm("proj_o", o, w_o, x1, g_ffn, after=relay(2, o))
    w_up, w_down = weights(2, h3)
    up = _mm_nn("proj_up", h3, w_up, _ACT, f, split_out=True)
    a, uc = _ffn_gate_fwd(up, fw, fb, seq)
    dx3, dx3b, dg_final, loss = _proj_loss_bwd("proj_down", a, w_down, x2, g_final, tgt)
    da = _mm_nt("d_act", dx3b, w_down, _ACT)
    gw_down = _mm_tn_rows("dw_down", a, dx3b, f // 2, d // 2)
    dup, sums_g, sums_v = _ffn_gate_bwd(up, uc, da, fw, seq)
    gw_up = _mm_tn_pieces("dw_up", h3, dup, f // 2)
    token = reduce(0, [gw_down.reshape(8, -1, d), gw_up])
    dx2, dx2b, dg_ffn = _dproj_rms_bwd("d_h3", dup, w_up, x2, g_ffn, dx3, after=token)
    do = _mm_nt("d_o", dx2b, w_o, _ACT)
    gw_o = _mm_tn_rows("dw_o", o, dx2b, d, d // 2)
    dq, dkv = _attn_bwd(q, kv, do, n_seq, seq, n_mem)
    gw_q = _mm_tn_rows("dw_q", h2, dq, d, d // 2)
    gw_kv = _mm_tn_pieces("dw_kv", mem_n, dkv, d // 2)
    dmem_n = _mm_nt("d_mem_n", dkv, w_kv, f32)
    dg_mem = _rms_gain_grad("norm_mem_bwd", mem, dmem_n)
    dx1, dx1b, dg_xattn = _dproj_rms_bwd("d_h2", dq, w_q, x1, g_xattn, dx2)
    dy = _mm_nt("d_y", dx1b, w_out, _ACT)
    gw_out = _mm_tn_rows("dw_out", y, dx1b, d, d // 2)
    token = reduce(1, [gw_o.reshape(8, -1, d), gw_q.reshape(8, -1, d), gw_kv, gw_out.reshape(8, -1, d)])
    dhc, sums_norm = _mix_bwd_norm(hc, dy, lg, lb, token)
    du, d_cw, d_ps, d_pw = _mix_bwd_taps(u, dhc, dy, cw, pw, ps, seq)
    gw_in = _mm_tn_pieces("dw_in", h1, du, c * 3 // 4)
    token = reduce(2, [gw_in])
    grad_x, dg_mix = _dproj_rms_bwd("d_h1", du, w_in, x, g_mix, dx1, storage_copy=False, after=token)
    zero_row = jnp.zeros((1, d), f32)
    gains = jnp.concatenate([dg_mix, dg_xattn, dg_mem, dg_ffn, dg_final, jnp.pad(loss, ((0, 0), (0, d - 1))), zero_row, zero_row], axis=0)
    conv_rows = jnp.concatenate([sums_norm[2:3], sums_norm[0:1], sums_norm[1:2], d_ps[0:1], jnp.zeros((4, c), f32)], axis=0)
    ffn_rows = jnp.concatenate([sums_g, sums_v], axis=1)
    small = [gains, conv_rows, d_pw.reshape(-1, d_pw.shape[-1]), ffn_rows, d_cw]
    return grad_x, small


def kernel(x, mem, norm_mix_g, w_in, conv_dw_w, conv_dw_b, conv_ln_g, conv_ln_b, pool_w, pool_scale, w_out, norm_xattn_g, norm_mem_g, w_q, w_kv, w_o, norm_ffn_g, w_up, ffn_dw_w, ffn_dw_b, w_down, norm_final_g, loss_target, m_norm_mix_g, m_w_in, m_conv_dw_w, m_conv_dw_b, m_conv_ln_g, m_conv_ln_b, m_pool_w, m_pool_scale, m_w_out, m_norm_xattn_g, m_norm_mem_g, m_w_q, m_w_kv, m_w_o, m_norm_ffn_g, m_w_up, m_ffn_dw_w, m_ffn_dw_b, m_w_down, m_norm_final_g, v_norm_mix_g, v_w_in, v_conv_dw_w, v_conv_dw_b, v_conv_ln_g, v_conv_ln_b, v_pool_w, v_pool_scale, v_w_out, v_norm_xattn_g, v_norm_mem_g, v_w_q, v_w_kv, v_w_o, v_norm_ffn_g, v_w_up, v_ffn_dw_w, v_ffn_dw_b, v_w_down, v_norm_final_g):
    n_seq, seq, d = x.shape
    n_mem = mem.shape[1]
    chip = 2 * lax.axis_index("x") + lax.axis_index("y")

    place = jnp.stack([chip, lax.axis_index("c")]).astype(jnp.int32)

    col_w = [w_in, w_kv, w_up]
    row_w = [w_out, w_q, w_o, w_down]
    kw = conv_dw_w.shape[1]

    def padded_in_place(shard, rows):
        full = jnp.zeros((rows, 4 * shard.shape[1]), shard.dtype)
        return lax.dynamic_update_slice(full, shard, (0, chip * shard.shape[1]))

    first = list(_place_shards("place_w_in", place, [w_in[0]], [True]))
    first += [padded_in_place(conv_dw_w[0], _HALO), padded_in_place(ffn_dw_w[0], 8)]
    first, first_sems, token = _allgather_start("allgather_start_0", first, [True] * 3, [False, True, True], [[0, 1, 2]])
    rest = [w_kv, w_up, w_out, w_q, w_o, w_down]
    rest_flags = [True, True, False, False, False, False]
    rest = list(_place_shards("place_rest", place, [w[0] for w in rest], rest_flags, after=token))
    rest, rest_sems, all_started = _allgather_start("allgather_start_1", rest, rest_flags, [False] * 6, [[2, 3, 0, 4], [1, 5]])
    started = [(first, [True] * 3, [False, True, True], first_sems[0]),
               ([rest[i] for i in (2, 3, 0, 4)], [False, False, True, False], [False] * 4, rest_sems[0]),
               ([rest[i] for i in (1, 5)], [True, False], [False] * 2, rest_sems[1])]
    relayed = {}

    def relay(g, after):
        group_bufs, flags, wholes, group_sems = started[g]
        group_bufs, sibling_sems, relay_token = _allgather_relay("allgather_relay_%d" % g, group_bufs, flags, wholes, group_sems, after)
        relayed[g] = (group_bufs, sibling_sems)
        return relay_token

    def weights(g, after):
        group_bufs, sibling_sems = relayed[g]
        return _allgather_wait("allgather_wait_%d" % g, group_bufs, started[g][1], started[g][2], sibling_sems, after)

    names = ["w_in", "w_kv", "w_up", "w_out", "w_q", "w_o", "w_down"]
    reduce_groups = [["w_down", "w_up"], ["w_o", "w_q", "w_kv", "w_out"], ["w_in"]]
    in_flight = {}

    def reduce(g, grads):
        grads, lands, rs_sems, token = _grad_exchange_start("rs_start_%d" % g, grads)
        in_flight[g] = (grads, lands, rs_sems)
        return token

    grad_x, small = _local_step(
        x.reshape(n_seq * seq, d), mem.reshape(n_seq * n_mem, d), loss_target.reshape(n_seq * seq, d),
        norm_mix_g, norm_xattn_g, norm_mem_g, norm_ffn_g, norm_final_g.reshape(1, d),
        conv_dw_b, conv_ln_g, conv_ln_b, pool_w[0], pool_scale, ffn_dw_b, all_started, relay, weights, reduce,
        n_seq, seq, n_mem)

    landed = {}
    for g, members in enumerate(reduce_groups):
        grads, lands, rs_sems = in_flight[g]
        grads, lands = _grad_exchange_wait("rs_wait_%d" % g, grads, lands, rs_sems, grad_x)
        landed.update(zip(members, zip(grads, lands)))
    finals = _sum_partials("rs_sum", place, [landed[n][0] for n in names], [landed[n][1] for n in names])
    finals, swap_sems, token = _swap_halves_start(finals)

    gains, conv_rows, d_pw, ffn_rows, d_cw = _allreduce_small(small, token)
    loss = gains[5, 0]
    shard_grads = _swap_halves_wait(finals, swap_sems, gains)

    outs = {}
    big_w = dict(zip(names, col_w + row_w))
    big_m = dict(w_in=m_w_in, w_kv=m_w_kv, w_up=m_w_up, w_out=m_w_out, w_q=m_w_q, w_o=m_w_o, w_down=m_w_down)
    big_v = dict(w_in=v_w_in, w_kv=v_w_kv, w_up=v_w_up, w_out=v_w_out, w_q=v_w_q, w_o=v_w_o, w_down=v_w_down)
    big_quads = [(big_w[n], g.reshape(big_w[n].shape[1:]), big_m[n], big_v[n]) for n, g in zip(names, shard_grads)]
    outs.update(zip(names, _adamw_shards(big_quads)))

    f2 = ffn_dw_b.shape[1]
    cs_c = conv_dw_w.shape[2]
    cs_f = ffn_dw_w.shape[2]
    g_cw = lax.dynamic_slice(d_cw, (0, chip * cs_c), (kw, cs_c)).reshape(conv_dw_w.shape)
    g_fw = lax.dynamic_slice(ffn_rows, (1, chip * cs_f), (ffn_dw_w.shape[1], cs_f)).reshape(ffn_dw_w.shape)
    small_params = [
        ("norm_mix_g", norm_mix_g, gains[0:1], m_norm_mix_g, v_norm_mix_g),
        ("conv_dw_w", conv_dw_w, g_cw, m_conv_dw_w, v_conv_dw_w),
        ("conv_dw_b", conv_dw_b, conv_rows[0:1], m_conv_dw_b, v_conv_dw_b),
        ("conv_ln_g", conv_ln_g, conv_rows[1:2], m_conv_ln_g, v_conv_ln_g),
        ("conv_ln_b", conv_ln_b, conv_rows[2:3], m_conv_ln_b, v_conv_ln_b),
        ("pool_w", pool_w, d_pw.reshape(pool_w.shape), m_pool_w, v_pool_w),
        ("pool_scale", pool_scale, conv_rows[3:4], m_pool_scale, v_pool_scale),
        ("norm_xattn_g", norm_xattn_g, gains[1:2], m_norm_xattn_g, v_norm_xattn_g),
        ("norm_mem_g", norm_mem_g, gains[2:3], m_norm_mem_g, v_norm_mem_g),
        ("norm_ffn_g", norm_ffn_g, gains[3:4], m_norm_ffn_g, v_norm_ffn_g),
        ("ffn_dw_w", ffn_dw_w, g_fw, m_ffn_dw_w, v_ffn_dw_w),
        ("ffn_dw_b", ffn_dw_b, ffn_rows[0:1, :f2], m_ffn_dw_b, v_ffn_dw_b),
        ("norm_final_g", norm_final_g.reshape(1, d), gains[4:5], m_norm_final_g.reshape(1, d), v_norm_final_g.reshape(1, d)),
    ]
    quads = []
    for _, w, g, m, v in small_params:
        shape2 = (-1, w.shape[-1])
        quads.append((w.reshape(shape2), g.reshape(shape2), m.reshape(shape2), v.reshape(shape2)))
    for (n, w, g, _, _), (delta, new_m, new_v) in zip(small_params, _adamw_small(quads)):
        shape = norm_final_g.shape if n == "norm_final_g" else w.shape
        outs[n] = (g.reshape(shape), delta.reshape(shape), new_m.reshape(shape), new_v.reshape(shape))

    order = ["norm_mix_g", "w_in", "conv_dw_w", "conv_dw_b", "conv_ln_g", "conv_ln_b", "pool_w", "pool_scale", "w_out",
             "norm_xattn_g", "norm_mem_g", "w_q", "w_kv", "w_o", "norm_ffn_g", "w_up", "ffn_dw_w", "ffn_dw_b", "w_down",
             "norm_final_g"]
    return (loss, grad_x.reshape(x.shape), *[outs[n][0] for n in order], *[outs[n][1] for n in order],
            *[outs[n][2] for n in order], *[outs[n][3] for n in order])
```

```python
import jax
import jax.numpy as jnp
from jax import lax
from jax.experimental import pallas as pl
from jax.experimental.pallas import tpu as pltpu

f32 = jnp.float32
_ACT = jnp.bfloat16

EPS = 1e-6
POOL_WINDOWS = (2, 4, 8, 16)
XATTN_HEADS = 4
ADAM_LR = 0.001
ADAM_B1 = 0.9
ADAM_B2 = 0.999
ADAM_EPS = 1e-08
ADAM_WD = 0.01
ADAM_STEP = 10

_VMEM_LIMIT_BYTES = 56 * 1024 * 1024
_MESH = pl.DeviceIdType.MESH
_ANY = pl.BlockSpec(memory_space=pl.ANY)
_VMEM = pl.BlockSpec(memory_space=pltpu.VMEM)
_HBM = pl.BlockSpec(memory_space=pltpu.HBM)
_SEM = pl.BlockSpec(memory_space=pltpu.SEMAPHORE)
_EFFECT = pltpu.SideEffectType.DATAFLOW_SIDE_EFFECTING

_NN = (((1,), (0,)), ((), ()))
_NT = (((1,), (1,)), ((), ()))
_TN = (((0,), (0,)), ((), ()))


def _params(n_grid):
    return pltpu.CompilerParams(dimension_semantics=("arbitrary",) * n_grid, vmem_limit_bytes=_VMEM_LIMIT_BYTES)


def _sigmoid(v):
    return 1.0 / (1.0 + jnp.exp(-v))


def _dot(a, b, dims):
    return lax.dot_general(a, b, dims, preferred_element_type=f32)


def _mm(name, a, b, *, dims, grid, a_spec, b_spec, o_spec, out_shape):
    def body(a_ref, b_ref, o_ref):
        o_ref[...] = _dot(a_ref[...], b_ref[...], dims).astype(o_ref.dtype)

    return pl.pallas_call(
        body, name=name, grid=grid, in_specs=[a_spec, b_spec], out_specs=o_spec, out_shape=out_shape,
        compiler_params=_params(len(grid)),
    )(a, b)


_NARROW = 2816


def _row_tile(m, width=_NARROW + 1):
    return min(1024 if width <= _NARROW else 512, m)


def _mm_nn(name, a, b, out_dtype, tn, split_out=False):
    m, k = a.shape
    n = b.shape[1]
    tm = _row_tile(m, max(k, tn))
    if split_out:
        out_shape = jax.ShapeDtypeStruct((n // tn, m, tn), out_dtype)
        o_spec = pl.BlockSpec((None, tm, tn), lambda j, i: (j, i, 0))
    else:
        out_shape = jax.ShapeDtypeStruct((m, n), out_dtype)
        o_spec = pl.BlockSpec((tm, tn), lambda j, i: (i, j))
    return _mm(
        name, a, b, dims=_NN, grid=(n // tn, m // tm),
        a_spec=pl.BlockSpec((tm, k), lambda j, i: (i, 0)), b_spec=pl.BlockSpec((k, tn), lambda j, i: (0, j)),
        o_spec=o_spec, out_shape=out_shape,
    )


def _mm_nt(name, a, b, out_dtype):
    n, kc = b.shape
    m = a.shape[0]
    tm = _row_tile(m, max(n, kc))
    return _mm(
        name, a, b, dims=_NT, grid=(m // tm,),
        a_spec=pl.BlockSpec((tm, kc), lambda i: (i, 0)),
        b_spec=pl.BlockSpec((n, kc), lambda i: (0, 0), pipeline_mode=pl.Buffered(1)),
        o_spec=pl.BlockSpec((tm, n), lambda i: (i, 0)),
        out_shape=jax.ShapeDtypeStruct((m, n), out_dtype),
    )


def _mm_tn_rows(name, a, b, tka, tn):
    m, ka = a.shape
    nb = b.shape[1]
    return _mm(
        name, a, b, dims=_TN, grid=(ka // tka, nb // tn),
        a_spec=pl.BlockSpec((m, tka), lambda i, j: (0, i)), b_spec=pl.BlockSpec((m, tn), lambda i, j: (0, j)),
        o_spec=pl.BlockSpec((tka, tn), lambda i, j: (i, j)),
        out_shape=jax.ShapeDtypeStruct((ka, nb), _ACT),
    )


def _mm_tn_pieces(name, a, b, cs):
    m, ka = a.shape
    if b.ndim == 3:
        b_spec = pl.BlockSpec((None, m, cs), lambda i, j: (j // 2, 0, j % 2))
    else:
        b_spec = pl.BlockSpec((m, cs), lambda i, j: (0, j))
    return _mm(
        name, a, b, dims=_TN, grid=(2, 4),
        a_spec=pl.BlockSpec((m, ka // 2), lambda i, j: (0, i)), b_spec=b_spec,
        o_spec=pl.BlockSpec((None, ka // 2, cs), lambda i, j: (2 * j + i, 0, 0)),
        out_shape=jax.ShapeDtypeStruct((8, ka // 2, cs), _ACT),
    )


def _after(after):
    return ([], []) if after is None else ([after], [_ANY])


def _rms_fwd(name, x, g, after=None):
    t, d = x.shape
    tm = _row_tile(t, d)
    more, more_specs = _after(after)

    def body(x_ref, g_ref, *refs):
        h_ref = refs[-1]
        xv = x_ref[...]
        r = lax.rsqrt(jnp.mean(xv * xv, axis=-1, keepdims=True) + EPS)
        h_ref[...] = (xv * r * g_ref[...]).astype(h_ref.dtype)

    return pl.pallas_call(
        body, name=name, grid=(t // tm,),
        in_specs=[pl.BlockSpec((tm, d), lambda i: (i, 0)), pl.BlockSpec((1, d), lambda i: (0, 0))] + more_specs,
        out_specs=pl.BlockSpec((tm, d), lambda i: (i, 0)), out_shape=jax.ShapeDtypeStruct((t, d), _ACT),
        compiler_params=_params(1),
    )(x, g, *more)


def _fused_rows(name, a, b, product, a_spec, tm, extras, extra_specs, out_shape, out_specs, epilogue):
    ne = len(extras)

    def body(a_ref, b_ref, *refs):
        epilogue(product(a_ref, b_ref), refs[:ne], refs[ne:])

    m = extras[0].shape[0]
    return pl.pallas_call(
        body, name=name, grid=(m // tm,),
        in_specs=[a_spec, pl.BlockSpec(b.shape, lambda i: (0, 0), pipeline_mode=pl.Buffered(1)), *extra_specs],
        out_specs=out_specs, out_shape=out_shape, compiler_params=_params(1),
    )(a, b, *extras)


def _proj_residual_norm(name, a, b, res, g, after=None):
    m, k = a.shape
    d = b.shape[1]
    tm = _row_tile(m, max(k, d))

    def epilogue(p, ins, outs):
        xv = p + ins[0][...]
        outs[0][...] = xv
        r = lax.rsqrt(jnp.mean(xv * xv, axis=-1, keepdims=True) + EPS)
        outs[1][...] = (xv * r * ins[1][...]).astype(outs[1].dtype)

    row = pl.BlockSpec((tm, d), lambda i: (i, 0))
    return _fused_rows(
        name, a, b, lambda a_ref, b_ref: _dot(a_ref[...], b_ref[...], _NN), pl.BlockSpec((tm, k), lambda i: (i, 0)), tm,
        [res, g] + _after(after)[0], [row, pl.BlockSpec((1, d), lambda i: (0, 0))] + _after(after)[1],
        [jax.ShapeDtypeStruct((m, d), f32), jax.ShapeDtypeStruct((m, d), _ACT)], [row, row], epilogue)


def _dproj_rms_bwd(name, a, b, x, g, dres, storage_copy=True, after=None):
    m, d = x.shape
    if a.ndim == 3:
        nh, _, kh = a.shape
        tm = _row_tile(m, nh * kh)
        a_spec = pl.BlockSpec((nh, tm, kh), lambda i: (0, i, 0))

        def product(a_ref, b_ref):
            p = _dot(a_ref[0], b_ref[:, 0:kh], _NT)
            for h in range(1, nh):
                p = p + _dot(a_ref[h], b_ref[:, h * kh:(h + 1) * kh], _NT)
            return p
    else:
        tm = _row_tile(m, max(a.shape[1], d))
        a_spec = pl.BlockSpec((tm, a.shape[1]), lambda i: (i, 0))

        def product(a_ref, b_ref):
            return _dot(a_ref[...], b_ref[...], _NT)

    def epilogue(dhv, ins, outs):
        x_ref, g_ref, dres_ref = ins[:3]
        dg_ref = outs[-1]

        @pl.when(pl.program_id(0) == 0)
        def _():
            dg_ref[...] = jnp.zeros_like(dg_ref)

        xv = x_ref[...]
        r = lax.rsqrt(jnp.mean(xv * xv, axis=-1, keepdims=True) + EPS)
        xn = xv * r
        dxn = dhv * g_ref[...]
        dx = r * (dxn - xn * jnp.mean(dxn * xn, axis=-1, keepdims=True)) + dres_ref[...]
        outs[0][...] = dx
        if storage_copy:
            outs[1][...] = dx.astype(outs[1].dtype)
        dg_ref[...] += jnp.sum(dhv * xn, axis=0, keepdims=True)

    row = pl.BlockSpec((tm, d), lambda i: (i, 0))
    vec = pl.BlockSpec((1, d), lambda i: (0, 0))
    copies = [jax.ShapeDtypeStruct((m, d), _ACT)] if storage_copy else []
    return _fused_rows(
        name, a, b, product, a_spec, tm, [x, g, dres] + _after(after)[0], [row, vec, row] + _after(after)[1],
        [jax.ShapeDtypeStruct((m, d), f32)] + copies + [jax.ShapeDtypeStruct((1, d), f32)],
        [row] * (1 + len(copies)) + [vec], epilogue)


def _proj_loss_bwd(name, a, b, res, g, tgt):
    m, k = a.shape
    d = b.shape[1]
    tm = _row_tile(m, max(k, d))

    def epilogue(p, ins, outs):
        res_ref, g_ref, t_ref = ins
        dx_ref, dxb_ref, dg_ref, loss_ref = outs

        @pl.when(pl.program_id(0) == 0)
        def _():
            dg_ref[...] = jnp.zeros_like(dg_ref)
            loss_ref[...] = jnp.zeros_like(loss_ref)

        xv = p + res_ref[...]
        gv = g_ref[...]
        r = lax.rsqrt(jnp.mean(xv * xv, axis=-1, keepdims=True) + EPS)
        xn = xv * r
        err = xn * gv - t_ref[...]
        loss_ref[...] += 0.5 * jnp.sum(jnp.mean(err * err, axis=-1, keepdims=True), axis=0, keepdims=True)
        dout = err * (1.0 / d)
        dxn = dout * gv
        dx = r * (dxn - xn * jnp.mean(dxn * xn, axis=-1, keepdims=True))
        dx_ref[...] = dx
        dxb_ref[...] = dx.astype(dxb_ref.dtype)
        dg_ref[...] += jnp.sum(dout * xn, axis=0, keepdims=True)

    row = pl.BlockSpec((tm, d), lambda i: (i, 0))
    vec = pl.BlockSpec((1, d), lambda i: (0, 0))
    return _fused_rows(
        name, a, b, lambda a_ref, b_ref: _dot(a_ref[...], b_ref[...], _NN), pl.BlockSpec((tm, k), lambda i: (i, 0)), tm,
        [res, g, tgt], [row, vec, row],
        [jax.ShapeDtypeStruct((m, d), f32), jax.ShapeDtypeStruct((m, d), _ACT), jax.ShapeDtypeStruct((1, d), f32),
         jax.ShapeDtypeStruct((1, 1), f32)],
        [row, row, vec, pl.BlockSpec((1, 1), lambda i: (0, 0))], epilogue)


def _rms_gain_grad(name, x, dh):
    t, d = x.shape
    tm = _row_tile(t)

    def body(x_ref, dh_ref, dg_ref):
        @pl.when(pl.program_id(0) == 0)
        def _():
            dg_ref[...] = jnp.zeros_like(dg_ref)

        xv = x_ref[...]
        r = lax.rsqrt(jnp.mean(xv * xv, axis=-1, keepdims=True) + EPS)
        dg_ref[...] += jnp.sum(dh_ref[...] * (xv * r), axis=0, keepdims=True)

    row = pl.BlockSpec((tm, d), lambda i: (i, 0))
    return pl.pallas_call(
        body, name=name, grid=(t // tm,), in_specs=[row, row], out_specs=pl.BlockSpec((1, d), lambda i: (0, 0)),
        out_shape=jax.ShapeDtypeStruct((1, d), f32), compiler_params=_params(1),
    )(x, dh)


_CONV_ROWS = 512
_CHUNK = 64
_HALO = 32


def _pool_counts(pos, w):
    return jnp.minimum(pos + 1.0, float(w))


def _rows_from(win, start, rows):
    if start % 8 == 0:
        return win[start:start + rows, :]
    n = win.shape[0]
    return pltpu.roll(win, n - start % 8, axis=0)[start - start % 8:start - start % 8 + rows, :]


def _tap_rows(buf, starts, rows):
    for residue in range(8):
        group = [(k, s) for k, s in starts.items() if s % 8 == residue]
        if group:
            lo = min(s for _, s in group) - residue
            hi = max(s for _, s in group) - residue + rows + (8 if residue else 0)
            win = buf[lo:hi, :]
            if residue:
                win = pltpu.roll(win, hi - lo - residue, axis=0)
            for k, s in group:
                yield k, win[s - residue - lo:s - residue - lo + rows, :]


def _mix_fwd(u, cw, cb, lg, lb, pw, ps, seq):
    t, c3 = u.shape
    c = c3 // 3
    kw = 31
    tm = min(_CONV_ROWS, seq)
    tps = seq // tm
    gd = c // len(POOL_WINDOWS)

    def body(u_ref, uh_ref, cw_ref, cb_ref, lg_ref, lb_ref, pw_ref, ps_ref, y_ref, hc_ref, hgbuf, pbuf):
        i = pl.program_id(0)
        keep = jnp.where(i % tps == 0, 0.0, 1.0)
        um = u_ref[...].astype(f32)
        uh = uh_ref[...].astype(f32) * keep
        hgbuf[0:_HALO, :] = uh[:, 0:c] * _sigmoid(uh[:, c:2 * c])
        hgbuf[_HALO:_HALO + tm, :] = um[:, 0:c] * _sigmoid(um[:, c:2 * c])
        pbuf[0:_HALO, :] = uh[:, 2 * c:]
        pbuf[_HALO:_HALO + tm, :] = um[:, 2 * c:]
        for r0 in range(0, tm, _CHUNK):
            acc = jnp.broadcast_to(cb_ref[...], (_CHUNK, c))
            for k, rows in _tap_rows(hgbuf, {k: r0 + _HALO - (kw - 1) + k for k in range(kw)}, _CHUNK):
                acc = acc + cw_ref[k:k + 1, :] * rows
            hc_ref[r0:r0 + _CHUNK, :] = acc
            mu = jnp.mean(acc, axis=-1, keepdims=True)
            xc = acc - mu
            var = jnp.mean(xc * xc, axis=-1, keepdims=True)
            hl = xc * lax.rsqrt(var + EPS) * lg_ref[...] + lb_ref[...]
            y_ref[r0:r0 + _CHUNK, 0:c] = (hl * _sigmoid(hl)).astype(y_ref.dtype)
        pos = ((i % tps) * tm).astype(f32) + lax.broadcasted_iota(jnp.int32, (tm, 1), 0).astype(f32)
        for gi, w in enumerate(POOL_WINDOWS):
            sl = slice(gi * gd, (gi + 1) * gd)
            v = pbuf[_HALO:_HALO + tm, sl]
            s = v
            for j in range(1, w):
                s = s + pbuf[_HALO - j:_HALO - j + tm, sl]
            pooled = s / _pool_counts(pos, w) - v
            mixed = _dot(pooled.astype(_ACT), pw_ref[gi].astype(_ACT), _NN)
            y_ref[:, c + gi * gd:c + (gi + 1) * gd] = (mixed * ps_ref[:, sl]).astype(y_ref.dtype)

    hb = tm // _HALO
    full = lambda shape: pl.BlockSpec(shape, lambda i: (0,) * len(shape))
    return pl.pallas_call(
        body, name="mix_fwd", grid=(t // tm,),
        in_specs=[pl.BlockSpec((tm, c3), lambda i: (i, 0)),
                  pl.BlockSpec((_HALO, c3), lambda i: (jnp.maximum(i * hb - 1, 0), 0)),
                  full((_HALO, c)), full((1, c)), full((1, c)), full((1, c)), full((len(POOL_WINDOWS), gd, gd)), full((1, c))],
        out_specs=[pl.BlockSpec((tm, 2 * c), lambda i: (i, 0)), pl.BlockSpec((tm, c), lambda i: (i, 0))],
        out_shape=[jax.ShapeDtypeStruct((t, 2 * c), _ACT), jax.ShapeDtypeStruct((t, c), f32)],
        scratch_shapes=[pltpu.VMEM((_HALO + tm, c), f32), pltpu.VMEM((_HALO + tm, c), f32)],
        compiler_params=_params(1),
    )(u, u, cw, cb, lg, lb, pw, ps)


def _mix_bwd_norm(hc, dy, lg, lb, after):
    t, c = hc.shape
    tm = _row_tile(t, c)

    def body(hc_ref, dy_ref, lg_ref, lb_ref, after_ref, dhc_ref, sums_ref):
        @pl.when(pl.program_id(0) == 0)
        def _():
            sums_ref[...] = jnp.zeros_like(sums_ref)

        hcv = hc_ref[...]
        mu = jnp.mean(hcv, axis=-1, keepdims=True)
        xc = hcv - mu
        rstd = lax.rsqrt(jnp.mean(xc * xc, axis=-1, keepdims=True) + EPS)
        n = xc * rstd
        hl = n * lg_ref[...] + lb_ref[...]
        sg = _sigmoid(hl)
        dhl = dy_ref[...].astype(f32) * (sg * (1.0 + hl * (1.0 - sg)))
        dn = dhl * lg_ref[...]
        dhc = rstd * (dn - jnp.mean(dn, axis=-1, keepdims=True) - n * jnp.mean(dn * n, axis=-1, keepdims=True))
        dhc_ref[...] = dhc
        sums_ref[0:1, :] += jnp.sum(dhl * n, axis=0, keepdims=True)
        sums_ref[1:2, :] += jnp.sum(dhl, axis=0, keepdims=True)
        sums_ref[2:3, :] += jnp.sum(dhc, axis=0, keepdims=True)

    row = pl.BlockSpec((tm, c), lambda i: (i, 0))
    vec = pl.BlockSpec((1, c), lambda i: (0, 0))
    return pl.pallas_call(
        body, name="mix_bwd_norm", grid=(t // tm,), in_specs=[row, row, vec, vec, _ANY],
        out_specs=[row, pl.BlockSpec((8, c), lambda i: (0, 0))],
        out_shape=[jax.ShapeDtypeStruct((t, c), f32), jax.ShapeDtypeStruct((8, c), f32)],
        compiler_params=_params(1),
    )(hc, dy, lg, lb, after)


def _mix_bwd_taps(u, dhc, dy, cw, pw, ps, seq):
    t, c3 = u.shape
    c = c3 // 3
    kw = 31
    tm = min(_CONV_ROWS, seq)
    tps = seq // tm
    ng = len(POOL_WINDOWS)
    gd = c // ng
    nh = 16

    def body(u_ref, uh_ref, dhc_ref, dhcn_ref, dy_ref, dyn_ref, cw_ref, pw_ref, ps_ref,
             du_ref, dcw_ref, dps_ref, dpw_ref, hgbuf, dcbuf, pbuf, dpbuf):
        i = pl.program_id(0)
        keep_prev = jnp.where(i % tps == 0, 0.0, 1.0)
        keep_next = jnp.where(i % tps == tps - 1, 0.0, 1.0)

        @pl.when(i == 0)
        def _():
            dcw_ref[...] = jnp.zeros_like(dcw_ref)
            dps_ref[...] = jnp.zeros_like(dps_ref)
            dpw_ref[...] = jnp.zeros_like(dpw_ref)

        uh = uh_ref[...].astype(f32) * keep_prev
        hgbuf[0:_HALO, :] = uh[:, 0:c] * _sigmoid(uh[:, c:2 * c])
        pbuf[0:_HALO, :] = uh[:, 2 * c:]
        um = u_ref[...].astype(f32)
        hgbuf[_HALO:_HALO + tm, :] = um[:, 0:c] * _sigmoid(um[:, c:2 * c])
        pbuf[_HALO:_HALO + tm, :] = um[:, 2 * c:]
        dcbuf[0:tm, :] = dhc_ref[...]
        dcbuf[tm:tm + _HALO, :] = dhcn_ref[...] * keep_next
        tap_sums = [None] * kw
        for r0 in range(0, tm, _CHUNK):
            dh = dcbuf[r0:r0 + _CHUNK, :]
            acc = jnp.zeros((_CHUNK, c), f32)
            for k, rows in _tap_rows(hgbuf, {k: r0 + _HALO - (kw - 1) + k for k in range(kw)}, _CHUNK):
                part = (dh * rows).reshape(_CHUNK // 8, 8, c).sum(axis=0)
                tap_sums[k] = part if tap_sums[k] is None else tap_sums[k] + part
            for k, rows in _tap_rows(dcbuf, {k: r0 + (kw - 1) - k for k in range(kw)}, _CHUNK):
                acc = acc + cw_ref[k:k + 1, :] * rows
            val = u_ref[r0:r0 + _CHUNK, 0:c].astype(f32)
            sg = _sigmoid(u_ref[r0:r0 + _CHUNK, c:2 * c].astype(f32))
            du_ref[r0:r0 + _CHUNK, 0:c] = (acc * sg).astype(du_ref.dtype)
            du_ref[r0:r0 + _CHUNK, c:2 * c] = (acc * val * sg * (1.0 - sg)).astype(du_ref.dtype)
        for k in range(kw):
            dcw_ref[k:k + 1, :] += jnp.sum(tap_sums[k], axis=0, keepdims=True)
        base = ((i % tps) * tm).astype(f32)
        pos = base + lax.broadcasted_iota(jnp.int32, (tm, 1), 0).astype(f32)
        pos_next = base + float(tm) + lax.broadcasted_iota(jnp.int32, (nh, 1), 0).astype(f32)
        for gi, w in enumerate(POOL_WINDOWS):
            sl = slice(gi * gd, (gi + 1) * gd)
            v = pbuf[_HALO:_HALO + tm, sl]
            s = v
            for j in range(1, w):
                s = s + pbuf[_HALO - j:_HALO - j + tm, sl]
            cnt = _pool_counts(pos, w)
            pooled = (s / cnt - v).astype(_ACT)
            pwg = pw_ref[gi].astype(_ACT)
            mixed = _dot(pooled, pwg, _NN)
            dyp = dy_ref[:, sl].astype(f32)
            dps_ref[0:1, sl] += jnp.sum(dyp * mixed, axis=0, keepdims=True)
            dmix = (dyp * ps_ref[:, sl]).astype(_ACT)
            dpw_ref[gi] += _dot(pooled, dmix, _TN)
            dmix_next = (dyn_ref[:, sl].astype(f32) * ps_ref[:, sl] * keep_next).astype(_ACT)
            dpool = _dot(dmix, pwg, _NT)
            dpbuf[0:tm, sl] = dpool / cnt
            dpbuf[tm:tm + nh, sl] = _dot(dmix_next, pwg, _NT) / _pool_counts(pos_next, w)
            acc = -dpool
            for j in range(w):
                acc = acc + dpbuf[j:j + tm, sl]
            du_ref[:, 2 * c + gi * gd:2 * c + (gi + 1) * gd] = acc.astype(du_ref.dtype)

    hb = tm // _HALO
    n_halo = t // _HALO
    n_nh = t // nh
    full = lambda shape: pl.BlockSpec(shape, lambda i: (0,) * len(shape))
    return pl.pallas_call(
        body, name="mix_bwd_taps", grid=(t // tm,),
        in_specs=[pl.BlockSpec((tm, c3), lambda i: (i, 0)),
                  pl.BlockSpec((_HALO, c3), lambda i: (jnp.maximum(i * hb - 1, 0), 0)),
                  pl.BlockSpec((tm, c), lambda i: (i, 0)),
                  pl.BlockSpec((_HALO, c), lambda i: (jnp.minimum((i + 1) * hb, n_halo - 1), 0)),
                  pl.BlockSpec((tm, c), lambda i: (i, 1)),
                  pl.BlockSpec((nh, c), lambda i: (jnp.minimum((i + 1) * (tm // nh), n_nh - 1), 1)),
                  full((_HALO, c)), full((ng, gd, gd)), full((1, c))],
        out_specs=[pl.BlockSpec((tm, c3), lambda i: (i, 0)), full((_HALO, c)), full((8, c)), full((ng, gd, gd))],
        out_shape=[jax.ShapeDtypeStruct((t, c3), _ACT), jax.ShapeDtypeStruct((_HALO, c), f32),
                   jax.ShapeDtypeStruct((8, c), f32), jax.ShapeDtypeStruct((ng, gd, gd), f32)],
        scratch_shapes=[pltpu.VMEM((_HALO + tm, c), f32), pltpu.VMEM((tm + _HALO, c), f32),
                        pltpu.VMEM((_HALO + tm, c), f32), pltpu.VMEM((tm + nh, c), f32)],
        compiler_params=_params(1),
    )(u, u, dhc, dhc, dy, dy, cw, pw, ps)


def _attn_fwd(q, kv, n_seq, seq, n_mem):
    t, d = q.shape
    dh = d // XATTN_HEADS
    tq = min(1024, seq)
    nq = seq // tq
    scale = dh ** -0.5

    def body(q_ref, kv_ref, o_ref):
        for h in range(XATTN_HEADS):
            cols = slice(h * dh, (h + 1) * dh)
            s = _dot(q_ref[:, cols], kv_ref[:, cols], _NT) * scale
            e = jnp.exp(s - jnp.max(s, axis=-1, keepdims=True))
            p = e / jnp.sum(e, axis=-1, keepdims=True)
            o_ref[:, cols] = _dot(p.astype(_ACT), kv_ref[:, d + h * dh:d + (h + 1) * dh], _NN).astype(o_ref.dtype)

    qs = pl.BlockSpec((tq, d), lambda b, i: (b * nq + i, 0))
    return pl.pallas_call(
        body, name="attn_fwd", grid=(n_seq, nq), in_specs=[qs, pl.BlockSpec((n_mem, 2 * d), lambda b, i: (b, 0))],
        out_specs=qs, out_shape=jax.ShapeDtypeStruct((t, d), _ACT), compiler_params=_params(2),
    )(q, kv)


def _attn_bwd(q, kv, do, n_seq, seq, n_mem):
    t, d = q.shape
    dh = d // XATTN_HEADS
    tq = min(1024, seq)
    nq = seq // tq
    scale = dh ** -0.5

    def body(q_ref, kv_ref, do_ref, dq_ref, dkv_ref, acc):
        i = pl.program_id(1)

        @pl.when(i == 0)
        def _():
            acc[...] = jnp.zeros_like(acc)

        for h in range(XATTN_HEADS):
            cols = slice(h * dh, (h + 1) * dh)
            vcols = slice(d + h * dh, d + (h + 1) * dh)
            qv = q_ref[:, cols]
            kh = kv_ref[:, cols]
            dov = do_ref[:, cols]
            s = _dot(qv, kh, _NT) * scale
            e = jnp.exp(s - jnp.max(s, axis=-1, keepdims=True))
            p = e / jnp.sum(e, axis=-1, keepdims=True)
            dp = _dot(dov, kv_ref[:, vcols], _NT)
            ds = (p * (dp - jnp.sum(dp * p, axis=-1, keepdims=True)) * scale).astype(_ACT)
            dq_ref[:, cols] = _dot(ds, kh, _NN).astype(dq_ref.dtype)
            acc[:, cols] += _dot(ds, qv, _TN)
            acc[:, vcols] += _dot(p.astype(_ACT), dov, _TN)

        @pl.when(i == nq - 1)
        def _():
            dkv_ref[...] = acc[...].astype(dkv_ref.dtype)

    qs = pl.BlockSpec((tq, d), lambda b, i: (b * nq + i, 0))
    ms = pl.BlockSpec((n_mem, 2 * d), lambda b, i: (b, 0))
    return pl.pallas_call(
        body, name="attn_bwd", grid=(n_seq, nq), in_specs=[qs, ms, qs], out_specs=[qs, ms],
        out_shape=[jax.ShapeDtypeStruct((t, d), _ACT), jax.ShapeDtypeStruct((n_seq * n_mem, 2 * d), _ACT)],
        scratch_shapes=[pltpu.VMEM((n_mem, 2 * d), f32)], compiler_params=_params(2),
    )(q, kv, do)


_FFN_ROWS = 2048
_FFN_COLS = 256
_FFN_HALO = 16


def _window(buf, g, start, rows):
    return buf[g, pl.ds(start, rows + 8), :]


def _taps3(win, rows):
    return [_rows_from(win, 6 + k, rows) for k in range(3)]


def _conv3(b_ref, w_ref, taps):
    acc = b_ref[...] + w_ref[0:1, :] * taps[0]
    for k in (1, 2):
        acc = acc + w_ref[k:k + 1, :] * taps[k]
    return acc


def _ffn_gate_fwd(up, fw, fb, seq):
    _, t, f = up.shape
    tm = min(_FFN_ROWS, seq)
    tps = seq // tm
    tc = _FFN_COLS
    nc = f // tc
    hl = _FFN_HALO

    def body(up_ref, uph_ref, wg_ref, wv_ref, bg_ref, bv_ref, a_ref, uc_ref):
        i = pl.program_id(1)
        before = uph_ref[...]
        before = jnp.where(i % tps == 0, jnp.zeros_like(before), before)

        def chunk(r0, wins):
            conv = []
            for g, (w_ref, b_ref) in enumerate(((wg_ref, bg_ref), (wv_ref, bv_ref))):
                conv.append(_conv3(b_ref, w_ref, _taps3(wins[g].astype(f32)[hl - 8:, :], _CHUNK)))
                uc_ref[g, pl.ds(r0, _CHUNK), :] = conv[g].astype(uc_ref.dtype)
            gate, val = conv
            a_ref[pl.ds(r0, _CHUNK), :] = (gate * _sigmoid(gate) * val).astype(a_ref.dtype)

        chunk(0, [jnp.concatenate([before[g], up_ref[g, 0:_CHUNK, :]], axis=0) for g in range(2)])

        def later(ci, carry):
            r0 = pl.multiple_of(ci * _CHUNK, _CHUNK)
            chunk(r0, [up_ref[g, pl.ds(r0 - hl, _CHUNK + hl), :] for g in range(2)])
            return carry

        lax.fori_loop(1, tm // _CHUNK, later, 0)

    hb = tm // hl
    return pl.pallas_call(
        body, name="ffn_gate_fwd", grid=(nc, t // tm),
        in_specs=[pl.BlockSpec((2, tm, tc), lambda j, i: (0, i, j)),
                  pl.BlockSpec((2, hl, tc), lambda j, i: (0, jnp.maximum(i * hb - 1, 0), j)),
                  pl.BlockSpec((8, tc), lambda j, i: (0, j)), pl.BlockSpec((8, tc), lambda j, i: (0, nc + j)),
                  pl.BlockSpec((1, tc), lambda j, i: (0, j)), pl.BlockSpec((1, tc), lambda j, i: (0, nc + j))],
        out_specs=[pl.BlockSpec((tm, tc), lambda j, i: (i, j)), pl.BlockSpec((2, tm, tc), lambda j, i: (0, i, j))],
        out_shape=[jax.ShapeDtypeStruct((t, f), _ACT), jax.ShapeDtypeStruct((2, t, f), _ACT)], compiler_params=_params(2),
    )(up, up, fw, fw, fb, fb)


def _ffn_gate_bwd(up, uc, da, fw, seq):
    _, t, f = up.shape
    tm = min(_FFN_ROWS, seq)
    tps = seq // tm
    tc = _FFN_COLS
    nc = f // tc
    hl = _FFN_HALO

    def body(up_ref, uph_ref, uc_ref, ucn_ref, da_ref, dan_ref, wg_ref, wv_ref, dup_ref, sg_ref, sv_ref, dbuf, sums):
        i = pl.program_id(1)
        at_end = i % tps == tps - 1

        @pl.when(i == 0)
        def _():
            sg_ref[...] = jnp.zeros_like(sg_ref)
            sv_ref[...] = jnp.zeros_like(sv_ref)

        sums[...] = jnp.zeros_like(sums)
        before = uph_ref[...]
        before = jnp.where(i % tps == 0, jnp.zeros_like(before), before)
        w_refs = (wg_ref, wv_ref)

        def grads(r0, rows, conv, dav):
            gate, val = [v.astype(f32) for v in conv]
            sg = _sigmoid(gate)
            douts = (dav * val * (sg * (1.0 + gate * (1.0 - sg))), dav * (gate * sg))
            for g in range(2):
                dbuf[g, pl.ds(r0, rows), :] = douts[g]
            return douts

        def count(douts, wins):
            for g in range(2):
                taps = _taps3(wins[g].astype(f32)[hl - 8:, :], _CHUNK)
                sums[g, 0] += douts[g].reshape(_CHUNK // 8, 8, tc).sum(axis=0)
                for k in range(3):
                    sums[g, 1 + k] += (douts[g] * taps[k]).reshape(_CHUNK // 8, 8, tc).sum(axis=0)

        count(grads(0, _CHUNK, [uc_ref[g, 0:_CHUNK, :] for g in range(2)], da_ref[0:_CHUNK, :].astype(f32)),
              [jnp.concatenate([before[g], up_ref[g, 0:_CHUNK, :]], axis=0) for g in range(2)])

        def first(ci, carry):
            r0 = pl.multiple_of(ci * _CHUNK, _CHUNK)
            douts = grads(r0, _CHUNK, [uc_ref[g, pl.ds(r0, _CHUNK), :] for g in range(2)],
                          da_ref[pl.ds(r0, _CHUNK), :].astype(f32))
            count(douts, [up_ref[g, pl.ds(r0 - hl, _CHUNK + hl), :] for g in range(2)])
            return carry

        lax.fori_loop(1, tm // _CHUNK, first, 0)
        da_after = dan_ref[...].astype(f32)
        grads(tm, hl, [ucn_ref[g] for g in range(2)], jnp.where(at_end, jnp.zeros_like(da_after), da_after))

        def second(ci, carry):
            r0 = pl.multiple_of(ci * _CHUNK, _CHUNK)
            for g in range(2):
                win = _window(dbuf, g, r0, _CHUNK)
                acc = jnp.zeros((_CHUNK, tc), f32)
                for k in range(3):
                    acc = acc + w_refs[g][k:k + 1, :] * _rows_from(win, 2 - k, _CHUNK)
                dup_ref[g, pl.ds(r0, _CHUNK), :] = acc.astype(dup_ref.dtype)
            return carry

        lax.fori_loop(0, tm // _CHUNK, second, 0)
        for g, s_ref in enumerate((sg_ref, sv_ref)):
            for r in range(4):
                s_ref[r:r + 1, :] += jnp.sum(sums[g, r], axis=0, keepdims=True)

    hb = tm // hl
    n_halo = t // hl
    return pl.pallas_call(
        body, name="ffn_gate_bwd", grid=(nc, t // tm),
        in_specs=[pl.BlockSpec((2, tm, tc), lambda j, i: (0, i, j)),
                  pl.BlockSpec((2, hl, tc), lambda j, i: (0, jnp.maximum(i * hb - 1, 0), j)),
                  pl.BlockSpec((2, tm, tc), lambda j, i: (0, i, j)),
                  pl.BlockSpec((2, hl, tc), lambda j, i: (0, jnp.minimum((i + 1) * hb, n_halo - 1), j)),
                  pl.BlockSpec((tm, tc), lambda j, i: (i, j)),
                  pl.BlockSpec((hl, tc), lambda j, i: (jnp.minimum((i + 1) * hb, n_halo - 1), j)),
                  pl.BlockSpec((8, tc), lambda j, i: (0, j)), pl.BlockSpec((8, tc), lambda j, i: (0, nc + j))],
        out_specs=[pl.BlockSpec((2, tm, tc), lambda j, i: (0, i, j)),
                   pl.BlockSpec((8, tc), lambda j, i: (0, j)), pl.BlockSpec((8, tc), lambda j, i: (0, j))],
        out_shape=[jax.ShapeDtypeStruct((2, t, f), _ACT), jax.ShapeDtypeStruct((8, f), f32), jax.ShapeDtypeStruct((8, f), f32)],
        scratch_shapes=[pltpu.VMEM((2, tm + hl, tc), f32), pltpu.VMEM((2, 4, 8, tc), f32)],
        compiler_params=_params(2),
    )(up, up, uc, uc, da, da, fw, fw)


def _adamw_math(w, g, m, v):
    m = ADAM_B1 * m + (1.0 - ADAM_B1) * g
    v = ADAM_B2 * v + (1.0 - ADAM_B2) * (g * g)
    m_hat = m / (1.0 - ADAM_B1 ** ADAM_STEP)
    v_hat = v / (1.0 - ADAM_B2 ** ADAM_STEP)
    delta = -ADAM_LR * (m_hat / (jnp.sqrt(v_hat) + ADAM_EPS) + ADAM_WD * w)
    return delta, m, v


def _adamw_shards(quads):
    n = len(quads)
    steps = 8

    def body(*refs):
        for p in range(n):
            w_ref, g_ref, m_ref, v_ref = refs[4 * p:4 * p + 4]
            go_ref, d_ref, mo_ref, vo_ref = refs[4 * n + 4 * p:4 * n + 4 * p + 4]
            gv = g_ref[...]
            d, mn, vn = _adamw_math(w_ref[...], gv, m_ref[...], v_ref[...])
            go_ref[...] = gv
            d_ref[...] = d
            mo_ref[...] = mn
            vo_ref[...] = vn

    in_specs, out_specs, out_shape = [], [], []
    for w, _, _, _ in quads:
        _, r, c = w.shape
        s3 = pl.BlockSpec((None, r // steps, c), lambda i: (0, i, 0))
        in_specs += [s3, pl.BlockSpec((r // steps, c), lambda i: (i, 0)), s3, s3]
        out_specs += [s3] * 4
        out_shape += [jax.ShapeDtypeStruct(w.shape, f32)] * 4
    outs = pl.pallas_call(
        body, name="adamw_shards", grid=(steps,), in_specs=in_specs, out_specs=out_specs, out_shape=out_shape,
        compiler_params=_params(1),
    )(*[a for q in quads for a in q])
    return [tuple(outs[4 * p:4 * p + 4]) for p in range(n)]


def _adamw_small(quads):
    n = len(quads)

    def body(*refs):
        ins, outs = refs[:4 * n], refs[4 * n:]
        for p in range(n):
            w_ref, g_ref, m_ref, v_ref = ins[4 * p:4 * p + 4]
            d, mn, vn = _adamw_math(w_ref[...], g_ref[...], m_ref[...], v_ref[...])
            outs[3 * p][...] = d
            outs[3 * p + 1][...] = mn
            outs[3 * p + 2][...] = vn

    flat = [a for q in quads for a in q]
    shapes = [jax.ShapeDtypeStruct(q[0].shape, f32) for q in quads for _ in range(3)]
    outs = pl.pallas_call(
        body, name="adamw_small", in_specs=[_VMEM] * (4 * n), out_specs=[_VMEM] * (3 * n), out_shape=shapes,
        compiler_params=pltpu.CompilerParams(vmem_limit_bytes=_VMEM_LIMIT_BYTES),
    )(*flat)
    return [tuple(outs[3 * p:3 * p + 3]) for p in range(n)]


def _sum_partials(name, place, grads, got):
    nw = len(grads)
    steps = 2

    def body(place_ref, *refs):
        for w in range(nw):
            own_ref, got_ref, f_ref = refs[w], refs[nw + w], refs[2 * nw + w]
            s = own_ref[...].astype(f32)
            for k in range(got[w].shape[0]):
                s = s + got_ref[k].astype(f32)
            f_ref[...] = s

    own_specs, got_specs, out_specs, out_shape = [], [], [], []
    for g, l in zip(grads, got):
        _, r, c = g.shape
        tr = r // steps
        own_specs.append(pl.BlockSpec((None, tr, c), lambda i, p: (2 * p[0] + p[1], i, 0)))
        got_specs.append(pl.BlockSpec((l.shape[0], tr, c), lambda i, p: (0, i, 0)))
        out_specs.append(pl.BlockSpec((None, tr, c), lambda i, p: (p[1], i, 0)))
        out_shape.append(jax.ShapeDtypeStruct((2, r, c), f32))
    grid_spec = pltpu.PrefetchScalarGridSpec(num_scalar_prefetch=1, grid=(steps,), in_specs=own_specs + got_specs, out_specs=out_specs)
    return pl.pallas_call(body, name=name, grid_spec=grid_spec, out_shape=out_shape,
                          compiler_params=_params(1))(place, *grads, *got)


def _place():
    return lax.axis_index("x"), lax.axis_index("y"), lax.axis_index("c")


def _other_chips(x, y):
    return [(1 - x, y), (x, 1 - y), (1 - x, 1 - y)]


def _remote(src, dst, send_sem, recv_sem, to):
    return pltpu.make_async_remote_copy(src_ref=src, dst_ref=dst, send_sem=send_sem, recv_sem=recv_sem,
                                        device_id=to, device_id_type=_MESH)


def _place_shards(name, place, shards, col_sharded, after=None):
    n = len(shards)
    steps = 4
    more, more_specs = _after(after)

    def body(place_ref, *refs):
        for src, dst in zip(refs[:n], refs[n + len(more):]):
            dst[...] = src[...].astype(dst.dtype)

    in_specs, out_specs, out_shape = [], [], []
    for w, col in zip(shards, col_sharded):
        r, cs = w.shape
        tr = r // steps
        in_specs.append(pl.BlockSpec((tr, cs), lambda i, p: (i, 0)))
        if col:
            out_specs.append(pl.BlockSpec((tr, cs), lambda i, p: (i, p[0])))
            out_shape.append(jax.ShapeDtypeStruct((r, 4 * cs), _ACT))
        else:
            out_specs.append(pl.BlockSpec((tr, cs), lambda i, p: (p[0] * steps + i, 0)))
            out_shape.append(jax.ShapeDtypeStruct((4 * r, cs), _ACT))
    grid_spec = pltpu.PrefetchScalarGridSpec(num_scalar_prefetch=1, grid=(steps,), in_specs=in_specs + more_specs,
                                            out_specs=out_specs)
    return pl.pallas_call(body, name=name, grid_spec=grid_spec, out_shape=out_shape,
                          compiler_params=_params(1))(place, *shards, *more)


def _shard_of(ref, col_sharded, s):
    rows, cols = ref.shape
    if col_sharded:
        return ref.at[:, pl.ds(s * (cols // 4), cols // 4)]
    return ref.at[pl.ds(s * (rows // 4), rows // 4), :]


def _part_of(ref, col_sharded, whole, s, h):
    if whole:
        return _shard_of(ref, col_sharded, s)
    rows, cols = ref.shape
    if col_sharded:
        return ref.at[pl.ds(h * (rows // 2), rows // 2), pl.ds(s * (cols // 4), cols // 4)]
    return ref.at[pl.ds((2 * s + h) * (rows // 8), rows // 8), :]


def _allgather_start(name, bufs, col_sharded, whole, groups):
    n = len(bufs)
    ng = len(groups)

    def body(*refs):
        out = refs[n:2 * n]
        sems = refs[2 * n:2 * n + 2 * ng]
        token = refs[2 * n + 2 * ng]
        x, y, c = _place()
        for g, members in enumerate(groups):
            for i, w in enumerate(members):
                mine = _part_of(out[w], col_sharded[w], whole[w], 2 * x + y, c)
                for j, chip in enumerate(_other_chips(x, y)):
                    _remote(mine, mine, sems[2 * g].at[3 * i + j], sems[2 * g + 1].at[3 * i + j], (*chip, c)).start()
        token[...] = jnp.zeros_like(token)

    sem_shapes = [pltpu.SemaphoreType.DMA((3 * len(m),)) for m in groups for _ in range(2)]
    outs = pl.pallas_call(
        body, name=name, in_specs=[_HBM] * n, out_specs=[_HBM] * n + [_SEM] * (2 * ng) + [_VMEM],
        out_shape=[pltpu.HBM(b.shape, b.dtype) for b in bufs] + sem_shapes + [jax.ShapeDtypeStruct((8, 128), f32)],
        input_output_aliases={i: i for i in range(n)},
        compiler_params=pltpu.CompilerParams(has_side_effects=_EFFECT),
    )(*[pltpu.with_memory_space_constraint(b, pltpu.HBM) for b in bufs])
    return list(outs[:n]), [(outs[n + 2 * g], outs[n + 2 * g + 1]) for g in range(ng)], outs[n + 2 * ng]


def _allgather_relay(name, bufs, col_sharded, whole, sems, after):
    n = len(bufs)

    def body(*refs):
        buf = refs[:n]
        send, recv = refs[n], refs[n + 1]
        out = refs[n + 3:2 * n + 3]
        to_sibling, from_sibling, token = refs[2 * n + 3:]
        token[...] = jnp.zeros_like(token)
        x, y, c = _place()
        for i in range(n):
            mine = _part_of(buf[i], col_sharded[i], whole[i], 2 * x + y, c)
            for j, chip in enumerate(_other_chips(x, y)):
                landed = _part_of(buf[i], col_sharded[i], whole[i], 2 * chip[0] + chip[1], c)
                cp = _remote(mine, landed, send.at[3 * i + j], recv.at[3 * i + j], (*chip, c))
                cp.wait_send()
                cp.wait_recv()
        for i in range(n):
            if not whole[i]:
                for j, chip in enumerate(_other_chips(x, y)):
                    landed = _part_of(out[i], col_sharded[i], False, 2 * chip[0] + chip[1], c)
                    _remote(landed, landed, to_sibling.at[3 * i + j], from_sibling.at[3 * i + j], (x, y, 1 - c)).start()

    outs = pl.pallas_call(
        body, name=name, in_specs=[_HBM] * n + [_SEM, _SEM, _ANY], out_specs=[_HBM] * n + [_SEM, _SEM, _VMEM],
        out_shape=[pltpu.HBM(b.shape, b.dtype) for b in bufs] + [pltpu.SemaphoreType.DMA((3 * n,))] * 2
        + [jax.ShapeDtypeStruct((8, 128), f32)],
        input_output_aliases={i: i for i in range(n)},
        compiler_params=pltpu.CompilerParams(has_side_effects=_EFFECT),
    )(*bufs, *sems, after)
    return list(outs[:n]), (outs[n], outs[n + 1]), outs[n + 2]


def _allgather_wait(name, bufs, col_sharded, whole, sems, after):
    n = len(bufs)

    def body(*refs):
        buf = refs[:n]
        to_sibling, from_sibling = refs[n], refs[n + 1]
        x, y, c = _place()
        for i in range(n):
            if not whole[i]:
                for j, chip in enumerate(_other_chips(x, y)):
                    sent = _part_of(buf[i], col_sharded[i], False, 2 * chip[0] + chip[1], c)
                    landed = _part_of(buf[i], col_sharded[i], False, 2 * chip[0] + chip[1], 1 - c)
                    cp = _remote(sent, landed, to_sibling.at[3 * i + j], from_sibling.at[3 * i + j], (x, y, 1 - c))
                    cp.wait_send()
                    cp.wait_recv()

    return pl.pallas_call(
        body, name=name, in_specs=[_HBM] * n + [_SEM, _SEM, _ANY], out_specs=[_HBM] * n,
        out_shape=[pltpu.HBM(b.shape, b.dtype) for b in bufs],
        input_output_aliases={i: i for i in range(n)},
        compiler_params=pltpu.CompilerParams(has_side_effects=_EFFECT),
    )(*bufs, *sems, after)


def _other_devices(x, y, c):
    flips = [(bx, by, bc) for bx in (0, 1) for by in (0, 1) for bc in (0, 1)][1:]
    return [(1 - x if bx else x, 1 - y if by else y, 1 - c if bc else c) for bx, by, bc in flips]


def _grad_exchange_start(name, grads):
    nw = len(grads)
    lands = [lax.empty((7,) + g.shape[1:], g.dtype) for g in grads]

    def body(*refs):
        src = refs[2 * nw:3 * nw]
        got = refs[3 * nw:4 * nw]
        send, recv, token = refs[4 * nw:]
        x, y, c = _place()
        for w in range(nw):
            for k, (px, py, pc) in enumerate(_other_devices(x, y, c)):
                _remote(src[w].at[4 * px + 2 * py + pc], got[w].at[k], send.at[7 * w + k], recv.at[7 * w + k], (px, py, pc)).start()
        token[...] = jnp.zeros_like(token)

    outs = pl.pallas_call(
        body, name=name, in_specs=[_HBM] * (2 * nw), out_specs=[_HBM] * (2 * nw) + [_SEM, _SEM, _VMEM],
        out_shape=[pltpu.HBM(a.shape, a.dtype) for a in list(grads) + lands]
        + [pltpu.SemaphoreType.DMA((7 * nw,)), pltpu.SemaphoreType.DMA((7 * nw,)), jax.ShapeDtypeStruct((8, 128), f32)],
        input_output_aliases={i: i for i in range(2 * nw)},
        compiler_params=pltpu.CompilerParams(has_side_effects=_EFFECT),
    )(*[pltpu.with_memory_space_constraint(a, pltpu.HBM) for a in list(grads) + lands])
    return list(outs[:nw]), list(outs[nw:2 * nw]), (outs[2 * nw], outs[2 * nw + 1]), outs[2 * nw + 2]


def _grad_exchange_wait(name, grads, got, sems, after):
    nw = len(grads)

    def body(*refs):
        src = refs[:nw]
        land = refs[nw:2 * nw]
        send, recv = refs[2 * nw], refs[2 * nw + 1]
        x, y, c = _place()
        for w in range(nw):
            for k, (px, py, pc) in enumerate(_other_devices(x, y, c)):
                cp = _remote(src[w].at[4 * px + 2 * py + pc], land[w].at[k], send.at[7 * w + k], recv.at[7 * w + k], (px, py, pc))
                cp.wait_send()
                cp.wait_recv()

    outs = pl.pallas_call(
        body, name=name, in_specs=[_HBM] * (2 * nw) + [_SEM, _SEM, _ANY], out_specs=[_HBM] * (2 * nw),
        out_shape=[pltpu.HBM(a.shape, a.dtype) for a in list(grads) + list(got)],
        input_output_aliases={i: i for i in range(2 * nw)},
        compiler_params=pltpu.CompilerParams(has_side_effects=_EFFECT),
    )(*grads, *got, *sems, after)
    return list(outs[:nw]), list(outs[nw:])


def _swap_halves_start(finals):
    nw = len(finals)

    def body(*refs):
        buf = refs[nw:2 * nw]
        send, recv, token = refs[2 * nw:]
        x, y, c = _place()
        for w in range(nw):
            _remote(buf[w].at[c], buf[w].at[c], send.at[w], recv.at[w], (x, y, 1 - c)).start()
        token[...] = jnp.zeros_like(token)

    outs = pl.pallas_call(
        body, name="rs_swap_start", in_specs=[_HBM] * nw, out_specs=[_HBM] * nw + [_SEM, _SEM, _VMEM],
        out_shape=[pltpu.HBM(g.shape, g.dtype) for g in finals] + [pltpu.SemaphoreType.DMA((nw,))] * 2
        + [jax.ShapeDtypeStruct((8, 128), f32)],
        input_output_aliases={i: i for i in range(nw)},
        compiler_params=pltpu.CompilerParams(has_side_effects=_EFFECT),
    )(*[pltpu.with_memory_space_constraint(g, pltpu.HBM) for g in finals])
    return list(outs[:nw]), (outs[nw], outs[nw + 1]), outs[nw + 2]


def _swap_halves_wait(bufs, sems, after):
    nw = len(bufs)

    def body(*refs):
        buf = refs[:nw]
        send, recv = refs[nw], refs[nw + 1]
        x, y, c = _place()
        for w in range(nw):
            cp = _remote(buf[w].at[c], buf[w].at[1 - c], send.at[w], recv.at[w], (x, y, 1 - c))
            cp.wait_send()
            cp.wait_recv()

    return pl.pallas_call(
        body, name="rs_swap_wait", in_specs=[_HBM] * nw + [_SEM, _SEM, _ANY], out_specs=[_HBM] * nw,
        out_shape=[pltpu.HBM(g.shape, g.dtype) for g in bufs],
        input_output_aliases={i: i for i in range(nw)},
        compiler_params=pltpu.CompilerParams(has_side_effects=_EFFECT),
    )(*bufs, *sems, after)


def _half_slices(shape, h):
    rows, cols = shape
    if cols % 256 == 0:
        return (slice(None), slice(h * (cols // 2), (h + 1) * (cols // 2)))
    return (slice(h * (rows // 2), (h + 1) * (rows // 2)), slice(None))


def _allreduce_small(parts, after):
    n = len(parts)

    def body(*refs):
        src = refs[:n]
        refs = refs[n + 1:]
        out = refs[:n]
        sib = refs[n:2 * n]
        chip_sum = refs[2 * n:3 * n]
        slots = refs[3 * n:4 * n]
        pair_send, pair_recv, ici_send, ici_recv, swap_send, swap_recv = refs[4 * n:]
        x, y, c = _place()
        me_chip = 2 * x + y
        chips = _other_chips(x, y)
        pairs = [_remote(src[a], sib[a], pair_send.at[a], pair_recv.at[a], (x, y, 1 - c)) for a in range(n)]
        for rc in pairs:
            rc.start()
        for a in range(n):
            pairs[a].wait_recv()
            chip_sum[a][...] = src[a][...] + sib[a][...]
        for h in (0, 1):
            @pl.when(c == h)
            def _():
                sends = []
                for a in range(n):
                    idx = _half_slices(parts[a].shape, h)
                    for j, chip in enumerate(chips):
                        rc = _remote(chip_sum[a].at[idx], slots[a].at[me_chip].at[idx], ici_send.at[3 * a + j], ici_recv.at[3 * a + j], (*chip, h))
                        rc.start()
                        sends.append(rc)
                    slots[a][(me_chip,) + idx] = chip_sum[a][idx]
                for a in range(n):
                    idx = _half_slices(parts[a].shape, h)
                    for j, chip in enumerate(chips):
                        landed = slots[a].at[2 * chip[0] + chip[1]].at[idx]
                        _remote(landed, landed, ici_send.at[3 * a + j], ici_recv.at[3 * a + j], (x, y, c)).wait_recv()
                    total = slots[a][(0,) + idx]
                    for s in range(1, 4):
                        total = total + slots[a][(s,) + idx]
                    out[a][idx] = total
                    rc = _remote(out[a].at[idx], out[a].at[idx], swap_send.at[a], swap_recv.at[a], (x, y, 1 - h))
                    rc.start()
                    sends.append(rc)
                for a in range(n):
                    other = out[a].at[_half_slices(parts[a].shape, 1 - h)]
                    _remote(other, other, swap_send.at[a], swap_recv.at[a], (x, y, c)).wait_recv()
                for rc in sends:
                    rc.wait_send()
        for rc in pairs:
            rc.wait_send()

    return pl.pallas_call(
        body, name="allreduce_small", in_specs=[_VMEM] * n + [_ANY], out_specs=[_VMEM] * n,
        out_shape=[jax.ShapeDtypeStruct(p.shape, f32) for p in parts],
        scratch_shapes=[pltpu.VMEM(p.shape, f32) for p in parts] * 2 + [pltpu.VMEM((4,) + p.shape, f32) for p in parts]
        + [pltpu.SemaphoreType.DMA((n,)), pltpu.SemaphoreType.DMA((n,)), pltpu.SemaphoreType.DMA((3 * n,)),
           pltpu.SemaphoreType.DMA((3 * n,)), pltpu.SemaphoreType.DMA((n,)), pltpu.SemaphoreType.DMA((n,))],
        compiler_params=pltpu.CompilerParams(vmem_limit_bytes=_VMEM_LIMIT_BYTES),
    )(*parts, after)


def _local_step(x, mem, tgt, g_mix, g_xattn, g_mem, g_ffn, g_final, cb, lg, lb, pw, ps, fb, started, relay, weights, reduce,
                n_seq, seq, n_mem):
    t, d = x.shape
    f = fb.shape[1] // 2
    c = cb.shape[1]
    h1 = _rms_fwd("norm_mix", x, g_mix, after=started)
    relay(0, h1)
    w_in, cw, fw = weights(0, h1)
    u = _mm_nn("proj_in", h1, w_in, _ACT, w_in.shape[1])
    y, hc = _mix_fwd(u, cw, cb, lg, lb, pw, ps, seq)
    relay(1, y)
    w_out, w_q, w_kv, w_o = weights(1, y)
    x1, h2 = _proj_residual_norm("proj_out", y, w_out, x, g_xattn)
    q = _mm_nn("proj_q", h2, w_q, _ACT, d)
    mem_n = _rms_fwd("norm_mem", mem, g_mem)
    kv = _mm_nn("proj_kv", mem_n, w_kv, _ACT, 2 * d)
    o = _attn_fwd(q, kv, n_seq, seq, n_mem)
    x2, h3 = _proj_residual_norm("proj_o", o, w_o, x1, g_ffn, after=relay(2, o))
    w_up, w_down = weights(2, h3)
    up = _mm_nn("proj_up", h3, w_up, _ACT, f, split_out=True)
    a, uc = _ffn_gate_fwd(up, fw, fb, seq)
    dx3, dx3b, dg_final, loss = _proj_loss_bwd("proj_down", a, w_down, x2, g_final, tgt)
    da = _mm_nt("d_act", dx3b, w_down, _ACT)
    gw_down = _mm_tn_rows("dw_down", a, dx3b, f // 2, d // 2)
    dup, sums_g, sums_v = _ffn_gate_bwd(up, uc, da, fw, seq)
    gw_up = _mm_tn_pieces("dw_up", h3, dup, f // 2)
    token = reduce(0, [gw_down.reshape(8, -1, d), gw_up])
    dx2, dx2b, dg_ffn = _dproj_rms_bwd("d_h3", dup, w_up, x2, g_ffn, dx3, after=token)
    do = _mm_nt("d_o", dx2b, w_o, _ACT)
    gw_o = _mm_tn_rows("dw_o", o, dx2b, d, d // 2)
    dq, dkv = _attn_bwd(q, kv, do, n_seq, seq, n_mem)
    gw_q = _mm_tn_rows("dw_q", h2, dq, d, d // 2)
    gw_kv = _mm_tn_pieces("dw_kv", mem_n, dkv, d // 2)
    dmem_n = _mm_nt("d_mem_n", dkv, w_kv, f32)
    dg_mem = _rms_gain_grad("norm_mem_bwd", mem, dmem_n)
    dx1, dx1b, dg_xattn = _dproj_rms_bwd("d_h2", dq, w_q, x1, g_xattn, dx2)
    dy = _mm_nt("d_y", dx1b, w_out, _ACT)
    gw_out = _mm_tn_rows("dw_out", y, dx1b, d, d // 2)
    token = reduce(1, [gw_o.reshape(8, -1, d), gw_q.reshape(8, -1, d), gw_kv, gw_out.reshape(8, -1, d)])
    dhc, sums_norm = _mix_bwd_norm(hc, dy, lg, lb, token)
    du, d_cw, d_ps, d_pw = _mix_bwd_taps(u, dhc, dy, cw, pw, ps, seq)
    gw_in = _mm_tn_pieces("dw_in", h1, du, c * 3 // 4)
    token = reduce(2, [gw_in])
    grad_x, dg_mix = _dproj_rms_bwd("d_h1", du, w_in, x, g_mix, dx1, storage_copy=False, after=token)
    zero_row = jnp.zeros((1, d), f32)
    gains = jnp.concatenate([dg_mix, dg_xattn, dg_mem, dg_ffn, dg_final, jnp.pad(loss, ((0, 0), (0, d - 1))), zero_row, zero_row], axis=0)
    conv_rows = jnp.concatenate([sums_norm[2:3], sums_norm[0:1], sums_norm[1:2], d_ps[0:1], jnp.zeros((4, c), f32)], axis=0)
    ffn_rows = jnp.concatenate([sums_g, sums_v], axis=1)
    small = [gains, conv_rows, d_pw.reshape(-1, d_pw.shape[-1]), ffn_rows, d_cw]
    return grad_x, small


def kernel(x, mem, norm_mix_g, w_in, conv_dw_w, conv_dw_b, conv_ln_g, conv_ln_b, pool_w, pool_scale, w_out, norm_xattn_g, norm_mem_g, w_q, w_kv, w_o, norm_ffn_g, w_up, ffn_dw_w, ffn_dw_b, w_down, norm_final_g, loss_target, m_norm_mix_g, m_w_in, m_conv_dw_w, m_conv_dw_b, m_conv_ln_g, m_conv_ln_b, m_pool_w, m_pool_scale, m_w_out, m_norm_xattn_g, m_norm_mem_g, m_w_q, m_w_kv, m_w_o, m_norm_ffn_g, m_w_up, m_ffn_dw_w, m_ffn_dw_b, m_w_down, m_norm_final_g, v_norm_mix_g, v_w_in, v_conv_dw_w, v_conv_dw_b, v_conv_ln_g, v_conv_ln_b, v_pool_w, v_pool_scale, v_w_out, v_norm_xattn_g, v_norm_mem_g, v_w_q, v_w_kv, v_w_o, v_norm_ffn_g, v_w_up, v_ffn_dw_w, v_ffn_dw_b, v_w_down, v_norm_final_g):
    n_seq, seq, d = x.shape
    n_mem = mem.shape[1]
    chip = 2 * lax.axis_index("x") + lax.axis_index("y")

    place = jnp.stack([chip, lax.axis_index("c")]).astype(jnp.int32)

    col_w = [w_in, w_kv, w_up]
    row_w = [w_out, w_q, w_o, w_down]
    kw = conv_dw_w.shape[1]

    def padded_in_place(shard, rows):
        full = jnp.zeros((rows, 4 * shard.shape[1]), shard.dtype)
        return lax.dynamic_update_slice(full, shard, (0, chip * shard.shape[1]))

    first = list(_place_shards("place_w_in", place, [w_in[0]], [True]))
    first += [padded_in_place(conv_dw_w[0], _HALO), padded_in_place(ffn_dw_w[0], 8)]
    first, first_sems, token = _allgather_start("allgather_start_0", first, [True] * 3, [False, True, True], [[0, 1, 2]])
    rest = [w_kv, w_up, w_out, w_q, w_o, w_down]
    rest_flags = [True, True, False, False, False, False]
    rest = list(_place_shards("place_rest", place, [w[0] for w in rest], rest_flags, after=token))
    rest, rest_sems, all_started = _allgather_start("allgather_start_1", rest, rest_flags, [False] * 6, [[2, 3, 0, 4], [1, 5]])
    started = [(first, [True] * 3, [False, True, True], first_sems[0]),
               ([rest[i] for i in (2, 3, 0, 4)], [False, False, True, False], [False] * 4, rest_sems[0]),
               ([rest[i] for i in (1, 5)], [True, False], [False] * 2, rest_sems[1])]
    relayed = {}

    def relay(g, after):
        group_bufs, flags, wholes, group_sems = started[g]
        group_bufs, sibling_sems, relay_token = _allgather_relay("allgather_relay_%d" % g, group_bufs, flags, wholes, group_sems, after)
        relayed[g] = (group_bufs, sibling_sems)
        return relay_token

    def weights(g, after):
        group_bufs, sibling_sems = relayed[g]
        return _allgather_wait("allgather_wait_%d" % g, group_bufs, started[g][1], started[g][2], sibling_sems, after)

    names = ["w_in", "w_kv", "w_up", "w_out", "w_q", "w_o", "w_down"]
    reduce_groups = [["w_down", "w_up"], ["w_o", "w_q", "w_kv", "w_out"], ["w_in"]]
    in_flight = {}

    def reduce(g, grads):
        grads, lands, rs_sems, token = _grad_exchange_start("rs_start_%d" % g, grads)
        in_flight[g] = (grads, lands, rs_sems)
        return token

    grad_x, small = _local_step(
        x.reshape(n_seq * seq, d), mem.reshape(n_seq * n_mem, d), loss_target.reshape(n_seq * seq, d),
        norm_mix_g, norm_xattn_g, norm_mem_g, norm_ffn_g, norm_final_g.reshape(1, d),
        conv_dw_b, conv_ln_g, conv_ln_b, pool_w[0], pool_scale, ffn_dw_b, all_started, relay, weights, reduce,
        n_seq, seq, n_mem)

    landed = {}
    for g, members in enumerate(reduce_groups):
        grads, lands, rs_sems = in_flight[g]
        grads, lands = _grad_exchange_wait("rs_wait_%d" % g, grads, lands, rs_sems, grad_x)
        landed.update(zip(members, zip(grads, lands)))
    finals = _sum_partials("rs_sum", place, [landed[n][0] for n in names], [landed[n][1] for n in names])
    finals, swap_sems, token = _swap_halves_start(finals)

    gains, conv_rows, d_pw, ffn_rows, d_cw = _allreduce_small(small, token)
    loss = gains[5, 0]
    shard_grads = _swap_halves_wait(finals, swap_sems, gains)

    outs = {}
    big_w = dict(zip(names, col_w + row_w))
    big_m = dict(w_in=m_w_in, w_kv=m_w_kv, w_up=m_w_up, w_out=m_w_out, w_q=m_w_q, w_o=m_w_o, w_down=m_w_down)
    big_v = dict(w_in=v_w_in, w_kv=v_w_kv, w_up=v_w_up, w_out=v_w_out, w_q=v_w_q, w_o=v_w_o, w_down=v_w_down)
    big_quads = [(big_w[n], g.reshape(big_w[n].shape[1:]), big_m[n], big_v[n]) for n, g in zip(names, shard_grads)]
    outs.update(zip(names, _adamw_shards(big_quads)))

    f2 = ffn_dw_b.shape[1]
    cs_c = conv_dw_w.shape[2]
    cs_f = ffn_dw_w.shape[2]
    g_cw = lax.dynamic_slice(d_cw, (0, chip * cs_c), (kw, cs_c)).reshape(conv_dw_w.shape)
    g_fw = lax.dynamic_slice(ffn_rows, (1, chip * cs_f), (ffn_dw_w.shape[1], cs_f)).reshape(ffn_dw_w.shape)
    small_params = [
        ("norm_mix_g", norm_mix_g, gains[0:1], m_norm_mix_g, v_norm_mix_g),
        ("conv_dw_w", conv_dw_w, g_cw, m_conv_dw_w, v_conv_dw_w),
        ("conv_dw_b", conv_dw_b, conv_rows[0:1], m_conv_dw_b, v_conv_dw_b),
        ("conv_ln_g", conv_ln_g, conv_rows[1:2], m_conv_ln_g, v_conv_ln_g),
        ("conv_ln_b", conv_ln_b, conv_rows[2:3], m_conv_ln_b, v_conv_ln_b),
        ("pool_w", pool_w, d_pw.reshape(pool_w.shape), m_pool_w, v_pool_w),
        ("pool_scale", pool_scale, conv_rows[3:4], m_pool_scale, v_pool_scale),
        ("norm_xattn_g", norm_xattn_g, gains[1:2], m_norm_xattn_g, v_norm_xattn_g),
        ("norm_mem_g", norm_mem_g, gains[2:3], m_norm_mem_g, v_norm_mem_g),
        ("norm_ffn_g", norm_ffn_g, gains[3:4], m_norm_ffn_g, v_norm_ffn_g),
        ("ffn_dw_w", ffn_dw_w, g_fw, m_ffn_dw_w, v_ffn_dw_w),
        ("ffn_dw_b", ffn_dw_b, ffn_rows[0:1, :f2], m_ffn_dw_b, v_ffn_dw_b),
        ("norm_final_g", norm_final_g.reshape(1, d), gains[4:5], m_norm_final_g.reshape(1, d), v_norm_final_g.reshape(1, d)),
    ]
    quads = []
    for _, w, g, m, v in small_params:
        shape2 = w.shape if w.ndim == 3 else (-1, w.shape[-1])
        quads.append((w.reshape(shape2), g.reshape(shape2), m.reshape(shape2), v.reshape(shape2)))
    for (n, w, g, _, _), (delta, new_m, new_v) in zip(small_params, _adamw_small(quads)):
        shape = norm_final_g.shape if n == "norm_final_g" else w.shape
        outs[n] = (g.reshape(shape), delta.reshape(shape), new_m.reshape(shape), new_v.reshape(shape))

    order = ["norm_mix_g", "w_in", "conv_dw_w", "conv_dw_b", "conv_ln_g", "conv_ln_b", "pool_w", "pool_scale", "w_out",
             "norm_xattn_g", "norm_mem_g", "w_q", "w_kv", "w_o", "norm_ffn_g", "w_up", "ffn_dw_w", "ffn_dw_b", "w_down",
             "norm_final_g"]
    return (loss, grad_x.reshape(x.shape), *[outs[n][0] for n in order], *[outs[n][1] for n in order],
            *[outs[n][2] for n in order], *[outs[n][3] for n in order])
```

```python
import jax
import jax.numpy as jnp
from jax import lax
from jax.experimental import pallas as pl
from jax.experimental.pallas import tpu as pltpu

f32 = jnp.float32
_ACT = jnp.bfloat16

EPS = 1e-6
POOL_WINDOWS = (2, 4, 8, 16)
XATTN_HEADS = 4
ADAM_LR = 0.001
ADAM_B1 = 0.9
ADAM_B2 = 0.999
ADAM_EPS = 1e-08
ADAM_WD = 0.01
ADAM_STEP = 10

_VMEM_LIMIT_BYTES = 56 * 1024 * 1024
_MESH = pl.DeviceIdType.MESH
_ANY = pl.BlockSpec(memory_space=pl.ANY)
_VMEM = pl.BlockSpec(memory_space=pltpu.VMEM)
_HBM = pl.BlockSpec(memory_space=pltpu.HBM)
_SEM = pl.BlockSpec(memory_space=pltpu.SEMAPHORE)
_EFFECT = pltpu.SideEffectType.DATAFLOW_SIDE_EFFECTING

_NN = (((1,), (0,)), ((), ()))
_NT = (((1,), (1,)), ((), ()))
_TN = (((0,), (0,)), ((), ()))


def _params(n_grid):
    return pltpu.CompilerParams(dimension_semantics=("arbitrary",) * n_grid, vmem_limit_bytes=_VMEM_LIMIT_BYTES)


def _sigmoid(v):
    return 1.0 / (1.0 + jnp.exp(-v))


def _dot(a, b, dims):
    return lax.dot_general(a, b, dims, preferred_element_type=f32)


def _mm(name, a, b, *, dims, grid, a_spec, b_spec, o_spec, out_shape):
    def body(a_ref, b_ref, o_ref):
        o_ref[...] = _dot(a_ref[...], b_ref[...], dims).astype(o_ref.dtype)

    return pl.pallas_call(
        body, name=name, grid=grid, in_specs=[a_spec, b_spec], out_specs=o_spec, out_shape=out_shape,
        compiler_params=_params(len(grid)),
    )(a, b)


_NARROW = 2816


def _row_tile(m, width=_NARROW + 1):
    return min(1024 if width <= _NARROW else 512, m)


def _mm_nn(name, a, b, out_dtype, tn, split_out=False):
    m, k = a.shape
    n = b.shape[1]
    tm = _row_tile(m, max(k, tn))
    if split_out:
        out_shape = jax.ShapeDtypeStruct((n // tn, m, tn), out_dtype)
        o_spec = pl.BlockSpec((None, tm, tn), lambda j, i: (j, i, 0))
    else:
        out_shape = jax.ShapeDtypeStruct((m, n), out_dtype)
        o_spec = pl.BlockSpec((tm, tn), lambda j, i: (i, j))
    return _mm(
        name, a, b, dims=_NN, grid=(n // tn, m // tm),
        a_spec=pl.BlockSpec((tm, k), lambda j, i: (i, 0)), b_spec=pl.BlockSpec((k, tn), lambda j, i: (0, j)),
        o_spec=o_spec, out_shape=out_shape,
    )


def _mm_nt(name, a, b, out_dtype):
    n, kc = b.shape
    m = a.shape[0]
    tm = _row_tile(m, max(n, kc))
    return _mm(
        name, a, b, dims=_NT, grid=(m // tm,),
        a_spec=pl.BlockSpec((tm, kc), lambda i: (i, 0)),
        b_spec=pl.BlockSpec((n, kc), lambda i: (0, 0), pipeline_mode=pl.Buffered(1)),
        o_spec=pl.BlockSpec((tm, n), lambda i: (i, 0)),
        out_shape=jax.ShapeDtypeStruct((m, n), out_dtype),
    )


def _mm_tn_rows(name, a, b, tka, tn):
    m, ka = a.shape
    nb = b.shape[1]
    return _mm(
        name, a, b, dims=_TN, grid=(ka // tka, nb // tn),
        a_spec=pl.BlockSpec((m, tka), lambda i, j: (0, i)), b_spec=pl.BlockSpec((m, tn), lambda i, j: (0, j)),
        o_spec=pl.BlockSpec((tka, tn), lambda i, j: (i, j)),
        out_shape=jax.ShapeDtypeStruct((ka, nb), _ACT),
    )


def _mm_tn_pieces(name, a, b, cs):
    m, ka = a.shape
    if b.ndim == 3:
        b_spec = pl.BlockSpec((None, m, cs), lambda i, j: (j // 2, 0, j % 2))
    else:
        b_spec = pl.BlockSpec((m, cs), lambda i, j: (0, j))
    return _mm(
        name, a, b, dims=_TN, grid=(2, 4),
        a_spec=pl.BlockSpec((m, ka // 2), lambda i, j: (0, i)), b_spec=b_spec,
        o_spec=pl.BlockSpec((None, ka // 2, cs), lambda i, j: (2 * j + i, 0, 0)),
        out_shape=jax.ShapeDtypeStruct((8, ka // 2, cs), _ACT),
    )


def _after(after):
    return ([], []) if after is None else ([after], [_ANY])


def _rms_fwd(name, x, g, after=None):
    t, d = x.shape
    tm = _row_tile(t, d)
    more, more_specs = _after(after)

    def body(x_ref, g_ref, *refs):
        h_ref = refs[-1]
        xv = x_ref[...]
        r = lax.rsqrt(jnp.mean(xv * xv, axis=-1, keepdims=True) + EPS)
        h_ref[...] = (xv * r * g_ref[...]).astype(h_ref.dtype)

    return pl.pallas_call(
        body, name=name, grid=(t // tm,),
        in_specs=[pl.BlockSpec((tm, d), lambda i: (i, 0)), pl.BlockSpec((1, d), lambda i: (0, 0))] + more_specs,
        out_specs=pl.BlockSpec((tm, d), lambda i: (i, 0)), out_shape=jax.ShapeDtypeStruct((t, d), _ACT),
        compiler_params=_params(1),
    )(x, g, *more)


def _fused_rows(name, a, b, product, a_spec, tm, extras, extra_specs, out_shape, out_specs, epilogue):
    ne = len(extras)

    def body(a_ref, b_ref, *refs):
        epilogue(product(a_ref, b_ref), refs[:ne], refs[ne:])

    m = extras[0].shape[0]
    return pl.pallas_call(
        body, name=name, grid=(m // tm,),
        in_specs=[a_spec, pl.BlockSpec(b.shape, lambda i: (0, 0), pipeline_mode=pl.Buffered(1)), *extra_specs],
        out_specs=out_specs, out_shape=out_shape, compiler_params=_params(1),
    )(a, b, *extras)


def _proj_residual_norm(name, a, b, res, g, after=None):
    m, k = a.shape
    d = b.shape[1]
    tm = _row_tile(m, max(k, d))

    def epilogue(p, ins, outs):
        xv = p + ins[0][...]
        outs[0][...] = xv
        r = lax.rsqrt(jnp.mean(xv * xv, axis=-1, keepdims=True) + EPS)
        outs[1][...] = (xv * r * ins[1][...]).astype(outs[1].dtype)

    row = pl.BlockSpec((tm, d), lambda i: (i, 0))
    return _fused_rows(
        name, a, b, lambda a_ref, b_ref: _dot(a_ref[...], b_ref[...], _NN), pl.BlockSpec((tm, k), lambda i: (i, 0)), tm,
        [res, g] + _after(after)[0], [row, pl.BlockSpec((1, d), lambda i: (0, 0))] + _after(after)[1],
        [jax.ShapeDtypeStruct((m, d), f32), jax.ShapeDtypeStruct((m, d), _ACT)], [row, row], epilogue)


def _dproj_rms_bwd(name, a, b, x, g, dres, storage_copy=True, after=None):
    m, d = x.shape
    if a.ndim == 3:
        nh, _, kh = a.shape
        tm = _row_tile(m, nh * kh)
        a_spec = pl.BlockSpec((nh, tm, kh), lambda i: (0, i, 0))

        def product(a_ref, b_ref):
            p = _dot(a_ref[0], b_ref[:, 0:kh], _NT)
            for h in range(1, nh):
                p = p + _dot(a_ref[h], b_ref[:, h * kh:(h + 1) * kh], _NT)
            return p
    else:
        tm = _row_tile(m, max(a.shape[1], d))
        a_spec = pl.BlockSpec((tm, a.shape[1]), lambda i: (i, 0))

        def product(a_ref, b_ref):
            return _dot(a_ref[...], b_ref[...], _NT)

    def epilogue(dhv, ins, outs):
        x_ref, g_ref, dres_ref = ins[:3]
        dg_ref = outs[-1]

        @pl.when(pl.program_id(0) == 0)
        def _():
            dg_ref[...] = jnp.zeros_like(dg_ref)

        xv = x_ref[...]
        r = lax.rsqrt(jnp.mean(xv * xv, axis=-1, keepdims=True) + EPS)
        xn = xv * r
        dxn = dhv * g_ref[...]
        dx = r * (dxn - xn * jnp.mean(dxn * xn, axis=-1, keepdims=True)) + dres_ref[...]
        outs[0][...] = dx
        if storage_copy:
            outs[1][...] = dx.astype(outs[1].dtype)
        dg_ref[...] += jnp.sum(dhv * xn, axis=0, keepdims=True)

    row = pl.BlockSpec((tm, d), lambda i: (i, 0))
    vec = pl.BlockSpec((1, d), lambda i: (0, 0))
    copies = [jax.ShapeDtypeStruct((m, d), _ACT)] if storage_copy else []
    return _fused_rows(
        name, a, b, product, a_spec, tm, [x, g, dres] + _after(after)[0], [row, vec, row] + _after(after)[1],
        [jax.ShapeDtypeStruct((m, d), f32)] + copies + [jax.ShapeDtypeStruct((1, d), f32)],
        [row] * (1 + len(copies)) + [vec], epilogue)


def _proj_loss_bwd(name, a, b, res, g, tgt):
    m, k = a.shape
    d = b.shape[1]
    tm = _row_tile(m, max(k, d))

    def epilogue(p, ins, outs):
        res_ref, g_ref, t_ref = ins
        dx_ref, dxb_ref, dg_ref, loss_ref = outs

        @pl.when(pl.program_id(0) == 0)
        def _():
            dg_ref[...] = jnp.zeros_like(dg_ref)
            loss_ref[...] = jnp.zeros_like(loss_ref)

        xv = p + res_ref[...]
        gv = g_ref[...]
        r = lax.rsqrt(jnp.mean(xv * xv, axis=-1, keepdims=True) + EPS)
        xn = xv * r
        err = xn * gv - t_ref[...]
        loss_ref[...] += 0.5 * jnp.sum(jnp.mean(err * err, axis=-1, keepdims=True), axis=0, keepdims=True)
        dout = err * (1.0 / d)
        dxn = dout * gv
        dx = r * (dxn - xn * jnp.mean(dxn * xn, axis=-1, keepdims=True))
        dx_ref[...] = dx
        dxb_ref[...] = dx.astype(dxb_ref.dtype)
        dg_ref[...] += jnp.sum(dout * xn, axis=0, keepdims=True)

    row = pl.BlockSpec((tm, d), lambda i: (i, 0))
    vec = pl.BlockSpec((1, d), lambda i: (0, 0))
    return _fused_rows(
        name, a, b, lambda a_ref, b_ref: _dot(a_ref[...], b_ref[...], _NN), pl.BlockSpec((tm, k), lambda i: (i, 0)), tm,
        [res, g, tgt], [row, vec, row],
        [jax.ShapeDtypeStruct((m, d), f32), jax.ShapeDtypeStruct((m, d), _ACT), jax.ShapeDtypeStruct((1, d), f32),
         jax.ShapeDtypeStruct((1, 1), f32)],
        [row, row, vec, pl.BlockSpec((1, 1), lambda i: (0, 0))], epilogue)


def _rms_gain_grad(name, x, dh):
    t, d = x.shape
    tm = _row_tile(t)

    def body(x_ref, dh_ref, dg_ref):
        @pl.when(pl.program_id(0) == 0)
        def _():
            dg_ref[...] = jnp.zeros_like(dg_ref)

        xv = x_ref[...]
        r = lax.rsqrt(jnp.mean(xv * xv, axis=-1, keepdims=True) + EPS)
        dg_ref[...] += jnp.sum(dh_ref[...] * (xv * r), axis=0, keepdims=True)

    row = pl.BlockSpec((tm, d), lambda i: (i, 0))
    return pl.pallas_call(
        body, name=name, grid=(t // tm,), in_specs=[row, row], out_specs=pl.BlockSpec((1, d), lambda i: (0, 0)),
        out_shape=jax.ShapeDtypeStruct((1, d), f32), compiler_params=_params(1),
    )(x, dh)


_CONV_ROWS = 512
_CHUNK = 64
_HALO = 32


def _pool_counts(pos, w):
    return jnp.minimum(pos + 1.0, float(w))


def _rows_from(win, start, rows):
    if start % 8 == 0:
        return win[start:start + rows, :]
    n = win.shape[0]
    return pltpu.roll(win, n - start % 8, axis=0)[start - start % 8:start - start % 8 + rows, :]


def _tap_rows(buf, starts, rows):
    for residue in range(8):
        group = [(k, s) for k, s in starts.items() if s % 8 == residue]
        if group:
            lo = min(s for _, s in group) - residue
            hi = max(s for _, s in group) - residue + rows + (8 if residue else 0)
            win = buf[lo:hi, :]
            if residue:
                win = pltpu.roll(win, hi - lo - residue, axis=0)
            for k, s in group:
                yield k, win[s - residue - lo:s - residue - lo + rows, :]


def _mix_fwd(u, cw, cb, lg, lb, pw, ps, seq):
    t, c3 = u.shape
    c = c3 // 3
    kw = 31
    tm = min(_CONV_ROWS, seq)
    tps = seq // tm
    gd = c // len(POOL_WINDOWS)

    def body(u_ref, uh_ref, cw_ref, cb_ref, lg_ref, lb_ref, pw_ref, ps_ref, y_ref, hc_ref, hgbuf, pbuf):
        i = pl.program_id(0)
        keep = jnp.where(i % tps == 0, 0.0, 1.0)
        um = u_ref[...].astype(f32)
        uh = uh_ref[...].astype(f32) * keep
        hgbuf[0:_HALO, :] = uh[:, 0:c] * _sigmoid(uh[:, c:2 * c])
        hgbuf[_HALO:_HALO + tm, :] = um[:, 0:c] * _sigmoid(um[:, c:2 * c])
        pbuf[0:_HALO, :] = uh[:, 2 * c:]
        pbuf[_HALO:_HALO + tm, :] = um[:, 2 * c:]
        for r0 in range(0, tm, _CHUNK):
            acc = jnp.broadcast_to(cb_ref[...], (_CHUNK, c))
            for k, rows in _tap_rows(hgbuf, {k: r0 + _HALO - (kw - 1) + k for k in range(kw)}, _CHUNK):
                acc = acc + cw_ref[k:k + 1, :] * rows
            hc_ref[r0:r0 + _CHUNK, :] = acc
            mu = jnp.mean(acc, axis=-1, keepdims=True)
            xc = acc - mu
            var = jnp.mean(xc * xc, axis=-1, keepdims=True)
            hl = xc * lax.rsqrt(var + EPS) * lg_ref[...] + lb_ref[...]
            y_ref[r0:r0 + _CHUNK, 0:c] = (hl * _sigmoid(hl)).astype(y_ref.dtype)
        pos = ((i % tps) * tm).astype(f32) + lax.broadcasted_iota(jnp.int32, (tm, 1), 0).astype(f32)
        for gi, w in enumerate(POOL_WINDOWS):
            sl = slice(gi * gd, (gi + 1) * gd)
            v = pbuf[_HALO:_HALO + tm, sl]
            s = v
            for j in range(1, w):
                s = s + pbuf[_HALO - j:_HALO - j + tm, sl]
            pooled = s / _pool_counts(pos, w) - v
            mixed = _dot(pooled.astype(_ACT), pw_ref[gi].astype(_ACT), _NN)
            y_ref[:, c + gi * gd:c + (gi + 1) * gd] = (mixed * ps_ref[:, sl]).astype(y_ref.dtype)

    hb = tm // _HALO
    full = lambda shape: pl.BlockSpec(shape, lambda i: (0,) * len(shape))
    return pl.pallas_call(
        body, name="mix_fwd", grid=(t // tm,),
        in_specs=[pl.BlockSpec((tm, c3), lambda i: (i, 0)),
                  pl.BlockSpec((_HALO, c3), lambda i: (jnp.maximum(i * hb - 1, 0), 0)),
                  full((_HALO, c)), full((1, c)), full((1, c)), full((1, c)), full((len(POOL_WINDOWS), gd, gd)), full((1, c))],
        out_specs=[pl.BlockSpec((tm, 2 * c), lambda i: (i, 0)), pl.BlockSpec((tm, c), lambda i: (i, 0))],
        out_shape=[jax.ShapeDtypeStruct((t, 2 * c), _ACT), jax.ShapeDtypeStruct((t, c), f32)],
        scratch_shapes=[pltpu.VMEM((_HALO + tm, c), f32), pltpu.VMEM((_HALO + tm, c), f32)],
        compiler_params=_params(1),
    )(u, u, cw, cb, lg, lb, pw, ps)


def _mix_bwd_norm(hc, dy, lg, lb, after):
    t, c = hc.shape
    tm = _row_tile(t, c)

    def body(hc_ref, dy_ref, lg_ref, lb_ref, after_ref, dhc_ref, sums_ref):
        @pl.when(pl.program_id(0) == 0)
        def _():
            sums_ref[...] = jnp.zeros_like(sums_ref)

        hcv = hc_ref[...]
        mu = jnp.mean(hcv, axis=-1, keepdims=True)
        xc = hcv - mu
        rstd = lax.rsqrt(jnp.mean(xc * xc, axis=-1, keepdims=True) + EPS)
        n = xc * rstd
        hl = n * lg_ref[...] + lb_ref[...]
        sg = _sigmoid(hl)
        dhl = dy_ref[...].astype(f32) * (sg * (1.0 + hl * (1.0 - sg)))
        dn = dhl * lg_ref[...]
        dhc = rstd * (dn - jnp.mean(dn, axis=-1, keepdims=True) - n * jnp.mean(dn * n, axis=-1, keepdims=True))
        dhc_ref[...] = dhc
        sums_ref[0:1, :] += jnp.sum(dhl * n, axis=0, keepdims=True)
        sums_ref[1:2, :] += jnp.sum(dhl, axis=0, keepdims=True)
        sums_ref[2:3, :] += jnp.sum(dhc, axis=0, keepdims=True)

    row = pl.BlockSpec((tm, c), lambda i: (i, 0))
    vec = pl.BlockSpec((1, c), lambda i: (0, 0))
    return pl.pallas_call(
        body, name="mix_bwd_norm", grid=(t // tm,), in_specs=[row, row, vec, vec, _ANY],
        out_specs=[row, pl.BlockSpec((8, c), lambda i: (0, 0))],
        out_shape=[jax.ShapeDtypeStruct((t, c), f32), jax.ShapeDtypeStruct((8, c), f32)],
        compiler_params=_params(1),
    )(hc, dy, lg, lb, after)


def _mix_bwd_taps(u, dhc, dy, cw, pw, ps, seq):
    t, c3 = u.shape
    c = c3 // 3
    kw = 31
    tm = min(_CONV_ROWS, seq)
    tps = seq // tm
    ng = len(POOL_WINDOWS)
    gd = c // ng
    nh = 16

    def body(u_ref, uh_ref, dhc_ref, dhcn_ref, dy_ref, dyn_ref, cw_ref, pw_ref, ps_ref,
             du_ref, dcw_ref, dps_ref, dpw_ref, hgbuf, dcbuf, pbuf, dpbuf):
        i = pl.program_id(0)
        keep_prev = jnp.where(i % tps == 0, 0.0, 1.0)
        keep_next = jnp.where(i % tps == tps - 1, 0.0, 1.0)

        @pl.when(i == 0)
        def _():
            dcw_ref[...] = jnp.zeros_like(dcw_ref)
            dps_ref[...] = jnp.zeros_like(dps_ref)
            dpw_ref[...] = jnp.zeros_like(dpw_ref)

        uh = uh_ref[...].astype(f32) * keep_prev
        hgbuf[0:_HALO, :] = uh[:, 0:c] * _sigmoid(uh[:, c:2 * c])
        pbuf[0:_HALO, :] = uh[:, 2 * c:]
        um = u_ref[...].astype(f32)
        hgbuf[_HALO:_HALO + tm, :] = um[:, 0:c] * _sigmoid(um[:, c:2 * c])
        pbuf[_HALO:_HALO + tm, :] = um[:, 2 * c:]
        dcbuf[0:tm, :] = dhc_ref[...]
        dcbuf[tm:tm + _HALO, :] = dhcn_ref[...] * keep_next
        tap_sums = [None] * kw
        for r0 in range(0, tm, _CHUNK):
            dh = dcbuf[r0:r0 + _CHUNK, :]
            acc = jnp.zeros((_CHUNK, c), f32)
            for k, rows in _tap_rows(hgbuf, {k: r0 + _HALO - (kw - 1) + k for k in range(kw)}, _CHUNK):
                part = (dh * rows).reshape(_CHUNK // 8, 8, c).sum(axis=0)
                tap_sums[k] = part if tap_sums[k] is None else tap_sums[k] + part
            for k, rows in _tap_rows(dcbuf, {k: r0 + (kw - 1) - k for k in range(kw)}, _CHUNK):
                acc = acc + cw_ref[k:k + 1, :] * rows
            val = u_ref[r0:r0 + _CHUNK, 0:c].astype(f32)
            sg = _sigmoid(u_ref[r0:r0 + _CHUNK, c:2 * c].astype(f32))
            du_ref[r0:r0 + _CHUNK, 0:c] = (acc * sg).astype(du_ref.dtype)
            du_ref[r0:r0 + _CHUNK, c:2 * c] = (acc * val * sg * (1.0 - sg)).astype(du_ref.dtype)
        for k in range(kw):
            dcw_ref[k:k + 1, :] += jnp.sum(tap_sums[k], axis=0, keepdims=True)
        base = ((i % tps) * tm).astype(f32)
        pos = base + lax.broadcasted_iota(jnp.int32, (tm, 1), 0).astype(f32)
        pos_next = base + float(tm) + lax.broadcasted_iota(jnp.int32, (nh, 1), 0).astype(f32)
        for gi, w in enumerate(POOL_WINDOWS):
            sl = slice(gi * gd, (gi + 1) * gd)
            v = pbuf[_HALO:_HALO + tm, sl]
            s = v
            for j in range(1, w):
                s = s + pbuf[_HALO - j:_HALO - j + tm, sl]
            cnt = _pool_counts(pos, w)
            pooled = (s / cnt - v).astype(_ACT)
            pwg = pw_ref[gi].astype(_ACT)
            mixed = _dot(pooled, pwg, _NN)
            dyp = dy_ref[:, sl].astype(f32)
            dps_ref[0:1, sl] += jnp.sum(dyp * mixed, axis=0, keepdims=True)
            dmix = (dyp * ps_ref[:, sl]).astype(_ACT)
            dpw_ref[gi] += _dot(pooled, dmix, _TN)
            dmix_next = (dyn_ref[:, sl].astype(f32) * ps_ref[:, sl] * keep_next).astype(_ACT)
            dpool = _dot(dmix, pwg, _NT)
            dpbuf[0:tm, sl] = dpool / cnt
            dpbuf[tm:tm + nh, sl] = _dot(dmix_next, pwg, _NT) / _pool_counts(pos_next, w)
            acc = -dpool
            for j in range(w):
                acc = acc + dpbuf[j:j + tm, sl]
            du_ref[:, 2 * c + gi * gd:2 * c + (gi + 1) * gd] = acc.astype(du_ref.dtype)

    hb = tm // _HALO
    n_halo = t // _HALO
    n_nh = t // nh
    full = lambda shape: pl.BlockSpec(shape, lambda i: (0,) * len(shape))
    return pl.pallas_call(
        body, name="mix_bwd_taps", grid=(t // tm,),
        in_specs=[pl.BlockSpec((tm, c3), lambda i: (i, 0)),
                  pl.BlockSpec((_HALO, c3), lambda i: (jnp.maximum(i * hb - 1, 0), 0)),
                  pl.BlockSpec((tm, c), lambda i: (i, 0)),
                  pl.BlockSpec((_HALO, c), lambda i: (jnp.minimum((i + 1) * hb, n_halo - 1), 0)),
                  pl.BlockSpec((tm, c), lambda i: (i, 1)),
                  pl.BlockSpec((nh, c), lambda i: (jnp.minimum((i + 1) * (tm // nh), n_nh - 1), 1)),
                  full((_HALO, c)), full((ng, gd, gd)), full((1, c))],
        out_specs=[pl.BlockSpec((tm, c3), lambda i: (i, 0)), full((_HALO, c)), full((8, c)), full((ng, gd, gd))],
        out_shape=[jax.ShapeDtypeStruct((t, c3), _ACT), jax.ShapeDtypeStruct((_HALO, c), f32),
                   jax.ShapeDtypeStruct((8, c), f32), jax.ShapeDtypeStruct((ng, gd, gd), f32)],
        scratch_shapes=[pltpu.VMEM((_HALO + tm, c), f32), pltpu.VMEM((tm + _HALO, c), f32),
                        pltpu.VMEM((_HALO + tm, c), f32), pltpu.VMEM((tm + nh, c), f32)],
        compiler_params=_params(1),
    )(u, u, dhc, dhc, dy, dy, cw, pw, ps)


def _attn_fwd(q, kv, n_seq, seq, n_mem):
    t, d = q.shape
    dh = d // XATTN_HEADS
    tq = min(1024, seq)
    nq = seq // tq
    scale = dh ** -0.5

    def body(q_ref, kv_ref, o_ref):
        for h in range(XATTN_HEADS):
            cols = slice(h * dh, (h + 1) * dh)
            s = _dot(q_ref[:, cols], kv_ref[:, cols], _NT) * scale
            e = jnp.exp(s - jnp.max(s, axis=-1, keepdims=True))
            p = e / jnp.sum(e, axis=-1, keepdims=True)
            o_ref[:, cols] = _dot(p.astype(_ACT), kv_ref[:, d + h * dh:d + (h + 1) * dh], _NN).astype(o_ref.dtype)

    qs = pl.BlockSpec((tq, d), lambda b, i: (b * nq + i, 0))
    return pl.pallas_call(
        body, name="attn_fwd", grid=(n_seq, nq), in_specs=[qs, pl.BlockSpec((n_mem, 2 * d), lambda b, i: (b, 0))],
        out_specs=qs, out_shape=jax.ShapeDtypeStruct((t, d), _ACT), compiler_params=_params(2),
    )(q, kv)


def _attn_bwd(q, kv, do, n_seq, seq, n_mem):
    t, d = q.shape
    dh = d // XATTN_HEADS
    tq = min(1024, seq)
    nq = seq // tq
    scale = dh ** -0.5

    def body(q_ref, kv_ref, do_ref, dq_ref, dkv_ref, acc):
        i = pl.program_id(1)

        @pl.when(i == 0)
        def _():
            acc[...] = jnp.zeros_like(acc)

        for h in range(XATTN_HEADS):
            cols = slice(h * dh, (h + 1) * dh)
            vcols = slice(d + h * dh, d + (h + 1) * dh)
            qv = q_ref[:, cols]
            kh = kv_ref[:, cols]
            dov = do_ref[:, cols]
            s = _dot(qv, kh, _NT) * scale
            e = jnp.exp(s - jnp.max(s, axis=-1, keepdims=True))
            p = e / jnp.sum(e, axis=-1, keepdims=True)
            dp = _dot(dov, kv_ref[:, vcols], _NT)
            ds = (p * (dp - jnp.sum(dp * p, axis=-1, keepdims=True)) * scale).astype(_ACT)
            dq_ref[:, cols] = _dot(ds, kh, _NN).astype(dq_ref.dtype)
            acc[:, cols] += _dot(ds, qv, _TN)
            acc[:, vcols] += _dot(p.astype(_ACT), dov, _TN)

        @pl.when(i == nq - 1)
        def _():
            dkv_ref[...] = acc[...].astype(dkv_ref.dtype)

    qs = pl.BlockSpec((tq, d), lambda b, i: (b * nq + i, 0))
    ms = pl.BlockSpec((n_mem, 2 * d), lambda b, i: (b, 0))
    return pl.pallas_call(
        body, name="attn_bwd", grid=(n_seq, nq), in_specs=[qs, ms, qs], out_specs=[qs, ms],
        out_shape=[jax.ShapeDtypeStruct((t, d), _ACT), jax.ShapeDtypeStruct((n_seq * n_mem, 2 * d), _ACT)],
        scratch_shapes=[pltpu.VMEM((n_mem, 2 * d), f32)], compiler_params=_params(2),
    )(q, kv, do)


_FFN_ROWS = 2048
_FFN_COLS = 256
_FFN_HALO = 16


def _window(buf, g, start, rows):
    return buf[g, pl.ds(start, rows + 8), :]


def _taps3(win, rows):
    return [_rows_from(win, 6 + k, rows) for k in range(3)]


def _conv3(b_ref, w_ref, taps):
    acc = b_ref[...] + w_ref[0:1, :] * taps[0]
    for k in (1, 2):
        acc = acc + w_ref[k:k + 1, :] * taps[k]
    return acc


def _ffn_gate_fwd(up, fw, fb, seq):
    _, t, f = up.shape
    tm = min(_FFN_ROWS, seq)
    tps = seq // tm
    tc = _FFN_COLS
    nc = f // tc
    hl = _FFN_HALO

    def body(up_ref, uph_ref, wg_ref, wv_ref, bg_ref, bv_ref, a_ref, uc_ref):
        i = pl.program_id(1)
        before = uph_ref[...]
        before = jnp.where(i % tps == 0, jnp.zeros_like(before), before)

        def chunk(r0, wins):
            conv = []
            for g, (w_ref, b_ref) in enumerate(((wg_ref, bg_ref), (wv_ref, bv_ref))):
                conv.append(_conv3(b_ref, w_ref, _taps3(wins[g].astype(f32)[hl - 8:, :], _CHUNK)))
                uc_ref[g, pl.ds(r0, _CHUNK), :] = conv[g].astype(uc_ref.dtype)
            gate, val = conv
            a_ref[pl.ds(r0, _CHUNK), :] = (gate * _sigmoid(gate) * val).astype(a_ref.dtype)

        chunk(0, [jnp.concatenate([before[g], up_ref[g, 0:_CHUNK, :]], axis=0) for g in range(2)])

        def later(ci, carry):
            r0 = pl.multiple_of(ci * _CHUNK, _CHUNK)
            chunk(r0, [up_ref[g, pl.ds(r0 - hl, _CHUNK + hl), :] for g in range(2)])
            return carry

        lax.fori_loop(1, tm // _CHUNK, later, 0)

    hb = tm // hl
    return pl.pallas_call(
        body, name="ffn_gate_fwd", grid=(nc, t // tm),
        in_specs=[pl.BlockSpec((2, tm, tc), lambda j, i: (0, i, j)),
                  pl.BlockSpec((2, hl, tc), lambda j, i: (0, jnp.maximum(i * hb - 1, 0), j)),
                  pl.BlockSpec((8, tc), lambda j, i: (0, j)), pl.BlockSpec((8, tc), lambda j, i: (0, nc + j)),
                  pl.BlockSpec((1, tc), lambda j, i: (0, j)), pl.BlockSpec((1, tc), lambda j, i: (0, nc + j))],
        out_specs=[pl.BlockSpec((tm, tc), lambda j, i: (i, j)), pl.BlockSpec((2, tm, tc), lambda j, i: (0, i, j))],
        out_shape=[jax.ShapeDtypeStruct((t, f), _ACT), jax.ShapeDtypeStruct((2, t, f), _ACT)], compiler_params=_params(2),
    )(up, up, fw, fw, fb, fb)


def _ffn_gate_bwd(up, uc, da, fw, seq):
    _, t, f = up.shape
    tm = min(_FFN_ROWS, seq)
    tps = seq // tm
    tc = _FFN_COLS
    nc = f // tc
    hl = _FFN_HALO

    def body(up_ref, uc_ref, ucn_ref, da_ref, dan_ref, wg_ref, wv_ref, dup_ref, sg_ref, sv_ref, dbuf, sums):
        i = pl.program_id(1)
        at_end = i % tps == tps - 1

        @pl.when(i == 0)
        def _():
            sg_ref[...] = jnp.zeros_like(sg_ref)
            sv_ref[...] = jnp.zeros_like(sv_ref)

        sums[...] = jnp.zeros_like(sums)
        w_refs = (wg_ref, wv_ref)

        def grads(r0, rows, conv, dav):
            gate, val = [v.astype(f32) for v in conv]
            sg = _sigmoid(gate)
            douts = (dav * val * (sg * (1.0 + gate * (1.0 - sg))), dav * (gate * sg))
            for g in range(2):
                dbuf[g, pl.ds(r0, rows), :] = douts[g]
            return douts

        def first(ci, carry):
            r0 = pl.multiple_of(ci * _CHUNK, _CHUNK)
            douts = grads(r0, _CHUNK, [uc_ref[g, pl.ds(r0, _CHUNK), :] for g in range(2)],
                          da_ref[pl.ds(r0, _CHUNK), :].astype(f32))
            for g in range(2):
                sums[g, 0] += douts[g].reshape(_CHUNK // 8, 8, tc).sum(axis=0)
            return carry

        lax.fori_loop(0, tm // _CHUNK, first, 0)
        da_after = dan_ref[...].astype(f32)
        grads(tm, hl, [ucn_ref[g] for g in range(2)], jnp.where(at_end, jnp.zeros_like(da_after), da_after))

        def second(ci, carry):
            r0 = pl.multiple_of(ci * _CHUNK, _CHUNK)
            for g in range(2):
                win = _window(dbuf, g, r0, _CHUNK)
                upv = up_ref[g, pl.ds(r0, _CHUNK), :].astype(f32)
                acc = jnp.zeros((_CHUNK, tc), f32)
                for k in range(3):
                    shifted = _rows_from(win, 2 - k, _CHUNK)
                    acc = acc + w_refs[g][k:k + 1, :] * shifted
                    sums[g, 1 + k] += (shifted * upv).reshape(_CHUNK // 8, 8, tc).sum(axis=0)
                dup_ref[g, pl.ds(r0, _CHUNK), :] = acc.astype(dup_ref.dtype)
            return carry

        lax.fori_loop(0, tm // _CHUNK, second, 0)
        for g, s_ref in enumerate((sg_ref, sv_ref)):
            for r in range(4):
                s_ref[r:r + 1, :] += jnp.sum(sums[g, r], axis=0, keepdims=True)

    hb = tm // hl
    n_halo = t // hl
    return pl.pallas_call(
        body, name="ffn_gate_bwd", grid=(nc, t // tm),
        in_specs=[pl.BlockSpec((2, tm, tc), lambda j, i: (0, i, j)),
                  pl.BlockSpec((2, tm, tc), lambda j, i: (0, i, j)),
                  pl.BlockSpec((2, hl, tc), lambda j, i: (0, jnp.minimum((i + 1) * hb, n_halo - 1), j)),
                  pl.BlockSpec((tm, tc), lambda j, i: (i, j)),
                  pl.BlockSpec((hl, tc), lambda j, i: (jnp.minimum((i + 1) * hb, n_halo - 1), j)),
                  pl.BlockSpec((8, tc), lambda j, i: (0, j)), pl.BlockSpec((8, tc), lambda j, i: (0, nc + j))],
        out_specs=[pl.BlockSpec((2, tm, tc), lambda j, i: (0, i, j)),
                   pl.BlockSpec((8, tc), lambda j, i: (0, j)), pl.BlockSpec((8, tc), lambda j, i: (0, j))],
        out_shape=[jax.ShapeDtypeStruct((2, t, f), _ACT), jax.ShapeDtypeStruct((8, f), f32), jax.ShapeDtypeStruct((8, f), f32)],
        scratch_shapes=[pltpu.VMEM((2, tm + hl, tc), f32), pltpu.VMEM((2, 4, 8, tc), f32)],
        compiler_params=_params(2),
    )(up, uc, uc, da, da, fw, fw)


def _adamw_math(w, g, m, v):
    m = ADAM_B1 * m + (1.0 - ADAM_B1) * g
    v = ADAM_B2 * v + (1.0 - ADAM_B2) * (g * g)
    m_hat = m / (1.0 - ADAM_B1 ** ADAM_STEP)
    v_hat = v / (1.0 - ADAM_B2 ** ADAM_STEP)
    delta = -ADAM_LR * (m_hat / (jnp.sqrt(v_hat) + ADAM_EPS) + ADAM_WD * w)
    return delta, m, v


def _adamw_shards(quads):
    n = len(quads)
    steps = 8

    def body(*refs):
        for p in range(n):
            w_ref, g_ref, m_ref, v_ref = refs[4 * p:4 * p + 4]
            go_ref, d_ref, mo_ref, vo_ref = refs[4 * n + 4 * p:4 * n + 4 * p + 4]
            gv = g_ref[...]
            d, mn, vn = _adamw_math(w_ref[...], gv, m_ref[...], v_ref[...])
            go_ref[...] = gv
            d_ref[...] = d
            mo_ref[...] = mn
            vo_ref[...] = vn

    in_specs, out_specs, out_shape = [], [], []
    for w, _, _, _ in quads:
        _, r, c = w.shape
        s3 = pl.BlockSpec((None, r // steps, c), lambda i: (0, i, 0))
        in_specs += [s3, pl.BlockSpec((r // steps, c), lambda i: (i, 0)), s3, s3]
        out_specs += [s3] * 4
        out_shape += [jax.ShapeDtypeStruct(w.shape, f32)] * 4
    outs = pl.pallas_call(
        body, name="adamw_shards", grid=(steps,), in_specs=in_specs, out_specs=out_specs, out_shape=out_shape,
        compiler_params=_params(1),
    )(*[a for q in quads for a in q])
    return [tuple(outs[4 * p:4 * p + 4]) for p in range(n)]


def _adamw_small(quads):
    n = len(quads)

    def body(*refs):
        ins, outs = refs[:4 * n], refs[4 * n:]
        for p in range(n):
            w_ref, g_ref, m_ref, v_ref = ins[4 * p:4 * p + 4]
            d, mn, vn = _adamw_math(w_ref[...], g_ref[...], m_ref[...], v_ref[...])
            outs[3 * p][...] = d
            outs[3 * p + 1][...] = mn
            outs[3 * p + 2][...] = vn

    flat = [a for q in quads for a in q]
    shapes = [jax.ShapeDtypeStruct(q[0].shape, f32) for q in quads for _ in range(3)]
    outs = pl.pallas_call(
        body, name="adamw_small", in_specs=[_VMEM] * (4 * n), out_specs=[_VMEM] * (3 * n), out_shape=shapes,
        compiler_params=pltpu.CompilerParams(vmem_limit_bytes=_VMEM_LIMIT_BYTES),
    )(*flat)
    return [tuple(outs[3 * p:3 * p + 3]) for p in range(n)]


def _sum_partials(name, place, grads, got):
    nw = len(grads)
    steps = 2

    def body(place_ref, *refs):
        for w in range(nw):
            own_ref, got_ref, f_ref = refs[w], refs[nw + w], refs[2 * nw + w]
            s = own_ref[...].astype(f32)
            for k in range(got[w].shape[0]):
                s = s + got_ref[k].astype(f32)
            f_ref[...] = s

    own_specs, got_specs, out_specs, out_shape = [], [], [], []
    for g, l in zip(grads, got):
        _, r, c = g.shape
        tr = r // steps
        own_specs.append(pl.BlockSpec((None, tr, c), lambda i, p: (2 * p[0] + p[1], i, 0)))
        got_specs.append(pl.BlockSpec((l.shape[0], tr, c), lambda i, p: (0, i, 0)))
        out_specs.append(pl.BlockSpec((None, tr, c), lambda i, p: (p[1], i, 0)))
        out_shape.append(jax.ShapeDtypeStruct((2, r, c), f32))
    grid_spec = pltpu.PrefetchScalarGridSpec(num_scalar_prefetch=1, grid=(steps,), in_specs=own_specs + got_specs, out_specs=out_specs)
    return pl.pallas_call(body, name=name, grid_spec=grid_spec, out_shape=out_shape,
                          compiler_params=_params(1))(place, *grads, *got)


def _place():
    return lax.axis_index("x"), lax.axis_index("y"), lax.axis_index("c")


def _other_chips(x, y):
    return [(1 - x, y), (x, 1 - y), (1 - x, 1 - y)]


def _remote(src, dst, send_sem, recv_sem, to):
    return pltpu.make_async_remote_copy(src_ref=src, dst_ref=dst, send_sem=send_sem, recv_sem=recv_sem,
                                        device_id=to, device_id_type=_MESH)


def _place_shards(name, place, shards, col_sharded, after=None):
    n = len(shards)
    steps = 4
    more, more_specs = _after(after)

    def body(place_ref, *refs):
        for src, dst in zip(refs[:n], refs[n + len(more):]):
            dst[...] = src[...].astype(dst.dtype)

    in_specs, out_specs, out_shape = [], [], []
    for w, col in zip(shards, col_sharded):
        r, cs = w.shape
        tr = r // steps
        in_specs.append(pl.BlockSpec((tr, cs), lambda i, p: (i, 0)))
        if col:
            out_specs.append(pl.BlockSpec((tr, cs), lambda i, p: (i, p[0])))
            out_shape.append(jax.ShapeDtypeStruct((r, 4 * cs), _ACT))
        else:
            out_specs.append(pl.BlockSpec((tr, cs), lambda i, p: (p[0] * steps + i, 0)))
            out_shape.append(jax.ShapeDtypeStruct((4 * r, cs), _ACT))
    grid_spec = pltpu.PrefetchScalarGridSpec(num_scalar_prefetch=1, grid=(steps,), in_specs=in_specs + more_specs,
                                            out_specs=out_specs)
    return pl.pallas_call(body, name=name, grid_spec=grid_spec, out_shape=out_shape,
                          compiler_params=_params(1))(place, *shards, *more)


def _shard_of(ref, col_sharded, s):
    rows, cols = ref.shape
    if col_sharded:
        return ref.at[:, pl.ds(s * (cols // 4), cols // 4)]
    return ref.at[pl.ds(s * (rows // 4), rows // 4), :]


def _part_of(ref, col_sharded, whole, s, h):
    if whole:
        return _shard_of(ref, col_sharded, s)
    rows, cols = ref.shape
    if col_sharded:
        return ref.at[pl.ds(h * (rows // 2), rows // 2), pl.ds(s * (cols // 4), cols // 4)]
    return ref.at[pl.ds((2 * s + h) * (rows // 8), rows // 8), :]


def _allgather_start(name, bufs, col_sharded, whole, groups):
    n = len(bufs)
    ng = len(groups)

    def body(*refs):
        out = refs[n:2 * n]
        sems = refs[2 * n:2 * n + 2 * ng]
        token = refs[2 * n + 2 * ng]
        x, y, c = _place()
        for g, members in enumerate(groups):
            for i, w in enumerate(members):
                mine = _part_of(out[w], col_sharded[w], whole[w], 2 * x + y, c)
                for j, chip in enumerate(_other_chips(x, y)):
                    _remote(mine, mine, sems[2 * g].at[3 * i + j], sems[2 * g + 1].at[3 * i + j], (*chip, c)).start()
        token[...] = jnp.zeros_like(token)

    sem_shapes = [pltpu.SemaphoreType.DMA((3 * len(m),)) for m in groups for _ in range(2)]
    outs = pl.pallas_call(
        body, name=name, in_specs=[_HBM] * n, out_specs=[_HBM] * n + [_SEM] * (2 * ng) + [_VMEM],
        out_shape=[pltpu.HBM(b.shape, b.dtype) for b in bufs] + sem_shapes + [jax.ShapeDtypeStruct((8, 128), f32)],
        input_output_aliases={i: i for i in range(n)},
        compiler_params=pltpu.CompilerParams(has_side_effects=_EFFECT),
    )(*[pltpu.with_memory_space_constraint(b, pltpu.HBM) for b in bufs])
    return list(outs[:n]), [(outs[n + 2 * g], outs[n + 2 * g + 1]) for g in range(ng)], outs[n + 2 * ng]


def _allgather_relay(name, bufs, col_sharded, whole, sems, after):
    n = len(bufs)

    def body(*refs):
        buf = refs[:n]
        send, recv = refs[n], refs[n + 1]
        out = refs[n + 3:2 * n + 3]
        to_sibling, from_sibling, token = refs[2 * n + 3:]
        token[...] = jnp.zeros_like(token)
        x, y, c = _place()
        for i in range(n):
            mine = _part_of(buf[i], col_sharded[i], whole[i], 2 * x + y, c)
            for j, chip in enumerate(_other_chips(x, y)):
                landed = _part_of(buf[i], col_sharded[i], whole[i], 2 * chip[0] + chip[1], c)
                cp = _remote(mine, landed, send.at[3 * i + j], recv.at[3 * i + j], (*chip, c))
                cp.wait_send()
                cp.wait_recv()
        for i in range(n):
            if not whole[i]:
                for j, chip in enumerate(_other_chips(x, y)):
                    landed = _part_of(out[i], col_sharded[i], False, 2 * chip[0] + chip[1], c)
                    _remote(landed, landed, to_sibling.at[3 * i + j], from_sibling.at[3 * i + j], (x, y, 1 - c)).start()

    outs = pl.pallas_call(
        body, name=name, in_specs=[_HBM] * n + [_SEM, _SEM, _ANY], out_specs=[_HBM] * n + [_SEM, _SEM, _VMEM],
        out_shape=[pltpu.HBM(b.shape, b.dtype) for b in bufs] + [pltpu.SemaphoreType.DMA((3 * n,))] * 2
        + [jax.ShapeDtypeStruct((8, 128), f32)],
        input_output_aliases={i: i for i in range(n)},
        compiler_params=pltpu.CompilerParams(has_side_effects=_EFFECT),
    )(*bufs, *sems, after)
    return list(outs[:n]), (outs[n], outs[n + 1]), outs[n + 2]


def _allgather_wait(name, bufs, col_sharded, whole, sems, after):
    n = len(bufs)

    def body(*refs):
        buf = refs[:n]
        to_sibling, from_sibling = refs[n], refs[n + 1]
        x, y, c = _place()
        for i in range(n):
            if not whole[i]:
                for j, chip in enumerate(_other_chips(x, y)):
                    sent = _part_of(buf[i], col_sharded[i], False, 2 * chip[0] + chip[1], c)
                    landed = _part_of(buf[i], col_sharded[i], False, 2 * chip[0] + chip[1], 1 - c)
                    cp = _remote(sent, landed, to_sibling.at[3 * i + j], from_sibling.at[3 * i + j], (x, y, 1 - c))
                    cp.wait_send()
                    cp.wait_recv()

    return pl.pallas_call(
        body, name=name, in_specs=[_HBM] * n + [_SEM, _SEM, _ANY], out_specs=[_HBM] * n,
        out_shape=[pltpu.HBM(b.shape, b.dtype) for b in bufs],
        input_output_aliases={i: i for i in range(n)},
        compiler_params=pltpu.CompilerParams(has_side_effects=_EFFECT),
    )(*bufs, *sems, after)


def _other_devices(x, y, c):
    flips = [(bx, by, bc) for bx in (0, 1) for by in (0, 1) for bc in (0, 1)][1:]
    return [(1 - x if bx else x, 1 - y if by else y, 1 - c if bc else c) for bx, by, bc in flips]


def _grad_exchange_start(name, grads):
    nw = len(grads)
    lands = [lax.empty((7,) + g.shape[1:], g.dtype) for g in grads]

    def body(*refs):
        src = refs[2 * nw:3 * nw]
        got = refs[3 * nw:4 * nw]
        send, recv, token = refs[4 * nw:]
        x, y, c = _place()
        for w in range(nw):
            for k, (px, py, pc) in enumerate(_other_devices(x, y, c)):
                _remote(src[w].at[4 * px + 2 * py + pc], got[w].at[k], send.at[7 * w + k], recv.at[7 * w + k], (px, py, pc)).start()
        token[...] = jnp.zeros_like(token)

    outs = pl.pallas_call(
        body, name=name, in_specs=[_HBM] * (2 * nw), out_specs=[_HBM] * (2 * nw) + [_SEM, _SEM, _VMEM],
        out_shape=[pltpu.HBM(a.shape, a.dtype) for a in list(grads) + lands]
        + [pltpu.SemaphoreType.DMA((7 * nw,)), pltpu.SemaphoreType.DMA((7 * nw,)), jax.ShapeDtypeStruct((8, 128), f32)],
        input_output_aliases={i: i for i in range(2 * nw)},
        compiler_params=pltpu.CompilerParams(has_side_effects=_EFFECT),
    )(*[pltpu.with_memory_space_constraint(a, pltpu.HBM) for a in list(grads) + lands])
    return list(outs[:nw]), list(outs[nw:2 * nw]), (outs[2 * nw], outs[2 * nw + 1]), outs[2 * nw + 2]


def _grad_exchange_wait(name, grads, got, sems, after):
    nw = len(grads)

    def body(*refs):
        src = refs[:nw]
        land = refs[nw:2 * nw]
        send, recv = refs[2 * nw], refs[2 * nw + 1]
        x, y, c = _place()
        for w in range(nw):
            for k, (px, py, pc) in enumerate(_other_devices(x, y, c)):
                cp = _remote(src[w].at[4 * px + 2 * py + pc], land[w].at[k], send.at[7 * w + k], recv.at[7 * w + k], (px, py, pc))
                cp.wait_send()
                cp.wait_recv()

    outs = pl.pallas_call(
        body, name=name, in_specs=[_HBM] * (2 * nw) + [_SEM, _SEM, _ANY], out_specs=[_HBM] * (2 * nw),
        out_shape=[pltpu.HBM(a.shape, a.dtype) for a in list(grads) + list(got)],
        input_output_aliases={i: i for i in range(2 * nw)},
        compiler_params=pltpu.CompilerParams(has_side_effects=_EFFECT),
    )(*grads, *got, *sems, after)
    return list(outs[:nw]), list(outs[nw:])


def _swap_halves_start(finals):
    nw = len(finals)

    def body(*refs):
        buf = refs[nw:2 * nw]
        send, recv, token = refs[2 * nw:]
        x, y, c = _place()
        for w in range(nw):
            _remote(buf[w].at[c], buf[w].at[c], send.at[w], recv.at[w], (x, y, 1 - c)).start()
        token[...] = jnp.zeros_like(token)

    outs = pl.pallas_call(
        body, name="rs_swap_start", in_specs=[_HBM] * nw, out_specs=[_HBM] * nw + [_SEM, _SEM, _VMEM],
        out_shape=[pltpu.HBM(g.shape, g.dtype) for g in finals] + [pltpu.SemaphoreType.DMA((nw,))] * 2
        + [jax.ShapeDtypeStruct((8, 128), f32)],
        input_output_aliases={i: i for i in range(nw)},
        compiler_params=pltpu.CompilerParams(has_side_effects=_EFFECT),
    )(*[pltpu.with_memory_space_constraint(g, pltpu.HBM) for g in finals])
    return list(outs[:nw]), (outs[nw], outs[nw + 1]), outs[nw + 2]


def _swap_halves_wait(bufs, sems, after):
    nw = len(bufs)

    def body(*refs):
        buf = refs[:nw]
        send, recv = refs[nw], refs[nw + 1]
        x, y, c = _place()
        for w in range(nw):
            cp = _remote(buf[w].at[c], buf[w].at[1 - c], send.at[w], recv.at[w], (x, y, 1 - c))
            cp.wait_send()
            cp.wait_recv()

    return pl.pallas_call(
        body, name="rs_swap_wait", in_specs=[_HBM] * nw + [_SEM, _SEM, _ANY], out_specs=[_HBM] * nw,
        out_shape=[pltpu.HBM(g.shape, g.dtype) for g in bufs],
        input_output_aliases={i: i for i in range(nw)},
        compiler_params=pltpu.CompilerParams(has_side_effects=_EFFECT),
    )(*bufs, *sems, after)


def _half_slices(shape, h):
    rows, cols = shape
    if cols % 256 == 0:
        return (slice(None), slice(h * (cols // 2), (h + 1) * (cols // 2)))
    return (slice(h * (rows // 2), (h + 1) * (rows // 2)), slice(None))


def _allreduce_small(parts, after):
    n = len(parts)

    def body(*refs):
        src = refs[:n]
        refs = refs[n + 1:]
        out = refs[:n]
        sib = refs[n:2 * n]
        chip_sum = refs[2 * n:3 * n]
        slots = refs[3 * n:4 * n]
        pair_send, pair_recv, ici_send, ici_recv, swap_send, swap_recv = refs[4 * n:]
        x, y, c = _place()
        me_chip = 2 * x + y
        chips = _other_chips(x, y)
        pairs = [_remote(src[a], sib[a], pair_send.at[a], pair_recv.at[a], (x, y, 1 - c)) for a in range(n)]
        for rc in pairs:
            rc.start()
        for a in range(n):
            pairs[a].wait_recv()
            chip_sum[a][...] = src[a][...] + sib[a][...]
        for h in (0, 1):
            @pl.when(c == h)
            def _():
                sends = []
                for a in range(n):
                    idx = _half_slices(parts[a].shape, h)
                    for j, chip in enumerate(chips):
                        rc = _remote(chip_sum[a].at[idx], slots[a].at[me_chip].at[idx], ici_send.at[3 * a + j], ici_recv.at[3 * a + j], (*chip, h))
                        rc.start()
                        sends.append(rc)
                    slots[a][(me_chip,) + idx] = chip_sum[a][idx]
                for a in range(n):
                    idx = _half_slices(parts[a].shape, h)
                    for j, chip in enumerate(chips):
                        landed = slots[a].at[2 * chip[0] + chip[1]].at[idx]
                        _remote(landed, landed, ici_send.at[3 * a + j], ici_recv.at[3 * a + j], (x, y, c)).wait_recv()
                    total = slots[a][(0,) + idx]
                    for s in range(1, 4):
                        total = total + slots[a][(s,) + idx]
                    out[a][idx] = total
                    rc = _remote(out[a].at[idx], out[a].at[idx], swap_send.at[a], swap_recv.at[a], (x, y, 1 - h))
                    rc.start()
                    sends.append(rc)
                for a in range(n):
                    other = out[a].at[_half_slices(parts[a].shape, 1 - h)]
                    _remote(other, other, swap_send.at[a], swap_recv.at[a], (x, y, c)).wait_recv()
                for rc in sends:
                    rc.wait_send()
        for rc in pairs:
            rc.wait_send()

    return pl.pallas_call(
        body, name="allreduce_small", in_specs=[_VMEM] * n + [_ANY], out_specs=[_VMEM] * n,
        out_shape=[jax.ShapeDtypeStruct(p.shape, f32) for p in parts],
        scratch_shapes=[pltpu.VMEM(p.shape, f32) for p in parts] * 2 + [pltpu.VMEM((4,) + p.shape, f32) for p in parts]
        + [pltpu.SemaphoreType.DMA((n,)), pltpu.SemaphoreType.DMA((n,)), pltpu.SemaphoreType.DMA((3 * n,)),
           pltpu.SemaphoreType.DMA((3 * n,)), pltpu.SemaphoreType.DMA((n,)), pltpu.SemaphoreType.DMA((n,))],
        compiler_params=pltpu.CompilerParams(vmem_limit_bytes=_VMEM_LIMIT_BYTES),
    )(*parts, after)


def _local_step(x, mem, tgt, g_mix, g_xattn, g_mem, g_ffn, g_final, cb, lg, lb, pw, ps, fb, started, relay, weights, reduce,
                n_seq, seq, n_mem):
    t, d = x.shape
    f = fb.shape[1] // 2
    c = cb.shape[1]
    h1 = _rms_fwd("norm_mix", x, g_mix, after=started)
    relay(0, h1)
    w_in, cw, fw = weights(0, h1)
    u = _mm_nn("proj_in", h1, w_in, _ACT, w_in.shape[1])
    y, hc = _mix_fwd(u, cw, cb, lg, lb, pw, ps, seq)
    relay(1, y)
    w_out, w_q, w_kv, w_o = weights(1, y)
    x1, h2 = _proj_residual_norm("proj_out", y, w_out, x, g_xattn)
    q = _mm_nn("proj_q", h2, w_q, _ACT, d)
    mem_n = _rms_fwd("norm_mem", mem, g_mem)
    kv = _mm_nn("proj_kv", mem_n, w_kv, _ACT, 2 * d)
    o = _attn_fwd(q, kv, n_seq, seq, n_mem)
    x2, h3 = _proj_residual_norm("proj_o", o, w_o, x1, g_ffn, after=relay(2, o))
    w_up, w_down = weights(2, h3)
    up = _mm_nn("proj_up", h3, w_up, _ACT, f, split_out=True)
    a, uc = _ffn_gate_fwd(up, fw, fb, seq)
    dx3, dx3b, dg_final, loss = _proj_loss_bwd("proj_down", a, w_down, x2, g_final, tgt)
    da = _mm_nt("d_act", dx3b, w_down, _ACT)
    gw_down = _mm_tn_rows("dw_down", a, dx3b, f // 2, d // 2)
    dup, sums_g, sums_v = _ffn_gate_bwd(up, uc, da, fw, seq)
    gw_up = _mm_tn_pieces("dw_up", h3, dup, f // 2)
    token = reduce(0, [gw_down.reshape(8, -1, d), gw_up])
    dx2, dx2b, dg_ffn = _dproj_rms_bwd("d_h3", dup, w_up, x2, g_ffn, dx3, after=token)
    do = _mm_nt("d_o", dx2b, w_o, _ACT)
    gw_o = _mm_tn_rows("dw_o", o, dx2b, d, d // 2)
    dq, dkv = _attn_bwd(q, kv, do, n_seq, seq, n_mem)
    gw_q = _mm_tn_rows("dw_q", h2, dq, d, d // 2)
    gw_kv = _mm_tn_pieces("dw_kv", mem_n, dkv, d // 2)
    dmem_n = _mm_nt("d_mem_n", dkv, w_kv, f32)
    dg_mem = _rms_gain_grad("norm_mem_bwd", mem, dmem_n)
    dx1, dx1b, dg_xattn = _dproj_rms_bwd("d_h2", dq, w_q, x1, g_xattn, dx2)
    dy = _mm_nt("d_y", dx1b, w_out, _ACT)
    gw_out = _mm_tn_rows("dw_out", y, dx1b, d, d // 2)
    token = reduce(1, [gw_o.reshape(8, -1, d), gw_q.reshape(8, -1, d), gw_kv, gw_out.reshape(8, -1, d)])
    dhc, sums_norm = _mix_bwd_norm(hc, dy, lg, lb, token)
    du, d_cw, d_ps, d_pw = _mix_bwd_taps(u, dhc, dy, cw, pw, ps, seq)
    gw_in = _mm_tn_pieces("dw_in", h1, du, c * 3 // 4)
    token = reduce(2, [gw_in])
    grad_x, dg_mix = _dproj_rms_bwd("d_h1", du, w_in, x, g_mix, dx1, storage_copy=False, after=token)
    zero_row = jnp.zeros((1, d), f32)
    gains = jnp.concatenate([dg_mix, dg_xattn, dg_mem, dg_ffn, dg_final, jnp.pad(loss, ((0, 0), (0, d - 1))), zero_row, zero_row], axis=0)
    conv_rows = jnp.concatenate([sums_norm[2:3], sums_norm[0:1], sums_norm[1:2], d_ps[0:1], jnp.zeros((4, c), f32)], axis=0)
    ffn_rows = jnp.concatenate([sums_g, sums_v], axis=1)
    small = [gains, conv_rows, d_pw.reshape(-1, d_pw.shape[-1]), ffn_rows, d_cw]
    return grad_x, small


def kernel(x, mem, norm_mix_g, w_in, conv_dw_w, conv_dw_b, conv_ln_g, conv_ln_b, pool_w, pool_scale, w_out, norm_xattn_g, norm_mem_g, w_q, w_kv, w_o, norm_ffn_g, w_up, ffn_dw_w, ffn_dw_b, w_down, norm_final_g, loss_target, m_norm_mix_g, m_w_in, m_conv_dw_w, m_conv_dw_b, m_conv_ln_g, m_conv_ln_b, m_pool_w, m_pool_scale, m_w_out, m_norm_xattn_g, m_norm_mem_g, m_w_q, m_w_kv, m_w_o, m_norm_ffn_g, m_w_up, m_ffn_dw_w, m_ffn_dw_b, m_w_down, m_norm_final_g, v_norm_mix_g, v_w_in, v_conv_dw_w, v_conv_dw_b, v_conv_ln_g, v_conv_ln_b, v_pool_w, v_pool_scale, v_w_out, v_norm_xattn_g, v_norm_mem_g, v_w_q, v_w_kv, v_w_o, v_norm_ffn_g, v_w_up, v_ffn_dw_w, v_ffn_dw_b, v_w_down, v_norm_final_g):
    n_seq, seq, d = x.shape
    n_mem = mem.shape[1]
    chip = 2 * lax.axis_index("x") + lax.axis_index("y")

    place = jnp.stack([chip, lax.axis_index("c")]).astype(jnp.int32)

    col_w = [w_in, w_kv, w_up]
    row_w = [w_out, w_q, w_o, w_down]
    kw = conv_dw_w.shape[1]

    def padded_in_place(shard, rows):
        full = jnp.zeros((rows, 4 * shard.shape[1]), shard.dtype)
        return lax.dynamic_update_slice(full, shard, (0, chip * shard.shape[1]))

    first = list(_place_shards("place_w_in", place, [w_in[0]], [True]))
    first += [padded_in_place(conv_dw_w[0], _HALO), padded_in_place(ffn_dw_w[0], 8)]
    first, first_sems, token = _allgather_start("allgather_start_0", first, [True] * 3, [False, True, True], [[0, 1, 2]])
    rest = [w_kv, w_up, w_out, w_q, w_o, w_down]
    rest_flags = [True, True, False, False, False, False]
    rest = list(_place_shards("place_rest", place, [w[0] for w in rest], rest_flags, after=token))
    rest, rest_sems, all_started = _allgather_start("allgather_start_1", rest, rest_flags, [False] * 6, [[2, 3, 0, 4], [1, 5]])
    started = [(first, [True] * 3, [False, True, True], first_sems[0]),
               ([rest[i] for i in (2, 3, 0, 4)], [False, False, True, False], [False] * 4, rest_sems[0]),
               ([rest[i] for i in (1, 5)], [True, False], [False] * 2, rest_sems[1])]
    relayed = {}

    def relay(g, after):
        group_bufs, flags, wholes, group_sems = started[g]
        group_bufs, sibling_sems, relay_token = _allgather_relay("allgather_relay_%d" % g, group_bufs, flags, wholes, group_sems, after)
        relayed[g] = (group_bufs, sibling_sems)
        return relay_token

    def weights(g, after):
        group_bufs, sibling_sems = relayed[g]
        return _allgather_wait("allgather_wait_%d" % g, group_bufs, started[g][1], started[g][2], sibling_sems, after)

    names = ["w_in", "w_kv", "w_up", "w_out", "w_q", "w_o", "w_down"]
    reduce_groups = [["w_down", "w_up"], ["w_o", "w_q", "w_kv", "w_out"], ["w_in"]]
    in_flight = {}

    def reduce(g, grads):
        grads, lands, rs_sems, token = _grad_exchange_start("rs_start_%d" % g, grads)
        in_flight[g] = (grads, lands, rs_sems)
        return token

    grad_x, small = _local_step(
        x.reshape(n_seq * seq, d), mem.reshape(n_seq * n_mem, d), loss_target.reshape(n_seq * seq, d),
        norm_mix_g, norm_xattn_g, norm_mem_g, norm_ffn_g, norm_final_g.reshape(1, d),
        conv_dw_b, conv_ln_g, conv_ln_b, pool_w[0], pool_scale, ffn_dw_b, all_started, relay, weights, reduce,
        n_seq, seq, n_mem)

    landed = {}
    for g, members in enumerate(reduce_groups):
        grads, lands, rs_sems = in_flight[g]
        grads, lands = _grad_exchange_wait("rs_wait_%d" % g, grads, lands, rs_sems, grad_x)
        landed.update(zip(members, zip(grads, lands)))
    finals = _sum_partials("rs_sum", place, [landed[n][0] for n in names], [landed[n][1] for n in names])
    finals, swap_sems, token = _swap_halves_start(finals)

    gains, conv_rows, d_pw, ffn_rows, d_cw = _allreduce_small(small, token)
    loss = gains[5, 0]
    shard_grads = _swap_halves_wait(finals, swap_sems, gains)

    outs = {}
    big_w = dict(zip(names, col_w + row_w))
    big_m = dict(w_in=m_w_in, w_kv=m_w_kv, w_up=m_w_up, w_out=m_w_out, w_q=m_w_q, w_o=m_w_o, w_down=m_w_down)
    big_v = dict(w_in=v_w_in, w_kv=v_w_kv, w_up=v_w_up, w_out=v_w_out, w_q=v_w_q, w_o=v_w_o, w_down=v_w_down)
    big_quads = [(big_w[n], g.reshape(big_w[n].shape[1:]), big_m[n], big_v[n]) for n, g in zip(names, shard_grads)]
    outs.update(zip(names, _adamw_shards(big_quads)))

    f2 = ffn_dw_b.shape[1]
    cs_c = conv_dw_w.shape[2]
    cs_f = ffn_dw_w.shape[2]
    g_cw = lax.dynamic_slice(d_cw, (0, chip * cs_c), (kw, cs_c)).reshape(conv_dw_w.shape)
    g_fw = lax.dynamic_slice(ffn_rows, (1, chip * cs_f), (ffn_dw_w.shape[1], cs_f)).reshape(ffn_dw_w.shape)
    small_params = [
        ("norm_mix_g", norm_mix_g, gains[0:1], m_norm_mix_g, v_norm_mix_g),
        ("conv_dw_w", conv_dw_w, g_cw, m_conv_dw_w, v_conv_dw_w),
        ("conv_dw_b", conv_dw_b, conv_rows[0:1], m_conv_dw_b, v_conv_dw_b),
        ("conv_ln_g", conv_ln_g, conv_rows[1:2], m_conv_ln_g, v_conv_ln_g),
        ("conv_ln_b", conv_ln_b, conv_rows[2:3], m_conv_ln_b, v_conv_ln_b),
        ("pool_w", pool_w, d_pw.reshape(pool_w.shape), m_pool_w, v_pool_w),
        ("pool_scale", pool_scale, conv_rows[3:4], m_pool_scale, v_pool_scale),
        ("norm_xattn_g", norm_xattn_g, gains[1:2], m_norm_xattn_g, v_norm_xattn_g),
        ("norm_mem_g", norm_mem_g, gains[2:3], m_norm_mem_g, v_norm_mem_g),
        ("norm_ffn_g", norm_ffn_g, gains[3:4], m_norm_ffn_g, v_norm_ffn_g),
        ("ffn_dw_w", ffn_dw_w, g_fw, m_ffn_dw_w, v_ffn_dw_w),
        ("ffn_dw_b", ffn_dw_b, ffn_rows[0:1, :f2], m_ffn_dw_b, v_ffn_dw_b),
        ("norm_final_g", norm_final_g.reshape(1, d), gains[4:5], m_norm_final_g.reshape(1, d), v_norm_final_g.reshape(1, d)),
    ]
    quads = []
    for _, w, g, m, v in small_params:
        shape2 = (-1, w.shape[-1])
        quads.append((w.reshape(shape2), g.reshape(shape2), m.reshape(shape2), v.reshape(shape2)))
    for (n, w, g, _, _), (delta, new_m, new_v) in zip(small_params, _adamw_small(quads)):
        shape = norm_final_g.shape if n == "norm_final_g" else w.shape
        outs[n] = (g.reshape(shape), delta.reshape(shape), new_m.reshape(shape), new_v.reshape(shape))

    order = ["norm_mix_g", "w_in", "conv_dw_w", "conv_dw_b", "conv_ln_g", "conv_ln_b", "pool_w", "pool_scale", "w_out",
             "norm_xattn_g", "norm_mem_g", "w_q", "w_kv", "w_o", "norm_ffn_g", "w_up", "ffn_dw_w", "ffn_dw_b", "w_down",
             "norm_final_g"]
    return (loss, grad_x.reshape(x.shape), *[outs[n][0] for n in order], *[outs[n][1] for n in order],
            *[outs[n][2] for n in order], *[outs[n][3] for n in order])
```

```python
import jax
import jax.numpy as jnp
from jax import lax
from jax.experimental import pallas as pl
from jax.experimental.pallas import tpu as pltpu

f32 = jnp.float32
_ACT = jnp.bfloat16

EPS = 1e-6
POOL_WINDOWS = (2, 4, 8, 16)
XATTN_HEADS = 4
ADAM_LR = 0.001
ADAM_B1 = 0.9
ADAM_B2 = 0.999
ADAM_EPS = 1e-08
ADAM_WD = 0.01
ADAM_STEP = 10

_VMEM_LIMIT_BYTES = 56 * 1024 * 1024
_MESH = pl.DeviceIdType.MESH
_ANY = pl.BlockSpec(memory_space=pl.ANY)
_VMEM = pl.BlockSpec(memory_space=pltpu.VMEM)
_HBM = pl.BlockSpec(memory_space=pltpu.HBM)
_SEM = pl.BlockSpec(memory_space=pltpu.SEMAPHORE)
_EFFECT = pltpu.SideEffectType.DATAFLOW_SIDE_EFFECTING

_NN = (((1,), (0,)), ((), ()))
_NT = (((1,), (1,)), ((), ()))
_TN = (((0,), (0,)), ((), ()))


def _params(n_grid):
    return pltpu.CompilerParams(dimension_semantics=("arbitrary",) * n_grid, vmem_limit_bytes=_VMEM_LIMIT_BYTES)


def _sigmoid(v):
    return 1.0 / (1.0 + jnp.exp(-v))


def _dot(a, b, dims):
    return lax.dot_general(a, b, dims, preferred_element_type=f32)


def _mm(name, a, b, *, dims, grid, a_spec, b_spec, o_spec, out_shape):
    def body(a_ref, b_ref, o_ref):
        o_ref[...] = _dot(a_ref[...], b_ref[...], dims).astype(o_ref.dtype)

    return pl.pallas_call(
        body, name=name, grid=grid, in_specs=[a_spec, b_spec], out_specs=o_spec, out_shape=out_shape,
        compiler_params=_params(len(grid)),
    )(a, b)


_NARROW = 2816


def _row_tile(m, width=_NARROW + 1):
    return min(1024 if width <= _NARROW else 512, m)


def _mm_nn(name, a, b, out_dtype, tn, split_out=False):
    m, k = a.shape
    n = b.shape[1]
    tm = _row_tile(m, max(k, tn))
    if split_out:
        out_shape = jax.ShapeDtypeStruct((n // tn, m, tn), out_dtype)
        o_spec = pl.BlockSpec((None, tm, tn), lambda j, i: (j, i, 0))
    else:
        out_shape = jax.ShapeDtypeStruct((m, n), out_dtype)
        o_spec = pl.BlockSpec((tm, tn), lambda j, i: (i, j))
    return _mm(
        name, a, b, dims=_NN, grid=(n // tn, m // tm),
        a_spec=pl.BlockSpec((tm, k), lambda j, i: (i, 0)), b_spec=pl.BlockSpec((k, tn), lambda j, i: (0, j)),
        o_spec=o_spec, out_shape=out_shape,
    )


def _mm_nt(name, a, b, out_dtype):
    n, kc = b.shape
    m = a.shape[0]
    tm = _row_tile(m, max(n, kc))
    return _mm(
        name, a, b, dims=_NT, grid=(m // tm,),
        a_spec=pl.BlockSpec((tm, kc), lambda i: (i, 0)),
        b_spec=pl.BlockSpec((n, kc), lambda i: (0, 0), pipeline_mode=pl.Buffered(1)),
        o_spec=pl.BlockSpec((tm, n), lambda i: (i, 0)),
        out_shape=jax.ShapeDtypeStruct((m, n), out_dtype),
    )


def _mm_tn_rows(name, a, b, tka, tn):
    m, ka = a.shape
    nb = b.shape[1]
    return _mm(
        name, a, b, dims=_TN, grid=(ka // tka, nb // tn),
        a_spec=pl.BlockSpec((m, tka), lambda i, j: (0, i)), b_spec=pl.BlockSpec((m, tn), lambda i, j: (0, j)),
        o_spec=pl.BlockSpec((tka, tn), lambda i, j: (i, j)),
        out_shape=jax.ShapeDtypeStruct((ka, nb), _ACT),
    )


def _mm_tn_pieces(name, a, b, cs):
    m, ka = a.shape
    if b.ndim == 3:
        b_spec = pl.BlockSpec((None, m, cs), lambda i, j: (j // 2, 0, j % 2))
    else:
        b_spec = pl.BlockSpec((m, cs), lambda i, j: (0, j))
    return _mm(
        name, a, b, dims=_TN, grid=(2, 4),
        a_spec=pl.BlockSpec((m, ka // 2), lambda i, j: (0, i)), b_spec=b_spec,
        o_spec=pl.BlockSpec((None, ka // 2, cs), lambda i, j: (2 * j + i, 0, 0)),
        out_shape=jax.ShapeDtypeStruct((8, ka // 2, cs), _ACT),
    )


def _after(after):
    return ([], []) if after is None else ([after], [_ANY])


def _rms_fwd(name, x, g, after=None):
    t, d = x.shape
    tm = _row_tile(t, d)
    more, more_specs = _after(after)

    def body(x_ref, g_ref, *refs):
        h_ref = refs[-1]
        xv = x_ref[...]
        r = lax.rsqrt(jnp.mean(xv * xv, axis=-1, keepdims=True) + EPS)
        h_ref[...] = (xv * r * g_ref[...]).astype(h_ref.dtype)

    return pl.pallas_call(
        body, name=name, grid=(t // tm,),
        in_specs=[pl.BlockSpec((tm, d), lambda i: (i, 0)), pl.BlockSpec((1, d), lambda i: (0, 0))] + more_specs,
        out_specs=pl.BlockSpec((tm, d), lambda i: (i, 0)), out_shape=jax.ShapeDtypeStruct((t, d), _ACT),
        compiler_params=_params(1),
    )(x, g, *more)


def _fused_rows(name, a, b, product, a_spec, tm, extras, extra_specs, out_shape, out_specs, epilogue):
    ne = len(extras)

    def body(a_ref, b_ref, *refs):
        epilogue(product(a_ref, b_ref), refs[:ne], refs[ne:])

    m = extras[0].shape[0]
    return pl.pallas_call(
        body, name=name, grid=(m // tm,),
        in_specs=[a_spec, pl.BlockSpec(b.shape, lambda i: (0, 0), pipeline_mode=pl.Buffered(1)), *extra_specs],
        out_specs=out_specs, out_shape=out_shape, compiler_params=_params(1),
    )(a, b, *extras)


def _proj_residual_norm(name, a, b, res, g, after=None):
    m, k = a.shape
    d = b.shape[1]
    tm = _row_tile(m, max(k, d))

    def epilogue(p, ins, outs):
        xv = p + ins[0][...]
        outs[0][...] = xv
        r = lax.rsqrt(jnp.mean(xv * xv, axis=-1, keepdims=True) + EPS)
        outs[1][...] = (xv * r * ins[1][...]).astype(outs[1].dtype)

    row = pl.BlockSpec((tm, d), lambda i: (i, 0))
    return _fused_rows(
        name, a, b, lambda a_ref, b_ref: _dot(a_ref[...], b_ref[...], _NN), pl.BlockSpec((tm, k), lambda i: (i, 0)), tm,
        [res, g] + _after(after)[0], [row, pl.BlockSpec((1, d), lambda i: (0, 0))] + _after(after)[1],
        [jax.ShapeDtypeStruct((m, d), f32), jax.ShapeDtypeStruct((m, d), _ACT)], [row, row], epilogue)


def _dproj_rms_bwd(name, a, b, x, g, dres, storage_copy=True, after=None):
    m, d = x.shape
    if a.ndim == 3:
        nh, _, kh = a.shape
        tm = _row_tile(m, nh * kh)
        a_spec = pl.BlockSpec((nh, tm, kh), lambda i: (0, i, 0))

        def product(a_ref, b_ref):
            p = _dot(a_ref[0], b_ref[:, 0:kh], _NT)
            for h in range(1, nh):
                p = p + _dot(a_ref[h], b_ref[:, h * kh:(h + 1) * kh], _NT)
            return p
    else:
        tm = _row_tile(m, max(a.shape[1], d))
        a_spec = pl.BlockSpec((tm, a.shape[1]), lambda i: (i, 0))

        def product(a_ref, b_ref):
            return _dot(a_ref[...], b_ref[...], _NT)

    def epilogue(dhv, ins, outs):
        x_ref, g_ref, dres_ref = ins[:3]
        dg_ref = outs[-1]

        @pl.when(pl.program_id(0) == 0)
        def _():
            dg_ref[...] = jnp.zeros_like(dg_ref)

        xv = x_ref[...]
        r = lax.rsqrt(jnp.mean(xv * xv, axis=-1, keepdims=True) + EPS)
        xn = xv * r
        dxn = dhv * g_ref[...]
        dx = r * (dxn - xn * jnp.mean(dxn * xn, axis=-1, keepdims=True)) + dres_ref[...]
        outs[0][...] = dx
        if storage_copy:
            outs[1][...] = dx.astype(outs[1].dtype)
        dg_ref[...] += jnp.sum(dhv * xn, axis=0, keepdims=True)

    row = pl.BlockSpec((tm, d), lambda i: (i, 0))
    vec = pl.BlockSpec((1, d), lambda i: (0, 0))
    copies = [jax.ShapeDtypeStruct((m, d), _ACT)] if storage_copy else []
    return _fused_rows(
        name, a, b, product, a_spec, tm, [x, g, dres] + _after(after)[0], [row, vec, row] + _after(after)[1],
        [jax.ShapeDtypeStruct((m, d), f32)] + copies + [jax.ShapeDtypeStruct((1, d), f32)],
        [row] * (1 + len(copies)) + [vec], epilogue)


def _proj_loss_bwd(name, a, b, res, g, tgt):
    m, k = a.shape
    d = b.shape[1]
    tm = _row_tile(m, max(k, d))

    def epilogue(p, ins, outs):
        res_ref, g_ref, t_ref = ins
        dx_ref, dxb_ref, dg_ref, loss_ref = outs

        @pl.when(pl.program_id(0) == 0)
        def _():
            dg_ref[...] = jnp.zeros_like(dg_ref)
            loss_ref[...] = jnp.zeros_like(loss_ref)

        xv = p + res_ref[...]
        gv = g_ref[...]
        r = lax.rsqrt(jnp.mean(xv * xv, axis=-1, keepdims=True) + EPS)
        xn = xv * r
        err = xn * gv - t_ref[...]
        loss_ref[...] += 0.5 * jnp.sum(jnp.mean(err * err, axis=-1, keepdims=True), axis=0, keepdims=True)
        dout = err * (1.0 / d)
        dxn = dout * gv
        dx = r * (dxn - xn * jnp.mean(dxn * xn, axis=-1, keepdims=True))
        dx_ref[...] = dx
        dxb_ref[...] = dx.astype(dxb_ref.dtype)
        dg_ref[...] += jnp.sum(dout * xn, axis=0, keepdims=True)

    row = pl.BlockSpec((tm, d), lambda i: (i, 0))
    vec = pl.BlockSpec((1, d), lambda i: (0, 0))
    return _fused_rows(
        name, a, b, lambda a_ref, b_ref: _dot(a_ref[...], b_ref[...], _NN), pl.BlockSpec((tm, k), lambda i: (i, 0)), tm,
        [res, g, tgt], [row, vec, row],
        [jax.ShapeDtypeStruct((m, d), f32), jax.ShapeDtypeStruct((m, d), _ACT), jax.ShapeDtypeStruct((1, d), f32),
         jax.ShapeDtypeStruct((1, 1), f32)],
        [row, row, vec, pl.BlockSpec((1, 1), lambda i: (0, 0))], epilogue)


def _rms_gain_grad(name, x, dh):
    t, d = x.shape
    tm = _row_tile(t)

    def body(x_ref, dh_ref, dg_ref):
        @pl.when(pl.program_id(0) == 0)
        def _():
            dg_ref[...] = jnp.zeros_like(dg_ref)

        xv = x_ref[...]
        r = lax.rsqrt(jnp.mean(xv * xv, axis=-1, keepdims=True) + EPS)
        dg_ref[...] += jnp.sum(dh_ref[...] * (xv * r), axis=0, keepdims=True)

    row = pl.BlockSpec((tm, d), lambda i: (i, 0))
    return pl.pallas_call(
        body, name=name, grid=(t // tm,), in_specs=[row, row], out_specs=pl.BlockSpec((1, d), lambda i: (0, 0)),
        out_shape=jax.ShapeDtypeStruct((1, d), f32), compiler_params=_params(1),
    )(x, dh)


_CONV_ROWS = 512
_CHUNK = 64
_HALO = 32


def _pool_counts(pos, w):
    return jnp.minimum(pos + 1.0, float(w))


def _rows_from(win, start, rows):
    if start % 8 == 0:
        return win[start:start + rows, :]
    n = win.shape[0]
    return pltpu.roll(win, n - start % 8, axis=0)[start - start % 8:start - start % 8 + rows, :]


def _tap_rows(buf, starts, rows):
    for residue in range(8):
        group = [(k, s) for k, s in starts.items() if s % 8 == residue]
        if group:
            lo = min(s for _, s in group) - residue
            hi = max(s for _, s in group) - residue + rows + (8 if residue else 0)
            win = buf[lo:hi, :]
            if residue:
                win = pltpu.roll(win, hi - lo - residue, axis=0)
            for k, s in group:
                yield k, win[s - residue - lo:s - residue - lo + rows, :]


def _mix_fwd(u, cw, cb, lg, lb, pw, ps, seq):
    t, c3 = u.shape
    c = c3 // 3
    kw = 31
    tm = min(_CONV_ROWS, seq)
    tps = seq // tm
    gd = c // len(POOL_WINDOWS)

    def body(u_ref, uh_ref, cw_ref, cb_ref, lg_ref, lb_ref, pw_ref, ps_ref, y_ref, hc_ref, hgbuf, pbuf):
        i = pl.program_id(0)
        keep = jnp.where(i % tps == 0, 0.0, 1.0)
        um = u_ref[...].astype(f32)
        uh = uh_ref[...].astype(f32) * keep
        hgbuf[0:_HALO, :] = uh[:, 0:c] * _sigmoid(uh[:, c:2 * c])
        hgbuf[_HALO:_HALO + tm, :] = um[:, 0:c] * _sigmoid(um[:, c:2 * c])
        pbuf[0:_HALO, :] = uh[:, 2 * c:]
        pbuf[_HALO:_HALO + tm, :] = um[:, 2 * c:]
        for r0 in range(0, tm, _CHUNK):
            acc = jnp.broadcast_to(cb_ref[...], (_CHUNK, c))
            for k, rows in _tap_rows(hgbuf, {k: r0 + _HALO - (kw - 1) + k for k in range(kw)}, _CHUNK):
                acc = acc + cw_ref[k:k + 1, :] * rows
            hc_ref[r0:r0 + _CHUNK, :] = acc
            mu = jnp.mean(acc, axis=-1, keepdims=True)
            xc = acc - mu
            var = jnp.mean(xc * xc, axis=-1, keepdims=True)
            hl = xc * lax.rsqrt(var + EPS) * lg_ref[...] + lb_ref[...]
            y_ref[r0:r0 + _CHUNK, 0:c] = (hl * _sigmoid(hl)).astype(y_ref.dtype)
        pos = ((i % tps) * tm).astype(f32) + lax.broadcasted_iota(jnp.int32, (tm, 1), 0).astype(f32)
        for gi, w in enumerate(POOL_WINDOWS):
            sl = slice(gi * gd, (gi + 1) * gd)
            v = pbuf[_HALO:_HALO + tm, sl]
            s = v
            for j in range(1, w):
                s = s + pbuf[_HALO - j:_HALO - j + tm, sl]
            pooled = s / _pool_counts(pos, w) - v
            mixed = _dot(pooled.astype(_ACT), pw_ref[gi].astype(_ACT), _NN)
            y_ref[:, c + gi * gd:c + (gi + 1) * gd] = (mixed * ps_ref[:, sl]).astype(y_ref.dtype)

    hb = tm // _HALO
    full = lambda shape: pl.BlockSpec(shape, lambda i: (0,) * len(shape))
    return pl.pallas_call(
        body, name="mix_fwd", grid=(t // tm,),
        in_specs=[pl.BlockSpec((tm, c3), lambda i: (i, 0)),
                  pl.BlockSpec((_HALO, c3), lambda i: (jnp.maximum(i * hb - 1, 0), 0)),
                  full((_HALO, c)), full((1, c)), full((1, c)), full((1, c)), full((len(POOL_WINDOWS), gd, gd)), full((1, c))],
        out_specs=[pl.BlockSpec((tm, 2 * c), lambda i: (i, 0)), pl.BlockSpec((tm, c), lambda i: (i, 0))],
        out_shape=[jax.ShapeDtypeStruct((t, 2 * c), _ACT), jax.ShapeDtypeStruct((t, c), f32)],
        scratch_shapes=[pltpu.VMEM((_HALO + tm, c), f32), pltpu.VMEM((_HALO + tm, c), f32)],
        compiler_params=_params(1),
    )(u, u, cw, cb, lg, lb, pw, ps)


def _mix_bwd_norm(hc, dy, lg, lb, after):
    t, c = hc.shape
    tm = _row_tile(t, c)

    def body(hc_ref, dy_ref, lg_ref, lb_ref, after_ref, dhc_ref, sums_ref):
        @pl.when(pl.program_id(0) == 0)
        def _():
            sums_ref[...] = jnp.zeros_like(sums_ref)

        hcv = hc_ref[...]
        mu = jnp.mean(hcv, axis=-1, keepdims=True)
        xc = hcv - mu
        rstd = lax.rsqrt(jnp.mean(xc * xc, axis=-1, keepdims=True) + EPS)
        n = xc * rstd
        hl = n * lg_ref[...] + lb_ref[...]
        sg = _sigmoid(hl)
        dhl = dy_ref[...].astype(f32) * (sg * (1.0 + hl * (1.0 - sg)))
        dn = dhl * lg_ref[...]
        dhc = rstd * (dn - jnp.mean(dn, axis=-1, keepdims=True) - n * jnp.mean(dn * n, axis=-1, keepdims=True))
        dhc_ref[...] = dhc
        sums_ref[0:1, :] += jnp.sum(dhl * n, axis=0, keepdims=True)
        sums_ref[1:2, :] += jnp.sum(dhl, axis=0, keepdims=True)
        sums_ref[2:3, :] += jnp.sum(dhc, axis=0, keepdims=True)

    row = pl.BlockSpec((tm, c), lambda i: (i, 0))
    vec = pl.BlockSpec((1, c), lambda i: (0, 0))
    return pl.pallas_call(
        body, name="mix_bwd_norm", grid=(t // tm,), in_specs=[row, row, vec, vec, _ANY],
        out_specs=[row, pl.BlockSpec((8, c), lambda i: (0, 0))],
        out_shape=[jax.ShapeDtypeStruct((t, c), f32), jax.ShapeDtypeStruct((8, c), f32)],
        compiler_params=_params(1),
    )(hc, dy, lg, lb, after)


def _mix_bwd_taps(u, dhc, dy, cw, pw, ps, seq):
    t, c3 = u.shape
    c = c3 // 3
    kw = 31
    tm = min(_CONV_ROWS, seq)
    tps = seq // tm
    ng = len(POOL_WINDOWS)
    gd = c // ng
    nh = 16

    def body(u_ref, uh_ref, dhc_ref, dhcn_ref, dy_ref, dyn_ref, cw_ref, pw_ref, ps_ref,
             du_ref, dcw_ref, dps_ref, dpw_ref, hgbuf, dcbuf, pbuf, dpbuf):
        i = pl.program_id(0)
        keep_prev = jnp.where(i % tps == 0, 0.0, 1.0)
        keep_next = jnp.where(i % tps == tps - 1, 0.0, 1.0)

        @pl.when(i == 0)
        def _():
            dcw_ref[...] = jnp.zeros_like(dcw_ref)
            dps_ref[...] = jnp.zeros_like(dps_ref)
            dpw_ref[...] = jnp.zeros_like(dpw_ref)

        pbuf[0:_HALO, :] = uh_ref[:, 2 * c:].astype(f32) * keep_prev
        um = u_ref[...].astype(f32)
        hgbuf[...] = um[:, 0:c] * _sigmoid(um[:, c:2 * c])
        pbuf[_HALO:_HALO + tm, :] = um[:, 2 * c:]
        dcbuf[0:tm, :] = dhc_ref[...]
        dcbuf[tm:tm + _HALO, :] = dhcn_ref[...] * keep_next
        tap_sums = [None] * kw
        for r0 in range(0, tm, _CHUNK):
            hg = hgbuf[r0:r0 + _CHUNK, :]
            acc = jnp.zeros((_CHUNK, c), f32)
            for k, rows in _tap_rows(dcbuf, {k: r0 + (kw - 1) - k for k in range(kw)}, _CHUNK):
                acc = acc + cw_ref[k:k + 1, :] * rows
                part = (rows * hg).reshape(_CHUNK // 8, 8, c).sum(axis=0)
                tap_sums[k] = part if tap_sums[k] is None else tap_sums[k] + part
            val = u_ref[r0:r0 + _CHUNK, 0:c].astype(f32)
            sg = _sigmoid(u_ref[r0:r0 + _CHUNK, c:2 * c].astype(f32))
            du_ref[r0:r0 + _CHUNK, 0:c] = (acc * sg).astype(du_ref.dtype)
            du_ref[r0:r0 + _CHUNK, c:2 * c] = (acc * val * sg * (1.0 - sg)).astype(du_ref.dtype)
        for k in range(kw):
            dcw_ref[k:k + 1, :] += jnp.sum(tap_sums[k], axis=0, keepdims=True)
        base = ((i % tps) * tm).astype(f32)
        pos = base + lax.broadcasted_iota(jnp.int32, (tm, 1), 0).astype(f32)
        pos_next = base + float(tm) + lax.broadcasted_iota(jnp.int32, (nh, 1), 0).astype(f32)
        for gi, w in enumerate(POOL_WINDOWS):
            sl = slice(gi * gd, (gi + 1) * gd)
            v = pbuf[_HALO:_HALO + tm, sl]
            s = v
            for j in range(1, w):
                s = s + pbuf[_HALO - j:_HALO - j + tm, sl]
            cnt = _pool_counts(pos, w)
            pooled = (s / cnt - v).astype(_ACT)
            pwg = pw_ref[gi].astype(_ACT)
            mixed = _dot(pooled, pwg, _NN)
            dyp = dy_ref[:, sl].astype(f32)
            dps_ref[0:1, sl] += jnp.sum(dyp * mixed, axis=0, keepdims=True)
            dmix = (dyp * ps_ref[:, sl]).astype(_ACT)
            dpw_ref[gi] += _dot(pooled, dmix, _TN)
            dmix_next = (dyn_ref[:, sl].astype(f32) * ps_ref[:, sl] * keep_next).astype(_ACT)
            dpool = _dot(dmix, pwg, _NT)
            dpbuf[0:tm, sl] = dpool / cnt
            dpbuf[tm:tm + nh, sl] = _dot(dmix_next, pwg, _NT) / _pool_counts(pos_next, w)
            acc = -dpool
            for j in range(w):
                acc = acc + dpbuf[j:j + tm, sl]
            du_ref[:, 2 * c + gi * gd:2 * c + (gi + 1) * gd] = acc.astype(du_ref.dtype)

    hb = tm // _HALO
    n_halo = t // _HALO
    n_nh = t // nh
    full = lambda shape: pl.BlockSpec(shape, lambda i: (0,) * len(shape))
    return pl.pallas_call(
        body, name="mix_bwd_taps", grid=(t // tm,),
        in_specs=[pl.BlockSpec((tm, c3), lambda i: (i, 0)),
                  pl.BlockSpec((_HALO, c3), lambda i: (jnp.maximum(i * hb - 1, 0), 0)),
                  pl.BlockSpec((tm, c), lambda i: (i, 0)),
                  pl.BlockSpec((_HALO, c), lambda i: (jnp.minimum((i + 1) * hb, n_halo - 1), 0)),
                  pl.BlockSpec((tm, c), lambda i: (i, 1)),
                  pl.BlockSpec((nh, c), lambda i: (jnp.minimum((i + 1) * (tm // nh), n_nh - 1), 1)),
                  full((_HALO, c)), full((ng, gd, gd)), full((1, c))],
        out_specs=[pl.BlockSpec((tm, c3), lambda i: (i, 0)), full((_HALO, c)), full((8, c)), full((ng, gd, gd))],
        out_shape=[jax.ShapeDtypeStruct((t, c3), _ACT), jax.ShapeDtypeStruct((_HALO, c), f32),
                   jax.ShapeDtypeStruct((8, c), f32), jax.ShapeDtypeStruct((ng, gd, gd), f32)],
        scratch_shapes=[pltpu.VMEM((tm, c), f32), pltpu.VMEM((tm + _HALO, c), f32),
                        pltpu.VMEM((_HALO + tm, c), f32), pltpu.VMEM((tm + nh, c), f32)],
        compiler_params=_params(1),
    )(u, u, dhc, dhc, dy, dy, cw, pw, ps)


def _attn_fwd(q, kv, n_seq, seq, n_mem):
    t, d = q.shape
    dh = d // XATTN_HEADS
    tq = min(1024, seq)
    nq = seq // tq
    scale = dh ** -0.5

    def body(q_ref, kv_ref, o_ref):
        for h in range(XATTN_HEADS):
            cols = slice(h * dh, (h + 1) * dh)
            s = _dot(q_ref[:, cols], kv_ref[:, cols], _NT) * scale
            e = jnp.exp(s - jnp.max(s, axis=-1, keepdims=True))
            p = e / jnp.sum(e, axis=-1, keepdims=True)
            o_ref[:, cols] = _dot(p.astype(_ACT), kv_ref[:, d + h * dh:d + (h + 1) * dh], _NN).astype(o_ref.dtype)

    qs = pl.BlockSpec((tq, d), lambda b, i: (b * nq + i, 0))
    return pl.pallas_call(
        body, name="attn_fwd", grid=(n_seq, nq), in_specs=[qs, pl.BlockSpec((n_mem, 2 * d), lambda b, i: (b, 0))],
        out_specs=qs, out_shape=jax.ShapeDtypeStruct((t, d), _ACT), compiler_params=_params(2),
    )(q, kv)


def _attn_bwd(q, kv, do, n_seq, seq, n_mem):
    t, d = q.shape
    dh = d // XATTN_HEADS
    tq = min(1024, seq)
    nq = seq // tq
    scale = dh ** -0.5

    def body(q_ref, kv_ref, do_ref, dq_ref, dkv_ref, acc):
        i = pl.program_id(1)

        @pl.when(i == 0)
        def _():
            acc[...] = jnp.zeros_like(acc)

        for h in range(XATTN_HEADS):
            cols = slice(h * dh, (h + 1) * dh)
            vcols = slice(d + h * dh, d + (h + 1) * dh)
            qv = q_ref[:, cols]
            kh = kv_ref[:, cols]
            dov = do_ref[:, cols]
            s = _dot(qv, kh, _NT) * scale
            e = jnp.exp(s - jnp.max(s, axis=-1, keepdims=True))
            p = e / jnp.sum(e, axis=-1, keepdims=True)
            dp = _dot(dov, kv_ref[:, vcols], _NT)
            ds = (p * (dp - jnp.sum(dp * p, axis=-1, keepdims=True)) * scale).astype(_ACT)
            dq_ref[:, cols] = _dot(ds, kh, _NN).astype(dq_ref.dtype)
            acc[:, cols] += _dot(ds, qv, _TN)
            acc[:, vcols] += _dot(p.astype(_ACT), dov, _TN)

        @pl.when(i == nq - 1)
        def _():
            dkv_ref[...] = acc[...].astype(dkv_ref.dtype)

    qs = pl.BlockSpec((tq, d), lambda b, i: (b * nq + i, 0))
    ms = pl.BlockSpec((n_mem, 2 * d), lambda b, i: (b, 0))
    return pl.pallas_call(
        body, name="attn_bwd", grid=(n_seq, nq), in_specs=[qs, ms, qs], out_specs=[qs, ms],
        out_shape=[jax.ShapeDtypeStruct((t, d), _ACT), jax.ShapeDtypeStruct((n_seq * n_mem, 2 * d), _ACT)],
        scratch_shapes=[pltpu.VMEM((n_mem, 2 * d), f32)], compiler_params=_params(2),
    )(q, kv, do)


_FFN_ROWS = 2048
_FFN_COLS = 256
_FFN_HALO = 16


def _window(buf, g, start, rows):
    return buf[g, pl.ds(start, rows + 8), :]


def _taps3(win, rows):
    return [_rows_from(win, 6 + k, rows) for k in range(3)]


def _conv3(b_ref, w_ref, taps):
    acc = b_ref[...] + w_ref[0:1, :] * taps[0]
    for k in (1, 2):
        acc = acc + w_ref[k:k + 1, :] * taps[k]
    return acc


def _ffn_gate_fwd(up, fw, fb, seq):
    _, t, f = up.shape
    tm = min(_FFN_ROWS, seq)
    tps = seq // tm
    tc = _FFN_COLS
    nc = f // tc
    hl = _FFN_HALO

    def body(up_ref, uph_ref, wg_ref, wv_ref, bg_ref, bv_ref, a_ref, uc_ref):
        i = pl.program_id(1)
        before = uph_ref[...]
        before = jnp.where(i % tps == 0, jnp.zeros_like(before), before)

        def chunk(r0, wins):
            conv = []
            for g, (w_ref, b_ref) in enumerate(((wg_ref, bg_ref), (wv_ref, bv_ref))):
                conv.append(_conv3(b_ref, w_ref, _taps3(wins[g].astype(f32)[hl - 8:, :], _CHUNK)))
                uc_ref[g, pl.ds(r0, _CHUNK), :] = conv[g].astype(uc_ref.dtype)
            gate, val = conv
            a_ref[pl.ds(r0, _CHUNK), :] = (gate * _sigmoid(gate) * val).astype(a_ref.dtype)

        chunk(0, [jnp.concatenate([before[g], up_ref[g, 0:_CHUNK, :]], axis=0) for g in range(2)])

        def later(ci, carry):
            r0 = pl.multiple_of(ci * _CHUNK, _CHUNK)
            chunk(r0, [up_ref[g, pl.ds(r0 - hl, _CHUNK + hl), :] for g in range(2)])
            return carry

        lax.fori_loop(1, tm // _CHUNK, later, 0)

    hb = tm // hl
    return pl.pallas_call(
        body, name="ffn_gate_fwd", grid=(nc, t // tm),
        in_specs=[pl.BlockSpec((2, tm, tc), lambda j, i: (0, i, j)),
                  pl.BlockSpec((2, hl, tc), lambda j, i: (0, jnp.maximum(i * hb - 1, 0), j)),
                  pl.BlockSpec((8, tc), lambda j, i: (0, j)), pl.BlockSpec((8, tc), lambda j, i: (0, nc + j)),
                  pl.BlockSpec((1, tc), lambda j, i: (0, j)), pl.BlockSpec((1, tc), lambda j, i: (0, nc + j))],
        out_specs=[pl.BlockSpec((tm, tc), lambda j, i: (i, j)), pl.BlockSpec((2, tm, tc), lambda j, i: (0, i, j))],
        out_shape=[jax.ShapeDtypeStruct((t, f), _ACT), jax.ShapeDtypeStruct((2, t, f), _ACT)], compiler_params=_params(2),
    )(up, up, fw, fw, fb, fb)


def _ffn_gate_bwd(up, uc, da, fw, seq):
    _, t, f = up.shape
    tm = min(_FFN_ROWS, seq)
    tps = seq // tm
    tc = _FFN_COLS
    nc = f // tc
    hl = _FFN_HALO

    def body(up_ref, uc_ref, ucn_ref, da_ref, dan_ref, wg_ref, wv_ref, dup_ref, sg_ref, sv_ref, dbuf, sums):
        i = pl.program_id(1)
        at_end = i % tps == tps - 1

        @pl.when(i == 0)
        def _():
            sg_ref[...] = jnp.zeros_like(sg_ref)
            sv_ref[...] = jnp.zeros_like(sv_ref)

        sums[...] = jnp.zeros_like(sums)
        w_refs = (wg_ref, wv_ref)

        def grads(r0, rows, conv, dav):
            gate, val = [v.astype(f32) for v in conv]
            sg = _sigmoid(gate)
            douts = (dav * val * (sg * (1.0 + gate * (1.0 - sg))), dav * (gate * sg))
            for g in range(2):
                dbuf[g, pl.ds(r0, rows), :] = douts[g]
            return douts

        def first(ci, carry):
            r0 = pl.multiple_of(ci * _CHUNK, _CHUNK)
            douts = grads(r0, _CHUNK, [uc_ref[g, pl.ds(r0, _CHUNK), :] for g in range(2)],
                          da_ref[pl.ds(r0, _CHUNK), :].astype(f32))
            for g in range(2):
                sums[g, 0] += douts[g].reshape(_CHUNK // 8, 8, tc).sum(axis=0)
            return carry

        lax.fori_loop(0, tm // _CHUNK, first, 0)
        da_after = dan_ref[...].astype(f32)
        grads(tm, hl, [ucn_ref[g] for g in range(2)], jnp.where(at_end, jnp.zeros_like(da_after), da_after))

        def second(ci, carry):
            r0 = pl.multiple_of(ci * _CHUNK, _CHUNK)
            for g in range(2):
                win = _window(dbuf, g, r0, _CHUNK)
                upv = up_ref[g, pl.ds(r0, _CHUNK), :].astype(f32)
                acc = jnp.zeros((_CHUNK, tc), f32)
                for k in range(3):
                    shifted = _rows_from(win, 2 - k, _CHUNK)
                    acc = acc + w_refs[g][k:k + 1, :] * shifted
                    sums[g, 1 + k] += (shifted * upv).reshape(_CHUNK // 8, 8, tc).sum(axis=0)
                dup_ref[g, pl.ds(r0, _CHUNK), :] = acc.astype(dup_ref.dtype)
            return carry

        lax.fori_loop(0, tm // _CHUNK, second, 0)
        for g, s_ref in enumerate((sg_ref, sv_ref)):
            for r in range(4):
                s_ref[r:r + 1, :] += jnp.sum(sums[g, r], axis=0, keepdims=True)

    hb = tm // hl
    n_halo = t // hl
    return pl.pallas_call(
        body, name="ffn_gate_bwd", grid=(nc, t // tm),
        in_specs=[pl.BlockSpec((2, tm, tc), lambda j, i: (0, i, j)),
                  pl.BlockSpec((2, tm, tc), lambda j, i: (0, i, j)),
                  pl.BlockSpec((2, hl, tc), lambda j, i: (0, jnp.minimum((i + 1) * hb, n_halo - 1), j)),
                  pl.BlockSpec((tm, tc), lambda j, i: (i, j)),
                  pl.BlockSpec((hl, tc), lambda j, i: (jnp.minimum((i + 1) * hb, n_halo - 1), j)),
                  pl.BlockSpec((8, tc), lambda j, i: (0, j)), pl.BlockSpec((8, tc), lambda j, i: (0, nc + j))],
        out_specs=[pl.BlockSpec((2, tm, tc), lambda j, i: (0, i, j)),
                   pl.BlockSpec((8, tc), lambda j, i: (0, j)), pl.BlockSpec((8, tc), lambda j, i: (0, j))],
        out_shape=[jax.ShapeDtypeStruct((2, t, f), _ACT), jax.ShapeDtypeStruct((8, f), f32), jax.ShapeDtypeStruct((8, f), f32)],
        scratch_shapes=[pltpu.VMEM((2, tm + hl, tc), f32), pltpu.VMEM((2, 4, 8, tc), f32)],
        compiler_params=_params(2),
    )(up, uc, uc, da, da, fw, fw)


def _adamw_math(w, g, m, v):
    m = ADAM_B1 * m + (1.0 - ADAM_B1) * g
    v = ADAM_B2 * v + (1.0 - ADAM_B2) * (g * g)
    m_hat = m / (1.0 - ADAM_B1 ** ADAM_STEP)
    v_hat = v / (1.0 - ADAM_B2 ** ADAM_STEP)
    delta = -ADAM_LR * (m_hat / (jnp.sqrt(v_hat) + ADAM_EPS) + ADAM_WD * w)
    return delta, m, v


def _adamw_shards(quads):
    n = len(quads)
    steps = 8

    def body(*refs):
        for p in range(n):
            w_ref, g_ref, m_ref, v_ref = refs[4 * p:4 * p + 4]
            go_ref, d_ref, mo_ref, vo_ref = refs[4 * n + 4 * p:4 * n + 4 * p + 4]
            gv = g_ref[...]
            d, mn, vn = _adamw_math(w_ref[...], gv, m_ref[...], v_ref[...])
            go_ref[...] = gv
            d_ref[...] = d
            mo_ref[...] = mn
            vo_ref[...] = vn

    in_specs, out_specs, out_shape = [], [], []
    for w, _, _, _ in quads:
        _, r, c = w.shape
        s3 = pl.BlockSpec((None, r // steps, c), lambda i: (0, i, 0))
        in_specs += [s3, pl.BlockSpec((r // steps, c), lambda i: (i, 0)), s3, s3]
        out_specs += [s3] * 4
        out_shape += [jax.ShapeDtypeStruct(w.shape, f32)] * 4
    outs = pl.pallas_call(
        body, name="adamw_shards", grid=(steps,), in_specs=in_specs, out_specs=out_specs, out_shape=out_shape,
        compiler_params=_params(1),
    )(*[a for q in quads for a in q])
    return [tuple(outs[4 * p:4 * p + 4]) for p in range(n)]


def _adamw_small(quads):
    n = len(quads)

    def body(*refs):
        ins, outs = refs[:4 * n], refs[4 * n:]
        for p in range(n):
            w_ref, g_ref, m_ref, v_ref = ins[4 * p:4 * p + 4]
            d, mn, vn = _adamw_math(w_ref[...], g_ref[...], m_ref[...], v_ref[...])
            outs[3 * p][...] = d
            outs[3 * p + 1][...] = mn
            outs[3 * p + 2][...] = vn

    flat = [a for q in quads for a in q]
    shapes = [jax.ShapeDtypeStruct(q[0].shape, f32) for q in quads for _ in range(3)]
    outs = pl.pallas_call(
        body, name="adamw_small", in_specs=[_VMEM] * (4 * n), out_specs=[_VMEM] * (3 * n), out_shape=shapes,
        compiler_params=pltpu.CompilerParams(vmem_limit_bytes=_VMEM_LIMIT_BYTES),
    )(*flat)
    return [tuple(outs[3 * p:3 * p + 3]) for p in range(n)]


def _sum_partials(name, place, grads, got):
    nw = len(grads)
    steps = 2

    def body(place_ref, *refs):
        for w in range(nw):
            own_ref, got_ref, f_ref = refs[w], refs[nw + w], refs[2 * nw + w]
            s = own_ref[...].astype(f32)
            for k in range(got[w].shape[0]):
                s = s + got_ref[k].astype(f32)
            f_ref[...] = s

    own_specs, got_specs, out_specs, out_shape = [], [], [], []
    for g, l in zip(grads, got):
        _, r, c = g.shape
        tr = r // steps
        own_specs.append(pl.BlockSpec((None, tr, c), lambda i, p: (2 * p[0] + p[1], i, 0)))
        got_specs.append(pl.BlockSpec((l.shape[0], tr, c), lambda i, p: (0, i, 0)))
        out_specs.append(pl.BlockSpec((None, tr, c), lambda i, p: (p[1], i, 0)))
        out_shape.append(jax.ShapeDtypeStruct((2, r, c), f32))
    grid_spec = pltpu.PrefetchScalarGridSpec(num_scalar_prefetch=1, grid=(steps,), in_specs=own_specs + got_specs, out_specs=out_specs)
    return pl.pallas_call(body, name=name, grid_spec=grid_spec, out_shape=out_shape,
                          compiler_params=_params(1))(place, *grads, *got)


def _place():
    return lax.axis_index("x"), lax.axis_index("y"), lax.axis_index("c")


def _other_chips(x, y):
    return [(1 - x, y), (x, 1 - y), (1 - x, 1 - y)]


def _remote(src, dst, send_sem, recv_sem, to):
    return pltpu.make_async_remote_copy(src_ref=src, dst_ref=dst, send_sem=send_sem, recv_sem=recv_sem,
                                        device_id=to, device_id_type=_MESH)


def _place_shards(name, place, shards, col_sharded, after=None):
    n = len(shards)
    steps = 4
    more, more_specs = _after(after)

    def body(place_ref, *refs):
        for src, dst in zip(refs[:n], refs[n + len(more):]):
            dst[...] = src[...].astype(dst.dtype)

    in_specs, out_specs, out_shape = [], [], []
    for w, col in zip(shards, col_sharded):
        r, cs = w.shape
        tr = r // steps
        in_specs.append(pl.BlockSpec((tr, cs), lambda i, p: (i, 0)))
        if col:
            out_specs.append(pl.BlockSpec((tr, cs), lambda i, p: (i, p[0])))
            out_shape.append(jax.ShapeDtypeStruct((r, 4 * cs), _ACT))
        else:
            out_specs.append(pl.BlockSpec((tr, cs), lambda i, p: (p[0] * steps + i, 0)))
            out_shape.append(jax.ShapeDtypeStruct((4 * r, cs), _ACT))
    grid_spec = pltpu.PrefetchScalarGridSpec(num_scalar_prefetch=1, grid=(steps,), in_specs=in_specs + more_specs,
                                            out_specs=out_specs)
    return pl.pallas_call(body, name=name, grid_spec=grid_spec, out_shape=out_shape,
                          compiler_params=_params(1))(place, *shards, *more)


def _shard_of(ref, col_sharded, s):
    rows, cols = ref.shape
    if col_sharded:
        return ref.at[:, pl.ds(s * (cols // 4), cols // 4)]
    return ref.at[pl.ds(s * (rows // 4), rows // 4), :]


def _part_of(ref, col_sharded, whole, s, h):
    if whole:
        return _shard_of(ref, col_sharded, s)
    rows, cols = ref.shape
    if col_sharded:
        return ref.at[pl.ds(h * (rows // 2), rows // 2), pl.ds(s * (cols // 4), cols // 4)]
    return ref.at[pl.ds((2 * s + h) * (rows // 8), rows // 8), :]


def _allgather_start(name, bufs, col_sharded, whole, groups):
    n = len(bufs)
    ng = len(groups)

    def body(*refs):
        out = refs[n:2 * n]
        sems = refs[2 * n:2 * n + 2 * ng]
        token = refs[2 * n + 2 * ng]
        x, y, c = _place()
        for g, members in enumerate(groups):
            for i, w in enumerate(members):
                mine = _part_of(out[w], col_sharded[w], whole[w], 2 * x + y, c)
                for j, chip in enumerate(_other_chips(x, y)):
                    _remote(mine, mine, sems[2 * g].at[3 * i + j], sems[2 * g + 1].at[3 * i + j], (*chip, c)).start()
        token[...] = jnp.zeros_like(token)

    sem_shapes = [pltpu.SemaphoreType.DMA((3 * len(m),)) for m in groups for _ in range(2)]
    outs = pl.pallas_call(
        body, name=name, in_specs=[_HBM] * n, out_specs=[_HBM] * n + [_SEM] * (2 * ng) + [_VMEM],
        out_shape=[pltpu.HBM(b.shape, b.dtype) for b in bufs] + sem_shapes + [jax.ShapeDtypeStruct((8, 128), f32)],
        input_output_aliases={i: i for i in range(n)},
        compiler_params=pltpu.CompilerParams(has_side_effects=_EFFECT),
    )(*[pltpu.with_memory_space_constraint(b, pltpu.HBM) for b in bufs])
    return list(outs[:n]), [(outs[n + 2 * g], outs[n + 2 * g + 1]) for g in range(ng)], outs[n + 2 * ng]


def _allgather_relay(name, bufs, col_sharded, whole, sems, after):
    n = len(bufs)

    def body(*refs):
        buf = refs[:n]
        send, recv = refs[n], refs[n + 1]
        out = refs[n + 3:2 * n + 3]
        to_sibling, from_sibling, token = refs[2 * n + 3:]
        token[...] = jnp.zeros_like(token)
        x, y, c = _place()
        for i in range(n):
            mine = _part_of(buf[i], col_sharded[i], whole[i], 2 * x + y, c)
            for j, chip in enumerate(_other_chips(x, y)):
                landed = _part_of(buf[i], col_sharded[i], whole[i], 2 * chip[0] + chip[1], c)
                cp = _remote(mine, landed, send.at[3 * i + j], recv.at[3 * i + j], (*chip, c))
                cp.wait_send()
                cp.wait_recv()
        for i in range(n):
            if not whole[i]:
                for j, chip in enumerate(_other_chips(x, y)):
                    landed = _part_of(out[i], col_sharded[i], False, 2 * chip[0] + chip[1], c)
                    _remote(landed, landed, to_sibling.at[3 * i + j], from_sibling.at[3 * i + j], (x, y, 1 - c)).start()

    outs = pl.pallas_call(
        body, name=name, in_specs=[_HBM] * n + [_SEM, _SEM, _ANY], out_specs=[_HBM] * n + [_SEM, _SEM, _VMEM],
        out_shape=[pltpu.HBM(b.shape, b.dtype) for b in bufs] + [pltpu.SemaphoreType.DMA((3 * n,))] * 2
        + [jax.ShapeDtypeStruct((8, 128), f32)],
        input_output_aliases={i: i for i in range(n)},
        compiler_params=pltpu.CompilerParams(has_side_effects=_EFFECT),
    )(*bufs, *sems, after)
    return list(outs[:n]), (outs[n], outs[n + 1]), outs[n + 2]


def _allgather_wait(name, bufs, col_sharded, whole, sems, after):
    n = len(bufs)

    def body(*refs):
        buf = refs[:n]
        to_sibling, from_sibling = refs[n], refs[n + 1]
        x, y, c = _place()
        for i in range(n):
            if not whole[i]:
                for j, chip in enumerate(_other_chips(x, y)):
                    sent = _part_of(buf[i], col_sharded[i], False, 2 * chip[0] + chip[1], c)
                    landed = _part_of(buf[i], col_sharded[i], False, 2 * chip[0] + chip[1], 1 - c)
                    cp = _remote(sent, landed, to_sibling.at[3 * i + j], from_sibling.at[3 * i + j], (x, y, 1 - c))
                    cp.wait_send()
                    cp.wait_recv()

    return pl.pallas_call(
        body, name=name, in_specs=[_HBM] * n + [_SEM, _SEM, _ANY], out_specs=[_HBM] * n,
        out_shape=[pltpu.HBM(b.shape, b.dtype) for b in bufs],
        input_output_aliases={i: i for i in range(n)},
        compiler_params=pltpu.CompilerParams(has_side_effects=_EFFECT),
    )(*bufs, *sems, after)


def _other_devices(x, y, c):
    flips = [(bx, by, bc) for bx in (0, 1) for by in (0, 1) for bc in (0, 1)][1:]
    return [(1 - x if bx else x, 1 - y if by else y, 1 - c if bc else c) for bx, by, bc in flips]


def _grad_exchange_start(name, grads):
    nw = len(grads)
    lands = [lax.empty((7,) + g.shape[1:], g.dtype) for g in grads]

    def body(*refs):
        src = refs[2 * nw:3 * nw]
        got = refs[3 * nw:4 * nw]
        send, recv, token = refs[4 * nw:]
        x, y, c = _place()
        for w in range(nw):
            for k, (px, py, pc) in enumerate(_other_devices(x, y, c)):
                _remote(src[w].at[4 * px + 2 * py + pc], got[w].at[k], send.at[7 * w + k], recv.at[7 * w + k], (px, py, pc)).start()
        token[...] = jnp.zeros_like(token)

    outs = pl.pallas_call(
        body, name=name, in_specs=[_HBM] * (2 * nw), out_specs=[_HBM] * (2 * nw) + [_SEM, _SEM, _VMEM],
        out_shape=[pltpu.HBM(a.shape, a.dtype) for a in list(grads) + lands]
        + [pltpu.SemaphoreType.DMA((7 * nw,)), pltpu.SemaphoreType.DMA((7 * nw,)), jax.ShapeDtypeStruct((8, 128), f32)],
        input_output_aliases={i: i for i in range(2 * nw)},
        compiler_params=pltpu.CompilerParams(has_side_effects=_EFFECT),
    )(*[pltpu.with_memory_space_constraint(a, pltpu.HBM) for a in list(grads) + lands])
    return list(outs[:nw]), list(outs[nw:2 * nw]), (outs[2 * nw], outs[2 * nw + 1]), outs[2 * nw + 2]


def _grad_exchange_wait(name, grads, got, sems, after):
    nw = len(grads)

    def body(*refs):
        src = refs[:nw]
        land = refs[nw:2 * nw]
        send, recv = refs[2 * nw], refs[2 * nw + 1]
        x, y, c = _place()
        for w in range(nw):
            for k, (px, py, pc) in enumerate(_other_devices(x, y, c)):
                cp = _remote(src[w].at[4 * px + 2 * py + pc], land[w].at[k], send.at[7 * w + k], recv.at[7 * w + k], (px, py, pc))
                cp.wait_send()
                cp.wait_recv()

    outs = pl.pallas_call(
        body, name=name, in_specs=[_HBM] * (2 * nw) + [_SEM, _SEM, _ANY], out_specs=[_HBM] * (2 * nw),
        out_shape=[pltpu.HBM(a.shape, a.dtype) for a in list(grads) + list(got)],
        input_output_aliases={i: i for i in range(2 * nw)},
        compiler_params=pltpu.CompilerParams(has_side_effects=_EFFECT),
    )(*grads, *got, *sems, after)
    return list(outs[:nw]), list(outs[nw:])


def _swap_halves_start(finals):
    nw = len(finals)

    def body(*refs):
        buf = refs[nw:2 * nw]
        send, recv, token = refs[2 * nw:]
        x, y, c = _place()
        for w in range(nw):
            _remote(buf[w].at[c], buf[w].at[c], send.at[w], recv.at[w], (x, y, 1 - c)).start()
        token[...] = jnp.zeros_like(token)

    outs = pl.pallas_call(
        body, name="rs_swap_start", in_specs=[_HBM] * nw, out_specs=[_HBM] * nw + [_SEM, _SEM, _VMEM],
        out_shape=[pltpu.HBM(g.shape, g.dtype) for g in finals] + [pltpu.SemaphoreType.DMA((nw,))] * 2
        + [jax.ShapeDtypeStruct((8, 128), f32)],
        input_output_aliases={i: i for i in range(nw)},
        compiler_params=pltpu.CompilerParams(has_side_effects=_EFFECT),
    )(*[pltpu.with_memory_space_constraint(g, pltpu.HBM) for g in finals])
    return list(outs[:nw]), (outs[nw], outs[nw + 1]), outs[nw + 2]


def _swap_halves_wait(bufs, sems, after):
    nw = len(bufs)

    def body(*refs):
        buf = refs[:nw]
        send, recv = refs[nw], refs[nw + 1]
        x, y, c = _place()
        for w in range(nw):
            cp = _remote(buf[w].at[c], buf[w].at[1 - c], send.at[w], recv.at[w], (x, y, 1 - c))
            cp.wait_send()
            cp.wait_recv()

    return pl.pallas_call(
        body, name="rs_swap_wait", in_specs=[_HBM] * nw + [_SEM, _SEM, _ANY], out_specs=[_HBM] * nw,
        out_shape=[pltpu.HBM(g.shape, g.dtype) for g in bufs],
        input_output_aliases={i: i for i in range(nw)},
        compiler_params=pltpu.CompilerParams(has_side_effects=_EFFECT),
    )(*bufs, *sems, after)


def _half_slices(shape, h):
    rows, cols = shape
    if cols % 256 == 0:
        return (slice(None), slice(h * (cols // 2), (h + 1) * (cols // 2)))
    return (slice(h * (rows // 2), (h + 1) * (rows // 2)), slice(None))


def _allreduce_small(parts, after):
    n = len(parts)

    def body(*refs):
        src = refs[:n]
        refs = refs[n + 1:]
        out = refs[:n]
        sib = refs[n:2 * n]
        chip_sum = refs[2 * n:3 * n]
        slots = refs[3 * n:4 * n]
        pair_send, pair_recv, ici_send, ici_recv, swap_send, swap_recv = refs[4 * n:]
        x, y, c = _place()
        me_chip = 2 * x + y
        chips = _other_chips(x, y)
        pairs = [_remote(src[a], sib[a], pair_send.at[a], pair_recv.at[a], (x, y, 1 - c)) for a in range(n)]
        for rc in pairs:
            rc.start()
        for a in range(n):
            pairs[a].wait_recv()
            chip_sum[a][...] = src[a][...] + sib[a][...]
        for h in (0, 1):
            @pl.when(c == h)
            def _():
                sends = []
                for a in range(n):
                    idx = _half_slices(parts[a].shape, h)
                    for j, chip in enumerate(chips):
                        rc = _remote(chip_sum[a].at[idx], slots[a].at[me_chip].at[idx], ici_send.at[3 * a + j], ici_recv.at[3 * a + j], (*chip, h))
                        rc.start()
                        sends.append(rc)
                    slots[a][(me_chip,) + idx] = chip_sum[a][idx]
                for a in range(n):
                    idx = _half_slices(parts[a].shape, h)
                    for j, chip in enumerate(chips):
                        landed = slots[a].at[2 * chip[0] + chip[1]].at[idx]
                        _remote(landed, landed, ici_send.at[3 * a + j], ici_recv.at[3 * a + j], (x, y, c)).wait_recv()
                    total = slots[a][(0,) + idx]
                    for s in range(1, 4):
                        total = total + slots[a][(s,) + idx]
                    out[a][idx] = total
                    rc = _remote(out[a].at[idx], out[a].at[idx], swap_send.at[a], swap_recv.at[a], (x, y, 1 - h))
                    rc.start()
                    sends.append(rc)
                for a in range(n):
                    other = out[a].at[_half_slices(parts[a].shape, 1 - h)]
                    _remote(other, other, swap_send.at[a], swap_recv.at[a], (x, y, c)).wait_recv()
                for rc in sends:
                    rc.wait_send()
        for rc in pairs:
            rc.wait_send()

    return pl.pallas_call(
        body, name="allreduce_small", in_specs=[_VMEM] * n + [_ANY], out_specs=[_VMEM] * n,
        out_shape=[jax.ShapeDtypeStruct(p.shape, f32) for p in parts],
        scratch_shapes=[pltpu.VMEM(p.shape, f32) for p in parts] * 2 + [pltpu.VMEM((4,) + p.shape, f32) for p in parts]
        + [pltpu.SemaphoreType.DMA((n,)), pltpu.SemaphoreType.DMA((n,)), pltpu.SemaphoreType.DMA((3 * n,)),
           pltpu.SemaphoreType.DMA((3 * n,)), pltpu.SemaphoreType.DMA((n,)), pltpu.SemaphoreType.DMA((n,))],
        compiler_params=pltpu.CompilerParams(vmem_limit_bytes=_VMEM_LIMIT_BYTES),
    )(*parts, after)


def _local_step(x, mem, tgt, g_mix, g_xattn, g_mem, g_ffn, g_final, cb, lg, lb, pw, ps, fb, started, relay, weights, reduce,
                n_seq, seq, n_mem):
    t, d = x.shape
    f = fb.shape[1] // 2
    c = cb.shape[1]
    h1 = _rms_fwd("norm_mix", x, g_mix, after=started)
    relay(0, h1)
    w_in, cw, fw = weights(0, h1)
    u = _mm_nn("proj_in", h1, w_in, _ACT, w_in.shape[1])
    y, hc = _mix_fwd(u, cw, cb, lg, lb, pw, ps, seq)
    relay(1, y)
    w_out, w_q, w_kv, w_o = weights(1, y)
    x1, h2 = _proj_residual_norm("proj_out", y, w_out, x, g_xattn)
    q = _mm_nn("proj_q", h2, w_q, _ACT, d)
    mem_n = _rms_fwd("norm_mem", mem, g_mem)
    kv = _mm_nn("proj_kv", mem_n, w_kv, _ACT, 2 * d)
    o = _attn_fwd(q, kv, n_seq, seq, n_mem)
    x2, h3 = _proj_residual_norm("proj_o", o, w_o, x1, g_ffn, after=relay(2, o))
    w_up, w_down = weights(2, h3)
    up = _mm_nn("proj_up", h3, w_up, _ACT, f, split_out=True)
    a, uc = _ffn_gate_fwd(up, fw, fb, seq)
    dx3, dx3b, dg_final, loss = _proj_loss_bwd("proj_down", a, w_down, x2, g_final, tgt)
    da = _mm_nt("d_act", dx3b, w_down, _ACT)
    gw_down = _mm_tn_rows("dw_down", a, dx3b, f // 2, d // 2)
    dup, sums_g, sums_v = _ffn_gate_bwd(up, uc, da, fw, seq)
    gw_up = _mm_tn_pieces("dw_up", h3, dup, f // 2)
    token = reduce(0, [gw_down.reshape(8, -1, d), gw_up])
    dx2, dx2b, dg_ffn = _dproj_rms_bwd("d_h3", dup, w_up, x2, g_ffn, dx3, after=token)
    do = _mm_nt("d_o", dx2b, w_o, _ACT)
    gw_o = _mm_tn_rows("dw_o", o, dx2b, d, d // 2)
    dq, dkv = _attn_bwd(q, kv, do, n_seq, seq, n_mem)
    gw_q = _mm_tn_rows("dw_q", h2, dq, d, d // 2)
    gw_kv = _mm_tn_pieces("dw_kv", mem_n, dkv, d // 2)
    dmem_n = _mm_nt("d_mem_n", dkv, w_kv, f32)
    dg_mem = _rms_gain_grad("norm_mem_bwd", mem, dmem_n)
    dx1, dx1b, dg_xattn = _dproj_rms_bwd("d_h2", dq, w_q, x1, g_xattn, dx2)
    dy = _mm_nt("d_y", dx1b, w_out, _ACT)
    gw_out = _mm_tn_rows("dw_out", y, dx1b, d, d // 2)
    token = reduce(1, [gw_o.reshape(8, -1, d), gw_q.reshape(8, -1, d), gw_kv, gw_out.reshape(8, -1, d)])
    dhc, sums_norm = _mix_bwd_norm(hc, dy, lg, lb, token)
    du, d_cw, d_ps, d_pw = _mix_bwd_taps(u, dhc, dy, cw, pw, ps, seq)
    gw_in = _mm_tn_pieces("dw_in", h1, du, c * 3 // 4)
    token = reduce(2, [gw_in])
    grad_x, dg_mix = _dproj_rms_bwd("d_h1", du, w_in, x, g_mix, dx1, storage_copy=False, after=token)
    zero_row = jnp.zeros((1, d), f32)
    gains = jnp.concatenate([dg_mix, dg_xattn, dg_mem, dg_ffn, dg_final, jnp.pad(loss, ((0, 0), (0, d - 1))), zero_row, zero_row], axis=0)
    conv_rows = jnp.concatenate([sums_norm[2:3], sums_norm[0:1], sums_norm[1:2], d_ps[0:1], jnp.zeros((4, c), f32)], axis=0)
    ffn_rows = jnp.concatenate([sums_g, sums_v], axis=1)
    small = [gains, conv_rows, d_pw.reshape(-1, d_pw.shape[-1]), ffn_rows, d_cw]
    return grad_x, small


def kernel(x, mem, norm_mix_g, w_in, conv_dw_w, conv_dw_b, conv_ln_g, conv_ln_b, pool_w, pool_scale, w_out, norm_xattn_g, norm_mem_g, w_q, w_kv, w_o, norm_ffn_g, w_up, ffn_dw_w, ffn_dw_b, w_down, norm_final_g, loss_target, m_norm_mix_g, m_w_in, m_conv_dw_w, m_conv_dw_b, m_conv_ln_g, m_conv_ln_b, m_pool_w, m_pool_scale, m_w_out, m_norm_xattn_g, m_norm_mem_g, m_w_q, m_w_kv, m_w_o, m_norm_ffn_g, m_w_up, m_ffn_dw_w, m_ffn_dw_b, m_w_down, m_norm_final_g, v_norm_mix_g, v_w_in, v_conv_dw_w, v_conv_dw_b, v_conv_ln_g, v_conv_ln_b, v_pool_w, v_pool_scale, v_w_out, v_norm_xattn_g, v_norm_mem_g, v_w_q, v_w_kv, v_w_o, v_norm_ffn_g, v_w_up, v_ffn_dw_w, v_ffn_dw_b, v_w_down, v_norm_final_g):
    n_seq, seq, d = x.shape
    n_mem = mem.shape[1]
    chip = 2 * lax.axis_index("x") + lax.axis_index("y")

    place = jnp.stack([chip, lax.axis_index("c")]).astype(jnp.int32)

    col_w = [w_in, w_kv, w_up]
    row_w = [w_out, w_q, w_o, w_down]
    kw = conv_dw_w.shape[1]

    def padded_in_place(shard, rows):
        full = jnp.zeros((rows, 4 * shard.shape[1]), shard.dtype)
        return lax.dynamic_update_slice(full, shard, (0, chip * shard.shape[1]))

    first = list(_place_shards("place_w_in", place, [w_in[0]], [True]))
    first += [padded_in_place(conv_dw_w[0], _HALO), padded_in_place(ffn_dw_w[0], 8)]
    first, first_sems, token = _allgather_start("allgather_start_0", first, [True] * 3, [False, True, True], [[0, 1, 2]])
    rest = [w_kv, w_up, w_out, w_q, w_o, w_down]
    rest_flags = [True, True, False, False, False, False]
    rest = list(_place_shards("place_rest", place, [w[0] for w in rest], rest_flags, after=token))
    rest, rest_sems, all_started = _allgather_start("allgather_start_1", rest, rest_flags, [False] * 6, [[2, 3, 0, 4], [1, 5]])
    started = [(first, [True] * 3, [False, True, True], first_sems[0]),
               ([rest[i] for i in (2, 3, 0, 4)], [False, False, True, False], [False] * 4, rest_sems[0]),
               ([rest[i] for i in (1, 5)], [True, False], [False] * 2, rest_sems[1])]
    relayed = {}

    def relay(g, after):
        group_bufs, flags, wholes, group_sems = started[g]
        group_bufs, sibling_sems, relay_token = _allgather_relay("allgather_relay_%d" % g, group_bufs, flags, wholes, group_sems, after)
        relayed[g] = (group_bufs, sibling_sems)
        return relay_token

    def weights(g, after):
        group_bufs, sibling_sems = relayed[g]
        return _allgather_wait("allgather_wait_%d" % g, group_bufs, started[g][1], started[g][2], sibling_sems, after)

    names = ["w_in", "w_kv", "w_up", "w_out", "w_q", "w_o", "w_down"]
    reduce_groups = [["w_down", "w_up"], ["w_o", "w_q", "w_kv", "w_out"], ["w_in"]]
    in_flight = {}

    def reduce(g, grads):
        grads, lands, rs_sems, token = _grad_exchange_start("rs_start_%d" % g, grads)
        in_flight[g] = (grads, lands, rs_sems)
        return token

    grad_x, small = _local_step(
        x.reshape(n_seq * seq, d), mem.reshape(n_seq * n_mem, d), loss_target.reshape(n_seq * seq, d),
        norm_mix_g, norm_xattn_g, norm_mem_g, norm_ffn_g, norm_final_g.reshape(1, d),
        conv_dw_b, conv_ln_g, conv_ln_b, pool_w[0], pool_scale, ffn_dw_b, all_started, relay, weights, reduce,
        n_seq, seq, n_mem)

    landed = {}
    for g, members in enumerate(reduce_groups):
        grads, lands, rs_sems = in_flight[g]
        grads, lands = _grad_exchange_wait("rs_wait_%d" % g, grads, lands, rs_sems, grad_x)
        landed.update(zip(members, zip(grads, lands)))
    finals = _sum_partials("rs_sum", place, [landed[n][0] for n in names], [landed[n][1] for n in names])
    finals, swap_sems, token = _swap_halves_start(finals)

    gains, conv_rows, d_pw, ffn_rows, d_cw = _allreduce_small(small, token)
    loss = gains[5, 0]
    shard_grads = _swap_halves_wait(finals, swap_sems, gains)

    outs = {}
    big_w = dict(zip(names, col_w + row_w))
    big_m = dict(w_in=m_w_in, w_kv=m_w_kv, w_up=m_w_up, w_out=m_w_out, w_q=m_w_q, w_o=m_w_o, w_down=m_w_down)
    big_v = dict(w_in=v_w_in, w_kv=v_w_kv, w_up=v_w_up, w_out=v_w_out, w_q=v_w_q, w_o=v_w_o, w_down=v_w_down)
    big_quads = [(big_w[n], g.reshape(big_w[n].shape[1:]), big_m[n], big_v[n]) for n, g in zip(names, shard_grads)]
    outs.update(zip(names, _adamw_shards(big_quads)))

    f2 = ffn_dw_b.shape[1]
    cs_c = conv_dw_w.shape[2]
    cs_f = ffn_dw_w.shape[2]
    g_cw = lax.dynamic_slice(d_cw, (0, chip * cs_c), (kw, cs_c)).reshape(conv_dw_w.shape)
    g_fw = lax.dynamic_slice(ffn_rows, (1, chip * cs_f), (ffn_dw_w.shape[1], cs_f)).reshape(ffn_dw_w.shape)
    small_params = [
        ("norm_mix_g", norm_mix_g, gains[0:1], m_norm_mix_g, v_norm_mix_g),
        ("conv_dw_w", conv_dw_w, g_cw, m_conv_dw_w, v_conv_dw_w),
        ("conv_dw_b", conv_dw_b, conv_rows[0:1], m_conv_dw_b, v_conv_dw_b),
        ("conv_ln_g", conv_ln_g, conv_rows[1:2], m_conv_ln_g, v_conv_ln_g),
        ("conv_ln_b", conv_ln_b, conv_rows[2:3], m_conv_ln_b, v_conv_ln_b),
        ("pool_w", pool_w, d_pw.reshape(pool_w.shape), m_pool_w, v_pool_w),
        ("pool_scale", pool_scale, conv_rows[3:4], m_pool_scale, v_pool_scale),
        ("norm_xattn_g", norm_xattn_g, gains[1:2], m_norm_xattn_g, v_norm_xattn_g),
        ("norm_mem_g", norm_mem_g, gains[2:3], m_norm_mem_g, v_norm_mem_g),
        ("norm_ffn_g", norm_ffn_g, gains[3:4], m_norm_ffn_g, v_norm_ffn_g),
        ("ffn_dw_w", ffn_dw_w, g_fw, m_ffn_dw_w, v_ffn_dw_w),
        ("ffn_dw_b", ffn_dw_b, ffn_rows[0:1, :f2], m_ffn_dw_b, v_ffn_dw_b),
        ("norm_final_g", norm_final_g.reshape(1, d), gains[4:5], m_norm_final_g.reshape(1, d), v_norm_final_g.reshape(1, d)),
    ]
    quads = []
    for _, w, g, m, v in small_params:
        shape2 = (-1, w.shape[-1])
        quads.append((w.reshape(shape2), g.reshape(shape2), m.reshape(shape2), v.reshape(shape2)))
    for (n, w, g, _, _), (delta, new_m, new_v) in zip(small_params, _adamw_small(quads)):
        shape = norm_final_g.shape if n == "norm_final_g" else w.shape
        outs[n] = (g.reshape(shape), delta.reshape(shape), new_m.reshape(shape), new_v.reshape(shape))

    order = ["norm_mix_g", "w_in", "conv_dw_w", "conv_dw_b", "conv_ln_g", "conv_ln_b", "pool_w", "pool_scale", "w_out",
             "norm_xattn_g", "norm_mem_g", "w_q", "w_kv", "w_o", "norm_ffn_g", "w_up", "ffn_dw_w", "ffn_dw_b", "w_down",
             "norm_final_g"]
    return (loss, grad_x.reshape(x.shape), *[outs[n][0] for n in order], *[outs[n][1] for n in order],
            *[outs[n][2] for n in order], *[outs[n][3] for n in order])
```

```python
import jax
import jax.numpy as jnp
from jax import lax
from jax.experimental import pallas as pl
from jax.experimental.pallas import tpu as pltpu

f32 = jnp.float32
_ACT = jnp.bfloat16

EPS = 1e-6
POOL_WINDOWS = (2, 4, 8, 16)
XATTN_HEADS = 4
ADAM_LR = 0.001
ADAM_B1 = 0.9
ADAM_B2 = 0.999
ADAM_EPS = 1e-08
ADAM_WD = 0.01
ADAM_STEP = 10

_VMEM_LIMIT_BYTES = 56 * 1024 * 1024
_MESH = pl.DeviceIdType.MESH
_ANY = pl.BlockSpec(memory_space=pl.ANY)
_VMEM = pl.BlockSpec(memory_space=pltpu.VMEM)
_HBM = pl.BlockSpec(memory_space=pltpu.HBM)
_SEM = pl.BlockSpec(memory_space=pltpu.SEMAPHORE)
_EFFECT = pltpu.SideEffectType.DATAFLOW_SIDE_EFFECTING

_NN = (((1,), (0,)), ((), ()))
_NT = (((1,), (1,)), ((), ()))
_TN = (((0,), (0,)), ((), ()))


def _params(n_grid):
    return pltpu.CompilerParams(dimension_semantics=("arbitrary",) * n_grid, vmem_limit_bytes=_VMEM_LIMIT_BYTES)


def _sigmoid(v):
    return 1.0 / (1.0 + jnp.exp(-v))


def _dot(a, b, dims):
    return lax.dot_general(a, b, dims, preferred_element_type=f32)


def _mm(name, a, b, *, dims, grid, a_spec, b_spec, o_spec, out_shape):
    def body(a_ref, b_ref, o_ref):
        o_ref[...] = _dot(a_ref[...], b_ref[...], dims).astype(o_ref.dtype)

    return pl.pallas_call(
        body, name=name, grid=grid, in_specs=[a_spec, b_spec], out_specs=o_spec, out_shape=out_shape,
        compiler_params=_params(len(grid)),
    )(a, b)


_NARROW = 2816


def _row_tile(m, width=_NARROW + 1):
    return min(1024 if width <= _NARROW else 512, m)


def _mm_nn(name, a, b, out_dtype, tn, split_out=False):
    m, k = a.shape
    n = b.shape[1]
    tm = _row_tile(m, max(k, tn))
    if split_out:
        out_shape = jax.ShapeDtypeStruct((n // tn, m, tn), out_dtype)
        o_spec = pl.BlockSpec((None, tm, tn), lambda j, i: (j, i, 0))
    else:
        out_shape = jax.ShapeDtypeStruct((m, n), out_dtype)
        o_spec = pl.BlockSpec((tm, tn), lambda j, i: (i, j))
    return _mm(
        name, a, b, dims=_NN, grid=(n // tn, m // tm),
        a_spec=pl.BlockSpec((tm, k), lambda j, i: (i, 0)), b_spec=pl.BlockSpec((k, tn), lambda j, i: (0, j)),
        o_spec=o_spec, out_shape=out_shape,
    )


def _mm_nt(name, a, b, out_dtype):
    n, kc = b.shape
    m = a.shape[0]
    tm = _row_tile(m, max(n, kc))
    return _mm(
        name, a, b, dims=_NT, grid=(m // tm,),
        a_spec=pl.BlockSpec((tm, kc), lambda i: (i, 0)),
        b_spec=pl.BlockSpec((n, kc), lambda i: (0, 0), pipeline_mode=pl.Buffered(1)),
        o_spec=pl.BlockSpec((tm, n), lambda i: (i, 0)),
        out_shape=jax.ShapeDtypeStruct((m, n), out_dtype),
    )


def _mm_tn_rows(name, a, b, tka, tn):
    m, ka = a.shape
    nb = b.shape[1]
    return _mm(
        name, a, b, dims=_TN, grid=(ka // tka, nb // tn),
        a_spec=pl.BlockSpec((m, tka), lambda i, j: (0, i)), b_spec=pl.BlockSpec((m, tn), lambda i, j: (0, j)),
        o_spec=pl.BlockSpec((tka, tn), lambda i, j: (i, j)),
        out_shape=jax.ShapeDtypeStruct((ka, nb), _ACT),
    )


def _mm_tn_pieces(name, a, b, cs):
    m, ka = a.shape
    if b.ndim == 3:
        b_spec = pl.BlockSpec((None, m, cs), lambda i, j: (j // 2, 0, j % 2))
    else:
        b_spec = pl.BlockSpec((m, cs), lambda i, j: (0, j))
    return _mm(
        name, a, b, dims=_TN, grid=(2, 4),
        a_spec=pl.BlockSpec((m, ka // 2), lambda i, j: (0, i)), b_spec=b_spec,
        o_spec=pl.BlockSpec((None, ka // 2, cs), lambda i, j: (2 * j + i, 0, 0)),
        out_shape=jax.ShapeDtypeStruct((8, ka // 2, cs), _ACT),
    )


def _after(after):
    return ([], []) if after is None else ([after], [_ANY])


def _rms_fwd(name, x, g, after=None):
    t, d = x.shape
    tm = _row_tile(t, d)
    more, more_specs = _after(after)

    def body(x_ref, g_ref, *refs):
        h_ref = refs[-1]
        xv = x_ref[...]
        r = lax.rsqrt(jnp.mean(xv * xv, axis=-1, keepdims=True) + EPS)
        h_ref[...] = (xv * r * g_ref[...]).astype(h_ref.dtype)

    return pl.pallas_call(
        body, name=name, grid=(t // tm,),
        in_specs=[pl.BlockSpec((tm, d), lambda i: (i, 0)), pl.BlockSpec((1, d), lambda i: (0, 0))] + more_specs,
        out_specs=pl.BlockSpec((tm, d), lambda i: (i, 0)), out_shape=jax.ShapeDtypeStruct((t, d), _ACT),
        compiler_params=_params(1),
    )(x, g, *more)


def _fused_rows(name, a, b, product, a_spec, tm, extras, extra_specs, out_shape, out_specs, epilogue):
    ne = len(extras)

    def body(a_ref, b_ref, *refs):
        epilogue(product(a_ref, b_ref), refs[:ne], refs[ne:])

    m = extras[0].shape[0]
    return pl.pallas_call(
        body, name=name, grid=(m // tm,),
        in_specs=[a_spec, pl.BlockSpec(b.shape, lambda i: (0, 0), pipeline_mode=pl.Buffered(1)), *extra_specs],
        out_specs=out_specs, out_shape=out_shape, compiler_params=_params(1),
    )(a, b, *extras)


def _proj_residual_norm(name, a, b, res, g, after=None):
    m, k = a.shape
    d = b.shape[1]
    tm = _row_tile(m, max(k, d))

    def epilogue(p, ins, outs):
        xv = p + ins[0][...]
        outs[0][...] = xv
        r = lax.rsqrt(jnp.mean(xv * xv, axis=-1, keepdims=True) + EPS)
        outs[1][...] = (xv * r * ins[1][...]).astype(outs[1].dtype)

    row = pl.BlockSpec((tm, d), lambda i: (i, 0))
    return _fused_rows(
        name, a, b, lambda a_ref, b_ref: _dot(a_ref[...], b_ref[...], _NN), pl.BlockSpec((tm, k), lambda i: (i, 0)), tm,
        [res, g] + _after(after)[0], [row, pl.BlockSpec((1, d), lambda i: (0, 0))] + _after(after)[1],
        [jax.ShapeDtypeStruct((m, d), f32), jax.ShapeDtypeStruct((m, d), _ACT)], [row, row], epilogue)


def _dproj_rms_bwd(name, a, b, x, g, dres, storage_copy=True, after=None):
    m, d = x.shape
    if a.ndim == 3:
        nh, _, kh = a.shape
        tm = _row_tile(m, nh * kh)
        a_spec = pl.BlockSpec((nh, tm, kh), lambda i: (0, i, 0))

        def product(a_ref, b_ref):
            p = _dot(a_ref[0], b_ref[:, 0:kh], _NT)
            for h in range(1, nh):
                p = p + _dot(a_ref[h], b_ref[:, h * kh:(h + 1) * kh], _NT)
            return p
    else:
        tm = _row_tile(m, max(a.shape[1], d))
        a_spec = pl.BlockSpec((tm, a.shape[1]), lambda i: (i, 0))

        def product(a_ref, b_ref):
            return _dot(a_ref[...], b_ref[...], _NT)

    def epilogue(dhv, ins, outs):
        x_ref, g_ref, dres_ref = ins[:3]
        dg_ref = outs[-1]

        @pl.when(pl.program_id(0) == 0)
        def _():
            dg_ref[...] = jnp.zeros_like(dg_ref)

        xv = x_ref[...]
        r = lax.rsqrt(jnp.mean(xv * xv, axis=-1, keepdims=True) + EPS)
        xn = xv * r
        dxn = dhv * g_ref[...]
        dx = r * (dxn - xn * jnp.mean(dxn * xn, axis=-1, keepdims=True)) + dres_ref[...]
        outs[0][...] = dx
        if storage_copy:
            outs[1][...] = dx.astype(outs[1].dtype)
        dg_ref[...] += jnp.sum(dhv * xn, axis=0, keepdims=True)

    row = pl.BlockSpec((tm, d), lambda i: (i, 0))
    vec = pl.BlockSpec((1, d), lambda i: (0, 0))
    copies = [jax.ShapeDtypeStruct((m, d), _ACT)] if storage_copy else []
    return _fused_rows(
        name, a, b, product, a_spec, tm, [x, g, dres] + _after(after)[0], [row, vec, row] + _after(after)[1],
        [jax.ShapeDtypeStruct((m, d), f32)] + copies + [jax.ShapeDtypeStruct((1, d), f32)],
        [row] * (1 + len(copies)) + [vec], epilogue)


def _proj_loss_bwd(name, a, b, res, g, tgt):
    m, k = a.shape
    d = b.shape[1]
    tm = _row_tile(m, max(k, d))

    def epilogue(p, ins, outs):
        res_ref, g_ref, t_ref = ins
        dx_ref, dxb_ref, dg_ref, loss_ref = outs

        @pl.when(pl.program_id(0) == 0)
        def _():
            dg_ref[...] = jnp.zeros_like(dg_ref)
            loss_ref[...] = jnp.zeros_like(loss_ref)

        xv = p + res_ref[...]
        gv = g_ref[...]
        r = lax.rsqrt(jnp.mean(xv * xv, axis=-1, keepdims=True) + EPS)
        xn = xv * r
        err = xn * gv - t_ref[...]
        loss_ref[...] += 0.5 * jnp.sum(jnp.mean(err * err, axis=-1, keepdims=True), axis=0, keepdims=True)
        dout = err * (1.0 / d)
        dxn = dout * gv
        dx = r * (dxn - xn * jnp.mean(dxn * xn, axis=-1, keepdims=True))
        dx_ref[...] = dx
        dxb_ref[...] = dx.astype(dxb_ref.dtype)
        dg_ref[...] += jnp.sum(dout * xn, axis=0, keepdims=True)

    row = pl.BlockSpec((tm, d), lambda i: (i, 0))
    vec = pl.BlockSpec((1, d), lambda i: (0, 0))
    return _fused_rows(
        name, a, b, lambda a_ref, b_ref: _dot(a_ref[...], b_ref[...], _NN), pl.BlockSpec((tm, k), lambda i: (i, 0)), tm,
        [res, g, tgt], [row, vec, row],
        [jax.ShapeDtypeStruct((m, d), f32), jax.ShapeDtypeStruct((m, d), _ACT), jax.ShapeDtypeStruct((1, d), f32),
         jax.ShapeDtypeStruct((1, 1), f32)],
        [row, row, vec, pl.BlockSpec((1, 1), lambda i: (0, 0))], epilogue)


def _rms_gain_grad(name, x, dh):
    t, d = x.shape
    tm = _row_tile(t)

    def body(x_ref, dh_ref, dg_ref):
        @pl.when(pl.program_id(0) == 0)
        def _():
            dg_ref[...] = jnp.zeros_like(dg_ref)

        xv = x_ref[...]
        r = lax.rsqrt(jnp.mean(xv * xv, axis=-1, keepdims=True) + EPS)
        dg_ref[...] += jnp.sum(dh_ref[...] * (xv * r), axis=0, keepdims=True)

    row = pl.BlockSpec((tm, d), lambda i: (i, 0))
    return pl.pallas_call(
        body, name=name, grid=(t // tm,), in_specs=[row, row], out_specs=pl.BlockSpec((1, d), lambda i: (0, 0)),
        out_shape=jax.ShapeDtypeStruct((1, d), f32), compiler_params=_params(1),
    )(x, dh)


_CONV_ROWS = 512
_CHUNK = 64
_HALO = 32


def _pool_counts(pos, w):
    return jnp.minimum(pos + 1.0, float(w))


def _rows_from(win, start, rows):
    if start % 8 == 0:
        return win[start:start + rows, :]
    n = win.shape[0]
    return pltpu.roll(win, n - start % 8, axis=0)[start - start % 8:start - start % 8 + rows, :]


def _tap_rows(buf, starts, rows):
    for residue in range(8):
        group = [(k, s) for k, s in starts.items() if s % 8 == residue]
        if group:
            lo = min(s for _, s in group) - residue
            hi = max(s for _, s in group) - residue + rows + (8 if residue else 0)
            win = buf[lo:hi, :]
            if residue:
                win = pltpu.roll(win, hi - lo - residue, axis=0)
            for k, s in group:
                yield k, win[s - residue - lo:s - residue - lo + rows, :]


def _mix_fwd(u, cw, cb, lg, lb, pw, ps, seq):
    t, c3 = u.shape
    c = c3 // 3
    kw = 31
    tm = min(_CONV_ROWS, seq)
    tps = seq // tm
    gd = c // len(POOL_WINDOWS)

    def body(u_ref, uh_ref, cw_ref, cb_ref, lg_ref, lb_ref, pw_ref, ps_ref, y_ref, hc_ref, hgbuf, pbuf):
        i = pl.program_id(0)
        keep = jnp.where(i % tps == 0, 0.0, 1.0)
        um = u_ref[...].astype(f32)
        uh = uh_ref[...].astype(f32) * keep
        hgbuf[0:_HALO, :] = uh[:, 0:c] * _sigmoid(uh[:, c:2 * c])
        hgbuf[_HALO:_HALO + tm, :] = um[:, 0:c] * _sigmoid(um[:, c:2 * c])
        pbuf[0:_HALO, :] = uh[:, 2 * c:]
        pbuf[_HALO:_HALO + tm, :] = um[:, 2 * c:]
        for r0 in range(0, tm, _CHUNK):
            acc = jnp.broadcast_to(cb_ref[...], (_CHUNK, c))
            for k, rows in _tap_rows(hgbuf, {k: r0 + _HALO - (kw - 1) + k for k in range(kw)}, _CHUNK):
                acc = acc + cw_ref[k:k + 1, :] * rows
            hc_ref[r0:r0 + _CHUNK, :] = acc
            mu = jnp.mean(acc, axis=-1, keepdims=True)
            xc = acc - mu
            var = jnp.mean(xc * xc, axis=-1, keepdims=True)
            hl = xc * lax.rsqrt(var + EPS) * lg_ref[...] + lb_ref[...]
            y_ref[r0:r0 + _CHUNK, 0:c] = (hl * _sigmoid(hl)).astype(y_ref.dtype)
        pos = ((i % tps) * tm).astype(f32) + lax.broadcasted_iota(jnp.int32, (tm, 1), 0).astype(f32)
        for gi, w in enumerate(POOL_WINDOWS):
            sl = slice(gi * gd, (gi + 1) * gd)
            v = pbuf[_HALO:_HALO + tm, sl]
            s = v
            for j in range(1, w):
                s = s + pbuf[_HALO - j:_HALO - j + tm, sl]
            pooled = s / _pool_counts(pos, w) - v
            mixed = _dot(pooled.astype(_ACT), pw_ref[gi].astype(_ACT), _NN)
            y_ref[:, c + gi * gd:c + (gi + 1) * gd] = (mixed * ps_ref[:, sl]).astype(y_ref.dtype)

    hb = tm // _HALO
    full = lambda shape: pl.BlockSpec(shape, lambda i: (0,) * len(shape))
    return pl.pallas_call(
        body, name="mix_fwd", grid=(t // tm,),
        in_specs=[pl.BlockSpec((tm, c3), lambda i: (i, 0)),
                  pl.BlockSpec((_HALO, c3), lambda i: (jnp.maximum(i * hb - 1, 0), 0)),
                  full((_HALO, c)), full((1, c)), full((1, c)), full((1, c)), full((len(POOL_WINDOWS), gd, gd)), full((1, c))],
        out_specs=[pl.BlockSpec((tm, 2 * c), lambda i: (i, 0)), pl.BlockSpec((tm, c), lambda i: (i, 0))],
        out_shape=[jax.ShapeDtypeStruct((t, 2 * c), _ACT), jax.ShapeDtypeStruct((t, c), f32)],
        scratch_shapes=[pltpu.VMEM((_HALO + tm, c), f32), pltpu.VMEM((_HALO + tm, c), f32)],
        compiler_params=_params(1),
    )(u, u, cw, cb, lg, lb, pw, ps)


def _mix_bwd_norm(hc, dy, lg, lb, after):
    t, c = hc.shape
    tm = _row_tile(t, c)

    def body(hc_ref, dy_ref, lg_ref, lb_ref, after_ref, dhc_ref, sums_ref):
        @pl.when(pl.program_id(0) == 0)
        def _():
            sums_ref[...] = jnp.zeros_like(sums_ref)

        hcv = hc_ref[...]
        mu = jnp.mean(hcv, axis=-1, keepdims=True)
        xc = hcv - mu
        rstd = lax.rsqrt(jnp.mean(xc * xc, axis=-1, keepdims=True) + EPS)
        n = xc * rstd
        hl = n * lg_ref[...] + lb_ref[...]
        sg = _sigmoid(hl)
        dhl = dy_ref[...].astype(f32) * (sg * (1.0 + hl * (1.0 - sg)))
        dn = dhl * lg_ref[...]
        dhc = rstd * (dn - jnp.mean(dn, axis=-1, keepdims=True) - n * jnp.mean(dn * n, axis=-1, keepdims=True))
        dhc_ref[...] = dhc
        sums_ref[0:1, :] += jnp.sum(dhl * n, axis=0, keepdims=True)
        sums_ref[1:2, :] += jnp.sum(dhl, axis=0, keepdims=True)
        sums_ref[2:3, :] += jnp.sum(dhc, axis=0, keepdims=True)

    row = pl.BlockSpec((tm, c), lambda i: (i, 0))
    vec = pl.BlockSpec((1, c), lambda i: (0, 0))
    return pl.pallas_call(
        body, name="mix_bwd_norm", grid=(t // tm,), in_specs=[row, row, vec, vec, _ANY],
        out_specs=[row, pl.BlockSpec((8, c), lambda i: (0, 0))],
        out_shape=[jax.ShapeDtypeStruct((t, c), f32), jax.ShapeDtypeStruct((8, c), f32)],
        compiler_params=_params(1),
    )(hc, dy, lg, lb, after)


def _mix_bwd_taps(u, dhc, dy, cw, pw, ps, seq):
    t, c3 = u.shape
    c = c3 // 3
    kw = 31
    tm = min(_CONV_ROWS, seq)
    tps = seq // tm
    ng = len(POOL_WINDOWS)
    gd = c // ng
    nh = 16

    def body(u_ref, uh_ref, dhc_ref, dhcn_ref, dy_ref, dyn_ref, cw_ref, pw_ref, ps_ref,
             du_ref, dcw_ref, dps_ref, dpw_ref, hgbuf, dcbuf, pbuf, dpbuf):
        i = pl.program_id(0)
        keep_prev = jnp.where(i % tps == 0, 0.0, 1.0)
        keep_next = jnp.where(i % tps == tps - 1, 0.0, 1.0)

        @pl.when(i == 0)
        def _():
            dcw_ref[...] = jnp.zeros_like(dcw_ref)
            dps_ref[...] = jnp.zeros_like(dps_ref)
            dpw_ref[...] = jnp.zeros_like(dpw_ref)

        uh = uh_ref[...].astype(f32) * keep_prev
        hgbuf[0:_HALO, :] = uh[:, 0:c] * _sigmoid(uh[:, c:2 * c])
        pbuf[0:_HALO, :] = uh[:, 2 * c:]
        um = u_ref[...].astype(f32)
        hgbuf[_HALO:_HALO + tm, :] = um[:, 0:c] * _sigmoid(um[:, c:2 * c])
        pbuf[_HALO:_HALO + tm, :] = um[:, 2 * c:]
        dcbuf[0:tm, :] = dhc_ref[...]
        dcbuf[tm:tm + _HALO, :] = dhcn_ref[...] * keep_next
        tap_sums = [None] * kw
        for r0 in range(0, tm, _CHUNK):
            dh = dcbuf[r0:r0 + _CHUNK, :]
            acc = jnp.zeros((_CHUNK, c), f32)
            for k, rows in _tap_rows(hgbuf, {k: r0 + _HALO - (kw - 1) + k for k in range(kw)}, _CHUNK):
                part = (dh * rows).reshape(_CHUNK // 8, 8, c).sum(axis=0)
                tap_sums[k] = part if tap_sums[k] is None else tap_sums[k] + part
            for k, rows in _tap_rows(dcbuf, {k: r0 + (kw - 1) - k for k in range(kw)}, _CHUNK):
                acc = acc + cw_ref[k:k + 1, :] * rows
            val = u_ref[r0:r0 + _CHUNK, 0:c].astype(f32)
            sg = _sigmoid(u_ref[r0:r0 + _CHUNK, c:2 * c].astype(f32))
            du_ref[r0:r0 + _CHUNK, 0:c] = (acc * sg).astype(du_ref.dtype)
            du_ref[r0:r0 + _CHUNK, c:2 * c] = (acc * val * sg * (1.0 - sg)).astype(du_ref.dtype)
        for k in range(kw):
            dcw_ref[k:k + 1, :] += jnp.sum(tap_sums[k], axis=0, keepdims=True)
        base = ((i % tps) * tm).astype(f32)
        pos = base + lax.broadcasted_iota(jnp.int32, (tm, 1), 0).astype(f32)
        pos_next = base + float(tm) + lax.broadcasted_iota(jnp.int32, (nh, 1), 0).astype(f32)
        for gi, w in enumerate(POOL_WINDOWS):
            sl = slice(gi * gd, (gi + 1) * gd)
            v = pbuf[_HALO:_HALO + tm, sl]
            s = v
            for j in range(1, w):
                s = s + pbuf[_HALO - j:_HALO - j + tm, sl]
            cnt = _pool_counts(pos, w)
            pooled = (s / cnt - v).astype(_ACT)
            pwg = pw_ref[gi].astype(_ACT)
            mixed = _dot(pooled, pwg, _NN)
            dyp = dy_ref[:, sl].astype(f32)
            dps_ref[0:1, sl] += jnp.sum(dyp * mixed, axis=0, keepdims=True)
            dmix = (dyp * ps_ref[:, sl]).astype(_ACT)
            dpw_ref[gi] += _dot(pooled, dmix, _TN)
            dmix_next = (dyn_ref[:, sl].astype(f32) * ps_ref[:, sl] * keep_next).astype(_ACT)
            dpool = _dot(dmix, pwg, _NT)
            dpbuf[0:tm, sl] = dpool / cnt
            dpbuf[tm:tm + nh, sl] = _dot(dmix_next, pwg, _NT) / _pool_counts(pos_next, w)
            acc = -dpool
            for j in range(w):
                acc = acc + dpbuf[j:j + tm, sl]
            du_ref[:, 2 * c + gi * gd:2 * c + (gi + 1) * gd] = acc.astype(du_ref.dtype)

    hb = tm // _HALO
    n_halo = t // _HALO
    n_nh = t // nh
    full = lambda shape: pl.BlockSpec(shape, lambda i: (0,) * len(shape))
    return pl.pallas_call(
        body, name="mix_bwd_taps", grid=(t // tm,),
        in_specs=[pl.BlockSpec((tm, c3), lambda i: (i, 0)),
                  pl.BlockSpec((_HALO, c3), lambda i: (jnp.maximum(i * hb - 1, 0), 0)),
                  pl.BlockSpec((tm, c), lambda i: (i, 0)),
                  pl.BlockSpec((_HALO, c), lambda i: (jnp.minimum((i + 1) * hb, n_halo - 1), 0)),
                  pl.BlockSpec((tm, c), lambda i: (i, 1)),
                  pl.BlockSpec((nh, c), lambda i: (jnp.minimum((i + 1) * (tm // nh), n_nh - 1), 1)),
                  full((_HALO, c)), full((ng, gd, gd)), full((1, c))],
        out_specs=[pl.BlockSpec((tm, c3), lambda i: (i, 0)), full((_HALO, c)), full((8, c)), full((ng, gd, gd))],
        out_shape=[jax.ShapeDtypeStruct((t, c3), _ACT), jax.ShapeDtypeStruct((_HALO, c), f32),
                   jax.ShapeDtypeStruct((8, c), f32), jax.ShapeDtypeStruct((ng, gd, gd), f32)],
        scratch_shapes=[pltpu.VMEM((_HALO + tm, c), f32), pltpu.VMEM((tm + _HALO, c), f32),
                        pltpu.VMEM((_HALO + tm, c), f32), pltpu.VMEM((tm + nh, c), f32)],
        compiler_params=_params(1),
    )(u, u, dhc, dhc, dy, dy, cw, pw, ps)


def _attn_fwd(q, kv, n_seq, seq, n_mem):
    t, d = q.shape
    dh = d // XATTN_HEADS
    tq = min(1024, seq)
    nq = seq // tq
    scale = dh ** -0.5

    def body(q_ref, kv_ref, o_ref):
        for h in range(XATTN_HEADS):
            cols = slice(h * dh, (h + 1) * dh)
            s = _dot(q_ref[:, cols], kv_ref[:, cols], _NT) * scale
            e = jnp.exp(s - jnp.max(s, axis=-1, keepdims=True))
            p = e / jnp.sum(e, axis=-1, keepdims=True)
            o_ref[:, cols] = _dot(p.astype(_ACT), kv_ref[:, d + h * dh:d + (h + 1) * dh], _NN).astype(o_ref.dtype)

    qs = pl.BlockSpec((tq, d), lambda b, i: (b * nq + i, 0))
    return pl.pallas_call(
        body, name="attn_fwd", grid=(n_seq, nq), in_specs=[qs, pl.BlockSpec((n_mem, 2 * d), lambda b, i: (b, 0))],
        out_specs=qs, out_shape=jax.ShapeDtypeStruct((t, d), _ACT), compiler_params=_params(2),
    )(q, kv)


def _attn_bwd(q, kv, do, n_seq, seq, n_mem):
    t, d = q.shape
    dh = d // XATTN_HEADS
    tq = min(1024, seq)
    nq = seq // tq
    scale = dh ** -0.5

    def body(q_ref, kv_ref, do_ref, dq_ref, dkv_ref, acc):
        i = pl.program_id(1)

        @pl.when(i == 0)
        def _():
            acc[...] = jnp.zeros_like(acc)

        for h in range(XATTN_HEADS):
            cols = slice(h * dh, (h + 1) * dh)
            vcols = slice(d + h * dh, d + (h + 1) * dh)
            qv = q_ref[:, cols]
            kh = kv_ref[:, cols]
            dov = do_ref[:, cols]
            s = _dot(qv, kh, _NT) * scale
            e = jnp.exp(s - jnp.max(s, axis=-1, keepdims=True))
            p = e / jnp.sum(e, axis=-1, keepdims=True)
            dp = _dot(dov, kv_ref[:, vcols], _NT)
            ds = (p * (dp - jnp.sum(dp * p, axis=-1, keepdims=True)) * scale).astype(_ACT)
            dq_ref[:, cols] = _dot(ds, kh, _NN).astype(dq_ref.dtype)
            acc[:, cols] += _dot(ds, qv, _TN)
            acc[:, vcols] += _dot(p.astype(_ACT), dov, _TN)

        @pl.when(i == nq - 1)
        def _():
            dkv_ref[...] = acc[...].astype(dkv_ref.dtype)

    qs = pl.BlockSpec((tq, d), lambda b, i: (b * nq + i, 0))
    ms = pl.BlockSpec((n_mem, 2 * d), lambda b, i: (b, 0))
    return pl.pallas_call(
        body, name="attn_bwd", grid=(n_seq, nq), in_specs=[qs, ms, qs], out_specs=[qs, ms],
        out_shape=[jax.ShapeDtypeStruct((t, d), _ACT), jax.ShapeDtypeStruct((n_seq * n_mem, 2 * d), _ACT)],
        scratch_shapes=[pltpu.VMEM((n_mem, 2 * d), f32)], compiler_params=_params(2),
    )(q, kv, do)


_FFN_ROWS = 2048
_FFN_COLS = 256
_FFN_HALO = 16


def _window(buf, g, start, rows):
    return buf[g, pl.ds(start, rows + 8), :]


def _taps3(win, rows):
    return [_rows_from(win, 6 + k, rows) for k in range(3)]


def _conv3(b_ref, w_ref, taps):
    acc = b_ref[...] + w_ref[0:1, :] * taps[0]
    for k in (1, 2):
        acc = acc + w_ref[k:k + 1, :] * taps[k]
    return acc


def _ffn_gate_fwd(up, fw, fb, seq):
    _, t, f = up.shape
    tm = min(_FFN_ROWS, seq)
    tps = seq // tm
    tc = _FFN_COLS
    nc = f // tc
    hl = _FFN_HALO

    def body(up_ref, uph_ref, wg_ref, wv_ref, bg_ref, bv_ref, a_ref, uc_ref):
        i = pl.program_id(1)
        before = uph_ref[...]
        before = jnp.where(i % tps == 0, jnp.zeros_like(before), before)

        def chunk(r0, wins):
            conv = []
            for g, (w_ref, b_ref) in enumerate(((wg_ref, bg_ref), (wv_ref, bv_ref))):
                conv.append(_conv3(b_ref, w_ref, _taps3(wins[g].astype(f32)[hl - 8:, :], _CHUNK)))
                uc_ref[g, pl.ds(r0, _CHUNK), :] = conv[g].astype(uc_ref.dtype)
            gate, val = conv
            a_ref[pl.ds(r0, _CHUNK), :] = (gate * _sigmoid(gate) * val).astype(a_ref.dtype)

        chunk(0, [jnp.concatenate([before[g], up_ref[g, 0:_CHUNK, :]], axis=0) for g in range(2)])

        def later(ci, carry):
            r0 = pl.multiple_of(ci * _CHUNK, _CHUNK)
            chunk(r0, [up_ref[g, pl.ds(r0 - hl, _CHUNK + hl), :] for g in range(2)])
            return carry

        lax.fori_loop(1, tm // _CHUNK, later, 0, unroll=2)

    hb = tm // hl
    return pl.pallas_call(
        body, name="ffn_gate_fwd", grid=(nc, t // tm),
        in_specs=[pl.BlockSpec((2, tm, tc), lambda j, i: (0, i, j)),
                  pl.BlockSpec((2, hl, tc), lambda j, i: (0, jnp.maximum(i * hb - 1, 0), j)),
                  pl.BlockSpec((8, tc), lambda j, i: (0, j)), pl.BlockSpec((8, tc), lambda j, i: (0, nc + j)),
                  pl.BlockSpec((1, tc), lambda j, i: (0, j)), pl.BlockSpec((1, tc), lambda j, i: (0, nc + j))],
        out_specs=[pl.BlockSpec((tm, tc), lambda j, i: (i, j)), pl.BlockSpec((2, tm, tc), lambda j, i: (0, i, j))],
        out_shape=[jax.ShapeDtypeStruct((t, f), _ACT), jax.ShapeDtypeStruct((2, t, f), _ACT)], compiler_params=_params(2),
    )(up, up, fw, fw, fb, fb)


def _ffn_gate_bwd(up, uc, da, fw, seq):
    _, t, f = up.shape
    tm = min(_FFN_ROWS, seq)
    tps = seq // tm
    tc = _FFN_COLS
    nc = f // tc
    hl = _FFN_HALO

    def body(up_ref, uc_ref, ucn_ref, da_ref, dan_ref, wg_ref, wv_ref, dup_ref, sg_ref, sv_ref, dbuf, sums):
        i = pl.program_id(1)
        at_end = i % tps == tps - 1

        @pl.when(i == 0)
        def _():
            sg_ref[...] = jnp.zeros_like(sg_ref)
            sv_ref[...] = jnp.zeros_like(sv_ref)

        sums[...] = jnp.zeros_like(sums)
        w_refs = (wg_ref, wv_ref)

        def grads(r0, rows, conv, dav):
            gate, val = [v.astype(f32) for v in conv]
            sg = _sigmoid(gate)
            douts = (dav * val * (sg * (1.0 + gate * (1.0 - sg))), dav * (gate * sg))
            for g in range(2):
                dbuf[g, pl.ds(r0, rows), :] = douts[g]
            return douts

        def first(ci, carry):
            r0 = pl.multiple_of(ci * _CHUNK, _CHUNK)
            douts = grads(r0, _CHUNK, [uc_ref[g, pl.ds(r0, _CHUNK), :] for g in range(2)],
                          da_ref[pl.ds(r0, _CHUNK), :].astype(f32))
            for g in range(2):
                sums[g, 0] += douts[g].reshape(_CHUNK // 8, 8, tc).sum(axis=0)
            return carry

        lax.fori_loop(0, tm // _CHUNK, first, 0, unroll=2)
        da_after = dan_ref[...].astype(f32)
        grads(tm, hl, [ucn_ref[g] for g in range(2)], jnp.where(at_end, jnp.zeros_like(da_after), da_after))

        def second(ci, carry):
            r0 = pl.multiple_of(ci * _CHUNK, _CHUNK)
            for g in range(2):
                win = _window(dbuf, g, r0, _CHUNK)
                upv = up_ref[g, pl.ds(r0, _CHUNK), :].astype(f32)
                acc = jnp.zeros((_CHUNK, tc), f32)
                for k in range(3):
                    shifted = _rows_from(win, 2 - k, _CHUNK)
                    acc = acc + w_refs[g][k:k + 1, :] * shifted
                    sums[g, 1 + k] += (shifted * upv).reshape(_CHUNK // 8, 8, tc).sum(axis=0)
                dup_ref[g, pl.ds(r0, _CHUNK), :] = acc.astype(dup_ref.dtype)
            return carry

        lax.fori_loop(0, tm // _CHUNK, second, 0, unroll=2)
        for g, s_ref in enumerate((sg_ref, sv_ref)):
            for r in range(4):
                s_ref[r:r + 1, :] += jnp.sum(sums[g, r], axis=0, keepdims=True)

    hb = tm // hl
    n_halo = t // hl
    return pl.pallas_call(
        body, name="ffn_gate_bwd", grid=(nc, t // tm),
        in_specs=[pl.BlockSpec((2, tm, tc), lambda j, i: (0, i, j)),
                  pl.BlockSpec((2, tm, tc), lambda j, i: (0, i, j)),
                  pl.BlockSpec((2, hl, tc), lambda j, i: (0, jnp.minimum((i + 1) * hb, n_halo - 1), j)),
                  pl.BlockSpec((tm, tc), lambda j, i: (i, j)),
                  pl.BlockSpec((hl, tc), lambda j, i: (jnp.minimum((i + 1) * hb, n_halo - 1), j)),
                  pl.BlockSpec((8, tc), lambda j, i: (0, j)), pl.BlockSpec((8, tc), lambda j, i: (0, nc + j))],
        out_specs=[pl.BlockSpec((2, tm, tc), lambda j, i: (0, i, j)),
                   pl.BlockSpec((8, tc), lambda j, i: (0, j)), pl.BlockSpec((8, tc), lambda j, i: (0, j))],
        out_shape=[jax.ShapeDtypeStruct((2, t, f), _ACT), jax.ShapeDtypeStruct((8, f), f32), jax.ShapeDtypeStruct((8, f), f32)],
        scratch_shapes=[pltpu.VMEM((2, tm + hl, tc), f32), pltpu.VMEM((2, 4, 8, tc), f32)],
        compiler_params=_params(2),
    )(up, uc, uc, da, da, fw, fw)


def _adamw_math(w, g, m, v):
    m = ADAM_B1 * m + (1.0 - ADAM_B1) * g
    v = ADAM_B2 * v + (1.0 - ADAM_B2) * (g * g)
    m_hat = m / (1.0 - ADAM_B1 ** ADAM_STEP)
    v_hat = v / (1.0 - ADAM_B2 ** ADAM_STEP)
    delta = -ADAM_LR * (m_hat / (jnp.sqrt(v_hat) + ADAM_EPS) + ADAM_WD * w)
    return delta, m, v


def _adamw_shards(quads):
    n = len(quads)
    steps = 8

    def body(*refs):
        for p in range(n):
            w_ref, g_ref, m_ref, v_ref = refs[4 * p:4 * p + 4]
            go_ref, d_ref, mo_ref, vo_ref = refs[4 * n + 4 * p:4 * n + 4 * p + 4]
            gv = g_ref[...]
            d, mn, vn = _adamw_math(w_ref[...], gv, m_ref[...], v_ref[...])
            go_ref[...] = gv
            d_ref[...] = d
            mo_ref[...] = mn
            vo_ref[...] = vn

    in_specs, out_specs, out_shape = [], [], []
    for w, _, _, _ in quads:
        _, r, c = w.shape
        s3 = pl.BlockSpec((None, r // steps, c), lambda i: (0, i, 0))
        in_specs += [s3, pl.BlockSpec((r // steps, c), lambda i: (i, 0)), s3, s3]
        out_specs += [s3] * 4
        out_shape += [jax.ShapeDtypeStruct(w.shape, f32)] * 4
    outs = pl.pallas_call(
        body, name="adamw_shards", grid=(steps,), in_specs=in_specs, out_specs=out_specs, out_shape=out_shape,
        compiler_params=_params(1),
    )(*[a for q in quads for a in q])
    return [tuple(outs[4 * p:4 * p + 4]) for p in range(n)]


def _adamw_small(quads):
    n = len(quads)

    def body(*refs):
        ins, outs = refs[:4 * n], refs[4 * n:]
        for p in range(n):
            w_ref, g_ref, m_ref, v_ref = ins[4 * p:4 * p + 4]
            d, mn, vn = _adamw_math(w_ref[...], g_ref[...], m_ref[...], v_ref[...])
            outs[3 * p][...] = d
            outs[3 * p + 1][...] = mn
            outs[3 * p + 2][...] = vn

    flat = [a for q in quads for a in q]
    shapes = [jax.ShapeDtypeStruct(q[0].shape, f32) for q in quads for _ in range(3)]
    outs = pl.pallas_call(
        body, name="adamw_small", in_specs=[_VMEM] * (4 * n), out_specs=[_VMEM] * (3 * n), out_shape=shapes,
        compiler_params=pltpu.CompilerParams(vmem_limit_bytes=_VMEM_LIMIT_BYTES),
    )(*flat)
    return [tuple(outs[3 * p:3 * p + 3]) for p in range(n)]


def _sum_partials(name, place, grads, got):
    nw = len(grads)
    steps = 2

    def body(place_ref, *refs):
        for w in range(nw):
            own_ref, got_ref, f_ref = refs[w], refs[nw + w], refs[2 * nw + w]
            s = own_ref[...].astype(f32)
            for k in range(got[w].shape[0]):
                s = s + got_ref[k].astype(f32)
            f_ref[...] = s

    own_specs, got_specs, out_specs, out_shape = [], [], [], []
    for g, l in zip(grads, got):
        _, r, c = g.shape
        tr = r // steps
        own_specs.append(pl.BlockSpec((None, tr, c), lambda i, p: (2 * p[0] + p[1], i, 0)))
        got_specs.append(pl.BlockSpec((l.shape[0], tr, c), lambda i, p: (0, i, 0)))
        out_specs.append(pl.BlockSpec((None, tr, c), lambda i, p: (p[1], i, 0)))
        out_shape.append(jax.ShapeDtypeStruct((2, r, c), f32))
    grid_spec = pltpu.PrefetchScalarGridSpec(num_scalar_prefetch=1, grid=(steps,), in_specs=own_specs + got_specs, out_specs=out_specs)
    return pl.pallas_call(body, name=name, grid_spec=grid_spec, out_shape=out_shape,
                          compiler_params=_params(1))(place, *grads, *got)


def _place():
    return lax.axis_index("x"), lax.axis_index("y"), lax.axis_index("c")


def _other_chips(x, y):
    return [(1 - x, y), (x, 1 - y), (1 - x, 1 - y)]


def _remote(src, dst, send_sem, recv_sem, to):
    return pltpu.make_async_remote_copy(src_ref=src, dst_ref=dst, send_sem=send_sem, recv_sem=recv_sem,
                                        device_id=to, device_id_type=_MESH)


def _place_shards(name, place, shards, col_sharded, after=None):
    n = len(shards)
    steps = 4
    more, more_specs = _after(after)

    def body(place_ref, *refs):
        for src, dst in zip(refs[:n], refs[n + len(more):]):
            dst[...] = src[...].astype(dst.dtype)

    in_specs, out_specs, out_shape = [], [], []
    for w, col in zip(shards, col_sharded):
        r, cs = w.shape
        tr = r // steps
        in_specs.append(pl.BlockSpec((tr, cs), lambda i, p: (i, 0)))
        if col:
            out_specs.append(pl.BlockSpec((tr, cs), lambda i, p: (i, p[0])))
            out_shape.append(jax.ShapeDtypeStruct((r, 4 * cs), _ACT))
        else:
            out_specs.append(pl.BlockSpec((tr, cs), lambda i, p: (p[0] * steps + i, 0)))
            out_shape.append(jax.ShapeDtypeStruct((4 * r, cs), _ACT))
    grid_spec = pltpu.PrefetchScalarGridSpec(num_scalar_prefetch=1, grid=(steps,), in_specs=in_specs + more_specs,
                                            out_specs=out_specs)
    return pl.pallas_call(body, name=name, grid_spec=grid_spec, out_shape=out_shape,
                          compiler_params=_params(1))(place, *shards, *more)


def _shard_of(ref, col_sharded, s):
    rows, cols = ref.shape
    if col_sharded:
        return ref.at[:, pl.ds(s * (cols // 4), cols // 4)]
    return ref.at[pl.ds(s * (rows // 4), rows // 4), :]


def _part_of(ref, col_sharded, whole, s, h):
    if whole:
        return _shard_of(ref, col_sharded, s)
    rows, cols = ref.shape
    if col_sharded:
        return ref.at[pl.ds(h * (rows // 2), rows // 2), pl.ds(s * (cols // 4), cols // 4)]
    return ref.at[pl.ds((2 * s + h) * (rows // 8), rows // 8), :]


def _allgather_start(name, bufs, col_sharded, whole, groups):
    n = len(bufs)
    ng = len(groups)

    def body(*refs):
        out = refs[n:2 * n]
        sems = refs[2 * n:2 * n + 2 * ng]
        token = refs[2 * n + 2 * ng]
        x, y, c = _place()
        for g, members in enumerate(groups):
            for i, w in enumerate(members):
                mine = _part_of(out[w], col_sharded[w], whole[w], 2 * x + y, c)
                for j, chip in enumerate(_other_chips(x, y)):
                    _remote(mine, mine, sems[2 * g].at[3 * i + j], sems[2 * g + 1].at[3 * i + j], (*chip, c)).start()
        token[...] = jnp.zeros_like(token)

    sem_shapes = [pltpu.SemaphoreType.DMA((3 * len(m),)) for m in groups for _ in range(2)]
    outs = pl.pallas_call(
        body, name=name, in_specs=[_HBM] * n, out_specs=[_HBM] * n + [_SEM] * (2 * ng) + [_VMEM],
        out_shape=[pltpu.HBM(b.shape, b.dtype) for b in bufs] + sem_shapes + [jax.ShapeDtypeStruct((8, 128), f32)],
        input_output_aliases={i: i for i in range(n)},
        compiler_params=pltpu.CompilerParams(has_side_effects=_EFFECT),
    )(*[pltpu.with_memory_space_constraint(b, pltpu.HBM) for b in bufs])
    return list(outs[:n]), [(outs[n + 2 * g], outs[n + 2 * g + 1]) for g in range(ng)], outs[n + 2 * ng]


def _allgather_relay(name, bufs, col_sharded, whole, sems, after):
    n = len(bufs)

    def body(*refs):
        buf = refs[:n]
        send, recv = refs[n], refs[n + 1]
        out = refs[n + 3:2 * n + 3]
        to_sibling, from_sibling, token = refs[2 * n + 3:]
        token[...] = jnp.zeros_like(token)
        x, y, c = _place()
        for i in range(n):
            mine = _part_of(buf[i], col_sharded[i], whole[i], 2 * x + y, c)
            for j, chip in enumerate(_other_chips(x, y)):
                landed = _part_of(buf[i], col_sharded[i], whole[i], 2 * chip[0] + chip[1], c)
                cp = _remote(mine, landed, send.at[3 * i + j], recv.at[3 * i + j], (*chip, c))
                cp.wait_send()
                cp.wait_recv()
        for i in range(n):
            if not whole[i]:
                for j, chip in enumerate(_other_chips(x, y)):
                    landed = _part_of(out[i], col_sharded[i], False, 2 * chip[0] + chip[1], c)
                    _remote(landed, landed, to_sibling.at[3 * i + j], from_sibling.at[3 * i + j], (x, y, 1 - c)).start()

    outs = pl.pallas_call(
        body, name=name, in_specs=[_HBM] * n + [_SEM, _SEM, _ANY], out_specs=[_HBM] * n + [_SEM, _SEM, _VMEM],
        out_shape=[pltpu.HBM(b.shape, b.dtype) for b in bufs] + [pltpu.SemaphoreType.DMA((3 * n,))] * 2
        + [jax.ShapeDtypeStruct((8, 128), f32)],
        input_output_aliases={i: i for i in range(n)},
        compiler_params=pltpu.CompilerParams(has_side_effects=_EFFECT),
    )(*bufs, *sems, after)
    return list(outs[:n]), (outs[n], outs[n + 1]), outs[n + 2]


def _allgather_wait(name, bufs, col_sharded, whole, sems, after):
    n = len(bufs)

    def body(*refs):
        buf = refs[:n]
        to_sibling, from_sibling = refs[n], refs[n + 1]
        x, y, c = _place()
        for i in range(n):
            if not whole[i]:
                for j, chip in enumerate(_other_chips(x, y)):
                    sent = _part_of(buf[i], col_sharded[i], False, 2 * chip[0] + chip[1], c)
                    landed = _part_of(buf[i], col_sharded[i], False, 2 * chip[0] + chip[1], 1 - c)
                    cp = _remote(sent, landed, to_sibling.at[3 * i + j], from_sibling.at[3 * i + j], (x, y, 1 - c))
                    cp.wait_send()
                    cp.wait_recv()

    return pl.pallas_call(
        body, name=name, in_specs=[_HBM] * n + [_SEM, _SEM, _ANY], out_specs=[_HBM] * n,
        out_shape=[pltpu.HBM(b.shape, b.dtype) for b in bufs],
        input_output_aliases={i: i for i in range(n)},
        compiler_params=pltpu.CompilerParams(has_side_effects=_EFFECT),
    )(*bufs, *sems, after)


def _other_devices(x, y, c):
    flips = [(bx, by, bc) for bx in (0, 1) for by in (0, 1) for bc in (0, 1)][1:]
    return [(1 - x if bx else x, 1 - y if by else y, 1 - c if bc else c) for bx, by, bc in flips]


def _grad_exchange_start(name, grads):
    nw = len(grads)
    lands = [lax.empty((7,) + g.shape[1:], g.dtype) for g in grads]

    def body(*refs):
        src = refs[2 * nw:3 * nw]
        got = refs[3 * nw:4 * nw]
        send, recv, token = refs[4 * nw:]
        x, y, c = _place()
        for w in range(nw):
            for k, (px, py, pc) in enumerate(_other_devices(x, y, c)):
                _remote(src[w].at[4 * px + 2 * py + pc], got[w].at[k], send.at[7 * w + k], recv.at[7 * w + k], (px, py, pc)).start()
        token[...] = jnp.zeros_like(token)

    outs = pl.pallas_call(
        body, name=name, in_specs=[_HBM] * (2 * nw), out_specs=[_HBM] * (2 * nw) + [_SEM, _SEM, _VMEM],
        out_shape=[pltpu.HBM(a.shape, a.dtype) for a in list(grads) + lands]
        + [pltpu.SemaphoreType.DMA((7 * nw,)), pltpu.SemaphoreType.DMA((7 * nw,)), jax.ShapeDtypeStruct((8, 128), f32)],
        input_output_aliases={i: i for i in range(2 * nw)},
        compiler_params=pltpu.CompilerParams(has_side_effects=_EFFECT),
    )(*[pltpu.with_memory_space_constraint(a, pltpu.HBM) for a in list(grads) + lands])
    return list(outs[:nw]), list(outs[nw:2 * nw]), (outs[2 * nw], outs[2 * nw + 1]), outs[2 * nw + 2]


def _grad_exchange_wait(name, grads, got, sems, after):
    nw = len(grads)

    def body(*refs):
        src = refs[:nw]
        land = refs[nw:2 * nw]
        send, recv = refs[2 * nw], refs[2 * nw + 1]
        x, y, c = _place()
        for w in range(nw):
            for k, (px, py, pc) in enumerate(_other_devices(x, y, c)):
                cp = _remote(src[w].at[4 * px + 2 * py + pc], land[w].at[k], send.at[7 * w + k], recv.at[7 * w + k], (px, py, pc))
                cp.wait_send()
                cp.wait_recv()

    outs = pl.pallas_call(
        body, name=name, in_specs=[_HBM] * (2 * nw) + [_SEM, _SEM, _ANY], out_specs=[_HBM] * (2 * nw),
        out_shape=[pltpu.HBM(a.shape, a.dtype) for a in list(grads) + list(got)],
        input_output_aliases={i: i for i in range(2 * nw)},
        compiler_params=pltpu.CompilerParams(has_side_effects=_EFFECT),
    )(*grads, *got, *sems, after)
    return list(outs[:nw]), list(outs[nw:])


def _swap_halves_start(finals):
    nw = len(finals)

    def body(*refs):
        buf = refs[nw:2 * nw]
        send, recv, token = refs[2 * nw:]
        x, y, c = _place()
        for w in range(nw):
            _remote(buf[w].at[c], buf[w].at[c], send.at[w], recv.at[w], (x, y, 1 - c)).start()
        token[...] = jnp.zeros_like(token)

    outs = pl.pallas_call(
        body, name="rs_swap_start", in_specs=[_HBM] * nw, out_specs=[_HBM] * nw + [_SEM, _SEM, _VMEM],
        out_shape=[pltpu.HBM(g.shape, g.dtype) for g in finals] + [pltpu.SemaphoreType.DMA((nw,))] * 2
        + [jax.ShapeDtypeStruct((8, 128), f32)],
        input_output_aliases={i: i for i in range(nw)},
        compiler_params=pltpu.CompilerParams(has_side_effects=_EFFECT),
    )(*[pltpu.with_memory_space_constraint(g, pltpu.HBM) for g in finals])
    return list(outs[:nw]), (outs[nw], outs[nw + 1]), outs[nw + 2]


def _swap_halves_wait(bufs, sems, after):
    nw = len(bufs)

    def body(*refs):
        buf = refs[:nw]
        send, recv = refs[nw], refs[nw + 1]
        x, y, c = _place()
        for w in range(nw):
            cp = _remote(buf[w].at[c], buf[w].at[1 - c], send.at[w], recv.at[w], (x, y, 1 - c))
            cp.wait_send()
            cp.wait_recv()

    return pl.pallas_call(
        body, name="rs_swap_wait", in_specs=[_HBM] * nw + [_SEM, _SEM, _ANY], out_specs=[_HBM] * nw,
        out_shape=[pltpu.HBM(g.shape, g.dtype) for g in bufs],
        input_output_aliases={i: i for i in range(nw)},
        compiler_params=pltpu.CompilerParams(has_side_effects=_EFFECT),
    )(*bufs, *sems, after)


def _half_slices(shape, h):
    rows, cols = shape
    if cols % 256 == 0:
        return (slice(None), slice(h * (cols // 2), (h + 1) * (cols // 2)))
    return (slice(h * (rows // 2), (h + 1) * (rows // 2)), slice(None))


def _allreduce_small(parts, after):
    n = len(parts)

    def body(*refs):
        src = refs[:n]
        refs = refs[n + 1:]
        out = refs[:n]
        sib = refs[n:2 * n]
        chip_sum = refs[2 * n:3 * n]
        slots = refs[3 * n:4 * n]
        pair_send, pair_recv, ici_send, ici_recv, swap_send, swap_recv = refs[4 * n:]
        x, y, c = _place()
        me_chip = 2 * x + y
        chips = _other_chips(x, y)
        pairs = [_remote(src[a], sib[a], pair_send.at[a], pair_recv.at[a], (x, y, 1 - c)) for a in range(n)]
        for rc in pairs:
            rc.start()
        for a in range(n):
            pairs[a].wait_recv()
            chip_sum[a][...] = src[a][...] + sib[a][...]
        for h in (0, 1):
            @pl.when(c == h)
            def _():
                sends = []
                for a in range(n):
                    idx = _half_slices(parts[a].shape, h)
                    for j, chip in enumerate(chips):
                        rc = _remote(chip_sum[a].at[idx], slots[a].at[me_chip].at[idx], ici_send.at[3 * a + j], ici_recv.at[3 * a + j], (*chip, h))
                        rc.start()
                        sends.append(rc)
                    slots[a][(me_chip,) + idx] = chip_sum[a][idx]
                for a in range(n):
                    idx = _half_slices(parts[a].shape, h)
                    for j, chip in enumerate(chips):
                        landed = slots[a].at[2 * chip[0] + chip[1]].at[idx]
                        _remote(landed, landed, ici_send.at[3 * a + j], ici_recv.at[3 * a + j], (x, y, c)).wait_recv()
                    total = slots[a][(0,) + idx]
                    for s in range(1, 4):
                        total = total + slots[a][(s,) + idx]
                    out[a][idx] = total
                    rc = _remote(out[a].at[idx], out[a].at[idx], swap_send.at[a], swap_recv.at[a], (x, y, 1 - h))
                    rc.start()
                    sends.append(rc)
                for a in range(n):
                    other = out[a].at[_half_slices(parts[a].shape, 1 - h)]
                    _remote(other, other, swap_send.at[a], swap_recv.at[a], (x, y, c)).wait_recv()
                for rc in sends:
                    rc.wait_send()
        for rc in pairs:
            rc.wait_send()

    return pl.pallas_call(
        body, name="allreduce_small", in_specs=[_VMEM] * n + [_ANY], out_specs=[_VMEM] * n,
        out_shape=[jax.ShapeDtypeStruct(p.shape, f32) for p in parts],
        scratch_shapes=[pltpu.VMEM(p.shape, f32) for p in parts] * 2 + [pltpu.VMEM((4,) + p.shape, f32) for p in parts]
        + [pltpu.SemaphoreType.DMA((n,)), pltpu.SemaphoreType.DMA((n,)), pltpu.SemaphoreType.DMA((3 * n,)),
           pltpu.SemaphoreType.DMA((3 * n,)), pltpu.SemaphoreType.DMA((n,)), pltpu.SemaphoreType.DMA((n,))],
        compiler_params=pltpu.CompilerParams(vmem_limit_bytes=_VMEM_LIMIT_BYTES),
    )(*parts, after)


def _local_step(x, mem, tgt, g_mix, g_xattn, g_mem, g_ffn, g_final, cb, lg, lb, pw, ps, fb, started, relay, weights, reduce,
                n_seq, seq, n_mem):
    t, d = x.shape
    f = fb.shape[1] // 2
    c = cb.shape[1]
    h1 = _rms_fwd("norm_mix", x, g_mix, after=started)
    relay(0, h1)
    w_in, cw, fw = weights(0, h1)
    u = _mm_nn("proj_in", h1, w_in, _ACT, w_in.shape[1])
    y, hc = _mix_fwd(u, cw, cb, lg, lb, pw, ps, seq)
    relay(1, y)
    w_out, w_q, w_kv, w_o = weights(1, y)
    x1, h2 = _proj_residual_norm("proj_out", y, w_out, x, g_xattn)
    q = _mm_nn("proj_q", h2, w_q, _ACT, d)
    mem_n = _rms_fwd("norm_mem", mem, g_mem)
    kv = _mm_nn("proj_kv", mem_n, w_kv, _ACT, 2 * d)
    o = _attn_fwd(q, kv, n_seq, seq, n_mem)
    x2, h3 = _proj_residual_norm("proj_o", o, w_o, x1, g_ffn, after=relay(2, o))
    w_up, w_down = weights(2, h3)
    up = _mm_nn("proj_up", h3, w_up, _ACT, f, split_out=True)
    a, uc = _ffn_gate_fwd(up, fw, fb, seq)
    dx3, dx3b, dg_final, loss = _proj_loss_bwd("proj_down", a, w_down, x2, g_final, tgt)
    da = _mm_nt("d_act", dx3b, w_down, _ACT)
    gw_down = _mm_tn_rows("dw_down", a, dx3b, f // 2, d // 2)
    dup, sums_g, sums_v = _ffn_gate_bwd(up, uc, da, fw, seq)
    gw_up = _mm_tn_pieces("dw_up", h3, dup, f // 2)
    token = reduce(0, [gw_down.reshape(8, -1, d), gw_up])
    dx2, dx2b, dg_ffn = _dproj_rms_bwd("d_h3", dup, w_up, x2, g_ffn, dx3, after=token)
    do = _mm_nt("d_o", dx2b, w_o, _ACT)
    gw_o = _mm_tn_rows("dw_o", o, dx2b, d, d // 2)
    dq, dkv = _attn_bwd(q, kv, do, n_seq, seq, n_mem)
    gw_q = _mm_tn_rows("dw_q", h2, dq, d, d // 2)
    gw_kv = _mm_tn_pieces("dw_kv", mem_n, dkv, d // 2)
    dmem_n = _mm_nt("d_mem_n", dkv, w_kv, f32)
    dg_mem = _rms_gain_grad("norm_mem_bwd", mem, dmem_n)
    dx1, dx1b, dg_xattn = _dproj_rms_bwd("d_h2", dq, w_q, x1, g_xattn, dx2)
    dy = _mm_nt("d_y", dx1b, w_out, _ACT)
    gw_out = _mm_tn_rows("dw_out", y, dx1b, d, d // 2)
    token = reduce(1, [gw_o.reshape(8, -1, d), gw_q.reshape(8, -1, d), gw_kv, gw_out.reshape(8, -1, d)])
    dhc, sums_norm = _mix_bwd_norm(hc, dy, lg, lb, token)
    du, d_cw, d_ps, d_pw = _mix_bwd_taps(u, dhc, dy, cw, pw, ps, seq)
    gw_in = _mm_tn_pieces("dw_in", h1, du, c * 3 // 4)
    token = reduce(2, [gw_in])
    grad_x, dg_mix = _dproj_rms_bwd("d_h1", du, w_in, x, g_mix, dx1, storage_copy=False, after=token)
    zero_row = jnp.zeros((1, d), f32)
    gains = jnp.concatenate([dg_mix, dg_xattn, dg_mem, dg_ffn, dg_final, jnp.pad(loss, ((0, 0), (0, d - 1))), zero_row, zero_row], axis=0)
    conv_rows = jnp.concatenate([sums_norm[2:3], sums_norm[0:1], sums_norm[1:2], d_ps[0:1], jnp.zeros((4, c), f32)], axis=0)
    ffn_rows = jnp.concatenate([sums_g, sums_v], axis=1)
    small = [gains, conv_rows, d_pw.reshape(-1, d_pw.shape[-1]), ffn_rows, d_cw]
    return grad_x, small


def kernel(x, mem, norm_mix_g, w_in, conv_dw_w, conv_dw_b, conv_ln_g, conv_ln_b, pool_w, pool_scale, w_out, norm_xattn_g, norm_mem_g, w_q, w_kv, w_o, norm_ffn_g, w_up, ffn_dw_w, ffn_dw_b, w_down, norm_final_g, loss_target, m_norm_mix_g, m_w_in, m_conv_dw_w, m_conv_dw_b, m_conv_ln_g, m_conv_ln_b, m_pool_w, m_pool_scale, m_w_out, m_norm_xattn_g, m_norm_mem_g, m_w_q, m_w_kv, m_w_o, m_norm_ffn_g, m_w_up, m_ffn_dw_w, m_ffn_dw_b, m_w_down, m_norm_final_g, v_norm_mix_g, v_w_in, v_conv_dw_w, v_conv_dw_b, v_conv_ln_g, v_conv_ln_b, v_pool_w, v_pool_scale, v_w_out, v_norm_xattn_g, v_norm_mem_g, v_w_q, v_w_kv, v_w_o, v_norm_ffn_g, v_w_up, v_ffn_dw_w, v_ffn_dw_b, v_w_down, v_norm_final_g):
    n_seq, seq, d = x.shape
    n_mem = mem.shape[1]
    chip = 2 * lax.axis_index("x") + lax.axis_index("y")

    place = jnp.stack([chip, lax.axis_index("c")]).astype(jnp.int32)

    col_w = [w_in, w_kv, w_up]
    row_w = [w_out, w_q, w_o, w_down]
    kw = conv_dw_w.shape[1]

    def padded_in_place(shard, rows):
        full = jnp.zeros((rows, 4 * shard.shape[1]), shard.dtype)
        return lax.dynamic_update_slice(full, shard, (0, chip * shard.shape[1]))

    first = list(_place_shards("place_w_in", place, [w_in[0]], [True]))
    first += [padded_in_place(conv_dw_w[0], _HALO), padded_in_place(ffn_dw_w[0], 8)]
    first, first_sems, token = _allgather_start("allgather_start_0", first, [True] * 3, [False, True, True], [[0, 1, 2]])
    rest = [w_kv, w_up, w_out, w_q, w_o, w_down]
    rest_flags = [True, True, False, False, False, False]
    rest = list(_place_shards("place_rest", place, [w[0] for w in rest], rest_flags, after=token))
    rest, rest_sems, all_started = _allgather_start("allgather_start_1", rest, rest_flags, [False] * 6, [[2, 3, 0, 4], [1, 5]])
    started = [(first, [True] * 3, [False, True, True], first_sems[0]),
               ([rest[i] for i in (2, 3, 0, 4)], [False, False, True, False], [False] * 4, rest_sems[0]),
               ([rest[i] for i in (1, 5)], [True, False], [False] * 2, rest_sems[1])]
    relayed = {}

    def relay(g, after):
        group_bufs, flags, wholes, group_sems = started[g]
        group_bufs, sibling_sems, relay_token = _allgather_relay("allgather_relay_%d" % g, group_bufs, flags, wholes, group_sems, after)
        relayed[g] = (group_bufs, sibling_sems)
        return relay_token

    def weights(g, after):
        group_bufs, sibling_sems = relayed[g]
        return _allgather_wait("allgather_wait_%d" % g, group_bufs, started[g][1], started[g][2], sibling_sems, after)

    names = ["w_in", "w_kv", "w_up", "w_out", "w_q", "w_o", "w_down"]
    reduce_groups = [["w_down", "w_up"], ["w_o", "w_q", "w_kv", "w_out"], ["w_in"]]
    in_flight = {}

    def reduce(g, grads):
        grads, lands, rs_sems, token = _grad_exchange_start("rs_start_%d" % g, grads)
        in_flight[g] = (grads, lands, rs_sems)
        return token

    grad_x, small = _local_step(
        x.reshape(n_seq * seq, d), mem.reshape(n_seq * n_mem, d), loss_target.reshape(n_seq * seq, d),
        norm_mix_g, norm_xattn_g, norm_mem_g, norm_ffn_g, norm_final_g.reshape(1, d),
        conv_dw_b, conv_ln_g, conv_ln_b, pool_w[0], pool_scale, ffn_dw_b, all_started, relay, weights, reduce,
        n_seq, seq, n_mem)

    landed = {}
    for g, members in enumerate(reduce_groups):
        grads, lands, rs_sems = in_flight[g]
        grads, lands = _grad_exchange_wait("rs_wait_%d" % g, grads, lands, rs_sems, grad_x)
        landed.update(zip(members, zip(grads, lands)))
    finals = _sum_partials("rs_sum", place, [landed[n][0] for n in names], [landed[n][1] for n in names])
    finals, swap_sems, token = _swap_halves_start(finals)

    gains, conv_rows, d_pw, ffn_rows, d_cw = _allreduce_small(small, token)
    loss = gains[5, 0]
    shard_grads = _swap_halves_wait(finals, swap_sems, gains)

    outs = {}
    big_w = dict(zip(names, col_w + row_w))
    big_m = dict(w_in=m_w_in, w_kv=m_w_kv, w_up=m_w_up, w_out=m_w_out, w_q=m_w_q, w_o=m_w_o, w_down=m_w_down)
    big_v = dict(w_in=v_w_in, w_kv=v_w_kv, w_up=v_w_up, w_out=v_w_out, w_q=v_w_q, w_o=v_w_o, w_down=v_w_down)
    big_quads = [(big_w[n], g.reshape(big_w[n].shape[1:]), big_m[n], big_v[n]) for n, g in zip(names, shard_grads)]
    outs.update(zip(names, _adamw_shards(big_quads)))

    f2 = ffn_dw_b.shape[1]
    cs_c = conv_dw_w.shape[2]
    cs_f = ffn_dw_w.shape[2]
    g_cw = lax.dynamic_slice(d_cw, (0, chip * cs_c), (kw, cs_c)).reshape(conv_dw_w.shape)
    g_fw = lax.dynamic_slice(ffn_rows, (1, chip * cs_f), (ffn_dw_w.shape[1], cs_f)).reshape(ffn_dw_w.shape)
    small_params = [
        ("norm_mix_g", norm_mix_g, gains[0:1], m_norm_mix_g, v_norm_mix_g),
        ("conv_dw_w", conv_dw_w, g_cw, m_conv_dw_w, v_conv_dw_w),
        ("conv_dw_b", conv_dw_b, conv_rows[0:1], m_conv_dw_b, v_conv_dw_b),
        ("conv_ln_g", conv_ln_g, conv_rows[1:2], m_conv_ln_g, v_conv_ln_g),
        ("conv_ln_b", conv_ln_b, conv_rows[2:3], m_conv_ln_b, v_conv_ln_b),
        ("pool_w", pool_w, d_pw.reshape(pool_w.shape), m_pool_w, v_pool_w),
        ("pool_scale", pool_scale, conv_rows[3:4], m_pool_scale, v_pool_scale),
        ("norm_xattn_g", norm_xattn_g, gains[1:2], m_norm_xattn_g, v_norm_xattn_g),
        ("norm_mem_g", norm_mem_g, gains[2:3], m_norm_mem_g, v_norm_mem_g),
        ("norm_ffn_g", norm_ffn_g, gains[3:4], m_norm_ffn_g, v_norm_ffn_g),
        ("ffn_dw_w", ffn_dw_w, g_fw, m_ffn_dw_w, v_ffn_dw_w),
        ("ffn_dw_b", ffn_dw_b, ffn_rows[0:1, :f2], m_ffn_dw_b, v_ffn_dw_b),
        ("norm_final_g", norm_final_g.reshape(1, d), gains[4:5], m_norm_final_g.reshape(1, d), v_norm_final_g.reshape(1, d)),
    ]
    quads = []
    for _, w, g, m, v in small_params:
        shape2 = (-1, w.shape[-1])
        quads.append((w.reshape(shape2), g.reshape(shape2), m.reshape(shape2), v.reshape(shape2)))
    for (n, w, g, _, _), (delta, new_m, new_v) in zip(small_params, _adamw_small(quads)):
        shape = norm_final_g.shape if n == "norm_final_g" else w.shape
        outs[n] = (g.reshape(shape), delta.reshape(shape), new_m.reshape(shape), new_v.reshape(shape))

    order = ["norm_mix_g", "w_in", "conv_dw_w", "conv_dw_b", "conv_ln_g", "conv_ln_b", "pool_w", "pool_scale", "w_out",
             "norm_xattn_g", "norm_mem_g", "w_q", "w_kv", "w_o", "norm_ffn_g", "w_up", "ffn_dw_w", "ffn_dw_b", "w_down",
             "norm_final_g"]
    return (loss, grad_x.reshape(x.shape), *[outs[n][0] for n in order], *[outs[n][1] for n in order],
            *[outs[n][2] for n in order], *[outs[n][3] for n in order])
```

```python
import jax
import jax.numpy as jnp
from jax import lax
from jax.experimental import pallas as pl
from jax.experimental.pallas import tpu as pltpu

f32 = jnp.float32
_ACT = jnp.bfloat16

EPS = 1e-6
POOL_WINDOWS = (2, 4, 8, 16)
XATTN_HEADS = 4
ADAM_LR = 0.001
ADAM_B1 = 0.9
ADAM_B2 = 0.999
ADAM_EPS = 1e-08
ADAM_WD = 0.01
ADAM_STEP = 10

_VMEM_LIMIT_BYTES = 56 * 1024 * 1024
_MESH = pl.DeviceIdType.MESH
_ANY = pl.BlockSpec(memory_space=pl.ANY)
_VMEM = pl.BlockSpec(memory_space=pltpu.VMEM)
_HBM = pl.BlockSpec(memory_space=pltpu.HBM)
_SEM = pl.BlockSpec(memory_space=pltpu.SEMAPHORE)
_EFFECT = pltpu.SideEffectType.DATAFLOW_SIDE_EFFECTING

_NN = (((1,), (0,)), ((), ()))
_NT = (((1,), (1,)), ((), ()))
_TN = (((0,), (0,)), ((), ()))


def _params(n_grid):
    return pltpu.CompilerParams(dimension_semantics=("arbitrary",) * n_grid, vmem_limit_bytes=_VMEM_LIMIT_BYTES)


def _sigmoid(v):
    return 1.0 / (1.0 + jnp.exp(-v))


def _dot(a, b, dims):
    return lax.dot_general(a, b, dims, preferred_element_type=f32)


def _mm(name, a, b, *, dims, grid, a_spec, b_spec, o_spec, out_shape):
    def body(a_ref, b_ref, o_ref):
        o_ref[...] = _dot(a_ref[...], b_ref[...], dims).astype(o_ref.dtype)

    return pl.pallas_call(
        body, name=name, grid=grid, in_specs=[a_spec, b_spec], out_specs=o_spec, out_shape=out_shape,
        compiler_params=_params(len(grid)),
    )(a, b)


_NARROW = 2816


def _row_tile(m, width=_NARROW + 1):
    return min(1024 if width <= _NARROW else 512, m)


def _mm_nn(name, a, b, out_dtype, tn, split_out=False):
    m, k = a.shape
    n = b.shape[1]
    tm = _row_tile(m, max(k, tn))
    if split_out:
        out_shape = jax.ShapeDtypeStruct((n // tn, m, tn), out_dtype)
        o_spec = pl.BlockSpec((None, tm, tn), lambda j, i: (j, i, 0))
    else:
        out_shape = jax.ShapeDtypeStruct((m, n), out_dtype)
        o_spec = pl.BlockSpec((tm, tn), lambda j, i: (i, j))
    return _mm(
        name, a, b, dims=_NN, grid=(n // tn, m // tm),
        a_spec=pl.BlockSpec((tm, k), lambda j, i: (i, 0)), b_spec=pl.BlockSpec((k, tn), lambda j, i: (0, j)),
        o_spec=o_spec, out_shape=out_shape,
    )


def _mm_nt(name, a, b, out_dtype):
    n, kc = b.shape
    m = a.shape[0]
    tm = _row_tile(m, max(n, kc))
    return _mm(
        name, a, b, dims=_NT, grid=(m // tm,),
        a_spec=pl.BlockSpec((tm, kc), lambda i: (i, 0)),
        b_spec=pl.BlockSpec((n, kc), lambda i: (0, 0), pipeline_mode=pl.Buffered(1)),
        o_spec=pl.BlockSpec((tm, n), lambda i: (i, 0)),
        out_shape=jax.ShapeDtypeStruct((m, n), out_dtype),
    )


def _mm_tn_rows(name, a, b, tka, tn):
    m, ka = a.shape
    nb = b.shape[1]
    return _mm(
        name, a, b, dims=_TN, grid=(ka // tka, nb // tn),
        a_spec=pl.BlockSpec((m, tka), lambda i, j: (0, i)), b_spec=pl.BlockSpec((m, tn), lambda i, j: (0, j)),
        o_spec=pl.BlockSpec((tka, tn), lambda i, j: (i, j)),
        out_shape=jax.ShapeDtypeStruct((ka, nb), _ACT),
    )


def _mm_tn_pieces(name, a, b, cs):
    m, ka = a.shape
    if b.ndim == 3:
        b_spec = pl.BlockSpec((None, m, cs), lambda i, j: (j // 2, 0, j % 2))
    else:
        b_spec = pl.BlockSpec((m, cs), lambda i, j: (0, j))
    return _mm(
        name, a, b, dims=_TN, grid=(2, 4),
        a_spec=pl.BlockSpec((m, ka // 2), lambda i, j: (0, i)), b_spec=b_spec,
        o_spec=pl.BlockSpec((None, ka // 2, cs), lambda i, j: (2 * j + i, 0, 0)),
        out_shape=jax.ShapeDtypeStruct((8, ka // 2, cs), _ACT),
    )


def _after(after):
    return ([], []) if after is None else ([after], [_ANY])


def _rms_fwd(name, x, g, after=None):
    t, d = x.shape
    tm = _row_tile(t, d)
    more, more_specs = _after(after)

    def body(x_ref, g_ref, *refs):
        h_ref = refs[-1]
        xv = x_ref[...]
        r = lax.rsqrt(jnp.mean(xv * xv, axis=-1, keepdims=True) + EPS)
        h_ref[...] = (xv * r * g_ref[...]).astype(h_ref.dtype)

    return pl.pallas_call(
        body, name=name, grid=(t // tm,),
        in_specs=[pl.BlockSpec((tm, d), lambda i: (i, 0)), pl.BlockSpec((1, d), lambda i: (0, 0))] + more_specs,
        out_specs=pl.BlockSpec((tm, d), lambda i: (i, 0)), out_shape=jax.ShapeDtypeStruct((t, d), _ACT),
        compiler_params=_params(1),
    )(x, g, *more)


def _fused_rows(name, a, b, product, a_spec, tm, extras, extra_specs, out_shape, out_specs, epilogue):
    ne = len(extras)

    def body(a_ref, b_ref, *refs):
        epilogue(product(a_ref, b_ref), refs[:ne], refs[ne:])

    m = extras[0].shape[0]
    return pl.pallas_call(
        body, name=name, grid=(m // tm,),
        in_specs=[a_spec, pl.BlockSpec(b.shape, lambda i: (0, 0), pipeline_mode=pl.Buffered(1)), *extra_specs],
        out_specs=out_specs, out_shape=out_shape, compiler_params=_params(1),
    )(a, b, *extras)


def _proj_residual_norm(name, a, b, res, g, after=None):
    m, k = a.shape
    d = b.shape[1]
    tm = _row_tile(m, max(k, d))

    def epilogue(p, ins, outs):
        xv = p + ins[0][...]
        outs[0][...] = xv
        r = lax.rsqrt(jnp.mean(xv * xv, axis=-1, keepdims=True) + EPS)
        outs[1][...] = (xv * r * ins[1][...]).astype(outs[1].dtype)

    row = pl.BlockSpec((tm, d), lambda i: (i, 0))
    return _fused_rows(
        name, a, b, lambda a_ref, b_ref: _dot(a_ref[...], b_ref[...], _NN), pl.BlockSpec((tm, k), lambda i: (i, 0)), tm,
        [res, g] + _after(after)[0], [row, pl.BlockSpec((1, d), lambda i: (0, 0))] + _after(after)[1],
        [jax.ShapeDtypeStruct((m, d), f32), jax.ShapeDtypeStruct((m, d), _ACT)], [row, row], epilogue)


def _dproj_rms_bwd(name, a, b, x, g, dres, storage_copy=True, after=None):
    m, d = x.shape
    if a.ndim == 3:
        nh, _, kh = a.shape
        tm = _row_tile(m, nh * kh)
        a_spec = pl.BlockSpec((nh, tm, kh), lambda i: (0, i, 0))

        def product(a_ref, b_ref):
            p = _dot(a_ref[0], b_ref[:, 0:kh], _NT)
            for h in range(1, nh):
                p = p + _dot(a_ref[h], b_ref[:, h * kh:(h + 1) * kh], _NT)
            return p
    else:
        tm = _row_tile(m, max(a.shape[1], d))
        a_spec = pl.BlockSpec((tm, a.shape[1]), lambda i: (i, 0))

        def product(a_ref, b_ref):
            return _dot(a_ref[...], b_ref[...], _NT)

    def epilogue(dhv, ins, outs):
        x_ref, g_ref, dres_ref = ins[:3]
        dg_ref = outs[-1]

        @pl.when(pl.program_id(0) == 0)
        def _():
            dg_ref[...] = jnp.zeros_like(dg_ref)

        xv = x_ref[...]
        r = lax.rsqrt(jnp.mean(xv * xv, axis=-1, keepdims=True) + EPS)
        xn = xv * r
        dxn = dhv * g_ref[...]
        dx = r * (dxn - xn * jnp.mean(dxn * xn, axis=-1, keepdims=True)) + dres_ref[...]
        outs[0][...] = dx
        if storage_copy:
            outs[1][...] = dx.astype(outs[1].dtype)
        dg_ref[...] += jnp.sum(dhv * xn, axis=0, keepdims=True)

    row = pl.BlockSpec((tm, d), lambda i: (i, 0))
    vec = pl.BlockSpec((1, d), lambda i: (0, 0))
    copies = [jax.ShapeDtypeStruct((m, d), _ACT)] if storage_copy else []
    return _fused_rows(
        name, a, b, product, a_spec, tm, [x, g, dres] + _after(after)[0], [row, vec, row] + _after(after)[1],
        [jax.ShapeDtypeStruct((m, d), f32)] + copies + [jax.ShapeDtypeStruct((1, d), f32)],
        [row] * (1 + len(copies)) + [vec], epilogue)


def _proj_loss_bwd(name, a, b, res, g, tgt):
    m, k = a.shape
    d = b.shape[1]
    tm = _row_tile(m, max(k, d))

    def epilogue(p, ins, outs):
        res_ref, g_ref, t_ref = ins
        dx_ref, dxb_ref, dg_ref, loss_ref = outs

        @pl.when(pl.program_id(0) == 0)
        def _():
            dg_ref[...] = jnp.zeros_like(dg_ref)
            loss_ref[...] = jnp.zeros_like(loss_ref)

        xv = p + res_ref[...]
        gv = g_ref[...]
        r = lax.rsqrt(jnp.mean(xv * xv, axis=-1, keepdims=True) + EPS)
        xn = xv * r
        err = xn * gv - t_ref[...]
        loss_ref[...] += 0.5 * jnp.sum(jnp.mean(err * err, axis=-1, keepdims=True), axis=0, keepdims=True)
        dout = err * (1.0 / d)
        dxn = dout * gv
        dx = r * (dxn - xn * jnp.mean(dxn * xn, axis=-1, keepdims=True))
        dx_ref[...] = dx
        dxb_ref[...] = dx.astype(dxb_ref.dtype)
        dg_ref[...] += jnp.sum(dout * xn, axis=0, keepdims=True)

    row = pl.BlockSpec((tm, d), lambda i: (i, 0))
    vec = pl.BlockSpec((1, d), lambda i: (0, 0))
    return _fused_rows(
        name, a, b, lambda a_ref, b_ref: _dot(a_ref[...], b_ref[...], _NN), pl.BlockSpec((tm, k), lambda i: (i, 0)), tm,
        [res, g, tgt], [row, vec, row],
        [jax.ShapeDtypeStruct((m, d), f32), jax.ShapeDtypeStruct((m, d), _ACT), jax.ShapeDtypeStruct((1, d), f32),
         jax.ShapeDtypeStruct((1, 1), f32)],
        [row, row, vec, pl.BlockSpec((1, 1), lambda i: (0, 0))], epilogue)


def _rms_gain_grad(name, x, dh):
    t, d = x.shape
    tm = _row_tile(t)

    def body(x_ref, dh_ref, dg_ref):
        @pl.when(pl.program_id(0) == 0)
        def _():
            dg_ref[...] = jnp.zeros_like(dg_ref)

        xv = x_ref[...]
        r = lax.rsqrt(jnp.mean(xv * xv, axis=-1, keepdims=True) + EPS)
        dg_ref[...] += jnp.sum(dh_ref[...] * (xv * r), axis=0, keepdims=True)

    row = pl.BlockSpec((tm, d), lambda i: (i, 0))
    return pl.pallas_call(
        body, name=name, grid=(t // tm,), in_specs=[row, row], out_specs=pl.BlockSpec((1, d), lambda i: (0, 0)),
        out_shape=jax.ShapeDtypeStruct((1, d), f32), compiler_params=_params(1),
    )(x, dh)


_CONV_ROWS = 512
_CHUNK = 64
_HALO = 32


def _pool_counts(pos, w):
    return jnp.minimum(pos + 1.0, float(w))


def _rows_from(win, start, rows):
    if start % 8 == 0:
        return win[start:start + rows, :]
    n = win.shape[0]
    return pltpu.roll(win, n - start % 8, axis=0)[start - start % 8:start - start % 8 + rows, :]


def _tap_rows(buf, starts, rows):
    for residue in range(8):
        group = [(k, s) for k, s in starts.items() if s % 8 == residue]
        if group:
            lo = min(s for _, s in group) - residue
            hi = max(s for _, s in group) - residue + rows + (8 if residue else 0)
            win = buf[lo:hi, :]
            if residue:
                win = pltpu.roll(win, hi - lo - residue, axis=0)
            for k, s in group:
                yield k, win[s - residue - lo:s - residue - lo + rows, :]


def _mix_fwd(u, cw, cb, lg, lb, pw, ps, seq):
    t, c3 = u.shape
    c = c3 // 3
    kw = 31
    tm = min(_CONV_ROWS, seq)
    tps = seq // tm
    gd = c // len(POOL_WINDOWS)

    def body(u_ref, uh_ref, cw_ref, cb_ref, lg_ref, lb_ref, pw_ref, ps_ref, y_ref, hc_ref, hgbuf, pbuf):
        i = pl.program_id(0)
        keep = jnp.where(i % tps == 0, 0.0, 1.0)
        uh = uh_ref[...].astype(f32) * keep
        hgbuf[0:_HALO, :] = uh[:, 0:c] * _sigmoid(uh[:, c:2 * c])
        pbuf[0:_HALO, :] = uh[:, 2 * c:]
        for r0 in range(0, tm, _CHUNK):
            um = u_ref[r0:r0 + _CHUNK, :].astype(f32)
            hgbuf[_HALO + r0:_HALO + r0 + _CHUNK, :] = um[:, 0:c] * _sigmoid(um[:, c:2 * c])
            pbuf[_HALO + r0:_HALO + r0 + _CHUNK, :] = um[:, 2 * c:]
        for r0 in range(0, tm, _CHUNK):
            acc = jnp.broadcast_to(cb_ref[...], (_CHUNK, c))
            for k, rows in _tap_rows(hgbuf, {k: r0 + _HALO - (kw - 1) + k for k in range(kw)}, _CHUNK):
                acc = acc + cw_ref[k:k + 1, :] * rows
            hc_ref[r0:r0 + _CHUNK, :] = acc
            mu = jnp.mean(acc, axis=-1, keepdims=True)
            xc = acc - mu
            var = jnp.mean(xc * xc, axis=-1, keepdims=True)
            hl = xc * lax.rsqrt(var + EPS) * lg_ref[...] + lb_ref[...]
            y_ref[r0:r0 + _CHUNK, 0:c] = (hl * _sigmoid(hl)).astype(y_ref.dtype)
        pos = ((i % tps) * tm).astype(f32) + lax.broadcasted_iota(jnp.int32, (tm, 1), 0).astype(f32)
        for gi, w in enumerate(POOL_WINDOWS):
            sl = slice(gi * gd, (gi + 1) * gd)
            v = pbuf[_HALO:_HALO + tm, sl]
            s = v
            for j in range(1, w):
                s = s + pbuf[_HALO - j:_HALO - j + tm, sl]
            pooled = s / _pool_counts(pos, w) - v
            mixed = _dot(pooled.astype(_ACT), pw_ref[gi].astype(_ACT), _NN)
            y_ref[:, c + gi * gd:c + (gi + 1) * gd] = (mixed * ps_ref[:, sl]).astype(y_ref.dtype)

    hb = tm // _HALO
    full = lambda shape: pl.BlockSpec(shape, lambda i: (0,) * len(shape))
    return pl.pallas_call(
        body, name="mix_fwd", grid=(t // tm,),
        in_specs=[pl.BlockSpec((tm, c3), lambda i: (i, 0)),
                  pl.BlockSpec((_HALO, c3), lambda i: (jnp.maximum(i * hb - 1, 0), 0)),
                  full((_HALO, c)), full((1, c)), full((1, c)), full((1, c)), full((len(POOL_WINDOWS), gd, gd)), full((1, c))],
        out_specs=[pl.BlockSpec((tm, 2 * c), lambda i: (i, 0)), pl.BlockSpec((tm, c), lambda i: (i, 0))],
        out_shape=[jax.ShapeDtypeStruct((t, 2 * c), _ACT), jax.ShapeDtypeStruct((t, c), f32)],
        scratch_shapes=[pltpu.VMEM((_HALO + tm, c), f32), pltpu.VMEM((_HALO + tm, c), f32)],
        compiler_params=_params(1),
    )(u, u, cw, cb, lg, lb, pw, ps)


def _mix_bwd_norm(hc, dy, lg, lb, after):
    t, c = hc.shape
    tm = _row_tile(t, c)

    def body(hc_ref, dy_ref, lg_ref, lb_ref, after_ref, dhc_ref, sums_ref):
        @pl.when(pl.program_id(0) == 0)
        def _():
            sums_ref[...] = jnp.zeros_like(sums_ref)

        hcv = hc_ref[...]
        mu = jnp.mean(hcv, axis=-1, keepdims=True)
        xc = hcv - mu
        rstd = lax.rsqrt(jnp.mean(xc * xc, axis=-1, keepdims=True) + EPS)
        n = xc * rstd
        hl = n * lg_ref[...] + lb_ref[...]
        sg = _sigmoid(hl)
        dhl = dy_ref[...].astype(f32) * (sg * (1.0 + hl * (1.0 - sg)))
        dn = dhl * lg_ref[...]
        dhc = rstd * (dn - jnp.mean(dn, axis=-1, keepdims=True) - n * jnp.mean(dn * n, axis=-1, keepdims=True))
        dhc_ref[...] = dhc
        sums_ref[0:1, :] += jnp.sum(dhl * n, axis=0, keepdims=True)
        sums_ref[1:2, :] += jnp.sum(dhl, axis=0, keepdims=True)
        sums_ref[2:3, :] += jnp.sum(dhc, axis=0, keepdims=True)

    row = pl.BlockSpec((tm, c), lambda i: (i, 0))
    vec = pl.BlockSpec((1, c), lambda i: (0, 0))
    return pl.pallas_call(
        body, name="mix_bwd_norm", grid=(t // tm,), in_specs=[row, row, vec, vec, _ANY],
        out_specs=[row, pl.BlockSpec((8, c), lambda i: (0, 0))],
        out_shape=[jax.ShapeDtypeStruct((t, c), f32), jax.ShapeDtypeStruct((8, c), f32)],
        compiler_params=_params(1),
    )(hc, dy, lg, lb, after)


def _mix_bwd_taps(u, dhc, dy, cw, pw, ps, seq):
    t, c3 = u.shape
    c = c3 // 3
    kw = 31
    tm = min(_CONV_ROWS, seq)
    tps = seq // tm
    ng = len(POOL_WINDOWS)
    gd = c // ng
    nh = 16

    def body(u_ref, uh_ref, dhc_ref, dhcn_ref, dy_ref, dyn_ref, cw_ref, pw_ref, ps_ref,
             du_ref, dcw_ref, dps_ref, dpw_ref, hgbuf, dcbuf, pbuf, dpbuf):
        i = pl.program_id(0)
        keep_prev = jnp.where(i % tps == 0, 0.0, 1.0)
        keep_next = jnp.where(i % tps == tps - 1, 0.0, 1.0)

        @pl.when(i == 0)
        def _():
            dcw_ref[...] = jnp.zeros_like(dcw_ref)
            dps_ref[...] = jnp.zeros_like(dps_ref)
            dpw_ref[...] = jnp.zeros_like(dpw_ref)

        uh = uh_ref[...].astype(f32) * keep_prev
        hgbuf[0:_HALO, :] = uh[:, 0:c] * _sigmoid(uh[:, c:2 * c])
        pbuf[0:_HALO, :] = uh[:, 2 * c:]
        for r0 in range(0, tm, _CHUNK):
            um = u_ref[r0:r0 + _CHUNK, :].astype(f32)
            hgbuf[_HALO + r0:_HALO + r0 + _CHUNK, :] = um[:, 0:c] * _sigmoid(um[:, c:2 * c])
            pbuf[_HALO + r0:_HALO + r0 + _CHUNK, :] = um[:, 2 * c:]
        dcbuf[0:tm, :] = dhc_ref[...]
        dcbuf[tm:tm + _HALO, :] = dhcn_ref[...] * keep_next
        tap_sums = [None] * kw
        for r0 in range(0, tm, _CHUNK):
            dh = dcbuf[r0:r0 + _CHUNK, :]
            acc = jnp.zeros((_CHUNK, c), f32)
            for k, rows in _tap_rows(hgbuf, {k: r0 + _HALO - (kw - 1) + k for k in range(kw)}, _CHUNK):
                part = (dh * rows).reshape(_CHUNK // 8, 8, c).sum(axis=0)
                tap_sums[k] = part if tap_sums[k] is None else tap_sums[k] + part
            for k, rows in _tap_rows(dcbuf, {k: r0 + (kw - 1) - k for k in range(kw)}, _CHUNK):
                acc = acc + cw_ref[k:k + 1, :] * rows
            val = u_ref[r0:r0 + _CHUNK, 0:c].astype(f32)
            sg = _sigmoid(u_ref[r0:r0 + _CHUNK, c:2 * c].astype(f32))
            du_ref[r0:r0 + _CHUNK, 0:c] = (acc * sg).astype(du_ref.dtype)
            du_ref[r0:r0 + _CHUNK, c:2 * c] = (acc * val * sg * (1.0 - sg)).astype(du_ref.dtype)
        for k in range(kw):
            dcw_ref[k:k + 1, :] += jnp.sum(tap_sums[k], axis=0, keepdims=True)
        base = ((i % tps) * tm).astype(f32)
        pos = base + lax.broadcasted_iota(jnp.int32, (tm, 1), 0).astype(f32)
        pos_next = base + float(tm) + lax.broadcasted_iota(jnp.int32, (nh, 1), 0).astype(f32)
        for gi, w in enumerate(POOL_WINDOWS):
            sl = slice(gi * gd, (gi + 1) * gd)
            v = pbuf[_HALO:_HALO + tm, sl]
            s = v
            for j in range(1, w):
                s = s + pbuf[_HALO - j:_HALO - j + tm, sl]
            cnt = _pool_counts(pos, w)
            pooled = (s / cnt - v).astype(_ACT)
            pwg = pw_ref[gi].astype(_ACT)
            mixed = _dot(pooled, pwg, _NN)
            dyp = dy_ref[:, sl].astype(f32)
            dps_ref[0:1, sl] += jnp.sum(dyp * mixed, axis=0, keepdims=True)
            dmix = (dyp * ps_ref[:, sl]).astype(_ACT)
            dpw_ref[gi] += _dot(pooled, dmix, _TN)
            dmix_next = (dyn_ref[:, sl].astype(f32) * ps_ref[:, sl] * keep_next).astype(_ACT)
            dpool = _dot(dmix, pwg, _NT)
            dpbuf[0:tm, sl] = dpool / cnt
            dpbuf[tm:tm + nh, sl] = _dot(dmix_next, pwg, _NT) / _pool_counts(pos_next, w)
            acc = -dpool
            for j in range(w):
                acc = acc + dpbuf[j:j + tm, sl]
            du_ref[:, 2 * c + gi * gd:2 * c + (gi + 1) * gd] = acc.astype(du_ref.dtype)

    hb = tm // _HALO
    n_halo = t // _HALO
    n_nh = t // nh
    full = lambda shape: pl.BlockSpec(shape, lambda i: (0,) * len(shape))
    return pl.pallas_call(
        body, name="mix_bwd_taps", grid=(t // tm,),
        in_specs=[pl.BlockSpec((tm, c3), lambda i: (i, 0)),
                  pl.BlockSpec((_HALO, c3), lambda i: (jnp.maximum(i * hb - 1, 0), 0)),
                  pl.BlockSpec((tm, c), lambda i: (i, 0)),
                  pl.BlockSpec((_HALO, c), lambda i: (jnp.minimum((i + 1) * hb, n_halo - 1), 0)),
                  pl.BlockSpec((tm, c), lambda i: (i, 1)),
                  pl.BlockSpec((nh, c), lambda i: (jnp.minimum((i + 1) * (tm // nh), n_nh - 1), 1)),
                  full((_HALO, c)), full((ng, gd, gd)), full((1, c))],
        out_specs=[pl.BlockSpec((tm, c3), lambda i: (i, 0)), full((_HALO, c)), full((8, c)), full((ng, gd, gd))],
        out_shape=[jax.ShapeDtypeStruct((t, c3), _ACT), jax.ShapeDtypeStruct((_HALO, c), f32),
                   jax.ShapeDtypeStruct((8, c), f32), jax.ShapeDtypeStruct((ng, gd, gd), f32)],
        scratch_shapes=[pltpu.VMEM((_HALO + tm, c), f32), pltpu.VMEM((tm + _HALO, c), f32),
                        pltpu.VMEM((_HALO + tm, c), f32), pltpu.VMEM((tm + nh, c), f32)],
        compiler_params=_params(1),
    )(u, u, dhc, dhc, dy, dy, cw, pw, ps)


def _attn_fwd(q, kv, n_seq, seq, n_mem):
    t, d = q.shape
    dh = d // XATTN_HEADS
    tq = min(1024, seq)
    nq = seq // tq
    scale = dh ** -0.5

    def body(q_ref, kv_ref, o_ref):
        for h in range(XATTN_HEADS):
            cols = slice(h * dh, (h + 1) * dh)
            s = _dot(q_ref[:, cols], kv_ref[:, cols], _NT) * scale
            e = jnp.exp(s - jnp.max(s, axis=-1, keepdims=True))
            p = e / jnp.sum(e, axis=-1, keepdims=True)
            o_ref[:, cols] = _dot(p.astype(_ACT), kv_ref[:, d + h * dh:d + (h + 1) * dh], _NN).astype(o_ref.dtype)

    qs = pl.BlockSpec((tq, d), lambda b, i: (b * nq + i, 0))
    return pl.pallas_call(
        body, name="attn_fwd", grid=(n_seq, nq), in_specs=[qs, pl.BlockSpec((n_mem, 2 * d), lambda b, i: (b, 0))],
        out_specs=qs, out_shape=jax.ShapeDtypeStruct((t, d), _ACT), compiler_params=_params(2),
    )(q, kv)


def _attn_bwd(q, kv, do, n_seq, seq, n_mem):
    t, d = q.shape
    dh = d // XATTN_HEADS
    tq = min(1024, seq)
    nq = seq // tq
    scale = dh ** -0.5

    def body(q_ref, kv_ref, do_ref, dq_ref, dkv_ref, acc):
        i = pl.program_id(1)

        @pl.when(i == 0)
        def _():
            acc[...] = jnp.zeros_like(acc)

        for h in range(XATTN_HEADS):
            cols = slice(h * dh, (h + 1) * dh)
            vcols = slice(d + h * dh, d + (h + 1) * dh)
            qv = q_ref[:, cols]
            kh = kv_ref[:, cols]
            dov = do_ref[:, cols]
            s = _dot(qv, kh, _NT) * scale
            e = jnp.exp(s - jnp.max(s, axis=-1, keepdims=True))
            p = e / jnp.sum(e, axis=-1, keepdims=True)
            dp = _dot(dov, kv_ref[:, vcols], _NT)
            ds = (p * (dp - jnp.sum(dp * p, axis=-1, keepdims=True)) * scale).astype(_ACT)
            dq_ref[:, cols] = _dot(ds, kh, _NN).astype(dq_ref.dtype)
            acc[:, cols] += _dot(ds, qv, _TN)
            acc[:, vcols] += _dot(p.astype(_ACT), dov, _TN)

        @pl.when(i == nq - 1)
        def _():
            dkv_ref[...] = acc[...].astype(dkv_ref.dtype)

    qs = pl.BlockSpec((tq, d), lambda b, i: (b * nq + i, 0))
    ms = pl.BlockSpec((n_mem, 2 * d), lambda b, i: (b, 0))
    return pl.pallas_call(
        body, name="attn_bwd", grid=(n_seq, nq), in_specs=[qs, ms, qs], out_specs=[qs, ms],
        out_shape=[jax.ShapeDtypeStruct((t, d), _ACT), jax.ShapeDtypeStruct((n_seq * n_mem, 2 * d), _ACT)],
        scratch_shapes=[pltpu.VMEM((n_mem, 2 * d), f32)], compiler_params=_params(2),
    )(q, kv, do)


_FFN_ROWS = 2048
_FFN_COLS = 256
_FFN_HALO = 16


def _window(buf, g, start, rows):
    return buf[g, pl.ds(start, rows + 8), :]


def _taps3(win, rows):
    return [_rows_from(win, 6 + k, rows) for k in range(3)]


def _conv3(b_ref, w_ref, taps):
    acc = b_ref[...] + w_ref[0:1, :] * taps[0]
    for k in (1, 2):
        acc = acc + w_ref[k:k + 1, :] * taps[k]
    return acc


def _ffn_gate_fwd(up, fw, fb, seq):
    _, t, f = up.shape
    tm = min(_FFN_ROWS, seq)
    tps = seq // tm
    tc = _FFN_COLS
    nc = f // tc
    hl = _FFN_HALO

    def body(up_ref, uph_ref, wg_ref, wv_ref, bg_ref, bv_ref, a_ref, uc_ref):
        i = pl.program_id(1)
        before = uph_ref[...]
        before = jnp.where(i % tps == 0, jnp.zeros_like(before), before)

        def chunk(r0, wins):
            conv = []
            for g, (w_ref, b_ref) in enumerate(((wg_ref, bg_ref), (wv_ref, bv_ref))):
                conv.append(_conv3(b_ref, w_ref, _taps3(wins[g].astype(f32)[hl - 8:, :], _CHUNK)))
                uc_ref[g, pl.ds(r0, _CHUNK), :] = conv[g].astype(uc_ref.dtype)
            gate, val = conv
            a_ref[pl.ds(r0, _CHUNK), :] = (gate * _sigmoid(gate) * val).astype(a_ref.dtype)

        chunk(0, [jnp.concatenate([before[g], up_ref[g, 0:_CHUNK, :]], axis=0) for g in range(2)])

        def later(ci, carry):
            r0 = pl.multiple_of(ci * _CHUNK, _CHUNK)
            chunk(r0, [up_ref[g, pl.ds(r0 - hl, _CHUNK + hl), :] for g in range(2)])
            return carry

        lax.fori_loop(1, tm // _CHUNK, later, 0)

    hb = tm // hl
    return pl.pallas_call(
        body, name="ffn_gate_fwd", grid=(nc, t // tm),
        in_specs=[pl.BlockSpec((2, tm, tc), lambda j, i: (0, i, j)),
                  pl.BlockSpec((2, hl, tc), lambda j, i: (0, jnp.maximum(i * hb - 1, 0), j)),
                  pl.BlockSpec((8, tc), lambda j, i: (0, j)), pl.BlockSpec((8, tc), lambda j, i: (0, nc + j)),
                  pl.BlockSpec((1, tc), lambda j, i: (0, j)), pl.BlockSpec((1, tc), lambda j, i: (0, nc + j))],
        out_specs=[pl.BlockSpec((tm, tc), lambda j, i: (i, j)), pl.BlockSpec((2, tm, tc), lambda j, i: (0, i, j))],
        out_shape=[jax.ShapeDtypeStruct((t, f), _ACT), jax.ShapeDtypeStruct((2, t, f), _ACT)], compiler_params=_params(2),
    )(up, up, fw, fw, fb, fb)


def _ffn_gate_bwd(up, uc, da, fw, seq):
    _, t, f = up.shape
    tm = min(_FFN_ROWS, seq)
    tps = seq // tm
    tc = _FFN_COLS
    nc = f // tc
    hl = _FFN_HALO

    def body(up_ref, uc_ref, ucn_ref, da_ref, dan_ref, wg_ref, wv_ref, dup_ref, sg_ref, sv_ref, dbuf, sums):
        i = pl.program_id(1)
        at_end = i % tps == tps - 1

        @pl.when(i == 0)
        def _():
            sg_ref[...] = jnp.zeros_like(sg_ref)
            sv_ref[...] = jnp.zeros_like(sv_ref)

        sums[...] = jnp.zeros_like(sums)
        w_refs = (wg_ref, wv_ref)

        def grads(r0, rows, conv, dav):
            gate, val = [v.astype(f32) for v in conv]
            sg = _sigmoid(gate)
            douts = (dav * val * (sg * (1.0 + gate * (1.0 - sg))), dav * (gate * sg))
            for g in range(2):
                dbuf[g, pl.ds(r0, rows), :] = douts[g]
            return douts

        def first(ci, carry):
            r0 = pl.multiple_of(ci * _CHUNK, _CHUNK)
            douts = grads(r0, _CHUNK, [uc_ref[g, pl.ds(r0, _CHUNK), :] for g in range(2)],
                          da_ref[pl.ds(r0, _CHUNK), :].astype(f32))
            for g in range(2):
                sums[g, 0] += douts[g].reshape(_CHUNK // 8, 8, tc).sum(axis=0)
            return carry

        lax.fori_loop(0, tm // _CHUNK, first, 0)
        da_after = dan_ref[...].astype(f32)
        grads(tm, hl, [ucn_ref[g] for g in range(2)], jnp.where(at_end, jnp.zeros_like(da_after), da_after))

        def second(ci, carry):
            r0 = pl.multiple_of(ci * _CHUNK, _CHUNK)
            for g in range(2):
                win = _window(dbuf, g, r0, _CHUNK)
                upv = up_ref[g, pl.ds(r0, _CHUNK), :].astype(f32)
                acc = jnp.zeros((_CHUNK, tc), f32)
                for k in range(3):
                    shifted = _rows_from(win, 2 - k, _CHUNK)
                    acc = acc + w_refs[g][k:k + 1, :] * shifted
                    sums[g, 1 + k] += (shifted * upv).reshape(_CHUNK // 8, 8, tc).sum(axis=0)
                dup_ref[g, pl.ds(r0, _CHUNK), :] = acc.astype(dup_ref.dtype)
            return carry

        lax.fori_loop(0, tm // _CHUNK, second, 0)
        for g, s_ref in enumerate((sg_ref, sv_ref)):
            for r in range(4):
                s_ref[r:r + 1, :] += jnp.sum(sums[g, r], axis=0, keepdims=True)

    hb = tm // hl
    n_halo = t // hl
    return pl.pallas_call(
        body, name="ffn_gate_bwd", grid=(nc, t // tm),
        in_specs=[pl.BlockSpec((2, tm, tc), lambda j, i: (0, i, j)),
                  pl.BlockSpec((2, tm, tc), lambda j, i: (0, i, j)),
                  pl.BlockSpec((2, hl, tc), lambda j, i: (0, jnp.minimum((i + 1) * hb, n_halo - 1), j)),
                  pl.BlockSpec((tm, tc), lambda j, i: (i, j)),
                  pl.BlockSpec((hl, tc), lambda j, i: (jnp.minimum((i + 1) * hb, n_halo - 1), j)),
                  pl.BlockSpec((8, tc), lambda j, i: (0, j)), pl.BlockSpec((8, tc), lambda j, i: (0, nc + j))],
        out_specs=[pl.BlockSpec((2, tm, tc), lambda j, i: (0, i, j)),
                   pl.BlockSpec((8, tc), lambda j, i: (0, j)), pl.BlockSpec((8, tc), lambda j, i: (0, j))],
        out_shape=[jax.ShapeDtypeStruct((2, t, f), _ACT), jax.ShapeDtypeStruct((8, f), f32), jax.ShapeDtypeStruct((8, f), f32)],
        scratch_shapes=[pltpu.VMEM((2, tm + hl, tc), f32), pltpu.VMEM((2, 4, 8, tc), f32)],
        compiler_params=_params(2),
    )(up, uc, uc, da, da, fw, fw)


def _adamw_math(w, g, m, v):
    m = ADAM_B1 * m + (1.0 - ADAM_B1) * g
    v = ADAM_B2 * v + (1.0 - ADAM_B2) * (g * g)
    m_hat = m / (1.0 - ADAM_B1 ** ADAM_STEP)
    v_hat = v / (1.0 - ADAM_B2 ** ADAM_STEP)
    delta = -ADAM_LR * (m_hat / (jnp.sqrt(v_hat) + ADAM_EPS) + ADAM_WD * w)
    return delta, m, v


def _adamw_shards(quads):
    n = len(quads)
    steps = 8

    def body(*refs):
        for p in range(n):
            w_ref, g_ref, m_ref, v_ref = refs[4 * p:4 * p + 4]
            go_ref, d_ref, mo_ref, vo_ref = refs[4 * n + 4 * p:4 * n + 4 * p + 4]
            gv = g_ref[...]
            d, mn, vn = _adamw_math(w_ref[...], gv, m_ref[...], v_ref[...])
            go_ref[...] = gv
            d_ref[...] = d
            mo_ref[...] = mn
            vo_ref[...] = vn

    in_specs, out_specs, out_shape = [], [], []
    for w, _, _, _ in quads:
        _, r, c = w.shape
        s3 = pl.BlockSpec((None, r // steps, c), lambda i: (0, i, 0))
        in_specs += [s3, pl.BlockSpec((r // steps, c), lambda i: (i, 0)), s3, s3]
        out_specs += [s3] * 4
        out_shape += [jax.ShapeDtypeStruct(w.shape, f32)] * 4
    outs = pl.pallas_call(
        body, name="adamw_shards", grid=(steps,), in_specs=in_specs, out_specs=out_specs, out_shape=out_shape,
        compiler_params=_params(1),
    )(*[a for q in quads for a in q])
    return [tuple(outs[4 * p:4 * p + 4]) for p in range(n)]


def _adamw_small(quads):
    n = len(quads)

    def body(*refs):
        ins, outs = refs[:4 * n], refs[4 * n:]
        for p in range(n):
            w_ref, g_ref, m_ref, v_ref = ins[4 * p:4 * p + 4]
            d, mn, vn = _adamw_math(w_ref[...], g_ref[...], m_ref[...], v_ref[...])
            outs[3 * p][...] = d
            outs[3 * p + 1][...] = mn
            outs[3 * p + 2][...] = vn

    flat = [a for q in quads for a in q]
    shapes = [jax.ShapeDtypeStruct(q[0].shape, f32) for q in quads for _ in range(3)]
    outs = pl.pallas_call(
        body, name="adamw_small", in_specs=[_VMEM] * (4 * n), out_specs=[_VMEM] * (3 * n), out_shape=shapes,
        compiler_params=pltpu.CompilerParams(vmem_limit_bytes=_VMEM_LIMIT_BYTES),
    )(*flat)
    return [tuple(outs[3 * p:3 * p + 3]) for p in range(n)]


def _sum_partials(name, place, grads, got):
    nw = len(grads)
    steps = 2

    def body(place_ref, *refs):
        for w in range(nw):
            own_ref, got_ref, f_ref = refs[w], refs[nw + w], refs[2 * nw + w]
            s = own_ref[...].astype(f32)
            for k in range(got[w].shape[0]):
                s = s + got_ref[k].astype(f32)
            f_ref[...] = s

    own_specs, got_specs, out_specs, out_shape = [], [], [], []
    for g, l in zip(grads, got):
        _, r, c = g.shape
        tr = r // steps
        own_specs.append(pl.BlockSpec((None, tr, c), lambda i, p: (2 * p[0] + p[1], i, 0)))
        got_specs.append(pl.BlockSpec((l.shape[0], tr, c), lambda i, p: (0, i, 0)))
        out_specs.append(pl.BlockSpec((None, tr, c), lambda i, p: (p[1], i, 0)))
        out_shape.append(jax.ShapeDtypeStruct((2, r, c), f32))
    grid_spec = pltpu.PrefetchScalarGridSpec(num_scalar_prefetch=1, grid=(steps,), in_specs=own_specs + got_specs, out_specs=out_specs)
    return pl.pallas_call(body, name=name, grid_spec=grid_spec, out_shape=out_shape,
                          compiler_params=_params(1))(place, *grads, *got)


def _place():
    return lax.axis_index("x"), lax.axis_index("y"), lax.axis_index("c")


def _other_chips(x, y):
    return [(1 - x, y), (x, 1 - y), (1 - x, 1 - y)]


def _remote(src, dst, send_sem, recv_sem, to):
    return pltpu.make_async_remote_copy(src_ref=src, dst_ref=dst, send_sem=send_sem, recv_sem=recv_sem,
                                        device_id=to, device_id_type=_MESH)


def _place_shards(name, place, shards, col_sharded, after=None):
    n = len(shards)
    steps = 4
    more, more_specs = _after(after)

    def body(place_ref, *refs):
        for src, dst in zip(refs[:n], refs[n + len(more):]):
            dst[...] = src[...].astype(dst.dtype)

    in_specs, out_specs, out_shape = [], [], []
    for w, col in zip(shards, col_sharded):
        r, cs = w.shape
        tr = r // steps
        in_specs.append(pl.BlockSpec((tr, cs), lambda i, p: (i, 0)))
        if col:
            out_specs.append(pl.BlockSpec((tr, cs), lambda i, p: (i, p[0])))
            out_shape.append(jax.ShapeDtypeStruct((r, 4 * cs), _ACT))
        else:
            out_specs.append(pl.BlockSpec((tr, cs), lambda i, p: (p[0] * steps + i, 0)))
            out_shape.append(jax.ShapeDtypeStruct((4 * r, cs), _ACT))
    grid_spec = pltpu.PrefetchScalarGridSpec(num_scalar_prefetch=1, grid=(steps,), in_specs=in_specs + more_specs,
                                            out_specs=out_specs)
    return pl.pallas_call(body, name=name, grid_spec=grid_spec, out_shape=out_shape,
                          compiler_params=_params(1))(place, *shards, *more)


def _shard_of(ref, col_sharded, s):
    rows, cols = ref.shape
    if col_sharded:
        return ref.at[:, pl.ds(s * (cols // 4), cols // 4)]
    return ref.at[pl.ds(s * (rows // 4), rows // 4), :]


def _part_of(ref, col_sharded, whole, s, h):
    if whole:
        return _shard_of(ref, col_sharded, s)
    rows, cols = ref.shape
    if col_sharded:
        return ref.at[pl.ds(h * (rows // 2), rows // 2), pl.ds(s * (cols // 4), cols // 4)]
    return ref.at[pl.ds((2 * s + h) * (rows // 8), rows // 8), :]


def _allgather_start(name, bufs, col_sharded, whole, groups):
    n = len(bufs)
    ng = len(groups)

    def body(*refs):
        out = refs[n:2 * n]
        sems = refs[2 * n:2 * n + 2 * ng]
        token = refs[2 * n + 2 * ng]
        x, y, c = _place()
        for g, members in enumerate(groups):
            for i, w in enumerate(members):
                mine = _part_of(out[w], col_sharded[w], whole[w], 2 * x + y, c)
                for j, chip in enumerate(_other_chips(x, y)):
                    _remote(mine, mine, sems[2 * g].at[3 * i + j], sems[2 * g + 1].at[3 * i + j], (*chip, c)).start()
        token[...] = jnp.zeros_like(token)

    sem_shapes = [pltpu.SemaphoreType.DMA((3 * len(m),)) for m in groups for _ in range(2)]
    outs = pl.pallas_call(
        body, name=name, in_specs=[_HBM] * n, out_specs=[_HBM] * n + [_SEM] * (2 * ng) + [_VMEM],
        out_shape=[pltpu.HBM(b.shape, b.dtype) for b in bufs] + sem_shapes + [jax.ShapeDtypeStruct((8, 128), f32)],
        input_output_aliases={i: i for i in range(n)},
        compiler_params=pltpu.CompilerParams(has_side_effects=_EFFECT),
    )(*[pltpu.with_memory_space_constraint(b, pltpu.HBM) for b in bufs])
    return list(outs[:n]), [(outs[n + 2 * g], outs[n + 2 * g + 1]) for g in range(ng)], outs[n + 2 * ng]


def _allgather_relay(name, bufs, col_sharded, whole, sems, after):
    n = len(bufs)

    def body(*refs):
        buf = refs[:n]
        send, recv = refs[n], refs[n + 1]
        out = refs[n + 3:2 * n + 3]
        to_sibling, from_sibling, token = refs[2 * n + 3:]
        token[...] = jnp.zeros_like(token)
        x, y, c = _place()
        for i in range(n):
            mine = _part_of(buf[i], col_sharded[i], whole[i], 2 * x + y, c)
            for j, chip in enumerate(_other_chips(x, y)):
                landed = _part_of(buf[i], col_sharded[i], whole[i], 2 * chip[0] + chip[1], c)
                cp = _remote(mine, landed, send.at[3 * i + j], recv.at[3 * i + j], (*chip, c))
                cp.wait_send()
                cp.wait_recv()
        for i in range(n):
            if not whole[i]:
                for j, chip in enumerate(_other_chips(x, y)):
                    landed = _part_of(out[i], col_sharded[i], False, 2 * chip[0] + chip[1], c)
                    _remote(landed, landed, to_sibling.at[3 * i + j], from_sibling.at[3 * i + j], (x, y, 1 - c)).start()

    outs = pl.pallas_call(
        body, name=name, in_specs=[_HBM] * n + [_SEM, _SEM, _ANY], out_specs=[_HBM] * n + [_SEM, _SEM, _VMEM],
        out_shape=[pltpu.HBM(b.shape, b.dtype) for b in bufs] + [pltpu.SemaphoreType.DMA((3 * n,))] * 2
        + [jax.ShapeDtypeStruct((8, 128), f32)],
        input_output_aliases={i: i for i in range(n)},
        compiler_params=pltpu.CompilerParams(has_side_effects=_EFFECT),
    )(*bufs, *sems, after)
    return list(outs[:n]), (outs[n], outs[n + 1]), outs[n + 2]


def _allgather_wait(name, bufs, col_sharded, whole, sems, after):
    n = len(bufs)

    def body(*refs):
        buf = refs[:n]
        to_sibling, from_sibling = refs[n], refs[n + 1]
        x, y, c = _place()
        for i in range(n):
            if not whole[i]:
                for j, chip in enumerate(_other_chips(x, y)):
                    sent = _part_of(buf[i], col_sharded[i], False, 2 * chip[0] + chip[1], c)
                    landed = _part_of(buf[i], col_sharded[i], False, 2 * chip[0] + chip[1], 1 - c)
                    cp = _remote(sent, landed, to_sibling.at[3 * i + j], from_sibling.at[3 * i + j], (x, y, 1 - c))
                    cp.wait_send()
                    cp.wait_recv()

    return pl.pallas_call(
        body, name=name, in_specs=[_HBM] * n + [_SEM, _SEM, _ANY], out_specs=[_HBM] * n,
        out_shape=[pltpu.HBM(b.shape, b.dtype) for b in bufs],
        input_output_aliases={i: i for i in range(n)},
        compiler_params=pltpu.CompilerParams(has_side_effects=_EFFECT),
    )(*bufs, *sems, after)


def _other_devices(x, y, c):
    flips = [(bx, by, bc) for bx in (0, 1) for by in (0, 1) for bc in (0, 1)][1:]
    return [(1 - x if bx else x, 1 - y if by else y, 1 - c if bc else c) for bx, by, bc in flips]


def _grad_exchange_start(name, grads):
    nw = len(grads)
    lands = [lax.empty((7,) + g.shape[1:], g.dtype) for g in grads]

    def body(*refs):
        src = refs[2 * nw:3 * nw]
        got = refs[3 * nw:4 * nw]
        send, recv, token = refs[4 * nw:]
        x, y, c = _place()
        for w in range(nw):
            for k, (px, py, pc) in enumerate(_other_devices(x, y, c)):
                _remote(src[w].at[4 * px + 2 * py + pc], got[w].at[k], send.at[7 * w + k], recv.at[7 * w + k], (px, py, pc)).start()
        token[...] = jnp.zeros_like(token)

    outs = pl.pallas_call(
        body, name=name, in_specs=[_HBM] * (2 * nw), out_specs=[_HBM] * (2 * nw) + [_SEM, _SEM, _VMEM],
        out_shape=[pltpu.HBM(a.shape, a.dtype) for a in list(grads) + lands]
        + [pltpu.SemaphoreType.DMA((7 * nw,)), pltpu.SemaphoreType.DMA((7 * nw,)), jax.ShapeDtypeStruct((8, 128), f32)],
        input_output_aliases={i: i for i in range(2 * nw)},
        compiler_params=pltpu.CompilerParams(has_side_effects=_EFFECT),
    )(*[pltpu.with_memory_space_constraint(a, pltpu.HBM) for a in list(grads) + lands])
    return list(outs[:nw]), list(outs[nw:2 * nw]), (outs[2 * nw], outs[2 * nw + 1]), outs[2 * nw + 2]


def _grad_exchange_wait(name, grads, got, sems, after):
    nw = len(grads)

    def body(*refs):
        src = refs[:nw]
        land = refs[nw:2 * nw]
        send, recv = refs[2 * nw], refs[2 * nw + 1]
        x, y, c = _place()
        for w in range(nw):
            for k, (px, py, pc) in enumerate(_other_devices(x, y, c)):
                cp = _remote(src[w].at[4 * px + 2 * py + pc], land[w].at[k], send.at[7 * w + k], recv.at[7 * w + k], (px, py, pc))
                cp.wait_send()
                cp.wait_recv()

    outs = pl.pallas_call(
        body, name=name, in_specs=[_HBM] * (2 * nw) + [_SEM, _SEM, _ANY], out_specs=[_HBM] * (2 * nw),
        out_shape=[pltpu.HBM(a.shape, a.dtype) for a in list(grads) + list(got)],
        input_output_aliases={i: i for i in range(2 * nw)},
        compiler_params=pltpu.CompilerParams(has_side_effects=_EFFECT),
    )(*grads, *got, *sems, after)
    return list(outs[:nw]), list(outs[nw:])


def _swap_halves_start(finals):
    nw = len(finals)

    def body(*refs):
        buf = refs[nw:2 * nw]
        send, recv, token = refs[2 * nw:]
        x, y, c = _place()
        for w in range(nw):
            _remote(buf[w].at[c], buf[w].at[c], send.at[w], recv.at[w], (x, y, 1 - c)).start()
        token[...] = jnp.zeros_like(token)

    outs = pl.pallas_call(
        body, name="rs_swap_start", in_specs=[_HBM] * nw, out_specs=[_HBM] * nw + [_SEM, _SEM, _VMEM],
        out_shape=[pltpu.HBM(g.shape, g.dtype) for g in finals] + [pltpu.SemaphoreType.DMA((nw,))] * 2
        + [jax.ShapeDtypeStruct((8, 128), f32)],
        input_output_aliases={i: i for i in range(nw)},
        compiler_params=pltpu.CompilerParams(has_side_effects=_EFFECT),
    )(*[pltpu.with_memory_space_constraint(g, pltpu.HBM) for g in finals])
    return list(outs[:nw]), (outs[nw], outs[nw + 1]), outs[nw + 2]


def _swap_halves_wait(bufs, sems, after):
    nw = len(bufs)

    def body(*refs):
        buf = refs[:nw]
        send, recv = refs[nw], refs[nw + 1]
        x, y, c = _place()
        for w in range(nw):
            cp = _remote(buf[w].at[c], buf[w].at[1 - c], send.at[w], recv.at[w], (x, y, 1 - c))
            cp.wait_send()
            cp.wait_recv()

    return pl.pallas_call(
        body, name="rs_swap_wait", in_specs=[_HBM] * nw + [_SEM, _SEM, _ANY], out_specs=[_HBM] * nw,
        out_shape=[pltpu.HBM(g.shape, g.dtype) for g in bufs],
        input_output_aliases={i: i for i in range(nw)},
        compiler_params=pltpu.CompilerParams(has_side_effects=_EFFECT),
    )(*bufs, *sems, after)


def _half_slices(shape, h):
    rows, cols = shape
    if cols % 256 == 0:
        return (slice(None), slice(h * (cols // 2), (h + 1) * (cols // 2)))
    return (slice(h * (rows // 2), (h + 1) * (rows // 2)), slice(None))


def _allreduce_small(parts, after):
    n = len(parts)

    def body(*refs):
        src = refs[:n]
        refs = refs[n + 1:]
        out = refs[:n]
        sib = refs[n:2 * n]
        chip_sum = refs[2 * n:3 * n]
        slots = refs[3 * n:4 * n]
        pair_send, pair_recv, ici_send, ici_recv, swap_send, swap_recv = refs[4 * n:]
        x, y, c = _place()
        me_chip = 2 * x + y
        chips = _other_chips(x, y)
        pairs = [_remote(src[a], sib[a], pair_send.at[a], pair_recv.at[a], (x, y, 1 - c)) for a in range(n)]
        for rc in pairs:
            rc.start()
        for a in range(n):
            pairs[a].wait_recv()
            chip_sum[a][...] = src[a][...] + sib[a][...]
        for h in (0, 1):
            @pl.when(c == h)
            def _():
                sends = []
                for a in range(n):
                    idx = _half_slices(parts[a].shape, h)
                    for j, chip in enumerate(chips):
                        rc = _remote(chip_sum[a].at[idx], slots[a].at[me_chip].at[idx], ici_send.at[3 * a + j], ici_recv.at[3 * a + j], (*chip, h))
                        rc.start()
                        sends.append(rc)
                    slots[a][(me_chip,) + idx] = chip_sum[a][idx]
                for a in range(n):
                    idx = _half_slices(parts[a].shape, h)
                    for j, chip in enumerate(chips):
                        landed = slots[a].at[2 * chip[0] + chip[1]].at[idx]
                        _remote(landed, landed, ici_send.at[3 * a + j], ici_recv.at[3 * a + j], (x, y, c)).wait_recv()
                    total = slots[a][(0,) + idx]
                    for s in range(1, 4):
                        total = total + slots[a][(s,) + idx]
                    out[a][idx] = total
                    rc = _remote(out[a].at[idx], out[a].at[idx], swap_send.at[a], swap_recv.at[a], (x, y, 1 - h))
                    rc.start()
                    sends.append(rc)
                for a in range(n):
                    other = out[a].at[_half_slices(parts[a].shape, 1 - h)]
                    _remote(other, other, swap_send.at[a], swap_recv.at[a], (x, y, c)).wait_recv()
                for rc in sends:
                    rc.wait_send()
        for rc in pairs:
            rc.wait_send()

    return pl.pallas_call(
        body, name="allreduce_small", in_specs=[_VMEM] * n + [_ANY], out_specs=[_VMEM] * n,
        out_shape=[jax.ShapeDtypeStruct(p.shape, f32) for p in parts],
        scratch_shapes=[pltpu.VMEM(p.shape, f32) for p in parts] * 2 + [pltpu.VMEM((4,) + p.shape, f32) for p in parts]
        + [pltpu.SemaphoreType.DMA((n,)), pltpu.SemaphoreType.DMA((n,)), pltpu.SemaphoreType.DMA((3 * n,)),
           pltpu.SemaphoreType.DMA((3 * n,)), pltpu.SemaphoreType.DMA((n,)), pltpu.SemaphoreType.DMA((n,))],
        compiler_params=pltpu.CompilerParams(vmem_limit_bytes=_VMEM_LIMIT_BYTES),
    )(*parts, after)


def _local_step(x, mem, tgt, g_mix, g_xattn, g_mem, g_ffn, g_final, cb, lg, lb, pw, ps, fb, started, relay, weights, reduce,
                n_seq, seq, n_mem):
    t, d = x.shape
    f = fb.shape[1] // 2
    c = cb.shape[1]
    h1 = _rms_fwd("norm_mix", x, g_mix, after=started)
    relay(0, h1)
    w_in, cw, fw = weights(0, h1)
    u = _mm_nn("proj_in", h1, w_in, _ACT, w_in.shape[1])
    y, hc = _mix_fwd(u, cw, cb, lg, lb, pw, ps, seq)
    relay(1, y)
    w_out, w_q, w_kv, w_o = weights(1, y)
    x1, h2 = _proj_residual_norm("proj_out", y, w_out, x, g_xattn)
    q = _mm_nn("proj_q", h2, w_q, _ACT, d)
    mem_n = _rms_fwd("norm_mem", mem, g_mem)
    kv = _mm_nn("proj_kv", mem_n, w_kv, _ACT, 2 * d)
    o = _attn_fwd(q, kv, n_seq, seq, n_mem)
    x2, h3 = _proj_residual_norm("proj_o", o, w_o, x1, g_ffn, after=relay(2, o))
    w_up, w_down = weights(2, h3)
    up = _mm_nn("proj_up", h3, w_up, _ACT, f, split_out=True)
    a, uc = _ffn_gate_fwd(up, fw, fb, seq)
    dx3, dx3b, dg_final, loss = _proj_loss_bwd("proj_down", a, w_down, x2, g_final, tgt)
    da = _mm_nt("d_act", dx3b, w_down, _ACT)
    gw_down = _mm_tn_rows("dw_down", a, dx3b, f // 2, d // 2)
    dup, sums_g, sums_v = _ffn_gate_bwd(up, uc, da, fw, seq)
    gw_up = _mm_tn_pieces("dw_up", h3, dup, f // 2)
    token = reduce(0, [gw_down.reshape(8, -1, d), gw_up])
    dx2, dx2b, dg_ffn = _dproj_rms_bwd("d_h3", dup, w_up, x2, g_ffn, dx3, after=token)
    do = _mm_nt("d_o", dx2b, w_o, _ACT)
    gw_o = _mm_tn_rows("dw_o", o, dx2b, d, d // 2)
    dq, dkv = _attn_bwd(q, kv, do, n_seq, seq, n_mem)
    gw_q = _mm_tn_rows("dw_q", h2, dq, d, d // 2)
    gw_kv = _mm_tn_pieces("dw_kv", mem_n, dkv, d // 2)
    dmem_n = _mm_nt("d_mem_n", dkv, w_kv, f32)
    dg_mem = _rms_gain_grad("norm_mem_bwd", mem, dmem_n)
    dx1, dx1b, dg_xattn = _dproj_rms_bwd("d_h2", dq, w_q, x1, g_xattn, dx2)
    dy = _mm_nt("d_y", dx1b, w_out, _ACT)
    gw_out = _mm_tn_rows("dw_out", y, dx1b, d, d // 2)
    token = reduce(1, [gw_o.reshape(8, -1, d), gw_q.reshape(8, -1, d), gw_kv, gw_out.reshape(8, -1, d)])
    dhc, sums_norm = _mix_bwd_norm(hc, dy, lg, lb, token)
    du, d_cw, d_ps, d_pw = _mix_bwd_taps(u, dhc, dy, cw, pw, ps, seq)
    gw_in = _mm_tn_pieces("dw_in", h1, du, c * 3 // 4)
    token = reduce(2, [gw_in])
    grad_x, dg_mix = _dproj_rms_bwd("d_h1", du, w_in, x, g_mix, dx1, storage_copy=False, after=token)
    zero_row = jnp.zeros((1, d), f32)
    gains = jnp.concatenate([dg_mix, dg_xattn, dg_mem, dg_ffn, dg_final, jnp.pad(loss, ((0, 0), (0, d - 1))), zero_row, zero_row], axis=0)
    conv_rows = jnp.concatenate([sums_norm[2:3], sums_norm[0:1], sums_norm[1:2], d_ps[0:1], jnp.zeros((4, c), f32)], axis=0)
    ffn_rows = jnp.concatenate([sums_g, sums_v], axis=1)
    small = [gains, conv_rows, d_pw.reshape(-1, d_pw.shape[-1]), ffn_rows, d_cw]
    return grad_x, small


def kernel(x, mem, norm_mix_g, w_in, conv_dw_w, conv_dw_b, conv_ln_g, conv_ln_b, pool_w, pool_scale, w_out, norm_xattn_g, norm_mem_g, w_q, w_kv, w_o, norm_ffn_g, w_up, ffn_dw_w, ffn_dw_b, w_down, norm_final_g, loss_target, m_norm_mix_g, m_w_in, m_conv_dw_w, m_conv_dw_b, m_conv_ln_g, m_conv_ln_b, m_pool_w, m_pool_scale, m_w_out, m_norm_xattn_g, m_norm_mem_g, m_w_q, m_w_kv, m_w_o, m_norm_ffn_g, m_w_up, m_ffn_dw_w, m_ffn_dw_b, m_w_down, m_norm_final_g, v_norm_mix_g, v_w_in, v_conv_dw_w, v_conv_dw_b, v_conv_ln_g, v_conv_ln_b, v_pool_w, v_pool_scale, v_w_out, v_norm_xattn_g, v_norm_mem_g, v_w_q, v_w_kv, v_w_o, v_norm_ffn_g, v_w_up, v_ffn_dw_w, v_ffn_dw_b, v_w_down, v_norm_final_g):
    n_seq, seq, d = x.shape
    n_mem = mem.shape[1]
    chip = 2 * lax.axis_index("x") + lax.axis_index("y")

    place = jnp.stack([chip, lax.axis_index("c")]).astype(jnp.int32)

    col_w = [w_in, w_kv, w_up]
    row_w = [w_out, w_q, w_o, w_down]
    kw = conv_dw_w.shape[1]

    def padded_in_place(shard, rows):
        full = jnp.zeros((rows, 4 * shard.shape[1]), shard.dtype)
        return lax.dynamic_update_slice(full, shard, (0, chip * shard.shape[1]))

    first = list(_place_shards("place_w_in", place, [w_in[0]], [True]))
    first += [padded_in_place(conv_dw_w[0], _HALO), padded_in_place(ffn_dw_w[0], 8)]
    first, first_sems, token = _allgather_start("allgather_start_0", first, [True] * 3, [False, True, True], [[0, 1, 2]])
    rest = [w_kv, w_up, w_out, w_q, w_o, w_down]
    rest_flags = [True, True, False, False, False, False]
    rest = list(_place_shards("place_rest", place, [w[0] for w in rest], rest_flags, after=token))
    rest, rest_sems, all_started = _allgather_start("allgather_start_1", rest, rest_flags, [False] * 6, [[2, 3, 0, 4], [1, 5]])
    started = [(first, [True] * 3, [False, True, True], first_sems[0]),
               ([rest[i] for i in (2, 3, 0, 4)], [False, False, True, False], [False] * 4, rest_sems[0]),
               ([rest[i] for i in (1, 5)], [True, False], [False] * 2, rest_sems[1])]
    relayed = {}

    def relay(g, after):
        group_bufs, flags, wholes, group_sems = started[g]
        group_bufs, sibling_sems, relay_token = _allgather_relay("allgather_relay_%d" % g, group_bufs, flags, wholes, group_sems, after)
        relayed[g] = (group_bufs, sibling_sems)
        return relay_token

    def weights(g, after):
        group_bufs, sibling_sems = relayed[g]
        return _allgather_wait("allgather_wait_%d" % g, group_bufs, started[g][1], started[g][2], sibling_sems, after)

    names = ["w_in", "w_kv", "w_up", "w_out", "w_q", "w_o", "w_down"]
    reduce_groups = [["w_down", "w_up"], ["w_o", "w_q", "w_kv", "w_out"], ["w_in"]]
    in_flight = {}

    def reduce(g, grads):
        grads, lands, rs_sems, token = _grad_exchange_start("rs_start_%d" % g, grads)
        in_flight[g] = (grads, lands, rs_sems)
        return token

    grad_x, small = _local_step(
        x.reshape(n_seq * seq, d), mem.reshape(n_seq * n_mem, d), loss_target.reshape(n_seq * seq, d),
        norm_mix_g, norm_xattn_g, norm_mem_g, norm_ffn_g, norm_final_g.reshape(1, d),
        conv_dw_b, conv_ln_g, conv_ln_b, pool_w[0], pool_scale, ffn_dw_b, all_started, relay, weights, reduce,
        n_seq, seq, n_mem)

    landed = {}
    for g, members in enumerate(reduce_groups):
        grads, lands, rs_sems = in_flight[g]
        grads, lands = _grad_exchange_wait("rs_wait_%d" % g, grads, lands, rs_sems, grad_x)
        landed.update(zip(members, zip(grads, lands)))
    finals = _sum_partials("rs_sum", place, [landed[n][0] for n in names], [landed[n][1] for n in names])
    finals, swap_sems, token = _swap_halves_start(finals)

    gains, conv_rows, d_pw, ffn_rows, d_cw = _allreduce_small(small, token)
    loss = gains[5, 0]
    shard_grads = _swap_halves_wait(finals, swap_sems, gains)

    outs = {}
    big_w = dict(zip(names, col_w + row_w))
    big_m = dict(w_in=m_w_in, w_kv=m_w_kv, w_up=m_w_up, w_out=m_w_out, w_q=m_w_q, w_o=m_w_o, w_down=m_w_down)
    big_v = dict(w_in=v_w_in, w_kv=v_w_kv, w_up=v_w_up, w_out=v_w_out, w_q=v_w_q, w_o=v_w_o, w_down=v_w_down)
    big_quads = [(big_w[n], g.reshape(big_w[n].shape[1:]), big_m[n], big_v[n]) for n, g in zip(names, shard_grads)]
    outs.update(zip(names, _adamw_shards(big_quads)))

    f2 = ffn_dw_b.shape[1]
    cs_c = conv_dw_w.shape[2]
    cs_f = ffn_dw_w.shape[2]
    g_cw = lax.dynamic_slice(d_cw, (0, chip * cs_c), (kw, cs_c)).reshape(conv_dw_w.shape)
    g_fw = lax.dynamic_slice(ffn_rows, (1, chip * cs_f), (ffn_dw_w.shape[1], cs_f)).reshape(ffn_dw_w.shape)
    small_params = [
        ("norm_mix_g", norm_mix_g, gains[0:1], m_norm_mix_g, v_norm_mix_g),
        ("conv_dw_w", conv_dw_w, g_cw, m_conv_dw_w, v_conv_dw_w),
        ("conv_dw_b", conv_dw_b, conv_rows[0:1], m_conv_dw_b, v_conv_dw_b),
        ("conv_ln_g", conv_ln_g, conv_rows[1:2], m_conv_ln_g, v_conv_ln_g),
        ("conv_ln_b", conv_ln_b, conv_rows[2:3], m_conv_ln_b, v_conv_ln_b),
        ("pool_w", pool_w, d_pw.reshape(pool_w.shape), m_pool_w, v_pool_w),
        ("pool_scale", pool_scale, conv_rows[3:4], m_pool_scale, v_pool_scale),
        ("norm_xattn_g", norm_xattn_g, gains[1:2], m_norm_xattn_g, v_norm_xattn_g),
        ("norm_mem_g", norm_mem_g, gains[2:3], m_norm_mem_g, v_norm_mem_g),
        ("norm_ffn_g", norm_ffn_g, gains[3:4], m_norm_ffn_g, v_norm_ffn_g),
        ("ffn_dw_w", ffn_dw_w, g_fw, m_ffn_dw_w, v_ffn_dw_w),
        ("ffn_dw_b", ffn_dw_b, ffn_rows[0:1, :f2], m_ffn_dw_b, v_ffn_dw_b),
        ("norm_final_g", norm_final_g.reshape(1, d), gains[4:5], m_norm_final_g.reshape(1, d), v_norm_final_g.reshape(1, d)),
    ]
    quads = []
    for _, w, g, m, v in small_params:
        shape2 = (-1, w.shape[-1])
        quads.append((w.reshape(shape2), g.reshape(shape2), m.reshape(shape2), v.reshape(shape2)))
    for (n, w, g, _, _), (delta, new_m, new_v) in zip(small_params, _adamw_small(quads)):
        shape = norm_final_g.shape if n == "norm_final_g" else w.shape
        outs[n] = (g.reshape(shape), delta.reshape(shape), new_m.reshape(shape), new_v.reshape(shape))

    order = ["norm_mix_g", "w_in", "conv_dw_w", "conv_dw_b", "conv_ln_g", "conv_ln_b", "pool_w", "pool_scale", "w_out",
             "norm_xattn_g", "norm_mem_g", "w_q", "w_kv", "w_o", "norm_ffn_g", "w_up", "ffn_dw_w", "ffn_dw_b", "w_down",
             "norm_final_g"]
    return (loss, grad_x.reshape(x.shape), *[outs[n][0] for n in order], *[outs[n][1] for n in order],
            *[outs[n][2] for n in order], *[outs[n][3] for n in order])
```
